```python
import jax, jax.numpy as jnp
from jax import lax
import numpy as np

D_MODEL = 1024
BATCH = 16
SEQ = 2048
DEPTH = 1

CHUNK = 64
D_MIX = D_MODEL
ATTN_WIDTH = D_MIX // 2
HGRN_WIDTH = D_MIX - ATTN_WIDTH
ATTN_HEAD_DIM = 64
ATTN_HEADS = ATTN_WIDTH // ATTN_HEAD_DIM
HGRN_HEAD_DIM = 128
HGRN_HEADS = HGRN_WIDTH // HGRN_HEAD_DIM
LEFT_CHUNKS = 8
BAND = (LEFT_CHUNKS + 1) * CHUNK
REL_CLIP = 128
N_REL = 2 * REL_CLIP + 1
D_FF = 2816
RMS_EPS = 1e-6
PROJ_SIZES = (ATTN_WIDTH, ATTN_WIDTH, ATTN_WIDTH, HGRN_WIDTH, HGRN_WIDTH, HGRN_WIDTH, HGRN_WIDTH)
PROJ_COLS = sum(PROJ_SIZES)
PROJ_SPLITS = tuple(int(v) for v in np.cumsum(PROJ_SIZES)[:-1])

kernel_name = "hybrid_chunk_attn_hgrn2_macaron"


def rms_norm(x, g):
    xf = x.astype(jnp.float32)
    y = xf * lax.rsqrt(jnp.mean(xf * xf, axis=-1, keepdims=True) + RMS_EPS)
    return (y * g.astype(jnp.float32)).astype(x.dtype)


def swiglu_ffn(h, w_gate, w_up, w_down):
    return (jax.nn.silu(h @ w_gate) * (h @ w_up)) @ w_down


def chunked_band_attention(q, k, v, rel_bias):
    B, S, H, Dh = q.shape
    n_chunks = S // CHUNK
    pad = LEFT_CHUNKS * CHUNK
    k_pad = jnp.pad(k, ((0, 0), (pad, 0), (0, 0), (0, 0)))
    v_pad = jnp.pad(v, ((0, 0), (pad, 0), (0, 0), (0, 0)))
    t_pos = jnp.arange(CHUNK)[:, None] + pad
    s_pos = jnp.arange(BAND)[None, :]
    rel_idx = jnp.clip(t_pos - s_pos, -REL_CLIP, REL_CLIP) + REL_CLIP
    bias = rel_bias.astype(jnp.float32)[:, rel_idx]
    scale = Dh ** -0.5
    q_chunks = q.reshape(B, n_chunks, CHUNK, H, Dh).transpose(1, 0, 2, 3, 4)

    def one_chunk(args):
        c, qc = args
        start = c * CHUNK
        kb = lax.dynamic_slice_in_dim(k_pad, start, BAND, axis=1)
        vb = lax.dynamic_slice_in_dim(v_pad, start, BAND, axis=1)
        scores = jnp.einsum('bqhd,bkhd->bhqk', qc, kb,
                            preferred_element_type=jnp.float32) * scale + bias
        key_pos = start - pad + jnp.arange(BAND)
        scores = jnp.where((key_pos >= 0)[None, None, None, :], scores, -jnp.inf)
        p = jax.nn.softmax(scores, axis=-1)
        return jnp.einsum('bhqk,bkhd->bqhd', p.astype(vb.dtype), vb)

    out = lax.map(one_chunk, (jnp.arange(n_chunks), q_chunks))
    return out.transpose(1, 0, 2, 3, 4).reshape(B, S, H * Dh)


def hgrn2_chunkwise(q, k, v, log_f):
    B, S, H, Dk = q.shape
    Dv = v.shape[-1]
    n_chunks = S // CHUNK

    def to_chunks(a):
        return a.reshape(B, n_chunks, CHUNK, H, a.shape[-1]).transpose(1, 0, 3, 2, 4)

    qc, kc, vc, gc = to_chunks(q), to_chunks(k), to_chunks(v), to_chunks(log_f)
    causal = jnp.tril(jnp.ones((CHUNK, CHUNK), dtype=bool))[None, None, :, :, None]

    def step(state, inp):
        qi, ki, vi, gi = inp
        b = jnp.cumsum(gi, axis=2)
        diff = b[:, :, :, None, :] - b[:, :, None, :, :]
        decay = jnp.exp(jnp.where(causal, diff, -jnp.inf))
        scores = jnp.einsum('bhtk,bhsk,bhtsk->bhts', qi, ki, decay)
        o = (jnp.einsum('bhts,bhsv->bhtv', scores, vi)
             + jnp.einsum('bhtk,bhkv->bhtv', qi * jnp.exp(b), state))
        b_last = b[:, :, -1:, :]
        new_state = (state * jnp.exp(b_last[:, :, 0, :])[..., None]
                     + jnp.einsum('bhsk,bhsv->bhkv', ki * jnp.exp(b_last - b), vi))
        return new_state, o

    s0 = jnp.zeros((B, H, Dk, Dv), jnp.float32)
    _, o = lax.scan(step, s0, (qc, kc, vc, gc))
    return o.transpose(1, 0, 3, 2, 4).reshape(B, S, H, Dv)


def _fwd_setup_inputs(seed: int = 0) -> dict:
    key = jax.random.key(seed)
    ks = jax.random.split(key, 20)
    f32 = jnp.float32

    def w(k, shape, fan_in):
        return jax.random.normal(k, shape, f32) * (fan_in ** -0.5)

    def gain(k, shape):
        return 1.0 + 0.05 * jax.random.normal(k, shape, f32)

    return {
        "x": jax.random.normal(ks[0], (BATCH, SEQ, D_MODEL), f32),
        "ffn1_norm_g": gain(ks[1], (DEPTH, D_MODEL)),
        "ffn1_w_gate": w(ks[2], (DEPTH, D_MODEL, D_FF), D_MODEL),
        "ffn1_w_up": w(ks[3], (DEPTH, D_MODEL, D_FF), D_MODEL),
        "ffn1_w_down": w(ks[4], (DEPTH, D_FF, D_MODEL), D_FF),
        "mix_norm_g": gain(ks[5], (DEPTH, D_MODEL)),
        "w_in": w(ks[6], (DEPTH, D_MODEL, PROJ_COLS), D_MODEL),
        "attn_q_norm_g": gain(ks[7], (DEPTH, ATTN_HEAD_DIM)),
        "attn_k_norm_g": gain(ks[8], (DEPTH, ATTN_HEAD_DIM)),
        "attn_rel_bias": 0.1 * jax.random.normal(ks[9], (DEPTH, ATTN_HEADS, N_REL), f32),
        "hgrn_lower_bounds": 0.1 * jax.random.normal(ks[10], (DEPTH + 1, HGRN_WIDTH), f32),
        "hgrn_out_norm_g": gain(ks[11], (DEPTH, HGRN_HEAD_DIM)),
        "w_out": w(ks[12], (DEPTH, D_MIX, D_MODEL), D_MIX),
        "ffn2_norm_g": gain(ks[13], (DEPTH, D_MODEL)),
        "ffn2_w_gate": w(ks[14], (DEPTH, D_MODEL, D_FF), D_MODEL),
        "ffn2_w_up": w(ks[15], (DEPTH, D_MODEL, D_FF), D_MODEL),
        "ffn2_w_down": w(ks[16], (DEPTH, D_FF, D_MODEL), D_FF),
    }


def _fwd_reference(x, ffn1_norm_g, ffn1_w_gate, ffn1_w_up, ffn1_w_down, mix_norm_g, w_in,
              attn_q_norm_g, attn_k_norm_g, attn_rel_bias, hgrn_lower_bounds, hgrn_out_norm_g,
              w_out, ffn2_norm_g, ffn2_w_gate, ffn2_w_up, ffn2_w_down):
    B, S, _ = x.shape
    lb_all = jnp.cumsum(jax.nn.softmax(hgrn_lower_bounds.astype(jnp.float32), axis=0), axis=0)

    for l in range(DEPTH):
        h = rms_norm(x, ffn1_norm_g[l])
        x = x + 0.5 * swiglu_ffn(h, ffn1_w_gate[l], ffn1_w_up[l], ffn1_w_down[l])

        h = rms_norm(x, mix_norm_g[l])
        proj = h @ w_in[l]
        aq, ak, av, hq, hf, hi, hg = jnp.split(proj, PROJ_SPLITS, axis=-1)

        aq = rms_norm(aq.reshape(B, S, ATTN_HEADS, ATTN_HEAD_DIM), attn_q_norm_g[l])
        ak = rms_norm(ak.reshape(B, S, ATTN_HEADS, ATTN_HEAD_DIM), attn_k_norm_g[l])
        av = av.reshape(B, S, ATTN_HEADS, ATTN_HEAD_DIM)
        attn_out = chunked_band_attention(aq, ak, av, attn_rel_bias[l])

        lb = lb_all[l]
        f = lb + (1.0 - lb) * jax.nn.sigmoid(hf.astype(jnp.float32))
        shp = (B, S, HGRN_HEADS, HGRN_HEAD_DIM)
        rq = jax.nn.silu(hq.astype(jnp.float32)).reshape(shp)
        rk = (1.0 - f).reshape(shp)
        rv = hi.astype(jnp.float32).reshape(shp)
        ro = hgrn2_chunkwise(rq, rk, rv, jnp.log(f).reshape(shp))
        ro = rms_norm(ro, hgrn_out_norm_g[l]) * jax.nn.silu(hg.astype(jnp.float32).reshape(shp))
        hgrn_out = ro.reshape(B, S, HGRN_WIDTH).astype(x.dtype)

        x = x + jnp.concatenate([attn_out, hgrn_out], axis=-1) @ w_out[l]

        h = rms_norm(x, ffn2_norm_g[l])
        x = x + 0.5 * swiglu_ffn(h, ffn2_w_gate[l], ffn2_w_up[l], ffn2_w_down[l])
    return x


import jax as _jax
import jax.numpy as _jnp

TWIN_FORMAT = 'train_step'
FWD_PARAMS = ['x', 'ffn1_norm_g', 'ffn1_w_gate', 'ffn1_w_up', 'ffn1_w_down', 'mix_norm_g', 'w_in', 'attn_q_norm_g', 'attn_k_norm_g', 'attn_rel_bias', 'hgrn_lower_bounds', 'hgrn_out_norm_g', 'w_out', 'ffn2_norm_g', 'ffn2_w_gate', 'ffn2_w_up', 'ffn2_w_down']
TWIN_WEIGHTS = ['ffn1_norm_g', 'ffn1_w_gate', 'ffn1_w_up', 'ffn1_w_down', 'mix_norm_g', 'w_in', 'attn_q_norm_g', 'attn_k_norm_g', 'attn_rel_bias', 'hgrn_lower_bounds', 'hgrn_out_norm_g', 'w_out', 'ffn2_norm_g', 'ffn2_w_gate', 'ffn2_w_up', 'ffn2_w_down']
TWIN_DIFF_INPUT = 'x'
TWIN_INPUTS = ['x', 'ffn1_norm_g', 'ffn1_w_gate', 'ffn1_w_up', 'ffn1_w_down', 'mix_norm_g', 'w_in', 'attn_q_norm_g', 'attn_k_norm_g', 'attn_rel_bias', 'hgrn_lower_bounds', 'hgrn_out_norm_g', 'w_out', 'ffn2_norm_g', 'ffn2_w_gate', 'ffn2_w_up', 'ffn2_w_down', 'loss_target', 'm_ffn1_norm_g', 'm_ffn1_w_gate', 'm_ffn1_w_up', 'm_ffn1_w_down', 'm_mix_norm_g', 'm_w_in', 'm_attn_q_norm_g', 'm_attn_k_norm_g', 'm_attn_rel_bias', 'm_hgrn_lower_bounds', 'm_hgrn_out_norm_g', 'm_w_out', 'm_ffn2_norm_g', 'm_ffn2_w_gate', 'm_ffn2_w_up', 'm_ffn2_w_down', 'v_ffn1_norm_g', 'v_ffn1_w_gate', 'v_ffn1_w_up', 'v_ffn1_w_down', 'v_mix_norm_g', 'v_w_in', 'v_attn_q_norm_g', 'v_attn_k_norm_g', 'v_attn_rel_bias', 'v_hgrn_lower_bounds', 'v_hgrn_out_norm_g', 'v_w_out', 'v_ffn2_norm_g', 'v_ffn2_w_gate', 'v_ffn2_w_up', 'v_ffn2_w_down']
TWIN_OUTPUTS = ['loss', 'grad_x', 'grad_ffn1_norm_g', 'grad_ffn1_w_gate', 'grad_ffn1_w_up', 'grad_ffn1_w_down', 'grad_mix_norm_g', 'grad_w_in', 'grad_attn_q_norm_g', 'grad_attn_k_norm_g', 'grad_attn_rel_bias', 'grad_hgrn_lower_bounds', 'grad_hgrn_out_norm_g', 'grad_w_out', 'grad_ffn2_norm_g', 'grad_ffn2_w_gate', 'grad_ffn2_w_up', 'grad_ffn2_w_down', 'delta_ffn1_norm_g', 'delta_ffn1_w_gate', 'delta_ffn1_w_up', 'delta_ffn1_w_down', 'delta_mix_norm_g', 'delta_w_in', 'delta_attn_q_norm_g', 'delta_attn_k_norm_g', 'delta_attn_rel_bias', 'delta_hgrn_lower_bounds', 'delta_hgrn_out_norm_g', 'delta_w_out', 'delta_ffn2_norm_g', 'delta_ffn2_w_gate', 'delta_ffn2_w_up', 'delta_ffn2_w_down', 'new_m_ffn1_norm_g', 'new_m_ffn1_w_gate', 'new_m_ffn1_w_up', 'new_m_ffn1_w_down', 'new_m_mix_norm_g', 'new_m_w_in', 'new_m_attn_q_norm_g', 'new_m_attn_k_norm_g', 'new_m_attn_rel_bias', 'new_m_hgrn_lower_bounds', 'new_m_hgrn_out_norm_g', 'new_m_w_out', 'new_m_ffn2_norm_g', 'new_m_ffn2_w_gate', 'new_m_ffn2_w_up', 'new_m_ffn2_w_down', 'new_v_ffn1_norm_g', 'new_v_ffn1_w_gate', 'new_v_ffn1_w_up', 'new_v_ffn1_w_down', 'new_v_mix_norm_g', 'new_v_w_in', 'new_v_attn_q_norm_g', 'new_v_attn_k_norm_g', 'new_v_attn_rel_bias', 'new_v_hgrn_lower_bounds', 'new_v_hgrn_out_norm_g', 'new_v_w_out', 'new_v_ffn2_norm_g', 'new_v_ffn2_w_gate', 'new_v_ffn2_w_up', 'new_v_ffn2_w_down']
TWIN_LEAF_KINDS = {'loss': 'loss', 'grad_x': 'grad_x', 'grad_ffn1_norm_g': 'grad_w', 'grad_ffn1_w_gate': 'grad_w', 'grad_ffn1_w_up': 'grad_w', 'grad_ffn1_w_down': 'grad_w', 'grad_mix_norm_g': 'grad_w', 'grad_w_in': 'grad_w', 'grad_attn_q_norm_g': 'grad_w', 'grad_attn_k_norm_g': 'grad_w', 'grad_attn_rel_bias': 'grad_w', 'grad_hgrn_lower_bounds': 'grad_w', 'grad_hgrn_out_norm_g': 'grad_w', 'grad_w_out': 'grad_w', 'grad_ffn2_norm_g': 'grad_w', 'grad_ffn2_w_gate': 'grad_w', 'grad_ffn2_w_up': 'grad_w', 'grad_ffn2_w_down': 'grad_w', 'delta_ffn1_norm_g': 'delta_w', 'delta_ffn1_w_gate': 'delta_w', 'delta_ffn1_w_up': 'delta_w', 'delta_ffn1_w_down': 'delta_w', 'delta_mix_norm_g': 'delta_w', 'delta_w_in': 'delta_w', 'delta_attn_q_norm_g': 'delta_w', 'delta_attn_k_norm_g': 'delta_w', 'delta_attn_rel_bias': 'delta_w', 'delta_hgrn_lower_bounds': 'delta_w', 'delta_hgrn_out_norm_g': 'delta_w', 'delta_w_out': 'delta_w', 'delta_ffn2_norm_g': 'delta_w', 'delta_ffn2_w_gate': 'delta_w', 'delta_ffn2_w_up': 'delta_w', 'delta_ffn2_w_down': 'delta_w', 'new_m_ffn1_norm_g': 'new_m', 'new_m_ffn1_w_gate': 'new_m', 'new_m_ffn1_w_up': 'new_m', 'new_m_ffn1_w_down': 'new_m', 'new_m_mix_norm_g': 'new_m', 'new_m_w_in': 'new_m', 'new_m_attn_q_norm_g': 'new_m', 'new_m_attn_k_norm_g': 'new_m', 'new_m_attn_rel_bias': 'new_m', 'new_m_hgrn_lower_bounds': 'new_m', 'new_m_hgrn_out_norm_g': 'new_m', 'new_m_w_out': 'new_m', 'new_m_ffn2_norm_g': 'new_m', 'new_m_ffn2_w_gate': 'new_m', 'new_m_ffn2_w_up': 'new_m', 'new_m_ffn2_w_down': 'new_m', 'new_v_ffn1_norm_g': 'new_v', 'new_v_ffn1_w_gate': 'new_v', 'new_v_ffn1_w_up': 'new_v', 'new_v_ffn1_w_down': 'new_v', 'new_v_mix_norm_g': 'new_v', 'new_v_w_in': 'new_v', 'new_v_attn_q_norm_g': 'new_v', 'new_v_attn_k_norm_g': 'new_v', 'new_v_attn_rel_bias': 'new_v', 'new_v_hgrn_lower_bounds': 'new_v', 'new_v_hgrn_out_norm_g': 'new_v', 'new_v_w_out': 'new_v', 'new_v_ffn2_norm_g': 'new_v', 'new_v_ffn2_w_gate': 'new_v', 'new_v_ffn2_w_up': 'new_v', 'new_v_ffn2_w_down': 'new_v'}


def _forward(args):
    return _fwd_reference(*[args[k] for k in FWD_PARAMS])


def _output_shape():
    out = _jax.eval_shape(lambda: _forward(_fwd_setup_inputs(0)))
    return out.shape, out.dtype

N_MICROBATCH = 1
ADAM_LR = 0.001
ADAM_B1 = 0.9
ADAM_B2 = 0.999
ADAM_EPS = 1e-08
ADAM_WD = 0.01
ADAM_STEP = 10
PER_EXAMPLE_BATCH_AXIS = {'x': 0, 'loss_target': 0}
SHARED_INPUTS = []
_WEIGHT_DTYPES = {'ffn1_norm_g': _jnp.float32, 'ffn1_w_gate': _jnp.float32, 'ffn1_w_up': _jnp.float32, 'ffn1_w_down': _jnp.float32, 'mix_norm_g': _jnp.float32, 'w_in': _jnp.float32, 'attn_q_norm_g': _jnp.float32, 'attn_k_norm_g': _jnp.float32, 'attn_rel_bias': _jnp.float32, 'hgrn_lower_bounds': _jnp.float32, 'hgrn_out_norm_g': _jnp.float32, 'w_out': _jnp.float32, 'ffn2_norm_g': _jnp.float32, 'ffn2_w_gate': _jnp.float32, 'ffn2_w_up': _jnp.float32, 'ffn2_w_down': _jnp.float32}
MOMENT_SCALE = {'ffn1_norm_g': 6.165473e+00, 'ffn1_w_gate': 7.628701e-02, 'ffn1_w_up': 8.152428e-02, 'ffn1_w_down': 1.333428e-01, 'mix_norm_g': 6.549493e+00, 'w_in': 1.631518e-01, 'attn_q_norm_g': 8.031659e-01, 'attn_k_norm_g': 8.036360e-01, 'attn_rel_bias': 2.050538e-02, 'hgrn_lower_bounds': 2.260931e-02, 'hgrn_out_norm_g': 4.530772e+01, 'w_out': 2.014389e-01, 'ffn2_norm_g': 6.216447e+00, 'ffn2_w_gate': 5.782382e-02, 'ffn2_w_up': 6.802872e-02, 'ffn2_w_down': 1.102180e-01}


def _to_microbatches(a, axis):
    t = _jnp.moveaxis(a, axis, 0)
    t = t.reshape((N_MICROBATCH, t.shape[0] // N_MICROBATCH) + t.shape[1:])
    return _jnp.moveaxis(t, 1, axis + 1)


def setup_inputs(seed: int = 0) -> dict:
    inp = _fwd_setup_inputs(seed)
    key = _jax.random.fold_in(_jax.random.key(seed), 7919)
    shape, _ = _output_shape()
    out = dict(inp)
    out["loss_target"] = _jax.random.normal(_jax.random.fold_in(key, 0), shape, _jnp.float32)
    for i, name in enumerate(TWIN_WEIGHTS):
        w = inp[name].astype(_jnp.float32)
        if MOMENT_SCALE is None:
            s = _jnp.sqrt(_jnp.mean(_jnp.square(w)) + 1e-30)
        else:
            s = MOMENT_SCALE[name]
        km, kv = _jax.random.split(_jax.random.fold_in(key, i + 1))
        out[name] = w
        out["m_" + name] = s * _jax.random.normal(km, w.shape, _jnp.float32)
        out["v_" + name] = (s * s) * _jax.random.uniform(kv, w.shape, _jnp.float32, 0.5, 1.5)
    if N_MICROBATCH > 1:
        for name, axis in PER_EXAMPLE_BATCH_AXIS.items():
            out[name] = _to_microbatches(out[name], axis)
    return {'x': out['x'], 'ffn1_norm_g': out['ffn1_norm_g'], 'ffn1_w_gate': out['ffn1_w_gate'], 'ffn1_w_up': out['ffn1_w_up'], 'ffn1_w_down': out['ffn1_w_down'], 'mix_norm_g': out['mix_norm_g'], 'w_in': out['w_in'], 'attn_q_norm_g': out['attn_q_norm_g'], 'attn_k_norm_g': out['attn_k_norm_g'], 'attn_rel_bias': out['attn_rel_bias'], 'hgrn_lower_bounds': out['hgrn_lower_bounds'], 'hgrn_out_norm_g': out['hgrn_out_norm_g'], 'w_out': out['w_out'], 'ffn2_norm_g': out['ffn2_norm_g'], 'ffn2_w_gate': out['ffn2_w_gate'], 'ffn2_w_up': out['ffn2_w_up'], 'ffn2_w_down': out['ffn2_w_down'], 'loss_target': out['loss_target'], 'm_ffn1_norm_g': out['m_ffn1_norm_g'], 'm_ffn1_w_gate': out['m_ffn1_w_gate'], 'm_ffn1_w_up': out['m_ffn1_w_up'], 'm_ffn1_w_down': out['m_ffn1_w_down'], 'm_mix_norm_g': out['m_mix_norm_g'], 'm_w_in': out['m_w_in'], 'm_attn_q_norm_g': out['m_attn_q_norm_g'], 'm_attn_k_norm_g': out['m_attn_k_norm_g'], 'm_attn_rel_bias': out['m_attn_rel_bias'], 'm_hgrn_lower_bounds': out['m_hgrn_lower_bounds'], 'm_hgrn_out_norm_g': out['m_hgrn_out_norm_g'], 'm_w_out': out['m_w_out'], 'm_ffn2_norm_g': out['m_ffn2_norm_g'], 'm_ffn2_w_gate': out['m_ffn2_w_gate'], 'm_ffn2_w_up': out['m_ffn2_w_up'], 'm_ffn2_w_down': out['m_ffn2_w_down'], 'v_ffn1_norm_g': out['v_ffn1_norm_g'], 'v_ffn1_w_gate': out['v_ffn1_w_gate'], 'v_ffn1_w_up': out['v_ffn1_w_up'], 'v_ffn1_w_down': out['v_ffn1_w_down'], 'v_mix_norm_g': out['v_mix_norm_g'], 'v_w_in': out['v_w_in'], 'v_attn_q_norm_g': out['v_attn_q_norm_g'], 'v_attn_k_norm_g': out['v_attn_k_norm_g'], 'v_attn_rel_bias': out['v_attn_rel_bias'], 'v_hgrn_lower_bounds': out['v_hgrn_lower_bounds'], 'v_hgrn_out_norm_g': out['v_hgrn_out_norm_g'], 'v_w_out': out['v_w_out'], 'v_ffn2_norm_g': out['v_ffn2_norm_g'], 'v_ffn2_w_gate': out['v_ffn2_w_gate'], 'v_ffn2_w_up': out['v_ffn2_w_up'], 'v_ffn2_w_down': out['v_ffn2_w_down']}


def _loss(weights, diff, rest, loss_target):
    with _jax.named_scope("forward"):
        args = {**rest, TWIN_DIFF_INPUT: diff, **{k: w.astype(_WEIGHT_DTYPES[k]) for k, w in weights.items()}}
        y = _forward(args)
    with _jax.named_scope("loss_head"):
        err = _jnp.square(y.astype(_jnp.float32) - loss_target)
        return 0.5 * _jnp.sum(_jnp.mean(err, axis=-1)) if err.ndim else 0.5 * err


def _adamw(w, g, m, v):
    m = ADAM_B1 * m + (1.0 - ADAM_B1) * g
    v = ADAM_B2 * v + (1.0 - ADAM_B2) * _jnp.square(g)
    m_hat = m / (1.0 - ADAM_B1 ** ADAM_STEP)
    v_hat = v / (1.0 - ADAM_B2 ** ADAM_STEP)
    delta = -ADAM_LR * (m_hat / (_jnp.sqrt(v_hat) + ADAM_EPS) + ADAM_WD * w)
    return delta, m, v


def reference(x, ffn1_norm_g, ffn1_w_gate, ffn1_w_up, ffn1_w_down, mix_norm_g, w_in, attn_q_norm_g, attn_k_norm_g, attn_rel_bias, hgrn_lower_bounds, hgrn_out_norm_g, w_out, ffn2_norm_g, ffn2_w_gate, ffn2_w_up, ffn2_w_down, loss_target, m_ffn1_norm_g, m_ffn1_w_gate, m_ffn1_w_up, m_ffn1_w_down, m_mix_norm_g, m_w_in, m_attn_q_norm_g, m_attn_k_norm_g, m_attn_rel_bias, m_hgrn_lower_bounds, m_hgrn_out_norm_g, m_w_out, m_ffn2_norm_g, m_ffn2_w_gate, m_ffn2_w_up, m_ffn2_w_down, v_ffn1_norm_g, v_ffn1_w_gate, v_ffn1_w_up, v_ffn1_w_down, v_mix_norm_g, v_w_in, v_attn_q_norm_g, v_attn_k_norm_g, v_attn_rel_bias, v_hgrn_lower_bounds, v_hgrn_out_norm_g, v_w_out, v_ffn2_norm_g, v_ffn2_w_gate, v_ffn2_w_up, v_ffn2_w_down):
    given = dict(x=x, ffn1_norm_g=ffn1_norm_g, ffn1_w_gate=ffn1_w_gate, ffn1_w_up=ffn1_w_up, ffn1_w_down=ffn1_w_down, mix_norm_g=mix_norm_g, w_in=w_in, attn_q_norm_g=attn_q_norm_g, attn_k_norm_g=attn_k_norm_g, attn_rel_bias=attn_rel_bias, hgrn_lower_bounds=hgrn_lower_bounds, hgrn_out_norm_g=hgrn_out_norm_g, w_out=w_out, ffn2_norm_g=ffn2_norm_g, ffn2_w_gate=ffn2_w_gate, ffn2_w_up=ffn2_w_up, ffn2_w_down=ffn2_w_down, loss_target=loss_target, m_ffn1_norm_g=m_ffn1_norm_g, m_ffn1_w_gate=m_ffn1_w_gate, m_ffn1_w_up=m_ffn1_w_up, m_ffn1_w_down=m_ffn1_w_down, m_mix_norm_g=m_mix_norm_g, m_w_in=m_w_in, m_attn_q_norm_g=m_attn_q_norm_g, m_attn_k_norm_g=m_attn_k_norm_g, m_attn_rel_bias=m_attn_rel_bias, m_hgrn_lower_bounds=m_hgrn_lower_bounds, m_hgrn_out_norm_g=m_hgrn_out_norm_g, m_w_out=m_w_out, m_ffn2_norm_g=m_ffn2_norm_g, m_ffn2_w_gate=m_ffn2_w_gate, m_ffn2_w_up=m_ffn2_w_up, m_ffn2_w_down=m_ffn2_w_down, v_ffn1_norm_g=v_ffn1_norm_g, v_ffn1_w_gate=v_ffn1_w_gate, v_ffn1_w_up=v_ffn1_w_up, v_ffn1_w_down=v_ffn1_w_down, v_mix_norm_g=v_mix_norm_g, v_w_in=v_w_in, v_attn_q_norm_g=v_attn_q_norm_g, v_attn_k_norm_g=v_attn_k_norm_g, v_attn_rel_bias=v_attn_rel_bias, v_hgrn_lower_bounds=v_hgrn_lower_bounds, v_hgrn_out_norm_g=v_hgrn_out_norm_g, v_w_out=v_w_out, v_ffn2_norm_g=v_ffn2_norm_g, v_ffn2_w_gate=v_ffn2_w_gate, v_ffn2_w_up=v_ffn2_w_up, v_ffn2_w_down=v_ffn2_w_down)
    weights = {n: given[n] for n in TWIN_WEIGHTS}
    shared = {n: given[n] for n in SHARED_INPUTS}
    per_example = {n: given[n] for n in ['x']}
    grad_fn = _jax.value_and_grad(_loss, argnums=(0, 1))

    def one_microbatch(ex, loss_target):
        ex = dict(ex)
        diff = ex.pop(TWIN_DIFF_INPUT)
        return grad_fn(weights, diff, {**shared, **ex}, loss_target)

    if N_MICROBATCH == 1:
        loss, (grad_w, grad_x) = one_microbatch(per_example, given["loss_target"])
    else:
        def body(carry, xs):
            loss_sum, grad_sum = carry
            l_k, (gw_k, gx_k) = one_microbatch(xs[0], xs[1])
            with _jax.named_scope("update"):
                return (loss_sum + l_k, _jax.tree.map(_jnp.add, grad_sum, gw_k)), gx_k

        init = (_jnp.zeros((), _jnp.float32), _jax.tree.map(_jnp.zeros_like, weights))
        (loss, grad_w), grad_x = _jax.lax.scan(body, init, (per_example, given["loss_target"]))
    with _jax.named_scope("update"):
        delta_w, new_m, new_v = {}, {}, {}
        for n in TWIN_WEIGHTS:
            delta_w[n], new_m[n], new_v[n] = _adamw(weights[n], grad_w[n], given["m_" + n], given["v_" + n])
    return (loss, grad_x, *[grad_w[n] for n in TWIN_WEIGHTS], *[delta_w[n] for n in TWIN_WEIGHTS],
            *[new_m[n] for n in TWIN_WEIGHTS], *[new_v[n] for n in TWIN_WEIGHTS])
```

```python
import functools

import jax
import jax.numpy as jnp
from jax import lax
from jax.experimental import pallas as pl
from jax.experimental.pallas import tpu as pltpu

F32 = jnp.float32
BF16 = jnp.bfloat16
MESH = pl.DeviceIdType.MESH

N_CHIPS = 4
N_DEV = 8
CHUNK = 64
ATTN_HEADS = 8
ATTN_DH = 64
ATTN_W = ATTN_HEADS * ATTN_DH
HGRN_HEADS = 4
HGRN_DH = 128
HGRN_W = HGRN_HEADS * HGRN_DH
LEFT_CHUNKS = 8
BAND = (LEFT_CHUNKS + 1) * CHUNK
KPAD = LEFT_CHUNKS * CHUNK
REL_CLIP = 128
N_REL = 2 * REL_CLIP + 1
N_REL_PAD = 384
RMS_EPS = 1e-6
LANES = 128
SMALL_ROWS = 8
SMALL_COLS = 1024

ADAM_LR = 0.001
ADAM_B1 = 0.9
ADAM_B2 = 0.999
ADAM_EPS = 1e-08
ADAM_WD = 0.01
ADAM_STEP = 10

NN = (((1,), (0,)), ((), ()))
NT = (((1,), (1,)), ((), ()))
TN = (((0,), (0,)), ((), ()))

VMEM_LIMIT = 48 * 1024 * 1024


def _sigmoid(x):
    return 1.0 / (1.0 + jnp.exp(-x))


def _silu(x):
    return x * _sigmoid(x)


def _dsilu(x):
    s = _sigmoid(x)
    return s * (1.0 + x * (1.0 - s))


def _dot(a, b, dims=NN):
    return lax.dot_general(a, b, dims, preferred_element_type=F32)


def _split3(x):
    hi = x.astype(BF16)
    r1 = x - hi.astype(F32)
    mid = r1.astype(BF16)
    lo = (r1 - mid.astype(F32)).astype(BF16)
    return hi, mid, lo


def _dot_exact_rhs(x, mat, dims=NN):
    hi, mid, lo = _split3(x)
    return _dot(hi, mat, dims) + _dot(mid, mat, dims) + _dot(lo, mat, dims)


def _dot_exact_lhs(mat, x, dims=NN):
    hi, mid, lo = _split3(x)
    return _dot(mat, hi, dims) + _dot(mat, mid, dims) + _dot(mat, lo, dims)


def _params(*sem):
    return pltpu.CompilerParams(dimension_semantics=sem, vmem_limit_bytes=VMEM_LIMIT)


def _mm(name, ins, terms, n_acc, grid, acc_shape, outs, epilogue, extras=()):
    nk = grid[2]
    ni, ne, no = len(ins), len(extras), len(outs)

    def body(*refs):
        in_refs = refs[:ni]
        ex_refs = refs[ni:ni + ne]
        out_refs = refs[ni + ne:ni + ne + no]
        acc_refs = refs[ni + ne + no:]
        parts = [None] * n_acc
        for ai, li, ri, dims in terms:
            d = _dot(in_refs[li][...], in_refs[ri][...], dims)
            parts[ai] = d if parts[ai] is None else parts[ai] + d

        def finish(accs):
            res = epilogue(accs, [e[...] for e in ex_refs])
            for o, r in zip(out_refs, res):
                o[...] = r.astype(o.dtype)

        if nk == 1:
            finish(parts)
        else:
            k = pl.program_id(2)

            @pl.when(k == 0)
            def _():
                for a, p in zip(acc_refs, parts):
                    a[...] = p

            @pl.when(k > 0)
            def _():
                for a, p in zip(acc_refs, parts):
                    a[...] += p

            @pl.when(k == nk - 1)
            def _():
                finish([a[...] for a in acc_refs])

    scratch = [] if nk == 1 else [pltpu.VMEM(acc_shape, F32) for _ in range(n_acc)]
    res = pl.pallas_call(
        body,
        name=name,
        grid=grid,
        in_specs=[s for _, s in ins] + [s for _, s in extras],
        out_specs=[s for _, s in outs],
        out_shape=[o for o, _ in outs],
        scratch_shapes=scratch,
        compiler_params=_params("parallel", "parallel", "arbitrary"),
    )(*[a for a, _ in ins], *[a for a, _ in extras])
    return res


def _row_tile(t):
    return 512 if t % 512 == 0 else t


def _k_tile(t):
    return 1024 if t % 1024 == 0 else t


def _rmsnorm_fwd(name, x, g):
    t, d = x.shape
    tm = _row_tile(t)

    def body(x_ref, g_ref, h_ref):
        xv = x_ref[...]
        ms = jnp.mean(xv * xv, axis=-1, keepdims=True)
        h_ref[...] = (xv * lax.rsqrt(ms + RMS_EPS) * g_ref[...]).astype(BF16)

    return pl.pallas_call(
        body,
        name=name,
        grid=(t // tm,),
        in_specs=[pl.BlockSpec((tm, d), lambda i: (i, 0)), pl.BlockSpec((1, d), lambda i: (0, 0))],
        out_specs=pl.BlockSpec((tm, d), lambda i: (i, 0)),
        out_shape=jax.ShapeDtypeStruct((t, d), BF16),
        compiler_params=_params("parallel"),
    )(x, g)


def _norm_bwd_epilogue(copy_scale):
    def epilogue(accs, ex):
        dh = accs[0]
        xv, g, dres = ex
        ms = jnp.mean(xv * xv, axis=-1, keepdims=True)
        rstd = lax.rsqrt(ms + RMS_EPS)
        xhat = xv * rstd
        dxhat = dh * g
        dx = rstd * (dxhat - xhat * jnp.mean(dxhat * xhat, axis=-1, keepdims=True))
        out = dres + dx
        dg = jnp.sum(dh * xhat, axis=0, keepdims=True)
        if copy_scale is None:
            return out, dg
        return out, out * copy_scale, dg

    return epilogue


def _ffn_up(name, h, wg, wu):
    t, d = h.shape
    ns, _, f = wg.shape
    tm = _row_tile(t)

    def epilogue(accs, ex):
        a, b = accs
        return a, b, _silu(a) * b

    w_spec = pl.BlockSpec((None, d, f), lambda j, i, k: (j, 0, 0))
    o_spec = pl.BlockSpec((None, tm, f), lambda j, i, k: (j, i, 0))
    o_shape = jax.ShapeDtypeStruct((ns, t, f), BF16)
    return _mm(
        name,
        ins=[(h, pl.BlockSpec((tm, d), lambda j, i, k: (i, 0))), (wg, w_spec), (wu, w_spec)],
        terms=[(0, 0, 1, NN), (1, 0, 2, NN)],
        n_acc=2,
        grid=(ns, t // tm, 1),
        acc_shape=(tm, f),
        outs=[(o_shape, o_spec)] * 3,
        epilogue=epilogue,
    )


def _ffn_down(name, z, wd, x):
    ns, t, f = z.shape
    d = wd.shape[2]
    tm = _row_tile(t)
    row = pl.BlockSpec((tm, d), lambda i, n, k: (i, 0))
    return _mm(
        name,
        ins=[(z, pl.BlockSpec((None, tm, f), lambda i, n, k: (k, i, 0))),
             (wd, pl.BlockSpec((None, f, d), lambda i, n, k: (k, 0, 0)))],
        terms=[(0, 0, 1, NN)],
        n_acc=1,
        grid=(t // tm, 1, ns),
        acc_shape=(tm, d),
        outs=[(jax.ShapeDtypeStruct((t, d), F32), row)],
        epilogue=lambda accs, ex: (ex[0] + 0.5 * accs[0],),
        extras=[(x, row)],
    )[0]


def _ffn_down_loss(name, z, wd, x, target):
    ns, t, f = z.shape
    d = wd.shape[2]
    tm = _row_tile(t)
    nt = t // tm
    row = pl.BlockSpec((tm, d), lambda i, n, k: (i, 0))

    def epilogue(accs, ex):
        e = ex[0] + 0.5 * accs[0] - ex[1]
        dy = e * (1.0 / d)
        return dy, 0.5 * dy, jnp.sum(e * e, axis=0, keepdims=True)

    return _mm(
        name,
        ins=[(z, pl.BlockSpec((None, tm, f), lambda i, n, k: (k, i, 0))),
             (wd, pl.BlockSpec((None, f, d), lambda i, n, k: (k, 0, 0)))],
        terms=[(0, 0, 1, NN)],
        n_acc=1,
        grid=(nt, 1, ns),
        acc_shape=(tm, d),
        outs=[(jax.ShapeDtypeStruct((t, d), F32), row), (jax.ShapeDtypeStruct((t, d), BF16), row),
              (jax.ShapeDtypeStruct((nt, 1, d), F32), pl.BlockSpec((None, 1, d), lambda i, n, k: (i, 0, 0)))],
        epilogue=epilogue,
        extras=[(x, row), (target, row)],
    )


def _ffn_bwd_act(name, dout, wd, a, b):
    t, d = dout.shape
    ns, f, _ = wd.shape
    tm = _row_tile(t)

    def epilogue(accs, ex):
        dz = accs[0]
        av = ex[0].astype(F32)
        bv = ex[1].astype(F32)
        return dz * bv * _dsilu(av), dz * _silu(av)

    act = pl.BlockSpec((None, tm, f), lambda j, i, k: (j, i, 0))
    o_shape = jax.ShapeDtypeStruct((ns, t, f), BF16)
    return _mm(
        name,
        ins=[(dout, pl.BlockSpec((tm, d), lambda j, i, k: (i, 0))),
             (wd, pl.BlockSpec((None, f, d), lambda j, i, k: (j, 0, 0)))],
        terms=[(0, 0, 1, NT)],
        n_acc=1,
        grid=(ns, t // tm, 1),
        acc_shape=(tm, f),
        outs=[(o_shape, act)] * 2,
        epilogue=epilogue,
        extras=[(a, act), (b, act)],
    )


def _grad_w_shardcols(name, h, da):
    t, d = h.shape
    ns, _, f = da.shape
    tk = _k_tile(t)
    return _mm(
        name,
        ins=[(h, pl.BlockSpec((tk, d), lambda j, n, k: (k, 0))),
             (da, pl.BlockSpec((None, tk, f), lambda j, n, k: (j, k, 0)))],
        terms=[(0, 0, 1, TN)],
        n_acc=1,
        grid=(ns, 1, t // tk),
        acc_shape=(d, f),
        outs=[(jax.ShapeDtypeStruct((ns, d, f), F32), pl.BlockSpec((None, d, f), lambda j, n, k: (j, 0, 0)))],
        epilogue=lambda accs, ex: (accs[0],),
    )[0]


def _grad_w_shardrows(name, z, dout):
    ns, t, f = z.shape
    d = dout.shape[1]
    tk = _k_tile(t)
    return _mm(
        name,
        ins=[(z, pl.BlockSpec((None, tk, f), lambda j, n, k: (j, k, 0))),
             (dout, pl.BlockSpec((tk, d), lambda j, n, k: (k, 0)))],
        terms=[(0, 0, 1, TN)],
        n_acc=1,
        grid=(ns, 1, t // tk),
        acc_shape=(f, d),
        outs=[(jax.ShapeDtypeStruct((ns, f, d), F32), pl.BlockSpec((None, f, d), lambda j, n, k: (j, 0, 0)))],
        epilogue=lambda accs, ex: (accs[0],),
    )[0]


def _ffn_bwd_in(name, da, db, wg, wu, x, g, dres, copy_scale):
    ns, t, f = da.shape
    d = wg.shape[1]
    tm = _row_tile(t)
    nt = t // tm
    act = pl.BlockSpec((None, tm, f), lambda i, n, k: (k, i, 0))
    w_spec = pl.BlockSpec((None, d, f), lambda i, n, k: (k, 0, 0))
    row = pl.BlockSpec((tm, d), lambda i, n, k: (i, 0))
    outs = [(jax.ShapeDtypeStruct((t, d), F32), row)]
    if copy_scale is not None:
        outs.append((jax.ShapeDtypeStruct((t, d), BF16), row))
    outs.append((jax.ShapeDtypeStruct((nt, 1, d), F32), pl.BlockSpec((None, 1, d), lambda i, n, k: (i, 0, 0))))
    return _mm(
        name,
        ins=[(da, act), (db, act), (wg, w_spec), (wu, w_spec)],
        terms=[(0, 0, 2, NT), (0, 1, 3, NT)],
        n_acc=1,
        grid=(nt, 1, ns),
        acc_shape=(tm, d),
        outs=outs,
        epilogue=_norm_bwd_epilogue(copy_scale),
        extras=[(x, row), (g, pl.BlockSpec((1, d), lambda i, n, k: (0, 0))), (dres, row)],
    )


def _in_proj(name, h, w_in):
    t, d = h.shape
    ns, _, pj = w_in.shape
    tm = _row_tile(t)
    return _mm(
        name,
        ins=[(h, pl.BlockSpec((tm, d), lambda j, i, k: (i, 0))),
             (w_in, pl.BlockSpec((None, d, pj), lambda j, i, k: (j, 0, 0)))],
        terms=[(0, 0, 1, NN)],
        n_acc=1,
        grid=(ns, t // tm, 1),
        acc_shape=(tm, pj),
        outs=[(jax.ShapeDtypeStruct((t, ns * pj), F32), pl.BlockSpec((tm, pj), lambda j, i, k: (i, j)))],
        epilogue=lambda accs, ex: (accs[0],),
    )[0]


def _in_proj_bwd(name, dp, w_in, x, g, dres, copy_scale):
    t = dp.shape[0]
    ns, d, pj = w_in.shape
    tm = _row_tile(t)
    nt = t // tm
    row = pl.BlockSpec((tm, d), lambda i, n, k: (i, 0))
    outs = [(jax.ShapeDtypeStruct((t, d), F32), row)]
    if copy_scale is not None:
        outs.append((jax.ShapeDtypeStruct((t, d), BF16), row))
    outs.append((jax.ShapeDtypeStruct((nt, 1, d), F32), pl.BlockSpec((None, 1, d), lambda i, n, k: (i, 0, 0))))
    return _mm(
        name,
        ins=[(dp, pl.BlockSpec((tm, pj), lambda i, n, k: (i, k))),
             (w_in, pl.BlockSpec((None, d, pj), lambda i, n, k: (k, 0, 0)))],
        terms=[(0, 0, 1, NT)],
        n_acc=1,
        grid=(nt, 1, ns),
        acc_shape=(tm, d),
        outs=outs,
        epilogue=_norm_bwd_epilogue(copy_scale),
        extras=[(x, row), (g, pl.BlockSpec((1, d), lambda i, n, k: (0, 0))), (dres, row)],
    )


def _grad_w_in(name, h, dp, ns):
    t, d = h.shape
    pj = dp.shape[1] // ns
    tk = _k_tile(t)
    return _mm(
        name,
        ins=[(h, pl.BlockSpec((tk, d), lambda j, n, k: (k, 0))),
             (dp, pl.BlockSpec((tk, pj), lambda j, n, k: (k, j)))],
        terms=[(0, 0, 1, TN)],
        n_acc=1,
        grid=(ns, 1, t // tk),
        acc_shape=(d, pj),
        outs=[(jax.ShapeDtypeStruct((ns, d, pj), F32), pl.BlockSpec((None, d, pj), lambda j, n, k: (j, 0, 0)))],
        epilogue=lambda accs, ex: (accs[0],),
    )[0]


def _out_proj(name, mix, w_out, x):
    t, dm = mix.shape
    d = w_out.shape[1]
    tm = _row_tile(t)
    row = pl.BlockSpec((tm, d), lambda i, n, k: (i, 0))
    return _mm(
        name,
        ins=[(mix, pl.BlockSpec((tm, dm), lambda i, n, k: (i, 0))),
             (w_out, pl.BlockSpec((dm, d), lambda i, n, k: (0, 0)))],
        terms=[(0, 0, 1, NN)],
        n_acc=1,
        grid=(t // tm, 1, 1),
        acc_shape=(tm, d),
        outs=[(jax.ShapeDtypeStruct((t, d), F32), row)],
        epilogue=lambda accs, ex: (ex[0] + accs[0],),
        extras=[(x, row)],
    )[0]


def _out_proj_bwd(name, dx, w_out):
    t, d = dx.shape
    dm = w_out.shape[0]
    tm = _row_tile(t)
    return _mm(
        name,
        ins=[(dx, pl.BlockSpec((tm, d), lambda i, n, k: (i, 0))),
             (w_out, pl.BlockSpec((dm, d), lambda i, n, k: (0, 0)))],
        terms=[(0, 0, 1, NT)],
        n_acc=1,
        grid=(t // tm, 1, 1),
        acc_shape=(tm, dm),
        outs=[(jax.ShapeDtypeStruct((t, dm), F32), pl.BlockSpec((tm, dm), lambda i, n, k: (i, 0)))],
        epilogue=lambda accs, ex: (accs[0],),
    )[0]


def _grad_w_out(name, mix, dx):
    t, dm = mix.shape
    d = dx.shape[1]
    tk = _k_tile(t)
    return _mm(
        name,
        ins=[(mix, pl.BlockSpec((tk, dm), lambda a, n, k: (k, 0))),
             (dx, pl.BlockSpec((tk, d), lambda a, n, k: (k, 0)))],
        terms=[(0, 0, 1, TN)],
        n_acc=1,
        grid=(1, 1, t // tk),
        acc_shape=(dm, d),
        outs=[(jax.ShapeDtypeStruct((dm, d), F32), pl.BlockSpec((dm, d), lambda a, n, k: (0, 0)))],
        epilogue=lambda accs, ex: (accs[0],),
    )[0]


def _head_group_matrix():
    r = lax.broadcasted_iota(jnp.int32, (ATTN_W, ATTN_W), 0)
    c = lax.broadcasted_iota(jnp.int32, (ATTN_W, ATTN_W), 1)
    same = jnp.right_shift(r, 6) == jnp.right_shift(c, 6)
    return jnp.where(same, 1.0, 0.0).astype(BF16)


def _qk_prep(name, proj, gq, gk):
    t = proj.shape[0]
    tm = _row_tile(t)

    def body(q_ref, k_ref, v_ref, gq_ref, gk_ref, qn_ref, kn_ref, vb_ref):
        bd = _head_group_matrix()

        def norm(xv, g):
            ms = _dot_exact_rhs(xv * xv, bd) * (1.0 / ATTN_DH)
            return xv * lax.rsqrt(ms + RMS_EPS) * g

        qn_ref[...] = norm(q_ref[...], gq_ref[...]).astype(BF16)
        kn_ref[...] = norm(k_ref[...], gk_ref[...]).astype(BF16)
        vb_ref[...] = v_ref[...].astype(BF16)

    col = lambda j: pl.BlockSpec((tm, ATTN_W), lambda i: (i, j))
    gspec = pl.BlockSpec((1, ATTN_W), lambda i: (0, 0))
    o_shape = jax.ShapeDtypeStruct((t, ATTN_W), BF16)
    return pl.pallas_call(
        body,
        name=name,
        grid=(t // tm,),
        in_specs=[col(0), col(1), col(2), gspec, gspec],
        out_specs=[col(0)] * 3,
        out_shape=[o_shape] * 3,
        compiler_params=_params("parallel"),
    )(proj, proj, proj, gq, gk)


def _qk_prep_bwd(name, proj, dqn, dkn, dv, gq, gk):
    t = proj.shape[0]
    tm = _row_tile(t)
    nt = t // tm

    def body(q_ref, k_ref, dqn_ref, dkn_ref, dv_ref, gq_ref, gk_ref, dq_ref, dk_ref, dvb_ref, dgq_ref, dgk_ref):
        bd = _head_group_matrix()

        def bwd(xv, dy, g):
            ms = _dot_exact_rhs(xv * xv, bd) * (1.0 / ATTN_DH)
            rstd = lax.rsqrt(ms + RMS_EPS)
            xhat = xv * rstd
            dxhat = dy * g
            gm = _dot_exact_rhs(dxhat * xhat, bd) * (1.0 / ATTN_DH)
            return rstd * (dxhat - xhat * gm), jnp.sum(dy * xhat, axis=0, keepdims=True)

        dq, dgq = bwd(q_ref[...], dqn_ref[...], gq_ref[...])
        dk, dgk = bwd(k_ref[...], dkn_ref[...], gk_ref[...])
        dq_ref[...] = dq.astype(BF16)
        dk_ref[...] = dk.astype(BF16)
        dvb_ref[...] = dv_ref[...].astype(BF16)
        dgq_ref[...] = dgq
        dgk_ref[...] = dgk

    col = lambda j: pl.BlockSpec((tm, ATTN_W), lambda i: (i, j))
    gspec = pl.BlockSpec((1, ATTN_W), lambda i: (0, 0))
    pspec = pl.BlockSpec((None, 1, ATTN_W), lambda i: (i, 0, 0))
    o_shape = jax.ShapeDtypeStruct((t, ATTN_W), BF16)
    p_shape = jax.ShapeDtypeStruct((nt, 1, ATTN_W), F32)
    return pl.pallas_call(
        body,
        name=name,
        grid=(nt,),
        in_specs=[col(0), col(1), col(0), col(0), col(0), gspec, gspec],
        out_specs=[col(0)] * 3 + [pspec] * 2,
        out_shape=[o_shape] * 3 + [p_shape] * 2,
        compiler_params=_params("parallel"),
    )(proj, proj, dqn, dkn, dv, gq, gk)


def _attn_probs(q, kb, bias, start):
    s = _dot(q, kb, NT) * (ATTN_DH ** -0.5) + bias
    col = lax.broadcasted_iota(jnp.int32, (CHUNK, BAND), 1)
    s = jnp.where(col + start >= KPAD, s, -jnp.inf)
    m = jnp.max(s, axis=-1, keepdims=True)
    p = jnp.exp(s - m)
    return p / jnp.sum(p, axis=-1, keepdims=True)


def _attn_fwd(name, q, k, v, bias):
    b, h, s, dh = q.shape
    sp = k.shape[2]
    nc = s // CHUNK
    hg = h

    def body(q_ref, k_ref, v_ref, bias_ref, o_ref):
        start = pl.multiple_of(pl.program_id(2) * CHUNK, CHUNK)
        for hh in range(hg):
            kb = k_ref[hh, pl.ds(start, BAND), :]
            vb = v_ref[hh, pl.ds(start, BAND), :]
            p = _attn_probs(q_ref[hh], kb, bias_ref[hh], start)
            o_ref[hh] = _dot(p.astype(BF16), vb).astype(BF16)

    qspec = pl.BlockSpec((None, hg, CHUNK, dh), lambda bi, g, c: (bi, g, c, 0))
    kspec = pl.BlockSpec((None, hg, sp, dh), lambda bi, g, c: (bi, g, 0, 0))
    return pl.pallas_call(
        body,
        name=name,
        grid=(b, h // hg, nc),
        in_specs=[qspec, kspec, kspec, pl.BlockSpec((hg, CHUNK, BAND), lambda bi, g, c: (g, 0, 0))],
        out_specs=qspec,
        out_shape=jax.ShapeDtypeStruct((b, h, s, dh), BF16),
        compiler_params=_params("parallel", "parallel", "arbitrary"),
    )(q, k, v, bias)


def _attn_bwd(name, q, k, v, bias, do):
    b, h, s, dh = q.shape
    sp = k.shape[2]
    nc = s // CHUNK
    hg = 4

    def body(q_ref, k_ref, v_ref, bias_ref, do_ref, dq_ref, dk_ref, dv_ref, dbias_ref):
        bi = pl.program_id(1)
        c = pl.program_id(2)
        start = pl.multiple_of(c * CHUNK, CHUNK)

        @pl.when(c == 0)
        def _():
            dk_ref[...] = jnp.zeros_like(dk_ref)
            dv_ref[...] = jnp.zeros_like(dv_ref)

        @pl.when(jnp.logical_and(c == 0, bi == 0))
        def _():
            dbias_ref[...] = jnp.zeros_like(dbias_ref)

        for hh in range(hg):
            band = pl.ds(start, BAND)
            qh = q_ref[hh]
            kb = k_ref[hh, band, :]
            vb = v_ref[hh, band, :]
            doh = do_ref[hh]
            p = _attn_probs(qh, kb, bias_ref[hh], start)
            dp = _dot(doh, vb, NT)
            ds = p * (dp - jnp.sum(p * dp, axis=-1, keepdims=True))
            dbias_ref[hh] += ds
            dsb = (ds * (ATTN_DH ** -0.5)).astype(BF16)
            dq_ref[hh] = _dot(dsb, kb)
            dk_ref[hh, band, :] += _dot(dsb, qh, TN)
            dv_ref[hh, band, :] += _dot(p.astype(BF16), doh, TN)

    qspec = pl.BlockSpec((None, hg, CHUNK, dh), lambda g, bi, c: (bi, g, c, 0))
    kspec = pl.BlockSpec((None, hg, sp, dh), lambda g, bi, c: (bi, g, 0, 0))
    bspec = pl.BlockSpec((hg, CHUNK, BAND), lambda g, bi, c: (g, 0, 0))
    return pl.pallas_call(
        body,
        name=name,
        grid=(h // hg, b, nc),
        in_specs=[qspec, kspec, kspec, bspec, qspec],
        out_specs=[qspec, kspec, kspec, bspec],
        out_shape=[jax.ShapeDtypeStruct((b, h, s, dh), F32), jax.ShapeDtypeStruct((b, h, sp, dh), F32),
                   jax.ShapeDtypeStruct((b, h, sp, dh), F32), jax.ShapeDtypeStruct((h, CHUNK, BAND), F32)],
        compiler_params=_params("arbitrary", "arbitrary", "arbitrary"),
    )(q, k, v, bias, do)


HQ_COL = 3 * ATTN_W // HGRN_DH
HF_COL = HQ_COL + HGRN_HEADS
HI_COL = HF_COL + HGRN_HEADS
HG_COL = HI_COL + HGRN_HEADS


def _tri(lower):
    r = lax.broadcasted_iota(jnp.int32, (CHUNK, CHUNK), 0)
    c = lax.broadcasted_iota(jnp.int32, (CHUNK, CHUNK), 1)
    return (r >= c) if lower else (r <= c)


def _hgrn_chunk(hq, hf, lb, tril):
    sig = _sigmoid(hf)
    f = lb + (1.0 - lb) * sig
    g = jnp.log(f)
    ones_l = jnp.where(tril, 1.0, 0.0).astype(BF16)
    b = _dot_exact_lhs(ones_l, g)
    bl = jnp.sum(g, axis=0, keepdims=True)
    rows = lax.broadcasted_iota(jnp.int32, g.shape, 0)
    bm = jnp.sum(jnp.where(rows <= CHUNK // 2, g, 0.0), axis=0, keepdims=True)
    sq = _sigmoid(hq)
    q = hq * sq
    k = 1.0 - f
    return sig, f, b, bl, bm, sq, q, k


def _hgrn_fwd(name, proj, lb, go, b, s):
    nc = s // CHUNK
    t = b * s

    def body(hq_ref, hf_ref, hi_ref, hg_ref, lb_ref, go_ref, ro_ref, oraw_ref, st_ref, s_scr):
        tril = _tri(True)
        lbv = lb_ref[...]
        gov = go_ref[...]
        s_scr[...] = jnp.zeros_like(s_scr)

        def step(c, carry):
            sl = pl.ds(pl.multiple_of(c * CHUNK, CHUNK), CHUNK)
            hg = hg_ref[sl, :]
            _, _, bb, bl, bm, _, q, k = _hgrn_chunk(hq_ref[sl, :], hf_ref[sl, :], lbv, tril)
            vb = hi_ref[sl, :].astype(BF16)
            qe = (q * jnp.exp(bb - bm)).astype(BF16)
            ke = (k * jnp.exp(bm - bb)).astype(BF16)
            a = jnp.where(tril, _dot(qe, ke, NT), 0.0)
            st = s_scr[...]
            st_ref[c] = st
            qb = (q * jnp.exp(bb)).astype(BF16)
            o = _dot(a.astype(BF16), vb) + _dot(qb, st.astype(BF16), NT)
            kb = (k * jnp.exp(bl - bb)).astype(BF16)
            s_scr[...] = st * jnp.exp(bl) + _dot(vb, kb, TN)
            rstd = lax.rsqrt(jnp.mean(o * o, axis=-1, keepdims=True) + RMS_EPS)
            ro_ref[sl, :] = ((o * rstd * gov) * _silu(hg)).astype(BF16)
            oraw_ref[sl, :] = o
            return carry

        lax.fori_loop(0, nc, step, 0)

    col = lambda base: pl.BlockSpec((s, HGRN_DH), lambda bi, h: (bi, base + h))
    vec = pl.BlockSpec((1, HGRN_DH), lambda bi, h: (0, h))
    out = pl.BlockSpec((s, HGRN_DH), lambda bi, h: (bi, h))
    return pl.pallas_call(
        body,
        name=name,
        grid=(b, HGRN_HEADS),
        in_specs=[col(HQ_COL), col(HF_COL), col(HI_COL), col(HG_COL), vec,
                  pl.BlockSpec((1, HGRN_DH), lambda bi, h: (0, 0))],
        out_specs=[out, out,
                   pl.BlockSpec((None, None, nc, HGRN_DH, HGRN_DH), lambda bi, h: (bi, h, 0, 0, 0))],
        out_shape=[jax.ShapeDtypeStruct((t, HGRN_W), BF16), jax.ShapeDtypeStruct((t, HGRN_W), F32),
                   jax.ShapeDtypeStruct((b, HGRN_HEADS, nc, HGRN_DH, HGRN_DH), F32)],
        scratch_shapes=[pltpu.VMEM((HGRN_DH, HGRN_DH), F32)],
        compiler_params=_params("parallel", "parallel"),
    )(proj, proj, proj, proj, lb, go)


def _hgrn_bwd(name, proj, lb, go, oraw, states, dmix, b, s):
    nc = s // CHUNK
    t = b * s

    def body(hq_ref, hf_ref, hi_ref, hg_ref, lb_ref, go_ref, oraw_ref, st_ref, dro_ref,
             dhq_ref, dhf_ref, dhi_ref, dhg_ref, dlb_ref, dgo_ref, ds_scr, dlb_scr, dgo_scr):
        tril = _tri(True)
        ones_u = jnp.where(_tri(False), 1.0, 0.0).astype(BF16)
        lbv = lb_ref[...]
        gov = go_ref[...]
        ds_scr[...] = jnp.zeros_like(ds_scr)
        dlb_scr[...] = jnp.zeros_like(dlb_scr)
        dgo_scr[...] = jnp.zeros_like(dgo_scr)

        def step(ci, carry):
            c = nc - 1 - ci
            sl = pl.ds(pl.multiple_of(c * CHUNK, CHUNK), CHUNK)
            hq = hq_ref[sl, :]
            hg = hg_ref[sl, :]
            sig, f, bb, bl, bm, sq, q, k = _hgrn_chunk(hq, hf_ref[sl, :], lbv, tril)
            vb = hi_ref[sl, :].astype(BF16)
            ebm = jnp.exp(bb - bm)
            embm = jnp.exp(bm - bb)
            eb = jnp.exp(bb)
            ebl = jnp.exp(bl - bb)
            e_last = jnp.exp(bl)
            qe = (q * ebm).astype(BF16)
            ke = (k * embm).astype(BF16)
            qb = (q * eb).astype(BF16)
            kb = (k * ebl).astype(BF16)
            a = jnp.where(tril, _dot(qe, ke, NT), 0.0)
            st = st_ref[c]
            dst = ds_scr[...]
            o = oraw_ref[sl, :]
            dro = dro_ref[sl, :]
            sg = _sigmoid(hg)
            rstd = lax.rsqrt(jnp.mean(o * o, axis=-1, keepdims=True) + RMS_EPS)
            ohat = o * rstd
            dn = dro * (hg * sg)
            dhg_ref[sl, :] = (dro * (ohat * gov) * (sg * (1.0 + hg * (1.0 - sg)))).astype(BF16)
            dgo_scr[...] += jnp.sum(dn * ohat, axis=0, keepdims=True)
            dohat = dn * gov
            do = rstd * (dohat - ohat * jnp.mean(dohat * ohat, axis=-1, keepdims=True))
            dob = do.astype(BF16)
            dab = jnp.where(tril, _dot(dob, vb, NT), 0.0).astype(BF16)
            stb = st.astype(BF16)
            dstb = dst.astype(BF16)
            dv = _dot(a.astype(BF16), dob, TN) + _dot(kb, dstb, NT)
            dqe = _dot(dab, ke)
            dke = _dot(dab, qe, TN)
            dqb = _dot(dob, stb)
            dkb = _dot(vb, dstb)
            dq = dqe * ebm + dqb * eb
            dk = dke * embm + dkb * ebl
            db = (qe.astype(F32) * dqe - ke.astype(F32) * dke) + q * (dqb * eb) - k * (dkb * ebl)
            d_last = (jnp.sum(k * ebl * dkb, axis=0, keepdims=True)
                      + jnp.sum(dst * st, axis=0, keepdims=True) * e_last)
            dg = _dot_exact_lhs(ones_u, db) + d_last
            df = dg / f - dk
            dhf_ref[sl, :] = (df * (1.0 - lbv) * sig * (1.0 - sig)).astype(BF16)
            dlb_scr[...] += jnp.sum(df * (1.0 - sig), axis=0, keepdims=True)
            dhq_ref[sl, :] = (dq * (sq * (1.0 + hq * (1.0 - sq)))).astype(BF16)
            dhi_ref[sl, :] = dv.astype(BF16)
            ds_scr[...] = dst * e_last + _dot(dob, qb, TN)
            return carry

        lax.fori_loop(0, nc, step, 0)
        dlb_ref[...] = dlb_scr[...]
        dgo_ref[...] = dgo_scr[...]

    col = lambda base: pl.BlockSpec((s, HGRN_DH), lambda bi, h: (bi, base + h))
    vec = pl.BlockSpec((1, HGRN_DH), lambda bi, h: (0, h))
    out = pl.BlockSpec((s, HGRN_DH), lambda bi, h: (bi, h))
    part = pl.BlockSpec((None, 1, HGRN_DH), lambda bi, h: (bi, 0, h))
    o_shape = jax.ShapeDtypeStruct((t, HGRN_W), BF16)
    p_shape = jax.ShapeDtypeStruct((b, 1, HGRN_W), F32)
    return pl.pallas_call(
        body,
        name=name,
        grid=(b, HGRN_HEADS),
        in_specs=[col(HQ_COL), col(HF_COL), col(HI_COL), col(HG_COL), vec,
                  pl.BlockSpec((1, HGRN_DH), lambda bi, h: (0, 0)), out,
                  pl.BlockSpec((None, None, nc, HGRN_DH, HGRN_DH), lambda bi, h: (bi, h, 0, 0, 0)),
                  col(ATTN_W // HGRN_DH)],
        out_specs=[out] * 4 + [part] * 2,
        out_shape=[o_shape] * 4 + [p_shape] * 2,
        scratch_shapes=[pltpu.VMEM((HGRN_DH, HGRN_DH), F32), pltpu.VMEM((1, HGRN_DH), F32),
                        pltpu.VMEM((1, HGRN_DH), F32)],
        compiler_params=_params("parallel", "parallel"),
    )(proj, proj, proj, proj, lb, go, oraw, states, dmix)


def _small_grads(name, dg1, dgm, dg2, dgq, dgk, dbias_t, dlb, dgo, lbp):
    d = dg1.shape[1]

    def body(dg1_ref, dgm_ref, dg2_ref, dgq_ref, dgk_ref, dbias_ref, dlb_ref, dgo_ref, lbp_ref,
             g1_ref, gm_ref, g2_ref, gq_ref, gk_ref, rb_ref, lbg_ref, go_ref):
        g1_ref[...] = jnp.sum(dg1_ref[...], axis=0, keepdims=True)
        gm_ref[...] = jnp.sum(dgm_ref[...], axis=0, keepdims=True)
        g2_ref[...] = jnp.sum(dg2_ref[...], axis=0, keepdims=True)
        r = lax.broadcasted_iota(jnp.int32, (ATTN_W, ATTN_DH), 0)
        cidx = lax.broadcasted_iota(jnp.int32, (ATTN_W, ATTN_DH), 1)
        fold = jnp.where(jnp.bitwise_and(r, ATTN_DH - 1) == cidx, 1.0, 0.0).astype(BF16)
        gq_ref[...] = jnp.sum(_dot_exact_rhs(dgq_ref[...], fold), axis=0, keepdims=True)
        gk_ref[...] = jnp.sum(_dot_exact_rhs(dgk_ref[...], fold), axis=0, keepdims=True)
        gosum = jnp.sum(dgo_ref[...], axis=0, keepdims=True)
        go_ref[...] = (gosum[:, 0:HGRN_DH] + gosum[:, HGRN_DH:2 * HGRN_DH]
                       + gosum[:, 2 * HGRN_DH:3 * HGRN_DH] + gosum[:, 3 * HGRN_DH:4 * HGRN_DH])
        p0 = lbp_ref[0:1, :]
        p1 = lbp_ref[1:2, :]
        lbv = 1.0 / (1.0 + jnp.exp(p1 - p0))
        dp0 = jnp.sum(dlb_ref[...], axis=0, keepdims=True) * lbv * (1.0 - lbv)
        lbg_ref[0:1, :] = dp0
        lbg_ref[1:2, :] = -dp0
        sidx = lax.broadcasted_iota(jnp.int32, (BAND, N_REL_PAD), 0)
        ridx = lax.broadcasted_iota(jnp.int32, (BAND, N_REL_PAD), 1)

        def step(tq, acc):
            rel = jnp.clip(tq + KPAD - sidx, -REL_CLIP, REL_CLIP) + REL_CLIP
            onehot = jnp.where(rel == ridx, 1.0, 0.0).astype(BF16)
            return acc + _dot_exact_rhs(dbias_ref[tq], onehot)

        rb_ref[...] = lax.fori_loop(0, CHUNK, step, jnp.zeros((ATTN_HEADS, N_REL_PAD), F32))

    ins = [dg1, dgm, dg2, dgq, dgk, dbias_t, dlb, dgo, lbp]
    outs = [jax.ShapeDtypeStruct((1, d), F32)] * 3 + [jax.ShapeDtypeStruct((1, ATTN_DH), F32)] * 2 + [
        jax.ShapeDtypeStruct((ATTN_HEADS, N_REL_PAD), F32), jax.ShapeDtypeStruct((2, HGRN_W), F32),
        jax.ShapeDtypeStruct((1, HGRN_DH), F32)]
    vm = pl.BlockSpec(memory_space=pltpu.VMEM)
    return pl.pallas_call(
        body,
        name=name,
        in_specs=[vm] * len(ins),
        out_specs=[vm] * len(outs),
        out_shape=outs,
        compiler_params=pltpu.CompilerParams(vmem_limit_bytes=VMEM_LIMIT),
    )(*ins)


def _adam_update(w, g, m, v):
    m2 = ADAM_B1 * m + (1.0 - ADAM_B1) * g
    v2 = ADAM_B2 * v + (1.0 - ADAM_B2) * (g * g)
    m_hat = m2 / (1.0 - ADAM_B1 ** ADAM_STEP)
    v_hat = v2 / (1.0 - ADAM_B2 ** ADAM_STEP)
    delta = -ADAM_LR * (m_hat / (jnp.sqrt(v_hat) + ADAM_EPS) + ADAM_WD * w)
    return delta, m2, v2


def _rows_tile(r):
    for cand in (256, 352, 128, 176, 64, 8):
        if r % cand == 0 and r > cand:
            return cand
    return r


def _add2(name, a, b):
    n, r, c = a.shape
    tr = _rows_tile(r)

    def body(a_ref, b_ref, o_ref):
        o_ref[...] = a_ref[...] + b_ref[...]

    spec = pl.BlockSpec((None, tr, c), lambda i, j: (i, j, 0))
    return pl.pallas_call(
        body, name=name, grid=(n, r // tr), in_specs=[spec, spec], out_specs=spec,
        out_shape=jax.ShapeDtypeStruct((n, r, c), F32), compiler_params=_params("parallel", "parallel"),
    )(a, b)


def _sum_slots(name, parts):
    n, r, c = parts.shape
    tr = _rows_tile(r)

    def body(p_ref, o_ref):
        acc = p_ref[0]
        for i in range(1, n):
            acc = acc + p_ref[i]
        o_ref[...] = acc

    return pl.pallas_call(
        body, name=name, grid=(r // tr,),
        in_specs=[pl.BlockSpec((n, tr, c), lambda j: (0, j, 0))],
        out_specs=pl.BlockSpec((tr, c), lambda j: (j, 0)),
        out_shape=jax.ShapeDtypeStruct((r, c), F32), compiler_params=_params("parallel"),
    )(parts)


def _adamw(name, w, g, m, v):
    r, c = w.shape
    tr = _rows_tile(r)

    def body(w_ref, g_ref, m_ref, v_ref, d_ref, m2_ref, v2_ref):
        delta, m2, v2 = _adam_update(w_ref[...], g_ref[...], m_ref[...], v_ref[...])
        d_ref[...] = delta
        m2_ref[...] = m2
        v2_ref[...] = v2

    spec = pl.BlockSpec((tr, c), lambda j: (j, 0))
    shape = jax.ShapeDtypeStruct((r, c), F32)
    return pl.pallas_call(
        body, name=name, grid=(r // tr,), in_specs=[spec] * 4, out_specs=[spec] * 3, out_shape=[shape] * 3,
        compiler_params=_params("parallel"),
    )(w, g, m, v)


def _adamw_small(name, w, parts, m, v):
    def body(w_ref, p_ref, m_ref, v_ref, g_ref, d_ref, m2_ref, v2_ref):
        g = p_ref[0]
        for i in range(1, N_DEV):
            g = g + p_ref[i]
        delta, m2, v2 = _adam_update(w_ref[...], g, m_ref[...], v_ref[...])
        g_ref[...] = g
        d_ref[...] = delta
        m2_ref[...] = m2
        v2_ref[...] = v2

    vm = pl.BlockSpec(memory_space=pltpu.VMEM)
    shape = jax.ShapeDtypeStruct((SMALL_ROWS, SMALL_COLS), F32)
    return pl.pallas_call(
        body, name=name, in_specs=[vm] * 4, out_specs=[vm] * 4, out_shape=[shape] * 4,
    )(w, parts, m, v)


def _position():
    return lax.axis_index("x"), lax.axis_index("y"), lax.axis_index("c")


def _other_chips(x, y):
    return [(1 - x, y), (x, 1 - y), (1 - x, 1 - y)]


ANY = pl.BlockSpec(memory_space=pl.ANY)


def _gather_weights(shards):
    n = len(shards)

    def body(*refs):
        ins, outs = refs[:n], refs[n:2 * n]
        ici_send, ici_recv, fwd_send, fwd_recv, local_sem = refs[2 * n:]
        x, y, c = _position()
        me = 2 * x + y
        chips = _other_chips(x, y)
        local = [pltpu.make_async_copy(ins[i], outs[i].at[me], local_sem.at[i]) for i in range(n)]
        for cp in local:
            cp.start()

        def push(i, j):
            return pltpu.make_async_remote_copy(
                src_ref=ins[i], dst_ref=outs[i].at[me], send_sem=ici_send.at[3 * i + j], recv_sem=ici_recv.at[3 * i + j],
                device_id=(chips[j][0], chips[j][1], 1), device_id_type=MESH)

        def landed(i, j):
            slot = outs[i].at[2 * chips[j][0] + chips[j][1]]
            arrive = pltpu.make_async_remote_copy(
                src_ref=ins[i], dst_ref=slot, send_sem=ici_send.at[3 * i + j], recv_sem=ici_recv.at[3 * i + j],
                device_id=(chips[j][0], chips[j][1], 1), device_id_type=MESH)
            onward = pltpu.make_async_remote_copy(
                src_ref=slot, dst_ref=slot, send_sem=fwd_send.at[3 * i + j], recv_sem=fwd_recv.at[3 * i + j],
                device_id=(x, y, 1 - c), device_id_type=MESH)
            return arrive, onward

        @pl.when(c == 1)
        def _north():
            for i in range(n):
                for j in range(3):
                    push(i, j).start()
            for i in range(n):
                for j in range(3):
                    arrive, onward = landed(i, j)
                    arrive.wait_recv()
                    onward.start()
            for i in range(n):
                for j in range(3):
                    push(i, j).wait_send()
                    landed(i, j)[1].wait_send()

        @pl.when(c == 0)
        def _south():
            for i in range(n):
                for j in range(3):
                    landed(i, j)[1].wait_recv()

        for cp in local:
            cp.wait()

    return pl.pallas_call(
        body,
        name="gather_weights",
        in_specs=[ANY] * n,
        out_specs=[ANY] * n,
        out_shape=[jax.ShapeDtypeStruct((N_CHIPS,) + s.shape, s.dtype) for s in shards],
        scratch_shapes=[pltpu.SemaphoreType.DMA((3 * n,))] * 4 + [pltpu.SemaphoreType.DMA((n,))],
    )(*shards)


def _pair_exchange(grads):
    n = len(grads)

    def body(*refs):
        ins, mine, theirs = refs[:n], refs[n:2 * n], refs[2 * n:3 * n]
        send_sem, recv_sem, local_sem = refs[3 * n:]
        x, y, c = _position()
        copies = []
        for i in range(n):
            half = ins[i].shape[1] // 2
            keep = pl.ds(pl.multiple_of(c * half, 8), half)
            give = pl.ds(pl.multiple_of((1 - c) * half, 8), half)
            swap = pltpu.make_async_remote_copy(
                src_ref=ins[i].at[:, give, :], dst_ref=theirs[i], send_sem=send_sem.at[i], recv_sem=recv_sem.at[i],
                device_id=(x, y, 1 - c), device_id_type=MESH)
            own = pltpu.make_async_copy(ins[i].at[:, keep, :], mine[i], local_sem.at[i])
            swap.start()
            own.start()
            copies.append((swap, own))
        for swap, own in copies:
            swap.wait()
            own.wait()

    halves = [jax.ShapeDtypeStruct((g.shape[0], g.shape[1] // 2, g.shape[2]), g.dtype) for g in grads]
    res = pl.pallas_call(
        body,
        name="pair_exchange",
        in_specs=[ANY] * n,
        out_specs=[ANY] * (2 * n),
        out_shape=halves + halves,
        scratch_shapes=[pltpu.SemaphoreType.DMA((n,))] * 3,
    )(*grads)
    return res[:n], res[n:]


def _chip_scatter(sums):
    n = len(sums)

    def body(*refs):
        ins, outs = refs[:n], refs[n:2 * n]
        send_sem, recv_sem, local_sem = refs[2 * n:]
        x, y, c = _position()
        me = 2 * x + y
        chips = _other_chips(x, y)
        local = [pltpu.make_async_copy(ins[i].at[me], outs[i].at[me], local_sem.at[i]) for i in range(n)]
        for cp in local:
            cp.start()

        def push(i, j):
            return pltpu.make_async_remote_copy(
                src_ref=ins[i].at[2 * chips[j][0] + chips[j][1]], dst_ref=outs[i].at[me],
                send_sem=send_sem.at[3 * i + j], recv_sem=recv_sem.at[3 * i + j],
                device_id=(chips[j][0], chips[j][1], c), device_id_type=MESH)

        def landed(i, j):
            return pltpu.make_async_remote_copy(
                src_ref=ins[i].at[me], dst_ref=outs[i].at[2 * chips[j][0] + chips[j][1]],
                send_sem=send_sem.at[3 * i + j], recv_sem=recv_sem.at[3 * i + j],
                device_id=(chips[j][0], chips[j][1], c), device_id_type=MESH)

        for i in range(n):
            for j in range(3):
                push(i, j).start()
        for i in range(n):
            for j in range(3):
                landed(i, j).wait_recv()
        for i in range(n):
            for j in range(3):
                push(i, j).wait_send()
        for cp in local:
            cp.wait()

    return pl.pallas_call(
        body,
        name="chip_scatter",
        in_specs=[ANY] * n,
        out_specs=[ANY] * n,
        out_shape=[jax.ShapeDtypeStruct(s.shape, s.dtype) for s in sums],
        scratch_shapes=[pltpu.SemaphoreType.DMA((3 * n,))] * 2 + [pltpu.SemaphoreType.DMA((n,))],
    )(*sums)


def _pair_join(halves, small):
    n = len(halves)

    def body(*refs):
        ins, small_ref = refs[:n], refs[n]
        outs, all_ref = refs[n + 1:2 * n + 1], refs[2 * n + 1]
        send_sem, recv_sem, local_sem, sm_send, sm_recv, sm_local = refs[2 * n + 2:]
        x, y, c = _position()
        started = []
        for i in range(n):
            half = ins[i].shape[0]
            mine = pl.ds(pl.multiple_of(c * half, 8), half)
            theirs = pl.ds(pl.multiple_of((1 - c) * half, 8), half)
            give = pltpu.make_async_remote_copy(
                src_ref=ins[i], dst_ref=outs[i].at[mine, :], send_sem=send_sem.at[i], recv_sem=recv_sem.at[i],
                device_id=(x, y, 1 - c), device_id_type=MESH)
            take = pltpu.make_async_remote_copy(
                src_ref=ins[i], dst_ref=outs[i].at[theirs, :], send_sem=send_sem.at[i], recv_sem=recv_sem.at[i],
                device_id=(x, y, 1 - c), device_id_type=MESH)
            own = pltpu.make_async_copy(ins[i], outs[i].at[mine, :], local_sem.at[i])
            give.start()
            own.start()
            started.append((give, take, own))
        me = 4 * x + 2 * y + c
        sm_own = pltpu.make_async_copy(small_ref, all_ref.at[me], sm_local)
        sm_own.start()
        pushes, arrivals = [], []
        for mask in range(1, N_DEV):
            px, py, pc = x ^ (mask >> 2), y ^ ((mask >> 1) & 1), c ^ (mask & 1)
            pushes.append(pltpu.make_async_remote_copy(
                src_ref=small_ref, dst_ref=all_ref.at[me], send_sem=sm_send.at[mask - 1], recv_sem=sm_recv.at[mask - 1],
                device_id=(px, py, pc), device_id_type=MESH))
            arrivals.append(pltpu.make_async_remote_copy(
                src_ref=small_ref, dst_ref=all_ref.at[4 * px + 2 * py + pc], send_sem=sm_send.at[mask - 1],
                recv_sem=sm_recv.at[mask - 1], device_id=(px, py, pc), device_id_type=MESH))
        for cp in pushes:
            cp.start()
        for give, take, own in started:
            give.wait_send()
            take.wait_recv()
            own.wait()
        for cp in arrivals:
            cp.wait_recv()
        for cp in pushes:
            cp.wait_send()
        sm_own.wait()

    res = pl.pallas_call(
        body,
        name="pair_join",
        in_specs=[ANY] * (n + 1),
        out_specs=[ANY] * (n + 1),
        out_shape=[jax.ShapeDtypeStruct((2 * h.shape[0], h.shape[1]), h.dtype) for h in halves]
        + [jax.ShapeDtypeStruct((N_DEV,) + small.shape, small.dtype)],
        scratch_shapes=[pltpu.SemaphoreType.DMA((n,))] * 3 + [pltpu.SemaphoreType.DMA((N_DEV - 1,))] * 2
        + [pltpu.SemaphoreType.DMA(())],
    )(*halves, small)
    return res[:n], res[n]


def _to_heads(a, b, s):
    return a.reshape(b, s, ATTN_HEADS, ATTN_DH).transpose(0, 2, 1, 3)


def _from_heads(a):
    b, h, s, dh = a.shape
    return a.transpose(0, 2, 1, 3).reshape(b * s, h * dh)


def _pad_keys(a):
    return jnp.pad(a, ((0, 0), (0, 0), (KPAD, 0), (0, 0)))


def _rel_bias_table(rel_bias):
    t_pos = jnp.arange(CHUNK)[:, None] + KPAD
    s_pos = jnp.arange(BAND)[None, :]
    rel_idx = jnp.clip(t_pos - s_pos, -REL_CLIP, REL_CLIP) + REL_CLIP
    return rel_bias[:, rel_idx]


def _lower_bound(lbp):
    return jax.nn.softmax(lbp, axis=0)[0:1]


def _local_step(x, target, g1, gm, g2, gq, gk, go, rel_bias, lbp, wg1, wu1, wd1, w_in, w_out, wg2, wu2, wd2):
    b, s, d = x.shape
    t = b * s
    ns = w_in.shape[0]
    x0 = x.reshape(t, d)
    tgt = target.reshape(t, d)
    gq_t = jnp.tile(gq, (1, ATTN_HEADS))
    gk_t = jnp.tile(gk, (1, ATTN_HEADS))
    lb = _lower_bound(lbp)
    bias = _rel_bias_table(rel_bias)

    h1 = _rmsnorm_fwd("norm1", x0, g1)
    a1, b1, z1 = _ffn_up("ffn1_up", h1, wg1, wu1)
    x1 = _ffn_down("ffn1_down", z1, wd1, x0)
    h2 = _rmsnorm_fwd("norm_mix", x1, gm)
    proj = _in_proj("in_proj", h2, w_in)
    qn, kn, vb = _qk_prep("qk_prep", proj, gq_t, gk_t)
    qh = _to_heads(qn, b, s)
    kh = _pad_keys(_to_heads(kn, b, s))
    vh = _pad_keys(_to_heads(vb, b, s))
    attn = _from_heads(_attn_fwd("attn_fwd", qh, kh, vh, bias))
    ro, oraw, states = _hgrn_fwd("hgrn_fwd", proj, lb, go, b, s)
    mix = jnp.concatenate([attn, ro], axis=1)
    x2 = _out_proj("out_proj", mix, w_out, x1)
    h3 = _rmsnorm_fwd("norm2", x2, g2)
    a2, b2, z2 = _ffn_up("ffn2_up", h3, wg2, wu2)
    dy, dyh, sq = _ffn_down_loss("ffn2_down_loss", z2, wd2, x2, tgt)
    loss = 0.5 * jnp.sum(sq) / d

    da2, db2 = _ffn_bwd_act("ffn2_bwd_act", dyh, wd2, a2, b2)
    dwd2 = _grad_w_shardrows("ffn2_dwd", z2, dyh)
    dwg2 = _grad_w_shardcols("ffn2_dwg", h3, da2)
    dwu2 = _grad_w_shardcols("ffn2_dwu", h3, db2)
    dx2, dx2b, dg2 = _ffn_bwd_in("ffn2_bwd_in", da2, db2, wg2, wu2, x2, g2, dy, 1.0)

    dwout = _grad_w_out("dw_out", mix, dx2b)
    dmix = _out_proj_bwd("out_proj_bwd", dx2b, w_out)
    doh = _to_heads(dmix[:, :ATTN_W].astype(BF16), b, s)
    dqh, dkh, dvh, dbias = _attn_bwd("attn_bwd", qh, kh, vh, bias, doh)
    dqn = _from_heads(dqh)
    dkn = _from_heads(dkh[:, :, KPAD:, :])
    dvn = _from_heads(dvh[:, :, KPAD:, :])
    dpq, dpk, dpv, dgq, dgk = _qk_prep_bwd("qk_prep_bwd", proj, dqn, dkn, dvn, gq_t, gk_t)
    dhq, dhf, dhi, dhg, dlb, dgo = _hgrn_bwd("hgrn_bwd", proj, lb, go, oraw, states, dmix, b, s)
    dproj = jnp.concatenate([dpq, dpk, dpv, dhq, dhf, dhi, dhg], axis=1)
    dwin = _grad_w_in("dw_in", h2, dproj, ns)
    dx1, dx1h, dgm = _in_proj_bwd("in_proj_bwd", dproj, w_in, x1, gm, dx2, 0.5)

    da1, db1 = _ffn_bwd_act("ffn1_bwd_act", dx1h, wd1, a1, b1)
    dwd1 = _grad_w_shardrows("ffn1_dwd", z1, dx1h)
    dwg1 = _grad_w_shardcols("ffn1_dwg", h1, da1)
    dwu1 = _grad_w_shardcols("ffn1_dwu", h1, db1)
    dx0, dg1 = _ffn_bwd_in("ffn1_bwd_in", da1, db1, wg1, wu1, x0, g1, dx1, None)

    nt = dg1.shape[0]
    sg = _small_grads(
        "small_grads", dg1.reshape(nt, d), dgm.reshape(nt, d), dg2.reshape(nt, d),
        dgq.reshape(nt, ATTN_W), dgk.reshape(nt, ATTN_W), dbias.transpose(1, 0, 2),
        dlb.reshape(b, HGRN_W), dgo.reshape(b, HGRN_W), lbp)
    g1g, gmg, g2g, gqg, gkg, rbg, lbg, gog = sg
    small = _pack_small(g1g, gmg, g2g, lbg, rbg[:, :N_REL], gqg, gkg, gog)
    big = [dwg1, dwu1, dwd1, dwin, dwout.reshape(ns, dwout.shape[0] // ns, d), dwg2, dwu2, dwd2]
    return loss, dx0.reshape(b, s, d), big, small


def _pack_small(g1, gm, g2, lbp, rel_bias, gq, gk, go):
    flat = [g1.reshape(-1), gm.reshape(-1), g2.reshape(-1), lbp.reshape(-1), rel_bias.reshape(-1)]
    n_bias = 3 * SMALL_COLS - rel_bias.size
    heads = [gq.reshape(-1), gk.reshape(-1), go.reshape(-1)]
    n_tail = SMALL_COLS - sum(h.size for h in heads)
    return jnp.concatenate(flat + [jnp.zeros((n_bias,), F32)] + heads + [jnp.zeros((n_tail,), F32)]).reshape(
        SMALL_ROWS, SMALL_COLS)


def _unpack_small(p, d):
    flat = p.reshape(-1)
    o = 3 * d
    g1, gm, g2 = p[0:1], p[1:2], p[2:3]
    lbp = flat[o:o + 2 * HGRN_W].reshape(2, HGRN_W)
    o = 4 * SMALL_COLS
    rel = flat[o:o + ATTN_HEADS * N_REL].reshape(1, ATTN_HEADS, N_REL)
    o = 7 * SMALL_COLS
    gq = flat[o:o + ATTN_DH].reshape(1, ATTN_DH)
    gk = flat[o + ATTN_DH:o + 2 * ATTN_DH].reshape(1, ATTN_DH)
    go = flat[o + 2 * ATTN_DH:o + 2 * ATTN_DH + HGRN_DH].reshape(1, HGRN_DH)
    return g1, gm, g2, gq, gk, rel, lbp, go


def kernel(x, ffn1_norm_g, ffn1_w_gate, ffn1_w_up, ffn1_w_down, mix_norm_g, w_in, attn_q_norm_g, attn_k_norm_g, attn_rel_bias, hgrn_lower_bounds, hgrn_out_norm_g, w_out, ffn2_norm_g, ffn2_w_gate, ffn2_w_up, ffn2_w_down, loss_target, m_ffn1_norm_g, m_ffn1_w_gate, m_ffn1_w_up, m_ffn1_w_down, m_mix_norm_g, m_w_in, m_attn_q_norm_g, m_attn_k_norm_g, m_attn_rel_bias, m_hgrn_lower_bounds, m_hgrn_out_norm_g, m_w_out, m_ffn2_norm_g, m_ffn2_w_gate, m_ffn2_w_up, m_ffn2_w_down, v_ffn1_norm_g, v_ffn1_w_gate, v_ffn1_w_up, v_ffn1_w_down, v_mix_norm_g, v_w_in, v_attn_q_norm_g, v_attn_k_norm_g, v_attn_rel_bias, v_hgrn_lower_bounds, v_hgrn_out_norm_g, v_w_out, v_ffn2_norm_g, v_ffn2_w_gate, v_ffn2_w_up, v_ffn2_w_down):
    d = x.shape[-1]
    big_w = [ffn1_w_gate, ffn1_w_up, ffn1_w_down, w_in, w_out, ffn2_w_gate, ffn2_w_up, ffn2_w_down]
    big_m = [m_ffn1_w_gate, m_ffn1_w_up, m_ffn1_w_down, m_w_in, m_w_out, m_ffn2_w_gate, m_ffn2_w_up, m_ffn2_w_down]
    big_v = [v_ffn1_w_gate, v_ffn1_w_up, v_ffn1_w_down, v_w_in, v_w_out, v_ffn2_w_gate, v_ffn2_w_up, v_ffn2_w_down]
    big_names = ["ffn1_w_gate", "ffn1_w_up", "ffn1_w_down", "w_in", "w_out", "ffn2_w_gate", "ffn2_w_up", "ffn2_w_down"]

    full = _gather_weights([w[0].astype(BF16) for w in big_w])
    wg1, wu1, wd1, win_f, wout_f, wg2, wu2, wd2 = full
    wout_f = wout_f.reshape(wout_f.shape[0] * wout_f.shape[1], d)

    loss, grad_x, big_g, small_g = _local_step(
        x, loss_target, ffn1_norm_g, mix_norm_g, ffn2_norm_g, attn_q_norm_g, attn_k_norm_g, hgrn_out_norm_g,
        attn_rel_bias[0], hgrn_lower_bounds, wg1, wu1, wd1, win_f, wout_f, wg2, wu2, wd2)
    loss = lax.psum(loss, ("x", "y", "c"))

    mine, theirs = _pair_exchange(big_g)
    sums = [_add2("pair_sum_" + nm, a, bq) for nm, a, bq in zip(big_names, mine, theirs)]
    parts = _chip_scatter(sums)
    halves = [_sum_slots("chip_sum_" + nm, p) for nm, p in zip(big_names, parts)]
    grads, small_all = _pair_join(halves, small_g)

    big_out = [_adamw("adamw_" + nm, w[0], g, m[0], v[0]) for nm, w, g, m, v in zip(big_names, big_w, grads, big_m, big_v)]
    pack = lambda g1, gm, g2, gq, gk, rel, lbp, go: _pack_small(g1, gm, g2, lbp, rel[0], gq, gk, go)
    small_w = pack(ffn1_norm_g, mix_norm_g, ffn2_norm_g, attn_q_norm_g, attn_k_norm_g, attn_rel_bias, hgrn_lower_bounds, hgrn_out_norm_g)
    small_m = pack(m_ffn1_norm_g, m_mix_norm_g, m_ffn2_norm_g, m_attn_q_norm_g, m_attn_k_norm_g, m_attn_rel_bias, m_hgrn_lower_bounds, m_hgrn_out_norm_g)
    small_v = pack(v_ffn1_norm_g, v_mix_norm_g, v_ffn2_norm_g, v_attn_q_norm_g, v_attn_k_norm_g, v_attn_rel_bias, v_hgrn_lower_bounds, v_hgrn_out_norm_g)
    small_out = [_unpack_small(p, d) for p in _adamw_small("adamw_small", small_w, small_all, small_m, small_v)]

    def assemble(kind):
        if kind == 0:
            bg = [g[None] for g in grads]
        else:
            bg = [o[kind - 1][None] for o in big_out]
        g1, gm, g2, gq, gk, rel, lbp, go = small_out[kind]
        return [g1, bg[0], bg[1], bg[2], gm, bg[3], gq, gk, rel, lbp, go, bg[4], g2, bg[5], bg[6], bg[7]]

    return (loss, grad_x, *assemble(0), *assemble(1), *assemble(2), *assemble(3))
```

```python
import functools

import jax
import jax.numpy as jnp
from jax import lax
from jax.experimental import pallas as pl
from jax.experimental.pallas import tpu as pltpu

F32 = jnp.float32
BF16 = jnp.bfloat16
MESH = pl.DeviceIdType.MESH

N_CHIPS = 4
N_DEV = 8
CHUNK = 64
ATTN_HEADS = 8
ATTN_DH = 64
ATTN_W = ATTN_HEADS * ATTN_DH
HGRN_HEADS = 4
HGRN_DH = 128
HGRN_W = HGRN_HEADS * HGRN_DH
LEFT_CHUNKS = 8
BAND = (LEFT_CHUNKS + 1) * CHUNK
KPAD = LEFT_CHUNKS * CHUNK
REL_CLIP = 128
N_REL = 2 * REL_CLIP + 1
N_REL_PAD = 384
RMS_EPS = 1e-6
LANES = 128
SMALL_ROWS = 8
SMALL_COLS = 1024

ADAM_LR = 0.001
ADAM_B1 = 0.9
ADAM_B2 = 0.999
ADAM_EPS = 1e-08
ADAM_WD = 0.01
ADAM_STEP = 10

NN = (((1,), (0,)), ((), ()))
NT = (((1,), (1,)), ((), ()))
TN = (((0,), (0,)), ((), ()))

VMEM_LIMIT = 48 * 1024 * 1024


def _sigmoid(x):
    return 1.0 / (1.0 + jnp.exp(-x))


def _silu(x):
    return x * _sigmoid(x)


def _dsilu(x):
    s = _sigmoid(x)
    return s * (1.0 + x * (1.0 - s))


def _dot(a, b, dims=NN):
    return lax.dot_general(a, b, dims, preferred_element_type=F32)


def _split3(x):
    hi = x.astype(BF16)
    r1 = x - hi.astype(F32)
    mid = r1.astype(BF16)
    lo = (r1 - mid.astype(F32)).astype(BF16)
    return hi, mid, lo


def _dot_exact_rhs(x, mat, dims=NN):
    hi, mid, lo = _split3(x)
    return _dot(hi, mat, dims) + _dot(mid, mat, dims) + _dot(lo, mat, dims)


def _dot_exact_lhs(mat, x, dims=NN):
    hi, mid, lo = _split3(x)
    return _dot(mat, hi, dims) + _dot(mat, mid, dims) + _dot(mat, lo, dims)


def _params(*sem):
    return pltpu.CompilerParams(dimension_semantics=sem, vmem_limit_bytes=VMEM_LIMIT)


def _mm(name, ins, terms, n_acc, grid, acc_shape, outs, epilogue, extras=()):
    nk = grid[2]
    ni, ne, no = len(ins), len(extras), len(outs)

    def body(*refs):
        in_refs = refs[:ni]
        ex_refs = refs[ni:ni + ne]
        out_refs = refs[ni + ne:ni + ne + no]
        acc_refs = refs[ni + ne + no:]
        parts = [None] * n_acc
        for ai, li, ri, dims in terms:
            d = _dot(in_refs[li][...], in_refs[ri][...], dims)
            parts[ai] = d if parts[ai] is None else parts[ai] + d

        def finish(accs):
            res = epilogue(accs, [e[...] for e in ex_refs])
            for o, r in zip(out_refs, res):
                o[...] = r.astype(o.dtype)

        if nk == 1:
            finish(parts)
        else:
            k = pl.program_id(2)

            @pl.when(k == 0)
            def _():
                for a, p in zip(acc_refs, parts):
                    a[...] = p

            @pl.when(k > 0)
            def _():
                for a, p in zip(acc_refs, parts):
                    a[...] += p

            @pl.when(k == nk - 1)
            def _():
                finish([a[...] for a in acc_refs])

    scratch = [] if nk == 1 else [pltpu.VMEM(acc_shape, F32) for _ in range(n_acc)]
    res = pl.pallas_call(
        body,
        name=name,
        grid=grid,
        in_specs=[s for _, s in ins] + [s for _, s in extras],
        out_specs=[s for _, s in outs],
        out_shape=[o for o, _ in outs],
        scratch_shapes=scratch,
        compiler_params=_params("parallel", "parallel", "arbitrary"),
    )(*[a for a, _ in ins], *[a for a, _ in extras])
    return res


def _row_tile(t):
    return 512 if t % 512 == 0 else t


def _k_tile(t):
    return 1024 if t % 1024 == 0 else t


def _rmsnorm_fwd(name, x, g):
    t, d = x.shape
    tm = _row_tile(t)

    def body(x_ref, g_ref, h_ref):
        xv = x_ref[...]
        ms = jnp.mean(xv * xv, axis=-1, keepdims=True)
        h_ref[...] = (xv * lax.rsqrt(ms + RMS_EPS) * g_ref[...]).astype(BF16)

    return pl.pallas_call(
        body,
        name=name,
        grid=(t // tm,),
        in_specs=[pl.BlockSpec((tm, d), lambda i: (i, 0)), pl.BlockSpec((1, d), lambda i: (0, 0))],
        out_specs=pl.BlockSpec((tm, d), lambda i: (i, 0)),
        out_shape=jax.ShapeDtypeStruct((t, d), BF16),
        compiler_params=_params("parallel"),
    )(x, g)


def _norm_bwd_epilogue(copy_scale):
    def epilogue(accs, ex):
        dh = accs[0]
        xv, g, dres = ex
        ms = jnp.mean(xv * xv, axis=-1, keepdims=True)
        rstd = lax.rsqrt(ms + RMS_EPS)
        xhat = xv * rstd
        dxhat = dh * g
        dx = rstd * (dxhat - xhat * jnp.mean(dxhat * xhat, axis=-1, keepdims=True))
        out = dres + dx
        dg = jnp.sum(dh * xhat, axis=0, keepdims=True)
        if copy_scale is None:
            return out, dg
        return out, out * copy_scale, dg

    return epilogue


def _ffn_up(name, h, wg, wu):
    t, d = h.shape
    ns, _, f = wg.shape
    tm = _row_tile(t)

    def epilogue(accs, ex):
        a, b = accs
        return a, b, _silu(a) * b

    w_spec = pl.BlockSpec((None, d, f), lambda j, i, k: (j, 0, 0))
    o_spec = pl.BlockSpec((None, tm, f), lambda j, i, k: (j, i, 0))
    o_shape = jax.ShapeDtypeStruct((ns, t, f), BF16)
    return _mm(
        name,
        ins=[(h, pl.BlockSpec((tm, d), lambda j, i, k: (i, 0))), (wg, w_spec), (wu, w_spec)],
        terms=[(0, 0, 1, NN), (1, 0, 2, NN)],
        n_acc=2,
        grid=(ns, t // tm, 1),
        acc_shape=(tm, f),
        outs=[(o_shape, o_spec)] * 3,
        epilogue=epilogue,
    )


def _ffn_down(name, z, wd, x):
    ns, t, f = z.shape
    d = wd.shape[2]
    tm = _row_tile(t)
    row = pl.BlockSpec((tm, d), lambda i, n, k: (i, 0))
    return _mm(
        name,
        ins=[(z, pl.BlockSpec((None, tm, f), lambda i, n, k: (k, i, 0))),
             (wd, pl.BlockSpec((None, f, d), lambda i, n, k: (k, 0, 0)))],
        terms=[(0, 0, 1, NN)],
        n_acc=1,
        grid=(t // tm, 1, ns),
        acc_shape=(tm, d),
        outs=[(jax.ShapeDtypeStruct((t, d), F32), row)],
        epilogue=lambda accs, ex: (ex[0] + 0.5 * accs[0],),
        extras=[(x, row)],
    )[0]


def _ffn_down_loss(name, z, wd, x, target):
    ns, t, f = z.shape
    d = wd.shape[2]
    tm = _row_tile(t)
    nt = t // tm
    row = pl.BlockSpec((tm, d), lambda i, n, k: (i, 0))

    def epilogue(accs, ex):
        e = ex[0] + 0.5 * accs[0] - ex[1]
        dy = e * (1.0 / d)
        return dy, 0.5 * dy, jnp.sum(e * e, axis=0, keepdims=True)

    return _mm(
        name,
        ins=[(z, pl.BlockSpec((None, tm, f), lambda i, n, k: (k, i, 0))),
             (wd, pl.BlockSpec((None, f, d), lambda i, n, k: (k, 0, 0)))],
        terms=[(0, 0, 1, NN)],
        n_acc=1,
        grid=(nt, 1, ns),
        acc_shape=(tm, d),
        outs=[(jax.ShapeDtypeStruct((t, d), F32), row), (jax.ShapeDtypeStruct((t, d), BF16), row),
              (jax.ShapeDtypeStruct((nt, 1, d), F32), pl.BlockSpec((None, 1, d), lambda i, n, k: (i, 0, 0)))],
        epilogue=epilogue,
        extras=[(x, row), (target, row)],
    )


def _ffn_bwd_act(name, dout, wd, a, b):
    t, d = dout.shape
    ns, f, _ = wd.shape
    tm = _row_tile(t)

    def epilogue(accs, ex):
        dz = accs[0]
        av = ex[0].astype(F32)
        bv = ex[1].astype(F32)
        return dz * bv * _dsilu(av), dz * _silu(av)

    act = pl.BlockSpec((None, tm, f), lambda j, i, k: (j, i, 0))
    o_shape = jax.ShapeDtypeStruct((ns, t, f), BF16)
    return _mm(
        name,
        ins=[(dout, pl.BlockSpec((tm, d), lambda j, i, k: (i, 0))),
             (wd, pl.BlockSpec((None, f, d), lambda j, i, k: (j, 0, 0)))],
        terms=[(0, 0, 1, NT)],
        n_acc=1,
        grid=(ns, t // tm, 1),
        acc_shape=(tm, f),
        outs=[(o_shape, act)] * 2,
        epilogue=epilogue,
        extras=[(a, act), (b, act)],
    )


def _grad_w_shardcols(name, h, da):
    t, d = h.shape
    ns, _, f = da.shape
    tk = _k_tile(t)
    return _mm(
        name,
        ins=[(h, pl.BlockSpec((tk, d), lambda j, n, k: (k, 0))),
             (da, pl.BlockSpec((None, tk, f), lambda j, n, k: (j, k, 0)))],
        terms=[(0, 0, 1, TN)],
        n_acc=1,
        grid=(ns, 1, t // tk),
        acc_shape=(d, f),
        outs=[(jax.ShapeDtypeStruct((ns, d, f), F32), pl.BlockSpec((None, d, f), lambda j, n, k: (j, 0, 0)))],
        epilogue=lambda accs, ex: (accs[0],),
    )[0]


def _grad_w_shardrows(name, z, dout):
    ns, t, f = z.shape
    d = dout.shape[1]
    tk = _k_tile(t)
    return _mm(
        name,
        ins=[(z, pl.BlockSpec((None, tk, f), lambda j, n, k: (j, k, 0))),
             (dout, pl.BlockSpec((tk, d), lambda j, n, k: (k, 0)))],
        terms=[(0, 0, 1, TN)],
        n_acc=1,
        grid=(ns, 1, t // tk),
        acc_shape=(f, d),
        outs=[(jax.ShapeDtypeStruct((ns, f, d), F32), pl.BlockSpec((None, f, d), lambda j, n, k: (j, 0, 0)))],
        epilogue=lambda accs, ex: (accs[0],),
    )[0]


def _ffn_bwd_in(name, da, db, wg, wu, x, g, dres, copy_scale):
    ns, t, f = da.shape
    d = wg.shape[1]
    tm = _row_tile(t)
    nt = t // tm
    act = pl.BlockSpec((None, tm, f), lambda i, n, k: (k, i, 0))
    w_spec = pl.BlockSpec((None, d, f), lambda i, n, k: (k, 0, 0))
    row = pl.BlockSpec((tm, d), lambda i, n, k: (i, 0))
    outs = [(jax.ShapeDtypeStruct((t, d), F32), row)]
    if copy_scale is not None:
        outs.append((jax.ShapeDtypeStruct((t, d), BF16), row))
    outs.append((jax.ShapeDtypeStruct((nt, 1, d), F32), pl.BlockSpec((None, 1, d), lambda i, n, k: (i, 0, 0))))
    return _mm(
        name,
        ins=[(da, act), (db, act), (wg, w_spec), (wu, w_spec)],
        terms=[(0, 0, 2, NT), (0, 1, 3, NT)],
        n_acc=1,
        grid=(nt, 1, ns),
        acc_shape=(tm, d),
        outs=outs,
        epilogue=_norm_bwd_epilogue(copy_scale),
        extras=[(x, row), (g, pl.BlockSpec((1, d), lambda i, n, k: (0, 0))), (dres, row)],
    )


def _in_proj(name, h, w_in):
    t, d = h.shape
    ns, _, pj = w_in.shape
    tm = _row_tile(t)
    return _mm(
        name,
        ins=[(h, pl.BlockSpec((tm, d), lambda j, i, k: (i, 0))),
             (w_in, pl.BlockSpec((None, d, pj), lambda j, i, k: (j, 0, 0)))],
        terms=[(0, 0, 1, NN)],
        n_acc=1,
        grid=(ns, t // tm, 1),
        acc_shape=(tm, pj),
        outs=[(jax.ShapeDtypeStruct((t, ns * pj), F32), pl.BlockSpec((tm, pj), lambda j, i, k: (i, j)))],
        epilogue=lambda accs, ex: (accs[0],),
    )[0]


def _in_proj_bwd(name, dp, w_in, x, g, dres, copy_scale):
    t = dp.shape[0]
    ns, d, pj = w_in.shape
    tm = _row_tile(t)
    nt = t // tm
    row = pl.BlockSpec((tm, d), lambda i, n, k: (i, 0))
    outs = [(jax.ShapeDtypeStruct((t, d), F32), row)]
    if copy_scale is not None:
        outs.append((jax.ShapeDtypeStruct((t, d), BF16), row))
    outs.append((jax.ShapeDtypeStruct((nt, 1, d), F32), pl.BlockSpec((None, 1, d), lambda i, n, k: (i, 0, 0))))
    return _mm(
        name,
        ins=[(dp, pl.BlockSpec((tm, pj), lambda i, n, k: (i, k))),
             (w_in, pl.BlockSpec((None, d, pj), lambda i, n, k: (k, 0, 0)))],
        terms=[(0, 0, 1, NT)],
        n_acc=1,
        grid=(nt, 1, ns),
        acc_shape=(tm, d),
        outs=outs,
        epilogue=_norm_bwd_epilogue(copy_scale),
        extras=[(x, row), (g, pl.BlockSpec((1, d), lambda i, n, k: (0, 0))), (dres, row)],
    )


def _grad_w_in(name, h, dp, ns):
    t, d = h.shape
    pj = dp.shape[1] // ns
    tk = _k_tile(t)
    return _mm(
        name,
        ins=[(h, pl.BlockSpec((tk, d), lambda j, n, k: (k, 0))),
             (dp, pl.BlockSpec((tk, pj), lambda j, n, k: (k, j)))],
        terms=[(0, 0, 1, TN)],
        n_acc=1,
        grid=(ns, 1, t // tk),
        acc_shape=(d, pj),
        outs=[(jax.ShapeDtypeStruct((ns, d, pj), F32), pl.BlockSpec((None, d, pj), lambda j, n, k: (j, 0, 0)))],
        epilogue=lambda accs, ex: (accs[0],),
    )[0]


def _out_proj(name, mix, w_out, x):
    t, dm = mix.shape
    d = w_out.shape[1]
    tm = _row_tile(t)
    row = pl.BlockSpec((tm, d), lambda i, n, k: (i, 0))
    return _mm(
        name,
        ins=[(mix, pl.BlockSpec((tm, dm), lambda i, n, k: (i, 0))),
             (w_out, pl.BlockSpec((dm, d), lambda i, n, k: (0, 0)))],
        terms=[(0, 0, 1, NN)],
        n_acc=1,
        grid=(t // tm, 1, 1),
        acc_shape=(tm, d),
        outs=[(jax.ShapeDtypeStruct((t, d), F32), row)],
        epilogue=lambda accs, ex: (ex[0] + accs[0],),
        extras=[(x, row)],
    )[0]


def _out_proj_bwd(name, dx, w_out):
    t, d = dx.shape
    dm = w_out.shape[0]
    tm = _row_tile(t)
    return _mm(
        name,
        ins=[(dx, pl.BlockSpec((tm, d), lambda i, n, k: (i, 0))),
             (w_out, pl.BlockSpec((dm, d), lambda i, n, k: (0, 0)))],
        terms=[(0, 0, 1, NT)],
        n_acc=1,
        grid=(t // tm, 1, 1),
        acc_shape=(tm, dm),
        outs=[(jax.ShapeDtypeStruct((t, dm), F32), pl.BlockSpec((tm, dm), lambda i, n, k: (i, 0)))],
        epilogue=lambda accs, ex: (accs[0],),
    )[0]


def _grad_w_out(name, mix, dx):
    t, dm = mix.shape
    d = dx.shape[1]
    tk = _k_tile(t)
    return _mm(
        name,
        ins=[(mix, pl.BlockSpec((tk, dm), lambda a, n, k: (k, 0))),
             (dx, pl.BlockSpec((tk, d), lambda a, n, k: (k, 0)))],
        terms=[(0, 0, 1, TN)],
        n_acc=1,
        grid=(1, 1, t // tk),
        acc_shape=(dm, d),
        outs=[(jax.ShapeDtypeStruct((dm, d), F32), pl.BlockSpec((dm, d), lambda a, n, k: (0, 0)))],
        epilogue=lambda accs, ex: (accs[0],),
    )[0]


def _head_group_matrix():
    r = lax.broadcasted_iota(jnp.int32, (ATTN_W, ATTN_W), 0)
    c = lax.broadcasted_iota(jnp.int32, (ATTN_W, ATTN_W), 1)
    same = jnp.right_shift(r, 6) == jnp.right_shift(c, 6)
    return jnp.where(same, 1.0, 0.0).astype(BF16)


def _qk_prep(name, proj, gq, gk):
    t = proj.shape[0]
    tm = _row_tile(t)

    def body(q_ref, k_ref, v_ref, gq_ref, gk_ref, qn_ref, kn_ref, vb_ref):
        bd = _head_group_matrix()

        def norm(xv, g):
            ms = _dot_exact_rhs(xv * xv, bd) * (1.0 / ATTN_DH)
            return xv * lax.rsqrt(ms + RMS_EPS) * g

        qn_ref[...] = norm(q_ref[...], gq_ref[...]).astype(BF16)
        kn_ref[...] = norm(k_ref[...], gk_ref[...]).astype(BF16)
        vb_ref[...] = v_ref[...].astype(BF16)

    col = lambda j: pl.BlockSpec((tm, ATTN_W), lambda i: (i, j))
    gspec = pl.BlockSpec((1, ATTN_W), lambda i: (0, 0))
    o_shape = jax.ShapeDtypeStruct((t, ATTN_W), BF16)
    return pl.pallas_call(
        body,
        name=name,
        grid=(t // tm,),
        in_specs=[col(0), col(1), col(2), gspec, gspec],
        out_specs=[col(0)] * 3,
        out_shape=[o_shape] * 3,
        compiler_params=_params("parallel"),
    )(proj, proj, proj, gq, gk)


def _qk_prep_bwd(name, proj, dqn, dkn, dv, gq, gk):
    t = proj.shape[0]
    tm = _row_tile(t)
    nt = t // tm

    def body(q_ref, k_ref, dqn_ref, dkn_ref, dv_ref, gq_ref, gk_ref, dq_ref, dk_ref, dvb_ref, dgq_ref, dgk_ref):
        bd = _head_group_matrix()

        def bwd(xv, dy, g):
            ms = _dot_exact_rhs(xv * xv, bd) * (1.0 / ATTN_DH)
            rstd = lax.rsqrt(ms + RMS_EPS)
            xhat = xv * rstd
            dxhat = dy * g
            gm = _dot_exact_rhs(dxhat * xhat, bd) * (1.0 / ATTN_DH)
            return rstd * (dxhat - xhat * gm), jnp.sum(dy * xhat, axis=0, keepdims=True)

        dq, dgq = bwd(q_ref[...], dqn_ref[...], gq_ref[...])
        dk, dgk = bwd(k_ref[...], dkn_ref[...], gk_ref[...])
        dq_ref[...] = dq.astype(BF16)
        dk_ref[...] = dk.astype(BF16)
        dvb_ref[...] = dv_ref[...].astype(BF16)
        dgq_ref[...] = dgq
        dgk_ref[...] = dgk

    col = lambda j: pl.BlockSpec((tm, ATTN_W), lambda i: (i, j))
    gspec = pl.BlockSpec((1, ATTN_W), lambda i: (0, 0))
    pspec = pl.BlockSpec((None, 1, ATTN_W), lambda i: (i, 0, 0))
    o_shape = jax.ShapeDtypeStruct((t, ATTN_W), BF16)
    p_shape = jax.ShapeDtypeStruct((nt, 1, ATTN_W), F32)
    return pl.pallas_call(
        body,
        name=name,
        grid=(nt,),
        in_specs=[col(0), col(1), col(0), col(0), col(0), gspec, gspec],
        out_specs=[col(0)] * 3 + [pspec] * 2,
        out_shape=[o_shape] * 3 + [p_shape] * 2,
        compiler_params=_params("parallel"),
    )(proj, proj, dqn, dkn, dv, gq, gk)


def _attn_probs(q, kb, bias, start):
    s = _dot(q, kb, NT) * (ATTN_DH ** -0.5) + bias
    col = lax.broadcasted_iota(jnp.int32, (CHUNK, BAND), 1)
    s = jnp.where(col + start >= KPAD, s, -jnp.inf)
    m = jnp.max(s, axis=-1, keepdims=True)
    p = jnp.exp(s - m)
    return p / jnp.sum(p, axis=-1, keepdims=True)


def _attn_fwd(name, q, k, v, bias):
    b, h, s, dh = q.shape
    sp = k.shape[2]
    nc = s // CHUNK
    hg = h

    def body(q_ref, k_ref, v_ref, bias_ref, o_ref):
        start = pl.multiple_of(pl.program_id(2) * CHUNK, CHUNK)
        for hh in range(hg):
            kb = k_ref[hh, pl.ds(start, BAND), :]
            vb = v_ref[hh, pl.ds(start, BAND), :]
            p = _attn_probs(q_ref[hh], kb, bias_ref[hh], start)
            o_ref[hh] = _dot(p.astype(BF16), vb).astype(BF16)

    qspec = pl.BlockSpec((None, hg, CHUNK, dh), lambda bi, g, c: (bi, g, c, 0))
    kspec = pl.BlockSpec((None, hg, sp, dh), lambda bi, g, c: (bi, g, 0, 0))
    return pl.pallas_call(
        body,
        name=name,
        grid=(b, h // hg, nc),
        in_specs=[qspec, kspec, kspec, pl.BlockSpec((hg, CHUNK, BAND), lambda bi, g, c: (g, 0, 0))],
        out_specs=qspec,
        out_shape=jax.ShapeDtypeStruct((b, h, s, dh), BF16),
        compiler_params=_params("parallel", "parallel", "arbitrary"),
    )(q, k, v, bias)


def _attn_bwd(name, q, k, v, bias, do):
    b, h, s, dh = q.shape
    sp = k.shape[2]
    nc = s // CHUNK
    hg = 4

    def body(q_ref, k_ref, v_ref, bias_ref, do_ref, dq_ref, dk_ref, dv_ref, dbias_ref):
        bi = pl.program_id(1)
        c = pl.program_id(2)
        start = pl.multiple_of(c * CHUNK, CHUNK)

        @pl.when(c == 0)
        def _():
            dk_ref[...] = jnp.zeros_like(dk_ref)
            dv_ref[...] = jnp.zeros_like(dv_ref)

        @pl.when(jnp.logical_and(c == 0, bi == 0))
        def _():
            dbias_ref[...] = jnp.zeros_like(dbias_ref)

        for hh in range(hg):
            band = pl.ds(start, BAND)
            qh = q_ref[hh]
            kb = k_ref[hh, band, :]
            vb = v_ref[hh, band, :]
            doh = do_ref[hh]
            p = _attn_probs(qh, kb, bias_ref[hh], start)
            dp = _dot(doh, vb, NT)
            ds = p * (dp - jnp.sum(p * dp, axis=-1, keepdims=True))
            dbias_ref[hh] += ds
            dsb = (ds * (ATTN_DH ** -0.5)).astype(BF16)
            dq_ref[hh] = _dot(dsb, kb)
            dk_ref[hh, band, :] += _dot(dsb, qh, TN)
            dv_ref[hh, band, :] += _dot(p.astype(BF16), doh, TN)

    qspec = pl.BlockSpec((None, hg, CHUNK, dh), lambda g, bi, c: (bi, g, c, 0))
    kspec = pl.BlockSpec((None, hg, sp, dh), lambda g, bi, c: (bi, g, 0, 0))
    bspec = pl.BlockSpec((hg, CHUNK, BAND), lambda g, bi, c: (g, 0, 0))
    return pl.pallas_call(
        body,
        name=name,
        grid=(h // hg, b, nc),
        in_specs=[qspec, kspec, kspec, bspec, qspec],
        out_specs=[qspec, kspec, kspec, bspec],
        out_shape=[jax.ShapeDtypeStruct((b, h, s, dh), F32), jax.ShapeDtypeStruct((b, h, sp, dh), F32),
                   jax.ShapeDtypeStruct((b, h, sp, dh), F32), jax.ShapeDtypeStruct((h, CHUNK, BAND), F32)],
        compiler_params=_params("arbitrary", "arbitrary", "arbitrary"),
    )(q, k, v, bias, do)


HQ_COL = 3 * ATTN_W // HGRN_DH
HF_COL = HQ_COL + HGRN_HEADS
HI_COL = HF_COL + HGRN_HEADS
HG_COL = HI_COL + HGRN_HEADS


def _tri(lower):
    r = lax.broadcasted_iota(jnp.int32, (CHUNK, CHUNK), 0)
    c = lax.broadcasted_iota(jnp.int32, (CHUNK, CHUNK), 1)
    return (r >= c) if lower else (r <= c)


def _hgrn_chunk(hq, hf, lb, tril):
    sig = _sigmoid(hf)
    f = lb + (1.0 - lb) * sig
    g = jnp.log(f)
    ones_l = jnp.where(tril, 1.0, 0.0).astype(BF16)
    b = _dot_exact_lhs(ones_l, g)
    bl = jnp.sum(g, axis=0, keepdims=True)
    rows = lax.broadcasted_iota(jnp.int32, g.shape, 0)
    bm = jnp.sum(jnp.where(rows <= CHUNK // 2, g, 0.0), axis=0, keepdims=True)
    sq = _sigmoid(hq)
    q = hq * sq
    k = 1.0 - f
    return sig, f, b, bl, bm, sq, q, k


def _hgrn_fwd(name, proj, lb, go, b, s):
    nc = s // CHUNK
    t = b * s

    def body(hq_ref, hf_ref, hi_ref, hg_ref, lb_ref, go_ref, ro_ref, oraw_ref, st_ref, s_scr):
        tril = _tri(True)
        lbv = lb_ref[...]
        gov = go_ref[...]
        s_scr[...] = jnp.zeros_like(s_scr)

        def step(c, carry):
            sl = pl.ds(pl.multiple_of(c * CHUNK, CHUNK), CHUNK)
            hg = hg_ref[sl, :]
            _, _, bb, bl, bm, _, q, k = _hgrn_chunk(hq_ref[sl, :], hf_ref[sl, :], lbv, tril)
            vb = hi_ref[sl, :].astype(BF16)
            qe = (q * jnp.exp(bb - bm)).astype(BF16)
            ke = (k * jnp.exp(bm - bb)).astype(BF16)
            a = jnp.where(tril, _dot(qe, ke, NT), 0.0)
            st = s_scr[...]
            st_ref[c] = st
            qb = (q * jnp.exp(bb)).astype(BF16)
            o = _dot(a.astype(BF16), vb) + _dot(qb, st.astype(BF16), NT)
            kb = (k * jnp.exp(bl - bb)).astype(BF16)
            s_scr[...] = st * jnp.exp(bl) + _dot(vb, kb, TN)
            rstd = lax.rsqrt(jnp.mean(o * o, axis=-1, keepdims=True) + RMS_EPS)
            ro_ref[sl, :] = ((o * rstd * gov) * _silu(hg)).astype(BF16)
            oraw_ref[sl, :] = o
            return carry

        lax.fori_loop(0, nc, step, 0)

    col = lambda base: pl.BlockSpec((s, HGRN_DH), lambda bi, h: (bi, base + h))
    vec = pl.BlockSpec((1, HGRN_DH), lambda bi, h: (0, h))
    out = pl.BlockSpec((s, HGRN_DH), lambda bi, h: (bi, h))
    return pl.pallas_call(
        body,
        name=name,
        grid=(b, HGRN_HEADS),
        in_specs=[col(HQ_COL), col(HF_COL), col(HI_COL), col(HG_COL), vec,
                  pl.BlockSpec((1, HGRN_DH), lambda bi, h: (0, 0))],
        out_specs=[out, out,
                   pl.BlockSpec((None, None, nc, HGRN_DH, HGRN_DH), lambda bi, h: (bi, h, 0, 0, 0))],
        out_shape=[jax.ShapeDtypeStruct((t, HGRN_W), BF16), jax.ShapeDtypeStruct((t, HGRN_W), F32),
                   jax.ShapeDtypeStruct((b, HGRN_HEADS, nc, HGRN_DH, HGRN_DH), F32)],
        scratch_shapes=[pltpu.VMEM((HGRN_DH, HGRN_DH), F32)],
        compiler_params=_params("parallel", "parallel"),
    )(proj, proj, proj, proj, lb, go)


def _hgrn_bwd(name, proj, lb, go, oraw, states, dmix, b, s):
    nc = s // CHUNK
    t = b * s

    def body(hq_ref, hf_ref, hi_ref, hg_ref, lb_ref, go_ref, oraw_ref, st_ref, dro_ref,
             dhq_ref, dhf_ref, dhi_ref, dhg_ref, dlb_ref, dgo_ref, ds_scr, dlb_scr, dgo_scr):
        tril = _tri(True)
        ones_u = jnp.where(_tri(False), 1.0, 0.0).astype(BF16)
        lbv = lb_ref[...]
        gov = go_ref[...]
        ds_scr[...] = jnp.zeros_like(ds_scr)
        dlb_scr[...] = jnp.zeros_like(dlb_scr)
        dgo_scr[...] = jnp.zeros_like(dgo_scr)

        def step(ci, carry):
            c = nc - 1 - ci
            sl = pl.ds(pl.multiple_of(c * CHUNK, CHUNK), CHUNK)
            hq = hq_ref[sl, :]
            hg = hg_ref[sl, :]
            sig, f, bb, bl, bm, sq, q, k = _hgrn_chunk(hq, hf_ref[sl, :], lbv, tril)
            vb = hi_ref[sl, :].astype(BF16)
            ebm = jnp.exp(bb - bm)
            embm = jnp.exp(bm - bb)
            eb = jnp.exp(bb)
            ebl = jnp.exp(bl - bb)
            e_last = jnp.exp(bl)
            qe = (q * ebm).astype(BF16)
            ke = (k * embm).astype(BF16)
            qb = (q * eb).astype(BF16)
            kb = (k * ebl).astype(BF16)
            a = jnp.where(tril, _dot(qe, ke, NT), 0.0)
            st = st_ref[c]
            dst = ds_scr[...]
            o = oraw_ref[sl, :]
            dro = dro_ref[sl, :]
            sg = _sigmoid(hg)
            rstd = lax.rsqrt(jnp.mean(o * o, axis=-1, keepdims=True) + RMS_EPS)
            ohat = o * rstd
            dn = dro * (hg * sg)
            dhg_ref[sl, :] = (dro * (ohat * gov) * (sg * (1.0 + hg * (1.0 - sg)))).astype(BF16)
            dgo_scr[...] += jnp.sum(dn * ohat, axis=0, keepdims=True)
            dohat = dn * gov
            do = rstd * (dohat - ohat * jnp.mean(dohat * ohat, axis=-1, keepdims=True))
            dob = do.astype(BF16)
            dab = jnp.where(tril, _dot(dob, vb, NT), 0.0).astype(BF16)
            stb = st.astype(BF16)
            dstb = dst.astype(BF16)
            dv = _dot(a.astype(BF16), dob, TN) + _dot(kb, dstb, NT)
            dqe = _dot(dab, ke)
            dke = _dot(dab, qe, TN)
            dqb = _dot(dob, stb)
            dkb = _dot(vb, dstb)
            dq = dqe * ebm + dqb * eb
            dk = dke * embm + dkb * ebl
            db = (qe.astype(F32) * dqe - ke.astype(F32) * dke) + q * (dqb * eb) - k * (dkb * ebl)
            d_last = (jnp.sum(k * ebl * dkb, axis=0, keepdims=True)
                      + jnp.sum(dst * st, axis=0, keepdims=True) * e_last)
            dg = _dot_exact_lhs(ones_u, db) + d_last
            df = dg / f - dk
            dhf_ref[sl, :] = (df * (1.0 - lbv) * sig * (1.0 - sig)).astype(BF16)
            dlb_scr[...] += jnp.sum(df * (1.0 - sig), axis=0, keepdims=True)
            dhq_ref[sl, :] = (dq * (sq * (1.0 + hq * (1.0 - sq)))).astype(BF16)
            dhi_ref[sl, :] = dv.astype(BF16)
            ds_scr[...] = dst * e_last + _dot(dob, qb, TN)
            return carry

        lax.fori_loop(0, nc, step, 0)
        dlb_ref[...] = dlb_scr[...]
        dgo_ref[...] = dgo_scr[...]

    col = lambda base: pl.BlockSpec((s, HGRN_DH), lambda bi, h: (bi, base + h))
    vec = pl.BlockSpec((1, HGRN_DH), lambda bi, h: (0, h))
    out = pl.BlockSpec((s, HGRN_DH), lambda bi, h: (bi, h))
    part = pl.BlockSpec((None, 1, HGRN_DH), lambda bi, h: (bi, 0, h))
    o_shape = jax.ShapeDtypeStruct((t, HGRN_W), BF16)
    p_shape = jax.ShapeDtypeStruct((b, 1, HGRN_W), F32)
    return pl.pallas_call(
        body,
        name=name,
        grid=(b, HGRN_HEADS),
        in_specs=[col(HQ_COL), col(HF_COL), col(HI_COL), col(HG_COL), vec,
                  pl.BlockSpec((1, HGRN_DH), lambda bi, h: (0, 0)), out,
                  pl.BlockSpec((None, None, nc, HGRN_DH, HGRN_DH), lambda bi, h: (bi, h, 0, 0, 0)),
                  col(ATTN_W // HGRN_DH)],
        out_specs=[out] * 4 + [part] * 2,
        out_shape=[o_shape] * 4 + [p_shape] * 2,
        scratch_shapes=[pltpu.VMEM((HGRN_DH, HGRN_DH), F32), pltpu.VMEM((1, HGRN_DH), F32),
                        pltpu.VMEM((1, HGRN_DH), F32)],
        compiler_params=_params("parallel", "parallel"),
    )(proj, proj, proj, proj, lb, go, oraw, states, dmix)


def _small_grads(name, dg1, dgm, dg2, dgq, dgk, dbias_t, dlb, dgo, lbp):
    d = dg1.shape[1]

    def body(dg1_ref, dgm_ref, dg2_ref, dgq_ref, dgk_ref, dbias_ref, dlb_ref, dgo_ref, lbp_ref,
             g1_ref, gm_ref, g2_ref, gq_ref, gk_ref, rb_ref, lbg_ref, go_ref):
        g1_ref[...] = jnp.sum(dg1_ref[...], axis=0, keepdims=True)
        gm_ref[...] = jnp.sum(dgm_ref[...], axis=0, keepdims=True)
        g2_ref[...] = jnp.sum(dg2_ref[...], axis=0, keepdims=True)
        r = lax.broadcasted_iota(jnp.int32, (ATTN_W, ATTN_DH), 0)
        cidx = lax.broadcasted_iota(jnp.int32, (ATTN_W, ATTN_DH), 1)
        fold = jnp.where(jnp.bitwise_and(r, ATTN_DH - 1) == cidx, 1.0, 0.0).astype(BF16)
        gq_ref[...] = jnp.sum(_dot_exact_rhs(dgq_ref[...], fold), axis=0, keepdims=True)
        gk_ref[...] = jnp.sum(_dot_exact_rhs(dgk_ref[...], fold), axis=0, keepdims=True)
        gosum = jnp.sum(dgo_ref[...], axis=0, keepdims=True)
        go_ref[...] = (gosum[:, 0:HGRN_DH] + gosum[:, HGRN_DH:2 * HGRN_DH]
                       + gosum[:, 2 * HGRN_DH:3 * HGRN_DH] + gosum[:, 3 * HGRN_DH:4 * HGRN_DH])
        p0 = lbp_ref[0:1, :]
        p1 = lbp_ref[1:2, :]
        lbv = 1.0 / (1.0 + jnp.exp(p1 - p0))
        dp0 = jnp.sum(dlb_ref[...], axis=0, keepdims=True) * lbv * (1.0 - lbv)
        lbg_ref[0:1, :] = dp0
        lbg_ref[1:2, :] = -dp0
        sidx = lax.broadcasted_iota(jnp.int32, (BAND, N_REL_PAD), 0)
        ridx = lax.broadcasted_iota(jnp.int32, (BAND, N_REL_PAD), 1)

        def step(tq, acc):
            rel = jnp.clip(tq + KPAD - sidx, -REL_CLIP, REL_CLIP) + REL_CLIP
            onehot = jnp.where(rel == ridx, 1.0, 0.0).astype(BF16)
            return acc + _dot_exact_rhs(dbias_ref[tq], onehot)

        rb_ref[...] = lax.fori_loop(0, CHUNK, step, jnp.zeros((ATTN_HEADS, N_REL_PAD), F32))

    ins = [dg1, dgm, dg2, dgq, dgk, dbias_t, dlb, dgo, lbp]
    outs = [jax.ShapeDtypeStruct((1, d), F32)] * 3 + [jax.ShapeDtypeStruct((1, ATTN_DH), F32)] * 2 + [
        jax.ShapeDtypeStruct((ATTN_HEADS, N_REL_PAD), F32), jax.ShapeDtypeStruct((2, HGRN_W), F32),
        jax.ShapeDtypeStruct((1, HGRN_DH), F32)]
    vm = pl.BlockSpec(memory_space=pltpu.VMEM)
    return pl.pallas_call(
        body,
        name=name,
        in_specs=[vm] * len(ins),
        out_specs=[vm] * len(outs),
        out_shape=outs,
        compiler_params=pltpu.CompilerParams(vmem_limit_bytes=VMEM_LIMIT),
    )(*ins)


def _adam_update(w, g, m, v):
    m2 = ADAM_B1 * m + (1.0 - ADAM_B1) * g
    v2 = ADAM_B2 * v + (1.0 - ADAM_B2) * (g * g)
    m_hat = m2 / (1.0 - ADAM_B1 ** ADAM_STEP)
    v_hat = v2 / (1.0 - ADAM_B2 ** ADAM_STEP)
    delta = -ADAM_LR * (m_hat / (jnp.sqrt(v_hat) + ADAM_EPS) + ADAM_WD * w)
    return delta, m2, v2


def _rows_tile(r):
    for cand in (256, 352, 128, 176, 64, 32, 16):
        if r % cand == 0 and r > cand:
            return cand
    return r


def _pair_sum(name, grad, theirs, core):
    n, half, c = theirs.shape
    tr = _rows_tile(half)
    nth = half // tr

    def body(core_ref, a_ref, b_ref, o_ref):
        o_ref[...] = (a_ref[...] + b_ref[...]).astype(o_ref.dtype)

    spec = pl.BlockSpec((None, tr, c), lambda i, j, core_ref: (i, j, 0))
    return pl.pallas_call(
        body, name=name,
        grid_spec=pltpu.PrefetchScalarGridSpec(
            num_scalar_prefetch=1, grid=(n, nth),
            in_specs=[pl.BlockSpec((None, tr, c), lambda i, j, core_ref: (i, core_ref[0] * nth + j, 0)), spec],
            out_specs=spec),
        out_shape=jax.ShapeDtypeStruct((n, half, c), BF16), compiler_params=_params("parallel", "parallel"),
    )(core, grad, theirs)


def _chip_sum(name, own, parts, chip):
    _, half, c = own.shape
    tr = _rows_tile(half)

    def body(chip_ref, own_ref, p_ref, o_ref):
        me = chip_ref[0]
        mine = own_ref[...].astype(F32)
        flip_x, flip_y, flip_xy = (p_ref[i].astype(F32) for i in range(3))
        acc = None
        for k in range(N_CHIPS):
            rel = jnp.bitwise_xor(me, k)
            term = jnp.where(rel == 0, mine, jnp.where(rel == 2, flip_x, jnp.where(rel == 1, flip_y, flip_xy)))
            acc = term if acc is None else acc + term
        o_ref[...] = acc

    return pl.pallas_call(
        body, name=name,
        grid_spec=pltpu.PrefetchScalarGridSpec(
            num_scalar_prefetch=1, grid=(half // tr,),
            in_specs=[pl.BlockSpec((None, tr, c), lambda j, chip_ref: (chip_ref[0], j, 0)),
                      pl.BlockSpec((3, tr, c), lambda j, chip_ref: (0, j, 0))],
            out_specs=pl.BlockSpec((tr, c), lambda j, chip_ref: (j, 0))),
        out_shape=jax.ShapeDtypeStruct((half, c), F32), compiler_params=_params("parallel"),
    )(chip, own, parts)


def _adamw(name, w, g_mine, g_theirs, m, v, core):
    r, c = w.shape
    half = r // 2
    tr = _rows_tile(half)
    nth = half // tr

    def body(core_ref, w_ref, gm_ref, gt_ref, m_ref, v_ref, g_ref, d_ref, m2_ref, v2_ref):
        g = jnp.where(pl.program_id(0) == core_ref[0], gm_ref[...], gt_ref[...])
        delta, m2, v2 = _adam_update(w_ref[...], g, m_ref[...], v_ref[...])
        g_ref[...] = g
        d_ref[...] = delta
        m2_ref[...] = m2
        v2_ref[...] = v2

    full = pl.BlockSpec((tr, c), lambda h, j, core_ref: (h * nth + j, 0))
    part = pl.BlockSpec((tr, c), lambda h, j, core_ref: (j, 0))
    shape = jax.ShapeDtypeStruct((r, c), F32)
    return pl.pallas_call(
        body, name=name,
        grid_spec=pltpu.PrefetchScalarGridSpec(
            num_scalar_prefetch=1, grid=(2, nth), in_specs=[full, part, part, full, full], out_specs=[full] * 4),
        out_shape=[shape] * 4, compiler_params=_params("parallel", "parallel"),
    )(core, w, g_mine, g_theirs, m, v)


def _rel_bias_table(name, rel_bias):
    padded = jnp.pad(rel_bias, ((0, 0), (0, N_REL_PAD - N_REL)))

    def body(rb_ref, o_ref):
        ridx = lax.broadcasted_iota(jnp.int32, (N_REL_PAD, BAND), 0)
        sidx = lax.broadcasted_iota(jnp.int32, (N_REL_PAD, BAND), 1)
        rb = rb_ref[...]

        def step(tq, carry):
            rel = jnp.clip(tq + KPAD - sidx, -REL_CLIP, REL_CLIP) + REL_CLIP
            onehot = jnp.where(rel == ridx, 1.0, 0.0).astype(BF16)
            o_ref[tq] = _dot_exact_rhs(rb, onehot)
            return carry

        lax.fori_loop(0, CHUNK, step, 0)

    vm = pl.BlockSpec(memory_space=pltpu.VMEM)
    table = pl.pallas_call(
        body, name=name, in_specs=[vm], out_specs=vm,
        out_shape=jax.ShapeDtypeStruct((CHUNK, ATTN_HEADS, BAND), F32),
    )(padded)
    return table.transpose(1, 0, 2)


def _adamw_small(name, w, parts, m, v):
    def body(w_ref, p_ref, m_ref, v_ref, g_ref, d_ref, m2_ref, v2_ref):
        g = p_ref[0]
        for i in range(1, N_DEV):
            g = g + p_ref[i]
        delta, m2, v2 = _adam_update(w_ref[...], g, m_ref[...], v_ref[...])
        g_ref[...] = g
        d_ref[...] = delta
        m2_ref[...] = m2
        v2_ref[...] = v2

    vm = pl.BlockSpec(memory_space=pltpu.VMEM)
    shape = jax.ShapeDtypeStruct((SMALL_ROWS, SMALL_COLS), F32)
    return pl.pallas_call(
        body, name=name, in_specs=[vm] * 4, out_specs=[vm] * 4, out_shape=[shape] * 4,
    )(w, parts, m, v)


def _position():
    return lax.axis_index("x"), lax.axis_index("y"), lax.axis_index("c")


def _other_chips(x, y):
    return [(1 - x, y), (x, 1 - y), (1 - x, 1 - y)]


ANY = pl.BlockSpec(memory_space=pl.ANY)


def _gather_weights(shards):
    n = len(shards)

    def body(*refs):
        ins, outs = refs[:n], refs[n:2 * n]
        ici_send, ici_recv, fwd_send, fwd_recv, own_send, own_recv = refs[2 * n:]
        x, y, c = _position()
        me = 2 * x + y
        chips = _other_chips(x, y)

        def own(i):
            return pltpu.make_async_remote_copy(
                src_ref=ins[i], dst_ref=outs[i].at[me], send_sem=own_send.at[i], recv_sem=own_recv.at[i],
                device_id=(x, y, 1 - c), device_id_type=MESH)

        def push(i, j):
            return pltpu.make_async_remote_copy(
                src_ref=ins[i], dst_ref=outs[i].at[me], send_sem=ici_send.at[3 * i + j], recv_sem=ici_recv.at[3 * i + j],
                device_id=(chips[j][0], chips[j][1], 1), device_id_type=MESH)

        def arrival(i, j):
            return pltpu.make_async_remote_copy(
                src_ref=ins[i], dst_ref=outs[i].at[2 * chips[j][0] + chips[j][1]],
                send_sem=ici_send.at[3 * i + j], recv_sem=ici_recv.at[3 * i + j],
                device_id=(chips[j][0], chips[j][1], 1), device_id_type=MESH)

        def onward(i, j):
            slot = outs[i].at[2 * chips[j][0] + chips[j][1]]
            return pltpu.make_async_remote_copy(
                src_ref=slot, dst_ref=slot, send_sem=fwd_send.at[3 * i + j], recv_sem=fwd_recv.at[3 * i + j],
                device_id=(x, y, 1 - c), device_id_type=MESH)

        @pl.when(c == 1)
        def _north():
            for i in range(n):
                for j in range(3):
                    push(i, j).start()

        for i in range(n):
            own(i).start()

        @pl.when(c == 1)
        def _north_forward():
            for i in range(n):
                for j in range(3):
                    arrival(i, j).wait_recv()
                    onward(i, j).start()
            for i in range(n):
                for j in range(3):
                    push(i, j).wait_send()
                    onward(i, j).wait_send()

        @pl.when(c == 0)
        def _south():
            for i in range(n):
                for j in range(3):
                    onward(i, j).wait_recv()

        for i in range(n):
            own(i).wait()

    return pl.pallas_call(
        body,
        name="gather_weights",
        in_specs=[ANY] * n,
        out_specs=[ANY] * n,
        out_shape=[jax.ShapeDtypeStruct((N_CHIPS,) + s.shape, s.dtype) for s in shards],
        scratch_shapes=[pltpu.SemaphoreType.DMA((3 * n,))] * 4 + [pltpu.SemaphoreType.DMA((n,))] * 2,
    )(*shards)


def _pair_exchange(grads):
    n = len(grads)

    def body(*refs):
        ins, theirs = refs[:n], refs[n:2 * n]
        send_sem, recv_sem = refs[2 * n:]
        x, y, c = _position()
        copies = []
        for i in range(n):
            half = ins[i].shape[1] // 2
            give = pl.ds(pl.multiple_of((1 - c) * half, 8), half)
            swap = pltpu.make_async_remote_copy(
                src_ref=ins[i].at[:, give, :], dst_ref=theirs[i], send_sem=send_sem.at[i], recv_sem=recv_sem.at[i],
                device_id=(x, y, 1 - c), device_id_type=MESH)
            swap.start()
            copies.append(swap)
        for swap in copies:
            swap.wait()

    return pl.pallas_call(
        body,
        name="pair_exchange",
        in_specs=[ANY] * n,
        out_specs=[ANY] * n,
        out_shape=[jax.ShapeDtypeStruct((g.shape[0], g.shape[1] // 2, g.shape[2]), g.dtype) for g in grads],
        scratch_shapes=[pltpu.SemaphoreType.DMA((n,))] * 2,
    )(*grads)


def _chip_scatter(sums):
    n = len(sums)

    def body(*refs):
        ins, outs = refs[:n], refs[n:2 * n]
        send_sem, recv_sem = refs[2 * n:]
        x, y, c = _position()
        chips = _other_chips(x, y)

        def push(i, j):
            return pltpu.make_async_remote_copy(
                src_ref=ins[i].at[2 * chips[j][0] + chips[j][1]], dst_ref=outs[i].at[j],
                send_sem=send_sem.at[3 * i + j], recv_sem=recv_sem.at[3 * i + j],
                device_id=(chips[j][0], chips[j][1], c), device_id_type=MESH)

        for i in range(n):
            for j in range(3):
                push(i, j).start()
        for i in range(n):
            for j in range(3):
                push(i, j).wait()

    return pl.pallas_call(
        body,
        name="chip_scatter",
        in_specs=[ANY] * n,
        out_specs=[ANY] * n,
        out_shape=[jax.ShapeDtypeStruct((3,) + s.shape[1:], s.dtype) for s in sums],
        scratch_shapes=[pltpu.SemaphoreType.DMA((3 * n,))] * 2,
    )(*sums)


def _pair_join(halves, small):
    n = len(halves)

    def body(*refs):
        ins, small_ref = refs[:n], refs[n]
        outs, all_ref = refs[n + 1:2 * n + 1], refs[2 * n + 1]
        send_sem, recv_sem, sm_send, sm_recv, sm_local = refs[2 * n + 2:]
        x, y, c = _position()
        swaps = []
        for i in range(n):
            swap = pltpu.make_async_remote_copy(
                src_ref=ins[i], dst_ref=outs[i], send_sem=send_sem.at[i], recv_sem=recv_sem.at[i],
                device_id=(x, y, 1 - c), device_id_type=MESH)
            swap.start()
            swaps.append(swap)
        me = 4 * x + 2 * y + c
        sm_own = pltpu.make_async_copy(small_ref, all_ref.at[me], sm_local)
        sm_own.start()
        pushes, arrivals = [], []
        for mask in range(1, N_DEV):
            px, py, pc = x ^ (mask >> 2), y ^ ((mask >> 1) & 1), c ^ (mask & 1)
            pushes.append(pltpu.make_async_remote_copy(
                src_ref=small_ref, dst_ref=all_ref.at[me], send_sem=sm_send.at[mask - 1], recv_sem=sm_recv.at[mask - 1],
                device_id=(px, py, pc), device_id_type=MESH))
            arrivals.append(pltpu.make_async_remote_copy(
                src_ref=small_ref, dst_ref=all_ref.at[4 * px + 2 * py + pc], send_sem=sm_send.at[mask - 1],
                recv_sem=sm_recv.at[mask - 1], device_id=(px, py, pc), device_id_type=MESH))
        for cp in pushes:
            cp.start()
        for swap in swaps:
            swap.wait()
        for cp in arrivals:
            cp.wait_recv()
        for cp in pushes:
            cp.wait_send()
        sm_own.wait()

    res = pl.pallas_call(
        body,
        name="pair_join",
        in_specs=[ANY] * (n + 1),
        out_specs=[ANY] * (n + 1),
        out_shape=[jax.ShapeDtypeStruct(h.shape, h.dtype) for h in halves]
        + [jax.ShapeDtypeStruct((N_DEV,) + small.shape, small.dtype)],
        scratch_shapes=[pltpu.SemaphoreType.DMA((n,))] * 2 + [pltpu.SemaphoreType.DMA((N_DEV - 1,))] * 2
        + [pltpu.SemaphoreType.DMA(())],
    )(*halves, small)
    return res[:n], res[n]


def _to_heads(a, b, s):
    return a.reshape(b, s, ATTN_HEADS, ATTN_DH).transpose(0, 2, 1, 3)


def _from_heads(a):
    b, h, s, dh = a.shape
    return a.transpose(0, 2, 1, 3).reshape(b * s, h * dh)


def _pad_keys(a):
    return jnp.pad(a, ((0, 0), (0, 0), (KPAD, 0), (0, 0)))


def _lower_bound(lbp):
    return jax.nn.softmax(lbp, axis=0)[0:1]


def _local_step(x, target, g1, gm, g2, gq, gk, go, rel_bias, lbp, wg1, wu1, wd1, w_in, w_out, wg2, wu2, wd2):
    b, s, d = x.shape
    t = b * s
    ns = w_in.shape[0]
    x0 = x.reshape(t, d)
    tgt = target.reshape(t, d)
    gq_t = jnp.tile(gq, (1, ATTN_HEADS))
    gk_t = jnp.tile(gk, (1, ATTN_HEADS))
    lb = _lower_bound(lbp)
    bias = _rel_bias_table("rel_bias_table", rel_bias)

    h1 = _rmsnorm_fwd("norm1", x0, g1)
    a1, b1, z1 = _ffn_up("ffn1_up", h1, wg1, wu1)
    x1 = _ffn_down("ffn1_down", z1, wd1, x0)
    h2 = _rmsnorm_fwd("norm_mix", x1, gm)
    proj = _in_proj("in_proj", h2, w_in)
    qn, kn, vb = _qk_prep("qk_prep", proj, gq_t, gk_t)
    qh = _to_heads(qn, b, s)
    kh = _pad_keys(_to_heads(kn, b, s))
    vh = _pad_keys(_to_heads(vb, b, s))
    attn = _from_heads(_attn_fwd("attn_fwd", qh, kh, vh, bias))
    ro, oraw, states = _hgrn_fwd("hgrn_fwd", proj, lb, go, b, s)
    mix = jnp.concatenate([attn, ro], axis=1)
    x2 = _out_proj("out_proj", mix, w_out, x1)
    h3 = _rmsnorm_fwd("norm2", x2, g2)
    a2, b2, z2 = _ffn_up("ffn2_up", h3, wg2, wu2)
    dy, dyh, sq = _ffn_down_loss("ffn2_down_loss", z2, wd2, x2, tgt)
    loss = 0.5 * jnp.sum(sq) / d

    da2, db2 = _ffn_bwd_act("ffn2_bwd_act", dyh, wd2, a2, b2)
    dwd2 = _grad_w_shardrows("ffn2_dwd", z2, dyh)
    dwg2 = _grad_w_shardcols("ffn2_dwg", h3, da2)
    dwu2 = _grad_w_shardcols("ffn2_dwu", h3, db2)
    dx2, dx2b, dg2 = _ffn_bwd_in("ffn2_bwd_in", da2, db2, wg2, wu2, x2, g2, dy, 1.0)

    dwout = _grad_w_out("dw_out", mix, dx2b)
    dmix = _out_proj_bwd("out_proj_bwd", dx2b, w_out)
    doh = _to_heads(dmix[:, :ATTN_W].astype(BF16), b, s)
    dqh, dkh, dvh, dbias = _attn_bwd("attn_bwd", qh, kh, vh, bias, doh)
    dqn = _from_heads(dqh)
    dkn = _from_heads(dkh[:, :, KPAD:, :])
    dvn = _from_heads(dvh[:, :, KPAD:, :])
    dpq, dpk, dpv, dgq, dgk = _qk_prep_bwd("qk_prep_bwd", proj, dqn, dkn, dvn, gq_t, gk_t)
    dhq, dhf, dhi, dhg, dlb, dgo = _hgrn_bwd("hgrn_bwd", proj, lb, go, oraw, states, dmix, b, s)
    dproj = jnp.concatenate([dpq, dpk, dpv, dhq, dhf, dhi, dhg], axis=1)
    dwin = _grad_w_in("dw_in", h2, dproj, ns)
    dx1, dx1h, dgm = _in_proj_bwd("in_proj_bwd", dproj, w_in, x1, gm, dx2, 0.5)

    da1, db1 = _ffn_bwd_act("ffn1_bwd_act", dx1h, wd1, a1, b1)
    dwd1 = _grad_w_shardrows("ffn1_dwd", z1, dx1h)
    dwg1 = _grad_w_shardcols("ffn1_dwg", h1, da1)
    dwu1 = _grad_w_shardcols("ffn1_dwu", h1, db1)
    dx0, dg1 = _ffn_bwd_in("ffn1_bwd_in", da1, db1, wg1, wu1, x0, g1, dx1, None)

    nt = dg1.shape[0]
    sg = _small_grads(
        "small_grads", dg1.reshape(nt, d), dgm.reshape(nt, d), dg2.reshape(nt, d),
        dgq.reshape(nt, ATTN_W), dgk.reshape(nt, ATTN_W), dbias.transpose(1, 0, 2),
        dlb.reshape(b, HGRN_W), dgo.reshape(b, HGRN_W), lbp)
    g1g, gmg, g2g, gqg, gkg, rbg, lbg, gog = sg
    small = _pack_small(g1g, gmg, g2g, lbg, rbg[:, :N_REL], gqg, gkg, gog)
    big = [dwg1, dwu1, dwd1, dwin, dwout.reshape(ns, dwout.shape[0] // ns, d), dwg2, dwu2, dwd2]
    return loss, dx0.reshape(b, s, d), big, small


def _pack_small(g1, gm, g2, lbp, rel_bias, gq, gk, go):
    flat = [g1.reshape(-1), gm.reshape(-1), g2.reshape(-1), lbp.reshape(-1), rel_bias.reshape(-1)]
    n_bias = 3 * SMALL_COLS - rel_bias.size
    heads = [gq.reshape(-1), gk.reshape(-1), go.reshape(-1)]
    n_tail = SMALL_COLS - sum(h.size for h in heads)
    return jnp.concatenate(flat + [jnp.zeros((n_bias,), F32)] + heads + [jnp.zeros((n_tail,), F32)]).reshape(
        SMALL_ROWS, SMALL_COLS)


def _unpack_small(p, d):
    flat = p.reshape(-1)
    o = 3 * d
    g1, gm, g2 = p[0:1], p[1:2], p[2:3]
    lbp = flat[o:o + 2 * HGRN_W].reshape(2, HGRN_W)
    o = 4 * SMALL_COLS
    rel = flat[o:o + ATTN_HEADS * N_REL].reshape(1, ATTN_HEADS, N_REL)
    o = 7 * SMALL_COLS
    gq = flat[o:o + ATTN_DH].reshape(1, ATTN_DH)
    gk = flat[o + ATTN_DH:o + 2 * ATTN_DH].reshape(1, ATTN_DH)
    go = flat[o + 2 * ATTN_DH:o + 2 * ATTN_DH + HGRN_DH].reshape(1, HGRN_DH)
    return g1, gm, g2, gq, gk, rel, lbp, go


def kernel(x, ffn1_norm_g, ffn1_w_gate, ffn1_w_up, ffn1_w_down, mix_norm_g, w_in, attn_q_norm_g, attn_k_norm_g, attn_rel_bias, hgrn_lower_bounds, hgrn_out_norm_g, w_out, ffn2_norm_g, ffn2_w_gate, ffn2_w_up, ffn2_w_down, loss_target, m_ffn1_norm_g, m_ffn1_w_gate, m_ffn1_w_up, m_ffn1_w_down, m_mix_norm_g, m_w_in, m_attn_q_norm_g, m_attn_k_norm_g, m_attn_rel_bias, m_hgrn_lower_bounds, m_hgrn_out_norm_g, m_w_out, m_ffn2_norm_g, m_ffn2_w_gate, m_ffn2_w_up, m_ffn2_w_down, v_ffn1_norm_g, v_ffn1_w_gate, v_ffn1_w_up, v_ffn1_w_down, v_mix_norm_g, v_w_in, v_attn_q_norm_g, v_attn_k_norm_g, v_attn_rel_bias, v_hgrn_lower_bounds, v_hgrn_out_norm_g, v_w_out, v_ffn2_norm_g, v_ffn2_w_gate, v_ffn2_w_up, v_ffn2_w_down):
    d = x.shape[-1]
    big_w = [ffn1_w_gate, ffn1_w_up, ffn1_w_down, w_in, w_out, ffn2_w_gate, ffn2_w_up, ffn2_w_down]
    big_m = [m_ffn1_w_gate, m_ffn1_w_up, m_ffn1_w_down, m_w_in, m_w_out, m_ffn2_w_gate, m_ffn2_w_up, m_ffn2_w_down]
    big_v = [v_ffn1_w_gate, v_ffn1_w_up, v_ffn1_w_down, v_w_in, v_w_out, v_ffn2_w_gate, v_ffn2_w_up, v_ffn2_w_down]
    big_names = ["ffn1_w_gate", "ffn1_w_up", "ffn1_w_down", "w_in", "w_out", "ffn2_w_gate", "ffn2_w_up", "ffn2_w_down"]

    full = _gather_weights([w[0].astype(BF16) for w in big_w])
    wg1, wu1, wd1, win_f, wout_f, wg2, wu2, wd2 = full
    wout_f = wout_f.reshape(wout_f.shape[0] * wout_f.shape[1], d)

    loss, grad_x, big_g, small_g = _local_step(
        x, loss_target, ffn1_norm_g, mix_norm_g, ffn2_norm_g, attn_q_norm_g, attn_k_norm_g, hgrn_out_norm_g,
        attn_rel_bias[0], hgrn_lower_bounds, wg1, wu1, wd1, win_f, wout_f, wg2, wu2, wd2)
    loss = lax.psum(loss, ("x", "y", "c"))

    core = lax.axis_index("c").astype(jnp.int32).reshape(1)
    chip = (2 * lax.axis_index("x") + lax.axis_index("y")).astype(jnp.int32).reshape(1)
    theirs = _pair_exchange(big_g)
    sums = [_pair_sum("pair_sum_" + nm, g, t, core) for nm, g, t in zip(big_names, big_g, theirs)]
    parts = _chip_scatter(sums)
    halves = [_chip_sum("chip_sum_" + nm, s, p, chip) for nm, s, p in zip(big_names, sums, parts)]
    other_halves, small_all = _pair_join(halves, small_g)

    big_out = [_adamw("adamw_" + nm, w[0], gm, gt, m[0], v[0], core)
               for nm, w, gm, gt, m, v in zip(big_names, big_w, halves, other_halves, big_m, big_v)]
    pack = lambda g1, gm, g2, gq, gk, rel, lbp, go: _pack_small(g1, gm, g2, lbp, rel[0], gq, gk, go)
    small_w = pack(ffn1_norm_g, mix_norm_g, ffn2_norm_g, attn_q_norm_g, attn_k_norm_g, attn_rel_bias, hgrn_lower_bounds, hgrn_out_norm_g)
    small_m = pack(m_ffn1_norm_g, m_mix_norm_g, m_ffn2_norm_g, m_attn_q_norm_g, m_attn_k_norm_g, m_attn_rel_bias, m_hgrn_lower_bounds, m_hgrn_out_norm_g)
    small_v = pack(v_ffn1_norm_g, v_mix_norm_g, v_ffn2_norm_g, v_attn_q_norm_g, v_attn_k_norm_g, v_attn_rel_bias, v_hgrn_lower_bounds, v_hgrn_out_norm_g)
    small_out = [_unpack_small(p, d) for p in _adamw_small("adamw_small", small_w, small_all, small_m, small_v)]

    def assemble(kind):
        bg = [o[kind][None] for o in big_out]
        g1, gm, g2, gq, gk, rel, lbp, go = small_out[kind]
        return [g1, bg[0], bg[1], bg[2], gm, bg[3], gq, gk, rel, lbp, go, bg[4], g2, bg[5], bg[6], bg[7]]

    return (loss, grad_x, *assemble(0), *assemble(1), *assemble(2), *assemble(3))
```

```python
import functools

import jax
import jax.numpy as jnp
from jax import lax
from jax.experimental import pallas as pl
from jax.experimental.pallas import tpu as pltpu

F32 = jnp.float32
BF16 = jnp.bfloat16
MESH = pl.DeviceIdType.MESH

N_CHIPS = 4
N_DEV = 8
CHUNK = 64
ATTN_HEADS = 8
ATTN_DH = 64
ATTN_W = ATTN_HEADS * ATTN_DH
HGRN_HEADS = 4
HGRN_DH = 128
HGRN_W = HGRN_HEADS * HGRN_DH
LEFT_CHUNKS = 8
BAND = (LEFT_CHUNKS + 1) * CHUNK
KPAD = LEFT_CHUNKS * CHUNK
REL_CLIP = 128
N_REL = 2 * REL_CLIP + 1
N_REL_PAD = 384
RMS_EPS = 1e-6
LANES = 128
SMALL_ROWS = 8
SMALL_COLS = 1024

ADAM_LR = 0.001
ADAM_B1 = 0.9
ADAM_B2 = 0.999
ADAM_EPS = 1e-08
ADAM_WD = 0.01
ADAM_STEP = 10

NN = (((1,), (0,)), ((), ()))
NT = (((1,), (1,)), ((), ()))
TN = (((0,), (0,)), ((), ()))

VMEM_LIMIT = 48 * 1024 * 1024


def _sigmoid(x):
    return 1.0 / (1.0 + jnp.exp(-x))


def _silu(x):
    return x * _sigmoid(x)


def _dsilu(x):
    s = _sigmoid(x)
    return s * (1.0 + x * (1.0 - s))


def _dot(a, b, dims=NN):
    return lax.dot_general(a, b, dims, preferred_element_type=F32)


def _split3(x):
    hi = x.astype(BF16)
    r1 = x - hi.astype(F32)
    mid = r1.astype(BF16)
    lo = (r1 - mid.astype(F32)).astype(BF16)
    return hi, mid, lo


def _dot_exact_rhs(x, mat, dims=NN):
    hi, mid, lo = _split3(x)
    return _dot(hi, mat, dims) + _dot(mid, mat, dims) + _dot(lo, mat, dims)


def _dot_exact_lhs(mat, x, dims=NN):
    hi, mid, lo = _split3(x)
    return _dot(mat, hi, dims) + _dot(mat, mid, dims) + _dot(mat, lo, dims)


def _params(*sem):
    return pltpu.CompilerParams(dimension_semantics=sem, vmem_limit_bytes=VMEM_LIMIT)


def _mm(name, ins, terms, n_acc, grid, acc_shape, outs, epilogue, extras=()):
    nk = grid[2]
    ni, ne, no = len(ins), len(extras), len(outs)

    def body(*refs):
        in_refs = refs[:ni]
        ex_refs = refs[ni:ni + ne]
        out_refs = refs[ni + ne:ni + ne + no]
        acc_refs = refs[ni + ne + no:]
        parts = [None] * n_acc
        for ai, li, ri, dims in terms:
            d = _dot(in_refs[li][...], in_refs[ri][...], dims)
            parts[ai] = d if parts[ai] is None else parts[ai] + d

        def finish(accs):
            res = epilogue(accs, [e[...] for e in ex_refs])
            for o, r in zip(out_refs, res):
                o[...] = r.astype(o.dtype)

        if nk == 1:
            finish(parts)
        else:
            k = pl.program_id(2)

            @pl.when(k == 0)
            def _():
                for a, p in zip(acc_refs, parts):
                    a[...] = p

            @pl.when(k > 0)
            def _():
                for a, p in zip(acc_refs, parts):
                    a[...] += p

            @pl.when(k == nk - 1)
            def _():
                finish([a[...] for a in acc_refs])

    scratch = [] if nk == 1 else [pltpu.VMEM(acc_shape, F32) for _ in range(n_acc)]
    res = pl.pallas_call(
        body,
        name=name,
        grid=grid,
        in_specs=[s for _, s in ins] + [s for _, s in extras],
        out_specs=[s for _, s in outs],
        out_shape=[o for o, _ in outs],
        scratch_shapes=scratch,
        compiler_params=_params("parallel", "parallel", "arbitrary"),
    )(*[a for a, _ in ins], *[a for a, _ in extras])
    return res


def _row_tile(t):
    return 512 if t % 512 == 0 else t


def _k_tile(t):
    return 1024 if t % 1024 == 0 else t


def _rmsnorm_fwd(name, x, g):
    t, d = x.shape
    tm = _row_tile(t)

    def body(x_ref, g_ref, h_ref):
        xv = x_ref[...]
        ms = jnp.mean(xv * xv, axis=-1, keepdims=True)
        h_ref[...] = (xv * lax.rsqrt(ms + RMS_EPS) * g_ref[...]).astype(BF16)

    return pl.pallas_call(
        body,
        name=name,
        grid=(t // tm,),
        in_specs=[pl.BlockSpec((tm, d), lambda i: (i, 0)), pl.BlockSpec((1, d), lambda i: (0, 0))],
        out_specs=pl.BlockSpec((tm, d), lambda i: (i, 0)),
        out_shape=jax.ShapeDtypeStruct((t, d), BF16),
        compiler_params=_params("parallel"),
    )(x, g)


def _norm_bwd_epilogue(copy_scale):
    def epilogue(accs, ex):
        dh = accs[0]
        xv, g, dres = ex
        ms = jnp.mean(xv * xv, axis=-1, keepdims=True)
        rstd = lax.rsqrt(ms + RMS_EPS)
        xhat = xv * rstd
        dxhat = dh * g
        dx = rstd * (dxhat - xhat * jnp.mean(dxhat * xhat, axis=-1, keepdims=True))
        out = dres + dx
        dg = jnp.sum(dh * xhat, axis=0, keepdims=True)
        if copy_scale is None:
            return out, dg
        return out, out * copy_scale, dg

    return epilogue


def _ffn_up(name, h, wg, wu):
    t, d = h.shape
    ns, _, f = wg.shape
    tm = _row_tile(t)

    def epilogue(accs, ex):
        a, b = accs
        return a, b, _silu(a) * b

    w_spec = pl.BlockSpec((None, d, f), lambda j, i, k: (j, 0, 0))
    o_spec = pl.BlockSpec((None, tm, f), lambda j, i, k: (j, i, 0))
    o_shape = jax.ShapeDtypeStruct((ns, t, f), BF16)
    return _mm(
        name,
        ins=[(h, pl.BlockSpec((tm, d), lambda j, i, k: (i, 0))), (wg, w_spec), (wu, w_spec)],
        terms=[(0, 0, 1, NN), (1, 0, 2, NN)],
        n_acc=2,
        grid=(ns, t // tm, 1),
        acc_shape=(tm, f),
        outs=[(o_shape, o_spec)] * 3,
        epilogue=epilogue,
    )


def _ffn_down(name, z, wd, x):
    ns, t, f = z.shape
    d = wd.shape[2]
    tm = _row_tile(t)
    row = pl.BlockSpec((tm, d), lambda i, n, k: (i, 0))
    return _mm(
        name,
        ins=[(z, pl.BlockSpec((None, tm, f), lambda i, n, k: (k, i, 0))),
             (wd, pl.BlockSpec((None, f, d), lambda i, n, k: (k, 0, 0)))],
        terms=[(0, 0, 1, NN)],
        n_acc=1,
        grid=(t // tm, 1, ns),
        acc_shape=(tm, d),
        outs=[(jax.ShapeDtypeStruct((t, d), F32), row)],
        epilogue=lambda accs, ex: (ex[0] + 0.5 * accs[0],),
        extras=[(x, row)],
    )[0]


def _ffn_down_loss(name, z, wd, x, target):
    ns, t, f = z.shape
    d = wd.shape[2]
    tm = _row_tile(t)
    nt = t // tm
    row = pl.BlockSpec((tm, d), lambda i, n, k: (i, 0))

    def epilogue(accs, ex):
        e = ex[0] + 0.5 * accs[0] - ex[1]
        dy = e * (1.0 / d)
        return dy, 0.5 * dy, jnp.sum(e * e, axis=0, keepdims=True)

    return _mm(
        name,
        ins=[(z, pl.BlockSpec((None, tm, f), lambda i, n, k: (k, i, 0))),
             (wd, pl.BlockSpec((None, f, d), lambda i, n, k: (k, 0, 0)))],
        terms=[(0, 0, 1, NN)],
        n_acc=1,
        grid=(nt, 1, ns),
        acc_shape=(tm, d),
        outs=[(jax.ShapeDtypeStruct((t, d), F32), row), (jax.ShapeDtypeStruct((t, d), BF16), row),
              (jax.ShapeDtypeStruct((nt, 1, d), F32), pl.BlockSpec((None, 1, d), lambda i, n, k: (i, 0, 0)))],
        epilogue=epilogue,
        extras=[(x, row), (target, row)],
    )


def _ffn_bwd_act(name, dout, wd, a, b):
    t, d = dout.shape
    ns, f, _ = wd.shape
    tm = _row_tile(t)

    def epilogue(accs, ex):
        dz = accs[0]
        av = ex[0].astype(F32)
        bv = ex[1].astype(F32)
        return dz * bv * _dsilu(av), dz * _silu(av)

    act = pl.BlockSpec((None, tm, f), lambda j, i, k: (j, i, 0))
    o_shape = jax.ShapeDtypeStruct((ns, t, f), BF16)
    return _mm(
        name,
        ins=[(dout, pl.BlockSpec((tm, d), lambda j, i, k: (i, 0))),
             (wd, pl.BlockSpec((None, f, d), lambda j, i, k: (j, 0, 0)))],
        terms=[(0, 0, 1, NT)],
        n_acc=1,
        grid=(ns, t // tm, 1),
        acc_shape=(tm, f),
        outs=[(o_shape, act)] * 2,
        epilogue=epilogue,
        extras=[(a, act), (b, act)],
    )


def _grad_w_shardcols(name, h, da):
    t, d = h.shape
    ns, _, f = da.shape
    tk = _k_tile(t)
    return _mm(
        name,
        ins=[(h, pl.BlockSpec((tk, d), lambda j, n, k: (k, 0))),
             (da, pl.BlockSpec((None, tk, f), lambda j, n, k: (j, k, 0)))],
        terms=[(0, 0, 1, TN)],
        n_acc=1,
        grid=(ns, 1, t // tk),
        acc_shape=(d, f),
        outs=[(jax.ShapeDtypeStruct((ns, d, f), F32), pl.BlockSpec((None, d, f), lambda j, n, k: (j, 0, 0)))],
        epilogue=lambda accs, ex: (accs[0],),
    )[0]


def _grad_w_shardrows(name, z, dout):
    ns, t, f = z.shape
    d = dout.shape[1]
    tk = _k_tile(t)
    return _mm(
        name,
        ins=[(z, pl.BlockSpec((None, tk, f), lambda j, n, k: (j, k, 0))),
             (dout, pl.BlockSpec((tk, d), lambda j, n, k: (k, 0)))],
        terms=[(0, 0, 1, TN)],
        n_acc=1,
        grid=(ns, 1, t // tk),
        acc_shape=(f, d),
        outs=[(jax.ShapeDtypeStruct((ns, f, d), F32), pl.BlockSpec((None, f, d), lambda j, n, k: (j, 0, 0)))],
        epilogue=lambda accs, ex: (accs[0],),
    )[0]


def _ffn_bwd_in(name, da, db, wg, wu, x, g, dres, copy_scale):
    ns, t, f = da.shape
    d = wg.shape[1]
    tm = _row_tile(t)
    nt = t // tm
    act = pl.BlockSpec((None, tm, f), lambda i, n, k: (k, i, 0))
    w_spec = pl.BlockSpec((None, d, f), lambda i, n, k: (k, 0, 0))
    row = pl.BlockSpec((tm, d), lambda i, n, k: (i, 0))
    outs = [(jax.ShapeDtypeStruct((t, d), F32), row)]
    if copy_scale is not None:
        outs.append((jax.ShapeDtypeStruct((t, d), BF16), row))
    outs.append((jax.ShapeDtypeStruct((nt, 1, d), F32), pl.BlockSpec((None, 1, d), lambda i, n, k: (i, 0, 0))))
    return _mm(
        name,
        ins=[(da, act), (db, act), (wg, w_spec), (wu, w_spec)],
        terms=[(0, 0, 2, NT), (0, 1, 3, NT)],
        n_acc=1,
        grid=(nt, 1, ns),
        acc_shape=(tm, d),
        outs=outs,
        epilogue=_norm_bwd_epilogue(copy_scale),
        extras=[(x, row), (g, pl.BlockSpec((1, d), lambda i, n, k: (0, 0))), (dres, row)],
    )


def _in_proj(name, h, w_in):
    t, d = h.shape
    ns, _, pj = w_in.shape
    tm = _row_tile(t)
    return _mm(
        name,
        ins=[(h, pl.BlockSpec((tm, d), lambda j, i, k: (i, 0))),
             (w_in, pl.BlockSpec((None, d, pj), lambda j, i, k: (j, 0, 0)))],
        terms=[(0, 0, 1, NN)],
        n_acc=1,
        grid=(ns, t // tm, 1),
        acc_shape=(tm, pj),
        outs=[(jax.ShapeDtypeStruct((t, ns * pj), F32), pl.BlockSpec((tm, pj), lambda j, i, k: (i, j)))],
        epilogue=lambda accs, ex: (accs[0],),
    )[0]


def _in_proj_bwd(name, dp, w_in, x, g, dres, copy_scale):
    t = dp.shape[0]
    ns, d, pj = w_in.shape
    tm = _row_tile(t)
    nt = t // tm
    row = pl.BlockSpec((tm, d), lambda i, n, k: (i, 0))
    outs = [(jax.ShapeDtypeStruct((t, d), F32), row)]
    if copy_scale is not None:
        outs.append((jax.ShapeDtypeStruct((t, d), BF16), row))
    outs.append((jax.ShapeDtypeStruct((nt, 1, d), F32), pl.BlockSpec((None, 1, d), lambda i, n, k: (i, 0, 0))))
    return _mm(
        name,
        ins=[(dp, pl.BlockSpec((tm, pj), lambda i, n, k: (i, k))),
             (w_in, pl.BlockSpec((None, d, pj), lambda i, n, k: (k, 0, 0)))],
        terms=[(0, 0, 1, NT)],
        n_acc=1,
        grid=(nt, 1, ns),
        acc_shape=(tm, d),
        outs=outs,
        epilogue=_norm_bwd_epilogue(copy_scale),
        extras=[(x, row), (g, pl.BlockSpec((1, d), lambda i, n, k: (0, 0))), (dres, row)],
    )


def _grad_w_in(name, h, dp, ns):
    t, d = h.shape
    pj = dp.shape[1] // ns
    tk = _k_tile(t)
    return _mm(
        name,
        ins=[(h, pl.BlockSpec((tk, d), lambda j, n, k: (k, 0))),
             (dp, pl.BlockSpec((tk, pj), lambda j, n, k: (k, j)))],
        terms=[(0, 0, 1, TN)],
        n_acc=1,
        grid=(ns, 1, t // tk),
        acc_shape=(d, pj),
        outs=[(jax.ShapeDtypeStruct((ns, d, pj), F32), pl.BlockSpec((None, d, pj), lambda j, n, k: (j, 0, 0)))],
        epilogue=lambda accs, ex: (accs[0],),
    )[0]


def _out_proj(name, mix, w_out, x):
    t, dm = mix.shape
    d = w_out.shape[1]
    tm = _row_tile(t)
    row = pl.BlockSpec((tm, d), lambda i, n, k: (i, 0))
    return _mm(
        name,
        ins=[(mix, pl.BlockSpec((tm, dm), lambda i, n, k: (i, 0))),
             (w_out, pl.BlockSpec((dm, d), lambda i, n, k: (0, 0)))],
        terms=[(0, 0, 1, NN)],
        n_acc=1,
        grid=(t // tm, 1, 1),
        acc_shape=(tm, d),
        outs=[(jax.ShapeDtypeStruct((t, d), F32), row)],
        epilogue=lambda accs, ex: (ex[0] + accs[0],),
        extras=[(x, row)],
    )[0]


def _out_proj_bwd(name, dx, w_out):
    t, d = dx.shape
    dm = w_out.shape[0]
    tm = _row_tile(t)
    return _mm(
        name,
        ins=[(dx, pl.BlockSpec((tm, d), lambda i, n, k: (i, 0))),
             (w_out, pl.BlockSpec((dm, d), lambda i, n, k: (0, 0)))],
        terms=[(0, 0, 1, NT)],
        n_acc=1,
        grid=(t // tm, 1, 1),
        acc_shape=(tm, dm),
        outs=[(jax.ShapeDtypeStruct((t, dm), F32), pl.BlockSpec((tm, dm), lambda i, n, k: (i, 0)))],
        epilogue=lambda accs, ex: (accs[0],),
    )[0]


def _grad_w_out(name, mix, dx):
    t, dm = mix.shape
    d = dx.shape[1]
    tk = _k_tile(t)
    return _mm(
        name,
        ins=[(mix, pl.BlockSpec((tk, dm), lambda a, n, k: (k, 0))),
             (dx, pl.BlockSpec((tk, d), lambda a, n, k: (k, 0)))],
        terms=[(0, 0, 1, TN)],
        n_acc=1,
        grid=(1, 1, t // tk),
        acc_shape=(dm, d),
        outs=[(jax.ShapeDtypeStruct((dm, d), F32), pl.BlockSpec((dm, d), lambda a, n, k: (0, 0)))],
        epilogue=lambda accs, ex: (accs[0],),
    )[0]


def _head_group_matrix():
    r = lax.broadcasted_iota(jnp.int32, (ATTN_W, ATTN_W), 0)
    c = lax.broadcasted_iota(jnp.int32, (ATTN_W, ATTN_W), 1)
    same = jnp.right_shift(r, 6) == jnp.right_shift(c, 6)
    return jnp.where(same, 1.0, 0.0).astype(BF16)


def _qk_prep(name, proj, gq, gk):
    b, s, _ = proj.shape
    tm = KPAD
    nb = s // tm

    def body(q_ref, k_ref, v_ref, gq_ref, gk_ref, qn_ref, kn_ref, vb_ref):
        j = pl.program_id(1)
        bd = _head_group_matrix()

        def norm(xv, g):
            ms = _dot_exact_rhs(xv * xv, bd) * (1.0 / ATTN_DH)
            return xv * lax.rsqrt(ms + RMS_EPS) * g

        @pl.when(j == 0)
        def _():
            kn_ref[...] = jnp.zeros_like(kn_ref)
            vb_ref[...] = jnp.zeros_like(vb_ref)

        @pl.when(j > 0)
        def _():
            qn_ref[...] = norm(q_ref[...], gq_ref[...]).astype(BF16)
            kn_ref[...] = norm(k_ref[...], gk_ref[...]).astype(BF16)
            vb_ref[...] = v_ref[...].astype(BF16)

    src_blk = lambda col: pl.BlockSpec((None, tm, ATTN_W), lambda bi, j: (bi, jnp.maximum(j - 1, 0), col))
    gspec = pl.BlockSpec((1, ATTN_W), lambda bi, j: (0, 0))
    padded = pl.BlockSpec((None, tm, ATTN_W), lambda bi, j: (bi, j, 0))
    return pl.pallas_call(
        body,
        name=name,
        grid=(b, nb + 1),
        in_specs=[src_blk(0), src_blk(1), src_blk(2), gspec, gspec],
        out_specs=[src_blk(0), padded, padded],
        out_shape=[jax.ShapeDtypeStruct((b, s, ATTN_W), BF16), jax.ShapeDtypeStruct((b, KPAD + s, ATTN_W), BF16),
                   jax.ShapeDtypeStruct((b, KPAD + s, ATTN_W), BF16)],
        compiler_params=_params("parallel", "arbitrary"),
    )(proj, proj, proj, gq, gk)


def _qk_prep_bwd(name, proj, dqn, dkn, dv, gq, gk):
    b, s, _ = proj.shape
    tm = KPAD
    nb = s // tm

    def body(q_ref, k_ref, dqn_ref, dkn_ref, dv_ref, gq_ref, gk_ref, dq_ref, dk_ref, dvb_ref, dgq_ref, dgk_ref):
        bd = _head_group_matrix()

        def bwd(xv, dy, g):
            ms = _dot_exact_rhs(xv * xv, bd) * (1.0 / ATTN_DH)
            rstd = lax.rsqrt(ms + RMS_EPS)
            xhat = xv * rstd
            dxhat = dy * g
            gm = _dot_exact_rhs(dxhat * xhat, bd) * (1.0 / ATTN_DH)
            return rstd * (dxhat - xhat * gm), jnp.sum(dy * xhat, axis=0, keepdims=True)

        dq, dgq = bwd(q_ref[...], dqn_ref[...], gq_ref[...])
        dk, dgk = bwd(k_ref[...], dkn_ref[...], gk_ref[...])
        dq_ref[...] = dq.astype(BF16)
        dk_ref[...] = dk.astype(BF16)
        dvb_ref[...] = dv_ref[...].astype(BF16)
        dgq_ref[...] = dgq
        dgk_ref[...] = dgk

    col = lambda c: pl.BlockSpec((None, tm, ATTN_W), lambda bi, j: (bi, j, c))
    past_pad = pl.BlockSpec((None, tm, ATTN_W), lambda bi, j: (bi, j + 1, 0))
    gspec = pl.BlockSpec((1, ATTN_W), lambda bi, j: (0, 0))
    pspec = pl.BlockSpec((None, 1, ATTN_W), lambda bi, j: (bi * nb + j, 0, 0))
    o_shape = jax.ShapeDtypeStruct((b, s, ATTN_W), BF16)
    p_shape = jax.ShapeDtypeStruct((b * nb, 1, ATTN_W), F32)
    return pl.pallas_call(
        body,
        name=name,
        grid=(b, nb),
        in_specs=[col(0), col(1), col(0), past_pad, past_pad, gspec, gspec],
        out_specs=[col(0)] * 3 + [pspec] * 2,
        out_shape=[o_shape] * 3 + [p_shape] * 2,
        compiler_params=_params("parallel", "parallel"),
    )(proj, proj, dqn, dkn, dv, gq, gk)


Q_CHUNKS = 4
QBLK = Q_CHUNKS * CHUNK
WIN = (LEFT_CHUNKS + Q_CHUNKS) * CHUNK
DB_W = BAND + CHUNK
MASKED = -1e30


def _band_table(bias):
    rows = [jnp.pad(bias, ((0, 0), (0, 0), (CHUNK * i, WIN - BAND - CHUNK * i)), constant_values=MASKED)
            for i in range(Q_CHUNKS)]
    return jnp.concatenate(rows, axis=1)


def _head_lanes(hh):
    lane = lax.broadcasted_iota(jnp.int32, (1, LANES), 1)
    return (lane < ATTN_DH) if hh == 0 else (lane >= ATTN_DH)


def _attn_probs(qh, kw, table, start):
    s = _dot(qh, kw, NT) * (ATTN_DH ** -0.5) + table
    col = lax.broadcasted_iota(jnp.int32, (QBLK, WIN), 1)
    s = jnp.where(col + start >= KPAD, s, MASKED)
    m = jnp.max(s, axis=-1, keepdims=True)
    p = jnp.exp(s - m)
    return p * (1.0 / jnp.sum(p, axis=-1, keepdims=True))


def _attn_fwd(name, q, k, v, table):
    b, s, w = q.shape
    sp = k.shape[1]

    def body(q_ref, k_ref, v_ref, t_ref, o_ref):
        start = pl.multiple_of(pl.program_id(2) * QBLK, QBLK)
        kw = k_ref[pl.ds(start, WIN), :]
        vw = v_ref[pl.ds(start, WIN), :]
        q2 = q_ref[...]
        out = jnp.zeros((QBLK, LANES), F32)
        for hh in range(2):
            mine = _head_lanes(hh)
            p = _attn_probs(jnp.where(mine, q2, jnp.zeros_like(q2)), kw, t_ref[hh], start)
            out = jnp.where(mine, _dot(p.astype(BF16), vw), out)
        o_ref[...] = out.astype(BF16)

    qspec = pl.BlockSpec((None, QBLK, LANES), lambda p, bi, i: (bi, i, p))
    kspec = pl.BlockSpec((None, sp, LANES), lambda p, bi, i: (bi, 0, p))
    return pl.pallas_call(
        body,
        name=name,
        grid=(w // LANES, b, s // QBLK),
        in_specs=[qspec, kspec, kspec, pl.BlockSpec((2, QBLK, WIN), lambda p, bi, i: (p, 0, 0))],
        out_specs=qspec,
        out_shape=jax.ShapeDtypeStruct((b, s, w), BF16),
        compiler_params=_params("parallel", "parallel", "arbitrary"),
    )(q, k, v, table)


def _attn_bwd(name, q, k, v, table, dmix):
    b, s, w = q.shape
    sp = k.shape[1]

    def body(q_ref, k_ref, v_ref, t_ref, do_ref, dq_ref, dk_ref, dv_ref, dbe_ref, dbo_ref):
        bi = pl.program_id(1)
        i = pl.program_id(2)
        start = pl.multiple_of(i * QBLK, QBLK)
        win = pl.ds(start, WIN)

        @pl.when(i == 0)
        def _():
            dk_ref[...] = jnp.zeros_like(dk_ref)
            dv_ref[...] = jnp.zeros_like(dv_ref)

        @pl.when(jnp.logical_and(i == 0, bi == 0))
        def _():
            dbe_ref[...] = jnp.zeros_like(dbe_ref)
            dbo_ref[...] = jnp.zeros_like(dbo_ref)

        kw = k_ref[win, :]
        vw = v_ref[win, :]
        q2 = q_ref[...]
        do2 = do_ref[...].astype(BF16)
        dq = jnp.zeros((QBLK, LANES), F32)
        for hh in range(2):
            mine = _head_lanes(hh)
            qh = jnp.where(mine, q2, jnp.zeros_like(q2))
            doh = jnp.where(mine, do2, jnp.zeros_like(do2))
            p = _attn_probs(qh, kw, t_ref[hh], start)
            dp = _dot(doh, vw, NT)
            ds = p * (dp - jnp.sum(p * dp, axis=-1, keepdims=True))
            for qi in range(Q_CHUNKS):
                c0 = (qi // 2) * LANES
                blk = ds[qi * CHUNK:(qi + 1) * CHUNK, c0:c0 + DB_W]
                if qi % 2 == 0:
                    dbe_ref[hh] += blk
                else:
                    dbo_ref[hh] += blk
            dsb = (ds * (ATTN_DH ** -0.5)).astype(BF16)
            dq = jnp.where(mine, _dot(dsb, kw), dq)
            dk_ref[win, :] += _dot(dsb, qh, TN)
            dv_ref[win, :] += _dot(p.astype(BF16), doh, TN)
        dq_ref[...] = dq

    qspec = pl.BlockSpec((None, QBLK, LANES), lambda p, bi, i: (bi, i, p))
    kspec = pl.BlockSpec((None, sp, LANES), lambda p, bi, i: (bi, 0, p))
    dbspec = pl.BlockSpec((2, CHUNK, DB_W), lambda p, bi, i: (p, 0, 0))
    db_shape = jax.ShapeDtypeStruct((ATTN_HEADS, CHUNK, DB_W), F32)
    return pl.pallas_call(
        body,
        name=name,
        grid=(w // LANES, b, s // QBLK),
        in_specs=[qspec, kspec, kspec, pl.BlockSpec((2, QBLK, WIN), lambda p, bi, i: (p, 0, 0)), qspec],
        out_specs=[qspec, kspec, kspec, dbspec, dbspec],
        out_shape=[jax.ShapeDtypeStruct((b, s, w), F32), jax.ShapeDtypeStruct((b, sp, w), F32),
                   jax.ShapeDtypeStruct((b, sp, w), F32), db_shape, db_shape],
        compiler_params=_params("arbitrary", "arbitrary", "arbitrary"),
    )(q, k, v, table, dmix)


HQ_COL = 3 * ATTN_W // HGRN_DH
HF_COL = HQ_COL + HGRN_HEADS
HI_COL = HF_COL + HGRN_HEADS
HG_COL = HI_COL + HGRN_HEADS


def _tri(lower):
    r = lax.broadcasted_iota(jnp.int32, (CHUNK, CHUNK), 0)
    c = lax.broadcasted_iota(jnp.int32, (CHUNK, CHUNK), 1)
    return (r >= c) if lower else (r <= c)


def _hgrn_chunk(hq, hf, lb, tril):
    sig = _sigmoid(hf)
    f = lb + (1.0 - lb) * sig
    g = jnp.log(f)
    ones_l = jnp.where(tril, 1.0, 0.0).astype(BF16)
    b = _dot_exact_lhs(ones_l, g)
    bl = jnp.sum(g, axis=0, keepdims=True)
    rows = lax.broadcasted_iota(jnp.int32, g.shape, 0)
    bm = jnp.sum(jnp.where(rows <= CHUNK // 2, g, 0.0), axis=0, keepdims=True)
    sq = _sigmoid(hq)
    q = hq * sq
    k = 1.0 - f
    return sig, f, b, bl, bm, sq, q, k


def _hgrn_fwd(name, proj, lb, go, b, s):
    nc = s // CHUNK
    t = b * s

    def body(hq_ref, hf_ref, hi_ref, hg_ref, lb_ref, go_ref, ro_ref, oraw_ref, st_ref, s_scr):
        tril = _tri(True)
        lbv = lb_ref[...]
        gov = go_ref[...]
        s_scr[...] = jnp.zeros_like(s_scr)

        def step(c, carry):
            sl = pl.ds(pl.multiple_of(c * CHUNK, CHUNK), CHUNK)
            hg = hg_ref[sl, :]
            _, _, bb, bl, bm, _, q, k = _hgrn_chunk(hq_ref[sl, :], hf_ref[sl, :], lbv, tril)
            vb = hi_ref[sl, :].astype(BF16)
            qe = (q * jnp.exp(bb - bm)).astype(BF16)
            ke = (k * jnp.exp(bm - bb)).astype(BF16)
            a = jnp.where(tril, _dot(qe, ke, NT), 0.0)
            st = s_scr[...]
            st_ref[c] = st
            qb = (q * jnp.exp(bb)).astype(BF16)
            o = _dot(a.astype(BF16), vb) + _dot(qb, st.astype(BF16), NT)
            kb = (k * jnp.exp(bl - bb)).astype(BF16)
            s_scr[...] = st * jnp.exp(bl) + _dot(vb, kb, TN)
            rstd = lax.rsqrt(jnp.mean(o * o, axis=-1, keepdims=True) + RMS_EPS)
            ro_ref[sl, :] = ((o * rstd * gov) * _silu(hg)).astype(BF16)
            oraw_ref[sl, :] = o
            return carry

        lax.fori_loop(0, nc, step, 0)

    col = lambda base: pl.BlockSpec((s, HGRN_DH), lambda bi, h: (bi, base + h))
    vec = pl.BlockSpec((1, HGRN_DH), lambda bi, h: (0, h))
    out = pl.BlockSpec((s, HGRN_DH), lambda bi, h: (bi, h))
    return pl.pallas_call(
        body,
        name=name,
        grid=(b, HGRN_HEADS),
        in_specs=[col(HQ_COL), col(HF_COL), col(HI_COL), col(HG_COL), vec,
                  pl.BlockSpec((1, HGRN_DH), lambda bi, h: (0, 0))],
        out_specs=[out, out,
                   pl.BlockSpec((None, None, nc, HGRN_DH, HGRN_DH), lambda bi, h: (bi, h, 0, 0, 0))],
        out_shape=[jax.ShapeDtypeStruct((t, HGRN_W), BF16), jax.ShapeDtypeStruct((t, HGRN_W), F32),
                   jax.ShapeDtypeStruct((b, HGRN_HEADS, nc, HGRN_DH, HGRN_DH), F32)],
        scratch_shapes=[pltpu.VMEM((HGRN_DH, HGRN_DH), F32)],
        compiler_params=_params("parallel", "parallel"),
    )(proj, proj, proj, proj, lb, go)


def _hgrn_bwd(name, proj, lb, go, oraw, states, dmix, b, s):
    nc = s // CHUNK
    t = b * s

    def body(hq_ref, hf_ref, hi_ref, hg_ref, lb_ref, go_ref, oraw_ref, st_ref, dro_ref,
             dhq_ref, dhf_ref, dhi_ref, dhg_ref, dlb_ref, dgo_ref, ds_scr, dlb_scr, dgo_scr):
        tril = _tri(True)
        ones_u = jnp.where(_tri(False), 1.0, 0.0).astype(BF16)
        lbv = lb_ref[...]
        gov = go_ref[...]
        ds_scr[...] = jnp.zeros_like(ds_scr)
        dlb_scr[...] = jnp.zeros_like(dlb_scr)
        dgo_scr[...] = jnp.zeros_like(dgo_scr)

        def step(ci, carry):
            c = nc - 1 - ci
            sl = pl.ds(pl.multiple_of(c * CHUNK, CHUNK), CHUNK)
            hq = hq_ref[sl, :]
            hg = hg_ref[sl, :]
            sig, f, bb, bl, bm, sq, q, k = _hgrn_chunk(hq, hf_ref[sl, :], lbv, tril)
            vb = hi_ref[sl, :].astype(BF16)
            ebm = jnp.exp(bb - bm)
            embm = jnp.exp(bm - bb)
            eb = jnp.exp(bb)
            ebl = jnp.exp(bl - bb)
            e_last = jnp.exp(bl)
            qe = (q * ebm).astype(BF16)
            ke = (k * embm).astype(BF16)
            qb = (q * eb).astype(BF16)
            kb = (k * ebl).astype(BF16)
            a = jnp.where(tril, _dot(qe, ke, NT), 0.0)
            st = st_ref[c]
            dst = ds_scr[...]
            o = oraw_ref[sl, :]
            dro = dro_ref[sl, :]
            sg = _sigmoid(hg)
            rstd = lax.rsqrt(jnp.mean(o * o, axis=-1, keepdims=True) + RMS_EPS)
            ohat = o * rstd
            dn = dro * (hg * sg)
            dhg_ref[sl, :] = (dro * (ohat * gov) * (sg * (1.0 + hg * (1.0 - sg)))).astype(BF16)
            dgo_scr[...] += jnp.sum(dn * ohat, axis=0, keepdims=True)
            dohat = dn * gov
            do = rstd * (dohat - ohat * jnp.mean(dohat * ohat, axis=-1, keepdims=True))
            dob = do.astype(BF16)
            dab = jnp.where(tril, _dot(dob, vb, NT), 0.0).astype(BF16)
            stb = st.astype(BF16)
            dstb = dst.astype(BF16)
            dv = _dot(a.astype(BF16), dob, TN) + _dot(kb, dstb, NT)
            dqe = _dot(dab, ke)
            dke = _dot(dab, qe, TN)
            dqb = _dot(dob, stb)
            dkb = _dot(vb, dstb)
            dq = dqe * ebm + dqb * eb
            dk = dke * embm + dkb * ebl
            db = (qe.astype(F32) * dqe - ke.astype(F32) * dke) + q * (dqb * eb) - k * (dkb * ebl)
            d_last = (jnp.sum(k * ebl * dkb, axis=0, keepdims=True)
                      + jnp.sum(dst * st, axis=0, keepdims=True) * e_last)
            dg = _dot_exact_lhs(ones_u, db) + d_last
            df = dg / f - dk
            dhf_ref[sl, :] = (df * (1.0 - lbv) * sig * (1.0 - sig)).astype(BF16)
            dlb_scr[...] += jnp.sum(df * (1.0 - sig), axis=0, keepdims=True)
            dhq_ref[sl, :] = (dq * (sq * (1.0 + hq * (1.0 - sq)))).astype(BF16)
            dhi_ref[sl, :] = dv.astype(BF16)
            ds_scr[...] = dst * e_last + _dot(dob, qb, TN)
            return carry

        lax.fori_loop(0, nc, step, 0)
        dlb_ref[...] = dlb_scr[...]
        dgo_ref[...] = dgo_scr[...]

    col = lambda base: pl.BlockSpec((s, HGRN_DH), lambda bi, h: (bi, base + h))
    vec = pl.BlockSpec((1, HGRN_DH), lambda bi, h: (0, h))
    out = pl.BlockSpec((s, HGRN_DH), lambda bi, h: (bi, h))
    part = pl.BlockSpec((None, 1, HGRN_DH), lambda bi, h: (bi, 0, h))
    o_shape = jax.ShapeDtypeStruct((t, HGRN_W), BF16)
    p_shape = jax.ShapeDtypeStruct((b, 1, HGRN_W), F32)
    return pl.pallas_call(
        body,
        name=name,
        grid=(b, HGRN_HEADS),
        in_specs=[col(HQ_COL), col(HF_COL), col(HI_COL), col(HG_COL), vec,
                  pl.BlockSpec((1, HGRN_DH), lambda bi, h: (0, 0)), out,
                  pl.BlockSpec((None, None, nc, HGRN_DH, HGRN_DH), lambda bi, h: (bi, h, 0, 0, 0)),
                  col(ATTN_W // HGRN_DH)],
        out_specs=[out] * 4 + [part] * 2,
        out_shape=[o_shape] * 4 + [p_shape] * 2,
        scratch_shapes=[pltpu.VMEM((HGRN_DH, HGRN_DH), F32), pltpu.VMEM((1, HGRN_DH), F32),
                        pltpu.VMEM((1, HGRN_DH), F32)],
        compiler_params=_params("parallel", "parallel"),
    )(proj, proj, proj, proj, lb, go, oraw, states, dmix)


def _small_grads(name, dg1, dgm, dg2, dgq, dgk, dbias_t, dlb, dgo, lbp):
    d = dg1.shape[1]

    def body(dg1_ref, dgm_ref, dg2_ref, dgq_ref, dgk_ref, dbias_ref, dlb_ref, dgo_ref, lbp_ref,
             g1_ref, gm_ref, g2_ref, gq_ref, gk_ref, rb_ref, lbg_ref, go_ref):
        g1_ref[...] = jnp.sum(dg1_ref[...], axis=0, keepdims=True)
        gm_ref[...] = jnp.sum(dgm_ref[...], axis=0, keepdims=True)
        g2_ref[...] = jnp.sum(dg2_ref[...], axis=0, keepdims=True)
        r = lax.broadcasted_iota(jnp.int32, (ATTN_W, ATTN_DH), 0)
        cidx = lax.broadcasted_iota(jnp.int32, (ATTN_W, ATTN_DH), 1)
        fold = jnp.where(jnp.bitwise_and(r, ATTN_DH - 1) == cidx, 1.0, 0.0).astype(BF16)
        gq_ref[...] = jnp.sum(_dot_exact_rhs(dgq_ref[...], fold), axis=0, keepdims=True)
        gk_ref[...] = jnp.sum(_dot_exact_rhs(dgk_ref[...], fold), axis=0, keepdims=True)
        gosum = jnp.sum(dgo_ref[...], axis=0, keepdims=True)
        go_ref[...] = (gosum[:, 0:HGRN_DH] + gosum[:, HGRN_DH:2 * HGRN_DH]
                       + gosum[:, 2 * HGRN_DH:3 * HGRN_DH] + gosum[:, 3 * HGRN_DH:4 * HGRN_DH])
        p0 = lbp_ref[0:1, :]
        p1 = lbp_ref[1:2, :]
        lbv = 1.0 / (1.0 + jnp.exp(p1 - p0))
        dp0 = jnp.sum(dlb_ref[...], axis=0, keepdims=True) * lbv * (1.0 - lbv)
        lbg_ref[0:1, :] = dp0
        lbg_ref[1:2, :] = -dp0
        sidx = lax.broadcasted_iota(jnp.int32, (BAND, N_REL_PAD), 0)
        ridx = lax.broadcasted_iota(jnp.int32, (BAND, N_REL_PAD), 1)

        def step(tq, acc):
            rel = jnp.clip(tq + KPAD - sidx, -REL_CLIP, REL_CLIP) + REL_CLIP
            onehot = jnp.where(rel == ridx, 1.0, 0.0).astype(BF16)
            return acc + _dot_exact_rhs(dbias_ref[tq], onehot)

        rb_ref[...] = lax.fori_loop(0, CHUNK, step, jnp.zeros((ATTN_HEADS, N_REL_PAD), F32))

    ins = [dg1, dgm, dg2, dgq, dgk, dbias_t, dlb, dgo, lbp]
    outs = [jax.ShapeDtypeStruct((1, d), F32)] * 3 + [jax.ShapeDtypeStruct((1, ATTN_DH), F32)] * 2 + [
        jax.ShapeDtypeStruct((ATTN_HEADS, N_REL_PAD), F32), jax.ShapeDtypeStruct((2, HGRN_W), F32),
        jax.ShapeDtypeStruct((1, HGRN_DH), F32)]
    vm = pl.BlockSpec(memory_space=pltpu.VMEM)
    return pl.pallas_call(
        body,
        name=name,
        in_specs=[vm] * len(ins),
        out_specs=[vm] * len(outs),
        out_shape=outs,
        compiler_params=pltpu.CompilerParams(vmem_limit_bytes=VMEM_LIMIT),
    )(*ins)


def _adam_update(w, g, m, v):
    m2 = ADAM_B1 * m + (1.0 - ADAM_B1) * g
    v2 = ADAM_B2 * v + (1.0 - ADAM_B2) * (g * g)
    m_hat = m2 / (1.0 - ADAM_B1 ** ADAM_STEP)
    v_hat = v2 / (1.0 - ADAM_B2 ** ADAM_STEP)
    delta = -ADAM_LR * (m_hat / (jnp.sqrt(v_hat) + ADAM_EPS) + ADAM_WD * w)
    return delta, m2, v2


def _rows_tile(r):
    for cand in (256, 352, 128, 176, 64, 32, 16):
        if r % cand == 0 and r > cand:
            return cand
    return r


def _pair_sum(name, grad, theirs, core):
    n, half, c = theirs.shape
    tr = _rows_tile(half)
    nth = half // tr

    def body(core_ref, a_ref, b_ref, o_ref):
        o_ref[...] = (a_ref[...] + b_ref[...]).astype(o_ref.dtype)

    spec = pl.BlockSpec((None, tr, c), lambda i, j, core_ref: (i, j, 0))
    return pl.pallas_call(
        body, name=name,
        grid_spec=pltpu.PrefetchScalarGridSpec(
            num_scalar_prefetch=1, grid=(n, nth),
            in_specs=[pl.BlockSpec((None, tr, c), lambda i, j, core_ref: (i, core_ref[0] * nth + j, 0)), spec],
            out_specs=spec),
        out_shape=jax.ShapeDtypeStruct((n, half, c), BF16), compiler_params=_params("parallel", "parallel"),
    )(core, grad, theirs)


def _chip_sum(name, own, parts, chip):
    _, half, c = own.shape
    tr = _rows_tile(half)

    def body(chip_ref, own_ref, p_ref, o_ref):
        me = chip_ref[0]
        mine = own_ref[...].astype(F32)
        flip_x, flip_y, flip_xy = (p_ref[i].astype(F32) for i in range(3))
        acc = None
        for k in range(N_CHIPS):
            rel = jnp.bitwise_xor(me, k)
            term = jnp.where(rel == 0, mine, jnp.where(rel == 2, flip_x, jnp.where(rel == 1, flip_y, flip_xy)))
            acc = term if acc is None else acc + term
        o_ref[...] = acc

    return pl.pallas_call(
        body, name=name,
        grid_spec=pltpu.PrefetchScalarGridSpec(
            num_scalar_prefetch=1, grid=(half // tr,),
            in_specs=[pl.BlockSpec((None, tr, c), lambda j, chip_ref: (chip_ref[0], j, 0)),
                      pl.BlockSpec((3, tr, c), lambda j, chip_ref: (0, j, 0))],
            out_specs=pl.BlockSpec((tr, c), lambda j, chip_ref: (j, 0))),
        out_shape=jax.ShapeDtypeStruct((half, c), F32), compiler_params=_params("parallel"),
    )(chip, own, parts)


def _adamw(name, w, g_mine, g_theirs, m, v, core):
    r, c = w.shape
    half = r // 2
    tr = _rows_tile(half)
    nth = half // tr

    def body(core_ref, w_ref, gm_ref, gt_ref, m_ref, v_ref, g_ref, d_ref, m2_ref, v2_ref):
        g = jnp.where(pl.program_id(0) == core_ref[0], gm_ref[...], gt_ref[...])
        delta, m2, v2 = _adam_update(w_ref[...], g, m_ref[...], v_ref[...])
        g_ref[...] = g
        d_ref[...] = delta
        m2_ref[...] = m2
        v2_ref[...] = v2

    full = pl.BlockSpec((tr, c), lambda h, j, core_ref: (h * nth + j, 0))
    part = pl.BlockSpec((tr, c), lambda h, j, core_ref: (j, 0))
    shape = jax.ShapeDtypeStruct((r, c), F32)
    return pl.pallas_call(
        body, name=name,
        grid_spec=pltpu.PrefetchScalarGridSpec(
            num_scalar_prefetch=1, grid=(2, nth), in_specs=[full, part, part, full, full], out_specs=[full] * 4),
        out_shape=[shape] * 4, compiler_params=_params("parallel", "parallel"),
    )(core, w, g_mine, g_theirs, m, v)


def _rel_bias_table(name, rel_bias):
    padded = jnp.pad(rel_bias, ((0, 0), (0, N_REL_PAD - N_REL)))

    def body(rb_ref, o_ref):
        ridx = lax.broadcasted_iota(jnp.int32, (N_REL_PAD, BAND), 0)
        sidx = lax.broadcasted_iota(jnp.int32, (N_REL_PAD, BAND), 1)
        rb = rb_ref[...]

        def step(tq, carry):
            rel = jnp.clip(tq + KPAD - sidx, -REL_CLIP, REL_CLIP) + REL_CLIP
            onehot = jnp.where(rel == ridx, 1.0, 0.0).astype(BF16)
            o_ref[tq] = _dot_exact_rhs(rb, onehot)
            return carry

        lax.fori_loop(0, CHUNK, step, 0)

    vm = pl.BlockSpec(memory_space=pltpu.VMEM)
    table = pl.pallas_call(
        body, name=name, in_specs=[vm], out_specs=vm,
        out_shape=jax.ShapeDtypeStruct((CHUNK, ATTN_HEADS, BAND), F32),
    )(padded)
    return table.transpose(1, 0, 2)


def _adamw_small(name, w, parts, m, v):
    def body(w_ref, p_ref, m_ref, v_ref, g_ref, d_ref, m2_ref, v2_ref):
        g = p_ref[0]
        for i in range(1, N_DEV):
            g = g + p_ref[i]
        delta, m2, v2 = _adam_update(w_ref[...], g, m_ref[...], v_ref[...])
        g_ref[...] = g
        d_ref[...] = delta
        m2_ref[...] = m2
        v2_ref[...] = v2

    vm = pl.BlockSpec(memory_space=pltpu.VMEM)
    shape = jax.ShapeDtypeStruct((SMALL_ROWS, SMALL_COLS), F32)
    return pl.pallas_call(
        body, name=name, in_specs=[vm] * 4, out_specs=[vm] * 4, out_shape=[shape] * 4,
    )(w, parts, m, v)


def _position():
    return lax.axis_index("x"), lax.axis_index("y"), lax.axis_index("c")


def _other_chips(x, y):
    return [(1 - x, y), (x, 1 - y), (1 - x, 1 - y)]


ANY = pl.BlockSpec(memory_space=pl.ANY)


def _gather_weights(shards):
    n = len(shards)

    def body(*refs):
        ins, outs = refs[:n], refs[n:2 * n]
        ici_send, ici_recv, fwd_send, fwd_recv, own_send, own_recv = refs[2 * n:]
        x, y, c = _position()
        me = 2 * x + y
        chips = _other_chips(x, y)

        def own(i):
            return pltpu.make_async_remote_copy(
                src_ref=ins[i], dst_ref=outs[i].at[me], send_sem=own_send.at[i], recv_sem=own_recv.at[i],
                device_id=(x, y, 1 - c), device_id_type=MESH)

        def push(i, j):
            return pltpu.make_async_remote_copy(
                src_ref=ins[i], dst_ref=outs[i].at[me], send_sem=ici_send.at[3 * i + j], recv_sem=ici_recv.at[3 * i + j],
                device_id=(chips[j][0], chips[j][1], 1), device_id_type=MESH)

        def arrival(i, j):
            return pltpu.make_async_remote_copy(
                src_ref=ins[i], dst_ref=outs[i].at[2 * chips[j][0] + chips[j][1]],
                send_sem=ici_send.at[3 * i + j], recv_sem=ici_recv.at[3 * i + j],
                device_id=(chips[j][0], chips[j][1], 1), device_id_type=MESH)

        def onward(i, j):
            slot = outs[i].at[2 * chips[j][0] + chips[j][1]]
            return pltpu.make_async_remote_copy(
                src_ref=slot, dst_ref=slot, send_sem=fwd_send.at[3 * i + j], recv_sem=fwd_recv.at[3 * i + j],
                device_id=(x, y, 1 - c), device_id_type=MESH)

        @pl.when(c == 1)
        def _north():
            for i in range(n):
                for j in range(3):
                    push(i, j).start()

        for i in range(n):
            own(i).start()

        @pl.when(c == 1)
        def _north_forward():
            for i in range(n):
                for j in range(3):
                    arrival(i, j).wait_recv()
                    onward(i, j).start()
            for i in range(n):
                for j in range(3):
                    push(i, j).wait_send()
                    onward(i, j).wait_send()

        @pl.when(c == 0)
        def _south():
            for i in range(n):
                for j in range(3):
                    onward(i, j).wait_recv()

        for i in range(n):
            own(i).wait()

    return pl.pallas_call(
        body,
        name="gather_weights",
        in_specs=[ANY] * n,
        out_specs=[ANY] * n,
        out_shape=[jax.ShapeDtypeStruct((N_CHIPS,) + s.shape, s.dtype) for s in shards],
        scratch_shapes=[pltpu.SemaphoreType.DMA((3 * n,))] * 4 + [pltpu.SemaphoreType.DMA((n,))] * 2,
    )(*shards)


def _pair_exchange(grads):
    n = len(grads)

    def body(*refs):
        ins, theirs = refs[:n], refs[n:2 * n]
        send_sem, recv_sem = refs[2 * n:]
        x, y, c = _position()
        copies = []
        for i in range(n):
            half = ins[i].shape[1] // 2
            give = pl.ds(pl.multiple_of((1 - c) * half, 8), half)
            swap = pltpu.make_async_remote_copy(
                src_ref=ins[i].at[:, give, :], dst_ref=theirs[i], send_sem=send_sem.at[i], recv_sem=recv_sem.at[i],
                device_id=(x, y, 1 - c), device_id_type=MESH)
            swap.start()
            copies.append(swap)
        for swap in copies:
            swap.wait()

    return pl.pallas_call(
        body,
        name="pair_exchange",
        in_specs=[ANY] * n,
        out_specs=[ANY] * n,
        out_shape=[jax.ShapeDtypeStruct((g.shape[0], g.shape[1] // 2, g.shape[2]), g.dtype) for g in grads],
        scratch_shapes=[pltpu.SemaphoreType.DMA((n,))] * 2,
    )(*grads)


def _chip_scatter(sums):
    n = len(sums)

    def body(*refs):
        ins, outs = refs[:n], refs[n:2 * n]
        send_sem, recv_sem = refs[2 * n:]
        x, y, c = _position()
        chips = _other_chips(x, y)

        def push(i, j):
            return pltpu.make_async_remote_copy(
                src_ref=ins[i].at[2 * chips[j][0] + chips[j][1]], dst_ref=outs[i].at[j],
                send_sem=send_sem.at[3 * i + j], recv_sem=recv_sem.at[3 * i + j],
                device_id=(chips[j][0], chips[j][1], c), device_id_type=MESH)

        for i in range(n):
            for j in range(3):
                push(i, j).start()
        for i in range(n):
            for j in range(3):
                push(i, j).wait()

    return pl.pallas_call(
        body,
        name="chip_scatter",
        in_specs=[ANY] * n,
        out_specs=[ANY] * n,
        out_shape=[jax.ShapeDtypeStruct((3,) + s.shape[1:], s.dtype) for s in sums],
        scratch_shapes=[pltpu.SemaphoreType.DMA((3 * n,))] * 2,
    )(*sums)


def _pair_join(halves, small):
    n = len(halves)

    def body(*refs):
        ins, small_ref = refs[:n], refs[n]
        outs, all_ref = refs[n + 1:2 * n + 1], refs[2 * n + 1]
        send_sem, recv_sem, sm_send, sm_recv, sm_local = refs[2 * n + 2:]
        x, y, c = _position()
        swaps = []
        for i in range(n):
            swap = pltpu.make_async_remote_copy(
                src_ref=ins[i], dst_ref=outs[i], send_sem=send_sem.at[i], recv_sem=recv_sem.at[i],
                device_id=(x, y, 1 - c), device_id_type=MESH)
            swap.start()
            swaps.append(swap)
        me = 4 * x + 2 * y + c
        sm_own = pltpu.make_async_copy(small_ref, all_ref.at[me], sm_local)
        sm_own.start()
        pushes, arrivals = [], []
        for mask in range(1, N_DEV):
            px, py, pc = x ^ (mask >> 2), y ^ ((mask >> 1) & 1), c ^ (mask & 1)
            pushes.append(pltpu.make_async_remote_copy(
                src_ref=small_ref, dst_ref=all_ref.at[me], send_sem=sm_send.at[mask - 1], recv_sem=sm_recv.at[mask - 1],
                device_id=(px, py, pc), device_id_type=MESH))
            arrivals.append(pltpu.make_async_remote_copy(
                src_ref=small_ref, dst_ref=all_ref.at[4 * px + 2 * py + pc], send_sem=sm_send.at[mask - 1],
                recv_sem=sm_recv.at[mask - 1], device_id=(px, py, pc), device_id_type=MESH))
        for cp in pushes:
            cp.start()
        for swap in swaps:
            swap.wait()
        for cp in arrivals:
            cp.wait_recv()
        for cp in pushes:
            cp.wait_send()
        sm_own.wait()

    res = pl.pallas_call(
        body,
        name="pair_join",
        in_specs=[ANY] * (n + 1),
        out_specs=[ANY] * (n + 1),
        out_shape=[jax.ShapeDtypeStruct(h.shape, h.dtype) for h in halves]
        + [jax.ShapeDtypeStruct((N_DEV,) + small.shape, small.dtype)],
        scratch_shapes=[pltpu.SemaphoreType.DMA((n,))] * 2 + [pltpu.SemaphoreType.DMA((N_DEV - 1,))] * 2
        + [pltpu.SemaphoreType.DMA(())],
    )(*halves, small)
    return res[:n], res[n]


def _lower_bound(lbp):
    return jax.nn.softmax(lbp, axis=0)[0:1]


def _local_step(x, target, g1, gm, g2, gq, gk, go, rel_bias, lbp, wg1, wu1, wd1, w_in, w_out, wg2, wu2, wd2):
    b, s, d = x.shape
    t = b * s
    ns = w_in.shape[0]
    x0 = x.reshape(t, d)
    tgt = target.reshape(t, d)
    gq_t = jnp.tile(gq, (1, ATTN_HEADS))
    gk_t = jnp.tile(gk, (1, ATTN_HEADS))
    lb = _lower_bound(lbp)
    bias = _rel_bias_table("rel_bias_table", rel_bias)

    h1 = _rmsnorm_fwd("norm1", x0, g1)
    a1, b1, z1 = _ffn_up("ffn1_up", h1, wg1, wu1)
    x1 = _ffn_down("ffn1_down", z1, wd1, x0)
    h2 = _rmsnorm_fwd("norm_mix", x1, gm)
    proj = _in_proj("in_proj", h2, w_in)
    proj3 = proj.reshape(b, s, proj.shape[1])
    table = _band_table(bias)
    qn, kn, vb = _qk_prep("qk_prep", proj3, gq_t, gk_t)
    attn = _attn_fwd("attn_fwd", qn, kn, vb, table).reshape(t, ATTN_W)
    ro, oraw, states = _hgrn_fwd("hgrn_fwd", proj, lb, go, b, s)
    mix = jnp.concatenate([attn, ro], axis=1)
    x2 = _out_proj("out_proj", mix, w_out, x1)
    h3 = _rmsnorm_fwd("norm2", x2, g2)
    a2, b2, z2 = _ffn_up("ffn2_up", h3, wg2, wu2)
    dy, dyh, sq = _ffn_down_loss("ffn2_down_loss", z2, wd2, x2, tgt)
    loss = 0.5 * jnp.sum(sq) / d

    da2, db2 = _ffn_bwd_act("ffn2_bwd_act", dyh, wd2, a2, b2)
    dwd2 = _grad_w_shardrows("ffn2_dwd", z2, dyh)
    dwg2 = _grad_w_shardcols("ffn2_dwg", h3, da2)
    dwu2 = _grad_w_shardcols("ffn2_dwu", h3, db2)
    dx2, dx2b, dg2 = _ffn_bwd_in("ffn2_bwd_in", da2, db2, wg2, wu2, x2, g2, dy, 1.0)

    dwout = _grad_w_out("dw_out", mix, dx2b)
    dmix = _out_proj_bwd("out_proj_bwd", dx2b, w_out)
    dqn, dkn, dvn, dbe, dbo = _attn_bwd("attn_bwd", qn, kn, vb, table, dmix.reshape(b, s, dmix.shape[1]))
    dbias = dbe[:, :, :BAND] + dbo[:, :, CHUNK:]
    dpq, dpk, dpv, dgq, dgk = _qk_prep_bwd("qk_prep_bwd", proj3, dqn, dkn, dvn, gq_t, gk_t)
    dpq, dpk, dpv = (a.reshape(t, ATTN_W) for a in (dpq, dpk, dpv))
    dhq, dhf, dhi, dhg, dlb, dgo = _hgrn_bwd("hgrn_bwd", proj, lb, go, oraw, states, dmix, b, s)
    dproj = jnp.concatenate([dpq, dpk, dpv, dhq, dhf, dhi, dhg], axis=1)
    dwin = _grad_w_in("dw_in", h2, dproj, ns)
    dx1, dx1h, dgm = _in_proj_bwd("in_proj_bwd", dproj, w_in, x1, gm, dx2, 0.5)

    da1, db1 = _ffn_bwd_act("ffn1_bwd_act", dx1h, wd1, a1, b1)
    dwd1 = _grad_w_shardrows("ffn1_dwd", z1, dx1h)
    dwg1 = _grad_w_shardcols("ffn1_dwg", h1, da1)
    dwu1 = _grad_w_shardcols("ffn1_dwu", h1, db1)
    dx0, dg1 = _ffn_bwd_in("ffn1_bwd_in", da1, db1, wg1, wu1, x0, g1, dx1, None)

    nt = dg1.shape[0]
    sg = _small_grads(
        "small_grads", dg1.reshape(nt, d), dgm.reshape(nt, d), dg2.reshape(nt, d),
        dgq.reshape(-1, ATTN_W), dgk.reshape(-1, ATTN_W), dbias.transpose(1, 0, 2),
        dlb.reshape(b, HGRN_W), dgo.reshape(b, HGRN_W), lbp)
    g1g, gmg, g2g, gqg, gkg, rbg, lbg, gog = sg
    small = _pack_small(g1g, gmg, g2g, lbg, rbg[:, :N_REL], gqg, gkg, gog)
    big = [dwg1, dwu1, dwd1, dwin, dwout.reshape(ns, dwout.shape[0] // ns, d), dwg2, dwu2, dwd2]
    return loss, dx0.reshape(b, s, d), big, small


def _pack_small(g1, gm, g2, lbp, rel_bias, gq, gk, go):
    flat = [g1.reshape(-1), gm.reshape(-1), g2.reshape(-1), lbp.reshape(-1), rel_bias.reshape(-1)]
    n_bias = 3 * SMALL_COLS - rel_bias.size
    heads = [gq.reshape(-1), gk.reshape(-1), go.reshape(-1)]
    n_tail = SMALL_COLS - sum(h.size for h in heads)
    return jnp.concatenate(flat + [jnp.zeros((n_bias,), F32)] + heads + [jnp.zeros((n_tail,), F32)]).reshape(
        SMALL_ROWS, SMALL_COLS)


def _unpack_small(p, d):
    flat = p.reshape(-1)
    o = 3 * d
    g1, gm, g2 = p[0:1], p[1:2], p[2:3]
    lbp = flat[o:o + 2 * HGRN_W].reshape(2, HGRN_W)
    o = 4 * SMALL_COLS
    rel = flat[o:o + ATTN_HEADS * N_REL].reshape(1, ATTN_HEADS, N_REL)
    o = 7 * SMALL_COLS
    gq = flat[o:o + ATTN_DH].reshape(1, ATTN_DH)
    gk = flat[o + ATTN_DH:o + 2 * ATTN_DH].reshape(1, ATTN_DH)
    go = flat[o + 2 * ATTN_DH:o + 2 * ATTN_DH + HGRN_DH].reshape(1, HGRN_DH)
    return g1, gm, g2, gq, gk, rel, lbp, go


def kernel(x, ffn1_norm_g, ffn1_w_gate, ffn1_w_up, ffn1_w_down, mix_norm_g, w_in, attn_q_norm_g, attn_k_norm_g, attn_rel_bias, hgrn_lower_bounds, hgrn_out_norm_g, w_out, ffn2_norm_g, ffn2_w_gate, ffn2_w_up, ffn2_w_down, loss_target, m_ffn1_norm_g, m_ffn1_w_gate, m_ffn1_w_up, m_ffn1_w_down, m_mix_norm_g, m_w_in, m_attn_q_norm_g, m_attn_k_norm_g, m_attn_rel_bias, m_hgrn_lower_bounds, m_hgrn_out_norm_g, m_w_out, m_ffn2_norm_g, m_ffn2_w_gate, m_ffn2_w_up, m_ffn2_w_down, v_ffn1_norm_g, v_ffn1_w_gate, v_ffn1_w_up, v_ffn1_w_down, v_mix_norm_g, v_w_in, v_attn_q_norm_g, v_attn_k_norm_g, v_attn_rel_bias, v_hgrn_lower_bounds, v_hgrn_out_norm_g, v_w_out, v_ffn2_norm_g, v_ffn2_w_gate, v_ffn2_w_up, v_ffn2_w_down):
    d = x.shape[-1]
    big_w = [ffn1_w_gate, ffn1_w_up, ffn1_w_down, w_in, w_out, ffn2_w_gate, ffn2_w_up, ffn2_w_down]
    big_m = [m_ffn1_w_gate, m_ffn1_w_up, m_ffn1_w_down, m_w_in, m_w_out, m_ffn2_w_gate, m_ffn2_w_up, m_ffn2_w_down]
    big_v = [v_ffn1_w_gate, v_ffn1_w_up, v_ffn1_w_down, v_w_in, v_w_out, v_ffn2_w_gate, v_ffn2_w_up, v_ffn2_w_down]
    big_names = ["ffn1_w_gate", "ffn1_w_up", "ffn1_w_down", "w_in", "w_out", "ffn2_w_gate", "ffn2_w_up", "ffn2_w_down"]

    full = _gather_weights([w[0].astype(BF16) for w in big_w])
    wg1, wu1, wd1, win_f, wout_f, wg2, wu2, wd2 = full
    wout_f = wout_f.reshape(wout_f.shape[0] * wout_f.shape[1], d)

    loss, grad_x, big_g, small_g = _local_step(
        x, loss_target, ffn1_norm_g, mix_norm_g, ffn2_norm_g, attn_q_norm_g, attn_k_norm_g, hgrn_out_norm_g,
        attn_rel_bias[0], hgrn_lower_bounds, wg1, wu1, wd1, win_f, wout_f, wg2, wu2, wd2)
    loss = lax.psum(loss, ("x", "y", "c"))

    core = lax.axis_index("c").astype(jnp.int32).reshape(1)
    chip = (2 * lax.axis_index("x") + lax.axis_index("y")).astype(jnp.int32).reshape(1)
    theirs = _pair_exchange(big_g)
    sums = [_pair_sum("pair_sum_" + nm, g, t, core) for nm, g, t in zip(big_names, big_g, theirs)]
    parts = _chip_scatter(sums)
    halves = [_chip_sum("chip_sum_" + nm, s, p, chip) for nm, s, p in zip(big_names, sums, parts)]
    other_halves, small_all = _pair_join(halves, small_g)

    big_out = [_adamw("adamw_" + nm, w[0], gm, gt, m[0], v[0], core)
               for nm, w, gm, gt, m, v in zip(big_names, big_w, halves, other_halves, big_m, big_v)]
    pack = lambda g1, gm, g2, gq, gk, rel, lbp, go: _pack_small(g1, gm, g2, lbp, rel[0], gq, gk, go)
    small_w = pack(ffn1_norm_g, mix_norm_g, ffn2_norm_g, attn_q_norm_g, attn_k_norm_g, attn_rel_bias, hgrn_lower_bounds, hgrn_out_norm_g)
    small_m = pack(m_ffn1_norm_g, m_mix_norm_g, m_ffn2_norm_g, m_attn_q_norm_g, m_attn_k_norm_g, m_attn_rel_bias, m_hgrn_lower_bounds, m_hgrn_out_norm_g)
    small_v = pack(v_ffn1_norm_g, v_mix_norm_g, v_ffn2_norm_g, v_attn_q_norm_g, v_attn_k_norm_g, v_attn_rel_bias, v_hgrn_lower_bounds, v_hgrn_out_norm_g)
    small_out = [_unpack_small(p, d) for p in _adamw_small("adamw_small", small_w, small_all, small_m, small_v)]

    def assemble(kind):
        bg = [o[kind][None] for o in big_out]
        g1, gm, g2, gq, gk, rel, lbp, go = small_out[kind]
        return [g1, bg[0], bg[1], bg[2], gm, bg[3], gq, gk, rel, lbp, go, bg[4], g2, bg[5], bg[6], bg[7]]

    return (loss, grad_x, *assemble(0), *assemble(1), *assemble(2), *assemble(3))
```

```python
import functools

import jax
import jax.numpy as jnp
from jax import lax
from jax.experimental import pallas as pl
from jax.experimental.pallas import tpu as pltpu

F32 = jnp.float32
BF16 = jnp.bfloat16
MESH = pl.DeviceIdType.MESH

N_CHIPS = 4
N_DEV = 8
CHUNK = 64
ATTN_HEADS = 8
ATTN_DH = 64
ATTN_W = ATTN_HEADS * ATTN_DH
HGRN_HEADS = 4
HGRN_DH = 128
HGRN_W = HGRN_HEADS * HGRN_DH
LEFT_CHUNKS = 8
BAND = (LEFT_CHUNKS + 1) * CHUNK
KPAD = LEFT_CHUNKS * CHUNK
REL_CLIP = 128
N_REL = 2 * REL_CLIP + 1
N_REL_PAD = 384
RMS_EPS = 1e-6
LANES = 128
SMALL_ROWS = 8
SMALL_COLS = 1024

ADAM_LR = 0.001
ADAM_B1 = 0.9
ADAM_B2 = 0.999
ADAM_EPS = 1e-08
ADAM_WD = 0.01
ADAM_STEP = 10

NN = (((1,), (0,)), ((), ()))
NT = (((1,), (1,)), ((), ()))
TN = (((0,), (0,)), ((), ()))

VMEM_LIMIT = 48 * 1024 * 1024


def _sigmoid(x):
    return 1.0 / (1.0 + jnp.exp(-x))


def _silu(x):
    return x * _sigmoid(x)


def _dsilu(x):
    s = _sigmoid(x)
    return s * (1.0 + x * (1.0 - s))


def _dot(a, b, dims=NN):
    return lax.dot_general(a, b, dims, preferred_element_type=F32)


def _split3(x):
    hi = x.astype(BF16)
    r1 = x - hi.astype(F32)
    mid = r1.astype(BF16)
    lo = (r1 - mid.astype(F32)).astype(BF16)
    return hi, mid, lo


def _dot_exact_rhs(x, mat, dims=NN):
    hi, mid, lo = _split3(x)
    return _dot(hi, mat, dims) + _dot(mid, mat, dims) + _dot(lo, mat, dims)


def _dot_exact_lhs(mat, x, dims=NN):
    hi, mid, lo = _split3(x)
    return _dot(mat, hi, dims) + _dot(mat, mid, dims) + _dot(mat, lo, dims)


def _params(*sem):
    return pltpu.CompilerParams(dimension_semantics=sem, vmem_limit_bytes=VMEM_LIMIT)


def _mm(name, ins, terms, n_acc, grid, acc_shape, outs, epilogue, extras=()):
    nk = grid[2]
    ni, ne, no = len(ins), len(extras), len(outs)

    def body(*refs):
        in_refs = refs[:ni]
        ex_refs = refs[ni:ni + ne]
        out_refs = refs[ni + ne:ni + ne + no]
        acc_refs = refs[ni + ne + no:]
        parts = [None] * n_acc
        for ai, li, ri, dims in terms:
            d = _dot(in_refs[li][...], in_refs[ri][...], dims)
            parts[ai] = d if parts[ai] is None else parts[ai] + d

        def finish(accs):
            res = epilogue(accs, [e[...] for e in ex_refs])
            for o, r in zip(out_refs, res):
                o[...] = r.astype(o.dtype)

        if nk == 1:
            finish(parts)
        else:
            k = pl.program_id(2)

            @pl.when(k == 0)
            def _():
                for a, p in zip(acc_refs, parts):
                    a[...] = p

            @pl.when(k > 0)
            def _():
                for a, p in zip(acc_refs, parts):
                    a[...] += p

            @pl.when(k == nk - 1)
            def _():
                finish([a[...] for a in acc_refs])

    scratch = [] if nk == 1 else [pltpu.VMEM(acc_shape, F32) for _ in range(n_acc)]
    res = pl.pallas_call(
        body,
        name=name,
        grid=grid,
        in_specs=[s for _, s in ins] + [s for _, s in extras],
        out_specs=[s for _, s in outs],
        out_shape=[o for o, _ in outs],
        scratch_shapes=scratch,
        compiler_params=_params("parallel", "parallel", "arbitrary"),
    )(*[a for a, _ in ins], *[a for a, _ in extras])
    return res


def _row_tile(t):
    return 512 if t % 512 == 0 else t


def _k_tile(t):
    return 1024 if t % 1024 == 0 else t


def _rmsnorm_fwd(name, x, g):
    t, d = x.shape
    tm = _row_tile(t)

    def body(x_ref, g_ref, h_ref):
        xv = x_ref[...]
        ms = jnp.mean(xv * xv, axis=-1, keepdims=True)
        h_ref[...] = (xv * lax.rsqrt(ms + RMS_EPS) * g_ref[...]).astype(BF16)

    return pl.pallas_call(
        body,
        name=name,
        grid=(t // tm,),
        in_specs=[pl.BlockSpec((tm, d), lambda i: (i, 0)), pl.BlockSpec((1, d), lambda i: (0, 0))],
        out_specs=pl.BlockSpec((tm, d), lambda i: (i, 0)),
        out_shape=jax.ShapeDtypeStruct((t, d), BF16),
        compiler_params=_params("parallel"),
    )(x, g)


def _norm_bwd_epilogue(copy_scale):
    def epilogue(accs, ex):
        dh = accs[0]
        xv, g, dres = ex
        ms = jnp.mean(xv * xv, axis=-1, keepdims=True)
        rstd = lax.rsqrt(ms + RMS_EPS)
        xhat = xv * rstd
        dxhat = dh * g
        dx = rstd * (dxhat - xhat * jnp.mean(dxhat * xhat, axis=-1, keepdims=True))
        out = dres + dx
        dg = jnp.sum(dh * xhat, axis=0, keepdims=True)
        if copy_scale is None:
            return out, dg
        return out, out * copy_scale, dg

    return epilogue


def _ffn_up(name, h, wg, wu):
    t, d = h.shape
    ns, _, f = wg.shape
    tm = _row_tile(t)

    def epilogue(accs, ex):
        a, b = accs
        return a, b, _silu(a) * b

    w_spec = pl.BlockSpec((None, d, f), lambda j, i, k: (j, 0, 0))
    o_spec = pl.BlockSpec((None, tm, f), lambda j, i, k: (j, i, 0))
    o_shape = jax.ShapeDtypeStruct((ns, t, f), BF16)
    return _mm(
        name,
        ins=[(h, pl.BlockSpec((tm, d), lambda j, i, k: (i, 0))), (wg, w_spec), (wu, w_spec)],
        terms=[(0, 0, 1, NN), (1, 0, 2, NN)],
        n_acc=2,
        grid=(ns, t // tm, 1),
        acc_shape=(tm, f),
        outs=[(o_shape, o_spec)] * 3,
        epilogue=epilogue,
    )


def _ffn_down(name, z, wd, x):
    ns, t, f = z.shape
    d = wd.shape[2]
    tm = _row_tile(t)
    row = pl.BlockSpec((tm, d), lambda i, n, k: (i, 0))
    return _mm(
        name,
        ins=[(z, pl.BlockSpec((None, tm, f), lambda i, n, k: (k, i, 0))),
             (wd, pl.BlockSpec((None, f, d), lambda i, n, k: (k, 0, 0)))],
        terms=[(0, 0, 1, NN)],
        n_acc=1,
        grid=(t // tm, 1, ns),
        acc_shape=(tm, d),
        outs=[(jax.ShapeDtypeStruct((t, d), F32), row)],
        epilogue=lambda accs, ex: (ex[0] + 0.5 * accs[0],),
        extras=[(x, row)],
    )[0]


def _ffn_down_loss(name, z, wd, x, target):
    ns, t, f = z.shape
    d = wd.shape[2]
    tm = _row_tile(t)
    nt = t // tm
    row = pl.BlockSpec((tm, d), lambda i, n, k: (i, 0))

    def epilogue(accs, ex):
        e = ex[0] + 0.5 * accs[0] - ex[1]
        dy = e * (1.0 / d)
        return dy, 0.5 * dy, jnp.sum(e * e, axis=0, keepdims=True)

    return _mm(
        name,
        ins=[(z, pl.BlockSpec((None, tm, f), lambda i, n, k: (k, i, 0))),
             (wd, pl.BlockSpec((None, f, d), lambda i, n, k: (k, 0, 0)))],
        terms=[(0, 0, 1, NN)],
        n_acc=1,
        grid=(nt, 1, ns),
        acc_shape=(tm, d),
        outs=[(jax.ShapeDtypeStruct((t, d), F32), row), (jax.ShapeDtypeStruct((t, d), BF16), row),
              (jax.ShapeDtypeStruct((nt, 1, d), F32), pl.BlockSpec((None, 1, d), lambda i, n, k: (i, 0, 0)))],
        epilogue=epilogue,
        extras=[(x, row), (target, row)],
    )


def _ffn_bwd_act(name, dout, wd, a, b):
    t, d = dout.shape
    ns, f, _ = wd.shape
    tm = _row_tile(t)

    def epilogue(accs, ex):
        dz = accs[0]
        av = ex[0].astype(F32)
        bv = ex[1].astype(F32)
        return dz * bv * _dsilu(av), dz * _silu(av)

    act = pl.BlockSpec((None, tm, f), lambda j, i, k: (j, i, 0))
    o_shape = jax.ShapeDtypeStruct((ns, t, f), BF16)
    return _mm(
        name,
        ins=[(dout, pl.BlockSpec((tm, d), lambda j, i, k: (i, 0))),
             (wd, pl.BlockSpec((None, f, d), lambda j, i, k: (j, 0, 0)))],
        terms=[(0, 0, 1, NT)],
        n_acc=1,
        grid=(ns, t // tm, 1),
        acc_shape=(tm, f),
        outs=[(o_shape, act)] * 2,
        epilogue=epilogue,
        extras=[(a, act), (b, act)],
    )


def _grad_w_shardcols(name, h, da):
    t, d = h.shape
    ns, _, f = da.shape
    tk = _k_tile(t)
    return _mm(
        name,
        ins=[(h, pl.BlockSpec((tk, d), lambda j, n, k: (k, 0))),
             (da, pl.BlockSpec((None, tk, f), lambda j, n, k: (j, k, 0)))],
        terms=[(0, 0, 1, TN)],
        n_acc=1,
        grid=(ns, 1, t // tk),
        acc_shape=(d, f),
        outs=[(jax.ShapeDtypeStruct((ns, d, f), F32), pl.BlockSpec((None, d, f), lambda j, n, k: (j, 0, 0)))],
        epilogue=lambda accs, ex: (accs[0],),
    )[0]


def _grad_w_shardrows(name, z, dout):
    ns, t, f = z.shape
    d = dout.shape[1]
    tk = _k_tile(t)
    return _mm(
        name,
        ins=[(z, pl.BlockSpec((None, tk, f), lambda j, n, k: (j, k, 0))),
             (dout, pl.BlockSpec((tk, d), lambda j, n, k: (k, 0)))],
        terms=[(0, 0, 1, TN)],
        n_acc=1,
        grid=(ns, 1, t // tk),
        acc_shape=(f, d),
        outs=[(jax.ShapeDtypeStruct((ns, f, d), F32), pl.BlockSpec((None, f, d), lambda j, n, k: (j, 0, 0)))],
        epilogue=lambda accs, ex: (accs[0],),
    )[0]


def _ffn_bwd_in(name, da, db, wg, wu, x, g, dres, copy_scale):
    ns, t, f = da.shape
    d = wg.shape[1]
    tm = _row_tile(t)
    nt = t // tm
    act = pl.BlockSpec((None, tm, f), lambda i, n, k: (k, i, 0))
    w_spec = pl.BlockSpec((None, d, f), lambda i, n, k: (k, 0, 0))
    row = pl.BlockSpec((tm, d), lambda i, n, k: (i, 0))
    outs = [(jax.ShapeDtypeStruct((t, d), F32), row)]
    if copy_scale is not None:
        outs.append((jax.ShapeDtypeStruct((t, d), BF16), row))
    outs.append((jax.ShapeDtypeStruct((nt, 1, d), F32), pl.BlockSpec((None, 1, d), lambda i, n, k: (i, 0, 0))))
    return _mm(
        name,
        ins=[(da, act), (db, act), (wg, w_spec), (wu, w_spec)],
        terms=[(0, 0, 2, NT), (0, 1, 3, NT)],
        n_acc=1,
        grid=(nt, 1, ns),
        acc_shape=(tm, d),
        outs=outs,
        epilogue=_norm_bwd_epilogue(copy_scale),
        extras=[(x, row), (g, pl.BlockSpec((1, d), lambda i, n, k: (0, 0))), (dres, row)],
    )


def _in_proj(name, h, w_in):
    t, d = h.shape
    ns, _, pj = w_in.shape
    tm = _row_tile(t)
    return _mm(
        name,
        ins=[(h, pl.BlockSpec((tm, d), lambda j, i, k: (i, 0))),
             (w_in, pl.BlockSpec((None, d, pj), lambda j, i, k: (j, 0, 0)))],
        terms=[(0, 0, 1, NN)],
        n_acc=1,
        grid=(ns, t // tm, 1),
        acc_shape=(tm, pj),
        outs=[(jax.ShapeDtypeStruct((t, ns * pj), F32), pl.BlockSpec((tm, pj), lambda j, i, k: (i, j)))],
        epilogue=lambda accs, ex: (accs[0],),
    )[0]


def _in_proj_bwd(name, dp, w_in, x, g, dres, copy_scale):
    t = dp.shape[0]
    ns, d, pj = w_in.shape
    tm = _row_tile(t)
    nt = t // tm
    row = pl.BlockSpec((tm, d), lambda i, n, k: (i, 0))
    outs = [(jax.ShapeDtypeStruct((t, d), F32), row)]
    if copy_scale is not None:
        outs.append((jax.ShapeDtypeStruct((t, d), BF16), row))
    outs.append((jax.ShapeDtypeStruct((nt, 1, d), F32), pl.BlockSpec((None, 1, d), lambda i, n, k: (i, 0, 0))))
    return _mm(
        name,
        ins=[(dp, pl.BlockSpec((tm, pj), lambda i, n, k: (i, k))),
             (w_in, pl.BlockSpec((None, d, pj), lambda i, n, k: (k, 0, 0)))],
        terms=[(0, 0, 1, NT)],
        n_acc=1,
        grid=(nt, 1, ns),
        acc_shape=(tm, d),
        outs=outs,
        epilogue=_norm_bwd_epilogue(copy_scale),
        extras=[(x, row), (g, pl.BlockSpec((1, d), lambda i, n, k: (0, 0))), (dres, row)],
    )


def _grad_w_in(name, h, dp, ns):
    t, d = h.shape
    pj = dp.shape[1] // ns
    tk = _k_tile(t)
    return _mm(
        name,
        ins=[(h, pl.BlockSpec((tk, d), lambda j, n, k: (k, 0))),
             (dp, pl.BlockSpec((tk, pj), lambda j, n, k: (k, j)))],
        terms=[(0, 0, 1, TN)],
        n_acc=1,
        grid=(ns, 1, t // tk),
        acc_shape=(d, pj),
        outs=[(jax.ShapeDtypeStruct((ns, d, pj), F32), pl.BlockSpec((None, d, pj), lambda j, n, k: (j, 0, 0)))],
        epilogue=lambda accs, ex: (accs[0],),
    )[0]


def _out_proj(name, mix, w_out, x):
    t, dm = mix.shape
    d = w_out.shape[1]
    tm = _row_tile(t)
    row = pl.BlockSpec((tm, d), lambda i, n, k: (i, 0))
    return _mm(
        name,
        ins=[(mix, pl.BlockSpec((tm, dm), lambda i, n, k: (i, 0))),
             (w_out, pl.BlockSpec((dm, d), lambda i, n, k: (0, 0)))],
        terms=[(0, 0, 1, NN)],
        n_acc=1,
        grid=(t // tm, 1, 1),
        acc_shape=(tm, d),
        outs=[(jax.ShapeDtypeStruct((t, d), F32), row)],
        epilogue=lambda accs, ex: (ex[0] + accs[0],),
        extras=[(x, row)],
    )[0]


def _out_proj_bwd(name, dx, w_out):
    t, d = dx.shape
    dm = w_out.shape[0]
    tm = _row_tile(t)
    return _mm(
        name,
        ins=[(dx, pl.BlockSpec((tm, d), lambda i, n, k: (i, 0))),
             (w_out, pl.BlockSpec((dm, d), lambda i, n, k: (0, 0)))],
        terms=[(0, 0, 1, NT)],
        n_acc=1,
        grid=(t // tm, 1, 1),
        acc_shape=(tm, dm),
        outs=[(jax.ShapeDtypeStruct((t, dm), F32), pl.BlockSpec((tm, dm), lambda i, n, k: (i, 0)))],
        epilogue=lambda accs, ex: (accs[0],),
    )[0]


def _grad_w_out(name, mix, dx):
    t, dm = mix.shape
    d = dx.shape[1]
    tk = _k_tile(t)
    return _mm(
        name,
        ins=[(mix, pl.BlockSpec((tk, dm), lambda a, n, k: (k, 0))),
             (dx, pl.BlockSpec((tk, d), lambda a, n, k: (k, 0)))],
        terms=[(0, 0, 1, TN)],
        n_acc=1,
        grid=(1, 1, t // tk),
        acc_shape=(dm, d),
        outs=[(jax.ShapeDtypeStruct((dm, d), F32), pl.BlockSpec((dm, d), lambda a, n, k: (0, 0)))],
        epilogue=lambda accs, ex: (accs[0],),
    )[0]


def _head_group_matrix():
    r = lax.broadcasted_iota(jnp.int32, (ATTN_W, ATTN_W), 0)
    c = lax.broadcasted_iota(jnp.int32, (ATTN_W, ATTN_W), 1)
    same = jnp.right_shift(r, 6) == jnp.right_shift(c, 6)
    return jnp.where(same, 1.0, 0.0).astype(BF16)


def _qk_prep(name, proj, gq, gk):
    b, s, _ = proj.shape
    tm = KPAD
    nb = s // tm

    def body(q_ref, k_ref, v_ref, gq_ref, gk_ref, qn_ref, kn_ref, vb_ref):
        j = pl.program_id(1)
        bd = _head_group_matrix()

        def norm(xv, g):
            ms = _dot_exact_rhs(xv * xv, bd) * (1.0 / ATTN_DH)
            return xv * lax.rsqrt(ms + RMS_EPS) * g

        @pl.when(j == 0)
        def _():
            kn_ref[...] = jnp.zeros_like(kn_ref)
            vb_ref[...] = jnp.zeros_like(vb_ref)

        @pl.when(j > 0)
        def _():
            qn_ref[...] = norm(q_ref[...], gq_ref[...]).astype(BF16)
            kn_ref[...] = norm(k_ref[...], gk_ref[...]).astype(BF16)
            vb_ref[...] = v_ref[...].astype(BF16)

    src_blk = lambda col: pl.BlockSpec((None, tm, ATTN_W), lambda bi, j: (bi, jnp.maximum(j - 1, 0), col))
    gspec = pl.BlockSpec((1, ATTN_W), lambda bi, j: (0, 0))
    padded = pl.BlockSpec((None, tm, ATTN_W), lambda bi, j: (bi, j, 0))
    return pl.pallas_call(
        body,
        name=name,
        grid=(b, nb + 1),
        in_specs=[src_blk(0), src_blk(1), src_blk(2), gspec, gspec],
        out_specs=[src_blk(0), padded, padded],
        out_shape=[jax.ShapeDtypeStruct((b, s, ATTN_W), BF16), jax.ShapeDtypeStruct((b, KPAD + s, ATTN_W), BF16),
                   jax.ShapeDtypeStruct((b, KPAD + s, ATTN_W), BF16)],
        compiler_params=_params("parallel", "arbitrary"),
    )(proj, proj, proj, gq, gk)


def _qk_prep_bwd(name, proj, dqn, dkn, dv, gq, gk):
    b, s, _ = proj.shape
    tm = KPAD
    nb = s // tm

    def body(q_ref, k_ref, dqn_ref, dkn_ref, dv_ref, gq_ref, gk_ref, dq_ref, dk_ref, dvb_ref, dgq_ref, dgk_ref):
        bd = _head_group_matrix()

        def bwd(xv, dy, g):
            ms = _dot_exact_rhs(xv * xv, bd) * (1.0 / ATTN_DH)
            rstd = lax.rsqrt(ms + RMS_EPS)
            xhat = xv * rstd
            dxhat = dy * g
            gm = _dot_exact_rhs(dxhat * xhat, bd) * (1.0 / ATTN_DH)
            return rstd * (dxhat - xhat * gm), jnp.sum(dy * xhat, axis=0, keepdims=True)

        dq, dgq = bwd(q_ref[...], dqn_ref[...], gq_ref[...])
        dk, dgk = bwd(k_ref[...], dkn_ref[...], gk_ref[...])
        dq_ref[...] = dq.astype(BF16)
        dk_ref[...] = dk.astype(BF16)
        dvb_ref[...] = dv_ref[...].astype(BF16)
        dgq_ref[...] = dgq
        dgk_ref[...] = dgk

    col = lambda c: pl.BlockSpec((None, tm, ATTN_W), lambda bi, j: (bi, j, c))
    past_pad = pl.BlockSpec((None, tm, ATTN_W), lambda bi, j: (bi, j + 1, 0))
    gspec = pl.BlockSpec((1, ATTN_W), lambda bi, j: (0, 0))
    pspec = pl.BlockSpec((None, 1, ATTN_W), lambda bi, j: (bi * nb + j, 0, 0))
    o_shape = jax.ShapeDtypeStruct((b, s, ATTN_W), BF16)
    p_shape = jax.ShapeDtypeStruct((b * nb, 1, ATTN_W), F32)
    return pl.pallas_call(
        body,
        name=name,
        grid=(b, nb),
        in_specs=[col(0), col(1), col(0), past_pad, past_pad, gspec, gspec],
        out_specs=[col(0)] * 3 + [pspec] * 2,
        out_shape=[o_shape] * 3 + [p_shape] * 2,
        compiler_params=_params("parallel", "parallel"),
    )(proj, proj, dqn, dkn, dv, gq, gk)


Q_CHUNKS = 4
QBLK = Q_CHUNKS * CHUNK
WIN = (LEFT_CHUNKS + Q_CHUNKS) * CHUNK
DB_W = BAND + CHUNK
MASKED = -1e30


def _band_table(bias):
    rows = [jnp.pad(bias, ((0, 0), (0, 0), (CHUNK * i, WIN - BAND - CHUNK * i)), constant_values=MASKED)
            for i in range(Q_CHUNKS)]
    return jnp.concatenate(rows, axis=1)


def _head_lanes(hh):
    lane = lax.broadcasted_iota(jnp.int32, (1, LANES), 1)
    return (lane < ATTN_DH) if hh == 0 else (lane >= ATTN_DH)


def _attn_probs(qh, kw, table, start):
    s = _dot(qh, kw, NT) * (ATTN_DH ** -0.5) + table
    col = lax.broadcasted_iota(jnp.int32, (QBLK, WIN), 1)
    s = jnp.where(col + start >= KPAD, s, MASKED)
    m = jnp.max(s, axis=-1, keepdims=True)
    p = jnp.exp(s - m)
    return p * (1.0 / jnp.sum(p, axis=-1, keepdims=True))


def _attn_fwd(name, q, k, v, table):
    b, s, w = q.shape
    sp = k.shape[1]

    def body(q_ref, k_ref, v_ref, t_ref, o_ref):
        start = pl.multiple_of(pl.program_id(2) * QBLK, QBLK)
        kw = k_ref[pl.ds(start, WIN), :]
        vw = v_ref[pl.ds(start, WIN), :]
        q2 = q_ref[...]
        out = jnp.zeros((QBLK, LANES), F32)
        for hh in range(2):
            mine = _head_lanes(hh)
            p = _attn_probs(jnp.where(mine, q2, jnp.zeros_like(q2)), kw, t_ref[hh], start)
            out = jnp.where(mine, _dot(p.astype(BF16), vw), out)
        o_ref[...] = out.astype(BF16)

    qspec = pl.BlockSpec((None, QBLK, LANES), lambda p, bi, i: (bi, i, p))
    kspec = pl.BlockSpec((None, sp, LANES), lambda p, bi, i: (bi, 0, p))
    return pl.pallas_call(
        body,
        name=name,
        grid=(w // LANES, b, s // QBLK),
        in_specs=[qspec, kspec, kspec, pl.BlockSpec((2, QBLK, WIN), lambda p, bi, i: (p, 0, 0))],
        out_specs=qspec,
        out_shape=jax.ShapeDtypeStruct((b, s, w), BF16),
        compiler_params=_params("parallel", "parallel", "arbitrary"),
    )(q, k, v, table)


def _attn_bwd(name, q, k, v, table, dmix):
    b, s, w = q.shape
    sp = k.shape[1]

    def body(q_ref, k_ref, v_ref, t_ref, do_ref, dq_ref, dk_ref, dv_ref, dbe_ref, dbo_ref):
        bi = pl.program_id(1)
        i = pl.program_id(2)
        start = pl.multiple_of(i * QBLK, QBLK)
        win = pl.ds(start, WIN)

        @pl.when(i == 0)
        def _():
            dk_ref[...] = jnp.zeros_like(dk_ref)
            dv_ref[...] = jnp.zeros_like(dv_ref)

        @pl.when(jnp.logical_and(i == 0, bi == 0))
        def _():
            dbe_ref[...] = jnp.zeros_like(dbe_ref)
            dbo_ref[...] = jnp.zeros_like(dbo_ref)

        kw = k_ref[win, :]
        vw = v_ref[win, :]
        q2 = q_ref[...]
        do2 = do_ref[...].astype(BF16)
        dq = jnp.zeros((QBLK, LANES), F32)
        for hh in range(2):
            mine = _head_lanes(hh)
            qh = jnp.where(mine, q2, jnp.zeros_like(q2))
            doh = jnp.where(mine, do2, jnp.zeros_like(do2))
            p = _attn_probs(qh, kw, t_ref[hh], start)
            dp = _dot(doh, vw, NT)
            ds = p * (dp - jnp.sum(p * dp, axis=-1, keepdims=True))
            for qi in range(Q_CHUNKS):
                c0 = (qi // 2) * LANES
                blk = ds[qi * CHUNK:(qi + 1) * CHUNK, c0:c0 + DB_W]
                if qi % 2 == 0:
                    dbe_ref[hh] += blk
                else:
                    dbo_ref[hh] += blk
            dsb = (ds * (ATTN_DH ** -0.5)).astype(BF16)
            dq = jnp.where(mine, _dot(dsb, kw), dq)
            dk_ref[win, :] += _dot(dsb, qh, TN)
            dv_ref[win, :] += _dot(p.astype(BF16), doh, TN)
        dq_ref[...] = dq

    qspec = pl.BlockSpec((None, QBLK, LANES), lambda p, bi, i: (bi, i, p))
    kspec = pl.BlockSpec((None, sp, LANES), lambda p, bi, i: (bi, 0, p))
    dbspec = pl.BlockSpec((2, CHUNK, DB_W), lambda p, bi, i: (p, 0, 0))
    db_shape = jax.ShapeDtypeStruct((ATTN_HEADS, CHUNK, DB_W), F32)
    return pl.pallas_call(
        body,
        name=name,
        grid=(w // LANES, b, s // QBLK),
        in_specs=[qspec, kspec, kspec, pl.BlockSpec((2, QBLK, WIN), lambda p, bi, i: (p, 0, 0)), qspec],
        out_specs=[qspec, kspec, kspec, dbspec, dbspec],
        out_shape=[jax.ShapeDtypeStruct((b, s, w), F32), jax.ShapeDtypeStruct((b, sp, w), F32),
                   jax.ShapeDtypeStruct((b, sp, w), F32), db_shape, db_shape],
        compiler_params=_params("arbitrary", "arbitrary", "arbitrary"),
    )(q, k, v, table, dmix)


HQ_COL = 3 * ATTN_W // HGRN_DH
HF_COL = HQ_COL + HGRN_HEADS
HI_COL = HF_COL + HGRN_HEADS
HG_COL = HI_COL + HGRN_HEADS


def _tri(lower):
    r = lax.broadcasted_iota(jnp.int32, (CHUNK, CHUNK), 0)
    c = lax.broadcasted_iota(jnp.int32, (CHUNK, CHUNK), 1)
    return (r >= c) if lower else (r <= c)


def _hgrn_chunk(hq, hf, lb, tril):
    sig = _sigmoid(hf)
    f = lb + (1.0 - lb) * sig
    g = jnp.log(f)
    ones_l = jnp.where(tril, 1.0, 0.0).astype(BF16)
    b = _dot_exact_lhs(ones_l, g)
    bl = jnp.sum(g, axis=0, keepdims=True)
    rows = lax.broadcasted_iota(jnp.int32, g.shape, 0)
    bm = jnp.sum(jnp.where(rows <= CHUNK // 2, g, 0.0), axis=0, keepdims=True)
    sq = _sigmoid(hq)
    q = hq * sq
    k = 1.0 - f
    return sig, f, b, bl, bm, sq, q, k


def _hgrn_fwd(name, proj, lb, go, b, s):
    nc = s // CHUNK
    t = b * s

    def body(hq_ref, hf_ref, hi_ref, hg_ref, lb_ref, go_ref, ro_ref, oraw_ref, st_ref, s_scr):
        tril = _tri(True)
        lbv = lb_ref[...]
        gov = go_ref[...]
        s_scr[...] = jnp.zeros_like(s_scr)

        def step(c, carry):
            sl = pl.ds(pl.multiple_of(c * CHUNK, CHUNK), CHUNK)
            hg = hg_ref[sl, :]
            _, _, bb, bl, bm, _, q, k = _hgrn_chunk(hq_ref[sl, :], hf_ref[sl, :], lbv, tril)
            vb = hi_ref[sl, :].astype(BF16)
            qe = (q * jnp.exp(bb - bm)).astype(BF16)
            ke = (k * jnp.exp(bm - bb)).astype(BF16)
            a = jnp.where(tril, _dot(qe, ke, NT), 0.0)
            st = s_scr[...]
            st_ref[c] = st
            qb = (q * jnp.exp(bb)).astype(BF16)
            o = _dot(a.astype(BF16), vb) + _dot(qb, st.astype(BF16), NT)
            kb = (k * jnp.exp(bl - bb)).astype(BF16)
            s_scr[...] = st * jnp.exp(bl) + _dot(vb, kb, TN)
            rstd = lax.rsqrt(jnp.mean(o * o, axis=-1, keepdims=True) + RMS_EPS)
            ro_ref[sl, :] = ((o * rstd * gov) * _silu(hg)).astype(BF16)
            oraw_ref[sl, :] = o
            return carry

        lax.fori_loop(0, nc, step, 0)

    col = lambda base: pl.BlockSpec((s, HGRN_DH), lambda bi, h: (bi, base + h))
    vec = pl.BlockSpec((1, HGRN_DH), lambda bi, h: (0, h))
    out = pl.BlockSpec((s, HGRN_DH), lambda bi, h: (bi, h))
    return pl.pallas_call(
        body,
        name=name,
        grid=(b, HGRN_HEADS),
        in_specs=[col(HQ_COL), col(HF_COL), col(HI_COL), col(HG_COL), vec,
                  pl.BlockSpec((1, HGRN_DH), lambda bi, h: (0, 0))],
        out_specs=[out, out,
                   pl.BlockSpec((None, None, nc, HGRN_DH, HGRN_DH), lambda bi, h: (bi, h, 0, 0, 0))],
        out_shape=[jax.ShapeDtypeStruct((t, HGRN_W), BF16), jax.ShapeDtypeStruct((t, HGRN_W), F32),
                   jax.ShapeDtypeStruct((b, HGRN_HEADS, nc, HGRN_DH, HGRN_DH), F32)],
        scratch_shapes=[pltpu.VMEM((HGRN_DH, HGRN_DH), F32)],
        compiler_params=_params("parallel", "parallel"),
    )(proj, proj, proj, proj, lb, go)


def _hgrn_bwd(name, proj, lb, go, oraw, states, dmix, b, s):
    nc = s // CHUNK
    t = b * s

    def body(hq_ref, hf_ref, hi_ref, hg_ref, lb_ref, go_ref, oraw_ref, st_ref, dro_ref,
             dhq_ref, dhf_ref, dhi_ref, dhg_ref, dlb_ref, dgo_ref, ds_scr, dlb_scr, dgo_scr):
        tril = _tri(True)
        ones_u = jnp.where(_tri(False), 1.0, 0.0).astype(BF16)
        lbv = lb_ref[...]
        gov = go_ref[...]
        ds_scr[...] = jnp.zeros_like(ds_scr)
        dlb_scr[...] = jnp.zeros_like(dlb_scr)
        dgo_scr[...] = jnp.zeros_like(dgo_scr)

        def step(ci, carry):
            c = nc - 1 - ci
            sl = pl.ds(pl.multiple_of(c * CHUNK, CHUNK), CHUNK)
            hq = hq_ref[sl, :]
            hg = hg_ref[sl, :]
            sig, f, bb, bl, bm, sq, q, k = _hgrn_chunk(hq, hf_ref[sl, :], lbv, tril)
            vb = hi_ref[sl, :].astype(BF16)
            ebm = jnp.exp(bb - bm)
            embm = jnp.exp(bm - bb)
            eb = jnp.exp(bb)
            ebl = jnp.exp(bl - bb)
            e_last = jnp.exp(bl)
            qe = (q * ebm).astype(BF16)
            ke = (k * embm).astype(BF16)
            qb = (q * eb).astype(BF16)
            kb = (k * ebl).astype(BF16)
            a = jnp.where(tril, _dot(qe, ke, NT), 0.0)
            st = st_ref[c]
            dst = ds_scr[...]
            o = oraw_ref[sl, :]
            dro = dro_ref[sl, :]
            sg = _sigmoid(hg)
            rstd = lax.rsqrt(jnp.mean(o * o, axis=-1, keepdims=True) + RMS_EPS)
            ohat = o * rstd
            dn = dro * (hg * sg)
            dhg_ref[sl, :] = (dro * (ohat * gov) * (sg * (1.0 + hg * (1.0 - sg)))).astype(BF16)
            dgo_scr[...] += jnp.sum(dn * ohat, axis=0, keepdims=True)
            dohat = dn * gov
            do = rstd * (dohat - ohat * jnp.mean(dohat * ohat, axis=-1, keepdims=True))
            dob = do.astype(BF16)
            dab = jnp.where(tril, _dot(dob, vb, NT), 0.0).astype(BF16)
            stb = st.astype(BF16)
            dstb = dst.astype(BF16)
            dv = _dot(a.astype(BF16), dob, TN) + _dot(kb, dstb, NT)
            dqe = _dot(dab, ke)
            dke = _dot(dab, qe, TN)
            dqb = _dot(dob, stb)
            dkb = _dot(vb, dstb)
            dq = dqe * ebm + dqb * eb
            dk = dke * embm + dkb * ebl
            db = (qe.astype(F32) * dqe - ke.astype(F32) * dke) + q * (dqb * eb) - k * (dkb * ebl)
            d_last = (jnp.sum(k * ebl * dkb, axis=0, keepdims=True)
                      + jnp.sum(dst * st, axis=0, keepdims=True) * e_last)
            dg = _dot_exact_lhs(ones_u, db) + d_last
            df = dg / f - dk
            dhf_ref[sl, :] = (df * (1.0 - lbv) * sig * (1.0 - sig)).astype(BF16)
            dlb_scr[...] += jnp.sum(df * (1.0 - sig), axis=0, keepdims=True)
            dhq_ref[sl, :] = (dq * (sq * (1.0 + hq * (1.0 - sq)))).astype(BF16)
            dhi_ref[sl, :] = dv.astype(BF16)
            ds_scr[...] = dst * e_last + _dot(dob, qb, TN)
            return carry

        lax.fori_loop(0, nc, step, 0)
        dlb_ref[...] = dlb_scr[...]
        dgo_ref[...] = dgo_scr[...]

    col = lambda base: pl.BlockSpec((s, HGRN_DH), lambda bi, h: (bi, base + h))
    vec = pl.BlockSpec((1, HGRN_DH), lambda bi, h: (0, h))
    out = pl.BlockSpec((s, HGRN_DH), lambda bi, h: (bi, h))
    part = pl.BlockSpec((None, 1, HGRN_DH), lambda bi, h: (bi, 0, h))
    o_shape = jax.ShapeDtypeStruct((t, HGRN_W), BF16)
    p_shape = jax.ShapeDtypeStruct((b, 1, HGRN_W), F32)
    return pl.pallas_call(
        body,
        name=name,
        grid=(b, HGRN_HEADS),
        in_specs=[col(HQ_COL), col(HF_COL), col(HI_COL), col(HG_COL), vec,
                  pl.BlockSpec((1, HGRN_DH), lambda bi, h: (0, 0)), out,
                  pl.BlockSpec((None, None, nc, HGRN_DH, HGRN_DH), lambda bi, h: (bi, h, 0, 0, 0)),
                  col(ATTN_W // HGRN_DH)],
        out_specs=[out] * 4 + [part] * 2,
        out_shape=[o_shape] * 4 + [p_shape] * 2,
        scratch_shapes=[pltpu.VMEM((HGRN_DH, HGRN_DH), F32), pltpu.VMEM((1, HGRN_DH), F32),
                        pltpu.VMEM((1, HGRN_DH), F32)],
        compiler_params=_params("parallel", "parallel"),
    )(proj, proj, proj, proj, lb, go, oraw, states, dmix)


def _small_grads(name, dg1, dgm, dg2, dgq, dgk, dbias_t, dlb, dgo, lbp):
    d = dg1.shape[1]

    def body(dg1_ref, dgm_ref, dg2_ref, dgq_ref, dgk_ref, dbias_ref, dlb_ref, dgo_ref, lbp_ref,
             g1_ref, gm_ref, g2_ref, gq_ref, gk_ref, rb_ref, lbg_ref, go_ref):
        g1_ref[...] = jnp.sum(dg1_ref[...], axis=0, keepdims=True)
        gm_ref[...] = jnp.sum(dgm_ref[...], axis=0, keepdims=True)
        g2_ref[...] = jnp.sum(dg2_ref[...], axis=0, keepdims=True)
        r = lax.broadcasted_iota(jnp.int32, (ATTN_W, ATTN_DH), 0)
        cidx = lax.broadcasted_iota(jnp.int32, (ATTN_W, ATTN_DH), 1)
        fold = jnp.where(jnp.bitwise_and(r, ATTN_DH - 1) == cidx, 1.0, 0.0).astype(BF16)
        gq_ref[...] = jnp.sum(_dot_exact_rhs(dgq_ref[...], fold), axis=0, keepdims=True)
        gk_ref[...] = jnp.sum(_dot_exact_rhs(dgk_ref[...], fold), axis=0, keepdims=True)
        gosum = jnp.sum(dgo_ref[...], axis=0, keepdims=True)
        go_ref[...] = (gosum[:, 0:HGRN_DH] + gosum[:, HGRN_DH:2 * HGRN_DH]
                       + gosum[:, 2 * HGRN_DH:3 * HGRN_DH] + gosum[:, 3 * HGRN_DH:4 * HGRN_DH])
        p0 = lbp_ref[0:1, :]
        p1 = lbp_ref[1:2, :]
        lbv = 1.0 / (1.0 + jnp.exp(p1 - p0))
        dp0 = jnp.sum(dlb_ref[...], axis=0, keepdims=True) * lbv * (1.0 - lbv)
        lbg_ref[0:1, :] = dp0
        lbg_ref[1:2, :] = -dp0
        sidx = lax.broadcasted_iota(jnp.int32, (BAND, N_REL_PAD), 0)
        ridx = lax.broadcasted_iota(jnp.int32, (BAND, N_REL_PAD), 1)

        def step(tq, acc):
            rel = jnp.clip(tq + KPAD - sidx, -REL_CLIP, REL_CLIP) + REL_CLIP
            onehot = jnp.where(rel == ridx, 1.0, 0.0).astype(BF16)
            return acc + _dot_exact_rhs(dbias_ref[tq], onehot)

        rb_ref[...] = lax.fori_loop(0, CHUNK, step, jnp.zeros((ATTN_HEADS, N_REL_PAD), F32))

    ins = [dg1, dgm, dg2, dgq, dgk, dbias_t, dlb, dgo, lbp]
    outs = [jax.ShapeDtypeStruct((1, d), F32)] * 3 + [jax.ShapeDtypeStruct((1, ATTN_DH), F32)] * 2 + [
        jax.ShapeDtypeStruct((ATTN_HEADS, N_REL_PAD), F32), jax.ShapeDtypeStruct((2, HGRN_W), F32),
        jax.ShapeDtypeStruct((1, HGRN_DH), F32)]
    vm = pl.BlockSpec(memory_space=pltpu.VMEM)
    return pl.pallas_call(
        body,
        name=name,
        in_specs=[vm] * len(ins),
        out_specs=[vm] * len(outs),
        out_shape=outs,
        compiler_params=pltpu.CompilerParams(vmem_limit_bytes=VMEM_LIMIT),
    )(*ins)


def _adam_update(w, g, m, v):
    m2 = ADAM_B1 * m + (1.0 - ADAM_B1) * g
    v2 = ADAM_B2 * v + (1.0 - ADAM_B2) * (g * g)
    m_hat = m2 / (1.0 - ADAM_B1 ** ADAM_STEP)
    v_hat = v2 / (1.0 - ADAM_B2 ** ADAM_STEP)
    delta = -ADAM_LR * (m_hat / (jnp.sqrt(v_hat) + ADAM_EPS) + ADAM_WD * w)
    return delta, m2, v2


def _rows_tile(r):
    for cand in (256, 352, 128, 176, 64, 32, 16):
        if r % cand == 0 and r > cand:
            return cand
    return r


def _pair_sum(name, grad, theirs, core):
    n, half, c = theirs.shape
    tr = _rows_tile(half)
    nth = half // tr

    def body(core_ref, a_ref, b_ref, o_ref):
        o_ref[...] = (a_ref[...] + b_ref[...]).astype(o_ref.dtype)

    spec = pl.BlockSpec((None, tr, c), lambda i, j, core_ref: (i, j, 0))
    return pl.pallas_call(
        body, name=name,
        grid_spec=pltpu.PrefetchScalarGridSpec(
            num_scalar_prefetch=1, grid=(n, nth),
            in_specs=[pl.BlockSpec((None, tr, c), lambda i, j, core_ref: (i, core_ref[0] * nth + j, 0)), spec],
            out_specs=spec),
        out_shape=jax.ShapeDtypeStruct((n, half, c), BF16), compiler_params=_params("parallel", "parallel"),
    )(core, grad, theirs)


def _chip_sum(name, own, parts, chip):
    _, half, c = own.shape
    tr = _rows_tile(half)

    def body(chip_ref, own_ref, p_ref, o_ref):
        me = chip_ref[0]
        mine = own_ref[...].astype(F32)
        flip_x, flip_y, flip_xy = (p_ref[i].astype(F32) for i in range(3))
        acc = None
        for k in range(N_CHIPS):
            rel = jnp.bitwise_xor(me, k)
            term = jnp.where(rel == 0, mine, jnp.where(rel == 2, flip_x, jnp.where(rel == 1, flip_y, flip_xy)))
            acc = term if acc is None else acc + term
        o_ref[...] = acc

    return pl.pallas_call(
        body, name=name,
        grid_spec=pltpu.PrefetchScalarGridSpec(
            num_scalar_prefetch=1, grid=(half // tr,),
            in_specs=[pl.BlockSpec((None, tr, c), lambda j, chip_ref: (chip_ref[0], j, 0)),
                      pl.BlockSpec((3, tr, c), lambda j, chip_ref: (0, j, 0))],
            out_specs=pl.BlockSpec((tr, c), lambda j, chip_ref: (j, 0))),
        out_shape=jax.ShapeDtypeStruct((half, c), F32), compiler_params=_params("parallel"),
    )(chip, own, parts)


def _adamw(name, w, g_mine, g_theirs, m, v, core):
    _, r, c = w.shape
    half = r // 2
    tr = _rows_tile(half)
    nth = half // tr

    def body(core_ref, w_ref, gm_ref, gt_ref, m_ref, v_ref, g_ref, d_ref, m2_ref, v2_ref):
        g = jnp.where(pl.program_id(0) == core_ref[0], gm_ref[...], gt_ref[...])
        delta, m2, v2 = _adam_update(w_ref[...], g, m_ref[...], v_ref[...])
        g_ref[...] = g
        d_ref[...] = delta
        m2_ref[...] = m2
        v2_ref[...] = v2

    full = pl.BlockSpec((None, tr, c), lambda h, j, core_ref: (0, h * nth + j, 0))
    part = pl.BlockSpec((tr, c), lambda h, j, core_ref: (j, 0))
    shape = jax.ShapeDtypeStruct((1, r, c), F32)
    return pl.pallas_call(
        body, name=name,
        grid_spec=pltpu.PrefetchScalarGridSpec(
            num_scalar_prefetch=1, grid=(2, nth), in_specs=[full, part, part, full, full], out_specs=[full] * 4),
        out_shape=[shape] * 4, compiler_params=_params("parallel", "parallel"),
    )(core, w, g_mine, g_theirs, m, v)


def _rel_bias_table(name, rel_bias):
    padded = jnp.pad(rel_bias, ((0, 0), (0, N_REL_PAD - N_REL)))

    def body(rb_ref, o_ref):
        ridx = lax.broadcasted_iota(jnp.int32, (N_REL_PAD, BAND), 0)
        sidx = lax.broadcasted_iota(jnp.int32, (N_REL_PAD, BAND), 1)
        rb = rb_ref[...]

        def step(tq, carry):
            rel = jnp.clip(tq + KPAD - sidx, -REL_CLIP, REL_CLIP) + REL_CLIP
            onehot = jnp.where(rel == ridx, 1.0, 0.0).astype(BF16)
            o_ref[tq] = _dot_exact_rhs(rb, onehot)
            return carry

        lax.fori_loop(0, CHUNK, step, 0)

    vm = pl.BlockSpec(memory_space=pltpu.VMEM)
    table = pl.pallas_call(
        body, name=name, in_specs=[vm], out_specs=vm,
        out_shape=jax.ShapeDtypeStruct((CHUNK, ATTN_HEADS, BAND), F32),
    )(padded)
    return table.transpose(1, 0, 2)


def _adamw_small(name, w, parts, m, v):
    def body(w_ref, p_ref, m_ref, v_ref, g_ref, d_ref, m2_ref, v2_ref):
        g = p_ref[0]
        for i in range(1, N_DEV):
            g = g + p_ref[i]
        delta, m2, v2 = _adam_update(w_ref[...], g, m_ref[...], v_ref[...])
        g_ref[...] = g
        d_ref[...] = delta
        m2_ref[...] = m2
        v2_ref[...] = v2

    vm = pl.BlockSpec(memory_space=pltpu.VMEM)
    shape = jax.ShapeDtypeStruct((SMALL_ROWS, SMALL_COLS), F32)
    return pl.pallas_call(
        body, name=name, in_specs=[vm] * 4, out_specs=[vm] * 4, out_shape=[shape] * 4,
    )(w, parts, m, v)


def _position():
    return lax.axis_index("x"), lax.axis_index("y"), lax.axis_index("c")


def _other_chips(x, y):
    return [(1 - x, y), (x, 1 - y), (1 - x, 1 - y)]


ANY = pl.BlockSpec(memory_space=pl.ANY)


def _gather_weights(shards):
    n = len(shards)

    def body(*refs):
        ins, outs = refs[:n], refs[n:2 * n]
        ici_send, ici_recv, fwd_send, fwd_recv, own_send, own_recv = refs[2 * n:]
        x, y, c = _position()
        me = 2 * x + y
        chips = _other_chips(x, y)

        def own(i):
            return pltpu.make_async_remote_copy(
                src_ref=ins[i], dst_ref=outs[i].at[me], send_sem=own_send.at[i], recv_sem=own_recv.at[i],
                device_id=(x, y, 1 - c), device_id_type=MESH)

        def push(i, j):
            return pltpu.make_async_remote_copy(
                src_ref=ins[i], dst_ref=outs[i].at[me], send_sem=ici_send.at[3 * i + j], recv_sem=ici_recv.at[3 * i + j],
                device_id=(chips[j][0], chips[j][1], 1), device_id_type=MESH)

        def arrival(i, j):
            return pltpu.make_async_remote_copy(
                src_ref=ins[i], dst_ref=outs[i].at[2 * chips[j][0] + chips[j][1]],
                send_sem=ici_send.at[3 * i + j], recv_sem=ici_recv.at[3 * i + j],
                device_id=(chips[j][0], chips[j][1], 1), device_id_type=MESH)

        def onward(i, j):
            slot = outs[i].at[2 * chips[j][0] + chips[j][1]]
            return pltpu.make_async_remote_copy(
                src_ref=slot, dst_ref=slot, send_sem=fwd_send.at[3 * i + j], recv_sem=fwd_recv.at[3 * i + j],
                device_id=(x, y, 1 - c), device_id_type=MESH)

        @pl.when(c == 1)
        def _north():
            for i in range(n):
                for j in range(3):
                    push(i, j).start()

        for i in range(n):
            own(i).start()

        @pl.when(c == 1)
        def _north_forward():
            for i in range(n):
                for j in range(3):
                    arrival(i, j).wait_recv()
                    onward(i, j).start()
            for i in range(n):
                for j in range(3):
                    push(i, j).wait_send()
                    onward(i, j).wait_send()

        @pl.when(c == 0)
        def _south():
            for i in range(n):
                for j in range(3):
                    onward(i, j).wait_recv()

        for i in range(n):
            own(i).wait()

    return pl.pallas_call(
        body,
        name="gather_weights",
        in_specs=[ANY] * n,
        out_specs=[ANY] * n,
        out_shape=[jax.ShapeDtypeStruct((N_CHIPS,) + s.shape, s.dtype) for s in shards],
        scratch_shapes=[pltpu.SemaphoreType.DMA((3 * n,))] * 4 + [pltpu.SemaphoreType.DMA((n,))] * 2,
    )(*shards)


def _pair_exchange(name, grads):
    n = len(grads)

    def body(*refs):
        ins, theirs = refs[:n], refs[n:2 * n]
        send_sem, recv_sem = refs[2 * n:]
        x, y, c = _position()
        copies = []
        for i in range(n):
            half = ins[i].shape[1] // 2
            give = pl.ds(pl.multiple_of((1 - c) * half, 8), half)
            swap = pltpu.make_async_remote_copy(
                src_ref=ins[i].at[:, give, :], dst_ref=theirs[i], send_sem=send_sem.at[i], recv_sem=recv_sem.at[i],
                device_id=(x, y, 1 - c), device_id_type=MESH)
            swap.start()
            copies.append(swap)
        for swap in copies:
            swap.wait()

    return pl.pallas_call(
        body,
        name=name,
        in_specs=[ANY] * n,
        out_specs=[ANY] * n,
        out_shape=[jax.ShapeDtypeStruct((g.shape[0], g.shape[1] // 2, g.shape[2]), g.dtype) for g in grads],
        scratch_shapes=[pltpu.SemaphoreType.DMA((n,))] * 2,
    )(*grads)


HBM = pl.BlockSpec(memory_space=pltpu.HBM)
SEM = pl.BlockSpec(memory_space=pltpu.SEMAPHORE)
SPLIT_COPY = pltpu.SideEffectType.DATAFLOW_SIDE_EFFECTING


def _scatter_copy(srcs, lands, send_sem, recv_sem, i, j):
    x, y, c = _position()
    chips = _other_chips(x, y)
    return pltpu.make_async_remote_copy(
        src_ref=srcs[i].at[2 * chips[j][0] + chips[j][1]], dst_ref=lands[i].at[j],
        send_sem=send_sem.at[3 * i + j], recv_sem=recv_sem.at[3 * i + j],
        device_id=(chips[j][0], chips[j][1], c), device_id_type=MESH)


def _scatter_start(name, sums):
    n = len(sums)

    def body(*refs):
        srcs, lands = refs[:n], refs[n:2 * n]
        send_sem, recv_sem = refs[2 * n], refs[2 * n + 1]
        token = refs[-1]
        for i in range(n):
            for j in range(3):
                _scatter_copy(srcs, lands, send_sem, recv_sem, i, j).start()
        token[...] = jnp.zeros_like(token)

    land_shapes = [(3,) + s.shape[1:] for s in sums]
    res = pl.pallas_call(
        body,
        name=name,
        in_specs=[HBM] * (2 * n),
        out_specs=[SEM, SEM] + [HBM] * (2 * n) + [pl.BlockSpec(memory_space=pltpu.VMEM)],
        out_shape=[pltpu.SemaphoreType.DMA((3 * n,)), pltpu.SemaphoreType.DMA((3 * n,))]
        + [pltpu.HBM(s.shape, s.dtype) for s in sums]
        + [pltpu.HBM(shp, s.dtype) for shp, s in zip(land_shapes, sums)]
        + [jax.ShapeDtypeStruct((8, LANES), F32)],
        input_output_aliases={i: 2 + i for i in range(2 * n)},
        compiler_params=pltpu.CompilerParams(has_side_effects=SPLIT_COPY),
    )(*[pltpu.with_memory_space_constraint(s, pltpu.HBM) for s in sums],
      *[pltpu.with_memory_space_constraint(lax.empty(shp, s.dtype), pltpu.HBM) for shp, s in zip(land_shapes, sums)])
    return res[0], res[1], list(res[2:2 + n]), list(res[2 + n:2 + 2 * n]), res[-1]


def _scatter_wait(name, send_sem, recv_sem, sums, lands, after):
    n = len(sums)

    def body(*refs):
        srcs, land_refs = refs[:n], refs[n:2 * n]
        send_ref, recv_ref = refs[2 * n], refs[2 * n + 1]
        for i in range(n):
            for j in range(3):
                copy = _scatter_copy(srcs, land_refs, send_ref, recv_ref, i, j)
                copy.wait_send()
                copy.wait_recv()

    res = pl.pallas_call(
        body,
        name=name,
        in_specs=[HBM] * (2 * n) + [SEM, SEM, ANY],
        out_specs=[HBM] * (2 * n),
        out_shape=[pltpu.HBM(s.shape, s.dtype) for s in sums] + [pltpu.HBM(l.shape, l.dtype) for l in lands],
        input_output_aliases={i: i for i in range(2 * n)},
        compiler_params=pltpu.CompilerParams(has_side_effects=SPLIT_COPY),
    )(*sums, *lands, send_sem, recv_sem, after)
    return list(res[:n]), list(res[n:])


def _pair_join(name, halves, small=None):
    n = len(halves)
    if small is None:
        def body_plain(*refs):
            ins, outs = refs[:n], refs[n:2 * n]
            send_sem, recv_sem = refs[2 * n:]
            x, y, c = _position()
            swaps = [pltpu.make_async_remote_copy(
                src_ref=ins[i], dst_ref=outs[i], send_sem=send_sem.at[i], recv_sem=recv_sem.at[i],
                device_id=(x, y, 1 - c), device_id_type=MESH) for i in range(n)]
            for swap in swaps:
                swap.start()
            for swap in swaps:
                swap.wait()

        return pl.pallas_call(
            body_plain,
            name=name,
            in_specs=[ANY] * n,
            out_specs=[ANY] * n,
            out_shape=[jax.ShapeDtypeStruct(h.shape, h.dtype) for h in halves],
            scratch_shapes=[pltpu.SemaphoreType.DMA((n,))] * 2,
        )(*halves)

    def body(*refs):
        ins, small_ref = refs[:n], refs[n]
        outs, all_ref = refs[n + 1:2 * n + 1], refs[2 * n + 1]
        send_sem, recv_sem, sm_send, sm_recv, sm_local = refs[2 * n + 2:]
        x, y, c = _position()
        swaps = []
        for i in range(n):
            swap = pltpu.make_async_remote_copy(
                src_ref=ins[i], dst_ref=outs[i], send_sem=send_sem.at[i], recv_sem=recv_sem.at[i],
                device_id=(x, y, 1 - c), device_id_type=MESH)
            swap.start()
            swaps.append(swap)
        me = 4 * x + 2 * y + c
        sm_own = pltpu.make_async_copy(small_ref, all_ref.at[me], sm_local)
        sm_own.start()
        pushes, arrivals = [], []
        for mask in range(1, N_DEV):
            px, py, pc = x ^ (mask >> 2), y ^ ((mask >> 1) & 1), c ^ (mask & 1)
            pushes.append(pltpu.make_async_remote_copy(
                src_ref=small_ref, dst_ref=all_ref.at[me], send_sem=sm_send.at[mask - 1], recv_sem=sm_recv.at[mask - 1],
                device_id=(px, py, pc), device_id_type=MESH))
            arrivals.append(pltpu.make_async_remote_copy(
                src_ref=small_ref, dst_ref=all_ref.at[4 * px + 2 * py + pc], send_sem=sm_send.at[mask - 1],
                recv_sem=sm_recv.at[mask - 1], device_id=(px, py, pc), device_id_type=MESH))
        for cp in pushes:
            cp.start()
        for swap in swaps:
            swap.wait()
        for cp in arrivals:
            cp.wait_recv()
        for cp in pushes:
            cp.wait_send()
        sm_own.wait()

    res = pl.pallas_call(
        body,
        name=name,
        in_specs=[ANY] * (n + 1),
        out_specs=[ANY] * (n + 1),
        out_shape=[jax.ShapeDtypeStruct(h.shape, h.dtype) for h in halves]
        + [jax.ShapeDtypeStruct((N_DEV,) + small.shape, small.dtype)],
        scratch_shapes=[pltpu.SemaphoreType.DMA((n,))] * 2 + [pltpu.SemaphoreType.DMA((N_DEV - 1,))] * 2
        + [pltpu.SemaphoreType.DMA(())],
    )(*halves, small)
    return res[:n], res[n]


def _lower_bound(lbp):
    return jax.nn.softmax(lbp, axis=0)[0:1]


def _local_step(x, target, g1, gm, g2, gq, gk, go, rel_bias, lbp, wg1, wu1, wd1, w_in, w_out, wg2, wu2, wd2, on_grads):
    b, s, d = x.shape
    t = b * s
    ns = w_in.shape[0]
    x0 = x.reshape(t, d)
    tgt = target.reshape(t, d)
    gq_t = jnp.tile(gq, (1, ATTN_HEADS))
    gk_t = jnp.tile(gk, (1, ATTN_HEADS))
    lb = _lower_bound(lbp)
    bias = _rel_bias_table("rel_bias_table", rel_bias)

    h1 = _rmsnorm_fwd("norm1", x0, g1)
    a1, b1, z1 = _ffn_up("ffn1_up", h1, wg1, wu1)
    x1 = _ffn_down("ffn1_down", z1, wd1, x0)
    h2 = _rmsnorm_fwd("norm_mix", x1, gm)
    proj = _in_proj("in_proj", h2, w_in)
    proj3 = proj.reshape(b, s, proj.shape[1])
    table = _band_table(bias)
    qn, kn, vb = _qk_prep("qk_prep", proj3, gq_t, gk_t)
    attn = _attn_fwd("attn_fwd", qn, kn, vb, table).reshape(t, ATTN_W)
    ro, oraw, states = _hgrn_fwd("hgrn_fwd", proj, lb, go, b, s)
    mix = jnp.concatenate([attn, ro], axis=1)
    x2 = _out_proj("out_proj", mix, w_out, x1)
    h3 = _rmsnorm_fwd("norm2", x2, g2)
    a2, b2, z2 = _ffn_up("ffn2_up", h3, wg2, wu2)
    dy, dyh, sq = _ffn_down_loss("ffn2_down_loss", z2, wd2, x2, tgt)
    loss = 0.5 * jnp.sum(sq) / d

    da2, db2 = _ffn_bwd_act("ffn2_bwd_act", dyh, wd2, a2, b2)
    dwd2 = _grad_w_shardrows("ffn2_dwd", z2, dyh)
    dwg2 = _grad_w_shardcols("ffn2_dwg", h3, da2)
    dwu2 = _grad_w_shardcols("ffn2_dwu", h3, db2)
    on_grads("ffn2", {"ffn2_w_gate": dwg2, "ffn2_w_up": dwu2, "ffn2_w_down": dwd2})
    dx2, dx2b, dg2 = _ffn_bwd_in("ffn2_bwd_in", da2, db2, wg2, wu2, x2, g2, dy, 1.0)

    dwout = _grad_w_out("dw_out", mix, dx2b)
    dmix = _out_proj_bwd("out_proj_bwd", dx2b, w_out)
    dqn, dkn, dvn, dbe, dbo = _attn_bwd("attn_bwd", qn, kn, vb, table, dmix.reshape(b, s, dmix.shape[1]))
    dbias = dbe[:, :, :BAND] + dbo[:, :, CHUNK:]
    dpq, dpk, dpv, dgq, dgk = _qk_prep_bwd("qk_prep_bwd", proj3, dqn, dkn, dvn, gq_t, gk_t)
    dpq, dpk, dpv = (a.reshape(t, ATTN_W) for a in (dpq, dpk, dpv))
    dhq, dhf, dhi, dhg, dlb, dgo = _hgrn_bwd("hgrn_bwd", proj, lb, go, oraw, states, dmix, b, s)
    dproj = jnp.concatenate([dpq, dpk, dpv, dhq, dhf, dhi, dhg], axis=1)
    dwin = _grad_w_in("dw_in", h2, dproj, ns)
    on_grads("mix", {"w_in": dwin, "w_out": dwout.reshape(ns, dwout.shape[0] // ns, d)})
    dx1, dx1h, dgm = _in_proj_bwd("in_proj_bwd", dproj, w_in, x1, gm, dx2, 0.5)

    da1, db1 = _ffn_bwd_act("ffn1_bwd_act", dx1h, wd1, a1, b1)
    dwd1 = _grad_w_shardrows("ffn1_dwd", z1, dx1h)
    dwg1 = _grad_w_shardcols("ffn1_dwg", h1, da1)
    dwu1 = _grad_w_shardcols("ffn1_dwu", h1, db1)
    on_grads("ffn1", {"ffn1_w_gate": dwg1, "ffn1_w_up": dwu1, "ffn1_w_down": dwd1})
    dx0, dg1 = _ffn_bwd_in("ffn1_bwd_in", da1, db1, wg1, wu1, x0, g1, dx1, None)

    nt = dg1.shape[0]
    sg = _small_grads(
        "small_grads", dg1.reshape(nt, d), dgm.reshape(nt, d), dg2.reshape(nt, d),
        dgq.reshape(-1, ATTN_W), dgk.reshape(-1, ATTN_W), dbias.transpose(1, 0, 2),
        dlb.reshape(b, HGRN_W), dgo.reshape(b, HGRN_W), lbp)
    g1g, gmg, g2g, gqg, gkg, rbg, lbg, gog = sg
    small = _pack_small(g1g, gmg, g2g, lbg, rbg[:, :N_REL], gqg, gkg, gog)
    return loss, dx0.reshape(b, s, d), small


def _pack_small(g1, gm, g2, lbp, rel_bias, gq, gk, go):
    flat = [g1.reshape(-1), gm.reshape(-1), g2.reshape(-1), lbp.reshape(-1), rel_bias.reshape(-1)]
    n_bias = 3 * SMALL_COLS - rel_bias.size
    heads = [gq.reshape(-1), gk.reshape(-1), go.reshape(-1)]
    n_tail = SMALL_COLS - sum(h.size for h in heads)
    return jnp.concatenate(flat + [jnp.zeros((n_bias,), F32)] + heads + [jnp.zeros((n_tail,), F32)]).reshape(
        SMALL_ROWS, SMALL_COLS)


def _unpack_small(p, d):
    flat = p.reshape(-1)
    o = 3 * d
    g1, gm, g2 = p[0:1], p[1:2], p[2:3]
    lbp = flat[o:o + 2 * HGRN_W].reshape(2, HGRN_W)
    o = 4 * SMALL_COLS
    rel = flat[o:o + ATTN_HEADS * N_REL].reshape(1, ATTN_HEADS, N_REL)
    o = 7 * SMALL_COLS
    gq = flat[o:o + ATTN_DH].reshape(1, ATTN_DH)
    gk = flat[o + ATTN_DH:o + 2 * ATTN_DH].reshape(1, ATTN_DH)
    go = flat[o + 2 * ATTN_DH:o + 2 * ATTN_DH + HGRN_DH].reshape(1, HGRN_DH)
    return g1, gm, g2, gq, gk, rel, lbp, go


def kernel(x, ffn1_norm_g, ffn1_w_gate, ffn1_w_up, ffn1_w_down, mix_norm_g, w_in, attn_q_norm_g, attn_k_norm_g, attn_rel_bias, hgrn_lower_bounds, hgrn_out_norm_g, w_out, ffn2_norm_g, ffn2_w_gate, ffn2_w_up, ffn2_w_down, loss_target, m_ffn1_norm_g, m_ffn1_w_gate, m_ffn1_w_up, m_ffn1_w_down, m_mix_norm_g, m_w_in, m_attn_q_norm_g, m_attn_k_norm_g, m_attn_rel_bias, m_hgrn_lower_bounds, m_hgrn_out_norm_g, m_w_out, m_ffn2_norm_g, m_ffn2_w_gate, m_ffn2_w_up, m_ffn2_w_down, v_ffn1_norm_g, v_ffn1_w_gate, v_ffn1_w_up, v_ffn1_w_down, v_mix_norm_g, v_w_in, v_attn_q_norm_g, v_attn_k_norm_g, v_attn_rel_bias, v_hgrn_lower_bounds, v_hgrn_out_norm_g, v_w_out, v_ffn2_norm_g, v_ffn2_w_gate, v_ffn2_w_up, v_ffn2_w_down):
    d = x.shape[-1]
    big_w = [ffn1_w_gate, ffn1_w_up, ffn1_w_down, w_in, w_out, ffn2_w_gate, ffn2_w_up, ffn2_w_down]
    big_m = [m_ffn1_w_gate, m_ffn1_w_up, m_ffn1_w_down, m_w_in, m_w_out, m_ffn2_w_gate, m_ffn2_w_up, m_ffn2_w_down]
    big_v = [v_ffn1_w_gate, v_ffn1_w_up, v_ffn1_w_down, v_w_in, v_w_out, v_ffn2_w_gate, v_ffn2_w_up, v_ffn2_w_down]
    big_names = ["ffn1_w_gate", "ffn1_w_up", "ffn1_w_down", "w_in", "w_out", "ffn2_w_gate", "ffn2_w_up", "ffn2_w_down"]

    full = _gather_weights([w[0].astype(BF16) for w in big_w])
    wg1, wu1, wd1, win_f, wout_f, wg2, wu2, wd2 = full
    wout_f = wout_f.reshape(wout_f.shape[0] * wout_f.shape[1], d)

    core = lax.axis_index("c").astype(jnp.int32).reshape(1)
    chip = (2 * lax.axis_index("x") + lax.axis_index("y")).astype(jnp.int32).reshape(1)
    started = {}

    def on_grads(tag, grads):
        names = list(grads)
        theirs = _pair_exchange("pair_exchange_" + tag, [grads[nm] for nm in names])
        sums = [_pair_sum("pair_sum_" + nm, grads[nm], th, core) for nm, th in zip(names, theirs)]
        started[tag] = (names, _scatter_start("scatter_start_" + tag, sums))

    loss, grad_x, small_g = _local_step(
        x, loss_target, ffn1_norm_g, mix_norm_g, ffn2_norm_g, attn_q_norm_g, attn_k_norm_g, hgrn_out_norm_g,
        attn_rel_bias[0], hgrn_lower_bounds, wg1, wu1, wd1, win_f, wout_f, wg2, wu2, wd2, on_grads)
    loss = lax.psum(loss, ("x", "y", "c"))

    def finish(tag, after):
        names, (send_sem, recv_sem, sums, lands, _) = started[tag]
        sums, lands = _scatter_wait("scatter_wait_" + tag, send_sem, recv_sem, sums, lands, after)
        return names, [_chip_sum("chip_sum_" + nm, sm, ld, chip) for nm, sm, ld in zip(names, sums, lands)]

    by_name = {nm: (w, m, v) for nm, w, m, v in zip(big_names, big_w, big_m, big_v)}
    updated = {}

    def update(names, halves, other_halves):
        for nm, mine, theirs in zip(names, halves, other_halves):
            w, m, v = by_name[nm]
            updated[nm] = _adamw("adamw_" + nm, w, mine, theirs, m, v, core)

    last_token = started["ffn1"][1][4]
    names_a, halves_a = finish("ffn2", last_token)
    names_m, halves_m = finish("mix", last_token)
    names_a, halves_a = names_a + names_m, halves_a + halves_m
    update(names_a, halves_a, _pair_join("pair_join_early", halves_a))
    names_b, halves_b = finish("ffn1", updated["w_out"][1])
    others_b, small_all = _pair_join("pair_join_last", halves_b, small_g)
    update(names_b, halves_b, others_b)
    big_out = [updated[nm] for nm in big_names]

    pack = lambda g1, gm, g2, gq, gk, rel, lbp, go: _pack_small(g1, gm, g2, lbp, rel[0], gq, gk, go)
    small_w = pack(ffn1_norm_g, mix_norm_g, ffn2_norm_g, attn_q_norm_g, attn_k_norm_g, attn_rel_bias, hgrn_lower_bounds, hgrn_out_norm_g)
    small_m = pack(m_ffn1_norm_g, m_mix_norm_g, m_ffn2_norm_g, m_attn_q_norm_g, m_attn_k_norm_g, m_attn_rel_bias, m_hgrn_lower_bounds, m_hgrn_out_norm_g)
    small_v = pack(v_ffn1_norm_g, v_mix_norm_g, v_ffn2_norm_g, v_attn_q_norm_g, v_attn_k_norm_g, v_attn_rel_bias, v_hgrn_lower_bounds, v_hgrn_out_norm_g)
    small_out = [_unpack_small(p, d) for p in _adamw_small("adamw_small", small_w, small_all, small_m, small_v)]

    def assemble(kind):
        bg = [o[kind] for o in big_out]
        g1, gm, g2, gq, gk, rel, lbp, go = small_out[kind]
        return [g1, bg[0], bg[1], bg[2], gm, bg[3], gq, gk, rel, lbp, go, bg[4], g2, bg[5], bg[6], bg[7]]

    return (loss, grad_x, *assemble(0), *assemble(1), *assemble(2), *assemble(3))
```

```python
import functools

import jax
import jax.numpy as jnp
from jax import lax
from jax.experimental import pallas as pl
from jax.experimental.pallas import tpu as pltpu

F32 = jnp.float32
BF16 = jnp.bfloat16
MESH = pl.DeviceIdType.MESH

N_CHIPS = 4
N_DEV = 8
CHUNK = 64
ATTN_HEADS = 8
ATTN_DH = 64
ATTN_W = ATTN_HEADS * ATTN_DH
HGRN_HEADS = 4
HGRN_DH = 128
HGRN_W = HGRN_HEADS * HGRN_DH
LEFT_CHUNKS = 8
BAND = (LEFT_CHUNKS + 1) * CHUNK
KPAD = LEFT_CHUNKS * CHUNK
REL_CLIP = 128
N_REL = 2 * REL_CLIP + 1
N_REL_PAD = 384
RMS_EPS = 1e-6
LANES = 128
SMALL_ROWS = 8
SMALL_COLS = 1024

ADAM_LR = 0.001
ADAM_B1 = 0.9
ADAM_B2 = 0.999
ADAM_EPS = 1e-08
ADAM_WD = 0.01
ADAM_STEP = 10

NN = (((1,), (0,)), ((), ()))
NT = (((1,), (1,)), ((), ()))
TN = (((0,), (0,)), ((), ()))

VMEM_LIMIT = 48 * 1024 * 1024


def _sigmoid(x):
    return 1.0 / (1.0 + jnp.exp(-x))


def _silu(x):
    return x * _sigmoid(x)


def _dsilu(x):
    s = _sigmoid(x)
    return s * (1.0 + x * (1.0 - s))


def _dot(a, b, dims=NN):
    return lax.dot_general(a, b, dims, preferred_element_type=F32)


def _split3(x):
    hi = x.astype(BF16)
    r1 = x - hi.astype(F32)
    mid = r1.astype(BF16)
    lo = (r1 - mid.astype(F32)).astype(BF16)
    return hi, mid, lo


def _dot_exact_rhs(x, mat, dims=NN):
    hi, mid, lo = _split3(x)
    return _dot(hi, mat, dims) + _dot(mid, mat, dims) + _dot(lo, mat, dims)


def _dot_exact_lhs(mat, x, dims=NN):
    hi, mid, lo = _split3(x)
    return _dot(mat, hi, dims) + _dot(mat, mid, dims) + _dot(mat, lo, dims)


def _params(*sem):
    return pltpu.CompilerParams(dimension_semantics=sem, vmem_limit_bytes=VMEM_LIMIT)


def _mm(name, ins, terms, n_acc, grid, acc_shape, outs, epilogue, extras=(), deps=()):
    nk = grid[2]
    ni, ne, nd, no = len(ins), len(extras), len(deps), len(outs)

    def body(*refs):
        in_refs = refs[:ni]
        ex_refs = refs[ni:ni + ne]
        out_refs = refs[ni + ne + nd:ni + ne + nd + no]
        acc_refs = refs[ni + ne + nd + no:]
        parts = [None] * n_acc
        for ai, li, ri, dims in terms:
            d = _dot(in_refs[li][...], in_refs[ri][...], dims)
            parts[ai] = d if parts[ai] is None else parts[ai] + d

        def finish(accs):
            res = epilogue(accs, [e[...] for e in ex_refs])
            for o, r in zip(out_refs, res):
                o[...] = r.astype(o.dtype)

        if nk == 1:
            finish(parts)
        else:
            k = pl.program_id(2)

            @pl.when(k == 0)
            def _():
                for a, p in zip(acc_refs, parts):
                    a[...] = p

            @pl.when(k > 0)
            def _():
                for a, p in zip(acc_refs, parts):
                    a[...] += p

            @pl.when(k == nk - 1)
            def _():
                finish([a[...] for a in acc_refs])

    scratch = [] if nk == 1 else [pltpu.VMEM(acc_shape, F32) for _ in range(n_acc)]
    res = pl.pallas_call(
        body,
        name=name,
        grid=grid,
        in_specs=[s for _, s in ins] + [s for _, s in extras] + [pl.BlockSpec(memory_space=pl.ANY)] * nd,
        out_specs=[s for _, s in outs],
        out_shape=[o for o, _ in outs],
        scratch_shapes=scratch,
        compiler_params=_params("parallel", "parallel", "arbitrary"),
    )(*[a for a, _ in ins], *[a for a, _ in extras], *deps)
    return res


def _row_tile(t):
    return 512 if t % 512 == 0 else t


def _k_tile(t):
    return 1024 if t % 1024 == 0 else t


def _rmsnorm_fwd(name, x, g):
    t, d = x.shape
    tm = _row_tile(t)

    def body(x_ref, g_ref, h_ref):
        xv = x_ref[...]
        ms = jnp.mean(xv * xv, axis=-1, keepdims=True)
        h_ref[...] = (xv * lax.rsqrt(ms + RMS_EPS) * g_ref[...]).astype(BF16)

    return pl.pallas_call(
        body,
        name=name,
        grid=(t // tm,),
        in_specs=[pl.BlockSpec((tm, d), lambda i: (i, 0)), pl.BlockSpec((1, d), lambda i: (0, 0))],
        out_specs=pl.BlockSpec((tm, d), lambda i: (i, 0)),
        out_shape=jax.ShapeDtypeStruct((t, d), BF16),
        compiler_params=_params("parallel"),
    )(x, g)


def _norm_bwd_epilogue(copy_scale):
    def epilogue(accs, ex):
        dh = accs[0]
        xv, g, dres = ex
        ms = jnp.mean(xv * xv, axis=-1, keepdims=True)
        rstd = lax.rsqrt(ms + RMS_EPS)
        xhat = xv * rstd
        dxhat = dh * g
        dx = rstd * (dxhat - xhat * jnp.mean(dxhat * xhat, axis=-1, keepdims=True))
        out = dres + dx
        dg = jnp.sum(dh * xhat, axis=0, keepdims=True)
        if copy_scale is None:
            return out, dg
        return out, out * copy_scale, dg

    return epilogue


def _ffn_up(name, h, wg, wu, deps=()):
    t, d = h.shape
    ns, _, f = wg.shape
    tm = _row_tile(t)

    def epilogue(accs, ex):
        a, b = accs
        return a, b, _silu(a) * b

    w_spec = pl.BlockSpec((None, d, f), lambda j, i, k: (j, 0, 0))
    o_spec = pl.BlockSpec((None, tm, f), lambda j, i, k: (j, i, 0))
    o_shape = jax.ShapeDtypeStruct((ns, t, f), BF16)
    return _mm(
        name,
        ins=[(h, pl.BlockSpec((tm, d), lambda j, i, k: (i, 0))), (wg, w_spec), (wu, w_spec)],
        terms=[(0, 0, 1, NN), (1, 0, 2, NN)],
        n_acc=2,
        grid=(ns, t // tm, 1),
        acc_shape=(tm, f),
        outs=[(o_shape, o_spec)] * 3,
        epilogue=epilogue,
        deps=deps,
    )


def _ffn_down(name, z, wd, x):
    ns, t, f = z.shape
    d = wd.shape[2]
    tm = _row_tile(t)
    row = pl.BlockSpec((tm, d), lambda i, n, k: (i, 0))
    return _mm(
        name,
        ins=[(z, pl.BlockSpec((None, tm, f), lambda i, n, k: (k, i, 0))),
             (wd, pl.BlockSpec((None, f, d), lambda i, n, k: (k, 0, 0)))],
        terms=[(0, 0, 1, NN)],
        n_acc=1,
        grid=(t // tm, 1, ns),
        acc_shape=(tm, d),
        outs=[(jax.ShapeDtypeStruct((t, d), F32), row)],
        epilogue=lambda accs, ex: (ex[0] + 0.5 * accs[0],),
        extras=[(x, row)],
    )[0]


def _ffn_down_loss(name, z, wd, x, target):
    ns, t, f = z.shape
    d = wd.shape[2]
    tm = _row_tile(t)
    nt = t // tm
    row = pl.BlockSpec((tm, d), lambda i, n, k: (i, 0))

    def epilogue(accs, ex):
        e = ex[0] + 0.5 * accs[0] - ex[1]
        dy = e * (1.0 / d)
        return dy, 0.5 * dy, jnp.sum(e * e, axis=0, keepdims=True)

    return _mm(
        name,
        ins=[(z, pl.BlockSpec((None, tm, f), lambda i, n, k: (k, i, 0))),
             (wd, pl.BlockSpec((None, f, d), lambda i, n, k: (k, 0, 0)))],
        terms=[(0, 0, 1, NN)],
        n_acc=1,
        grid=(nt, 1, ns),
        acc_shape=(tm, d),
        outs=[(jax.ShapeDtypeStruct((t, d), F32), row), (jax.ShapeDtypeStruct((t, d), BF16), row),
              (jax.ShapeDtypeStruct((nt, 1, d), F32), pl.BlockSpec((None, 1, d), lambda i, n, k: (i, 0, 0)))],
        epilogue=epilogue,
        extras=[(x, row), (target, row)],
    )


def _ffn_bwd_act(name, dout, wd, a, b, deps=()):
    t, d = dout.shape
    ns, f, _ = wd.shape
    tm = _row_tile(t)

    def epilogue(accs, ex):
        dz = accs[0]
        av = ex[0].astype(F32)
        bv = ex[1].astype(F32)
        return dz * bv * _dsilu(av), dz * _silu(av)

    act = pl.BlockSpec((None, tm, f), lambda j, i, k: (j, i, 0))
    o_shape = jax.ShapeDtypeStruct((ns, t, f), BF16)
    return _mm(
        name,
        ins=[(dout, pl.BlockSpec((tm, d), lambda j, i, k: (i, 0))),
             (wd, pl.BlockSpec((None, f, d), lambda j, i, k: (j, 0, 0)))],
        terms=[(0, 0, 1, NT)],
        n_acc=1,
        grid=(ns, t // tm, 1),
        acc_shape=(tm, f),
        outs=[(o_shape, act)] * 2,
        epilogue=epilogue,
        extras=[(a, act), (b, act)],
        deps=deps,
    )


def _grad_w_shardcols(name, h, da):
    t, d = h.shape
    ns, _, f = da.shape
    tk = _k_tile(t)
    return _mm(
        name,
        ins=[(h, pl.BlockSpec((tk, d), lambda j, n, k: (k, 0))),
             (da, pl.BlockSpec((None, tk, f), lambda j, n, k: (j, k, 0)))],
        terms=[(0, 0, 1, TN)],
        n_acc=1,
        grid=(ns, 1, t // tk),
        acc_shape=(d, f),
        outs=[(jax.ShapeDtypeStruct((ns, d, f), F32), pl.BlockSpec((None, d, f), lambda j, n, k: (j, 0, 0)))],
        epilogue=lambda accs, ex: (accs[0],),
    )[0]


def _grad_w_shardrows(name, z, dout):
    ns, t, f = z.shape
    d = dout.shape[1]
    tk = _k_tile(t)
    return _mm(
        name,
        ins=[(z, pl.BlockSpec((None, tk, f), lambda j, n, k: (j, k, 0))),
             (dout, pl.BlockSpec((tk, d), lambda j, n, k: (k, 0)))],
        terms=[(0, 0, 1, TN)],
        n_acc=1,
        grid=(ns, 1, t // tk),
        acc_shape=(f, d),
        outs=[(jax.ShapeDtypeStruct((ns, f, d), F32), pl.BlockSpec((None, f, d), lambda j, n, k: (j, 0, 0)))],
        epilogue=lambda accs, ex: (accs[0],),
    )[0]


def _ffn_bwd_in(name, da, db, wg, wu, x, g, dres, copy_scale):
    ns, t, f = da.shape
    d = wg.shape[1]
    tm = _row_tile(t)
    nt = t // tm
    act = pl.BlockSpec((None, tm, f), lambda i, n, k: (k, i, 0))
    w_spec = pl.BlockSpec((None, d, f), lambda i, n, k: (k, 0, 0))
    row = pl.BlockSpec((tm, d), lambda i, n, k: (i, 0))
    outs = [(jax.ShapeDtypeStruct((t, d), F32), row)]
    if copy_scale is not None:
        outs.append((jax.ShapeDtypeStruct((t, d), BF16), row))
    outs.append((jax.ShapeDtypeStruct((nt, 1, d), F32), pl.BlockSpec((None, 1, d), lambda i, n, k: (i, 0, 0))))
    return _mm(
        name,
        ins=[(da, act), (db, act), (wg, w_spec), (wu, w_spec)],
        terms=[(0, 0, 2, NT), (0, 1, 3, NT)],
        n_acc=1,
        grid=(nt, 1, ns),
        acc_shape=(tm, d),
        outs=outs,
        epilogue=_norm_bwd_epilogue(copy_scale),
        extras=[(x, row), (g, pl.BlockSpec((1, d), lambda i, n, k: (0, 0))), (dres, row)],
    )


def _in_proj(name, h, w_in):
    t, d = h.shape
    ns, _, pj = w_in.shape
    tm = _row_tile(t)
    return _mm(
        name,
        ins=[(h, pl.BlockSpec((tm, d), lambda j, i, k: (i, 0))),
             (w_in, pl.BlockSpec((None, d, pj), lambda j, i, k: (j, 0, 0)))],
        terms=[(0, 0, 1, NN)],
        n_acc=1,
        grid=(ns, t // tm, 1),
        acc_shape=(tm, pj),
        outs=[(jax.ShapeDtypeStruct((t, ns * pj), F32), pl.BlockSpec((tm, pj), lambda j, i, k: (i, j)))],
        epilogue=lambda accs, ex: (accs[0],),
    )[0]


def _in_proj_bwd(name, dp, w_in, x, g, dres, copy_scale):
    t = dp.shape[0]
    ns, d, pj = w_in.shape
    tm = _row_tile(t)
    nt = t // tm
    row = pl.BlockSpec((tm, d), lambda i, n, k: (i, 0))
    outs = [(jax.ShapeDtypeStruct((t, d), F32), row)]
    if copy_scale is not None:
        outs.append((jax.ShapeDtypeStruct((t, d), BF16), row))
    outs.append((jax.ShapeDtypeStruct((nt, 1, d), F32), pl.BlockSpec((None, 1, d), lambda i, n, k: (i, 0, 0))))
    return _mm(
        name,
        ins=[(dp, pl.BlockSpec((tm, pj), lambda i, n, k: (i, k))),
             (w_in, pl.BlockSpec((None, d, pj), lambda i, n, k: (k, 0, 0)))],
        terms=[(0, 0, 1, NT)],
        n_acc=1,
        grid=(nt, 1, ns),
        acc_shape=(tm, d),
        outs=outs,
        epilogue=_norm_bwd_epilogue(copy_scale),
        extras=[(x, row), (g, pl.BlockSpec((1, d), lambda i, n, k: (0, 0))), (dres, row)],
    )


def _grad_w_in(name, h, dp, ns):
    t, d = h.shape
    pj = dp.shape[1] // ns
    tk = _k_tile(t)
    return _mm(
        name,
        ins=[(h, pl.BlockSpec((tk, d), lambda j, n, k: (k, 0))),
             (dp, pl.BlockSpec((tk, pj), lambda j, n, k: (k, j)))],
        terms=[(0, 0, 1, TN)],
        n_acc=1,
        grid=(ns, 1, t // tk),
        acc_shape=(d, pj),
        outs=[(jax.ShapeDtypeStruct((ns, d, pj), F32), pl.BlockSpec((None, d, pj), lambda j, n, k: (j, 0, 0)))],
        epilogue=lambda accs, ex: (accs[0],),
    )[0]


def _out_proj(name, mix, w_out, x):
    t, dm = mix.shape
    d = w_out.shape[1]
    tm = _row_tile(t)
    row = pl.BlockSpec((tm, d), lambda i, n, k: (i, 0))
    return _mm(
        name,
        ins=[(mix, pl.BlockSpec((tm, dm), lambda i, n, k: (i, 0))),
             (w_out, pl.BlockSpec((dm, d), lambda i, n, k: (0, 0)))],
        terms=[(0, 0, 1, NN)],
        n_acc=1,
        grid=(t // tm, 1, 1),
        acc_shape=(tm, d),
        outs=[(jax.ShapeDtypeStruct((t, d), F32), row)],
        epilogue=lambda accs, ex: (ex[0] + accs[0],),
        extras=[(x, row)],
    )[0]


def _out_proj_bwd(name, dx, w_out, deps=()):
    t, d = dx.shape
    dm = w_out.shape[0]
    tm = _row_tile(t)
    return _mm(
        name,
        ins=[(dx, pl.BlockSpec((tm, d), lambda i, n, k: (i, 0))),
             (w_out, pl.BlockSpec((dm, d), lambda i, n, k: (0, 0)))],
        terms=[(0, 0, 1, NT)],
        n_acc=1,
        grid=(t // tm, 1, 1),
        acc_shape=(tm, dm),
        outs=[(jax.ShapeDtypeStruct((t, dm), F32), pl.BlockSpec((tm, dm), lambda i, n, k: (i, 0)))],
        epilogue=lambda accs, ex: (accs[0],),
        deps=deps,
    )[0]


def _grad_w_out(name, mix, dx):
    t, dm = mix.shape
    d = dx.shape[1]
    tk = _k_tile(t)
    return _mm(
        name,
        ins=[(mix, pl.BlockSpec((tk, dm), lambda a, n, k: (k, 0))),
             (dx, pl.BlockSpec((tk, d), lambda a, n, k: (k, 0)))],
        terms=[(0, 0, 1, TN)],
        n_acc=1,
        grid=(1, 1, t // tk),
        acc_shape=(dm, d),
        outs=[(jax.ShapeDtypeStruct((dm, d), F32), pl.BlockSpec((dm, d), lambda a, n, k: (0, 0)))],
        epilogue=lambda accs, ex: (accs[0],),
    )[0]


def _head_group_matrix():
    r = lax.broadcasted_iota(jnp.int32, (ATTN_W, ATTN_W), 0)
    c = lax.broadcasted_iota(jnp.int32, (ATTN_W, ATTN_W), 1)
    same = jnp.right_shift(r, 6) == jnp.right_shift(c, 6)
    return jnp.where(same, 1.0, 0.0).astype(BF16)


def _qk_prep(name, proj, gq, gk):
    b, s, _ = proj.shape
    tm = KPAD
    nb = s // tm

    def body(q_ref, k_ref, v_ref, gq_ref, gk_ref, qn_ref, kn_ref, vb_ref):
        j = pl.program_id(1)
        bd = _head_group_matrix()

        def norm(xv, g):
            ms = _dot_exact_rhs(xv * xv, bd) * (1.0 / ATTN_DH)
            return xv * lax.rsqrt(ms + RMS_EPS) * g

        @pl.when(j == 0)
        def _():
            kn_ref[...] = jnp.zeros_like(kn_ref)
            vb_ref[...] = jnp.zeros_like(vb_ref)

        @pl.when(j > 0)
        def _():
            qn_ref[...] = norm(q_ref[...], gq_ref[...]).astype(BF16)
            kn_ref[...] = norm(k_ref[...], gk_ref[...]).astype(BF16)
            vb_ref[...] = v_ref[...].astype(BF16)

    src_blk = lambda col: pl.BlockSpec((None, tm, ATTN_W), lambda bi, j: (bi, jnp.maximum(j - 1, 0), col))
    gspec = pl.BlockSpec((1, ATTN_W), lambda bi, j: (0, 0))
    padded = pl.BlockSpec((None, tm, ATTN_W), lambda bi, j: (bi, j, 0))
    return pl.pallas_call(
        body,
        name=name,
        grid=(b, nb + 1),
        in_specs=[src_blk(0), src_blk(1), src_blk(2), gspec, gspec],
        out_specs=[src_blk(0), padded, padded],
        out_shape=[jax.ShapeDtypeStruct((b, s, ATTN_W), BF16), jax.ShapeDtypeStruct((b, KPAD + s, ATTN_W), BF16),
                   jax.ShapeDtypeStruct((b, KPAD + s, ATTN_W), BF16)],
        compiler_params=_params("parallel", "arbitrary"),
    )(proj, proj, proj, gq, gk)


def _qk_prep_bwd(name, proj, dqn, dkn, dv, gq, gk):
    b, s, _ = proj.shape
    tm = KPAD
    nb = s // tm

    def body(q_ref, k_ref, dqn_ref, dkn_ref, dv_ref, gq_ref, gk_ref, dq_ref, dk_ref, dvb_ref, dgq_ref, dgk_ref):
        bd = _head_group_matrix()

        def bwd(xv, dy, g):
            ms = _dot_exact_rhs(xv * xv, bd) * (1.0 / ATTN_DH)
            rstd = lax.rsqrt(ms + RMS_EPS)
            xhat = xv * rstd
            dxhat = dy * g
            gm = _dot_exact_rhs(dxhat * xhat, bd) * (1.0 / ATTN_DH)
            return rstd * (dxhat - xhat * gm), jnp.sum(dy * xhat, axis=0, keepdims=True)

        dq, dgq = bwd(q_ref[...], dqn_ref[...], gq_ref[...])
        dk, dgk = bwd(k_ref[...], dkn_ref[...], gk_ref[...])
        dq_ref[...] = dq.astype(BF16)
        dk_ref[...] = dk.astype(BF16)
        dvb_ref[...] = dv_ref[...].astype(BF16)
        dgq_ref[...] = dgq
        dgk_ref[...] = dgk

    col = lambda c: pl.BlockSpec((None, tm, ATTN_W), lambda bi, j: (bi, j, c))
    past_pad = pl.BlockSpec((None, tm, ATTN_W), lambda bi, j: (bi, j + 1, 0))
    gspec = pl.BlockSpec((1, ATTN_W), lambda bi, j: (0, 0))
    pspec = pl.BlockSpec((None, 1, ATTN_W), lambda bi, j: (bi * nb + j, 0, 0))
    o_shape = jax.ShapeDtypeStruct((b, s, ATTN_W), BF16)
    p_shape = jax.ShapeDtypeStruct((b * nb, 1, ATTN_W), F32)
    return pl.pallas_call(
        body,
        name=name,
        grid=(b, nb),
        in_specs=[col(0), col(1), col(0), past_pad, past_pad, gspec, gspec],
        out_specs=[col(0)] * 3 + [pspec] * 2,
        out_shape=[o_shape] * 3 + [p_shape] * 2,
        compiler_params=_params("parallel", "parallel"),
    )(proj, proj, dqn, dkn, dv, gq, gk)


Q_CHUNKS = 4
QBLK = Q_CHUNKS * CHUNK
WIN = (LEFT_CHUNKS + Q_CHUNKS) * CHUNK
DB_W = BAND + CHUNK
MASKED = -1e30


def _band_table(bias):
    rows = [jnp.pad(bias, ((0, 0), (0, 0), (CHUNK * i, WIN - BAND - CHUNK * i)), constant_values=MASKED)
            for i in range(Q_CHUNKS)]
    return jnp.concatenate(rows, axis=1)


def _head_lanes(hh):
    lane = lax.broadcasted_iota(jnp.int32, (1, LANES), 1)
    return (lane < ATTN_DH) if hh == 0 else (lane >= ATTN_DH)


def _attn_probs(qh, kw, table, start):
    s = _dot(qh, kw, NT) * (ATTN_DH ** -0.5) + table
    col = lax.broadcasted_iota(jnp.int32, (QBLK, WIN), 1)
    s = jnp.where(col + start >= KPAD, s, MASKED)
    m = jnp.max(s, axis=-1, keepdims=True)
    p = jnp.exp(s - m)
    return p * (1.0 / jnp.sum(p, axis=-1, keepdims=True))


def _attn_fwd(name, q, k, v, table):
    b, s, w = q.shape
    sp = k.shape[1]

    def body(q_ref, k_ref, v_ref, t_ref, o_ref):
        start = pl.multiple_of(pl.program_id(2) * QBLK, QBLK)
        kw = k_ref[pl.ds(start, WIN), :]
        vw = v_ref[pl.ds(start, WIN), :]
        q2 = q_ref[...]
        out = jnp.zeros((QBLK, LANES), F32)
        for hh in range(2):
            mine = _head_lanes(hh)
            p = _attn_probs(jnp.where(mine, q2, jnp.zeros_like(q2)), kw, t_ref[hh], start)
            out = jnp.where(mine, _dot(p.astype(BF16), vw), out)
        o_ref[...] = out.astype(BF16)

    qspec = pl.BlockSpec((None, QBLK, LANES), lambda p, bi, i: (bi, i, p))
    kspec = pl.BlockSpec((None, sp, LANES), lambda p, bi, i: (bi, 0, p))
    return pl.pallas_call(
        body,
        name=name,
        grid=(w // LANES, b, s // QBLK),
        in_specs=[qspec, kspec, kspec, pl.BlockSpec((2, QBLK, WIN), lambda p, bi, i: (p, 0, 0))],
        out_specs=qspec,
        out_shape=jax.ShapeDtypeStruct((b, s, w), BF16),
        compiler_params=_params("parallel", "parallel", "arbitrary"),
    )(q, k, v, table)


def _attn_bwd(name, q, k, v, table, dmix):
    b, s, w = q.shape
    sp = k.shape[1]

    def body(q_ref, k_ref, v_ref, t_ref, do_ref, dq_ref, dk_ref, dv_ref, dbe_ref, dbo_ref):
        bi = pl.program_id(1)
        i = pl.program_id(2)
        start = pl.multiple_of(i * QBLK, QBLK)
        win = pl.ds(start, WIN)

        @pl.when(i == 0)
        def _():
            dk_ref[...] = jnp.zeros_like(dk_ref)
            dv_ref[...] = jnp.zeros_like(dv_ref)

        @pl.when(jnp.logical_and(i == 0, bi == 0))
        def _():
            dbe_ref[...] = jnp.zeros_like(dbe_ref)
            dbo_ref[...] = jnp.zeros_like(dbo_ref)

        kw = k_ref[win, :]
        vw = v_ref[win, :]
        q2 = q_ref[...]
        do2 = do_ref[...].astype(BF16)
        dq = jnp.zeros((QBLK, LANES), F32)
        for hh in range(2):
            mine = _head_lanes(hh)
            qh = jnp.where(mine, q2, jnp.zeros_like(q2))
            doh = jnp.where(mine, do2, jnp.zeros_like(do2))
            p = _attn_probs(qh, kw, t_ref[hh], start)
            dp = _dot(doh, vw, NT)
            ds = p * (dp - jnp.sum(p * dp, axis=-1, keepdims=True))
            for qi in range(Q_CHUNKS):
                c0 = (qi // 2) * LANES
                blk = ds[qi * CHUNK:(qi + 1) * CHUNK, c0:c0 + DB_W]
                if qi % 2 == 0:
                    dbe_ref[hh] += blk
                else:
                    dbo_ref[hh] += blk
            dsb = (ds * (ATTN_DH ** -0.5)).astype(BF16)
            dq = jnp.where(mine, _dot(dsb, kw), dq)
            dk_ref[win, :] += _dot(dsb, qh, TN)
            dv_ref[win, :] += _dot(p.astype(BF16), doh, TN)
        dq_ref[...] = dq

    qspec = pl.BlockSpec((None, QBLK, LANES), lambda p, bi, i: (bi, i, p))
    kspec = pl.BlockSpec((None, sp, LANES), lambda p, bi, i: (bi, 0, p))
    dbspec = pl.BlockSpec((2, CHUNK, DB_W), lambda p, bi, i: (p, 0, 0))
    db_shape = jax.ShapeDtypeStruct((ATTN_HEADS, CHUNK, DB_W), F32)
    return pl.pallas_call(
        body,
        name=name,
        grid=(w // LANES, b, s // QBLK),
        in_specs=[qspec, kspec, kspec, pl.BlockSpec((2, QBLK, WIN), lambda p, bi, i: (p, 0, 0)), qspec],
        out_specs=[qspec, kspec, kspec, dbspec, dbspec],
        out_shape=[jax.ShapeDtypeStruct((b, s, w), F32), jax.ShapeDtypeStruct((b, sp, w), F32),
                   jax.ShapeDtypeStruct((b, sp, w), F32), db_shape, db_shape],
        compiler_params=_params("arbitrary", "arbitrary", "arbitrary"),
    )(q, k, v, table, dmix)


HQ_COL = 3 * ATTN_W // HGRN_DH
HF_COL = HQ_COL + HGRN_HEADS
HI_COL = HF_COL + HGRN_HEADS
HG_COL = HI_COL + HGRN_HEADS


def _tri(lower):
    r = lax.broadcasted_iota(jnp.int32, (CHUNK, CHUNK), 0)
    c = lax.broadcasted_iota(jnp.int32, (CHUNK, CHUNK), 1)
    return (r >= c) if lower else (r <= c)


def _hgrn_chunk(hq, hf, lb, tril):
    sig = _sigmoid(hf)
    f = lb + (1.0 - lb) * sig
    g = jnp.log(f)
    ones_l = jnp.where(tril, 1.0, 0.0).astype(BF16)
    b = _dot_exact_lhs(ones_l, g)
    bl = jnp.sum(g, axis=0, keepdims=True)
    rows = lax.broadcasted_iota(jnp.int32, g.shape, 0)
    bm = jnp.sum(jnp.where(rows <= CHUNK // 2, g, 0.0), axis=0, keepdims=True)
    sq = _sigmoid(hq)
    q = hq * sq
    k = 1.0 - f
    return sig, f, b, bl, bm, sq, q, k


def _hgrn_fwd(name, proj, lb, go, b, s):
    nc = s // CHUNK
    t = b * s

    def body(hq_ref, hf_ref, hi_ref, hg_ref, lb_ref, go_ref, ro_ref, oraw_ref, st_ref, s_scr):
        tril = _tri(True)
        lbv = lb_ref[...]
        gov = go_ref[...]
        s_scr[...] = jnp.zeros_like(s_scr)

        def step(c, carry):
            sl = pl.ds(pl.multiple_of(c * CHUNK, CHUNK), CHUNK)
            hg = hg_ref[sl, :]
            _, _, bb, bl, bm, _, q, k = _hgrn_chunk(hq_ref[sl, :], hf_ref[sl, :], lbv, tril)
            vb = hi_ref[sl, :].astype(BF16)
            qe = (q * jnp.exp(bb - bm)).astype(BF16)
            ke = (k * jnp.exp(bm - bb)).astype(BF16)
            a = jnp.where(tril, _dot(qe, ke, NT), 0.0)
            st = s_scr[...]
            st_ref[c] = st
            qb = (q * jnp.exp(bb)).astype(BF16)
            o = _dot(a.astype(BF16), vb) + _dot(qb, st.astype(BF16), NT)
            kb = (k * jnp.exp(bl - bb)).astype(BF16)
            s_scr[...] = st * jnp.exp(bl) + _dot(vb, kb, TN)
            rstd = lax.rsqrt(jnp.mean(o * o, axis=-1, keepdims=True) + RMS_EPS)
            ro_ref[sl, :] = ((o * rstd * gov) * _silu(hg)).astype(BF16)
            oraw_ref[sl, :] = o
            return carry

        lax.fori_loop(0, nc, step, 0)

    col = lambda base: pl.BlockSpec((s, HGRN_DH), lambda bi, h: (bi, base + h))
    vec = pl.BlockSpec((1, HGRN_DH), lambda bi, h: (0, h))
    out = pl.BlockSpec((s, HGRN_DH), lambda bi, h: (bi, h))
    return pl.pallas_call(
        body,
        name=name,
        grid=(b, HGRN_HEADS),
        in_specs=[col(HQ_COL), col(HF_COL), col(HI_COL), col(HG_COL), vec,
                  pl.BlockSpec((1, HGRN_DH), lambda bi, h: (0, 0))],
        out_specs=[out, out,
                   pl.BlockSpec((None, None, nc, HGRN_DH, HGRN_DH), lambda bi, h: (bi, h, 0, 0, 0))],
        out_shape=[jax.ShapeDtypeStruct((t, HGRN_W), BF16), jax.ShapeDtypeStruct((t, HGRN_W), F32),
                   jax.ShapeDtypeStruct((b, HGRN_HEADS, nc, HGRN_DH, HGRN_DH), F32)],
        scratch_shapes=[pltpu.VMEM((HGRN_DH, HGRN_DH), F32)],
        compiler_params=_params("parallel", "parallel"),
    )(proj, proj, proj, proj, lb, go)


def _hgrn_bwd(name, proj, lb, go, oraw, states, dmix, b, s):
    nc = s // CHUNK
    t = b * s

    def body(hq_ref, hf_ref, hi_ref, hg_ref, lb_ref, go_ref, oraw_ref, st_ref, dro_ref,
             dhq_ref, dhf_ref, dhi_ref, dhg_ref, dlb_ref, dgo_ref, ds_scr, dlb_scr, dgo_scr):
        tril = _tri(True)
        ones_u = jnp.where(_tri(False), 1.0, 0.0).astype(BF16)
        lbv = lb_ref[...]
        gov = go_ref[...]
        ds_scr[...] = jnp.zeros_like(ds_scr)
        dlb_scr[...] = jnp.zeros_like(dlb_scr)
        dgo_scr[...] = jnp.zeros_like(dgo_scr)

        def step(ci, carry):
            c = nc - 1 - ci
            sl = pl.ds(pl.multiple_of(c * CHUNK, CHUNK), CHUNK)
            hq = hq_ref[sl, :]
            hg = hg_ref[sl, :]
            sig, f, bb, bl, bm, sq, q, k = _hgrn_chunk(hq, hf_ref[sl, :], lbv, tril)
            vb = hi_ref[sl, :].astype(BF16)
            ebm = jnp.exp(bb - bm)
            embm = jnp.exp(bm - bb)
            eb = jnp.exp(bb)
            ebl = jnp.exp(bl - bb)
            e_last = jnp.exp(bl)
            qe = (q * ebm).astype(BF16)
            ke = (k * embm).astype(BF16)
            qb = (q * eb).astype(BF16)
            kb = (k * ebl).astype(BF16)
            a = jnp.where(tril, _dot(qe, ke, NT), 0.0)
            st = st_ref[c]
            dst = ds_scr[...]
            o = oraw_ref[sl, :]
            dro = dro_ref[sl, :]
            sg = _sigmoid(hg)
            rstd = lax.rsqrt(jnp.mean(o * o, axis=-1, keepdims=True) + RMS_EPS)
            ohat = o * rstd
            dn = dro * (hg * sg)
            dhg_ref[sl, :] = (dro * (ohat * gov) * (sg * (1.0 + hg * (1.0 - sg)))).astype(BF16)
            dgo_scr[...] += jnp.sum(dn * ohat, axis=0, keepdims=True)
            dohat = dn * gov
            do = rstd * (dohat - ohat * jnp.mean(dohat * ohat, axis=-1, keepdims=True))
            dob = do.astype(BF16)
            dab = jnp.where(tril, _dot(dob, vb, NT), 0.0).astype(BF16)
            stb = st.astype(BF16)
            dstb = dst.astype(BF16)
            dv = _dot(a.astype(BF16), dob, TN) + _dot(kb, dstb, NT)
            dqe = _dot(dab, ke)
            dke = _dot(dab, qe, TN)
            dqb = _dot(dob, stb)
            dkb = _dot(vb, dstb)
            dq = dqe * ebm + dqb * eb
            dk = dke * embm + dkb * ebl
            db = (qe.astype(F32) * dqe - ke.astype(F32) * dke) + q * (dqb * eb) - k * (dkb * ebl)
            d_last = (jnp.sum(k * ebl * dkb, axis=0, keepdims=True)
                      + jnp.sum(dst * st, axis=0, keepdims=True) * e_last)
            dg = _dot_exact_lhs(ones_u, db) + d_last
            df = dg / f - dk
            dhf_ref[sl, :] = (df * (1.0 - lbv) * sig * (1.0 - sig)).astype(BF16)
            dlb_scr[...] += jnp.sum(df * (1.0 - sig), axis=0, keepdims=True)
            dhq_ref[sl, :] = (dq * (sq * (1.0 + hq * (1.0 - sq)))).astype(BF16)
            dhi_ref[sl, :] = dv.astype(BF16)
            ds_scr[...] = dst * e_last + _dot(dob, qb, TN)
            return carry

        lax.fori_loop(0, nc, step, 0)
        dlb_ref[...] = dlb_scr[...]
        dgo_ref[...] = dgo_scr[...]

    col = lambda base: pl.BlockSpec((s, HGRN_DH), lambda bi, h: (bi, base + h))
    vec = pl.BlockSpec((1, HGRN_DH), lambda bi, h: (0, h))
    out = pl.BlockSpec((s, HGRN_DH), lambda bi, h: (bi, h))
    part = pl.BlockSpec((None, 1, HGRN_DH), lambda bi, h: (bi, 0, h))
    o_shape = jax.ShapeDtypeStruct((t, HGRN_W), BF16)
    p_shape = jax.ShapeDtypeStruct((b, 1, HGRN_W), F32)
    return pl.pallas_call(
        body,
        name=name,
        grid=(b, HGRN_HEADS),
        in_specs=[col(HQ_COL), col(HF_COL), col(HI_COL), col(HG_COL), vec,
                  pl.BlockSpec((1, HGRN_DH), lambda bi, h: (0, 0)), out,
                  pl.BlockSpec((None, None, nc, HGRN_DH, HGRN_DH), lambda bi, h: (bi, h, 0, 0, 0)),
                  col(ATTN_W // HGRN_DH)],
        out_specs=[out] * 4 + [part] * 2,
        out_shape=[o_shape] * 4 + [p_shape] * 2,
        scratch_shapes=[pltpu.VMEM((HGRN_DH, HGRN_DH), F32), pltpu.VMEM((1, HGRN_DH), F32),
                        pltpu.VMEM((1, HGRN_DH), F32)],
        compiler_params=_params("parallel", "parallel"),
    )(proj, proj, proj, proj, lb, go, oraw, states, dmix)


def _small_grads(name, dg1, dgm, dg2, dgq, dgk, dbias_t, dlb, dgo, lbp):
    d = dg1.shape[1]

    def body(dg1_ref, dgm_ref, dg2_ref, dgq_ref, dgk_ref, dbias_ref, dlb_ref, dgo_ref, lbp_ref,
             g1_ref, gm_ref, g2_ref, gq_ref, gk_ref, rb_ref, lbg_ref, go_ref):
        g1_ref[...] = jnp.sum(dg1_ref[...], axis=0, keepdims=True)
        gm_ref[...] = jnp.sum(dgm_ref[...], axis=0, keepdims=True)
        g2_ref[...] = jnp.sum(dg2_ref[...], axis=0, keepdims=True)
        r = lax.broadcasted_iota(jnp.int32, (ATTN_W, ATTN_DH), 0)
        cidx = lax.broadcasted_iota(jnp.int32, (ATTN_W, ATTN_DH), 1)
        fold = jnp.where(jnp.bitwise_and(r, ATTN_DH - 1) == cidx, 1.0, 0.0).astype(BF16)
        gq_ref[...] = jnp.sum(_dot_exact_rhs(dgq_ref[...], fold), axis=0, keepdims=True)
        gk_ref[...] = jnp.sum(_dot_exact_rhs(dgk_ref[...], fold), axis=0, keepdims=True)
        gosum = jnp.sum(dgo_ref[...], axis=0, keepdims=True)
        go_ref[...] = (gosum[:, 0:HGRN_DH] + gosum[:, HGRN_DH:2 * HGRN_DH]
                       + gosum[:, 2 * HGRN_DH:3 * HGRN_DH] + gosum[:, 3 * HGRN_DH:4 * HGRN_DH])
        p0 = lbp_ref[0:1, :]
        p1 = lbp_ref[1:2, :]
        lbv = 1.0 / (1.0 + jnp.exp(p1 - p0))
        dp0 = jnp.sum(dlb_ref[...], axis=0, keepdims=True) * lbv * (1.0 - lbv)
        lbg_ref[0:1, :] = dp0
        lbg_ref[1:2, :] = -dp0
        sidx = lax.broadcasted_iota(jnp.int32, (BAND, N_REL_PAD), 0)
        ridx = lax.broadcasted_iota(jnp.int32, (BAND, N_REL_PAD), 1)

        def step(tq, acc):
            rel = jnp.clip(tq + KPAD - sidx, -REL_CLIP, REL_CLIP) + REL_CLIP
            onehot = jnp.where(rel == ridx, 1.0, 0.0).astype(BF16)
            return acc + _dot_exact_rhs(dbias_ref[tq], onehot)

        rb_ref[...] = lax.fori_loop(0, CHUNK, step, jnp.zeros((ATTN_HEADS, N_REL_PAD), F32))

    ins = [dg1, dgm, dg2, dgq, dgk, dbias_t, dlb, dgo, lbp]
    outs = [jax.ShapeDtypeStruct((1, d), F32)] * 3 + [jax.ShapeDtypeStruct((1, ATTN_DH), F32)] * 2 + [
        jax.ShapeDtypeStruct((ATTN_HEADS, N_REL_PAD), F32), jax.ShapeDtypeStruct((2, HGRN_W), F32),
        jax.ShapeDtypeStruct((1, HGRN_DH), F32)]
    vm = pl.BlockSpec(memory_space=pltpu.VMEM)
    return pl.pallas_call(
        body,
        name=name,
        in_specs=[vm] * len(ins),
        out_specs=[vm] * len(outs),
        out_shape=outs,
        compiler_params=pltpu.CompilerParams(vmem_limit_bytes=VMEM_LIMIT),
    )(*ins)


def _adam_update(w, g, m, v):
    m2 = ADAM_B1 * m + (1.0 - ADAM_B1) * g
    v2 = ADAM_B2 * v + (1.0 - ADAM_B2) * (g * g)
    m_hat = m2 / (1.0 - ADAM_B1 ** ADAM_STEP)
    v_hat = v2 / (1.0 - ADAM_B2 ** ADAM_STEP)
    delta = -ADAM_LR * (m_hat / (jnp.sqrt(v_hat) + ADAM_EPS) + ADAM_WD * w)
    return delta, m2, v2


def _rows_tile(r):
    for cand in (256, 352, 128, 176, 64, 32, 16):
        if r % cand == 0 and r > cand:
            return cand
    return r


def _pair_sum(name, grad, theirs, core):
    n, half, c = theirs.shape
    tr = _rows_tile(half)
    nth = half // tr

    def body(core_ref, a_ref, b_ref, o_ref):
        o_ref[...] = (a_ref[...] + b_ref[...]).astype(o_ref.dtype)

    spec = pl.BlockSpec((None, tr, c), lambda i, j, core_ref: (i, j, 0))
    return pl.pallas_call(
        body, name=name,
        grid_spec=pltpu.PrefetchScalarGridSpec(
            num_scalar_prefetch=1, grid=(n, nth),
            in_specs=[pl.BlockSpec((None, tr, c), lambda i, j, core_ref: (i, core_ref[0] * nth + j, 0)), spec],
            out_specs=spec),
        out_shape=jax.ShapeDtypeStruct((n, half, c), BF16), compiler_params=_params("parallel", "parallel"),
    )(core, grad, theirs)


def _chip_sum(name, own, parts, chip):
    _, half, c = own.shape
    tr = _rows_tile(half)

    def body(chip_ref, own_ref, p_ref, o_ref):
        me = chip_ref[0]
        mine = own_ref[...].astype(F32)
        flip_x, flip_y, flip_xy = (p_ref[i].astype(F32) for i in range(3))
        acc = None
        for k in range(N_CHIPS):
            rel = jnp.bitwise_xor(me, k)
            term = jnp.where(rel == 0, mine, jnp.where(rel == 2, flip_x, jnp.where(rel == 1, flip_y, flip_xy)))
            acc = term if acc is None else acc + term
        o_ref[...] = acc

    return pl.pallas_call(
        body, name=name,
        grid_spec=pltpu.PrefetchScalarGridSpec(
            num_scalar_prefetch=1, grid=(half // tr,),
            in_specs=[pl.BlockSpec((None, tr, c), lambda j, chip_ref: (chip_ref[0], j, 0)),
                      pl.BlockSpec((3, tr, c), lambda j, chip_ref: (0, j, 0))],
            out_specs=pl.BlockSpec((tr, c), lambda j, chip_ref: (j, 0))),
        out_shape=jax.ShapeDtypeStruct((half, c), F32), compiler_params=_params("parallel"),
    )(chip, own, parts)


def _adamw(name, w, g_mine, g_theirs, m, v, core):
    _, r, c = w.shape
    half = r // 2
    tr = _rows_tile(half)
    nth = half // tr

    def body(core_ref, w_ref, gm_ref, gt_ref, m_ref, v_ref, g_ref, d_ref, m2_ref, v2_ref):
        g = jnp.where(pl.program_id(0) == core_ref[0], gm_ref[...], gt_ref[...])
        delta, m2, v2 = _adam_update(w_ref[...], g, m_ref[...], v_ref[...])
        g_ref[...] = g
        d_ref[...] = delta
        m2_ref[...] = m2
        v2_ref[...] = v2

    full = pl.BlockSpec((None, tr, c), lambda h, j, core_ref: (0, h * nth + j, 0))
    part = pl.BlockSpec((tr, c), lambda h, j, core_ref: (j, 0))
    shape = jax.ShapeDtypeStruct((1, r, c), F32)
    return pl.pallas_call(
        body, name=name,
        grid_spec=pltpu.PrefetchScalarGridSpec(
            num_scalar_prefetch=1, grid=(2, nth), in_specs=[full, part, part, full, full], out_specs=[full] * 4),
        out_shape=[shape] * 4, compiler_params=_params("parallel", "parallel"),
    )(core, w, g_mine, g_theirs, m, v)


def _rel_bias_table(name, rel_bias):
    padded = jnp.pad(rel_bias, ((0, 0), (0, N_REL_PAD - N_REL)))

    def body(rb_ref, o_ref):
        ridx = lax.broadcasted_iota(jnp.int32, (N_REL_PAD, BAND), 0)
        sidx = lax.broadcasted_iota(jnp.int32, (N_REL_PAD, BAND), 1)
        rb = rb_ref[...]

        def step(tq, carry):
            rel = jnp.clip(tq + KPAD - sidx, -REL_CLIP, REL_CLIP) + REL_CLIP
            onehot = jnp.where(rel == ridx, 1.0, 0.0).astype(BF16)
            o_ref[tq] = _dot_exact_rhs(rb, onehot)
            return carry

        lax.fori_loop(0, CHUNK, step, 0)

    vm = pl.BlockSpec(memory_space=pltpu.VMEM)
    table = pl.pallas_call(
        body, name=name, in_specs=[vm], out_specs=vm,
        out_shape=jax.ShapeDtypeStruct((CHUNK, ATTN_HEADS, BAND), F32),
    )(padded)
    return table.transpose(1, 0, 2)


def _adamw_small(name, w, parts, m, v):
    def body(w_ref, p_ref, m_ref, v_ref, g_ref, d_ref, m2_ref, v2_ref):
        g = p_ref[0]
        for i in range(1, N_DEV):
            g = g + p_ref[i]
        delta, m2, v2 = _adam_update(w_ref[...], g, m_ref[...], v_ref[...])
        g_ref[...] = g
        d_ref[...] = delta
        m2_ref[...] = m2
        v2_ref[...] = v2

    vm = pl.BlockSpec(memory_space=pltpu.VMEM)
    shape = jax.ShapeDtypeStruct((SMALL_ROWS, SMALL_COLS), F32)
    return pl.pallas_call(
        body, name=name, in_specs=[vm] * 4, out_specs=[vm] * 4, out_shape=[shape] * 4,
    )(w, parts, m, v)


def _position():
    return lax.axis_index("x"), lax.axis_index("y"), lax.axis_index("c")


def _other_chips(x, y):
    return [(1 - x, y), (x, 1 - y), (1 - x, 1 - y)]


ANY = pl.BlockSpec(memory_space=pl.ANY)


HBM = pl.BlockSpec(memory_space=pltpu.HBM)
SEM = pl.BlockSpec(memory_space=pltpu.SEMAPHORE)
SPLIT_COPY = pltpu.SideEffectType.DATAFLOW_SIDE_EFFECTING


def _gather_copy(shards, outs, send_sem, recv_sem, i, j):
    x, y, c = _position()
    chips = _other_chips(x, y)
    half = shards[i].shape[0] // 2
    rows = pl.ds(pl.multiple_of(c * half, 16), half)
    return pltpu.make_async_remote_copy(
        src_ref=shards[i].at[rows, :], dst_ref=outs[i].at[2 * x + y, rows, :],
        send_sem=send_sem.at[3 * i + j], recv_sem=recv_sem.at[3 * i + j],
        device_id=(chips[j][0], chips[j][1], c), device_id_type=MESH)


def _gather_start(name, shards, after):
    n = len(shards)

    def body(*refs):
        srcs, outs = refs[:n], refs[n:2 * n]
        send_sem, recv_sem = refs[2 * n + len(after)], refs[2 * n + len(after) + 1]
        token = refs[-1]
        for i in range(n):
            for j in range(3):
                _gather_copy(srcs, outs, send_sem, recv_sem, i, j).start()
        token[...] = jnp.zeros_like(token)

    full = [(N_CHIPS,) + s.shape for s in shards]
    res = pl.pallas_call(
        body,
        name=name,
        in_specs=[HBM] * (2 * n) + [ANY] * len(after),
        out_specs=[SEM, SEM] + [HBM] * (2 * n) + [pl.BlockSpec(memory_space=pltpu.VMEM)],
        out_shape=[pltpu.SemaphoreType.DMA((3 * n,)), pltpu.SemaphoreType.DMA((3 * n,))]
        + [pltpu.HBM(s.shape, s.dtype) for s in shards]
        + [pltpu.HBM(shp, s.dtype) for shp, s in zip(full, shards)]
        + [jax.ShapeDtypeStruct((8, LANES), F32)],
        input_output_aliases={i: 2 + i for i in range(2 * n)},
        compiler_params=pltpu.CompilerParams(has_side_effects=SPLIT_COPY),
    )(*[pltpu.with_memory_space_constraint(s, pltpu.HBM) for s in shards],
      *[pltpu.with_memory_space_constraint(lax.empty(shp, s.dtype), pltpu.HBM) for shp, s in zip(full, shards)],
      *after)
    return res[0], res[1], list(res[2:2 + n]), list(res[2 + n:2 + 2 * n]), res[-1]


def _gather_wait(name, send_sem, recv_sem, shards, outs, after):
    n = len(shards)

    def body(*refs):
        srcs, out_refs = refs[:n], refs[n:2 * n]
        send_ref, recv_ref = refs[2 * n], refs[2 * n + 1]
        for i in range(n):
            for j in range(3):
                copy = _gather_copy(srcs, out_refs, send_ref, recv_ref, i, j)
                copy.wait_send()
                copy.wait_recv()

    res = pl.pallas_call(
        body,
        name=name,
        in_specs=[HBM] * (2 * n) + [SEM, SEM, ANY],
        out_specs=[HBM] * (2 * n),
        out_shape=[pltpu.HBM(s.shape, s.dtype) for s in shards] + [pltpu.HBM(o.shape, o.dtype) for o in outs],
        input_output_aliases={i: i for i in range(2 * n)},
        compiler_params=pltpu.CompilerParams(has_side_effects=SPLIT_COPY),
    )(*shards, *outs, send_sem, recv_sem, after)
    return list(res[:n]), list(res[n:])


def _gather_join(name, shards, outs):
    n = len(shards)

    def body(*refs):
        srcs, ins, outs_ = refs[:n], refs[n:2 * n], refs[2 * n:3 * n]
        own_send, own_recv, half_send, half_recv = refs[3 * n:]
        x, y, c = _position()
        chips = _other_chips(x, y)
        copies = []
        for i in range(n):
            copies.append(pltpu.make_async_remote_copy(
                src_ref=srcs[i], dst_ref=outs_[i].at[2 * x + y], send_sem=own_send.at[i], recv_sem=own_recv.at[i],
                device_id=(x, y, 1 - c), device_id_type=MESH))
            half = srcs[i].shape[0] // 2
            rows = pl.ds(pl.multiple_of(c * half, 16), half)
            for j in range(3):
                slot = 2 * chips[j][0] + chips[j][1]
                copies.append(pltpu.make_async_remote_copy(
                    src_ref=ins[i].at[slot, rows, :], dst_ref=outs_[i].at[slot, rows, :],
                    send_sem=half_send.at[3 * i + j], recv_sem=half_recv.at[3 * i + j],
                    device_id=(x, y, 1 - c), device_id_type=MESH))
        for cp in copies:
            cp.start()
        for cp in copies:
            cp.wait()

    return pl.pallas_call(
        body,
        name=name,
        in_specs=[ANY] * (2 * n),
        out_specs=[ANY] * n,
        out_shape=[jax.ShapeDtypeStruct(o.shape, o.dtype) for o in outs],
        input_output_aliases={n + i: i for i in range(n)},
        scratch_shapes=[pltpu.SemaphoreType.DMA((n,))] * 2 + [pltpu.SemaphoreType.DMA((3 * n,))] * 2,
    )(*shards, *outs)


def _pair_exchange(name, grads):
    n = len(grads)

    def body(*refs):
        ins, theirs = refs[:n], refs[n:2 * n]
        send_sem, recv_sem = refs[2 * n:]
        x, y, c = _position()
        copies = []
        for i in range(n):
            half = ins[i].shape[1] // 2
            give = pl.ds(pl.multiple_of((1 - c) * half, 8), half)
            swap = pltpu.make_async_remote_copy(
                src_ref=ins[i].at[:, give, :], dst_ref=theirs[i], send_sem=send_sem.at[i], recv_sem=recv_sem.at[i],
                device_id=(x, y, 1 - c), device_id_type=MESH)
            swap.start()
            copies.append(swap)
        for swap in copies:
            swap.wait()

    return pl.pallas_call(
        body,
        name=name,
        in_specs=[ANY] * n,
        out_specs=[ANY] * n,
        out_shape=[jax.ShapeDtypeStruct((g.shape[0], g.shape[1] // 2, g.shape[2]), g.dtype) for g in grads],
        scratch_shapes=[pltpu.SemaphoreType.DMA((n,))] * 2,
    )(*grads)


def _scatter_copy(srcs, lands, send_sem, recv_sem, i, j):
    x, y, c = _position()
    chips = _other_chips(x, y)
    return pltpu.make_async_remote_copy(
        src_ref=srcs[i].at[2 * chips[j][0] + chips[j][1]], dst_ref=lands[i].at[j],
        send_sem=send_sem.at[3 * i + j], recv_sem=recv_sem.at[3 * i + j],
        device_id=(chips[j][0], chips[j][1], c), device_id_type=MESH)


def _scatter_start(name, sums):
    n = len(sums)

    def body(*refs):
        srcs, lands = refs[:n], refs[n:2 * n]
        send_sem, recv_sem = refs[2 * n], refs[2 * n + 1]
        token = refs[-1]
        for i in range(n):
            for j in range(3):
                _scatter_copy(srcs, lands, send_sem, recv_sem, i, j).start()
        token[...] = jnp.zeros_like(token)

    land_shapes = [(3,) + s.shape[1:] for s in sums]
    res = pl.pallas_call(
        body,
        name=name,
        in_specs=[HBM] * (2 * n),
        out_specs=[SEM, SEM] + [HBM] * (2 * n) + [pl.BlockSpec(memory_space=pltpu.VMEM)],
        out_shape=[pltpu.SemaphoreType.DMA((3 * n,)), pltpu.SemaphoreType.DMA((3 * n,))]
        + [pltpu.HBM(s.shape, s.dtype) for s in sums]
        + [pltpu.HBM(shp, s.dtype) for shp, s in zip(land_shapes, sums)]
        + [jax.ShapeDtypeStruct((8, LANES), F32)],
        input_output_aliases={i: 2 + i for i in range(2 * n)},
        compiler_params=pltpu.CompilerParams(has_side_effects=SPLIT_COPY),
    )(*[pltpu.with_memory_space_constraint(s, pltpu.HBM) for s in sums],
      *[pltpu.with_memory_space_constraint(lax.empty(shp, s.dtype), pltpu.HBM) for shp, s in zip(land_shapes, sums)])
    return res[0], res[1], list(res[2:2 + n]), list(res[2 + n:2 + 2 * n]), res[-1]


def _scatter_wait(name, send_sem, recv_sem, sums, lands, after):
    n = len(sums)

    def body(*refs):
        srcs, land_refs = refs[:n], refs[n:2 * n]
        send_ref, recv_ref = refs[2 * n], refs[2 * n + 1]
        for i in range(n):
            for j in range(3):
                copy = _scatter_copy(srcs, land_refs, send_ref, recv_ref, i, j)
                copy.wait_send()
                copy.wait_recv()

    res = pl.pallas_call(
        body,
        name=name,
        in_specs=[HBM] * (2 * n) + [SEM, SEM, ANY],
        out_specs=[HBM] * (2 * n),
        out_shape=[pltpu.HBM(s.shape, s.dtype) for s in sums] + [pltpu.HBM(l.shape, l.dtype) for l in lands],
        input_output_aliases={i: i for i in range(2 * n)},
        compiler_params=pltpu.CompilerParams(has_side_effects=SPLIT_COPY),
    )(*sums, *lands, send_sem, recv_sem, after)
    return list(res[:n]), list(res[n:])


def _pair_join(name, halves, small=None):
    n = len(halves)
    if small is None:
        def body_plain(*refs):
            ins, outs = refs[:n], refs[n:2 * n]
            send_sem, recv_sem = refs[2 * n:]
            x, y, c = _position()
            swaps = [pltpu.make_async_remote_copy(
                src_ref=ins[i], dst_ref=outs[i], send_sem=send_sem.at[i], recv_sem=recv_sem.at[i],
                device_id=(x, y, 1 - c), device_id_type=MESH) for i in range(n)]
            for swap in swaps:
                swap.start()
            for swap in swaps:
                swap.wait()

        return pl.pallas_call(
            body_plain,
            name=name,
            in_specs=[ANY] * n,
            out_specs=[ANY] * n,
            out_shape=[jax.ShapeDtypeStruct(h.shape, h.dtype) for h in halves],
            scratch_shapes=[pltpu.SemaphoreType.DMA((n,))] * 2,
        )(*halves)

    def body(*refs):
        ins, small_ref = refs[:n], refs[n]
        outs, all_ref = refs[n + 1:2 * n + 1], refs[2 * n + 1]
        send_sem, recv_sem, sm_send, sm_recv, sm_local = refs[2 * n + 2:]
        x, y, c = _position()
        swaps = []
        for i in range(n):
            swap = pltpu.make_async_remote_copy(
                src_ref=ins[i], dst_ref=outs[i], send_sem=send_sem.at[i], recv_sem=recv_sem.at[i],
                device_id=(x, y, 1 - c), device_id_type=MESH)
            swap.start()
            swaps.append(swap)
        me = 4 * x + 2 * y + c
        sm_own = pltpu.make_async_copy(small_ref, all_ref.at[me], sm_local)
        sm_own.start()
        pushes, arrivals = [], []
        for mask in range(1, N_DEV):
            px, py, pc = x ^ (mask >> 2), y ^ ((mask >> 1) & 1), c ^ (mask & 1)
            pushes.append(pltpu.make_async_remote_copy(
                src_ref=small_ref, dst_ref=all_ref.at[me], send_sem=sm_send.at[mask - 1], recv_sem=sm_recv.at[mask - 1],
                device_id=(px, py, pc), device_id_type=MESH))
            arrivals.append(pltpu.make_async_remote_copy(
                src_ref=small_ref, dst_ref=all_ref.at[4 * px + 2 * py + pc], send_sem=sm_send.at[mask - 1],
                recv_sem=sm_recv.at[mask - 1], device_id=(px, py, pc), device_id_type=MESH))
        for cp in pushes:
            cp.start()
        for swap in swaps:
            swap.wait()
        for cp in arrivals:
            cp.wait_recv()
        for cp in pushes:
            cp.wait_send()
        sm_own.wait()

    res = pl.pallas_call(
        body,
        name=name,
        in_specs=[ANY] * (n + 1),
        out_specs=[ANY] * (n + 1),
        out_shape=[jax.ShapeDtypeStruct(h.shape, h.dtype) for h in halves]
        + [jax.ShapeDtypeStruct((N_DEV,) + small.shape, small.dtype)],
        scratch_shapes=[pltpu.SemaphoreType.DMA((n,))] * 2 + [pltpu.SemaphoreType.DMA((N_DEV - 1,))] * 2
        + [pltpu.SemaphoreType.DMA(())],
    )(*halves, small)
    return res[:n], res[n]


def _lower_bound(lbp):
    return jax.nn.softmax(lbp, axis=0)[0:1]


def _local_step(x, target, g1, gm, g2, gq, gk, go, rel_bias, lbp, first_weights, rest_weights, on_grads):
    b, s, d = x.shape
    t = b * s
    x0 = x.reshape(t, d)
    tgt = target.reshape(t, d)
    gq_t = jnp.tile(gq, (1, ATTN_HEADS))
    gk_t = jnp.tile(gk, (1, ATTN_HEADS))
    lb = _lower_bound(lbp)
    bias = _rel_bias_table("rel_bias_table", rel_bias)

    h1 = _rmsnorm_fwd("norm1", x0, g1)
    wg1, wu1, wd1, deps1 = first_weights(h1)
    a1, b1, z1 = _ffn_up("ffn1_up", h1, wg1, wu1, deps1)
    x1 = _ffn_down("ffn1_down", z1, wd1, x0)
    w_in, w_out, wg2, wu2, wd2 = rest_weights(x1)
    ns = w_in.shape[0]
    h2 = _rmsnorm_fwd("norm_mix", x1, gm)
    proj = _in_proj("in_proj", h2, w_in)
    proj3 = proj.reshape(b, s, proj.shape[1])
    table = _band_table(bias)
    qn, kn, vb = _qk_prep("qk_prep", proj3, gq_t, gk_t)
    attn = _attn_fwd("attn_fwd", qn, kn, vb, table).reshape(t, ATTN_W)
    ro, oraw, states = _hgrn_fwd("hgrn_fwd", proj, lb, go, b, s)
    mix = jnp.concatenate([attn, ro], axis=1)
    x2 = _out_proj("out_proj", mix, w_out, x1)
    h3 = _rmsnorm_fwd("norm2", x2, g2)
    a2, b2, z2 = _ffn_up("ffn2_up", h3, wg2, wu2)
    dy, dyh, sq = _ffn_down_loss("ffn2_down_loss", z2, wd2, x2, tgt)
    loss = 0.5 * jnp.sum(sq) / d

    da2, db2 = _ffn_bwd_act("ffn2_bwd_act", dyh, wd2, a2, b2)
    dwd2 = _grad_w_shardrows("ffn2_dwd", z2, dyh)
    dwg2 = _grad_w_shardcols("ffn2_dwg", h3, da2)
    dwu2 = _grad_w_shardcols("ffn2_dwu", h3, db2)
    sent2 = on_grads("ffn2", {"ffn2_w_gate": dwg2, "ffn2_w_up": dwu2, "ffn2_w_down": dwd2})
    dx2, dx2b, dg2 = _ffn_bwd_in("ffn2_bwd_in", da2, db2, wg2, wu2, x2, g2, dy, 1.0)

    dwout = _grad_w_out("dw_out", mix, dx2b)
    dmix = _out_proj_bwd("out_proj_bwd", dx2b, w_out, sent2)
    dqn, dkn, dvn, dbe, dbo = _attn_bwd("attn_bwd", qn, kn, vb, table, dmix.reshape(b, s, dmix.shape[1]))
    dbias = dbe[:, :, :BAND] + dbo[:, :, CHUNK:]
    dpq, dpk, dpv, dgq, dgk = _qk_prep_bwd("qk_prep_bwd", proj3, dqn, dkn, dvn, gq_t, gk_t)
    dpq, dpk, dpv = (a.reshape(t, ATTN_W) for a in (dpq, dpk, dpv))
    dhq, dhf, dhi, dhg, dlb, dgo = _hgrn_bwd("hgrn_bwd", proj, lb, go, oraw, states, dmix, b, s)
    dproj = jnp.concatenate([dpq, dpk, dpv, dhq, dhf, dhi, dhg], axis=1)
    dwin = _grad_w_in("dw_in", h2, dproj, ns)
    sent_mix = on_grads("mix", {"w_in": dwin, "w_out": dwout.reshape(ns, dwout.shape[0] // ns, d)})
    dx1, dx1h, dgm = _in_proj_bwd("in_proj_bwd", dproj, w_in, x1, gm, dx2, 0.5)

    da1, db1 = _ffn_bwd_act("ffn1_bwd_act", dx1h, wd1, a1, b1, sent_mix)
    dwd1 = _grad_w_shardrows("ffn1_dwd", z1, dx1h)
    dwg1 = _grad_w_shardcols("ffn1_dwg", h1, da1)
    dwu1 = _grad_w_shardcols("ffn1_dwu", h1, db1)
    on_grads("ffn1", {"ffn1_w_gate": dwg1, "ffn1_w_up": dwu1, "ffn1_w_down": dwd1})
    dx0, dg1 = _ffn_bwd_in("ffn1_bwd_in", da1, db1, wg1, wu1, x0, g1, dx1, None)

    nt = dg1.shape[0]
    sg = _small_grads(
        "small_grads", dg1.reshape(nt, d), dgm.reshape(nt, d), dg2.reshape(nt, d),
        dgq.reshape(-1, ATTN_W), dgk.reshape(-1, ATTN_W), dbias.transpose(1, 0, 2),
        dlb.reshape(b, HGRN_W), dgo.reshape(b, HGRN_W), lbp)
    g1g, gmg, g2g, gqg, gkg, rbg, lbg, gog = sg
    small = _pack_small(g1g, gmg, g2g, lbg, rbg[:, :N_REL], gqg, gkg, gog)
    return loss, dx0.reshape(b, s, d), small


def _pack_small(g1, gm, g2, lbp, rel_bias, gq, gk, go):
    flat = [g1.reshape(-1), gm.reshape(-1), g2.reshape(-1), lbp.reshape(-1), rel_bias.reshape(-1)]
    n_bias = 3 * SMALL_COLS - rel_bias.size
    heads = [gq.reshape(-1), gk.reshape(-1), go.reshape(-1)]
    n_tail = SMALL_COLS - sum(h.size for h in heads)
    return jnp.concatenate(flat + [jnp.zeros((n_bias,), F32)] + heads + [jnp.zeros((n_tail,), F32)]).reshape(
        SMALL_ROWS, SMALL_COLS)


def _unpack_small(p, d):
    flat = p.reshape(-1)
    o = 3 * d
    g1, gm, g2 = p[0:1], p[1:2], p[2:3]
    lbp = flat[o:o + 2 * HGRN_W].reshape(2, HGRN_W)
    o = 4 * SMALL_COLS
    rel = flat[o:o + ATTN_HEADS * N_REL].reshape(1, ATTN_HEADS, N_REL)
    o = 7 * SMALL_COLS
    gq = flat[o:o + ATTN_DH].reshape(1, ATTN_DH)
    gk = flat[o + ATTN_DH:o + 2 * ATTN_DH].reshape(1, ATTN_DH)
    go = flat[o + 2 * ATTN_DH:o + 2 * ATTN_DH + HGRN_DH].reshape(1, HGRN_DH)
    return g1, gm, g2, gq, gk, rel, lbp, go


def kernel(x, ffn1_norm_g, ffn1_w_gate, ffn1_w_up, ffn1_w_down, mix_norm_g, w_in, attn_q_norm_g, attn_k_norm_g, attn_rel_bias, hgrn_lower_bounds, hgrn_out_norm_g, w_out, ffn2_norm_g, ffn2_w_gate, ffn2_w_up, ffn2_w_down, loss_target, m_ffn1_norm_g, m_ffn1_w_gate, m_ffn1_w_up, m_ffn1_w_down, m_mix_norm_g, m_w_in, m_attn_q_norm_g, m_attn_k_norm_g, m_attn_rel_bias, m_hgrn_lower_bounds, m_hgrn_out_norm_g, m_w_out, m_ffn2_norm_g, m_ffn2_w_gate, m_ffn2_w_up, m_ffn2_w_down, v_ffn1_norm_g, v_ffn1_w_gate, v_ffn1_w_up, v_ffn1_w_down, v_mix_norm_g, v_w_in, v_attn_q_norm_g, v_attn_k_norm_g, v_attn_rel_bias, v_hgrn_lower_bounds, v_hgrn_out_norm_g, v_w_out, v_ffn2_norm_g, v_ffn2_w_gate, v_ffn2_w_up, v_ffn2_w_down):
    d = x.shape[-1]
    big_w = [ffn1_w_gate, ffn1_w_up, ffn1_w_down, w_in, w_out, ffn2_w_gate, ffn2_w_up, ffn2_w_down]
    big_m = [m_ffn1_w_gate, m_ffn1_w_up, m_ffn1_w_down, m_w_in, m_w_out, m_ffn2_w_gate, m_ffn2_w_up, m_ffn2_w_down]
    big_v = [v_ffn1_w_gate, v_ffn1_w_up, v_ffn1_w_down, v_w_in, v_w_out, v_ffn2_w_gate, v_ffn2_w_up, v_ffn2_w_down]
    big_names = ["ffn1_w_gate", "ffn1_w_up", "ffn1_w_down", "w_in", "w_out", "ffn2_w_gate", "ffn2_w_up", "ffn2_w_down"]

    shards = [w[0].astype(BF16) for w in big_w]
    start_a = _gather_start("gather_start_ffn1", shards[:3], ())
    start_b = _gather_start("gather_start_rest", shards[3:], (start_a[4],))

    def gathered(tag, started, after):
        send_sem, recv_sem, srcs, outs, _ = started
        srcs, outs = _gather_wait("gather_wait_" + tag, send_sem, recv_sem, srcs, outs, after)
        return _gather_join("gather_join_" + tag, srcs, outs)

    def first_weights(after):
        return (*gathered("ffn1", start_a, after), (start_b[4],))

    def rest_weights(after):
        win_f, wout_f, wg2, wu2, wd2 = gathered("rest", start_b, after)
        return win_f, wout_f.reshape(wout_f.shape[0] * wout_f.shape[1], d), wg2, wu2, wd2

    core = lax.axis_index("c").astype(jnp.int32).reshape(1)
    chip = (2 * lax.axis_index("x") + lax.axis_index("y")).astype(jnp.int32).reshape(1)
    started = {}

    def on_grads(tag, grads):
        names = list(grads)
        theirs = _pair_exchange("pair_exchange_" + tag, [grads[nm] for nm in names])
        sums = [_pair_sum("pair_sum_" + nm, grads[nm], th, core) for nm, th in zip(names, theirs)]
        started[tag] = (names, _scatter_start("scatter_start_" + tag, sums))
        return (started[tag][1][4],)

    loss, grad_x, small_g = _local_step(
        x, loss_target, ffn1_norm_g, mix_norm_g, ffn2_norm_g, attn_q_norm_g, attn_k_norm_g, hgrn_out_norm_g,
        attn_rel_bias[0], hgrn_lower_bounds, first_weights, rest_weights, on_grads)
    loss = lax.psum(loss, ("x", "y", "c"))

    def finish(tag, after):
        names, (send_sem, recv_sem, sums, lands, _) = started[tag]
        sums, lands = _scatter_wait("scatter_wait_" + tag, send_sem, recv_sem, sums, lands, after)
        return names, [_chip_sum("chip_sum_" + nm, sm, ld, chip) for nm, sm, ld in zip(names, sums, lands)]

    by_name = {nm: (w, m, v) for nm, w, m, v in zip(big_names, big_w, big_m, big_v)}
    updated = {}

    def update(names, halves, other_halves):
        for nm, mine, theirs in zip(names, halves, other_halves):
            w, m, v = by_name[nm]
            updated[nm] = _adamw("adamw_" + nm, w, mine, theirs, m, v, core)

    last_token = started["ffn1"][1][4]
    names_a, halves_a = finish("ffn2", last_token)
    names_m, halves_m = finish("mix", last_token)
    names_a, halves_a = names_a + names_m, halves_a + halves_m
    update(names_a, halves_a, _pair_join("pair_join_early", halves_a))
    names_b, halves_b = finish("ffn1", updated["w_out"][1])
    others_b, small_all = _pair_join("pair_join_last", halves_b, small_g)
    update(names_b, halves_b, others_b)
    big_out = [updated[nm] for nm in big_names]

    pack = lambda g1, gm, g2, gq, gk, rel, lbp, go: _pack_small(g1, gm, g2, lbp, rel[0], gq, gk, go)
    small_w = pack(ffn1_norm_g, mix_norm_g, ffn2_norm_g, attn_q_norm_g, attn_k_norm_g, attn_rel_bias, hgrn_lower_bounds, hgrn_out_norm_g)
    small_m = pack(m_ffn1_norm_g, m_mix_norm_g, m_ffn2_norm_g, m_attn_q_norm_g, m_attn_k_norm_g, m_attn_rel_bias, m_hgrn_lower_bounds, m_hgrn_out_norm_g)
    small_v = pack(v_ffn1_norm_g, v_mix_norm_g, v_ffn2_norm_g, v_attn_q_norm_g, v_attn_k_norm_g, v_attn_rel_bias, v_hgrn_lower_bounds, v_hgrn_out_norm_g)
    small_out = [_unpack_small(p, d) for p in _adamw_small("adamw_small", small_w, small_all, small_m, small_v)]

    def assemble(kind):
        bg = [o[kind] for o in big_out]
        g1, gm, g2, gq, gk, rel, lbp, go = small_out[kind]
        return [g1, bg[0], bg[1], bg[2], gm, bg[3], gq, gk, rel, lbp, go, bg[4], g2, bg[5], bg[6], bg[7]]

    return (loss, grad_x, *assemble(0), *assemble(1), *assemble(2), *assemble(3))
```

```python
import functools

import jax
import jax.numpy as jnp
from jax import lax
from jax.experimental import pallas as pl
from jax.experimental.pallas import tpu as pltpu

F32 = jnp.float32
BF16 = jnp.bfloat16
MESH = pl.DeviceIdType.MESH

N_CHIPS = 4
N_DEV = 8
CHUNK = 64
ATTN_HEADS = 8
ATTN_DH = 64
ATTN_W = ATTN_HEADS * ATTN_DH
HGRN_HEADS = 4
HGRN_DH = 128
HGRN_W = HGRN_HEADS * HGRN_DH
LEFT_CHUNKS = 8
BAND = (LEFT_CHUNKS + 1) * CHUNK
KPAD = LEFT_CHUNKS * CHUNK
REL_CLIP = 128
N_REL = 2 * REL_CLIP + 1
N_REL_PAD = 384
RMS_EPS = 1e-6
LANES = 128
SMALL_ROWS = 8
SMALL_COLS = 1024

ADAM_LR = 0.001
ADAM_B1 = 0.9
ADAM_B2 = 0.999
ADAM_EPS = 1e-08
ADAM_WD = 0.01
ADAM_STEP = 10

NN = (((1,), (0,)), ((), ()))
NT = (((1,), (1,)), ((), ()))
TN = (((0,), (0,)), ((), ()))

VMEM_LIMIT = 48 * 1024 * 1024
VMEM_LIMIT_BIG = 56 * 1024 * 1024


def _sigmoid(x):
    return 1.0 / (1.0 + jnp.exp(-x))


def _silu(x):
    return x * _sigmoid(x)


def _dsilu(x):
    s = _sigmoid(x)
    return s * (1.0 + x * (1.0 - s))


def _dot(a, b, dims=NN):
    return lax.dot_general(a, b, dims, preferred_element_type=F32)


def _split3(x):
    hi = x.astype(BF16)
    r1 = x - hi.astype(F32)
    mid = r1.astype(BF16)
    lo = (r1 - mid.astype(F32)).astype(BF16)
    return hi, mid, lo


def _dot_exact_rhs(x, mat, dims=NN):
    hi, mid, lo = _split3(x)
    return _dot(hi, mat, dims) + _dot(mid, mat, dims) + _dot(lo, mat, dims)


def _dot_exact_lhs(mat, x, dims=NN):
    hi, mid, lo = _split3(x)
    return _dot(mat, hi, dims) + _dot(mat, mid, dims) + _dot(mat, lo, dims)


def _params(*sem):
    return pltpu.CompilerParams(dimension_semantics=sem, vmem_limit_bytes=VMEM_LIMIT)


def _mm(name, ins, terms, n_acc, grid, acc_shape, outs, epilogue, extras=(), deps=()):
    nk = grid[2]
    ni, ne, nd, no = len(ins), len(extras), len(deps), len(outs)

    def body(*refs):
        in_refs = refs[:ni]
        ex_refs = refs[ni:ni + ne]
        out_refs = refs[ni + ne + nd:ni + ne + nd + no]
        acc_refs = refs[ni + ne + nd + no:]
        parts = [None] * n_acc
        for ai, li, ri, dims in terms:
            d = _dot(in_refs[li][...], in_refs[ri][...], dims)
            parts[ai] = d if parts[ai] is None else parts[ai] + d

        def finish(accs):
            res = epilogue(accs, [e[...] for e in ex_refs])
            for o, r in zip(out_refs, res):
                o[...] = r.astype(o.dtype)

        if nk == 1:
            finish(parts)
        else:
            k = pl.program_id(2)

            @pl.when(k == 0)
            def _():
                for a, p in zip(acc_refs, parts):
                    a[...] = p

            @pl.when(k > 0)
            def _():
                for a, p in zip(acc_refs, parts):
                    a[...] += p

            @pl.when(k == nk - 1)
            def _():
                finish([a[...] for a in acc_refs])

    scratch = [] if nk == 1 else [pltpu.VMEM(acc_shape, F32) for _ in range(n_acc)]
    res = pl.pallas_call(
        body,
        name=name,
        grid=grid,
        in_specs=[s for _, s in ins] + [s for _, s in extras] + [pl.BlockSpec(memory_space=pl.ANY)] * nd,
        out_specs=[s for _, s in outs],
        out_shape=[o for o, _ in outs],
        scratch_shapes=scratch,
        compiler_params=_params("parallel", "parallel", "arbitrary"),
    )(*[a for a, _ in ins], *[a for a, _ in extras], *deps)
    return res


def _row_tile(t):
    return 512 if t % 512 == 0 else t


def _k_tile(t):
    return 1024 if t % 1024 == 0 else t


def _rmsnorm_fwd(name, x, g):
    t, d = x.shape
    tm = _row_tile(t)

    def body(x_ref, g_ref, h_ref):
        xv = x_ref[...]
        ms = jnp.mean(xv * xv, axis=-1, keepdims=True)
        h_ref[...] = (xv * lax.rsqrt(ms + RMS_EPS) * g_ref[...]).astype(BF16)

    return pl.pallas_call(
        body,
        name=name,
        grid=(t // tm,),
        in_specs=[pl.BlockSpec((tm, d), lambda i: (i, 0)), pl.BlockSpec((1, d), lambda i: (0, 0))],
        out_specs=pl.BlockSpec((tm, d), lambda i: (i, 0)),
        out_shape=jax.ShapeDtypeStruct((t, d), BF16),
        compiler_params=_params("parallel"),
    )(x, g)


def _norm_bwd_epilogue(copy_scale):
    def epilogue(accs, ex):
        dh = accs[0]
        xv, g, dres = ex
        ms = jnp.mean(xv * xv, axis=-1, keepdims=True)
        rstd = lax.rsqrt(ms + RMS_EPS)
        xhat = xv * rstd
        dxhat = dh * g
        dx = rstd * (dxhat - xhat * jnp.mean(dxhat * xhat, axis=-1, keepdims=True))
        out = dres + dx
        dg = jnp.sum(dh * xhat, axis=0, keepdims=True)
        if copy_scale is None:
            return out, dg
        return out, out * copy_scale, dg

    return epilogue


def _ffn_up(name, h, wg, wu, deps=()):
    t, d = h.shape
    ns, f, _ = wg.shape
    tm = _row_tile(t)

    def epilogue(accs, ex):
        a, b = accs
        return a, b, _silu(a) * b

    w_spec = pl.BlockSpec((None, f, d), lambda j, i, k: (j, 0, 0))
    o_spec = pl.BlockSpec((None, tm, f), lambda j, i, k: (j, i, 0))
    o_shape = jax.ShapeDtypeStruct((ns, t, f), BF16)
    return _mm(
        name,
        ins=[(h, pl.BlockSpec((tm, d), lambda j, i, k: (i, 0))), (wg, w_spec), (wu, w_spec)],
        terms=[(0, 0, 1, NT), (1, 0, 2, NT)],
        n_acc=2,
        grid=(ns, t // tm, 1),
        acc_shape=(tm, f),
        outs=[(o_shape, o_spec)] * 3,
        epilogue=epilogue,
        deps=deps,
    )


def _ffn_down(name, z, wd, x):
    ns, t, f = z.shape
    d = wd.shape[2]
    tm = _row_tile(t)
    row = pl.BlockSpec((tm, d), lambda i, n, k: (i, 0))
    return _mm(
        name,
        ins=[(z, pl.BlockSpec((None, tm, f), lambda i, n, k: (k, i, 0))),
             (wd, pl.BlockSpec((None, f, d), lambda i, n, k: (k, 0, 0)))],
        terms=[(0, 0, 1, NN)],
        n_acc=1,
        grid=(t // tm, 1, ns),
        acc_shape=(tm, d),
        outs=[(jax.ShapeDtypeStruct((t, d), F32), row)],
        epilogue=lambda accs, ex: (ex[0] + 0.5 * accs[0],),
        extras=[(x, row)],
    )[0]


def _ffn_down_loss(name, z, wd, x, target):
    ns, t, f = z.shape
    d = wd.shape[2]
    tm = _row_tile(t)
    nt = t // tm
    row = pl.BlockSpec((tm, d), lambda i, n, k: (i, 0))

    def epilogue(accs, ex):
        e = ex[0] + 0.5 * accs[0] - ex[1]
        dy = e * (1.0 / d)
        return dy, 0.5 * dy, jnp.sum(e * e, axis=0, keepdims=True)

    return _mm(
        name,
        ins=[(z, pl.BlockSpec((None, tm, f), lambda i, n, k: (k, i, 0))),
             (wd, pl.BlockSpec((None, f, d), lambda i, n, k: (k, 0, 0)))],
        terms=[(0, 0, 1, NN)],
        n_acc=1,
        grid=(nt, 1, ns),
        acc_shape=(tm, d),
        outs=[(jax.ShapeDtypeStruct((t, d), F32), row), (jax.ShapeDtypeStruct((t, d), BF16), row),
              (jax.ShapeDtypeStruct((nt, 1, d), F32), pl.BlockSpec((None, 1, d), lambda i, n, k: (i, 0, 0)))],
        epilogue=epilogue,
        extras=[(x, row), (target, row)],
    )


def _ffn_bwd_act(name, dout, wd, a, b, deps=()):
    t, d = dout.shape
    ns, f, _ = wd.shape
    tm = _row_tile(t)

    def epilogue(accs, ex):
        dz = accs[0]
        av = ex[0].astype(F32)
        bv = ex[1].astype(F32)
        return dz * bv * _dsilu(av), dz * _silu(av)

    act = pl.BlockSpec((None, tm, f), lambda j, i, k: (j, i, 0))
    o_shape = jax.ShapeDtypeStruct((ns, t, f), BF16)
    return _mm(
        name,
        ins=[(dout, pl.BlockSpec((tm, d), lambda j, i, k: (i, 0))),
             (wd, pl.BlockSpec((None, f, d), lambda j, i, k: (j, 0, 0)))],
        terms=[(0, 0, 1, NT)],
        n_acc=1,
        grid=(ns, t // tm, 1),
        acc_shape=(tm, f),
        outs=[(o_shape, act)] * 2,
        epilogue=epilogue,
        extras=[(a, act), (b, act)],
        deps=deps,
    )


def _grad_w_shardrows(name, z, dout):
    ns, t, f = z.shape
    d = dout.shape[1]
    tk = _k_tile(t)
    return _mm(
        name,
        ins=[(z, pl.BlockSpec((None, tk, f), lambda j, n, k: (j, k, 0))),
             (dout, pl.BlockSpec((tk, d), lambda j, n, k: (k, 0)))],
        terms=[(0, 0, 1, TN)],
        n_acc=1,
        grid=(ns, 1, t // tk),
        acc_shape=(f, d),
        outs=[(jax.ShapeDtypeStruct((ns, f, d), F32), pl.BlockSpec((None, f, d), lambda j, n, k: (j, 0, 0)))],
        epilogue=lambda accs, ex: (accs[0],),
    )[0]


def _ffn_bwd_in(name, da, db, wg, wu, x, g, dres, copy_scale):
    ns, t, f = da.shape
    d = wg.shape[2]
    tm = _row_tile(t)
    nt = t // tm
    act = pl.BlockSpec((None, tm, f), lambda i, n, k: (k, i, 0))
    w_spec = pl.BlockSpec((None, f, d), lambda i, n, k: (k, 0, 0))
    row = pl.BlockSpec((tm, d), lambda i, n, k: (i, 0))
    outs = [(jax.ShapeDtypeStruct((t, d), F32), row)]
    if copy_scale is not None:
        outs.append((jax.ShapeDtypeStruct((t, d), BF16), row))
    outs.append((jax.ShapeDtypeStruct((nt, 1, d), F32), pl.BlockSpec((None, 1, d), lambda i, n, k: (i, 0, 0))))
    return _mm(
        name,
        ins=[(da, act), (db, act), (wg, w_spec), (wu, w_spec)],
        terms=[(0, 0, 2, NN), (0, 1, 3, NN)],
        n_acc=1,
        grid=(nt, 1, ns),
        acc_shape=(tm, d),
        outs=outs,
        epilogue=_norm_bwd_epilogue(copy_scale),
        extras=[(x, row), (g, pl.BlockSpec((1, d), lambda i, n, k: (0, 0))), (dres, row)],
    )


def _in_proj(name, h, w_in):
    t, d = h.shape
    ns, _, pj = w_in.shape
    tm = _row_tile(t)
    return _mm(
        name,
        ins=[(h, pl.BlockSpec((tm, d), lambda j, i, k: (i, 0))),
             (w_in, pl.BlockSpec((None, d, pj), lambda j, i, k: (j, 0, 0)))],
        terms=[(0, 0, 1, NN)],
        n_acc=1,
        grid=(ns, t // tm, 1),
        acc_shape=(tm, pj),
        outs=[(jax.ShapeDtypeStruct((t, ns * pj), F32), pl.BlockSpec((tm, pj), lambda j, i, k: (i, j)))],
        epilogue=lambda accs, ex: (accs[0],),
    )[0]


def _in_proj_bwd(name, dp, w_in, x, g, dres, copy_scale):
    t = dp.shape[0]
    ns, d, pj = w_in.shape
    tm = _row_tile(t)
    nt = t // tm
    row = pl.BlockSpec((tm, d), lambda i, n, k: (i, 0))
    outs = [(jax.ShapeDtypeStruct((t, d), F32), row)]
    if copy_scale is not None:
        outs.append((jax.ShapeDtypeStruct((t, d), BF16), row))
    outs.append((jax.ShapeDtypeStruct((nt, 1, d), F32), pl.BlockSpec((None, 1, d), lambda i, n, k: (i, 0, 0))))
    return _mm(
        name,
        ins=[(dp, pl.BlockSpec((tm, pj), lambda i, n, k: (i, k))),
             (w_in, pl.BlockSpec((None, d, pj), lambda i, n, k: (k, 0, 0)))],
        terms=[(0, 0, 1, NT)],
        n_acc=1,
        grid=(nt, 1, ns),
        acc_shape=(tm, d),
        outs=outs,
        epilogue=_norm_bwd_epilogue(copy_scale),
        extras=[(x, row), (g, pl.BlockSpec((1, d), lambda i, n, k: (0, 0))), (dres, row)],
    )


def _grad_w_in(name, h, dp, ns):
    t, d = h.shape
    pj = dp.shape[1] // ns
    tk = _k_tile(t)
    return _mm(
        name,
        ins=[(h, pl.BlockSpec((tk, d), lambda j, n, k: (k, 0))),
             (dp, pl.BlockSpec((tk, pj), lambda j, n, k: (k, j)))],
        terms=[(0, 0, 1, TN)],
        n_acc=1,
        grid=(ns, 1, t // tk),
        acc_shape=(d, pj),
        outs=[(jax.ShapeDtypeStruct((ns, d, pj), F32), pl.BlockSpec((None, d, pj), lambda j, n, k: (j, 0, 0)))],
        epilogue=lambda accs, ex: (accs[0],),
    )[0]


def _out_proj(name, mix, w_out, x):
    t, dm = mix.shape
    d = w_out.shape[1]
    tm = _row_tile(t)
    row = pl.BlockSpec((tm, d), lambda i, n, k: (i, 0))
    return _mm(
        name,
        ins=[(mix, pl.BlockSpec((tm, dm), lambda i, n, k: (i, 0))),
             (w_out, pl.BlockSpec((dm, d), lambda i, n, k: (0, 0)))],
        terms=[(0, 0, 1, NN)],
        n_acc=1,
        grid=(t // tm, 1, 1),
        acc_shape=(tm, d),
        outs=[(jax.ShapeDtypeStruct((t, d), F32), row)],
        epilogue=lambda accs, ex: (ex[0] + accs[0],),
        extras=[(x, row)],
    )[0]


def _out_proj_bwd(name, dx, w_out, deps=()):
    t, d = dx.shape
    dm = w_out.shape[0]
    tm = _row_tile(t)
    return _mm(
        name,
        ins=[(dx, pl.BlockSpec((tm, d), lambda i, n, k: (i, 0))),
             (w_out, pl.BlockSpec((dm, d), lambda i, n, k: (0, 0)))],
        terms=[(0, 0, 1, NT)],
        n_acc=1,
        grid=(t // tm, 1, 1),
        acc_shape=(tm, dm),
        outs=[(jax.ShapeDtypeStruct((t, dm), F32), pl.BlockSpec((tm, dm), lambda i, n, k: (i, 0)))],
        epilogue=lambda accs, ex: (accs[0],),
        deps=deps,
    )[0]


def _grad_w_out(name, mix, dx):
    t, dm = mix.shape
    d = dx.shape[1]
    tk = _k_tile(t)
    return _mm(
        name,
        ins=[(mix, pl.BlockSpec((tk, dm), lambda a, n, k: (k, 0))),
             (dx, pl.BlockSpec((tk, d), lambda a, n, k: (k, 0)))],
        terms=[(0, 0, 1, TN)],
        n_acc=1,
        grid=(1, 1, t // tk),
        acc_shape=(dm, d),
        outs=[(jax.ShapeDtypeStruct((dm, d), F32), pl.BlockSpec((dm, d), lambda a, n, k: (0, 0)))],
        epilogue=lambda accs, ex: (accs[0],),
    )[0]


def _head_group_matrix():
    r = lax.broadcasted_iota(jnp.int32, (ATTN_W, ATTN_W), 0)
    c = lax.broadcasted_iota(jnp.int32, (ATTN_W, ATTN_W), 1)
    same = jnp.right_shift(r, 6) == jnp.right_shift(c, 6)
    return jnp.where(same, 1.0, 0.0).astype(BF16)


def _qk_prep(name, proj, gq, gk):
    b, s, _ = proj.shape
    tm = KPAD
    nb = s // tm

    def body(q_ref, k_ref, v_ref, gq_ref, gk_ref, qn_ref, kn_ref, vb_ref):
        j = pl.program_id(1)
        bd = _head_group_matrix()

        def norm(xv, g):
            ms = _dot_exact_rhs(xv * xv, bd) * (1.0 / ATTN_DH)
            return xv * lax.rsqrt(ms + RMS_EPS) * g

        @pl.when(j == 0)
        def _():
            kn_ref[...] = jnp.zeros_like(kn_ref)
            vb_ref[...] = jnp.zeros_like(vb_ref)

        @pl.when(j > 0)
        def _():
            qn_ref[...] = norm(q_ref[...], gq_ref[...]).astype(BF16)
            kn_ref[...] = norm(k_ref[...], gk_ref[...]).astype(BF16)
            vb_ref[...] = v_ref[...].astype(BF16)

    src_blk = lambda col: pl.BlockSpec((None, tm, ATTN_W), lambda bi, j: (bi, jnp.maximum(j - 1, 0), col))
    gspec = pl.BlockSpec((1, ATTN_W), lambda bi, j: (0, 0))
    padded = pl.BlockSpec((None, tm, ATTN_W), lambda bi, j: (bi, j, 0))
    return pl.pallas_call(
        body,
        name=name,
        grid=(b, nb + 1),
        in_specs=[src_blk(0), src_blk(1), src_blk(2), gspec, gspec],
        out_specs=[src_blk(0), padded, padded],
        out_shape=[jax.ShapeDtypeStruct((b, s, ATTN_W), BF16), jax.ShapeDtypeStruct((b, KPAD + s, ATTN_W), BF16),
                   jax.ShapeDtypeStruct((b, KPAD + s, ATTN_W), BF16)],
        compiler_params=_params("parallel", "arbitrary"),
    )(proj, proj, proj, gq, gk)


def _qk_prep_bwd(name, proj, dqn, dkn, dv, gq, gk):
    b, s, _ = proj.shape
    tm = KPAD
    nb = s // tm

    def body(q_ref, k_ref, dqn_ref, dkn_ref, dv_ref, gq_ref, gk_ref, dq_ref, dk_ref, dvb_ref, dgq_ref, dgk_ref):
        bd = _head_group_matrix()

        def bwd(xv, dy, g):
            ms = _dot_exact_rhs(xv * xv, bd) * (1.0 / ATTN_DH)
            rstd = lax.rsqrt(ms + RMS_EPS)
            xhat = xv * rstd
            dxhat = dy * g
            gm = _dot_exact_rhs(dxhat * xhat, bd) * (1.0 / ATTN_DH)
            return rstd * (dxhat - xhat * gm), jnp.sum(dy * xhat, axis=0, keepdims=True)

        dq, dgq = bwd(q_ref[...], dqn_ref[...], gq_ref[...])
        dk, dgk = bwd(k_ref[...], dkn_ref[...], gk_ref[...])
        dq_ref[...] = dq.astype(BF16)
        dk_ref[...] = dk.astype(BF16)
        dvb_ref[...] = dv_ref[...].astype(BF16)
        dgq_ref[...] = dgq
        dgk_ref[...] = dgk

    col = lambda c: pl.BlockSpec((None, tm, ATTN_W), lambda bi, j: (bi, j, c))
    past_pad = pl.BlockSpec((None, tm, ATTN_W), lambda bi, j: (bi, j + 1, 0))
    gspec = pl.BlockSpec((1, ATTN_W), lambda bi, j: (0, 0))
    pspec = pl.BlockSpec((None, 1, ATTN_W), lambda bi, j: (bi * nb + j, 0, 0))
    o_shape = jax.ShapeDtypeStruct((b, s, ATTN_W), BF16)
    p_shape = jax.ShapeDtypeStruct((b * nb, 1, ATTN_W), F32)
    return pl.pallas_call(
        body,
        name=name,
        grid=(b, nb),
        in_specs=[col(0), col(1), col(0), past_pad, past_pad, gspec, gspec],
        out_specs=[col(0)] * 3 + [pspec] * 2,
        out_shape=[o_shape] * 3 + [p_shape] * 2,
        compiler_params=_params("parallel", "parallel"),
    )(proj, proj, dqn, dkn, dv, gq, gk)


Q_CHUNKS = 4
QBLK = Q_CHUNKS * CHUNK
WIN = (LEFT_CHUNKS + Q_CHUNKS) * CHUNK
DB_W = BAND + CHUNK
MASKED = -1e30


def _band_table(bias):
    rows = [jnp.pad(bias, ((0, 0), (0, 0), (CHUNK * i, WIN - BAND - CHUNK * i)), constant_values=MASKED)
            for i in range(Q_CHUNKS)]
    return jnp.concatenate(rows, axis=1)


def _head_lanes(hh):
    lane = lax.broadcasted_iota(jnp.int32, (1, LANES), 1)
    return (lane < ATTN_DH) if hh == 0 else (lane >= ATTN_DH)


def _attn_probs(qh, kw, table, start):
    s = _dot(qh, kw, NT) * (ATTN_DH ** -0.5) + table
    col = lax.broadcasted_iota(jnp.int32, (QBLK, WIN), 1)
    s = jnp.where(col + start >= KPAD, s, MASKED)
    m = jnp.max(s, axis=-1, keepdims=True)
    p = jnp.exp(s - m)
    return p * (1.0 / jnp.sum(p, axis=-1, keepdims=True))


def _attn_fwd(name, q, k, v, table):
    b, s, w = q.shape
    sp = k.shape[1]

    def body(q_ref, k_ref, v_ref, t_ref, o_ref):
        start = pl.multiple_of(pl.program_id(2) * QBLK, QBLK)
        kw = k_ref[pl.ds(start, WIN), :]
        vw = v_ref[pl.ds(start, WIN), :]
        q2 = q_ref[...]
        out = jnp.zeros((QBLK, LANES), F32)
        for hh in range(2):
            mine = _head_lanes(hh)
            p = _attn_probs(jnp.where(mine, q2, jnp.zeros_like(q2)), kw, t_ref[hh], start)
            out = jnp.where(mine, _dot(p.astype(BF16), vw), out)
        o_ref[...] = out.astype(BF16)

    qspec = pl.BlockSpec((None, QBLK, LANES), lambda p, bi, i: (bi, i, p))
    kspec = pl.BlockSpec((None, sp, LANES), lambda p, bi, i: (bi, 0, p))
    return pl.pallas_call(
        body,
        name=name,
        grid=(w // LANES, b, s // QBLK),
        in_specs=[qspec, kspec, kspec, pl.BlockSpec((2, QBLK, WIN), lambda p, bi, i: (p, 0, 0))],
        out_specs=qspec,
        out_shape=jax.ShapeDtypeStruct((b, s, w), BF16),
        compiler_params=_params("parallel", "parallel", "arbitrary"),
    )(q, k, v, table)


def _attn_bwd(name, q, k, v, table, dmix):
    b, s, w = q.shape
    sp = k.shape[1]

    def body(q_ref, k_ref, v_ref, t_ref, do_ref, dq_ref, dk_ref, dv_ref, dbe_ref, dbo_ref):
        bi = pl.program_id(1)
        i = pl.program_id(2)
        start = pl.multiple_of(i * QBLK, QBLK)
        win = pl.ds(start, WIN)

        @pl.when(i == 0)
        def _():
            dk_ref[...] = jnp.zeros_like(dk_ref)
            dv_ref[...] = jnp.zeros_like(dv_ref)

        @pl.when(jnp.logical_and(i == 0, bi == 0))
        def _():
            dbe_ref[...] = jnp.zeros_like(dbe_ref)
            dbo_ref[...] = jnp.zeros_like(dbo_ref)

        kw = k_ref[win, :]
        vw = v_ref[win, :]
        q2 = q_ref[...]
        do2 = do_ref[...].astype(BF16)
        dq = jnp.zeros((QBLK, LANES), F32)
        for hh in range(2):
            mine = _head_lanes(hh)
            qh = jnp.where(mine, q2, jnp.zeros_like(q2))
            doh = jnp.where(mine, do2, jnp.zeros_like(do2))
            p = _attn_probs(qh, kw, t_ref[hh], start)
            dp = _dot(doh, vw, NT)
            ds = p * (dp - jnp.sum(p * dp, axis=-1, keepdims=True))
            for qi in range(Q_CHUNKS):
                c0 = (qi // 2) * LANES
                blk = ds[qi * CHUNK:(qi + 1) * CHUNK, c0:c0 + DB_W]
                if qi % 2 == 0:
                    dbe_ref[hh] += blk
                else:
                    dbo_ref[hh] += blk
            dsb = (ds * (ATTN_DH ** -0.5)).astype(BF16)
            dq = jnp.where(mine, _dot(dsb, kw), dq)
            dk_ref[win, :] += _dot(dsb, qh, TN)
            dv_ref[win, :] += _dot(p.astype(BF16), doh, TN)
        dq_ref[...] = dq

    qspec = pl.BlockSpec((None, QBLK, LANES), lambda p, bi, i: (bi, i, p))
    kspec = pl.BlockSpec((None, sp, LANES), lambda p, bi, i: (bi, 0, p))
    dbspec = pl.BlockSpec((2, CHUNK, DB_W), lambda p, bi, i: (p, 0, 0))
    db_shape = jax.ShapeDtypeStruct((ATTN_HEADS, CHUNK, DB_W), F32)
    return pl.pallas_call(
        body,
        name=name,
        grid=(w // LANES, b, s // QBLK),
        in_specs=[qspec, kspec, kspec, pl.BlockSpec((2, QBLK, WIN), lambda p, bi, i: (p, 0, 0)), qspec],
        out_specs=[qspec, kspec, kspec, dbspec, dbspec],
        out_shape=[jax.ShapeDtypeStruct((b, s, w), F32), jax.ShapeDtypeStruct((b, sp, w), F32),
                   jax.ShapeDtypeStruct((b, sp, w), F32), db_shape, db_shape],
        compiler_params=_params("arbitrary", "arbitrary", "arbitrary"),
    )(q, k, v, table, dmix)


HQ_COL = 3 * ATTN_W // HGRN_DH
HF_COL = HQ_COL + HGRN_HEADS
HI_COL = HF_COL + HGRN_HEADS
HG_COL = HI_COL + HGRN_HEADS
HGRN_PAIR = 2
PAIR_W = HGRN_PAIR * HGRN_DH


def _tri(lower):
    r = lax.broadcasted_iota(jnp.int32, (CHUNK, CHUNK), 0)
    c = lax.broadcasted_iota(jnp.int32, (CHUNK, CHUNK), 1)
    return (r >= c) if lower else (r <= c)


def _hgrn_chunk(hq, hf, lb, tril):
    sig = _sigmoid(hf)
    f = lb + (1.0 - lb) * sig
    g = jnp.log(f)
    ones_l = jnp.where(tril, 1.0, 0.0).astype(BF16)
    b = _dot_exact_lhs(ones_l, g)
    bl = jnp.sum(g, axis=0, keepdims=True)
    rows = lax.broadcasted_iota(jnp.int32, g.shape, 0)
    bm = jnp.sum(jnp.where(rows <= CHUNK // 2, g, 0.0), axis=0, keepdims=True)
    sq = _sigmoid(hq)
    q = hq * sq
    k = 1.0 - f
    return sig, f, b, bl, bm, sq, q, k


def _hgrn_fwd(name, proj, lb, go, b, s):
    nc = s // CHUNK
    t = b * s

    def body(hq_ref, hf_ref, hi_ref, hg_ref, lb_ref, go_ref, ro_ref, oraw_ref, st_ref, s_scr):
        tril = _tri(True)
        gov = go_ref[...]
        s_scr[...] = jnp.zeros_like(s_scr)

        def step(c, carry):
            sl = pl.ds(pl.multiple_of(c * CHUNK, CHUNK), CHUNK)
            for hh in range(HGRN_PAIR):
                cols = pl.ds(hh * HGRN_DH, HGRN_DH)
                lbv = lb_ref[:, cols]
                hg = hg_ref[sl, cols]
                _, _, bb, bl, bm, _, q, k = _hgrn_chunk(hq_ref[sl, cols], hf_ref[sl, cols], lbv, tril)
                vb = hi_ref[sl, cols].astype(BF16)
                qe = (q * jnp.exp(bb - bm)).astype(BF16)
                ke = (k * jnp.exp(bm - bb)).astype(BF16)
                a = jnp.where(tril, _dot(qe, ke, NT), 0.0)
                st = s_scr[hh]
                st_ref[hh, c] = st
                qb = (q * jnp.exp(bb)).astype(BF16)
                o = _dot(a.astype(BF16), vb) + _dot(qb, st.astype(BF16), NT)
                kb = (k * jnp.exp(bl - bb)).astype(BF16)
                s_scr[hh] = st * jnp.exp(bl) + _dot(vb, kb, TN)
                rstd = lax.rsqrt(jnp.mean(o * o, axis=-1, keepdims=True) + RMS_EPS)
                ro_ref[sl, cols] = ((o * rstd * gov) * _silu(hg)).astype(BF16)
                oraw_ref[sl, cols] = o
            return carry

        lax.fori_loop(0, nc, step, 0)

    col = lambda base: pl.BlockSpec((s, PAIR_W), lambda bi, g: (bi, base // HGRN_PAIR + g))
    vec = pl.BlockSpec((1, PAIR_W), lambda bi, g: (0, g))
    out = pl.BlockSpec((s, PAIR_W), lambda bi, g: (bi, g))
    return pl.pallas_call(
        body,
        name=name,
        grid=(b, HGRN_HEADS // HGRN_PAIR),
        in_specs=[col(HQ_COL), col(HF_COL), col(HI_COL), col(HG_COL), vec,
                  pl.BlockSpec((1, HGRN_DH), lambda bi, g: (0, 0))],
        out_specs=[out, out,
                   pl.BlockSpec((None, HGRN_PAIR, nc, HGRN_DH, HGRN_DH), lambda bi, g: (bi, g, 0, 0, 0))],
        out_shape=[jax.ShapeDtypeStruct((t, HGRN_W), BF16), jax.ShapeDtypeStruct((t, HGRN_W), F32),
                   jax.ShapeDtypeStruct((b, HGRN_HEADS, nc, HGRN_DH, HGRN_DH), F32)],
        scratch_shapes=[pltpu.VMEM((HGRN_PAIR, HGRN_DH, HGRN_DH), F32)],
        compiler_params=_params("parallel", "parallel"),
    )(proj, proj, proj, proj, lb, go)


def _hgrn_bwd(name, proj, lb, go, oraw, states, dmix, b, s):
    nc = s // CHUNK
    t = b * s

    def body(hq_ref, hf_ref, hi_ref, hg_ref, lb_ref, go_ref, oraw_ref, st_ref, dro_ref,
             dhq_ref, dhf_ref, dhi_ref, dhg_ref, dlb_ref, dgo_ref, ds_scr, dlb_scr, dgo_scr):
        tril = _tri(True)
        ones_u = jnp.where(_tri(False), 1.0, 0.0).astype(BF16)
        gov = go_ref[...]
        ds_scr[...] = jnp.zeros_like(ds_scr)
        dlb_scr[...] = jnp.zeros_like(dlb_scr)
        dgo_scr[...] = jnp.zeros_like(dgo_scr)

        def step(ci, carry):
            c = nc - 1 - ci
            sl = pl.ds(pl.multiple_of(c * CHUNK, CHUNK), CHUNK)
            for hh in range(HGRN_PAIR):
                cols = pl.ds(hh * HGRN_DH, HGRN_DH)
                lbv = lb_ref[:, cols]
                hq = hq_ref[sl, cols]
                hg = hg_ref[sl, cols]
                sig, f, bb, bl, bm, sq, q, k = _hgrn_chunk(hq, hf_ref[sl, cols], lbv, tril)
                vb = hi_ref[sl, cols].astype(BF16)
                ebm = jnp.exp(bb - bm)
                embm = jnp.exp(bm - bb)
                eb = jnp.exp(bb)
                ebl = jnp.exp(bl - bb)
                e_last = jnp.exp(bl)
                qe = (q * ebm).astype(BF16)
                ke = (k * embm).astype(BF16)
                qb = (q * eb).astype(BF16)
                kb = (k * ebl).astype(BF16)
                a = jnp.where(tril, _dot(qe, ke, NT), 0.0)
                st = st_ref[hh, c]
                dst = ds_scr[hh]
                o = oraw_ref[sl, cols]
                dro = dro_ref[sl, cols]
                sg = _sigmoid(hg)
                rstd = lax.rsqrt(jnp.mean(o * o, axis=-1, keepdims=True) + RMS_EPS)
                ohat = o * rstd
                dn = dro * (hg * sg)
                dhg_ref[sl, cols] = (dro * (ohat * gov) * (sg * (1.0 + hg * (1.0 - sg)))).astype(BF16)
                dgo_scr[:, cols] += jnp.sum(dn * ohat, axis=0, keepdims=True)
                dohat = dn * gov
                do = rstd * (dohat - ohat * jnp.mean(dohat * ohat, axis=-1, keepdims=True))
                dob = do.astype(BF16)
                dab = jnp.where(tril, _dot(dob, vb, NT), 0.0).astype(BF16)
                stb = st.astype(BF16)
                dstb = dst.astype(BF16)
                dv = _dot(a.astype(BF16), dob, TN) + _dot(kb, dstb, NT)
                dqe = _dot(dab, ke)
                dke = _dot(dab, qe, TN)
                dqb = _dot(dob, stb)
                dkb = _dot(vb, dstb)
                dq = dqe * ebm + dqb * eb
                dk = dke * embm + dkb * ebl
                db = (qe.astype(F32) * dqe - ke.astype(F32) * dke) + q * (dqb * eb) - k * (dkb * ebl)
                d_last = (jnp.sum(k * ebl * dkb, axis=0, keepdims=True)
                          + jnp.sum(dst * st, axis=0, keepdims=True) * e_last)
                dg = _dot_exact_lhs(ones_u, db) + d_last
                df = dg / f - dk
                dhf_ref[sl, cols] = (df * (1.0 - lbv) * sig * (1.0 - sig)).astype(BF16)
                dlb_scr[:, cols] += jnp.sum(df * (1.0 - sig), axis=0, keepdims=True)
                dhq_ref[sl, cols] = (dq * (sq * (1.0 + hq * (1.0 - sq)))).astype(BF16)
                dhi_ref[sl, cols] = dv.astype(BF16)
                ds_scr[hh] = dst * e_last + _dot(dob, qb, TN)
            return carry

        lax.fori_loop(0, nc, step, 0)
        dlb_ref[...] = dlb_scr[...]
        dgo_ref[...] = dgo_scr[...]

    col = lambda base: pl.BlockSpec((s, PAIR_W), lambda bi, g: (bi, base // HGRN_PAIR + g))
    vec = pl.BlockSpec((1, PAIR_W), lambda bi, g: (0, g))
    out = pl.BlockSpec((s, PAIR_W), lambda bi, g: (bi, g))
    part = pl.BlockSpec((None, 1, PAIR_W), lambda bi, g: (bi, 0, g))
    o_shape = jax.ShapeDtypeStruct((t, HGRN_W), BF16)
    p_shape = jax.ShapeDtypeStruct((b, 1, HGRN_W), F32)
    return pl.pallas_call(
        body,
        name=name,
        grid=(b, HGRN_HEADS // HGRN_PAIR),
        in_specs=[col(HQ_COL), col(HF_COL), col(HI_COL), col(HG_COL), vec,
                  pl.BlockSpec((1, HGRN_DH), lambda bi, g: (0, 0)), out,
                  pl.BlockSpec((None, HGRN_PAIR, nc, HGRN_DH, HGRN_DH), lambda bi, g: (bi, g, 0, 0, 0)),
                  col(ATTN_W // HGRN_DH)],
        out_specs=[out] * 4 + [part] * 2,
        out_shape=[o_shape] * 4 + [p_shape] * 2,
        scratch_shapes=[pltpu.VMEM((HGRN_PAIR, HGRN_DH, HGRN_DH), F32), pltpu.VMEM((1, PAIR_W), F32),
                        pltpu.VMEM((1, PAIR_W), F32)],
        compiler_params=pltpu.CompilerParams(dimension_semantics=("parallel", "parallel"),
                                             vmem_limit_bytes=VMEM_LIMIT_BIG),
    )(proj, proj, proj, proj, lb, go, oraw, states, dmix)


def _small_grads(name, dg1, dgm, dg2, dgq, dgk, dbias_t, dlb, dgo, lbp):
    d = dg1.shape[1]

    def body(dg1_ref, dgm_ref, dg2_ref, dgq_ref, dgk_ref, dbias_ref, dlb_ref, dgo_ref, lbp_ref,
             g1_ref, gm_ref, g2_ref, gq_ref, gk_ref, rb_ref, lbg_ref, go_ref):
        g1_ref[...] = jnp.sum(dg1_ref[...], axis=0, keepdims=True)
        gm_ref[...] = jnp.sum(dgm_ref[...], axis=0, keepdims=True)
        g2_ref[...] = jnp.sum(dg2_ref[...], axis=0, keepdims=True)
        r = lax.broadcasted_iota(jnp.int32, (ATTN_W, ATTN_DH), 0)
        cidx = lax.broadcasted_iota(jnp.int32, (ATTN_W, ATTN_DH), 1)
        fold = jnp.where(jnp.bitwise_and(r, ATTN_DH - 1) == cidx, 1.0, 0.0).astype(BF16)
        gq_ref[...] = jnp.sum(_dot_exact_rhs(dgq_ref[...], fold), axis=0, keepdims=True)
        gk_ref[...] = jnp.sum(_dot_exact_rhs(dgk_ref[...], fold), axis=0, keepdims=True)
        gosum = jnp.sum(dgo_ref[...], axis=0, keepdims=True)
        go_ref[...] = (gosum[:, 0:HGRN_DH] + gosum[:, HGRN_DH:2 * HGRN_DH]
                       + gosum[:, 2 * HGRN_DH:3 * HGRN_DH] + gosum[:, 3 * HGRN_DH:4 * HGRN_DH])
        p0 = lbp_ref[0:1, :]
        p1 = lbp_ref[1:2, :]
        lbv = 1.0 / (1.0 + jnp.exp(p1 - p0))
        dp0 = jnp.sum(dlb_ref[...], axis=0, keepdims=True) * lbv * (1.0 - lbv)
        lbg_ref[0:1, :] = dp0
        lbg_ref[1:2, :] = -dp0
        sidx = lax.broadcasted_iota(jnp.int32, (BAND, N_REL_PAD), 0)
        ridx = lax.broadcasted_iota(jnp.int32, (BAND, N_REL_PAD), 1)

        def step(tq, acc):
            rel = jnp.clip(tq + KPAD - sidx, -REL_CLIP, REL_CLIP) + REL_CLIP
            onehot = jnp.where(rel == ridx, 1.0, 0.0).astype(BF16)
            return acc + _dot_exact_rhs(dbias_ref[tq], onehot)

        rb_ref[...] = lax.fori_loop(0, CHUNK, step, jnp.zeros((ATTN_HEADS, N_REL_PAD), F32))

    ins = [dg1, dgm, dg2, dgq, dgk, dbias_t, dlb, dgo, lbp]
    outs = [jax.ShapeDtypeStruct((1, d), F32)] * 3 + [jax.ShapeDtypeStruct((1, ATTN_DH), F32)] * 2 + [
        jax.ShapeDtypeStruct((ATTN_HEADS, N_REL_PAD), F32), jax.ShapeDtypeStruct((2, HGRN_W), F32),
        jax.ShapeDtypeStruct((1, HGRN_DH), F32)]
    vm = pl.BlockSpec(memory_space=pltpu.VMEM)
    return pl.pallas_call(
        body,
        name=name,
        in_specs=[vm] * len(ins),
        out_specs=[vm] * len(outs),
        out_shape=outs,
        compiler_params=pltpu.CompilerParams(vmem_limit_bytes=VMEM_LIMIT),
    )(*ins)


def _adam_update(w, g, m, v):
    m2 = ADAM_B1 * m + (1.0 - ADAM_B1) * g
    v2 = ADAM_B2 * v + (1.0 - ADAM_B2) * (g * g)
    m_hat = m2 / (1.0 - ADAM_B1 ** ADAM_STEP)
    v_hat = v2 / (1.0 - ADAM_B2 ** ADAM_STEP)
    delta = -ADAM_LR * (m_hat / (jnp.sqrt(v_hat) + ADAM_EPS) + ADAM_WD * w)
    return delta, m2, v2


def _rows_tile(r):
    for cand in (256, 352, 128, 176, 64, 32, 16):
        if r % cand == 0 and r > cand:
            return cand
    return r


def _pair_sum(name, grad, theirs, core):
    n, half, c = theirs.shape
    tr = _rows_tile(half)
    nth = half // tr

    def body(core_ref, a_ref, b_ref, o_ref):
        o_ref[...] = (a_ref[...] + b_ref[...]).astype(o_ref.dtype)

    spec = pl.BlockSpec((None, tr, c), lambda i, j, core_ref: (i, j, 0))
    return pl.pallas_call(
        body, name=name,
        grid_spec=pltpu.PrefetchScalarGridSpec(
            num_scalar_prefetch=1, grid=(n, nth),
            in_specs=[pl.BlockSpec((None, tr, c), lambda i, j, core_ref: (i, core_ref[0] * nth + j, 0)), spec],
            out_specs=spec),
        out_shape=jax.ShapeDtypeStruct((n, half, c), BF16), compiler_params=_params("parallel", "parallel"),
    )(core, grad, theirs)


def _chip_sum(name, own, parts, chip):
    _, half, c = own.shape
    tr = _rows_tile(half)

    def body(chip_ref, own_ref, p_ref, o_ref):
        me = chip_ref[0]
        mine = own_ref[...].astype(F32)
        flip_x, flip_y, flip_xy = (p_ref[i].astype(F32) for i in range(3))
        acc = None
        for k in range(N_CHIPS):
            rel = jnp.bitwise_xor(me, k)
            term = jnp.where(rel == 0, mine, jnp.where(rel == 2, flip_x, jnp.where(rel == 1, flip_y, flip_xy)))
            acc = term if acc is None else acc + term
        o_ref[...] = acc

    return pl.pallas_call(
        body, name=name,
        grid_spec=pltpu.PrefetchScalarGridSpec(
            num_scalar_prefetch=1, grid=(half // tr,),
            in_specs=[pl.BlockSpec((None, tr, c), lambda j, chip_ref: (chip_ref[0], j, 0)),
                      pl.BlockSpec((3, tr, c), lambda j, chip_ref: (0, j, 0))],
            out_specs=pl.BlockSpec((tr, c), lambda j, chip_ref: (j, 0))),
        out_shape=jax.ShapeDtypeStruct((half, c), F32), compiler_params=_params("parallel"),
    )(chip, own, parts)


def _adamw(name, w, g_mine, g_theirs, m, v, core):
    _, r, c = w.shape
    half = r // 2
    tr = _rows_tile(half)
    nth = half // tr

    def body(core_ref, w_ref, gm_ref, gt_ref, m_ref, v_ref, g_ref, d_ref, m2_ref, v2_ref):
        g = jnp.where(pl.program_id(0) == core_ref[0], gm_ref[...], gt_ref[...])
        delta, m2, v2 = _adam_update(w_ref[...], g, m_ref[...], v_ref[...])
        g_ref[...] = g
        d_ref[...] = delta
        m2_ref[...] = m2
        v2_ref[...] = v2

    full = pl.BlockSpec((None, tr, c), lambda h, j, core_ref: (0, h * nth + j, 0))
    part = pl.BlockSpec((tr, c), lambda h, j, core_ref: (j, 0))
    shape = jax.ShapeDtypeStruct((1, r, c), F32)
    return pl.pallas_call(
        body, name=name,
        grid_spec=pltpu.PrefetchScalarGridSpec(
            num_scalar_prefetch=1, grid=(2, nth), in_specs=[full, part, part, full, full], out_specs=[full] * 4),
        out_shape=[shape] * 4, compiler_params=_params("parallel", "parallel"),
    )(core, w, g_mine, g_theirs, m, v)


def _rel_bias_table(name, rel_bias):
    padded = jnp.pad(rel_bias, ((0, 0), (0, N_REL_PAD - N_REL)))

    def body(rb_ref, o_ref):
        ridx = lax.broadcasted_iota(jnp.int32, (N_REL_PAD, BAND), 0)
        sidx = lax.broadcasted_iota(jnp.int32, (N_REL_PAD, BAND), 1)
        rb = rb_ref[...]

        def step(tq, carry):
            rel = jnp.clip(tq + KPAD - sidx, -REL_CLIP, REL_CLIP) + REL_CLIP
            onehot = jnp.where(rel == ridx, 1.0, 0.0).astype(BF16)
            o_ref[tq] = _dot_exact_rhs(rb, onehot)
            return carry

        lax.fori_loop(0, CHUNK, step, 0)

    vm = pl.BlockSpec(memory_space=pltpu.VMEM)
    table = pl.pallas_call(
        body, name=name, in_specs=[vm], out_specs=vm,
        out_shape=jax.ShapeDtypeStruct((CHUNK, ATTN_HEADS, BAND), F32),
    )(padded)
    return table.transpose(1, 0, 2)


def _adamw_small(name, w, parts, m, v):
    def body(w_ref, p_ref, m_ref, v_ref, g_ref, d_ref, m2_ref, v2_ref):
        g = p_ref[0]
        for i in range(1, N_DEV):
            g = g + p_ref[i]
        delta, m2, v2 = _adam_update(w_ref[...], g, m_ref[...], v_ref[...])
        g_ref[...] = g
        d_ref[...] = delta
        m2_ref[...] = m2
        v2_ref[...] = v2

    vm = pl.BlockSpec(memory_space=pltpu.VMEM)
    shape = jax.ShapeDtypeStruct((SMALL_ROWS, SMALL_COLS), F32)
    return pl.pallas_call(
        body, name=name, in_specs=[vm] * 4, out_specs=[vm] * 4, out_shape=[shape] * 4,
    )(w, parts, m, v)


def _position():
    return lax.axis_index("x"), lax.axis_index("y"), lax.axis_index("c")


def _other_chips(x, y):
    return [(1 - x, y), (x, 1 - y), (1 - x, 1 - y)]


ANY = pl.BlockSpec(memory_space=pl.ANY)


HBM = pl.BlockSpec(memory_space=pltpu.HBM)
SEM = pl.BlockSpec(memory_space=pltpu.SEMAPHORE)
SPLIT_COPY = pltpu.SideEffectType.DATAFLOW_SIDE_EFFECTING


def _gather_copy(shards, outs, send_sem, recv_sem, i, j):
    x, y, c = _position()
    chips = _other_chips(x, y)
    half = shards[i].shape[0] // 2
    rows = pl.ds(pl.multiple_of(c * half, 16), half)
    return pltpu.make_async_remote_copy(
        src_ref=shards[i].at[rows, :], dst_ref=outs[i].at[2 * x + y, rows, :],
        send_sem=send_sem.at[3 * i + j], recv_sem=recv_sem.at[3 * i + j],
        device_id=(chips[j][0], chips[j][1], c), device_id_type=MESH)


def _gather_start(name, shards, after):
    n = len(shards)

    def body(*refs):
        srcs, outs = refs[:n], refs[n:2 * n]
        send_sem, recv_sem = refs[2 * n + len(after)], refs[2 * n + len(after) + 1]
        token = refs[-1]
        for i in range(n):
            for j in range(3):
                _gather_copy(srcs, outs, send_sem, recv_sem, i, j).start()
        token[...] = jnp.zeros_like(token)

    full = [(N_CHIPS,) + s.shape for s in shards]
    res = pl.pallas_call(
        body,
        name=name,
        in_specs=[HBM] * (2 * n) + [ANY] * len(after),
        out_specs=[SEM, SEM] + [HBM] * (2 * n) + [pl.BlockSpec(memory_space=pltpu.VMEM)],
        out_shape=[pltpu.SemaphoreType.DMA((3 * n,)), pltpu.SemaphoreType.DMA((3 * n,))]
        + [pltpu.HBM(s.shape, s.dtype) for s in shards]
        + [pltpu.HBM(shp, s.dtype) for shp, s in zip(full, shards)]
        + [jax.ShapeDtypeStruct((8, LANES), F32)],
        input_output_aliases={i: 2 + i for i in range(2 * n)},
        compiler_params=pltpu.CompilerParams(has_side_effects=SPLIT_COPY),
    )(*[pltpu.with_memory_space_constraint(s, pltpu.HBM) for s in shards],
      *[pltpu.with_memory_space_constraint(lax.empty(shp, s.dtype), pltpu.HBM) for shp, s in zip(full, shards)],
      *after)
    return res[0], res[1], list(res[2:2 + n]), list(res[2 + n:2 + 2 * n]), res[-1]


def _gather_wait(name, send_sem, recv_sem, shards, outs, after):
    n = len(shards)

    def body(*refs):
        srcs, out_refs = refs[:n], refs[n:2 * n]
        send_ref, recv_ref = refs[2 * n], refs[2 * n + 1]
        for i in range(n):
            for j in range(3):
                copy = _gather_copy(srcs, out_refs, send_ref, recv_ref, i, j)
                copy.wait_send()
                copy.wait_recv()

    res = pl.pallas_call(
        body,
        name=name,
        in_specs=[HBM] * (2 * n) + [SEM, SEM, ANY],
        out_specs=[HBM] * (2 * n),
        out_shape=[pltpu.HBM(s.shape, s.dtype) for s in shards] + [pltpu.HBM(o.shape, o.dtype) for o in outs],
        input_output_aliases={i: i for i in range(2 * n)},
        compiler_params=pltpu.CompilerParams(has_side_effects=SPLIT_COPY),
    )(*shards, *outs, send_sem, recv_sem, after)
    return list(res[:n]), list(res[n:])


def _gather_join(name, shards, outs):
    n = len(shards)

    def body(*refs):
        srcs, ins, outs_ = refs[:n], refs[n:2 * n], refs[2 * n:3 * n]
        own_send, own_recv, half_send, half_recv = refs[3 * n:]
        x, y, c = _position()
        chips = _other_chips(x, y)
        copies = []
        for i in range(n):
            copies.append(pltpu.make_async_remote_copy(
                src_ref=srcs[i], dst_ref=outs_[i].at[2 * x + y], send_sem=own_send.at[i], recv_sem=own_recv.at[i],
                device_id=(x, y, 1 - c), device_id_type=MESH))
            half = srcs[i].shape[0] // 2
            rows = pl.ds(pl.multiple_of(c * half, 16), half)
            for j in range(3):
                slot = 2 * chips[j][0] + chips[j][1]
                copies.append(pltpu.make_async_remote_copy(
                    src_ref=ins[i].at[slot, rows, :], dst_ref=outs_[i].at[slot, rows, :],
                    send_sem=half_send.at[3 * i + j], recv_sem=half_recv.at[3 * i + j],
                    device_id=(x, y, 1 - c), device_id_type=MESH))
        for cp in copies:
            cp.start()
        for cp in copies:
            cp.wait()

    return pl.pallas_call(
        body,
        name=name,
        in_specs=[ANY] * (2 * n),
        out_specs=[ANY] * n,
        out_shape=[jax.ShapeDtypeStruct(o.shape, o.dtype) for o in outs],
        input_output_aliases={n + i: i for i in range(n)},
        scratch_shapes=[pltpu.SemaphoreType.DMA((n,))] * 2 + [pltpu.SemaphoreType.DMA((3 * n,))] * 2,
    )(*shards, *outs)


def _pair_exchange(name, grads):
    n = len(grads)

    def body(*refs):
        ins, theirs = refs[:n], refs[n:2 * n]
        send_sem, recv_sem = refs[2 * n:]
        x, y, c = _position()
        copies = []
        for i in range(n):
            half = ins[i].shape[1] // 2
            give = pl.ds(pl.multiple_of((1 - c) * half, 8), half)
            swap = pltpu.make_async_remote_copy(
                src_ref=ins[i].at[:, give, :], dst_ref=theirs[i], send_sem=send_sem.at[i], recv_sem=recv_sem.at[i],
                device_id=(x, y, 1 - c), device_id_type=MESH)
            swap.start()
            copies.append(swap)
        for swap in copies:
            swap.wait()

    return pl.pallas_call(
        body,
        name=name,
        in_specs=[ANY] * n,
        out_specs=[ANY] * n,
        out_shape=[jax.ShapeDtypeStruct((g.shape[0], g.shape[1] // 2, g.shape[2]), g.dtype) for g in grads],
        scratch_shapes=[pltpu.SemaphoreType.DMA((n,))] * 2,
    )(*grads)


def _scatter_copy(srcs, lands, send_sem, recv_sem, i, j):
    x, y, c = _position()
    chips = _other_chips(x, y)
    return pltpu.make_async_remote_copy(
        src_ref=srcs[i].at[2 * chips[j][0] + chips[j][1]], dst_ref=lands[i].at[j],
        send_sem=send_sem.at[3 * i + j], recv_sem=recv_sem.at[3 * i + j],
        device_id=(chips[j][0], chips[j][1], c), device_id_type=MESH)


def _scatter_start(name, sums):
    n = len(sums)

    def body(*refs):
        srcs, lands = refs[:n], refs[n:2 * n]
        send_sem, recv_sem = refs[2 * n], refs[2 * n + 1]
        token = refs[-1]
        for i in range(n):
            for j in range(3):
                _scatter_copy(srcs, lands, send_sem, recv_sem, i, j).start()
        token[...] = jnp.zeros_like(token)

    land_shapes = [(3,) + s.shape[1:] for s in sums]
    res = pl.pallas_call(
        body,
        name=name,
        in_specs=[HBM] * (2 * n),
        out_specs=[SEM, SEM] + [HBM] * (2 * n) + [pl.BlockSpec(memory_space=pltpu.VMEM)],
        out_shape=[pltpu.SemaphoreType.DMA((3 * n,)), pltpu.SemaphoreType.DMA((3 * n,))]
        + [pltpu.HBM(s.shape, s.dtype) for s in sums]
        + [pltpu.HBM(shp, s.dtype) for shp, s in zip(land_shapes, sums)]
        + [jax.ShapeDtypeStruct((8, LANES), F32)],
        input_output_aliases={i: 2 + i for i in range(2 * n)},
        compiler_params=pltpu.CompilerParams(has_side_effects=SPLIT_COPY),
    )(*[pltpu.with_memory_space_constraint(s, pltpu.HBM) for s in sums],
      *[pltpu.with_memory_space_constraint(lax.empty(shp, s.dtype), pltpu.HBM) for shp, s in zip(land_shapes, sums)])
    return res[0], res[1], list(res[2:2 + n]), list(res[2 + n:2 + 2 * n]), res[-1]


def _scatter_wait(name, send_sem, recv_sem, sums, lands, after):
    n = len(sums)

    def body(*refs):
        srcs, land_refs = refs[:n], refs[n:2 * n]
        send_ref, recv_ref = refs[2 * n], refs[2 * n + 1]
        for i in range(n):
            for j in range(3):
                copy = _scatter_copy(srcs, land_refs, send_ref, recv_ref, i, j)
                copy.wait_send()
                copy.wait_recv()

    res = pl.pallas_call(
        body,
        name=name,
        in_specs=[HBM] * (2 * n) + [SEM, SEM, ANY],
        out_specs=[HBM] * (2 * n),
        out_shape=[pltpu.HBM(s.shape, s.dtype) for s in sums] + [pltpu.HBM(l.shape, l.dtype) for l in lands],
        input_output_aliases={i: i for i in range(2 * n)},
        compiler_params=pltpu.CompilerParams(has_side_effects=SPLIT_COPY),
    )(*sums, *lands, send_sem, recv_sem, after)
    return list(res[:n]), list(res[n:])


def _pair_join(name, halves, small=None):
    n = len(halves)
    if small is None:
        def body_plain(*refs):
            ins, outs = refs[:n], refs[n:2 * n]
            send_sem, recv_sem = refs[2 * n:]
            x, y, c = _position()
            swaps = [pltpu.make_async_remote_copy(
                src_ref=ins[i], dst_ref=outs[i], send_sem=send_sem.at[i], recv_sem=recv_sem.at[i],
                device_id=(x, y, 1 - c), device_id_type=MESH) for i in range(n)]
            for swap in swaps:
                swap.start()
            for swap in swaps:
                swap.wait()

        return pl.pallas_call(
            body_plain,
            name=name,
            in_specs=[ANY] * n,
            out_specs=[ANY] * n,
            out_shape=[jax.ShapeDtypeStruct(h.shape, h.dtype) for h in halves],
            scratch_shapes=[pltpu.SemaphoreType.DMA((n,))] * 2,
        )(*halves)

    def body(*refs):
        ins, small_ref = refs[:n], refs[n]
        outs, all_ref = refs[n + 1:2 * n + 1], refs[2 * n + 1]
        send_sem, recv_sem, sm_send, sm_recv, sm_local = refs[2 * n + 2:]
        x, y, c = _position()
        swaps = []
        for i in range(n):
            swap = pltpu.make_async_remote_copy(
                src_ref=ins[i], dst_ref=outs[i], send_sem=send_sem.at[i], recv_sem=recv_sem.at[i],
                device_id=(x, y, 1 - c), device_id_type=MESH)
            swap.start()
            swaps.append(swap)
        me = 4 * x + 2 * y + c
        sm_own = pltpu.make_async_copy(small_ref, all_ref.at[me], sm_local)
        sm_own.start()
        pushes, arrivals = [], []
        for mask in range(1, N_DEV):
            px, py, pc = x ^ (mask >> 2), y ^ ((mask >> 1) & 1), c ^ (mask & 1)
            pushes.append(pltpu.make_async_remote_copy(
                src_ref=small_ref, dst_ref=all_ref.at[me], send_sem=sm_send.at[mask - 1], recv_sem=sm_recv.at[mask - 1],
                device_id=(px, py, pc), device_id_type=MESH))
            arrivals.append(pltpu.make_async_remote_copy(
                src_ref=small_ref, dst_ref=all_ref.at[4 * px + 2 * py + pc], send_sem=sm_send.at[mask - 1],
                recv_sem=sm_recv.at[mask - 1], device_id=(px, py, pc), device_id_type=MESH))
        for cp in pushes:
            cp.start()
        for swap in swaps:
            swap.wait()
        for cp in arrivals:
            cp.wait_recv()
        for cp in pushes:
            cp.wait_send()
        sm_own.wait()

    res = pl.pallas_call(
        body,
        name=name,
        in_specs=[ANY] * (n + 1),
        out_specs=[ANY] * (n + 1),
        out_shape=[jax.ShapeDtypeStruct(h.shape, h.dtype) for h in halves]
        + [jax.ShapeDtypeStruct((N_DEV,) + small.shape, small.dtype)],
        scratch_shapes=[pltpu.SemaphoreType.DMA((n,))] * 2 + [pltpu.SemaphoreType.DMA((N_DEV - 1,))] * 2
        + [pltpu.SemaphoreType.DMA(())],
    )(*halves, small)
    return res[:n], res[n]


def _lower_bound(lbp):
    return jax.nn.softmax(lbp, axis=0)[0:1]


def _local_step(x, target, g1, gm, g2, gq, gk, go, rel_bias, lbp, first_weights, rest_weights, on_grads):
    b, s, d = x.shape
    t = b * s
    x0 = x.reshape(t, d)
    tgt = target.reshape(t, d)
    gq_t = jnp.tile(gq, (1, ATTN_HEADS))
    gk_t = jnp.tile(gk, (1, ATTN_HEADS))
    lb = _lower_bound(lbp)
    bias = _rel_bias_table("rel_bias_table", rel_bias)

    h1 = _rmsnorm_fwd("norm1", x0, g1)
    wg1, wu1, wd1, deps1 = first_weights(h1)
    a1, b1, z1 = _ffn_up("ffn1_up", h1, wg1, wu1, deps1)
    x1 = _ffn_down("ffn1_down", z1, wd1, x0)
    w_in, w_out, wg2, wu2, wd2 = rest_weights(x1)
    ns = w_in.shape[0]
    h2 = _rmsnorm_fwd("norm_mix", x1, gm)
    proj = _in_proj("in_proj", h2, w_in)
    proj3 = proj.reshape(b, s, proj.shape[1])
    table = _band_table(bias)
    qn, kn, vb = _qk_prep("qk_prep", proj3, gq_t, gk_t)
    attn = _attn_fwd("attn_fwd", qn, kn, vb, table).reshape(t, ATTN_W)
    ro, oraw, states = _hgrn_fwd("hgrn_fwd", proj, lb, go, b, s)
    mix = jnp.concatenate([attn, ro], axis=1)
    x2 = _out_proj("out_proj", mix, w_out, x1)
    h3 = _rmsnorm_fwd("norm2", x2, g2)
    a2, b2, z2 = _ffn_up("ffn2_up", h3, wg2, wu2)
    dy, dyh, sq = _ffn_down_loss("ffn2_down_loss", z2, wd2, x2, tgt)
    loss = 0.5 * jnp.sum(sq) / d

    da2, db2 = _ffn_bwd_act("ffn2_bwd_act", dyh, wd2, a2, b2)
    dwd2 = _grad_w_shardrows("ffn2_dwd", z2, dyh)
    dwg2 = _grad_w_shardrows("ffn2_dwg", da2, h3)
    dwu2 = _grad_w_shardrows("ffn2_dwu", db2, h3)
    sent2 = on_grads("ffn2", {"ffn2_w_gate": dwg2, "ffn2_w_up": dwu2, "ffn2_w_down": dwd2})
    dx2, dx2b, dg2 = _ffn_bwd_in("ffn2_bwd_in", da2, db2, wg2, wu2, x2, g2, dy, 1.0)

    dwout = _grad_w_out("dw_out", mix, dx2b)
    dmix = _out_proj_bwd("out_proj_bwd", dx2b, w_out, sent2)
    dqn, dkn, dvn, dbe, dbo = _attn_bwd("attn_bwd", qn, kn, vb, table, dmix.reshape(b, s, dmix.shape[1]))
    dbias = dbe[:, :, :BAND] + dbo[:, :, CHUNK:]
    dpq, dpk, dpv, dgq, dgk = _qk_prep_bwd("qk_prep_bwd", proj3, dqn, dkn, dvn, gq_t, gk_t)
    dpq, dpk, dpv = (a.reshape(t, ATTN_W) for a in (dpq, dpk, dpv))
    dhq, dhf, dhi, dhg, dlb, dgo = _hgrn_bwd("hgrn_bwd", proj, lb, go, oraw, states, dmix, b, s)
    dproj = jnp.concatenate([dpq, dpk, dpv, dhq, dhf, dhi, dhg], axis=1)
    dwin = _grad_w_in("dw_in", h2, dproj, ns)
    sent_mix = on_grads("mix", {"w_in": dwin, "w_out": dwout.reshape(ns, dwout.shape[0] // ns, d)})
    dx1, dx1h, dgm = _in_proj_bwd("in_proj_bwd", dproj, w_in, x1, gm, dx2, 0.5)

    da1, db1 = _ffn_bwd_act("ffn1_bwd_act", dx1h, wd1, a1, b1, sent_mix)
    dwd1 = _grad_w_shardrows("ffn1_dwd", z1, dx1h)
    dwg1 = _grad_w_shardrows("ffn1_dwg", da1, h1)
    dwu1 = _grad_w_shardrows("ffn1_dwu", db1, h1)
    on_grads("ffn1", {"ffn1_w_gate": dwg1, "ffn1_w_up": dwu1, "ffn1_w_down": dwd1})
    dx0, dg1 = _ffn_bwd_in("ffn1_bwd_in", da1, db1, wg1, wu1, x0, g1, dx1, None)

    nt = dg1.shape[0]
    sg = _small_grads(
        "small_grads", dg1.reshape(nt, d), dgm.reshape(nt, d), dg2.reshape(nt, d),
        dgq.reshape(-1, ATTN_W), dgk.reshape(-1, ATTN_W), dbias.transpose(1, 0, 2),
        dlb.reshape(b, HGRN_W), dgo.reshape(b, HGRN_W), lbp)
    g1g, gmg, g2g, gqg, gkg, rbg, lbg, gog = sg
    small = _pack_small(g1g, gmg, g2g, lbg, rbg[:, :N_REL], gqg, gkg, gog)
    return loss, dx0.reshape(b, s, d), small


def _pack_small(g1, gm, g2, lbp, rel_bias, gq, gk, go):
    flat = [g1.reshape(-1), gm.reshape(-1), g2.reshape(-1), lbp.reshape(-1), rel_bias.reshape(-1)]
    n_bias = 3 * SMALL_COLS - rel_bias.size
    heads = [gq.reshape(-1), gk.reshape(-1), go.reshape(-1)]
    n_tail = SMALL_COLS - sum(h.size for h in heads)
    return jnp.concatenate(flat + [jnp.zeros((n_bias,), F32)] + heads + [jnp.zeros((n_tail,), F32)]).reshape(
        SMALL_ROWS, SMALL_COLS)


def _unpack_small(p, d):
    flat = p.reshape(-1)
    o = 3 * d
    g1, gm, g2 = p[0:1], p[1:2], p[2:3]
    lbp = flat[o:o + 2 * HGRN_W].reshape(2, HGRN_W)
    o = 4 * SMALL_COLS
    rel = flat[o:o + ATTN_HEADS * N_REL].reshape(1, ATTN_HEADS, N_REL)
    o = 7 * SMALL_COLS
    gq = flat[o:o + ATTN_DH].reshape(1, ATTN_DH)
    gk = flat[o + ATTN_DH:o + 2 * ATTN_DH].reshape(1, ATTN_DH)
    go = flat[o + 2 * ATTN_DH:o + 2 * ATTN_DH + HGRN_DH].reshape(1, HGRN_DH)
    return g1, gm, g2, gq, gk, rel, lbp, go


def kernel(x, ffn1_norm_g, ffn1_w_gate, ffn1_w_up, ffn1_w_down, mix_norm_g, w_in, attn_q_norm_g, attn_k_norm_g, attn_rel_bias, hgrn_lower_bounds, hgrn_out_norm_g, w_out, ffn2_norm_g, ffn2_w_gate, ffn2_w_up, ffn2_w_down, loss_target, m_ffn1_norm_g, m_ffn1_w_gate, m_ffn1_w_up, m_ffn1_w_down, m_mix_norm_g, m_w_in, m_attn_q_norm_g, m_attn_k_norm_g, m_attn_rel_bias, m_hgrn_lower_bounds, m_hgrn_out_norm_g, m_w_out, m_ffn2_norm_g, m_ffn2_w_gate, m_ffn2_w_up, m_ffn2_w_down, v_ffn1_norm_g, v_ffn1_w_gate, v_ffn1_w_up, v_ffn1_w_down, v_mix_norm_g, v_w_in, v_attn_q_norm_g, v_attn_k_norm_g, v_attn_rel_bias, v_hgrn_lower_bounds, v_hgrn_out_norm_g, v_w_out, v_ffn2_norm_g, v_ffn2_w_gate, v_ffn2_w_up, v_ffn2_w_down):
    d = x.shape[-1]
    big_w = [ffn1_w_gate, ffn1_w_up, ffn1_w_down, w_in, w_out, ffn2_w_gate, ffn2_w_up, ffn2_w_down]
    big_m = [m_ffn1_w_gate, m_ffn1_w_up, m_ffn1_w_down, m_w_in, m_w_out, m_ffn2_w_gate, m_ffn2_w_up, m_ffn2_w_down]
    big_v = [v_ffn1_w_gate, v_ffn1_w_up, v_ffn1_w_down, v_w_in, v_w_out, v_ffn2_w_gate, v_ffn2_w_up, v_ffn2_w_down]
    big_names = ["ffn1_w_gate", "ffn1_w_up", "ffn1_w_down", "w_in", "w_out", "ffn2_w_gate", "ffn2_w_up", "ffn2_w_down"]
    flipped = {nm for nm in big_names if nm.endswith("gate") or nm.endswith("up")}
    flip = lambda nm, a: jnp.swapaxes(a, 1, 2) if nm in flipped else a
    big_w, big_m, big_v = ([flip(nm, a) for nm, a in zip(big_names, arrs)] for arrs in (big_w, big_m, big_v))

    shards = [w[0].astype(BF16) for w in big_w]
    start_a = _gather_start("gather_start_ffn1", shards[:3], ())
    start_b = _gather_start("gather_start_rest", shards[3:], (start_a[4],))

    def gathered(tag, started, after):
        send_sem, recv_sem, srcs, outs, _ = started
        srcs, outs = _gather_wait("gather_wait_" + tag, send_sem, recv_sem, srcs, outs, after)
        return _gather_join("gather_join_" + tag, srcs, outs)

    def first_weights(after):
        return (*gathered("ffn1", start_a, after), (start_b[4],))

    def rest_weights(after):
        win_f, wout_f, wg2, wu2, wd2 = gathered("rest", start_b, after)
        return win_f, wout_f.reshape(wout_f.shape[0] * wout_f.shape[1], d), wg2, wu2, wd2

    core = lax.axis_index("c").astype(jnp.int32).reshape(1)
    chip = (2 * lax.axis_index("x") + lax.axis_index("y")).astype(jnp.int32).reshape(1)
    started = {}

    def on_grads(tag, grads):
        names = list(grads)
        theirs = _pair_exchange("pair_exchange_" + tag, [grads[nm] for nm in names])
        sums = [_pair_sum("pair_sum_" + nm, grads[nm], th, core) for nm, th in zip(names, theirs)]
        started[tag] = (names, _scatter_start("scatter_start_" + tag, sums))
        return (started[tag][1][4],)

    loss, grad_x, small_g = _local_step(
        x, loss_target, ffn1_norm_g, mix_norm_g, ffn2_norm_g, attn_q_norm_g, attn_k_norm_g, hgrn_out_norm_g,
        attn_rel_bias[0], hgrn_lower_bounds, first_weights, rest_weights, on_grads)
    loss = lax.psum(loss, ("x", "y", "c"))

    def finish(tag, after):
        names, (send_sem, recv_sem, sums, lands, _) = started[tag]
        sums, lands = _scatter_wait("scatter_wait_" + tag, send_sem, recv_sem, sums, lands, after)
        return names, [_chip_sum("chip_sum_" + nm, sm, ld, chip) for nm, sm, ld in zip(names, sums, lands)]

    by_name = {nm: (w, m, v) for nm, w, m, v in zip(big_names, big_w, big_m, big_v)}
    updated = {}

    def update(names, halves, other_halves):
        for nm, mine, theirs in zip(names, halves, other_halves):
            w, m, v = by_name[nm]
            updated[nm] = _adamw("adamw_" + nm, w, mine, theirs, m, v, core)

    last_token = started["ffn1"][1][4]
    names_a, halves_a = finish("ffn2", last_token)
    names_m, halves_m = finish("mix", last_token)
    names_a, halves_a = names_a + names_m, halves_a + halves_m
    update(names_a, halves_a, _pair_join("pair_join_early", halves_a))
    names_b, halves_b = finish("ffn1", updated["w_out"][1])
    others_b, small_all = _pair_join("pair_join_last", halves_b, small_g)
    update(names_b, halves_b, others_b)
    big_out = [updated[nm] for nm in big_names]

    pack = lambda g1, gm, g2, gq, gk, rel, lbp, go: _pack_small(g1, gm, g2, lbp, rel[0], gq, gk, go)
    small_w = pack(ffn1_norm_g, mix_norm_g, ffn2_norm_g, attn_q_norm_g, attn_k_norm_g, attn_rel_bias, hgrn_lower_bounds, hgrn_out_norm_g)
    small_m = pack(m_ffn1_norm_g, m_mix_norm_g, m_ffn2_norm_g, m_attn_q_norm_g, m_attn_k_norm_g, m_attn_rel_bias, m_hgrn_lower_bounds, m_hgrn_out_norm_g)
    small_v = pack(v_ffn1_norm_g, v_mix_norm_g, v_ffn2_norm_g, v_attn_q_norm_g, v_attn_k_norm_g, v_attn_rel_bias, v_hgrn_lower_bounds, v_hgrn_out_norm_g)
    small_out = [_unpack_small(p, d) for p in _adamw_small("adamw_small", small_w, small_all, small_m, small_v)]

    def assemble(kind):
        bg = [flip(nm, o[kind]) for nm, o in zip(big_names, big_out)]
        g1, gm, g2, gq, gk, rel, lbp, go = small_out[kind]
        return [g1, bg[0], bg[1], bg[2], gm, bg[3], gq, gk, rel, lbp, go, bg[4], g2, bg[5], bg[6], bg[7]]

    return (loss, grad_x, *assemble(0), *assemble(1), *assemble(2), *assemble(3))
```

```python
import functools

import jax
import jax.numpy as jnp
from jax import lax
from jax.experimental import pallas as pl
from jax.experimental.pallas import tpu as pltpu

F32 = jnp.float32
BF16 = jnp.bfloat16
MESH = pl.DeviceIdType.MESH

N_CHIPS = 4
N_DEV = 8
CHUNK = 64
ATTN_HEADS = 8
ATTN_DH = 64
ATTN_W = ATTN_HEADS * ATTN_DH
HGRN_HEADS = 4
HGRN_DH = 128
HGRN_W = HGRN_HEADS * HGRN_DH
LEFT_CHUNKS = 8
BAND = (LEFT_CHUNKS + 1) * CHUNK
KPAD = LEFT_CHUNKS * CHUNK
REL_CLIP = 128
N_REL = 2 * REL_CLIP + 1
N_REL_PAD = 384
RMS_EPS = 1e-6
LANES = 128
SMALL_ROWS = 8
SMALL_COLS = 1024

ADAM_LR = 0.001
ADAM_B1 = 0.9
ADAM_B2 = 0.999
ADAM_EPS = 1e-08
ADAM_WD = 0.01
ADAM_STEP = 10

NN = (((1,), (0,)), ((), ()))
NT = (((1,), (1,)), ((), ()))
TN = (((0,), (0,)), ((), ()))

VMEM_LIMIT = 48 * 1024 * 1024
VMEM_LIMIT_BIG = 56 * 1024 * 1024


def _sigmoid(x):
    return 1.0 / (1.0 + jnp.exp(-x))


def _silu(x):
    return x * _sigmoid(x)


def _dot(a, b, dims=NN):
    return lax.dot_general(a, b, dims, preferred_element_type=F32)


def _split3(x):
    hi = x.astype(BF16)
    r1 = x - hi.astype(F32)
    mid = r1.astype(BF16)
    lo = (r1 - mid.astype(F32)).astype(BF16)
    return hi, mid, lo


def _dot_exact_rhs(x, mat, dims=NN):
    hi, mid, lo = _split3(x)
    return _dot(hi, mat, dims) + _dot(mid, mat, dims) + _dot(lo, mat, dims)


def _dot_exact_lhs(mat, x, dims=NN):
    hi, mid, lo = _split3(x)
    return _dot(mat, hi, dims) + _dot(mat, mid, dims) + _dot(mat, lo, dims)


def _params(*sem):
    return pltpu.CompilerParams(dimension_semantics=sem, vmem_limit_bytes=VMEM_LIMIT)


def _mm(name, ins, terms, n_acc, grid, acc_shape, outs, epilogue, extras=(), deps=()):
    nk = grid[2]
    ni, ne, nd, no = len(ins), len(extras), len(deps), len(outs)

    def body(*refs):
        in_refs = refs[:ni]
        ex_refs = refs[ni:ni + ne]
        out_refs = refs[ni + ne + nd:ni + ne + nd + no]
        acc_refs = refs[ni + ne + nd + no:]
        parts = [None] * n_acc
        for ai, li, ri, dims in terms:
            d = _dot(in_refs[li][...], in_refs[ri][...], dims)
            parts[ai] = d if parts[ai] is None else parts[ai] + d

        def finish(accs):
            res = epilogue(accs, [e[...] for e in ex_refs])
            for o, r in zip(out_refs, res):
                o[...] = r.astype(o.dtype)

        if nk == 1:
            finish(parts)
        else:
            k = pl.program_id(2)

            @pl.when(k == 0)
            def _():
                for a, p in zip(acc_refs, parts):
                    a[...] = p

            @pl.when(k > 0)
            def _():
                for a, p in zip(acc_refs, parts):
                    a[...] += p

            @pl.when(k == nk - 1)
            def _():
                finish([a[...] for a in acc_refs])

    scratch = [] if nk == 1 else [pltpu.VMEM(acc_shape, F32) for _ in range(n_acc)]
    res = pl.pallas_call(
        body,
        name=name,
        grid=grid,
        in_specs=[s for _, s in ins] + [s for _, s in extras] + [pl.BlockSpec(memory_space=pl.ANY)] * nd,
        out_specs=[s for _, s in outs],
        out_shape=[o for o, _ in outs],
        scratch_shapes=scratch,
        compiler_params=_params("parallel", "parallel", "arbitrary"),
    )(*[a for a, _ in ins], *[a for a, _ in extras], *deps)
    return res


def _mm_rows(name, lhs, weights, dims, t, outs, epilogue, extras=(), deps=()):
    tm = _row_tile(t)
    nl, ne, nd, no = len(lhs), len(extras), len(deps), len(outs)
    ns = weights[0].shape[0]

    def body(*refs):
        lhs_refs = refs[:nl]
        w_hbm = refs[nl:2 * nl]
        ex_refs = refs[2 * nl:2 * nl + ne]
        out_refs = refs[2 * nl + ne + nd:2 * nl + ne + nd + no]
        w_vmem = refs[2 * nl + ne + nd + no:3 * nl + ne + nd + no]
        sem = refs[-1]

        @pl.when(pl.program_id(0) == 0)
        def _():
            copies = [pltpu.make_async_copy(w_hbm[p], w_vmem[p], sem.at[p]) for p in range(nl)]
            for cp in copies:
                cp.start()
            for cp in copies:
                cp.wait()

        acc = None
        for p in range(nl):
            pick = lhs[p][2]
            for j in range(ns):
                part = _dot(pick(lhs_refs[p], j), w_vmem[p][j], dims)
                acc = part if acc is None else acc + part
        res = epilogue([acc], [e[...] for e in ex_refs])
        for o, r in zip(out_refs, res):
            o[...] = r.astype(o.dtype)

    return pl.pallas_call(
        body,
        name=name,
        grid=(t // tm,),
        in_specs=[s for _, s, _ in lhs] + [pl.BlockSpec(memory_space=pl.ANY)] * nl + [s for _, s in extras]
        + [pl.BlockSpec(memory_space=pl.ANY)] * nd,
        out_specs=[s for _, s in outs],
        out_shape=[o for o, _ in outs],
        scratch_shapes=[pltpu.VMEM(w.shape, w.dtype) for w in weights] + [pltpu.SemaphoreType.DMA((nl,))],
        compiler_params=_params("arbitrary"),
    )(*[a for a, _, _ in lhs], *weights, *[a for a, _ in extras], *deps)


def _row_tile(t):
    return 512 if t % 512 == 0 else t


def _k_tile(t):
    return 1024 if t % 1024 == 0 else t


def _rmsnorm_fwd(name, x, g):
    t, d = x.shape
    tm = _row_tile(t)

    def body(x_ref, g_ref, h_ref):
        xv = x_ref[...]
        ms = jnp.mean(xv * xv, axis=-1, keepdims=True)
        h_ref[...] = (xv * lax.rsqrt(ms + RMS_EPS) * g_ref[...]).astype(BF16)

    return pl.pallas_call(
        body,
        name=name,
        grid=(t // tm,),
        in_specs=[pl.BlockSpec((tm, d), lambda i: (i, 0)), pl.BlockSpec((1, d), lambda i: (0, 0))],
        out_specs=pl.BlockSpec((tm, d), lambda i: (i, 0)),
        out_shape=jax.ShapeDtypeStruct((t, d), BF16),
        compiler_params=_params("parallel"),
    )(x, g)


def _norm_bwd_epilogue(copy_scale):
    def epilogue(accs, ex):
        dh = accs[0]
        xv, g, dres = ex
        ms = jnp.mean(xv * xv, axis=-1, keepdims=True)
        rstd = lax.rsqrt(ms + RMS_EPS)
        xhat = xv * rstd
        dxhat = dh * g
        dx = rstd * (dxhat - xhat * jnp.mean(dxhat * xhat, axis=-1, keepdims=True))
        out = dres + dx
        dg = jnp.sum(dh * xhat, axis=0, keepdims=True)
        if copy_scale is None:
            return out, dg
        return out, out * copy_scale, dg

    return epilogue


def _ffn_up(name, h, wg, wu, deps=()):
    t, d = h.shape
    ns, f, _ = wg.shape
    tm = _row_tile(t)

    def epilogue(accs, ex):
        a, b = accs
        return a, b, _silu(a) * b

    w_spec = pl.BlockSpec((None, f, d), lambda j, i, k: (j, 0, 0))
    o_spec = pl.BlockSpec((None, tm, f), lambda j, i, k: (j, i, 0))
    o_shape = jax.ShapeDtypeStruct((ns, t, f), BF16)
    return _mm(
        name,
        ins=[(h, pl.BlockSpec((tm, d), lambda j, i, k: (i, 0))), (wg, w_spec), (wu, w_spec)],
        terms=[(0, 0, 1, NT), (1, 0, 2, NT)],
        n_acc=2,
        grid=(ns, t // tm, 1),
        acc_shape=(tm, f),
        outs=[(o_shape, o_spec)] * 3,
        epilogue=epilogue,
        deps=deps,
    )


def _shard_rows(arr, tm):
    ns, _, f = arr.shape
    return arr, pl.BlockSpec((ns, tm, f), lambda i: (0, i, 0)), lambda ref, j: ref[j]


def _ffn_down(name, z, wd, x):
    _, t, _ = z.shape
    d = wd.shape[2]
    tm = _row_tile(t)
    row = pl.BlockSpec((tm, d), lambda i: (i, 0))
    return _mm_rows(
        name, [_shard_rows(z, tm)], [wd], NN, t,
        outs=[(jax.ShapeDtypeStruct((t, d), F32), row)],
        epilogue=lambda accs, ex: (ex[0] + 0.5 * accs[0],),
        extras=[(x, row)],
    )[0]


def _ffn_down_loss(name, z, wd, x, target):
    _, t, _ = z.shape
    d = wd.shape[2]
    tm = _row_tile(t)
    nt = t // tm
    row = pl.BlockSpec((tm, d), lambda i: (i, 0))

    def epilogue(accs, ex):
        e = ex[0] + 0.5 * accs[0] - ex[1]
        dy = e * (1.0 / d)
        return dy, 0.5 * dy, jnp.sum(e * e, axis=0, keepdims=True)

    return _mm_rows(
        name, [_shard_rows(z, tm)], [wd], NN, t,
        outs=[(jax.ShapeDtypeStruct((t, d), F32), row), (jax.ShapeDtypeStruct((t, d), BF16), row),
              (jax.ShapeDtypeStruct((nt, 1, d), F32), pl.BlockSpec((None, 1, d), lambda i: (i, 0, 0)))],
        epilogue=epilogue,
        extras=[(x, row), (target, row)],
    )


def _ffn_bwd_act(name, dout, wd, a, b, deps=()):
    t, d = dout.shape
    ns, f, _ = wd.shape
    tm = _row_tile(t)

    def epilogue(accs, ex):
        dz = accs[0]
        av = ex[0].astype(F32)
        bv = ex[1].astype(F32)
        sg = _sigmoid(av)
        return dz * bv * (sg * (1.0 + av * (1.0 - sg))), dz * (av * sg)

    act = pl.BlockSpec((None, tm, f), lambda j, i, k: (j, i, 0))
    o_shape = jax.ShapeDtypeStruct((ns, t, f), BF16)
    return _mm(
        name,
        ins=[(dout, pl.BlockSpec((tm, d), lambda j, i, k: (i, 0))),
             (wd, pl.BlockSpec((None, f, d), lambda j, i, k: (j, 0, 0)))],
        terms=[(0, 0, 1, NT)],
        n_acc=1,
        grid=(ns, t // tm, 1),
        acc_shape=(tm, f),
        outs=[(o_shape, act)] * 2,
        epilogue=epilogue,
        extras=[(a, act), (b, act)],
        deps=deps,
    )


def _grad_w_shardrows(name, z, dout):
    ns, t, f = z.shape
    d = dout.shape[1]
    tk = _k_tile(t)
    return _mm(
        name,
        ins=[(z, pl.BlockSpec((None, tk, f), lambda j, n, k: (j, k, 0))),
             (dout, pl.BlockSpec((tk, d), lambda j, n, k: (k, 0)))],
        terms=[(0, 0, 1, TN)],
        n_acc=1,
        grid=(ns, 1, t // tk),
        acc_shape=(f, d),
        outs=[(jax.ShapeDtypeStruct((ns, f, d), F32), pl.BlockSpec((None, f, d), lambda j, n, k: (j, 0, 0)))],
        epilogue=lambda accs, ex: (accs[0],),
    )[0]


def _norm_bwd_outs(t, d, tm, copy_scale):
    row = pl.BlockSpec((tm, d), lambda i: (i, 0))
    outs = [(jax.ShapeDtypeStruct((t, d), F32), row)]
    if copy_scale is not None:
        outs.append((jax.ShapeDtypeStruct((t, d), BF16), row))
    outs.append((jax.ShapeDtypeStruct((t // tm, 1, d), F32), pl.BlockSpec((None, 1, d), lambda i: (i, 0, 0))))
    return row, outs


def _ffn_bwd_in(name, da, db, wg, wu, x, g, dres, copy_scale):
    _, t, _ = da.shape
    d = wg.shape[2]
    tm = _row_tile(t)
    row, outs = _norm_bwd_outs(t, d, tm, copy_scale)
    return _mm_rows(
        name, [_shard_rows(da, tm), _shard_rows(db, tm)], [wg, wu], NN, t,
        outs=outs,
        epilogue=_norm_bwd_epilogue(copy_scale),
        extras=[(x, row), (g, pl.BlockSpec((1, d), lambda i: (0, 0))), (dres, row)],
    )


def _in_proj(name, h, w_in):
    t, d = h.shape
    ns, _, pj = w_in.shape
    tm = _row_tile(t)
    return _mm(
        name,
        ins=[(h, pl.BlockSpec((tm, d), lambda j, i, k: (i, 0))),
             (w_in, pl.BlockSpec((None, d, pj), lambda j, i, k: (j, 0, 0)))],
        terms=[(0, 0, 1, NN)],
        n_acc=1,
        grid=(ns, t // tm, 1),
        acc_shape=(tm, pj),
        outs=[(jax.ShapeDtypeStruct((t, ns * pj), F32), pl.BlockSpec((tm, pj), lambda j, i, k: (i, j)))],
        epilogue=lambda accs, ex: (accs[0],),
    )[0]


def _in_proj_bwd(name, dp, w_in, x, g, dres, copy_scale):
    t = dp.shape[0]
    ns, d, pj = w_in.shape
    tm = _row_tile(t)
    row, outs = _norm_bwd_outs(t, d, tm, copy_scale)
    cols = (dp, pl.BlockSpec((tm, ns * pj), lambda i: (i, 0)), lambda ref, j: ref[:, j * pj:(j + 1) * pj])
    return _mm_rows(
        name, [cols], [w_in], NT, t,
        outs=outs,
        epilogue=_norm_bwd_epilogue(copy_scale),
        extras=[(x, row), (g, pl.BlockSpec((1, d), lambda i: (0, 0))), (dres, row)],
    )


def _grad_w_in(name, h, dp, ns):
    t, d = h.shape
    pj = dp.shape[1] // ns
    tk = _k_tile(t)
    return _mm(
        name,
        ins=[(h, pl.BlockSpec((tk, d), lambda j, n, k: (k, 0))),
             (dp, pl.BlockSpec((tk, pj), lambda j, n, k: (k, j)))],
        terms=[(0, 0, 1, TN)],
        n_acc=1,
        grid=(ns, 1, t // tk),
        acc_shape=(d, pj),
        outs=[(jax.ShapeDtypeStruct((ns, d, pj), F32), pl.BlockSpec((None, d, pj), lambda j, n, k: (j, 0, 0)))],
        epilogue=lambda accs, ex: (accs[0],),
    )[0]


def _out_proj(name, mix, w_out, x):
    t, dm = mix.shape
    d = w_out.shape[1]
    tm = _row_tile(t)
    row = pl.BlockSpec((tm, d), lambda i, n, k: (i, 0))
    return _mm(
        name,
        ins=[(mix, pl.BlockSpec((tm, dm), lambda i, n, k: (i, 0))),
             (w_out, pl.BlockSpec((dm, d), lambda i, n, k: (0, 0)))],
        terms=[(0, 0, 1, NN)],
        n_acc=1,
        grid=(t // tm, 1, 1),
        acc_shape=(tm, d),
        outs=[(jax.ShapeDtypeStruct((t, d), F32), row)],
        epilogue=lambda accs, ex: (ex[0] + accs[0],),
        extras=[(x, row)],
    )[0]


def _out_proj_bwd(name, dx, w_out, deps=()):
    t, d = dx.shape
    dm = w_out.shape[0]
    tm = _row_tile(t)
    return _mm(
        name,
        ins=[(dx, pl.BlockSpec((tm, d), lambda i, n, k: (i, 0))),
             (w_out, pl.BlockSpec((dm, d), lambda i, n, k: (0, 0)))],
        terms=[(0, 0, 1, NT)],
        n_acc=1,
        grid=(t // tm, 1, 1),
        acc_shape=(tm, dm),
        outs=[(jax.ShapeDtypeStruct((t, dm), F32), pl.BlockSpec((tm, dm), lambda i, n, k: (i, 0)))],
        epilogue=lambda accs, ex: (accs[0],),
        deps=deps,
    )[0]


def _grad_w_out(name, mix, dx):
    t, dm = mix.shape
    d = dx.shape[1]
    tk = _k_tile(t)
    return _mm(
        name,
        ins=[(mix, pl.BlockSpec((tk, dm), lambda a, n, k: (k, 0))),
             (dx, pl.BlockSpec((tk, d), lambda a, n, k: (k, 0)))],
        terms=[(0, 0, 1, TN)],
        n_acc=1,
        grid=(1, 1, t // tk),
        acc_shape=(dm, d),
        outs=[(jax.ShapeDtypeStruct((dm, d), F32), pl.BlockSpec((dm, d), lambda a, n, k: (0, 0)))],
        epilogue=lambda accs, ex: (accs[0],),
    )[0]


def _head_group_matrix():
    r = lax.broadcasted_iota(jnp.int32, (ATTN_W, ATTN_W), 0)
    c = lax.broadcasted_iota(jnp.int32, (ATTN_W, ATTN_W), 1)
    same = jnp.right_shift(r, 6) == jnp.right_shift(c, 6)
    return jnp.where(same, 1.0, 0.0).astype(BF16)


def _qk_prep(name, proj, gq, gk):
    b, s, _ = proj.shape
    tm = KPAD
    nb = s // tm

    def body(q_ref, k_ref, v_ref, gq_ref, gk_ref, qn_ref, kn_ref, vb_ref):
        j = pl.program_id(1)
        bd = _head_group_matrix()

        def norm(xv, g):
            ms = _dot_exact_rhs(xv * xv, bd) * (1.0 / ATTN_DH)
            return xv * lax.rsqrt(ms + RMS_EPS) * g

        @pl.when(j == 0)
        def _():
            kn_ref[...] = jnp.zeros_like(kn_ref)
            vb_ref[...] = jnp.zeros_like(vb_ref)

        @pl.when(j > 0)
        def _():
            qn_ref[...] = norm(q_ref[...], gq_ref[...]).astype(BF16)
            kn_ref[...] = norm(k_ref[...], gk_ref[...]).astype(BF16)
            vb_ref[...] = v_ref[...].astype(BF16)

    src_blk = lambda col: pl.BlockSpec((None, tm, ATTN_W), lambda bi, j: (bi, jnp.maximum(j - 1, 0), col))
    gspec = pl.BlockSpec((1, ATTN_W), lambda bi, j: (0, 0))
    padded = pl.BlockSpec((None, tm, ATTN_W), lambda bi, j: (bi, j, 0))
    return pl.pallas_call(
        body,
        name=name,
        grid=(b, nb + 1),
        in_specs=[src_blk(0), src_blk(1), src_blk(2), gspec, gspec],
        out_specs=[src_blk(0), padded, padded],
        out_shape=[jax.ShapeDtypeStruct((b, s, ATTN_W), BF16), jax.ShapeDtypeStruct((b, KPAD + s, ATTN_W), BF16),
                   jax.ShapeDtypeStruct((b, KPAD + s, ATTN_W), BF16)],
        compiler_params=_params("parallel", "arbitrary"),
    )(proj, proj, proj, gq, gk)


def _qk_prep_bwd(name, proj, dqn, dkn, dv, gq, gk):
    b, s, _ = proj.shape
    tm = KPAD
    nb = s // tm

    def body(q_ref, k_ref, dqn_ref, dkn_ref, dv_ref, gq_ref, gk_ref, dq_ref, dk_ref, dvb_ref, dgq_ref, dgk_ref):
        bd = _head_group_matrix()

        def bwd(xv, dy, g):
            ms = _dot_exact_rhs(xv * xv, bd) * (1.0 / ATTN_DH)
            rstd = lax.rsqrt(ms + RMS_EPS)
            xhat = xv * rstd
            dxhat = dy * g
            gm = _dot_exact_rhs(dxhat * xhat, bd) * (1.0 / ATTN_DH)
            return rstd * (dxhat - xhat * gm), jnp.sum(dy * xhat, axis=0, keepdims=True)

        dq, dgq = bwd(q_ref[...], dqn_ref[...], gq_ref[...])
        dk, dgk = bwd(k_ref[...], dkn_ref[...], gk_ref[...])
        dq_ref[...] = dq.astype(BF16)
        dk_ref[...] = dk.astype(BF16)
        dvb_ref[...] = dv_ref[...].astype(BF16)
        dgq_ref[...] = dgq
        dgk_ref[...] = dgk

    col = lambda c: pl.BlockSpec((None, tm, ATTN_W), lambda bi, j: (bi, j, c))
    past_pad = pl.BlockSpec((None, tm, ATTN_W), lambda bi, j: (bi, j + 1, 0))
    gspec = pl.BlockSpec((1, ATTN_W), lambda bi, j: (0, 0))
    pspec = pl.BlockSpec((None, 1, ATTN_W), lambda bi, j: (bi * nb + j, 0, 0))
    o_shape = jax.ShapeDtypeStruct((b, s, ATTN_W), BF16)
    p_shape = jax.ShapeDtypeStruct((b * nb, 1, ATTN_W), F32)
    return pl.pallas_call(
        body,
        name=name,
        grid=(b, nb),
        in_specs=[col(0), col(1), col(0), past_pad, past_pad, gspec, gspec],
        out_specs=[col(0)] * 3 + [pspec] * 2,
        out_shape=[o_shape] * 3 + [p_shape] * 2,
        compiler_params=_params("parallel", "parallel"),
    )(proj, proj, dqn, dkn, dv, gq, gk)


Q_CHUNKS = 4
QBLK = Q_CHUNKS * CHUNK
WIN = (LEFT_CHUNKS + Q_CHUNKS) * CHUNK
DB_W = BAND + CHUNK
MASKED = -1e30


def _band_table(bias):
    rows = [jnp.pad(bias, ((0, 0), (0, 0), (CHUNK * i, WIN - BAND - CHUNK * i)), constant_values=MASKED)
            for i in range(Q_CHUNKS)]
    return jnp.concatenate(rows, axis=1)


def _head_lanes(hh):
    lane = lax.broadcasted_iota(jnp.int32, (1, LANES), 1)
    return (lane < ATTN_DH) if hh == 0 else (lane >= ATTN_DH)


def _attn_probs(qh, kw, table, start):
    s = _dot(qh, kw, NT) * (ATTN_DH ** -0.5) + table
    col = lax.broadcasted_iota(jnp.int32, (QBLK, WIN), 1)
    s = jnp.where(col + start >= KPAD, s, MASKED)
    m = jnp.max(s, axis=-1, keepdims=True)
    p = jnp.exp(s - m)
    return p * (1.0 / jnp.sum(p, axis=-1, keepdims=True))


def _attn_fwd(name, q, k, v, table):
    b, s, w = q.shape
    sp = k.shape[1]

    def body(q_ref, k_ref, v_ref, t_ref, o_ref):
        start = pl.multiple_of(pl.program_id(2) * QBLK, QBLK)
        kw = k_ref[pl.ds(start, WIN), :]
        vw = v_ref[pl.ds(start, WIN), :]
        q2 = q_ref[...]
        out = jnp.zeros((QBLK, LANES), F32)
        for hh in range(2):
            mine = _head_lanes(hh)
            p = _attn_probs(jnp.where(mine, q2, jnp.zeros_like(q2)), kw, t_ref[hh], start)
            out = jnp.where(mine, _dot(p.astype(BF16), vw), out)
        o_ref[...] = out.astype(BF16)

    qspec = pl.BlockSpec((None, QBLK, LANES), lambda p, bi, i: (bi, i, p))
    kspec = pl.BlockSpec((None, sp, LANES), lambda p, bi, i: (bi, 0, p))
    return pl.pallas_call(
        body,
        name=name,
        grid=(w // LANES, b, s // QBLK),
        in_specs=[qspec, kspec, kspec, pl.BlockSpec((2, QBLK, WIN), lambda p, bi, i: (p, 0, 0))],
        out_specs=qspec,
        out_shape=jax.ShapeDtypeStruct((b, s, w), BF16),
        compiler_params=_params("parallel", "parallel", "arbitrary"),
    )(q, k, v, table)


def _attn_bwd(name, q, k, v, table, dmix):
    b, s, w = q.shape
    sp = k.shape[1]

    def body(q_ref, k_ref, v_ref, t_ref, do_ref, dq_ref, dk_ref, dv_ref, dbe_ref, dbo_ref):
        bi = pl.program_id(1)
        i = pl.program_id(2)
        start = pl.multiple_of(i * QBLK, QBLK)
        win = pl.ds(start, WIN)

        @pl.when(i == 0)
        def _():
            dk_ref[...] = jnp.zeros_like(dk_ref)
            dv_ref[...] = jnp.zeros_like(dv_ref)

        @pl.when(jnp.logical_and(i == 0, bi == 0))
        def _():
            dbe_ref[...] = jnp.zeros_like(dbe_ref)
            dbo_ref[...] = jnp.zeros_like(dbo_ref)

        kw = k_ref[win, :]
        vw = v_ref[win, :]
        q2 = q_ref[...]
        do2 = do_ref[...].astype(BF16)
        dq = jnp.zeros((QBLK, LANES), F32)
        for hh in range(2):
            mine = _head_lanes(hh)
            qh = jnp.where(mine, q2, jnp.zeros_like(q2))
            doh = jnp.where(mine, do2, jnp.zeros_like(do2))
            p = _attn_probs(qh, kw, t_ref[hh], start)
            dp = _dot(doh, vw, NT)
            ds = p * (dp - jnp.sum(p * dp, axis=-1, keepdims=True))
            for qi in range(Q_CHUNKS):
                c0 = (qi // 2) * LANES
                blk = ds[qi * CHUNK:(qi + 1) * CHUNK, c0:c0 + DB_W]
                if qi % 2 == 0:
                    dbe_ref[hh] += blk
                else:
                    dbo_ref[hh] += blk
            dsb = (ds * (ATTN_DH ** -0.5)).astype(BF16)
            dq = jnp.where(mine, _dot(dsb, kw), dq)
            dk_ref[win, :] += _dot(dsb, qh, TN)
            dv_ref[win, :] += _dot(p.astype(BF16), doh, TN)
        dq_ref[...] = dq

    qspec = pl.BlockSpec((None, QBLK, LANES), lambda p, bi, i: (bi, i, p))
    kspec = pl.BlockSpec((None, sp, LANES), lambda p, bi, i: (bi, 0, p))
    dbspec = pl.BlockSpec((2, CHUNK, DB_W), lambda p, bi, i: (p, 0, 0))
    db_shape = jax.ShapeDtypeStruct((ATTN_HEADS, CHUNK, DB_W), F32)
    return pl.pallas_call(
        body,
        name=name,
        grid=(w // LANES, b, s // QBLK),
        in_specs=[qspec, kspec, kspec, pl.BlockSpec((2, QBLK, WIN), lambda p, bi, i: (p, 0, 0)), qspec],
        out_specs=[qspec, kspec, kspec, dbspec, dbspec],
        out_shape=[jax.ShapeDtypeStruct((b, s, w), F32), jax.ShapeDtypeStruct((b, sp, w), F32),
                   jax.ShapeDtypeStruct((b, sp, w), F32), db_shape, db_shape],
        compiler_params=_params("arbitrary", "arbitrary", "arbitrary"),
    )(q, k, v, table, dmix)


HQ_COL = 3 * ATTN_W // HGRN_DH
HF_COL = HQ_COL + HGRN_HEADS
HI_COL = HF_COL + HGRN_HEADS
HG_COL = HI_COL + HGRN_HEADS
HGRN_PAIR = 2
PAIR_W = HGRN_PAIR * HGRN_DH


def _tri(lower):
    r = lax.broadcasted_iota(jnp.int32, (CHUNK, CHUNK), 0)
    c = lax.broadcasted_iota(jnp.int32, (CHUNK, CHUNK), 1)
    return (r >= c) if lower else (r <= c)


def _hgrn_chunk(hq, hf, lb, tril):
    sig = _sigmoid(hf)
    f = lb + (1.0 - lb) * sig
    g = jnp.log(f)
    ones_l = jnp.where(tril, 1.0, 0.0).astype(BF16)
    b = _dot_exact_lhs(ones_l, g)
    bl = jnp.sum(g, axis=0, keepdims=True)
    rows = lax.broadcasted_iota(jnp.int32, g.shape, 0)
    bm = jnp.sum(jnp.where(rows <= CHUNK // 2, g, 0.0), axis=0, keepdims=True)
    sq = _sigmoid(hq)
    q = hq * sq
    k = 1.0 - f
    return sig, f, b, bl, bm, sq, q, k


def _hgrn_fwd(name, proj, lb, go, b, s):
    nc = s // CHUNK
    t = b * s

    def body(hq_ref, hf_ref, hi_ref, hg_ref, lb_ref, go_ref, ro_ref, oraw_ref, st_ref, s_scr):
        tril = _tri(True)
        gov = go_ref[...]
        s_scr[...] = jnp.zeros_like(s_scr)

        def step(c, carry):
            sl = pl.ds(pl.multiple_of(c * CHUNK, CHUNK), CHUNK)
            for hh in range(HGRN_PAIR):
                cols = pl.ds(hh * HGRN_DH, HGRN_DH)
                lbv = lb_ref[:, cols]
                hg = hg_ref[sl, cols]
                _, _, bb, bl, bm, _, q, k = _hgrn_chunk(hq_ref[sl, cols], hf_ref[sl, cols], lbv, tril)
                vb = hi_ref[sl, cols].astype(BF16)
                qe = (q * jnp.exp(bb - bm)).astype(BF16)
                ke = (k * jnp.exp(bm - bb)).astype(BF16)
                a = jnp.where(tril, _dot(qe, ke, NT), 0.0)
                st = s_scr[hh]
                st_ref[hh, c] = st
                qb = (q * jnp.exp(bb)).astype(BF16)
                o = _dot(a.astype(BF16), vb) + _dot(qb, st.astype(BF16), NT)
                kb = (k * jnp.exp(bl - bb)).astype(BF16)
                s_scr[hh] = st * jnp.exp(bl) + _dot(vb, kb, TN)
                rstd = lax.rsqrt(jnp.mean(o * o, axis=-1, keepdims=True) + RMS_EPS)
                ro_ref[sl, cols] = ((o * rstd * gov) * _silu(hg)).astype(BF16)
                oraw_ref[sl, cols] = o
            return carry

        lax.fori_loop(0, nc, step, 0)

    col = lambda base: pl.BlockSpec((s, PAIR_W), lambda bi, g: (bi, base // HGRN_PAIR + g))
    vec = pl.BlockSpec((1, PAIR_W), lambda bi, g: (0, g))
    out = pl.BlockSpec((s, PAIR_W), lambda bi, g: (bi, g))
    return pl.pallas_call(
        body,
        name=name,
        grid=(b, HGRN_HEADS // HGRN_PAIR),
        in_specs=[col(HQ_COL), col(HF_COL), col(HI_COL), col(HG_COL), vec,
                  pl.BlockSpec((1, HGRN_DH), lambda bi, g: (0, 0))],
        out_specs=[out, out,
                   pl.BlockSpec((None, HGRN_PAIR, nc, HGRN_DH, HGRN_DH), lambda bi, g: (bi, g, 0, 0, 0))],
        out_shape=[jax.ShapeDtypeStruct((t, HGRN_W), BF16), jax.ShapeDtypeStruct((t, HGRN_W), F32),
                   jax.ShapeDtypeStruct((b, HGRN_HEADS, nc, HGRN_DH, HGRN_DH), F32)],
        scratch_shapes=[pltpu.VMEM((HGRN_PAIR, HGRN_DH, HGRN_DH), F32)],
        compiler_params=_params("parallel", "parallel"),
    )(proj, proj, proj, proj, lb, go)


def _hgrn_bwd(name, proj, lb, go, oraw, states, dmix, b, s):
    nc = s // CHUNK
    t = b * s

    def body(hq_ref, hf_ref, hi_ref, hg_ref, lb_ref, go_ref, oraw_ref, st_ref, dro_ref,
             dhq_ref, dhf_ref, dhi_ref, dhg_ref, dlb_ref, dgo_ref, ds_scr, dlb_scr, dgo_scr):
        tril = _tri(True)
        ones_u = jnp.where(_tri(False), 1.0, 0.0).astype(BF16)
        gov = go_ref[...]
        ds_scr[...] = jnp.zeros_like(ds_scr)
        dlb_scr[...] = jnp.zeros_like(dlb_scr)
        dgo_scr[...] = jnp.zeros_like(dgo_scr)

        def step(ci, carry):
            c = nc - 1 - ci
            sl = pl.ds(pl.multiple_of(c * CHUNK, CHUNK), CHUNK)
            for hh in range(HGRN_PAIR):
                cols = pl.ds(hh * HGRN_DH, HGRN_DH)
                lbv = lb_ref[:, cols]
                hq = hq_ref[sl, cols]
                hg = hg_ref[sl, cols]
                sig, f, bb, bl, bm, sq, q, k = _hgrn_chunk(hq, hf_ref[sl, cols], lbv, tril)
                vb = hi_ref[sl, cols].astype(BF16)
                ebm = jnp.exp(bb - bm)
                embm = jnp.exp(bm - bb)
                eb = jnp.exp(bb)
                ebl = jnp.exp(bl - bb)
                e_last = jnp.exp(bl)
                qe = (q * ebm).astype(BF16)
                ke = (k * embm).astype(BF16)
                qb = (q * eb).astype(BF16)
                kb = (k * ebl).astype(BF16)
                a = jnp.where(tril, _dot(qe, ke, NT), 0.0)
                st = st_ref[hh, c]
                dst = ds_scr[hh]
                o = oraw_ref[sl, cols]
                dro = dro_ref[sl, cols]
                sg = _sigmoid(hg)
                rstd = lax.rsqrt(jnp.mean(o * o, axis=-1, keepdims=True) + RMS_EPS)
                ohat = o * rstd
                dn = dro * (hg * sg)
                dhg_ref[sl, cols] = (dro * (ohat * gov) * (sg * (1.0 + hg * (1.0 - sg)))).astype(BF16)
                dgo_scr[:, cols] += jnp.sum(dn * ohat, axis=0, keepdims=True)
                dohat = dn * gov
                do = rstd * (dohat - ohat * jnp.mean(dohat * ohat, axis=-1, keepdims=True))
                dob = do.astype(BF16)
                dab = jnp.where(tril, _dot(dob, vb, NT), 0.0).astype(BF16)
                stb = st.astype(BF16)
                dstb = dst.astype(BF16)
                dv = _dot(a.astype(BF16), dob, TN) + _dot(kb, dstb, NT)
                dqe = _dot(dab, ke)
                dke = _dot(dab, qe, TN)
                dqb = _dot(dob, stb)
                dkb = _dot(vb, dstb)
                dq = dqe * ebm + dqb * eb
                dk = dke * embm + dkb * ebl
                db = (qe.astype(F32) * dqe - ke.astype(F32) * dke) + q * (dqb * eb) - k * (dkb * ebl)
                d_last = (jnp.sum(k * ebl * dkb, axis=0, keepdims=True)
                          + jnp.sum(dst * st, axis=0, keepdims=True) * e_last)
                dg = _dot_exact_lhs(ones_u, db) + d_last
                df = dg / f - dk
                dhf_ref[sl, cols] = (df * (1.0 - lbv) * sig * (1.0 - sig)).astype(BF16)
                dlb_scr[:, cols] += jnp.sum(df * (1.0 - sig), axis=0, keepdims=True)
                dhq_ref[sl, cols] = (dq * (sq * (1.0 + hq * (1.0 - sq)))).astype(BF16)
                dhi_ref[sl, cols] = dv.astype(BF16)
                ds_scr[hh] = dst * e_last + _dot(dob, qb, TN)
            return carry

        lax.fori_loop(0, nc, step, 0)
        dlb_ref[...] = dlb_scr[...]
        dgo_ref[...] = dgo_scr[...]

    col = lambda base: pl.BlockSpec((s, PAIR_W), lambda bi, g: (bi, base // HGRN_PAIR + g))
    vec = pl.BlockSpec((1, PAIR_W), lambda bi, g: (0, g))
    out = pl.BlockSpec((s, PAIR_W), lambda bi, g: (bi, g))
    part = pl.BlockSpec((None, 1, PAIR_W), lambda bi, g: (bi, 0, g))
    o_shape = jax.ShapeDtypeStruct((t, HGRN_W), BF16)
    p_shape = jax.ShapeDtypeStruct((b, 1, HGRN_W), F32)
    return pl.pallas_call(
        body,
        name=name,
        grid=(b, HGRN_HEADS // HGRN_PAIR),
        in_specs=[col(HQ_COL), col(HF_COL), col(HI_COL), col(HG_COL), vec,
                  pl.BlockSpec((1, HGRN_DH), lambda bi, g: (0, 0)), out,
                  pl.BlockSpec((None, HGRN_PAIR, nc, HGRN_DH, HGRN_DH), lambda bi, g: (bi, g, 0, 0, 0)),
                  col(ATTN_W // HGRN_DH)],
        out_specs=[out] * 4 + [part] * 2,
        out_shape=[o_shape] * 4 + [p_shape] * 2,
        scratch_shapes=[pltpu.VMEM((HGRN_PAIR, HGRN_DH, HGRN_DH), F32), pltpu.VMEM((1, PAIR_W), F32),
                        pltpu.VMEM((1, PAIR_W), F32)],
        compiler_params=pltpu.CompilerParams(dimension_semantics=("parallel", "parallel"),
                                             vmem_limit_bytes=VMEM_LIMIT_BIG),
    )(proj, proj, proj, proj, lb, go, oraw, states, dmix)


def _small_grads(name, dg1, dgm, dg2, dgq, dgk, dbias_t, dlb, dgo, lbp):
    d = dg1.shape[1]

    def body(dg1_ref, dgm_ref, dg2_ref, dgq_ref, dgk_ref, dbias_ref, dlb_ref, dgo_ref, lbp_ref,
             g1_ref, gm_ref, g2_ref, gq_ref, gk_ref, rb_ref, lbg_ref, go_ref):
        g1_ref[...] = jnp.sum(dg1_ref[...], axis=0, keepdims=True)
        gm_ref[...] = jnp.sum(dgm_ref[...], axis=0, keepdims=True)
        g2_ref[...] = jnp.sum(dg2_ref[...], axis=0, keepdims=True)
        r = lax.broadcasted_iota(jnp.int32, (ATTN_W, ATTN_DH), 0)
        cidx = lax.broadcasted_iota(jnp.int32, (ATTN_W, ATTN_DH), 1)
        fold = jnp.where(jnp.bitwise_and(r, ATTN_DH - 1) == cidx, 1.0, 0.0).astype(BF16)
        gq_ref[...] = jnp.sum(_dot_exact_rhs(dgq_ref[...], fold), axis=0, keepdims=True)
        gk_ref[...] = jnp.sum(_dot_exact_rhs(dgk_ref[...], fold), axis=0, keepdims=True)
        gosum = jnp.sum(dgo_ref[...], axis=0, keepdims=True)
        go_ref[...] = (gosum[:, 0:HGRN_DH] + gosum[:, HGRN_DH:2 * HGRN_DH]
                       + gosum[:, 2 * HGRN_DH:3 * HGRN_DH] + gosum[:, 3 * HGRN_DH:4 * HGRN_DH])
        p0 = lbp_ref[0:1, :]
        p1 = lbp_ref[1:2, :]
        lbv = 1.0 / (1.0 + jnp.exp(p1 - p0))
        dp0 = jnp.sum(dlb_ref[...], axis=0, keepdims=True) * lbv * (1.0 - lbv)
        lbg_ref[0:1, :] = dp0
        lbg_ref[1:2, :] = -dp0
        sidx = lax.broadcasted_iota(jnp.int32, (BAND, N_REL_PAD), 0)
        ridx = lax.broadcasted_iota(jnp.int32, (BAND, N_REL_PAD), 1)

        def step(tq, acc):
            rel = jnp.clip(tq + KPAD - sidx, -REL_CLIP, REL_CLIP) + REL_CLIP
            onehot = jnp.where(rel == ridx, 1.0, 0.0).astype(BF16)
            return acc + _dot_exact_rhs(dbias_ref[tq], onehot)

        rb_ref[...] = lax.fori_loop(0, CHUNK, step, jnp.zeros((ATTN_HEADS, N_REL_PAD), F32))

    ins = [dg1, dgm, dg2, dgq, dgk, dbias_t, dlb, dgo, lbp]
    outs = [jax.ShapeDtypeStruct((1, d), F32)] * 3 + [jax.ShapeDtypeStruct((1, ATTN_DH), F32)] * 2 + [
        jax.ShapeDtypeStruct((ATTN_HEADS, N_REL_PAD), F32), jax.ShapeDtypeStruct((2, HGRN_W), F32),
        jax.ShapeDtypeStruct((1, HGRN_DH), F32)]
    vm = pl.BlockSpec(memory_space=pltpu.VMEM)
    return pl.pallas_call(
        body,
        name=name,
        in_specs=[vm] * len(ins),
        out_specs=[vm] * len(outs),
        out_shape=outs,
        compiler_params=pltpu.CompilerParams(vmem_limit_bytes=VMEM_LIMIT),
    )(*ins)


def _adam_update(w, g, m, v):
    m2 = ADAM_B1 * m + (1.0 - ADAM_B1) * g
    v2 = ADAM_B2 * v + (1.0 - ADAM_B2) * (g * g)
    m_hat = m2 / (1.0 - ADAM_B1 ** ADAM_STEP)
    v_hat = v2 / (1.0 - ADAM_B2 ** ADAM_STEP)
    delta = -ADAM_LR * (m_hat / (jnp.sqrt(v_hat) + ADAM_EPS) + ADAM_WD * w)
    return delta, m2, v2


def _rows_tile(r):
    for cand in (256, 352, 128, 176, 64, 32, 16):
        if r % cand == 0 and r > cand:
            return cand
    return r


def _pair_sum(name, grad, theirs, core):
    n, half, c = theirs.shape
    tr = _rows_tile(half)
    nth = half // tr

    def body(core_ref, a_ref, b_ref, o_ref):
        o_ref[...] = (a_ref[...] + b_ref[...]).astype(o_ref.dtype)

    spec = pl.BlockSpec((None, tr, c), lambda i, j, core_ref: (i, j, 0))
    return pl.pallas_call(
        body, name=name,
        grid_spec=pltpu.PrefetchScalarGridSpec(
            num_scalar_prefetch=1, grid=(n, nth),
            in_specs=[pl.BlockSpec((None, tr, c), lambda i, j, core_ref: (i, core_ref[0] * nth + j, 0)), spec],
            out_specs=spec),
        out_shape=jax.ShapeDtypeStruct((n, half, c), BF16), compiler_params=_params("parallel", "parallel"),
    )(core, grad, theirs)


def _chip_sum(name, own, parts, chip):
    _, half, c = own.shape
    tr = _rows_tile(half)

    def body(chip_ref, own_ref, p_ref, o_ref):
        me = chip_ref[0]
        mine = own_ref[...].astype(F32)
        flip_x, flip_y, flip_xy = (p_ref[i].astype(F32) for i in range(3))
        acc = None
        for k in range(N_CHIPS):
            rel = jnp.bitwise_xor(me, k)
            term = jnp.where(rel == 0, mine, jnp.where(rel == 2, flip_x, jnp.where(rel == 1, flip_y, flip_xy)))
            acc = term if acc is None else acc + term
        o_ref[...] = acc

    return pl.pallas_call(
        body, name=name,
        grid_spec=pltpu.PrefetchScalarGridSpec(
            num_scalar_prefetch=1, grid=(half // tr,),
            in_specs=[pl.BlockSpec((None, tr, c), lambda j, chip_ref: (chip_ref[0], j, 0)),
                      pl.BlockSpec((3, tr, c), lambda j, chip_ref: (0, j, 0))],
            out_specs=pl.BlockSpec((tr, c), lambda j, chip_ref: (j, 0))),
        out_shape=jax.ShapeDtypeStruct((half, c), F32), compiler_params=_params("parallel"),
    )(chip, own, parts)


def _adamw(name, w, g_mine, g_theirs, m, v, core):
    _, r, c = w.shape
    half = r // 2
    tr = _rows_tile(half)
    nth = half // tr

    def body(core_ref, w_ref, gm_ref, gt_ref, m_ref, v_ref, g_ref, d_ref, m2_ref, v2_ref):
        g = jnp.where(pl.program_id(0) == core_ref[0], gm_ref[...], gt_ref[...])
        delta, m2, v2 = _adam_update(w_ref[...], g, m_ref[...], v_ref[...])
        g_ref[...] = g
        d_ref[...] = delta
        m2_ref[...] = m2
        v2_ref[...] = v2

    full = pl.BlockSpec((None, tr, c), lambda h, j, core_ref: (0, h * nth + j, 0))
    part = pl.BlockSpec((tr, c), lambda h, j, core_ref: (j, 0))
    shape = jax.ShapeDtypeStruct((1, r, c), F32)
    return pl.pallas_call(
        body, name=name,
        grid_spec=pltpu.PrefetchScalarGridSpec(
            num_scalar_prefetch=1, grid=(2, nth), in_specs=[full, part, part, full, full], out_specs=[full] * 4),
        out_shape=[shape] * 4, compiler_params=_params("parallel", "parallel"),
    )(core, w, g_mine, g_theirs, m, v)


def _rel_bias_table(name, rel_bias):
    padded = jnp.pad(rel_bias, ((0, 0), (0, N_REL_PAD - N_REL)))

    def body(rb_ref, o_ref):
        ridx = lax.broadcasted_iota(jnp.int32, (N_REL_PAD, BAND), 0)
        sidx = lax.broadcasted_iota(jnp.int32, (N_REL_PAD, BAND), 1)
        rb = rb_ref[...]

        def step(tq, carry):
            rel = jnp.clip(tq + KPAD - sidx, -REL_CLIP, REL_CLIP) + REL_CLIP
            onehot = jnp.where(rel == ridx, 1.0, 0.0).astype(BF16)
            o_ref[tq] = _dot_exact_rhs(rb, onehot)
            return carry

        lax.fori_loop(0, CHUNK, step, 0)

    vm = pl.BlockSpec(memory_space=pltpu.VMEM)
    table = pl.pallas_call(
        body, name=name, in_specs=[vm], out_specs=vm,
        out_shape=jax.ShapeDtypeStruct((CHUNK, ATTN_HEADS, BAND), F32),
    )(padded)
    return table.transpose(1, 0, 2)


def _adamw_small(name, w, parts, m, v):
    def body(w_ref, p_ref, m_ref, v_ref, g_ref, d_ref, m2_ref, v2_ref):
        g = p_ref[0]
        for i in range(1, N_DEV):
            g = g + p_ref[i]
        delta, m2, v2 = _adam_update(w_ref[...], g, m_ref[...], v_ref[...])
        g_ref[...] = g
        d_ref[...] = delta
        m2_ref[...] = m2
        v2_ref[...] = v2

    vm = pl.BlockSpec(memory_space=pltpu.VMEM)
    shape = jax.ShapeDtypeStruct((SMALL_ROWS, SMALL_COLS), F32)
    return pl.pallas_call(
        body, name=name, in_specs=[vm] * 4, out_specs=[vm] * 4, out_shape=[shape] * 4,
    )(w, parts, m, v)


def _position():
    return lax.axis_index("x"), lax.axis_index("y"), lax.axis_index("c")


def _other_chips(x, y):
    return [(1 - x, y), (x, 1 - y), (1 - x, 1 - y)]


ANY = pl.BlockSpec(memory_space=pl.ANY)


HBM = pl.BlockSpec(memory_space=pltpu.HBM)
SEM = pl.BlockSpec(memory_space=pltpu.SEMAPHORE)
SPLIT_COPY = pltpu.SideEffectType.DATAFLOW_SIDE_EFFECTING


def _gather_copy(shards, outs, send_sem, recv_sem, i, j):
    x, y, c = _position()
    chips = _other_chips(x, y)
    half = shards[i].shape[0] // 2
    rows = pl.ds(pl.multiple_of(c * half, 16), half)
    return pltpu.make_async_remote_copy(
        src_ref=shards[i].at[rows, :], dst_ref=outs[i].at[2 * x + y, rows, :],
        send_sem=send_sem.at[3 * i + j], recv_sem=recv_sem.at[3 * i + j],
        device_id=(chips[j][0], chips[j][1], c), device_id_type=MESH)


def _gather_start(name, shards, after):
    n = len(shards)

    def body(*refs):
        srcs, outs = refs[:n], refs[n:2 * n]
        send_sem, recv_sem = refs[2 * n + len(after)], refs[2 * n + len(after) + 1]
        token = refs[-1]
        for i in range(n):
            for j in range(3):
                _gather_copy(srcs, outs, send_sem, recv_sem, i, j).start()
        token[...] = jnp.zeros_like(token)

    full = [(N_CHIPS,) + s.shape for s in shards]
    res = pl.pallas_call(
        body,
        name=name,
        in_specs=[HBM] * (2 * n) + [ANY] * len(after),
        out_specs=[SEM, SEM] + [HBM] * (2 * n) + [pl.BlockSpec(memory_space=pltpu.VMEM)],
        out_shape=[pltpu.SemaphoreType.DMA((3 * n,)), pltpu.SemaphoreType.DMA((3 * n,))]
        + [pltpu.HBM(s.shape, s.dtype) for s in shards]
        + [pltpu.HBM(shp, s.dtype) for shp, s in zip(full, shards)]
        + [jax.ShapeDtypeStruct((8, LANES), F32)],
        input_output_aliases={i: 2 + i for i in range(2 * n)},
        compiler_params=pltpu.CompilerParams(has_side_effects=SPLIT_COPY),
    )(*[pltpu.with_memory_space_constraint(s, pltpu.HBM) for s in shards],
      *[pltpu.with_memory_space_constraint(lax.empty(shp, s.dtype), pltpu.HBM) for shp, s in zip(full, shards)],
      *after)
    return res[0], res[1], list(res[2:2 + n]), list(res[2 + n:2 + 2 * n]), res[-1]


def _gather_wait(name, send_sem, recv_sem, shards, outs, after):
    n = len(shards)

    def body(*refs):
        srcs, out_refs = refs[:n], refs[n:2 * n]
        send_ref, recv_ref = refs[2 * n], refs[2 * n + 1]
        for i in range(n):
            for j in range(3):
                copy = _gather_copy(srcs, out_refs, send_ref, recv_ref, i, j)
                copy.wait_send()
                copy.wait_recv()

    res = pl.pallas_call(
        body,
        name=name,
        in_specs=[HBM] * (2 * n) + [SEM, SEM, ANY],
        out_specs=[HBM] * (2 * n),
        out_shape=[pltpu.HBM(s.shape, s.dtype) for s in shards] + [pltpu.HBM(o.shape, o.dtype) for o in outs],
        input_output_aliases={i: i for i in range(2 * n)},
        compiler_params=pltpu.CompilerParams(has_side_effects=SPLIT_COPY),
    )(*shards, *outs, send_sem, recv_sem, after)
    return list(res[:n]), list(res[n:])


def _gather_join(name, shards, outs):
    n = len(shards)

    def body(*refs):
        srcs, ins, outs_ = refs[:n], refs[n:2 * n], refs[2 * n:3 * n]
        own_send, own_recv, half_send, half_recv = refs[3 * n:]
        x, y, c = _position()
        chips = _other_chips(x, y)
        copies = []
        for i in range(n):
            copies.append(pltpu.make_async_remote_copy(
                src_ref=srcs[i], dst_ref=outs_[i].at[2 * x + y], send_sem=own_send.at[i], recv_sem=own_recv.at[i],
                device_id=(x, y, 1 - c), device_id_type=MESH))
            half = srcs[i].shape[0] // 2
            rows = pl.ds(pl.multiple_of(c * half, 16), half)
            for j in range(3):
                slot = 2 * chips[j][0] + chips[j][1]
                copies.append(pltpu.make_async_remote_copy(
                    src_ref=ins[i].at[slot, rows, :], dst_ref=outs_[i].at[slot, rows, :],
                    send_sem=half_send.at[3 * i + j], recv_sem=half_recv.at[3 * i + j],
                    device_id=(x, y, 1 - c), device_id_type=MESH))
        for cp in copies:
            cp.start()
        for cp in copies:
            cp.wait()

    return pl.pallas_call(
        body,
        name=name,
        in_specs=[ANY] * (2 * n),
        out_specs=[ANY] * n,
        out_shape=[jax.ShapeDtypeStruct(o.shape, o.dtype) for o in outs],
        input_output_aliases={n + i: i for i in range(n)},
        scratch_shapes=[pltpu.SemaphoreType.DMA((n,))] * 2 + [pltpu.SemaphoreType.DMA((3 * n,))] * 2,
    )(*shards, *outs)


def _pair_exchange(name, grads):
    n = len(grads)

    def body(*refs):
        ins, theirs = refs[:n], refs[n:2 * n]
        send_sem, recv_sem = refs[2 * n:]
        x, y, c = _position()
        copies = []
        for i in range(n):
            half = ins[i].shape[1] // 2
            give = pl.ds(pl.multiple_of((1 - c) * half, 8), half)
            swap = pltpu.make_async_remote_copy(
                src_ref=ins[i].at[:, give, :], dst_ref=theirs[i], send_sem=send_sem.at[i], recv_sem=recv_sem.at[i],
                device_id=(x, y, 1 - c), device_id_type=MESH)
            swap.start()
            copies.append(swap)
        for swap in copies:
            swap.wait()

    return pl.pallas_call(
        body,
        name=name,
        in_specs=[ANY] * n,
        out_specs=[ANY] * n,
        out_shape=[jax.ShapeDtypeStruct((g.shape[0], g.shape[1] // 2, g.shape[2]), g.dtype) for g in grads],
        scratch_shapes=[pltpu.SemaphoreType.DMA((n,))] * 2,
    )(*grads)


def _scatter_copy(srcs, lands, send_sem, recv_sem, i, j):
    x, y, c = _position()
    chips = _other_chips(x, y)
    return pltpu.make_async_remote_copy(
        src_ref=srcs[i].at[2 * chips[j][0] + chips[j][1]], dst_ref=lands[i].at[j],
        send_sem=send_sem.at[3 * i + j], recv_sem=recv_sem.at[3 * i + j],
        device_id=(chips[j][0], chips[j][1], c), device_id_type=MESH)


def _scatter_start(name, sums):
    n = len(sums)

    def body(*refs):
        srcs, lands = refs[:n], refs[n:2 * n]
        send_sem, recv_sem = refs[2 * n], refs[2 * n + 1]
        token = refs[-1]
        for i in range(n):
            for j in range(3):
                _scatter_copy(srcs, lands, send_sem, recv_sem, i, j).start()
        token[...] = jnp.zeros_like(token)

    land_shapes = [(3,) + s.shape[1:] for s in sums]
    res = pl.pallas_call(
        body,
        name=name,
        in_specs=[HBM] * (2 * n),
        out_specs=[SEM, SEM] + [HBM] * (2 * n) + [pl.BlockSpec(memory_space=pltpu.VMEM)],
        out_shape=[pltpu.SemaphoreType.DMA((3 * n,)), pltpu.SemaphoreType.DMA((3 * n,))]
        + [pltpu.HBM(s.shape, s.dtype) for s in sums]
        + [pltpu.HBM(shp, s.dtype) for shp, s in zip(land_shapes, sums)]
        + [jax.ShapeDtypeStruct((8, LANES), F32)],
        input_output_aliases={i: 2 + i for i in range(2 * n)},
        compiler_params=pltpu.CompilerParams(has_side_effects=SPLIT_COPY),
    )(*[pltpu.with_memory_space_constraint(s, pltpu.HBM) for s in sums],
      *[pltpu.with_memory_space_constraint(lax.empty(shp, s.dtype), pltpu.HBM) for shp, s in zip(land_shapes, sums)])
    return res[0], res[1], list(res[2:2 + n]), list(res[2 + n:2 + 2 * n]), res[-1]


def _scatter_wait(name, send_sem, recv_sem, sums, lands, after):
    n = len(sums)

    def body(*refs):
        srcs, land_refs = refs[:n], refs[n:2 * n]
        send_ref, recv_ref = refs[2 * n], refs[2 * n + 1]
        for i in range(n):
            for j in range(3):
                copy = _scatter_copy(srcs, land_refs, send_ref, recv_ref, i, j)
                copy.wait_send()
                copy.wait_recv()

    res = pl.pallas_call(
        body,
        name=name,
        in_specs=[HBM] * (2 * n) + [SEM, SEM, ANY],
        out_specs=[HBM] * (2 * n),
        out_shape=[pltpu.HBM(s.shape, s.dtype) for s in sums] + [pltpu.HBM(l.shape, l.dtype) for l in lands],
        input_output_aliases={i: i for i in range(2 * n)},
        compiler_params=pltpu.CompilerParams(has_side_effects=SPLIT_COPY),
    )(*sums, *lands, send_sem, recv_sem, after)
    return list(res[:n]), list(res[n:])


def _pair_join(name, halves, small=None):
    n = len(halves)
    if small is None:
        def body_plain(*refs):
            ins, outs = refs[:n], refs[n:2 * n]
            send_sem, recv_sem = refs[2 * n:]
            x, y, c = _position()
            swaps = [pltpu.make_async_remote_copy(
                src_ref=ins[i], dst_ref=outs[i], send_sem=send_sem.at[i], recv_sem=recv_sem.at[i],
                device_id=(x, y, 1 - c), device_id_type=MESH) for i in range(n)]
            for swap in swaps:
                swap.start()
            for swap in swaps:
                swap.wait()

        return pl.pallas_call(
            body_plain,
            name=name,
            in_specs=[ANY] * n,
            out_specs=[ANY] * n,
            out_shape=[jax.ShapeDtypeStruct(h.shape, h.dtype) for h in halves],
            scratch_shapes=[pltpu.SemaphoreType.DMA((n,))] * 2,
        )(*halves)

    def body(*refs):
        ins, small_ref = refs[:n], refs[n]
        outs, all_ref = refs[n + 1:2 * n + 1], refs[2 * n + 1]
        send_sem, recv_sem, sm_send, sm_recv, sm_local = refs[2 * n + 2:]
        x, y, c = _position()
        swaps = []
        for i in range(n):
            swap = pltpu.make_async_remote_copy(
                src_ref=ins[i], dst_ref=outs[i], send_sem=send_sem.at[i], recv_sem=recv_sem.at[i],
                device_id=(x, y, 1 - c), device_id_type=MESH)
            swap.start()
            swaps.append(swap)
        me = 4 * x + 2 * y + c
        sm_own = pltpu.make_async_copy(small_ref, all_ref.at[me], sm_local)
        sm_own.start()
        pushes, arrivals = [], []
        for mask in range(1, N_DEV):
            px, py, pc = x ^ (mask >> 2), y ^ ((mask >> 1) & 1), c ^ (mask & 1)
            pushes.append(pltpu.make_async_remote_copy(
                src_ref=small_ref, dst_ref=all_ref.at[me], send_sem=sm_send.at[mask - 1], recv_sem=sm_recv.at[mask - 1],
                device_id=(px, py, pc), device_id_type=MESH))
            arrivals.append(pltpu.make_async_remote_copy(
                src_ref=small_ref, dst_ref=all_ref.at[4 * px + 2 * py + pc], send_sem=sm_send.at[mask - 1],
                recv_sem=sm_recv.at[mask - 1], device_id=(px, py, pc), device_id_type=MESH))
        for cp in pushes:
            cp.start()
        for swap in swaps:
            swap.wait()
        for cp in arrivals:
            cp.wait_recv()
        for cp in pushes:
            cp.wait_send()
        sm_own.wait()

    res = pl.pallas_call(
        body,
        name=name,
        in_specs=[ANY] * (n + 1),
        out_specs=[ANY] * (n + 1),
        out_shape=[jax.ShapeDtypeStruct(h.shape, h.dtype) for h in halves]
        + [jax.ShapeDtypeStruct((N_DEV,) + small.shape, small.dtype)],
        scratch_shapes=[pltpu.SemaphoreType.DMA((n,))] * 2 + [pltpu.SemaphoreType.DMA((N_DEV - 1,))] * 2
        + [pltpu.SemaphoreType.DMA(())],
    )(*halves, small)
    return res[:n], res[n]


def _lower_bound(lbp):
    return jax.nn.softmax(lbp, axis=0)[0:1]


def _local_step(x, target, g1, gm, g2, gq, gk, go, rel_bias, lbp, first_weights, mid_weights, last_weights, on_grads):
    b, s, d = x.shape
    t = b * s
    x0 = x.reshape(t, d)
    tgt = target.reshape(t, d)
    gq_t = jnp.tile(gq, (1, ATTN_HEADS))
    gk_t = jnp.tile(gk, (1, ATTN_HEADS))
    lb = _lower_bound(lbp)
    bias = _rel_bias_table("rel_bias_table", rel_bias)

    h1 = _rmsnorm_fwd("norm1", x0, g1)
    wg1, wu1, deps1 = first_weights(h1)
    a1, b1, z1 = _ffn_up("ffn1_up", h1, wg1, wu1, deps1)
    wd1, w_in, w_out = mid_weights(z1)
    ns = w_in.shape[0]
    x1 = _ffn_down("ffn1_down", z1, wd1, x0)
    h2 = _rmsnorm_fwd("norm_mix", x1, gm)
    proj = _in_proj("in_proj", h2, w_in)
    proj3 = proj.reshape(b, s, proj.shape[1])
    table = _band_table(bias)
    qn, kn, vb = _qk_prep("qk_prep", proj3, gq_t, gk_t)
    attn = _attn_fwd("attn_fwd", qn, kn, vb, table).reshape(t, ATTN_W)
    ro, oraw, states = _hgrn_fwd("hgrn_fwd", proj, lb, go, b, s)
    mix = jnp.concatenate([attn, ro], axis=1)
    x2 = _out_proj("out_proj", mix, w_out, x1)
    h3 = _rmsnorm_fwd("norm2", x2, g2)
    wg2, wu2, wd2 = last_weights(h3)
    a2, b2, z2 = _ffn_up("ffn2_up", h3, wg2, wu2)
    dy, dyh, sq = _ffn_down_loss("ffn2_down_loss", z2, wd2, x2, tgt)
    loss = 0.5 * jnp.sum(sq) / d

    da2, db2 = _ffn_bwd_act("ffn2_bwd_act", dyh, wd2, a2, b2)
    dwd2 = _grad_w_shardrows("ffn2_dwd", z2, dyh)
    dwg2 = _grad_w_shardrows("ffn2_dwg", da2, h3)
    dwu2 = _grad_w_shardrows("ffn2_dwu", db2, h3)
    sent2 = on_grads("ffn2", {"ffn2_w_gate": dwg2, "ffn2_w_up": dwu2, "ffn2_w_down": dwd2})
    dx2, dx2b, dg2 = _ffn_bwd_in("ffn2_bwd_in", da2, db2, wg2, wu2, x2, g2, dy, 1.0)

    dwout = _grad_w_out("dw_out", mix, dx2b)
    dmix = _out_proj_bwd("out_proj_bwd", dx2b, w_out, sent2)
    dqn, dkn, dvn, dbe, dbo = _attn_bwd("attn_bwd", qn, kn, vb, table, dmix.reshape(b, s, dmix.shape[1]))
    dbias = dbe[:, :, :BAND] + dbo[:, :, CHUNK:]
    dpq, dpk, dpv, dgq, dgk = _qk_prep_bwd("qk_prep_bwd", proj3, dqn, dkn, dvn, gq_t, gk_t)
    dpq, dpk, dpv = (a.reshape(t, ATTN_W) for a in (dpq, dpk, dpv))
    dhq, dhf, dhi, dhg, dlb, dgo = _hgrn_bwd("hgrn_bwd", proj, lb, go, oraw, states, dmix, b, s)
    dproj = jnp.concatenate([dpq, dpk, dpv, dhq, dhf, dhi, dhg], axis=1)
    dwin = _grad_w_in("dw_in", h2, dproj, ns)
    sent_mix = on_grads("mix", {"w_in": dwin, "w_out": dwout.reshape(ns, dwout.shape[0] // ns, d)})
    dx1, dx1h, dgm = _in_proj_bwd("in_proj_bwd", dproj, w_in, x1, gm, dx2, 0.5)

    da1, db1 = _ffn_bwd_act("ffn1_bwd_act", dx1h, wd1, a1, b1, sent_mix)
    dwd1 = _grad_w_shardrows("ffn1_dwd", z1, dx1h)
    dwg1 = _grad_w_shardrows("ffn1_dwg", da1, h1)
    dwu1 = _grad_w_shardrows("ffn1_dwu", db1, h1)
    on_grads("ffn1", {"ffn1_w_gate": dwg1, "ffn1_w_up": dwu1, "ffn1_w_down": dwd1})
    dx0, dg1 = _ffn_bwd_in("ffn1_bwd_in", da1, db1, wg1, wu1, x0, g1, dx1, None)

    nt = dg1.shape[0]
    sg = _small_grads(
        "small_grads", dg1.reshape(nt, d), dgm.reshape(nt, d), dg2.reshape(nt, d),
        dgq.reshape(-1, ATTN_W), dgk.reshape(-1, ATTN_W), dbias.transpose(1, 0, 2),
        dlb.reshape(b, HGRN_W), dgo.reshape(b, HGRN_W), lbp)
    g1g, gmg, g2g, gqg, gkg, rbg, lbg, gog = sg
    small = _pack_small(g1g, gmg, g2g, lbg, rbg[:, :N_REL], gqg, gkg, gog)
    return loss, dx0.reshape(b, s, d), small


def _pack_small(g1, gm, g2, lbp, rel_bias, gq, gk, go):
    flat = [g1.reshape(-1), gm.reshape(-1), g2.reshape(-1), lbp.reshape(-1), rel_bias.reshape(-1)]
    n_bias = 3 * SMALL_COLS - rel_bias.size
    heads = [gq.reshape(-1), gk.reshape(-1), go.reshape(-1)]
    n_tail = SMALL_COLS - sum(h.size for h in heads)
    return jnp.concatenate(flat + [jnp.zeros((n_bias,), F32)] + heads + [jnp.zeros((n_tail,), F32)]).reshape(
        SMALL_ROWS, SMALL_COLS)


def _unpack_small(p, d):
    flat = p.reshape(-1)
    o = 3 * d
    g1, gm, g2 = p[0:1], p[1:2], p[2:3]
    lbp = flat[o:o + 2 * HGRN_W].reshape(2, HGRN_W)
    o = 4 * SMALL_COLS
    rel = flat[o:o + ATTN_HEADS * N_REL].reshape(1, ATTN_HEADS, N_REL)
    o = 7 * SMALL_COLS
    gq = flat[o:o + ATTN_DH].reshape(1, ATTN_DH)
    gk = flat[o + ATTN_DH:o + 2 * ATTN_DH].reshape(1, ATTN_DH)
    go = flat[o + 2 * ATTN_DH:o + 2 * ATTN_DH + HGRN_DH].reshape(1, HGRN_DH)
    return g1, gm, g2, gq, gk, rel, lbp, go


def kernel(x, ffn1_norm_g, ffn1_w_gate, ffn1_w_up, ffn1_w_down, mix_norm_g, w_in, attn_q_norm_g, attn_k_norm_g, attn_rel_bias, hgrn_lower_bounds, hgrn_out_norm_g, w_out, ffn2_norm_g, ffn2_w_gate, ffn2_w_up, ffn2_w_down, loss_target, m_ffn1_norm_g, m_ffn1_w_gate, m_ffn1_w_up, m_ffn1_w_down, m_mix_norm_g, m_w_in, m_attn_q_norm_g, m_attn_k_norm_g, m_attn_rel_bias, m_hgrn_lower_bounds, m_hgrn_out_norm_g, m_w_out, m_ffn2_norm_g, m_ffn2_w_gate, m_ffn2_w_up, m_ffn2_w_down, v_ffn1_norm_g, v_ffn1_w_gate, v_ffn1_w_up, v_ffn1_w_down, v_mix_norm_g, v_w_in, v_attn_q_norm_g, v_attn_k_norm_g, v_attn_rel_bias, v_hgrn_lower_bounds, v_hgrn_out_norm_g, v_w_out, v_ffn2_norm_g, v_ffn2_w_gate, v_ffn2_w_up, v_ffn2_w_down):
    d = x.shape[-1]
    big_w = [ffn1_w_gate, ffn1_w_up, ffn1_w_down, w_in, w_out, ffn2_w_gate, ffn2_w_up, ffn2_w_down]
    big_m = [m_ffn1_w_gate, m_ffn1_w_up, m_ffn1_w_down, m_w_in, m_w_out, m_ffn2_w_gate, m_ffn2_w_up, m_ffn2_w_down]
    big_v = [v_ffn1_w_gate, v_ffn1_w_up, v_ffn1_w_down, v_w_in, v_w_out, v_ffn2_w_gate, v_ffn2_w_up, v_ffn2_w_down]
    big_names = ["ffn1_w_gate", "ffn1_w_up", "ffn1_w_down", "w_in", "w_out", "ffn2_w_gate", "ffn2_w_up", "ffn2_w_down"]
    flipped = {nm for nm in big_names if nm.endswith("gate") or nm.endswith("up")}
    flip = lambda nm, a: jnp.swapaxes(a, 1, 2) if nm in flipped else a
    big_w, big_m, big_v = ([flip(nm, a) for nm, a in zip(big_names, arrs)] for arrs in (big_w, big_m, big_v))

    shards = [w[0].astype(BF16) for w in big_w]
    start_a = _gather_start("gather_start_up1", shards[:2], ())
    start_b = _gather_start("gather_start_mid", shards[2:5], (start_a[4],))
    start_c = _gather_start("gather_start_ffn2", shards[5:], (start_b[4],))

    def gathered(tag, started, after):
        send_sem, recv_sem, srcs, outs, _ = started
        srcs, outs = _gather_wait("gather_wait_" + tag, send_sem, recv_sem, srcs, outs, after)
        return _gather_join("gather_join_" + tag, srcs, outs)

    def first_weights(after):
        return (*gathered("up1", start_a, after), (start_c[4],))

    def mid_weights(after):
        wd1, win_f, wout_f = gathered("mid", start_b, after)
        return wd1, win_f, wout_f.reshape(wout_f.shape[0] * wout_f.shape[1], d)

    def last_weights(after):
        return gathered("ffn2", start_c, after)

    core = lax.axis_index("c").astype(jnp.int32).reshape(1)
    chip = (2 * lax.axis_index("x") + lax.axis_index("y")).astype(jnp.int32).reshape(1)
    started = {}

    def on_grads(tag, grads):
        names = list(grads)
        theirs = _pair_exchange("pair_exchange_" + tag, [grads[nm] for nm in names])
        sums = [_pair_sum("pair_sum_" + nm, grads[nm], th, core) for nm, th in zip(names, theirs)]
        started[tag] = (names, _scatter_start("scatter_start_" + tag, sums))
        return (started[tag][1][4],)

    loss, grad_x, small_g = _local_step(
        x, loss_target, ffn1_norm_g, mix_norm_g, ffn2_norm_g, attn_q_norm_g, attn_k_norm_g, hgrn_out_norm_g,
        attn_rel_bias[0], hgrn_lower_bounds, first_weights, mid_weights, last_weights, on_grads)
    loss = lax.psum(loss, ("x", "y", "c"))

    def finish(tag, after):
        names, (send_sem, recv_sem, sums, lands, _) = started[tag]
        sums, lands = _scatter_wait("scatter_wait_" + tag, send_sem, recv_sem, sums, lands, after)
        return names, [_chip_sum("chip_sum_" + nm, sm, ld, chip) for nm, sm, ld in zip(names, sums, lands)]

    by_name = {nm: (w, m, v) for nm, w, m, v in zip(big_names, big_w, big_m, big_v)}
    updated = {}

    def update(names, halves, other_halves):
        for nm, mine, theirs in zip(names, halves, other_halves):
            w, m, v = by_name[nm]
            updated[nm] = _adamw("adamw_" + nm, w, mine, theirs, m, v, core)

    last_token = started["ffn1"][1][4]
    names_a, halves_a = finish("ffn2", last_token)
    names_m, halves_m = finish("mix", last_token)
    names_a, halves_a = names_a + names_m, halves_a + halves_m
    update(names_a, halves_a, _pair_join("pair_join_early", halves_a))
    names_b, halves_b = finish("ffn1", updated["w_out"][1])
    others_b, small_all = _pair_join("pair_join_last", halves_b, small_g)
    update(names_b, halves_b, others_b)
    big_out = [updated[nm] for nm in big_names]

    pack = lambda g1, gm, g2, gq, gk, rel, lbp, go: _pack_small(g1, gm, g2, lbp, rel[0], gq, gk, go)
    small_w = pack(ffn1_norm_g, mix_norm_g, ffn2_norm_g, attn_q_norm_g, attn_k_norm_g, attn_rel_bias, hgrn_lower_bounds, hgrn_out_norm_g)
    small_m = pack(m_ffn1_norm_g, m_mix_norm_g, m_ffn2_norm_g, m_attn_q_norm_g, m_attn_k_norm_g, m_attn_rel_bias, m_hgrn_lower_bounds, m_hgrn_out_norm_g)
    small_v = pack(v_ffn1_norm_g, v_mix_norm_g, v_ffn2_norm_g, v_attn_q_norm_g, v_attn_k_norm_g, v_attn_rel_bias, v_hgrn_lower_bounds, v_hgrn_out_norm_g)
    small_out = [_unpack_small(p, d) for p in _adamw_small("adamw_small", small_w, small_all, small_m, small_v)]

    def assemble(kind):
        bg = [flip(nm, o[kind]) for nm, o in zip(big_names, big_out)]
        g1, gm, g2, gq, gk, rel, lbp, go = small_out[kind]
        return [g1, bg[0], bg[1], bg[2], gm, bg[3], gq, gk, rel, lbp, go, bg[4], g2, bg[5], bg[6], bg[7]]

    return (loss, grad_x, *assemble(0), *assemble(1), *assemble(2), *assemble(3))
```

```python
import functools

import jax
import jax.numpy as jnp
from jax import lax
from jax.experimental import pallas as pl
from jax.experimental.pallas import tpu as pltpu

F32 = jnp.float32
BF16 = jnp.bfloat16
MESH = pl.DeviceIdType.MESH

N_CHIPS = 4
N_DEV = 8
CHUNK = 64
ATTN_HEADS = 8
ATTN_DH = 64
ATTN_W = ATTN_HEADS * ATTN_DH
HGRN_HEADS = 4
HGRN_DH = 128
HGRN_W = HGRN_HEADS * HGRN_DH
LEFT_CHUNKS = 8
BAND = (LEFT_CHUNKS + 1) * CHUNK
KPAD = LEFT_CHUNKS * CHUNK
REL_CLIP = 128
N_REL = 2 * REL_CLIP + 1
N_REL_PAD = 384
RMS_EPS = 1e-6
LANES = 128
SMALL_ROWS = 8
SMALL_COLS = 1024

ADAM_LR = 0.001
ADAM_B1 = 0.9
ADAM_B2 = 0.999
ADAM_EPS = 1e-08
ADAM_WD = 0.01
ADAM_STEP = 10

NN = (((1,), (0,)), ((), ()))
NT = (((1,), (1,)), ((), ()))
TN = (((0,), (0,)), ((), ()))

VMEM_LIMIT = 48 * 1024 * 1024
VMEM_LIMIT_BIG = 56 * 1024 * 1024


def _sigmoid(x):
    return 1.0 / (1.0 + jnp.exp(-x))


def _silu(x):
    return x * _sigmoid(x)


def _dot(a, b, dims=NN):
    return lax.dot_general(a, b, dims, preferred_element_type=F32)


def _split3(x):
    hi = x.astype(BF16)
    r1 = x - hi.astype(F32)
    mid = r1.astype(BF16)
    lo = (r1 - mid.astype(F32)).astype(BF16)
    return hi, mid, lo


def _dot_exact_rhs(x, mat, dims=NN):
    hi, mid, lo = _split3(x)
    return _dot(hi, mat, dims) + _dot(mid, mat, dims) + _dot(lo, mat, dims)


def _dot_exact_lhs(mat, x, dims=NN):
    hi, mid, lo = _split3(x)
    return _dot(mat, hi, dims) + _dot(mat, mid, dims) + _dot(mat, lo, dims)


def _params(*sem):
    return pltpu.CompilerParams(dimension_semantics=sem, vmem_limit_bytes=VMEM_LIMIT)


def _mm(name, ins, terms, n_acc, grid, acc_shape, outs, epilogue, extras=(), deps=()):
    nk = grid[2]
    ni, ne, nd, no = len(ins), len(extras), len(deps), len(outs)

    def body(*refs):
        in_refs = refs[:ni]
        ex_refs = refs[ni:ni + ne]
        out_refs = refs[ni + ne + nd:ni + ne + nd + no]
        acc_refs = refs[ni + ne + nd + no:]
        parts = [None] * n_acc
        for ai, li, ri, dims in terms:
            d = _dot(in_refs[li][...], in_refs[ri][...], dims)
            parts[ai] = d if parts[ai] is None else parts[ai] + d

        def finish(accs):
            res = epilogue(accs, [e[...] for e in ex_refs])
            for o, r in zip(out_refs, res):
                o[...] = r.astype(o.dtype)

        if nk == 1:
            finish(parts)
        else:
            k = pl.program_id(2)

            @pl.when(k == 0)
            def _():
                for a, p in zip(acc_refs, parts):
                    a[...] = p

            @pl.when(k > 0)
            def _():
                for a, p in zip(acc_refs, parts):
                    a[...] += p

            @pl.when(k == nk - 1)
            def _():
                finish([a[...] for a in acc_refs])

    scratch = [] if nk == 1 else [pltpu.VMEM(acc_shape, F32) for _ in range(n_acc)]
    res = pl.pallas_call(
        body,
        name=name,
        grid=grid,
        in_specs=[s for _, s in ins] + [s for _, s in extras] + [pl.BlockSpec(memory_space=pl.ANY)] * nd,
        out_specs=[s for _, s in outs],
        out_shape=[o for o, _ in outs],
        scratch_shapes=scratch,
        compiler_params=_params("parallel", "parallel", "arbitrary"),
    )(*[a for a, _ in ins], *[a for a, _ in extras], *deps)
    return res


def _mm_rows(name, lhs, weights, dims, t, outs, epilogue, extras=(), deps=()):
    tm = _row_tile(t)
    nl, ne, nd, no = len(lhs), len(extras), len(deps), len(outs)
    ns = weights[0].shape[0]

    def body(*refs):
        lhs_refs = refs[:nl]
        w_hbm = refs[nl:2 * nl]
        ex_refs = refs[2 * nl:2 * nl + ne]
        out_refs = refs[2 * nl + ne + nd:2 * nl + ne + nd + no]
        w_vmem = refs[2 * nl + ne + nd + no:3 * nl + ne + nd + no]
        sem = refs[-1]

        @pl.when(pl.program_id(0) == 0)
        def _():
            copies = [pltpu.make_async_copy(w_hbm[p], w_vmem[p], sem.at[p]) for p in range(nl)]
            for cp in copies:
                cp.start()
            for cp in copies:
                cp.wait()

        acc = None
        for p in range(nl):
            pick = lhs[p][2]
            for j in range(ns):
                part = _dot(pick(lhs_refs[p], j), w_vmem[p][j], dims)
                acc = part if acc is None else acc + part
        res = epilogue([acc], [e[...] for e in ex_refs])
        for o, r in zip(out_refs, res):
            o[...] = r.astype(o.dtype)

    return pl.pallas_call(
        body,
        name=name,
        grid=(t // tm,),
        in_specs=[s for _, s, _ in lhs] + [pl.BlockSpec(memory_space=pl.ANY)] * nl + [s for _, s in extras]
        + [pl.BlockSpec(memory_space=pl.ANY)] * nd,
        out_specs=[s for _, s in outs],
        out_shape=[o for o, _ in outs],
        scratch_shapes=[pltpu.VMEM(w.shape, w.dtype) for w in weights] + [pltpu.SemaphoreType.DMA((nl,))],
        compiler_params=_params("arbitrary"),
    )(*[a for a, _, _ in lhs], *weights, *[a for a, _ in extras], *deps)


def _row_tile(t):
    return 512 if t % 512 == 0 else t


def _k_tile(t):
    return t if t <= 4096 else 1024


def _rmsnorm_fwd(name, x, g):
    t, d = x.shape
    tm = _row_tile(t)

    def body(x_ref, g_ref, h_ref):
        xv = x_ref[...]
        ms = jnp.mean(xv * xv, axis=-1, keepdims=True)
        h_ref[...] = (xv * lax.rsqrt(ms + RMS_EPS) * g_ref[...]).astype(BF16)

    return pl.pallas_call(
        body,
        name=name,
        grid=(t // tm,),
        in_specs=[pl.BlockSpec((tm, d), lambda i: (i, 0)), pl.BlockSpec((1, d), lambda i: (0, 0))],
        out_specs=pl.BlockSpec((tm, d), lambda i: (i, 0)),
        out_shape=jax.ShapeDtypeStruct((t, d), BF16),
        compiler_params=_params("parallel"),
    )(x, g)


def _norm_bwd_epilogue(copy_scale):
    def epilogue(accs, ex):
        dh = accs[0]
        xv, g, dres = ex
        ms = jnp.mean(xv * xv, axis=-1, keepdims=True)
        rstd = lax.rsqrt(ms + RMS_EPS)
        xhat = xv * rstd
        dxhat = dh * g
        dx = rstd * (dxhat - xhat * jnp.mean(dxhat * xhat, axis=-1, keepdims=True))
        out = dres + dx
        dg = jnp.sum(dh * xhat, axis=0, keepdims=True)
        if copy_scale is None:
            return out, dg
        return out, out * copy_scale, dg

    return epilogue


def _ffn_up(name, h, wg, wu, deps=()):
    t, d = h.shape
    ns, f, _ = wg.shape
    tm = _row_tile(t)

    def epilogue(accs, ex):
        a, b = accs
        sg = _sigmoid(a)
        act = a * sg
        return act, b * (sg * (1.0 + a * (1.0 - sg))), act * b

    w_spec = pl.BlockSpec((None, f, d), lambda j, i, k: (j, 0, 0))
    o_spec = pl.BlockSpec((None, tm, f), lambda j, i, k: (j, i, 0))
    o_shape = jax.ShapeDtypeStruct((ns, t, f), BF16)
    return _mm(
        name,
        ins=[(h, pl.BlockSpec((tm, d), lambda j, i, k: (i, 0))), (wg, w_spec), (wu, w_spec)],
        terms=[(0, 0, 1, NT), (1, 0, 2, NT)],
        n_acc=2,
        grid=(ns, t // tm, 1),
        acc_shape=(tm, f),
        outs=[(o_shape, o_spec)] * 3,
        epilogue=epilogue,
        deps=deps,
    )


def _shard_rows(arr, tm):
    ns, _, f = arr.shape
    return arr, pl.BlockSpec((ns, tm, f), lambda i: (0, i, 0)), lambda ref, j: ref[j]


def _ffn_down(name, z, wd, x):
    _, t, _ = z.shape
    d = wd.shape[2]
    tm = _row_tile(t)
    row = pl.BlockSpec((tm, d), lambda i: (i, 0))
    return _mm_rows(
        name, [_shard_rows(z, tm)], [wd], NN, t,
        outs=[(jax.ShapeDtypeStruct((t, d), F32), row)],
        epilogue=lambda accs, ex: (ex[0] + 0.5 * accs[0],),
        extras=[(x, row)],
    )[0]


def _ffn_down_loss(name, z, wd, x, target):
    _, t, _ = z.shape
    d = wd.shape[2]
    tm = _row_tile(t)
    nt = t // tm
    row = pl.BlockSpec((tm, d), lambda i: (i, 0))

    def epilogue(accs, ex):
        e = ex[0] + 0.5 * accs[0] - ex[1]
        dy = e * (1.0 / d)
        return dy, 0.5 * dy, jnp.sum(e * e, axis=0, keepdims=True)

    return _mm_rows(
        name, [_shard_rows(z, tm)], [wd], NN, t,
        outs=[(jax.ShapeDtypeStruct((t, d), F32), row), (jax.ShapeDtypeStruct((t, d), BF16), row),
              (jax.ShapeDtypeStruct((nt, 1, d), F32), pl.BlockSpec((None, 1, d), lambda i: (i, 0, 0)))],
        epilogue=epilogue,
        extras=[(x, row), (target, row)],
    )


def _ffn_bwd_act(name, dout, wd, act_a, dact_b, deps=()):
    t, d = dout.shape
    ns, f, _ = wd.shape
    tm = _row_tile(t)

    def epilogue(accs, ex):
        dz = accs[0]
        return dz * ex[1].astype(F32), dz * ex[0].astype(F32)

    act = pl.BlockSpec((None, tm, f), lambda j, i, k: (j, i, 0))
    o_shape = jax.ShapeDtypeStruct((ns, t, f), BF16)
    return _mm(
        name,
        ins=[(dout, pl.BlockSpec((tm, d), lambda j, i, k: (i, 0))),
             (wd, pl.BlockSpec((None, f, d), lambda j, i, k: (j, 0, 0)))],
        terms=[(0, 0, 1, NT)],
        n_acc=1,
        grid=(ns, t // tm, 1),
        acc_shape=(tm, f),
        outs=[(o_shape, act)] * 2,
        epilogue=epilogue,
        extras=[(act_a, act), (dact_b, act)],
        deps=deps,
    )


def _grad_w_shardrows(name, z, dout):
    ns, t, f = z.shape
    d = dout.shape[1]
    tk = _k_tile(t)
    return _mm(
        name,
        ins=[(z, pl.BlockSpec((None, tk, f), lambda j, n, k: (j, k, 0))),
             (dout, pl.BlockSpec((tk, d), lambda j, n, k: (k, 0)))],
        terms=[(0, 0, 1, TN)],
        n_acc=1,
        grid=(ns, 1, t // tk),
        acc_shape=(f, d),
        outs=[(jax.ShapeDtypeStruct((ns, f, d), F32), pl.BlockSpec((None, f, d), lambda j, n, k: (j, 0, 0)))],
        epilogue=lambda accs, ex: (accs[0],),
    )[0]


def _norm_bwd_outs(t, d, tm, copy_scale):
    row = pl.BlockSpec((tm, d), lambda i: (i, 0))
    outs = [(jax.ShapeDtypeStruct((t, d), F32), row)]
    if copy_scale is not None:
        outs.append((jax.ShapeDtypeStruct((t, d), BF16), row))
    outs.append((jax.ShapeDtypeStruct((t // tm, 1, d), F32), pl.BlockSpec((None, 1, d), lambda i: (i, 0, 0))))
    return row, outs


def _ffn_bwd_in(name, da, db, wg, wu, x, g, dres, copy_scale):
    _, t, _ = da.shape
    d = wg.shape[2]
    tm = _row_tile(t)
    row, outs = _norm_bwd_outs(t, d, tm, copy_scale)
    return _mm_rows(
        name, [_shard_rows(da, tm), _shard_rows(db, tm)], [wg, wu], NN, t,
        outs=outs,
        epilogue=_norm_bwd_epilogue(copy_scale),
        extras=[(x, row), (g, pl.BlockSpec((1, d), lambda i: (0, 0))), (dres, row)],
    )


def _in_proj(name, h, w_in):
    t, d = h.shape
    ns, _, pj = w_in.shape
    tm = _row_tile(t)
    return _mm(
        name,
        ins=[(h, pl.BlockSpec((tm, d), lambda j, i, k: (i, 0))),
             (w_in, pl.BlockSpec((None, d, pj), lambda j, i, k: (j, 0, 0)))],
        terms=[(0, 0, 1, NN)],
        n_acc=1,
        grid=(ns, t // tm, 1),
        acc_shape=(tm, pj),
        outs=[(jax.ShapeDtypeStruct((t, ns * pj), F32), pl.BlockSpec((tm, pj), lambda j, i, k: (i, j)))],
        epilogue=lambda accs, ex: (accs[0],),
    )[0]


def _in_proj_bwd(name, dp, w_in, x, g, dres, copy_scale):
    t = dp.shape[0]
    ns, d, pj = w_in.shape
    tm = _row_tile(t)
    row, outs = _norm_bwd_outs(t, d, tm, copy_scale)
    cols = (dp, pl.BlockSpec((tm, ns * pj), lambda i: (i, 0)), lambda ref, j: ref[:, j * pj:(j + 1) * pj])
    return _mm_rows(
        name, [cols], [w_in], NT, t,
        outs=outs,
        epilogue=_norm_bwd_epilogue(copy_scale),
        extras=[(x, row), (g, pl.BlockSpec((1, d), lambda i: (0, 0))), (dres, row)],
    )


def _grad_w_in(name, h, dp, ns):
    t, d = h.shape
    pj = dp.shape[1] // ns
    tk = _k_tile(t)
    return _mm(
        name,
        ins=[(h, pl.BlockSpec((tk, d), lambda j, n, k: (k, 0))),
             (dp, pl.BlockSpec((tk, pj), lambda j, n, k: (k, j)))],
        terms=[(0, 0, 1, TN)],
        n_acc=1,
        grid=(ns, 1, t // tk),
        acc_shape=(d, pj),
        outs=[(jax.ShapeDtypeStruct((ns, d, pj), F32), pl.BlockSpec((None, d, pj), lambda j, n, k: (j, 0, 0)))],
        epilogue=lambda accs, ex: (accs[0],),
    )[0]


def _out_proj(name, mix, w_out, x):
    t, dm = mix.shape
    d = w_out.shape[1]
    tm = _row_tile(t)
    row = pl.BlockSpec((tm, d), lambda i, n, k: (i, 0))
    return _mm(
        name,
        ins=[(mix, pl.BlockSpec((tm, dm), lambda i, n, k: (i, 0))),
             (w_out, pl.BlockSpec((dm, d), lambda i, n, k: (0, 0)))],
        terms=[(0, 0, 1, NN)],
        n_acc=1,
        grid=(t // tm, 1, 1),
        acc_shape=(tm, d),
        outs=[(jax.ShapeDtypeStruct((t, d), F32), row)],
        epilogue=lambda accs, ex: (ex[0] + accs[0],),
        extras=[(x, row)],
    )[0]


def _out_proj_bwd(name, dx, w_out, deps=()):
    t, d = dx.shape
    dm = w_out.shape[0]
    tm = _row_tile(t)
    return _mm(
        name,
        ins=[(dx, pl.BlockSpec((tm, d), lambda i, n, k: (i, 0))),
             (w_out, pl.BlockSpec((dm, d), lambda i, n, k: (0, 0)))],
        terms=[(0, 0, 1, NT)],
        n_acc=1,
        grid=(t // tm, 1, 1),
        acc_shape=(tm, dm),
        outs=[(jax.ShapeDtypeStruct((t, dm), F32), pl.BlockSpec((tm, dm), lambda i, n, k: (i, 0)))],
        epilogue=lambda accs, ex: (accs[0],),
        deps=deps,
    )[0]


def _grad_w_out(name, mix, dx):
    t, dm = mix.shape
    d = dx.shape[1]
    tk = _k_tile(t)
    return _mm(
        name,
        ins=[(mix, pl.BlockSpec((tk, dm), lambda a, n, k: (k, 0))),
             (dx, pl.BlockSpec((tk, d), lambda a, n, k: (k, 0)))],
        terms=[(0, 0, 1, TN)],
        n_acc=1,
        grid=(1, 1, t // tk),
        acc_shape=(dm, d),
        outs=[(jax.ShapeDtypeStruct((dm, d), F32), pl.BlockSpec((dm, d), lambda a, n, k: (0, 0)))],
        epilogue=lambda accs, ex: (accs[0],),
    )[0]


def _head_group_matrix():
    r = lax.broadcasted_iota(jnp.int32, (ATTN_W, ATTN_W), 0)
    c = lax.broadcasted_iota(jnp.int32, (ATTN_W, ATTN_W), 1)
    same = jnp.right_shift(r, 6) == jnp.right_shift(c, 6)
    return jnp.where(same, 1.0, 0.0).astype(BF16)


def _qk_prep(name, proj, gq, gk):
    b, s, _ = proj.shape
    tm = KPAD
    nb = s // tm

    def body(q_ref, k_ref, v_ref, gq_ref, gk_ref, qn_ref, kn_ref, vb_ref):
        j = pl.program_id(1)
        bd = _head_group_matrix()

        def norm(xv, g):
            ms = _dot_exact_rhs(xv * xv, bd) * (1.0 / ATTN_DH)
            return xv * lax.rsqrt(ms + RMS_EPS) * g

        @pl.when(j == 0)
        def _():
            kn_ref[...] = jnp.zeros_like(kn_ref)
            vb_ref[...] = jnp.zeros_like(vb_ref)

        @pl.when(j > 0)
        def _():
            qn_ref[...] = norm(q_ref[...], gq_ref[...]).astype(BF16)
            kn_ref[...] = norm(k_ref[...], gk_ref[...]).astype(BF16)
            vb_ref[...] = v_ref[...].astype(BF16)

    src_blk = lambda col: pl.BlockSpec((None, tm, ATTN_W), lambda bi, j: (bi, jnp.maximum(j - 1, 0), col))
    gspec = pl.BlockSpec((1, ATTN_W), lambda bi, j: (0, 0))
    padded = pl.BlockSpec((None, tm, ATTN_W), lambda bi, j: (bi, j, 0))
    return pl.pallas_call(
        body,
        name=name,
        grid=(b, nb + 1),
        in_specs=[src_blk(0), src_blk(1), src_blk(2), gspec, gspec],
        out_specs=[src_blk(0), padded, padded],
        out_shape=[jax.ShapeDtypeStruct((b, s, ATTN_W), BF16), jax.ShapeDtypeStruct((b, KPAD + s, ATTN_W), BF16),
                   jax.ShapeDtypeStruct((b, KPAD + s, ATTN_W), BF16)],
        compiler_params=_params("parallel", "arbitrary"),
    )(proj, proj, proj, gq, gk)


def _qk_prep_bwd(name, proj, dqn, dkn, dv, gq, gk):
    b, s, _ = proj.shape
    tm = KPAD
    nb = s // tm

    def body(q_ref, k_ref, dqn_ref, dkn_ref, dv_ref, gq_ref, gk_ref, dq_ref, dk_ref, dvb_ref, dgq_ref, dgk_ref):
        bd = _head_group_matrix()

        def bwd(xv, dy, g):
            ms = _dot_exact_rhs(xv * xv, bd) * (1.0 / ATTN_DH)
            rstd = lax.rsqrt(ms + RMS_EPS)
            xhat = xv * rstd
            dxhat = dy * g
            gm = _dot_exact_rhs(dxhat * xhat, bd) * (1.0 / ATTN_DH)
            return rstd * (dxhat - xhat * gm), jnp.sum(dy * xhat, axis=0, keepdims=True)

        dq, dgq = bwd(q_ref[...], dqn_ref[...], gq_ref[...])
        dk, dgk = bwd(k_ref[...], dkn_ref[...], gk_ref[...])
        dq_ref[...] = dq.astype(BF16)
        dk_ref[...] = dk.astype(BF16)
        dvb_ref[...] = dv_ref[...].astype(BF16)
        dgq_ref[...] = dgq
        dgk_ref[...] = dgk

    col = lambda c: pl.BlockSpec((None, tm, ATTN_W), lambda bi, j: (bi, j, c))
    past_pad = pl.BlockSpec((None, tm, ATTN_W), lambda bi, j: (bi, j + 1, 0))
    gspec = pl.BlockSpec((1, ATTN_W), lambda bi, j: (0, 0))
    pspec = pl.BlockSpec((None, 1, ATTN_W), lambda bi, j: (bi * nb + j, 0, 0))
    o_shape = jax.ShapeDtypeStruct((b, s, ATTN_W), BF16)
    p_shape = jax.ShapeDtypeStruct((b * nb, 1, ATTN_W), F32)
    return pl.pallas_call(
        body,
        name=name,
        grid=(b, nb),
        in_specs=[col(0), col(1), col(0), past_pad, past_pad, gspec, gspec],
        out_specs=[col(0)] * 3 + [pspec] * 2,
        out_shape=[o_shape] * 3 + [p_shape] * 2,
        compiler_params=_params("parallel", "parallel"),
    )(proj, proj, dqn, dkn, dv, gq, gk)


Q_CHUNKS = 4
QBLK = Q_CHUNKS * CHUNK
WIN = (LEFT_CHUNKS + Q_CHUNKS) * CHUNK
DB_W = BAND + CHUNK
MASKED = -1e30


def _band_table(bias):
    rows = [jnp.pad(bias, ((0, 0), (0, 0), (CHUNK * i, WIN - BAND - CHUNK * i)), constant_values=MASKED)
            for i in range(Q_CHUNKS)]
    return jnp.concatenate(rows, axis=1)


def _head_lanes(hh):
    lane = lax.broadcasted_iota(jnp.int32, (1, LANES), 1)
    return (lane < ATTN_DH) if hh == 0 else (lane >= ATTN_DH)


def _attn_probs(qh, kw, table, start):
    s = _dot(qh, kw, NT) * (ATTN_DH ** -0.5) + table
    col = lax.broadcasted_iota(jnp.int32, (QBLK, WIN), 1)
    s = jnp.where(col + start >= KPAD, s, MASKED)
    m = jnp.max(s, axis=-1, keepdims=True)
    p = jnp.exp(s - m)
    return p * (1.0 / jnp.sum(p, axis=-1, keepdims=True))


def _attn_fwd(name, q, k, v, table):
    b, s, w = q.shape
    sp = k.shape[1]

    def body(q_ref, k_ref, v_ref, t_ref, o_ref):
        start = pl.multiple_of(pl.program_id(2) * QBLK, QBLK)
        kw = k_ref[pl.ds(start, WIN), :]
        vw = v_ref[pl.ds(start, WIN), :]
        q2 = q_ref[...]
        out = jnp.zeros((QBLK, LANES), F32)
        for hh in range(2):
            mine = _head_lanes(hh)
            p = _attn_probs(jnp.where(mine, q2, jnp.zeros_like(q2)), kw, t_ref[hh], start)
            out = jnp.where(mine, _dot(p.astype(BF16), vw), out)
        o_ref[...] = out.astype(BF16)

    qspec = pl.BlockSpec((None, QBLK, LANES), lambda p, bi, i: (bi, i, p))
    kspec = pl.BlockSpec((None, sp, LANES), lambda p, bi, i: (bi, 0, p))
    return pl.pallas_call(
        body,
        name=name,
        grid=(w // LANES, b, s // QBLK),
        in_specs=[qspec, kspec, kspec, pl.BlockSpec((2, QBLK, WIN), lambda p, bi, i: (p, 0, 0))],
        out_specs=qspec,
        out_shape=jax.ShapeDtypeStruct((b, s, w), BF16),
        compiler_params=_params("parallel", "parallel", "arbitrary"),
    )(q, k, v, table)


def _attn_bwd(name, q, k, v, table, dmix):
    b, s, w = q.shape
    sp = k.shape[1]

    def body(q_ref, k_ref, v_ref, t_ref, do_ref, dq_ref, dk_ref, dv_ref, dbe_ref, dbo_ref):
        bi = pl.program_id(1)
        i = pl.program_id(2)
        start = pl.multiple_of(i * QBLK, QBLK)
        win = pl.ds(start, WIN)

        @pl.when(i == 0)
        def _():
            dk_ref[...] = jnp.zeros_like(dk_ref)
            dv_ref[...] = jnp.zeros_like(dv_ref)

        @pl.when(jnp.logical_and(i == 0, bi == 0))
        def _():
            dbe_ref[...] = jnp.zeros_like(dbe_ref)
            dbo_ref[...] = jnp.zeros_like(dbo_ref)

        kw = k_ref[win, :]
        vw = v_ref[win, :]
        q2 = q_ref[...]
        do2 = do_ref[...].astype(BF16)
        dq = jnp.zeros((QBLK, LANES), F32)
        for hh in range(2):
            mine = _head_lanes(hh)
            qh = jnp.where(mine, q2, jnp.zeros_like(q2))
            doh = jnp.where(mine, do2, jnp.zeros_like(do2))
            p = _attn_probs(qh, kw, t_ref[hh], start)
            dp = _dot(doh, vw, NT)
            ds = p * (dp - jnp.sum(p * dp, axis=-1, keepdims=True))
            for qi in range(Q_CHUNKS):
                c0 = (qi // 2) * LANES
                blk = ds[qi * CHUNK:(qi + 1) * CHUNK, c0:c0 + DB_W]
                if qi % 2 == 0:
                    dbe_ref[hh] += blk
                else:
                    dbo_ref[hh] += blk
            dsb = (ds * (ATTN_DH ** -0.5)).astype(BF16)
            dq = jnp.where(mine, _dot(dsb, kw), dq)
            dk_ref[win, :] += _dot(dsb, qh, TN)
            dv_ref[win, :] += _dot(p.astype(BF16), doh, TN)
        dq_ref[...] = dq

    qspec = pl.BlockSpec((None, QBLK, LANES), lambda p, bi, i: (bi, i, p))
    kspec = pl.BlockSpec((None, sp, LANES), lambda p, bi, i: (bi, 0, p))
    dbspec = pl.BlockSpec((2, CHUNK, DB_W), lambda p, bi, i: (p, 0, 0))
    db_shape = jax.ShapeDtypeStruct((ATTN_HEADS, CHUNK, DB_W), F32)
    return pl.pallas_call(
        body,
        name=name,
        grid=(w // LANES, b, s // QBLK),
        in_specs=[qspec, kspec, kspec, pl.BlockSpec((2, QBLK, WIN), lambda p, bi, i: (p, 0, 0)), qspec],
        out_specs=[qspec, kspec, kspec, dbspec, dbspec],
        out_shape=[jax.ShapeDtypeStruct((b, s, w), F32), jax.ShapeDtypeStruct((b, sp, w), F32),
                   jax.ShapeDtypeStruct((b, sp, w), F32), db_shape, db_shape],
        compiler_params=_params("arbitrary", "arbitrary", "arbitrary"),
    )(q, k, v, table, dmix)


HQ_COL = 3 * ATTN_W // HGRN_DH
HF_COL = HQ_COL + HGRN_HEADS
HI_COL = HF_COL + HGRN_HEADS
HG_COL = HI_COL + HGRN_HEADS
HGRN_PAIR = 2
PAIR_W = HGRN_PAIR * HGRN_DH


def _tri(lower):
    r = lax.broadcasted_iota(jnp.int32, (CHUNK, CHUNK), 0)
    c = lax.broadcasted_iota(jnp.int32, (CHUNK, CHUNK), 1)
    return (r >= c) if lower else (r <= c)


def _hgrn_chunk(hq, hf, lb, tril):
    sig = _sigmoid(hf)
    f = lb + (1.0 - lb) * sig
    g = jnp.log(f)
    ones_l = jnp.where(tril, 1.0, 0.0).astype(BF16)
    b = _dot_exact_lhs(ones_l, g)
    bl = jnp.sum(g, axis=0, keepdims=True)
    rows = lax.broadcasted_iota(jnp.int32, g.shape, 0)
    bm = jnp.sum(jnp.where(rows <= CHUNK // 2, g, 0.0), axis=0, keepdims=True)
    sq = _sigmoid(hq)
    q = hq * sq
    k = 1.0 - f
    return sig, f, b, bl, bm, sq, q, k


def _hgrn_fwd(name, proj, lb, go, b, s):
    nc = s // CHUNK
    t = b * s

    def body(hq_ref, hf_ref, hi_ref, hg_ref, lb_ref, go_ref, ro_ref, oraw_ref, st_ref, s_scr):
        tril = _tri(True)
        gov = go_ref[...]
        s_scr[...] = jnp.zeros_like(s_scr)

        def step(c, carry):
            sl = pl.ds(pl.multiple_of(c * CHUNK, CHUNK), CHUNK)
            for hh in range(HGRN_PAIR):
                cols = pl.ds(hh * HGRN_DH, HGRN_DH)
                lbv = lb_ref[:, cols]
                hg = hg_ref[sl, cols]
                _, _, bb, bl, bm, _, q, k = _hgrn_chunk(hq_ref[sl, cols], hf_ref[sl, cols], lbv, tril)
                vb = hi_ref[sl, cols].astype(BF16)
                qe = (q * jnp.exp(bb - bm)).astype(BF16)
                ke = (k * jnp.exp(bm - bb)).astype(BF16)
                a = jnp.where(tril, _dot(qe, ke, NT), 0.0)
                st = s_scr[hh]
                st_ref[hh, c] = st
                qb = (q * jnp.exp(bb)).astype(BF16)
                o = _dot(a.astype(BF16), vb) + _dot(qb, st.astype(BF16), NT)
                kb = (k * jnp.exp(bl - bb)).astype(BF16)
                s_scr[hh] = st * jnp.exp(bl) + _dot(vb, kb, TN)
                rstd = lax.rsqrt(jnp.mean(o * o, axis=-1, keepdims=True) + RMS_EPS)
                ro_ref[sl, cols] = ((o * rstd * gov) * _silu(hg)).astype(BF16)
                oraw_ref[sl, cols] = o
            return carry

        lax.fori_loop(0, nc, step, 0)

    col = lambda base: pl.BlockSpec((s, PAIR_W), lambda bi, g: (bi, base // HGRN_PAIR + g))
    vec = pl.BlockSpec((1, PAIR_W), lambda bi, g: (0, g))
    out = pl.BlockSpec((s, PAIR_W), lambda bi, g: (bi, g))
    return pl.pallas_call(
        body,
        name=name,
        grid=(b, HGRN_HEADS // HGRN_PAIR),
        in_specs=[col(HQ_COL), col(HF_COL), col(HI_COL), col(HG_COL), vec,
                  pl.BlockSpec((1, HGRN_DH), lambda bi, g: (0, 0))],
        out_specs=[out, out,
                   pl.BlockSpec((None, HGRN_PAIR, nc, HGRN_DH, HGRN_DH), lambda bi, g: (bi, g, 0, 0, 0))],
        out_shape=[jax.ShapeDtypeStruct((t, HGRN_W), BF16), jax.ShapeDtypeStruct((t, HGRN_W), F32),
                   jax.ShapeDtypeStruct((b, HGRN_HEADS, nc, HGRN_DH, HGRN_DH), F32)],
        scratch_shapes=[pltpu.VMEM((HGRN_PAIR, HGRN_DH, HGRN_DH), F32)],
        compiler_params=_params("parallel", "parallel"),
    )(proj, proj, proj, proj, lb, go)


def _hgrn_bwd(name, proj, lb, go, oraw, states, dmix, b, s):
    nc = s // CHUNK
    t = b * s

    def body(hq_ref, hf_ref, hi_ref, hg_ref, lb_ref, go_ref, oraw_ref, st_ref, dro_ref,
             dhq_ref, dhf_ref, dhi_ref, dhg_ref, dlb_ref, dgo_ref, ds_scr, dlb_scr, dgo_scr):
        tril = _tri(True)
        ones_u = jnp.where(_tri(False), 1.0, 0.0).astype(BF16)
        gov = go_ref[...]
        ds_scr[...] = jnp.zeros_like(ds_scr)
        dlb_scr[...] = jnp.zeros_like(dlb_scr)
        dgo_scr[...] = jnp.zeros_like(dgo_scr)

        def step(ci, carry):
            c = nc - 1 - ci
            sl = pl.ds(pl.multiple_of(c * CHUNK, CHUNK), CHUNK)
            for hh in range(HGRN_PAIR):
                cols = pl.ds(hh * HGRN_DH, HGRN_DH)
                lbv = lb_ref[:, cols]
                hq = hq_ref[sl, cols]
                hg = hg_ref[sl, cols]
                sig, f, bb, bl, bm, sq, q, k = _hgrn_chunk(hq, hf_ref[sl, cols], lbv, tril)
                vb = hi_ref[sl, cols].astype(BF16)
                ebm = jnp.exp(bb - bm)
                embm = jnp.exp(bm - bb)
                eb = jnp.exp(bb)
                ebl = jnp.exp(bl - bb)
                e_last = jnp.exp(bl)
                qe = (q * ebm).astype(BF16)
                ke = (k * embm).astype(BF16)
                qb = (q * eb).astype(BF16)
                kb = (k * ebl).astype(BF16)
                a = jnp.where(tril, _dot(qe, ke, NT), 0.0)
                st = st_ref[hh, c]
                dst = ds_scr[hh]
                o = oraw_ref[sl, cols]
                dro = dro_ref[sl, cols]
                sg = _sigmoid(hg)
                rstd = lax.rsqrt(jnp.mean(o * o, axis=-1, keepdims=True) + RMS_EPS)
                ohat = o * rstd
                dn = dro * (hg * sg)
                dhg_ref[sl, cols] = (dro * (ohat * gov) * (sg * (1.0 + hg * (1.0 - sg)))).astype(BF16)
                dgo_scr[:, cols] += jnp.sum(dn * ohat, axis=0, keepdims=True)
                dohat = dn * gov
                do = rstd * (dohat - ohat * jnp.mean(dohat * ohat, axis=-1, keepdims=True))
                dob = do.astype(BF16)
                dab = jnp.where(tril, _dot(dob, vb, NT), 0.0).astype(BF16)
                stb = st.astype(BF16)
                dstb = dst.astype(BF16)
                dv = _dot(a.astype(BF16), dob, TN) + _dot(kb, dstb, NT)
                dqe = _dot(dab, ke)
                dke = _dot(dab, qe, TN)
                dqb = _dot(dob, stb)
                dkb = _dot(vb, dstb)
                dq = dqe * ebm + dqb * eb
                dk = dke * embm + dkb * ebl
                db = (qe.astype(F32) * dqe - ke.astype(F32) * dke) + q * (dqb * eb) - k * (dkb * ebl)
                d_last = (jnp.sum(k * ebl * dkb, axis=0, keepdims=True)
                          + jnp.sum(dst * st, axis=0, keepdims=True) * e_last)
                dg = _dot_exact_lhs(ones_u, db) + d_last
                df = dg / f - dk
                dhf_ref[sl, cols] = (df * (1.0 - lbv) * sig * (1.0 - sig)).astype(BF16)
                dlb_scr[:, cols] += jnp.sum(df * (1.0 - sig), axis=0, keepdims=True)
                dhq_ref[sl, cols] = (dq * (sq * (1.0 + hq * (1.0 - sq)))).astype(BF16)
                dhi_ref[sl, cols] = dv.astype(BF16)
                ds_scr[hh] = dst * e_last + _dot(dob, qb, TN)
            return carry

        lax.fori_loop(0, nc, step, 0)
        dlb_ref[...] = dlb_scr[...]
        dgo_ref[...] = dgo_scr[...]

    col = lambda base: pl.BlockSpec((s, PAIR_W), lambda bi, g: (bi, base // HGRN_PAIR + g))
    vec = pl.BlockSpec((1, PAIR_W), lambda bi, g: (0, g))
    out = pl.BlockSpec((s, PAIR_W), lambda bi, g: (bi, g))
    part = pl.BlockSpec((None, 1, PAIR_W), lambda bi, g: (bi, 0, g))
    o_shape = jax.ShapeDtypeStruct((t, HGRN_W), BF16)
    p_shape = jax.ShapeDtypeStruct((b, 1, HGRN_W), F32)
    return pl.pallas_call(
        body,
        name=name,
        grid=(b, HGRN_HEADS // HGRN_PAIR),
        in_specs=[col(HQ_COL), col(HF_COL), col(HI_COL), col(HG_COL), vec,
                  pl.BlockSpec((1, HGRN_DH), lambda bi, g: (0, 0)), out,
                  pl.BlockSpec((None, HGRN_PAIR, nc, HGRN_DH, HGRN_DH), lambda bi, g: (bi, g, 0, 0, 0)),
                  col(ATTN_W // HGRN_DH)],
        out_specs=[out] * 4 + [part] * 2,
        out_shape=[o_shape] * 4 + [p_shape] * 2,
        scratch_shapes=[pltpu.VMEM((HGRN_PAIR, HGRN_DH, HGRN_DH), F32), pltpu.VMEM((1, PAIR_W), F32),
                        pltpu.VMEM((1, PAIR_W), F32)],
        compiler_params=pltpu.CompilerParams(dimension_semantics=("parallel", "parallel"),
                                             vmem_limit_bytes=VMEM_LIMIT_BIG),
    )(proj, proj, proj, proj, lb, go, oraw, states, dmix)


def _small_grads(name, dg1, dgm, dg2, dgq, dgk, dbias_t, dlb, dgo, lbp):
    d = dg1.shape[1]

    def body(dg1_ref, dgm_ref, dg2_ref, dgq_ref, dgk_ref, dbias_ref, dlb_ref, dgo_ref, lbp_ref,
             g1_ref, gm_ref, g2_ref, gq_ref, gk_ref, rb_ref, lbg_ref, go_ref):
        g1_ref[...] = jnp.sum(dg1_ref[...], axis=0, keepdims=True)
        gm_ref[...] = jnp.sum(dgm_ref[...], axis=0, keepdims=True)
        g2_ref[...] = jnp.sum(dg2_ref[...], axis=0, keepdims=True)
        r = lax.broadcasted_iota(jnp.int32, (ATTN_W, ATTN_DH), 0)
        cidx = lax.broadcasted_iota(jnp.int32, (ATTN_W, ATTN_DH), 1)
        fold = jnp.where(jnp.bitwise_and(r, ATTN_DH - 1) == cidx, 1.0, 0.0).astype(BF16)
        gq_ref[...] = jnp.sum(_dot_exact_rhs(dgq_ref[...], fold), axis=0, keepdims=True)
        gk_ref[...] = jnp.sum(_dot_exact_rhs(dgk_ref[...], fold), axis=0, keepdims=True)
        gosum = jnp.sum(dgo_ref[...], axis=0, keepdims=True)
        go_ref[...] = (gosum[:, 0:HGRN_DH] + gosum[:, HGRN_DH:2 * HGRN_DH]
                       + gosum[:, 2 * HGRN_DH:3 * HGRN_DH] + gosum[:, 3 * HGRN_DH:4 * HGRN_DH])
        p0 = lbp_ref[0:1, :]
        p1 = lbp_ref[1:2, :]
        lbv = 1.0 / (1.0 + jnp.exp(p1 - p0))
        dp0 = jnp.sum(dlb_ref[...], axis=0, keepdims=True) * lbv * (1.0 - lbv)
        lbg_ref[0:1, :] = dp0
        lbg_ref[1:2, :] = -dp0
        sidx = lax.broadcasted_iota(jnp.int32, (BAND, N_REL_PAD), 0)
        ridx = lax.broadcasted_iota(jnp.int32, (BAND, N_REL_PAD), 1)

        def step(tq, acc):
            rel = jnp.clip(tq + KPAD - sidx, -REL_CLIP, REL_CLIP) + REL_CLIP
            onehot = jnp.where(rel == ridx, 1.0, 0.0).astype(BF16)
            return acc + _dot_exact_rhs(dbias_ref[tq], onehot)

        rb_ref[...] = lax.fori_loop(0, CHUNK, step, jnp.zeros((ATTN_HEADS, N_REL_PAD), F32))

    ins = [dg1, dgm, dg2, dgq, dgk, dbias_t, dlb, dgo, lbp]
    outs = [jax.ShapeDtypeStruct((1, d), F32)] * 3 + [jax.ShapeDtypeStruct((1, ATTN_DH), F32)] * 2 + [
        jax.ShapeDtypeStruct((ATTN_HEADS, N_REL_PAD), F32), jax.ShapeDtypeStruct((2, HGRN_W), F32),
        jax.ShapeDtypeStruct((1, HGRN_DH), F32)]
    vm = pl.BlockSpec(memory_space=pltpu.VMEM)
    return pl.pallas_call(
        body,
        name=name,
        in_specs=[vm] * len(ins),
        out_specs=[vm] * len(outs),
        out_shape=outs,
        compiler_params=pltpu.CompilerParams(vmem_limit_bytes=VMEM_LIMIT),
    )(*ins)


def _adam_update(w, g, m, v):
    m2 = ADAM_B1 * m + (1.0 - ADAM_B1) * g
    v2 = ADAM_B2 * v + (1.0 - ADAM_B2) * (g * g)
    m_hat = m2 / (1.0 - ADAM_B1 ** ADAM_STEP)
    v_hat = v2 / (1.0 - ADAM_B2 ** ADAM_STEP)
    delta = -ADAM_LR * (m_hat / (jnp.sqrt(v_hat) + ADAM_EPS) + ADAM_WD * w)
    return delta, m2, v2


def _rows_tile(r):
    for cand in (256, 352, 128, 176, 64, 32, 16):
        if r % cand == 0 and r > cand:
            return cand
    return r


def _pair_sum(name, grad, theirs, core):
    n, half, c = theirs.shape
    tr = _rows_tile(half)
    nth = half // tr

    def body(core_ref, a_ref, b_ref, o_ref):
        o_ref[...] = (a_ref[...] + b_ref[...]).astype(o_ref.dtype)

    spec = pl.BlockSpec((None, tr, c), lambda i, j, core_ref: (i, j, 0))
    return pl.pallas_call(
        body, name=name,
        grid_spec=pltpu.PrefetchScalarGridSpec(
            num_scalar_prefetch=1, grid=(n, nth),
            in_specs=[pl.BlockSpec((None, tr, c), lambda i, j, core_ref: (i, core_ref[0] * nth + j, 0)), spec],
            out_specs=spec),
        out_shape=jax.ShapeDtypeStruct((n, half, c), BF16), compiler_params=_params("parallel", "parallel"),
    )(core, grad, theirs)


def _chip_sum(name, own, parts, chip):
    _, half, c = own.shape
    tr = _rows_tile(half)

    def body(chip_ref, own_ref, p_ref, o_ref):
        me = chip_ref[0]
        mine = own_ref[...].astype(F32)
        flip_x, flip_y, flip_xy = (p_ref[i].astype(F32) for i in range(3))
        acc = None
        for k in range(N_CHIPS):
            rel = jnp.bitwise_xor(me, k)
            term = jnp.where(rel == 0, mine, jnp.where(rel == 2, flip_x, jnp.where(rel == 1, flip_y, flip_xy)))
            acc = term if acc is None else acc + term
        o_ref[...] = acc

    return pl.pallas_call(
        body, name=name,
        grid_spec=pltpu.PrefetchScalarGridSpec(
            num_scalar_prefetch=1, grid=(half // tr,),
            in_specs=[pl.BlockSpec((None, tr, c), lambda j, chip_ref: (chip_ref[0], j, 0)),
                      pl.BlockSpec((3, tr, c), lambda j, chip_ref: (0, j, 0))],
            out_specs=pl.BlockSpec((tr, c), lambda j, chip_ref: (j, 0))),
        out_shape=jax.ShapeDtypeStruct((half, c), F32), compiler_params=_params("parallel"),
    )(chip, own, parts)


def _adamw(name, w, g_mine, g_theirs, m, v, core):
    _, r, c = w.shape
    half = r // 2
    tr = _rows_tile(half)
    nth = half // tr

    def body(core_ref, w_ref, gm_ref, gt_ref, m_ref, v_ref, g_ref, d_ref, m2_ref, v2_ref):
        g = jnp.where(pl.program_id(0) == core_ref[0], gm_ref[...], gt_ref[...])
        delta, m2, v2 = _adam_update(w_ref[...], g, m_ref[...], v_ref[...])
        g_ref[...] = g
        d_ref[...] = delta
        m2_ref[...] = m2
        v2_ref[...] = v2

    full = pl.BlockSpec((None, tr, c), lambda h, j, core_ref: (0, h * nth + j, 0))
    part = pl.BlockSpec((tr, c), lambda h, j, core_ref: (j, 0))
    shape = jax.ShapeDtypeStruct((1, r, c), F32)
    return pl.pallas_call(
        body, name=name,
        grid_spec=pltpu.PrefetchScalarGridSpec(
            num_scalar_prefetch=1, grid=(2, nth), in_specs=[full, part, part, full, full], out_specs=[full] * 4),
        out_shape=[shape] * 4, compiler_params=_params("parallel", "parallel"),
    )(core, w, g_mine, g_theirs, m, v)


def _rel_bias_table(name, rel_bias):
    padded = jnp.pad(rel_bias, ((0, 0), (0, N_REL_PAD - N_REL)))

    def body(rb_ref, o_ref):
        ridx = lax.broadcasted_iota(jnp.int32, (N_REL_PAD, BAND), 0)
        sidx = lax.broadcasted_iota(jnp.int32, (N_REL_PAD, BAND), 1)
        rb = rb_ref[...]

        def step(tq, carry):
            rel = jnp.clip(tq + KPAD - sidx, -REL_CLIP, REL_CLIP) + REL_CLIP
            onehot = jnp.where(rel == ridx, 1.0, 0.0).astype(BF16)
            o_ref[tq] = _dot_exact_rhs(rb, onehot)
            return carry

        lax.fori_loop(0, CHUNK, step, 0)

    vm = pl.BlockSpec(memory_space=pltpu.VMEM)
    table = pl.pallas_call(
        body, name=name, in_specs=[vm], out_specs=vm,
        out_shape=jax.ShapeDtypeStruct((CHUNK, ATTN_HEADS, BAND), F32),
    )(padded)
    return table.transpose(1, 0, 2)


def _adamw_small(name, w, parts, m, v):
    def body(w_ref, p_ref, m_ref, v_ref, g_ref, d_ref, m2_ref, v2_ref):
        g = p_ref[0]
        for i in range(1, N_DEV):
            g = g + p_ref[i]
        delta, m2, v2 = _adam_update(w_ref[...], g, m_ref[...], v_ref[...])
        g_ref[...] = g
        d_ref[...] = delta
        m2_ref[...] = m2
        v2_ref[...] = v2

    vm = pl.BlockSpec(memory_space=pltpu.VMEM)
    shape = jax.ShapeDtypeStruct((SMALL_ROWS, SMALL_COLS), F32)
    return pl.pallas_call(
        body, name=name, in_specs=[vm] * 4, out_specs=[vm] * 4, out_shape=[shape] * 4,
    )(w, parts, m, v)


def _position():
    return lax.axis_index("x"), lax.axis_index("y"), lax.axis_index("c")


def _other_chips(x, y):
    return [(1 - x, y), (x, 1 - y), (1 - x, 1 - y)]


ANY = pl.BlockSpec(memory_space=pl.ANY)


HBM = pl.BlockSpec(memory_space=pltpu.HBM)
SEM = pl.BlockSpec(memory_space=pltpu.SEMAPHORE)
SPLIT_COPY = pltpu.SideEffectType.DATAFLOW_SIDE_EFFECTING


def _gather_copy(shards, outs, send_sem, recv_sem, i, j):
    x, y, c = _position()
    chips = _other_chips(x, y)
    half = shards[i].shape[0] // 2
    rows = pl.ds(pl.multiple_of(c * half, 16), half)
    return pltpu.make_async_remote_copy(
        src_ref=shards[i].at[rows, :], dst_ref=outs[i].at[2 * x + y, rows, :],
        send_sem=send_sem.at[3 * i + j], recv_sem=recv_sem.at[3 * i + j],
        device_id=(chips[j][0], chips[j][1], c), device_id_type=MESH)


def _gather_start(name, shards, after):
    n = len(shards)

    def body(*refs):
        srcs, outs = refs[:n], refs[n:2 * n]
        send_sem, recv_sem = refs[2 * n + len(after)], refs[2 * n + len(after) + 1]
        token = refs[-1]
        for i in range(n):
            for j in range(3):
                _gather_copy(srcs, outs, send_sem, recv_sem, i, j).start()
        token[...] = jnp.zeros_like(token)

    full = [(N_CHIPS,) + s.shape for s in shards]
    res = pl.pallas_call(
        body,
        name=name,
        in_specs=[HBM] * (2 * n) + [ANY] * len(after),
        out_specs=[SEM, SEM] + [HBM] * (2 * n) + [pl.BlockSpec(memory_space=pltpu.VMEM)],
        out_shape=[pltpu.SemaphoreType.DMA((3 * n,)), pltpu.SemaphoreType.DMA((3 * n,))]
        + [pltpu.HBM(s.shape, s.dtype) for s in shards]
        + [pltpu.HBM(shp, s.dtype) for shp, s in zip(full, shards)]
        + [jax.ShapeDtypeStruct((8, LANES), F32)],
        input_output_aliases={i: 2 + i for i in range(2 * n)},
        compiler_params=pltpu.CompilerParams(has_side_effects=SPLIT_COPY),
    )(*[pltpu.with_memory_space_constraint(s, pltpu.HBM) for s in shards],
      *[pltpu.with_memory_space_constraint(lax.empty(shp, s.dtype), pltpu.HBM) for shp, s in zip(full, shards)],
      *after)
    return res[0], res[1], list(res[2:2 + n]), list(res[2 + n:2 + 2 * n]), res[-1]


def _gather_wait(name, send_sem, recv_sem, shards, outs, after):
    n = len(shards)

    def body(*refs):
        srcs, out_refs = refs[:n], refs[n:2 * n]
        send_ref, recv_ref = refs[2 * n], refs[2 * n + 1]
        for i in range(n):
            for j in range(3):
                copy = _gather_copy(srcs, out_refs, send_ref, recv_ref, i, j)
                copy.wait_send()
                copy.wait_recv()

    res = pl.pallas_call(
        body,
        name=name,
        in_specs=[HBM] * (2 * n) + [SEM, SEM, ANY],
        out_specs=[HBM] * (2 * n),
        out_shape=[pltpu.HBM(s.shape, s.dtype) for s in shards] + [pltpu.HBM(o.shape, o.dtype) for o in outs],
        input_output_aliases={i: i for i in range(2 * n)},
        compiler_params=pltpu.CompilerParams(has_side_effects=SPLIT_COPY),
    )(*shards, *outs, send_sem, recv_sem, after)
    return list(res[:n]), list(res[n:])


def _gather_join(name, shards, outs):
    n = len(shards)

    def body(*refs):
        srcs, ins, outs_ = refs[:n], refs[n:2 * n], refs[2 * n:3 * n]
        own_send, own_recv, half_send, half_recv = refs[3 * n:]
        x, y, c = _position()
        chips = _other_chips(x, y)
        copies = []
        for i in range(n):
            copies.append(pltpu.make_async_remote_copy(
                src_ref=srcs[i], dst_ref=outs_[i].at[2 * x + y], send_sem=own_send.at[i], recv_sem=own_recv.at[i],
                device_id=(x, y, 1 - c), device_id_type=MESH))
            half = srcs[i].shape[0] // 2
            rows = pl.ds(pl.multiple_of(c * half, 16), half)
            for j in range(3):
                slot = 2 * chips[j][0] + chips[j][1]
                copies.append(pltpu.make_async_remote_copy(
                    src_ref=ins[i].at[slot, rows, :], dst_ref=outs_[i].at[slot, rows, :],
                    send_sem=half_send.at[3 * i + j], recv_sem=half_recv.at[3 * i + j],
                    device_id=(x, y, 1 - c), device_id_type=MESH))
        for cp in copies:
            cp.start()
        for cp in copies:
            cp.wait()

    return pl.pallas_call(
        body,
        name=name,
        in_specs=[ANY] * (2 * n),
        out_specs=[ANY] * n,
        out_shape=[jax.ShapeDtypeStruct(o.shape, o.dtype) for o in outs],
        input_output_aliases={n + i: i for i in range(n)},
        scratch_shapes=[pltpu.SemaphoreType.DMA((n,))] * 2 + [pltpu.SemaphoreType.DMA((3 * n,))] * 2,
    )(*shards, *outs)


def _pair_exchange(name, grads):
    n = len(grads)

    def body(*refs):
        ins, theirs = refs[:n], refs[n:2 * n]
        send_sem, recv_sem = refs[2 * n:]
        x, y, c = _position()
        copies = []
        for i in range(n):
            half = ins[i].shape[1] // 2
            give = pl.ds(pl.multiple_of((1 - c) * half, 8), half)
            swap = pltpu.make_async_remote_copy(
                src_ref=ins[i].at[:, give, :], dst_ref=theirs[i], send_sem=send_sem.at[i], recv_sem=recv_sem.at[i],
                device_id=(x, y, 1 - c), device_id_type=MESH)
            swap.start()
            copies.append(swap)
        for swap in copies:
            swap.wait()

    return pl.pallas_call(
        body,
        name=name,
        in_specs=[ANY] * n,
        out_specs=[ANY] * n,
        out_shape=[jax.ShapeDtypeStruct((g.shape[0], g.shape[1] // 2, g.shape[2]), g.dtype) for g in grads],
        scratch_shapes=[pltpu.SemaphoreType.DMA((n,))] * 2,
    )(*grads)


def _scatter_copy(srcs, lands, send_sem, recv_sem, i, j):
    x, y, c = _position()
    chips = _other_chips(x, y)
    return pltpu.make_async_remote_copy(
        src_ref=srcs[i].at[2 * chips[j][0] + chips[j][1]], dst_ref=lands[i].at[j],
        send_sem=send_sem.at[3 * i + j], recv_sem=recv_sem.at[3 * i + j],
        device_id=(chips[j][0], chips[j][1], c), device_id_type=MESH)


def _scatter_start(name, sums):
    n = len(sums)

    def body(*refs):
        srcs, lands = refs[:n], refs[n:2 * n]
        send_sem, recv_sem = refs[2 * n], refs[2 * n + 1]
        token = refs[-1]
        for i in range(n):
            for j in range(3):
                _scatter_copy(srcs, lands, send_sem, recv_sem, i, j).start()
        token[...] = jnp.zeros_like(token)

    land_shapes = [(3,) + s.shape[1:] for s in sums]
    res = pl.pallas_call(
        body,
        name=name,
        in_specs=[HBM] * (2 * n),
        out_specs=[SEM, SEM] + [HBM] * (2 * n) + [pl.BlockSpec(memory_space=pltpu.VMEM)],
        out_shape=[pltpu.SemaphoreType.DMA((3 * n,)), pltpu.SemaphoreType.DMA((3 * n,))]
        + [pltpu.HBM(s.shape, s.dtype) for s in sums]
        + [pltpu.HBM(shp, s.dtype) for shp, s in zip(land_shapes, sums)]
        + [jax.ShapeDtypeStruct((8, LANES), F32)],
        input_output_aliases={i: 2 + i for i in range(2 * n)},
        compiler_params=pltpu.CompilerParams(has_side_effects=SPLIT_COPY),
    )(*[pltpu.with_memory_space_constraint(s, pltpu.HBM) for s in sums],
      *[pltpu.with_memory_space_constraint(lax.empty(shp, s.dtype), pltpu.HBM) for shp, s in zip(land_shapes, sums)])
    return res[0], res[1], list(res[2:2 + n]), list(res[2 + n:2 + 2 * n]), res[-1]


def _scatter_wait(name, send_sem, recv_sem, sums, lands, after):
    n = len(sums)

    def body(*refs):
        srcs, land_refs = refs[:n], refs[n:2 * n]
        send_ref, recv_ref = refs[2 * n], refs[2 * n + 1]
        for i in range(n):
            for j in range(3):
                copy = _scatter_copy(srcs, land_refs, send_ref, recv_ref, i, j)
                copy.wait_send()
                copy.wait_recv()

    res = pl.pallas_call(
        body,
        name=name,
        in_specs=[HBM] * (2 * n) + [SEM, SEM, ANY],
        out_specs=[HBM] * (2 * n),
        out_shape=[pltpu.HBM(s.shape, s.dtype) for s in sums] + [pltpu.HBM(l.shape, l.dtype) for l in lands],
        input_output_aliases={i: i for i in range(2 * n)},
        compiler_params=pltpu.CompilerParams(has_side_effects=SPLIT_COPY),
    )(*sums, *lands, send_sem, recv_sem, after)
    return list(res[:n]), list(res[n:])


def _pair_join(name, halves, small=None):
    n = len(halves)
    if small is None:
        def body_plain(*refs):
            ins, outs = refs[:n], refs[n:2 * n]
            send_sem, recv_sem = refs[2 * n:]
            x, y, c = _position()
            swaps = [pltpu.make_async_remote_copy(
                src_ref=ins[i], dst_ref=outs[i], send_sem=send_sem.at[i], recv_sem=recv_sem.at[i],
                device_id=(x, y, 1 - c), device_id_type=MESH) for i in range(n)]
            for swap in swaps:
                swap.start()
            for swap in swaps:
                swap.wait()

        return pl.pallas_call(
            body_plain,
            name=name,
            in_specs=[ANY] * n,
            out_specs=[ANY] * n,
            out_shape=[jax.ShapeDtypeStruct(h.shape, h.dtype) for h in halves],
            scratch_shapes=[pltpu.SemaphoreType.DMA((n,))] * 2,
        )(*halves)

    def body(*refs):
        ins, small_ref = refs[:n], refs[n]
        outs, all_ref = refs[n + 1:2 * n + 1], refs[2 * n + 1]
        send_sem, recv_sem, sm_send, sm_recv, sm_local = refs[2 * n + 2:]
        x, y, c = _position()
        swaps = []
        for i in range(n):
            swap = pltpu.make_async_remote_copy(
                src_ref=ins[i], dst_ref=outs[i], send_sem=send_sem.at[i], recv_sem=recv_sem.at[i],
                device_id=(x, y, 1 - c), device_id_type=MESH)
            swap.start()
            swaps.append(swap)
        me = 4 * x + 2 * y + c
        sm_own = pltpu.make_async_copy(small_ref, all_ref.at[me], sm_local)
        sm_own.start()
        pushes, arrivals = [], []
        for mask in range(1, N_DEV):
            px, py, pc = x ^ (mask >> 2), y ^ ((mask >> 1) & 1), c ^ (mask & 1)
            pushes.append(pltpu.make_async_remote_copy(
                src_ref=small_ref, dst_ref=all_ref.at[me], send_sem=sm_send.at[mask - 1], recv_sem=sm_recv.at[mask - 1],
                device_id=(px, py, pc), device_id_type=MESH))
            arrivals.append(pltpu.make_async_remote_copy(
                src_ref=small_ref, dst_ref=all_ref.at[4 * px + 2 * py + pc], send_sem=sm_send.at[mask - 1],
                recv_sem=sm_recv.at[mask - 1], device_id=(px, py, pc), device_id_type=MESH))
        for cp in pushes:
            cp.start()
        for swap in swaps:
            swap.wait()
        for cp in arrivals:
            cp.wait_recv()
        for cp in pushes:
            cp.wait_send()
        sm_own.wait()

    res = pl.pallas_call(
        body,
        name=name,
        in_specs=[ANY] * (n + 1),
        out_specs=[ANY] * (n + 1),
        out_shape=[jax.ShapeDtypeStruct(h.shape, h.dtype) for h in halves]
        + [jax.ShapeDtypeStruct((N_DEV,) + small.shape, small.dtype)],
        scratch_shapes=[pltpu.SemaphoreType.DMA((n,))] * 2 + [pltpu.SemaphoreType.DMA((N_DEV - 1,))] * 2
        + [pltpu.SemaphoreType.DMA(())],
    )(*halves, small)
    return res[:n], res[n]


def _lower_bound(lbp):
    return jax.nn.softmax(lbp, axis=0)[0:1]


def _local_step(x, target, g1, gm, g2, gq, gk, go, rel_bias, lbp, first_weights, mid_weights, last_weights, on_grads):
    b, s, d = x.shape
    t = b * s
    x0 = x.reshape(t, d)
    tgt = target.reshape(t, d)
    gq_t = jnp.tile(gq, (1, ATTN_HEADS))
    gk_t = jnp.tile(gk, (1, ATTN_HEADS))
    lb = _lower_bound(lbp)
    bias = _rel_bias_table("rel_bias_table", rel_bias)

    h1 = _rmsnorm_fwd("norm1", x0, g1)
    wg1, wu1, deps1 = first_weights(h1)
    a1, b1, z1 = _ffn_up("ffn1_up", h1, wg1, wu1, deps1)
    wd1, w_in, w_out = mid_weights(z1)
    ns = w_in.shape[0]
    x1 = _ffn_down("ffn1_down", z1, wd1, x0)
    h2 = _rmsnorm_fwd("norm_mix", x1, gm)
    proj = _in_proj("in_proj", h2, w_in)
    proj3 = proj.reshape(b, s, proj.shape[1])
    table = _band_table(bias)
    qn, kn, vb = _qk_prep("qk_prep", proj3, gq_t, gk_t)
    attn = _attn_fwd("attn_fwd", qn, kn, vb, table).reshape(t, ATTN_W)
    ro, oraw, states = _hgrn_fwd("hgrn_fwd", proj, lb, go, b, s)
    mix = jnp.concatenate([attn, ro], axis=1)
    x2 = _out_proj("out_proj", mix, w_out, x1)
    h3 = _rmsnorm_fwd("norm2", x2, g2)
    wg2, wu2, wd2 = last_weights(h3)
    a2, b2, z2 = _ffn_up("ffn2_up", h3, wg2, wu2)
    dy, dyh, sq = _ffn_down_loss("ffn2_down_loss", z2, wd2, x2, tgt)
    loss = 0.5 * jnp.sum(sq) / d

    da2, db2 = _ffn_bwd_act("ffn2_bwd_act", dyh, wd2, a2, b2)
    dwd2 = _grad_w_shardrows("ffn2_dwd", z2, dyh)
    dwg2 = _grad_w_shardrows("ffn2_dwg", da2, h3)
    dwu2 = _grad_w_shardrows("ffn2_dwu", db2, h3)
    sent2 = on_grads("ffn2", {"ffn2_w_gate": dwg2, "ffn2_w_up": dwu2, "ffn2_w_down": dwd2})
    dx2, dx2b, dg2 = _ffn_bwd_in("ffn2_bwd_in", da2, db2, wg2, wu2, x2, g2, dy, 1.0)

    dwout = _grad_w_out("dw_out", mix, dx2b)
    dmix = _out_proj_bwd("out_proj_bwd", dx2b, w_out, sent2)
    dqn, dkn, dvn, dbe, dbo = _attn_bwd("attn_bwd", qn, kn, vb, table, dmix.reshape(b, s, dmix.shape[1]))
    dbias = dbe[:, :, :BAND] + dbo[:, :, CHUNK:]
    dpq, dpk, dpv, dgq, dgk = _qk_prep_bwd("qk_prep_bwd", proj3, dqn, dkn, dvn, gq_t, gk_t)
    dpq, dpk, dpv = (a.reshape(t, ATTN_W) for a in (dpq, dpk, dpv))
    dhq, dhf, dhi, dhg, dlb, dgo = _hgrn_bwd("hgrn_bwd", proj, lb, go, oraw, states, dmix, b, s)
    dproj = jnp.concatenate([dpq, dpk, dpv, dhq, dhf, dhi, dhg], axis=1)
    dwin = _grad_w_in("dw_in", h2, dproj, ns)
    sent_mix = on_grads("mix", {"w_in": dwin, "w_out": dwout.reshape(ns, dwout.shape[0] // ns, d)})
    dx1, dx1h, dgm = _in_proj_bwd("in_proj_bwd", dproj, w_in, x1, gm, dx2, 0.5)

    da1, db1 = _ffn_bwd_act("ffn1_bwd_act", dx1h, wd1, a1, b1, sent_mix)
    dwd1 = _grad_w_shardrows("ffn1_dwd", z1, dx1h)
    dwg1 = _grad_w_shardrows("ffn1_dwg", da1, h1)
    dwu1 = _grad_w_shardrows("ffn1_dwu", db1, h1)
    on_grads("ffn1", {"ffn1_w_gate": dwg1, "ffn1_w_up": dwu1, "ffn1_w_down": dwd1})
    dx0, dg1 = _ffn_bwd_in("ffn1_bwd_in", da1, db1, wg1, wu1, x0, g1, dx1, None)

    nt = dg1.shape[0]
    sg = _small_grads(
        "small_grads", dg1.reshape(nt, d), dgm.reshape(nt, d), dg2.reshape(nt, d),
        dgq.reshape(-1, ATTN_W), dgk.reshape(-1, ATTN_W), dbias.transpose(1, 0, 2),
        dlb.reshape(b, HGRN_W), dgo.reshape(b, HGRN_W), lbp)
    g1g, gmg, g2g, gqg, gkg, rbg, lbg, gog = sg
    small = _pack_small(g1g, gmg, g2g, lbg, rbg[:, :N_REL], gqg, gkg, gog)
    return loss, dx0.reshape(b, s, d), small


def _pack_small(g1, gm, g2, lbp, rel_bias, gq, gk, go):
    flat = [g1.reshape(-1), gm.reshape(-1), g2.reshape(-1), lbp.reshape(-1), rel_bias.reshape(-1)]
    n_bias = 3 * SMALL_COLS - rel_bias.size
    heads = [gq.reshape(-1), gk.reshape(-1), go.reshape(-1)]
    n_tail = SMALL_COLS - sum(h.size for h in heads)
    return jnp.concatenate(flat + [jnp.zeros((n_bias,), F32)] + heads + [jnp.zeros((n_tail,), F32)]).reshape(
        SMALL_ROWS, SMALL_COLS)


def _unpack_small(p, d):
    flat = p.reshape(-1)
    o = 3 * d
    g1, gm, g2 = p[0:1], p[1:2], p[2:3]
    lbp = flat[o:o + 2 * HGRN_W].reshape(2, HGRN_W)
    o = 4 * SMALL_COLS
    rel = flat[o:o + ATTN_HEADS * N_REL].reshape(1, ATTN_HEADS, N_REL)
    o = 7 * SMALL_COLS
    gq = flat[o:o + ATTN_DH].reshape(1, ATTN_DH)
    gk = flat[o + ATTN_DH:o + 2 * ATTN_DH].reshape(1, ATTN_DH)
    go = flat[o + 2 * ATTN_DH:o + 2 * ATTN_DH + HGRN_DH].reshape(1, HGRN_DH)
    return g1, gm, g2, gq, gk, rel, lbp, go


def kernel(x, ffn1_norm_g, ffn1_w_gate, ffn1_w_up, ffn1_w_down, mix_norm_g, w_in, attn_q_norm_g, attn_k_norm_g, attn_rel_bias, hgrn_lower_bounds, hgrn_out_norm_g, w_out, ffn2_norm_g, ffn2_w_gate, ffn2_w_up, ffn2_w_down, loss_target, m_ffn1_norm_g, m_ffn1_w_gate, m_ffn1_w_up, m_ffn1_w_down, m_mix_norm_g, m_w_in, m_attn_q_norm_g, m_attn_k_norm_g, m_attn_rel_bias, m_hgrn_lower_bounds, m_hgrn_out_norm_g, m_w_out, m_ffn2_norm_g, m_ffn2_w_gate, m_ffn2_w_up, m_ffn2_w_down, v_ffn1_norm_g, v_ffn1_w_gate, v_ffn1_w_up, v_ffn1_w_down, v_mix_norm_g, v_w_in, v_attn_q_norm_g, v_attn_k_norm_g, v_attn_rel_bias, v_hgrn_lower_bounds, v_hgrn_out_norm_g, v_w_out, v_ffn2_norm_g, v_ffn2_w_gate, v_ffn2_w_up, v_ffn2_w_down):
    d = x.shape[-1]
    big_w = [ffn1_w_gate, ffn1_w_up, ffn1_w_down, w_in, w_out, ffn2_w_gate, ffn2_w_up, ffn2_w_down]
    big_m = [m_ffn1_w_gate, m_ffn1_w_up, m_ffn1_w_down, m_w_in, m_w_out, m_ffn2_w_gate, m_ffn2_w_up, m_ffn2_w_down]
    big_v = [v_ffn1_w_gate, v_ffn1_w_up, v_ffn1_w_down, v_w_in, v_w_out, v_ffn2_w_gate, v_ffn2_w_up, v_ffn2_w_down]
    big_names = ["ffn1_w_gate", "ffn1_w_up", "ffn1_w_down", "w_in", "w_out", "ffn2_w_gate", "ffn2_w_up", "ffn2_w_down"]
    flipped = {nm for nm in big_names if nm.endswith("gate") or nm.endswith("up")}
    flip = lambda nm, a: jnp.swapaxes(a, 1, 2) if nm in flipped else a
    big_w, big_m, big_v = ([flip(nm, a) for nm, a in zip(big_names, arrs)] for arrs in (big_w, big_m, big_v))

    shards = [w[0].astype(BF16) for w in big_w]
    start_a = _gather_start("gather_start_up1", shards[:2], ())
    start_b = _gather_start("gather_start_mid", shards[2:5], (start_a[4],))
    start_c = _gather_start("gather_start_ffn2", shards[5:], (start_b[4],))

    def gathered(tag, started, after):
        send_sem, recv_sem, srcs, outs, _ = started
        srcs, outs = _gather_wait("gather_wait_" + tag, send_sem, recv_sem, srcs, outs, after)
        return _gather_join("gather_join_" + tag, srcs, outs)

    def first_weights(after):
        return (*gathered("up1", start_a, after), (start_c[4],))

    def mid_weights(after):
        wd1, win_f, wout_f = gathered("mid", start_b, after)
        return wd1, win_f, wout_f.reshape(wout_f.shape[0] * wout_f.shape[1], d)

    def last_weights(after):
        return gathered("ffn2", start_c, after)

    core = lax.axis_index("c").astype(jnp.int32).reshape(1)
    chip = (2 * lax.axis_index("x") + lax.axis_index("y")).astype(jnp.int32).reshape(1)
    started = {}

    def on_grads(tag, grads):
        names = list(grads)
        theirs = _pair_exchange("pair_exchange_" + tag, [grads[nm] for nm in names])
        sums = [_pair_sum("pair_sum_" + nm, grads[nm], th, core) for nm, th in zip(names, theirs)]
        started[tag] = (names, _scatter_start("scatter_start_" + tag, sums))
        return (started[tag][1][4],)

    loss, grad_x, small_g = _local_step(
        x, loss_target, ffn1_norm_g, mix_norm_g, ffn2_norm_g, attn_q_norm_g, attn_k_norm_g, hgrn_out_norm_g,
        attn_rel_bias[0], hgrn_lower_bounds, first_weights, mid_weights, last_weights, on_grads)
    loss = lax.psum(loss, ("x", "y", "c"))

    def finish(tag, after):
        names, (send_sem, recv_sem, sums, lands, _) = started[tag]
        sums, lands = _scatter_wait("scatter_wait_" + tag, send_sem, recv_sem, sums, lands, after)
        return names, [_chip_sum("chip_sum_" + nm, sm, ld, chip) for nm, sm, ld in zip(names, sums, lands)]

    by_name = {nm: (w, m, v) for nm, w, m, v in zip(big_names, big_w, big_m, big_v)}
    updated = {}

    def update(names, halves, other_halves):
        for nm, mine, theirs in zip(names, halves, other_halves):
            w, m, v = by_name[nm]
            updated[nm] = _adamw("adamw_" + nm, w, mine, theirs, m, v, core)

    last_token = started["ffn1"][1][4]
    names_a, halves_a = finish("ffn2", last_token)
    names_m, halves_m = finish("mix", last_token)
    names_a, halves_a = names_a + names_m, halves_a + halves_m
    update(names_a, halves_a, _pair_join("pair_join_early", halves_a))
    names_b, halves_b = finish("ffn1", updated["w_out"][1])
    others_b, small_all = _pair_join("pair_join_last", halves_b, small_g)
    update(names_b, halves_b, others_b)
    big_out = [updated[nm] for nm in big_names]

    pack = lambda g1, gm, g2, gq, gk, rel, lbp, go: _pack_small(g1, gm, g2, lbp, rel[0], gq, gk, go)
    small_w = pack(ffn1_norm_g, mix_norm_g, ffn2_norm_g, attn_q_norm_g, attn_k_norm_g, attn_rel_bias, hgrn_lower_bounds, hgrn_out_norm_g)
    small_m = pack(m_ffn1_norm_g, m_mix_norm_g, m_ffn2_norm_g, m_attn_q_norm_g, m_attn_k_norm_g, m_attn_rel_bias, m_hgrn_lower_bounds, m_hgrn_out_norm_g)
    small_v = pack(v_ffn1_norm_g, v_mix_norm_g, v_ffn2_norm_g, v_attn_q_norm_g, v_attn_k_norm_g, v_attn_rel_bias, v_hgrn_lower_bounds, v_hgrn_out_norm_g)
    small_out = [_unpack_small(p, d) for p in _adamw_small("adamw_small", small_w, small_all, small_m, small_v)]

    def assemble(kind):
        bg = [flip(nm, o[kind]) for nm, o in zip(big_names, big_out)]
        g1, gm, g2, gq, gk, rel, lbp, go = small_out[kind]
        return [g1, bg[0], bg[1], bg[2], gm, bg[3], gq, gk, rel, lbp, go, bg[4], g2, bg[5], bg[6], bg[7]]

    return (loss, grad_x, *assemble(0), *assemble(1), *assemble(2), *assemble(3))
```

```python
import functools

import jax
import jax.numpy as jnp
from jax import lax
from jax.experimental import pallas as pl
from jax.experimental.pallas import tpu as pltpu

F32 = jnp.float32
BF16 = jnp.bfloat16
MESH = pl.DeviceIdType.MESH

N_CHIPS = 4
N_DEV = 8
CHUNK = 64
ATTN_HEADS = 8
ATTN_DH = 64
ATTN_W = ATTN_HEADS * ATTN_DH
HGRN_HEADS = 4
HGRN_DH = 128
HGRN_W = HGRN_HEADS * HGRN_DH
LEFT_CHUNKS = 8
BAND = (LEFT_CHUNKS + 1) * CHUNK
KPAD = LEFT_CHUNKS * CHUNK
REL_CLIP = 128
N_REL = 2 * REL_CLIP + 1
N_REL_PAD = 384
RMS_EPS = 1e-6
LANES = 128
SMALL_ROWS = 8
SMALL_COLS = 1024

ADAM_LR = 0.001
ADAM_B1 = 0.9
ADAM_B2 = 0.999
ADAM_EPS = 1e-08
ADAM_WD = 0.01
ADAM_STEP = 10

NN = (((1,), (0,)), ((), ()))
NT = (((1,), (1,)), ((), ()))
TN = (((0,), (0,)), ((), ()))

VMEM_LIMIT = 48 * 1024 * 1024


def _sigmoid(x):
    return 1.0 / (1.0 + jnp.exp(-x))


def _silu(x):
    return x * _sigmoid(x)


def _dot(a, b, dims=NN):
    return lax.dot_general(a, b, dims, preferred_element_type=F32)


def _split3(x):
    hi = x.astype(BF16)
    r1 = x - hi.astype(F32)
    mid = r1.astype(BF16)
    lo = (r1 - mid.astype(F32)).astype(BF16)
    return hi, mid, lo


def _dot_exact_rhs(x, mat, dims=NN):
    hi, mid, lo = _split3(x)
    return _dot(hi, mat, dims) + _dot(mid, mat, dims) + _dot(lo, mat, dims)


def _dot_exact_lhs(mat, x, dims=NN):
    hi, mid, lo = _split3(x)
    return _dot(mat, hi, dims) + _dot(mat, mid, dims) + _dot(mat, lo, dims)


def _params(*sem):
    return pltpu.CompilerParams(dimension_semantics=sem, vmem_limit_bytes=VMEM_LIMIT)


def _mm(name, ins, terms, n_acc, grid, acc_shape, outs, epilogue, extras=(), deps=()):
    nk = grid[2]
    ni, ne, nd, no = len(ins), len(extras), len(deps), len(outs)

    def body(*refs):
        in_refs = refs[:ni]
        ex_refs = refs[ni:ni + ne]
        out_refs = refs[ni + ne + nd:ni + ne + nd + no]
        acc_refs = refs[ni + ne + nd + no:]
        parts = [None] * n_acc
        for ai, li, ri, dims in terms:
            d = _dot(in_refs[li][...], in_refs[ri][...], dims)
            parts[ai] = d if parts[ai] is None else parts[ai] + d

        def finish(accs):
            res = epilogue(accs, [e[...] for e in ex_refs])
            for o, r in zip(out_refs, res):
                o[...] = r.astype(o.dtype)

        if nk == 1:
            finish(parts)
        else:
            k = pl.program_id(2)

            @pl.when(k == 0)
            def _():
                for a, p in zip(acc_refs, parts):
                    a[...] = p

            @pl.when(k > 0)
            def _():
                for a, p in zip(acc_refs, parts):
                    a[...] += p

            @pl.when(k == nk - 1)
            def _():
                finish([a[...] for a in acc_refs])

    scratch = [] if nk == 1 else [pltpu.VMEM(acc_shape, F32) for _ in range(n_acc)]
    res = pl.pallas_call(
        body,
        name=name,
        grid=grid,
        in_specs=[s for _, s in ins] + [s for _, s in extras] + [pl.BlockSpec(memory_space=pl.ANY)] * nd,
        out_specs=[s for _, s in outs],
        out_shape=[o for o, _ in outs],
        scratch_shapes=scratch,
        compiler_params=_params("parallel", "parallel", "arbitrary"),
    )(*[a for a, _ in ins], *[a for a, _ in extras], *deps)
    return res


def _mm_rows(name, lhs, weights, dims, t, outs, epilogue, extras=(), deps=()):
    tm = _row_tile(t)
    nl, ne, nd, no = len(lhs), len(extras), len(deps), len(outs)
    ns = weights[0].shape[0]

    def body(*refs):
        lhs_refs = refs[:nl]
        w_hbm = refs[nl:2 * nl]
        ex_refs = refs[2 * nl:2 * nl + ne]
        out_refs = refs[2 * nl + ne + nd:2 * nl + ne + nd + no]
        w_vmem = refs[2 * nl + ne + nd + no:3 * nl + ne + nd + no]
        sem = refs[-1]

        @pl.when(pl.program_id(0) == 0)
        def _():
            copies = [pltpu.make_async_copy(w_hbm[p], w_vmem[p], sem.at[p]) for p in range(nl)]
            for cp in copies:
                cp.start()
            for cp in copies:
                cp.wait()

        acc = None
        for p in range(nl):
            pick = lhs[p][2]
            for j in range(ns):
                part = _dot(pick(lhs_refs[p], j), w_vmem[p][j], dims)
                acc = part if acc is None else acc + part
        res = epilogue([acc], [e[...] for e in ex_refs])
        for o, r in zip(out_refs, res):
            o[...] = r.astype(o.dtype)

    return pl.pallas_call(
        body,
        name=name,
        grid=(t // tm,),
        in_specs=[s for _, s, _ in lhs] + [pl.BlockSpec(memory_space=pl.ANY)] * nl + [s for _, s in extras]
        + [pl.BlockSpec(memory_space=pl.ANY)] * nd,
        out_specs=[s for _, s in outs],
        out_shape=[o for o, _ in outs],
        scratch_shapes=[pltpu.VMEM(w.shape, w.dtype) for w in weights] + [pltpu.SemaphoreType.DMA((nl,))],
        compiler_params=_params("arbitrary"),
    )(*[a for a, _, _ in lhs], *weights, *[a for a, _ in extras], *deps)


def _row_tile(t):
    return 512 if t % 512 == 0 else t


def _k_tile(t):
    return t if t <= 4096 else 1024


def _rmsnorm_fwd(name, x, g):
    t, d = x.shape
    tm = _row_tile(t)

    def body(x_ref, g_ref, h_ref):
        xv = x_ref[...]
        ms = jnp.mean(xv * xv, axis=-1, keepdims=True)
        h_ref[...] = (xv * lax.rsqrt(ms + RMS_EPS) * g_ref[...]).astype(BF16)

    return pl.pallas_call(
        body,
        name=name,
        grid=(t // tm,),
        in_specs=[pl.BlockSpec((tm, d), lambda i: (i, 0)), pl.BlockSpec((1, d), lambda i: (0, 0))],
        out_specs=pl.BlockSpec((tm, d), lambda i: (i, 0)),
        out_shape=jax.ShapeDtypeStruct((t, d), BF16),
        compiler_params=_params("parallel"),
    )(x, g)


def _norm_bwd_epilogue(copy_scale):
    def epilogue(accs, ex):
        dh = accs[0]
        xv, g, dres = ex
        ms = jnp.mean(xv * xv, axis=-1, keepdims=True)
        rstd = lax.rsqrt(ms + RMS_EPS)
        xhat = xv * rstd
        dxhat = dh * g
        dx = rstd * (dxhat - xhat * jnp.mean(dxhat * xhat, axis=-1, keepdims=True))
        out = dres + dx
        dg = jnp.sum(dh * xhat, axis=0, keepdims=True)
        if copy_scale is None:
            return out, dg
        return out, out * copy_scale, dg

    return epilogue


def _ffn_up(name, h, wg, wu, deps=()):
    t, d = h.shape
    ns, f, _ = wg.shape
    tm = _row_tile(t)

    def epilogue(accs, ex):
        a, b = accs
        sg = _sigmoid(a)
        act = a * sg
        return act, b * (sg * (1.0 + a * (1.0 - sg))), act * b

    w_spec = pl.BlockSpec((None, f, d), lambda j, i, k: (j, 0, 0))
    o_spec = pl.BlockSpec((None, tm, f), lambda j, i, k: (j, i, 0))
    o_shape = jax.ShapeDtypeStruct((ns, t, f), BF16)
    return _mm(
        name,
        ins=[(h, pl.BlockSpec((tm, d), lambda j, i, k: (i, 0))), (wg, w_spec), (wu, w_spec)],
        terms=[(0, 0, 1, NT), (1, 0, 2, NT)],
        n_acc=2,
        grid=(ns, t // tm, 1),
        acc_shape=(tm, f),
        outs=[(o_shape, o_spec)] * 3,
        epilogue=epilogue,
        deps=deps,
    )


def _shard_rows(arr, tm):
    ns, _, f = arr.shape
    return arr, pl.BlockSpec((ns, tm, f), lambda i: (0, i, 0)), lambda ref, j: ref[j]


def _ffn_down(name, z, wd, x):
    _, t, _ = z.shape
    d = wd.shape[2]
    tm = _row_tile(t)
    row = pl.BlockSpec((tm, d), lambda i: (i, 0))
    return _mm_rows(
        name, [_shard_rows(z, tm)], [wd], NN, t,
        outs=[(jax.ShapeDtypeStruct((t, d), F32), row)],
        epilogue=lambda accs, ex: (ex[0] + 0.5 * accs[0],),
        extras=[(x, row)],
    )[0]


def _ffn_down_loss(name, z, wd, x, target):
    _, t, _ = z.shape
    d = wd.shape[2]
    tm = _row_tile(t)
    nt = t // tm
    row = pl.BlockSpec((tm, d), lambda i: (i, 0))

    def epilogue(accs, ex):
        e = ex[0] + 0.5 * accs[0] - ex[1]
        dy = e * (1.0 / d)
        return dy, 0.5 * dy, jnp.sum(e * e, axis=0, keepdims=True)

    return _mm_rows(
        name, [_shard_rows(z, tm)], [wd], NN, t,
        outs=[(jax.ShapeDtypeStruct((t, d), F32), row), (jax.ShapeDtypeStruct((t, d), BF16), row),
              (jax.ShapeDtypeStruct((nt, 1, d), F32), pl.BlockSpec((None, 1, d), lambda i: (i, 0, 0)))],
        epilogue=epilogue,
        extras=[(x, row), (target, row)],
    )


def _ffn_bwd_act(name, dout, wd, act_a, dact_b, deps=()):
    t, d = dout.shape
    ns, f, _ = wd.shape
    tm = _row_tile(t)

    def epilogue(accs, ex):
        dz = accs[0]
        return dz * ex[1].astype(F32), dz * ex[0].astype(F32)

    act = pl.BlockSpec((None, tm, f), lambda j, i, k: (j, i, 0))
    o_shape = jax.ShapeDtypeStruct((ns, t, f), BF16)
    return _mm(
        name,
        ins=[(dout, pl.BlockSpec((tm, d), lambda j, i, k: (i, 0))),
             (wd, pl.BlockSpec((None, f, d), lambda j, i, k: (j, 0, 0)))],
        terms=[(0, 0, 1, NT)],
        n_acc=1,
        grid=(ns, t // tm, 1),
        acc_shape=(tm, f),
        outs=[(o_shape, act)] * 2,
        epilogue=epilogue,
        extras=[(act_a, act), (dact_b, act)],
        deps=deps,
    )


def _grad_w_shardrows(name, z, dout):
    ns, t, f = z.shape
    d = dout.shape[1]
    tk = _k_tile(t)
    return _mm(
        name,
        ins=[(z, pl.BlockSpec((None, tk, f), lambda j, n, k: (j, k, 0))),
             (dout, pl.BlockSpec((tk, d), lambda j, n, k: (k, 0)))],
        terms=[(0, 0, 1, TN)],
        n_acc=1,
        grid=(ns, 1, t // tk),
        acc_shape=(f, d),
        outs=[(jax.ShapeDtypeStruct((ns, f, d), F32), pl.BlockSpec((None, f, d), lambda j, n, k: (j, 0, 0)))],
        epilogue=lambda accs, ex: (accs[0],),
    )[0]


def _norm_bwd_outs(t, d, tm, copy_scale):
    row = pl.BlockSpec((tm, d), lambda i: (i, 0))
    outs = [(jax.ShapeDtypeStruct((t, d), F32), row)]
    if copy_scale is not None:
        outs.append((jax.ShapeDtypeStruct((t, d), BF16), row))
    outs.append((jax.ShapeDtypeStruct((t // tm, 1, d), F32), pl.BlockSpec((None, 1, d), lambda i: (i, 0, 0))))
    return row, outs


def _ffn_bwd_in(name, da, db, wg, wu, x, g, dres, copy_scale, deps=()):
    _, t, _ = da.shape
    d = wg.shape[2]
    tm = _row_tile(t)
    row, outs = _norm_bwd_outs(t, d, tm, copy_scale)
    return _mm_rows(
        name, [_shard_rows(da, tm), _shard_rows(db, tm)], [wg, wu], NN, t,
        outs=outs,
        epilogue=_norm_bwd_epilogue(copy_scale),
        extras=[(x, row), (g, pl.BlockSpec((1, d), lambda i: (0, 0))), (dres, row)],
        deps=deps,
    )


def _in_proj(name, h, w_in):
    t, d = h.shape
    ns, _, pj = w_in.shape
    tm = _row_tile(t)
    return _mm(
        name,
        ins=[(h, pl.BlockSpec((tm, d), lambda j, i, k: (i, 0))),
             (w_in, pl.BlockSpec((None, d, pj), lambda j, i, k: (j, 0, 0)))],
        terms=[(0, 0, 1, NN)],
        n_acc=1,
        grid=(ns, t // tm, 1),
        acc_shape=(tm, pj),
        outs=[(jax.ShapeDtypeStruct((t, ns * pj), F32), pl.BlockSpec((tm, pj), lambda j, i, k: (i, j)))],
        epilogue=lambda accs, ex: (accs[0],),
    )[0]


def _in_proj_bwd(name, dp, w_in, x, g, dres, copy_scale, deps=()):
    t = dp.shape[0]
    ns, d, pj = w_in.shape
    tm = _row_tile(t)
    row, outs = _norm_bwd_outs(t, d, tm, copy_scale)
    cols = (dp, pl.BlockSpec((tm, ns * pj), lambda i: (i, 0)), lambda ref, j: ref[:, j * pj:(j + 1) * pj])
    return _mm_rows(
        name, [cols], [w_in], NT, t,
        outs=outs,
        epilogue=_norm_bwd_epilogue(copy_scale),
        extras=[(x, row), (g, pl.BlockSpec((1, d), lambda i: (0, 0))), (dres, row)],
        deps=deps,
    )


def _grad_w_in(name, h, dp, ns):
    t, d = h.shape
    pj = dp.shape[1] // ns
    tk = _k_tile(t)
    return _mm(
        name,
        ins=[(h, pl.BlockSpec((tk, d), lambda j, n, k: (k, 0))),
             (dp, pl.BlockSpec((tk, pj), lambda j, n, k: (k, j)))],
        terms=[(0, 0, 1, TN)],
        n_acc=1,
        grid=(ns, 1, t // tk),
        acc_shape=(d, pj),
        outs=[(jax.ShapeDtypeStruct((ns, d, pj), F32), pl.BlockSpec((None, d, pj), lambda j, n, k: (j, 0, 0)))],
        epilogue=lambda accs, ex: (accs[0],),
    )[0]


def _out_proj(name, mix, w_out, x):
    t, dm = mix.shape
    d = w_out.shape[1]
    tm = _row_tile(t)
    row = pl.BlockSpec((tm, d), lambda i, n, k: (i, 0))
    return _mm(
        name,
        ins=[(mix, pl.BlockSpec((tm, dm), lambda i, n, k: (i, 0))),
             (w_out, pl.BlockSpec((dm, d), lambda i, n, k: (0, 0)))],
        terms=[(0, 0, 1, NN)],
        n_acc=1,
        grid=(t // tm, 1, 1),
        acc_shape=(tm, d),
        outs=[(jax.ShapeDtypeStruct((t, d), F32), row)],
        epilogue=lambda accs, ex: (ex[0] + accs[0],),
        extras=[(x, row)],
    )[0]


def _out_proj_bwd(name, dx, w_out, deps=()):
    t, d = dx.shape
    dm = w_out.shape[0]
    tm = _row_tile(t)
    return _mm(
        name,
        ins=[(dx, pl.BlockSpec((tm, d), lambda i, n, k: (i, 0))),
             (w_out, pl.BlockSpec((dm, d), lambda i, n, k: (0, 0)))],
        terms=[(0, 0, 1, NT)],
        n_acc=1,
        grid=(t // tm, 1, 1),
        acc_shape=(tm, dm),
        outs=[(jax.ShapeDtypeStruct((t, dm), F32), pl.BlockSpec((tm, dm), lambda i, n, k: (i, 0)))],
        epilogue=lambda accs, ex: (accs[0],),
        deps=deps,
    )[0]


def _grad_w_out(name, mix, dx):
    t, dm = mix.shape
    d = dx.shape[1]
    tk = _k_tile(t)
    return _mm(
        name,
        ins=[(mix, pl.BlockSpec((tk, dm), lambda a, n, k: (k, 0))),
             (dx, pl.BlockSpec((tk, d), lambda a, n, k: (k, 0)))],
        terms=[(0, 0, 1, TN)],
        n_acc=1,
        grid=(1, 1, t // tk),
        acc_shape=(dm, d),
        outs=[(jax.ShapeDtypeStruct((dm, d), F32), pl.BlockSpec((dm, d), lambda a, n, k: (0, 0)))],
        epilogue=lambda accs, ex: (accs[0],),
    )[0]


def _head_group_matrix():
    r = lax.broadcasted_iota(jnp.int32, (ATTN_W, ATTN_W), 0)
    c = lax.broadcasted_iota(jnp.int32, (ATTN_W, ATTN_W), 1)
    same = jnp.right_shift(r, 6) == jnp.right_shift(c, 6)
    return jnp.where(same, 1.0, 0.0).astype(BF16)


def _qk_prep(name, proj, gq, gk):
    b, s, _ = proj.shape
    tm = KPAD
    nb = s // tm

    def body(q_ref, k_ref, v_ref, gq_ref, gk_ref, qn_ref, kn_ref, vb_ref):
        j = pl.program_id(1)
        bd = _head_group_matrix()

        def norm(xv, g):
            ms = _dot_exact_rhs(xv * xv, bd) * (1.0 / ATTN_DH)
            return xv * lax.rsqrt(ms + RMS_EPS) * g

        @pl.when(j == 0)
        def _():
            kn_ref[...] = jnp.zeros_like(kn_ref)
            vb_ref[...] = jnp.zeros_like(vb_ref)

        @pl.when(j > 0)
        def _():
            qn_ref[...] = norm(q_ref[...], gq_ref[...]).astype(BF16)
            kn_ref[...] = norm(k_ref[...], gk_ref[...]).astype(BF16)
            vb_ref[...] = v_ref[...].astype(BF16)

    src_blk = lambda col: pl.BlockSpec((None, tm, ATTN_W), lambda bi, j: (bi, jnp.maximum(j - 1, 0), col))
    gspec = pl.BlockSpec((1, ATTN_W), lambda bi, j: (0, 0))
    padded = pl.BlockSpec((None, tm, ATTN_W), lambda bi, j: (bi, j, 0))
    return pl.pallas_call(
        body,
        name=name,
        grid=(b, nb + 1),
        in_specs=[src_blk(0), src_blk(1), src_blk(2), gspec, gspec],
        out_specs=[src_blk(0), padded, padded],
        out_shape=[jax.ShapeDtypeStruct((b, s, ATTN_W), BF16), jax.ShapeDtypeStruct((b, KPAD + s, ATTN_W), BF16),
                   jax.ShapeDtypeStruct((b, KPAD + s, ATTN_W), BF16)],
        compiler_params=_params("parallel", "arbitrary"),
    )(proj, proj, proj, gq, gk)


def _qk_prep_bwd(name, proj, dqn, dkn, dv, gq, gk):
    b, s, _ = proj.shape
    tm = KPAD
    nb = s // tm

    def body(q_ref, k_ref, dqn_ref, dkn_ref, dv_ref, gq_ref, gk_ref, dq_ref, dk_ref, dvb_ref, dgq_ref, dgk_ref):
        bd = _head_group_matrix()

        def bwd(xv, dy, g):
            ms = _dot_exact_rhs(xv * xv, bd) * (1.0 / ATTN_DH)
            rstd = lax.rsqrt(ms + RMS_EPS)
            xhat = xv * rstd
            dxhat = dy * g
            gm = _dot_exact_rhs(dxhat * xhat, bd) * (1.0 / ATTN_DH)
            return rstd * (dxhat - xhat * gm), jnp.sum(dy * xhat, axis=0, keepdims=True)

        dq, dgq = bwd(q_ref[...], dqn_ref[...], gq_ref[...])
        dk, dgk = bwd(k_ref[...], dkn_ref[...], gk_ref[...])
        dq_ref[...] = dq.astype(BF16)
        dk_ref[...] = dk.astype(BF16)
        dvb_ref[...] = dv_ref[...].astype(BF16)
        dgq_ref[...] = dgq
        dgk_ref[...] = dgk

    col = lambda c: pl.BlockSpec((None, tm, ATTN_W), lambda bi, j: (bi, j, c))
    past_pad = pl.BlockSpec((None, tm, ATTN_W), lambda bi, j: (bi, j + 1, 0))
    gspec = pl.BlockSpec((1, ATTN_W), lambda bi, j: (0, 0))
    pspec = pl.BlockSpec((None, 1, ATTN_W), lambda bi, j: (bi * nb + j, 0, 0))
    o_shape = jax.ShapeDtypeStruct((b, s, ATTN_W), BF16)
    p_shape = jax.ShapeDtypeStruct((b * nb, 1, ATTN_W), F32)
    return pl.pallas_call(
        body,
        name=name,
        grid=(b, nb),
        in_specs=[col(0), col(1), col(0), past_pad, past_pad, gspec, gspec],
        out_specs=[col(0)] * 3 + [pspec] * 2,
        out_shape=[o_shape] * 3 + [p_shape] * 2,
        compiler_params=_params("parallel", "parallel"),
    )(proj, proj, dqn, dkn, dv, gq, gk)


Q_CHUNKS = 4
QBLK = Q_CHUNKS * CHUNK
WIN = (LEFT_CHUNKS + Q_CHUNKS) * CHUNK
DB_W = BAND + CHUNK
MASKED = -1e30


def _band_table(bias):
    rows = [jnp.pad(bias, ((0, 0), (0, 0), (CHUNK * i, WIN - BAND - CHUNK * i)), constant_values=MASKED)
            for i in range(Q_CHUNKS)]
    return jnp.concatenate(rows, axis=1)


def _head_lanes(hh):
    lane = lax.broadcasted_iota(jnp.int32, (1, LANES), 1)
    return (lane < ATTN_DH) if hh == 0 else (lane >= ATTN_DH)


def _attn_probs(qh, kw, table, start):
    s = _dot(qh, kw, NT) * (ATTN_DH ** -0.5) + table
    col = lax.broadcasted_iota(jnp.int32, (QBLK, WIN), 1)
    s = jnp.where(col + start >= KPAD, s, MASKED)
    m = jnp.max(s, axis=-1, keepdims=True)
    p = jnp.exp(s - m)
    return p * (1.0 / jnp.sum(p, axis=-1, keepdims=True))


def _attn_fwd(name, q, k, v, table):
    b, s, w = q.shape
    sp = k.shape[1]

    def body(q_ref, k_ref, v_ref, t_ref, o_ref):
        start = pl.multiple_of(pl.program_id(2) * QBLK, QBLK)
        kw = k_ref[pl.ds(start, WIN), :]
        vw = v_ref[pl.ds(start, WIN), :]
        q2 = q_ref[...]
        out = jnp.zeros((QBLK, LANES), F32)
        for hh in range(2):
            mine = _head_lanes(hh)
            p = _attn_probs(jnp.where(mine, q2, jnp.zeros_like(q2)), kw, t_ref[hh], start)
            out = jnp.where(mine, _dot(p.astype(BF16), vw), out)
        o_ref[...] = out.astype(BF16)

    qspec = pl.BlockSpec((None, QBLK, LANES), lambda p, bi, i: (bi, i, p))
    kspec = pl.BlockSpec((None, sp, LANES), lambda p, bi, i: (bi, 0, p))
    return pl.pallas_call(
        body,
        name=name,
        grid=(w // LANES, b, s // QBLK),
        in_specs=[qspec, kspec, kspec, pl.BlockSpec((2, QBLK, WIN), lambda p, bi, i: (p, 0, 0))],
        out_specs=qspec,
        out_shape=jax.ShapeDtypeStruct((b, s, w), BF16),
        compiler_params=_params("parallel", "parallel", "arbitrary"),
    )(q, k, v, table)


def _attn_bwd(name, q, k, v, table, dmix):
    b, s, w = q.shape
    sp = k.shape[1]

    def body(q_ref, k_ref, v_ref, t_ref, do_ref, dq_ref, dk_ref, dv_ref, dbe_ref, dbo_ref):
        bi = pl.program_id(1)
        i = pl.program_id(2)
        start = pl.multiple_of(i * QBLK, QBLK)
        win = pl.ds(start, WIN)

        @pl.when(i == 0)
        def _():
            dk_ref[...] = jnp.zeros_like(dk_ref)
            dv_ref[...] = jnp.zeros_like(dv_ref)

        @pl.when(jnp.logical_and(i == 0, bi == 0))
        def _():
            dbe_ref[...] = jnp.zeros_like(dbe_ref)
            dbo_ref[...] = jnp.zeros_like(dbo_ref)

        kw = k_ref[win, :]
        vw = v_ref[win, :]
        q2 = q_ref[...]
        do2 = do_ref[...].astype(BF16)
        dq = jnp.zeros((QBLK, LANES), F32)
        for hh in range(2):
            mine = _head_lanes(hh)
            qh = jnp.where(mine, q2, jnp.zeros_like(q2))
            doh = jnp.where(mine, do2, jnp.zeros_like(do2))
            p = _attn_probs(qh, kw, t_ref[hh], start)
            dp = _dot(doh, vw, NT)
            ds = p * (dp - jnp.sum(p * dp, axis=-1, keepdims=True))
            for qi in range(Q_CHUNKS):
                c0 = (qi // 2) * LANES
                blk = ds[qi * CHUNK:(qi + 1) * CHUNK, c0:c0 + DB_W]
                if qi % 2 == 0:
                    dbe_ref[hh] += blk
                else:
                    dbo_ref[hh] += blk
            dsb = (ds * (ATTN_DH ** -0.5)).astype(BF16)
            dq = jnp.where(mine, _dot(dsb, kw), dq)
            dk_ref[win, :] += _dot(dsb, qh, TN)
            dv_ref[win, :] += _dot(p.astype(BF16), doh, TN)
        dq_ref[...] = dq

    qspec = pl.BlockSpec((None, QBLK, LANES), lambda p, bi, i: (bi, i, p))
    kspec = pl.BlockSpec((None, sp, LANES), lambda p, bi, i: (bi, 0, p))
    dbspec = pl.BlockSpec((2, CHUNK, DB_W), lambda p, bi, i: (p, 0, 0))
    db_shape = jax.ShapeDtypeStruct((ATTN_HEADS, CHUNK, DB_W), F32)
    return pl.pallas_call(
        body,
        name=name,
        grid=(w // LANES, b, s // QBLK),
        in_specs=[qspec, kspec, kspec, pl.BlockSpec((2, QBLK, WIN), lambda p, bi, i: (p, 0, 0)), qspec],
        out_specs=[qspec, kspec, kspec, dbspec, dbspec],
        out_shape=[jax.ShapeDtypeStruct((b, s, w), F32), jax.ShapeDtypeStruct((b, sp, w), F32),
                   jax.ShapeDtypeStruct((b, sp, w), F32), db_shape, db_shape],
        compiler_params=_params("arbitrary", "arbitrary", "arbitrary"),
    )(q, k, v, table, dmix)


HQ_COL = 3 * ATTN_W // HGRN_DH
HF_COL = HQ_COL + HGRN_HEADS
HI_COL = HF_COL + HGRN_HEADS
HG_COL = HI_COL + HGRN_HEADS
HGRN_ROWS = 8 * CHUNK


def _tri(lower):
    r = lax.broadcasted_iota(jnp.int32, (CHUNK, CHUNK), 0)
    c = lax.broadcasted_iota(jnp.int32, (CHUNK, CHUNK), 1)
    return (r >= c) if lower else (r <= c)


def _hgrn_chunk(hq, hf, lb, tril):
    sig = _sigmoid(hf)
    f = lb + (1.0 - lb) * sig
    g = jnp.log(f)
    ones_l = jnp.where(tril, 1.0, 0.0).astype(BF16)
    b = _dot_exact_lhs(ones_l, g)
    bl = jnp.sum(g, axis=0, keepdims=True)
    rows = lax.broadcasted_iota(jnp.int32, g.shape, 0)
    bm = jnp.sum(jnp.where(rows <= CHUNK // 2, g, 0.0), axis=0, keepdims=True)
    sq = _sigmoid(hq)
    q = hq * sq
    k = 1.0 - f
    return sig, f, b, bl, bm, sq, q, k


def _hgrn_fwd(name, proj, lb, go, b, s):
    nc = s // CHUNK
    t = b * s
    nblk = s // HGRN_ROWS
    cpb = HGRN_ROWS // CHUNK

    def body(hq_ref, hf_ref, hi_ref, hg_ref, lb_ref, go_ref, ro_ref, oraw_ref, st_ref, s_scr):
        tril = _tri(True)
        gov = go_ref[...]

        @pl.when(pl.program_id(1) == 0)
        def _():
            s_scr[...] = jnp.zeros_like(s_scr)

        def step(c, carry):
            sl = pl.ds(pl.multiple_of(c * CHUNK, CHUNK), CHUNK)
            for hh in range(HGRN_HEADS):
                cols = pl.ds(hh * HGRN_DH, HGRN_DH)
                lbv = lb_ref[:, cols]
                hg = hg_ref[sl, cols]
                _, _, bb, bl, bm, _, q, k = _hgrn_chunk(hq_ref[sl, cols], hf_ref[sl, cols], lbv, tril)
                vb = hi_ref[sl, cols].astype(BF16)
                qe = (q * jnp.exp(bb - bm)).astype(BF16)
                ke = (k * jnp.exp(bm - bb)).astype(BF16)
                a = jnp.where(tril, _dot(qe, ke, NT), 0.0)
                st = s_scr[hh]
                st_ref[hh, c] = st
                qb = (q * jnp.exp(bb)).astype(BF16)
                o = _dot(a.astype(BF16), vb) + _dot(qb, st.astype(BF16), NT)
                kb = (k * jnp.exp(bl - bb)).astype(BF16)
                s_scr[hh] = st * jnp.exp(bl) + _dot(vb, kb, TN)
                rstd = lax.rsqrt(jnp.mean(o * o, axis=-1, keepdims=True) + RMS_EPS)
                ro_ref[sl, cols] = ((o * rstd * gov) * _silu(hg)).astype(BF16)
                oraw_ref[sl, cols] = o
            return carry

        lax.fori_loop(0, cpb, step, 0)

    col = lambda base: pl.BlockSpec((HGRN_ROWS, HGRN_W), lambda bi, i: (bi * nblk + i, base // HGRN_HEADS))
    out = pl.BlockSpec((HGRN_ROWS, HGRN_W), lambda bi, i: (bi * nblk + i, 0))
    return pl.pallas_call(
        body,
        name=name,
        grid=(b, nblk),
        in_specs=[col(HQ_COL), col(HF_COL), col(HI_COL), col(HG_COL),
                  pl.BlockSpec((1, HGRN_W), lambda bi, i: (0, 0)), pl.BlockSpec((1, HGRN_DH), lambda bi, i: (0, 0))],
        out_specs=[out, out,
                   pl.BlockSpec((None, HGRN_HEADS, cpb, HGRN_DH, HGRN_DH), lambda bi, i: (bi, 0, i, 0, 0))],
        out_shape=[jax.ShapeDtypeStruct((t, HGRN_W), BF16), jax.ShapeDtypeStruct((t, HGRN_W), F32),
                   jax.ShapeDtypeStruct((b, HGRN_HEADS, nc, HGRN_DH, HGRN_DH), F32)],
        scratch_shapes=[pltpu.VMEM((HGRN_HEADS, HGRN_DH, HGRN_DH), F32)],
        compiler_params=_params("parallel", "arbitrary"),
    )(proj, proj, proj, proj, lb, go)


def _hgrn_bwd(name, proj, lb, go, oraw, states, dmix, b, s):
    t = b * s
    nblk = s // HGRN_ROWS
    cpb = HGRN_ROWS // CHUNK

    def body(hq_ref, hf_ref, hi_ref, hg_ref, lb_ref, go_ref, oraw_ref, st_ref, dro_ref,
             dhq_ref, dhf_ref, dhi_ref, dhg_ref, dlb_ref, dgo_ref, ds_scr, dlb_scr, dgo_scr):
        tril = _tri(True)
        ones_u = jnp.where(_tri(False), 1.0, 0.0).astype(BF16)
        gov = go_ref[...]

        @pl.when(pl.program_id(1) == 0)
        def _():
            ds_scr[...] = jnp.zeros_like(ds_scr)
            dlb_scr[...] = jnp.zeros_like(dlb_scr)
            dgo_scr[...] = jnp.zeros_like(dgo_scr)

        def step(ci, carry):
            c = cpb - 1 - ci
            sl = pl.ds(pl.multiple_of(c * CHUNK, CHUNK), CHUNK)
            for hh in range(HGRN_HEADS):
                cols = pl.ds(hh * HGRN_DH, HGRN_DH)
                lbv = lb_ref[:, cols]
                hq = hq_ref[sl, cols]
                hg = hg_ref[sl, cols]
                sig, f, bb, bl, bm, sq, q, k = _hgrn_chunk(hq, hf_ref[sl, cols], lbv, tril)
                vb = hi_ref[sl, cols].astype(BF16)
                ebm = jnp.exp(bb - bm)
                embm = jnp.exp(bm - bb)
                eb = jnp.exp(bb)
                ebl = jnp.exp(bl - bb)
                e_last = jnp.exp(bl)
                qe = (q * ebm).astype(BF16)
                ke = (k * embm).astype(BF16)
                qb = (q * eb).astype(BF16)
                kb = (k * ebl).astype(BF16)
                a = jnp.where(tril, _dot(qe, ke, NT), 0.0)
                st = st_ref[hh, c]
                dst = ds_scr[hh]
                o = oraw_ref[sl, cols]
                dro = dro_ref[sl, cols]
                sg = _sigmoid(hg)
                rstd = lax.rsqrt(jnp.mean(o * o, axis=-1, keepdims=True) + RMS_EPS)
                ohat = o * rstd
                dn = dro * (hg * sg)
                dhg_ref[sl, cols] = (dro * (ohat * gov) * (sg * (1.0 + hg * (1.0 - sg)))).astype(BF16)
                dgo_scr[:, cols] += jnp.sum(dn * ohat, axis=0, keepdims=True)
                dohat = dn * gov
                do = rstd * (dohat - ohat * jnp.mean(dohat * ohat, axis=-1, keepdims=True))
                dob = do.astype(BF16)
                dab = jnp.where(tril, _dot(dob, vb, NT), 0.0).astype(BF16)
                stb = st.astype(BF16)
                dstb = dst.astype(BF16)
                dv = _dot(a.astype(BF16), dob, TN) + _dot(kb, dstb, NT)
                dqe = _dot(dab, ke)
                dke = _dot(dab, qe, TN)
                dqb = _dot(dob, stb)
                dkb = _dot(vb, dstb)
                dq = dqe * ebm + dqb * eb
                dk = dke * embm + dkb * ebl
                db = (qe.astype(F32) * dqe - ke.astype(F32) * dke) + q * (dqb * eb) - k * (dkb * ebl)
                d_last = (jnp.sum(k * ebl * dkb, axis=0, keepdims=True)
                          + jnp.sum(dst * st, axis=0, keepdims=True) * e_last)
                dg = _dot_exact_lhs(ones_u, db) + d_last
                df = dg / f - dk
                dhf_ref[sl, cols] = (df * (1.0 - lbv) * sig * (1.0 - sig)).astype(BF16)
                dlb_scr[:, cols] += jnp.sum(df * (1.0 - sig), axis=0, keepdims=True)
                dhq_ref[sl, cols] = (dq * (sq * (1.0 + hq * (1.0 - sq)))).astype(BF16)
                dhi_ref[sl, cols] = dv.astype(BF16)
                ds_scr[hh] = dst * e_last + _dot(dob, qb, TN)
            return carry

        lax.fori_loop(0, cpb, step, 0)

        @pl.when(pl.program_id(1) == nblk - 1)
        def _():
            dlb_ref[...] = dlb_scr[...]
            dgo_ref[...] = dgo_scr[...]

    rows = lambda bi, i: bi * nblk + (nblk - 1 - i)
    col = lambda base: pl.BlockSpec((HGRN_ROWS, HGRN_W), lambda bi, i: (rows(bi, i), base // HGRN_HEADS))
    out = pl.BlockSpec((HGRN_ROWS, HGRN_W), lambda bi, i: (rows(bi, i), 0))
    part = pl.BlockSpec((None, 1, HGRN_W), lambda bi, i: (bi, 0, 0))
    o_shape = jax.ShapeDtypeStruct((t, HGRN_W), BF16)
    p_shape = jax.ShapeDtypeStruct((b, 1, HGRN_W), F32)
    return pl.pallas_call(
        body,
        name=name,
        grid=(b, nblk),
        in_specs=[col(HQ_COL), col(HF_COL), col(HI_COL), col(HG_COL),
                  pl.BlockSpec((1, HGRN_W), lambda bi, i: (0, 0)), pl.BlockSpec((1, HGRN_DH), lambda bi, i: (0, 0)), out,
                  pl.BlockSpec((None, HGRN_HEADS, cpb, HGRN_DH, HGRN_DH), lambda bi, i: (bi, 0, nblk - 1 - i, 0, 0)),
                  col(ATTN_W // HGRN_DH)],
        out_specs=[out] * 4 + [part] * 2,
        out_shape=[o_shape] * 4 + [p_shape] * 2,
        scratch_shapes=[pltpu.VMEM((HGRN_HEADS, HGRN_DH, HGRN_DH), F32), pltpu.VMEM((1, HGRN_W), F32),
                        pltpu.VMEM((1, HGRN_W), F32)],
        compiler_params=_params("parallel", "arbitrary"),
    )(proj, proj, proj, proj, lb, go, oraw, states, dmix)


def _small_grads(name, dg1, dgm, dg2, dgq, dgk, dbias_t, dlb, dgo, lbp):
    d = dg1.shape[1]

    def body(dg1_ref, dgm_ref, dg2_ref, dgq_ref, dgk_ref, dbias_ref, dlb_ref, dgo_ref, lbp_ref,
             g1_ref, gm_ref, g2_ref, gq_ref, gk_ref, rb_ref, lbg_ref, go_ref):
        g1_ref[...] = jnp.sum(dg1_ref[...], axis=0, keepdims=True)
        gm_ref[...] = jnp.sum(dgm_ref[...], axis=0, keepdims=True)
        g2_ref[...] = jnp.sum(dg2_ref[...], axis=0, keepdims=True)
        r = lax.broadcasted_iota(jnp.int32, (ATTN_W, ATTN_DH), 0)
        cidx = lax.broadcasted_iota(jnp.int32, (ATTN_W, ATTN_DH), 1)
        fold = jnp.where(jnp.bitwise_and(r, ATTN_DH - 1) == cidx, 1.0, 0.0).astype(BF16)
        gq_ref[...] = jnp.sum(_dot_exact_rhs(dgq_ref[...], fold), axis=0, keepdims=True)
        gk_ref[...] = jnp.sum(_dot_exact_rhs(dgk_ref[...], fold), axis=0, keepdims=True)
        gosum = jnp.sum(dgo_ref[...], axis=0, keepdims=True)
        go_ref[...] = (gosum[:, 0:HGRN_DH] + gosum[:, HGRN_DH:2 * HGRN_DH]
                       + gosum[:, 2 * HGRN_DH:3 * HGRN_DH] + gosum[:, 3 * HGRN_DH:4 * HGRN_DH])
        p0 = lbp_ref[0:1, :]
        p1 = lbp_ref[1:2, :]
        lbv = 1.0 / (1.0 + jnp.exp(p1 - p0))
        dp0 = jnp.sum(dlb_ref[...], axis=0, keepdims=True) * lbv * (1.0 - lbv)
        lbg_ref[0:1, :] = dp0
        lbg_ref[1:2, :] = -dp0
        sidx = lax.broadcasted_iota(jnp.int32, (BAND, N_REL_PAD), 0)
        ridx = lax.broadcasted_iota(jnp.int32, (BAND, N_REL_PAD), 1)

        def step(tq, acc):
            rel = jnp.clip(tq + KPAD - sidx, -REL_CLIP, REL_CLIP) + REL_CLIP
            onehot = jnp.where(rel == ridx, 1.0, 0.0).astype(BF16)
            return acc + _dot_exact_rhs(dbias_ref[tq], onehot)

        rb_ref[...] = lax.fori_loop(0, CHUNK, step, jnp.zeros((ATTN_HEADS, N_REL_PAD), F32))

    ins = [dg1, dgm, dg2, dgq, dgk, dbias_t, dlb, dgo, lbp]
    outs = [jax.ShapeDtypeStruct((1, d), F32)] * 3 + [jax.ShapeDtypeStruct((1, ATTN_DH), F32)] * 2 + [
        jax.ShapeDtypeStruct((ATTN_HEADS, N_REL_PAD), F32), jax.ShapeDtypeStruct((2, HGRN_W), F32),
        jax.ShapeDtypeStruct((1, HGRN_DH), F32)]
    vm = pl.BlockSpec(memory_space=pltpu.VMEM)
    return pl.pallas_call(
        body,
        name=name,
        in_specs=[vm] * len(ins),
        out_specs=[vm] * len(outs),
        out_shape=outs,
        compiler_params=pltpu.CompilerParams(vmem_limit_bytes=VMEM_LIMIT),
    )(*ins)


def _adam_update(w, g, m, v):
    m2 = ADAM_B1 * m + (1.0 - ADAM_B1) * g
    v2 = ADAM_B2 * v + (1.0 - ADAM_B2) * (g * g)
    m_hat = m2 / (1.0 - ADAM_B1 ** ADAM_STEP)
    v_hat = v2 / (1.0 - ADAM_B2 ** ADAM_STEP)
    delta = -ADAM_LR * (m_hat / (jnp.sqrt(v_hat) + ADAM_EPS) + ADAM_WD * w)
    return delta, m2, v2


def _rows_tile(r):
    for cand in (256, 352, 128, 176, 64, 32, 16):
        if r % cand == 0 and r > cand:
            return cand
    return r


def _pair_sum(name, grad, theirs, core):
    n, half, c = theirs.shape
    tr = _rows_tile(half)
    nth = half // tr

    def body(core_ref, a_ref, b_ref, o_ref):
        o_ref[...] = (a_ref[...] + b_ref[...]).astype(o_ref.dtype)

    spec = pl.BlockSpec((None, tr, c), lambda i, j, core_ref: (i, j, 0))
    return pl.pallas_call(
        body, name=name,
        grid_spec=pltpu.PrefetchScalarGridSpec(
            num_scalar_prefetch=1, grid=(n, nth),
            in_specs=[pl.BlockSpec((None, tr, c), lambda i, j, core_ref: (i, core_ref[0] * nth + j, 0)), spec],
            out_specs=spec),
        out_shape=jax.ShapeDtypeStruct((n, half, c), BF16), compiler_params=_params("parallel", "parallel"),
    )(core, grad, theirs)


def _chip_sum(name, own, parts, chip):
    _, half, c = own.shape
    tr = _rows_tile(half)

    def body(chip_ref, own_ref, p_ref, o_ref):
        me = chip_ref[0]
        mine = own_ref[...].astype(F32)
        flip_x, flip_y, flip_xy = (p_ref[i].astype(F32) for i in range(3))
        acc = None
        for k in range(N_CHIPS):
            rel = jnp.bitwise_xor(me, k)
            term = jnp.where(rel == 0, mine, jnp.where(rel == 2, flip_x, jnp.where(rel == 1, flip_y, flip_xy)))
            acc = term if acc is None else acc + term
        o_ref[...] = acc

    return pl.pallas_call(
        body, name=name,
        grid_spec=pltpu.PrefetchScalarGridSpec(
            num_scalar_prefetch=1, grid=(half // tr,),
            in_specs=[pl.BlockSpec((None, tr, c), lambda j, chip_ref: (chip_ref[0], j, 0)),
                      pl.BlockSpec((3, tr, c), lambda j, chip_ref: (0, j, 0))],
            out_specs=pl.BlockSpec((tr, c), lambda j, chip_ref: (j, 0))),
        out_shape=jax.ShapeDtypeStruct((half, c), F32), compiler_params=_params("parallel"),
    )(chip, own, parts)


def _adamw(name, w, g_mine, g_theirs, m, v, core):
    _, r, c = w.shape
    half = r // 2
    tr = _rows_tile(half)
    nth = half // tr

    def body(core_ref, w_ref, gm_ref, gt_ref, m_ref, v_ref, g_ref, d_ref, m2_ref, v2_ref):
        g = jnp.where(pl.program_id(0) == core_ref[0], gm_ref[...], gt_ref[...])
        delta, m2, v2 = _adam_update(w_ref[...], g, m_ref[...], v_ref[...])
        g_ref[...] = g
        d_ref[...] = delta
        m2_ref[...] = m2
        v2_ref[...] = v2

    full = pl.BlockSpec((None, tr, c), lambda h, j, core_ref: (0, h * nth + j, 0))
    part = pl.BlockSpec((tr, c), lambda h, j, core_ref: (j, 0))
    shape = jax.ShapeDtypeStruct((1, r, c), F32)
    return pl.pallas_call(
        body, name=name,
        grid_spec=pltpu.PrefetchScalarGridSpec(
            num_scalar_prefetch=1, grid=(2, nth), in_specs=[full, part, part, full, full], out_specs=[full] * 4),
        out_shape=[shape] * 4, compiler_params=_params("parallel", "parallel"),
    )(core, w, g_mine, g_theirs, m, v)


def _rel_bias_table(name, rel_bias):
    padded = jnp.pad(rel_bias, ((0, 0), (0, N_REL_PAD - N_REL)))

    def body(rb_ref, o_ref):
        ridx = lax.broadcasted_iota(jnp.int32, (N_REL_PAD, BAND), 0)
        sidx = lax.broadcasted_iota(jnp.int32, (N_REL_PAD, BAND), 1)
        rb = rb_ref[...]

        def step(tq, carry):
            rel = jnp.clip(tq + KPAD - sidx, -REL_CLIP, REL_CLIP) + REL_CLIP
            onehot = jnp.where(rel == ridx, 1.0, 0.0).astype(BF16)
            o_ref[tq] = _dot_exact_rhs(rb, onehot)
            return carry

        lax.fori_loop(0, CHUNK, step, 0)

    vm = pl.BlockSpec(memory_space=pltpu.VMEM)
    table = pl.pallas_call(
        body, name=name, in_specs=[vm], out_specs=vm,
        out_shape=jax.ShapeDtypeStruct((CHUNK, ATTN_HEADS, BAND), F32),
    )(padded)
    return table.transpose(1, 0, 2)


def _adamw_small(name, w, parts, m, v):
    def body(w_ref, p_ref, m_ref, v_ref, g_ref, d_ref, m2_ref, v2_ref):
        g = p_ref[0]
        for i in range(1, N_DEV):
            g = g + p_ref[i]
        delta, m2, v2 = _adam_update(w_ref[...], g, m_ref[...], v_ref[...])
        g_ref[...] = g
        d_ref[...] = delta
        m2_ref[...] = m2
        v2_ref[...] = v2

    vm = pl.BlockSpec(memory_space=pltpu.VMEM)
    shape = jax.ShapeDtypeStruct((SMALL_ROWS, SMALL_COLS), F32)
    return pl.pallas_call(
        body, name=name, in_specs=[vm] * 4, out_specs=[vm] * 4, out_shape=[shape] * 4,
    )(w, parts, m, v)


def _position():
    return lax.axis_index("x"), lax.axis_index("y"), lax.axis_index("c")


def _other_chips(x, y):
    return [(1 - x, y), (x, 1 - y), (1 - x, 1 - y)]


ANY = pl.BlockSpec(memory_space=pl.ANY)


HBM = pl.BlockSpec(memory_space=pltpu.HBM)
SEM = pl.BlockSpec(memory_space=pltpu.SEMAPHORE)
SPLIT_COPY = pltpu.SideEffectType.DATAFLOW_SIDE_EFFECTING


def _gather_copy(shards, outs, send_sem, recv_sem, i, j):
    x, y, c = _position()
    chips = _other_chips(x, y)
    half = shards[i].shape[0] // 2
    rows = pl.ds(pl.multiple_of(c * half, 16), half)
    return pltpu.make_async_remote_copy(
        src_ref=shards[i].at[rows, :], dst_ref=outs[i].at[2 * x + y, rows, :],
        send_sem=send_sem.at[3 * i + j], recv_sem=recv_sem.at[3 * i + j],
        device_id=(chips[j][0], chips[j][1], c), device_id_type=MESH)


def _gather_start(name, shards, after):
    n = len(shards)

    def body(*refs):
        srcs, outs = refs[:n], refs[n:2 * n]
        send_sem, recv_sem = refs[2 * n + len(after)], refs[2 * n + len(after) + 1]
        token = refs[-1]
        for i in range(n):
            for j in range(3):
                _gather_copy(srcs, outs, send_sem, recv_sem, i, j).start()
        token[...] = jnp.zeros_like(token)

    full = [(N_CHIPS,) + s.shape for s in shards]
    res = pl.pallas_call(
        body,
        name=name,
        in_specs=[HBM] * (2 * n) + [ANY] * len(after),
        out_specs=[SEM, SEM] + [HBM] * (2 * n) + [pl.BlockSpec(memory_space=pltpu.VMEM)],
        out_shape=[pltpu.SemaphoreType.DMA((3 * n,)), pltpu.SemaphoreType.DMA((3 * n,))]
        + [pltpu.HBM(s.shape, s.dtype) for s in shards]
        + [pltpu.HBM(shp, s.dtype) for shp, s in zip(full, shards)]
        + [jax.ShapeDtypeStruct((8, LANES), F32)],
        input_output_aliases={i: 2 + i for i in range(2 * n)},
        compiler_params=pltpu.CompilerParams(has_side_effects=SPLIT_COPY),
    )(*[pltpu.with_memory_space_constraint(s, pltpu.HBM) for s in shards],
      *[pltpu.with_memory_space_constraint(lax.empty(shp, s.dtype), pltpu.HBM) for shp, s in zip(full, shards)],
      *after)
    return res[0], res[1], list(res[2:2 + n]), list(res[2 + n:2 + 2 * n]), res[-1]


def _gather_wait(name, send_sem, recv_sem, shards, outs, after):
    n = len(shards)

    def body(*refs):
        srcs, out_refs = refs[:n], refs[n:2 * n]
        send_ref, recv_ref = refs[2 * n], refs[2 * n + 1]
        for i in range(n):
            for j in range(3):
                copy = _gather_copy(srcs, out_refs, send_ref, recv_ref, i, j)
                copy.wait_send()
                copy.wait_recv()

    res = pl.pallas_call(
        body,
        name=name,
        in_specs=[HBM] * (2 * n) + [SEM, SEM, ANY],
        out_specs=[HBM] * (2 * n),
        out_shape=[pltpu.HBM(s.shape, s.dtype) for s in shards] + [pltpu.HBM(o.shape, o.dtype) for o in outs],
        input_output_aliases={i: i for i in range(2 * n)},
        compiler_params=pltpu.CompilerParams(has_side_effects=SPLIT_COPY),
    )(*shards, *outs, send_sem, recv_sem, after)
    return list(res[:n]), list(res[n:])


def _gather_join(name, shards, outs):
    n = len(shards)

    def body(*refs):
        srcs, ins, outs_ = refs[:n], refs[n:2 * n], refs[2 * n:3 * n]
        own_send, own_recv, half_send, half_recv = refs[3 * n:]
        x, y, c = _position()
        chips = _other_chips(x, y)
        copies = []
        for i in range(n):
            copies.append(pltpu.make_async_remote_copy(
                src_ref=srcs[i], dst_ref=outs_[i].at[2 * x + y], send_sem=own_send.at[i], recv_sem=own_recv.at[i],
                device_id=(x, y, 1 - c), device_id_type=MESH))
            half = srcs[i].shape[0] // 2
            rows = pl.ds(pl.multiple_of(c * half, 16), half)
            for j in range(3):
                slot = 2 * chips[j][0] + chips[j][1]
                copies.append(pltpu.make_async_remote_copy(
                    src_ref=ins[i].at[slot, rows, :], dst_ref=outs_[i].at[slot, rows, :],
                    send_sem=half_send.at[3 * i + j], recv_sem=half_recv.at[3 * i + j],
                    device_id=(x, y, 1 - c), device_id_type=MESH))
        for cp in copies:
            cp.start()
        for cp in copies:
            cp.wait()

    return pl.pallas_call(
        body,
        name=name,
        in_specs=[ANY] * (2 * n),
        out_specs=[ANY] * n,
        out_shape=[jax.ShapeDtypeStruct(o.shape, o.dtype) for o in outs],
        input_output_aliases={n + i: i for i in range(n)},
        scratch_shapes=[pltpu.SemaphoreType.DMA((n,))] * 2 + [pltpu.SemaphoreType.DMA((3 * n,))] * 2,
    )(*shards, *outs)


def _pair_copy(grads, lands, send_sem, recv_sem, i):
    x, y, c = _position()
    half = grads[i].shape[1] // 2
    give = pl.ds(pl.multiple_of((1 - c) * half, 8), half)
    return pltpu.make_async_remote_copy(
        src_ref=grads[i].at[:, give, :], dst_ref=lands[i], send_sem=send_sem.at[i], recv_sem=recv_sem.at[i],
        device_id=(x, y, 1 - c), device_id_type=MESH)


def _pair_start(name, grads):
    n = len(grads)

    def body(*refs):
        srcs, lands = refs[:n], refs[n:2 * n]
        send_sem, recv_sem = refs[2 * n], refs[2 * n + 1]
        token = refs[-1]
        for i in range(n):
            _pair_copy(srcs, lands, send_sem, recv_sem, i).start()
        token[...] = jnp.zeros_like(token)

    halves = [(g.shape[0], g.shape[1] // 2, g.shape[2]) for g in grads]
    res = pl.pallas_call(
        body,
        name=name,
        in_specs=[HBM] * (2 * n),
        out_specs=[SEM, SEM] + [HBM] * (2 * n) + [pl.BlockSpec(memory_space=pltpu.VMEM)],
        out_shape=[pltpu.SemaphoreType.DMA((n,)), pltpu.SemaphoreType.DMA((n,))]
        + [pltpu.HBM(g.shape, g.dtype) for g in grads]
        + [pltpu.HBM(shp, g.dtype) for shp, g in zip(halves, grads)]
        + [jax.ShapeDtypeStruct((8, LANES), F32)],
        input_output_aliases={i: 2 + i for i in range(2 * n)},
        compiler_params=pltpu.CompilerParams(has_side_effects=SPLIT_COPY),
    )(*[pltpu.with_memory_space_constraint(g, pltpu.HBM) for g in grads],
      *[pltpu.with_memory_space_constraint(lax.empty(shp, g.dtype), pltpu.HBM) for shp, g in zip(halves, grads)])
    return res[0], res[1], list(res[2:2 + n]), list(res[2 + n:2 + 2 * n]), res[-1]


def _pair_wait(name, send_sem, recv_sem, grads, lands, after):
    n = len(grads)

    def body(*refs):
        srcs, land_refs = refs[:n], refs[n:2 * n]
        send_ref, recv_ref = refs[2 * n], refs[2 * n + 1]
        for i in range(n):
            copy = _pair_copy(srcs, land_refs, send_ref, recv_ref, i)
            copy.wait_send()
            copy.wait_recv()

    res = pl.pallas_call(
        body,
        name=name,
        in_specs=[HBM] * (2 * n) + [SEM, SEM, ANY],
        out_specs=[HBM] * (2 * n),
        out_shape=[pltpu.HBM(g.shape, g.dtype) for g in grads] + [pltpu.HBM(l.shape, l.dtype) for l in lands],
        input_output_aliases={i: i for i in range(2 * n)},
        compiler_params=pltpu.CompilerParams(has_side_effects=SPLIT_COPY),
    )(*grads, *lands, send_sem, recv_sem, after)
    return list(res[:n]), list(res[n:])


def _scatter_copy(srcs, lands, send_sem, recv_sem, i, j):
    x, y, c = _position()
    chips = _other_chips(x, y)
    return pltpu.make_async_remote_copy(
        src_ref=srcs[i].at[2 * chips[j][0] + chips[j][1]], dst_ref=lands[i].at[j],
        send_sem=send_sem.at[3 * i + j], recv_sem=recv_sem.at[3 * i + j],
        device_id=(chips[j][0], chips[j][1], c), device_id_type=MESH)


def _scatter_start(name, sums):
    n = len(sums)

    def body(*refs):
        srcs, lands = refs[:n], refs[n:2 * n]
        send_sem, recv_sem = refs[2 * n], refs[2 * n + 1]
        token = refs[-1]
        for i in range(n):
            for j in range(3):
                _scatter_copy(srcs, lands, send_sem, recv_sem, i, j).start()
        token[...] = jnp.zeros_like(token)

    land_shapes = [(3,) + s.shape[1:] for s in sums]
    res = pl.pallas_call(
        body,
        name=name,
        in_specs=[HBM] * (2 * n),
        out_specs=[SEM, SEM] + [HBM] * (2 * n) + [pl.BlockSpec(memory_space=pltpu.VMEM)],
        out_shape=[pltpu.SemaphoreType.DMA((3 * n,)), pltpu.SemaphoreType.DMA((3 * n,))]
        + [pltpu.HBM(s.shape, s.dtype) for s in sums]
        + [pltpu.HBM(shp, s.dtype) for shp, s in zip(land_shapes, sums)]
        + [jax.ShapeDtypeStruct((8, LANES), F32)],
        input_output_aliases={i: 2 + i for i in range(2 * n)},
        compiler_params=pltpu.CompilerParams(has_side_effects=SPLIT_COPY),
    )(*[pltpu.with_memory_space_constraint(s, pltpu.HBM) for s in sums],
      *[pltpu.with_memory_space_constraint(lax.empty(shp, s.dtype), pltpu.HBM) for shp, s in zip(land_shapes, sums)])
    return res[0], res[1], list(res[2:2 + n]), list(res[2 + n:2 + 2 * n]), res[-1]


def _scatter_wait(name, send_sem, recv_sem, sums, lands, after):
    n = len(sums)

    def body(*refs):
        srcs, land_refs = refs[:n], refs[n:2 * n]
        send_ref, recv_ref = refs[2 * n], refs[2 * n + 1]
        for i in range(n):
            for j in range(3):
                copy = _scatter_copy(srcs, land_refs, send_ref, recv_ref, i, j)
                copy.wait_send()
                copy.wait_recv()

    res = pl.pallas_call(
        body,
        name=name,
        in_specs=[HBM] * (2 * n) + [SEM, SEM, ANY],
        out_specs=[HBM] * (2 * n),
        out_shape=[pltpu.HBM(s.shape, s.dtype) for s in sums] + [pltpu.HBM(l.shape, l.dtype) for l in lands],
        input_output_aliases={i: i for i in range(2 * n)},
        compiler_params=pltpu.CompilerParams(has_side_effects=SPLIT_COPY),
    )(*sums, *lands, send_sem, recv_sem, after)
    return list(res[:n]), list(res[n:])


def _pair_join(name, halves, small=None):
    n = len(halves)
    if small is None:
        def body_plain(*refs):
            ins, outs = refs[:n], refs[n:2 * n]
            send_sem, recv_sem = refs[2 * n:]
            x, y, c = _position()
            swaps = [pltpu.make_async_remote_copy(
                src_ref=ins[i], dst_ref=outs[i], send_sem=send_sem.at[i], recv_sem=recv_sem.at[i],
                device_id=(x, y, 1 - c), device_id_type=MESH) for i in range(n)]
            for swap in swaps:
                swap.start()
            for swap in swaps:
                swap.wait()

        return pl.pallas_call(
            body_plain,
            name=name,
            in_specs=[ANY] * n,
            out_specs=[ANY] * n,
            out_shape=[jax.ShapeDtypeStruct(h.shape, h.dtype) for h in halves],
            scratch_shapes=[pltpu.SemaphoreType.DMA((n,))] * 2,
        )(*halves)

    def body(*refs):
        ins, small_ref = refs[:n], refs[n]
        outs, all_ref = refs[n + 1:2 * n + 1], refs[2 * n + 1]
        send_sem, recv_sem, sm_send, sm_recv, sm_local = refs[2 * n + 2:]
        x, y, c = _position()
        swaps = []
        for i in range(n):
            swap = pltpu.make_async_remote_copy(
                src_ref=ins[i], dst_ref=outs[i], send_sem=send_sem.at[i], recv_sem=recv_sem.at[i],
                device_id=(x, y, 1 - c), device_id_type=MESH)
            swap.start()
            swaps.append(swap)
        me = 4 * x + 2 * y + c
        sm_own = pltpu.make_async_copy(small_ref, all_ref.at[me], sm_local)
        sm_own.start()
        pushes, arrivals = [], []
        for mask in range(1, N_DEV):
            px, py, pc = x ^ (mask >> 2), y ^ ((mask >> 1) & 1), c ^ (mask & 1)
            pushes.append(pltpu.make_async_remote_copy(
                src_ref=small_ref, dst_ref=all_ref.at[me], send_sem=sm_send.at[mask - 1], recv_sem=sm_recv.at[mask - 1],
                device_id=(px, py, pc), device_id_type=MESH))
            arrivals.append(pltpu.make_async_remote_copy(
                src_ref=small_ref, dst_ref=all_ref.at[4 * px + 2 * py + pc], send_sem=sm_send.at[mask - 1],
                recv_sem=sm_recv.at[mask - 1], device_id=(px, py, pc), device_id_type=MESH))
        for cp in pushes:
            cp.start()
        for swap in swaps:
            swap.wait()
        for cp in arrivals:
            cp.wait_recv()
        for cp in pushes:
            cp.wait_send()
        sm_own.wait()

    res = pl.pallas_call(
        body,
        name=name,
        in_specs=[ANY] * (n + 1),
        out_specs=[ANY] * (n + 1),
        out_shape=[jax.ShapeDtypeStruct(h.shape, h.dtype) for h in halves]
        + [jax.ShapeDtypeStruct((N_DEV,) + small.shape, small.dtype)],
        scratch_shapes=[pltpu.SemaphoreType.DMA((n,))] * 2 + [pltpu.SemaphoreType.DMA((N_DEV - 1,))] * 2
        + [pltpu.SemaphoreType.DMA(())],
    )(*halves, small)
    return res[:n], res[n]


def _lower_bound(lbp):
    return jax.nn.softmax(lbp, axis=0)[0:1]


def _local_step(x, target, g1, gm, g2, gq, gk, go, rel_bias, lbp, first_weights, mid_weights, last_weights, on_grads, grads_sent):
    b, s, d = x.shape
    t = b * s
    x0 = x.reshape(t, d)
    tgt = target.reshape(t, d)
    gq_t = jnp.tile(gq, (1, ATTN_HEADS))
    gk_t = jnp.tile(gk, (1, ATTN_HEADS))
    lb = _lower_bound(lbp)
    bias = _rel_bias_table("rel_bias_table", rel_bias)

    h1 = _rmsnorm_fwd("norm1", x0, g1)
    wg1, wu1, deps1 = first_weights(h1)
    a1, b1, z1 = _ffn_up("ffn1_up", h1, wg1, wu1, deps1)
    wd1, w_in, w_out = mid_weights(z1)
    ns = w_in.shape[0]
    x1 = _ffn_down("ffn1_down", z1, wd1, x0)
    h2 = _rmsnorm_fwd("norm_mix", x1, gm)
    proj = _in_proj("in_proj", h2, w_in)
    proj3 = proj.reshape(b, s, proj.shape[1])
    table = _band_table(bias)
    qn, kn, vb = _qk_prep("qk_prep", proj3, gq_t, gk_t)
    attn = _attn_fwd("attn_fwd", qn, kn, vb, table).reshape(t, ATTN_W)
    ro, oraw, states = _hgrn_fwd("hgrn_fwd", proj, lb, go, b, s)
    mix = jnp.concatenate([attn, ro], axis=1)
    x2 = _out_proj("out_proj", mix, w_out, x1)
    h3 = _rmsnorm_fwd("norm2", x2, g2)
    wg2, wu2, wd2 = last_weights(h3)
    a2, b2, z2 = _ffn_up("ffn2_up", h3, wg2, wu2)
    dy, dyh, sq = _ffn_down_loss("ffn2_down_loss", z2, wd2, x2, tgt)
    loss = 0.5 * jnp.sum(sq) / d

    da2, db2 = _ffn_bwd_act("ffn2_bwd_act", dyh, wd2, a2, b2)
    dwd2 = _grad_w_shardrows("ffn2_dwd", z2, dyh)
    dwg2 = _grad_w_shardrows("ffn2_dwg", da2, h3)
    dwu2 = _grad_w_shardrows("ffn2_dwu", db2, h3)
    sent2 = on_grads("ffn2", {"ffn2_w_gate": dwg2, "ffn2_w_up": dwu2, "ffn2_w_down": dwd2})
    dx2, dx2b, dg2 = _ffn_bwd_in("ffn2_bwd_in", da2, db2, wg2, wu2, x2, g2, dy, 1.0, sent2)
    sent2 = grads_sent("ffn2", dx2b)

    dwout = _grad_w_out("dw_out", mix, dx2b)
    dmix = _out_proj_bwd("out_proj_bwd", dx2b, w_out, sent2)
    dqn, dkn, dvn, dbe, dbo = _attn_bwd("attn_bwd", qn, kn, vb, table, dmix.reshape(b, s, dmix.shape[1]))
    dbias = dbe[:, :, :BAND] + dbo[:, :, CHUNK:]
    dpq, dpk, dpv, dgq, dgk = _qk_prep_bwd("qk_prep_bwd", proj3, dqn, dkn, dvn, gq_t, gk_t)
    dpq, dpk, dpv = (a.reshape(t, ATTN_W) for a in (dpq, dpk, dpv))
    dhq, dhf, dhi, dhg, dlb, dgo = _hgrn_bwd("hgrn_bwd", proj, lb, go, oraw, states, dmix, b, s)
    dproj = jnp.concatenate([dpq, dpk, dpv, dhq, dhf, dhi, dhg], axis=1)
    dwin = _grad_w_in("dw_in", h2, dproj, ns)
    sent_mix = on_grads("mix", {"w_in": dwin, "w_out": dwout.reshape(ns, dwout.shape[0] // ns, d)})
    dx1, dx1h, dgm = _in_proj_bwd("in_proj_bwd", dproj, w_in, x1, gm, dx2, 0.5, sent_mix)
    sent_mix = grads_sent("mix", dx1h)

    da1, db1 = _ffn_bwd_act("ffn1_bwd_act", dx1h, wd1, a1, b1, sent_mix)
    dwd1 = _grad_w_shardrows("ffn1_dwd", z1, dx1h)
    dwg1 = _grad_w_shardrows("ffn1_dwg", da1, h1)
    dwu1 = _grad_w_shardrows("ffn1_dwu", db1, h1)
    sent1 = on_grads("ffn1", {"ffn1_w_gate": dwg1, "ffn1_w_up": dwu1, "ffn1_w_down": dwd1})
    dx0, dg1 = _ffn_bwd_in("ffn1_bwd_in", da1, db1, wg1, wu1, x0, g1, dx1, None, sent1)
    grads_sent("ffn1", dx0)

    nt = dg1.shape[0]
    sg = _small_grads(
        "small_grads", dg1.reshape(nt, d), dgm.reshape(nt, d), dg2.reshape(nt, d),
        dgq.reshape(-1, ATTN_W), dgk.reshape(-1, ATTN_W), dbias.transpose(1, 0, 2),
        dlb.reshape(b, HGRN_W), dgo.reshape(b, HGRN_W), lbp)
    g1g, gmg, g2g, gqg, gkg, rbg, lbg, gog = sg
    small = _pack_small(g1g, gmg, g2g, lbg, rbg[:, :N_REL], gqg, gkg, gog)
    return loss, dx0.reshape(b, s, d), small


def _pack_small(g1, gm, g2, lbp, rel_bias, gq, gk, go):
    flat = [g1.reshape(-1), gm.reshape(-1), g2.reshape(-1), lbp.reshape(-1), rel_bias.reshape(-1)]
    n_bias = 3 * SMALL_COLS - rel_bias.size
    heads = [gq.reshape(-1), gk.reshape(-1), go.reshape(-1)]
    n_tail = SMALL_COLS - sum(h.size for h in heads)
    return jnp.concatenate(flat + [jnp.zeros((n_bias,), F32)] + heads + [jnp.zeros((n_tail,), F32)]).reshape(
        SMALL_ROWS, SMALL_COLS)


def _unpack_small(p, d):
    flat = p.reshape(-1)
    o = 3 * d
    g1, gm, g2 = p[0:1], p[1:2], p[2:3]
    lbp = flat[o:o + 2 * HGRN_W].reshape(2, HGRN_W)
    o = 4 * SMALL_COLS
    rel = flat[o:o + ATTN_HEADS * N_REL].reshape(1, ATTN_HEADS, N_REL)
    o = 7 * SMALL_COLS
    gq = flat[o:o + ATTN_DH].reshape(1, ATTN_DH)
    gk = flat[o + ATTN_DH:o + 2 * ATTN_DH].reshape(1, ATTN_DH)
    go = flat[o + 2 * ATTN_DH:o + 2 * ATTN_DH + HGRN_DH].reshape(1, HGRN_DH)
    return g1, gm, g2, gq, gk, rel, lbp, go


def kernel(x, ffn1_norm_g, ffn1_w_gate, ffn1_w_up, ffn1_w_down, mix_norm_g, w_in, attn_q_norm_g, attn_k_norm_g, attn_rel_bias, hgrn_lower_bounds, hgrn_out_norm_g, w_out, ffn2_norm_g, ffn2_w_gate, ffn2_w_up, ffn2_w_down, loss_target, m_ffn1_norm_g, m_ffn1_w_gate, m_ffn1_w_up, m_ffn1_w_down, m_mix_norm_g, m_w_in, m_attn_q_norm_g, m_attn_k_norm_g, m_attn_rel_bias, m_hgrn_lower_bounds, m_hgrn_out_norm_g, m_w_out, m_ffn2_norm_g, m_ffn2_w_gate, m_ffn2_w_up, m_ffn2_w_down, v_ffn1_norm_g, v_ffn1_w_gate, v_ffn1_w_up, v_ffn1_w_down, v_mix_norm_g, v_w_in, v_attn_q_norm_g, v_attn_k_norm_g, v_attn_rel_bias, v_hgrn_lower_bounds, v_hgrn_out_norm_g, v_w_out, v_ffn2_norm_g, v_ffn2_w_gate, v_ffn2_w_up, v_ffn2_w_down):
    d = x.shape[-1]
    big_w = [ffn1_w_gate, ffn1_w_up, ffn1_w_down, w_in, w_out, ffn2_w_gate, ffn2_w_up, ffn2_w_down]
    big_m = [m_ffn1_w_gate, m_ffn1_w_up, m_ffn1_w_down, m_w_in, m_w_out, m_ffn2_w_gate, m_ffn2_w_up, m_ffn2_w_down]
    big_v = [v_ffn1_w_gate, v_ffn1_w_up, v_ffn1_w_down, v_w_in, v_w_out, v_ffn2_w_gate, v_ffn2_w_up, v_ffn2_w_down]
    big_names = ["ffn1_w_gate", "ffn1_w_up", "ffn1_w_down", "w_in", "w_out", "ffn2_w_gate", "ffn2_w_up", "ffn2_w_down"]
    flipped = {nm for nm in big_names if nm.endswith("gate") or nm.endswith("up")}
    flip = lambda nm, a: jnp.swapaxes(a, 1, 2) if nm in flipped else a
    big_w, big_m, big_v = ([flip(nm, a) for nm, a in zip(big_names, arrs)] for arrs in (big_w, big_m, big_v))

    shards = [w[0].astype(BF16) for w in big_w]
    start_a = _gather_start("gather_start_up1", shards[:2], ())
    start_b = _gather_start("gather_start_mid", shards[2:5], (start_a[4],))
    start_c = _gather_start("gather_start_ffn2", shards[5:], (start_b[4],))

    def gathered(tag, started, after):
        send_sem, recv_sem, srcs, outs, _ = started
        srcs, outs = _gather_wait("gather_wait_" + tag, send_sem, recv_sem, srcs, outs, after)
        return _gather_join("gather_join_" + tag, srcs, outs)

    def first_weights(after):
        return (*gathered("up1", start_a, after), (start_c[4],))

    def mid_weights(after):
        wd1, win_f, wout_f = gathered("mid", start_b, after)
        return wd1, win_f, wout_f.reshape(wout_f.shape[0] * wout_f.shape[1], d)

    def last_weights(after):
        return gathered("ffn2", start_c, after)

    core = lax.axis_index("c").astype(jnp.int32).reshape(1)
    chip = (2 * lax.axis_index("x") + lax.axis_index("y")).astype(jnp.int32).reshape(1)
    started = {}

    def on_grads(tag, grads):
        names = list(grads)
        started[tag] = (names, _pair_start("pair_start_" + tag, [grads[nm] for nm in names]))
        return (started[tag][1][4],)

    def grads_sent(tag, after):
        names, (send_sem, recv_sem, grads, lands, _) = started[tag]
        grads, theirs = _pair_wait("pair_wait_" + tag, send_sem, recv_sem, grads, lands, after)
        sums = [_pair_sum("pair_sum_" + nm, g, th, core) for nm, g, th in zip(names, grads, theirs)]
        started[tag] = (names, _scatter_start("scatter_start_" + tag, sums))
        return (started[tag][1][4],)

    loss, grad_x, small_g = _local_step(
        x, loss_target, ffn1_norm_g, mix_norm_g, ffn2_norm_g, attn_q_norm_g, attn_k_norm_g, hgrn_out_norm_g,
        attn_rel_bias[0], hgrn_lower_bounds, first_weights, mid_weights, last_weights, on_grads, grads_sent)
    loss = lax.psum(loss, ("x", "y", "c"))

    def finish(tag, after):
        names, (send_sem, recv_sem, sums, lands, _) = started[tag]
        sums, lands = _scatter_wait("scatter_wait_" + tag, send_sem, recv_sem, sums, lands, after)
        return names, [_chip_sum("chip_sum_" + nm, sm, ld, chip) for nm, sm, ld in zip(names, sums, lands)]

    by_name = {nm: (w, m, v) for nm, w, m, v in zip(big_names, big_w, big_m, big_v)}
    updated = {}

    def update(names, halves, other_halves):
        for nm, mine, theirs in zip(names, halves, other_halves):
            w, m, v = by_name[nm]
            updated[nm] = _adamw("adamw_" + nm, w, mine, theirs, m, v, core)

    last_token = started["ffn1"][1][4]
    names_a, halves_a = finish("ffn2", last_token)
    names_m, halves_m = finish("mix", last_token)
    names_a, halves_a = names_a + names_m, halves_a + halves_m
    update(names_a, halves_a, _pair_join("pair_join_early", halves_a))
    names_b, halves_b = finish("ffn1", updated["w_out"][1])
    others_b, small_all = _pair_join("pair_join_last", halves_b, small_g)
    update(names_b, halves_b, others_b)
    big_out = [updated[nm] for nm in big_names]

    pack = lambda g1, gm, g2, gq, gk, rel, lbp, go: _pack_small(g1, gm, g2, lbp, rel[0], gq, gk, go)
    small_w = pack(ffn1_norm_g, mix_norm_g, ffn2_norm_g, attn_q_norm_g, attn_k_norm_g, attn_rel_bias, hgrn_lower_bounds, hgrn_out_norm_g)
    small_m = pack(m_ffn1_norm_g, m_mix_norm_g, m_ffn2_norm_g, m_attn_q_norm_g, m_attn_k_norm_g, m_attn_rel_bias, m_hgrn_lower_bounds, m_hgrn_out_norm_g)
    small_v = pack(v_ffn1_norm_g, v_mix_norm_g, v_ffn2_norm_g, v_attn_q_norm_g, v_attn_k_norm_g, v_attn_rel_bias, v_hgrn_lower_bounds, v_hgrn_out_norm_g)
    small_out = [_unpack_small(p, d) for p in _adamw_small("adamw_small", small_w, small_all, small_m, small_v)]

    def assemble(kind):
        bg = [flip(nm, o[kind]) for nm, o in zip(big_names, big_out)]
        g1, gm, g2, gq, gk, rel, lbp, go = small_out[kind]
        return [g1, bg[0], bg[1], bg[2], gm, bg[3], gq, gk, rel, lbp, go, bg[4], g2, bg[5], bg[6], bg[7]]

    return (loss, grad_x, *assemble(0), *assemble(1), *assemble(2), *assemble(3))
```

```python
import functools

import jax
import jax.numpy as jnp
from jax import lax
from jax.experimental import pallas as pl
from jax.experimental.pallas import tpu as pltpu

F32 = jnp.float32
BF16 = jnp.bfloat16
MESH = pl.DeviceIdType.MESH

N_CHIPS = 4
N_DEV = 8
CHUNK = 64
ATTN_HEADS = 8
ATTN_DH = 64
ATTN_W = ATTN_HEADS * ATTN_DH
HGRN_HEADS = 4
HGRN_DH = 128
HGRN_W = HGRN_HEADS * HGRN_DH
LEFT_CHUNKS = 8
BAND = (LEFT_CHUNKS + 1) * CHUNK
KPAD = LEFT_CHUNKS * CHUNK
REL_CLIP = 128
N_REL = 2 * REL_CLIP + 1
N_REL_PAD = 384
RMS_EPS = 1e-6
LANES = 128
SMALL_ROWS = 8
SMALL_COLS = 1024

ADAM_LR = 0.001
ADAM_B1 = 0.9
ADAM_B2 = 0.999
ADAM_EPS = 1e-08
ADAM_WD = 0.01
ADAM_STEP = 10

NN = (((1,), (0,)), ((), ()))
NT = (((1,), (1,)), ((), ()))
TN = (((0,), (0,)), ((), ()))

VMEM_LIMIT = 48 * 1024 * 1024


def _sigmoid(x):
    return 1.0 / (1.0 + jnp.exp(-x))


def _silu(x):
    return x * _sigmoid(x)


def _dot(a, b, dims=NN):
    return lax.dot_general(a, b, dims, preferred_element_type=F32)


def _split3(x):
    hi = x.astype(BF16)
    r1 = x - hi.astype(F32)
    mid = r1.astype(BF16)
    lo = (r1 - mid.astype(F32)).astype(BF16)
    return hi, mid, lo


def _dot_exact_rhs(x, mat, dims=NN):
    hi, mid, lo = _split3(x)
    return _dot(hi, mat, dims) + _dot(mid, mat, dims) + _dot(lo, mat, dims)


def _dot_exact_lhs(mat, x, dims=NN):
    hi, mid, lo = _split3(x)
    return _dot(mat, hi, dims) + _dot(mat, mid, dims) + _dot(mat, lo, dims)


def _params(*sem):
    return pltpu.CompilerParams(dimension_semantics=sem, vmem_limit_bytes=VMEM_LIMIT)


def _mm(name, ins, terms, n_acc, grid, acc_shape, outs, epilogue, extras=(), deps=()):
    nk = grid[2]
    ni, ne, nd, no = len(ins), len(extras), len(deps), len(outs)

    def body(*refs):
        in_refs = refs[:ni]
        ex_refs = refs[ni:ni + ne]
        out_refs = refs[ni + ne + nd:ni + ne + nd + no]
        acc_refs = refs[ni + ne + nd + no:]
        parts = [None] * n_acc
        for ai, li, ri, dims in terms:
            d = _dot(in_refs[li][...], in_refs[ri][...], dims)
            parts[ai] = d if parts[ai] is None else parts[ai] + d

        def finish(accs):
            res = epilogue(accs, [e[...] for e in ex_refs])
            for o, r in zip(out_refs, res):
                o[...] = r.astype(o.dtype)

        if nk == 1:
            finish(parts)
        else:
            k = pl.program_id(2)

            @pl.when(k == 0)
            def _():
                for a, p in zip(acc_refs, parts):
                    a[...] = p

            @pl.when(k > 0)
            def _():
                for a, p in zip(acc_refs, parts):
                    a[...] += p

            @pl.when(k == nk - 1)
            def _():
                finish([a[...] for a in acc_refs])

    scratch = [] if nk == 1 else [pltpu.VMEM(acc_shape, F32) for _ in range(n_acc)]
    res = pl.pallas_call(
        body,
        name=name,
        grid=grid,
        in_specs=[s for _, s in ins] + [s for _, s in extras] + [pl.BlockSpec(memory_space=pl.ANY)] * nd,
        out_specs=[s for _, s in outs],
        out_shape=[o for o, _ in outs],
        scratch_shapes=scratch,
        compiler_params=_params("parallel", "parallel", "arbitrary"),
    )(*[a for a, _ in ins], *[a for a, _ in extras], *deps)
    return res


def _mm_rows(name, lhs, weights, dims, t, outs, epilogue, extras=(), deps=()):
    tm = _row_tile(t)
    nl, ne, nd, no = len(lhs), len(extras), len(deps), len(outs)
    ns = weights[0].shape[0]

    def body(*refs):
        lhs_refs = refs[:nl]
        w_hbm = refs[nl:2 * nl]
        ex_refs = refs[2 * nl:2 * nl + ne]
        out_refs = refs[2 * nl + ne + nd:2 * nl + ne + nd + no]
        w_vmem = refs[2 * nl + ne + nd + no:3 * nl + ne + nd + no]
        sem = refs[-1]

        @pl.when(pl.program_id(0) == 0)
        def _():
            copies = [pltpu.make_async_copy(w_hbm[p], w_vmem[p], sem.at[p]) for p in range(nl)]
            for cp in copies:
                cp.start()
            for cp in copies:
                cp.wait()

        acc = None
        for p in range(nl):
            pick = lhs[p][2]
            for j in range(ns):
                part = _dot(pick(lhs_refs[p], j), w_vmem[p][j], dims)
                acc = part if acc is None else acc + part
        res = epilogue([acc], [e[...] for e in ex_refs])
        for o, r in zip(out_refs, res):
            o[...] = r.astype(o.dtype)

    return pl.pallas_call(
        body,
        name=name,
        grid=(t // tm,),
        in_specs=[s for _, s, _ in lhs] + [pl.BlockSpec(memory_space=pl.ANY)] * nl + [s for _, s in extras]
        + [pl.BlockSpec(memory_space=pl.ANY)] * nd,
        out_specs=[s for _, s in outs],
        out_shape=[o for o, _ in outs],
        scratch_shapes=[pltpu.VMEM(w.shape, w.dtype) for w in weights] + [pltpu.SemaphoreType.DMA((nl,))],
        compiler_params=_params("arbitrary"),
    )(*[a for a, _, _ in lhs], *weights, *[a for a, _ in extras], *deps)


def _row_tile(t):
    return 512 if t % 512 == 0 else t


def _k_tile(t):
    return t if t <= 4096 else 1024


def _rmsnorm_fwd(name, x, g):
    t, d = x.shape
    tm = _row_tile(t)

    def body(x_ref, g_ref, h_ref):
        xv = x_ref[...]
        ms = jnp.mean(xv * xv, axis=-1, keepdims=True)
        h_ref[...] = (xv * lax.rsqrt(ms + RMS_EPS) * g_ref[...]).astype(BF16)

    return pl.pallas_call(
        body,
        name=name,
        grid=(t // tm,),
        in_specs=[pl.BlockSpec((tm, d), lambda i: (i, 0)), pl.BlockSpec((1, d), lambda i: (0, 0))],
        out_specs=pl.BlockSpec((tm, d), lambda i: (i, 0)),
        out_shape=jax.ShapeDtypeStruct((t, d), BF16),
        compiler_params=_params("parallel"),
    )(x, g)


def _norm_bwd_epilogue(copy_scale):
    def epilogue(accs, ex):
        dh = accs[0]
        xv, g, dres = ex
        ms = jnp.mean(xv * xv, axis=-1, keepdims=True)
        rstd = lax.rsqrt(ms + RMS_EPS)
        xhat = xv * rstd
        dxhat = dh * g
        dx = rstd * (dxhat - xhat * jnp.mean(dxhat * xhat, axis=-1, keepdims=True))
        out = dres + dx
        dg = jnp.sum(dh * xhat, axis=0, keepdims=True)
        if copy_scale is None:
            return out, dg
        return out, out * copy_scale, dg

    return epilogue


def _ffn_up(name, h, wg, wu, deps=()):
    t, d = h.shape
    ns, f, _ = wg.shape
    tm = _row_tile(t)

    def epilogue(accs, ex):
        a, b = accs
        sg = _sigmoid(a)
        act = a * sg
        return act, b * (sg * (1.0 + a * (1.0 - sg))), act * b

    w_spec = pl.BlockSpec((None, f, d), lambda j, i, k: (j, 0, 0))
    o_spec = pl.BlockSpec((None, tm, f), lambda j, i, k: (j, i, 0))
    o_shape = jax.ShapeDtypeStruct((ns, t, f), BF16)
    return _mm(
        name,
        ins=[(h, pl.BlockSpec((tm, d), lambda j, i, k: (i, 0))), (wg, w_spec), (wu, w_spec)],
        terms=[(0, 0, 1, NT), (1, 0, 2, NT)],
        n_acc=2,
        grid=(ns, t // tm, 1),
        acc_shape=(tm, f),
        outs=[(o_shape, o_spec)] * 3,
        epilogue=epilogue,
        deps=deps,
    )


def _shard_rows(arr, tm):
    ns, _, f = arr.shape
    return arr, pl.BlockSpec((ns, tm, f), lambda i: (0, i, 0)), lambda ref, j: ref[j]


def _ffn_down(name, z, wd, x):
    _, t, _ = z.shape
    d = wd.shape[2]
    tm = _row_tile(t)
    row = pl.BlockSpec((tm, d), lambda i: (i, 0))
    return _mm_rows(
        name, [_shard_rows(z, tm)], [wd], NN, t,
        outs=[(jax.ShapeDtypeStruct((t, d), F32), row)],
        epilogue=lambda accs, ex: (ex[0] + 0.5 * accs[0],),
        extras=[(x, row)],
    )[0]


def _ffn_down_loss(name, z, wd, x, target):
    _, t, _ = z.shape
    d = wd.shape[2]
    tm = _row_tile(t)
    nt = t // tm
    row = pl.BlockSpec((tm, d), lambda i: (i, 0))

    def epilogue(accs, ex):
        e = ex[0] + 0.5 * accs[0] - ex[1]
        dy = e * (1.0 / d)
        return dy, 0.5 * dy, jnp.sum(e * e, axis=0, keepdims=True)

    return _mm_rows(
        name, [_shard_rows(z, tm)], [wd], NN, t,
        outs=[(jax.ShapeDtypeStruct((t, d), F32), row), (jax.ShapeDtypeStruct((t, d), BF16), row),
              (jax.ShapeDtypeStruct((nt, 1, d), F32), pl.BlockSpec((None, 1, d), lambda i: (i, 0, 0)))],
        epilogue=epilogue,
        extras=[(x, row), (target, row)],
    )


def _ffn_bwd_act(name, dout, wd, act_a, dact_b, deps=()):
    t, d = dout.shape
    ns, f, _ = wd.shape
    tm = _row_tile(t)

    def epilogue(accs, ex):
        dz = accs[0]
        return dz * ex[1].astype(F32), dz * ex[0].astype(F32)

    act = pl.BlockSpec((None, tm, f), lambda j, i, k: (j, i, 0))
    o_shape = jax.ShapeDtypeStruct((ns, t, f), BF16)
    return _mm(
        name,
        ins=[(dout, pl.BlockSpec((tm, d), lambda j, i, k: (i, 0))),
             (wd, pl.BlockSpec((None, f, d), lambda j, i, k: (j, 0, 0)))],
        terms=[(0, 0, 1, NT)],
        n_acc=1,
        grid=(ns, t // tm, 1),
        acc_shape=(tm, f),
        outs=[(o_shape, act)] * 2,
        epilogue=epilogue,
        extras=[(act_a, act), (dact_b, act)],
        deps=deps,
    )


def _grad_w_shardrows(name, z, dout, deps=()):
    ns, t, f = z.shape
    d = dout.shape[1]
    tk = _k_tile(t)
    return _mm(
        name,
        ins=[(z, pl.BlockSpec((None, tk, f), lambda j, n, k: (j, k, 0))),
             (dout, pl.BlockSpec((tk, d), lambda j, n, k: (k, 0)))],
        terms=[(0, 0, 1, TN)],
        n_acc=1,
        grid=(ns, 1, t // tk),
        acc_shape=(f, d),
        outs=[(jax.ShapeDtypeStruct((ns, f, d), F32), pl.BlockSpec((None, f, d), lambda j, n, k: (j, 0, 0)))],
        epilogue=lambda accs, ex: (accs[0],),
        deps=deps,
    )[0]


def _norm_bwd_outs(t, d, tm, copy_scale):
    row = pl.BlockSpec((tm, d), lambda i: (i, 0))
    outs = [(jax.ShapeDtypeStruct((t, d), F32), row)]
    if copy_scale is not None:
        outs.append((jax.ShapeDtypeStruct((t, d), BF16), row))
    outs.append((jax.ShapeDtypeStruct((t // tm, 1, d), F32), pl.BlockSpec((None, 1, d), lambda i: (i, 0, 0))))
    return row, outs


def _ffn_bwd_in(name, da, db, wg, wu, x, g, dres, copy_scale, deps=()):
    _, t, _ = da.shape
    d = wg.shape[2]
    tm = _row_tile(t)
    row, outs = _norm_bwd_outs(t, d, tm, copy_scale)
    return _mm_rows(
        name, [_shard_rows(da, tm), _shard_rows(db, tm)], [wg, wu], NN, t,
        outs=outs,
        epilogue=_norm_bwd_epilogue(copy_scale),
        extras=[(x, row), (g, pl.BlockSpec((1, d), lambda i: (0, 0))), (dres, row)],
        deps=deps,
    )


def _in_proj(name, h, w_in):
    t, d = h.shape
    ns, _, pj = w_in.shape
    tm = _row_tile(t)
    return _mm(
        name,
        ins=[(h, pl.BlockSpec((tm, d), lambda j, i, k: (i, 0))),
             (w_in, pl.BlockSpec((None, d, pj), lambda j, i, k: (j, 0, 0)))],
        terms=[(0, 0, 1, NN)],
        n_acc=1,
        grid=(ns, t // tm, 1),
        acc_shape=(tm, pj),
        outs=[(jax.ShapeDtypeStruct((t, ns * pj), F32), pl.BlockSpec((tm, pj), lambda j, i, k: (i, j)))],
        epilogue=lambda accs, ex: (accs[0],),
    )[0]


def _in_proj_bwd(name, dp, w_in, x, g, dres, copy_scale, deps=()):
    t = dp.shape[0]
    ns, d, pj = w_in.shape
    tm = _row_tile(t)
    row, outs = _norm_bwd_outs(t, d, tm, copy_scale)
    cols = (dp, pl.BlockSpec((tm, ns * pj), lambda i: (i, 0)), lambda ref, j: ref[:, j * pj:(j + 1) * pj])
    return _mm_rows(
        name, [cols], [w_in], NT, t,
        outs=outs,
        epilogue=_norm_bwd_epilogue(copy_scale),
        extras=[(x, row), (g, pl.BlockSpec((1, d), lambda i: (0, 0))), (dres, row)],
        deps=deps,
    )


def _grad_w_in(name, h, dp, ns):
    t, d = h.shape
    pj = dp.shape[1] // ns
    tk = _k_tile(t)
    return _mm(
        name,
        ins=[(h, pl.BlockSpec((tk, d), lambda j, n, k: (k, 0))),
             (dp, pl.BlockSpec((tk, pj), lambda j, n, k: (k, j)))],
        terms=[(0, 0, 1, TN)],
        n_acc=1,
        grid=(ns, 1, t // tk),
        acc_shape=(d, pj),
        outs=[(jax.ShapeDtypeStruct((ns, d, pj), F32), pl.BlockSpec((None, d, pj), lambda j, n, k: (j, 0, 0)))],
        epilogue=lambda accs, ex: (accs[0],),
    )[0]


def _out_proj(name, mix, w_out, x):
    t, dm = mix.shape
    d = w_out.shape[1]
    tm = _row_tile(t)
    row = pl.BlockSpec((tm, d), lambda i, n, k: (i, 0))
    return _mm(
        name,
        ins=[(mix, pl.BlockSpec((tm, dm), lambda i, n, k: (i, 0))),
             (w_out, pl.BlockSpec((dm, d), lambda i, n, k: (0, 0)))],
        terms=[(0, 0, 1, NN)],
        n_acc=1,
        grid=(t // tm, 1, 1),
        acc_shape=(tm, d),
        outs=[(jax.ShapeDtypeStruct((t, d), F32), row)],
        epilogue=lambda accs, ex: (ex[0] + accs[0],),
        extras=[(x, row)],
    )[0]


def _out_proj_bwd(name, dx, w_out, deps=()):
    t, d = dx.shape
    dm = w_out.shape[0]
    tm = _row_tile(t)
    return _mm(
        name,
        ins=[(dx, pl.BlockSpec((tm, d), lambda i, n, k: (i, 0))),
             (w_out, pl.BlockSpec((dm, d), lambda i, n, k: (0, 0)))],
        terms=[(0, 0, 1, NT)],
        n_acc=1,
        grid=(t // tm, 1, 1),
        acc_shape=(tm, dm),
        outs=[(jax.ShapeDtypeStruct((t, dm), F32), pl.BlockSpec((tm, dm), lambda i, n, k: (i, 0)))],
        epilogue=lambda accs, ex: (accs[0],),
        deps=deps,
    )[0]


def _grad_w_out(name, mix, dx):
    t, dm = mix.shape
    d = dx.shape[1]
    tk = _k_tile(t)
    return _mm(
        name,
        ins=[(mix, pl.BlockSpec((tk, dm), lambda a, n, k: (k, 0))),
             (dx, pl.BlockSpec((tk, d), lambda a, n, k: (k, 0)))],
        terms=[(0, 0, 1, TN)],
        n_acc=1,
        grid=(1, 1, t // tk),
        acc_shape=(dm, d),
        outs=[(jax.ShapeDtypeStruct((dm, d), F32), pl.BlockSpec((dm, d), lambda a, n, k: (0, 0)))],
        epilogue=lambda accs, ex: (accs[0],),
    )[0]


def _head_group_matrix():
    r = lax.broadcasted_iota(jnp.int32, (ATTN_W, ATTN_W), 0)
    c = lax.broadcasted_iota(jnp.int32, (ATTN_W, ATTN_W), 1)
    same = jnp.right_shift(r, 6) == jnp.right_shift(c, 6)
    return jnp.where(same, 1.0, 0.0).astype(BF16)


def _qk_prep(name, proj, gq, gk):
    b, s, _ = proj.shape
    tm = KPAD
    nb = s // tm

    def body(q_ref, k_ref, v_ref, gq_ref, gk_ref, qn_ref, kn_ref, vb_ref):
        j = pl.program_id(1)
        bd = _head_group_matrix()

        def norm(xv, g):
            ms = _dot_exact_rhs(xv * xv, bd) * (1.0 / ATTN_DH)
            return xv * lax.rsqrt(ms + RMS_EPS) * g

        @pl.when(j == 0)
        def _():
            kn_ref[...] = jnp.zeros_like(kn_ref)
            vb_ref[...] = jnp.zeros_like(vb_ref)

        @pl.when(j > 0)
        def _():
            qn_ref[...] = norm(q_ref[...], gq_ref[...]).astype(BF16)
            kn_ref[...] = norm(k_ref[...], gk_ref[...]).astype(BF16)
            vb_ref[...] = v_ref[...].astype(BF16)

    src_blk = lambda col: pl.BlockSpec((None, tm, ATTN_W), lambda bi, j: (bi, jnp.maximum(j - 1, 0), col))
    gspec = pl.BlockSpec((1, ATTN_W), lambda bi, j: (0, 0))
    padded = pl.BlockSpec((None, tm, ATTN_W), lambda bi, j: (bi, j, 0))
    return pl.pallas_call(
        body,
        name=name,
        grid=(b, nb + 1),
        in_specs=[src_blk(0), src_blk(1), src_blk(2), gspec, gspec],
        out_specs=[src_blk(0), padded, padded],
        out_shape=[jax.ShapeDtypeStruct((b, s, ATTN_W), BF16), jax.ShapeDtypeStruct((b, KPAD + s, ATTN_W), BF16),
                   jax.ShapeDtypeStruct((b, KPAD + s, ATTN_W), BF16)],
        compiler_params=_params("parallel", "arbitrary"),
    )(proj, proj, proj, gq, gk)


def _qk_prep_bwd(name, proj, dqn, dkn, dv, gq, gk):
    b, s, _ = proj.shape
    tm = KPAD
    nb = s // tm

    def body(q_ref, k_ref, dqn_ref, dkn_ref, dv_ref, gq_ref, gk_ref, dq_ref, dk_ref, dvb_ref, dgq_ref, dgk_ref):
        bd = _head_group_matrix()

        def bwd(xv, dy, g):
            ms = _dot_exact_rhs(xv * xv, bd) * (1.0 / ATTN_DH)
            rstd = lax.rsqrt(ms + RMS_EPS)
            xhat = xv * rstd
            dxhat = dy * g
            gm = _dot_exact_rhs(dxhat * xhat, bd) * (1.0 / ATTN_DH)
            return rstd * (dxhat - xhat * gm), jnp.sum(dy * xhat, axis=0, keepdims=True)

        dq, dgq = bwd(q_ref[...], dqn_ref[...], gq_ref[...])
        dk, dgk = bwd(k_ref[...], dkn_ref[...], gk_ref[...])
        dq_ref[...] = dq.astype(BF16)
        dk_ref[...] = dk.astype(BF16)
        dvb_ref[...] = dv_ref[...].astype(BF16)
        dgq_ref[...] = dgq
        dgk_ref[...] = dgk

    col = lambda c: pl.BlockSpec((None, tm, ATTN_W), lambda bi, j: (bi, j, c))
    past_pad = pl.BlockSpec((None, tm, ATTN_W), lambda bi, j: (bi, j + 1, 0))
    gspec = pl.BlockSpec((1, ATTN_W), lambda bi, j: (0, 0))
    pspec = pl.BlockSpec((None, 1, ATTN_W), lambda bi, j: (bi * nb + j, 0, 0))
    o_shape = jax.ShapeDtypeStruct((b, s, ATTN_W), BF16)
    p_shape = jax.ShapeDtypeStruct((b * nb, 1, ATTN_W), F32)
    return pl.pallas_call(
        body,
        name=name,
        grid=(b, nb),
        in_specs=[col(0), col(1), col(0), past_pad, past_pad, gspec, gspec],
        out_specs=[col(0)] * 3 + [pspec] * 2,
        out_shape=[o_shape] * 3 + [p_shape] * 2,
        compiler_params=_params("parallel", "parallel"),
    )(proj, proj, dqn, dkn, dv, gq, gk)


Q_CHUNKS = 4
QBLK = Q_CHUNKS * CHUNK
WIN = (LEFT_CHUNKS + Q_CHUNKS) * CHUNK
DB_W = BAND + CHUNK
MASKED = -1e30


def _band_table(bias):
    rows = [jnp.pad(bias, ((0, 0), (0, 0), (CHUNK * i, WIN - BAND - CHUNK * i)), constant_values=MASKED)
            for i in range(Q_CHUNKS)]
    return jnp.concatenate(rows, axis=1)


def _head_lanes(hh):
    lane = lax.broadcasted_iota(jnp.int32, (1, LANES), 1)
    return (lane < ATTN_DH) if hh == 0 else (lane >= ATTN_DH)


def _attn_probs(qh, kw, table, start):
    s = _dot(qh, kw, NT) * (ATTN_DH ** -0.5) + table
    col = lax.broadcasted_iota(jnp.int32, (QBLK, WIN), 1)
    s = jnp.where(col + start >= KPAD, s, MASKED)
    m = jnp.max(s, axis=-1, keepdims=True)
    p = jnp.exp(s - m)
    return p * (1.0 / jnp.sum(p, axis=-1, keepdims=True))


def _attn_fwd(name, q, k, v, table):
    b, s, w = q.shape
    sp = k.shape[1]

    def body(q_ref, k_ref, v_ref, t_ref, o_ref):
        start = pl.multiple_of(pl.program_id(2) * QBLK, QBLK)
        kw = k_ref[pl.ds(start, WIN), :]
        vw = v_ref[pl.ds(start, WIN), :]
        q2 = q_ref[...]
        out = jnp.zeros((QBLK, LANES), F32)
        for hh in range(2):
            mine = _head_lanes(hh)
            p = _attn_probs(jnp.where(mine, q2, jnp.zeros_like(q2)), kw, t_ref[hh], start)
            out = jnp.where(mine, _dot(p.astype(BF16), vw), out)
        o_ref[...] = out.astype(BF16)

    qspec = pl.BlockSpec((None, QBLK, LANES), lambda p, bi, i: (bi, i, p))
    kspec = pl.BlockSpec((None, sp, LANES), lambda p, bi, i: (bi, 0, p))
    return pl.pallas_call(
        body,
        name=name,
        grid=(w // LANES, b, s // QBLK),
        in_specs=[qspec, kspec, kspec, pl.BlockSpec((2, QBLK, WIN), lambda p, bi, i: (p, 0, 0))],
        out_specs=qspec,
        out_shape=jax.ShapeDtypeStruct((b, s, w), BF16),
        compiler_params=_params("parallel", "parallel", "arbitrary"),
    )(q, k, v, table)


def _attn_bwd(name, q, k, v, table, dmix):
    b, s, w = q.shape
    sp = k.shape[1]

    def body(q_ref, k_ref, v_ref, t_ref, do_ref, dq_ref, dk_ref, dv_ref, dbe_ref, dbo_ref):
        bi = pl.program_id(1)
        i = pl.program_id(2)
        start = pl.multiple_of(i * QBLK, QBLK)
        win = pl.ds(start, WIN)

        @pl.when(i == 0)
        def _():
            dk_ref[...] = jnp.zeros_like(dk_ref)
            dv_ref[...] = jnp.zeros_like(dv_ref)

        @pl.when(jnp.logical_and(i == 0, bi == 0))
        def _():
            dbe_ref[...] = jnp.zeros_like(dbe_ref)
            dbo_ref[...] = jnp.zeros_like(dbo_ref)

        kw = k_ref[win, :]
        vw = v_ref[win, :]
        q2 = q_ref[...]
        do2 = do_ref[...].astype(BF16)
        dq = jnp.zeros((QBLK, LANES), F32)
        dk = dv = None
        for hh in range(2):
            mine = _head_lanes(hh)
            qh = jnp.where(mine, q2, jnp.zeros_like(q2))
            doh = jnp.where(mine, do2, jnp.zeros_like(do2))
            p = _attn_probs(qh, kw, t_ref[hh], start)
            dp = _dot(doh, vw, NT)
            ds = p * (dp - jnp.sum(p * dp, axis=-1, keepdims=True))
            for qi in range(Q_CHUNKS):
                c0 = (qi // 2) * LANES
                blk = ds[qi * CHUNK:(qi + 1) * CHUNK, c0:c0 + DB_W]
                if qi % 2 == 0:
                    dbe_ref[hh] += blk
                else:
                    dbo_ref[hh] += blk
            dsb = (ds * (ATTN_DH ** -0.5)).astype(BF16)
            dq = jnp.where(mine, _dot(dsb, kw), dq)
            dk_h = _dot(dsb, qh, TN)
            dv_h = _dot(p.astype(BF16), doh, TN)
            dk = dk_h if dk is None else dk + dk_h
            dv = dv_h if dv is None else dv + dv_h
        dq_ref[...] = dq
        dk_ref[win, :] += dk
        dv_ref[win, :] += dv

    qspec = pl.BlockSpec((None, QBLK, LANES), lambda p, bi, i: (bi, i, p))
    kspec = pl.BlockSpec((None, sp, LANES), lambda p, bi, i: (bi, 0, p))
    dbspec = pl.BlockSpec((2, CHUNK, DB_W), lambda p, bi, i: (p, 0, 0))
    db_shape = jax.ShapeDtypeStruct((ATTN_HEADS, CHUNK, DB_W), F32)
    return pl.pallas_call(
        body,
        name=name,
        grid=(w // LANES, b, s // QBLK),
        in_specs=[qspec, kspec, kspec, pl.BlockSpec((2, QBLK, WIN), lambda p, bi, i: (p, 0, 0)), qspec],
        out_specs=[qspec, kspec, kspec, dbspec, dbspec],
        out_shape=[jax.ShapeDtypeStruct((b, s, w), F32), jax.ShapeDtypeStruct((b, sp, w), F32),
                   jax.ShapeDtypeStruct((b, sp, w), F32), db_shape, db_shape],
        compiler_params=_params("arbitrary", "arbitrary", "arbitrary"),
    )(q, k, v, table, dmix)


HQ_COL = 3 * ATTN_W // HGRN_DH
HF_COL = HQ_COL + HGRN_HEADS
HI_COL = HF_COL + HGRN_HEADS
HG_COL = HI_COL + HGRN_HEADS
HGRN_ROWS = 8 * CHUNK


def _tri(lower):
    r = lax.broadcasted_iota(jnp.int32, (CHUNK, CHUNK), 0)
    c = lax.broadcasted_iota(jnp.int32, (CHUNK, CHUNK), 1)
    return (r >= c) if lower else (r <= c)


def _hgrn_chunk(hq, hf, lb, tril):
    sig = _sigmoid(hf)
    f = lb + (1.0 - lb) * sig
    g = jnp.log(f)
    ones_l = jnp.where(tril, 1.0, 0.0).astype(BF16)
    b = _dot_exact_lhs(ones_l, g)
    bl = jnp.sum(g, axis=0, keepdims=True)
    rows = lax.broadcasted_iota(jnp.int32, g.shape, 0)
    bm = jnp.sum(jnp.where(rows <= CHUNK // 2, g, 0.0), axis=0, keepdims=True)
    sq = _sigmoid(hq)
    q = hq * sq
    k = 1.0 - f
    return sig, f, b, bl, bm, sq, q, k


def _hgrn_fwd(name, proj, lb, go, b, s):
    nc = s // CHUNK
    t = b * s
    nblk = s // HGRN_ROWS
    cpb = HGRN_ROWS // CHUNK

    def body(hq_ref, hf_ref, hi_ref, hg_ref, lb_ref, go_ref, ro_ref, oraw_ref, st_ref, s_scr):
        tril = _tri(True)
        gov = go_ref[...]

        @pl.when(pl.program_id(1) == 0)
        def _():
            s_scr[...] = jnp.zeros_like(s_scr)

        def step(c, carry):
            sl = pl.ds(pl.multiple_of(c * CHUNK, CHUNK), CHUNK)
            done = []
            for hh in range(HGRN_HEADS):
                cols = pl.ds(hh * HGRN_DH, HGRN_DH)
                lbv = lb_ref[:, cols]
                hg = hg_ref[sl, cols]
                _, _, bb, bl, bm, _, q, k = _hgrn_chunk(hq_ref[sl, cols], hf_ref[sl, cols], lbv, tril)
                vb = hi_ref[sl, cols].astype(BF16)
                qe = (q * jnp.exp(bb - bm)).astype(BF16)
                ke = (k * jnp.exp(bm - bb)).astype(BF16)
                a = jnp.where(tril, _dot(qe, ke, NT), 0.0)
                st = s_scr[hh]
                qb = (q * jnp.exp(bb)).astype(BF16)
                o = _dot(a.astype(BF16), vb) + _dot(qb, st.astype(BF16), NT)
                kb = (k * jnp.exp(bl - bb)).astype(BF16)
                rstd = lax.rsqrt(jnp.mean(o * o, axis=-1, keepdims=True) + RMS_EPS)
                done.append((cols, st, st * jnp.exp(bl) + _dot(vb, kb, TN), o,
                             ((o * rstd * gov) * _silu(hg)).astype(BF16)))
            for hh, (cols, st, st_next, o, ro) in enumerate(done):
                st_ref[hh, c] = st
                s_scr[hh] = st_next
                ro_ref[sl, cols] = ro
                oraw_ref[sl, cols] = o
            return carry

        lax.fori_loop(0, cpb, step, 0)

    col = lambda base: pl.BlockSpec((HGRN_ROWS, HGRN_W), lambda bi, i: (bi * nblk + i, base // HGRN_HEADS))
    out = pl.BlockSpec((HGRN_ROWS, HGRN_W), lambda bi, i: (bi * nblk + i, 0))
    return pl.pallas_call(
        body,
        name=name,
        grid=(b, nblk),
        in_specs=[col(HQ_COL), col(HF_COL), col(HI_COL), col(HG_COL),
                  pl.BlockSpec((1, HGRN_W), lambda bi, i: (0, 0)), pl.BlockSpec((1, HGRN_DH), lambda bi, i: (0, 0))],
        out_specs=[out, out,
                   pl.BlockSpec((None, HGRN_HEADS, cpb, HGRN_DH, HGRN_DH), lambda bi, i: (bi, 0, i, 0, 0))],
        out_shape=[jax.ShapeDtypeStruct((t, HGRN_W), BF16), jax.ShapeDtypeStruct((t, HGRN_W), F32),
                   jax.ShapeDtypeStruct((b, HGRN_HEADS, nc, HGRN_DH, HGRN_DH), F32)],
        scratch_shapes=[pltpu.VMEM((HGRN_HEADS, HGRN_DH, HGRN_DH), F32)],
        compiler_params=_params("parallel", "arbitrary"),
    )(proj, proj, proj, proj, lb, go)


def _hgrn_bwd(name, proj, lb, go, oraw, states, dmix, b, s):
    t = b * s
    nblk = s // HGRN_ROWS
    cpb = HGRN_ROWS // CHUNK

    def body(hq_ref, hf_ref, hi_ref, hg_ref, lb_ref, go_ref, oraw_ref, st_ref, dro_ref,
             dhq_ref, dhf_ref, dhi_ref, dhg_ref, dlb_ref, dgo_ref, ds_scr, dlb_scr, dgo_scr):
        tril = _tri(True)
        ones_u = jnp.where(_tri(False), 1.0, 0.0).astype(BF16)
        gov = go_ref[...]

        @pl.when(pl.program_id(1) == 0)
        def _():
            ds_scr[...] = jnp.zeros_like(ds_scr)
            dlb_scr[...] = jnp.zeros_like(dlb_scr)
            dgo_scr[...] = jnp.zeros_like(dgo_scr)

        def step(ci, carry):
            c = cpb - 1 - ci
            sl = pl.ds(pl.multiple_of(c * CHUNK, CHUNK), CHUNK)
            done = []
            for hh in range(HGRN_HEADS):
                cols = pl.ds(hh * HGRN_DH, HGRN_DH)
                lbv = lb_ref[:, cols]
                hq = hq_ref[sl, cols]
                hg = hg_ref[sl, cols]
                sig, f, bb, bl, bm, sq, q, k = _hgrn_chunk(hq, hf_ref[sl, cols], lbv, tril)
                vb = hi_ref[sl, cols].astype(BF16)
                ebm = jnp.exp(bb - bm)
                embm = jnp.exp(bm - bb)
                eb = jnp.exp(bb)
                ebl = jnp.exp(bl - bb)
                e_last = jnp.exp(bl)
                qe = (q * ebm).astype(BF16)
                ke = (k * embm).astype(BF16)
                qb = (q * eb).astype(BF16)
                kb = (k * ebl).astype(BF16)
                a = jnp.where(tril, _dot(qe, ke, NT), 0.0)
                st = st_ref[hh, c]
                dst = ds_scr[hh]
                o = oraw_ref[sl, cols]
                dro = dro_ref[sl, cols]
                sg = _sigmoid(hg)
                rstd = lax.rsqrt(jnp.mean(o * o, axis=-1, keepdims=True) + RMS_EPS)
                ohat = o * rstd
                dn = dro * (hg * sg)
                dhg = (dro * (ohat * gov) * (sg * (1.0 + hg * (1.0 - sg)))).astype(BF16)
                dgo_h = jnp.sum(dn * ohat, axis=0, keepdims=True)
                dohat = dn * gov
                do = rstd * (dohat - ohat * jnp.mean(dohat * ohat, axis=-1, keepdims=True))
                dob = do.astype(BF16)
                dab = jnp.where(tril, _dot(dob, vb, NT), 0.0).astype(BF16)
                stb = st.astype(BF16)
                dstb = dst.astype(BF16)
                dv = _dot(a.astype(BF16), dob, TN) + _dot(kb, dstb, NT)
                dqe = _dot(dab, ke)
                dke = _dot(dab, qe, TN)
                dqb = _dot(dob, stb)
                dkb = _dot(vb, dstb)
                dq = dqe * ebm + dqb * eb
                dk = dke * embm + dkb * ebl
                db = (qe.astype(F32) * dqe - ke.astype(F32) * dke) + q * (dqb * eb) - k * (dkb * ebl)
                d_last = (jnp.sum(k * ebl * dkb, axis=0, keepdims=True)
                          + jnp.sum(dst * st, axis=0, keepdims=True) * e_last)
                dg = _dot_exact_lhs(ones_u, db) + d_last
                df = dg / f - dk
                done.append((cols, (dq * (sq * (1.0 + hq * (1.0 - sq)))).astype(BF16),
                             (df * (1.0 - lbv) * sig * (1.0 - sig)).astype(BF16), dv.astype(BF16), dhg,
                             jnp.sum(df * (1.0 - sig), axis=0, keepdims=True), dgo_h,
                             dst * e_last + _dot(dob, qb, TN)))
            for hh, (cols, dhq, dhf, dhi, dhg, dlb_h, dgo_h, dst_next) in enumerate(done):
                dhq_ref[sl, cols] = dhq
                dhf_ref[sl, cols] = dhf
                dhi_ref[sl, cols] = dhi
                dhg_ref[sl, cols] = dhg
                dlb_scr[:, cols] += dlb_h
                dgo_scr[:, cols] += dgo_h
                ds_scr[hh] = dst_next
            return carry

        lax.fori_loop(0, cpb, step, 0)

        @pl.when(pl.program_id(1) == nblk - 1)
        def _():
            dlb_ref[...] = dlb_scr[...]
            dgo_ref[...] = dgo_scr[...]

    rows = lambda bi, i: bi * nblk + (nblk - 1 - i)
    col = lambda base: pl.BlockSpec((HGRN_ROWS, HGRN_W), lambda bi, i: (rows(bi, i), base // HGRN_HEADS))
    out = pl.BlockSpec((HGRN_ROWS, HGRN_W), lambda bi, i: (rows(bi, i), 0))
    part = pl.BlockSpec((None, 1, HGRN_W), lambda bi, i: (bi, 0, 0))
    o_shape = jax.ShapeDtypeStruct((t, HGRN_W), BF16)
    p_shape = jax.ShapeDtypeStruct((b, 1, HGRN_W), F32)
    return pl.pallas_call(
        body,
        name=name,
        grid=(b, nblk),
        in_specs=[col(HQ_COL), col(HF_COL), col(HI_COL), col(HG_COL),
                  pl.BlockSpec((1, HGRN_W), lambda bi, i: (0, 0)), pl.BlockSpec((1, HGRN_DH), lambda bi, i: (0, 0)), out,
                  pl.BlockSpec((None, HGRN_HEADS, cpb, HGRN_DH, HGRN_DH), lambda bi, i: (bi, 0, nblk - 1 - i, 0, 0)),
                  col(ATTN_W // HGRN_DH)],
        out_specs=[out] * 4 + [part] * 2,
        out_shape=[o_shape] * 4 + [p_shape] * 2,
        scratch_shapes=[pltpu.VMEM((HGRN_HEADS, HGRN_DH, HGRN_DH), F32), pltpu.VMEM((1, HGRN_W), F32),
                        pltpu.VMEM((1, HGRN_W), F32)],
        compiler_params=_params("parallel", "arbitrary"),
    )(proj, proj, proj, proj, lb, go, oraw, states, dmix)


def _small_grads(name, dg1, dgm, dg2, dgq, dgk, dbias_t, dlb, dgo, lbp):
    d = dg1.shape[1]

    def body(dg1_ref, dgm_ref, dg2_ref, dgq_ref, dgk_ref, dbias_ref, dlb_ref, dgo_ref, lbp_ref,
             g1_ref, gm_ref, g2_ref, gq_ref, gk_ref, rb_ref, lbg_ref, go_ref):
        g1_ref[...] = jnp.sum(dg1_ref[...], axis=0, keepdims=True)
        gm_ref[...] = jnp.sum(dgm_ref[...], axis=0, keepdims=True)
        g2_ref[...] = jnp.sum(dg2_ref[...], axis=0, keepdims=True)
        r = lax.broadcasted_iota(jnp.int32, (ATTN_W, ATTN_DH), 0)
        cidx = lax.broadcasted_iota(jnp.int32, (ATTN_W, ATTN_DH), 1)
        fold = jnp.where(jnp.bitwise_and(r, ATTN_DH - 1) == cidx, 1.0, 0.0).astype(BF16)
        gq_ref[...] = jnp.sum(_dot_exact_rhs(dgq_ref[...], fold), axis=0, keepdims=True)
        gk_ref[...] = jnp.sum(_dot_exact_rhs(dgk_ref[...], fold), axis=0, keepdims=True)
        gosum = jnp.sum(dgo_ref[...], axis=0, keepdims=True)
        go_ref[...] = (gosum[:, 0:HGRN_DH] + gosum[:, HGRN_DH:2 * HGRN_DH]
                       + gosum[:, 2 * HGRN_DH:3 * HGRN_DH] + gosum[:, 3 * HGRN_DH:4 * HGRN_DH])
        p0 = lbp_ref[0:1, :]
        p1 = lbp_ref[1:2, :]
        lbv = 1.0 / (1.0 + jnp.exp(p1 - p0))
        dp0 = jnp.sum(dlb_ref[...], axis=0, keepdims=True) * lbv * (1.0 - lbv)
        lbg_ref[0:1, :] = dp0
        lbg_ref[1:2, :] = -dp0
        sidx = lax.broadcasted_iota(jnp.int32, (BAND, N_REL_PAD), 0)
        ridx = lax.broadcasted_iota(jnp.int32, (BAND, N_REL_PAD), 1)

        def step(tq, acc):
            rel = jnp.clip(tq + KPAD - sidx, -REL_CLIP, REL_CLIP) + REL_CLIP
            onehot = jnp.where(rel == ridx, 1.0, 0.0).astype(BF16)
            return acc + _dot_exact_rhs(dbias_ref[tq], onehot)

        rb_ref[...] = lax.fori_loop(0, CHUNK, step, jnp.zeros((ATTN_HEADS, N_REL_PAD), F32))

    ins = [dg1, dgm, dg2, dgq, dgk, dbias_t, dlb, dgo, lbp]
    outs = [jax.ShapeDtypeStruct((1, d), F32)] * 3 + [jax.ShapeDtypeStruct((1, ATTN_DH), F32)] * 2 + [
        jax.ShapeDtypeStruct((ATTN_HEADS, N_REL_PAD), F32), jax.ShapeDtypeStruct((2, HGRN_W), F32),
        jax.ShapeDtypeStruct((1, HGRN_DH), F32)]
    vm = pl.BlockSpec(memory_space=pltpu.VMEM)
    return pl.pallas_call(
        body,
        name=name,
        in_specs=[vm] * len(ins),
        out_specs=[vm] * len(outs),
        out_shape=outs,
        compiler_params=pltpu.CompilerParams(vmem_limit_bytes=VMEM_LIMIT),
    )(*ins)


def _adam_update(w, g, m, v):
    m2 = ADAM_B1 * m + (1.0 - ADAM_B1) * g
    v2 = ADAM_B2 * v + (1.0 - ADAM_B2) * (g * g)
    m_hat = m2 / (1.0 - ADAM_B1 ** ADAM_STEP)
    v_hat = v2 / (1.0 - ADAM_B2 ** ADAM_STEP)
    delta = -ADAM_LR * (m_hat / (jnp.sqrt(v_hat) + ADAM_EPS) + ADAM_WD * w)
    return delta, m2, v2


def _rows_tile(r):
    for cand in (256, 352, 128, 176, 64, 32, 16):
        if r % cand == 0 and r > cand:
            return cand
    return r


def _pair_sum(name, grad, theirs, core):
    n, half, c = theirs.shape
    tr = _rows_tile(half)
    nth = half // tr

    def body(core_ref, a_ref, b_ref, o_ref):
        o_ref[...] = (a_ref[...] + b_ref[...]).astype(o_ref.dtype)

    spec = pl.BlockSpec((None, tr, c), lambda i, j, core_ref: (i, j, 0))
    return pl.pallas_call(
        body, name=name,
        grid_spec=pltpu.PrefetchScalarGridSpec(
            num_scalar_prefetch=1, grid=(n, nth),
            in_specs=[pl.BlockSpec((None, tr, c), lambda i, j, core_ref: (i, core_ref[0] * nth + j, 0)), spec],
            out_specs=spec),
        out_shape=jax.ShapeDtypeStruct((n, half, c), BF16), compiler_params=_params("parallel", "parallel"),
    )(core, grad, theirs)


def _chip_sum(name, own, parts, chip):
    _, half, c = own.shape
    tr = _rows_tile(half)

    def body(chip_ref, own_ref, p_ref, o_ref):
        me = chip_ref[0]
        mine = own_ref[...].astype(F32)
        flip_x, flip_y, flip_xy = (p_ref[i].astype(F32) for i in range(3))
        acc = None
        for k in range(N_CHIPS):
            rel = jnp.bitwise_xor(me, k)
            term = jnp.where(rel == 0, mine, jnp.where(rel == 2, flip_x, jnp.where(rel == 1, flip_y, flip_xy)))
            acc = term if acc is None else acc + term
        o_ref[...] = acc

    return pl.pallas_call(
        body, name=name,
        grid_spec=pltpu.PrefetchScalarGridSpec(
            num_scalar_prefetch=1, grid=(half // tr,),
            in_specs=[pl.BlockSpec((None, tr, c), lambda j, chip_ref: (chip_ref[0], j, 0)),
                      pl.BlockSpec((3, tr, c), lambda j, chip_ref: (0, j, 0))],
            out_specs=pl.BlockSpec((tr, c), lambda j, chip_ref: (j, 0))),
        out_shape=jax.ShapeDtypeStruct((half, c), F32), compiler_params=_params("parallel"),
    )(chip, own, parts)


def _adamw(name, w, g_mine, g_theirs, m, v, core):
    _, r, c = w.shape
    half = r // 2
    tr = _rows_tile(half)
    nth = half // tr

    def body(core_ref, w_ref, gm_ref, gt_ref, m_ref, v_ref, g_ref, d_ref, m2_ref, v2_ref):
        g = jnp.where(pl.program_id(0) == core_ref[0], gm_ref[...], gt_ref[...])
        delta, m2, v2 = _adam_update(w_ref[...], g, m_ref[...], v_ref[...])
        g_ref[...] = g
        d_ref[...] = delta
        m2_ref[...] = m2
        v2_ref[...] = v2

    full = pl.BlockSpec((None, tr, c), lambda h, j, core_ref: (0, h * nth + j, 0))
    part = pl.BlockSpec((tr, c), lambda h, j, core_ref: (j, 0))
    shape = jax.ShapeDtypeStruct((1, r, c), F32)
    return pl.pallas_call(
        body, name=name,
        grid_spec=pltpu.PrefetchScalarGridSpec(
            num_scalar_prefetch=1, grid=(2, nth), in_specs=[full, part, part, full, full], out_specs=[full] * 4),
        out_shape=[shape] * 4, compiler_params=_params("parallel", "parallel"),
    )(core, w, g_mine, g_theirs, m, v)


def _rel_bias_table(name, rel_bias):
    padded = jnp.pad(rel_bias, ((0, 0), (0, N_REL_PAD - N_REL)))

    def body(rb_ref, o_ref):
        ridx = lax.broadcasted_iota(jnp.int32, (N_REL_PAD, BAND), 0)
        sidx = lax.broadcasted_iota(jnp.int32, (N_REL_PAD, BAND), 1)
        rb = rb_ref[...]

        def step(tq, carry):
            rel = jnp.clip(tq + KPAD - sidx, -REL_CLIP, REL_CLIP) + REL_CLIP
            onehot = jnp.where(rel == ridx, 1.0, 0.0).astype(BF16)
            o_ref[tq] = _dot_exact_rhs(rb, onehot)
            return carry

        lax.fori_loop(0, CHUNK, step, 0)

    vm = pl.BlockSpec(memory_space=pltpu.VMEM)
    table = pl.pallas_call(
        body, name=name, in_specs=[vm], out_specs=vm,
        out_shape=jax.ShapeDtypeStruct((CHUNK, ATTN_HEADS, BAND), F32),
    )(padded)
    return table.transpose(1, 0, 2)


def _adamw_small(name, w, parts, m, v):
    def body(w_ref, p_ref, m_ref, v_ref, g_ref, d_ref, m2_ref, v2_ref):
        g = p_ref[0]
        for i in range(1, N_DEV):
            g = g + p_ref[i]
        delta, m2, v2 = _adam_update(w_ref[...], g, m_ref[...], v_ref[...])
        g_ref[...] = g
        d_ref[...] = delta
        m2_ref[...] = m2
        v2_ref[...] = v2

    vm = pl.BlockSpec(memory_space=pltpu.VMEM)
    shape = jax.ShapeDtypeStruct((SMALL_ROWS, SMALL_COLS), F32)
    return pl.pallas_call(
        body, name=name, in_specs=[vm] * 4, out_specs=[vm] * 4, out_shape=[shape] * 4,
    )(w, parts, m, v)


def _position():
    return lax.axis_index("x"), lax.axis_index("y"), lax.axis_index("c")


def _other_chips(x, y):
    return [(1 - x, y), (x, 1 - y), (1 - x, 1 - y)]


ANY = pl.BlockSpec(memory_space=pl.ANY)


HBM = pl.BlockSpec(memory_space=pltpu.HBM)
SEM = pl.BlockSpec(memory_space=pltpu.SEMAPHORE)
SPLIT_COPY = pltpu.SideEffectType.DATAFLOW_SIDE_EFFECTING


def _gather_copy(shards, outs, send_sem, recv_sem, i, j):
    x, y, c = _position()
    chips = _other_chips(x, y)
    half = shards[i].shape[0] // 2
    rows = pl.ds(pl.multiple_of(c * half, 16), half)
    return pltpu.make_async_remote_copy(
        src_ref=shards[i].at[rows, :], dst_ref=outs[i].at[2 * x + y, rows, :],
        send_sem=send_sem.at[3 * i + j], recv_sem=recv_sem.at[3 * i + j],
        device_id=(chips[j][0], chips[j][1], c), device_id_type=MESH)


def _gather_start(name, shards, after):
    n = len(shards)

    def body(*refs):
        srcs, outs = refs[:n], refs[n:2 * n]
        send_sem, recv_sem = refs[2 * n + len(after)], refs[2 * n + len(after) + 1]
        token = refs[-1]
        for i in range(n):
            for j in range(3):
                _gather_copy(srcs, outs, send_sem, recv_sem, i, j).start()
        token[...] = jnp.zeros_like(token)

    full = [(N_CHIPS,) + s.shape for s in shards]
    res = pl.pallas_call(
        body,
        name=name,
        in_specs=[HBM] * (2 * n) + [ANY] * len(after),
        out_specs=[SEM, SEM] + [HBM] * (2 * n) + [pl.BlockSpec(memory_space=pltpu.VMEM)],
        out_shape=[pltpu.SemaphoreType.DMA((3 * n,)), pltpu.SemaphoreType.DMA((3 * n,))]
        + [pltpu.HBM(s.shape, s.dtype) for s in shards]
        + [pltpu.HBM(shp, s.dtype) for shp, s in zip(full, shards)]
        + [jax.ShapeDtypeStruct((8, LANES), F32)],
        input_output_aliases={i: 2 + i for i in range(2 * n)},
        compiler_params=pltpu.CompilerParams(has_side_effects=SPLIT_COPY),
    )(*[pltpu.with_memory_space_constraint(s, pltpu.HBM) for s in shards],
      *[pltpu.with_memory_space_constraint(lax.empty(shp, s.dtype), pltpu.HBM) for shp, s in zip(full, shards)],
      *after)
    return res[0], res[1], list(res[2:2 + n]), list(res[2 + n:2 + 2 * n]), res[-1]


def _gather_wait(name, send_sem, recv_sem, shards, outs, after):
    n = len(shards)

    def body(*refs):
        srcs, out_refs = refs[:n], refs[n:2 * n]
        send_ref, recv_ref = refs[2 * n], refs[2 * n + 1]
        for i in range(n):
            for j in range(3):
                copy = _gather_copy(srcs, out_refs, send_ref, recv_ref, i, j)
                copy.wait_send()
                copy.wait_recv()

    res = pl.pallas_call(
        body,
        name=name,
        in_specs=[HBM] * (2 * n) + [SEM, SEM, ANY],
        out_specs=[HBM] * (2 * n),
        out_shape=[pltpu.HBM(s.shape, s.dtype) for s in shards] + [pltpu.HBM(o.shape, o.dtype) for o in outs],
        input_output_aliases={i: i for i in range(2 * n)},
        compiler_params=pltpu.CompilerParams(has_side_effects=SPLIT_COPY),
    )(*shards, *outs, send_sem, recv_sem, after)
    return list(res[:n]), list(res[n:])


def _gather_join(name, shards, outs):
    n = len(shards)

    def body(*refs):
        srcs, ins, outs_ = refs[:n], refs[n:2 * n], refs[2 * n:3 * n]
        own_send, own_recv, half_send, half_recv = refs[3 * n:]
        x, y, c = _position()
        chips = _other_chips(x, y)
        copies = []
        for i in range(n):
            copies.append(pltpu.make_async_remote_copy(
                src_ref=srcs[i], dst_ref=outs_[i].at[2 * x + y], send_sem=own_send.at[i], recv_sem=own_recv.at[i],
                device_id=(x, y, 1 - c), device_id_type=MESH))
            half = srcs[i].shape[0] // 2
            rows = pl.ds(pl.multiple_of(c * half, 16), half)
            for j in range(3):
                slot = 2 * chips[j][0] + chips[j][1]
                copies.append(pltpu.make_async_remote_copy(
                    src_ref=ins[i].at[slot, rows, :], dst_ref=outs_[i].at[slot, rows, :],
                    send_sem=half_send.at[3 * i + j], recv_sem=half_recv.at[3 * i + j],
                    device_id=(x, y, 1 - c), device_id_type=MESH))
        for cp in copies:
            cp.start()
        for cp in copies:
            cp.wait()

    return pl.pallas_call(
        body,
        name=name,
        in_specs=[ANY] * (2 * n),
        out_specs=[ANY] * n,
        out_shape=[jax.ShapeDtypeStruct(o.shape, o.dtype) for o in outs],
        input_output_aliases={n + i: i for i in range(n)},
        scratch_shapes=[pltpu.SemaphoreType.DMA((n,))] * 2 + [pltpu.SemaphoreType.DMA((3 * n,))] * 2,
    )(*shards, *outs)


def _pair_copy(grads, lands, send_sem, recv_sem, i):
    x, y, c = _position()
    half = grads[i].shape[1] // 2
    give = pl.ds(pl.multiple_of((1 - c) * half, 8), half)
    return pltpu.make_async_remote_copy(
        src_ref=grads[i].at[:, give, :], dst_ref=lands[i], send_sem=send_sem.at[i], recv_sem=recv_sem.at[i],
        device_id=(x, y, 1 - c), device_id_type=MESH)


def _pair_start(name, grads):
    n = len(grads)

    def body(*refs):
        srcs, lands = refs[:n], refs[n:2 * n]
        send_sem, recv_sem = refs[2 * n], refs[2 * n + 1]
        token = refs[-1]
        for i in range(n):
            _pair_copy(srcs, lands, send_sem, recv_sem, i).start()
        token[...] = jnp.zeros_like(token)

    halves = [(g.shape[0], g.shape[1] // 2, g.shape[2]) for g in grads]
    res = pl.pallas_call(
        body,
        name=name,
        in_specs=[HBM] * (2 * n),
        out_specs=[SEM, SEM] + [HBM] * (2 * n) + [pl.BlockSpec(memory_space=pltpu.VMEM)],
        out_shape=[pltpu.SemaphoreType.DMA((n,)), pltpu.SemaphoreType.DMA((n,))]
        + [pltpu.HBM(g.shape, g.dtype) for g in grads]
        + [pltpu.HBM(shp, g.dtype) for shp, g in zip(halves, grads)]
        + [jax.ShapeDtypeStruct((8, LANES), F32)],
        input_output_aliases={i: 2 + i for i in range(2 * n)},
        compiler_params=pltpu.CompilerParams(has_side_effects=SPLIT_COPY),
    )(*[pltpu.with_memory_space_constraint(g, pltpu.HBM) for g in grads],
      *[pltpu.with_memory_space_constraint(lax.empty(shp, g.dtype), pltpu.HBM) for shp, g in zip(halves, grads)])
    return res[0], res[1], list(res[2:2 + n]), list(res[2 + n:2 + 2 * n]), res[-1]


def _pair_wait(name, send_sem, recv_sem, grads, lands, after):
    n = len(grads)

    def body(*refs):
        srcs, land_refs = refs[:n], refs[n:2 * n]
        send_ref, recv_ref = refs[2 * n], refs[2 * n + 1]
        for i in range(n):
            copy = _pair_copy(srcs, land_refs, send_ref, recv_ref, i)
            copy.wait_send()
            copy.wait_recv()

    res = pl.pallas_call(
        body,
        name=name,
        in_specs=[HBM] * (2 * n) + [SEM, SEM, ANY],
        out_specs=[HBM] * (2 * n),
        out_shape=[pltpu.HBM(g.shape, g.dtype) for g in grads] + [pltpu.HBM(l.shape, l.dtype) for l in lands],
        input_output_aliases={i: i for i in range(2 * n)},
        compiler_params=pltpu.CompilerParams(has_side_effects=SPLIT_COPY),
    )(*grads, *lands, send_sem, recv_sem, after)
    return list(res[:n]), list(res[n:])


def _scatter_copy(srcs, lands, send_sem, recv_sem, i, j):
    x, y, c = _position()
    chips = _other_chips(x, y)
    return pltpu.make_async_remote_copy(
        src_ref=srcs[i].at[2 * chips[j][0] + chips[j][1]], dst_ref=lands[i].at[j],
        send_sem=send_sem.at[3 * i + j], recv_sem=recv_sem.at[3 * i + j],
        device_id=(chips[j][0], chips[j][1], c), device_id_type=MESH)


def _scatter_start(name, sums):
    n = len(sums)

    def body(*refs):
        srcs, lands = refs[:n], refs[n:2 * n]
        send_sem, recv_sem = refs[2 * n], refs[2 * n + 1]
        token = refs[-1]
        for i in range(n):
            for j in range(3):
                _scatter_copy(srcs, lands, send_sem, recv_sem, i, j).start()
        token[...] = jnp.zeros_like(token)

    land_shapes = [(3,) + s.shape[1:] for s in sums]
    res = pl.pallas_call(
        body,
        name=name,
        in_specs=[HBM] * (2 * n),
        out_specs=[SEM, SEM] + [HBM] * (2 * n) + [pl.BlockSpec(memory_space=pltpu.VMEM)],
        out_shape=[pltpu.SemaphoreType.DMA((3 * n,)), pltpu.SemaphoreType.DMA((3 * n,))]
        + [pltpu.HBM(s.shape, s.dtype) for s in sums]
        + [pltpu.HBM(shp, s.dtype) for shp, s in zip(land_shapes, sums)]
        + [jax.ShapeDtypeStruct((8, LANES), F32)],
        input_output_aliases={i: 2 + i for i in range(2 * n)},
        compiler_params=pltpu.CompilerParams(has_side_effects=SPLIT_COPY),
    )(*[pltpu.with_memory_space_constraint(s, pltpu.HBM) for s in sums],
      *[pltpu.with_memory_space_constraint(lax.empty(shp, s.dtype), pltpu.HBM) for shp, s in zip(land_shapes, sums)])
    return res[0], res[1], list(res[2:2 + n]), list(res[2 + n:2 + 2 * n]), res[-1]


def _scatter_wait(name, send_sem, recv_sem, sums, lands, after):
    n = len(sums)

    def body(*refs):
        srcs, land_refs = refs[:n], refs[n:2 * n]
        send_ref, recv_ref = refs[2 * n], refs[2 * n + 1]
        for i in range(n):
            for j in range(3):
                copy = _scatter_copy(srcs, land_refs, send_ref, recv_ref, i, j)
                copy.wait_send()
                copy.wait_recv()

    res = pl.pallas_call(
        body,
        name=name,
        in_specs=[HBM] * (2 * n) + [SEM, SEM, ANY],
        out_specs=[HBM] * (2 * n),
        out_shape=[pltpu.HBM(s.shape, s.dtype) for s in sums] + [pltpu.HBM(l.shape, l.dtype) for l in lands],
        input_output_aliases={i: i for i in range(2 * n)},
        compiler_params=pltpu.CompilerParams(has_side_effects=SPLIT_COPY),
    )(*sums, *lands, send_sem, recv_sem, after)
    return list(res[:n]), list(res[n:])


def _pair_join(name, halves, small=None):
    n = len(halves)
    if small is None:
        def body_plain(*refs):
            ins, outs = refs[:n], refs[n:2 * n]
            send_sem, recv_sem = refs[2 * n:]
            x, y, c = _position()
            swaps = [pltpu.make_async_remote_copy(
                src_ref=ins[i], dst_ref=outs[i], send_sem=send_sem.at[i], recv_sem=recv_sem.at[i],
                device_id=(x, y, 1 - c), device_id_type=MESH) for i in range(n)]
            for swap in swaps:
                swap.start()
            for swap in swaps:
                swap.wait()

        return pl.pallas_call(
            body_plain,
            name=name,
            in_specs=[ANY] * n,
            out_specs=[ANY] * n,
            out_shape=[jax.ShapeDtypeStruct(h.shape, h.dtype) for h in halves],
            scratch_shapes=[pltpu.SemaphoreType.DMA((n,))] * 2,
        )(*halves)

    def body(*refs):
        ins, small_ref = refs[:n], refs[n]
        outs, all_ref = refs[n + 1:2 * n + 1], refs[2 * n + 1]
        send_sem, recv_sem, sm_send, sm_recv, sm_local = refs[2 * n + 2:]
        x, y, c = _position()
        swaps = []
        for i in range(n):
            swap = pltpu.make_async_remote_copy(
                src_ref=ins[i], dst_ref=outs[i], send_sem=send_sem.at[i], recv_sem=recv_sem.at[i],
                device_id=(x, y, 1 - c), device_id_type=MESH)
            swap.start()
            swaps.append(swap)
        me = 4 * x + 2 * y + c
        sm_own = pltpu.make_async_copy(small_ref, all_ref.at[me], sm_local)
        sm_own.start()
        pushes, arrivals = [], []
        for mask in range(1, N_DEV):
            px, py, pc = x ^ (mask >> 2), y ^ ((mask >> 1) & 1), c ^ (mask & 1)
            pushes.append(pltpu.make_async_remote_copy(
                src_ref=small_ref, dst_ref=all_ref.at[me], send_sem=sm_send.at[mask - 1], recv_sem=sm_recv.at[mask - 1],
                device_id=(px, py, pc), device_id_type=MESH))
            arrivals.append(pltpu.make_async_remote_copy(
                src_ref=small_ref, dst_ref=all_ref.at[4 * px + 2 * py + pc], send_sem=sm_send.at[mask - 1],
                recv_sem=sm_recv.at[mask - 1], device_id=(px, py, pc), device_id_type=MESH))
        for cp in pushes:
            cp.start()
        for swap in swaps:
            swap.wait()
        for cp in arrivals:
            cp.wait_recv()
        for cp in pushes:
            cp.wait_send()
        sm_own.wait()

    res = pl.pallas_call(
        body,
        name=name,
        in_specs=[ANY] * (n + 1),
        out_specs=[ANY] * (n + 1),
        out_shape=[jax.ShapeDtypeStruct(h.shape, h.dtype) for h in halves]
        + [jax.ShapeDtypeStruct((N_DEV,) + small.shape, small.dtype)],
        scratch_shapes=[pltpu.SemaphoreType.DMA((n,))] * 2 + [pltpu.SemaphoreType.DMA((N_DEV - 1,))] * 2
        + [pltpu.SemaphoreType.DMA(())],
    )(*halves, small)
    return res[:n], res[n]


def _lower_bound(lbp):
    return jax.nn.softmax(lbp, axis=0)[0:1]


def _local_step(x, target, g1, gm, g2, gq, gk, go, rel_bias, lbp, first_weights, mid_weights, last_weights, on_grads, grads_sent):
    b, s, d = x.shape
    t = b * s
    x0 = x.reshape(t, d)
    tgt = target.reshape(t, d)
    gq_t = jnp.tile(gq, (1, ATTN_HEADS))
    gk_t = jnp.tile(gk, (1, ATTN_HEADS))
    lb = _lower_bound(lbp)
    bias = _rel_bias_table("rel_bias_table", rel_bias)

    h1 = _rmsnorm_fwd("norm1", x0, g1)
    wg1, wu1, deps1 = first_weights(h1)
    a1, b1, z1 = _ffn_up("ffn1_up", h1, wg1, wu1, deps1)
    wd1, w_in, w_out = mid_weights(z1)
    ns = w_in.shape[0]
    x1 = _ffn_down("ffn1_down", z1, wd1, x0)
    h2 = _rmsnorm_fwd("norm_mix", x1, gm)
    proj = _in_proj("in_proj", h2, w_in)
    proj3 = proj.reshape(b, s, proj.shape[1])
    table = _band_table(bias)
    qn, kn, vb = _qk_prep("qk_prep", proj3, gq_t, gk_t)
    attn = _attn_fwd("attn_fwd", qn, kn, vb, table).reshape(t, ATTN_W)
    ro, oraw, states = _hgrn_fwd("hgrn_fwd", proj, lb, go, b, s)
    mix = jnp.concatenate([attn, ro], axis=1)
    x2 = _out_proj("out_proj", mix, w_out, x1)
    h3 = _rmsnorm_fwd("norm2", x2, g2)
    wg2, wu2, wd2 = last_weights(h3)
    a2, b2, z2 = _ffn_up("ffn2_up", h3, wg2, wu2)
    dy, dyh, sq = _ffn_down_loss("ffn2_down_loss", z2, wd2, x2, tgt)
    loss = 0.5 * jnp.sum(sq) / d

    da2, db2 = _ffn_bwd_act("ffn2_bwd_act", dyh, wd2, a2, b2)
    dwd2 = _grad_w_shardrows("ffn2_dwd", z2, dyh)
    dwg2 = _grad_w_shardrows("ffn2_dwg", da2, h3)
    dwu2 = _grad_w_shardrows("ffn2_dwu", db2, h3)
    sent2 = on_grads("ffn2", {"ffn2_w_gate": dwg2, "ffn2_w_up": dwu2, "ffn2_w_down": dwd2})
    dx2, dx2b, dg2 = _ffn_bwd_in("ffn2_bwd_in", da2, db2, wg2, wu2, x2, g2, dy, 1.0, sent2)
    sent2 = grads_sent("ffn2", dx2b)

    dwout = _grad_w_out("dw_out", mix, dx2b)
    dmix = _out_proj_bwd("out_proj_bwd", dx2b, w_out, sent2)
    dqn, dkn, dvn, dbe, dbo = _attn_bwd("attn_bwd", qn, kn, vb, table, dmix.reshape(b, s, dmix.shape[1]))
    dbias = dbe[:, :, :BAND] + dbo[:, :, CHUNK:]
    dpq, dpk, dpv, dgq, dgk = _qk_prep_bwd("qk_prep_bwd", proj3, dqn, dkn, dvn, gq_t, gk_t)
    dpq, dpk, dpv = (a.reshape(t, ATTN_W) for a in (dpq, dpk, dpv))
    dhq, dhf, dhi, dhg, dlb, dgo = _hgrn_bwd("hgrn_bwd", proj, lb, go, oraw, states, dmix, b, s)
    dproj = jnp.concatenate([dpq, dpk, dpv, dhq, dhf, dhi, dhg], axis=1)
    dwin = _grad_w_in("dw_in", h2, dproj, ns)
    dx1, dx1h, dgm = _in_proj_bwd("in_proj_bwd", dproj, w_in, x1, gm, dx2, 0.5)

    dwd1 = _grad_w_shardrows("ffn1_dwd", z1, dx1h)
    sent_mix = on_grads("mix", {"w_in": dwin, "w_out": dwout.reshape(ns, dwout.shape[0] // ns, d),
                                "ffn1_w_down": dwd1})
    da1, db1 = _ffn_bwd_act("ffn1_bwd_act", dx1h, wd1, a1, b1, sent_mix)
    sent_mix = grads_sent("mix", da1)
    dwg1 = _grad_w_shardrows("ffn1_dwg", da1, h1, sent_mix)
    dwu1 = _grad_w_shardrows("ffn1_dwu", db1, h1)
    on_grads("ffn1", {"ffn1_w_gate": dwg1, "ffn1_w_up": dwu1})
    sent1 = grads_sent("ffn1", dwu1)
    dx0, dg1 = _ffn_bwd_in("ffn1_bwd_in", da1, db1, wg1, wu1, x0, g1, dx1, None, sent1)

    nt = dg1.shape[0]
    sg = _small_grads(
        "small_grads", dg1.reshape(nt, d), dgm.reshape(nt, d), dg2.reshape(nt, d),
        dgq.reshape(-1, ATTN_W), dgk.reshape(-1, ATTN_W), dbias.transpose(1, 0, 2),
        dlb.reshape(b, HGRN_W), dgo.reshape(b, HGRN_W), lbp)
    g1g, gmg, g2g, gqg, gkg, rbg, lbg, gog = sg
    small = _pack_small(g1g, gmg, g2g, lbg, rbg[:, :N_REL], gqg, gkg, gog)
    return loss, dx0.reshape(b, s, d), small


def _pack_small(g1, gm, g2, lbp, rel_bias, gq, gk, go):
    flat = [g1.reshape(-1), gm.reshape(-1), g2.reshape(-1), lbp.reshape(-1), rel_bias.reshape(-1)]
    n_bias = 3 * SMALL_COLS - rel_bias.size
    heads = [gq.reshape(-1), gk.reshape(-1), go.reshape(-1)]
    n_tail = SMALL_COLS - sum(h.size for h in heads)
    return jnp.concatenate(flat + [jnp.zeros((n_bias,), F32)] + heads + [jnp.zeros((n_tail,), F32)]).reshape(
        SMALL_ROWS, SMALL_COLS)


def _unpack_small(p, d):
    flat = p.reshape(-1)
    o = 3 * d
    g1, gm, g2 = p[0:1], p[1:2], p[2:3]
    lbp = flat[o:o + 2 * HGRN_W].reshape(2, HGRN_W)
    o = 4 * SMALL_COLS
    rel = flat[o:o + ATTN_HEADS * N_REL].reshape(1, ATTN_HEADS, N_REL)
    o = 7 * SMALL_COLS
    gq = flat[o:o + ATTN_DH].reshape(1, ATTN_DH)
    gk = flat[o + ATTN_DH:o + 2 * ATTN_DH].reshape(1, ATTN_DH)
    go = flat[o + 2 * ATTN_DH:o + 2 * ATTN_DH + HGRN_DH].reshape(1, HGRN_DH)
    return g1, gm, g2, gq, gk, rel, lbp, go


def kernel(x, ffn1_norm_g, ffn1_w_gate, ffn1_w_up, ffn1_w_down, mix_norm_g, w_in, attn_q_norm_g, attn_k_norm_g, attn_rel_bias, hgrn_lower_bounds, hgrn_out_norm_g, w_out, ffn2_norm_g, ffn2_w_gate, ffn2_w_up, ffn2_w_down, loss_target, m_ffn1_norm_g, m_ffn1_w_gate, m_ffn1_w_up, m_ffn1_w_down, m_mix_norm_g, m_w_in, m_attn_q_norm_g, m_attn_k_norm_g, m_attn_rel_bias, m_hgrn_lower_bounds, m_hgrn_out_norm_g, m_w_out, m_ffn2_norm_g, m_ffn2_w_gate, m_ffn2_w_up, m_ffn2_w_down, v_ffn1_norm_g, v_ffn1_w_gate, v_ffn1_w_up, v_ffn1_w_down, v_mix_norm_g, v_w_in, v_attn_q_norm_g, v_attn_k_norm_g, v_attn_rel_bias, v_hgrn_lower_bounds, v_hgrn_out_norm_g, v_w_out, v_ffn2_norm_g, v_ffn2_w_gate, v_ffn2_w_up, v_ffn2_w_down):
    d = x.shape[-1]
    big_w = [ffn1_w_gate, ffn1_w_up, ffn1_w_down, w_in, w_out, ffn2_w_gate, ffn2_w_up, ffn2_w_down]
    big_m = [m_ffn1_w_gate, m_ffn1_w_up, m_ffn1_w_down, m_w_in, m_w_out, m_ffn2_w_gate, m_ffn2_w_up, m_ffn2_w_down]
    big_v = [v_ffn1_w_gate, v_ffn1_w_up, v_ffn1_w_down, v_w_in, v_w_out, v_ffn2_w_gate, v_ffn2_w_up, v_ffn2_w_down]
    big_names = ["ffn1_w_gate", "ffn1_w_up", "ffn1_w_down", "w_in", "w_out", "ffn2_w_gate", "ffn2_w_up", "ffn2_w_down"]
    flipped = {nm for nm in big_names if nm.endswith("gate") or nm.endswith("up")}
    flip = lambda nm, a: jnp.swapaxes(a, 1, 2) if nm in flipped else a
    big_w, big_m, big_v = ([flip(nm, a) for nm, a in zip(big_names, arrs)] for arrs in (big_w, big_m, big_v))

    shards = [w[0].astype(BF16) for w in big_w]
    start_a = _gather_start("gather_start_up1", shards[:2], ())
    start_b = _gather_start("gather_start_mid", shards[2:5], (start_a[4],))
    start_c = _gather_start("gather_start_ffn2", shards[5:], (start_b[4],))

    def gathered(tag, started, after):
        send_sem, recv_sem, srcs, outs, _ = started
        srcs, outs = _gather_wait("gather_wait_" + tag, send_sem, recv_sem, srcs, outs, after)
        return _gather_join("gather_join_" + tag, srcs, outs)

    def first_weights(after):
        return (*gathered("up1", start_a, after), (start_c[4],))

    def mid_weights(after):
        wd1, win_f, wout_f = gathered("mid", start_b, after)
        return wd1, win_f, wout_f.reshape(wout_f.shape[0] * wout_f.shape[1], d)

    def last_weights(after):
        return gathered("ffn2", start_c, after)

    core = lax.axis_index("c").astype(jnp.int32).reshape(1)
    chip = (2 * lax.axis_index("x") + lax.axis_index("y")).astype(jnp.int32).reshape(1)
    started = {}

    def on_grads(tag, grads):
        names = list(grads)
        started[tag] = (names, _pair_start("pair_start_" + tag, [grads[nm] for nm in names]))
        return (started[tag][1][4],)

    def grads_sent(tag, after):
        names, (send_sem, recv_sem, grads, lands, _) = started[tag]
        grads, theirs = _pair_wait("pair_wait_" + tag, send_sem, recv_sem, grads, lands, after)
        sums = [_pair_sum("pair_sum_" + nm, g, th, core) for nm, g, th in zip(names, grads, theirs)]
        started[tag] = (names, _scatter_start("scatter_start_" + tag, sums))
        return (started[tag][1][4],)

    loss, grad_x, small_g = _local_step(
        x, loss_target, ffn1_norm_g, mix_norm_g, ffn2_norm_g, attn_q_norm_g, attn_k_norm_g, hgrn_out_norm_g,
        attn_rel_bias[0], hgrn_lower_bounds, first_weights, mid_weights, last_weights, on_grads, grads_sent)
    loss = lax.psum(loss, ("x", "y", "c"))

    def finish(tag, after):
        names, (send_sem, recv_sem, sums, lands, _) = started[tag]
        sums, lands = _scatter_wait("scatter_wait_" + tag, send_sem, recv_sem, sums, lands, after)
        return names, [_chip_sum("chip_sum_" + nm, sm, ld, chip) for nm, sm, ld in zip(names, sums, lands)]

    by_name = {nm: (w, m, v) for nm, w, m, v in zip(big_names, big_w, big_m, big_v)}
    updated = {}

    def update(names, halves, other_halves):
        for nm, mine, theirs in zip(names, halves, other_halves):
            w, m, v = by_name[nm]
            updated[nm] = _adamw("adamw_" + nm, w, mine, theirs, m, v, core)

    last_token = started["ffn1"][1][4]
    names_a, halves_a = finish("ffn2", last_token)
    names_m, halves_m = finish("mix", last_token)
    names_a, halves_a = names_a + names_m, halves_a + halves_m
    update(names_a, halves_a, _pair_join("pair_join_early", halves_a))
    names_b, halves_b = finish("ffn1", updated[names_a[-1]][1])
    others_b, small_all = _pair_join("pair_join_last", halves_b, small_g)
    update(names_b, halves_b, others_b)
    big_out = [updated[nm] for nm in big_names]

    pack = lambda g1, gm, g2, gq, gk, rel, lbp, go: _pack_small(g1, gm, g2, lbp, rel[0], gq, gk, go)
    small_w = pack(ffn1_norm_g, mix_norm_g, ffn2_norm_g, attn_q_norm_g, attn_k_norm_g, attn_rel_bias, hgrn_lower_bounds, hgrn_out_norm_g)
    small_m = pack(m_ffn1_norm_g, m_mix_norm_g, m_ffn2_norm_g, m_attn_q_norm_g, m_attn_k_norm_g, m_attn_rel_bias, m_hgrn_lower_bounds, m_hgrn_out_norm_g)
    small_v = pack(v_ffn1_norm_g, v_mix_norm_g, v_ffn2_norm_g, v_attn_q_norm_g, v_attn_k_norm_g, v_attn_rel_bias, v_hgrn_lower_bounds, v_hgrn_out_norm_g)
    small_out = [_unpack_small(p, d) for p in _adamw_small("adamw_small", small_w, small_all, small_m, small_v)]

    def assemble(kind):
        bg = [flip(nm, o[kind]) for nm, o in zip(big_names, big_out)]
        g1, gm, g2, gq, gk, rel, lbp, go = small_out[kind]
        return [g1, bg[0], bg[1], bg[2], gm, bg[3], gq, gk, rel, lbp, go, bg[4], g2, bg[5], bg[6], bg[7]]

    return (loss, grad_x, *assemble(0), *assemble(1), *assemble(2), *assemble(3))
```

```python
import functools

import jax
import jax.numpy as jnp
from jax import lax
from jax.experimental import pallas as pl
from jax.experimental.pallas import tpu as pltpu

F32 = jnp.float32
BF16 = jnp.bfloat16
MESH = pl.DeviceIdType.MESH

N_CHIPS = 4
N_DEV = 8
CHUNK = 64
ATTN_HEADS = 8
ATTN_DH = 64
ATTN_W = ATTN_HEADS * ATTN_DH
HGRN_HEADS = 4
HGRN_DH = 128
HGRN_W = HGRN_HEADS * HGRN_DH
LEFT_CHUNKS = 8
BAND = (LEFT_CHUNKS + 1) * CHUNK
KPAD = LEFT_CHUNKS * CHUNK
REL_CLIP = 128
N_REL = 2 * REL_CLIP + 1
N_REL_PAD = 384
RMS_EPS = 1e-6
LANES = 128
SMALL_ROWS = 8
SMALL_COLS = 1024

ADAM_LR = 0.001
ADAM_B1 = 0.9
ADAM_B2 = 0.999
ADAM_EPS = 1e-08
ADAM_WD = 0.01
ADAM_STEP = 10

NN = (((1,), (0,)), ((), ()))
NT = (((1,), (1,)), ((), ()))
TN = (((0,), (0,)), ((), ()))

VMEM_LIMIT = 48 * 1024 * 1024


def _sigmoid(x):
    return 1.0 / (1.0 + jnp.exp(-x))


def _silu(x):
    return x * _sigmoid(x)


def _dot(a, b, dims=NN):
    return lax.dot_general(a, b, dims, preferred_element_type=F32)


def _split3(x):
    hi = x.astype(BF16)
    r1 = x - hi.astype(F32)
    mid = r1.astype(BF16)
    lo = (r1 - mid.astype(F32)).astype(BF16)
    return hi, mid, lo


def _dot_exact_rhs(x, mat, dims=NN):
    hi, mid, lo = _split3(x)
    return _dot(hi, mat, dims) + _dot(mid, mat, dims) + _dot(lo, mat, dims)


def _dot_exact_lhs(mat, x, dims=NN):
    hi, mid, lo = _split3(x)
    return _dot(mat, hi, dims) + _dot(mat, mid, dims) + _dot(mat, lo, dims)


def _params(*sem):
    return pltpu.CompilerParams(dimension_semantics=sem, vmem_limit_bytes=VMEM_LIMIT)


def _mm(name, ins, terms, n_acc, grid, acc_shape, outs, epilogue, extras=(), deps=()):
    nk = grid[2]
    ni, ne, nd, no = len(ins), len(extras), len(deps), len(outs)

    def body(*refs):
        in_refs = refs[:ni]
        ex_refs = refs[ni:ni + ne]
        out_refs = refs[ni + ne + nd:ni + ne + nd + no]
        acc_refs = refs[ni + ne + nd + no:]
        parts = [None] * n_acc
        for ai, li, ri, dims in terms:
            d = _dot(in_refs[li][...], in_refs[ri][...], dims)
            parts[ai] = d if parts[ai] is None else parts[ai] + d

        def finish(accs):
            res = epilogue(accs, [e[...] for e in ex_refs])
            for o, r in zip(out_refs, res):
                o[...] = r.astype(o.dtype)

        if nk == 1:
            finish(parts)
        else:
            k = pl.program_id(2)

            @pl.when(k == 0)
            def _():
                for a, p in zip(acc_refs, parts):
                    a[...] = p

            @pl.when(k > 0)
            def _():
                for a, p in zip(acc_refs, parts):
                    a[...] += p

            @pl.when(k == nk - 1)
            def _():
                finish([a[...] for a in acc_refs])

    scratch = [] if nk == 1 else [pltpu.VMEM(acc_shape, F32) for _ in range(n_acc)]
    res = pl.pallas_call(
        body,
        name=name,
        grid=grid,
        in_specs=[s for _, s in ins] + [s for _, s in extras] + [pl.BlockSpec(memory_space=pl.ANY)] * nd,
        out_specs=[s for _, s in outs],
        out_shape=[o for o, _ in outs],
        scratch_shapes=scratch,
        compiler_params=_params("parallel", "parallel", "arbitrary"),
    )(*[a for a, _ in ins], *[a for a, _ in extras], *deps)
    return res


def _mm_rows(name, lhs, weights, dims, t, outs, epilogue, extras=(), deps=()):
    tm = _row_tile(t)
    nl, ne, nd, no = len(lhs), len(extras), len(deps), len(outs)
    ns = weights[0].shape[0]

    def body(*refs):
        lhs_refs = refs[:nl]
        w_hbm = refs[nl:2 * nl]
        ex_refs = refs[2 * nl:2 * nl + ne]
        out_refs = refs[2 * nl + ne + nd:2 * nl + ne + nd + no]
        w_vmem = refs[2 * nl + ne + nd + no:3 * nl + ne + nd + no]
        sem = refs[-1]

        @pl.when(pl.program_id(0) == 0)
        def _():
            copies = [pltpu.make_async_copy(w_hbm[p], w_vmem[p], sem.at[p]) for p in range(nl)]
            for cp in copies:
                cp.start()
            for cp in copies:
                cp.wait()

        acc = None
        for p in range(nl):
            pick = lhs[p][2]
            for j in range(ns):
                part = _dot(pick(lhs_refs[p], j), w_vmem[p][j], dims)
                acc = part if acc is None else acc + part
        res = epilogue([acc], [e[...] for e in ex_refs])
        for o, r in zip(out_refs, res):
            o[...] = r.astype(o.dtype)

    return pl.pallas_call(
        body,
        name=name,
        grid=(t // tm,),
        in_specs=[s for _, s, _ in lhs] + [pl.BlockSpec(memory_space=pl.ANY)] * nl + [s for _, s in extras]
        + [pl.BlockSpec(memory_space=pl.ANY)] * nd,
        out_specs=[s for _, s in outs],
        out_shape=[o for o, _ in outs],
        scratch_shapes=[pltpu.VMEM(w.shape, w.dtype) for w in weights] + [pltpu.SemaphoreType.DMA((nl,))],
        compiler_params=_params("arbitrary"),
    )(*[a for a, _, _ in lhs], *weights, *[a for a, _ in extras], *deps)


def _row_tile(t):
    return 512 if t % 512 == 0 else t


def _k_tile(t):
    return t if t <= 4096 else 1024


def _rmsnorm_fwd(name, x, g):
    t, d = x.shape
    tm = _row_tile(t)

    def body(x_ref, g_ref, h_ref):
        xv = x_ref[...]
        ms = jnp.mean(xv * xv, axis=-1, keepdims=True)
        h_ref[...] = (xv * lax.rsqrt(ms + RMS_EPS) * g_ref[...]).astype(BF16)

    return pl.pallas_call(
        body,
        name=name,
        grid=(t // tm,),
        in_specs=[pl.BlockSpec((tm, d), lambda i: (i, 0)), pl.BlockSpec((1, d), lambda i: (0, 0))],
        out_specs=pl.BlockSpec((tm, d), lambda i: (i, 0)),
        out_shape=jax.ShapeDtypeStruct((t, d), BF16),
        compiler_params=_params("parallel"),
    )(x, g)


def _norm_bwd_epilogue(copy_scale):
    def epilogue(accs, ex):
        dh = accs[0]
        xv, g, dres = ex
        ms = jnp.mean(xv * xv, axis=-1, keepdims=True)
        rstd = lax.rsqrt(ms + RMS_EPS)
        xhat = xv * rstd
        dxhat = dh * g
        dx = rstd * (dxhat - xhat * jnp.mean(dxhat * xhat, axis=-1, keepdims=True))
        out = dres + dx
        dg = jnp.sum(dh * xhat, axis=0, keepdims=True)
        if copy_scale is None:
            return out, dg
        return out, out * copy_scale, dg

    return epilogue


def _ffn_up(name, h, wg, wu, deps=()):
    t, d = h.shape
    ns, f, _ = wg.shape
    tm = _row_tile(t)

    def epilogue(accs, ex):
        a, b = accs
        sg = _sigmoid(a)
        act = a * sg
        return act, b * (sg * (1.0 + a * (1.0 - sg))), act * b

    w_spec = pl.BlockSpec((None, f, d), lambda j, i, k: (j, 0, 0))
    o_spec = pl.BlockSpec((None, tm, f), lambda j, i, k: (j, i, 0))
    o_shape = jax.ShapeDtypeStruct((ns, t, f), BF16)
    return _mm(
        name,
        ins=[(h, pl.BlockSpec((tm, d), lambda j, i, k: (i, 0))), (wg, w_spec), (wu, w_spec)],
        terms=[(0, 0, 1, NT), (1, 0, 2, NT)],
        n_acc=2,
        grid=(ns, t // tm, 1),
        acc_shape=(tm, f),
        outs=[(o_shape, o_spec)] * 3,
        epilogue=epilogue,
        deps=deps,
    )


def _shard_rows(arr, tm):
    ns, _, f = arr.shape
    return arr, pl.BlockSpec((ns, tm, f), lambda i: (0, i, 0)), lambda ref, j: ref[j]


def _ffn_down(name, z, wd, x):
    _, t, _ = z.shape
    d = wd.shape[2]
    tm = _row_tile(t)
    row = pl.BlockSpec((tm, d), lambda i: (i, 0))
    return _mm_rows(
        name, [_shard_rows(z, tm)], [wd], NN, t,
        outs=[(jax.ShapeDtypeStruct((t, d), F32), row)],
        epilogue=lambda accs, ex: (ex[0] + 0.5 * accs[0],),
        extras=[(x, row)],
    )[0]


def _ffn_down_loss(name, z, wd, x, target):
    _, t, _ = z.shape
    d = wd.shape[2]
    tm = _row_tile(t)
    nt = t // tm
    row = pl.BlockSpec((tm, d), lambda i: (i, 0))

    def epilogue(accs, ex):
        e = ex[0] + 0.5 * accs[0] - ex[1]
        dy = e * (1.0 / d)
        return dy, 0.5 * dy, jnp.sum(e * e, axis=0, keepdims=True)

    return _mm_rows(
        name, [_shard_rows(z, tm)], [wd], NN, t,
        outs=[(jax.ShapeDtypeStruct((t, d), F32), row), (jax.ShapeDtypeStruct((t, d), BF16), row),
              (jax.ShapeDtypeStruct((nt, 1, d), F32), pl.BlockSpec((None, 1, d), lambda i: (i, 0, 0)))],
        epilogue=epilogue,
        extras=[(x, row), (target, row)],
    )


def _ffn_bwd_act(name, dout, wd, act_a, dact_b, deps=()):
    t, d = dout.shape
    ns, f, _ = wd.shape
    tm = _row_tile(t)

    def epilogue(accs, ex):
        dz = accs[0]
        return dz * ex[1].astype(F32), dz * ex[0].astype(F32)

    act = pl.BlockSpec((None, tm, f), lambda j, i, k: (j, i, 0))
    o_shape = jax.ShapeDtypeStruct((ns, t, f), BF16)
    return _mm(
        name,
        ins=[(dout, pl.BlockSpec((tm, d), lambda j, i, k: (i, 0))),
             (wd, pl.BlockSpec((None, f, d), lambda j, i, k: (j, 0, 0)))],
        terms=[(0, 0, 1, NT)],
        n_acc=1,
        grid=(ns, t // tm, 1),
        acc_shape=(tm, f),
        outs=[(o_shape, act)] * 2,
        epilogue=epilogue,
        extras=[(act_a, act), (dact_b, act)],
        deps=deps,
    )


def _grad_w_shardrows(name, z, dout, deps=()):
    ns, t, f = z.shape
    d = dout.shape[1]
    tk = _k_tile(t)
    return _mm(
        name,
        ins=[(z, pl.BlockSpec((None, tk, f), lambda j, n, k: (j, k, 0))),
             (dout, pl.BlockSpec((tk, d), lambda j, n, k: (k, 0)))],
        terms=[(0, 0, 1, TN)],
        n_acc=1,
        grid=(ns, 1, t // tk),
        acc_shape=(f, d),
        outs=[(jax.ShapeDtypeStruct((ns, f, d), F32), pl.BlockSpec((None, f, d), lambda j, n, k: (j, 0, 0)))],
        epilogue=lambda accs, ex: (accs[0],),
        deps=deps,
    )[0]


def _norm_bwd_outs(t, d, tm, copy_scale):
    row = pl.BlockSpec((tm, d), lambda i: (i, 0))
    outs = [(jax.ShapeDtypeStruct((t, d), F32), row)]
    if copy_scale is not None:
        outs.append((jax.ShapeDtypeStruct((t, d), BF16), row))
    outs.append((jax.ShapeDtypeStruct((t // tm, 1, d), F32), pl.BlockSpec((None, 1, d), lambda i: (i, 0, 0))))
    return row, outs


def _ffn_bwd_in(name, da, db, wg, wu, x, g, dres, copy_scale, deps=()):
    _, t, _ = da.shape
    d = wg.shape[2]
    tm = _row_tile(t)
    row, outs = _norm_bwd_outs(t, d, tm, copy_scale)
    return _mm_rows(
        name, [_shard_rows(da, tm), _shard_rows(db, tm)], [wg, wu], NN, t,
        outs=outs,
        epilogue=_norm_bwd_epilogue(copy_scale),
        extras=[(x, row), (g, pl.BlockSpec((1, d), lambda i: (0, 0))), (dres, row)],
        deps=deps,
    )


def _in_proj(name, h, w_in):
    t, d = h.shape
    ns, _, pj = w_in.shape
    tm = _row_tile(t)
    return _mm(
        name,
        ins=[(h, pl.BlockSpec((tm, d), lambda j, i, k: (i, 0))),
             (w_in, pl.BlockSpec((None, d, pj), lambda j, i, k: (j, 0, 0)))],
        terms=[(0, 0, 1, NN)],
        n_acc=1,
        grid=(ns, t // tm, 1),
        acc_shape=(tm, pj),
        outs=[(jax.ShapeDtypeStruct((t, ns * pj), F32), pl.BlockSpec((tm, pj), lambda j, i, k: (i, j)))],
        epilogue=lambda accs, ex: (accs[0],),
    )[0]


def _in_proj_bwd(name, dp, w_in, x, g, dres, copy_scale, deps=()):
    t = dp.shape[0]
    ns, d, pj = w_in.shape
    tm = _row_tile(t)
    row, outs = _norm_bwd_outs(t, d, tm, copy_scale)
    cols = (dp, pl.BlockSpec((tm, ns * pj), lambda i: (i, 0)), lambda ref, j: ref[:, j * pj:(j + 1) * pj])
    return _mm_rows(
        name, [cols], [w_in], NT, t,
        outs=outs,
        epilogue=_norm_bwd_epilogue(copy_scale),
        extras=[(x, row), (g, pl.BlockSpec((1, d), lambda i: (0, 0))), (dres, row)],
        deps=deps,
    )


def _grad_w_in(name, h, dp, ns):
    t, d = h.shape
    pj = dp.shape[1] // ns
    tk = _k_tile(t)
    return _mm(
        name,
        ins=[(h, pl.BlockSpec((tk, d), lambda j, n, k: (k, 0))),
             (dp, pl.BlockSpec((tk, pj), lambda j, n, k: (k, j)))],
        terms=[(0, 0, 1, TN)],
        n_acc=1,
        grid=(ns, 1, t // tk),
        acc_shape=(d, pj),
        outs=[(jax.ShapeDtypeStruct((ns, d, pj), F32), pl.BlockSpec((None, d, pj), lambda j, n, k: (j, 0, 0)))],
        epilogue=lambda accs, ex: (accs[0],),
    )[0]


def _out_proj(name, mix, w_out, x):
    t, dm = mix.shape
    d = w_out.shape[1]
    tm = _row_tile(t)
    row = pl.BlockSpec((tm, d), lambda i, n, k: (i, 0))
    return _mm(
        name,
        ins=[(mix, pl.BlockSpec((tm, dm), lambda i, n, k: (i, 0))),
             (w_out, pl.BlockSpec((dm, d), lambda i, n, k: (0, 0)))],
        terms=[(0, 0, 1, NN)],
        n_acc=1,
        grid=(t // tm, 1, 1),
        acc_shape=(tm, d),
        outs=[(jax.ShapeDtypeStruct((t, d), F32), row)],
        epilogue=lambda accs, ex: (ex[0] + accs[0],),
        extras=[(x, row)],
    )[0]


def _out_proj_bwd(name, dx, w_out, deps=()):
    t, d = dx.shape
    dm = w_out.shape[0]
    tm = _row_tile(t)
    return _mm(
        name,
        ins=[(dx, pl.BlockSpec((tm, d), lambda i, n, k: (i, 0))),
             (w_out, pl.BlockSpec((dm, d), lambda i, n, k: (0, 0)))],
        terms=[(0, 0, 1, NT)],
        n_acc=1,
        grid=(t // tm, 1, 1),
        acc_shape=(tm, dm),
        outs=[(jax.ShapeDtypeStruct((t, dm), F32), pl.BlockSpec((tm, dm), lambda i, n, k: (i, 0)))],
        epilogue=lambda accs, ex: (accs[0],),
        deps=deps,
    )[0]


def _grad_w_out(name, mix, dx):
    t, dm = mix.shape
    d = dx.shape[1]
    tk = _k_tile(t)
    return _mm(
        name,
        ins=[(mix, pl.BlockSpec((tk, dm), lambda a, n, k: (k, 0))),
             (dx, pl.BlockSpec((tk, d), lambda a, n, k: (k, 0)))],
        terms=[(0, 0, 1, TN)],
        n_acc=1,
        grid=(1, 1, t // tk),
        acc_shape=(dm, d),
        outs=[(jax.ShapeDtypeStruct((dm, d), F32), pl.BlockSpec((dm, d), lambda a, n, k: (0, 0)))],
        epilogue=lambda accs, ex: (accs[0],),
    )[0]


def _head_group_matrix():
    r = lax.broadcasted_iota(jnp.int32, (ATTN_W, ATTN_W), 0)
    c = lax.broadcasted_iota(jnp.int32, (ATTN_W, ATTN_W), 1)
    same = jnp.right_shift(r, 6) == jnp.right_shift(c, 6)
    return jnp.where(same, 1.0, 0.0).astype(BF16)


def _qk_prep(name, proj, gq, gk):
    b, s, _ = proj.shape
    tm = KPAD
    nb = s // tm

    def body(q_ref, k_ref, v_ref, gq_ref, gk_ref, qn_ref, kn_ref, vb_ref):
        j = pl.program_id(1)
        bd = _head_group_matrix()

        def norm(xv, g):
            ms = _dot_exact_rhs(xv * xv, bd) * (1.0 / ATTN_DH)
            return xv * lax.rsqrt(ms + RMS_EPS) * g

        @pl.when(j == 0)
        def _():
            kn_ref[...] = jnp.zeros_like(kn_ref)
            vb_ref[...] = jnp.zeros_like(vb_ref)

        @pl.when(j > 0)
        def _():
            qn_ref[...] = norm(q_ref[...], gq_ref[...]).astype(BF16)
            kn_ref[...] = norm(k_ref[...], gk_ref[...]).astype(BF16)
            vb_ref[...] = v_ref[...].astype(BF16)

    src_blk = lambda col: pl.BlockSpec((None, tm, ATTN_W), lambda bi, j: (bi, jnp.maximum(j - 1, 0), col))
    gspec = pl.BlockSpec((1, ATTN_W), lambda bi, j: (0, 0))
    padded = pl.BlockSpec((None, tm, ATTN_W), lambda bi, j: (bi, j, 0))
    return pl.pallas_call(
        body,
        name=name,
        grid=(b, nb + 1),
        in_specs=[src_blk(0), src_blk(1), src_blk(2), gspec, gspec],
        out_specs=[src_blk(0), padded, padded],
        out_shape=[jax.ShapeDtypeStruct((b, s, ATTN_W), BF16), jax.ShapeDtypeStruct((b, KPAD + s, ATTN_W), BF16),
                   jax.ShapeDtypeStruct((b, KPAD + s, ATTN_W), BF16)],
        compiler_params=_params("parallel", "arbitrary"),
    )(proj, proj, proj, gq, gk)


def _qk_prep_bwd(name, proj, dqn, dkn, dv, gq, gk):
    b, s, _ = proj.shape
    tm = KPAD
    nb = s // tm

    def body(q_ref, k_ref, dqn_ref, dkn_ref, dv_ref, gq_ref, gk_ref, dq_ref, dk_ref, dvb_ref, dgq_ref, dgk_ref):
        bd = _head_group_matrix()

        def bwd(xv, dy, g):
            ms = _dot_exact_rhs(xv * xv, bd) * (1.0 / ATTN_DH)
            rstd = lax.rsqrt(ms + RMS_EPS)
            xhat = xv * rstd
            dxhat = dy * g
            gm = _dot_exact_rhs(dxhat * xhat, bd) * (1.0 / ATTN_DH)
            return rstd * (dxhat - xhat * gm), jnp.sum(dy * xhat, axis=0, keepdims=True)

        dq, dgq = bwd(q_ref[...], dqn_ref[...], gq_ref[...])
        dk, dgk = bwd(k_ref[...], dkn_ref[...], gk_ref[...])
        dq_ref[...] = dq.astype(BF16)
        dk_ref[...] = dk.astype(BF16)
        dvb_ref[...] = dv_ref[...].astype(BF16)
        dgq_ref[...] = dgq
        dgk_ref[...] = dgk

    col = lambda c: pl.BlockSpec((None, tm, ATTN_W), lambda bi, j: (bi, j, c))
    past_pad = pl.BlockSpec((None, tm, ATTN_W), lambda bi, j: (bi, j + 1, 0))
    gspec = pl.BlockSpec((1, ATTN_W), lambda bi, j: (0, 0))
    pspec = pl.BlockSpec((None, 1, ATTN_W), lambda bi, j: (bi * nb + j, 0, 0))
    o_shape = jax.ShapeDtypeStruct((b, s, ATTN_W), BF16)
    p_shape = jax.ShapeDtypeStruct((b * nb, 1, ATTN_W), F32)
    return pl.pallas_call(
        body,
        name=name,
        grid=(b, nb),
        in_specs=[col(0), col(1), col(0), past_pad, past_pad, gspec, gspec],
        out_specs=[col(0)] * 3 + [pspec] * 2,
        out_shape=[o_shape] * 3 + [p_shape] * 2,
        compiler_params=_params("parallel", "parallel"),
    )(proj, proj, dqn, dkn, dv, gq, gk)


Q_CHUNKS = 4
QBLK = Q_CHUNKS * CHUNK
WIN = (LEFT_CHUNKS + Q_CHUNKS) * CHUNK
DB_W = BAND + CHUNK
MASKED = -1e30


def _band_table(bias):
    rows = [jnp.pad(bias, ((0, 0), (0, 0), (CHUNK * i, WIN - BAND - CHUNK * i)), constant_values=MASKED)
            for i in range(Q_CHUNKS)]
    return jnp.concatenate(rows, axis=1)


def _head_lanes(hh):
    lane = lax.broadcasted_iota(jnp.int32, (1, LANES), 1)
    return (lane < ATTN_DH) if hh == 0 else (lane >= ATTN_DH)


def _attn_probs(qh, kw, table, start):
    s = _dot(qh, kw, NT) * (ATTN_DH ** -0.5) + table
    col = lax.broadcasted_iota(jnp.int32, (QBLK, WIN), 1)
    s = jnp.where(col + start >= KPAD, s, MASKED)
    m = jnp.max(s, axis=-1, keepdims=True)
    p = jnp.exp(s - m)
    return p * (1.0 / jnp.sum(p, axis=-1, keepdims=True))


def _attn_fwd(name, q, k, v, table):
    b, s, w = q.shape
    sp = k.shape[1]

    def body(q_ref, k_ref, v_ref, t_ref, o_ref):
        start = pl.multiple_of(pl.program_id(2) * QBLK, QBLK)
        kw = k_ref[pl.ds(start, WIN), :]
        vw = v_ref[pl.ds(start, WIN), :]
        q2 = q_ref[...]
        out = jnp.zeros((QBLK, LANES), F32)
        for hh in range(2):
            mine = _head_lanes(hh)
            p = _attn_probs(jnp.where(mine, q2, jnp.zeros_like(q2)), kw, t_ref[hh], start)
            out = jnp.where(mine, _dot(p.astype(BF16), vw), out)
        o_ref[...] = out.astype(BF16)

    qspec = pl.BlockSpec((None, QBLK, LANES), lambda p, bi, i: (bi, i, p))
    kspec = pl.BlockSpec((None, sp, LANES), lambda p, bi, i: (bi, 0, p))
    return pl.pallas_call(
        body,
        name=name,
        grid=(w // LANES, b, s // QBLK),
        in_specs=[qspec, kspec, kspec, pl.BlockSpec((2, QBLK, WIN), lambda p, bi, i: (p, 0, 0))],
        out_specs=qspec,
        out_shape=jax.ShapeDtypeStruct((b, s, w), BF16),
        compiler_params=_params("parallel", "parallel", "arbitrary"),
    )(q, k, v, table)


def _attn_bwd(name, q, k, v, table, dmix):
    b, s, w = q.shape
    sp = k.shape[1]

    def body(q_ref, k_ref, v_ref, t_ref, do_ref, dq_ref, dk_ref, dv_ref, dbe_ref, dbo_ref):
        bi = pl.program_id(1)
        i = pl.program_id(2)
        start = pl.multiple_of(i * QBLK, QBLK)
        win = pl.ds(start, WIN)

        @pl.when(i == 0)
        def _():
            dk_ref[...] = jnp.zeros_like(dk_ref)
            dv_ref[...] = jnp.zeros_like(dv_ref)

        @pl.when(jnp.logical_and(i == 0, bi == 0))
        def _():
            dbe_ref[...] = jnp.zeros_like(dbe_ref)
            dbo_ref[...] = jnp.zeros_like(dbo_ref)

        kw = k_ref[win, :]
        vw = v_ref[win, :]
        q2 = q_ref[...]
        do2 = do_ref[...].astype(BF16)
        dq = jnp.zeros((QBLK, LANES), F32)
        dk = dv = None
        for hh in range(2):
            mine = _head_lanes(hh)
            qh = jnp.where(mine, q2, jnp.zeros_like(q2))
            doh = jnp.where(mine, do2, jnp.zeros_like(do2))
            p = _attn_probs(qh, kw, t_ref[hh], start)
            dp = _dot(doh, vw, NT)
            ds = p * (dp - jnp.sum(p * dp, axis=-1, keepdims=True))
            for qi in range(Q_CHUNKS):
                c0 = (qi // 2) * LANES
                blk = ds[qi * CHUNK:(qi + 1) * CHUNK, c0:c0 + DB_W]
                if qi % 2 == 0:
                    dbe_ref[hh] += blk
                else:
                    dbo_ref[hh] += blk
            dsb = (ds * (ATTN_DH ** -0.5)).astype(BF16)
            dq = jnp.where(mine, _dot(dsb, kw), dq)
            dk_h = _dot(dsb, qh, TN)
            dv_h = _dot(p.astype(BF16), doh, TN)
            dk = dk_h if dk is None else dk + dk_h
            dv = dv_h if dv is None else dv + dv_h
        dq_ref[...] = dq
        dk_ref[win, :] += dk
        dv_ref[win, :] += dv

    qspec = pl.BlockSpec((None, QBLK, LANES), lambda p, bi, i: (bi, i, p))
    kspec = pl.BlockSpec((None, sp, LANES), lambda p, bi, i: (bi, 0, p))
    dbspec = pl.BlockSpec((2, CHUNK, DB_W), lambda p, bi, i: (p, 0, 0))
    db_shape = jax.ShapeDtypeStruct((ATTN_HEADS, CHUNK, DB_W), F32)
    return pl.pallas_call(
        body,
        name=name,
        grid=(w // LANES, b, s // QBLK),
        in_specs=[qspec, kspec, kspec, pl.BlockSpec((2, QBLK, WIN), lambda p, bi, i: (p, 0, 0)), qspec],
        out_specs=[qspec, kspec, kspec, dbspec, dbspec],
        out_shape=[jax.ShapeDtypeStruct((b, s, w), F32), jax.ShapeDtypeStruct((b, sp, w), F32),
                   jax.ShapeDtypeStruct((b, sp, w), F32), db_shape, db_shape],
        compiler_params=_params("arbitrary", "arbitrary", "arbitrary"),
    )(q, k, v, table, dmix)


HQ_COL = 3 * ATTN_W // HGRN_DH
HF_COL = HQ_COL + HGRN_HEADS
HI_COL = HF_COL + HGRN_HEADS
HG_COL = HI_COL + HGRN_HEADS
HGRN_ROWS = 8 * CHUNK
HEAD_LANES = [slice(hh * HGRN_DH, (hh + 1) * HGRN_DH) for hh in range(HGRN_HEADS)]


def _tri(lower):
    r = lax.broadcasted_iota(jnp.int32, (CHUNK, CHUNK), 0)
    c = lax.broadcasted_iota(jnp.int32, (CHUNK, CHUNK), 1)
    return (r >= c) if lower else (r <= c)


def _hgrn_chunk(hq, hf, lb, tril):
    sig = _sigmoid(hf)
    f = lb + (1.0 - lb) * sig
    g = jnp.log(f)
    ones_l = jnp.where(tril, 1.0, 0.0).astype(BF16)
    b = _dot_exact_lhs(ones_l, g)
    bl = jnp.sum(g, axis=0, keepdims=True)
    rows = lax.broadcasted_iota(jnp.int32, g.shape, 0)
    bm = jnp.sum(jnp.where(rows <= CHUNK // 2, g, 0.0), axis=0, keepdims=True)
    sq = _sigmoid(hq)
    q = hq * sq
    k = 1.0 - f
    return sig, f, b, bl, bm, sq, q, k


def _hgrn_fwd(name, proj, lb, go, b, s):
    nc = s // CHUNK
    t = b * s
    nblk = s // HGRN_ROWS
    cpb = HGRN_ROWS // CHUNK

    def body(hq_ref, hf_ref, hi_ref, hg_ref, lb_ref, go_ref, ro_ref, oraw_ref, st_ref, s_scr):
        tril = _tri(True)
        gov = go_ref[...]

        @pl.when(pl.program_id(1) == 0)
        def _():
            s_scr[...] = jnp.zeros_like(s_scr)

        def step(c, carry):
            sl = pl.ds(pl.multiple_of(c * CHUNK, CHUNK), CHUNK)
            hg = hg_ref[sl, :]
            _, _, bb, bl, bm, _, q, k = _hgrn_chunk(hq_ref[sl, :], hf_ref[sl, :], lb_ref[...], tril)
            vb = hi_ref[sl, :].astype(BF16)
            qe = (q * jnp.exp(bb - bm)).astype(BF16)
            ke = (k * jnp.exp(bm - bb)).astype(BF16)
            qb = (q * jnp.exp(bb)).astype(BF16)
            kb = (k * jnp.exp(bl - bb)).astype(BF16)
            e_last = jnp.exp(bl)
            gate = _silu(hg)
            st = [s_scr[hh] for hh in range(HGRN_HEADS)]
            a = [jnp.where(tril, _dot(qe[:, hs], ke[:, hs], NT), 0.0).astype(BF16) for hs in HEAD_LANES]
            o_state = [_dot(qb[:, hs], st[hh].astype(BF16), NT) for hh, hs in enumerate(HEAD_LANES)]
            st_next = [st[hh] * e_last[:, hs] + _dot(vb[:, hs], kb[:, hs], TN) for hh, hs in enumerate(HEAD_LANES)]
            o = [_dot(a[hh], vb[:, hs]) + o_state[hh] for hh, hs in enumerate(HEAD_LANES)]
            ro = [(oh * lax.rsqrt(jnp.mean(oh * oh, axis=-1, keepdims=True) + RMS_EPS) * gov) * gate[:, hs]
                  for oh, hs in zip(o, HEAD_LANES)]
            for hh in range(HGRN_HEADS):
                st_ref[hh, c] = st[hh]
                s_scr[hh] = st_next[hh]
            ro_ref[sl, :] = jnp.concatenate(ro, axis=1).astype(BF16)
            oraw_ref[sl, :] = jnp.concatenate(o, axis=1)
            return carry

        lax.fori_loop(0, cpb, step, 0)

    col = lambda base: pl.BlockSpec((HGRN_ROWS, HGRN_W), lambda bi, i: (bi * nblk + i, base // HGRN_HEADS))
    out = pl.BlockSpec((HGRN_ROWS, HGRN_W), lambda bi, i: (bi * nblk + i, 0))
    return pl.pallas_call(
        body,
        name=name,
        grid=(b, nblk),
        in_specs=[col(HQ_COL), col(HF_COL), col(HI_COL), col(HG_COL),
                  pl.BlockSpec((1, HGRN_W), lambda bi, i: (0, 0)), pl.BlockSpec((1, HGRN_DH), lambda bi, i: (0, 0))],
        out_specs=[out, out,
                   pl.BlockSpec((None, HGRN_HEADS, cpb, HGRN_DH, HGRN_DH), lambda bi, i: (bi, 0, i, 0, 0))],
        out_shape=[jax.ShapeDtypeStruct((t, HGRN_W), BF16), jax.ShapeDtypeStruct((t, HGRN_W), F32),
                   jax.ShapeDtypeStruct((b, HGRN_HEADS, nc, HGRN_DH, HGRN_DH), F32)],
        scratch_shapes=[pltpu.VMEM((HGRN_HEADS, HGRN_DH, HGRN_DH), F32)],
        compiler_params=_params("parallel", "arbitrary"),
    )(proj, proj, proj, proj, lb, go)


def _hgrn_bwd(name, proj, lb, go, oraw, states, dmix, b, s):
    t = b * s
    nblk = s // HGRN_ROWS
    cpb = HGRN_ROWS // CHUNK

    def body(hq_ref, hf_ref, hi_ref, hg_ref, lb_ref, go_ref, oraw_ref, st_ref, dro_ref,
             dhq_ref, dhf_ref, dhi_ref, dhg_ref, dlb_ref, dgo_ref, ds_scr, dlb_scr, dgo_scr):
        tril = _tri(True)
        ones_u = jnp.where(_tri(False), 1.0, 0.0).astype(BF16)
        gov = go_ref[...]

        @pl.when(pl.program_id(1) == 0)
        def _():
            ds_scr[...] = jnp.zeros_like(ds_scr)
            dlb_scr[...] = jnp.zeros_like(dlb_scr)
            dgo_scr[...] = jnp.zeros_like(dgo_scr)

        def step(ci, carry):
            c = cpb - 1 - ci
            sl = pl.ds(pl.multiple_of(c * CHUNK, CHUNK), CHUNK)
            hq = hq_ref[sl, :]
            hg = hg_ref[sl, :]
            sig, f, bb, bl, bm, sq, q, k = _hgrn_chunk(hq, hf_ref[sl, :], lb_ref[...], tril)
            vb = hi_ref[sl, :].astype(BF16)
            ebm = jnp.exp(bb - bm)
            embm = jnp.exp(bm - bb)
            eb = jnp.exp(bb)
            ebl = jnp.exp(bl - bb)
            e_last = jnp.exp(bl)
            qe = (q * ebm).astype(BF16)
            ke = (k * embm).astype(BF16)
            qb = (q * eb).astype(BF16)
            kb = (k * ebl).astype(BF16)
            st = [st_ref[hh, c] for hh in range(HGRN_HEADS)]
            dst = [ds_scr[hh] for hh in range(HGRN_HEADS)]
            o = oraw_ref[sl, :]
            dro = dro_ref[sl, :]
            sg = _sigmoid(hg)
            gov4 = jnp.concatenate([gov] * HGRN_HEADS, axis=1)
            rstd = jnp.concatenate(
                [jnp.broadcast_to(lax.rsqrt(jnp.mean(o[:, hs] * o[:, hs], axis=-1, keepdims=True) + RMS_EPS),
                                  (CHUNK, HGRN_DH)) for hs in HEAD_LANES], axis=1)
            ohat = o * rstd
            dn = dro * (hg * sg)
            dhg = dro * (ohat * gov4) * (sg * (1.0 + hg * (1.0 - sg)))
            dgo_inc = jnp.sum(dn * ohat, axis=0, keepdims=True)
            dohat = dn * gov4
            proj_h = dohat * ohat
            pm = jnp.concatenate(
                [jnp.broadcast_to(jnp.mean(proj_h[:, hs], axis=-1, keepdims=True), (CHUNK, HGRN_DH))
                 for hs in HEAD_LANES], axis=1)
            dob = (rstd * (dohat - ohat * pm)).astype(BF16)
            stb = [x.astype(BF16) for x in st]
            dstb = [x.astype(BF16) for x in dst]
            a = [jnp.where(tril, _dot(qe[:, hs], ke[:, hs], NT), 0.0).astype(BF16) for hs in HEAD_LANES]
            dab = [jnp.where(tril, _dot(dob[:, hs], vb[:, hs], NT), 0.0).astype(BF16) for hs in HEAD_LANES]
            dqb = [_dot(dob[:, hs], stb[hh]) for hh, hs in enumerate(HEAD_LANES)]
            dkb = [_dot(vb[:, hs], dstb[hh]) for hh, hs in enumerate(HEAD_LANES)]
            dv_state = [_dot(kb[:, hs], dstb[hh], NT) for hh, hs in enumerate(HEAD_LANES)]
            dst_next = [dst[hh] * e_last[:, hs] + _dot(dob[:, hs], qb[:, hs], TN) for hh, hs in enumerate(HEAD_LANES)]
            dv = [_dot(a[hh], dob[:, hs], TN) + dv_state[hh] for hh, hs in enumerate(HEAD_LANES)]
            dqe = jnp.concatenate([_dot(dab[hh], ke[:, hs]) for hh, hs in enumerate(HEAD_LANES)], axis=1)
            dke = jnp.concatenate([_dot(dab[hh], qe[:, hs], TN) for hh, hs in enumerate(HEAD_LANES)], axis=1)
            dqb = jnp.concatenate(dqb, axis=1)
            dkb = jnp.concatenate(dkb, axis=1)
            state_term = jnp.concatenate(
                [jnp.sum(dst[hh] * st[hh], axis=0, keepdims=True) for hh in range(HGRN_HEADS)], axis=1)
            dq = dqe * ebm + dqb * eb
            dk = dke * embm + dkb * ebl
            db = (qe.astype(F32) * dqe - ke.astype(F32) * dke) + q * (dqb * eb) - k * (dkb * ebl)
            d_last = jnp.sum(k * ebl * dkb, axis=0, keepdims=True) + state_term * e_last
            dg = _dot_exact_lhs(ones_u, db) + d_last
            df = dg / f - dk
            dhq_ref[sl, :] = (dq * (sq * (1.0 + hq * (1.0 - sq)))).astype(BF16)
            dhf_ref[sl, :] = (df * (1.0 - lb_ref[...]) * sig * (1.0 - sig)).astype(BF16)
            dhi_ref[sl, :] = jnp.concatenate(dv, axis=1).astype(BF16)
            dhg_ref[sl, :] = dhg.astype(BF16)
            dlb_scr[...] += jnp.sum(df * (1.0 - sig), axis=0, keepdims=True)
            dgo_scr[...] += dgo_inc
            for hh in range(HGRN_HEADS):
                ds_scr[hh] = dst_next[hh]
            return carry

        lax.fori_loop(0, cpb, step, 0)

        @pl.when(pl.program_id(1) == nblk - 1)
        def _():
            dlb_ref[...] = dlb_scr[...]
            dgo_ref[...] = dgo_scr[...]

    rows = lambda bi, i: bi * nblk + (nblk - 1 - i)
    col = lambda base: pl.BlockSpec((HGRN_ROWS, HGRN_W), lambda bi, i: (rows(bi, i), base // HGRN_HEADS))
    out = pl.BlockSpec((HGRN_ROWS, HGRN_W), lambda bi, i: (rows(bi, i), 0))
    part = pl.BlockSpec((None, 1, HGRN_W), lambda bi, i: (bi, 0, 0))
    o_shape = jax.ShapeDtypeStruct((t, HGRN_W), BF16)
    p_shape = jax.ShapeDtypeStruct((b, 1, HGRN_W), F32)
    return pl.pallas_call(
        body,
        name=name,
        grid=(b, nblk),
        in_specs=[col(HQ_COL), col(HF_COL), col(HI_COL), col(HG_COL),
                  pl.BlockSpec((1, HGRN_W), lambda bi, i: (0, 0)), pl.BlockSpec((1, HGRN_DH), lambda bi, i: (0, 0)), out,
                  pl.BlockSpec((None, HGRN_HEADS, cpb, HGRN_DH, HGRN_DH), lambda bi, i: (bi, 0, nblk - 1 - i, 0, 0)),
                  col(ATTN_W // HGRN_DH)],
        out_specs=[out] * 4 + [part] * 2,
        out_shape=[o_shape] * 4 + [p_shape] * 2,
        scratch_shapes=[pltpu.VMEM((HGRN_HEADS, HGRN_DH, HGRN_DH), F32), pltpu.VMEM((1, HGRN_W), F32),
                        pltpu.VMEM((1, HGRN_W), F32)],
        compiler_params=_params("parallel", "arbitrary"),
    )(proj, proj, proj, proj, lb, go, oraw, states, dmix)


def _small_grads(name, dg1, dgm, dg2, dgq, dgk, dbias_t, dlb, dgo, lbp):
    d = dg1.shape[1]

    def body(dg1_ref, dgm_ref, dg2_ref, dgq_ref, dgk_ref, dbias_ref, dlb_ref, dgo_ref, lbp_ref,
             g1_ref, gm_ref, g2_ref, gq_ref, gk_ref, rb_ref, lbg_ref, go_ref):
        g1_ref[...] = jnp.sum(dg1_ref[...], axis=0, keepdims=True)
        gm_ref[...] = jnp.sum(dgm_ref[...], axis=0, keepdims=True)
        g2_ref[...] = jnp.sum(dg2_ref[...], axis=0, keepdims=True)
        r = lax.broadcasted_iota(jnp.int32, (ATTN_W, ATTN_DH), 0)
        cidx = lax.broadcasted_iota(jnp.int32, (ATTN_W, ATTN_DH), 1)
        fold = jnp.where(jnp.bitwise_and(r, ATTN_DH - 1) == cidx, 1.0, 0.0).astype(BF16)
        gq_ref[...] = jnp.sum(_dot_exact_rhs(dgq_ref[...], fold), axis=0, keepdims=True)
        gk_ref[...] = jnp.sum(_dot_exact_rhs(dgk_ref[...], fold), axis=0, keepdims=True)
        gosum = jnp.sum(dgo_ref[...], axis=0, keepdims=True)
        go_ref[...] = (gosum[:, 0:HGRN_DH] + gosum[:, HGRN_DH:2 * HGRN_DH]
                       + gosum[:, 2 * HGRN_DH:3 * HGRN_DH] + gosum[:, 3 * HGRN_DH:4 * HGRN_DH])
        p0 = lbp_ref[0:1, :]
        p1 = lbp_ref[1:2, :]
        lbv = 1.0 / (1.0 + jnp.exp(p1 - p0))
        dp0 = jnp.sum(dlb_ref[...], axis=0, keepdims=True) * lbv * (1.0 - lbv)
        lbg_ref[0:1, :] = dp0
        lbg_ref[1:2, :] = -dp0
        sidx = lax.broadcasted_iota(jnp.int32, (BAND, N_REL_PAD), 0)
        ridx = lax.broadcasted_iota(jnp.int32, (BAND, N_REL_PAD), 1)

        def step(tq, acc):
            rel = jnp.clip(tq + KPAD - sidx, -REL_CLIP, REL_CLIP) + REL_CLIP
            onehot = jnp.where(rel == ridx, 1.0, 0.0).astype(BF16)
            return acc + _dot_exact_rhs(dbias_ref[tq], onehot)

        rb_ref[...] = lax.fori_loop(0, CHUNK, step, jnp.zeros((ATTN_HEADS, N_REL_PAD), F32))

    ins = [dg1, dgm, dg2, dgq, dgk, dbias_t, dlb, dgo, lbp]
    outs = [jax.ShapeDtypeStruct((1, d), F32)] * 3 + [jax.ShapeDtypeStruct((1, ATTN_DH), F32)] * 2 + [
        jax.ShapeDtypeStruct((ATTN_HEADS, N_REL_PAD), F32), jax.ShapeDtypeStruct((2, HGRN_W), F32),
        jax.ShapeDtypeStruct((1, HGRN_DH), F32)]
    vm = pl.BlockSpec(memory_space=pltpu.VMEM)
    return pl.pallas_call(
        body,
        name=name,
        in_specs=[vm] * len(ins),
        out_specs=[vm] * len(outs),
        out_shape=outs,
        compiler_params=pltpu.CompilerParams(vmem_limit_bytes=VMEM_LIMIT),
    )(*ins)


def _adam_update(w, g, m, v):
    m2 = ADAM_B1 * m + (1.0 - ADAM_B1) * g
    v2 = ADAM_B2 * v + (1.0 - ADAM_B2) * (g * g)
    m_hat = m2 / (1.0 - ADAM_B1 ** ADAM_STEP)
    v_hat = v2 / (1.0 - ADAM_B2 ** ADAM_STEP)
    delta = -ADAM_LR * (m_hat / (jnp.sqrt(v_hat) + ADAM_EPS) + ADAM_WD * w)
    return delta, m2, v2


def _rows_tile(r):
    for cand in (256, 352, 128, 176, 64, 32, 16):
        if r % cand == 0 and r > cand:
            return cand
    return r


def _pair_sum(name, grad, theirs, core):
    n, half, c = theirs.shape
    tr = _rows_tile(half)
    nth = half // tr

    def body(core_ref, a_ref, b_ref, o_ref):
        o_ref[...] = (a_ref[...] + b_ref[...]).astype(o_ref.dtype)

    spec = pl.BlockSpec((None, tr, c), lambda i, j, core_ref: (i, j, 0))
    return pl.pallas_call(
        body, name=name,
        grid_spec=pltpu.PrefetchScalarGridSpec(
            num_scalar_prefetch=1, grid=(n, nth),
            in_specs=[pl.BlockSpec((None, tr, c), lambda i, j, core_ref: (i, core_ref[0] * nth + j, 0)), spec],
            out_specs=spec),
        out_shape=jax.ShapeDtypeStruct((n, half, c), BF16), compiler_params=_params("parallel", "parallel"),
    )(core, grad, theirs)


def _chip_sum(name, own, parts, chip):
    _, half, c = own.shape
    tr = _rows_tile(half)

    def body(chip_ref, own_ref, p_ref, o_ref):
        me = chip_ref[0]
        mine = own_ref[...].astype(F32)
        flip_x, flip_y, flip_xy = (p_ref[i].astype(F32) for i in range(3))
        acc = None
        for k in range(N_CHIPS):
            rel = jnp.bitwise_xor(me, k)
            term = jnp.where(rel == 0, mine, jnp.where(rel == 2, flip_x, jnp.where(rel == 1, flip_y, flip_xy)))
            acc = term if acc is None else acc + term
        o_ref[...] = acc

    return pl.pallas_call(
        body, name=name,
        grid_spec=pltpu.PrefetchScalarGridSpec(
            num_scalar_prefetch=1, grid=(half // tr,),
            in_specs=[pl.BlockSpec((None, tr, c), lambda j, chip_ref: (chip_ref[0], j, 0)),
                      pl.BlockSpec((3, tr, c), lambda j, chip_ref: (0, j, 0))],
            out_specs=pl.BlockSpec((tr, c), lambda j, chip_ref: (j, 0))),
        out_shape=jax.ShapeDtypeStruct((half, c), F32), compiler_params=_params("parallel"),
    )(chip, own, parts)


def _adamw(name, w, g_mine, g_theirs, m, v, core):
    _, r, c = w.shape
    half = r // 2
    tr = _rows_tile(half)
    nth = half // tr

    def body(core_ref, w_ref, gm_ref, gt_ref, m_ref, v_ref, g_ref, d_ref, m2_ref, v2_ref):
        g = jnp.where(pl.program_id(0) == core_ref[0], gm_ref[...], gt_ref[...])
        delta, m2, v2 = _adam_update(w_ref[...], g, m_ref[...], v_ref[...])
        g_ref[...] = g
        d_ref[...] = delta
        m2_ref[...] = m2
        v2_ref[...] = v2

    full = pl.BlockSpec((None, tr, c), lambda h, j, core_ref: (0, h * nth + j, 0))
    part = pl.BlockSpec((tr, c), lambda h, j, core_ref: (j, 0))
    shape = jax.ShapeDtypeStruct((1, r, c), F32)
    return pl.pallas_call(
        body, name=name,
        grid_spec=pltpu.PrefetchScalarGridSpec(
            num_scalar_prefetch=1, grid=(2, nth), in_specs=[full, part, part, full, full], out_specs=[full] * 4),
        out_shape=[shape] * 4, compiler_params=_params("parallel", "parallel"),
    )(core, w, g_mine, g_theirs, m, v)


def _rel_bias_table(name, rel_bias):
    padded = jnp.pad(rel_bias, ((0, 0), (0, N_REL_PAD - N_REL)))

    def body(rb_ref, o_ref):
        ridx = lax.broadcasted_iota(jnp.int32, (N_REL_PAD, BAND), 0)
        sidx = lax.broadcasted_iota(jnp.int32, (N_REL_PAD, BAND), 1)
        rb = rb_ref[...]

        def step(tq, carry):
            rel = jnp.clip(tq + KPAD - sidx, -REL_CLIP, REL_CLIP) + REL_CLIP
            onehot = jnp.where(rel == ridx, 1.0, 0.0).astype(BF16)
            o_ref[tq] = _dot_exact_rhs(rb, onehot)
            return carry

        lax.fori_loop(0, CHUNK, step, 0)

    vm = pl.BlockSpec(memory_space=pltpu.VMEM)
    table = pl.pallas_call(
        body, name=name, in_specs=[vm], out_specs=vm,
        out_shape=jax.ShapeDtypeStruct((CHUNK, ATTN_HEADS, BAND), F32),
    )(padded)
    return table.transpose(1, 0, 2)


def _adamw_small(name, w, parts, m, v):
    def body(w_ref, p_ref, m_ref, v_ref, g_ref, d_ref, m2_ref, v2_ref):
        g = p_ref[0]
        for i in range(1, N_DEV):
            g = g + p_ref[i]
        delta, m2, v2 = _adam_update(w_ref[...], g, m_ref[...], v_ref[...])
        g_ref[...] = g
        d_ref[...] = delta
        m2_ref[...] = m2
        v2_ref[...] = v2

    vm = pl.BlockSpec(memory_space=pltpu.VMEM)
    shape = jax.ShapeDtypeStruct((SMALL_ROWS, SMALL_COLS), F32)
    return pl.pallas_call(
        body, name=name, in_specs=[vm] * 4, out_specs=[vm] * 4, out_shape=[shape] * 4,
    )(w, parts, m, v)


def _position():
    return lax.axis_index("x"), lax.axis_index("y"), lax.axis_index("c")


def _other_chips(x, y):
    return [(1 - x, y), (x, 1 - y), (1 - x, 1 - y)]


ANY = pl.BlockSpec(memory_space=pl.ANY)


HBM = pl.BlockSpec(memory_space=pltpu.HBM)
SEM = pl.BlockSpec(memory_space=pltpu.SEMAPHORE)
SPLIT_COPY = pltpu.SideEffectType.DATAFLOW_SIDE_EFFECTING


def _gather_copy(shards, outs, send_sem, recv_sem, i, j):
    x, y, c = _position()
    chips = _other_chips(x, y)
    half = shards[i].shape[0] // 2
    rows = pl.ds(pl.multiple_of(c * half, 16), half)
    return pltpu.make_async_remote_copy(
        src_ref=shards[i].at[rows, :], dst_ref=outs[i].at[2 * x + y, rows, :],
        send_sem=send_sem.at[3 * i + j], recv_sem=recv_sem.at[3 * i + j],
        device_id=(chips[j][0], chips[j][1], c), device_id_type=MESH)


def _gather_start(name, shards, after):
    n = len(shards)

    def body(*refs):
        srcs, outs = refs[:n], refs[n:2 * n]
        send_sem, recv_sem = refs[2 * n + len(after)], refs[2 * n + len(after) + 1]
        token = refs[-1]
        for i in range(n):
            for j in range(3):
                _gather_copy(srcs, outs, send_sem, recv_sem, i, j).start()
        token[...] = jnp.zeros_like(token)

    full = [(N_CHIPS,) + s.shape for s in shards]
    res = pl.pallas_call(
        body,
        name=name,
        in_specs=[HBM] * (2 * n) + [ANY] * len(after),
        out_specs=[SEM, SEM] + [HBM] * (2 * n) + [pl.BlockSpec(memory_space=pltpu.VMEM)],
        out_shape=[pltpu.SemaphoreType.DMA((3 * n,)), pltpu.SemaphoreType.DMA((3 * n,))]
        + [pltpu.HBM(s.shape, s.dtype) for s in shards]
        + [pltpu.HBM(shp, s.dtype) for shp, s in zip(full, shards)]
        + [jax.ShapeDtypeStruct((8, LANES), F32)],
        input_output_aliases={i: 2 + i for i in range(2 * n)},
        compiler_params=pltpu.CompilerParams(has_side_effects=SPLIT_COPY),
    )(*[pltpu.with_memory_space_constraint(s, pltpu.HBM) for s in shards],
      *[pltpu.with_memory_space_constraint(lax.empty(shp, s.dtype), pltpu.HBM) for shp, s in zip(full, shards)],
      *after)
    return res[0], res[1], list(res[2:2 + n]), list(res[2 + n:2 + 2 * n]), res[-1]


def _gather_wait(name, send_sem, recv_sem, shards, outs, after):
    n = len(shards)

    def body(*refs):
        srcs, out_refs = refs[:n], refs[n:2 * n]
        send_ref, recv_ref = refs[2 * n], refs[2 * n + 1]
        for i in range(n):
            for j in range(3):
                copy = _gather_copy(srcs, out_refs, send_ref, recv_ref, i, j)
                copy.wait_send()
                copy.wait_recv()

    res = pl.pallas_call(
        body,
        name=name,
        in_specs=[HBM] * (2 * n) + [SEM, SEM, ANY],
        out_specs=[HBM] * (2 * n),
        out_shape=[pltpu.HBM(s.shape, s.dtype) for s in shards] + [pltpu.HBM(o.shape, o.dtype) for o in outs],
        input_output_aliases={i: i for i in range(2 * n)},
        compiler_params=pltpu.CompilerParams(has_side_effects=SPLIT_COPY),
    )(*shards, *outs, send_sem, recv_sem, after)
    return list(res[:n]), list(res[n:])


def _gather_join(name, shards, outs):
    n = len(shards)

    def body(*refs):
        srcs, ins, outs_ = refs[:n], refs[n:2 * n], refs[2 * n:3 * n]
        own_send, own_recv, half_send, half_recv = refs[3 * n:]
        x, y, c = _position()
        chips = _other_chips(x, y)
        copies = []
        for i in range(n):
            copies.append(pltpu.make_async_remote_copy(
                src_ref=srcs[i], dst_ref=outs_[i].at[2 * x + y], send_sem=own_send.at[i], recv_sem=own_recv.at[i],
                device_id=(x, y, 1 - c), device_id_type=MESH))
            half = srcs[i].shape[0] // 2
            rows = pl.ds(pl.multiple_of(c * half, 16), half)
            for j in range(3):
                slot = 2 * chips[j][0] + chips[j][1]
                copies.append(pltpu.make_async_remote_copy(
                    src_ref=ins[i].at[slot, rows, :], dst_ref=outs_[i].at[slot, rows, :],
                    send_sem=half_send.at[3 * i + j], recv_sem=half_recv.at[3 * i + j],
                    device_id=(x, y, 1 - c), device_id_type=MESH))
        for cp in copies:
            cp.start()
        for cp in copies:
            cp.wait()

    return pl.pallas_call(
        body,
        name=name,
        in_specs=[ANY] * (2 * n),
        out_specs=[ANY] * n,
        out_shape=[jax.ShapeDtypeStruct(o.shape, o.dtype) for o in outs],
        input_output_aliases={n + i: i for i in range(n)},
        scratch_shapes=[pltpu.SemaphoreType.DMA((n,))] * 2 + [pltpu.SemaphoreType.DMA((3 * n,))] * 2,
    )(*shards, *outs)


def _pair_copy(grads, lands, send_sem, recv_sem, i):
    x, y, c = _position()
    half = grads[i].shape[1] // 2
    give = pl.ds(pl.multiple_of((1 - c) * half, 8), half)
    return pltpu.make_async_remote_copy(
        src_ref=grads[i].at[:, give, :], dst_ref=lands[i], send_sem=send_sem.at[i], recv_sem=recv_sem.at[i],
        device_id=(x, y, 1 - c), device_id_type=MESH)


def _pair_start(name, grads):
    n = len(grads)

    def body(*refs):
        srcs, lands = refs[:n], refs[n:2 * n]
        send_sem, recv_sem = refs[2 * n], refs[2 * n + 1]
        token = refs[-1]
        for i in range(n):
            _pair_copy(srcs, lands, send_sem, recv_sem, i).start()
        token[...] = jnp.zeros_like(token)

    halves = [(g.shape[0], g.shape[1] // 2, g.shape[2]) for g in grads]
    res = pl.pallas_call(
        body,
        name=name,
        in_specs=[HBM] * (2 * n),
        out_specs=[SEM, SEM] + [HBM] * (2 * n) + [pl.BlockSpec(memory_space=pltpu.VMEM)],
        out_shape=[pltpu.SemaphoreType.DMA((n,)), pltpu.SemaphoreType.DMA((n,))]
        + [pltpu.HBM(g.shape, g.dtype) for g in grads]
        + [pltpu.HBM(shp, g.dtype) for shp, g in zip(halves, grads)]
        + [jax.ShapeDtypeStruct((8, LANES), F32)],
        input_output_aliases={i: 2 + i for i in range(2 * n)},
        compiler_params=pltpu.CompilerParams(has_side_effects=SPLIT_COPY),
    )(*[pltpu.with_memory_space_constraint(g, pltpu.HBM) for g in grads],
      *[pltpu.with_memory_space_constraint(lax.empty(shp, g.dtype), pltpu.HBM) for shp, g in zip(halves, grads)])
    return res[0], res[1], list(res[2:2 + n]), list(res[2 + n:2 + 2 * n]), res[-1]


def _pair_wait(name, send_sem, recv_sem, grads, lands, after):
    n = len(grads)

    def body(*refs):
        srcs, land_refs = refs[:n], refs[n:2 * n]
        send_ref, recv_ref = refs[2 * n], refs[2 * n + 1]
        for i in range(n):
            copy = _pair_copy(srcs, land_refs, send_ref, recv_ref, i)
            copy.wait_send()
            copy.wait_recv()

    res = pl.pallas_call(
        body,
        name=name,
        in_specs=[HBM] * (2 * n) + [SEM, SEM, ANY],
        out_specs=[HBM] * (2 * n),
        out_shape=[pltpu.HBM(g.shape, g.dtype) for g in grads] + [pltpu.HBM(l.shape, l.dtype) for l in lands],
        input_output_aliases={i: i for i in range(2 * n)},
        compiler_params=pltpu.CompilerParams(has_side_effects=SPLIT_COPY),
    )(*grads, *lands, send_sem, recv_sem, after)
    return list(res[:n]), list(res[n:])


def _scatter_copy(srcs, lands, send_sem, recv_sem, i, j):
    x, y, c = _position()
    chips = _other_chips(x, y)
    return pltpu.make_async_remote_copy(
        src_ref=srcs[i].at[2 * chips[j][0] + chips[j][1]], dst_ref=lands[i].at[j],
        send_sem=send_sem.at[3 * i + j], recv_sem=recv_sem.at[3 * i + j],
        device_id=(chips[j][0], chips[j][1], c), device_id_type=MESH)


def _scatter_start(name, sums):
    n = len(sums)

    def body(*refs):
        srcs, lands = refs[:n], refs[n:2 * n]
        send_sem, recv_sem = refs[2 * n], refs[2 * n + 1]
        token = refs[-1]
        for i in range(n):
            for j in range(3):
                _scatter_copy(srcs, lands, send_sem, recv_sem, i, j).start()
        token[...] = jnp.zeros_like(token)

    land_shapes = [(3,) + s.shape[1:] for s in sums]
    res = pl.pallas_call(
        body,
        name=name,
        in_specs=[HBM] * (2 * n),
        out_specs=[SEM, SEM] + [HBM] * (2 * n) + [pl.BlockSpec(memory_space=pltpu.VMEM)],
        out_shape=[pltpu.SemaphoreType.DMA((3 * n,)), pltpu.SemaphoreType.DMA((3 * n,))]
        + [pltpu.HBM(s.shape, s.dtype) for s in sums]
        + [pltpu.HBM(shp, s.dtype) for shp, s in zip(land_shapes, sums)]
        + [jax.ShapeDtypeStruct((8, LANES), F32)],
        input_output_aliases={i: 2 + i for i in range(2 * n)},
        compiler_params=pltpu.CompilerParams(has_side_effects=SPLIT_COPY),
    )(*[pltpu.with_memory_space_constraint(s, pltpu.HBM) for s in sums],
      *[pltpu.with_memory_space_constraint(lax.empty(shp, s.dtype), pltpu.HBM) for shp, s in zip(land_shapes, sums)])
    return res[0], res[1], list(res[2:2 + n]), list(res[2 + n:2 + 2 * n]), res[-1]


def _scatter_wait(name, send_sem, recv_sem, sums, lands, after):
    n = len(sums)

    def body(*refs):
        srcs, land_refs = refs[:n], refs[n:2 * n]
        send_ref, recv_ref = refs[2 * n], refs[2 * n + 1]
        for i in range(n):
            for j in range(3):
                copy = _scatter_copy(srcs, land_refs, send_ref, recv_ref, i, j)
                copy.wait_send()
                copy.wait_recv()

    res = pl.pallas_call(
        body,
        name=name,
        in_specs=[HBM] * (2 * n) + [SEM, SEM, ANY],
        out_specs=[HBM] * (2 * n),
        out_shape=[pltpu.HBM(s.shape, s.dtype) for s in sums] + [pltpu.HBM(l.shape, l.dtype) for l in lands],
        input_output_aliases={i: i for i in range(2 * n)},
        compiler_params=pltpu.CompilerParams(has_side_effects=SPLIT_COPY),
    )(*sums, *lands, send_sem, recv_sem, after)
    return list(res[:n]), list(res[n:])


def _pair_join(name, halves, small=None):
    n = len(halves)
    if small is None:
        def body_plain(*refs):
            ins, outs = refs[:n], refs[n:2 * n]
            send_sem, recv_sem = refs[2 * n:]
            x, y, c = _position()
            swaps = [pltpu.make_async_remote_copy(
                src_ref=ins[i], dst_ref=outs[i], send_sem=send_sem.at[i], recv_sem=recv_sem.at[i],
                device_id=(x, y, 1 - c), device_id_type=MESH) for i in range(n)]
            for swap in swaps:
                swap.start()
            for swap in swaps:
                swap.wait()

        return pl.pallas_call(
            body_plain,
            name=name,
            in_specs=[ANY] * n,
            out_specs=[ANY] * n,
            out_shape=[jax.ShapeDtypeStruct(h.shape, h.dtype) for h in halves],
            scratch_shapes=[pltpu.SemaphoreType.DMA((n,))] * 2,
        )(*halves)

    def body(*refs):
        ins, small_ref = refs[:n], refs[n]
        outs, all_ref = refs[n + 1:2 * n + 1], refs[2 * n + 1]
        send_sem, recv_sem, sm_send, sm_recv, sm_local = refs[2 * n + 2:]
        x, y, c = _position()
        swaps = []
        for i in range(n):
            swap = pltpu.make_async_remote_copy(
                src_ref=ins[i], dst_ref=outs[i], send_sem=send_sem.at[i], recv_sem=recv_sem.at[i],
                device_id=(x, y, 1 - c), device_id_type=MESH)
            swap.start()
            swaps.append(swap)
        me = 4 * x + 2 * y + c
        sm_own = pltpu.make_async_copy(small_ref, all_ref.at[me], sm_local)
        sm_own.start()
        pushes, arrivals = [], []
        for mask in range(1, N_DEV):
            px, py, pc = x ^ (mask >> 2), y ^ ((mask >> 1) & 1), c ^ (mask & 1)
            pushes.append(pltpu.make_async_remote_copy(
                src_ref=small_ref, dst_ref=all_ref.at[me], send_sem=sm_send.at[mask - 1], recv_sem=sm_recv.at[mask - 1],
                device_id=(px, py, pc), device_id_type=MESH))
            arrivals.append(pltpu.make_async_remote_copy(
                src_ref=small_ref, dst_ref=all_ref.at[4 * px + 2 * py + pc], send_sem=sm_send.at[mask - 1],
                recv_sem=sm_recv.at[mask - 1], device_id=(px, py, pc), device_id_type=MESH))
        for cp in pushes:
            cp.start()
        for swap in swaps:
            swap.wait()
        for cp in arrivals:
            cp.wait_recv()
        for cp in pushes:
            cp.wait_send()
        sm_own.wait()

    res = pl.pallas_call(
        body,
        name=name,
        in_specs=[ANY] * (n + 1),
        out_specs=[ANY] * (n + 1),
        out_shape=[jax.ShapeDtypeStruct(h.shape, h.dtype) for h in halves]
        + [jax.ShapeDtypeStruct((N_DEV,) + small.shape, small.dtype)],
        scratch_shapes=[pltpu.SemaphoreType.DMA((n,))] * 2 + [pltpu.SemaphoreType.DMA((N_DEV - 1,))] * 2
        + [pltpu.SemaphoreType.DMA(())],
    )(*halves, small)
    return res[:n], res[n]


def _lower_bound(lbp):
    return jax.nn.softmax(lbp, axis=0)[0:1]


def _local_step(x, target, g1, gm, g2, gq, gk, go, rel_bias, lbp, first_weights, mid_weights, last_weights, on_grads, grads_sent):
    b, s, d = x.shape
    t = b * s
    x0 = x.reshape(t, d)
    tgt = target.reshape(t, d)
    gq_t = jnp.tile(gq, (1, ATTN_HEADS))
    gk_t = jnp.tile(gk, (1, ATTN_HEADS))
    lb = _lower_bound(lbp)
    bias = _rel_bias_table("rel_bias_table", rel_bias)

    h1 = _rmsnorm_fwd("norm1", x0, g1)
    wg1, wu1, deps1 = first_weights(h1)
    a1, b1, z1 = _ffn_up("ffn1_up", h1, wg1, wu1, deps1)
    wd1, w_in, w_out = mid_weights(z1)
    ns = w_in.shape[0]
    x1 = _ffn_down("ffn1_down", z1, wd1, x0)
    h2 = _rmsnorm_fwd("norm_mix", x1, gm)
    proj = _in_proj("in_proj", h2, w_in)
    proj3 = proj.reshape(b, s, proj.shape[1])
    table = _band_table(bias)
    qn, kn, vb = _qk_prep("qk_prep", proj3, gq_t, gk_t)
    attn = _attn_fwd("attn_fwd", qn, kn, vb, table).reshape(t, ATTN_W)
    ro, oraw, states = _hgrn_fwd("hgrn_fwd", proj, lb, go, b, s)
    mix = jnp.concatenate([attn, ro], axis=1)
    x2 = _out_proj("out_proj", mix, w_out, x1)
    h3 = _rmsnorm_fwd("norm2", x2, g2)
    wg2, wu2, wd2 = last_weights(h3)
    a2, b2, z2 = _ffn_up("ffn2_up", h3, wg2, wu2)
    dy, dyh, sq = _ffn_down_loss("ffn2_down_loss", z2, wd2, x2, tgt)
    loss = 0.5 * jnp.sum(sq) / d

    da2, db2 = _ffn_bwd_act("ffn2_bwd_act", dyh, wd2, a2, b2)
    dwd2 = _grad_w_shardrows("ffn2_dwd", z2, dyh)
    dwg2 = _grad_w_shardrows("ffn2_dwg", da2, h3)
    dwu2 = _grad_w_shardrows("ffn2_dwu", db2, h3)
    sent2 = on_grads("ffn2", {"ffn2_w_gate": dwg2, "ffn2_w_up": dwu2, "ffn2_w_down": dwd2})
    dx2, dx2b, dg2 = _ffn_bwd_in("ffn2_bwd_in", da2, db2, wg2, wu2, x2, g2, dy, 1.0, sent2)
    sent2 = grads_sent("ffn2", dx2b)

    dwout = _grad_w_out("dw_out", mix, dx2b)
    dmix = _out_proj_bwd("out_proj_bwd", dx2b, w_out, sent2)
    dqn, dkn, dvn, dbe, dbo = _attn_bwd("attn_bwd", qn, kn, vb, table, dmix.reshape(b, s, dmix.shape[1]))
    dbias = dbe[:, :, :BAND] + dbo[:, :, CHUNK:]
    dpq, dpk, dpv, dgq, dgk = _qk_prep_bwd("qk_prep_bwd", proj3, dqn, dkn, dvn, gq_t, gk_t)
    dpq, dpk, dpv = (a.reshape(t, ATTN_W) for a in (dpq, dpk, dpv))
    dhq, dhf, dhi, dhg, dlb, dgo = _hgrn_bwd("hgrn_bwd", proj, lb, go, oraw, states, dmix, b, s)
    dproj = jnp.concatenate([dpq, dpk, dpv, dhq, dhf, dhi, dhg], axis=1)
    dwin = _grad_w_in("dw_in", h2, dproj, ns)
    dx1, dx1h, dgm = _in_proj_bwd("in_proj_bwd", dproj, w_in, x1, gm, dx2, 0.5)

    dwd1 = _grad_w_shardrows("ffn1_dwd", z1, dx1h)
    sent_mix = on_grads("mix", {"w_in": dwin, "w_out": dwout.reshape(ns, dwout.shape[0] // ns, d),
                                "ffn1_w_down": dwd1})
    da1, db1 = _ffn_bwd_act("ffn1_bwd_act", dx1h, wd1, a1, b1, sent_mix)
    sent_mix = grads_sent("mix", da1)
    dwg1 = _grad_w_shardrows("ffn1_dwg", da1, h1, sent_mix)
    dwu1 = _grad_w_shardrows("ffn1_dwu", db1, h1)
    on_grads("ffn1", {"ffn1_w_gate": dwg1, "ffn1_w_up": dwu1})
    sent1 = grads_sent("ffn1", dwu1)
    dx0, dg1 = _ffn_bwd_in("ffn1_bwd_in", da1, db1, wg1, wu1, x0, g1, dx1, None, sent1)

    nt = dg1.shape[0]
    sg = _small_grads(
        "small_grads", dg1.reshape(nt, d), dgm.reshape(nt, d), dg2.reshape(nt, d),
        dgq.reshape(-1, ATTN_W), dgk.reshape(-1, ATTN_W), dbias.transpose(1, 0, 2),
        dlb.reshape(b, HGRN_W), dgo.reshape(b, HGRN_W), lbp)
    g1g, gmg, g2g, gqg, gkg, rbg, lbg, gog = sg
    small = _pack_small(g1g, gmg, g2g, lbg, rbg[:, :N_REL], gqg, gkg, gog)
    return loss, dx0.reshape(b, s, d), small


def _pack_small(g1, gm, g2, lbp, rel_bias, gq, gk, go):
    flat = [g1.reshape(-1), gm.reshape(-1), g2.reshape(-1), lbp.reshape(-1), rel_bias.reshape(-1)]
    n_bias = 3 * SMALL_COLS - rel_bias.size
    heads = [gq.reshape(-1), gk.reshape(-1), go.reshape(-1)]
    n_tail = SMALL_COLS - sum(h.size for h in heads)
    return jnp.concatenate(flat + [jnp.zeros((n_bias,), F32)] + heads + [jnp.zeros((n_tail,), F32)]).reshape(
        SMALL_ROWS, SMALL_COLS)


def _unpack_small(p, d):
    flat = p.reshape(-1)
    o = 3 * d
    g1, gm, g2 = p[0:1], p[1:2], p[2:3]
    lbp = flat[o:o + 2 * HGRN_W].reshape(2, HGRN_W)
    o = 4 * SMALL_COLS
    rel = flat[o:o + ATTN_HEADS * N_REL].reshape(1, ATTN_HEADS, N_REL)
    o = 7 * SMALL_COLS
    gq = flat[o:o + ATTN_DH].reshape(1, ATTN_DH)
    gk = flat[o + ATTN_DH:o + 2 * ATTN_DH].reshape(1, ATTN_DH)
    go = flat[o + 2 * ATTN_DH:o + 2 * ATTN_DH + HGRN_DH].reshape(1, HGRN_DH)
    return g1, gm, g2, gq, gk, rel, lbp, go


def kernel(x, ffn1_norm_g, ffn1_w_gate, ffn1_w_up, ffn1_w_down, mix_norm_g, w_in, attn_q_norm_g, attn_k_norm_g, attn_rel_bias, hgrn_lower_bounds, hgrn_out_norm_g, w_out, ffn2_norm_g, ffn2_w_gate, ffn2_w_up, ffn2_w_down, loss_target, m_ffn1_norm_g, m_ffn1_w_gate, m_ffn1_w_up, m_ffn1_w_down, m_mix_norm_g, m_w_in, m_attn_q_norm_g, m_attn_k_norm_g, m_attn_rel_bias, m_hgrn_lower_bounds, m_hgrn_out_norm_g, m_w_out, m_ffn2_norm_g, m_ffn2_w_gate, m_ffn2_w_up, m_ffn2_w_down, v_ffn1_norm_g, v_ffn1_w_gate, v_ffn1_w_up, v_ffn1_w_down, v_mix_norm_g, v_w_in, v_attn_q_norm_g, v_attn_k_norm_g, v_attn_rel_bias, v_hgrn_lower_bounds, v_hgrn_out_norm_g, v_w_out, v_ffn2_norm_g, v_ffn2_w_gate, v_ffn2_w_up, v_ffn2_w_down):
    d = x.shape[-1]
    big_w = [ffn1_w_gate, ffn1_w_up, ffn1_w_down, w_in, w_out, ffn2_w_gate, ffn2_w_up, ffn2_w_down]
    big_m = [m_ffn1_w_gate, m_ffn1_w_up, m_ffn1_w_down, m_w_in, m_w_out, m_ffn2_w_gate, m_ffn2_w_up, m_ffn2_w_down]
    big_v = [v_ffn1_w_gate, v_ffn1_w_up, v_ffn1_w_down, v_w_in, v_w_out, v_ffn2_w_gate, v_ffn2_w_up, v_ffn2_w_down]
    big_names = ["ffn1_w_gate", "ffn1_w_up", "ffn1_w_down", "w_in", "w_out", "ffn2_w_gate", "ffn2_w_up", "ffn2_w_down"]
    flipped = {nm for nm in big_names if nm.endswith("gate") or nm.endswith("up")}
    flip = lambda nm, a: jnp.swapaxes(a, 1, 2) if nm in flipped else a
    big_w, big_m, big_v = ([flip(nm, a) for nm, a in zip(big_names, arrs)] for arrs in (big_w, big_m, big_v))

    shards = [w[0].astype(BF16) for w in big_w]
    start_a = _gather_start("gather_start_up1", shards[:2], ())
    start_b = _gather_start("gather_start_mid", shards[2:5], (start_a[4],))
    start_c = _gather_start("gather_start_ffn2", shards[5:], (start_b[4],))

    def gathered(tag, started, after):
        send_sem, recv_sem, srcs, outs, _ = started
        srcs, outs = _gather_wait("gather_wait_" + tag, send_sem, recv_sem, srcs, outs, after)
        return _gather_join("gather_join_" + tag, srcs, outs)

    def first_weights(after):
        return (*gathered("up1", start_a, after), (start_c[4],))

    def mid_weights(after):
        wd1, win_f, wout_f = gathered("mid", start_b, after)
        return wd1, win_f, wout_f.reshape(wout_f.shape[0] * wout_f.shape[1], d)

    def last_weights(after):
        return gathered("ffn2", start_c, after)

    core = lax.axis_index("c").astype(jnp.int32).reshape(1)
    chip = (2 * lax.axis_index("x") + lax.axis_index("y")).astype(jnp.int32).reshape(1)
    started = {}

    def on_grads(tag, grads):
        names = list(grads)
        started[tag] = (names, _pair_start("pair_start_" + tag, [grads[nm] for nm in names]))
        return (started[tag][1][4],)

    def grads_sent(tag, after):
        names, (send_sem, recv_sem, grads, lands, _) = started[tag]
        grads, theirs = _pair_wait("pair_wait_" + tag, send_sem, recv_sem, grads, lands, after)
        sums = [_pair_sum("pair_sum_" + nm, g, th, core) for nm, g, th in zip(names, grads, theirs)]
        started[tag] = (names, _scatter_start("scatter_start_" + tag, sums))
        return (started[tag][1][4],)

    loss, grad_x, small_g = _local_step(
        x, loss_target, ffn1_norm_g, mix_norm_g, ffn2_norm_g, attn_q_norm_g, attn_k_norm_g, hgrn_out_norm_g,
        attn_rel_bias[0], hgrn_lower_bounds, first_weights, mid_weights, last_weights, on_grads, grads_sent)
    loss = lax.psum(loss, ("x", "y", "c"))

    def finish(tag, after):
        names, (send_sem, recv_sem, sums, lands, _) = started[tag]
        sums, lands = _scatter_wait("scatter_wait_" + tag, send_sem, recv_sem, sums, lands, after)
        return names, [_chip_sum("chip_sum_" + nm, sm, ld, chip) for nm, sm, ld in zip(names, sums, lands)]

    by_name = {nm: (w, m, v) for nm, w, m, v in zip(big_names, big_w, big_m, big_v)}
    updated = {}

    def update(names, halves, other_halves):
        for nm, mine, theirs in zip(names, halves, other_halves):
            w, m, v = by_name[nm]
            updated[nm] = _adamw("adamw_" + nm, w, mine, theirs, m, v, core)

    last_token = started["ffn1"][1][4]
    names_a, halves_a = finish("ffn2", last_token)
    names_m, halves_m = finish("mix", last_token)
    names_a, halves_a = names_a + names_m, halves_a + halves_m
    update(names_a, halves_a, _pair_join("pair_join_early", halves_a))
    names_b, halves_b = finish("ffn1", updated[names_a[-1]][1])
    others_b, small_all = _pair_join("pair_join_last", halves_b, small_g)
    update(names_b, halves_b, others_b)
    big_out = [updated[nm] for nm in big_names]

    pack = lambda g1, gm, g2, gq, gk, rel, lbp, go: _pack_small(g1, gm, g2, lbp, rel[0], gq, gk, go)
    small_w = pack(ffn1_norm_g, mix_norm_g, ffn2_norm_g, attn_q_norm_g, attn_k_norm_g, attn_rel_bias, hgrn_lower_bounds, hgrn_out_norm_g)
    small_m = pack(m_ffn1_norm_g, m_mix_norm_g, m_ffn2_norm_g, m_attn_q_norm_g, m_attn_k_norm_g, m_attn_rel_bias, m_hgrn_lower_bounds, m_hgrn_out_norm_g)
    small_v = pack(v_ffn1_norm_g, v_mix_norm_g, v_ffn2_norm_g, v_attn_q_norm_g, v_attn_k_norm_g, v_attn_rel_bias, v_hgrn_lower_bounds, v_hgrn_out_norm_g)
    small_out = [_unpack_small(p, d) for p in _adamw_small("adamw_small", small_w, small_all, small_m, small_v)]

    def assemble(kind):
        bg = [flip(nm, o[kind]) for nm, o in zip(big_names, big_out)]
        g1, gm, g2, gq, gk, rel, lbp, go = small_out[kind]
        return [g1, bg[0], bg[1], bg[2], gm, bg[3], gq, gk, rel, lbp, go, bg[4], g2, bg[5], bg[6], bg[7]]

    return (loss, grad_x, *assemble(0), *assemble(1), *assemble(2), *assemble(3))
```

```python
import functools

import jax
import jax.numpy as jnp
from jax import lax
from jax.experimental import pallas as pl
from jax.experimental.pallas import tpu as pltpu

F32 = jnp.float32
BF16 = jnp.bfloat16
MESH = pl.DeviceIdType.MESH

N_CHIPS = 4
N_DEV = 8
CHUNK = 64
ATTN_HEADS = 8
ATTN_DH = 64
ATTN_W = ATTN_HEADS * ATTN_DH
HGRN_HEADS = 4
HGRN_DH = 128
HGRN_W = HGRN_HEADS * HGRN_DH
LEFT_CHUNKS = 8
BAND = (LEFT_CHUNKS + 1) * CHUNK
KPAD = LEFT_CHUNKS * CHUNK
REL_CLIP = 128
N_REL = 2 * REL_CLIP + 1
N_REL_PAD = 384
RMS_EPS = 1e-6
LANES = 128
SMALL_ROWS = 8
SMALL_COLS = 1024

ADAM_LR = 0.001
ADAM_B1 = 0.9
ADAM_B2 = 0.999
ADAM_EPS = 1e-08
ADAM_WD = 0.01
ADAM_STEP = 10

NN = (((1,), (0,)), ((), ()))
NT = (((1,), (1,)), ((), ()))
TN = (((0,), (0,)), ((), ()))

VMEM_LIMIT = 48 * 1024 * 1024


def _sigmoid(x):
    return 1.0 / (1.0 + jnp.exp(-x))


def _silu(x):
    return x * _sigmoid(x)


def _dot(a, b, dims=NN):
    return lax.dot_general(a, b, dims, preferred_element_type=F32)


def _split3(x):
    hi = x.astype(BF16)
    r1 = x - hi.astype(F32)
    mid = r1.astype(BF16)
    lo = (r1 - mid.astype(F32)).astype(BF16)
    return hi, mid, lo


def _dot_exact_rhs(x, mat, dims=NN):
    hi, mid, lo = _split3(x)
    return _dot(hi, mat, dims) + _dot(mid, mat, dims) + _dot(lo, mat, dims)


def _dot_exact_lhs(mat, x, dims=NN):
    hi, mid, lo = _split3(x)
    return _dot(mat, hi, dims) + _dot(mat, mid, dims) + _dot(mat, lo, dims)


def _params(*sem):
    return pltpu.CompilerParams(dimension_semantics=sem, vmem_limit_bytes=VMEM_LIMIT)


def _mm(name, ins, terms, n_acc, grid, acc_shape, outs, epilogue, extras=(), deps=()):
    nk = grid[2]
    ni, ne, nd, no = len(ins), len(extras), len(deps), len(outs)

    def body(*refs):
        in_refs = refs[:ni]
        ex_refs = refs[ni:ni + ne]
        out_refs = refs[ni + ne + nd:ni + ne + nd + no]
        acc_refs = refs[ni + ne + nd + no:]
        parts = [None] * n_acc
        for ai, li, ri, dims in terms:
            d = _dot(in_refs[li][...], in_refs[ri][...], dims)
            parts[ai] = d if parts[ai] is None else parts[ai] + d

        def finish(accs):
            res = epilogue(accs, [e[...] for e in ex_refs])
            for o, r in zip(out_refs, res):
                o[...] = r.astype(o.dtype)

        if nk == 1:
            finish(parts)
        else:
            k = pl.program_id(2)

            @pl.when(k == 0)
            def _():
                for a, p in zip(acc_refs, parts):
                    a[...] = p

            @pl.when(k > 0)
            def _():
                for a, p in zip(acc_refs, parts):
                    a[...] += p

            @pl.when(k == nk - 1)
            def _():
                finish([a[...] for a in acc_refs])

    scratch = [] if nk == 1 else [pltpu.VMEM(acc_shape, F32) for _ in range(n_acc)]
    res = pl.pallas_call(
        body,
        name=name,
        grid=grid,
        in_specs=[s for _, s in ins] + [s for _, s in extras] + [pl.BlockSpec(memory_space=pl.ANY)] * nd,
        out_specs=[s for _, s in outs],
        out_shape=[o for o, _ in outs],
        scratch_shapes=scratch,
        compiler_params=_params("parallel", "parallel", "arbitrary"),
    )(*[a for a, _ in ins], *[a for a, _ in extras], *deps)
    return res


def _mm_rows(name, lhs, weights, dims, t, outs, epilogue, extras=(), deps=()):
    tm = _row_tile(t)
    nl, ne, nd, no = len(lhs), len(extras), len(deps), len(outs)
    ns = weights[0].shape[0]

    def body(*refs):
        lhs_refs = refs[:nl]
        w_hbm = refs[nl:2 * nl]
        ex_refs = refs[2 * nl:2 * nl + ne]
        out_refs = refs[2 * nl + ne + nd:2 * nl + ne + nd + no]
        w_vmem = refs[2 * nl + ne + nd + no:3 * nl + ne + nd + no]
        sem = refs[-1]

        @pl.when(pl.program_id(0) == 0)
        def _():
            copies = [pltpu.make_async_copy(w_hbm[p], w_vmem[p], sem.at[p]) for p in range(nl)]
            for cp in copies:
                cp.start()
            for cp in copies:
                cp.wait()

        acc = None
        for p in range(nl):
            pick = lhs[p][2]
            for j in range(ns):
                part = _dot(pick(lhs_refs[p], j), w_vmem[p][j], dims)
                acc = part if acc is None else acc + part
        res = epilogue([acc], [e[...] for e in ex_refs])
        for o, r in zip(out_refs, res):
            o[...] = r.astype(o.dtype)

    return pl.pallas_call(
        body,
        name=name,
        grid=(t // tm,),
        in_specs=[s for _, s, _ in lhs] + [pl.BlockSpec(memory_space=pl.ANY)] * nl + [s for _, s in extras]
        + [pl.BlockSpec(memory_space=pl.ANY)] * nd,
        out_specs=[s for _, s in outs],
        out_shape=[o for o, _ in outs],
        scratch_shapes=[pltpu.VMEM(w.shape, w.dtype) for w in weights] + [pltpu.SemaphoreType.DMA((nl,))],
        compiler_params=_params("arbitrary"),
    )(*[a for a, _, _ in lhs], *weights, *[a for a, _ in extras], *deps)


def _row_tile(t):
    return 512 if t % 512 == 0 else t


def _k_tile(t):
    return t if t <= 4096 else 1024


def _rmsnorm_fwd(name, x, g):
    t, d = x.shape
    tm = _row_tile(t)

    def body(x_ref, g_ref, h_ref):
        xv = x_ref[...]
        ms = jnp.mean(xv * xv, axis=-1, keepdims=True)
        h_ref[...] = (xv * lax.rsqrt(ms + RMS_EPS) * g_ref[...]).astype(BF16)

    return pl.pallas_call(
        body,
        name=name,
        grid=(t // tm,),
        in_specs=[pl.BlockSpec((tm, d), lambda i: (i, 0)), pl.BlockSpec((1, d), lambda i: (0, 0))],
        out_specs=pl.BlockSpec((tm, d), lambda i: (i, 0)),
        out_shape=jax.ShapeDtypeStruct((t, d), BF16),
        compiler_params=_params("parallel"),
    )(x, g)


def _norm_bwd_epilogue(copy_scale):
    def epilogue(accs, ex):
        dh = accs[0]
        xv, g, dres = ex
        ms = jnp.mean(xv * xv, axis=-1, keepdims=True)
        rstd = lax.rsqrt(ms + RMS_EPS)
        xhat = xv * rstd
        dxhat = dh * g
        dx = rstd * (dxhat - xhat * jnp.mean(dxhat * xhat, axis=-1, keepdims=True))
        out = dres + dx
        dg = jnp.sum(dh * xhat, axis=0, keepdims=True)
        if copy_scale is None:
            return out, dg
        return out, out * copy_scale, dg

    return epilogue


def _ffn_up(name, h, wg, wu, deps=()):
    t, d = h.shape
    ns, f, _ = wg.shape
    tm = _row_tile(t)

    def epilogue(accs, ex):
        a, b = accs
        sg = _sigmoid(a)
        act = a * sg
        return act, b * (sg * (1.0 + a * (1.0 - sg))), act * b

    w_spec = pl.BlockSpec((None, f, d), lambda j, i, k: (j, 0, 0))
    o_spec = pl.BlockSpec((None, tm, f), lambda j, i, k: (j, i, 0))
    o_shape = jax.ShapeDtypeStruct((ns, t, f), BF16)
    return _mm(
        name,
        ins=[(h, pl.BlockSpec((tm, d), lambda j, i, k: (i, 0))), (wg, w_spec), (wu, w_spec)],
        terms=[(0, 0, 1, NT), (1, 0, 2, NT)],
        n_acc=2,
        grid=(ns, t // tm, 1),
        acc_shape=(tm, f),
        outs=[(o_shape, o_spec)] * 3,
        epilogue=epilogue,
        deps=deps,
    )


def _shard_rows(arr, tm):
    ns, _, f = arr.shape
    return arr, pl.BlockSpec((ns, tm, f), lambda i: (0, i, 0)), lambda ref, j: ref[j]


def _ffn_down(name, z, wd, x):
    _, t, _ = z.shape
    d = wd.shape[2]
    tm = _row_tile(t)
    row = pl.BlockSpec((tm, d), lambda i: (i, 0))
    return _mm_rows(
        name, [_shard_rows(z, tm)], [wd], NN, t,
        outs=[(jax.ShapeDtypeStruct((t, d), F32), row)],
        epilogue=lambda accs, ex: (ex[0] + 0.5 * accs[0],),
        extras=[(x, row)],
    )[0]


def _ffn_down_loss(name, z, wd, x, target):
    _, t, _ = z.shape
    d = wd.shape[2]
    tm = _row_tile(t)
    nt = t // tm
    row = pl.BlockSpec((tm, d), lambda i: (i, 0))

    def epilogue(accs, ex):
        e = ex[0] + 0.5 * accs[0] - ex[1]
        dy = e * (1.0 / d)
        return dy, 0.5 * dy, jnp.sum(e * e, axis=0, keepdims=True)

    return _mm_rows(
        name, [_shard_rows(z, tm)], [wd], NN, t,
        outs=[(jax.ShapeDtypeStruct((t, d), F32), row), (jax.ShapeDtypeStruct((t, d), BF16), row),
              (jax.ShapeDtypeStruct((nt, 1, d), F32), pl.BlockSpec((None, 1, d), lambda i: (i, 0, 0)))],
        epilogue=epilogue,
        extras=[(x, row), (target, row)],
    )


def _ffn_bwd_act(name, dout, wd, act_a, dact_b, deps=()):
    t, d = dout.shape
    ns, f, _ = wd.shape
    tm = _row_tile(t)

    def epilogue(accs, ex):
        dz = accs[0]
        return dz * ex[1].astype(F32), dz * ex[0].astype(F32)

    act = pl.BlockSpec((None, tm, f), lambda j, i, k: (j, i, 0))
    o_shape = jax.ShapeDtypeStruct((ns, t, f), BF16)
    return _mm(
        name,
        ins=[(dout, pl.BlockSpec((tm, d), lambda j, i, k: (i, 0))),
             (wd, pl.BlockSpec((None, f, d), lambda j, i, k: (j, 0, 0)))],
        terms=[(0, 0, 1, NT)],
        n_acc=1,
        grid=(ns, t // tm, 1),
        acc_shape=(tm, f),
        outs=[(o_shape, act)] * 2,
        epilogue=epilogue,
        extras=[(act_a, act), (dact_b, act)],
        deps=deps,
    )


def _grad_w_shardrows(name, z, dout, deps=()):
    ns, t, f = z.shape
    d = dout.shape[1]
    tk = _k_tile(t)
    return _mm(
        name,
        ins=[(z, pl.BlockSpec((None, tk, f), lambda j, n, k: (j, k, 0))),
             (dout, pl.BlockSpec((tk, d), lambda j, n, k: (k, 0)))],
        terms=[(0, 0, 1, TN)],
        n_acc=1,
        grid=(ns, 1, t // tk),
        acc_shape=(f, d),
        outs=[(jax.ShapeDtypeStruct((ns, f, d), BF16), pl.BlockSpec((None, f, d), lambda j, n, k: (j, 0, 0)))],
        epilogue=lambda accs, ex: (accs[0],),
        deps=deps,
    )[0]


def _norm_bwd_outs(t, d, tm, copy_scale):
    row = pl.BlockSpec((tm, d), lambda i: (i, 0))
    outs = [(jax.ShapeDtypeStruct((t, d), F32), row)]
    if copy_scale is not None:
        outs.append((jax.ShapeDtypeStruct((t, d), BF16), row))
    outs.append((jax.ShapeDtypeStruct((t // tm, 1, d), F32), pl.BlockSpec((None, 1, d), lambda i: (i, 0, 0))))
    return row, outs


def _ffn_bwd_in(name, da, db, wg, wu, x, g, dres, copy_scale, deps=()):
    _, t, _ = da.shape
    d = wg.shape[2]
    tm = _row_tile(t)
    row, outs = _norm_bwd_outs(t, d, tm, copy_scale)
    return _mm_rows(
        name, [_shard_rows(da, tm), _shard_rows(db, tm)], [wg, wu], NN, t,
        outs=outs,
        epilogue=_norm_bwd_epilogue(copy_scale),
        extras=[(x, row), (g, pl.BlockSpec((1, d), lambda i: (0, 0))), (dres, row)],
        deps=deps,
    )


def _in_proj(name, h, w_in):
    t, d = h.shape
    ns, _, pj = w_in.shape
    tm = _row_tile(t)
    return _mm(
        name,
        ins=[(h, pl.BlockSpec((tm, d), lambda j, i, k: (i, 0))),
             (w_in, pl.BlockSpec((None, d, pj), lambda j, i, k: (j, 0, 0)))],
        terms=[(0, 0, 1, NN)],
        n_acc=1,
        grid=(ns, t // tm, 1),
        acc_shape=(tm, pj),
        outs=[(jax.ShapeDtypeStruct((t, ns * pj), F32), pl.BlockSpec((tm, pj), lambda j, i, k: (i, j)))],
        epilogue=lambda accs, ex: (accs[0],),
    )[0]


def _in_proj_bwd(name, dp, w_in, x, g, dres, copy_scale, deps=()):
    t = dp.shape[0]
    ns, d, pj = w_in.shape
    tm = _row_tile(t)
    row, outs = _norm_bwd_outs(t, d, tm, copy_scale)
    cols = (dp, pl.BlockSpec((tm, ns * pj), lambda i: (i, 0)), lambda ref, j: ref[:, j * pj:(j + 1) * pj])
    return _mm_rows(
        name, [cols], [w_in], NT, t,
        outs=outs,
        epilogue=_norm_bwd_epilogue(copy_scale),
        extras=[(x, row), (g, pl.BlockSpec((1, d), lambda i: (0, 0))), (dres, row)],
        deps=deps,
    )


def _grad_w_in(name, h, dp, ns):
    t, d = h.shape
    pj = dp.shape[1] // ns
    tk = _k_tile(t)
    return _mm(
        name,
        ins=[(h, pl.BlockSpec((tk, d), lambda j, n, k: (k, 0))),
             (dp, pl.BlockSpec((tk, pj), lambda j, n, k: (k, j)))],
        terms=[(0, 0, 1, TN)],
        n_acc=1,
        grid=(ns, 1, t // tk),
        acc_shape=(d, pj),
        outs=[(jax.ShapeDtypeStruct((ns, d, pj), BF16), pl.BlockSpec((None, d, pj), lambda j, n, k: (j, 0, 0)))],
        epilogue=lambda accs, ex: (accs[0],),
    )[0]


def _out_proj(name, mix, w_out, x):
    t, dm = mix.shape
    d = w_out.shape[1]
    tm = _row_tile(t)
    row = pl.BlockSpec((tm, d), lambda i, n, k: (i, 0))
    return _mm(
        name,
        ins=[(mix, pl.BlockSpec((tm, dm), lambda i, n, k: (i, 0))),
             (w_out, pl.BlockSpec((dm, d), lambda i, n, k: (0, 0)))],
        terms=[(0, 0, 1, NN)],
        n_acc=1,
        grid=(t // tm, 1, 1),
        acc_shape=(tm, d),
        outs=[(jax.ShapeDtypeStruct((t, d), F32), row)],
        epilogue=lambda accs, ex: (ex[0] + accs[0],),
        extras=[(x, row)],
    )[0]


def _out_proj_bwd(name, dx, w_out, deps=()):
    t, d = dx.shape
    dm = w_out.shape[0]
    tm = _row_tile(t)
    return _mm(
        name,
        ins=[(dx, pl.BlockSpec((tm, d), lambda i, n, k: (i, 0))),
             (w_out, pl.BlockSpec((dm, d), lambda i, n, k: (0, 0)))],
        terms=[(0, 0, 1, NT)],
        n_acc=1,
        grid=(t // tm, 1, 1),
        acc_shape=(tm, dm),
        outs=[(jax.ShapeDtypeStruct((t, dm), F32), pl.BlockSpec((tm, dm), lambda i, n, k: (i, 0)))],
        epilogue=lambda accs, ex: (accs[0],),
        deps=deps,
    )[0]


def _grad_w_out(name, mix, dx):
    t, dm = mix.shape
    d = dx.shape[1]
    tk = _k_tile(t)
    return _mm(
        name,
        ins=[(mix, pl.BlockSpec((tk, dm), lambda a, n, k: (k, 0))),
             (dx, pl.BlockSpec((tk, d), lambda a, n, k: (k, 0)))],
        terms=[(0, 0, 1, TN)],
        n_acc=1,
        grid=(1, 1, t // tk),
        acc_shape=(dm, d),
        outs=[(jax.ShapeDtypeStruct((dm, d), BF16), pl.BlockSpec((dm, d), lambda a, n, k: (0, 0)))],
        epilogue=lambda accs, ex: (accs[0],),
    )[0]


def _head_group_matrix():
    r = lax.broadcasted_iota(jnp.int32, (ATTN_W, ATTN_W), 0)
    c = lax.broadcasted_iota(jnp.int32, (ATTN_W, ATTN_W), 1)
    same = jnp.right_shift(r, 6) == jnp.right_shift(c, 6)
    return jnp.where(same, 1.0, 0.0).astype(BF16)


def _qk_prep(name, proj, gq, gk):
    b, s, _ = proj.shape
    tm = KPAD
    nb = s // tm

    def body(q_ref, k_ref, v_ref, gq_ref, gk_ref, qn_ref, kn_ref, vb_ref):
        j = pl.program_id(1)
        bd = _head_group_matrix()

        def norm(xv, g):
            ms = _dot_exact_rhs(xv * xv, bd) * (1.0 / ATTN_DH)
            return xv * lax.rsqrt(ms + RMS_EPS) * g

        @pl.when(j == 0)
        def _():
            kn_ref[...] = jnp.zeros_like(kn_ref)
            vb_ref[...] = jnp.zeros_like(vb_ref)

        @pl.when(j > 0)
        def _():
            qn_ref[...] = norm(q_ref[...], gq_ref[...]).astype(BF16)
            kn_ref[...] = norm(k_ref[...], gk_ref[...]).astype(BF16)
            vb_ref[...] = v_ref[...].astype(BF16)

    src_blk = lambda col: pl.BlockSpec((None, tm, ATTN_W), lambda bi, j: (bi, jnp.maximum(j - 1, 0), col))
    gspec = pl.BlockSpec((1, ATTN_W), lambda bi, j: (0, 0))
    padded = pl.BlockSpec((None, tm, ATTN_W), lambda bi, j: (bi, j, 0))
    return pl.pallas_call(
        body,
        name=name,
        grid=(b, nb + 1),
        in_specs=[src_blk(0), src_blk(1), src_blk(2), gspec, gspec],
        out_specs=[src_blk(0), padded, padded],
        out_shape=[jax.ShapeDtypeStruct((b, s, ATTN_W), BF16), jax.ShapeDtypeStruct((b, KPAD + s, ATTN_W), BF16),
                   jax.ShapeDtypeStruct((b, KPAD + s, ATTN_W), BF16)],
        compiler_params=_params("parallel", "arbitrary"),
    )(proj, proj, proj, gq, gk)


def _qk_prep_bwd(name, proj, dqn, dkn, dv, gq, gk):
    b, s, _ = proj.shape
    tm = KPAD
    nb = s // tm

    def body(q_ref, k_ref, dqn_ref, dkn_ref, dv_ref, gq_ref, gk_ref, dq_ref, dk_ref, dvb_ref, dgq_ref, dgk_ref):
        bd = _head_group_matrix()

        def bwd(xv, dy, g):
            ms = _dot_exact_rhs(xv * xv, bd) * (1.0 / ATTN_DH)
            rstd = lax.rsqrt(ms + RMS_EPS)
            xhat = xv * rstd
            dxhat = dy * g
            gm = _dot_exact_rhs(dxhat * xhat, bd) * (1.0 / ATTN_DH)
            return rstd * (dxhat - xhat * gm), jnp.sum(dy * xhat, axis=0, keepdims=True)

        dq, dgq = bwd(q_ref[...], dqn_ref[...], gq_ref[...])
        dk, dgk = bwd(k_ref[...], dkn_ref[...], gk_ref[...])
        dq_ref[...] = dq.astype(BF16)
        dk_ref[...] = dk.astype(BF16)
        dvb_ref[...] = dv_ref[...].astype(BF16)
        dgq_ref[...] = dgq
        dgk_ref[...] = dgk

    col = lambda c: pl.BlockSpec((None, tm, ATTN_W), lambda bi, j: (bi, j, c))
    past_pad = pl.BlockSpec((None, tm, ATTN_W), lambda bi, j: (bi, j + 1, 0))
    gspec = pl.BlockSpec((1, ATTN_W), lambda bi, j: (0, 0))
    pspec = pl.BlockSpec((None, 1, ATTN_W), lambda bi, j: (bi * nb + j, 0, 0))
    o_shape = jax.ShapeDtypeStruct((b, s, ATTN_W), BF16)
    p_shape = jax.ShapeDtypeStruct((b * nb, 1, ATTN_W), F32)
    return pl.pallas_call(
        body,
        name=name,
        grid=(b, nb),
        in_specs=[col(0), col(1), col(0), past_pad, past_pad, gspec, gspec],
        out_specs=[col(0)] * 3 + [pspec] * 2,
        out_shape=[o_shape] * 3 + [p_shape] * 2,
        compiler_params=_params("parallel", "parallel"),
    )(proj, proj, dqn, dkn, dv, gq, gk)


Q_CHUNKS = 4
QBLK = Q_CHUNKS * CHUNK
WIN = (LEFT_CHUNKS + Q_CHUNKS) * CHUNK
DB_W = BAND + CHUNK
MASKED = -1e30


def _band_table(bias):
    rows = [jnp.pad(bias, ((0, 0), (0, 0), (CHUNK * i, WIN - BAND - CHUNK * i)), constant_values=MASKED)
            for i in range(Q_CHUNKS)]
    return jnp.concatenate(rows, axis=1)


def _head_lanes(hh):
    lane = lax.broadcasted_iota(jnp.int32, (1, LANES), 1)
    return (lane < ATTN_DH) if hh == 0 else (lane >= ATTN_DH)


def _attn_probs(qh, kw, table, start):
    s = _dot(qh, kw, NT) * (ATTN_DH ** -0.5) + table
    col = lax.broadcasted_iota(jnp.int32, (QBLK, WIN), 1)
    s = jnp.where(col + start >= KPAD, s, MASKED)
    m = jnp.max(s, axis=-1, keepdims=True)
    p = jnp.exp(s - m)
    return p * (1.0 / jnp.sum(p, axis=-1, keepdims=True))


def _attn_fwd(name, q, k, v, table):
    b, s, w = q.shape
    sp = k.shape[1]

    def body(q_ref, k_ref, v_ref, t_ref, o_ref):
        start = pl.multiple_of(pl.program_id(2) * QBLK, QBLK)
        kw = k_ref[pl.ds(start, WIN), :]
        vw = v_ref[pl.ds(start, WIN), :]
        q2 = q_ref[...]
        lanes = [_head_lanes(hh) for hh in range(2)]
        probs = [_attn_probs(jnp.where(mine, q2, jnp.zeros_like(q2)), kw, t_ref[hh], start).astype(BF16)
                 for hh, mine in enumerate(lanes)]
        outs = [_dot(p, vw) for p in probs]
        o_ref[...] = jnp.where(lanes[0], outs[0], outs[1]).astype(BF16)

    qspec = pl.BlockSpec((None, QBLK, LANES), lambda p, bi, i: (bi, i, p))
    kspec = pl.BlockSpec((None, sp, LANES), lambda p, bi, i: (bi, 0, p))
    return pl.pallas_call(
        body,
        name=name,
        grid=(w // LANES, b, s // QBLK),
        in_specs=[qspec, kspec, kspec, pl.BlockSpec((2, QBLK, WIN), lambda p, bi, i: (p, 0, 0))],
        out_specs=qspec,
        out_shape=jax.ShapeDtypeStruct((b, s, w), BF16),
        compiler_params=_params("parallel", "parallel", "arbitrary"),
    )(q, k, v, table)


def _attn_bwd(name, q, k, v, table, dmix):
    b, s, w = q.shape
    sp = k.shape[1]

    def body(q_ref, k_ref, v_ref, t_ref, do_ref, dq_ref, dk_ref, dv_ref, dbe_ref, dbo_ref):
        bi = pl.program_id(1)
        i = pl.program_id(2)
        start = pl.multiple_of(i * QBLK, QBLK)
        win = pl.ds(start, WIN)

        @pl.when(i == 0)
        def _():
            dk_ref[...] = jnp.zeros_like(dk_ref)
            dv_ref[...] = jnp.zeros_like(dv_ref)

        @pl.when(jnp.logical_and(i == 0, bi == 0))
        def _():
            dbe_ref[...] = jnp.zeros_like(dbe_ref)
            dbo_ref[...] = jnp.zeros_like(dbo_ref)

        kw = k_ref[win, :]
        vw = v_ref[win, :]
        q2 = q_ref[...]
        do2 = do_ref[...].astype(BF16)
        dq = jnp.zeros((QBLK, LANES), F32)
        dk = dv = None
        for hh in range(2):
            mine = _head_lanes(hh)
            qh = jnp.where(mine, q2, jnp.zeros_like(q2))
            doh = jnp.where(mine, do2, jnp.zeros_like(do2))
            p = _attn_probs(qh, kw, t_ref[hh], start)
            dp = _dot(doh, vw, NT)
            ds = p * (dp - jnp.sum(p * dp, axis=-1, keepdims=True))
            for qi in range(Q_CHUNKS):
                c0 = (qi // 2) * LANES
                blk = ds[qi * CHUNK:(qi + 1) * CHUNK, c0:c0 + DB_W]
                if qi % 2 == 0:
                    dbe_ref[hh] += blk
                else:
                    dbo_ref[hh] += blk
            dsb = (ds * (ATTN_DH ** -0.5)).astype(BF16)
            dq = jnp.where(mine, _dot(dsb, kw), dq)
            dk_h = _dot(dsb, qh, TN)
            dv_h = _dot(p.astype(BF16), doh, TN)
            dk = dk_h if dk is None else dk + dk_h
            dv = dv_h if dv is None else dv + dv_h
        dq_ref[...] = dq
        dk_ref[win, :] += dk
        dv_ref[win, :] += dv

    qspec = pl.BlockSpec((None, QBLK, LANES), lambda p, bi, i: (bi, i, p))
    kspec = pl.BlockSpec((None, sp, LANES), lambda p, bi, i: (bi, 0, p))
    dbspec = pl.BlockSpec((2, CHUNK, DB_W), lambda p, bi, i: (p, 0, 0))
    db_shape = jax.ShapeDtypeStruct((ATTN_HEADS, CHUNK, DB_W), F32)
    return pl.pallas_call(
        body,
        name=name,
        grid=(w // LANES, b, s // QBLK),
        in_specs=[qspec, kspec, kspec, pl.BlockSpec((2, QBLK, WIN), lambda p, bi, i: (p, 0, 0)), qspec],
        out_specs=[qspec, kspec, kspec, dbspec, dbspec],
        out_shape=[jax.ShapeDtypeStruct((b, s, w), F32), jax.ShapeDtypeStruct((b, sp, w), F32),
                   jax.ShapeDtypeStruct((b, sp, w), F32), db_shape, db_shape],
        compiler_params=_params("arbitrary", "arbitrary", "arbitrary"),
    )(q, k, v, table, dmix)


HQ_COL = 3 * ATTN_W // HGRN_DH
HF_COL = HQ_COL + HGRN_HEADS
HI_COL = HF_COL + HGRN_HEADS
HG_COL = HI_COL + HGRN_HEADS
HGRN_ROWS = 8 * CHUNK
HEAD_LANES = [slice(hh * HGRN_DH, (hh + 1) * HGRN_DH) for hh in range(HGRN_HEADS)]


def _tri(lower):
    r = lax.broadcasted_iota(jnp.int32, (CHUNK, CHUNK), 0)
    c = lax.broadcasted_iota(jnp.int32, (CHUNK, CHUNK), 1)
    return (r >= c) if lower else (r <= c)


def _hgrn_chunk(hq, hf, lb, tril):
    sig = _sigmoid(hf)
    f = lb + (1.0 - lb) * sig
    g = jnp.log(f)
    ones_l = jnp.where(tril, 1.0, 0.0).astype(BF16)
    b = _dot_exact_lhs(ones_l, g)
    bl = jnp.sum(g, axis=0, keepdims=True)
    rows = lax.broadcasted_iota(jnp.int32, g.shape, 0)
    bm = jnp.sum(jnp.where(rows <= CHUNK // 2, g, 0.0), axis=0, keepdims=True)
    sq = _sigmoid(hq)
    q = hq * sq
    k = 1.0 - f
    return sig, f, b, bl, bm, sq, q, k


def _hgrn_fwd(name, proj, attn, lb, go, b, s):
    nc = s // CHUNK
    t = b * s
    nblk = s // HGRN_ROWS
    cpb = HGRN_ROWS // CHUNK

    def body(hq_ref, hf_ref, hi_ref, hg_ref, attn_ref, lb_ref, go_ref, mix_ref, oraw_ref, st_ref, s_scr):
        tril = _tri(True)
        gov = go_ref[...]
        mix_ref[:, 0:ATTN_W] = attn_ref[...]

        @pl.when(pl.program_id(1) == 0)
        def _():
            s_scr[...] = jnp.zeros_like(s_scr)

        def step(c, carry):
            sl = pl.ds(pl.multiple_of(c * CHUNK, CHUNK), CHUNK)
            hg = hg_ref[sl, :]
            _, _, bb, bl, bm, _, q, k = _hgrn_chunk(hq_ref[sl, :], hf_ref[sl, :], lb_ref[...], tril)
            vb = hi_ref[sl, :].astype(BF16)
            qe = (q * jnp.exp(bb - bm)).astype(BF16)
            ke = (k * jnp.exp(bm - bb)).astype(BF16)
            qb = (q * jnp.exp(bb)).astype(BF16)
            kb = (k * jnp.exp(bl - bb)).astype(BF16)
            e_last = jnp.exp(bl)
            gate = _silu(hg)
            st = [s_scr[hh] for hh in range(HGRN_HEADS)]
            a = [jnp.where(tril, _dot(qe[:, hs], ke[:, hs], NT), 0.0).astype(BF16) for hs in HEAD_LANES]
            o_state = [_dot(qb[:, hs], st[hh].astype(BF16), NT) for hh, hs in enumerate(HEAD_LANES)]
            st_next = [st[hh] * e_last[:, hs] + _dot(vb[:, hs], kb[:, hs], TN) for hh, hs in enumerate(HEAD_LANES)]
            o = [_dot(a[hh], vb[:, hs]) + o_state[hh] for hh, hs in enumerate(HEAD_LANES)]
            ro = [(oh * lax.rsqrt(jnp.mean(oh * oh, axis=-1, keepdims=True) + RMS_EPS) * gov) * gate[:, hs]
                  for oh, hs in zip(o, HEAD_LANES)]
            for hh in range(HGRN_HEADS):
                st_ref[hh, c] = st[hh]
                s_scr[hh] = st_next[hh]
            mix_ref[sl, ATTN_W:ATTN_W + HGRN_W] = jnp.concatenate(ro, axis=1).astype(BF16)
            oraw_ref[sl, :] = jnp.concatenate(o, axis=1)
            return carry

        lax.fori_loop(0, cpb, step, 0)

    col = lambda base: pl.BlockSpec((HGRN_ROWS, HGRN_W), lambda bi, i: (bi * nblk + i, base // HGRN_HEADS))
    out = pl.BlockSpec((HGRN_ROWS, HGRN_W), lambda bi, i: (bi * nblk + i, 0))
    return pl.pallas_call(
        body,
        name=name,
        grid=(b, nblk),
        in_specs=[col(HQ_COL), col(HF_COL), col(HI_COL), col(HG_COL), out,
                  pl.BlockSpec((1, HGRN_W), lambda bi, i: (0, 0)), pl.BlockSpec((1, HGRN_DH), lambda bi, i: (0, 0))],
        out_specs=[pl.BlockSpec((HGRN_ROWS, ATTN_W + HGRN_W), lambda bi, i: (bi * nblk + i, 0)), out,
                   pl.BlockSpec((None, HGRN_HEADS, cpb, HGRN_DH, HGRN_DH), lambda bi, i: (bi, 0, i, 0, 0))],
        out_shape=[jax.ShapeDtypeStruct((t, ATTN_W + HGRN_W), BF16), jax.ShapeDtypeStruct((t, HGRN_W), F32),
                   jax.ShapeDtypeStruct((b, HGRN_HEADS, nc, HGRN_DH, HGRN_DH), F32)],
        scratch_shapes=[pltpu.VMEM((HGRN_HEADS, HGRN_DH, HGRN_DH), F32)],
        compiler_params=_params("parallel", "arbitrary"),
    )(proj, proj, proj, proj, attn, lb, go)


def _hgrn_bwd(name, proj, dqkv, lb, go, oraw, states, dmix, b, s):
    t = b * s
    nblk = s // HGRN_ROWS
    cpb = HGRN_ROWS // CHUNK

    def body(hq_ref, hf_ref, hi_ref, hg_ref, dq_ref, dk_ref, dv_ref, lb_ref, go_ref, oraw_ref, st_ref, dro_ref,
             dp_ref, dlb_ref, dgo_ref, ds_scr, dlb_scr, dgo_scr):
        tril = _tri(True)
        ones_u = jnp.where(_tri(False), 1.0, 0.0).astype(BF16)
        gov = go_ref[...]
        dp_ref[:, 0:ATTN_W] = dq_ref[...]
        dp_ref[:, ATTN_W:2 * ATTN_W] = dk_ref[...]
        dp_ref[:, 2 * ATTN_W:3 * ATTN_W] = dv_ref[...]

        @pl.when(pl.program_id(1) == 0)
        def _():
            ds_scr[...] = jnp.zeros_like(ds_scr)
            dlb_scr[...] = jnp.zeros_like(dlb_scr)
            dgo_scr[...] = jnp.zeros_like(dgo_scr)

        def step(ci, carry):
            c = cpb - 1 - ci
            sl = pl.ds(pl.multiple_of(c * CHUNK, CHUNK), CHUNK)
            hq = hq_ref[sl, :]
            hg = hg_ref[sl, :]
            sig, f, bb, bl, bm, sq, q, k = _hgrn_chunk(hq, hf_ref[sl, :], lb_ref[...], tril)
            vb = hi_ref[sl, :].astype(BF16)
            ebm = jnp.exp(bb - bm)
            embm = jnp.exp(bm - bb)
            eb = jnp.exp(bb)
            ebl = jnp.exp(bl - bb)
            e_last = jnp.exp(bl)
            qe = (q * ebm).astype(BF16)
            ke = (k * embm).astype(BF16)
            qb = (q * eb).astype(BF16)
            kb = (k * ebl).astype(BF16)
            st = [st_ref[hh, c] for hh in range(HGRN_HEADS)]
            dst = [ds_scr[hh] for hh in range(HGRN_HEADS)]
            o = oraw_ref[sl, :]
            dro = dro_ref[sl, :]
            sg = _sigmoid(hg)
            gov4 = jnp.concatenate([gov] * HGRN_HEADS, axis=1)
            rstd = jnp.concatenate(
                [jnp.broadcast_to(lax.rsqrt(jnp.mean(o[:, hs] * o[:, hs], axis=-1, keepdims=True) + RMS_EPS),
                                  (CHUNK, HGRN_DH)) for hs in HEAD_LANES], axis=1)
            ohat = o * rstd
            dn = dro * (hg * sg)
            dhg = dro * (ohat * gov4) * (sg * (1.0 + hg * (1.0 - sg)))
            dgo_inc = jnp.sum(dn * ohat, axis=0, keepdims=True)
            dohat = dn * gov4
            proj_h = dohat * ohat
            pm = jnp.concatenate(
                [jnp.broadcast_to(jnp.mean(proj_h[:, hs], axis=-1, keepdims=True), (CHUNK, HGRN_DH))
                 for hs in HEAD_LANES], axis=1)
            dob = (rstd * (dohat - ohat * pm)).astype(BF16)
            stb = [x.astype(BF16) for x in st]
            dstb = [x.astype(BF16) for x in dst]
            a = [jnp.where(tril, _dot(qe[:, hs], ke[:, hs], NT), 0.0).astype(BF16) for hs in HEAD_LANES]
            dab = [jnp.where(tril, _dot(dob[:, hs], vb[:, hs], NT), 0.0).astype(BF16) for hs in HEAD_LANES]
            dqb = [_dot(dob[:, hs], stb[hh]) for hh, hs in enumerate(HEAD_LANES)]
            dkb = [_dot(vb[:, hs], dstb[hh]) for hh, hs in enumerate(HEAD_LANES)]
            dv_state = [_dot(kb[:, hs], dstb[hh], NT) for hh, hs in enumerate(HEAD_LANES)]
            dst_next = [dst[hh] * e_last[:, hs] + _dot(dob[:, hs], qb[:, hs], TN) for hh, hs in enumerate(HEAD_LANES)]
            dv = [_dot(a[hh], dob[:, hs], TN) + dv_state[hh] for hh, hs in enumerate(HEAD_LANES)]
            dqe = jnp.concatenate([_dot(dab[hh], ke[:, hs]) for hh, hs in enumerate(HEAD_LANES)], axis=1)
            dke = jnp.concatenate([_dot(dab[hh], qe[:, hs], TN) for hh, hs in enumerate(HEAD_LANES)], axis=1)
            dqb = jnp.concatenate(dqb, axis=1)
            dkb = jnp.concatenate(dkb, axis=1)
            state_term = jnp.concatenate(
                [jnp.sum(dst[hh] * st[hh], axis=0, keepdims=True) for hh in range(HGRN_HEADS)], axis=1)
            dq = dqe * ebm + dqb * eb
            dk = dke * embm + dkb * ebl
            db = (qe.astype(F32) * dqe - ke.astype(F32) * dke) + q * (dqb * eb) - k * (dkb * ebl)
            d_last = jnp.sum(k * ebl * dkb, axis=0, keepdims=True) + state_term * e_last
            dg = _dot_exact_lhs(ones_u, db) + d_last
            df = dg / f - dk
            first = HQ_COL * HGRN_DH
            dp_ref[sl, first:first + HGRN_W] = (dq * (sq * (1.0 + hq * (1.0 - sq)))).astype(BF16)
            dp_ref[sl, first + HGRN_W:first + 2 * HGRN_W] = (df * (1.0 - lb_ref[...]) * sig * (1.0 - sig)).astype(BF16)
            dp_ref[sl, first + 2 * HGRN_W:first + 3 * HGRN_W] = jnp.concatenate(dv, axis=1).astype(BF16)
            dp_ref[sl, first + 3 * HGRN_W:first + 4 * HGRN_W] = dhg.astype(BF16)
            dlb_scr[...] += jnp.sum(df * (1.0 - sig), axis=0, keepdims=True)
            dgo_scr[...] += dgo_inc
            for hh in range(HGRN_HEADS):
                ds_scr[hh] = dst_next[hh]
            return carry

        lax.fori_loop(0, cpb, step, 0)

        @pl.when(pl.program_id(1) == nblk - 1)
        def _():
            dlb_ref[...] = dlb_scr[...]
            dgo_ref[...] = dgo_scr[...]

    rows = lambda bi, i: bi * nblk + (nblk - 1 - i)
    col = lambda base: pl.BlockSpec((HGRN_ROWS, HGRN_W), lambda bi, i: (rows(bi, i), base // HGRN_HEADS))
    out = pl.BlockSpec((HGRN_ROWS, HGRN_W), lambda bi, i: (rows(bi, i), 0))
    part = pl.BlockSpec((None, 1, HGRN_W), lambda bi, i: (bi, 0, 0))
    width = HG_COL * HGRN_DH + HGRN_W
    o_shape = jax.ShapeDtypeStruct((t, width), BF16)
    p_shape = jax.ShapeDtypeStruct((b, 1, HGRN_W), F32)
    return pl.pallas_call(
        body,
        name=name,
        grid=(b, nblk),
        in_specs=[col(HQ_COL), col(HF_COL), col(HI_COL), col(HG_COL), out, out, out,
                  pl.BlockSpec((1, HGRN_W), lambda bi, i: (0, 0)), pl.BlockSpec((1, HGRN_DH), lambda bi, i: (0, 0)), out,
                  pl.BlockSpec((None, HGRN_HEADS, cpb, HGRN_DH, HGRN_DH), lambda bi, i: (bi, 0, nblk - 1 - i, 0, 0)),
                  col(ATTN_W // HGRN_DH)],
        out_specs=[pl.BlockSpec((HGRN_ROWS, width), lambda bi, i: (rows(bi, i), 0))] + [part] * 2,
        out_shape=[o_shape] + [p_shape] * 2,
        scratch_shapes=[pltpu.VMEM((HGRN_HEADS, HGRN_DH, HGRN_DH), F32), pltpu.VMEM((1, HGRN_W), F32),
                        pltpu.VMEM((1, HGRN_W), F32)],
        compiler_params=_params("parallel", "arbitrary"),
    )(proj, proj, proj, proj, *dqkv, lb, go, oraw, states, dmix)


def _small_grads(name, dg1, dgm, dg2, dgq, dgk, dbias_t, dlb, dgo, lbp):
    d = dg1.shape[1]

    def body(dg1_ref, dgm_ref, dg2_ref, dgq_ref, dgk_ref, dbias_ref, dlb_ref, dgo_ref, lbp_ref,
             g1_ref, gm_ref, g2_ref, gq_ref, gk_ref, rb_ref, lbg_ref, go_ref):
        g1_ref[...] = jnp.sum(dg1_ref[...], axis=0, keepdims=True)
        gm_ref[...] = jnp.sum(dgm_ref[...], axis=0, keepdims=True)
        g2_ref[...] = jnp.sum(dg2_ref[...], axis=0, keepdims=True)
        r = lax.broadcasted_iota(jnp.int32, (ATTN_W, ATTN_DH), 0)
        cidx = lax.broadcasted_iota(jnp.int32, (ATTN_W, ATTN_DH), 1)
        fold = jnp.where(jnp.bitwise_and(r, ATTN_DH - 1) == cidx, 1.0, 0.0).astype(BF16)
        gq_ref[...] = jnp.sum(_dot_exact_rhs(dgq_ref[...], fold), axis=0, keepdims=True)
        gk_ref[...] = jnp.sum(_dot_exact_rhs(dgk_ref[...], fold), axis=0, keepdims=True)
        gosum = jnp.sum(dgo_ref[...], axis=0, keepdims=True)
        go_ref[...] = (gosum[:, 0:HGRN_DH] + gosum[:, HGRN_DH:2 * HGRN_DH]
                       + gosum[:, 2 * HGRN_DH:3 * HGRN_DH] + gosum[:, 3 * HGRN_DH:4 * HGRN_DH])
        p0 = lbp_ref[0:1, :]
        p1 = lbp_ref[1:2, :]
        lbv = 1.0 / (1.0 + jnp.exp(p1 - p0))
        dp0 = jnp.sum(dlb_ref[...], axis=0, keepdims=True) * lbv * (1.0 - lbv)
        lbg_ref[0:1, :] = dp0
        lbg_ref[1:2, :] = -dp0
        sidx = lax.broadcasted_iota(jnp.int32, (BAND, N_REL_PAD), 0)
        ridx = lax.broadcasted_iota(jnp.int32, (BAND, N_REL_PAD), 1)

        def step(tq, acc):
            rel = jnp.clip(tq + KPAD - sidx, -REL_CLIP, REL_CLIP) + REL_CLIP
            onehot = jnp.where(rel == ridx, 1.0, 0.0).astype(BF16)
            return acc + _dot_exact_rhs(dbias_ref[tq], onehot)

        rb_ref[...] = lax.fori_loop(0, CHUNK, step, jnp.zeros((ATTN_HEADS, N_REL_PAD), F32))

    ins = [dg1, dgm, dg2, dgq, dgk, dbias_t, dlb, dgo, lbp]
    outs = [jax.ShapeDtypeStruct((1, d), F32)] * 3 + [jax.ShapeDtypeStruct((1, ATTN_DH), F32)] * 2 + [
        jax.ShapeDtypeStruct((ATTN_HEADS, N_REL_PAD), F32), jax.ShapeDtypeStruct((2, HGRN_W), F32),
        jax.ShapeDtypeStruct((1, HGRN_DH), F32)]
    vm = pl.BlockSpec(memory_space=pltpu.VMEM)
    return pl.pallas_call(
        body,
        name=name,
        in_specs=[vm] * len(ins),
        out_specs=[vm] * len(outs),
        out_shape=outs,
        compiler_params=pltpu.CompilerParams(vmem_limit_bytes=VMEM_LIMIT),
    )(*ins)


def _adam_update(w, g, m, v):
    m2 = ADAM_B1 * m + (1.0 - ADAM_B1) * g
    v2 = ADAM_B2 * v + (1.0 - ADAM_B2) * (g * g)
    m_hat = m2 / (1.0 - ADAM_B1 ** ADAM_STEP)
    v_hat = v2 / (1.0 - ADAM_B2 ** ADAM_STEP)
    delta = -ADAM_LR * (m_hat / (jnp.sqrt(v_hat) + ADAM_EPS) + ADAM_WD * w)
    return delta, m2, v2


def _rows_tile(r):
    for cand in (256, 352, 128, 176, 64, 32, 16):
        if r % cand == 0 and r > cand:
            return cand
    return r


def _pair_sum(name, grad, theirs, core):
    n, half, c = theirs.shape
    tr = _rows_tile(half)
    nth = half // tr

    def body(core_ref, a_ref, b_ref, o_ref):
        o_ref[...] = (a_ref[...].astype(F32) + b_ref[...].astype(F32)).astype(o_ref.dtype)

    spec = pl.BlockSpec((None, tr, c), lambda i, j, core_ref: (i, j, 0))
    return pl.pallas_call(
        body, name=name,
        grid_spec=pltpu.PrefetchScalarGridSpec(
            num_scalar_prefetch=1, grid=(n, nth),
            in_specs=[pl.BlockSpec((None, tr, c), lambda i, j, core_ref: (i, core_ref[0] * nth + j, 0)), spec],
            out_specs=spec),
        out_shape=jax.ShapeDtypeStruct((n, half, c), BF16), compiler_params=_params("parallel", "parallel"),
    )(core, grad, theirs)


def _chip_sum(name, own, parts, chip):
    _, half, c = own.shape
    tr = _rows_tile(half)

    def body(chip_ref, own_ref, p_ref, o_ref):
        me = chip_ref[0]
        mine = own_ref[...].astype(F32)
        flip_x, flip_y, flip_xy = (p_ref[i].astype(F32) for i in range(3))
        acc = None
        for k in range(N_CHIPS):
            rel = jnp.bitwise_xor(me, k)
            term = jnp.where(rel == 0, mine, jnp.where(rel == 2, flip_x, jnp.where(rel == 1, flip_y, flip_xy)))
            acc = term if acc is None else acc + term
        o_ref[...] = acc

    return pl.pallas_call(
        body, name=name,
        grid_spec=pltpu.PrefetchScalarGridSpec(
            num_scalar_prefetch=1, grid=(half // tr,),
            in_specs=[pl.BlockSpec((None, tr, c), lambda j, chip_ref: (chip_ref[0], j, 0)),
                      pl.BlockSpec((3, tr, c), lambda j, chip_ref: (0, j, 0))],
            out_specs=pl.BlockSpec((tr, c), lambda j, chip_ref: (j, 0))),
        out_shape=jax.ShapeDtypeStruct((half, c), F32), compiler_params=_params("parallel"),
    )(chip, own, parts)


def _adamw(name, w, g_mine, g_theirs, m, v, core):
    _, r, c = w.shape
    half = r // 2
    tr = _rows_tile(half)
    nth = half // tr

    def body(core_ref, w_ref, gm_ref, gt_ref, m_ref, v_ref, g_ref, d_ref, m2_ref, v2_ref):
        g = jnp.where(pl.program_id(0) == core_ref[0], gm_ref[...], gt_ref[...])
        delta, m2, v2 = _adam_update(w_ref[...], g, m_ref[...], v_ref[...])
        g_ref[...] = g
        d_ref[...] = delta
        m2_ref[...] = m2
        v2_ref[...] = v2

    full = pl.BlockSpec((None, tr, c), lambda h, j, core_ref: (0, h * nth + j, 0))
    part = pl.BlockSpec((tr, c), lambda h, j, core_ref: (j, 0))
    shape = jax.ShapeDtypeStruct((1, r, c), F32)
    return pl.pallas_call(
        body, name=name,
        grid_spec=pltpu.PrefetchScalarGridSpec(
            num_scalar_prefetch=1, grid=(2, nth), in_specs=[full, part, part, full, full], out_specs=[full] * 4),
        out_shape=[shape] * 4, compiler_params=_params("parallel", "parallel"),
    )(core, w, g_mine, g_theirs, m, v)


def _rel_bias_table(name, rel_bias):
    padded = jnp.pad(rel_bias, ((0, 0), (0, N_REL_PAD - N_REL)))

    def body(rb_ref, o_ref):
        ridx = lax.broadcasted_iota(jnp.int32, (N_REL_PAD, BAND), 0)
        sidx = lax.broadcasted_iota(jnp.int32, (N_REL_PAD, BAND), 1)
        rb = rb_ref[...]

        def step(tq, carry):
            rel = jnp.clip(tq + KPAD - sidx, -REL_CLIP, REL_CLIP) + REL_CLIP
            onehot = jnp.where(rel == ridx, 1.0, 0.0).astype(BF16)
            o_ref[tq] = _dot_exact_rhs(rb, onehot)
            return carry

        lax.fori_loop(0, CHUNK, step, 0)

    vm = pl.BlockSpec(memory_space=pltpu.VMEM)
    table = pl.pallas_call(
        body, name=name, in_specs=[vm], out_specs=vm,
        out_shape=jax.ShapeDtypeStruct((CHUNK, ATTN_HEADS, BAND), F32),
    )(padded)
    return table.transpose(1, 0, 2)


def _adamw_small(name, w, parts, m, v):
    def body(w_ref, p_ref, m_ref, v_ref, g_ref, d_ref, m2_ref, v2_ref):
        g = p_ref[0]
        for i in range(1, N_DEV):
            g = g + p_ref[i]
        delta, m2, v2 = _adam_update(w_ref[...], g, m_ref[...], v_ref[...])
        g_ref[...] = g
        d_ref[...] = delta
        m2_ref[...] = m2
        v2_ref[...] = v2

    vm = pl.BlockSpec(memory_space=pltpu.VMEM)
    shape = jax.ShapeDtypeStruct((SMALL_ROWS, SMALL_COLS), F32)
    return pl.pallas_call(
        body, name=name, in_specs=[vm] * 4, out_specs=[vm] * 4, out_shape=[shape] * 4,
    )(w, parts, m, v)


def _position():
    return lax.axis_index("x"), lax.axis_index("y"), lax.axis_index("c")


def _other_chips(x, y):
    return [(1 - x, y), (x, 1 - y), (1 - x, 1 - y)]


ANY = pl.BlockSpec(memory_space=pl.ANY)


HBM = pl.BlockSpec(memory_space=pltpu.HBM)
SEM = pl.BlockSpec(memory_space=pltpu.SEMAPHORE)
SPLIT_COPY = pltpu.SideEffectType.DATAFLOW_SIDE_EFFECTING


def _gather_copy(shards, outs, send_sem, recv_sem, i, j):
    x, y, c = _position()
    chips = _other_chips(x, y)
    half = shards[i].shape[0] // 2
    rows = pl.ds(pl.multiple_of(c * half, 16), half)
    return pltpu.make_async_remote_copy(
        src_ref=shards[i].at[rows, :], dst_ref=outs[i].at[2 * x + y, rows, :],
        send_sem=send_sem.at[3 * i + j], recv_sem=recv_sem.at[3 * i + j],
        device_id=(chips[j][0], chips[j][1], c), device_id_type=MESH)


def _gather_start(name, shards, after):
    n = len(shards)

    def body(*refs):
        srcs, outs = refs[:n], refs[n:2 * n]
        send_sem, recv_sem = refs[2 * n + len(after)], refs[2 * n + len(after) + 1]
        token = refs[-1]
        for i in range(n):
            for j in range(3):
                _gather_copy(srcs, outs, send_sem, recv_sem, i, j).start()
        token[...] = jnp.zeros_like(token)

    full = [(N_CHIPS,) + s.shape for s in shards]
    res = pl.pallas_call(
        body,
        name=name,
        in_specs=[HBM] * (2 * n) + [ANY] * len(after),
        out_specs=[SEM, SEM] + [HBM] * (2 * n) + [pl.BlockSpec(memory_space=pltpu.VMEM)],
        out_shape=[pltpu.SemaphoreType.DMA((3 * n,)), pltpu.SemaphoreType.DMA((3 * n,))]
        + [pltpu.HBM(s.shape, s.dtype) for s in shards]
        + [pltpu.HBM(shp, s.dtype) for shp, s in zip(full, shards)]
        + [jax.ShapeDtypeStruct((8, LANES), F32)],
        input_output_aliases={i: 2 + i for i in range(2 * n)},
        compiler_params=pltpu.CompilerParams(has_side_effects=SPLIT_COPY),
    )(*[pltpu.with_memory_space_constraint(s, pltpu.HBM) for s in shards],
      *[pltpu.with_memory_space_constraint(lax.empty(shp, s.dtype), pltpu.HBM) for shp, s in zip(full, shards)],
      *after)
    return res[0], res[1], list(res[2:2 + n]), list(res[2 + n:2 + 2 * n]), res[-1]


def _gather_wait(name, send_sem, recv_sem, shards, outs, after):
    n = len(shards)

    def body(*refs):
        srcs, out_refs = refs[:n], refs[n:2 * n]
        send_ref, recv_ref = refs[2 * n], refs[2 * n + 1]
        for i in range(n):
            for j in range(3):
                copy = _gather_copy(srcs, out_refs, send_ref, recv_ref, i, j)
                copy.wait_send()
                copy.wait_recv()

    res = pl.pallas_call(
        body,
        name=name,
        in_specs=[HBM] * (2 * n) + [SEM, SEM] + [ANY] * len(after),
        out_specs=[HBM] * (2 * n),
        out_shape=[pltpu.HBM(s.shape, s.dtype) for s in shards] + [pltpu.HBM(o.shape, o.dtype) for o in outs],
        input_output_aliases={i: i for i in range(2 * n)},
        compiler_params=pltpu.CompilerParams(has_side_effects=SPLIT_COPY),
    )(*shards, *outs, send_sem, recv_sem, *after)
    return list(res[:n]), list(res[n:])


def _gather_join(name, shards, outs):
    n = len(shards)

    def body(*refs):
        srcs, ins, outs_ = refs[:n], refs[n:2 * n], refs[2 * n:3 * n]
        own_send, own_recv, half_send, half_recv = refs[3 * n:]
        x, y, c = _position()
        chips = _other_chips(x, y)
        copies = []
        for i in range(n):
            copies.append(pltpu.make_async_remote_copy(
                src_ref=srcs[i], dst_ref=outs_[i].at[2 * x + y], send_sem=own_send.at[i], recv_sem=own_recv.at[i],
                device_id=(x, y, 1 - c), device_id_type=MESH))
            half = srcs[i].shape[0] // 2
            rows = pl.ds(pl.multiple_of(c * half, 16), half)
            for j in range(3):
                slot = 2 * chips[j][0] + chips[j][1]
                copies.append(pltpu.make_async_remote_copy(
                    src_ref=ins[i].at[slot, rows, :], dst_ref=outs_[i].at[slot, rows, :],
                    send_sem=half_send.at[3 * i + j], recv_sem=half_recv.at[3 * i + j],
                    device_id=(x, y, 1 - c), device_id_type=MESH))
        for cp in copies:
            cp.start()
        for cp in copies:
            cp.wait()

    return pl.pallas_call(
        body,
        name=name,
        in_specs=[ANY] * (2 * n),
        out_specs=[ANY] * n,
        out_shape=[jax.ShapeDtypeStruct(o.shape, o.dtype) for o in outs],
        input_output_aliases={n + i: i for i in range(n)},
        scratch_shapes=[pltpu.SemaphoreType.DMA((n,))] * 2 + [pltpu.SemaphoreType.DMA((3 * n,))] * 2,
    )(*shards, *outs)


def _pair_copy(grads, lands, send_sem, recv_sem, i):
    x, y, c = _position()
    half = grads[i].shape[1] // 2
    give = pl.ds(pl.multiple_of((1 - c) * half, 16), half)
    return pltpu.make_async_remote_copy(
        src_ref=grads[i].at[:, give, :], dst_ref=lands[i], send_sem=send_sem.at[i], recv_sem=recv_sem.at[i],
        device_id=(x, y, 1 - c), device_id_type=MESH)


def _pair_start(name, grads):
    n = len(grads)

    def body(*refs):
        srcs, lands = refs[:n], refs[n:2 * n]
        send_sem, recv_sem = refs[2 * n], refs[2 * n + 1]
        token = refs[-1]
        for i in range(n):
            _pair_copy(srcs, lands, send_sem, recv_sem, i).start()
        token[...] = jnp.zeros_like(token)

    halves = [(g.shape[0], g.shape[1] // 2, g.shape[2]) for g in grads]
    res = pl.pallas_call(
        body,
        name=name,
        in_specs=[HBM] * (2 * n),
        out_specs=[SEM, SEM] + [HBM] * (2 * n) + [pl.BlockSpec(memory_space=pltpu.VMEM)],
        out_shape=[pltpu.SemaphoreType.DMA((n,)), pltpu.SemaphoreType.DMA((n,))]
        + [pltpu.HBM(g.shape, g.dtype) for g in grads]
        + [pltpu.HBM(shp, g.dtype) for shp, g in zip(halves, grads)]
        + [jax.ShapeDtypeStruct((8, LANES), F32)],
        input_output_aliases={i: 2 + i for i in range(2 * n)},
        compiler_params=pltpu.CompilerParams(has_side_effects=SPLIT_COPY),
    )(*[pltpu.with_memory_space_constraint(g, pltpu.HBM) for g in grads],
      *[pltpu.with_memory_space_constraint(lax.empty(shp, g.dtype), pltpu.HBM) for shp, g in zip(halves, grads)])
    return res[0], res[1], list(res[2:2 + n]), list(res[2 + n:2 + 2 * n]), res[-1]


def _pair_wait(name, send_sem, recv_sem, grads, lands, after):
    n = len(grads)

    def body(*refs):
        srcs, land_refs = refs[:n], refs[n:2 * n]
        send_ref, recv_ref = refs[2 * n], refs[2 * n + 1]
        for i in range(n):
            copy = _pair_copy(srcs, land_refs, send_ref, recv_ref, i)
            copy.wait_send()
            copy.wait_recv()

    res = pl.pallas_call(
        body,
        name=name,
        in_specs=[HBM] * (2 * n) + [SEM, SEM, ANY],
        out_specs=[HBM] * (2 * n),
        out_shape=[pltpu.HBM(g.shape, g.dtype) for g in grads] + [pltpu.HBM(l.shape, l.dtype) for l in lands],
        input_output_aliases={i: i for i in range(2 * n)},
        compiler_params=pltpu.CompilerParams(has_side_effects=SPLIT_COPY),
    )(*grads, *lands, send_sem, recv_sem, after)
    return list(res[:n]), list(res[n:])


def _scatter_copy(srcs, lands, send_sem, recv_sem, i, j):
    x, y, c = _position()
    chips = _other_chips(x, y)
    return pltpu.make_async_remote_copy(
        src_ref=srcs[i].at[2 * chips[j][0] + chips[j][1]], dst_ref=lands[i].at[j],
        send_sem=send_sem.at[3 * i + j], recv_sem=recv_sem.at[3 * i + j],
        device_id=(chips[j][0], chips[j][1], c), device_id_type=MESH)


def _scatter_start(name, sums):
    n = len(sums)

    def body(*refs):
        srcs, lands = refs[:n], refs[n:2 * n]
        send_sem, recv_sem = refs[2 * n], refs[2 * n + 1]
        token = refs[-1]
        for i in range(n):
            for j in range(3):
                _scatter_copy(srcs, lands, send_sem, recv_sem, i, j).start()
        token[...] = jnp.zeros_like(token)

    land_shapes = [(3,) + s.shape[1:] for s in sums]
    res = pl.pallas_call(
        body,
        name=name,
        in_specs=[HBM] * (2 * n),
        out_specs=[SEM, SEM] + [HBM] * (2 * n) + [pl.BlockSpec(memory_space=pltpu.VMEM)],
        out_shape=[pltpu.SemaphoreType.DMA((3 * n,)), pltpu.SemaphoreType.DMA((3 * n,))]
        + [pltpu.HBM(s.shape, s.dtype) for s in sums]
        + [pltpu.HBM(shp, s.dtype) for shp, s in zip(land_shapes, sums)]
        + [jax.ShapeDtypeStruct((8, LANES), F32)],
        input_output_aliases={i: 2 + i for i in range(2 * n)},
        compiler_params=pltpu.CompilerParams(has_side_effects=SPLIT_COPY),
    )(*[pltpu.with_memory_space_constraint(s, pltpu.HBM) for s in sums],
      *[pltpu.with_memory_space_constraint(lax.empty(shp, s.dtype), pltpu.HBM) for shp, s in zip(land_shapes, sums)])
    return res[0], res[1], list(res[2:2 + n]), list(res[2 + n:2 + 2 * n]), res[-1]


def _scatter_wait(name, send_sem, recv_sem, sums, lands, after):
    n = len(sums)

    def body(*refs):
        srcs, land_refs = refs[:n], refs[n:2 * n]
        send_ref, recv_ref = refs[2 * n], refs[2 * n + 1]
        for i in range(n):
            for j in range(3):
                copy = _scatter_copy(srcs, land_refs, send_ref, recv_ref, i, j)
                copy.wait_send()
                copy.wait_recv()

    res = pl.pallas_call(
        body,
        name=name,
        in_specs=[HBM] * (2 * n) + [SEM, SEM, ANY],
        out_specs=[HBM] * (2 * n),
        out_shape=[pltpu.HBM(s.shape, s.dtype) for s in sums] + [pltpu.HBM(l.shape, l.dtype) for l in lands],
        input_output_aliases={i: i for i in range(2 * n)},
        compiler_params=pltpu.CompilerParams(has_side_effects=SPLIT_COPY),
    )(*sums, *lands, send_sem, recv_sem, after)
    return list(res[:n]), list(res[n:])


def _pair_join(name, halves, small=None):
    n = len(halves)
    if small is None:
        def body_plain(*refs):
            ins, outs = refs[:n], refs[n:2 * n]
            send_sem, recv_sem = refs[2 * n:]
            x, y, c = _position()
            swaps = [pltpu.make_async_remote_copy(
                src_ref=ins[i], dst_ref=outs[i], send_sem=send_sem.at[i], recv_sem=recv_sem.at[i],
                device_id=(x, y, 1 - c), device_id_type=MESH) for i in range(n)]
            for swap in swaps:
                swap.start()
            for swap in swaps:
                swap.wait()

        return pl.pallas_call(
            body_plain,
            name=name,
            in_specs=[ANY] * n,
            out_specs=[ANY] * n,
            out_shape=[jax.ShapeDtypeStruct(h.shape, h.dtype) for h in halves],
            scratch_shapes=[pltpu.SemaphoreType.DMA((n,))] * 2,
        )(*halves)

    def body(*refs):
        ins, small_ref = refs[:n], refs[n]
        outs, all_ref = refs[n + 1:2 * n + 1], refs[2 * n + 1]
        send_sem, recv_sem, sm_send, sm_recv, sm_local = refs[2 * n + 2:]
        x, y, c = _position()
        swaps = []
        for i in range(n):
            swap = pltpu.make_async_remote_copy(
                src_ref=ins[i], dst_ref=outs[i], send_sem=send_sem.at[i], recv_sem=recv_sem.at[i],
                device_id=(x, y, 1 - c), device_id_type=MESH)
            swap.start()
            swaps.append(swap)
        me = 4 * x + 2 * y + c
        sm_own = pltpu.make_async_copy(small_ref, all_ref.at[me], sm_local)
        sm_own.start()
        pushes, arrivals = [], []
        for mask in range(1, N_DEV):
            px, py, pc = x ^ (mask >> 2), y ^ ((mask >> 1) & 1), c ^ (mask & 1)
            pushes.append(pltpu.make_async_remote_copy(
                src_ref=small_ref, dst_ref=all_ref.at[me], send_sem=sm_send.at[mask - 1], recv_sem=sm_recv.at[mask - 1],
                device_id=(px, py, pc), device_id_type=MESH))
            arrivals.append(pltpu.make_async_remote_copy(
                src_ref=small_ref, dst_ref=all_ref.at[4 * px + 2 * py + pc], send_sem=sm_send.at[mask - 1],
                recv_sem=sm_recv.at[mask - 1], device_id=(px, py, pc), device_id_type=MESH))
        for cp in pushes:
            cp.start()
        for swap in swaps:
            swap.wait()
        for cp in arrivals:
            cp.wait_recv()
        for cp in pushes:
            cp.wait_send()
        sm_own.wait()

    res = pl.pallas_call(
        body,
        name=name,
        in_specs=[ANY] * (n + 1),
        out_specs=[ANY] * (n + 1),
        out_shape=[jax.ShapeDtypeStruct(h.shape, h.dtype) for h in halves]
        + [jax.ShapeDtypeStruct((N_DEV,) + small.shape, small.dtype)],
        scratch_shapes=[pltpu.SemaphoreType.DMA((n,))] * 2 + [pltpu.SemaphoreType.DMA((N_DEV - 1,))] * 2
        + [pltpu.SemaphoreType.DMA(())],
    )(*halves, small)
    return res[:n], res[n]


def _lower_bound(lbp):
    return jax.nn.softmax(lbp, axis=0)[0:1]


def _local_step(x, target, g1, gm, g2, gq, gk, go, rel_bias, lbp, first_weights, mid_weights, last_weights, on_grads, grads_sent):
    b, s, d = x.shape
    t = b * s
    x0 = x.reshape(t, d)
    tgt = target.reshape(t, d)
    gq_t = jnp.tile(gq, (1, ATTN_HEADS))
    gk_t = jnp.tile(gk, (1, ATTN_HEADS))
    lb = _lower_bound(lbp)
    table = _band_table(_rel_bias_table("rel_bias_table", rel_bias))

    h1 = _rmsnorm_fwd("norm1", x0, g1)
    wg1, wu1, deps1 = first_weights((h1, table))
    a1, b1, z1 = _ffn_up("ffn1_up", h1, wg1, wu1, deps1)
    wd1, w_in, w_out = mid_weights((z1,))
    ns = w_in.shape[0]
    x1 = _ffn_down("ffn1_down", z1, wd1, x0)
    h2 = _rmsnorm_fwd("norm_mix", x1, gm)
    proj = _in_proj("in_proj", h2, w_in)
    proj3 = proj.reshape(b, s, proj.shape[1])
    qn, kn, vb = _qk_prep("qk_prep", proj3, gq_t, gk_t)
    attn = _attn_fwd("attn_fwd", qn, kn, vb, table).reshape(t, ATTN_W)
    mix, oraw, states = _hgrn_fwd("hgrn_fwd", proj, attn, lb, go, b, s)
    x2 = _out_proj("out_proj", mix, w_out, x1)
    h3 = _rmsnorm_fwd("norm2", x2, g2)
    wg2, wu2, wd2 = last_weights((h3,))
    a2, b2, z2 = _ffn_up("ffn2_up", h3, wg2, wu2)
    dy, dyh, sq = _ffn_down_loss("ffn2_down_loss", z2, wd2, x2, tgt)
    loss = 0.5 * jnp.sum(sq) / d

    da2, db2 = _ffn_bwd_act("ffn2_bwd_act", dyh, wd2, a2, b2)
    dwd2 = _grad_w_shardrows("ffn2_dwd", z2, dyh)
    dwg2 = _grad_w_shardrows("ffn2_dwg", da2, h3)
    dwu2 = _grad_w_shardrows("ffn2_dwu", db2, h3)
    sent2 = on_grads("ffn2", {"ffn2_w_gate": dwg2, "ffn2_w_up": dwu2, "ffn2_w_down": dwd2})
    dx2, dx2b, dg2 = _ffn_bwd_in("ffn2_bwd_in", da2, db2, wg2, wu2, x2, g2, dy, 1.0, sent2)
    sent2 = grads_sent("ffn2", dx2b)

    dwout = _grad_w_out("dw_out", mix, dx2b)
    dmix = _out_proj_bwd("out_proj_bwd", dx2b, w_out, sent2)
    dqn, dkn, dvn, dbe, dbo = _attn_bwd("attn_bwd", qn, kn, vb, table, dmix.reshape(b, s, dmix.shape[1]))
    dbias = dbe[:, :, :BAND] + dbo[:, :, CHUNK:]
    dpq, dpk, dpv, dgq, dgk = _qk_prep_bwd("qk_prep_bwd", proj3, dqn, dkn, dvn, gq_t, gk_t)
    dpq, dpk, dpv = (a.reshape(t, ATTN_W) for a in (dpq, dpk, dpv))
    dproj, dlb, dgo = _hgrn_bwd("hgrn_bwd", proj, (dpq, dpk, dpv), lb, go, oraw, states, dmix, b, s)
    dwin = _grad_w_in("dw_in", h2, dproj, ns)
    dx1, dx1h, dgm = _in_proj_bwd("in_proj_bwd", dproj, w_in, x1, gm, dx2, 0.5)

    dwd1 = _grad_w_shardrows("ffn1_dwd", z1, dx1h)
    sent_mix = on_grads("mix", {"w_in": dwin, "w_out": dwout.reshape(ns, dwout.shape[0] // ns, d),
                                "ffn1_w_down": dwd1})
    da1, db1 = _ffn_bwd_act("ffn1_bwd_act", dx1h, wd1, a1, b1, sent_mix)
    sent_mix = grads_sent("mix", da1)
    dwg1 = _grad_w_shardrows("ffn1_dwg", da1, h1, sent_mix)
    dwu1 = _grad_w_shardrows("ffn1_dwu", db1, h1)
    on_grads("ffn1", {"ffn1_w_gate": dwg1, "ffn1_w_up": dwu1})
    sent1 = grads_sent("ffn1", None)
    dx0, dg1 = _ffn_bwd_in("ffn1_bwd_in", da1, db1, wg1, wu1, x0, g1, dx1, None, sent1)

    nt = dg1.shape[0]
    sg = _small_grads(
        "small_grads", dg1.reshape(nt, d), dgm.reshape(nt, d), dg2.reshape(nt, d),
        dgq.reshape(-1, ATTN_W), dgk.reshape(-1, ATTN_W), dbias.transpose(1, 0, 2),
        dlb.reshape(b, HGRN_W), dgo.reshape(b, HGRN_W), lbp)
    g1g, gmg, g2g, gqg, gkg, rbg, lbg, gog = sg
    small = _pack_small(g1g, gmg, g2g, lbg, rbg[:, :N_REL], gqg, gkg, gog)
    return loss, dx0.reshape(b, s, d), small


def _pack_small(g1, gm, g2, lbp, rel_bias, gq, gk, go):
    flat = [g1.reshape(-1), gm.reshape(-1), g2.reshape(-1), lbp.reshape(-1), rel_bias.reshape(-1)]
    n_bias = 3 * SMALL_COLS - rel_bias.size
    heads = [gq.reshape(-1), gk.reshape(-1), go.reshape(-1)]
    n_tail = SMALL_COLS - sum(h.size for h in heads)
    return jnp.concatenate(flat + [jnp.zeros((n_bias,), F32)] + heads + [jnp.zeros((n_tail,), F32)]).reshape(
        SMALL_ROWS, SMALL_COLS)


def _unpack_small(p, d):
    flat = p.reshape(-1)
    o = 3 * d
    g1, gm, g2 = p[0:1], p[1:2], p[2:3]
    lbp = flat[o:o + 2 * HGRN_W].reshape(2, HGRN_W)
    o = 4 * SMALL_COLS
    rel = flat[o:o + ATTN_HEADS * N_REL].reshape(1, ATTN_HEADS, N_REL)
    o = 7 * SMALL_COLS
    gq = flat[o:o + ATTN_DH].reshape(1, ATTN_DH)
    gk = flat[o + ATTN_DH:o + 2 * ATTN_DH].reshape(1, ATTN_DH)
    go = flat[o + 2 * ATTN_DH:o + 2 * ATTN_DH + HGRN_DH].reshape(1, HGRN_DH)
    return g1, gm, g2, gq, gk, rel, lbp, go


def kernel(x, ffn1_norm_g, ffn1_w_gate, ffn1_w_up, ffn1_w_down, mix_norm_g, w_in, attn_q_norm_g, attn_k_norm_g, attn_rel_bias, hgrn_lower_bounds, hgrn_out_norm_g, w_out, ffn2_norm_g, ffn2_w_gate, ffn2_w_up, ffn2_w_down, loss_target, m_ffn1_norm_g, m_ffn1_w_gate, m_ffn1_w_up, m_ffn1_w_down, m_mix_norm_g, m_w_in, m_attn_q_norm_g, m_attn_k_norm_g, m_attn_rel_bias, m_hgrn_lower_bounds, m_hgrn_out_norm_g, m_w_out, m_ffn2_norm_g, m_ffn2_w_gate, m_ffn2_w_up, m_ffn2_w_down, v_ffn1_norm_g, v_ffn1_w_gate, v_ffn1_w_up, v_ffn1_w_down, v_mix_norm_g, v_w_in, v_attn_q_norm_g, v_attn_k_norm_g, v_attn_rel_bias, v_hgrn_lower_bounds, v_hgrn_out_norm_g, v_w_out, v_ffn2_norm_g, v_ffn2_w_gate, v_ffn2_w_up, v_ffn2_w_down):
    d = x.shape[-1]
    big_w = [ffn1_w_gate, ffn1_w_up, ffn1_w_down, w_in, w_out, ffn2_w_gate, ffn2_w_up, ffn2_w_down]
    big_m = [m_ffn1_w_gate, m_ffn1_w_up, m_ffn1_w_down, m_w_in, m_w_out, m_ffn2_w_gate, m_ffn2_w_up, m_ffn2_w_down]
    big_v = [v_ffn1_w_gate, v_ffn1_w_up, v_ffn1_w_down, v_w_in, v_w_out, v_ffn2_w_gate, v_ffn2_w_up, v_ffn2_w_down]
    big_names = ["ffn1_w_gate", "ffn1_w_up", "ffn1_w_down", "w_in", "w_out", "ffn2_w_gate", "ffn2_w_up", "ffn2_w_down"]
    flipped = {nm for nm in big_names if nm.endswith("gate") or nm.endswith("up")}
    flip = lambda nm, a: jnp.swapaxes(a, 1, 2) if nm in flipped else a
    big_w, big_m, big_v = ([flip(nm, a) for nm, a in zip(big_names, arrs)] for arrs in (big_w, big_m, big_v))

    shards = [w[0].astype(BF16) for w in big_w]
    start_a = _gather_start("gather_start_up1", shards[:2], ())
    start_b = _gather_start("gather_start_mid", shards[2:5], (start_a[4],))
    start_c = _gather_start("gather_start_ffn2", shards[5:], (start_b[4],))

    def gathered(tag, started, after):
        send_sem, recv_sem, srcs, outs, _ = started
        srcs, outs = _gather_wait("gather_wait_" + tag, send_sem, recv_sem, srcs, outs, after)
        return _gather_join("gather_join_" + tag, srcs, outs)

    def first_weights(after):
        return (*gathered("up1", start_a, after), (start_c[4],))

    def mid_weights(after):
        wd1, win_f, wout_f = gathered("mid", start_b, after)
        return wd1, win_f, wout_f.reshape(wout_f.shape[0] * wout_f.shape[1], d)

    def last_weights(after):
        return gathered("ffn2", start_c, after)

    core = lax.axis_index("c").astype(jnp.int32).reshape(1)
    chip = (2 * lax.axis_index("x") + lax.axis_index("y")).astype(jnp.int32).reshape(1)
    started = {}

    def on_grads(tag, grads):
        names = list(grads)
        started[tag] = (names, _pair_start("pair_start_" + tag, [grads[nm] for nm in names]))
        return (started[tag][1][4],)

    def grads_sent(tag, after):
        names, (send_sem, recv_sem, grads, lands, token) = started[tag]
        grads, theirs = _pair_wait("pair_wait_" + tag, send_sem, recv_sem, grads, lands, token if after is None else after)
        sums = [_pair_sum("pair_sum_" + nm, g, th, core) for nm, g, th in zip(names, grads, theirs)]
        started[tag] = (names, _scatter_start("scatter_start_" + tag, sums))
        return (started[tag][1][4],)

    loss, grad_x, small_g = _local_step(
        x, loss_target, ffn1_norm_g, mix_norm_g, ffn2_norm_g, attn_q_norm_g, attn_k_norm_g, hgrn_out_norm_g,
        attn_rel_bias[0], hgrn_lower_bounds, first_weights, mid_weights, last_weights, on_grads, grads_sent)
    loss = lax.psum(loss, ("x", "y", "c"))

    def finish(tag, after):
        names, (send_sem, recv_sem, sums, lands, _) = started[tag]
        sums, lands = _scatter_wait("scatter_wait_" + tag, send_sem, recv_sem, sums, lands, after)
        return names, [_chip_sum("chip_sum_" + nm, sm, ld, chip) for nm, sm, ld in zip(names, sums, lands)]

    by_name = {nm: (w, m, v) for nm, w, m, v in zip(big_names, big_w, big_m, big_v)}
    updated = {}

    def update(names, halves, other_halves):
        for nm, mine, theirs in zip(names, halves, other_halves):
            w, m, v = by_name[nm]
            updated[nm] = _adamw("adamw_" + nm, w, mine, theirs, m, v, core)

    last_token = started["ffn1"][1][4]
    names_a, halves_a = finish("ffn2", last_token)
    names_m, halves_m = finish("mix", last_token)
    names_a, halves_a = names_a + names_m, halves_a + halves_m
    update(names_a, halves_a, _pair_join("pair_join_early", halves_a))
    names_b, halves_b = finish("ffn1", updated[names_a[-1]][1])
    others_b, small_all = _pair_join("pair_join_last", halves_b, small_g)
    update(names_b, halves_b, others_b)
    big_out = [updated[nm] for nm in big_names]

    pack = lambda g1, gm, g2, gq, gk, rel, lbp, go: _pack_small(g1, gm, g2, lbp, rel[0], gq, gk, go)
    small_w = pack(ffn1_norm_g, mix_norm_g, ffn2_norm_g, attn_q_norm_g, attn_k_norm_g, attn_rel_bias, hgrn_lower_bounds, hgrn_out_norm_g)
    small_m = pack(m_ffn1_norm_g, m_mix_norm_g, m_ffn2_norm_g, m_attn_q_norm_g, m_attn_k_norm_g, m_attn_rel_bias, m_hgrn_lower_bounds, m_hgrn_out_norm_g)
    small_v = pack(v_ffn1_norm_g, v_mix_norm_g, v_ffn2_norm_g, v_attn_q_norm_g, v_attn_k_norm_g, v_attn_rel_bias, v_hgrn_lower_bounds, v_hgrn_out_norm_g)
    small_out = [_unpack_small(p, d) for p in _adamw_small("adamw_small", small_w, small_all, small_m, small_v)]

    def assemble(kind):
        bg = [flip(nm, o[kind]) for nm, o in zip(big_names, big_out)]
        g1, gm, g2, gq, gk, rel, lbp, go = small_out[kind]
        return [g1, bg[0], bg[1], bg[2], gm, bg[3], gq, gk, rel, lbp, go, bg[4], g2, bg[5], bg[6], bg[7]]

    return (loss, grad_x, *assemble(0), *assemble(1), *assemble(2), *assemble(3))
```

```python
import functools

import jax
import jax.numpy as jnp
from jax import lax
from jax.experimental import pallas as pl
from jax.experimental.pallas import tpu as pltpu

F32 = jnp.float32
BF16 = jnp.bfloat16
MESH = pl.DeviceIdType.MESH

N_CHIPS = 4
N_DEV = 8
CHUNK = 64
ATTN_HEADS = 8
ATTN_DH = 64
ATTN_W = ATTN_HEADS * ATTN_DH
HGRN_HEADS = 4
HGRN_DH = 128
HGRN_W = HGRN_HEADS * HGRN_DH
LEFT_CHUNKS = 8
BAND = (LEFT_CHUNKS + 1) * CHUNK
KPAD = LEFT_CHUNKS * CHUNK
REL_CLIP = 128
N_REL = 2 * REL_CLIP + 1
N_REL_PAD = 384
RMS_EPS = 1e-6
LANES = 128
SMALL_ROWS = 8
SMALL_COLS = 1024

ADAM_LR = 0.001
ADAM_B1 = 0.9
ADAM_B2 = 0.999
ADAM_EPS = 1e-08
ADAM_WD = 0.01
ADAM_STEP = 10

NN = (((1,), (0,)), ((), ()))
NT = (((1,), (1,)), ((), ()))
TN = (((0,), (0,)), ((), ()))

VMEM_LIMIT = 48 * 1024 * 1024


def _sigmoid(x):
    return 1.0 / (1.0 + jnp.exp(-x))


def _silu(x):
    return x * _sigmoid(x)


def _dot(a, b, dims=NN):
    return lax.dot_general(a, b, dims, preferred_element_type=F32)


def _split3(x):
    hi = x.astype(BF16)
    r1 = x - hi.astype(F32)
    mid = r1.astype(BF16)
    lo = (r1 - mid.astype(F32)).astype(BF16)
    return hi, mid, lo


def _dot_exact_rhs(x, mat, dims=NN):
    hi, mid, lo = _split3(x)
    return _dot(hi, mat, dims) + _dot(mid, mat, dims) + _dot(lo, mat, dims)


def _dot_exact_lhs(mat, x, dims=NN):
    hi, mid, lo = _split3(x)
    return _dot(mat, hi, dims) + _dot(mat, mid, dims) + _dot(mat, lo, dims)


def _params(*sem):
    return pltpu.CompilerParams(dimension_semantics=sem, vmem_limit_bytes=VMEM_LIMIT)


def _row_halves(tm):
    return [slice(0, tm // 2), slice(tm // 2, tm)]


def _epilogue_by_halves(tm, accs_by_half, ex_refs, out_refs, epilogue):
    sums = {}
    for rows, accs in zip(_row_halves(tm), accs_by_half):
        res = epilogue(accs, [e[rows, :] if e.shape[0] == tm else e[...] for e in ex_refs])
        for i, (o, r) in enumerate(zip(out_refs, res)):
            if o.shape[0] == tm:
                o[rows, :] = r.astype(o.dtype)
            else:
                sums[i] = r if i not in sums else sums[i] + r
    for i, r in sums.items():
        out_refs[i][...] = r.astype(out_refs[i].dtype)


def _mm(name, ins, terms, n_acc, grid, acc_shape, outs, epilogue, extras=(), deps=(), split_rows=0):
    nk = grid[2]
    ni, ne, nd, no = len(ins), len(extras), len(deps), len(outs)

    def body(*refs):
        in_refs = refs[:ni]
        ex_refs = refs[ni:ni + ne]
        out_refs = refs[ni + ne + nd:ni + ne + nd + no]
        acc_refs = refs[ni + ne + nd + no:]
        if split_rows:
            by_half = []
            for rows in _row_halves(split_rows):
                half = [None] * n_acc
                for ai, li, ri, dims in terms:
                    d = _dot(in_refs[li][rows, :], in_refs[ri][...], dims)
                    half[ai] = d if half[ai] is None else half[ai] + d
                by_half.append(half)
            _epilogue_by_halves(split_rows, by_half, ex_refs, out_refs, epilogue)
            return
        parts = [None] * n_acc
        for ai, li, ri, dims in terms:
            d = _dot(in_refs[li][...], in_refs[ri][...], dims)
            parts[ai] = d if parts[ai] is None else parts[ai] + d

        def finish(accs):
            res = epilogue(accs, [e[...] for e in ex_refs])
            for o, r in zip(out_refs, res):
                o[...] = r.astype(o.dtype)

        if nk == 1:
            finish(parts)
        else:
            k = pl.program_id(2)

            @pl.when(k == 0)
            def _():
                for a, p in zip(acc_refs, parts):
                    a[...] = p

            @pl.when(k > 0)
            def _():
                for a, p in zip(acc_refs, parts):
                    a[...] += p

            @pl.when(k == nk - 1)
            def _():
                finish([a[...] for a in acc_refs])

    scratch = [] if nk == 1 else [pltpu.VMEM(acc_shape, F32) for _ in range(n_acc)]
    res = pl.pallas_call(
        body,
        name=name,
        grid=grid,
        in_specs=[s for _, s in ins] + [s for _, s in extras] + [pl.BlockSpec(memory_space=pl.ANY)] * nd,
        out_specs=[s for _, s in outs],
        out_shape=[o for o, _ in outs],
        scratch_shapes=scratch,
        compiler_params=_params("parallel", "parallel", "arbitrary"),
    )(*[a for a, _ in ins], *[a for a, _ in extras], *deps)
    return res


def _mm_rows(name, lhs, weights, dims, t, outs, epilogue, extras=(), deps=()):
    tm = _row_tile(t)
    nl, ne, nd, no = len(lhs), len(extras), len(deps), len(outs)
    ns = weights[0].shape[0]

    def body(*refs):
        lhs_refs = refs[:nl]
        w_hbm = refs[nl:2 * nl]
        ex_refs = refs[2 * nl:2 * nl + ne]
        out_refs = refs[2 * nl + ne + nd:2 * nl + ne + nd + no]
        w_vmem = refs[2 * nl + ne + nd + no:3 * nl + ne + nd + no]
        sem = refs[-1]

        @pl.when(pl.program_id(0) == 0)
        def _():
            copies = [pltpu.make_async_copy(w_hbm[p], w_vmem[p], sem.at[p]) for p in range(nl)]
            for cp in copies:
                cp.start()
            for cp in copies:
                cp.wait()

        acc = None
        for p in range(nl):
            pick = lhs[p][2]
            for j in range(ns):
                part = _dot(pick(lhs_refs[p], j, slice(None)), w_vmem[p][j], dims)
                acc = part if acc is None else acc + part
        res = epilogue([acc], [e[...] for e in ex_refs])
        for o, r in zip(out_refs, res):
            o[...] = r.astype(o.dtype)

    return pl.pallas_call(
        body,
        name=name,
        grid=(t // tm,),
        in_specs=[s for _, s, _ in lhs] + [pl.BlockSpec(memory_space=pl.ANY)] * nl + [s for _, s in extras]
        + [pl.BlockSpec(memory_space=pl.ANY)] * nd,
        out_specs=[s for _, s in outs],
        out_shape=[o for o, _ in outs],
        scratch_shapes=[pltpu.VMEM(w.shape, w.dtype) for w in weights] + [pltpu.SemaphoreType.DMA((nl,))],
        compiler_params=_params("arbitrary"),
    )(*[a for a, _, _ in lhs], *weights, *[a for a, _ in extras], *deps)


def _row_tile(t):
    return 512 if t % 512 == 0 else t


def _k_tile(t):
    return t if t <= 4096 else 1024


def _rmsnorm_fwd(name, x, g):
    t, d = x.shape
    tm = _row_tile(t)

    def body(x_ref, g_ref, h_ref):
        xv = x_ref[...]
        ms = jnp.mean(xv * xv, axis=-1, keepdims=True)
        h_ref[...] = (xv * lax.rsqrt(ms + RMS_EPS) * g_ref[...]).astype(BF16)

    return pl.pallas_call(
        body,
        name=name,
        grid=(t // tm,),
        in_specs=[pl.BlockSpec((tm, d), lambda i: (i, 0)), pl.BlockSpec((1, d), lambda i: (0, 0))],
        out_specs=pl.BlockSpec((tm, d), lambda i: (i, 0)),
        out_shape=jax.ShapeDtypeStruct((t, d), BF16),
        compiler_params=_params("parallel"),
    )(x, g)


def _norm_bwd_epilogue(copy_scale):
    def epilogue(accs, ex):
        dh = accs[0]
        xv, g, dres = ex
        ms = jnp.mean(xv * xv, axis=-1, keepdims=True)
        rstd = lax.rsqrt(ms + RMS_EPS)
        xhat = xv * rstd
        dxhat = dh * g
        dx = rstd * (dxhat - xhat * jnp.mean(dxhat * xhat, axis=-1, keepdims=True))
        out = dres + dx
        dg = jnp.sum(dh * xhat, axis=0, keepdims=True)
        if copy_scale is None:
            return out, dg
        return out, out * copy_scale, dg

    return epilogue


def _ffn_up(name, h, wg, wu, deps=()):
    t, d = h.shape
    ns, f, _ = wg.shape
    tm = _row_tile(t)

    def epilogue(accs, ex):
        a, b = accs
        sg = _sigmoid(a)
        act = a * sg
        return act, b * (sg * (1.0 + a * (1.0 - sg))), act * b

    w_spec = pl.BlockSpec((None, f, d), lambda j, i, k: (j, 0, 0))
    o_spec = pl.BlockSpec((None, tm, f), lambda j, i, k: (j, i, 0))
    o_shape = jax.ShapeDtypeStruct((ns, t, f), BF16)
    return _mm(
        name,
        ins=[(h, pl.BlockSpec((tm, d), lambda j, i, k: (i, 0))), (wg, w_spec), (wu, w_spec)],
        terms=[(0, 0, 1, NT), (1, 0, 2, NT)],
        n_acc=2,
        grid=(ns, t // tm, 1),
        acc_shape=(tm, f),
        outs=[(o_shape, o_spec)] * 3,
        epilogue=epilogue,
        deps=deps,
        split_rows=tm,
    )


def _shard_rows(arr, tm):
    ns, _, f = arr.shape
    return arr, pl.BlockSpec((ns, tm, f), lambda i: (0, i, 0)), lambda ref, j, rows: ref[j, rows, :]


def _ffn_down(name, z, wd, x):
    _, t, _ = z.shape
    d = wd.shape[2]
    tm = _row_tile(t)
    row = pl.BlockSpec((tm, d), lambda i: (i, 0))
    return _mm_rows(
        name, [_shard_rows(z, tm)], [wd], NN, t,
        outs=[(jax.ShapeDtypeStruct((t, d), F32), row)],
        epilogue=lambda accs, ex: (ex[0] + 0.5 * accs[0],),
        extras=[(x, row)],
    )[0]


def _ffn_down_loss(name, z, wd, x, target):
    _, t, _ = z.shape
    d = wd.shape[2]
    tm = _row_tile(t)
    nt = t // tm
    row = pl.BlockSpec((tm, d), lambda i: (i, 0))

    def epilogue(accs, ex):
        e = ex[0] + 0.5 * accs[0] - ex[1]
        dy = e * (1.0 / d)
        return dy, 0.5 * dy, jnp.sum(e * e, axis=0, keepdims=True)

    return _mm_rows(
        name, [_shard_rows(z, tm)], [wd], NN, t,
        outs=[(jax.ShapeDtypeStruct((t, d), F32), row), (jax.ShapeDtypeStruct((t, d), BF16), row),
              (jax.ShapeDtypeStruct((nt, 1, d), F32), pl.BlockSpec((None, 1, d), lambda i: (i, 0, 0)))],
        epilogue=epilogue,
        extras=[(x, row), (target, row)],
    )


def _ffn_bwd_act(name, dout, wd, act_a, dact_b, deps=()):
    t, d = dout.shape
    ns, f, _ = wd.shape
    tm = _row_tile(t)

    def epilogue(accs, ex):
        dz = accs[0]
        return dz * ex[1].astype(F32), dz * ex[0].astype(F32)

    act = pl.BlockSpec((None, tm, f), lambda j, i, k: (j, i, 0))
    o_shape = jax.ShapeDtypeStruct((ns, t, f), BF16)
    return _mm(
        name,
        ins=[(dout, pl.BlockSpec((tm, d), lambda j, i, k: (i, 0))),
             (wd, pl.BlockSpec((None, f, d), lambda j, i, k: (j, 0, 0)))],
        terms=[(0, 0, 1, NT)],
        n_acc=1,
        grid=(ns, t // tm, 1),
        acc_shape=(tm, f),
        outs=[(o_shape, act)] * 2,
        epilogue=epilogue,
        extras=[(act_a, act), (dact_b, act)],
        deps=deps,
        split_rows=tm,
    )


def _grad_w_shardrows(name, z, dout, deps=()):
    ns, t, f = z.shape
    d = dout.shape[1]
    tk = _k_tile(t)
    return _mm(
        name,
        ins=[(z, pl.BlockSpec((None, tk, f), lambda j, n, k: (j, k, 0))),
             (dout, pl.BlockSpec((tk, d), lambda j, n, k: (k, 0)))],
        terms=[(0, 0, 1, TN)],
        n_acc=1,
        grid=(ns, 1, t // tk),
        acc_shape=(f, d),
        outs=[(jax.ShapeDtypeStruct((ns, f, d), BF16), pl.BlockSpec((None, f, d), lambda j, n, k: (j, 0, 0)))],
        epilogue=lambda accs, ex: (accs[0],),
        deps=deps,
    )[0]


def _norm_bwd_outs(t, d, tm, copy_scale):
    row = pl.BlockSpec((tm, d), lambda i: (i, 0))
    outs = [(jax.ShapeDtypeStruct((t, d), F32), row)]
    if copy_scale is not None:
        outs.append((jax.ShapeDtypeStruct((t, d), BF16), row))
    outs.append((jax.ShapeDtypeStruct((t // tm, 1, d), F32), pl.BlockSpec((None, 1, d), lambda i: (i, 0, 0))))
    return row, outs


def _ffn_bwd_in(name, da, db, wg, wu, x, g, dres, copy_scale, deps=()):
    _, t, _ = da.shape
    d = wg.shape[2]
    tm = _row_tile(t)
    row, outs = _norm_bwd_outs(t, d, tm, copy_scale)
    return _mm_rows(
        name, [_shard_rows(da, tm), _shard_rows(db, tm)], [wg, wu], NN, t,
        outs=outs,
        epilogue=_norm_bwd_epilogue(copy_scale),
        extras=[(x, row), (g, pl.BlockSpec((1, d), lambda i: (0, 0))), (dres, row)],
        deps=deps,
    )


def _in_proj(name, h, w_in):
    t, d = h.shape
    ns, _, pj = w_in.shape
    tm = _row_tile(t)
    return _mm(
        name,
        ins=[(h, pl.BlockSpec((tm, d), lambda j, i, k: (i, 0))),
             (w_in, pl.BlockSpec((None, d, pj), lambda j, i, k: (j, 0, 0)))],
        terms=[(0, 0, 1, NN)],
        n_acc=1,
        grid=(ns, t // tm, 1),
        acc_shape=(tm, pj),
        outs=[(jax.ShapeDtypeStruct((t, ns * pj), F32), pl.BlockSpec((tm, pj), lambda j, i, k: (i, j)))],
        epilogue=lambda accs, ex: (accs[0],),
        split_rows=tm,
    )[0]


def _in_proj_bwd(name, dp, w_in, x, g, dres, copy_scale, deps=()):
    t = dp.shape[0]
    ns, d, pj = w_in.shape
    tm = _row_tile(t)
    row, outs = _norm_bwd_outs(t, d, tm, copy_scale)
    cols = (dp, pl.BlockSpec((tm, ns * pj), lambda i: (i, 0)), lambda ref, j, rows: ref[rows, j * pj:(j + 1) * pj])
    return _mm_rows(
        name, [cols], [w_in], NT, t,
        outs=outs,
        epilogue=_norm_bwd_epilogue(copy_scale),
        extras=[(x, row), (g, pl.BlockSpec((1, d), lambda i: (0, 0))), (dres, row)],
        deps=deps,
    )


def _grad_w_in(name, h, dp, ns):
    t, d = h.shape
    pj = dp.shape[1] // ns
    tk = _k_tile(t)
    return _mm(
        name,
        ins=[(h, pl.BlockSpec((tk, d), lambda j, n, k: (k, 0))),
             (dp, pl.BlockSpec((tk, pj), lambda j, n, k: (k, j)))],
        terms=[(0, 0, 1, TN)],
        n_acc=1,
        grid=(ns, 1, t // tk),
        acc_shape=(d, pj),
        outs=[(jax.ShapeDtypeStruct((ns, d, pj), BF16), pl.BlockSpec((None, d, pj), lambda j, n, k: (j, 0, 0)))],
        epilogue=lambda accs, ex: (accs[0],),
    )[0]


def _out_proj(name, mix, w_out, x):
    t, dm = mix.shape
    d = w_out.shape[1]
    tm = _row_tile(t)
    row = pl.BlockSpec((tm, d), lambda i, n, k: (i, 0))
    return _mm(
        name,
        ins=[(mix, pl.BlockSpec((tm, dm), lambda i, n, k: (i, 0))),
             (w_out, pl.BlockSpec((dm, d), lambda i, n, k: (0, 0)))],
        terms=[(0, 0, 1, NN)],
        n_acc=1,
        grid=(t // tm, 1, 1),
        acc_shape=(tm, d),
        outs=[(jax.ShapeDtypeStruct((t, d), F32), row)],
        epilogue=lambda accs, ex: (ex[0] + accs[0],),
        extras=[(x, row)],
    )[0]


def _out_proj_bwd(name, dx, w_out, deps=()):
    t, d = dx.shape
    dm = w_out.shape[0]
    tm = _row_tile(t)
    return _mm(
        name,
        ins=[(dx, pl.BlockSpec((tm, d), lambda i, n, k: (i, 0))),
             (w_out, pl.BlockSpec((dm, d), lambda i, n, k: (0, 0)))],
        terms=[(0, 0, 1, NT)],
        n_acc=1,
        grid=(t // tm, 1, 1),
        acc_shape=(tm, dm),
        outs=[(jax.ShapeDtypeStruct((t, dm), F32), pl.BlockSpec((tm, dm), lambda i, n, k: (i, 0)))],
        epilogue=lambda accs, ex: (accs[0],),
        deps=deps,
    )[0]


def _grad_w_out(name, mix, dx):
    t, dm = mix.shape
    d = dx.shape[1]
    tk = _k_tile(t)
    return _mm(
        name,
        ins=[(mix, pl.BlockSpec((tk, dm), lambda a, n, k: (k, 0))),
             (dx, pl.BlockSpec((tk, d), lambda a, n, k: (k, 0)))],
        terms=[(0, 0, 1, TN)],
        n_acc=1,
        grid=(1, 1, t // tk),
        acc_shape=(dm, d),
        outs=[(jax.ShapeDtypeStruct((dm, d), BF16), pl.BlockSpec((dm, d), lambda a, n, k: (0, 0)))],
        epilogue=lambda accs, ex: (accs[0],),
    )[0]


def _head_group_matrix():
    r = lax.broadcasted_iota(jnp.int32, (ATTN_W, ATTN_W), 0)
    c = lax.broadcasted_iota(jnp.int32, (ATTN_W, ATTN_W), 1)
    same = jnp.right_shift(r, 6) == jnp.right_shift(c, 6)
    return jnp.where(same, 1.0, 0.0).astype(BF16)


def _qk_prep(name, proj, gq, gk):
    b, s, _ = proj.shape
    tm = KPAD
    nb = s // tm

    def body(q_ref, k_ref, v_ref, gq_ref, gk_ref, qn_ref, kn_ref, vb_ref):
        j = pl.program_id(1)
        bd = _head_group_matrix()

        def norm(xv, g):
            ms = _dot_exact_rhs(xv * xv, bd) * (1.0 / ATTN_DH)
            return xv * lax.rsqrt(ms + RMS_EPS) * g

        @pl.when(j == 0)
        def _():
            kn_ref[...] = jnp.zeros_like(kn_ref)
            vb_ref[...] = jnp.zeros_like(vb_ref)

        @pl.when(j > 0)
        def _():
            qn_ref[...] = norm(q_ref[...], gq_ref[...]).astype(BF16)
            kn_ref[...] = norm(k_ref[...], gk_ref[...]).astype(BF16)
            vb_ref[...] = v_ref[...].astype(BF16)

    src_blk = lambda col: pl.BlockSpec((None, tm, ATTN_W), lambda bi, j: (bi, jnp.maximum(j - 1, 0), col))
    gspec = pl.BlockSpec((1, ATTN_W), lambda bi, j: (0, 0))
    padded = pl.BlockSpec((None, tm, ATTN_W), lambda bi, j: (bi, j, 0))
    return pl.pallas_call(
        body,
        name=name,
        grid=(b, nb + 1),
        in_specs=[src_blk(0), src_blk(1), src_blk(2), gspec, gspec],
        out_specs=[src_blk(0), padded, padded],
        out_shape=[jax.ShapeDtypeStruct((b, s, ATTN_W), BF16), jax.ShapeDtypeStruct((b, KPAD + s, ATTN_W), BF16),
                   jax.ShapeDtypeStruct((b, KPAD + s, ATTN_W), BF16)],
        compiler_params=_params("parallel", "arbitrary"),
    )(proj, proj, proj, gq, gk)


def _qk_prep_bwd(name, proj, dqn, dkn, dv, gq, gk):
    b, s, _ = proj.shape
    tm = KPAD
    nb = s // tm

    def body(q_ref, k_ref, dqn_ref, dkn_ref, dv_ref, gq_ref, gk_ref, dq_ref, dk_ref, dvb_ref, dgq_ref, dgk_ref):
        bd = _head_group_matrix()

        def bwd(xv, dy, g):
            ms = _dot_exact_rhs(xv * xv, bd) * (1.0 / ATTN_DH)
            rstd = lax.rsqrt(ms + RMS_EPS)
            xhat = xv * rstd
            dxhat = dy * g
            gm = _dot_exact_rhs(dxhat * xhat, bd) * (1.0 / ATTN_DH)
            return rstd * (dxhat - xhat * gm), jnp.sum(dy * xhat, axis=0, keepdims=True)

        dq, dgq = bwd(q_ref[...], dqn_ref[...], gq_ref[...])
        dk, dgk = bwd(k_ref[...], dkn_ref[...], gk_ref[...])
        dq_ref[...] = dq.astype(BF16)
        dk_ref[...] = dk.astype(BF16)
        dvb_ref[...] = dv_ref[...].astype(BF16)
        dgq_ref[...] = dgq
        dgk_ref[...] = dgk

    col = lambda c: pl.BlockSpec((None, tm, ATTN_W), lambda bi, j: (bi, j, c))
    past_pad = pl.BlockSpec((None, tm, ATTN_W), lambda bi, j: (bi, j + 1, 0))
    gspec = pl.BlockSpec((1, ATTN_W), lambda bi, j: (0, 0))
    pspec = pl.BlockSpec((None, 1, ATTN_W), lambda bi, j: (bi * nb + j, 0, 0))
    o_shape = jax.ShapeDtypeStruct((b, s, ATTN_W), BF16)
    p_shape = jax.ShapeDtypeStruct((b * nb, 1, ATTN_W), F32)
    return pl.pallas_call(
        body,
        name=name,
        grid=(b, nb),
        in_specs=[col(0), col(1), col(0), past_pad, past_pad, gspec, gspec],
        out_specs=[col(0)] * 3 + [pspec] * 2,
        out_shape=[o_shape] * 3 + [p_shape] * 2,
        compiler_params=_params("parallel", "parallel"),
    )(proj, proj, dqn, dkn, dv, gq, gk)


Q_CHUNKS = 4
QBLK = Q_CHUNKS * CHUNK
WIN = (LEFT_CHUNKS + Q_CHUNKS) * CHUNK
DB_W = BAND + CHUNK
MASKED = -1e30


def _band_table(bias):
    rows = [jnp.pad(bias, ((0, 0), (0, 0), (CHUNK * i, WIN - BAND - CHUNK * i)), constant_values=MASKED)
            for i in range(Q_CHUNKS)]
    return jnp.concatenate(rows, axis=1)


def _head_lanes(hh):
    lane = lax.broadcasted_iota(jnp.int32, (1, LANES), 1)
    return (lane < ATTN_DH) if hh == 0 else (lane >= ATTN_DH)


def _attn_probs(qh, kw, table, start):
    s = _dot(qh, kw, NT) * (ATTN_DH ** -0.5) + table
    col = lax.broadcasted_iota(jnp.int32, (QBLK, WIN), 1)
    s = jnp.where(col + start >= KPAD, s, MASKED)
    m = jnp.max(s, axis=-1, keepdims=True)
    p = jnp.exp(s - m)
    return p * (1.0 / jnp.sum(p, axis=-1, keepdims=True))


def _attn_fwd(name, q, k, v, table):
    b, s, w = q.shape
    sp = k.shape[1]

    def body(q_ref, k_ref, v_ref, t_ref, o_ref):
        start = pl.multiple_of(pl.program_id(2) * QBLK, QBLK)
        kw = k_ref[pl.ds(start, WIN), :]
        vw = v_ref[pl.ds(start, WIN), :]
        q2 = q_ref[...]
        lanes = [_head_lanes(hh) for hh in range(2)]
        probs = [_attn_probs(jnp.where(mine, q2, jnp.zeros_like(q2)), kw, t_ref[hh], start).astype(BF16)
                 for hh, mine in enumerate(lanes)]
        outs = [_dot(p, vw) for p in probs]
        o_ref[...] = jnp.where(lanes[0], outs[0], outs[1]).astype(BF16)

    qspec = pl.BlockSpec((None, QBLK, LANES), lambda p, bi, i: (bi, i, p))
    kspec = pl.BlockSpec((None, sp, LANES), lambda p, bi, i: (bi, 0, p))
    return pl.pallas_call(
        body,
        name=name,
        grid=(w // LANES, b, s // QBLK),
        in_specs=[qspec, kspec, kspec, pl.BlockSpec((2, QBLK, WIN), lambda p, bi, i: (p, 0, 0))],
        out_specs=qspec,
        out_shape=jax.ShapeDtypeStruct((b, s, w), BF16),
        compiler_params=_params("parallel", "parallel", "arbitrary"),
    )(q, k, v, table)


def _attn_bwd(name, q, k, v, table, dmix):
    b, s, w = q.shape
    sp = k.shape[1]

    def body(q_ref, k_ref, v_ref, t_ref, do_ref, dq_ref, dk_ref, dv_ref, dbe_ref, dbo_ref):
        bi = pl.program_id(1)
        i = pl.program_id(2)
        start = pl.multiple_of(i * QBLK, QBLK)
        win = pl.ds(start, WIN)

        @pl.when(i == 0)
        def _():
            dk_ref[...] = jnp.zeros_like(dk_ref)
            dv_ref[...] = jnp.zeros_like(dv_ref)

        @pl.when(jnp.logical_and(i == 0, bi == 0))
        def _():
            dbe_ref[...] = jnp.zeros_like(dbe_ref)
            dbo_ref[...] = jnp.zeros_like(dbo_ref)

        kw = k_ref[win, :]
        vw = v_ref[win, :]
        q2 = q_ref[...]
        do2 = do_ref[...].astype(BF16)
        lanes = [_head_lanes(hh) for hh in range(2)]
        qh = [jnp.where(mine, q2, jnp.zeros_like(q2)) for mine in lanes]
        doh = [jnp.where(mine, do2, jnp.zeros_like(do2)) for mine in lanes]
        p = [_attn_probs(qh[hh], kw, t_ref[hh], start) for hh in range(2)]
        dp = [_dot(doh[hh], vw, NT) for hh in range(2)]
        ds = [p[hh] * (dp[hh] - jnp.sum(p[hh] * dp[hh], axis=-1, keepdims=True)) for hh in range(2)]
        dsb = [(x * (ATTN_DH ** -0.5)).astype(BF16) for x in ds]
        pb = [x.astype(BF16) for x in p]
        dq = [_dot(dsb[hh], kw) for hh in range(2)]
        dk = [_dot(dsb[hh], qh[hh], TN) for hh in range(2)]
        dv = [_dot(pb[hh], doh[hh], TN) for hh in range(2)]
        for hh in range(2):
            for qi in range(Q_CHUNKS):
                c0 = (qi // 2) * LANES
                blk = ds[hh][qi * CHUNK:(qi + 1) * CHUNK, c0:c0 + DB_W]
                if qi % 2 == 0:
                    dbe_ref[hh] += blk
                else:
                    dbo_ref[hh] += blk
        dq_ref[...] = jnp.where(lanes[0], dq[0], dq[1])
        dk_ref[win, :] += dk[0] + dk[1]
        dv_ref[win, :] += dv[0] + dv[1]

    qspec = pl.BlockSpec((None, QBLK, LANES), lambda p, bi, i: (bi, i, p))
    kspec = pl.BlockSpec((None, sp, LANES), lambda p, bi, i: (bi, 0, p))
    dbspec = pl.BlockSpec((2, CHUNK, DB_W), lambda p, bi, i: (p, 0, 0))
    db_shape = jax.ShapeDtypeStruct((ATTN_HEADS, CHUNK, DB_W), F32)
    return pl.pallas_call(
        body,
        name=name,
        grid=(w // LANES, b, s // QBLK),
        in_specs=[qspec, kspec, kspec, pl.BlockSpec((2, QBLK, WIN), lambda p, bi, i: (p, 0, 0)), qspec],
        out_specs=[qspec, kspec, kspec, dbspec, dbspec],
        out_shape=[jax.ShapeDtypeStruct((b, s, w), F32), jax.ShapeDtypeStruct((b, sp, w), F32),
                   jax.ShapeDtypeStruct((b, sp, w), F32), db_shape, db_shape],
        compiler_params=_params("arbitrary", "arbitrary", "arbitrary"),
    )(q, k, v, table, dmix)


HQ_COL = 3 * ATTN_W // HGRN_DH
HF_COL = HQ_COL + HGRN_HEADS
HI_COL = HF_COL + HGRN_HEADS
HG_COL = HI_COL + HGRN_HEADS
HGRN_ROWS = 8 * CHUNK
HEAD_LANES = [slice(hh * HGRN_DH, (hh + 1) * HGRN_DH) for hh in range(HGRN_HEADS)]


def _tri(lower):
    r = lax.broadcasted_iota(jnp.int32, (CHUNK, CHUNK), 0)
    c = lax.broadcasted_iota(jnp.int32, (CHUNK, CHUNK), 1)
    return (r >= c) if lower else (r <= c)


def _hgrn_chunk(hq, hf, lb, tril):
    sig = _sigmoid(hf)
    f = lb + (1.0 - lb) * sig
    g = jnp.log(f)
    ones_l = jnp.where(tril, 1.0, 0.0).astype(BF16)
    b = _dot_exact_lhs(ones_l, g)
    bl = jnp.sum(g, axis=0, keepdims=True)
    rows = lax.broadcasted_iota(jnp.int32, g.shape, 0)
    bm = jnp.sum(jnp.where(rows <= CHUNK // 2, g, 0.0), axis=0, keepdims=True)
    sq = _sigmoid(hq)
    q = hq * sq
    k = 1.0 - f
    return sig, f, b, bl, bm, sq, q, k


def _hgrn_fwd(name, proj, attn, lb, go, b, s):
    nc = s // CHUNK
    t = b * s
    nblk = s // HGRN_ROWS
    cpb = HGRN_ROWS // CHUNK

    def body(hq_ref, hf_ref, hi_ref, hg_ref, attn_ref, lb_ref, go_ref, mix_ref, oraw_ref, st_ref, s_scr):
        tril = _tri(True)
        gov = go_ref[...]
        mix_ref[:, 0:ATTN_W] = attn_ref[...]

        @pl.when(pl.program_id(1) == 0)
        def _():
            s_scr[...] = jnp.zeros_like(s_scr)

        def step(c, carry):
            sl = pl.ds(pl.multiple_of(c * CHUNK, CHUNK), CHUNK)
            hg = hg_ref[sl, :]
            _, _, bb, bl, bm, _, q, k = _hgrn_chunk(hq_ref[sl, :], hf_ref[sl, :], lb_ref[...], tril)
            vb = hi_ref[sl, :].astype(BF16)
            qe = (q * jnp.exp(bb - bm)).astype(BF16)
            ke = (k * jnp.exp(bm - bb)).astype(BF16)
            qb = (q * jnp.exp(bb)).astype(BF16)
            kb = (k * jnp.exp(bl - bb)).astype(BF16)
            e_last = jnp.exp(bl)
            gate = _silu(hg)
            st = [s_scr[hh] for hh in range(HGRN_HEADS)]
            a = [jnp.where(tril, _dot(qe[:, hs], ke[:, hs], NT), 0.0).astype(BF16) for hs in HEAD_LANES]
            o_state = [_dot(qb[:, hs], st[hh].astype(BF16), NT) for hh, hs in enumerate(HEAD_LANES)]
            st_next = [st[hh] * e_last[:, hs] + _dot(vb[:, hs], kb[:, hs], TN) for hh, hs in enumerate(HEAD_LANES)]
            o = [_dot(a[hh], vb[:, hs]) + o_state[hh] for hh, hs in enumerate(HEAD_LANES)]
            ro = [(oh * lax.rsqrt(jnp.mean(oh * oh, axis=-1, keepdims=True) + RMS_EPS) * gov) * gate[:, hs]
                  for oh, hs in zip(o, HEAD_LANES)]
            for hh in range(HGRN_HEADS):
                st_ref[hh, c] = st[hh]
                s_scr[hh] = st_next[hh]
            mix_ref[sl, ATTN_W:ATTN_W + HGRN_W] = jnp.concatenate(ro, axis=1).astype(BF16)
            oraw_ref[sl, :] = jnp.concatenate(o, axis=1)
            return carry

        lax.fori_loop(0, cpb, step, 0)

    col = lambda base: pl.BlockSpec((HGRN_ROWS, HGRN_W), lambda bi, i: (bi * nblk + i, base // HGRN_HEADS))
    out = pl.BlockSpec((HGRN_ROWS, HGRN_W), lambda bi, i: (bi * nblk + i, 0))
    return pl.pallas_call(
        body,
        name=name,
        grid=(b, nblk),
        in_specs=[col(HQ_COL), col(HF_COL), col(HI_COL), col(HG_COL), out,
                  pl.BlockSpec((1, HGRN_W), lambda bi, i: (0, 0)), pl.BlockSpec((1, HGRN_DH), lambda bi, i: (0, 0))],
        out_specs=[pl.BlockSpec((HGRN_ROWS, ATTN_W + HGRN_W), lambda bi, i: (bi * nblk + i, 0)), out,
                   pl.BlockSpec((None, HGRN_HEADS, cpb, HGRN_DH, HGRN_DH), lambda bi, i: (bi, 0, i, 0, 0))],
        out_shape=[jax.ShapeDtypeStruct((t, ATTN_W + HGRN_W), BF16), jax.ShapeDtypeStruct((t, HGRN_W), F32),
                   jax.ShapeDtypeStruct((b, HGRN_HEADS, nc, HGRN_DH, HGRN_DH), F32)],
        scratch_shapes=[pltpu.VMEM((HGRN_HEADS, HGRN_DH, HGRN_DH), F32)],
        compiler_params=_params("parallel", "arbitrary"),
    )(proj, proj, proj, proj, attn, lb, go)


def _hgrn_bwd(name, proj, dqkv, lb, go, oraw, states, dmix, b, s):
    t = b * s
    nblk = s // HGRN_ROWS
    cpb = HGRN_ROWS // CHUNK

    def body(hq_ref, hf_ref, hi_ref, hg_ref, dq_ref, dk_ref, dv_ref, lb_ref, go_ref, oraw_ref, st_ref, dro_ref,
             dp_ref, dlb_ref, dgo_ref, ds_scr, dlb_scr, dgo_scr):
        tril = _tri(True)
        ones_u = jnp.where(_tri(False), 1.0, 0.0).astype(BF16)
        gov = go_ref[...]
        dp_ref[:, 0:ATTN_W] = dq_ref[...]
        dp_ref[:, ATTN_W:2 * ATTN_W] = dk_ref[...]
        dp_ref[:, 2 * ATTN_W:3 * ATTN_W] = dv_ref[...]

        @pl.when(pl.program_id(1) == 0)
        def _():
            ds_scr[...] = jnp.zeros_like(ds_scr)
            dlb_scr[...] = jnp.zeros_like(dlb_scr)
            dgo_scr[...] = jnp.zeros_like(dgo_scr)

        def step(ci, carry):
            c = cpb - 1 - ci
            sl = pl.ds(pl.multiple_of(c * CHUNK, CHUNK), CHUNK)
            hq = hq_ref[sl, :]
            hg = hg_ref[sl, :]
            sig, f, bb, bl, bm, sq, q, k = _hgrn_chunk(hq, hf_ref[sl, :], lb_ref[...], tril)
            vb = hi_ref[sl, :].astype(BF16)
            ebm = jnp.exp(bb - bm)
            embm = jnp.exp(bm - bb)
            eb = jnp.exp(bb)
            ebl = jnp.exp(bl - bb)
            e_last = jnp.exp(bl)
            qe = (q * ebm).astype(BF16)
            ke = (k * embm).astype(BF16)
            qb = (q * eb).astype(BF16)
            kb = (k * ebl).astype(BF16)
            st = [st_ref[hh, c] for hh in range(HGRN_HEADS)]
            dst = [ds_scr[hh] for hh in range(HGRN_HEADS)]
            o = oraw_ref[sl, :]
            dro = dro_ref[sl, :]
            sg = _sigmoid(hg)
            gov4 = jnp.concatenate([gov] * HGRN_HEADS, axis=1)
            rstd = jnp.concatenate(
                [jnp.broadcast_to(lax.rsqrt(jnp.mean(o[:, hs] * o[:, hs], axis=-1, keepdims=True) + RMS_EPS),
                                  (CHUNK, HGRN_DH)) for hs in HEAD_LANES], axis=1)
            ohat = o * rstd
            dn = dro * (hg * sg)
            dhg = dro * (ohat * gov4) * (sg * (1.0 + hg * (1.0 - sg)))
            dgo_inc = jnp.sum(dn * ohat, axis=0, keepdims=True)
            dohat = dn * gov4
            proj_h = dohat * ohat
            pm = jnp.concatenate(
                [jnp.broadcast_to(jnp.mean(proj_h[:, hs], axis=-1, keepdims=True), (CHUNK, HGRN_DH))
                 for hs in HEAD_LANES], axis=1)
            dob = (rstd * (dohat - ohat * pm)).astype(BF16)
            stb = [x.astype(BF16) for x in st]
            dstb = [x.astype(BF16) for x in dst]
            a = [jnp.where(tril, _dot(qe[:, hs], ke[:, hs], NT), 0.0).astype(BF16) for hs in HEAD_LANES]
            dab = [jnp.where(tril, _dot(dob[:, hs], vb[:, hs], NT), 0.0).astype(BF16) for hs in HEAD_LANES]
            dqb = [_dot(dob[:, hs], stb[hh]) for hh, hs in enumerate(HEAD_LANES)]
            dkb = [_dot(vb[:, hs], dstb[hh]) for hh, hs in enumerate(HEAD_LANES)]
            dv_state = [_dot(kb[:, hs], dstb[hh], NT) for hh, hs in enumerate(HEAD_LANES)]
            dst_next = [dst[hh] * e_last[:, hs] + _dot(dob[:, hs], qb[:, hs], TN) for hh, hs in enumerate(HEAD_LANES)]
            dv = [_dot(a[hh], dob[:, hs], TN) + dv_state[hh] for hh, hs in enumerate(HEAD_LANES)]
            dqe = jnp.concatenate([_dot(dab[hh], ke[:, hs]) for hh, hs in enumerate(HEAD_LANES)], axis=1)
            dke = jnp.concatenate([_dot(dab[hh], qe[:, hs], TN) for hh, hs in enumerate(HEAD_LANES)], axis=1)
            dqb = jnp.concatenate(dqb, axis=1)
            dkb = jnp.concatenate(dkb, axis=1)
            state_term = jnp.concatenate(
                [jnp.sum(dst[hh] * st[hh], axis=0, keepdims=True) for hh in range(HGRN_HEADS)], axis=1)
            dq = dqe * ebm + dqb * eb
            dk = dke * embm + dkb * ebl
            db = (qe.astype(F32) * dqe - ke.astype(F32) * dke) + q * (dqb * eb) - k * (dkb * ebl)
            d_last = jnp.sum(k * ebl * dkb, axis=0, keepdims=True) + state_term * e_last
            dg = _dot_exact_lhs(ones_u, db) + d_last
            df = dg / f - dk
            first = HQ_COL * HGRN_DH
            dp_ref[sl, first:first + HGRN_W] = (dq * (sq * (1.0 + hq * (1.0 - sq)))).astype(BF16)
            dp_ref[sl, first + HGRN_W:first + 2 * HGRN_W] = (df * (1.0 - lb_ref[...]) * sig * (1.0 - sig)).astype(BF16)
            dp_ref[sl, first + 2 * HGRN_W:first + 3 * HGRN_W] = jnp.concatenate(dv, axis=1).astype(BF16)
            dp_ref[sl, first + 3 * HGRN_W:first + 4 * HGRN_W] = dhg.astype(BF16)
            dlb_scr[...] += jnp.sum(df * (1.0 - sig), axis=0, keepdims=True)
            dgo_scr[...] += dgo_inc
            for hh in range(HGRN_HEADS):
                ds_scr[hh] = dst_next[hh]
            return carry

        lax.fori_loop(0, cpb, step, 0)

        @pl.when(pl.program_id(1) == nblk - 1)
        def _():
            dlb_ref[...] = dlb_scr[...]
            dgo_ref[...] = dgo_scr[...]

    rows = lambda bi, i: bi * nblk + (nblk - 1 - i)
    col = lambda base: pl.BlockSpec((HGRN_ROWS, HGRN_W), lambda bi, i: (rows(bi, i), base // HGRN_HEADS))
    out = pl.BlockSpec((HGRN_ROWS, HGRN_W), lambda bi, i: (rows(bi, i), 0))
    part = pl.BlockSpec((None, 1, HGRN_W), lambda bi, i: (bi, 0, 0))
    width = HG_COL * HGRN_DH + HGRN_W
    o_shape = jax.ShapeDtypeStruct((t, width), BF16)
    p_shape = jax.ShapeDtypeStruct((b, 1, HGRN_W), F32)
    return pl.pallas_call(
        body,
        name=name,
        grid=(b, nblk),
        in_specs=[col(HQ_COL), col(HF_COL), col(HI_COL), col(HG_COL), out, out, out,
                  pl.BlockSpec((1, HGRN_W), lambda bi, i: (0, 0)), pl.BlockSpec((1, HGRN_DH), lambda bi, i: (0, 0)), out,
                  pl.BlockSpec((None, HGRN_HEADS, cpb, HGRN_DH, HGRN_DH), lambda bi, i: (bi, 0, nblk - 1 - i, 0, 0)),
                  col(ATTN_W // HGRN_DH)],
        out_specs=[pl.BlockSpec((HGRN_ROWS, width), lambda bi, i: (rows(bi, i), 0))] + [part] * 2,
        out_shape=[o_shape] + [p_shape] * 2,
        scratch_shapes=[pltpu.VMEM((HGRN_HEADS, HGRN_DH, HGRN_DH), F32), pltpu.VMEM((1, HGRN_W), F32),
                        pltpu.VMEM((1, HGRN_W), F32)],
        compiler_params=_params("parallel", "arbitrary"),
    )(proj, proj, proj, proj, *dqkv, lb, go, oraw, states, dmix)


def _small_grads(name, dg1, dgm, dg2, dgq, dgk, dbias_t, dlb, dgo, lbp):
    d = dg1.shape[1]

    def body(dg1_ref, dgm_ref, dg2_ref, dgq_ref, dgk_ref, dbias_ref, dlb_ref, dgo_ref, lbp_ref,
             g1_ref, gm_ref, g2_ref, gq_ref, gk_ref, rb_ref, lbg_ref, go_ref):
        g1_ref[...] = jnp.sum(dg1_ref[...], axis=0, keepdims=True)
        gm_ref[...] = jnp.sum(dgm_ref[...], axis=0, keepdims=True)
        g2_ref[...] = jnp.sum(dg2_ref[...], axis=0, keepdims=True)
        r = lax.broadcasted_iota(jnp.int32, (ATTN_W, ATTN_DH), 0)
        cidx = lax.broadcasted_iota(jnp.int32, (ATTN_W, ATTN_DH), 1)
        fold = jnp.where(jnp.bitwise_and(r, ATTN_DH - 1) == cidx, 1.0, 0.0).astype(BF16)
        gq_ref[...] = jnp.sum(_dot_exact_rhs(dgq_ref[...], fold), axis=0, keepdims=True)
        gk_ref[...] = jnp.sum(_dot_exact_rhs(dgk_ref[...], fold), axis=0, keepdims=True)
        gosum = jnp.sum(dgo_ref[...], axis=0, keepdims=True)
        go_ref[...] = (gosum[:, 0:HGRN_DH] + gosum[:, HGRN_DH:2 * HGRN_DH]
                       + gosum[:, 2 * HGRN_DH:3 * HGRN_DH] + gosum[:, 3 * HGRN_DH:4 * HGRN_DH])
        p0 = lbp_ref[0:1, :]
        p1 = lbp_ref[1:2, :]
        lbv = 1.0 / (1.0 + jnp.exp(p1 - p0))
        dp0 = jnp.sum(dlb_ref[...], axis=0, keepdims=True) * lbv * (1.0 - lbv)
        lbg_ref[0:1, :] = dp0
        lbg_ref[1:2, :] = -dp0
        sidx = lax.broadcasted_iota(jnp.int32, (BAND, N_REL_PAD), 0)
        ridx = lax.broadcasted_iota(jnp.int32, (BAND, N_REL_PAD), 1)

        def step(tq, acc):
            rel = jnp.clip(tq + KPAD - sidx, -REL_CLIP, REL_CLIP) + REL_CLIP
            onehot = jnp.where(rel == ridx, 1.0, 0.0).astype(BF16)
            return acc + _dot_exact_rhs(dbias_ref[tq], onehot)

        rb_ref[...] = lax.fori_loop(0, CHUNK, step, jnp.zeros((ATTN_HEADS, N_REL_PAD), F32))

    ins = [dg1, dgm, dg2, dgq, dgk, dbias_t, dlb, dgo, lbp]
    outs = [jax.ShapeDtypeStruct((1, d), F32)] * 3 + [jax.ShapeDtypeStruct((1, ATTN_DH), F32)] * 2 + [
        jax.ShapeDtypeStruct((ATTN_HEADS, N_REL_PAD), F32), jax.ShapeDtypeStruct((2, HGRN_W), F32),
        jax.ShapeDtypeStruct((1, HGRN_DH), F32)]
    vm = pl.BlockSpec(memory_space=pltpu.VMEM)
    return pl.pallas_call(
        body,
        name=name,
        in_specs=[vm] * len(ins),
        out_specs=[vm] * len(outs),
        out_shape=outs,
        compiler_params=pltpu.CompilerParams(vmem_limit_bytes=VMEM_LIMIT),
    )(*ins)


def _adam_update(w, g, m, v):
    m2 = ADAM_B1 * m + (1.0 - ADAM_B1) * g
    v2 = ADAM_B2 * v + (1.0 - ADAM_B2) * (g * g)
    m_hat = m2 / (1.0 - ADAM_B1 ** ADAM_STEP)
    v_hat = v2 / (1.0 - ADAM_B2 ** ADAM_STEP)
    delta = -ADAM_LR * (m_hat / (jnp.sqrt(v_hat) + ADAM_EPS) + ADAM_WD * w)
    return delta, m2, v2


def _rows_tile(r):
    for cand in (256, 352, 128, 176, 64, 32, 16):
        if r % cand == 0 and r > cand:
            return cand
    return r


def _pair_sum(name, grad, theirs, core):
    n, half, c = theirs.shape
    tr = _rows_tile(half)
    nth = half // tr

    def body(core_ref, a_ref, b_ref, o_ref):
        o_ref[...] = (a_ref[...].astype(F32) + b_ref[...].astype(F32)).astype(o_ref.dtype)

    spec = pl.BlockSpec((None, tr, c), lambda i, j, core_ref: (i, j, 0))
    return pl.pallas_call(
        body, name=name,
        grid_spec=pltpu.PrefetchScalarGridSpec(
            num_scalar_prefetch=1, grid=(n, nth),
            in_specs=[pl.BlockSpec((None, tr, c), lambda i, j, core_ref: (i, core_ref[0] * nth + j, 0)), spec],
            out_specs=spec),
        out_shape=jax.ShapeDtypeStruct((n, half, c), BF16), compiler_params=_params("parallel", "parallel"),
    )(core, grad, theirs)


def _chip_sum(name, own, parts, chip):
    _, half, c = own.shape
    tr = _rows_tile(half)

    def body(chip_ref, own_ref, p_ref, o_ref):
        me = chip_ref[0]
        mine = own_ref[...].astype(F32)
        flip_x, flip_y, flip_xy = (p_ref[i].astype(F32) for i in range(3))
        acc = None
        for k in range(N_CHIPS):
            rel = jnp.bitwise_xor(me, k)
            term = jnp.where(rel == 0, mine, jnp.where(rel == 2, flip_x, jnp.where(rel == 1, flip_y, flip_xy)))
            acc = term if acc is None else acc + term
        o_ref[...] = acc

    return pl.pallas_call(
        body, name=name,
        grid_spec=pltpu.PrefetchScalarGridSpec(
            num_scalar_prefetch=1, grid=(half // tr,),
            in_specs=[pl.BlockSpec((None, tr, c), lambda j, chip_ref: (chip_ref[0], j, 0)),
                      pl.BlockSpec((3, tr, c), lambda j, chip_ref: (0, j, 0))],
            out_specs=pl.BlockSpec((tr, c), lambda j, chip_ref: (j, 0))),
        out_shape=jax.ShapeDtypeStruct((half, c), F32), compiler_params=_params("parallel"),
    )(chip, own, parts)


def _adamw(name, w, g_mine, g_theirs, m, v, core):
    _, r, c = w.shape
    half = r // 2
    tr = _rows_tile(half)
    nth = half // tr

    def body(core_ref, w_ref, gm_ref, gt_ref, m_ref, v_ref, g_ref, d_ref, m2_ref, v2_ref):
        g = jnp.where(pl.program_id(0) == core_ref[0], gm_ref[...], gt_ref[...])
        delta, m2, v2 = _adam_update(w_ref[...], g, m_ref[...], v_ref[...])
        g_ref[...] = g
        d_ref[...] = delta
        m2_ref[...] = m2
        v2_ref[...] = v2

    full = pl.BlockSpec((None, tr, c), lambda h, j, core_ref: (0, h * nth + j, 0))
    part = pl.BlockSpec((tr, c), lambda h, j, core_ref: (j, 0))
    shape = jax.ShapeDtypeStruct((1, r, c), F32)
    return pl.pallas_call(
        body, name=name,
        grid_spec=pltpu.PrefetchScalarGridSpec(
            num_scalar_prefetch=1, grid=(2, nth), in_specs=[full, part, part, full, full], out_specs=[full] * 4),
        out_shape=[shape] * 4, compiler_params=_params("parallel", "parallel"),
    )(core, w, g_mine, g_theirs, m, v)


def _rel_bias_table(name, rel_bias):
    padded = jnp.pad(rel_bias, ((0, 0), (0, N_REL_PAD - N_REL)))

    def body(rb_ref, o_ref):
        ridx = lax.broadcasted_iota(jnp.int32, (N_REL_PAD, BAND), 0)
        sidx = lax.broadcasted_iota(jnp.int32, (N_REL_PAD, BAND), 1)
        rb = rb_ref[...]

        def step(tq, carry):
            rel = jnp.clip(tq + KPAD - sidx, -REL_CLIP, REL_CLIP) + REL_CLIP
            onehot = jnp.where(rel == ridx, 1.0, 0.0).astype(BF16)
            o_ref[tq] = _dot_exact_rhs(rb, onehot)
            return carry

        lax.fori_loop(0, CHUNK, step, 0)

    vm = pl.BlockSpec(memory_space=pltpu.VMEM)
    table = pl.pallas_call(
        body, name=name, in_specs=[vm], out_specs=vm,
        out_shape=jax.ShapeDtypeStruct((CHUNK, ATTN_HEADS, BAND), F32),
    )(padded)
    return table.transpose(1, 0, 2)


def _adamw_small(name, w, parts, m, v):
    def body(w_ref, p_ref, m_ref, v_ref, g_ref, d_ref, m2_ref, v2_ref):
        g = p_ref[0]
        for i in range(1, N_DEV):
            g = g + p_ref[i]
        delta, m2, v2 = _adam_update(w_ref[...], g, m_ref[...], v_ref[...])
        g_ref[...] = g
        d_ref[...] = delta
        m2_ref[...] = m2
        v2_ref[...] = v2

    vm = pl.BlockSpec(memory_space=pltpu.VMEM)
    shape = jax.ShapeDtypeStruct((SMALL_ROWS, SMALL_COLS), F32)
    return pl.pallas_call(
        body, name=name, in_specs=[vm] * 4, out_specs=[vm] * 4, out_shape=[shape] * 4,
    )(w, parts, m, v)


def _position():
    return lax.axis_index("x"), lax.axis_index("y"), lax.axis_index("c")


def _other_chips(x, y):
    return [(1 - x, y), (x, 1 - y), (1 - x, 1 - y)]


ANY = pl.BlockSpec(memory_space=pl.ANY)


HBM = pl.BlockSpec(memory_space=pltpu.HBM)
SEM = pl.BlockSpec(memory_space=pltpu.SEMAPHORE)
SPLIT_COPY = pltpu.SideEffectType.DATAFLOW_SIDE_EFFECTING


def _gather_copy(shards, outs, send_sem, recv_sem, i, j):
    x, y, c = _position()
    chips = _other_chips(x, y)
    half = shards[i].shape[0] // 2
    rows = pl.ds(pl.multiple_of(c * half, 16), half)
    return pltpu.make_async_remote_copy(
        src_ref=shards[i].at[rows, :], dst_ref=outs[i].at[2 * x + y, rows, :],
        send_sem=send_sem.at[3 * i + j], recv_sem=recv_sem.at[3 * i + j],
        device_id=(chips[j][0], chips[j][1], c), device_id_type=MESH)


def _gather_start(name, shards, after):
    n = len(shards)

    def body(*refs):
        srcs, outs = refs[:n], refs[n:2 * n]
        send_sem, recv_sem = refs[2 * n + len(after)], refs[2 * n + len(after) + 1]
        token = refs[-1]
        for i in range(n):
            for j in range(3):
                _gather_copy(srcs, outs, send_sem, recv_sem, i, j).start()
        token[...] = jnp.zeros_like(token)

    full = [(N_CHIPS,) + s.shape for s in shards]
    res = pl.pallas_call(
        body,
        name=name,
        in_specs=[HBM] * (2 * n) + [ANY] * len(after),
        out_specs=[SEM, SEM] + [HBM] * (2 * n) + [pl.BlockSpec(memory_space=pltpu.VMEM)],
        out_shape=[pltpu.SemaphoreType.DMA((3 * n,)), pltpu.SemaphoreType.DMA((3 * n,))]
        + [pltpu.HBM(s.shape, s.dtype) for s in shards]
        + [pltpu.HBM(shp, s.dtype) for shp, s in zip(full, shards)]
        + [jax.ShapeDtypeStruct((8, LANES), F32)],
        input_output_aliases={i: 2 + i for i in range(2 * n)},
        compiler_params=pltpu.CompilerParams(has_side_effects=SPLIT_COPY),
    )(*[pltpu.with_memory_space_constraint(s, pltpu.HBM) for s in shards],
      *[pltpu.with_memory_space_constraint(lax.empty(shp, s.dtype), pltpu.HBM) for shp, s in zip(full, shards)],
      *after)
    return res[0], res[1], list(res[2:2 + n]), list(res[2 + n:2 + 2 * n]), res[-1]


def _gather_wait(name, send_sem, recv_sem, shards, outs, after):
    n = len(shards)

    def body(*refs):
        srcs, out_refs = refs[:n], refs[n:2 * n]
        send_ref, recv_ref = refs[2 * n], refs[2 * n + 1]
        for i in range(n):
            for j in range(3):
                copy = _gather_copy(srcs, out_refs, send_ref, recv_ref, i, j)
                copy.wait_send()
                copy.wait_recv()

    res = pl.pallas_call(
        body,
        name=name,
        in_specs=[HBM] * (2 * n) + [SEM, SEM] + [ANY] * len(after),
        out_specs=[HBM] * (2 * n),
        out_shape=[pltpu.HBM(s.shape, s.dtype) for s in shards] + [pltpu.HBM(o.shape, o.dtype) for o in outs],
        input_output_aliases={i: i for i in range(2 * n)},
        compiler_params=pltpu.CompilerParams(has_side_effects=SPLIT_COPY),
    )(*shards, *outs, send_sem, recv_sem, *after)
    return list(res[:n]), list(res[n:])


def _gather_join(name, shards, outs):
    n = len(shards)

    def body(*refs):
        srcs, ins, outs_ = refs[:n], refs[n:2 * n], refs[2 * n:3 * n]
        own_send, own_recv, half_send, half_recv = refs[3 * n:]
        x, y, c = _position()
        chips = _other_chips(x, y)
        copies = []
        for i in range(n):
            copies.append(pltpu.make_async_remote_copy(
                src_ref=srcs[i], dst_ref=outs_[i].at[2 * x + y], send_sem=own_send.at[i], recv_sem=own_recv.at[i],
                device_id=(x, y, 1 - c), device_id_type=MESH))
            half = srcs[i].shape[0] // 2
            rows = pl.ds(pl.multiple_of(c * half, 16), half)
            for j in range(3):
                slot = 2 * chips[j][0] + chips[j][1]
                copies.append(pltpu.make_async_remote_copy(
                    src_ref=ins[i].at[slot, rows, :], dst_ref=outs_[i].at[slot, rows, :],
                    send_sem=half_send.at[3 * i + j], recv_sem=half_recv.at[3 * i + j],
                    device_id=(x, y, 1 - c), device_id_type=MESH))
        for cp in copies:
            cp.start()
        for cp in copies:
            cp.wait()

    return pl.pallas_call(
        body,
        name=name,
        in_specs=[ANY] * (2 * n),
        out_specs=[ANY] * n,
        out_shape=[jax.ShapeDtypeStruct(o.shape, o.dtype) for o in outs],
        input_output_aliases={n + i: i for i in range(n)},
        scratch_shapes=[pltpu.SemaphoreType.DMA((n,))] * 2 + [pltpu.SemaphoreType.DMA((3 * n,))] * 2,
    )(*shards, *outs)


def _pair_copy(grads, lands, send_sem, recv_sem, i):
    x, y, c = _position()
    half = grads[i].shape[1] // 2
    give = pl.ds(pl.multiple_of((1 - c) * half, 16), half)
    return pltpu.make_async_remote_copy(
        src_ref=grads[i].at[:, give, :], dst_ref=lands[i], send_sem=send_sem.at[i], recv_sem=recv_sem.at[i],
        device_id=(x, y, 1 - c), device_id_type=MESH)


def _pair_start(name, grads):
    n = len(grads)

    def body(*refs):
        srcs, lands = refs[:n], refs[n:2 * n]
        send_sem, recv_sem = refs[2 * n], refs[2 * n + 1]
        token = refs[-1]
        for i in range(n):
            _pair_copy(srcs, lands, send_sem, recv_sem, i).start()
        token[...] = jnp.zeros_like(token)

    halves = [(g.shape[0], g.shape[1] // 2, g.shape[2]) for g in grads]
    res = pl.pallas_call(
        body,
        name=name,
        in_specs=[HBM] * (2 * n),
        out_specs=[SEM, SEM] + [HBM] * (2 * n) + [pl.BlockSpec(memory_space=pltpu.VMEM)],
        out_shape=[pltpu.SemaphoreType.DMA((n,)), pltpu.SemaphoreType.DMA((n,))]
        + [pltpu.HBM(g.shape, g.dtype) for g in grads]
        + [pltpu.HBM(shp, g.dtype) for shp, g in zip(halves, grads)]
        + [jax.ShapeDtypeStruct((8, LANES), F32)],
        input_output_aliases={i: 2 + i for i in range(2 * n)},
        compiler_params=pltpu.CompilerParams(has_side_effects=SPLIT_COPY),
    )(*[pltpu.with_memory_space_constraint(g, pltpu.HBM) for g in grads],
      *[pltpu.with_memory_space_constraint(lax.empty(shp, g.dtype), pltpu.HBM) for shp, g in zip(halves, grads)])
    return res[0], res[1], list(res[2:2 + n]), list(res[2 + n:2 + 2 * n]), res[-1]


def _pair_wait(name, send_sem, recv_sem, grads, lands, after):
    n = len(grads)

    def body(*refs):
        srcs, land_refs = refs[:n], refs[n:2 * n]
        send_ref, recv_ref = refs[2 * n], refs[2 * n + 1]
        for i in range(n):
            copy = _pair_copy(srcs, land_refs, send_ref, recv_ref, i)
            copy.wait_send()
            copy.wait_recv()

    res = pl.pallas_call(
        body,
        name=name,
        in_specs=[HBM] * (2 * n) + [SEM, SEM, ANY],
        out_specs=[HBM] * (2 * n),
        out_shape=[pltpu.HBM(g.shape, g.dtype) for g in grads] + [pltpu.HBM(l.shape, l.dtype) for l in lands],
        input_output_aliases={i: i for i in range(2 * n)},
        compiler_params=pltpu.CompilerParams(has_side_effects=SPLIT_COPY),
    )(*grads, *lands, send_sem, recv_sem, after)
    return list(res[:n]), list(res[n:])


def _scatter_copy(srcs, lands, send_sem, recv_sem, i, j):
    x, y, c = _position()
    chips = _other_chips(x, y)
    return pltpu.make_async_remote_copy(
        src_ref=srcs[i].at[2 * chips[j][0] + chips[j][1]], dst_ref=lands[i].at[j],
        send_sem=send_sem.at[3 * i + j], recv_sem=recv_sem.at[3 * i + j],
        device_id=(chips[j][0], chips[j][1], c), device_id_type=MESH)


def _scatter_start(name, sums):
    n = len(sums)

    def body(*refs):
        srcs, lands = refs[:n], refs[n:2 * n]
        send_sem, recv_sem = refs[2 * n], refs[2 * n + 1]
        token = refs[-1]
        for i in range(n):
            for j in range(3):
                _scatter_copy(srcs, lands, send_sem, recv_sem, i, j).start()
        token[...] = jnp.zeros_like(token)

    land_shapes = [(3,) + s.shape[1:] for s in sums]
    res = pl.pallas_call(
        body,
        name=name,
        in_specs=[HBM] * (2 * n),
        out_specs=[SEM, SEM] + [HBM] * (2 * n) + [pl.BlockSpec(memory_space=pltpu.VMEM)],
        out_shape=[pltpu.SemaphoreType.DMA((3 * n,)), pltpu.SemaphoreType.DMA((3 * n,))]
        + [pltpu.HBM(s.shape, s.dtype) for s in sums]
        + [pltpu.HBM(shp, s.dtype) for shp, s in zip(land_shapes, sums)]
        + [jax.ShapeDtypeStruct((8, LANES), F32)],
        input_output_aliases={i: 2 + i for i in range(2 * n)},
        compiler_params=pltpu.CompilerParams(has_side_effects=SPLIT_COPY),
    )(*[pltpu.with_memory_space_constraint(s, pltpu.HBM) for s in sums],
      *[pltpu.with_memory_space_constraint(lax.empty(shp, s.dtype), pltpu.HBM) for shp, s in zip(land_shapes, sums)])
    return res[0], res[1], list(res[2:2 + n]), list(res[2 + n:2 + 2 * n]), res[-1]


def _scatter_wait(name, send_sem, recv_sem, sums, lands, after):
    n = len(sums)

    def body(*refs):
        srcs, land_refs = refs[:n], refs[n:2 * n]
        send_ref, recv_ref = refs[2 * n], refs[2 * n + 1]
        for i in range(n):
            for j in range(3):
                copy = _scatter_copy(srcs, land_refs, send_ref, recv_ref, i, j)
                copy.wait_send()
                copy.wait_recv()

    res = pl.pallas_call(
        body,
        name=name,
        in_specs=[HBM] * (2 * n) + [SEM, SEM, ANY],
        out_specs=[HBM] * (2 * n),
        out_shape=[pltpu.HBM(s.shape, s.dtype) for s in sums] + [pltpu.HBM(l.shape, l.dtype) for l in lands],
        input_output_aliases={i: i for i in range(2 * n)},
        compiler_params=pltpu.CompilerParams(has_side_effects=SPLIT_COPY),
    )(*sums, *lands, send_sem, recv_sem, after)
    return list(res[:n]), list(res[n:])


def _pair_join(name, halves, small=None):
    n = len(halves)
    if small is None:
        def body_plain(*refs):
            ins, outs = refs[:n], refs[n:2 * n]
            send_sem, recv_sem = refs[2 * n:]
            x, y, c = _position()
            swaps = [pltpu.make_async_remote_copy(
                src_ref=ins[i], dst_ref=outs[i], send_sem=send_sem.at[i], recv_sem=recv_sem.at[i],
                device_id=(x, y, 1 - c), device_id_type=MESH) for i in range(n)]
            for swap in swaps:
                swap.start()
            for swap in swaps:
                swap.wait()

        return pl.pallas_call(
            body_plain,
            name=name,
            in_specs=[ANY] * n,
            out_specs=[ANY] * n,
            out_shape=[jax.ShapeDtypeStruct(h.shape, h.dtype) for h in halves],
            scratch_shapes=[pltpu.SemaphoreType.DMA((n,))] * 2,
        )(*halves)

    def body(*refs):
        ins, small_ref = refs[:n], refs[n]
        outs, all_ref = refs[n + 1:2 * n + 1], refs[2 * n + 1]
        send_sem, recv_sem, sm_send, sm_recv, sm_local = refs[2 * n + 2:]
        x, y, c = _position()
        swaps = []
        for i in range(n):
            swap = pltpu.make_async_remote_copy(
                src_ref=ins[i], dst_ref=outs[i], send_sem=send_sem.at[i], recv_sem=recv_sem.at[i],
                device_id=(x, y, 1 - c), device_id_type=MESH)
            swap.start()
            swaps.append(swap)
        me = 4 * x + 2 * y + c
        sm_own = pltpu.make_async_copy(small_ref, all_ref.at[me], sm_local)
        sm_own.start()
        pushes, arrivals = [], []
        for mask in range(1, N_DEV):
            px, py, pc = x ^ (mask >> 2), y ^ ((mask >> 1) & 1), c ^ (mask & 1)
            pushes.append(pltpu.make_async_remote_copy(
                src_ref=small_ref, dst_ref=all_ref.at[me], send_sem=sm_send.at[mask - 1], recv_sem=sm_recv.at[mask - 1],
                device_id=(px, py, pc), device_id_type=MESH))
            arrivals.append(pltpu.make_async_remote_copy(
                src_ref=small_ref, dst_ref=all_ref.at[4 * px + 2 * py + pc], send_sem=sm_send.at[mask - 1],
                recv_sem=sm_recv.at[mask - 1], device_id=(px, py, pc), device_id_type=MESH))
        for cp in pushes:
            cp.start()
        for swap in swaps:
            swap.wait()
        for cp in arrivals:
            cp.wait_recv()
        for cp in pushes:
            cp.wait_send()
        sm_own.wait()

    res = pl.pallas_call(
        body,
        name=name,
        in_specs=[ANY] * (n + 1),
        out_specs=[ANY] * (n + 1),
        out_shape=[jax.ShapeDtypeStruct(h.shape, h.dtype) for h in halves]
        + [jax.ShapeDtypeStruct((N_DEV,) + small.shape, small.dtype)],
        scratch_shapes=[pltpu.SemaphoreType.DMA((n,))] * 2 + [pltpu.SemaphoreType.DMA((N_DEV - 1,))] * 2
        + [pltpu.SemaphoreType.DMA(())],
    )(*halves, small)
    return res[:n], res[n]


def _lower_bound(lbp):
    return jax.nn.softmax(lbp, axis=0)[0:1]


def _local_step(x, target, g1, gm, g2, gq, gk, go, rel_bias, lbp, first_weights, mid_weights, last_weights, on_grads, grads_sent):
    b, s, d = x.shape
    t = b * s
    x0 = x.reshape(t, d)
    tgt = target.reshape(t, d)
    gq_t = jnp.tile(gq, (1, ATTN_HEADS))
    gk_t = jnp.tile(gk, (1, ATTN_HEADS))
    lb = _lower_bound(lbp)
    table = _band_table(_rel_bias_table("rel_bias_table", rel_bias))

    h1 = _rmsnorm_fwd("norm1", x0, g1)
    wg1, wu1, deps1 = first_weights((h1, table))
    a1, b1, z1 = _ffn_up("ffn1_up", h1, wg1, wu1, deps1)
    wd1, w_in, w_out = mid_weights((z1,))
    ns = w_in.shape[0]
    x1 = _ffn_down("ffn1_down", z1, wd1, x0)
    h2 = _rmsnorm_fwd("norm_mix", x1, gm)
    proj = _in_proj("in_proj", h2, w_in)
    proj3 = proj.reshape(b, s, proj.shape[1])
    qn, kn, vb = _qk_prep("qk_prep", proj3, gq_t, gk_t)
    attn = _attn_fwd("attn_fwd", qn, kn, vb, table).reshape(t, ATTN_W)
    mix, oraw, states = _hgrn_fwd("hgrn_fwd", proj, attn, lb, go, b, s)
    x2 = _out_proj("out_proj", mix, w_out, x1)
    h3 = _rmsnorm_fwd("norm2", x2, g2)
    wg2, wu2, wd2 = last_weights((h3,))
    a2, b2, z2 = _ffn_up("ffn2_up", h3, wg2, wu2)
    dy, dyh, sq = _ffn_down_loss("ffn2_down_loss", z2, wd2, x2, tgt)
    loss = 0.5 * jnp.sum(sq) / d

    da2, db2 = _ffn_bwd_act("ffn2_bwd_act", dyh, wd2, a2, b2)
    dwd2 = _grad_w_shardrows("ffn2_dwd", z2, dyh)
    dwg2 = _grad_w_shardrows("ffn2_dwg", da2, h3)
    dwu2 = _grad_w_shardrows("ffn2_dwu", db2, h3)
    sent2 = on_grads("ffn2", {"ffn2_w_gate": dwg2, "ffn2_w_up": dwu2, "ffn2_w_down": dwd2})
    dx2, dx2b, dg2 = _ffn_bwd_in("ffn2_bwd_in", da2, db2, wg2, wu2, x2, g2, dy, 1.0, sent2)
    sent2 = grads_sent("ffn2", dx2b)

    dwout = _grad_w_out("dw_out", mix, dx2b)
    dmix = _out_proj_bwd("out_proj_bwd", dx2b, w_out, sent2)
    dqn, dkn, dvn, dbe, dbo = _attn_bwd("attn_bwd", qn, kn, vb, table, dmix.reshape(b, s, dmix.shape[1]))
    dbias = dbe[:, :, :BAND] + dbo[:, :, CHUNK:]
    dpq, dpk, dpv, dgq, dgk = _qk_prep_bwd("qk_prep_bwd", proj3, dqn, dkn, dvn, gq_t, gk_t)
    dpq, dpk, dpv = (a.reshape(t, ATTN_W) for a in (dpq, dpk, dpv))
    dproj, dlb, dgo = _hgrn_bwd("hgrn_bwd", proj, (dpq, dpk, dpv), lb, go, oraw, states, dmix, b, s)
    dwin = _grad_w_in("dw_in", h2, dproj, ns)
    dx1, dx1h, dgm = _in_proj_bwd("in_proj_bwd", dproj, w_in, x1, gm, dx2, 0.5)

    dwd1 = _grad_w_shardrows("ffn1_dwd", z1, dx1h)
    sent_mix = on_grads("mix", {"w_in": dwin, "w_out": dwout.reshape(ns, dwout.shape[0] // ns, d),
                                "ffn1_w_down": dwd1})
    da1, db1 = _ffn_bwd_act("ffn1_bwd_act", dx1h, wd1, a1, b1, sent_mix)
    sent_mix = grads_sent("mix", da1)
    dwg1 = _grad_w_shardrows("ffn1_dwg", da1, h1, sent_mix)
    dwu1 = _grad_w_shardrows("ffn1_dwu", db1, h1)
    on_grads("ffn1", {"ffn1_w_gate": dwg1, "ffn1_w_up": dwu1})
    sent1 = grads_sent("ffn1", None)
    dx0, dg1 = _ffn_bwd_in("ffn1_bwd_in", da1, db1, wg1, wu1, x0, g1, dx1, None, sent1)

    nt = dg1.shape[0]
    sg = _small_grads(
        "small_grads", dg1.reshape(nt, d), dgm.reshape(nt, d), dg2.reshape(nt, d),
        dgq.reshape(-1, ATTN_W), dgk.reshape(-1, ATTN_W), dbias.transpose(1, 0, 2),
        dlb.reshape(b, HGRN_W), dgo.reshape(b, HGRN_W), lbp)
    g1g, gmg, g2g, gqg, gkg, rbg, lbg, gog = sg
    small = _pack_small(g1g, gmg, g2g, lbg, rbg[:, :N_REL], gqg, gkg, gog)
    return loss, dx0.reshape(b, s, d), small


def _pack_small(g1, gm, g2, lbp, rel_bias, gq, gk, go):
    flat = [g1.reshape(-1), gm.reshape(-1), g2.reshape(-1), lbp.reshape(-1), rel_bias.reshape(-1)]
    n_bias = 3 * SMALL_COLS - rel_bias.size
    heads = [gq.reshape(-1), gk.reshape(-1), go.reshape(-1)]
    n_tail = SMALL_COLS - sum(h.size for h in heads)
    return jnp.concatenate(flat + [jnp.zeros((n_bias,), F32)] + heads + [jnp.zeros((n_tail,), F32)]).reshape(
        SMALL_ROWS, SMALL_COLS)


def _unpack_small(p, d):
    flat = p.reshape(-1)
    o = 3 * d
    g1, gm, g2 = p[0:1], p[1:2], p[2:3]
    lbp = flat[o:o + 2 * HGRN_W].reshape(2, HGRN_W)
    o = 4 * SMALL_COLS
    rel = flat[o:o + ATTN_HEADS * N_REL].reshape(1, ATTN_HEADS, N_REL)
    o = 7 * SMALL_COLS
    gq = flat[o:o + ATTN_DH].reshape(1, ATTN_DH)
    gk = flat[o + ATTN_DH:o + 2 * ATTN_DH].reshape(1, ATTN_DH)
    go = flat[o + 2 * ATTN_DH:o + 2 * ATTN_DH + HGRN_DH].reshape(1, HGRN_DH)
    return g1, gm, g2, gq, gk, rel, lbp, go


def kernel(x, ffn1_norm_g, ffn1_w_gate, ffn1_w_up, ffn1_w_down, mix_norm_g, w_in, attn_q_norm_g, attn_k_norm_g, attn_rel_bias, hgrn_lower_bounds, hgrn_out_norm_g, w_out, ffn2_norm_g, ffn2_w_gate, ffn2_w_up, ffn2_w_down, loss_target, m_ffn1_norm_g, m_ffn1_w_gate, m_ffn1_w_up, m_ffn1_w_down, m_mix_norm_g, m_w_in, m_attn_q_norm_g, m_attn_k_norm_g, m_attn_rel_bias, m_hgrn_lower_bounds, m_hgrn_out_norm_g, m_w_out, m_ffn2_norm_g, m_ffn2_w_gate, m_ffn2_w_up, m_ffn2_w_down, v_ffn1_norm_g, v_ffn1_w_gate, v_ffn1_w_up, v_ffn1_w_down, v_mix_norm_g, v_w_in, v_attn_q_norm_g, v_attn_k_norm_g, v_attn_rel_bias, v_hgrn_lower_bounds, v_hgrn_out_norm_g, v_w_out, v_ffn2_norm_g, v_ffn2_w_gate, v_ffn2_w_up, v_ffn2_w_down):
    d = x.shape[-1]
    big_w = [ffn1_w_gate, ffn1_w_up, ffn1_w_down, w_in, w_out, ffn2_w_gate, ffn2_w_up, ffn2_w_down]
    big_m = [m_ffn1_w_gate, m_ffn1_w_up, m_ffn1_w_down, m_w_in, m_w_out, m_ffn2_w_gate, m_ffn2_w_up, m_ffn2_w_down]
    big_v = [v_ffn1_w_gate, v_ffn1_w_up, v_ffn1_w_down, v_w_in, v_w_out, v_ffn2_w_gate, v_ffn2_w_up, v_ffn2_w_down]
    big_names = ["ffn1_w_gate", "ffn1_w_up", "ffn1_w_down", "w_in", "w_out", "ffn2_w_gate", "ffn2_w_up", "ffn2_w_down"]
    flipped = {nm for nm in big_names if nm.endswith("gate") or nm.endswith("up")}
    flip = lambda nm, a: jnp.swapaxes(a, 1, 2) if nm in flipped else a
    big_w, big_m, big_v = ([flip(nm, a) for nm, a in zip(big_names, arrs)] for arrs in (big_w, big_m, big_v))

    shards = [w[0].astype(BF16) for w in big_w]
    start_a = _gather_start("gather_start_up1", shards[:2], ())
    start_b = _gather_start("gather_start_mid", shards[2:5], (start_a[4],))
    start_c = _gather_start("gather_start_ffn2", shards[5:], (start_b[4],))

    def gathered(tag, started, after):
        send_sem, recv_sem, srcs, outs, _ = started
        srcs, outs = _gather_wait("gather_wait_" + tag, send_sem, recv_sem, srcs, outs, after)
        return _gather_join("gather_join_" + tag, srcs, outs)

    def first_weights(after):
        return (*gathered("up1", start_a, after), (start_c[4],))

    def mid_weights(after):
        wd1, win_f, wout_f = gathered("mid", start_b, after)
        return wd1, win_f, wout_f.reshape(wout_f.shape[0] * wout_f.shape[1], d)

    def last_weights(after):
        return gathered("ffn2", start_c, after)

    core = lax.axis_index("c").astype(jnp.int32).reshape(1)
    chip = (2 * lax.axis_index("x") + lax.axis_index("y")).astype(jnp.int32).reshape(1)
    started = {}

    def on_grads(tag, grads):
        names = list(grads)
        started[tag] = (names, _pair_start("pair_start_" + tag, [grads[nm] for nm in names]))
        return (started[tag][1][4],)

    def grads_sent(tag, after):
        names, (send_sem, recv_sem, grads, lands, token) = started[tag]
        grads, theirs = _pair_wait("pair_wait_" + tag, send_sem, recv_sem, grads, lands, token if after is None else after)
        sums = [_pair_sum("pair_sum_" + nm, g, th, core) for nm, g, th in zip(names, grads, theirs)]
        started[tag] = (names, _scatter_start("scatter_start_" + tag, sums))
        return (started[tag][1][4],)

    loss, grad_x, small_g = _local_step(
        x, loss_target, ffn1_norm_g, mix_norm_g, ffn2_norm_g, attn_q_norm_g, attn_k_norm_g, hgrn_out_norm_g,
        attn_rel_bias[0], hgrn_lower_bounds, first_weights, mid_weights, last_weights, on_grads, grads_sent)
    loss = lax.psum(loss, ("x", "y", "c"))

    def finish(tag, after):
        names, (send_sem, recv_sem, sums, lands, _) = started[tag]
        sums, lands = _scatter_wait("scatter_wait_" + tag, send_sem, recv_sem, sums, lands, after)
        return names, [_chip_sum("chip_sum_" + nm, sm, ld, chip) for nm, sm, ld in zip(names, sums, lands)]

    by_name = {nm: (w, m, v) for nm, w, m, v in zip(big_names, big_w, big_m, big_v)}
    updated = {}

    def update(names, halves, other_halves):
        for nm, mine, theirs in zip(names, halves, other_halves):
            w, m, v = by_name[nm]
            updated[nm] = _adamw("adamw_" + nm, w, mine, theirs, m, v, core)

    last_token = started["ffn1"][1][4]
    names_a, halves_a = finish("ffn2", last_token)
    names_m, halves_m = finish("mix", last_token)
    names_a, halves_a = names_a + names_m, halves_a + halves_m
    update(names_a, halves_a, _pair_join("pair_join_early", halves_a))
    names_b, halves_b = finish("ffn1", updated[names_a[-1]][1])
    others_b, small_all = _pair_join("pair_join_last", halves_b, small_g)
    update(names_b, halves_b, others_b)
    big_out = [updated[nm] for nm in big_names]

    pack = lambda g1, gm, g2, gq, gk, rel, lbp, go: _pack_small(g1, gm, g2, lbp, rel[0], gq, gk, go)
    small_w = pack(ffn1_norm_g, mix_norm_g, ffn2_norm_g, attn_q_norm_g, attn_k_norm_g, attn_rel_bias, hgrn_lower_bounds, hgrn_out_norm_g)
    small_m = pack(m_ffn1_norm_g, m_mix_norm_g, m_ffn2_norm_g, m_attn_q_norm_g, m_attn_k_norm_g, m_attn_rel_bias, m_hgrn_lower_bounds, m_hgrn_out_norm_g)
    small_v = pack(v_ffn1_norm_g, v_mix_norm_g, v_ffn2_norm_g, v_attn_q_norm_g, v_attn_k_norm_g, v_attn_rel_bias, v_hgrn_lower_bounds, v_hgrn_out_norm_g)
    small_out = [_unpack_small(p, d) for p in _adamw_small("adamw_small", small_w, small_all, small_m, small_v)]

    def assemble(kind):
        bg = [flip(nm, o[kind]) for nm, o in zip(big_names, big_out)]
        g1, gm, g2, gq, gk, rel, lbp, go = small_out[kind]
        return [g1, bg[0], bg[1], bg[2], gm, bg[3], gq, gk, rel, lbp, go, bg[4], g2, bg[5], bg[6], bg[7]]

    return (loss, grad_x, *assemble(0), *assemble(1), *assemble(2), *assemble(3))
```

```python
import functools

import jax
import jax.numpy as jnp
from jax import lax
from jax.experimental import pallas as pl
from jax.experimental.pallas import tpu as pltpu

F32 = jnp.float32
BF16 = jnp.bfloat16
MESH = pl.DeviceIdType.MESH

N_CHIPS = 4
N_DEV = 8
CHUNK = 64
ATTN_HEADS = 8
ATTN_DH = 64
ATTN_W = ATTN_HEADS * ATTN_DH
HGRN_HEADS = 4
HGRN_DH = 128
HGRN_W = HGRN_HEADS * HGRN_DH
LEFT_CHUNKS = 8
BAND = (LEFT_CHUNKS + 1) * CHUNK
KPAD = LEFT_CHUNKS * CHUNK
REL_CLIP = 128
N_REL = 2 * REL_CLIP + 1
N_REL_PAD = 384
RMS_EPS = 1e-6
LANES = 128
SMALL_ROWS = 8
SMALL_COLS = 1024

ADAM_LR = 0.001
ADAM_B1 = 0.9
ADAM_B2 = 0.999
ADAM_EPS = 1e-08
ADAM_WD = 0.01
ADAM_STEP = 10

NN = (((1,), (0,)), ((), ()))
NT = (((1,), (1,)), ((), ()))
TN = (((0,), (0,)), ((), ()))

VMEM_LIMIT = 48 * 1024 * 1024


def _sigmoid(x):
    return 1.0 / (1.0 + jnp.exp(-x))


def _silu(x):
    return x * _sigmoid(x)


def _dot(a, b, dims=NN):
    return lax.dot_general(a, b, dims, preferred_element_type=F32)


def _split3(x):
    hi = x.astype(BF16)
    r1 = x - hi.astype(F32)
    mid = r1.astype(BF16)
    lo = (r1 - mid.astype(F32)).astype(BF16)
    return hi, mid, lo


def _dot_exact_rhs(x, mat, dims=NN, pieces=3):
    hi, mid, lo = _split3(x)
    out = _dot(hi, mat, dims) + _dot(mid, mat, dims)
    return out + _dot(lo, mat, dims) if pieces == 3 else out


def _dot_exact_lhs(mat, x, dims=NN):
    hi, mid, lo = _split3(x)
    return _dot(mat, hi, dims) + _dot(mat, mid, dims) + _dot(mat, lo, dims)


def _params(*sem):
    return pltpu.CompilerParams(dimension_semantics=sem, vmem_limit_bytes=VMEM_LIMIT)


def _row_halves(tm):
    return [slice(0, tm // 2), slice(tm // 2, tm)]


def _epilogue_by_halves(tm, accs_by_half, ex_refs, out_refs, epilogue):
    sums = {}
    for rows, accs in zip(_row_halves(tm), accs_by_half):
        res = epilogue(accs, [e[rows, :] if e.shape[0] == tm else e[...] for e in ex_refs])
        for i, (o, r) in enumerate(zip(out_refs, res)):
            if o.shape[0] == tm:
                o[rows, :] = r.astype(o.dtype)
            else:
                sums[i] = r if i not in sums else sums[i] + r
    for i, r in sums.items():
        out_refs[i][...] = r.astype(out_refs[i].dtype)


def _mm(name, ins, terms, n_acc, grid, acc_shape, outs, epilogue, extras=(), deps=(), split_rows=0):
    nk = grid[2]
    ni, ne, nd, no = len(ins), len(extras), len(deps), len(outs)

    def body(*refs):
        in_refs = refs[:ni]
        ex_refs = refs[ni:ni + ne]
        out_refs = refs[ni + ne + nd:ni + ne + nd + no]
        acc_refs = refs[ni + ne + nd + no:]
        if split_rows:
            by_half = []
            for rows in _row_halves(split_rows):
                half = [None] * n_acc
                for ai, li, ri, dims in terms:
                    d = _dot(in_refs[li][rows, :], in_refs[ri][...], dims)
                    half[ai] = d if half[ai] is None else half[ai] + d
                by_half.append(half)
            _epilogue_by_halves(split_rows, by_half, ex_refs, out_refs, epilogue)
            return
        parts = [None] * n_acc
        for ai, li, ri, dims in terms:
            d = _dot(in_refs[li][...], in_refs[ri][...], dims)
            parts[ai] = d if parts[ai] is None else parts[ai] + d

        def finish(accs):
            res = epilogue(accs, [e[...] for e in ex_refs])
            for o, r in zip(out_refs, res):
                o[...] = r.astype(o.dtype)

        if nk == 1:
            finish(parts)
        else:
            k = pl.program_id(2)

            @pl.when(k == 0)
            def _():
                for a, p in zip(acc_refs, parts):
                    a[...] = p

            @pl.when(k > 0)
            def _():
                for a, p in zip(acc_refs, parts):
                    a[...] += p

            @pl.when(k == nk - 1)
            def _():
                finish([a[...] for a in acc_refs])

    scratch = [] if nk == 1 else [pltpu.VMEM(acc_shape, F32) for _ in range(n_acc)]
    res = pl.pallas_call(
        body,
        name=name,
        grid=grid,
        in_specs=[s for _, s in ins] + [s for _, s in extras] + [pl.BlockSpec(memory_space=pl.ANY)] * nd,
        out_specs=[s for _, s in outs],
        out_shape=[o for o, _ in outs],
        scratch_shapes=scratch,
        compiler_params=_params("parallel", "parallel", "arbitrary"),
    )(*[a for a, _ in ins], *[a for a, _ in extras], *deps)
    return res


def _mm_rows(name, lhs, weights, dims, t, outs, epilogue, extras=(), deps=()):
    tm = _row_tile(t)
    nl, ne, nd, no = len(lhs), len(extras), len(deps), len(outs)
    ns = weights[0].shape[0]

    def body(*refs):
        lhs_refs = refs[:nl]
        w_hbm = refs[nl:2 * nl]
        ex_refs = refs[2 * nl:2 * nl + ne]
        out_refs = refs[2 * nl + ne + nd:2 * nl + ne + nd + no]
        w_vmem = refs[2 * nl + ne + nd + no:3 * nl + ne + nd + no]
        sem = refs[-1]

        @pl.when(pl.program_id(0) == 0)
        def _():
            copies = [pltpu.make_async_copy(w_hbm[p], w_vmem[p], sem.at[p]) for p in range(nl)]
            for cp in copies:
                cp.start()
            for cp in copies:
                cp.wait()

        acc = None
        for p in range(nl):
            pick = lhs[p][2]
            for j in range(ns):
                part = _dot(pick(lhs_refs[p], j, slice(None)), w_vmem[p][j], dims)
                acc = part if acc is None else acc + part
        res = epilogue([acc], [e[...] for e in ex_refs])
        for o, r in zip(out_refs, res):
            o[...] = r.astype(o.dtype)

    return pl.pallas_call(
        body,
        name=name,
        grid=(t // tm,),
        in_specs=[s for _, s, _ in lhs] + [pl.BlockSpec(memory_space=pl.ANY)] * nl + [s for _, s in extras]
        + [pl.BlockSpec(memory_space=pl.ANY)] * nd,
        out_specs=[s for _, s in outs],
        out_shape=[o for o, _ in outs],
        scratch_shapes=[pltpu.VMEM(w.shape, w.dtype) for w in weights] + [pltpu.SemaphoreType.DMA((nl,))],
        compiler_params=_params("arbitrary"),
    )(*[a for a, _, _ in lhs], *weights, *[a for a, _ in extras], *deps)


def _row_tile(t):
    return 512 if t % 512 == 0 else t


def _k_tile(t):
    return t if t <= 4096 else 1024


def _rmsnorm(xv, g):
    ms = jnp.mean(xv * xv, axis=-1, keepdims=True)
    return xv * lax.rsqrt(ms + RMS_EPS) * g


def _rmsnorm_fwd(name, x, g):
    t, d = x.shape
    tm = _row_tile(t)

    def body(x_ref, g_ref, h_ref):
        h_ref[...] = _rmsnorm(x_ref[...], g_ref[...]).astype(BF16)

    return pl.pallas_call(
        body,
        name=name,
        grid=(t // tm,),
        in_specs=[pl.BlockSpec((tm, d), lambda i: (i, 0)), pl.BlockSpec((1, d), lambda i: (0, 0))],
        out_specs=pl.BlockSpec((tm, d), lambda i: (i, 0)),
        out_shape=jax.ShapeDtypeStruct((t, d), BF16),
        compiler_params=_params("parallel"),
    )(x, g)


def _norm_bwd_epilogue(copy_scale):
    def epilogue(accs, ex):
        dh = accs[0]
        xv, g, dres = ex
        ms = jnp.mean(xv * xv, axis=-1, keepdims=True)
        rstd = lax.rsqrt(ms + RMS_EPS)
        xhat = xv * rstd
        dxhat = dh * g
        dx = rstd * (dxhat - xhat * jnp.mean(dxhat * xhat, axis=-1, keepdims=True))
        out = dres + dx
        dg = jnp.sum(dh * xhat, axis=0, keepdims=True)
        if copy_scale is None:
            return out, dg
        return out, out * copy_scale, dg

    return epilogue


def _ffn_up(name, h, wg, wu, deps=()):
    t, d = h.shape
    ns, f, _ = wg.shape
    tm = _row_tile(t)

    def epilogue(accs, ex):
        a, b = accs
        sg = _sigmoid(a)
        act = a * sg
        return act, b * (sg * (1.0 + a * (1.0 - sg))), act * b

    w_spec = pl.BlockSpec((None, f, d), lambda j, i, k: (j, 0, 0))
    o_spec = pl.BlockSpec((None, tm, f), lambda j, i, k: (j, i, 0))
    o_shape = jax.ShapeDtypeStruct((ns, t, f), BF16)
    return _mm(
        name,
        ins=[(h, pl.BlockSpec((tm, d), lambda j, i, k: (i, 0))), (wg, w_spec), (wu, w_spec)],
        terms=[(0, 0, 1, NT), (1, 0, 2, NT)],
        n_acc=2,
        grid=(ns, t // tm, 1),
        acc_shape=(tm, f),
        outs=[(o_shape, o_spec)] * 3,
        epilogue=epilogue,
        deps=deps,
        split_rows=tm,
    )


def _shard_rows(arr, tm):
    ns, _, f = arr.shape
    return arr, pl.BlockSpec((ns, tm, f), lambda i: (0, i, 0)), lambda ref, j, rows: ref[j, rows, :]


def _ffn_down(name, z, wd, x, g_next):
    _, t, _ = z.shape
    d = wd.shape[2]
    tm = _row_tile(t)
    row = pl.BlockSpec((tm, d), lambda i: (i, 0))

    def epilogue(accs, ex):
        y = ex[0] + 0.5 * accs[0]
        return y, _rmsnorm(y, ex[1])

    return _mm_rows(
        name, [_shard_rows(z, tm)], [wd], NN, t,
        outs=[(jax.ShapeDtypeStruct((t, d), F32), row), (jax.ShapeDtypeStruct((t, d), BF16), row)],
        epilogue=epilogue,
        extras=[(x, row), (g_next, pl.BlockSpec((1, d), lambda i: (0, 0)))],
    )


def _ffn_down_loss(name, z, wd, x, target):
    _, t, _ = z.shape
    d = wd.shape[2]
    tm = _row_tile(t)
    nt = t // tm
    row = pl.BlockSpec((tm, d), lambda i: (i, 0))

    def epilogue(accs, ex):
        e = ex[0] + 0.5 * accs[0] - ex[1]
        dy = e * (1.0 / d)
        return dy, 0.5 * dy, jnp.sum(e * e, axis=0, keepdims=True)

    return _mm_rows(
        name, [_shard_rows(z, tm)], [wd], NN, t,
        outs=[(jax.ShapeDtypeStruct((t, d), F32), row), (jax.ShapeDtypeStruct((t, d), BF16), row),
              (jax.ShapeDtypeStruct((nt, 1, d), F32), pl.BlockSpec((None, 1, d), lambda i: (i, 0, 0)))],
        epilogue=epilogue,
        extras=[(x, row), (target, row)],
    )


def _ffn_bwd_act(name, dout, wd, act_a, dact_b, deps=()):
    t, d = dout.shape
    ns, f, _ = wd.shape
    tm = _row_tile(t)

    def epilogue(accs, ex):
        dz = accs[0]
        return dz * ex[1].astype(F32), dz * ex[0].astype(F32)

    act = pl.BlockSpec((None, tm, f), lambda j, i, k: (j, i, 0))
    o_shape = jax.ShapeDtypeStruct((ns, t, f), BF16)
    return _mm(
        name,
        ins=[(dout, pl.BlockSpec((tm, d), lambda j, i, k: (i, 0))),
             (wd, pl.BlockSpec((None, f, d), lambda j, i, k: (j, 0, 0)))],
        terms=[(0, 0, 1, NT)],
        n_acc=1,
        grid=(ns, t // tm, 1),
        acc_shape=(tm, f),
        outs=[(o_shape, act)] * 2,
        epilogue=epilogue,
        extras=[(act_a, act), (dact_b, act)],
        deps=deps,
        split_rows=tm,
    )


def _grad_w_shardrows(name, z, dout, deps=()):
    ns, t, f = z.shape
    d = dout.shape[1]
    tk = _k_tile(t)
    return _mm(
        name,
        ins=[(z, pl.BlockSpec((None, tk, f), lambda j, n, k: (j, k, 0))),
             (dout, pl.BlockSpec((tk, d), lambda j, n, k: (k, 0)))],
        terms=[(0, 0, 1, TN)],
        n_acc=1,
        grid=(ns, 1, t // tk),
        acc_shape=(f, d),
        outs=[(jax.ShapeDtypeStruct((ns, f, d), BF16), pl.BlockSpec((None, f, d), lambda j, n, k: (j, 0, 0)))],
        epilogue=lambda accs, ex: (accs[0],),
        deps=deps,
    )[0]


def _norm_bwd_outs(t, d, tm, copy_scale):
    row = pl.BlockSpec((tm, d), lambda i: (i, 0))
    outs = [(jax.ShapeDtypeStruct((t, d), F32), row)]
    if copy_scale is not None:
        outs.append((jax.ShapeDtypeStruct((t, d), BF16), row))
    outs.append((jax.ShapeDtypeStruct((t // tm, 1, d), F32), pl.BlockSpec((None, 1, d), lambda i: (i, 0, 0))))
    return row, outs


def _ffn_bwd_in(name, da, db, wg, wu, x, g, dres, copy_scale, deps=()):
    _, t, _ = da.shape
    d = wg.shape[2]
    tm = _row_tile(t)
    row, outs = _norm_bwd_outs(t, d, tm, copy_scale)
    return _mm_rows(
        name, [_shard_rows(da, tm), _shard_rows(db, tm)], [wg, wu], NN, t,
        outs=outs,
        epilogue=_norm_bwd_epilogue(copy_scale),
        extras=[(x, row), (g, pl.BlockSpec((1, d), lambda i: (0, 0))), (dres, row)],
        deps=deps,
    )


def _in_proj(name, h, w_in):
    t, d = h.shape
    ns, _, pj = w_in.shape
    tm = _row_tile(t)
    return _mm(
        name,
        ins=[(h, pl.BlockSpec((tm, d), lambda j, i, k: (i, 0))),
             (w_in, pl.BlockSpec((None, d, pj), lambda j, i, k: (j, 0, 0)))],
        terms=[(0, 0, 1, NN)],
        n_acc=1,
        grid=(ns, t // tm, 1),
        acc_shape=(tm, pj),
        outs=[(jax.ShapeDtypeStruct((t, ns * pj), F32), pl.BlockSpec((tm, pj), lambda j, i, k: (i, j)))],
        epilogue=lambda accs, ex: (accs[0],),
        split_rows=tm,
    )[0]


def _in_proj_bwd(name, dp, w_in, x, g, dres, copy_scale, deps=()):
    t = dp.shape[0]
    ns, d, pj = w_in.shape
    tm = _row_tile(t)
    row, outs = _norm_bwd_outs(t, d, tm, copy_scale)
    cols = (dp, pl.BlockSpec((tm, ns * pj), lambda i: (i, 0)), lambda ref, j, rows: ref[rows, j * pj:(j + 1) * pj])
    return _mm_rows(
        name, [cols], [w_in], NT, t,
        outs=outs,
        epilogue=_norm_bwd_epilogue(copy_scale),
        extras=[(x, row), (g, pl.BlockSpec((1, d), lambda i: (0, 0))), (dres, row)],
        deps=deps,
    )


def _grad_w_in(name, h, dp, ns):
    t, d = h.shape
    pj = dp.shape[1] // ns
    tk = _k_tile(t)
    return _mm(
        name,
        ins=[(h, pl.BlockSpec((tk, d), lambda j, n, k: (k, 0))),
             (dp, pl.BlockSpec((tk, pj), lambda j, n, k: (k, j)))],
        terms=[(0, 0, 1, TN)],
        n_acc=1,
        grid=(ns, 1, t // tk),
        acc_shape=(d, pj),
        outs=[(jax.ShapeDtypeStruct((ns, d, pj), BF16), pl.BlockSpec((None, d, pj), lambda j, n, k: (j, 0, 0)))],
        epilogue=lambda accs, ex: (accs[0],),
    )[0]


def _out_proj(name, mix, w_out, x, g_next):
    t, dm = mix.shape
    d = w_out.shape[1]
    tm = _row_tile(t)
    row = pl.BlockSpec((tm, d), lambda i, n, k: (i, 0))
    return _mm(
        name,
        ins=[(mix, pl.BlockSpec((tm, dm), lambda i, n, k: (i, 0))),
             (w_out, pl.BlockSpec((dm, d), lambda i, n, k: (0, 0)))],
        terms=[(0, 0, 1, NN)],
        n_acc=1,
        grid=(t // tm, 1, 1),
        acc_shape=(tm, d),
        outs=[(jax.ShapeDtypeStruct((t, d), F32), row), (jax.ShapeDtypeStruct((t, d), BF16), row)],
        epilogue=lambda accs, ex: (ex[0] + accs[0], _rmsnorm(ex[0] + accs[0], ex[1])),
        extras=[(x, row), (g_next, pl.BlockSpec((1, d), lambda i, n, k: (0, 0)))],
    )


def _out_proj_bwd(name, dx, w_out, deps=()):
    t, d = dx.shape
    dm = w_out.shape[0]
    tm = _row_tile(t)
    return _mm(
        name,
        ins=[(dx, pl.BlockSpec((tm, d), lambda i, n, k: (i, 0))),
             (w_out, pl.BlockSpec((dm, d), lambda i, n, k: (0, 0)))],
        terms=[(0, 0, 1, NT)],
        n_acc=1,
        grid=(t // tm, 1, 1),
        acc_shape=(tm, dm),
        outs=[(jax.ShapeDtypeStruct((t, dm), F32), pl.BlockSpec((tm, dm), lambda i, n, k: (i, 0)))],
        epilogue=lambda accs, ex: (accs[0],),
        deps=deps,
    )[0]


def _grad_w_out(name, mix, dx):
    t, dm = mix.shape
    d = dx.shape[1]
    tk = _k_tile(t)
    return _mm(
        name,
        ins=[(mix, pl.BlockSpec((tk, dm), lambda a, n, k: (k, 0))),
             (dx, pl.BlockSpec((tk, d), lambda a, n, k: (k, 0)))],
        terms=[(0, 0, 1, TN)],
        n_acc=1,
        grid=(1, 1, t // tk),
        acc_shape=(dm, d),
        outs=[(jax.ShapeDtypeStruct((dm, d), BF16), pl.BlockSpec((dm, d), lambda a, n, k: (0, 0)))],
        epilogue=lambda accs, ex: (accs[0],),
    )[0]


def _head_group_matrix():
    r = lax.broadcasted_iota(jnp.int32, (ATTN_W, ATTN_W), 0)
    c = lax.broadcasted_iota(jnp.int32, (ATTN_W, ATTN_W), 1)
    same = jnp.right_shift(r, 6) == jnp.right_shift(c, 6)
    return jnp.where(same, 1.0, 0.0).astype(BF16)


def _qk_prep(name, proj, gq, gk):
    b, s, _ = proj.shape
    tm = KPAD
    nb = s // tm

    def body(q_ref, k_ref, v_ref, gq_ref, gk_ref, qn_ref, kn_ref, vb_ref):
        j = pl.program_id(1)
        bd = _head_group_matrix()

        def norm(xv, g):
            ms = _dot_exact_rhs(xv * xv, bd, pieces=2) * (1.0 / ATTN_DH)
            return xv * lax.rsqrt(ms + RMS_EPS) * g

        @pl.when(j == 0)
        def _():
            kn_ref[...] = jnp.zeros_like(kn_ref)
            vb_ref[...] = jnp.zeros_like(vb_ref)

        @pl.when(j > 0)
        def _():
            qn_ref[...] = norm(q_ref[...], gq_ref[...]).astype(BF16)
            kn_ref[...] = norm(k_ref[...], gk_ref[...]).astype(BF16)
            vb_ref[...] = v_ref[...].astype(BF16)

    src_blk = lambda col: pl.BlockSpec((None, tm, ATTN_W), lambda bi, j: (bi, jnp.maximum(j - 1, 0), col))
    gspec = pl.BlockSpec((1, ATTN_W), lambda bi, j: (0, 0))
    padded = pl.BlockSpec((None, tm, ATTN_W), lambda bi, j: (bi, j, 0))
    return pl.pallas_call(
        body,
        name=name,
        grid=(b, nb + 1),
        in_specs=[src_blk(0), src_blk(1), src_blk(2), gspec, gspec],
        out_specs=[src_blk(0), padded, padded],
        out_shape=[jax.ShapeDtypeStruct((b, s, ATTN_W), BF16), jax.ShapeDtypeStruct((b, KPAD + s, ATTN_W), BF16),
                   jax.ShapeDtypeStruct((b, KPAD + s, ATTN_W), BF16)],
        compiler_params=_params("parallel", "arbitrary"),
    )(proj, proj, proj, gq, gk)


def _qk_prep_bwd(name, proj, dqn, dkn, dv, gq, gk):
    b, s, _ = proj.shape
    tm = KPAD
    nb = s // tm

    def body(q_ref, k_ref, dqn_ref, dkn_ref, dv_ref, gq_ref, gk_ref, dq_ref, dk_ref, dvb_ref, dgq_ref, dgk_ref):
        bd = _head_group_matrix()

        def bwd(xv, dy, g):
            ms = _dot_exact_rhs(xv * xv, bd, pieces=2) * (1.0 / ATTN_DH)
            rstd = lax.rsqrt(ms + RMS_EPS)
            xhat = xv * rstd
            dxhat = dy * g
            gm = _dot_exact_rhs(dxhat * xhat, bd, pieces=2) * (1.0 / ATTN_DH)
            return rstd * (dxhat - xhat * gm), jnp.sum(dy * xhat, axis=0, keepdims=True)

        dq, dgq = bwd(q_ref[...], dqn_ref[...], gq_ref[...])
        dk, dgk = bwd(k_ref[...], dkn_ref[...], gk_ref[...])
        dq_ref[...] = dq.astype(BF16)
        dk_ref[...] = dk.astype(BF16)
        dvb_ref[...] = dv_ref[...].astype(BF16)
        dgq_ref[...] = dgq
        dgk_ref[...] = dgk

    col = lambda c: pl.BlockSpec((None, tm, ATTN_W), lambda bi, j: (bi, j, c))
    past_pad = pl.BlockSpec((None, tm, ATTN_W), lambda bi, j: (bi, j + 1, 0))
    gspec = pl.BlockSpec((1, ATTN_W), lambda bi, j: (0, 0))
    pspec = pl.BlockSpec((None, 1, ATTN_W), lambda bi, j: (bi * nb + j, 0, 0))
    o_shape = jax.ShapeDtypeStruct((b, s, ATTN_W), BF16)
    p_shape = jax.ShapeDtypeStruct((b * nb, 1, ATTN_W), F32)
    return pl.pallas_call(
        body,
        name=name,
        grid=(b, nb),
        in_specs=[col(0), col(1), col(0), past_pad, past_pad, gspec, gspec],
        out_specs=[col(0)] * 3 + [pspec] * 2,
        out_shape=[o_shape] * 3 + [p_shape] * 2,
        compiler_params=_params("parallel", "parallel"),
    )(proj, proj, dqn, dkn, dv, gq, gk)


Q_CHUNKS = 4
QBLK = Q_CHUNKS * CHUNK
WIN = (LEFT_CHUNKS + Q_CHUNKS) * CHUNK
DB_W = BAND + CHUNK
MASKED = -1e30


def _band_table(bias):
    rows = [jnp.pad(bias, ((0, 0), (0, 0), (CHUNK * i, WIN - BAND - CHUNK * i)), constant_values=MASKED)
            for i in range(Q_CHUNKS)]
    return jnp.concatenate(rows, axis=1)


def _head_lanes(hh):
    lane = lax.broadcasted_iota(jnp.int32, (1, LANES), 1)
    return (lane < ATTN_DH) if hh == 0 else (lane >= ATTN_DH)


def _attn_probs(qh, kw, table, start):
    s = _dot(qh, kw, NT) * (ATTN_DH ** -0.5) + table
    col = lax.broadcasted_iota(jnp.int32, (QBLK, WIN), 1)
    s = jnp.where(col + start >= KPAD, s, MASKED)
    m = jnp.max(s, axis=-1, keepdims=True)
    p = jnp.exp(s - m)
    return p * (1.0 / jnp.sum(p, axis=-1, keepdims=True))


def _attn_fwd(name, q, k, v, table):
    b, s, w = q.shape
    sp = k.shape[1]

    def body(q_ref, k_ref, v_ref, t_ref, o_ref):
        start = pl.multiple_of(pl.program_id(2) * QBLK, QBLK)
        kw = k_ref[pl.ds(start, WIN), :]
        vw = v_ref[pl.ds(start, WIN), :]
        q2 = q_ref[...]
        lanes = [_head_lanes(hh) for hh in range(2)]
        probs = [_attn_probs(jnp.where(mine, q2, jnp.zeros_like(q2)), kw, t_ref[hh], start).astype(BF16)
                 for hh, mine in enumerate(lanes)]
        outs = [_dot(p, vw) for p in probs]
        o_ref[...] = jnp.where(lanes[0], outs[0], outs[1]).astype(BF16)

    qspec = pl.BlockSpec((None, QBLK, LANES), lambda p, bi, i: (bi, i, p))
    kspec = pl.BlockSpec((None, sp, LANES), lambda p, bi, i: (bi, 0, p))
    return pl.pallas_call(
        body,
        name=name,
        grid=(w // LANES, b, s // QBLK),
        in_specs=[qspec, kspec, kspec, pl.BlockSpec((2, QBLK, WIN), lambda p, bi, i: (p, 0, 0))],
        out_specs=qspec,
        out_shape=jax.ShapeDtypeStruct((b, s, w), BF16),
        compiler_params=_params("parallel", "parallel", "arbitrary"),
    )(q, k, v, table)


def _attn_bwd(name, q, k, v, table, dmix):
    b, s, w = q.shape
    sp = k.shape[1]

    def body(q_ref, k_ref, v_ref, t_ref, do_ref, dq_ref, dk_ref, dv_ref, dbe_ref, dbo_ref):
        bi = pl.program_id(1)
        i = pl.program_id(2)
        start = pl.multiple_of(i * QBLK, QBLK)
        win = pl.ds(start, WIN)

        @pl.when(i == 0)
        def _():
            dk_ref[...] = jnp.zeros_like(dk_ref)
            dv_ref[...] = jnp.zeros_like(dv_ref)

        @pl.when(jnp.logical_and(i == 0, bi == 0))
        def _():
            dbe_ref[...] = jnp.zeros_like(dbe_ref)
            dbo_ref[...] = jnp.zeros_like(dbo_ref)

        kw = k_ref[win, :]
        vw = v_ref[win, :]
        q2 = q_ref[...]
        do2 = do_ref[...].astype(BF16)
        lanes = [_head_lanes(hh) for hh in range(2)]
        qh = [jnp.where(mine, q2, jnp.zeros_like(q2)) for mine in lanes]
        doh = [jnp.where(mine, do2, jnp.zeros_like(do2)) for mine in lanes]
        p = [_attn_probs(qh[hh], kw, t_ref[hh], start) for hh in range(2)]
        dp = [_dot(doh[hh], vw, NT) for hh in range(2)]
        ds = [p[hh] * (dp[hh] - jnp.sum(p[hh] * dp[hh], axis=-1, keepdims=True)) for hh in range(2)]
        dsb = [(x * (ATTN_DH ** -0.5)).astype(BF16) for x in ds]
        pb = [x.astype(BF16) for x in p]
        dq = [_dot(dsb[hh], kw) for hh in range(2)]
        dk = [_dot(dsb[hh], qh[hh], TN) for hh in range(2)]
        dv = [_dot(pb[hh], doh[hh], TN) for hh in range(2)]
        for hh in range(2):
            for qi in range(Q_CHUNKS):
                c0 = (qi // 2) * LANES
                blk = ds[hh][qi * CHUNK:(qi + 1) * CHUNK, c0:c0 + DB_W]
                if qi % 2 == 0:
                    dbe_ref[hh] += blk
                else:
                    dbo_ref[hh] += blk
        dq_ref[...] = jnp.where(lanes[0], dq[0], dq[1])
        dk_ref[win, :] += dk[0] + dk[1]
        dv_ref[win, :] += dv[0] + dv[1]

    qspec = pl.BlockSpec((None, QBLK, LANES), lambda p, bi, i: (bi, i, p))
    kspec = pl.BlockSpec((None, sp, LANES), lambda p, bi, i: (bi, 0, p))
    dbspec = pl.BlockSpec((2, CHUNK, DB_W), lambda p, bi, i: (p, 0, 0))
    db_shape = jax.ShapeDtypeStruct((ATTN_HEADS, CHUNK, DB_W), F32)
    return pl.pallas_call(
        body,
        name=name,
        grid=(w // LANES, b, s // QBLK),
        in_specs=[qspec, kspec, kspec, pl.BlockSpec((2, QBLK, WIN), lambda p, bi, i: (p, 0, 0)), qspec],
        out_specs=[qspec, kspec, kspec, dbspec, dbspec],
        out_shape=[jax.ShapeDtypeStruct((b, s, w), F32), jax.ShapeDtypeStruct((b, sp, w), F32),
                   jax.ShapeDtypeStruct((b, sp, w), F32), db_shape, db_shape],
        compiler_params=_params("arbitrary", "arbitrary", "arbitrary"),
    )(q, k, v, table, dmix)


HQ_COL = 3 * ATTN_W // HGRN_DH
HF_COL = HQ_COL + HGRN_HEADS
HI_COL = HF_COL + HGRN_HEADS
HG_COL = HI_COL + HGRN_HEADS
HGRN_ROWS = 8 * CHUNK
HEAD_LANES = [slice(hh * HGRN_DH, (hh + 1) * HGRN_DH) for hh in range(HGRN_HEADS)]


def _tri(lower):
    r = lax.broadcasted_iota(jnp.int32, (CHUNK, CHUNK), 0)
    c = lax.broadcasted_iota(jnp.int32, (CHUNK, CHUNK), 1)
    return (r >= c) if lower else (r <= c)


def _hgrn_chunk(hq, hf, lb, tril):
    sig = _sigmoid(hf)
    f = lb + (1.0 - lb) * sig
    g = jnp.log(f)
    ones_l = jnp.where(tril, 1.0, 0.0).astype(BF16)
    b = _dot_exact_lhs(ones_l, g)
    bl = jnp.sum(g, axis=0, keepdims=True)
    rows = lax.broadcasted_iota(jnp.int32, g.shape, 0)
    bm = jnp.sum(jnp.where(rows <= CHUNK // 2, g, 0.0), axis=0, keepdims=True)
    sq = _sigmoid(hq)
    q = hq * sq
    k = 1.0 - f
    return sig, f, b, bl, bm, sq, q, k


def _hgrn_fwd(name, proj, attn, lb, go, b, s):
    nc = s // CHUNK
    t = b * s
    nblk = s // HGRN_ROWS
    cpb = HGRN_ROWS // CHUNK

    def body(hq_ref, hf_ref, hi_ref, hg_ref, attn_ref, lb_ref, go_ref, mix_ref, oraw_ref, st_ref, s_scr):
        tril = _tri(True)
        gov = go_ref[...]
        mix_ref[:, 0:ATTN_W] = attn_ref[...]

        @pl.when(pl.program_id(1) == 0)
        def _():
            s_scr[...] = jnp.zeros_like(s_scr)

        def step(c, carry):
            sl = pl.ds(pl.multiple_of(c * CHUNK, CHUNK), CHUNK)
            hg = hg_ref[sl, :]
            _, _, bb, bl, bm, _, q, k = _hgrn_chunk(hq_ref[sl, :], hf_ref[sl, :], lb_ref[...], tril)
            vb = hi_ref[sl, :].astype(BF16)
            qe = (q * jnp.exp(bb - bm)).astype(BF16)
            ke = (k * jnp.exp(bm - bb)).astype(BF16)
            qb = (q * jnp.exp(bb)).astype(BF16)
            kb = (k * jnp.exp(bl - bb)).astype(BF16)
            e_last = jnp.exp(bl)
            gate = _silu(hg)
            st = [s_scr[hh] for hh in range(HGRN_HEADS)]
            a = [jnp.where(tril, _dot(qe[:, hs], ke[:, hs], NT), 0.0).astype(BF16) for hs in HEAD_LANES]
            o_state = [_dot(qb[:, hs], st[hh].astype(BF16), NT) for hh, hs in enumerate(HEAD_LANES)]
            st_next = [st[hh] * e_last[:, hs] + _dot(vb[:, hs], kb[:, hs], TN) for hh, hs in enumerate(HEAD_LANES)]
            o = [_dot(a[hh], vb[:, hs]) + o_state[hh] for hh, hs in enumerate(HEAD_LANES)]
            ro = [(oh * lax.rsqrt(jnp.mean(oh * oh, axis=-1, keepdims=True) + RMS_EPS) * gov) * gate[:, hs]
                  for oh, hs in zip(o, HEAD_LANES)]
            for hh in range(HGRN_HEADS):
                st_ref[hh, c] = st[hh]
                s_scr[hh] = st_next[hh]
            mix_ref[sl, ATTN_W:ATTN_W + HGRN_W] = jnp.concatenate(ro, axis=1).astype(BF16)
            oraw_ref[sl, :] = jnp.concatenate(o, axis=1)
            return carry

        lax.fori_loop(0, cpb, step, 0)

    col = lambda base: pl.BlockSpec((HGRN_ROWS, HGRN_W), lambda bi, i: (bi * nblk + i, base // HGRN_HEADS))
    out = pl.BlockSpec((HGRN_ROWS, HGRN_W), lambda bi, i: (bi * nblk + i, 0))
    return pl.pallas_call(
        body,
        name=name,
        grid=(b, nblk),
        in_specs=[col(HQ_COL), col(HF_COL), col(HI_COL), col(HG_COL), out,
                  pl.BlockSpec((1, HGRN_W), lambda bi, i: (0, 0)), pl.BlockSpec((1, HGRN_DH), lambda bi, i: (0, 0))],
        out_specs=[pl.BlockSpec((HGRN_ROWS, ATTN_W + HGRN_W), lambda bi, i: (bi * nblk + i, 0)), out,
                   pl.BlockSpec((None, HGRN_HEADS, cpb, HGRN_DH, HGRN_DH), lambda bi, i: (bi, 0, i, 0, 0))],
        out_shape=[jax.ShapeDtypeStruct((t, ATTN_W + HGRN_W), BF16), jax.ShapeDtypeStruct((t, HGRN_W), F32),
                   jax.ShapeDtypeStruct((b, HGRN_HEADS, nc, HGRN_DH, HGRN_DH), F32)],
        scratch_shapes=[pltpu.VMEM((HGRN_HEADS, HGRN_DH, HGRN_DH), F32)],
        compiler_params=_params("parallel", "arbitrary"),
    )(proj, proj, proj, proj, attn, lb, go)


def _hgrn_bwd(name, proj, dqkv, lb, go, oraw, states, dmix, b, s):
    t = b * s
    nblk = s // HGRN_ROWS
    cpb = HGRN_ROWS // CHUNK

    def body(hq_ref, hf_ref, hi_ref, hg_ref, dq_ref, dk_ref, dv_ref, lb_ref, go_ref, oraw_ref, st_ref, dro_ref,
             dp_ref, dlb_ref, dgo_ref, ds_scr, dlb_scr, dgo_scr):
        tril = _tri(True)
        ones_u = jnp.where(_tri(False), 1.0, 0.0).astype(BF16)
        gov = go_ref[...]
        dp_ref[:, 0:ATTN_W] = dq_ref[...]
        dp_ref[:, ATTN_W:2 * ATTN_W] = dk_ref[...]
        dp_ref[:, 2 * ATTN_W:3 * ATTN_W] = dv_ref[...]

        @pl.when(pl.program_id(1) == 0)
        def _():
            ds_scr[...] = jnp.zeros_like(ds_scr)
            dlb_scr[...] = jnp.zeros_like(dlb_scr)
            dgo_scr[...] = jnp.zeros_like(dgo_scr)

        def step(ci, carry):
            c = cpb - 1 - ci
            sl = pl.ds(pl.multiple_of(c * CHUNK, CHUNK), CHUNK)
            hq = hq_ref[sl, :]
            hg = hg_ref[sl, :]
            sig, f, bb, bl, bm, sq, q, k = _hgrn_chunk(hq, hf_ref[sl, :], lb_ref[...], tril)
            vb = hi_ref[sl, :].astype(BF16)
            ebm = jnp.exp(bb - bm)
            embm = jnp.exp(bm - bb)
            eb = jnp.exp(bb)
            ebl = jnp.exp(bl - bb)
            e_last = jnp.exp(bl)
            qe = (q * ebm).astype(BF16)
            ke = (k * embm).astype(BF16)
            qb = (q * eb).astype(BF16)
            kb = (k * ebl).astype(BF16)
            st = [st_ref[hh, c] for hh in range(HGRN_HEADS)]
            dst = [ds_scr[hh] for hh in range(HGRN_HEADS)]
            o = oraw_ref[sl, :]
            dro = dro_ref[sl, :]
            sg = _sigmoid(hg)
            gov4 = jnp.concatenate([gov] * HGRN_HEADS, axis=1)
            rstd = jnp.concatenate(
                [jnp.broadcast_to(lax.rsqrt(jnp.mean(o[:, hs] * o[:, hs], axis=-1, keepdims=True) + RMS_EPS),
                                  (CHUNK, HGRN_DH)) for hs in HEAD_LANES], axis=1)
            ohat = o * rstd
            dn = dro * (hg * sg)
            dhg = dro * (ohat * gov4) * (sg * (1.0 + hg * (1.0 - sg)))
            dgo_inc = jnp.sum(dn * ohat, axis=0, keepdims=True)
            dohat = dn * gov4
            proj_h = dohat * ohat
            pm = jnp.concatenate(
                [jnp.broadcast_to(jnp.mean(proj_h[:, hs], axis=-1, keepdims=True), (CHUNK, HGRN_DH))
                 for hs in HEAD_LANES], axis=1)
            dob = (rstd * (dohat - ohat * pm)).astype(BF16)
            stb = [x.astype(BF16) for x in st]
            dstb = [x.astype(BF16) for x in dst]
            a = [jnp.where(tril, _dot(qe[:, hs], ke[:, hs], NT), 0.0).astype(BF16) for hs in HEAD_LANES]
            dab = [jnp.where(tril, _dot(dob[:, hs], vb[:, hs], NT), 0.0).astype(BF16) for hs in HEAD_LANES]
            dqb = [_dot(dob[:, hs], stb[hh]) for hh, hs in enumerate(HEAD_LANES)]
            dkb = [_dot(vb[:, hs], dstb[hh]) for hh, hs in enumerate(HEAD_LANES)]
            dv_state = [_dot(kb[:, hs], dstb[hh], NT) for hh, hs in enumerate(HEAD_LANES)]
            dst_next = [dst[hh] * e_last[:, hs] + _dot(dob[:, hs], qb[:, hs], TN) for hh, hs in enumerate(HEAD_LANES)]
            dv = [_dot(a[hh], dob[:, hs], TN) + dv_state[hh] for hh, hs in enumerate(HEAD_LANES)]
            dqe = jnp.concatenate([_dot(dab[hh], ke[:, hs]) for hh, hs in enumerate(HEAD_LANES)], axis=1)
            dke = jnp.concatenate([_dot(dab[hh], qe[:, hs], TN) for hh, hs in enumerate(HEAD_LANES)], axis=1)
            dqb = jnp.concatenate(dqb, axis=1)
            dkb = jnp.concatenate(dkb, axis=1)
            state_term = jnp.concatenate(
                [jnp.sum(dst[hh] * st[hh], axis=0, keepdims=True) for hh in range(HGRN_HEADS)], axis=1)
            dq = dqe * ebm + dqb * eb
            dk = dke * embm + dkb * ebl
            db = (qe.astype(F32) * dqe - ke.astype(F32) * dke) + q * (dqb * eb) - k * (dkb * ebl)
            d_last = jnp.sum(k * ebl * dkb, axis=0, keepdims=True) + state_term * e_last
            dg = _dot_exact_lhs(ones_u, db) + d_last
            df = dg / f - dk
            first = HQ_COL * HGRN_DH
            dp_ref[sl, first:first + HGRN_W] = (dq * (sq * (1.0 + hq * (1.0 - sq)))).astype(BF16)
            dp_ref[sl, first + HGRN_W:first + 2 * HGRN_W] = (df * (1.0 - lb_ref[...]) * sig * (1.0 - sig)).astype(BF16)
            dp_ref[sl, first + 2 * HGRN_W:first + 3 * HGRN_W] = jnp.concatenate(dv, axis=1).astype(BF16)
            dp_ref[sl, first + 3 * HGRN_W:first + 4 * HGRN_W] = dhg.astype(BF16)
            dlb_scr[...] += jnp.sum(df * (1.0 - sig), axis=0, keepdims=True)
            dgo_scr[...] += dgo_inc
            for hh in range(HGRN_HEADS):
                ds_scr[hh] = dst_next[hh]
            return carry

        lax.fori_loop(0, cpb, step, 0)

        @pl.when(pl.program_id(1) == nblk - 1)
        def _():
            dlb_ref[...] = dlb_scr[...]
            dgo_ref[...] = dgo_scr[...]

    rows = lambda bi, i: bi * nblk + (nblk - 1 - i)
    col = lambda base: pl.BlockSpec((HGRN_ROWS, HGRN_W), lambda bi, i: (rows(bi, i), base // HGRN_HEADS))
    out = pl.BlockSpec((HGRN_ROWS, HGRN_W), lambda bi, i: (rows(bi, i), 0))
    part = pl.BlockSpec((None, 1, HGRN_W), lambda bi, i: (bi, 0, 0))
    width = HG_COL * HGRN_DH + HGRN_W
    o_shape = jax.ShapeDtypeStruct((t, width), BF16)
    p_shape = jax.ShapeDtypeStruct((b, 1, HGRN_W), F32)
    return pl.pallas_call(
        body,
        name=name,
        grid=(b, nblk),
        in_specs=[col(HQ_COL), col(HF_COL), col(HI_COL), col(HG_COL), out, out, out,
                  pl.BlockSpec((1, HGRN_W), lambda bi, i: (0, 0)), pl.BlockSpec((1, HGRN_DH), lambda bi, i: (0, 0)), out,
                  pl.BlockSpec((None, HGRN_HEADS, cpb, HGRN_DH, HGRN_DH), lambda bi, i: (bi, 0, nblk - 1 - i, 0, 0)),
                  col(ATTN_W // HGRN_DH)],
        out_specs=[pl.BlockSpec((HGRN_ROWS, width), lambda bi, i: (rows(bi, i), 0))] + [part] * 2,
        out_shape=[o_shape] + [p_shape] * 2,
        scratch_shapes=[pltpu.VMEM((HGRN_HEADS, HGRN_DH, HGRN_DH), F32), pltpu.VMEM((1, HGRN_W), F32),
                        pltpu.VMEM((1, HGRN_W), F32)],
        compiler_params=_params("parallel", "arbitrary"),
    )(proj, proj, proj, proj, *dqkv, lb, go, oraw, states, dmix)


def _small_grads(name, dg1, dgm, dg2, dgq, dgk, dbias_t, dlb, dgo, lbp):
    d = dg1.shape[1]

    def body(dg1_ref, dgm_ref, dg2_ref, dgq_ref, dgk_ref, dbias_ref, dlb_ref, dgo_ref, lbp_ref,
             g1_ref, gm_ref, g2_ref, gq_ref, gk_ref, rb_ref, lbg_ref, go_ref):
        g1_ref[...] = jnp.sum(dg1_ref[...], axis=0, keepdims=True)
        gm_ref[...] = jnp.sum(dgm_ref[...], axis=0, keepdims=True)
        g2_ref[...] = jnp.sum(dg2_ref[...], axis=0, keepdims=True)
        r = lax.broadcasted_iota(jnp.int32, (ATTN_W, ATTN_DH), 0)
        cidx = lax.broadcasted_iota(jnp.int32, (ATTN_W, ATTN_DH), 1)
        fold = jnp.where(jnp.bitwise_and(r, ATTN_DH - 1) == cidx, 1.0, 0.0).astype(BF16)
        gq_ref[...] = jnp.sum(_dot_exact_rhs(dgq_ref[...], fold), axis=0, keepdims=True)
        gk_ref[...] = jnp.sum(_dot_exact_rhs(dgk_ref[...], fold), axis=0, keepdims=True)
        gosum = jnp.sum(dgo_ref[...], axis=0, keepdims=True)
        go_ref[...] = (gosum[:, 0:HGRN_DH] + gosum[:, HGRN_DH:2 * HGRN_DH]
                       + gosum[:, 2 * HGRN_DH:3 * HGRN_DH] + gosum[:, 3 * HGRN_DH:4 * HGRN_DH])
        p0 = lbp_ref[0:1, :]
        p1 = lbp_ref[1:2, :]
        lbv = 1.0 / (1.0 + jnp.exp(p1 - p0))
        dp0 = jnp.sum(dlb_ref[...], axis=0, keepdims=True) * lbv * (1.0 - lbv)
        lbg_ref[0:1, :] = dp0
        lbg_ref[1:2, :] = -dp0
        sidx = lax.broadcasted_iota(jnp.int32, (BAND, N_REL_PAD), 0)
        ridx = lax.broadcasted_iota(jnp.int32, (BAND, N_REL_PAD), 1)

        def step(tq, acc):
            rel = jnp.clip(tq + KPAD - sidx, -REL_CLIP, REL_CLIP) + REL_CLIP
            onehot = jnp.where(rel == ridx, 1.0, 0.0).astype(BF16)
            return acc + _dot_exact_rhs(dbias_ref[tq], onehot)

        rb_ref[...] = lax.fori_loop(0, CHUNK, step, jnp.zeros((ATTN_HEADS, N_REL_PAD), F32))

    ins = [dg1, dgm, dg2, dgq, dgk, dbias_t, dlb, dgo, lbp]
    outs = [jax.ShapeDtypeStruct((1, d), F32)] * 3 + [jax.ShapeDtypeStruct((1, ATTN_DH), F32)] * 2 + [
        jax.ShapeDtypeStruct((ATTN_HEADS, N_REL_PAD), F32), jax.ShapeDtypeStruct((2, HGRN_W), F32),
        jax.ShapeDtypeStruct((1, HGRN_DH), F32)]
    vm = pl.BlockSpec(memory_space=pltpu.VMEM)
    return pl.pallas_call(
        body,
        name=name,
        in_specs=[vm] * len(ins),
        out_specs=[vm] * len(outs),
        out_shape=outs,
        compiler_params=pltpu.CompilerParams(vmem_limit_bytes=VMEM_LIMIT),
    )(*ins)


def _adam_update(w, g, m, v):
    m2 = ADAM_B1 * m + (1.0 - ADAM_B1) * g
    v2 = ADAM_B2 * v + (1.0 - ADAM_B2) * (g * g)
    m_hat = m2 / (1.0 - ADAM_B1 ** ADAM_STEP)
    v_hat = v2 / (1.0 - ADAM_B2 ** ADAM_STEP)
    delta = -ADAM_LR * (m_hat / (jnp.sqrt(v_hat) + ADAM_EPS) + ADAM_WD * w)
    return delta, m2, v2


def _rows_tile(r):
    for cand in (256, 352, 128, 176, 64, 32, 16):
        if r % cand == 0 and r > cand:
            return cand
    return r


def _pair_sum(name, grad, theirs, core):
    n, half, c = theirs.shape
    tr = _rows_tile(half)
    nth = half // tr

    def body(core_ref, a_ref, b_ref, o_ref):
        o_ref[...] = (a_ref[...].astype(F32) + b_ref[...].astype(F32)).astype(o_ref.dtype)

    spec = pl.BlockSpec((None, tr, c), lambda i, j, core_ref: (i, j, 0))
    return pl.pallas_call(
        body, name=name,
        grid_spec=pltpu.PrefetchScalarGridSpec(
            num_scalar_prefetch=1, grid=(n, nth),
            in_specs=[pl.BlockSpec((None, tr, c), lambda i, j, core_ref: (i, core_ref[0] * nth + j, 0)), spec],
            out_specs=spec),
        out_shape=jax.ShapeDtypeStruct((n, half, c), BF16), compiler_params=_params("parallel", "parallel"),
    )(core, grad, theirs)


def _chip_sum(name, own, parts, chip):
    _, half, c = own.shape
    tr = _rows_tile(half)

    def body(chip_ref, own_ref, p_ref, o_ref):
        me = chip_ref[0]
        mine = own_ref[...].astype(F32)
        flip_x, flip_y, flip_xy = (p_ref[i].astype(F32) for i in range(3))
        acc = None
        for k in range(N_CHIPS):
            rel = jnp.bitwise_xor(me, k)
            term = jnp.where(rel == 0, mine, jnp.where(rel == 2, flip_x, jnp.where(rel == 1, flip_y, flip_xy)))
            acc = term if acc is None else acc + term
        o_ref[...] = acc

    return pl.pallas_call(
        body, name=name,
        grid_spec=pltpu.PrefetchScalarGridSpec(
            num_scalar_prefetch=1, grid=(half // tr,),
            in_specs=[pl.BlockSpec((None, tr, c), lambda j, chip_ref: (chip_ref[0], j, 0)),
                      pl.BlockSpec((3, tr, c), lambda j, chip_ref: (0, j, 0))],
            out_specs=pl.BlockSpec((tr, c), lambda j, chip_ref: (j, 0))),
        out_shape=jax.ShapeDtypeStruct((half, c), F32), compiler_params=_params("parallel"),
    )(chip, own, parts)


def _adamw(name, w, g_mine, g_theirs, m, v, core):
    _, r, c = w.shape
    half = r // 2
    tr = _rows_tile(half)
    nth = half // tr

    def body(core_ref, w_ref, gm_ref, gt_ref, m_ref, v_ref, g_ref, d_ref, m2_ref, v2_ref):
        g = jnp.where(pl.program_id(0) == core_ref[0], gm_ref[...], gt_ref[...])
        delta, m2, v2 = _adam_update(w_ref[...], g, m_ref[...], v_ref[...])
        g_ref[...] = g
        d_ref[...] = delta
        m2_ref[...] = m2
        v2_ref[...] = v2

    full = pl.BlockSpec((None, tr, c), lambda h, j, core_ref: (0, h * nth + j, 0))
    part = pl.BlockSpec((tr, c), lambda h, j, core_ref: (j, 0))
    shape = jax.ShapeDtypeStruct((1, r, c), F32)
    return pl.pallas_call(
        body, name=name,
        grid_spec=pltpu.PrefetchScalarGridSpec(
            num_scalar_prefetch=1, grid=(2, nth), in_specs=[full, part, part, full, full], out_specs=[full] * 4),
        out_shape=[shape] * 4, compiler_params=_params("parallel", "parallel"),
    )(core, w, g_mine, g_theirs, m, v)


def _rel_bias_table(name, rel_bias):
    padded = jnp.pad(rel_bias, ((0, 0), (0, N_REL_PAD - N_REL)))

    def body(rb_ref, o_ref):
        ridx = lax.broadcasted_iota(jnp.int32, (N_REL_PAD, BAND), 0)
        sidx = lax.broadcasted_iota(jnp.int32, (N_REL_PAD, BAND), 1)
        rb = rb_ref[...]

        def step(tq, carry):
            rel = jnp.clip(tq + KPAD - sidx, -REL_CLIP, REL_CLIP) + REL_CLIP
            onehot = jnp.where(rel == ridx, 1.0, 0.0).astype(BF16)
            o_ref[tq] = _dot_exact_rhs(rb, onehot)
            return carry

        lax.fori_loop(0, CHUNK, step, 0)

    vm = pl.BlockSpec(memory_space=pltpu.VMEM)
    table = pl.pallas_call(
        body, name=name, in_specs=[vm], out_specs=vm,
        out_shape=jax.ShapeDtypeStruct((CHUNK, ATTN_HEADS, BAND), F32),
    )(padded)
    return table.transpose(1, 0, 2)


def _adamw_small(name, w, parts, m, v):
    def body(w_ref, p_ref, m_ref, v_ref, g_ref, d_ref, m2_ref, v2_ref):
        g = p_ref[0]
        for i in range(1, N_DEV):
            g = g + p_ref[i]
        delta, m2, v2 = _adam_update(w_ref[...], g, m_ref[...], v_ref[...])
        g_ref[...] = g
        d_ref[...] = delta
        m2_ref[...] = m2
        v2_ref[...] = v2

    vm = pl.BlockSpec(memory_space=pltpu.VMEM)
    shape = jax.ShapeDtypeStruct((SMALL_ROWS, SMALL_COLS), F32)
    return pl.pallas_call(
        body, name=name, in_specs=[vm] * 4, out_specs=[vm] * 4, out_shape=[shape] * 4,
    )(w, parts, m, v)


def _position():
    return lax.axis_index("x"), lax.axis_index("y"), lax.axis_index("c")


def _other_chips(x, y):
    return [(1 - x, y), (x, 1 - y), (1 - x, 1 - y)]


ANY = pl.BlockSpec(memory_space=pl.ANY)


HBM = pl.BlockSpec(memory_space=pltpu.HBM)
SEM = pl.BlockSpec(memory_space=pltpu.SEMAPHORE)
SPLIT_COPY = pltpu.SideEffectType.DATAFLOW_SIDE_EFFECTING


def _gather_copy(shards, outs, send_sem, recv_sem, i, j):
    x, y, c = _position()
    chips = _other_chips(x, y)
    half = shards[i].shape[0] // 2
    rows = pl.ds(pl.multiple_of(c * half, 16), half)
    return pltpu.make_async_remote_copy(
        src_ref=shards[i].at[rows, :], dst_ref=outs[i].at[2 * x + y, rows, :],
        send_sem=send_sem.at[3 * i + j], recv_sem=recv_sem.at[3 * i + j],
        device_id=(chips[j][0], chips[j][1], c), device_id_type=MESH)


def _gather_start(name, shards, after):
    n = len(shards)

    def body(*refs):
        srcs, outs = refs[:n], refs[n:2 * n]
        send_sem, recv_sem = refs[2 * n + len(after)], refs[2 * n + len(after) + 1]
        token = refs[-1]
        for i in range(n):
            for j in range(3):
                _gather_copy(srcs, outs, send_sem, recv_sem, i, j).start()
        token[...] = jnp.zeros_like(token)

    full = [(N_CHIPS,) + s.shape for s in shards]
    res = pl.pallas_call(
        body,
        name=name,
        in_specs=[HBM] * (2 * n) + [ANY] * len(after),
        out_specs=[SEM, SEM] + [HBM] * (2 * n) + [pl.BlockSpec(memory_space=pltpu.VMEM)],
        out_shape=[pltpu.SemaphoreType.DMA((3 * n,)), pltpu.SemaphoreType.DMA((3 * n,))]
        + [pltpu.HBM(s.shape, s.dtype) for s in shards]
        + [pltpu.HBM(shp, s.dtype) for shp, s in zip(full, shards)]
        + [jax.ShapeDtypeStruct((8, LANES), F32)],
        input_output_aliases={i: 2 + i for i in range(2 * n)},
        compiler_params=pltpu.CompilerParams(has_side_effects=SPLIT_COPY),
    )(*[pltpu.with_memory_space_constraint(s, pltpu.HBM) for s in shards],
      *[pltpu.with_memory_space_constraint(lax.empty(shp, s.dtype), pltpu.HBM) for shp, s in zip(full, shards)],
      *after)
    return res[0], res[1], list(res[2:2 + n]), list(res[2 + n:2 + 2 * n]), res[-1]


def _gather_wait(name, send_sem, recv_sem, shards, outs, after):
    n = len(shards)

    def body(*refs):
        srcs, out_refs = refs[:n], refs[n:2 * n]
        send_ref, recv_ref = refs[2 * n], refs[2 * n + 1]
        for i in range(n):
            for j in range(3):
                copy = _gather_copy(srcs, out_refs, send_ref, recv_ref, i, j)
                copy.wait_send()
                copy.wait_recv()

    res = pl.pallas_call(
        body,
        name=name,
        in_specs=[HBM] * (2 * n) + [SEM, SEM] + [ANY] * len(after),
        out_specs=[HBM] * (2 * n),
        out_shape=[pltpu.HBM(s.shape, s.dtype) for s in shards] + [pltpu.HBM(o.shape, o.dtype) for o in outs],
        input_output_aliases={i: i for i in range(2 * n)},
        compiler_params=pltpu.CompilerParams(has_side_effects=SPLIT_COPY),
    )(*shards, *outs, send_sem, recv_sem, *after)
    return list(res[:n]), list(res[n:])


def _gather_join(name, shards, outs):
    n = len(shards)

    def body(*refs):
        srcs, ins, outs_ = refs[:n], refs[n:2 * n], refs[2 * n:3 * n]
        own_send, own_recv, half_send, half_recv = refs[3 * n:]
        x, y, c = _position()
        chips = _other_chips(x, y)
        copies = []
        for i in range(n):
            copies.append(pltpu.make_async_remote_copy(
                src_ref=srcs[i], dst_ref=outs_[i].at[2 * x + y], send_sem=own_send.at[i], recv_sem=own_recv.at[i],
                device_id=(x, y, 1 - c), device_id_type=MESH))
            half = srcs[i].shape[0] // 2
            rows = pl.ds(pl.multiple_of(c * half, 16), half)
            for j in range(3):
                slot = 2 * chips[j][0] + chips[j][1]
                copies.append(pltpu.make_async_remote_copy(
                    src_ref=ins[i].at[slot, rows, :], dst_ref=outs_[i].at[slot, rows, :],
                    send_sem=half_send.at[3 * i + j], recv_sem=half_recv.at[3 * i + j],
                    device_id=(x, y, 1 - c), device_id_type=MESH))
        for cp in copies:
            cp.start()
        for cp in copies:
            cp.wait()

    return pl.pallas_call(
        body,
        name=name,
        in_specs=[ANY] * (2 * n),
        out_specs=[ANY] * n,
        out_shape=[jax.ShapeDtypeStruct(o.shape, o.dtype) for o in outs],
        input_output_aliases={n + i: i for i in range(n)},
        scratch_shapes=[pltpu.SemaphoreType.DMA((n,))] * 2 + [pltpu.SemaphoreType.DMA((3 * n,))] * 2,
    )(*shards, *outs)


def _pair_copy(grads, lands, send_sem, recv_sem, i):
    x, y, c = _position()
    half = grads[i].shape[1] // 2
    give = pl.ds(pl.multiple_of((1 - c) * half, 16), half)
    return pltpu.make_async_remote_copy(
        src_ref=grads[i].at[:, give, :], dst_ref=lands[i], send_sem=send_sem.at[i], recv_sem=recv_sem.at[i],
        device_id=(x, y, 1 - c), device_id_type=MESH)


def _pair_start(name, grads):
    n = len(grads)

    def body(*refs):
        srcs, lands = refs[:n], refs[n:2 * n]
        send_sem, recv_sem = refs[2 * n], refs[2 * n + 1]
        token = refs[-1]
        for i in range(n):
            _pair_copy(srcs, lands, send_sem, recv_sem, i).start()
        token[...] = jnp.zeros_like(token)

    halves = [(g.shape[0], g.shape[1] // 2, g.shape[2]) for g in grads]
    res = pl.pallas_call(
        body,
        name=name,
        in_specs=[HBM] * (2 * n),
        out_specs=[SEM, SEM] + [HBM] * (2 * n) + [pl.BlockSpec(memory_space=pltpu.VMEM)],
        out_shape=[pltpu.SemaphoreType.DMA((n,)), pltpu.SemaphoreType.DMA((n,))]
        + [pltpu.HBM(g.shape, g.dtype) for g in grads]
        + [pltpu.HBM(shp, g.dtype) for shp, g in zip(halves, grads)]
        + [jax.ShapeDtypeStruct((8, LANES), F32)],
        input_output_aliases={i: 2 + i for i in range(2 * n)},
        compiler_params=pltpu.CompilerParams(has_side_effects=SPLIT_COPY),
    )(*[pltpu.with_memory_space_constraint(g, pltpu.HBM) for g in grads],
      *[pltpu.with_memory_space_constraint(lax.empty(shp, g.dtype), pltpu.HBM) for shp, g in zip(halves, grads)])
    return res[0], res[1], list(res[2:2 + n]), list(res[2 + n:2 + 2 * n]), res[-1]


def _pair_wait(name, send_sem, recv_sem, grads, lands, after):
    n = len(grads)

    def body(*refs):
        srcs, land_refs = refs[:n], refs[n:2 * n]
        send_ref, recv_ref = refs[2 * n], refs[2 * n + 1]
        for i in range(n):
            copy = _pair_copy(srcs, land_refs, send_ref, recv_ref, i)
            copy.wait_send()
            copy.wait_recv()

    res = pl.pallas_call(
        body,
        name=name,
        in_specs=[HBM] * (2 * n) + [SEM, SEM, ANY],
        out_specs=[HBM] * (2 * n),
        out_shape=[pltpu.HBM(g.shape, g.dtype) for g in grads] + [pltpu.HBM(l.shape, l.dtype) for l in lands],
        input_output_aliases={i: i for i in range(2 * n)},
        compiler_params=pltpu.CompilerParams(has_side_effects=SPLIT_COPY),
    )(*grads, *lands, send_sem, recv_sem, after)
    return list(res[:n]), list(res[n:])


def _scatter_copy(srcs, lands, send_sem, recv_sem, i, j):
    x, y, c = _position()
    chips = _other_chips(x, y)
    return pltpu.make_async_remote_copy(
        src_ref=srcs[i].at[2 * chips[j][0] + chips[j][1]], dst_ref=lands[i].at[j],
        send_sem=send_sem.at[3 * i + j], recv_sem=recv_sem.at[3 * i + j],
        device_id=(chips[j][0], chips[j][1], c), device_id_type=MESH)


def _scatter_start(name, sums):
    n = len(sums)

    def body(*refs):
        srcs, lands = refs[:n], refs[n:2 * n]
        send_sem, recv_sem = refs[2 * n], refs[2 * n + 1]
        token = refs[-1]
        for i in range(n):
            for j in range(3):
                _scatter_copy(srcs, lands, send_sem, recv_sem, i, j).start()
        token[...] = jnp.zeros_like(token)

    land_shapes = [(3,) + s.shape[1:] for s in sums]
    res = pl.pallas_call(
        body,
        name=name,
        in_specs=[HBM] * (2 * n),
        out_specs=[SEM, SEM] + [HBM] * (2 * n) + [pl.BlockSpec(memory_space=pltpu.VMEM)],
        out_shape=[pltpu.SemaphoreType.DMA((3 * n,)), pltpu.SemaphoreType.DMA((3 * n,))]
        + [pltpu.HBM(s.shape, s.dtype) for s in sums]
        + [pltpu.HBM(shp, s.dtype) for shp, s in zip(land_shapes, sums)]
        + [jax.ShapeDtypeStruct((8, LANES), F32)],
        input_output_aliases={i: 2 + i for i in range(2 * n)},
        compiler_params=pltpu.CompilerParams(has_side_effects=SPLIT_COPY),
    )(*[pltpu.with_memory_space_constraint(s, pltpu.HBM) for s in sums],
      *[pltpu.with_memory_space_constraint(lax.empty(shp, s.dtype), pltpu.HBM) for shp, s in zip(land_shapes, sums)])
    return res[0], res[1], list(res[2:2 + n]), list(res[2 + n:2 + 2 * n]), res[-1]


def _scatter_wait(name, send_sem, recv_sem, sums, lands, after):
    n = len(sums)

    def body(*refs):
        srcs, land_refs = refs[:n], refs[n:2 * n]
        send_ref, recv_ref = refs[2 * n], refs[2 * n + 1]
        for i in range(n):
            for j in range(3):
                copy = _scatter_copy(srcs, land_refs, send_ref, recv_ref, i, j)
                copy.wait_send()
                copy.wait_recv()

    res = pl.pallas_call(
        body,
        name=name,
        in_specs=[HBM] * (2 * n) + [SEM, SEM, ANY],
        out_specs=[HBM] * (2 * n),
        out_shape=[pltpu.HBM(s.shape, s.dtype) for s in sums] + [pltpu.HBM(l.shape, l.dtype) for l in lands],
        input_output_aliases={i: i for i in range(2 * n)},
        compiler_params=pltpu.CompilerParams(has_side_effects=SPLIT_COPY),
    )(*sums, *lands, send_sem, recv_sem, after)
    return list(res[:n]), list(res[n:])


def _pair_join(name, halves, small=None):
    n = len(halves)
    if small is None:
        def body_plain(*refs):
            ins, outs = refs[:n], refs[n:2 * n]
            send_sem, recv_sem = refs[2 * n:]
            x, y, c = _position()
            swaps = [pltpu.make_async_remote_copy(
                src_ref=ins[i], dst_ref=outs[i], send_sem=send_sem.at[i], recv_sem=recv_sem.at[i],
                device_id=(x, y, 1 - c), device_id_type=MESH) for i in range(n)]
            for swap in swaps:
                swap.start()
            for swap in swaps:
                swap.wait()

        return pl.pallas_call(
            body_plain,
            name=name,
            in_specs=[ANY] * n,
            out_specs=[ANY] * n,
            out_shape=[jax.ShapeDtypeStruct(h.shape, h.dtype) for h in halves],
            scratch_shapes=[pltpu.SemaphoreType.DMA((n,))] * 2,
        )(*halves)

    def body(*refs):
        ins, small_ref = refs[:n], refs[n]
        outs, all_ref = refs[n + 1:2 * n + 1], refs[2 * n + 1]
        send_sem, recv_sem, sm_send, sm_recv, sm_local = refs[2 * n + 2:]
        x, y, c = _position()
        swaps = []
        for i in range(n):
            swap = pltpu.make_async_remote_copy(
                src_ref=ins[i], dst_ref=outs[i], send_sem=send_sem.at[i], recv_sem=recv_sem.at[i],
                device_id=(x, y, 1 - c), device_id_type=MESH)
            swap.start()
            swaps.append(swap)
        me = 4 * x + 2 * y + c
        sm_own = pltpu.make_async_copy(small_ref, all_ref.at[me], sm_local)
        sm_own.start()
        pushes, arrivals = [], []
        for mask in range(1, N_DEV):
            px, py, pc = x ^ (mask >> 2), y ^ ((mask >> 1) & 1), c ^ (mask & 1)
            pushes.append(pltpu.make_async_remote_copy(
                src_ref=small_ref, dst_ref=all_ref.at[me], send_sem=sm_send.at[mask - 1], recv_sem=sm_recv.at[mask - 1],
                device_id=(px, py, pc), device_id_type=MESH))
            arrivals.append(pltpu.make_async_remote_copy(
                src_ref=small_ref, dst_ref=all_ref.at[4 * px + 2 * py + pc], send_sem=sm_send.at[mask - 1],
                recv_sem=sm_recv.at[mask - 1], device_id=(px, py, pc), device_id_type=MESH))
        for cp in pushes:
            cp.start()
        for swap in swaps:
            swap.wait()
        for cp in arrivals:
            cp.wait_recv()
        for cp in pushes:
            cp.wait_send()
        sm_own.wait()

    res = pl.pallas_call(
        body,
        name=name,
        in_specs=[ANY] * (n + 1),
        out_specs=[ANY] * (n + 1),
        out_shape=[jax.ShapeDtypeStruct(h.shape, h.dtype) for h in halves]
        + [jax.ShapeDtypeStruct((N_DEV,) + small.shape, small.dtype)],
        scratch_shapes=[pltpu.SemaphoreType.DMA((n,))] * 2 + [pltpu.SemaphoreType.DMA((N_DEV - 1,))] * 2
        + [pltpu.SemaphoreType.DMA(())],
    )(*halves, small)
    return res[:n], res[n]


def _lower_bound(lbp):
    return jax.nn.softmax(lbp, axis=0)[0:1]


def _local_step(x, target, g1, gm, g2, gq, gk, go, rel_bias, lbp, first_weights, mid_weights, last_weights, on_grads, grads_sent):
    b, s, d = x.shape
    t = b * s
    x0 = x.reshape(t, d)
    tgt = target.reshape(t, d)
    gq_t = jnp.tile(gq, (1, ATTN_HEADS))
    gk_t = jnp.tile(gk, (1, ATTN_HEADS))
    lb = _lower_bound(lbp)
    table = _band_table(_rel_bias_table("rel_bias_table", rel_bias))

    h1 = _rmsnorm_fwd("norm1", x0, g1)
    wg1, wu1, deps1 = first_weights((h1, table))
    a1, b1, z1 = _ffn_up("ffn1_up", h1, wg1, wu1, deps1)
    wd1, w_in, w_out = mid_weights((z1,))
    ns = w_in.shape[0]
    x1, h2 = _ffn_down("ffn1_down", z1, wd1, x0, gm)
    proj = _in_proj("in_proj", h2, w_in)
    proj3 = proj.reshape(b, s, proj.shape[1])
    qn, kn, vb = _qk_prep("qk_prep", proj3, gq_t, gk_t)
    attn = _attn_fwd("attn_fwd", qn, kn, vb, table).reshape(t, ATTN_W)
    mix, oraw, states = _hgrn_fwd("hgrn_fwd", proj, attn, lb, go, b, s)
    x2, h3 = _out_proj("out_proj", mix, w_out, x1, g2)
    wg2, wu2, wd2 = last_weights((h3,))
    a2, b2, z2 = _ffn_up("ffn2_up", h3, wg2, wu2)
    dy, dyh, sq = _ffn_down_loss("ffn2_down_loss", z2, wd2, x2, tgt)
    loss = 0.5 * jnp.sum(sq) / d

    da2, db2 = _ffn_bwd_act("ffn2_bwd_act", dyh, wd2, a2, b2)
    dwd2 = _grad_w_shardrows("ffn2_dwd", z2, dyh)
    dwg2 = _grad_w_shardrows("ffn2_dwg", da2, h3)
    dwu2 = _grad_w_shardrows("ffn2_dwu", db2, h3)
    sent2 = on_grads("ffn2", {"ffn2_w_gate": dwg2, "ffn2_w_up": dwu2, "ffn2_w_down": dwd2})
    dx2, dx2b, dg2 = _ffn_bwd_in("ffn2_bwd_in", da2, db2, wg2, wu2, x2, g2, dy, 1.0, sent2)
    sent2 = grads_sent("ffn2", dx2b)

    dwout = _grad_w_out("dw_out", mix, dx2b)
    dmix = _out_proj_bwd("out_proj_bwd", dx2b, w_out, sent2)
    dqn, dkn, dvn, dbe, dbo = _attn_bwd("attn_bwd", qn, kn, vb, table, dmix.reshape(b, s, dmix.shape[1]))
    dbias = dbe[:, :, :BAND] + dbo[:, :, CHUNK:]
    dpq, dpk, dpv, dgq, dgk = _qk_prep_bwd("qk_prep_bwd", proj3, dqn, dkn, dvn, gq_t, gk_t)
    dpq, dpk, dpv = (a.reshape(t, ATTN_W) for a in (dpq, dpk, dpv))
    dproj, dlb, dgo = _hgrn_bwd("hgrn_bwd", proj, (dpq, dpk, dpv), lb, go, oraw, states, dmix, b, s)
    dwin = _grad_w_in("dw_in", h2, dproj, ns)
    dx1, dx1h, dgm = _in_proj_bwd("in_proj_bwd", dproj, w_in, x1, gm, dx2, 0.5)

    dwd1 = _grad_w_shardrows("ffn1_dwd", z1, dx1h)
    sent_mix = on_grads("mix", {"w_in": dwin, "w_out": dwout.reshape(ns, dwout.shape[0] // ns, d),
                                "ffn1_w_down": dwd1})
    da1, db1 = _ffn_bwd_act("ffn1_bwd_act", dx1h, wd1, a1, b1, sent_mix)
    sent_mix = grads_sent("mix", da1)
    dwg1 = _grad_w_shardrows("ffn1_dwg", da1, h1, sent_mix)
    dwu1 = _grad_w_shardrows("ffn1_dwu", db1, h1)
    on_grads("ffn1", {"ffn1_w_gate": dwg1, "ffn1_w_up": dwu1})
    sent1 = grads_sent("ffn1", None)
    dx0, dg1 = _ffn_bwd_in("ffn1_bwd_in", da1, db1, wg1, wu1, x0, g1, dx1, None, sent1)

    nt = dg1.shape[0]
    sg = _small_grads(
        "small_grads", dg1.reshape(nt, d), dgm.reshape(nt, d), dg2.reshape(nt, d),
        dgq.reshape(-1, ATTN_W), dgk.reshape(-1, ATTN_W), dbias.transpose(1, 0, 2),
        dlb.reshape(b, HGRN_W), dgo.reshape(b, HGRN_W), lbp)
    g1g, gmg, g2g, gqg, gkg, rbg, lbg, gog = sg
    small = _pack_small(g1g, gmg, g2g, lbg, rbg[:, :N_REL], gqg, gkg, gog)
    return loss, dx0.reshape(b, s, d), small


def _pack_small(g1, gm, g2, lbp, rel_bias, gq, gk, go):
    flat = [g1.reshape(-1), gm.reshape(-1), g2.reshape(-1), lbp.reshape(-1), rel_bias.reshape(-1)]
    n_bias = 3 * SMALL_COLS - rel_bias.size
    heads = [gq.reshape(-1), gk.reshape(-1), go.reshape(-1)]
    n_tail = SMALL_COLS - sum(h.size for h in heads)
    return jnp.concatenate(flat + [jnp.zeros((n_bias,), F32)] + heads + [jnp.zeros((n_tail,), F32)]).reshape(
        SMALL_ROWS, SMALL_COLS)


def _unpack_small(p, d):
    flat = p.reshape(-1)
    o = 3 * d
    g1, gm, g2 = p[0:1], p[1:2], p[2:3]
    lbp = flat[o:o + 2 * HGRN_W].reshape(2, HGRN_W)
    o = 4 * SMALL_COLS
    rel = flat[o:o + ATTN_HEADS * N_REL].reshape(1, ATTN_HEADS, N_REL)
    o = 7 * SMALL_COLS
    gq = flat[o:o + ATTN_DH].reshape(1, ATTN_DH)
    gk = flat[o + ATTN_DH:o + 2 * ATTN_DH].reshape(1, ATTN_DH)
    go = flat[o + 2 * ATTN_DH:o + 2 * ATTN_DH + HGRN_DH].reshape(1, HGRN_DH)
    return g1, gm, g2, gq, gk, rel, lbp, go


def kernel(x, ffn1_norm_g, ffn1_w_gate, ffn1_w_up, ffn1_w_down, mix_norm_g, w_in, attn_q_norm_g, attn_k_norm_g, attn_rel_bias, hgrn_lower_bounds, hgrn_out_norm_g, w_out, ffn2_norm_g, ffn2_w_gate, ffn2_w_up, ffn2_w_down, loss_target, m_ffn1_norm_g, m_ffn1_w_gate, m_ffn1_w_up, m_ffn1_w_down, m_mix_norm_g, m_w_in, m_attn_q_norm_g, m_attn_k_norm_g, m_attn_rel_bias, m_hgrn_lower_bounds, m_hgrn_out_norm_g, m_w_out, m_ffn2_norm_g, m_ffn2_w_gate, m_ffn2_w_up, m_ffn2_w_down, v_ffn1_norm_g, v_ffn1_w_gate, v_ffn1_w_up, v_ffn1_w_down, v_mix_norm_g, v_w_in, v_attn_q_norm_g, v_attn_k_norm_g, v_attn_rel_bias, v_hgrn_lower_bounds, v_hgrn_out_norm_g, v_w_out, v_ffn2_norm_g, v_ffn2_w_gate, v_ffn2_w_up, v_ffn2_w_down):
    d = x.shape[-1]
    big_w = [ffn1_w_gate, ffn1_w_up, ffn1_w_down, w_in, w_out, ffn2_w_gate, ffn2_w_up, ffn2_w_down]
    big_m = [m_ffn1_w_gate, m_ffn1_w_up, m_ffn1_w_down, m_w_in, m_w_out, m_ffn2_w_gate, m_ffn2_w_up, m_ffn2_w_down]
    big_v = [v_ffn1_w_gate, v_ffn1_w_up, v_ffn1_w_down, v_w_in, v_w_out, v_ffn2_w_gate, v_ffn2_w_up, v_ffn2_w_down]
    big_names = ["ffn1_w_gate", "ffn1_w_up", "ffn1_w_down", "w_in", "w_out", "ffn2_w_gate", "ffn2_w_up", "ffn2_w_down"]
    flipped = {nm for nm in big_names if nm.endswith("gate") or nm.endswith("up")}
    flip = lambda nm, a: jnp.swapaxes(a, 1, 2) if nm in flipped else a
    big_w, big_m, big_v = ([flip(nm, a) for nm, a in zip(big_names, arrs)] for arrs in (big_w, big_m, big_v))

    shards = [w[0].astype(BF16) for w in big_w]
    start_a = _gather_start("gather_start_up1", shards[:2], ())
    start_b = _gather_start("gather_start_mid", shards[2:5], (start_a[4],))
    start_c = _gather_start("gather_start_ffn2", shards[5:], (start_b[4],))

    def gathered(tag, started, after):
        send_sem, recv_sem, srcs, outs, _ = started
        srcs, outs = _gather_wait("gather_wait_" + tag, send_sem, recv_sem, srcs, outs, after)
        return _gather_join("gather_join_" + tag, srcs, outs)

    def first_weights(after):
        return (*gathered("up1", start_a, after), (start_c[4],))

    def mid_weights(after):
        wd1, win_f, wout_f = gathered("mid", start_b, after)
        return wd1, win_f, wout_f.reshape(wout_f.shape[0] * wout_f.shape[1], d)

    def last_weights(after):
        return gathered("ffn2", start_c, after)

    core = lax.axis_index("c").astype(jnp.int32).reshape(1)
    chip = (2 * lax.axis_index("x") + lax.axis_index("y")).astype(jnp.int32).reshape(1)
    started = {}

    def on_grads(tag, grads):
        names = list(grads)
        started[tag] = (names, _pair_start("pair_start_" + tag, [grads[nm] for nm in names]))
        return (started[tag][1][4],)

    def grads_sent(tag, after):
        names, (send_sem, recv_sem, grads, lands, token) = started[tag]
        grads, theirs = _pair_wait("pair_wait_" + tag, send_sem, recv_sem, grads, lands, token if after is None else after)
        sums = [_pair_sum("pair_sum_" + nm, g, th, core) for nm, g, th in zip(names, grads, theirs)]
        started[tag] = (names, _scatter_start("scatter_start_" + tag, sums))
        return (started[tag][1][4],)

    loss, grad_x, small_g = _local_step(
        x, loss_target, ffn1_norm_g, mix_norm_g, ffn2_norm_g, attn_q_norm_g, attn_k_norm_g, hgrn_out_norm_g,
        attn_rel_bias[0], hgrn_lower_bounds, first_weights, mid_weights, last_weights, on_grads, grads_sent)
    loss = lax.psum(loss, ("x", "y", "c"))

    def finish(tag, after):
        names, (send_sem, recv_sem, sums, lands, _) = started[tag]
        sums, lands = _scatter_wait("scatter_wait_" + tag, send_sem, recv_sem, sums, lands, after)
        return names, [_chip_sum("chip_sum_" + nm, sm, ld, chip) for nm, sm, ld in zip(names, sums, lands)]

    by_name = {nm: (w, m, v) for nm, w, m, v in zip(big_names, big_w, big_m, big_v)}
    updated = {}

    def update(names, halves, other_halves):
        for nm, mine, theirs in zip(names, halves, other_halves):
            w, m, v = by_name[nm]
            updated[nm] = _adamw("adamw_" + nm, w, mine, theirs, m, v, core)

    last_token = started["ffn1"][1][4]
    names_a, halves_a = finish("ffn2", last_token)
    names_m, halves_m = finish("mix", last_token)
    names_a, halves_a = names_a + names_m, halves_a + halves_m
    update(names_a, halves_a, _pair_join("pair_join_early", halves_a))
    names_b, halves_b = finish("ffn1", updated[names_a[-1]][1])
    others_b, small_all = _pair_join("pair_join_last", halves_b, small_g)
    update(names_b, halves_b, others_b)
    big_out = [updated[nm] for nm in big_names]

    pack = lambda g1, gm, g2, gq, gk, rel, lbp, go: _pack_small(g1, gm, g2, lbp, rel[0], gq, gk, go)
    small_w = pack(ffn1_norm_g, mix_norm_g, ffn2_norm_g, attn_q_norm_g, attn_k_norm_g, attn_rel_bias, hgrn_lower_bounds, hgrn_out_norm_g)
    small_m = pack(m_ffn1_norm_g, m_mix_norm_g, m_ffn2_norm_g, m_attn_q_norm_g, m_attn_k_norm_g, m_attn_rel_bias, m_hgrn_lower_bounds, m_hgrn_out_norm_g)
    small_v = pack(v_ffn1_norm_g, v_mix_norm_g, v_ffn2_norm_g, v_attn_q_norm_g, v_attn_k_norm_g, v_attn_rel_bias, v_hgrn_lower_bounds, v_hgrn_out_norm_g)
    small_out = [_unpack_small(p, d) for p in _adamw_small("adamw_small", small_w, small_all, small_m, small_v)]

    def assemble(kind):
        bg = [flip(nm, o[kind]) for nm, o in zip(big_names, big_out)]
        g1, gm, g2, gq, gk, rel, lbp, go = small_out[kind]
        return [g1, bg[0], bg[1], bg[2], gm, bg[3], gq, gk, rel, lbp, go, bg[4], g2, bg[5], bg[6], bg[7]]

    return (loss, grad_x, *assemble(0), *assemble(1), *assemble(2), *assemble(3))
```

```python
import functools

import jax
import jax.numpy as jnp
from jax import lax
from jax.experimental import pallas as pl
from jax.experimental.pallas import tpu as pltpu

F32 = jnp.float32
BF16 = jnp.bfloat16
MESH = pl.DeviceIdType.MESH

N_CHIPS = 4
N_DEV = 8
CHUNK = 64
ATTN_HEADS = 8
ATTN_DH = 64
ATTN_W = ATTN_HEADS * ATTN_DH
HGRN_HEADS = 4
HGRN_DH = 128
HGRN_W = HGRN_HEADS * HGRN_DH
LEFT_CHUNKS = 8
BAND = (LEFT_CHUNKS + 1) * CHUNK
KPAD = LEFT_CHUNKS * CHUNK
REL_CLIP = 128
N_REL = 2 * REL_CLIP + 1
N_REL_PAD = 384
RMS_EPS = 1e-6
LANES = 128
SMALL_ROWS = 8
SMALL_COLS = 1024

ADAM_LR = 0.001
ADAM_B1 = 0.9
ADAM_B2 = 0.999
ADAM_EPS = 1e-08
ADAM_WD = 0.01
ADAM_STEP = 10

NN = (((1,), (0,)), ((), ()))
NT = (((1,), (1,)), ((), ()))
TN = (((0,), (0,)), ((), ()))

VMEM_LIMIT = 48 * 1024 * 1024


def _sigmoid(x):
    return 1.0 / (1.0 + jnp.exp(-x))


def _silu(x):
    return x * _sigmoid(x)


def _dot(a, b, dims=NN):
    return lax.dot_general(a, b, dims, preferred_element_type=F32)


def _split3(x):
    hi = x.astype(BF16)
    r1 = x - hi.astype(F32)
    mid = r1.astype(BF16)
    lo = (r1 - mid.astype(F32)).astype(BF16)
    return hi, mid, lo


def _dot_exact_rhs(x, mat, dims=NN, pieces=3):
    hi, mid, lo = _split3(x)
    out = _dot(hi, mat, dims) + _dot(mid, mat, dims)
    return out + _dot(lo, mat, dims) if pieces == 3 else out


def _dot_exact_lhs(mat, x, dims=NN):
    hi, mid, lo = _split3(x)
    return _dot(mat, hi, dims) + _dot(mat, mid, dims) + _dot(mat, lo, dims)


def _params(*sem):
    return pltpu.CompilerParams(dimension_semantics=sem, vmem_limit_bytes=VMEM_LIMIT)


def _mm(name, ins, terms, n_acc, grid, acc_shape, outs, epilogue, extras=(), deps=()):
    nk = grid[2]
    ni, ne, nd, no = len(ins), len(extras), len(deps), len(outs)

    def body(*refs):
        in_refs = refs[:ni]
        ex_refs = refs[ni:ni + ne]
        out_refs = refs[ni + ne + nd:ni + ne + nd + no]
        acc_refs = refs[ni + ne + nd + no:]
        parts = [None] * n_acc
        for ai, li, ri, dims in terms:
            d = _dot(in_refs[li][...], in_refs[ri][...], dims)
            parts[ai] = d if parts[ai] is None else parts[ai] + d

        def finish(accs):
            res = epilogue(accs, [e[...] for e in ex_refs])
            for o, r in zip(out_refs, res):
                o[...] = r.astype(o.dtype)

        if nk == 1:
            finish(parts)
        else:
            k = pl.program_id(2)

            @pl.when(k == 0)
            def _():
                for a, p in zip(acc_refs, parts):
                    a[...] = p

            @pl.when(k > 0)
            def _():
                for a, p in zip(acc_refs, parts):
                    a[...] += p

            @pl.when(k == nk - 1)
            def _():
                finish([a[...] for a in acc_refs])

    scratch = [] if nk == 1 else [pltpu.VMEM(acc_shape, F32) for _ in range(n_acc)]
    res = pl.pallas_call(
        body,
        name=name,
        grid=grid,
        in_specs=[s for _, s in ins] + [s for _, s in extras] + [pl.BlockSpec(memory_space=pl.ANY)] * nd,
        out_specs=[s for _, s in outs],
        out_shape=[o for o, _ in outs],
        scratch_shapes=scratch,
        compiler_params=_params("parallel", "parallel", "arbitrary"),
    )(*[a for a, _ in ins], *[a for a, _ in extras], *deps)
    return res


def _mm_rows(name, lhs, weights, dims, t, outs, epilogue, extras=(), deps=()):
    tm = _row_tile(t)
    nl, ne, nd, no = len(lhs), len(extras), len(deps), len(outs)
    ns = weights[0].shape[0]

    def body(*refs):
        lhs_refs = refs[:nl]
        w_hbm = refs[nl:2 * nl]
        ex_refs = refs[2 * nl:2 * nl + ne]
        out_refs = refs[2 * nl + ne + nd:2 * nl + ne + nd + no]
        w_vmem = refs[2 * nl + ne + nd + no:3 * nl + ne + nd + no]
        sem = refs[-1]

        @pl.when(pl.program_id(0) == 0)
        def _():
            copies = [pltpu.make_async_copy(w_hbm[p], w_vmem[p], sem.at[p]) for p in range(nl)]
            for cp in copies:
                cp.start()
            for cp in copies:
                cp.wait()

        acc = None
        for p in range(nl):
            pick = lhs[p][2]
            for j in range(ns):
                part = _dot(pick(lhs_refs[p], j), w_vmem[p][j], dims)
                acc = part if acc is None else acc + part
        res = epilogue([acc], [e[...] for e in ex_refs])
        for o, r in zip(out_refs, res):
            o[...] = r.astype(o.dtype)

    return pl.pallas_call(
        body,
        name=name,
        grid=(t // tm,),
        in_specs=[s for _, s, _ in lhs] + [pl.BlockSpec(memory_space=pl.ANY)] * nl + [s for _, s in extras]
        + [pl.BlockSpec(memory_space=pl.ANY)] * nd,
        out_specs=[s for _, s in outs],
        out_shape=[o for o, _ in outs],
        scratch_shapes=[pltpu.VMEM(w.shape, w.dtype) for w in weights] + [pltpu.SemaphoreType.DMA((nl,))],
        compiler_params=_params("arbitrary"),
    )(*[a for a, _, _ in lhs], *weights, *[a for a, _ in extras], *deps)


def _mm_shards(name, x, weights, dims, outs, epilogue, extras=(), deps=()):
    t = x.shape[0]
    tm = _row_tile(t)
    nw, ne, nd, no = len(weights), len(extras), len(deps), len(outs)
    ns = weights[0].shape[0]

    def body(*refs):
        x_ref = refs[0]
        w_hbm = refs[1:1 + nw]
        ex_refs = refs[1 + nw:1 + nw + ne]
        out_refs = refs[1 + nw + ne + nd:1 + nw + ne + nd + no]
        w_vmem = refs[1 + nw + ne + nd + no:1 + 2 * nw + ne + nd + no]
        sem = refs[-1]

        @pl.when(pl.program_id(0) == 0)
        def _():
            copies = [pltpu.make_async_copy(w_hbm[p], w_vmem[p], sem.at[p]) for p in range(nw)]
            for cp in copies:
                cp.start()
            for cp in copies:
                cp.wait()

        xv = x_ref[...]
        accs = [_dot(xv, w_vmem[p][0], dims) for p in range(nw)]
        for j in range(ns):
            nxt = [_dot(xv, w_vmem[p][j + 1], dims) for p in range(nw)] if j + 1 < ns else None
            res = epilogue(accs, [e[j] for e in ex_refs])
            for (_, _, store), o, r in zip(outs, out_refs, res):
                store(o, j, r.astype(o.dtype))
            accs = nxt

    return pl.pallas_call(
        body,
        name=name,
        grid=(t // tm,),
        in_specs=[pl.BlockSpec((tm, x.shape[1]), lambda i: (i, 0))] + [pl.BlockSpec(memory_space=pl.ANY)] * nw
        + [s for _, s in extras] + [pl.BlockSpec(memory_space=pl.ANY)] * nd,
        out_specs=[s for _, s, _ in outs],
        out_shape=[o for o, _, _ in outs],
        scratch_shapes=[pltpu.VMEM(w.shape, w.dtype) for w in weights] + [pltpu.SemaphoreType.DMA((nw,))],
        compiler_params=_params("arbitrary"),
    )(x, *weights, *[a for a, _ in extras], *deps)


def _store_shard(ref, j, value):
    ref[j] = value


def _row_tile(t):
    return 512 if t % 512 == 0 else t


def _k_tile(t):
    return t if t <= 4096 else 1024


def _rmsnorm(xv, g):
    ms = jnp.mean(xv * xv, axis=-1, keepdims=True)
    return xv * lax.rsqrt(ms + RMS_EPS) * g


def _rmsnorm_fwd(name, x, g):
    t, d = x.shape
    tm = _row_tile(t)

    def body(x_ref, g_ref, h_ref):
        h_ref[...] = _rmsnorm(x_ref[...], g_ref[...]).astype(BF16)

    return pl.pallas_call(
        body,
        name=name,
        grid=(t // tm,),
        in_specs=[pl.BlockSpec((tm, d), lambda i: (i, 0)), pl.BlockSpec((1, d), lambda i: (0, 0))],
        out_specs=pl.BlockSpec((tm, d), lambda i: (i, 0)),
        out_shape=jax.ShapeDtypeStruct((t, d), BF16),
        compiler_params=_params("parallel"),
    )(x, g)


def _norm_bwd_epilogue(copy_scale):
    def epilogue(accs, ex):
        dh = accs[0]
        xv, g, dres = ex
        ms = jnp.mean(xv * xv, axis=-1, keepdims=True)
        rstd = lax.rsqrt(ms + RMS_EPS)
        xhat = xv * rstd
        dxhat = dh * g
        dx = rstd * (dxhat - xhat * jnp.mean(dxhat * xhat, axis=-1, keepdims=True))
        out = dres + dx
        dg = jnp.sum(dh * xhat, axis=0, keepdims=True)
        if copy_scale is None:
            return out, dg
        return out, out * copy_scale, dg

    return epilogue


def _ffn_up(name, h, wg, wu, deps=()):
    t, d = h.shape
    ns, f, _ = wg.shape
    tm = _row_tile(t)

    def epilogue(accs, ex):
        a, b = accs
        sg = _sigmoid(a)
        act = a * sg
        return act, b * (sg * (1.0 + a * (1.0 - sg))), act * b

    out = (jax.ShapeDtypeStruct((ns, t, f), BF16), pl.BlockSpec((ns, tm, f), lambda i: (0, i, 0)), _store_shard)
    return _mm_shards(name, h, [wg, wu], NT, [out] * 3, epilogue, deps=deps)


def _shard_rows(arr, tm):
    ns, _, f = arr.shape
    return arr, pl.BlockSpec((ns, tm, f), lambda i: (0, i, 0)), lambda ref, j: ref[j]


def _ffn_down(name, z, wd, x, g_next):
    _, t, _ = z.shape
    d = wd.shape[2]
    tm = _row_tile(t)
    row = pl.BlockSpec((tm, d), lambda i: (i, 0))

    def epilogue(accs, ex):
        y = ex[0] + 0.5 * accs[0]
        return y, _rmsnorm(y, ex[1])

    return _mm_rows(
        name, [_shard_rows(z, tm)], [wd], NN, t,
        outs=[(jax.ShapeDtypeStruct((t, d), F32), row), (jax.ShapeDtypeStruct((t, d), BF16), row)],
        epilogue=epilogue,
        extras=[(x, row), (g_next, pl.BlockSpec((1, d), lambda i: (0, 0)))],
    )


def _ffn_down_loss(name, z, wd, x, target):
    _, t, _ = z.shape
    d = wd.shape[2]
    tm = _row_tile(t)
    nt = t // tm
    row = pl.BlockSpec((tm, d), lambda i: (i, 0))

    def epilogue(accs, ex):
        e = ex[0] + 0.5 * accs[0] - ex[1]
        dy = e * (1.0 / d)
        return dy, 0.5 * dy, jnp.sum(e * e, axis=0, keepdims=True)

    return _mm_rows(
        name, [_shard_rows(z, tm)], [wd], NN, t,
        outs=[(jax.ShapeDtypeStruct((t, d), F32), row), (jax.ShapeDtypeStruct((t, d), BF16), row),
              (jax.ShapeDtypeStruct((nt, 1, d), F32), pl.BlockSpec((None, 1, d), lambda i: (i, 0, 0)))],
        epilogue=epilogue,
        extras=[(x, row), (target, row)],
    )


def _ffn_bwd_act(name, dout, wd, act_a, dact_b, deps=()):
    t, d = dout.shape
    ns, f, _ = wd.shape
    tm = _row_tile(t)

    def epilogue(accs, ex):
        dz = accs[0]
        return dz * ex[1].astype(F32), dz * ex[0].astype(F32)

    act = pl.BlockSpec((ns, tm, f), lambda i: (0, i, 0))
    out = (jax.ShapeDtypeStruct((ns, t, f), BF16), act, _store_shard)
    return _mm_shards(name, dout, [wd], NT, [out] * 2, epilogue, extras=[(act_a, act), (dact_b, act)], deps=deps)


def _grad_w_shardrows(name, z, dout, deps=()):
    ns, t, f = z.shape
    d = dout.shape[1]
    tk = _k_tile(t)
    return _mm(
        name,
        ins=[(z, pl.BlockSpec((None, tk, f), lambda j, n, k: (j, k, 0))),
             (dout, pl.BlockSpec((tk, d), lambda j, n, k: (k, 0)))],
        terms=[(0, 0, 1, TN)],
        n_acc=1,
        grid=(ns, 1, t // tk),
        acc_shape=(f, d),
        outs=[(jax.ShapeDtypeStruct((ns, f, d), BF16), pl.BlockSpec((None, f, d), lambda j, n, k: (j, 0, 0)))],
        epilogue=lambda accs, ex: (accs[0],),
        deps=deps,
    )[0]


def _norm_bwd_outs(t, d, tm, copy_scale):
    row = pl.BlockSpec((tm, d), lambda i: (i, 0))
    outs = [(jax.ShapeDtypeStruct((t, d), F32), row)]
    if copy_scale is not None:
        outs.append((jax.ShapeDtypeStruct((t, d), BF16), row))
    outs.append((jax.ShapeDtypeStruct((t // tm, 1, d), F32), pl.BlockSpec((None, 1, d), lambda i: (i, 0, 0))))
    return row, outs


def _ffn_bwd_in(name, da, db, wg, wu, x, g, dres, copy_scale, deps=()):
    _, t, _ = da.shape
    d = wg.shape[2]
    tm = _row_tile(t)
    row, outs = _norm_bwd_outs(t, d, tm, copy_scale)
    return _mm_rows(
        name, [_shard_rows(da, tm), _shard_rows(db, tm)], [wg, wu], NN, t,
        outs=outs,
        epilogue=_norm_bwd_epilogue(copy_scale),
        extras=[(x, row), (g, pl.BlockSpec((1, d), lambda i: (0, 0))), (dres, row)],
        deps=deps,
    )


def _in_proj(name, h, w_in):
    t, d = h.shape
    ns, _, pj = w_in.shape
    tm = _row_tile(t)
    def store(ref, j, value):
        ref[:, j * pj:(j + 1) * pj] = value

    out = (jax.ShapeDtypeStruct((t, ns * pj), F32), pl.BlockSpec((tm, ns * pj), lambda i: (i, 0)), store)
    return _mm_shards(name, h, [w_in], NN, [out], lambda accs, ex: (accs[0],))[0]


def _in_proj_bwd(name, dp, w_in, x, g, dres, copy_scale, deps=()):
    t = dp.shape[0]
    ns, d, pj = w_in.shape
    tm = _row_tile(t)
    row, outs = _norm_bwd_outs(t, d, tm, copy_scale)
    cols = (dp, pl.BlockSpec((tm, ns * pj), lambda i: (i, 0)), lambda ref, j: ref[:, j * pj:(j + 1) * pj])
    return _mm_rows(
        name, [cols], [w_in], NT, t,
        outs=outs,
        epilogue=_norm_bwd_epilogue(copy_scale),
        extras=[(x, row), (g, pl.BlockSpec((1, d), lambda i: (0, 0))), (dres, row)],
        deps=deps,
    )


def _grad_w_in(name, h, dp, ns):
    t, d = h.shape
    pj = dp.shape[1] // ns
    tk = _k_tile(t)
    return _mm(
        name,
        ins=[(h, pl.BlockSpec((tk, d), lambda j, n, k: (k, 0))),
             (dp, pl.BlockSpec((tk, pj), lambda j, n, k: (k, j)))],
        terms=[(0, 0, 1, TN)],
        n_acc=1,
        grid=(ns, 1, t // tk),
        acc_shape=(d, pj),
        outs=[(jax.ShapeDtypeStruct((ns, d, pj), BF16), pl.BlockSpec((None, d, pj), lambda j, n, k: (j, 0, 0)))],
        epilogue=lambda accs, ex: (accs[0],),
    )[0]


def _out_proj(name, mix, w_out, x, g_next):
    t, dm = mix.shape
    d = w_out.shape[1]
    tm = _row_tile(t)
    row = pl.BlockSpec((tm, d), lambda i, n, k: (i, 0))
    return _mm(
        name,
        ins=[(mix, pl.BlockSpec((tm, dm), lambda i, n, k: (i, 0))),
             (w_out, pl.BlockSpec((dm, d), lambda i, n, k: (0, 0)))],
        terms=[(0, 0, 1, NN)],
        n_acc=1,
        grid=(t // tm, 1, 1),
        acc_shape=(tm, d),
        outs=[(jax.ShapeDtypeStruct((t, d), F32), row), (jax.ShapeDtypeStruct((t, d), BF16), row)],
        epilogue=lambda accs, ex: (ex[0] + accs[0], _rmsnorm(ex[0] + accs[0], ex[1])),
        extras=[(x, row), (g_next, pl.BlockSpec((1, d), lambda i, n, k: (0, 0)))],
    )


def _out_proj_bwd(name, dx, w_out, deps=()):
    t, d = dx.shape
    dm = w_out.shape[0]
    tm = _row_tile(t)
    return _mm(
        name,
        ins=[(dx, pl.BlockSpec((tm, d), lambda i, n, k: (i, 0))),
             (w_out, pl.BlockSpec((dm, d), lambda i, n, k: (0, 0)))],
        terms=[(0, 0, 1, NT)],
        n_acc=1,
        grid=(t // tm, 1, 1),
        acc_shape=(tm, dm),
        outs=[(jax.ShapeDtypeStruct((t, dm), F32), pl.BlockSpec((tm, dm), lambda i, n, k: (i, 0)))],
        epilogue=lambda accs, ex: (accs[0],),
        deps=deps,
    )[0]


def _grad_w_out(name, mix, dx):
    t, dm = mix.shape
    d = dx.shape[1]
    tk = _k_tile(t)
    return _mm(
        name,
        ins=[(mix, pl.BlockSpec((tk, dm), lambda a, n, k: (k, 0))),
             (dx, pl.BlockSpec((tk, d), lambda a, n, k: (k, 0)))],
        terms=[(0, 0, 1, TN)],
        n_acc=1,
        grid=(1, 1, t // tk),
        acc_shape=(dm, d),
        outs=[(jax.ShapeDtypeStruct((dm, d), BF16), pl.BlockSpec((dm, d), lambda a, n, k: (0, 0)))],
        epilogue=lambda accs, ex: (accs[0],),
    )[0]


def _head_group_matrix():
    r = lax.broadcasted_iota(jnp.int32, (ATTN_W, ATTN_W), 0)
    c = lax.broadcasted_iota(jnp.int32, (ATTN_W, ATTN_W), 1)
    same = jnp.right_shift(r, 6) == jnp.right_shift(c, 6)
    return jnp.where(same, 1.0, 0.0).astype(BF16)


def _qk_prep(name, proj, gq, gk):
    b, s, _ = proj.shape
    tm = KPAD
    nb = s // tm

    def body(q_ref, k_ref, v_ref, gq_ref, gk_ref, qn_ref, kn_ref, vb_ref):
        j = pl.program_id(1)
        bd = _head_group_matrix()

        def norm(xv, g):
            ms = _dot_exact_rhs(xv * xv, bd, pieces=2) * (1.0 / ATTN_DH)
            return xv * lax.rsqrt(ms + RMS_EPS) * g

        @pl.when(j == 0)
        def _():
            kn_ref[...] = jnp.zeros_like(kn_ref)
            vb_ref[...] = jnp.zeros_like(vb_ref)

        @pl.when(j > 0)
        def _():
            qn_ref[...] = norm(q_ref[...], gq_ref[...]).astype(BF16)
            kn_ref[...] = norm(k_ref[...], gk_ref[...]).astype(BF16)
            vb_ref[...] = v_ref[...].astype(BF16)

    src_blk = lambda col: pl.BlockSpec((None, tm, ATTN_W), lambda bi, j: (bi, jnp.maximum(j - 1, 0), col))
    gspec = pl.BlockSpec((1, ATTN_W), lambda bi, j: (0, 0))
    padded = pl.BlockSpec((None, tm, ATTN_W), lambda bi, j: (bi, j, 0))
    return pl.pallas_call(
        body,
        name=name,
        grid=(b, nb + 1),
        in_specs=[src_blk(0), src_blk(1), src_blk(2), gspec, gspec],
        out_specs=[src_blk(0), padded, padded],
        out_shape=[jax.ShapeDtypeStruct((b, s, ATTN_W), BF16), jax.ShapeDtypeStruct((b, KPAD + s, ATTN_W), BF16),
                   jax.ShapeDtypeStruct((b, KPAD + s, ATTN_W), BF16)],
        compiler_params=_params("parallel", "arbitrary"),
    )(proj, proj, proj, gq, gk)


def _qk_prep_bwd(name, proj, dqn, dkn, dv, gq, gk):
    b, s, _ = proj.shape
    tm = KPAD
    nb = s // tm

    def body(q_ref, k_ref, dqn_ref, dkn_ref, dv_ref, gq_ref, gk_ref, dq_ref, dk_ref, dvb_ref, dgq_ref, dgk_ref):
        bd = _head_group_matrix()

        def bwd(xv, dy, g):
            ms = _dot_exact_rhs(xv * xv, bd, pieces=2) * (1.0 / ATTN_DH)
            rstd = lax.rsqrt(ms + RMS_EPS)
            xhat = xv * rstd
            dxhat = dy * g
            gm = _dot_exact_rhs(dxhat * xhat, bd, pieces=2) * (1.0 / ATTN_DH)
            return rstd * (dxhat - xhat * gm), jnp.sum(dy * xhat, axis=0, keepdims=True)

        dq, dgq = bwd(q_ref[...], dqn_ref[...], gq_ref[...])
        dk, dgk = bwd(k_ref[...], dkn_ref[...], gk_ref[...])
        dq_ref[...] = dq.astype(BF16)
        dk_ref[...] = dk.astype(BF16)
        dvb_ref[...] = dv_ref[...].astype(BF16)
        dgq_ref[...] = dgq
        dgk_ref[...] = dgk

    col = lambda c: pl.BlockSpec((None, tm, ATTN_W), lambda bi, j: (bi, j, c))
    past_pad = pl.BlockSpec((None, tm, ATTN_W), lambda bi, j: (bi, j + 1, 0))
    gspec = pl.BlockSpec((1, ATTN_W), lambda bi, j: (0, 0))
    pspec = pl.BlockSpec((None, 1, ATTN_W), lambda bi, j: (bi * nb + j, 0, 0))
    o_shape = jax.ShapeDtypeStruct((b, s, ATTN_W), BF16)
    p_shape = jax.ShapeDtypeStruct((b * nb, 1, ATTN_W), F32)
    return pl.pallas_call(
        body,
        name=name,
        grid=(b, nb),
        in_specs=[col(0), col(1), col(0), past_pad, past_pad, gspec, gspec],
        out_specs=[col(0)] * 3 + [pspec] * 2,
        out_shape=[o_shape] * 3 + [p_shape] * 2,
        compiler_params=_params("parallel", "parallel"),
    )(proj, proj, dqn, dkn, dv, gq, gk)


Q_CHUNKS = 4
QBLK = Q_CHUNKS * CHUNK
WIN = (LEFT_CHUNKS + Q_CHUNKS) * CHUNK
DB_W = BAND + CHUNK
MASKED = -1e30


def _band_table(bias):
    rows = [jnp.pad(bias, ((0, 0), (0, 0), (CHUNK * i, WIN - BAND - CHUNK * i)), constant_values=MASKED)
            for i in range(Q_CHUNKS)]
    return jnp.concatenate(rows, axis=1)


def _head_lanes(hh):
    lane = lax.broadcasted_iota(jnp.int32, (1, LANES), 1)
    return (lane < ATTN_DH) if hh == 0 else (lane >= ATTN_DH)


def _attn_probs(qh, kw, table, start):
    s = _dot(qh, kw, NT) * (ATTN_DH ** -0.5) + table
    col = lax.broadcasted_iota(jnp.int32, (QBLK, WIN), 1)
    s = jnp.where(col + start >= KPAD, s, MASKED)
    m = jnp.max(s, axis=-1, keepdims=True)
    p = jnp.exp(s - m)
    return p * (1.0 / jnp.sum(p, axis=-1, keepdims=True))


def _attn_fwd(name, q, k, v, table):
    b, s, w = q.shape
    sp = k.shape[1]

    def body(q_ref, k_ref, v_ref, t_ref, o_ref):
        start = pl.multiple_of(pl.program_id(2) * QBLK, QBLK)
        kw = k_ref[pl.ds(start, WIN), :]
        vw = v_ref[pl.ds(start, WIN), :]
        q2 = q_ref[...]
        lanes = [_head_lanes(hh) for hh in range(2)]
        probs = [_attn_probs(jnp.where(mine, q2, jnp.zeros_like(q2)), kw, t_ref[hh], start).astype(BF16)
                 for hh, mine in enumerate(lanes)]
        outs = [_dot(p, vw) for p in probs]
        o_ref[...] = jnp.where(lanes[0], outs[0], outs[1]).astype(BF16)

    qspec = pl.BlockSpec((None, QBLK, LANES), lambda p, bi, i: (bi, i, p))
    kspec = pl.BlockSpec((None, sp, LANES), lambda p, bi, i: (bi, 0, p))
    return pl.pallas_call(
        body,
        name=name,
        grid=(w // LANES, b, s // QBLK),
        in_specs=[qspec, kspec, kspec, pl.BlockSpec((2, QBLK, WIN), lambda p, bi, i: (p, 0, 0))],
        out_specs=qspec,
        out_shape=jax.ShapeDtypeStruct((b, s, w), BF16),
        compiler_params=_params("parallel", "parallel", "arbitrary"),
    )(q, k, v, table)


def _attn_bwd(name, q, k, v, table, dmix):
    b, s, w = q.shape
    sp = k.shape[1]

    def body(q_ref, k_ref, v_ref, t_ref, do_ref, dq_ref, dk_ref, dv_ref, dbe_ref, dbo_ref):
        bi = pl.program_id(1)
        i = pl.program_id(2)
        start = pl.multiple_of(i * QBLK, QBLK)
        win = pl.ds(start, WIN)

        @pl.when(i == 0)
        def _():
            dk_ref[...] = jnp.zeros_like(dk_ref)
            dv_ref[...] = jnp.zeros_like(dv_ref)

        @pl.when(jnp.logical_and(i == 0, bi == 0))
        def _():
            dbe_ref[...] = jnp.zeros_like(dbe_ref)
            dbo_ref[...] = jnp.zeros_like(dbo_ref)

        kw = k_ref[win, :]
        vw = v_ref[win, :]
        q2 = q_ref[...]
        do2 = do_ref[...].astype(BF16)
        lanes = [_head_lanes(hh) for hh in range(2)]
        qh = [jnp.where(mine, q2, jnp.zeros_like(q2)) for mine in lanes]
        doh = [jnp.where(mine, do2, jnp.zeros_like(do2)) for mine in lanes]
        p = [_attn_probs(qh[hh], kw, t_ref[hh], start) for hh in range(2)]
        dp = [_dot(doh[hh], vw, NT) for hh in range(2)]
        ds = [p[hh] * (dp[hh] - jnp.sum(p[hh] * dp[hh], axis=-1, keepdims=True)) for hh in range(2)]
        dsb = [(x * (ATTN_DH ** -0.5)).astype(BF16) for x in ds]
        pb = [x.astype(BF16) for x in p]
        dq = [_dot(dsb[hh], kw) for hh in range(2)]
        dk = [_dot(dsb[hh], qh[hh], TN) for hh in range(2)]
        dv = [_dot(pb[hh], doh[hh], TN) for hh in range(2)]
        for hh in range(2):
            for qi in range(Q_CHUNKS):
                c0 = (qi // 2) * LANES
                blk = ds[hh][qi * CHUNK:(qi + 1) * CHUNK, c0:c0 + DB_W]
                if qi % 2 == 0:
                    dbe_ref[hh] += blk
                else:
                    dbo_ref[hh] += blk
        dq_ref[...] = jnp.where(lanes[0], dq[0], dq[1])
        dk_ref[win, :] += dk[0] + dk[1]
        dv_ref[win, :] += dv[0] + dv[1]

    qspec = pl.BlockSpec((None, QBLK, LANES), lambda p, bi, i: (bi, i, p))
    kspec = pl.BlockSpec((None, sp, LANES), lambda p, bi, i: (bi, 0, p))
    dbspec = pl.BlockSpec((2, CHUNK, DB_W), lambda p, bi, i: (p, 0, 0))
    db_shape = jax.ShapeDtypeStruct((ATTN_HEADS, CHUNK, DB_W), F32)
    return pl.pallas_call(
        body,
        name=name,
        grid=(w // LANES, b, s // QBLK),
        in_specs=[qspec, kspec, kspec, pl.BlockSpec((2, QBLK, WIN), lambda p, bi, i: (p, 0, 0)), qspec],
        out_specs=[qspec, kspec, kspec, dbspec, dbspec],
        out_shape=[jax.ShapeDtypeStruct((b, s, w), F32), jax.ShapeDtypeStruct((b, sp, w), F32),
                   jax.ShapeDtypeStruct((b, sp, w), F32), db_shape, db_shape],
        compiler_params=_params("arbitrary", "arbitrary", "arbitrary"),
    )(q, k, v, table, dmix)


HQ_COL = 3 * ATTN_W // HGRN_DH
HF_COL = HQ_COL + HGRN_HEADS
HI_COL = HF_COL + HGRN_HEADS
HG_COL = HI_COL + HGRN_HEADS
HGRN_ROWS = 8 * CHUNK
HEAD_LANES = [slice(hh * HGRN_DH, (hh + 1) * HGRN_DH) for hh in range(HGRN_HEADS)]


def _tri(lower):
    r = lax.broadcasted_iota(jnp.int32, (CHUNK, CHUNK), 0)
    c = lax.broadcasted_iota(jnp.int32, (CHUNK, CHUNK), 1)
    return (r >= c) if lower else (r <= c)


def _hgrn_chunk(hq, hf, lb, tril):
    sig = _sigmoid(hf)
    f = lb + (1.0 - lb) * sig
    g = jnp.log(f)
    ones_l = jnp.where(tril, 1.0, 0.0).astype(BF16)
    b = _dot_exact_lhs(ones_l, g)
    bl = jnp.sum(g, axis=0, keepdims=True)
    rows = lax.broadcasted_iota(jnp.int32, g.shape, 0)
    bm = jnp.sum(jnp.where(rows <= CHUNK // 2, g, 0.0), axis=0, keepdims=True)
    sq = _sigmoid(hq)
    q = hq * sq
    k = 1.0 - f
    return sig, f, b, bl, bm, sq, q, k


def _hgrn_fwd(name, proj, attn, lb, go, b, s):
    nc = s // CHUNK
    t = b * s
    nblk = s // HGRN_ROWS
    cpb = HGRN_ROWS // CHUNK

    def body(hq_ref, hf_ref, hi_ref, hg_ref, attn_ref, lb_ref, go_ref, mix_ref, oraw_ref, st_ref, s_scr):
        tril = _tri(True)
        gov = go_ref[...]
        mix_ref[:, 0:ATTN_W] = attn_ref[...]

        @pl.when(pl.program_id(1) == 0)
        def _():
            s_scr[...] = jnp.zeros_like(s_scr)

        def step(c, carry):
            sl = pl.ds(pl.multiple_of(c * CHUNK, CHUNK), CHUNK)
            hg = hg_ref[sl, :]
            _, _, bb, bl, bm, _, q, k = _hgrn_chunk(hq_ref[sl, :], hf_ref[sl, :], lb_ref[...], tril)
            vb = hi_ref[sl, :].astype(BF16)
            qe = (q * jnp.exp(bb - bm)).astype(BF16)
            ke = (k * jnp.exp(bm - bb)).astype(BF16)
            qb = (q * jnp.exp(bb)).astype(BF16)
            kb = (k * jnp.exp(bl - bb)).astype(BF16)
            e_last = jnp.exp(bl)
            gate = _silu(hg)
            st = [s_scr[hh] for hh in range(HGRN_HEADS)]
            a = [jnp.where(tril, _dot(qe[:, hs], ke[:, hs], NT), 0.0).astype(BF16) for hs in HEAD_LANES]
            o_state = [_dot(qb[:, hs], st[hh].astype(BF16), NT) for hh, hs in enumerate(HEAD_LANES)]
            st_next = [st[hh] * e_last[:, hs] + _dot(vb[:, hs], kb[:, hs], TN) for hh, hs in enumerate(HEAD_LANES)]
            o = [_dot(a[hh], vb[:, hs]) + o_state[hh] for hh, hs in enumerate(HEAD_LANES)]
            ro = [(oh * lax.rsqrt(jnp.mean(oh * oh, axis=-1, keepdims=True) + RMS_EPS) * gov) * gate[:, hs]
                  for oh, hs in zip(o, HEAD_LANES)]
            for hh in range(HGRN_HEADS):
                st_ref[hh, c] = st[hh]
                s_scr[hh] = st_next[hh]
            mix_ref[sl, ATTN_W:ATTN_W + HGRN_W] = jnp.concatenate(ro, axis=1).astype(BF16)
            oraw_ref[sl, :] = jnp.concatenate(o, axis=1)
            return carry

        lax.fori_loop(0, cpb, step, 0)

    col = lambda base: pl.BlockSpec((HGRN_ROWS, HGRN_W), lambda bi, i: (bi * nblk + i, base // HGRN_HEADS))
    out = pl.BlockSpec((HGRN_ROWS, HGRN_W), lambda bi, i: (bi * nblk + i, 0))
    return pl.pallas_call(
        body,
        name=name,
        grid=(b, nblk),
        in_specs=[col(HQ_COL), col(HF_COL), col(HI_COL), col(HG_COL), out,
                  pl.BlockSpec((1, HGRN_W), lambda bi, i: (0, 0)), pl.BlockSpec((1, HGRN_DH), lambda bi, i: (0, 0))],
        out_specs=[pl.BlockSpec((HGRN_ROWS, ATTN_W + HGRN_W), lambda bi, i: (bi * nblk + i, 0)), out,
                   pl.BlockSpec((None, HGRN_HEADS, cpb, HGRN_DH, HGRN_DH), lambda bi, i: (bi, 0, i, 0, 0))],
        out_shape=[jax.ShapeDtypeStruct((t, ATTN_W + HGRN_W), BF16), jax.ShapeDtypeStruct((t, HGRN_W), F32),
                   jax.ShapeDtypeStruct((b, HGRN_HEADS, nc, HGRN_DH, HGRN_DH), F32)],
        scratch_shapes=[pltpu.VMEM((HGRN_HEADS, HGRN_DH, HGRN_DH), F32)],
        compiler_params=_params("parallel", "arbitrary"),
    )(proj, proj, proj, proj, attn, lb, go)


def _hgrn_bwd(name, proj, dqkv, lb, go, oraw, states, dmix, b, s):
    t = b * s
    nblk = s // HGRN_ROWS
    cpb = HGRN_ROWS // CHUNK

    def body(hq_ref, hf_ref, hi_ref, hg_ref, dq_ref, dk_ref, dv_ref, lb_ref, go_ref, oraw_ref, st_ref, dro_ref,
             dp_ref, dlb_ref, dgo_ref, ds_scr, dlb_scr, dgo_scr):
        tril = _tri(True)
        ones_u = jnp.where(_tri(False), 1.0, 0.0).astype(BF16)
        gov = go_ref[...]
        dp_ref[:, 0:ATTN_W] = dq_ref[...]
        dp_ref[:, ATTN_W:2 * ATTN_W] = dk_ref[...]
        dp_ref[:, 2 * ATTN_W:3 * ATTN_W] = dv_ref[...]

        @pl.when(pl.program_id(1) == 0)
        def _():
            ds_scr[...] = jnp.zeros_like(ds_scr)
            dlb_scr[...] = jnp.zeros_like(dlb_scr)
            dgo_scr[...] = jnp.zeros_like(dgo_scr)

        def step(ci, carry):
            c = cpb - 1 - ci
            sl = pl.ds(pl.multiple_of(c * CHUNK, CHUNK), CHUNK)
            hq = hq_ref[sl, :]
            hg = hg_ref[sl, :]
            sig, f, bb, bl, bm, sq, q, k = _hgrn_chunk(hq, hf_ref[sl, :], lb_ref[...], tril)
            vb = hi_ref[sl, :].astype(BF16)
            ebm = jnp.exp(bb - bm)
            embm = jnp.exp(bm - bb)
            eb = jnp.exp(bb)
            ebl = jnp.exp(bl - bb)
            e_last = jnp.exp(bl)
            qe = (q * ebm).astype(BF16)
            ke = (k * embm).astype(BF16)
            qb = (q * eb).astype(BF16)
            kb = (k * ebl).astype(BF16)
            st = [st_ref[hh, c] for hh in range(HGRN_HEADS)]
            dst = [ds_scr[hh] for hh in range(HGRN_HEADS)]
            o = oraw_ref[sl, :]
            dro = dro_ref[sl, :]
            sg = _sigmoid(hg)
            gov4 = jnp.concatenate([gov] * HGRN_HEADS, axis=1)
            rstd = jnp.concatenate(
                [jnp.broadcast_to(lax.rsqrt(jnp.mean(o[:, hs] * o[:, hs], axis=-1, keepdims=True) + RMS_EPS),
                                  (CHUNK, HGRN_DH)) for hs in HEAD_LANES], axis=1)
            ohat = o * rstd
            dn = dro * (hg * sg)
            dhg = dro * (ohat * gov4) * (sg * (1.0 + hg * (1.0 - sg)))
            dgo_inc = jnp.sum(dn * ohat, axis=0, keepdims=True)
            dohat = dn * gov4
            proj_h = dohat * ohat
            pm = jnp.concatenate(
                [jnp.broadcast_to(jnp.mean(proj_h[:, hs], axis=-1, keepdims=True), (CHUNK, HGRN_DH))
                 for hs in HEAD_LANES], axis=1)
            dob = (rstd * (dohat - ohat * pm)).astype(BF16)
            stb = [x.astype(BF16) for x in st]
            dstb = [x.astype(BF16) for x in dst]
            a = [jnp.where(tril, _dot(qe[:, hs], ke[:, hs], NT), 0.0).astype(BF16) for hs in HEAD_LANES]
            dab = [jnp.where(tril, _dot(dob[:, hs], vb[:, hs], NT), 0.0).astype(BF16) for hs in HEAD_LANES]
            dqb = [_dot(dob[:, hs], stb[hh]) for hh, hs in enumerate(HEAD_LANES)]
            dkb = [_dot(vb[:, hs], dstb[hh]) for hh, hs in enumerate(HEAD_LANES)]
            dv_state = [_dot(kb[:, hs], dstb[hh], NT) for hh, hs in enumerate(HEAD_LANES)]
            dst_next = [dst[hh] * e_last[:, hs] + _dot(dob[:, hs], qb[:, hs], TN) for hh, hs in enumerate(HEAD_LANES)]
            dv = [_dot(a[hh], dob[:, hs], TN) + dv_state[hh] for hh, hs in enumerate(HEAD_LANES)]
            dqe = jnp.concatenate([_dot(dab[hh], ke[:, hs]) for hh, hs in enumerate(HEAD_LANES)], axis=1)
            dke = jnp.concatenate([_dot(dab[hh], qe[:, hs], TN) for hh, hs in enumerate(HEAD_LANES)], axis=1)
            dqb = jnp.concatenate(dqb, axis=1)
            dkb = jnp.concatenate(dkb, axis=1)
            state_term = jnp.concatenate(
                [jnp.sum(dst[hh] * st[hh], axis=0, keepdims=True) for hh in range(HGRN_HEADS)], axis=1)
            dq = dqe * ebm + dqb * eb
            dk = dke * embm + dkb * ebl
            db = (qe.astype(F32) * dqe - ke.astype(F32) * dke) + q * (dqb * eb) - k * (dkb * ebl)
            d_last = jnp.sum(k * ebl * dkb, axis=0, keepdims=True) + state_term * e_last
            dg = _dot_exact_lhs(ones_u, db) + d_last
            df = dg / f - dk
            first = HQ_COL * HGRN_DH
            dp_ref[sl, first:first + HGRN_W] = (dq * (sq * (1.0 + hq * (1.0 - sq)))).astype(BF16)
            dp_ref[sl, first + HGRN_W:first + 2 * HGRN_W] = (df * (1.0 - lb_ref[...]) * sig * (1.0 - sig)).astype(BF16)
            dp_ref[sl, first + 2 * HGRN_W:first + 3 * HGRN_W] = jnp.concatenate(dv, axis=1).astype(BF16)
            dp_ref[sl, first + 3 * HGRN_W:first + 4 * HGRN_W] = dhg.astype(BF16)
            dlb_scr[...] += jnp.sum(df * (1.0 - sig), axis=0, keepdims=True)
            dgo_scr[...] += dgo_inc
            for hh in range(HGRN_HEADS):
                ds_scr[hh] = dst_next[hh]
            return carry

        lax.fori_loop(0, cpb, step, 0)

        @pl.when(pl.program_id(1) == nblk - 1)
        def _():
            dlb_ref[...] = dlb_scr[...]
            dgo_ref[...] = dgo_scr[...]

    rows = lambda bi, i: bi * nblk + (nblk - 1 - i)
    col = lambda base: pl.BlockSpec((HGRN_ROWS, HGRN_W), lambda bi, i: (rows(bi, i), base // HGRN_HEADS))
    out = pl.BlockSpec((HGRN_ROWS, HGRN_W), lambda bi, i: (rows(bi, i), 0))
    part = pl.BlockSpec((None, 1, HGRN_W), lambda bi, i: (bi, 0, 0))
    width = HG_COL * HGRN_DH + HGRN_W
    o_shape = jax.ShapeDtypeStruct((t, width), BF16)
    p_shape = jax.ShapeDtypeStruct((b, 1, HGRN_W), F32)
    return pl.pallas_call(
        body,
        name=name,
        grid=(b, nblk),
        in_specs=[col(HQ_COL), col(HF_COL), col(HI_COL), col(HG_COL), out, out, out,
                  pl.BlockSpec((1, HGRN_W), lambda bi, i: (0, 0)), pl.BlockSpec((1, HGRN_DH), lambda bi, i: (0, 0)), out,
                  pl.BlockSpec((None, HGRN_HEADS, cpb, HGRN_DH, HGRN_DH), lambda bi, i: (bi, 0, nblk - 1 - i, 0, 0)),
                  col(ATTN_W // HGRN_DH)],
        out_specs=[pl.BlockSpec((HGRN_ROWS, width), lambda bi, i: (rows(bi, i), 0))] + [part] * 2,
        out_shape=[o_shape] + [p_shape] * 2,
        scratch_shapes=[pltpu.VMEM((HGRN_HEADS, HGRN_DH, HGRN_DH), F32), pltpu.VMEM((1, HGRN_W), F32),
                        pltpu.VMEM((1, HGRN_W), F32)],
        compiler_params=_params("parallel", "arbitrary"),
    )(proj, proj, proj, proj, *dqkv, lb, go, oraw, states, dmix)


def _small_grads(name, dg1, dgm, dg2, dgq, dgk, dbias_t, dlb, dgo, lbp):
    d = dg1.shape[1]

    def body(dg1_ref, dgm_ref, dg2_ref, dgq_ref, dgk_ref, dbias_ref, dlb_ref, dgo_ref, lbp_ref,
             g1_ref, gm_ref, g2_ref, gq_ref, gk_ref, rb_ref, lbg_ref, go_ref):
        g1_ref[...] = jnp.sum(dg1_ref[...], axis=0, keepdims=True)
        gm_ref[...] = jnp.sum(dgm_ref[...], axis=0, keepdims=True)
        g2_ref[...] = jnp.sum(dg2_ref[...], axis=0, keepdims=True)
        r = lax.broadcasted_iota(jnp.int32, (ATTN_W, ATTN_DH), 0)
        cidx = lax.broadcasted_iota(jnp.int32, (ATTN_W, ATTN_DH), 1)
        fold = jnp.where(jnp.bitwise_and(r, ATTN_DH - 1) == cidx, 1.0, 0.0).astype(BF16)
        gq_ref[...] = jnp.sum(_dot_exact_rhs(dgq_ref[...], fold), axis=0, keepdims=True)
        gk_ref[...] = jnp.sum(_dot_exact_rhs(dgk_ref[...], fold), axis=0, keepdims=True)
        gosum = jnp.sum(dgo_ref[...], axis=0, keepdims=True)
        go_ref[...] = (gosum[:, 0:HGRN_DH] + gosum[:, HGRN_DH:2 * HGRN_DH]
                       + gosum[:, 2 * HGRN_DH:3 * HGRN_DH] + gosum[:, 3 * HGRN_DH:4 * HGRN_DH])
        p0 = lbp_ref[0:1, :]
        p1 = lbp_ref[1:2, :]
        lbv = 1.0 / (1.0 + jnp.exp(p1 - p0))
        dp0 = jnp.sum(dlb_ref[...], axis=0, keepdims=True) * lbv * (1.0 - lbv)
        lbg_ref[0:1, :] = dp0
        lbg_ref[1:2, :] = -dp0
        sidx = lax.broadcasted_iota(jnp.int32, (BAND, N_REL_PAD), 0)
        ridx = lax.broadcasted_iota(jnp.int32, (BAND, N_REL_PAD), 1)

        def step(tq, acc):
            rel = jnp.clip(tq + KPAD - sidx, -REL_CLIP, REL_CLIP) + REL_CLIP
            onehot = jnp.where(rel == ridx, 1.0, 0.0).astype(BF16)
            return acc + _dot_exact_rhs(dbias_ref[tq], onehot)

        rb_ref[...] = lax.fori_loop(0, CHUNK, step, jnp.zeros((ATTN_HEADS, N_REL_PAD), F32))

    ins = [dg1, dgm, dg2, dgq, dgk, dbias_t, dlb, dgo, lbp]
    outs = [jax.ShapeDtypeStruct((1, d), F32)] * 3 + [jax.ShapeDtypeStruct((1, ATTN_DH), F32)] * 2 + [
        jax.ShapeDtypeStruct((ATTN_HEADS, N_REL_PAD), F32), jax.ShapeDtypeStruct((2, HGRN_W), F32),
        jax.ShapeDtypeStruct((1, HGRN_DH), F32)]
    vm = pl.BlockSpec(memory_space=pltpu.VMEM)
    return pl.pallas_call(
        body,
        name=name,
        in_specs=[vm] * len(ins),
        out_specs=[vm] * len(outs),
        out_shape=outs,
        compiler_params=pltpu.CompilerParams(vmem_limit_bytes=VMEM_LIMIT),
    )(*ins)


def _adam_update(w, g, m, v):
    m2 = ADAM_B1 * m + (1.0 - ADAM_B1) * g
    v2 = ADAM_B2 * v + (1.0 - ADAM_B2) * (g * g)
    m_hat = m2 / (1.0 - ADAM_B1 ** ADAM_STEP)
    v_hat = v2 / (1.0 - ADAM_B2 ** ADAM_STEP)
    delta = -ADAM_LR * (m_hat / (jnp.sqrt(v_hat) + ADAM_EPS) + ADAM_WD * w)
    return delta, m2, v2


def _rows_tile(r):
    for cand in (256, 352, 128, 176, 64, 32, 16):
        if r % cand == 0 and r > cand:
            return cand
    return r


def _pair_sum(name, grad, theirs, core):
    n, half, c = theirs.shape
    tr = _rows_tile(half)
    nth = half // tr

    def body(core_ref, a_ref, b_ref, o_ref):
        o_ref[...] = (a_ref[...].astype(F32) + b_ref[...].astype(F32)).astype(o_ref.dtype)

    spec = pl.BlockSpec((None, tr, c), lambda i, j, core_ref: (i, j, 0))
    return pl.pallas_call(
        body, name=name,
        grid_spec=pltpu.PrefetchScalarGridSpec(
            num_scalar_prefetch=1, grid=(n, nth),
            in_specs=[pl.BlockSpec((None, tr, c), lambda i, j, core_ref: (i, core_ref[0] * nth + j, 0)), spec],
            out_specs=spec),
        out_shape=jax.ShapeDtypeStruct((n, half, c), BF16), compiler_params=_params("parallel", "parallel"),
    )(core, grad, theirs)


def _chip_sum(name, own, parts, chip):
    _, half, c = own.shape
    tr = _rows_tile(half)

    def body(chip_ref, own_ref, p_ref, o_ref):
        me = chip_ref[0]
        mine = own_ref[...].astype(F32)
        flip_x, flip_y, flip_xy = (p_ref[i].astype(F32) for i in range(3))
        acc = None
        for k in range(N_CHIPS):
            rel = jnp.bitwise_xor(me, k)
            term = jnp.where(rel == 0, mine, jnp.where(rel == 2, flip_x, jnp.where(rel == 1, flip_y, flip_xy)))
            acc = term if acc is None else acc + term
        o_ref[...] = acc

    return pl.pallas_call(
        body, name=name,
        grid_spec=pltpu.PrefetchScalarGridSpec(
            num_scalar_prefetch=1, grid=(half // tr,),
            in_specs=[pl.BlockSpec((None, tr, c), lambda j, chip_ref: (chip_ref[0], j, 0)),
                      pl.BlockSpec((3, tr, c), lambda j, chip_ref: (0, j, 0))],
            out_specs=pl.BlockSpec((tr, c), lambda j, chip_ref: (j, 0))),
        out_shape=jax.ShapeDtypeStruct((half, c), F32), compiler_params=_params("parallel"),
    )(chip, own, parts)


def _adamw(name, w, g_mine, g_theirs, m, v, core):
    _, r, c = w.shape
    half = r // 2
    tr = _rows_tile(half)
    nth = half // tr

    def body(core_ref, w_ref, gm_ref, gt_ref, m_ref, v_ref, g_ref, d_ref, m2_ref, v2_ref):
        g = jnp.where(pl.program_id(0) == core_ref[0], gm_ref[...], gt_ref[...])
        delta, m2, v2 = _adam_update(w_ref[...], g, m_ref[...], v_ref[...])
        g_ref[...] = g
        d_ref[...] = delta
        m2_ref[...] = m2
        v2_ref[...] = v2

    full = pl.BlockSpec((None, tr, c), lambda h, j, core_ref: (0, h * nth + j, 0))
    part = pl.BlockSpec((tr, c), lambda h, j, core_ref: (j, 0))
    shape = jax.ShapeDtypeStruct((1, r, c), F32)
    return pl.pallas_call(
        body, name=name,
        grid_spec=pltpu.PrefetchScalarGridSpec(
            num_scalar_prefetch=1, grid=(2, nth), in_specs=[full, part, part, full, full], out_specs=[full] * 4),
        out_shape=[shape] * 4, compiler_params=_params("parallel", "parallel"),
    )(core, w, g_mine, g_theirs, m, v)


def _rel_bias_table(name, rel_bias):
    padded = jnp.pad(rel_bias, ((0, 0), (0, N_REL_PAD - N_REL)))

    def body(rb_ref, o_ref):
        ridx = lax.broadcasted_iota(jnp.int32, (N_REL_PAD, BAND), 0)
        sidx = lax.broadcasted_iota(jnp.int32, (N_REL_PAD, BAND), 1)
        rb = rb_ref[...]

        def step(tq, carry):
            rel = jnp.clip(tq + KPAD - sidx, -REL_CLIP, REL_CLIP) + REL_CLIP
            onehot = jnp.where(rel == ridx, 1.0, 0.0).astype(BF16)
            o_ref[tq] = _dot_exact_rhs(rb, onehot)
            return carry

        lax.fori_loop(0, CHUNK, step, 0)

    vm = pl.BlockSpec(memory_space=pltpu.VMEM)
    table = pl.pallas_call(
        body, name=name, in_specs=[vm], out_specs=vm,
        out_shape=jax.ShapeDtypeStruct((CHUNK, ATTN_HEADS, BAND), F32),
    )(padded)
    return table.transpose(1, 0, 2)


def _adamw_small(name, w, parts, m, v):
    def body(w_ref, p_ref, m_ref, v_ref, g_ref, d_ref, m2_ref, v2_ref):
        g = p_ref[0]
        for i in range(1, N_DEV):
            g = g + p_ref[i]
        delta, m2, v2 = _adam_update(w_ref[...], g, m_ref[...], v_ref[...])
        g_ref[...] = g
        d_ref[...] = delta
        m2_ref[...] = m2
        v2_ref[...] = v2

    vm = pl.BlockSpec(memory_space=pltpu.VMEM)
    shape = jax.ShapeDtypeStruct((SMALL_ROWS, SMALL_COLS), F32)
    return pl.pallas_call(
        body, name=name, in_specs=[vm] * 4, out_specs=[vm] * 4, out_shape=[shape] * 4,
    )(w, parts, m, v)


def _position():
    return lax.axis_index("x"), lax.axis_index("y"), lax.axis_index("c")


def _other_chips(x, y):
    return [(1 - x, y), (x, 1 - y), (1 - x, 1 - y)]


ANY = pl.BlockSpec(memory_space=pl.ANY)


HBM = pl.BlockSpec(memory_space=pltpu.HBM)
SEM = pl.BlockSpec(memory_space=pltpu.SEMAPHORE)
SPLIT_COPY = pltpu.SideEffectType.DATAFLOW_SIDE_EFFECTING


def _gather_copy(shards, outs, send_sem, recv_sem, i, j):
    x, y, c = _position()
    chips = _other_chips(x, y)
    half = shards[i].shape[0] // 2
    rows = pl.ds(pl.multiple_of(c * half, 16), half)
    return pltpu.make_async_remote_copy(
        src_ref=shards[i].at[rows, :], dst_ref=outs[i].at[2 * x + y, rows, :],
        send_sem=send_sem.at[3 * i + j], recv_sem=recv_sem.at[3 * i + j],
        device_id=(chips[j][0], chips[j][1], c), device_id_type=MESH)


def _gather_start(name, shards, after):
    n = len(shards)

    def body(*refs):
        srcs, outs = refs[:n], refs[n:2 * n]
        send_sem, recv_sem = refs[2 * n + len(after)], refs[2 * n + len(after) + 1]
        token = refs[-1]
        for i in range(n):
            for j in range(3):
                _gather_copy(srcs, outs, send_sem, recv_sem, i, j).start()
        token[...] = jnp.zeros_like(token)

    full = [(N_CHIPS,) + s.shape for s in shards]
    res = pl.pallas_call(
        body,
        name=name,
        in_specs=[HBM] * (2 * n) + [ANY] * len(after),
        out_specs=[SEM, SEM] + [HBM] * (2 * n) + [pl.BlockSpec(memory_space=pltpu.VMEM)],
        out_shape=[pltpu.SemaphoreType.DMA((3 * n,)), pltpu.SemaphoreType.DMA((3 * n,))]
        + [pltpu.HBM(s.shape, s.dtype) for s in shards]
        + [pltpu.HBM(shp, s.dtype) for shp, s in zip(full, shards)]
        + [jax.ShapeDtypeStruct((8, LANES), F32)],
        input_output_aliases={i: 2 + i for i in range(2 * n)},
        compiler_params=pltpu.CompilerParams(has_side_effects=SPLIT_COPY),
    )(*[pltpu.with_memory_space_constraint(s, pltpu.HBM) for s in shards],
      *[pltpu.with_memory_space_constraint(lax.empty(shp, s.dtype), pltpu.HBM) for shp, s in zip(full, shards)],
      *after)
    return res[0], res[1], list(res[2:2 + n]), list(res[2 + n:2 + 2 * n]), res[-1]


def _gather_wait(name, send_sem, recv_sem, shards, outs, after):
    n = len(shards)

    def body(*refs):
        srcs, out_refs = refs[:n], refs[n:2 * n]
        send_ref, recv_ref = refs[2 * n], refs[2 * n + 1]
        for i in range(n):
            for j in range(3):
                copy = _gather_copy(srcs, out_refs, send_ref, recv_ref, i, j)
                copy.wait_send()
                copy.wait_recv()

    res = pl.pallas_call(
        body,
        name=name,
        in_specs=[HBM] * (2 * n) + [SEM, SEM] + [ANY] * len(after),
        out_specs=[HBM] * (2 * n),
        out_shape=[pltpu.HBM(s.shape, s.dtype) for s in shards] + [pltpu.HBM(o.shape, o.dtype) for o in outs],
        input_output_aliases={i: i for i in range(2 * n)},
        compiler_params=pltpu.CompilerParams(has_side_effects=SPLIT_COPY),
    )(*shards, *outs, send_sem, recv_sem, *after)
    return list(res[:n]), list(res[n:])


def _gather_join(name, shards, outs):
    n = len(shards)

    def body(*refs):
        srcs, ins, outs_ = refs[:n], refs[n:2 * n], refs[2 * n:3 * n]
        own_send, own_recv, half_send, half_recv = refs[3 * n:]
        x, y, c = _position()
        chips = _other_chips(x, y)
        copies = []
        for i in range(n):
            copies.append(pltpu.make_async_remote_copy(
                src_ref=srcs[i], dst_ref=outs_[i].at[2 * x + y], send_sem=own_send.at[i], recv_sem=own_recv.at[i],
                device_id=(x, y, 1 - c), device_id_type=MESH))
            half = srcs[i].shape[0] // 2
            rows = pl.ds(pl.multiple_of(c * half, 16), half)
            for j in range(3):
                slot = 2 * chips[j][0] + chips[j][1]
                copies.append(pltpu.make_async_remote_copy(
                    src_ref=ins[i].at[slot, rows, :], dst_ref=outs_[i].at[slot, rows, :],
                    send_sem=half_send.at[3 * i + j], recv_sem=half_recv.at[3 * i + j],
                    device_id=(x, y, 1 - c), device_id_type=MESH))
        for cp in copies:
            cp.start()
        for cp in copies:
            cp.wait()

    return pl.pallas_call(
        body,
        name=name,
        in_specs=[ANY] * (2 * n),
        out_specs=[ANY] * n,
        out_shape=[jax.ShapeDtypeStruct(o.shape, o.dtype) for o in outs],
        input_output_aliases={n + i: i for i in range(n)},
        scratch_shapes=[pltpu.SemaphoreType.DMA((n,))] * 2 + [pltpu.SemaphoreType.DMA((3 * n,))] * 2,
    )(*shards, *outs)


def _pair_copy(grads, lands, send_sem, recv_sem, i):
    x, y, c = _position()
    half = grads[i].shape[1] // 2
    give = pl.ds(pl.multiple_of((1 - c) * half, 16), half)
    return pltpu.make_async_remote_copy(
        src_ref=grads[i].at[:, give, :], dst_ref=lands[i], send_sem=send_sem.at[i], recv_sem=recv_sem.at[i],
        device_id=(x, y, 1 - c), device_id_type=MESH)


def _pair_start(name, grads):
    n = len(grads)

    def body(*refs):
        srcs, lands = refs[:n], refs[n:2 * n]
        send_sem, recv_sem = refs[2 * n], refs[2 * n + 1]
        token = refs[-1]
        for i in range(n):
            _pair_copy(srcs, lands, send_sem, recv_sem, i).start()
        token[...] = jnp.zeros_like(token)

    halves = [(g.shape[0], g.shape[1] // 2, g.shape[2]) for g in grads]
    res = pl.pallas_call(
        body,
        name=name,
        in_specs=[HBM] * (2 * n),
        out_specs=[SEM, SEM] + [HBM] * (2 * n) + [pl.BlockSpec(memory_space=pltpu.VMEM)],
        out_shape=[pltpu.SemaphoreType.DMA((n,)), pltpu.SemaphoreType.DMA((n,))]
        + [pltpu.HBM(g.shape, g.dtype) for g in grads]
        + [pltpu.HBM(shp, g.dtype) for shp, g in zip(halves, grads)]
        + [jax.ShapeDtypeStruct((8, LANES), F32)],
        input_output_aliases={i: 2 + i for i in range(2 * n)},
        compiler_params=pltpu.CompilerParams(has_side_effects=SPLIT_COPY),
    )(*[pltpu.with_memory_space_constraint(g, pltpu.HBM) for g in grads],
      *[pltpu.with_memory_space_constraint(lax.empty(shp, g.dtype), pltpu.HBM) for shp, g in zip(halves, grads)])
    return res[0], res[1], list(res[2:2 + n]), list(res[2 + n:2 + 2 * n]), res[-1]


def _pair_wait(name, send_sem, recv_sem, grads, lands, after):
    n = len(grads)

    def body(*refs):
        srcs, land_refs = refs[:n], refs[n:2 * n]
        send_ref, recv_ref = refs[2 * n], refs[2 * n + 1]
        for i in range(n):
            copy = _pair_copy(srcs, land_refs, send_ref, recv_ref, i)
            copy.wait_send()
            copy.wait_recv()

    res = pl.pallas_call(
        body,
        name=name,
        in_specs=[HBM] * (2 * n) + [SEM, SEM, ANY],
        out_specs=[HBM] * (2 * n),
        out_shape=[pltpu.HBM(g.shape, g.dtype) for g in grads] + [pltpu.HBM(l.shape, l.dtype) for l in lands],
        input_output_aliases={i: i for i in range(2 * n)},
        compiler_params=pltpu.CompilerParams(has_side_effects=SPLIT_COPY),
    )(*grads, *lands, send_sem, recv_sem, after)
    return list(res[:n]), list(res[n:])


def _scatter_copy(srcs, lands, send_sem, recv_sem, i, j):
    x, y, c = _position()
    chips = _other_chips(x, y)
    return pltpu.make_async_remote_copy(
        src_ref=srcs[i].at[2 * chips[j][0] + chips[j][1]], dst_ref=lands[i].at[j],
        send_sem=send_sem.at[3 * i + j], recv_sem=recv_sem.at[3 * i + j],
        device_id=(chips[j][0], chips[j][1], c), device_id_type=MESH)


def _scatter_start(name, sums):
    n = len(sums)

    def body(*refs):
        srcs, lands = refs[:n], refs[n:2 * n]
        send_sem, recv_sem = refs[2 * n], refs[2 * n + 1]
        token = refs[-1]
        for i in range(n):
            for j in range(3):
                _scatter_copy(srcs, lands, send_sem, recv_sem, i, j).start()
        token[...] = jnp.zeros_like(token)

    land_shapes = [(3,) + s.shape[1:] for s in sums]
    res = pl.pallas_call(
        body,
        name=name,
        in_specs=[HBM] * (2 * n),
        out_specs=[SEM, SEM] + [HBM] * (2 * n) + [pl.BlockSpec(memory_space=pltpu.VMEM)],
        out_shape=[pltpu.SemaphoreType.DMA((3 * n,)), pltpu.SemaphoreType.DMA((3 * n,))]
        + [pltpu.HBM(s.shape, s.dtype) for s in sums]
        + [pltpu.HBM(shp, s.dtype) for shp, s in zip(land_shapes, sums)]
        + [jax.ShapeDtypeStruct((8, LANES), F32)],
        input_output_aliases={i: 2 + i for i in range(2 * n)},
        compiler_params=pltpu.CompilerParams(has_side_effects=SPLIT_COPY),
    )(*[pltpu.with_memory_space_constraint(s, pltpu.HBM) for s in sums],
      *[pltpu.with_memory_space_constraint(lax.empty(shp, s.dtype), pltpu.HBM) for shp, s in zip(land_shapes, sums)])
    return res[0], res[1], list(res[2:2 + n]), list(res[2 + n:2 + 2 * n]), res[-1]


def _scatter_wait(name, send_sem, recv_sem, sums, lands, after):
    n = len(sums)

    def body(*refs):
        srcs, land_refs = refs[:n], refs[n:2 * n]
        send_ref, recv_ref = refs[2 * n], refs[2 * n + 1]
        for i in range(n):
            for j in range(3):
                copy = _scatter_copy(srcs, land_refs, send_ref, recv_ref, i, j)
                copy.wait_send()
                copy.wait_recv()

    res = pl.pallas_call(
        body,
        name=name,
        in_specs=[HBM] * (2 * n) + [SEM, SEM, ANY],
        out_specs=[HBM] * (2 * n),
        out_shape=[pltpu.HBM(s.shape, s.dtype) for s in sums] + [pltpu.HBM(l.shape, l.dtype) for l in lands],
        input_output_aliases={i: i for i in range(2 * n)},
        compiler_params=pltpu.CompilerParams(has_side_effects=SPLIT_COPY),
    )(*sums, *lands, send_sem, recv_sem, after)
    return list(res[:n]), list(res[n:])


def _pair_join(name, halves, small=None):
    n = len(halves)
    if small is None:
        def body_plain(*refs):
            ins, outs = refs[:n], refs[n:2 * n]
            send_sem, recv_sem = refs[2 * n:]
            x, y, c = _position()
            swaps = [pltpu.make_async_remote_copy(
                src_ref=ins[i], dst_ref=outs[i], send_sem=send_sem.at[i], recv_sem=recv_sem.at[i],
                device_id=(x, y, 1 - c), device_id_type=MESH) for i in range(n)]
            for swap in swaps:
                swap.start()
            for swap in swaps:
                swap.wait()

        return pl.pallas_call(
            body_plain,
            name=name,
            in_specs=[ANY] * n,
            out_specs=[ANY] * n,
            out_shape=[jax.ShapeDtypeStruct(h.shape, h.dtype) for h in halves],
            scratch_shapes=[pltpu.SemaphoreType.DMA((n,))] * 2,
        )(*halves)

    def body(*refs):
        ins, small_ref = refs[:n], refs[n]
        outs, all_ref = refs[n + 1:2 * n + 1], refs[2 * n + 1]
        send_sem, recv_sem, sm_send, sm_recv, sm_local = refs[2 * n + 2:]
        x, y, c = _position()
        swaps = []
        for i in range(n):
            swap = pltpu.make_async_remote_copy(
                src_ref=ins[i], dst_ref=outs[i], send_sem=send_sem.at[i], recv_sem=recv_sem.at[i],
                device_id=(x, y, 1 - c), device_id_type=MESH)
            swap.start()
            swaps.append(swap)
        me = 4 * x + 2 * y + c
        sm_own = pltpu.make_async_copy(small_ref, all_ref.at[me], sm_local)
        sm_own.start()
        pushes, arrivals = [], []
        for mask in range(1, N_DEV):
            px, py, pc = x ^ (mask >> 2), y ^ ((mask >> 1) & 1), c ^ (mask & 1)
            pushes.append(pltpu.make_async_remote_copy(
                src_ref=small_ref, dst_ref=all_ref.at[me], send_sem=sm_send.at[mask - 1], recv_sem=sm_recv.at[mask - 1],
                device_id=(px, py, pc), device_id_type=MESH))
            arrivals.append(pltpu.make_async_remote_copy(
                src_ref=small_ref, dst_ref=all_ref.at[4 * px + 2 * py + pc], send_sem=sm_send.at[mask - 1],
                recv_sem=sm_recv.at[mask - 1], device_id=(px, py, pc), device_id_type=MESH))
        for cp in pushes:
            cp.start()
        for swap in swaps:
            swap.wait()
        for cp in arrivals:
            cp.wait_recv()
        for cp in pushes:
            cp.wait_send()
        sm_own.wait()

    res = pl.pallas_call(
        body,
        name=name,
        in_specs=[ANY] * (n + 1),
        out_specs=[ANY] * (n + 1),
        out_shape=[jax.ShapeDtypeStruct(h.shape, h.dtype) for h in halves]
        + [jax.ShapeDtypeStruct((N_DEV,) + small.shape, small.dtype)],
        scratch_shapes=[pltpu.SemaphoreType.DMA((n,))] * 2 + [pltpu.SemaphoreType.DMA((N_DEV - 1,))] * 2
        + [pltpu.SemaphoreType.DMA(())],
    )(*halves, small)
    return res[:n], res[n]


def _lower_bound(lbp):
    return jax.nn.softmax(lbp, axis=0)[0:1]


def _local_step(x, target, g1, gm, g2, gq, gk, go, rel_bias, lbp, first_weights, mid_weights, last_weights, on_grads, grads_sent):
    b, s, d = x.shape
    t = b * s
    x0 = x.reshape(t, d)
    tgt = target.reshape(t, d)
    gq_t = jnp.tile(gq, (1, ATTN_HEADS))
    gk_t = jnp.tile(gk, (1, ATTN_HEADS))
    lb = _lower_bound(lbp)
    table = _band_table(_rel_bias_table("rel_bias_table", rel_bias))

    h1 = _rmsnorm_fwd("norm1", x0, g1)
    wg1, wu1, deps1 = first_weights((h1, table))
    a1, b1, z1 = _ffn_up("ffn1_up", h1, wg1, wu1, deps1)
    wd1, w_in, w_out = mid_weights((z1,))
    ns = w_in.shape[0]
    x1, h2 = _ffn_down("ffn1_down", z1, wd1, x0, gm)
    proj = _in_proj("in_proj", h2, w_in)
    proj3 = proj.reshape(b, s, proj.shape[1])
    qn, kn, vb = _qk_prep("qk_prep", proj3, gq_t, gk_t)
    attn = _attn_fwd("attn_fwd", qn, kn, vb, table).reshape(t, ATTN_W)
    mix, oraw, states = _hgrn_fwd("hgrn_fwd", proj, attn, lb, go, b, s)
    x2, h3 = _out_proj("out_proj", mix, w_out, x1, g2)
    wg2, wu2, wd2 = last_weights((h3,))
    a2, b2, z2 = _ffn_up("ffn2_up", h3, wg2, wu2)
    dy, dyh, sq = _ffn_down_loss("ffn2_down_loss", z2, wd2, x2, tgt)
    loss = 0.5 * jnp.sum(sq) / d

    da2, db2 = _ffn_bwd_act("ffn2_bwd_act", dyh, wd2, a2, b2)
    dwd2 = _grad_w_shardrows("ffn2_dwd", z2, dyh)
    dwg2 = _grad_w_shardrows("ffn2_dwg", da2, h3)
    dwu2 = _grad_w_shardrows("ffn2_dwu", db2, h3)
    sent2 = on_grads("ffn2", {"ffn2_w_gate": dwg2, "ffn2_w_up": dwu2, "ffn2_w_down": dwd2})
    dx2, dx2b, dg2 = _ffn_bwd_in("ffn2_bwd_in", da2, db2, wg2, wu2, x2, g2, dy, 1.0, sent2)
    sent2 = grads_sent("ffn2", dx2b)

    dwout = _grad_w_out("dw_out", mix, dx2b)
    dmix = _out_proj_bwd("out_proj_bwd", dx2b, w_out, sent2)
    dqn, dkn, dvn, dbe, dbo = _attn_bwd("attn_bwd", qn, kn, vb, table, dmix.reshape(b, s, dmix.shape[1]))
    dbias = dbe[:, :, :BAND] + dbo[:, :, CHUNK:]
    dpq, dpk, dpv, dgq, dgk = _qk_prep_bwd("qk_prep_bwd", proj3, dqn, dkn, dvn, gq_t, gk_t)
    dpq, dpk, dpv = (a.reshape(t, ATTN_W) for a in (dpq, dpk, dpv))
    dproj, dlb, dgo = _hgrn_bwd("hgrn_bwd", proj, (dpq, dpk, dpv), lb, go, oraw, states, dmix, b, s)
    dwin = _grad_w_in("dw_in", h2, dproj, ns)
    dx1, dx1h, dgm = _in_proj_bwd("in_proj_bwd", dproj, w_in, x1, gm, dx2, 0.5)

    dwd1 = _grad_w_shardrows("ffn1_dwd", z1, dx1h)
    sent_mix = on_grads("mix", {"w_in": dwin, "w_out": dwout.reshape(ns, dwout.shape[0] // ns, d),
                                "ffn1_w_down": dwd1})
    da1, db1 = _ffn_bwd_act("ffn1_bwd_act", dx1h, wd1, a1, b1, sent_mix)
    sent_mix = grads_sent("mix", da1)
    dwg1 = _grad_w_shardrows("ffn1_dwg", da1, h1, sent_mix)
    dwu1 = _grad_w_shardrows("ffn1_dwu", db1, h1)
    on_grads("ffn1", {"ffn1_w_gate": dwg1, "ffn1_w_up": dwu1})
    sent1 = grads_sent("ffn1", None)
    dx0, dg1 = _ffn_bwd_in("ffn1_bwd_in", da1, db1, wg1, wu1, x0, g1, dx1, None, sent1)

    nt = dg1.shape[0]
    sg = _small_grads(
        "small_grads", dg1.reshape(nt, d), dgm.reshape(nt, d), dg2.reshape(nt, d),
        dgq.reshape(-1, ATTN_W), dgk.reshape(-1, ATTN_W), dbias.transpose(1, 0, 2),
        dlb.reshape(b, HGRN_W), dgo.reshape(b, HGRN_W), lbp)
    g1g, gmg, g2g, gqg, gkg, rbg, lbg, gog = sg
    small = _pack_small(g1g, gmg, g2g, lbg, rbg[:, :N_REL], gqg, gkg, gog)
    return loss, dx0.reshape(b, s, d), small


def _pack_small(g1, gm, g2, lbp, rel_bias, gq, gk, go):
    flat = [g1.reshape(-1), gm.reshape(-1), g2.reshape(-1), lbp.reshape(-1), rel_bias.reshape(-1)]
    n_bias = 3 * SMALL_COLS - rel_bias.size
    heads = [gq.reshape(-1), gk.reshape(-1), go.reshape(-1)]
    n_tail = SMALL_COLS - sum(h.size for h in heads)
    return jnp.concatenate(flat + [jnp.zeros((n_bias,), F32)] + heads + [jnp.zeros((n_tail,), F32)]).reshape(
        SMALL_ROWS, SMALL_COLS)


def _unpack_small(p, d):
    flat = p.reshape(-1)
    o = 3 * d
    g1, gm, g2 = p[0:1], p[1:2], p[2:3]
    lbp = flat[o:o + 2 * HGRN_W].reshape(2, HGRN_W)
    o = 4 * SMALL_COLS
    rel = flat[o:o + ATTN_HEADS * N_REL].reshape(1, ATTN_HEADS, N_REL)
    o = 7 * SMALL_COLS
    gq = flat[o:o + ATTN_DH].reshape(1, ATTN_DH)
    gk = flat[o + ATTN_DH:o + 2 * ATTN_DH].reshape(1, ATTN_DH)
    go = flat[o + 2 * ATTN_DH:o + 2 * ATTN_DH + HGRN_DH].reshape(1, HGRN_DH)
    return g1, gm, g2, gq, gk, rel, lbp, go


def kernel(x, ffn1_norm_g, ffn1_w_gate, ffn1_w_up, ffn1_w_down, mix_norm_g, w_in, attn_q_norm_g, attn_k_norm_g, attn_rel_bias, hgrn_lower_bounds, hgrn_out_norm_g, w_out, ffn2_norm_g, ffn2_w_gate, ffn2_w_up, ffn2_w_down, loss_target, m_ffn1_norm_g, m_ffn1_w_gate, m_ffn1_w_up, m_ffn1_w_down, m_mix_norm_g, m_w_in, m_attn_q_norm_g, m_attn_k_norm_g, m_attn_rel_bias, m_hgrn_lower_bounds, m_hgrn_out_norm_g, m_w_out, m_ffn2_norm_g, m_ffn2_w_gate, m_ffn2_w_up, m_ffn2_w_down, v_ffn1_norm_g, v_ffn1_w_gate, v_ffn1_w_up, v_ffn1_w_down, v_mix_norm_g, v_w_in, v_attn_q_norm_g, v_attn_k_norm_g, v_attn_rel_bias, v_hgrn_lower_bounds, v_hgrn_out_norm_g, v_w_out, v_ffn2_norm_g, v_ffn2_w_gate, v_ffn2_w_up, v_ffn2_w_down):
    d = x.shape[-1]
    big_w = [ffn1_w_gate, ffn1_w_up, ffn1_w_down, w_in, w_out, ffn2_w_gate, ffn2_w_up, ffn2_w_down]
    big_m = [m_ffn1_w_gate, m_ffn1_w_up, m_ffn1_w_down, m_w_in, m_w_out, m_ffn2_w_gate, m_ffn2_w_up, m_ffn2_w_down]
    big_v = [v_ffn1_w_gate, v_ffn1_w_up, v_ffn1_w_down, v_w_in, v_w_out, v_ffn2_w_gate, v_ffn2_w_up, v_ffn2_w_down]
    big_names = ["ffn1_w_gate", "ffn1_w_up", "ffn1_w_down", "w_in", "w_out", "ffn2_w_gate", "ffn2_w_up", "ffn2_w_down"]
    flipped = {nm for nm in big_names if nm.endswith("gate") or nm.endswith("up")}
    flip = lambda nm, a: jnp.swapaxes(a, 1, 2) if nm in flipped else a
    big_w, big_m, big_v = ([flip(nm, a) for nm, a in zip(big_names, arrs)] for arrs in (big_w, big_m, big_v))

    shards = [w[0].astype(BF16) for w in big_w]
    start_a = _gather_start("gather_start_up1", shards[:2], ())
    start_b = _gather_start("gather_start_mid", shards[2:5], (start_a[4],))
    start_c = _gather_start("gather_start_ffn2", shards[5:], (start_b[4],))

    def gathered(tag, started, after):
        send_sem, recv_sem, srcs, outs, _ = started
        srcs, outs = _gather_wait("gather_wait_" + tag, send_sem, recv_sem, srcs, outs, after)
        return _gather_join("gather_join_" + tag, srcs, outs)

    def first_weights(after):
        return (*gathered("up1", start_a, after), (start_c[4],))

    def mid_weights(after):
        wd1, win_f, wout_f = gathered("mid", start_b, after)
        return wd1, win_f, wout_f.reshape(wout_f.shape[0] * wout_f.shape[1], d)

    def last_weights(after):
        return gathered("ffn2", start_c, after)

    core = lax.axis_index("c").astype(jnp.int32).reshape(1)
    chip = (2 * lax.axis_index("x") + lax.axis_index("y")).astype(jnp.int32).reshape(1)
    started = {}

    def on_grads(tag, grads):
        names = list(grads)
        started[tag] = (names, _pair_start("pair_start_" + tag, [grads[nm] for nm in names]))
        return (started[tag][1][4],)

    def grads_sent(tag, after):
        names, (send_sem, recv_sem, grads, lands, token) = started[tag]
        grads, theirs = _pair_wait("pair_wait_" + tag, send_sem, recv_sem, grads, lands, token if after is None else after)
        sums = [_pair_sum("pair_sum_" + nm, g, th, core) for nm, g, th in zip(names, grads, theirs)]
        started[tag] = (names, _scatter_start("scatter_start_" + tag, sums))
        return (started[tag][1][4],)

    loss, grad_x, small_g = _local_step(
        x, loss_target, ffn1_norm_g, mix_norm_g, ffn2_norm_g, attn_q_norm_g, attn_k_norm_g, hgrn_out_norm_g,
        attn_rel_bias[0], hgrn_lower_bounds, first_weights, mid_weights, last_weights, on_grads, grads_sent)
    loss = lax.psum(loss, ("x", "y", "c"))

    def finish(tag, after):
        names, (send_sem, recv_sem, sums, lands, _) = started[tag]
        sums, lands = _scatter_wait("scatter_wait_" + tag, send_sem, recv_sem, sums, lands, after)
        return names, [_chip_sum("chip_sum_" + nm, sm, ld, chip) for nm, sm, ld in zip(names, sums, lands)]

    by_name = {nm: (w, m, v) for nm, w, m, v in zip(big_names, big_w, big_m, big_v)}
    updated = {}

    def update(names, halves, other_halves):
        for nm, mine, theirs in zip(names, halves, other_halves):
            w, m, v = by_name[nm]
            updated[nm] = _adamw("adamw_" + nm, w, mine, theirs, m, v, core)

    last_token = started["ffn1"][1][4]
    names_a, halves_a = finish("ffn2", last_token)
    names_m, halves_m = finish("mix", last_token)
    names_a, halves_a = names_a + names_m, halves_a + halves_m
    update(names_a, halves_a, _pair_join("pair_join_early", halves_a))
    names_b, halves_b = finish("ffn1", updated[names_a[-1]][1])
    others_b, small_all = _pair_join("pair_join_last", halves_b, small_g)
    update(names_b, halves_b, others_b)
    big_out = [updated[nm] for nm in big_names]

    pack = lambda g1, gm, g2, gq, gk, rel, lbp, go: _pack_small(g1, gm, g2, lbp, rel[0], gq, gk, go)
    small_w = pack(ffn1_norm_g, mix_norm_g, ffn2_norm_g, attn_q_norm_g, attn_k_norm_g, attn_rel_bias, hgrn_lower_bounds, hgrn_out_norm_g)
    small_m = pack(m_ffn1_norm_g, m_mix_norm_g, m_ffn2_norm_g, m_attn_q_norm_g, m_attn_k_norm_g, m_attn_rel_bias, m_hgrn_lower_bounds, m_hgrn_out_norm_g)
    small_v = pack(v_ffn1_norm_g, v_mix_norm_g, v_ffn2_norm_g, v_attn_q_norm_g, v_attn_k_norm_g, v_attn_rel_bias, v_hgrn_lower_bounds, v_hgrn_out_norm_g)
    small_out = [_unpack_small(p, d) for p in _adamw_small("adamw_small", small_w, small_all, small_m, small_v)]

    def assemble(kind):
        bg = [flip(nm, o[kind]) for nm, o in zip(big_names, big_out)]
        g1, gm, g2, gq, gk, rel, lbp, go = small_out[kind]
        return [g1, bg[0], bg[1], bg[2], gm, bg[3], gq, gk, rel, lbp, go, bg[4], g2, bg[5], bg[6], bg[7]]

    return (loss, grad_x, *assemble(0), *assemble(1), *assemble(2), *assemble(3))
```

```python
import functools

import jax
import jax.numpy as jnp
from jax import lax
from jax.experimental import pallas as pl
from jax.experimental.pallas import tpu as pltpu

F32 = jnp.float32
BF16 = jnp.bfloat16
MESH = pl.DeviceIdType.MESH

N_CHIPS = 4
N_DEV = 8
CHUNK = 64
ATTN_HEADS = 8
ATTN_DH = 64
ATTN_W = ATTN_HEADS * ATTN_DH
HGRN_HEADS = 4
HGRN_DH = 128
HGRN_W = HGRN_HEADS * HGRN_DH
LEFT_CHUNKS = 8
BAND = (LEFT_CHUNKS + 1) * CHUNK
KPAD = LEFT_CHUNKS * CHUNK
REL_CLIP = 128
N_REL = 2 * REL_CLIP + 1
N_REL_PAD = 384
RMS_EPS = 1e-6
LANES = 128
SMALL_ROWS = 8
SMALL_COLS = 1024

ADAM_LR = 0.001
ADAM_B1 = 0.9
ADAM_B2 = 0.999
ADAM_EPS = 1e-08
ADAM_WD = 0.01
ADAM_STEP = 10

NN = (((1,), (0,)), ((), ()))
NT = (((1,), (1,)), ((), ()))
TN = (((0,), (0,)), ((), ()))

VMEM_LIMIT = 48 * 1024 * 1024


def _sigmoid(x):
    return 1.0 / (1.0 + jnp.exp(-x))


def _silu(x):
    return x * _sigmoid(x)


def _dot(a, b, dims=NN):
    return lax.dot_general(a, b, dims, preferred_element_type=F32)


def _split3(x):
    hi = x.astype(BF16)
    r1 = x - hi.astype(F32)
    mid = r1.astype(BF16)
    lo = (r1 - mid.astype(F32)).astype(BF16)
    return hi, mid, lo


def _dot_exact_rhs(x, mat, dims=NN, pieces=3):
    hi, mid, lo = _split3(x)
    out = _dot(hi, mat, dims) + _dot(mid, mat, dims)
    return out + _dot(lo, mat, dims) if pieces == 3 else out


def _dot_exact_lhs(mat, x, dims=NN):
    hi, mid, lo = _split3(x)
    return _dot(mat, hi, dims) + _dot(mat, mid, dims) + _dot(mat, lo, dims)


def _params(*sem):
    return pltpu.CompilerParams(dimension_semantics=sem, vmem_limit_bytes=VMEM_LIMIT)


def _mm(name, ins, terms, n_acc, grid, acc_shape, outs, epilogue, extras=(), deps=()):
    nk = grid[2]
    ni, ne, nd, no = len(ins), len(extras), len(deps), len(outs)

    def body(*refs):
        in_refs = refs[:ni]
        ex_refs = refs[ni:ni + ne]
        out_refs = refs[ni + ne + nd:ni + ne + nd + no]
        acc_refs = refs[ni + ne + nd + no:]
        parts = [None] * n_acc
        for ai, li, ri, dims in terms:
            d = _dot(in_refs[li][...], in_refs[ri][...], dims)
            parts[ai] = d if parts[ai] is None else parts[ai] + d

        def finish(accs):
            res = epilogue(accs, [e[...] for e in ex_refs])
            for o, r in zip(out_refs, res):
                o[...] = r.astype(o.dtype)

        if nk == 1:
            finish(parts)
        else:
            k = pl.program_id(2)

            @pl.when(k == 0)
            def _():
                for a, p in zip(acc_refs, parts):
                    a[...] = p

            @pl.when(k > 0)
            def _():
                for a, p in zip(acc_refs, parts):
                    a[...] += p

            @pl.when(k == nk - 1)
            def _():
                finish([a[...] for a in acc_refs])

    scratch = [] if nk == 1 else [pltpu.VMEM(acc_shape, F32) for _ in range(n_acc)]
    res = pl.pallas_call(
        body,
        name=name,
        grid=grid,
        in_specs=[s for _, s in ins] + [s for _, s in extras] + [pl.BlockSpec(memory_space=pl.ANY)] * nd,
        out_specs=[s for _, s in outs],
        out_shape=[o for o, _ in outs],
        scratch_shapes=scratch,
        compiler_params=_params("parallel", "parallel", "arbitrary"),
    )(*[a for a, _ in ins], *[a for a, _ in extras], *deps)
    return res


def _mm_rows(name, lhs, weights, dims, t, outs, epilogue, extras=(), deps=()):
    tm = _row_tile(t)
    nl, ne, nd, no = len(lhs), len(extras), len(deps), len(outs)
    ns = weights[0].shape[0]

    def body(*refs):
        lhs_refs = refs[:nl]
        w_hbm = refs[nl:2 * nl]
        ex_refs = refs[2 * nl:2 * nl + ne]
        out_refs = refs[2 * nl + ne + nd:2 * nl + ne + nd + no]
        w_vmem = refs[2 * nl + ne + nd + no:3 * nl + ne + nd + no]
        sem = refs[-1]

        @pl.when(pl.program_id(0) == 0)
        def _():
            copies = [pltpu.make_async_copy(w_hbm[p], w_vmem[p], sem.at[p]) for p in range(nl)]
            for cp in copies:
                cp.start()
            for cp in copies:
                cp.wait()

        acc = None
        for p in range(nl):
            pick = lhs[p][2]
            for j in range(ns):
                part = _dot(pick(lhs_refs[p], j), w_vmem[p][j], dims)
                acc = part if acc is None else acc + part
        res = epilogue([acc], [e[...] for e in ex_refs])
        for o, r in zip(out_refs, res):
            o[...] = r.astype(o.dtype)

    return pl.pallas_call(
        body,
        name=name,
        grid=(t // tm,),
        in_specs=[s for _, s, _ in lhs] + [pl.BlockSpec(memory_space=pl.ANY)] * nl + [s for _, s in extras]
        + [pl.BlockSpec(memory_space=pl.ANY)] * nd,
        out_specs=[s for _, s in outs],
        out_shape=[o for o, _ in outs],
        scratch_shapes=[pltpu.VMEM(w.shape, w.dtype) for w in weights] + [pltpu.SemaphoreType.DMA((nl,))],
        compiler_params=_params("arbitrary"),
    )(*[a for a, _, _ in lhs], *weights, *[a for a, _ in extras], *deps)


def _mm_shards(name, x, weights, dims, outs, epilogue, extras=(), deps=()):
    t = x.shape[0]
    tm = _row_tile(t)
    nw, ne, nd, no = len(weights), len(extras), len(deps), len(outs)
    ns = weights[0].shape[0]

    def body(*refs):
        x_ref = refs[0]
        w_hbm = refs[1:1 + nw]
        ex_refs = refs[1 + nw:1 + nw + ne]
        out_refs = refs[1 + nw + ne + nd:1 + nw + ne + nd + no]
        w_vmem = refs[1 + nw + ne + nd + no:1 + 2 * nw + ne + nd + no]
        sem = refs[-1]

        @pl.when(pl.program_id(0) == 0)
        def _():
            copies = [pltpu.make_async_copy(w_hbm[p], w_vmem[p], sem.at[p]) for p in range(nw)]
            for cp in copies:
                cp.start()
            for cp in copies:
                cp.wait()

        xv = x_ref[...]
        accs = [_dot(xv, w_vmem[p][0], dims) for p in range(nw)]
        for j in range(ns):
            nxt = [_dot(xv, w_vmem[p][j + 1], dims) for p in range(nw)] if j + 1 < ns else None
            res = epilogue(accs, [e[j] for e in ex_refs])
            for (_, _, store), o, r in zip(outs, out_refs, res):
                store(o, j, r.astype(o.dtype))
            accs = nxt

    return pl.pallas_call(
        body,
        name=name,
        grid=(t // tm,),
        in_specs=[pl.BlockSpec((tm, x.shape[1]), lambda i: (i, 0))] + [pl.BlockSpec(memory_space=pl.ANY)] * nw
        + [s for _, s in extras] + [pl.BlockSpec(memory_space=pl.ANY)] * nd,
        out_specs=[s for _, s, _ in outs],
        out_shape=[o for o, _, _ in outs],
        scratch_shapes=[pltpu.VMEM(w.shape, w.dtype) for w in weights] + [pltpu.SemaphoreType.DMA((nw,))],
        compiler_params=_params("arbitrary"),
    )(x, *weights, *[a for a, _ in extras], *deps)


def _store_shard(ref, j, value):
    ref[j] = value


def _row_tile(t):
    return 512 if t % 512 == 0 else t


def _k_tile(t):
    return t if t <= 4096 else 1024


def _rmsnorm(xv, g):
    ms = jnp.mean(xv * xv, axis=-1, keepdims=True)
    return xv * lax.rsqrt(ms + RMS_EPS) * g


def _rmsnorm_fwd(name, x, g):
    t, d = x.shape
    tm = _row_tile(t)

    def body(x_ref, g_ref, h_ref):
        h_ref[...] = _rmsnorm(x_ref[...], g_ref[...]).astype(BF16)

    return pl.pallas_call(
        body,
        name=name,
        grid=(t // tm,),
        in_specs=[pl.BlockSpec((tm, d), lambda i: (i, 0)), pl.BlockSpec((1, d), lambda i: (0, 0))],
        out_specs=pl.BlockSpec((tm, d), lambda i: (i, 0)),
        out_shape=jax.ShapeDtypeStruct((t, d), BF16),
        compiler_params=_params("parallel"),
    )(x, g)


def _norm_bwd_epilogue(copy_scale):
    def epilogue(accs, ex):
        dh = accs[0]
        xv, g, dres = ex
        ms = jnp.mean(xv * xv, axis=-1, keepdims=True)
        rstd = lax.rsqrt(ms + RMS_EPS)
        xhat = xv * rstd
        dxhat = dh * g
        dx = rstd * (dxhat - xhat * jnp.mean(dxhat * xhat, axis=-1, keepdims=True))
        out = dres + dx
        dg = jnp.sum(dh * xhat, axis=0, keepdims=True)
        if copy_scale is None:
            return out, dg
        return out, out * copy_scale, dg

    return epilogue


def _ffn_up(name, h, wg, wu, deps=()):
    t, d = h.shape
    ns, f, _ = wg.shape
    tm = _row_tile(t)

    def epilogue(accs, ex):
        a, b = accs
        sg = _sigmoid(a)
        act = a * sg
        return act, b * (sg * (1.0 + a * (1.0 - sg))), act * b

    out = (jax.ShapeDtypeStruct((ns, t, f), BF16), pl.BlockSpec((ns, tm, f), lambda i: (0, i, 0)), _store_shard)
    return _mm_shards(name, h, [wg, wu], NT, [out] * 3, epilogue, deps=deps)


def _shard_rows(arr, tm):
    ns, _, f = arr.shape
    return arr, pl.BlockSpec((ns, tm, f), lambda i: (0, i, 0)), lambda ref, j: ref[j]


def _ffn_down(name, z, wd, x, g_next):
    _, t, _ = z.shape
    d = wd.shape[2]
    tm = _row_tile(t)
    row = pl.BlockSpec((tm, d), lambda i: (i, 0))

    def epilogue(accs, ex):
        y = ex[0] + 0.5 * accs[0]
        return y, _rmsnorm(y, ex[1])

    return _mm_rows(
        name, [_shard_rows(z, tm)], [wd], NN, t,
        outs=[(jax.ShapeDtypeStruct((t, d), F32), row), (jax.ShapeDtypeStruct((t, d), BF16), row)],
        epilogue=epilogue,
        extras=[(x, row), (g_next, pl.BlockSpec((1, d), lambda i: (0, 0)))],
    )


def _ffn_down_loss(name, z, wd, x, target):
    _, t, _ = z.shape
    d = wd.shape[2]
    tm = _row_tile(t)
    nt = t // tm
    row = pl.BlockSpec((tm, d), lambda i: (i, 0))

    def epilogue(accs, ex):
        e = ex[0] + 0.5 * accs[0] - ex[1]
        dy = e * (1.0 / d)
        return dy, 0.5 * dy, jnp.sum(e * e, axis=0, keepdims=True)

    return _mm_rows(
        name, [_shard_rows(z, tm)], [wd], NN, t,
        outs=[(jax.ShapeDtypeStruct((t, d), F32), row), (jax.ShapeDtypeStruct((t, d), BF16), row),
              (jax.ShapeDtypeStruct((nt, 1, d), F32), pl.BlockSpec((None, 1, d), lambda i: (i, 0, 0)))],
        epilogue=epilogue,
        extras=[(x, row), (target, row)],
    )


def _ffn_bwd_act(name, dout, wd, act_a, dact_b, deps=()):
    t, d = dout.shape
    ns, f, _ = wd.shape
    tm = _row_tile(t)

    def epilogue(accs, ex):
        dz = accs[0]
        return dz * ex[1].astype(F32), dz * ex[0].astype(F32)

    act = pl.BlockSpec((ns, tm, f), lambda i: (0, i, 0))
    out = (jax.ShapeDtypeStruct((ns, t, f), BF16), act, _store_shard)
    return _mm_shards(name, dout, [wd], NT, [out] * 2, epilogue, extras=[(act_a, act), (dact_b, act)], deps=deps)


def _grad_w_shardrows(name, z, dout, deps=()):
    ns, t, f = z.shape
    d = dout.shape[1]
    tk = _k_tile(t)
    return _mm(
        name,
        ins=[(z, pl.BlockSpec((None, tk, f), lambda j, n, k: (j, k, 0))),
             (dout, pl.BlockSpec((tk, d), lambda j, n, k: (k, 0)))],
        terms=[(0, 0, 1, TN)],
        n_acc=1,
        grid=(ns, 1, t // tk),
        acc_shape=(f, d),
        outs=[(jax.ShapeDtypeStruct((ns, f, d), BF16), pl.BlockSpec((None, f, d), lambda j, n, k: (j, 0, 0)))],
        epilogue=lambda accs, ex: (accs[0],),
        deps=deps,
    )[0]


def _norm_bwd_outs(t, d, tm, copy_scale):
    row = pl.BlockSpec((tm, d), lambda i: (i, 0))
    outs = [(jax.ShapeDtypeStruct((t, d), F32), row)]
    if copy_scale is not None:
        outs.append((jax.ShapeDtypeStruct((t, d), BF16), row))
    outs.append((jax.ShapeDtypeStruct((t // tm, 1, d), F32), pl.BlockSpec((None, 1, d), lambda i: (i, 0, 0))))
    return row, outs


def _ffn_bwd_in(name, da, db, wg, wu, x, g, dres, copy_scale, deps=()):
    _, t, _ = da.shape
    d = wg.shape[2]
    tm = _row_tile(t)
    row, outs = _norm_bwd_outs(t, d, tm, copy_scale)
    return _mm_rows(
        name, [_shard_rows(da, tm), _shard_rows(db, tm)], [wg, wu], NN, t,
        outs=outs,
        epilogue=_norm_bwd_epilogue(copy_scale),
        extras=[(x, row), (g, pl.BlockSpec((1, d), lambda i: (0, 0))), (dres, row)],
        deps=deps,
    )


def _in_proj(name, h, w_in):
    t, d = h.shape
    ns, _, pj = w_in.shape
    tm = _row_tile(t)
    def store(ref, j, value):
        ref[:, j * pj:(j + 1) * pj] = value

    out = (jax.ShapeDtypeStruct((t, ns * pj), F32), pl.BlockSpec((tm, ns * pj), lambda i: (i, 0)), store)
    return _mm_shards(name, h, [w_in], NN, [out], lambda accs, ex: (accs[0],))[0]


def _in_proj_bwd(name, dp, w_in, x, g, dres, copy_scale, deps=()):
    t = dp.shape[0]
    ns, d, pj = w_in.shape
    tm = _row_tile(t)
    row, outs = _norm_bwd_outs(t, d, tm, copy_scale)
    cols = (dp, pl.BlockSpec((tm, ns * pj), lambda i: (i, 0)), lambda ref, j: ref[:, j * pj:(j + 1) * pj])
    return _mm_rows(
        name, [cols], [w_in], NT, t,
        outs=outs,
        epilogue=_norm_bwd_epilogue(copy_scale),
        extras=[(x, row), (g, pl.BlockSpec((1, d), lambda i: (0, 0))), (dres, row)],
        deps=deps,
    )


def _grad_w_in(name, h, dp, ns):
    t, d = h.shape
    pj = dp.shape[1] // ns
    tk = _k_tile(t)
    return _mm(
        name,
        ins=[(h, pl.BlockSpec((tk, d), lambda j, n, k: (k, 0))),
             (dp, pl.BlockSpec((tk, pj), lambda j, n, k: (k, j)))],
        terms=[(0, 0, 1, TN)],
        n_acc=1,
        grid=(ns, 1, t // tk),
        acc_shape=(d, pj),
        outs=[(jax.ShapeDtypeStruct((ns, d, pj), BF16), pl.BlockSpec((None, d, pj), lambda j, n, k: (j, 0, 0)))],
        epilogue=lambda accs, ex: (accs[0],),
    )[0]


def _out_proj(name, mix, w_out, x, g_next):
    t, dm = mix.shape
    d = w_out.shape[1]
    tm = _row_tile(t)
    row = pl.BlockSpec((tm, d), lambda i, n, k: (i, 0))
    return _mm(
        name,
        ins=[(mix, pl.BlockSpec((tm, dm), lambda i, n, k: (i, 0))),
             (w_out, pl.BlockSpec((dm, d), lambda i, n, k: (0, 0)))],
        terms=[(0, 0, 1, NN)],
        n_acc=1,
        grid=(t // tm, 1, 1),
        acc_shape=(tm, d),
        outs=[(jax.ShapeDtypeStruct((t, d), F32), row), (jax.ShapeDtypeStruct((t, d), BF16), row)],
        epilogue=lambda accs, ex: (ex[0] + accs[0], _rmsnorm(ex[0] + accs[0], ex[1])),
        extras=[(x, row), (g_next, pl.BlockSpec((1, d), lambda i, n, k: (0, 0)))],
    )


def _out_proj_bwd(name, dx, w_out, deps=()):
    t, d = dx.shape
    dm = w_out.shape[0]
    tm = _row_tile(t)
    return _mm(
        name,
        ins=[(dx, pl.BlockSpec((tm, d), lambda i, n, k: (i, 0))),
             (w_out, pl.BlockSpec((dm, d), lambda i, n, k: (0, 0)))],
        terms=[(0, 0, 1, NT)],
        n_acc=1,
        grid=(t // tm, 1, 1),
        acc_shape=(tm, dm),
        outs=[(jax.ShapeDtypeStruct((t, dm), F32), pl.BlockSpec((tm, dm), lambda i, n, k: (i, 0)))],
        epilogue=lambda accs, ex: (accs[0],),
        deps=deps,
    )[0]


def _grad_w_out(name, mix, dx):
    t, dm = mix.shape
    d = dx.shape[1]
    tk = _k_tile(t)
    return _mm(
        name,
        ins=[(mix, pl.BlockSpec((tk, dm), lambda a, n, k: (k, 0))),
             (dx, pl.BlockSpec((tk, d), lambda a, n, k: (k, 0)))],
        terms=[(0, 0, 1, TN)],
        n_acc=1,
        grid=(1, 1, t // tk),
        acc_shape=(dm, d),
        outs=[(jax.ShapeDtypeStruct((dm, d), BF16), pl.BlockSpec((dm, d), lambda a, n, k: (0, 0)))],
        epilogue=lambda accs, ex: (accs[0],),
    )[0]


def _head_group_matrix():
    r = lax.broadcasted_iota(jnp.int32, (ATTN_W, ATTN_W), 0)
    c = lax.broadcasted_iota(jnp.int32, (ATTN_W, ATTN_W), 1)
    same = jnp.right_shift(r, 6) == jnp.right_shift(c, 6)
    return jnp.where(same, 1.0, 0.0).astype(BF16)


def _qk_prep(name, proj, gq, gk):
    b, s, _ = proj.shape
    tm = KPAD
    nb = s // tm

    def body(q_ref, k_ref, v_ref, gq_ref, gk_ref, qn_ref, kn_ref, vb_ref):
        j = pl.program_id(1)
        bd = _head_group_matrix()

        def norm(xv, g):
            ms = _dot_exact_rhs(xv * xv, bd, pieces=2) * (1.0 / ATTN_DH)
            return xv * lax.rsqrt(ms + RMS_EPS) * g

        @pl.when(j == 0)
        def _():
            kn_ref[...] = jnp.zeros_like(kn_ref)
            vb_ref[...] = jnp.zeros_like(vb_ref)

        @pl.when(j > 0)
        def _():
            qn_ref[...] = norm(q_ref[...], gq_ref[...]).astype(BF16)
            kn_ref[...] = norm(k_ref[...], gk_ref[...]).astype(BF16)
            vb_ref[...] = v_ref[...].astype(BF16)

    src_blk = lambda col: pl.BlockSpec((None, tm, ATTN_W), lambda bi, j: (bi, jnp.maximum(j - 1, 0), col))
    gspec = pl.BlockSpec((1, ATTN_W), lambda bi, j: (0, 0))
    padded = pl.BlockSpec((None, tm, ATTN_W), lambda bi, j: (bi, j, 0))
    return pl.pallas_call(
        body,
        name=name,
        grid=(b, nb + 1),
        in_specs=[src_blk(0), src_blk(1), src_blk(2), gspec, gspec],
        out_specs=[src_blk(0), padded, padded],
        out_shape=[jax.ShapeDtypeStruct((b, s, ATTN_W), BF16), jax.ShapeDtypeStruct((b, KPAD + s, ATTN_W), BF16),
                   jax.ShapeDtypeStruct((b, KPAD + s, ATTN_W), BF16)],
        compiler_params=_params("parallel", "arbitrary"),
    )(proj, proj, proj, gq, gk)


def _qk_prep_bwd(name, proj, dqn, dkn, dv, gq, gk):
    b, s, _ = proj.shape
    tm = KPAD
    nb = s // tm

    def body(q_ref, k_ref, dqn_ref, dkn_ref, dv_ref, gq_ref, gk_ref, dq_ref, dk_ref, dvb_ref, dgq_ref, dgk_ref):
        bd = _head_group_matrix()

        def bwd(xv, dy, g):
            ms = _dot_exact_rhs(xv * xv, bd, pieces=2) * (1.0 / ATTN_DH)
            rstd = lax.rsqrt(ms + RMS_EPS)
            xhat = xv * rstd
            dxhat = dy * g
            gm = _dot_exact_rhs(dxhat * xhat, bd, pieces=2) * (1.0 / ATTN_DH)
            return rstd * (dxhat - xhat * gm), jnp.sum(dy * xhat, axis=0, keepdims=True)

        dq, dgq = bwd(q_ref[...], dqn_ref[...], gq_ref[...])
        dk, dgk = bwd(k_ref[...], dkn_ref[...], gk_ref[...])
        dq_ref[...] = dq.astype(BF16)
        dk_ref[...] = dk.astype(BF16)
        dvb_ref[...] = dv_ref[...].astype(BF16)
        dgq_ref[...] = dgq
        dgk_ref[...] = dgk

    col = lambda c: pl.BlockSpec((None, tm, ATTN_W), lambda bi, j: (bi, j, c))
    past_pad = pl.BlockSpec((None, tm, ATTN_W), lambda bi, j: (bi, j + 1, 0))
    gspec = pl.BlockSpec((1, ATTN_W), lambda bi, j: (0, 0))
    pspec = pl.BlockSpec((None, 1, ATTN_W), lambda bi, j: (bi * nb + j, 0, 0))
    o_shape = jax.ShapeDtypeStruct((b, s, ATTN_W), BF16)
    p_shape = jax.ShapeDtypeStruct((b * nb, 1, ATTN_W), F32)
    return pl.pallas_call(
        body,
        name=name,
        grid=(b, nb),
        in_specs=[col(0), col(1), col(0), past_pad, past_pad, gspec, gspec],
        out_specs=[col(0)] * 3 + [pspec] * 2,
        out_shape=[o_shape] * 3 + [p_shape] * 2,
        compiler_params=_params("parallel", "parallel"),
    )(proj, proj, dqn, dkn, dv, gq, gk)


Q_CHUNKS = 4
QBLK = Q_CHUNKS * CHUNK
WIN = (LEFT_CHUNKS + Q_CHUNKS) * CHUNK
DB_W = BAND + CHUNK
MASKED = -1e30


def _band_table(bias):
    rows = [jnp.pad(bias, ((0, 0), (0, 0), (CHUNK * i, WIN - BAND - CHUNK * i)), constant_values=MASKED)
            for i in range(Q_CHUNKS)]
    return jnp.concatenate(rows, axis=1)


def _head_lanes(hh):
    lane = lax.broadcasted_iota(jnp.int32, (1, LANES), 1)
    return (lane < ATTN_DH) if hh == 0 else (lane >= ATTN_DH)


def _attn_probs(qh, kw, table, start):
    s = _dot(qh, kw, NT) * (ATTN_DH ** -0.5) + table
    col = lax.broadcasted_iota(jnp.int32, (QBLK, WIN), 1)
    s = jnp.where(col + start >= KPAD, s, MASKED)
    m = jnp.max(s, axis=-1, keepdims=True)
    p = jnp.exp(s - m)
    return p * (1.0 / jnp.sum(p, axis=-1, keepdims=True))


def _attn_fwd(name, q, k, v, table):
    b, s, w = q.shape
    sp = k.shape[1]

    def body(q_ref, k_ref, v_ref, t_ref, o_ref):
        start = pl.multiple_of(pl.program_id(2) * QBLK, QBLK)
        kw = k_ref[pl.ds(start, WIN), :]
        vw = v_ref[pl.ds(start, WIN), :]
        q2 = q_ref[...]
        lanes = [_head_lanes(hh) for hh in range(2)]
        probs = [_attn_probs(jnp.where(mine, q2, jnp.zeros_like(q2)), kw, t_ref[hh], start).astype(BF16)
                 for hh, mine in enumerate(lanes)]
        outs = [_dot(p, vw) for p in probs]
        o_ref[...] = jnp.where(lanes[0], outs[0], outs[1]).astype(BF16)

    qspec = pl.BlockSpec((None, QBLK, LANES), lambda p, bi, i: (bi, i, p))
    kspec = pl.BlockSpec((None, sp, LANES), lambda p, bi, i: (bi, 0, p))
    return pl.pallas_call(
        body,
        name=name,
        grid=(w // LANES, b, s // QBLK),
        in_specs=[qspec, kspec, kspec, pl.BlockSpec((2, QBLK, WIN), lambda p, bi, i: (p, 0, 0))],
        out_specs=qspec,
        out_shape=jax.ShapeDtypeStruct((b, s, w), BF16),
        compiler_params=_params("parallel", "parallel", "arbitrary"),
    )(q, k, v, table)


def _attn_bwd(name, q, k, v, table, dmix):
    b, s, w = q.shape
    sp = k.shape[1]

    def body(q_ref, k_ref, v_ref, t_ref, do_ref, dq_ref, dk_ref, dv_ref, dbe_ref, dbo_ref):
        bi = pl.program_id(1)
        i = pl.program_id(2)
        start = pl.multiple_of(i * QBLK, QBLK)
        win = pl.ds(start, WIN)

        @pl.when(i == 0)
        def _():
            dk_ref[...] = jnp.zeros_like(dk_ref)
            dv_ref[...] = jnp.zeros_like(dv_ref)

        @pl.when(jnp.logical_and(i == 0, bi == 0))
        def _():
            dbe_ref[...] = jnp.zeros_like(dbe_ref)
            dbo_ref[...] = jnp.zeros_like(dbo_ref)

        kw = k_ref[win, :]
        vw = v_ref[win, :]
        q2 = q_ref[...]
        do2 = do_ref[...].astype(BF16)
        lanes = [_head_lanes(hh) for hh in range(2)]
        qh = [jnp.where(mine, q2, jnp.zeros_like(q2)) for mine in lanes]
        doh = [jnp.where(mine, do2, jnp.zeros_like(do2)) for mine in lanes]
        p = [_attn_probs(qh[hh], kw, t_ref[hh], start) for hh in range(2)]
        dp = [_dot(doh[hh], vw, NT) for hh in range(2)]
        ds = [p[hh] * (dp[hh] - jnp.sum(p[hh] * dp[hh], axis=-1, keepdims=True)) for hh in range(2)]
        dsb = [(x * (ATTN_DH ** -0.5)).astype(BF16) for x in ds]
        pb = [x.astype(BF16) for x in p]
        dq = [_dot(dsb[hh], kw) for hh in range(2)]
        dk = [_dot(dsb[hh], qh[hh], TN) for hh in range(2)]
        dv = [_dot(pb[hh], doh[hh], TN) for hh in range(2)]
        for hh in range(2):
            for qi in range(Q_CHUNKS):
                c0 = (qi // 2) * LANES
                blk = ds[hh][qi * CHUNK:(qi + 1) * CHUNK, c0:c0 + DB_W]
                if qi % 2 == 0:
                    dbe_ref[hh] += blk
                else:
                    dbo_ref[hh] += blk
        dq_ref[...] = jnp.where(lanes[0], dq[0], dq[1])
        dk_ref[win, :] += dk[0] + dk[1]
        dv_ref[win, :] += dv[0] + dv[1]

    qspec = pl.BlockSpec((None, QBLK, LANES), lambda p, bi, i: (bi, i, p))
    kspec = pl.BlockSpec((None, sp, LANES), lambda p, bi, i: (bi, 0, p))
    dbspec = pl.BlockSpec((2, CHUNK, DB_W), lambda p, bi, i: (p, 0, 0))
    db_shape = jax.ShapeDtypeStruct((ATTN_HEADS, CHUNK, DB_W), F32)
    return pl.pallas_call(
        body,
        name=name,
        grid=(w // LANES, b, s // QBLK),
        in_specs=[qspec, kspec, kspec, pl.BlockSpec((2, QBLK, WIN), lambda p, bi, i: (p, 0, 0)), qspec],
        out_specs=[qspec, kspec, kspec, dbspec, dbspec],
        out_shape=[jax.ShapeDtypeStruct((b, s, w), F32), jax.ShapeDtypeStruct((b, sp, w), F32),
                   jax.ShapeDtypeStruct((b, sp, w), F32), db_shape, db_shape],
        compiler_params=_params("arbitrary", "arbitrary", "arbitrary"),
    )(q, k, v, table, dmix)


HQ_COL = 3 * ATTN_W // HGRN_DH
HF_COL = HQ_COL + HGRN_HEADS
HI_COL = HF_COL + HGRN_HEADS
HG_COL = HI_COL + HGRN_HEADS
HGRN_ROWS = 8 * CHUNK
HEAD_LANES = [slice(hh * HGRN_DH, (hh + 1) * HGRN_DH) for hh in range(HGRN_HEADS)]


def _tri(lower):
    r = lax.broadcasted_iota(jnp.int32, (CHUNK, CHUNK), 0)
    c = lax.broadcasted_iota(jnp.int32, (CHUNK, CHUNK), 1)
    return (r >= c) if lower else (r <= c)


def _hgrn_chunk(hq, hf, lb, tril):
    sig = _sigmoid(hf)
    f = lb + (1.0 - lb) * sig
    g = jnp.log(f)
    ones_l = jnp.where(tril, 1.0, 0.0).astype(BF16)
    b = _dot_exact_lhs(ones_l, g)
    bl = jnp.sum(g, axis=0, keepdims=True)
    rows = lax.broadcasted_iota(jnp.int32, g.shape, 0)
    bm = jnp.sum(jnp.where(rows <= CHUNK // 2, g, 0.0), axis=0, keepdims=True)
    sq = _sigmoid(hq)
    q = hq * sq
    k = 1.0 - f
    return sig, f, b, bl, bm, sq, q, k


def _hgrn_fwd(name, proj, attn, lb, go, b, s):
    nc = s // CHUNK
    t = b * s
    nblk = s // HGRN_ROWS
    cpb = HGRN_ROWS // CHUNK

    def body(hq_ref, hf_ref, hi_ref, hg_ref, attn_ref, lb_ref, go_ref, mix_ref, oraw_ref, st_ref, s_scr):
        tril = _tri(True)
        gov = go_ref[...]
        mix_ref[:, 0:ATTN_W] = attn_ref[...]

        @pl.when(pl.program_id(1) == 0)
        def _():
            s_scr[...] = jnp.zeros_like(s_scr)

        def step(c, carry):
            sl = pl.ds(pl.multiple_of(c * CHUNK, CHUNK), CHUNK)
            hg = hg_ref[sl, :]
            _, _, bb, bl, bm, _, q, k = _hgrn_chunk(hq_ref[sl, :], hf_ref[sl, :], lb_ref[...], tril)
            vb = hi_ref[sl, :].astype(BF16)
            qe = (q * jnp.exp(bb - bm)).astype(BF16)
            ke = (k * jnp.exp(bm - bb)).astype(BF16)
            qb = (q * jnp.exp(bb)).astype(BF16)
            kb = (k * jnp.exp(bl - bb)).astype(BF16)
            e_last = jnp.exp(bl)
            gate = _silu(hg)
            st = [s_scr[hh] for hh in range(HGRN_HEADS)]
            a = [jnp.where(tril, _dot(qe[:, hs], ke[:, hs], NT), 0.0).astype(BF16) for hs in HEAD_LANES]
            o_state = [_dot(qb[:, hs], st[hh].astype(BF16), NT) for hh, hs in enumerate(HEAD_LANES)]
            st_next = [st[hh] * e_last[:, hs] + _dot(vb[:, hs], kb[:, hs], TN) for hh, hs in enumerate(HEAD_LANES)]
            o = [_dot(a[hh], vb[:, hs]) + o_state[hh] for hh, hs in enumerate(HEAD_LANES)]
            ro = [(oh * lax.rsqrt(jnp.mean(oh * oh, axis=-1, keepdims=True) + RMS_EPS) * gov) * gate[:, hs]
                  for oh, hs in zip(o, HEAD_LANES)]
            for hh in range(HGRN_HEADS):
                st_ref[hh, c] = st[hh]
                s_scr[hh] = st_next[hh]
            mix_ref[sl, ATTN_W:ATTN_W + HGRN_W] = jnp.concatenate(ro, axis=1).astype(BF16)
            oraw_ref[sl, :] = jnp.concatenate(o, axis=1)
            return carry

        lax.fori_loop(0, cpb, step, 0)

    col = lambda base: pl.BlockSpec((HGRN_ROWS, HGRN_W), lambda bi, i: (bi * nblk + i, base // HGRN_HEADS))
    out = pl.BlockSpec((HGRN_ROWS, HGRN_W), lambda bi, i: (bi * nblk + i, 0))
    return pl.pallas_call(
        body,
        name=name,
        grid=(b, nblk),
        in_specs=[col(HQ_COL), col(HF_COL), col(HI_COL), col(HG_COL), out,
                  pl.BlockSpec((1, HGRN_W), lambda bi, i: (0, 0)), pl.BlockSpec((1, HGRN_DH), lambda bi, i: (0, 0))],
        out_specs=[pl.BlockSpec((HGRN_ROWS, ATTN_W + HGRN_W), lambda bi, i: (bi * nblk + i, 0)), out,
                   pl.BlockSpec((None, HGRN_HEADS, cpb, HGRN_DH, HGRN_DH), lambda bi, i: (bi, 0, i, 0, 0))],
        out_shape=[jax.ShapeDtypeStruct((t, ATTN_W + HGRN_W), BF16), jax.ShapeDtypeStruct((t, HGRN_W), F32),
                   jax.ShapeDtypeStruct((b, HGRN_HEADS, nc, HGRN_DH, HGRN_DH), F32)],
        scratch_shapes=[pltpu.VMEM((HGRN_HEADS, HGRN_DH, HGRN_DH), F32)],
        compiler_params=_params("parallel", "arbitrary"),
    )(proj, proj, proj, proj, attn, lb, go)


def _hgrn_bwd(name, proj, dqkv, lb, go, oraw, states, dmix, b, s):
    t = b * s
    nblk = s // HGRN_ROWS
    cpb = HGRN_ROWS // CHUNK

    def body(hq_ref, hf_ref, hi_ref, hg_ref, dq_ref, dk_ref, dv_ref, lb_ref, go_ref, oraw_ref, st_ref, dro_ref,
             dp_ref, dlb_ref, dgo_ref, ds_scr, dlb_scr, dgo_scr):
        tril = _tri(True)
        ones_u = jnp.where(_tri(False), 1.0, 0.0).astype(BF16)
        gov = go_ref[...]
        dp_ref[:, 0:ATTN_W] = dq_ref[...]
        dp_ref[:, ATTN_W:2 * ATTN_W] = dk_ref[...]
        dp_ref[:, 2 * ATTN_W:3 * ATTN_W] = dv_ref[...]

        @pl.when(pl.program_id(1) == 0)
        def _():
            ds_scr[...] = jnp.zeros_like(ds_scr)
            dlb_scr[...] = jnp.zeros_like(dlb_scr)
            dgo_scr[...] = jnp.zeros_like(dgo_scr)

        def step(ci, carry):
            c = cpb - 1 - ci
            sl = pl.ds(pl.multiple_of(c * CHUNK, CHUNK), CHUNK)
            hq = hq_ref[sl, :]
            hg = hg_ref[sl, :]
            sig, f, bb, bl, bm, sq, q, k = _hgrn_chunk(hq, hf_ref[sl, :], lb_ref[...], tril)
            vb = hi_ref[sl, :].astype(BF16)
            ebm = jnp.exp(bb - bm)
            embm = jnp.exp(bm - bb)
            eb = jnp.exp(bb)
            ebl = jnp.exp(bl - bb)
            e_last = jnp.exp(bl)
            qe = (q * ebm).astype(BF16)
            ke = (k * embm).astype(BF16)
            qb = (q * eb).astype(BF16)
            kb = (k * ebl).astype(BF16)
            st = [st_ref[hh, c] for hh in range(HGRN_HEADS)]
            dst = [ds_scr[hh] for hh in range(HGRN_HEADS)]
            o = oraw_ref[sl, :]
            dro = dro_ref[sl, :]
            sg = _sigmoid(hg)
            gov4 = jnp.concatenate([gov] * HGRN_HEADS, axis=1)
            rstd = jnp.concatenate(
                [jnp.broadcast_to(lax.rsqrt(jnp.mean(o[:, hs] * o[:, hs], axis=-1, keepdims=True) + RMS_EPS),
                                  (CHUNK, HGRN_DH)) for hs in HEAD_LANES], axis=1)
            ohat = o * rstd
            dn = dro * (hg * sg)
            dhg = dro * (ohat * gov4) * (sg * (1.0 + hg * (1.0 - sg)))
            dgo_inc = jnp.sum(dn * ohat, axis=0, keepdims=True)
            dohat = dn * gov4
            proj_h = dohat * ohat
            pm = jnp.concatenate(
                [jnp.broadcast_to(jnp.mean(proj_h[:, hs], axis=-1, keepdims=True), (CHUNK, HGRN_DH))
                 for hs in HEAD_LANES], axis=1)
            dob = (rstd * (dohat - ohat * pm)).astype(BF16)
            stb = [x.astype(BF16) for x in st]
            dstb = [x.astype(BF16) for x in dst]
            a = [jnp.where(tril, _dot(qe[:, hs], ke[:, hs], NT), 0.0).astype(BF16) for hs in HEAD_LANES]
            dab = [jnp.where(tril, _dot(dob[:, hs], vb[:, hs], NT), 0.0).astype(BF16) for hs in HEAD_LANES]
            dqb = [_dot(dob[:, hs], stb[hh]) for hh, hs in enumerate(HEAD_LANES)]
            dkb = [_dot(vb[:, hs], dstb[hh]) for hh, hs in enumerate(HEAD_LANES)]
            dv_state = [_dot(kb[:, hs], dstb[hh], NT) for hh, hs in enumerate(HEAD_LANES)]
            dst_next = [dst[hh] * e_last[:, hs] + _dot(dob[:, hs], qb[:, hs], TN) for hh, hs in enumerate(HEAD_LANES)]
            dv = [_dot(a[hh], dob[:, hs], TN) + dv_state[hh] for hh, hs in enumerate(HEAD_LANES)]
            dqe = jnp.concatenate([_dot(dab[hh], ke[:, hs]) for hh, hs in enumerate(HEAD_LANES)], axis=1)
            dke = jnp.concatenate([_dot(dab[hh], qe[:, hs], TN) for hh, hs in enumerate(HEAD_LANES)], axis=1)
            dqb = jnp.concatenate(dqb, axis=1)
            dkb = jnp.concatenate(dkb, axis=1)
            state_term = jnp.concatenate(
                [jnp.sum(dst[hh] * st[hh], axis=0, keepdims=True) for hh in range(HGRN_HEADS)], axis=1)
            dq = dqe * ebm + dqb * eb
            dk = dke * embm + dkb * ebl
            db = (qe.astype(F32) * dqe - ke.astype(F32) * dke) + q * (dqb * eb) - k * (dkb * ebl)
            d_last = jnp.sum(k * ebl * dkb, axis=0, keepdims=True) + state_term * e_last
            dg = _dot_exact_lhs(ones_u, db) + d_last
            df = dg / f - dk
            first = HQ_COL * HGRN_DH
            dp_ref[sl, first:first + HGRN_W] = (dq * (sq * (1.0 + hq * (1.0 - sq)))).astype(BF16)
            dp_ref[sl, first + HGRN_W:first + 2 * HGRN_W] = (df * (1.0 - lb_ref[...]) * sig * (1.0 - sig)).astype(BF16)
            dp_ref[sl, first + 2 * HGRN_W:first + 3 * HGRN_W] = jnp.concatenate(dv, axis=1).astype(BF16)
            dp_ref[sl, first + 3 * HGRN_W:first + 4 * HGRN_W] = dhg.astype(BF16)
            dlb_scr[...] += jnp.sum(df * (1.0 - sig), axis=0, keepdims=True)
            dgo_scr[...] += dgo_inc
            for hh in range(HGRN_HEADS):
                ds_scr[hh] = dst_next[hh]
            return carry

        lax.fori_loop(0, cpb, step, 0)

        @pl.when(pl.program_id(1) == nblk - 1)
        def _():
            dlb_ref[...] = dlb_scr[...]
            dgo_ref[...] = dgo_scr[...]

    rows = lambda bi, i: bi * nblk + (nblk - 1 - i)
    col = lambda base: pl.BlockSpec((HGRN_ROWS, HGRN_W), lambda bi, i: (rows(bi, i), base // HGRN_HEADS))
    out = pl.BlockSpec((HGRN_ROWS, HGRN_W), lambda bi, i: (rows(bi, i), 0))
    part = pl.BlockSpec((None, 1, HGRN_W), lambda bi, i: (bi, 0, 0))
    width = HG_COL * HGRN_DH + HGRN_W
    o_shape = jax.ShapeDtypeStruct((t, width), BF16)
    p_shape = jax.ShapeDtypeStruct((b, 1, HGRN_W), F32)
    return pl.pallas_call(
        body,
        name=name,
        grid=(b, nblk),
        in_specs=[col(HQ_COL), col(HF_COL), col(HI_COL), col(HG_COL), out, out, out,
                  pl.BlockSpec((1, HGRN_W), lambda bi, i: (0, 0)), pl.BlockSpec((1, HGRN_DH), lambda bi, i: (0, 0)), out,
                  pl.BlockSpec((None, HGRN_HEADS, cpb, HGRN_DH, HGRN_DH), lambda bi, i: (bi, 0, nblk - 1 - i, 0, 0)),
                  col(ATTN_W // HGRN_DH)],
        out_specs=[pl.BlockSpec((HGRN_ROWS, width), lambda bi, i: (rows(bi, i), 0))] + [part] * 2,
        out_shape=[o_shape] + [p_shape] * 2,
        scratch_shapes=[pltpu.VMEM((HGRN_HEADS, HGRN_DH, HGRN_DH), F32), pltpu.VMEM((1, HGRN_W), F32),
                        pltpu.VMEM((1, HGRN_W), F32)],
        compiler_params=_params("parallel", "arbitrary"),
    )(proj, proj, proj, proj, *dqkv, lb, go, oraw, states, dmix)


def _small_grads(name, dg1, dgm, dg2, dgq, dgk, dbias_t, dlb, dgo, lbp):
    d = dg1.shape[1]

    def body(dg1_ref, dgm_ref, dg2_ref, dgq_ref, dgk_ref, dbias_ref, dlb_ref, dgo_ref, lbp_ref,
             g1_ref, gm_ref, g2_ref, gq_ref, gk_ref, rb_ref, lbg_ref, go_ref):
        g1_ref[...] = jnp.sum(dg1_ref[...], axis=0, keepdims=True)
        gm_ref[...] = jnp.sum(dgm_ref[...], axis=0, keepdims=True)
        g2_ref[...] = jnp.sum(dg2_ref[...], axis=0, keepdims=True)
        r = lax.broadcasted_iota(jnp.int32, (ATTN_W, ATTN_DH), 0)
        cidx = lax.broadcasted_iota(jnp.int32, (ATTN_W, ATTN_DH), 1)
        fold = jnp.where(jnp.bitwise_and(r, ATTN_DH - 1) == cidx, 1.0, 0.0).astype(BF16)
        gq_ref[...] = jnp.sum(_dot_exact_rhs(dgq_ref[...], fold), axis=0, keepdims=True)
        gk_ref[...] = jnp.sum(_dot_exact_rhs(dgk_ref[...], fold), axis=0, keepdims=True)
        gosum = jnp.sum(dgo_ref[...], axis=0, keepdims=True)
        go_ref[...] = (gosum[:, 0:HGRN_DH] + gosum[:, HGRN_DH:2 * HGRN_DH]
                       + gosum[:, 2 * HGRN_DH:3 * HGRN_DH] + gosum[:, 3 * HGRN_DH:4 * HGRN_DH])
        p0 = lbp_ref[0:1, :]
        p1 = lbp_ref[1:2, :]
        lbv = 1.0 / (1.0 + jnp.exp(p1 - p0))
        dp0 = jnp.sum(dlb_ref[...], axis=0, keepdims=True) * lbv * (1.0 - lbv)
        lbg_ref[0:1, :] = dp0
        lbg_ref[1:2, :] = -dp0
        sidx = lax.broadcasted_iota(jnp.int32, (BAND, N_REL_PAD), 0)
        ridx = lax.broadcasted_iota(jnp.int32, (BAND, N_REL_PAD), 1)

        def step(tq, acc):
            rel = jnp.clip(tq + KPAD - sidx, -REL_CLIP, REL_CLIP) + REL_CLIP
            onehot = jnp.where(rel == ridx, 1.0, 0.0).astype(BF16)
            return acc + _dot_exact_rhs(dbias_ref[tq], onehot)

        rb_ref[...] = lax.fori_loop(0, CHUNK, step, jnp.zeros((ATTN_HEADS, N_REL_PAD), F32))

    ins = [dg1, dgm, dg2, dgq, dgk, dbias_t, dlb, dgo, lbp]
    outs = [jax.ShapeDtypeStruct((1, d), F32)] * 3 + [jax.ShapeDtypeStruct((1, ATTN_DH), F32)] * 2 + [
        jax.ShapeDtypeStruct((ATTN_HEADS, N_REL_PAD), F32), jax.ShapeDtypeStruct((2, HGRN_W), F32),
        jax.ShapeDtypeStruct((1, HGRN_DH), F32)]
    vm = pl.BlockSpec(memory_space=pltpu.VMEM)
    return pl.pallas_call(
        body,
        name=name,
        in_specs=[vm] * len(ins),
        out_specs=[vm] * len(outs),
        out_shape=outs,
        compiler_params=pltpu.CompilerParams(vmem_limit_bytes=VMEM_LIMIT),
    )(*ins)


def _adam_update(w, g, m, v):
    m2 = ADAM_B1 * m + (1.0 - ADAM_B1) * g
    v2 = ADAM_B2 * v + (1.0 - ADAM_B2) * (g * g)
    m_hat = m2 / (1.0 - ADAM_B1 ** ADAM_STEP)
    v_hat = v2 / (1.0 - ADAM_B2 ** ADAM_STEP)
    delta = -ADAM_LR * (m_hat / (jnp.sqrt(v_hat) + ADAM_EPS) + ADAM_WD * w)
    return delta, m2, v2


def _rows_tile(r):
    return r if r <= 512 or r % 512 else 512


def _pair_sum(name, grad, theirs, core):
    n, half, c = theirs.shape
    tr = _rows_tile(half)
    nth = half // tr

    def body(core_ref, a_ref, b_ref, o_ref):
        o_ref[...] = (a_ref[...].astype(F32) + b_ref[...].astype(F32)).astype(o_ref.dtype)

    spec = pl.BlockSpec((None, tr, c), lambda i, j, core_ref: (i, j, 0))
    return pl.pallas_call(
        body, name=name,
        grid_spec=pltpu.PrefetchScalarGridSpec(
            num_scalar_prefetch=1, grid=(n, nth),
            in_specs=[pl.BlockSpec((None, tr, c), lambda i, j, core_ref: (i, core_ref[0] * nth + j, 0)), spec],
            out_specs=spec),
        out_shape=jax.ShapeDtypeStruct((n, half, c), BF16), compiler_params=_params("parallel", "parallel"),
    )(core, grad, theirs)


def _chip_sum(name, own, parts, chip):
    _, half, c = own.shape
    tr = _rows_tile(half)

    def body(chip_ref, own_ref, p_ref, o_ref):
        me = chip_ref[0]
        mine = own_ref[...].astype(F32)
        flip_x, flip_y, flip_xy = (p_ref[i].astype(F32) for i in range(3))
        acc = None
        for k in range(N_CHIPS):
            rel = jnp.bitwise_xor(me, k)
            term = jnp.where(rel == 0, mine, jnp.where(rel == 2, flip_x, jnp.where(rel == 1, flip_y, flip_xy)))
            acc = term if acc is None else acc + term
        o_ref[...] = acc

    return pl.pallas_call(
        body, name=name,
        grid_spec=pltpu.PrefetchScalarGridSpec(
            num_scalar_prefetch=1, grid=(half // tr,),
            in_specs=[pl.BlockSpec((None, tr, c), lambda j, chip_ref: (chip_ref[0], j, 0)),
                      pl.BlockSpec((3, tr, c), lambda j, chip_ref: (0, j, 0))],
            out_specs=pl.BlockSpec((tr, c), lambda j, chip_ref: (j, 0))),
        out_shape=jax.ShapeDtypeStruct((half, c), F32), compiler_params=_params("parallel"),
    )(chip, own, parts)


def _adamw(name, w, g_mine, g_theirs, m, v, core):
    _, r, c = w.shape
    half = r // 2
    tr = _rows_tile(half)
    nth = half // tr

    def body(core_ref, w_ref, gm_ref, gt_ref, m_ref, v_ref, g_ref, d_ref, m2_ref, v2_ref):
        g = jnp.where(pl.program_id(0) == core_ref[0], gm_ref[...], gt_ref[...])
        delta, m2, v2 = _adam_update(w_ref[...], g, m_ref[...], v_ref[...])
        g_ref[...] = g
        d_ref[...] = delta
        m2_ref[...] = m2
        v2_ref[...] = v2

    full = pl.BlockSpec((None, tr, c), lambda h, j, core_ref: (0, h * nth + j, 0))
    part = pl.BlockSpec((tr, c), lambda h, j, core_ref: (j, 0))
    shape = jax.ShapeDtypeStruct((1, r, c), F32)
    return pl.pallas_call(
        body, name=name,
        grid_spec=pltpu.PrefetchScalarGridSpec(
            num_scalar_prefetch=1, grid=(2, nth), in_specs=[full, part, part, full, full], out_specs=[full] * 4),
        out_shape=[shape] * 4, compiler_params=_params("parallel", "parallel"),
    )(core, w, g_mine, g_theirs, m, v)


def _rel_bias_table(name, rel_bias):
    padded = jnp.pad(rel_bias, ((0, 0), (0, N_REL_PAD - N_REL)))

    def body(rb_ref, o_ref):
        ridx = lax.broadcasted_iota(jnp.int32, (N_REL_PAD, BAND), 0)
        sidx = lax.broadcasted_iota(jnp.int32, (N_REL_PAD, BAND), 1)
        rb = rb_ref[...]

        def step(tq, carry):
            rel = jnp.clip(tq + KPAD - sidx, -REL_CLIP, REL_CLIP) + REL_CLIP
            onehot = jnp.where(rel == ridx, 1.0, 0.0).astype(BF16)
            o_ref[tq] = _dot_exact_rhs(rb, onehot)
            return carry

        lax.fori_loop(0, CHUNK, step, 0)

    vm = pl.BlockSpec(memory_space=pltpu.VMEM)
    table = pl.pallas_call(
        body, name=name, in_specs=[vm], out_specs=vm,
        out_shape=jax.ShapeDtypeStruct((CHUNK, ATTN_HEADS, BAND), F32),
    )(padded)
    return table.transpose(1, 0, 2)


def _adamw_small(name, w, parts, m, v):
    def body(w_ref, p_ref, m_ref, v_ref, g_ref, d_ref, m2_ref, v2_ref):
        g = p_ref[0]
        for i in range(1, N_DEV):
            g = g + p_ref[i]
        delta, m2, v2 = _adam_update(w_ref[...], g, m_ref[...], v_ref[...])
        g_ref[...] = g
        d_ref[...] = delta
        m2_ref[...] = m2
        v2_ref[...] = v2

    vm = pl.BlockSpec(memory_space=pltpu.VMEM)
    shape = jax.ShapeDtypeStruct((SMALL_ROWS, SMALL_COLS), F32)
    return pl.pallas_call(
        body, name=name, in_specs=[vm] * 4, out_specs=[vm] * 4, out_shape=[shape] * 4,
    )(w, parts, m, v)


def _position():
    return lax.axis_index("x"), lax.axis_index("y"), lax.axis_index("c")


def _other_chips(x, y):
    return [(1 - x, y), (x, 1 - y), (1 - x, 1 - y)]


ANY = pl.BlockSpec(memory_space=pl.ANY)


HBM = pl.BlockSpec(memory_space=pltpu.HBM)
SEM = pl.BlockSpec(memory_space=pltpu.SEMAPHORE)
SPLIT_COPY = pltpu.SideEffectType.DATAFLOW_SIDE_EFFECTING


def _gather_copy(shards, outs, send_sem, recv_sem, i, j):
    x, y, c = _position()
    chips = _other_chips(x, y)
    half = shards[i].shape[0] // 2
    rows = pl.ds(pl.multiple_of(c * half, 16), half)
    return pltpu.make_async_remote_copy(
        src_ref=shards[i].at[rows, :], dst_ref=outs[i].at[2 * x + y, rows, :],
        send_sem=send_sem.at[3 * i + j], recv_sem=recv_sem.at[3 * i + j],
        device_id=(chips[j][0], chips[j][1], c), device_id_type=MESH)


def _gather_start(name, shards, after):
    n = len(shards)

    def body(*refs):
        srcs, outs = refs[:n], refs[n:2 * n]
        send_sem, recv_sem = refs[2 * n + len(after)], refs[2 * n + len(after) + 1]
        token = refs[-1]
        for i in range(n):
            for j in range(3):
                _gather_copy(srcs, outs, send_sem, recv_sem, i, j).start()
        token[...] = jnp.zeros_like(token)

    full = [(N_CHIPS,) + s.shape for s in shards]
    res = pl.pallas_call(
        body,
        name=name,
        in_specs=[HBM] * (2 * n) + [ANY] * len(after),
        out_specs=[SEM, SEM] + [HBM] * (2 * n) + [pl.BlockSpec(memory_space=pltpu.VMEM)],
        out_shape=[pltpu.SemaphoreType.DMA((3 * n,)), pltpu.SemaphoreType.DMA((3 * n,))]
        + [pltpu.HBM(s.shape, s.dtype) for s in shards]
        + [pltpu.HBM(shp, s.dtype) for shp, s in zip(full, shards)]
        + [jax.ShapeDtypeStruct((8, LANES), F32)],
        input_output_aliases={i: 2 + i for i in range(2 * n)},
        compiler_params=pltpu.CompilerParams(has_side_effects=SPLIT_COPY),
    )(*[pltpu.with_memory_space_constraint(s, pltpu.HBM) for s in shards],
      *[pltpu.with_memory_space_constraint(lax.empty(shp, s.dtype), pltpu.HBM) for shp, s in zip(full, shards)],
      *after)
    return res[0], res[1], list(res[2:2 + n]), list(res[2 + n:2 + 2 * n]), res[-1]


def _gather_wait(name, send_sem, recv_sem, shards, outs, after):
    n = len(shards)

    def body(*refs):
        srcs, out_refs = refs[:n], refs[n:2 * n]
        send_ref, recv_ref = refs[2 * n], refs[2 * n + 1]
        for i in range(n):
            for j in range(3):
                copy = _gather_copy(srcs, out_refs, send_ref, recv_ref, i, j)
                copy.wait_send()
                copy.wait_recv()

    res = pl.pallas_call(
        body,
        name=name,
        in_specs=[HBM] * (2 * n) + [SEM, SEM] + [ANY] * len(after),
        out_specs=[HBM] * (2 * n),
        out_shape=[pltpu.HBM(s.shape, s.dtype) for s in shards] + [pltpu.HBM(o.shape, o.dtype) for o in outs],
        input_output_aliases={i: i for i in range(2 * n)},
        compiler_params=pltpu.CompilerParams(has_side_effects=SPLIT_COPY),
    )(*shards, *outs, send_sem, recv_sem, *after)
    return list(res[:n]), list(res[n:])


def _gather_join(name, shards, outs):
    n = len(shards)

    def body(*refs):
        srcs, ins, outs_ = refs[:n], refs[n:2 * n], refs[2 * n:3 * n]
        own_send, own_recv, half_send, half_recv = refs[3 * n:]
        x, y, c = _position()
        chips = _other_chips(x, y)
        copies = []
        for i in range(n):
            copies.append(pltpu.make_async_remote_copy(
                src_ref=srcs[i], dst_ref=outs_[i].at[2 * x + y], send_sem=own_send.at[i], recv_sem=own_recv.at[i],
                device_id=(x, y, 1 - c), device_id_type=MESH))
            half = srcs[i].shape[0] // 2
            rows = pl.ds(pl.multiple_of(c * half, 16), half)
            for j in range(3):
                slot = 2 * chips[j][0] + chips[j][1]
                copies.append(pltpu.make_async_remote_copy(
                    src_ref=ins[i].at[slot, rows, :], dst_ref=outs_[i].at[slot, rows, :],
                    send_sem=half_send.at[3 * i + j], recv_sem=half_recv.at[3 * i + j],
                    device_id=(x, y, 1 - c), device_id_type=MESH))
        for cp in copies:
            cp.start()
        for cp in copies:
            cp.wait()

    return pl.pallas_call(
        body,
        name=name,
        in_specs=[ANY] * (2 * n),
        out_specs=[ANY] * n,
        out_shape=[jax.ShapeDtypeStruct(o.shape, o.dtype) for o in outs],
        input_output_aliases={n + i: i for i in range(n)},
        scratch_shapes=[pltpu.SemaphoreType.DMA((n,))] * 2 + [pltpu.SemaphoreType.DMA((3 * n,))] * 2,
    )(*shards, *outs)


def _pair_copy(grads, lands, send_sem, recv_sem, i):
    x, y, c = _position()
    half = grads[i].shape[1] // 2
    give = pl.ds(pl.multiple_of((1 - c) * half, 16), half)
    return pltpu.make_async_remote_copy(
        src_ref=grads[i].at[:, give, :], dst_ref=lands[i], send_sem=send_sem.at[i], recv_sem=recv_sem.at[i],
        device_id=(x, y, 1 - c), device_id_type=MESH)


def _pair_start(name, grads):
    n = len(grads)

    def body(*refs):
        srcs, lands = refs[:n], refs[n:2 * n]
        send_sem, recv_sem = refs[2 * n], refs[2 * n + 1]
        token = refs[-1]
        for i in range(n):
            _pair_copy(srcs, lands, send_sem, recv_sem, i).start()
        token[...] = jnp.zeros_like(token)

    halves = [(g.shape[0], g.shape[1] // 2, g.shape[2]) for g in grads]
    res = pl.pallas_call(
        body,
        name=name,
        in_specs=[HBM] * (2 * n),
        out_specs=[SEM, SEM] + [HBM] * (2 * n) + [pl.BlockSpec(memory_space=pltpu.VMEM)],
        out_shape=[pltpu.SemaphoreType.DMA((n,)), pltpu.SemaphoreType.DMA((n,))]
        + [pltpu.HBM(g.shape, g.dtype) for g in grads]
        + [pltpu.HBM(shp, g.dtype) for shp, g in zip(halves, grads)]
        + [jax.ShapeDtypeStruct((8, LANES), F32)],
        input_output_aliases={i: 2 + i for i in range(2 * n)},
        compiler_params=pltpu.CompilerParams(has_side_effects=SPLIT_COPY),
    )(*[pltpu.with_memory_space_constraint(g, pltpu.HBM) for g in grads],
      *[pltpu.with_memory_space_constraint(lax.empty(shp, g.dtype), pltpu.HBM) for shp, g in zip(halves, grads)])
    return res[0], res[1], list(res[2:2 + n]), list(res[2 + n:2 + 2 * n]), res[-1]


def _pair_wait(name, send_sem, recv_sem, grads, lands, after):
    n = len(grads)

    def body(*refs):
        srcs, land_refs = refs[:n], refs[n:2 * n]
        send_ref, recv_ref = refs[2 * n], refs[2 * n + 1]
        for i in range(n):
            copy = _pair_copy(srcs, land_refs, send_ref, recv_ref, i)
            copy.wait_send()
            copy.wait_recv()

    res = pl.pallas_call(
        body,
        name=name,
        in_specs=[HBM] * (2 * n) + [SEM, SEM, ANY],
        out_specs=[HBM] * (2 * n),
        out_shape=[pltpu.HBM(g.shape, g.dtype) for g in grads] + [pltpu.HBM(l.shape, l.dtype) for l in lands],
        input_output_aliases={i: i for i in range(2 * n)},
        compiler_params=pltpu.CompilerParams(has_side_effects=SPLIT_COPY),
    )(*grads, *lands, send_sem, recv_sem, after)
    return list(res[:n]), list(res[n:])


def _scatter_copy(srcs, lands, send_sem, recv_sem, i, j):
    x, y, c = _position()
    chips = _other_chips(x, y)
    return pltpu.make_async_remote_copy(
        src_ref=srcs[i].at[2 * chips[j][0] + chips[j][1]], dst_ref=lands[i].at[j],
        send_sem=send_sem.at[3 * i + j], recv_sem=recv_sem.at[3 * i + j],
        device_id=(chips[j][0], chips[j][1], c), device_id_type=MESH)


def _scatter_start(name, sums):
    n = len(sums)

    def body(*refs):
        srcs, lands = refs[:n], refs[n:2 * n]
        send_sem, recv_sem = refs[2 * n], refs[2 * n + 1]
        token = refs[-1]
        for i in range(n):
            for j in range(3):
                _scatter_copy(srcs, lands, send_sem, recv_sem, i, j).start()
        token[...] = jnp.zeros_like(token)

    land_shapes = [(3,) + s.shape[1:] for s in sums]
    res = pl.pallas_call(
        body,
        name=name,
        in_specs=[HBM] * (2 * n),
        out_specs=[SEM, SEM] + [HBM] * (2 * n) + [pl.BlockSpec(memory_space=pltpu.VMEM)],
        out_shape=[pltpu.SemaphoreType.DMA((3 * n,)), pltpu.SemaphoreType.DMA((3 * n,))]
        + [pltpu.HBM(s.shape, s.dtype) for s in sums]
        + [pltpu.HBM(shp, s.dtype) for shp, s in zip(land_shapes, sums)]
        + [jax.ShapeDtypeStruct((8, LANES), F32)],
        input_output_aliases={i: 2 + i for i in range(2 * n)},
        compiler_params=pltpu.CompilerParams(has_side_effects=SPLIT_COPY),
    )(*[pltpu.with_memory_space_constraint(s, pltpu.HBM) for s in sums],
      *[pltpu.with_memory_space_constraint(lax.empty(shp, s.dtype), pltpu.HBM) for shp, s in zip(land_shapes, sums)])
    return res[0], res[1], list(res[2:2 + n]), list(res[2 + n:2 + 2 * n]), res[-1]


def _scatter_wait(name, send_sem, recv_sem, sums, lands, after):
    n = len(sums)

    def body(*refs):
        srcs, land_refs = refs[:n], refs[n:2 * n]
        send_ref, recv_ref = refs[2 * n], refs[2 * n + 1]
        for i in range(n):
            for j in range(3):
                copy = _scatter_copy(srcs, land_refs, send_ref, recv_ref, i, j)
                copy.wait_send()
                copy.wait_recv()

    res = pl.pallas_call(
        body,
        name=name,
        in_specs=[HBM] * (2 * n) + [SEM, SEM, ANY],
        out_specs=[HBM] * (2 * n),
        out_shape=[pltpu.HBM(s.shape, s.dtype) for s in sums] + [pltpu.HBM(l.shape, l.dtype) for l in lands],
        input_output_aliases={i: i for i in range(2 * n)},
        compiler_params=pltpu.CompilerParams(has_side_effects=SPLIT_COPY),
    )(*sums, *lands, send_sem, recv_sem, after)
    return list(res[:n]), list(res[n:])


def _pair_join(name, halves, small=None):
    n = len(halves)
    if small is None:
        def body_plain(*refs):
            ins, outs = refs[:n], refs[n:2 * n]
            send_sem, recv_sem = refs[2 * n:]
            x, y, c = _position()
            swaps = [pltpu.make_async_remote_copy(
                src_ref=ins[i], dst_ref=outs[i], send_sem=send_sem.at[i], recv_sem=recv_sem.at[i],
                device_id=(x, y, 1 - c), device_id_type=MESH) for i in range(n)]
            for swap in swaps:
                swap.start()
            for swap in swaps:
                swap.wait()

        return pl.pallas_call(
            body_plain,
            name=name,
            in_specs=[ANY] * n,
            out_specs=[ANY] * n,
            out_shape=[jax.ShapeDtypeStruct(h.shape, h.dtype) for h in halves],
            scratch_shapes=[pltpu.SemaphoreType.DMA((n,))] * 2,
        )(*halves)

    def body(*refs):
        ins, small_ref = refs[:n], refs[n]
        outs, all_ref = refs[n + 1:2 * n + 1], refs[2 * n + 1]
        send_sem, recv_sem, sm_send, sm_recv, sm_local = refs[2 * n + 2:]
        x, y, c = _position()
        swaps = []
        for i in range(n):
            swap = pltpu.make_async_remote_copy(
                src_ref=ins[i], dst_ref=outs[i], send_sem=send_sem.at[i], recv_sem=recv_sem.at[i],
                device_id=(x, y, 1 - c), device_id_type=MESH)
            swap.start()
            swaps.append(swap)
        me = 4 * x + 2 * y + c
        sm_own = pltpu.make_async_copy(small_ref, all_ref.at[me], sm_local)
        sm_own.start()
        pushes, arrivals = [], []
        for mask in range(1, N_DEV):
            px, py, pc = x ^ (mask >> 2), y ^ ((mask >> 1) & 1), c ^ (mask & 1)
            pushes.append(pltpu.make_async_remote_copy(
                src_ref=small_ref, dst_ref=all_ref.at[me], send_sem=sm_send.at[mask - 1], recv_sem=sm_recv.at[mask - 1],
                device_id=(px, py, pc), device_id_type=MESH))
            arrivals.append(pltpu.make_async_remote_copy(
                src_ref=small_ref, dst_ref=all_ref.at[4 * px + 2 * py + pc], send_sem=sm_send.at[mask - 1],
                recv_sem=sm_recv.at[mask - 1], device_id=(px, py, pc), device_id_type=MESH))
        for cp in pushes:
            cp.start()
        for swap in swaps:
            swap.wait()
        for cp in arrivals:
            cp.wait_recv()
        for cp in pushes:
            cp.wait_send()
        sm_own.wait()

    res = pl.pallas_call(
        body,
        name=name,
        in_specs=[ANY] * (n + 1),
        out_specs=[ANY] * (n + 1),
        out_shape=[jax.ShapeDtypeStruct(h.shape, h.dtype) for h in halves]
        + [jax.ShapeDtypeStruct((N_DEV,) + small.shape, small.dtype)],
        scratch_shapes=[pltpu.SemaphoreType.DMA((n,))] * 2 + [pltpu.SemaphoreType.DMA((N_DEV - 1,))] * 2
        + [pltpu.SemaphoreType.DMA(())],
    )(*halves, small)
    return res[:n], res[n]


def _lower_bound(lbp):
    return jax.nn.softmax(lbp, axis=0)[0:1]


def _local_step(x, target, g1, gm, g2, gq, gk, go, rel_bias, lbp, first_weights, mid_weights, last_weights, on_grads, grads_sent):
    b, s, d = x.shape
    t = b * s
    x0 = x.reshape(t, d)
    tgt = target.reshape(t, d)
    gq_t = jnp.tile(gq, (1, ATTN_HEADS))
    gk_t = jnp.tile(gk, (1, ATTN_HEADS))
    lb = _lower_bound(lbp)
    table = _band_table(_rel_bias_table("rel_bias_table", rel_bias))

    h1 = _rmsnorm_fwd("norm1", x0, g1)
    wg1, wu1, deps1 = first_weights((h1, table))
    a1, b1, z1 = _ffn_up("ffn1_up", h1, wg1, wu1, deps1)
    wd1, w_in, w_out = mid_weights((z1,))
    ns = w_in.shape[0]
    x1, h2 = _ffn_down("ffn1_down", z1, wd1, x0, gm)
    proj = _in_proj("in_proj", h2, w_in)
    proj3 = proj.reshape(b, s, proj.shape[1])
    qn, kn, vb = _qk_prep("qk_prep", proj3, gq_t, gk_t)
    attn = _attn_fwd("attn_fwd", qn, kn, vb, table).reshape(t, ATTN_W)
    mix, oraw, states = _hgrn_fwd("hgrn_fwd", proj, attn, lb, go, b, s)
    x2, h3 = _out_proj("out_proj", mix, w_out, x1, g2)
    wg2, wu2, wd2 = last_weights((h3,))
    a2, b2, z2 = _ffn_up("ffn2_up", h3, wg2, wu2)
    dy, dyh, sq = _ffn_down_loss("ffn2_down_loss", z2, wd2, x2, tgt)
    loss = 0.5 * jnp.sum(sq) / d

    da2, db2 = _ffn_bwd_act("ffn2_bwd_act", dyh, wd2, a2, b2)
    dwd2 = _grad_w_shardrows("ffn2_dwd", z2, dyh)
    dwg2 = _grad_w_shardrows("ffn2_dwg", da2, h3)
    dwu2 = _grad_w_shardrows("ffn2_dwu", db2, h3)
    sent2 = on_grads("ffn2", {"ffn2_w_gate": dwg2, "ffn2_w_up": dwu2, "ffn2_w_down": dwd2})
    dx2, dx2b, dg2 = _ffn_bwd_in("ffn2_bwd_in", da2, db2, wg2, wu2, x2, g2, dy, 1.0, sent2)
    sent2 = grads_sent("ffn2", dx2b)

    dwout = _grad_w_out("dw_out", mix, dx2b)
    dmix = _out_proj_bwd("out_proj_bwd", dx2b, w_out, sent2)
    dqn, dkn, dvn, dbe, dbo = _attn_bwd("attn_bwd", qn, kn, vb, table, dmix.reshape(b, s, dmix.shape[1]))
    dbias = dbe[:, :, :BAND] + dbo[:, :, CHUNK:]
    dpq, dpk, dpv, dgq, dgk = _qk_prep_bwd("qk_prep_bwd", proj3, dqn, dkn, dvn, gq_t, gk_t)
    dpq, dpk, dpv = (a.reshape(t, ATTN_W) for a in (dpq, dpk, dpv))
    dproj, dlb, dgo = _hgrn_bwd("hgrn_bwd", proj, (dpq, dpk, dpv), lb, go, oraw, states, dmix, b, s)
    dwin = _grad_w_in("dw_in", h2, dproj, ns)
    dx1, dx1h, dgm = _in_proj_bwd("in_proj_bwd", dproj, w_in, x1, gm, dx2, 0.5)

    dwd1 = _grad_w_shardrows("ffn1_dwd", z1, dx1h)
    sent_mix = on_grads("mix", {"w_in": dwin, "w_out": dwout.reshape(ns, dwout.shape[0] // ns, d),
                                "ffn1_w_down": dwd1})
    da1, db1 = _ffn_bwd_act("ffn1_bwd_act", dx1h, wd1, a1, b1, sent_mix)
    sent_mix = grads_sent("mix", da1)
    dwg1 = _grad_w_shardrows("ffn1_dwg", da1, h1, sent_mix)
    dwu1 = _grad_w_shardrows("ffn1_dwu", db1, h1)
    on_grads("ffn1", {"ffn1_w_gate": dwg1, "ffn1_w_up": dwu1})
    sent1 = grads_sent("ffn1", None)
    dx0, dg1 = _ffn_bwd_in("ffn1_bwd_in", da1, db1, wg1, wu1, x0, g1, dx1, None, sent1)

    nt = dg1.shape[0]
    sg = _small_grads(
        "small_grads", dg1.reshape(nt, d), dgm.reshape(nt, d), dg2.reshape(nt, d),
        dgq.reshape(-1, ATTN_W), dgk.reshape(-1, ATTN_W), dbias.transpose(1, 0, 2),
        dlb.reshape(b, HGRN_W), dgo.reshape(b, HGRN_W), lbp)
    g1g, gmg, g2g, gqg, gkg, rbg, lbg, gog = sg
    small = _pack_small(g1g, gmg, g2g, lbg, rbg[:, :N_REL], gqg, gkg, gog, loss)
    return dx0.reshape(b, s, d), small


LOSS_SLOT = 7 * SMALL_COLS + 2 * ATTN_DH + HGRN_DH


def _pack_small(g1, gm, g2, lbp, rel_bias, gq, gk, go, loss=None):
    flat = [g1.reshape(-1), gm.reshape(-1), g2.reshape(-1), lbp.reshape(-1), rel_bias.reshape(-1)]
    n_bias = 3 * SMALL_COLS - rel_bias.size
    heads = [gq.reshape(-1), gk.reshape(-1), go.reshape(-1)]
    heads.append(jnp.zeros((1,), F32) if loss is None else loss.reshape(1))
    n_tail = SMALL_COLS - sum(h.size for h in heads)
    return jnp.concatenate(flat + [jnp.zeros((n_bias,), F32)] + heads + [jnp.zeros((n_tail,), F32)]).reshape(
        SMALL_ROWS, SMALL_COLS)


def _unpack_small(p, d):
    flat = p.reshape(-1)
    o = 3 * d
    g1, gm, g2 = p[0:1], p[1:2], p[2:3]
    lbp = flat[o:o + 2 * HGRN_W].reshape(2, HGRN_W)
    o = 4 * SMALL_COLS
    rel = flat[o:o + ATTN_HEADS * N_REL].reshape(1, ATTN_HEADS, N_REL)
    o = 7 * SMALL_COLS
    gq = flat[o:o + ATTN_DH].reshape(1, ATTN_DH)
    gk = flat[o + ATTN_DH:o + 2 * ATTN_DH].reshape(1, ATTN_DH)
    go = flat[o + 2 * ATTN_DH:o + 2 * ATTN_DH + HGRN_DH].reshape(1, HGRN_DH)
    return g1, gm, g2, gq, gk, rel, lbp, go


def kernel(x, ffn1_norm_g, ffn1_w_gate, ffn1_w_up, ffn1_w_down, mix_norm_g, w_in, attn_q_norm_g, attn_k_norm_g, attn_rel_bias, hgrn_lower_bounds, hgrn_out_norm_g, w_out, ffn2_norm_g, ffn2_w_gate, ffn2_w_up, ffn2_w_down, loss_target, m_ffn1_norm_g, m_ffn1_w_gate, m_ffn1_w_up, m_ffn1_w_down, m_mix_norm_g, m_w_in, m_attn_q_norm_g, m_attn_k_norm_g, m_attn_rel_bias, m_hgrn_lower_bounds, m_hgrn_out_norm_g, m_w_out, m_ffn2_norm_g, m_ffn2_w_gate, m_ffn2_w_up, m_ffn2_w_down, v_ffn1_norm_g, v_ffn1_w_gate, v_ffn1_w_up, v_ffn1_w_down, v_mix_norm_g, v_w_in, v_attn_q_norm_g, v_attn_k_norm_g, v_attn_rel_bias, v_hgrn_lower_bounds, v_hgrn_out_norm_g, v_w_out, v_ffn2_norm_g, v_ffn2_w_gate, v_ffn2_w_up, v_ffn2_w_down):
    d = x.shape[-1]
    big_w = [ffn1_w_gate, ffn1_w_up, ffn1_w_down, w_in, w_out, ffn2_w_gate, ffn2_w_up, ffn2_w_down]
    big_m = [m_ffn1_w_gate, m_ffn1_w_up, m_ffn1_w_down, m_w_in, m_w_out, m_ffn2_w_gate, m_ffn2_w_up, m_ffn2_w_down]
    big_v = [v_ffn1_w_gate, v_ffn1_w_up, v_ffn1_w_down, v_w_in, v_w_out, v_ffn2_w_gate, v_ffn2_w_up, v_ffn2_w_down]
    big_names = ["ffn1_w_gate", "ffn1_w_up", "ffn1_w_down", "w_in", "w_out", "ffn2_w_gate", "ffn2_w_up", "ffn2_w_down"]
    flipped = {nm for nm in big_names if nm.endswith("gate") or nm.endswith("up")}
    flip = lambda nm, a: jnp.swapaxes(a, 1, 2) if nm in flipped else a
    big_w, big_m, big_v = ([flip(nm, a) for nm, a in zip(big_names, arrs)] for arrs in (big_w, big_m, big_v))

    shards = [w[0].astype(BF16) for w in big_w]
    start_a = _gather_start("gather_start_up1", shards[:2], ())
    start_b = _gather_start("gather_start_mid", shards[2:5], (start_a[4],))
    start_c = _gather_start("gather_start_ffn2", shards[5:], (start_b[4],))

    def gathered(tag, started, after):
        send_sem, recv_sem, srcs, outs, _ = started
        srcs, outs = _gather_wait("gather_wait_" + tag, send_sem, recv_sem, srcs, outs, after)
        return _gather_join("gather_join_" + tag, srcs, outs)

    def first_weights(after):
        return (*gathered("up1", start_a, after), (start_c[4],))

    def mid_weights(after):
        wd1, win_f, wout_f = gathered("mid", start_b, after)
        return wd1, win_f, wout_f.reshape(wout_f.shape[0] * wout_f.shape[1], d)

    def last_weights(after):
        return gathered("ffn2", start_c, after)

    core = lax.axis_index("c").astype(jnp.int32).reshape(1)
    chip = (2 * lax.axis_index("x") + lax.axis_index("y")).astype(jnp.int32).reshape(1)
    started = {}

    def on_grads(tag, grads):
        names = list(grads)
        started[tag] = (names, _pair_start("pair_start_" + tag, [grads[nm] for nm in names]))
        return (started[tag][1][4],)

    def grads_sent(tag, after):
        names, (send_sem, recv_sem, grads, lands, token) = started[tag]
        grads, theirs = _pair_wait("pair_wait_" + tag, send_sem, recv_sem, grads, lands, token if after is None else after)
        sums = [_pair_sum("pair_sum_" + nm, g, th, core) for nm, g, th in zip(names, grads, theirs)]
        started[tag] = (names, _scatter_start("scatter_start_" + tag, sums))
        return (started[tag][1][4],)

    grad_x, small_g = _local_step(
        x, loss_target, ffn1_norm_g, mix_norm_g, ffn2_norm_g, attn_q_norm_g, attn_k_norm_g, hgrn_out_norm_g,
        attn_rel_bias[0], hgrn_lower_bounds, first_weights, mid_weights, last_weights, on_grads, grads_sent)

    def finish(tag, after):
        names, (send_sem, recv_sem, sums, lands, _) = started[tag]
        sums, lands = _scatter_wait("scatter_wait_" + tag, send_sem, recv_sem, sums, lands, after)
        return names, [_chip_sum("chip_sum_" + nm, sm, ld, chip) for nm, sm, ld in zip(names, sums, lands)]

    by_name = {nm: (w, m, v) for nm, w, m, v in zip(big_names, big_w, big_m, big_v)}
    updated = {}

    def update(names, halves, other_halves):
        for nm, mine, theirs in zip(names, halves, other_halves):
            w, m, v = by_name[nm]
            updated[nm] = _adamw("adamw_" + nm, w, mine, theirs, m, v, core)

    last_token = started["ffn1"][1][4]
    names_a, halves_a = finish("ffn2", last_token)
    names_m, halves_m = finish("mix", last_token)
    names_a, halves_a = names_a + names_m, halves_a + halves_m
    update(names_a, halves_a, _pair_join("pair_join_early", halves_a))
    names_b, halves_b = finish("ffn1", updated[names_a[-1]][1])
    others_b, small_all = _pair_join("pair_join_last", halves_b, small_g)
    update(names_b, halves_b, others_b)
    big_out = [updated[nm] for nm in big_names]

    pack = lambda g1, gm, g2, gq, gk, rel, lbp, go: _pack_small(g1, gm, g2, lbp, rel[0], gq, gk, go)
    small_w = pack(ffn1_norm_g, mix_norm_g, ffn2_norm_g, attn_q_norm_g, attn_k_norm_g, attn_rel_bias, hgrn_lower_bounds, hgrn_out_norm_g)
    small_m = pack(m_ffn1_norm_g, m_mix_norm_g, m_ffn2_norm_g, m_attn_q_norm_g, m_attn_k_norm_g, m_attn_rel_bias, m_hgrn_lower_bounds, m_hgrn_out_norm_g)
    small_v = pack(v_ffn1_norm_g, v_mix_norm_g, v_ffn2_norm_g, v_attn_q_norm_g, v_attn_k_norm_g, v_attn_rel_bias, v_hgrn_lower_bounds, v_hgrn_out_norm_g)
    small_res = _adamw_small("adamw_small", small_w, small_all, small_m, small_v)
    small_out = [_unpack_small(p, d) for p in small_res]
    loss = small_res[0].reshape(-1)[LOSS_SLOT]

    def assemble(kind):
        bg = [flip(nm, o[kind]) for nm, o in zip(big_names, big_out)]
        g1, gm, g2, gq, gk, rel, lbp, go = small_out[kind]
        return [g1, bg[0], bg[1], bg[2], gm, bg[3], gq, gk, rel, lbp, go, bg[4], g2, bg[5], bg[6], bg[7]]

    return (loss, grad_x, *assemble(0), *assemble(1), *assemble(2), *assemble(3))
```

```python
import functools

import jax
import jax.numpy as jnp
from jax import lax
from jax.experimental import pallas as pl
from jax.experimental.pallas import tpu as pltpu

F32 = jnp.float32
BF16 = jnp.bfloat16
MESH = pl.DeviceIdType.MESH

N_CHIPS = 4
N_DEV = 8
CHUNK = 64
ATTN_HEADS = 8
ATTN_DH = 64
ATTN_W = ATTN_HEADS * ATTN_DH
HGRN_HEADS = 4
HGRN_DH = 128
HGRN_W = HGRN_HEADS * HGRN_DH
LEFT_CHUNKS = 8
BAND = (LEFT_CHUNKS + 1) * CHUNK
KPAD = LEFT_CHUNKS * CHUNK
REL_CLIP = 128
N_REL = 2 * REL_CLIP + 1
N_REL_PAD = 384
RMS_EPS = 1e-6
LANES = 128
SMALL_ROWS = 8
SMALL_COLS = 1024

ADAM_LR = 0.001
ADAM_B1 = 0.9
ADAM_B2 = 0.999
ADAM_EPS = 1e-08
ADAM_WD = 0.01
ADAM_STEP = 10

NN = (((1,), (0,)), ((), ()))
NT = (((1,), (1,)), ((), ()))
TN = (((0,), (0,)), ((), ()))

VMEM_LIMIT = 48 * 1024 * 1024


def _sigmoid(x):
    return 1.0 / (1.0 + jnp.exp(-x))


def _silu(x):
    return x * _sigmoid(x)


def _dot(a, b, dims=NN):
    return lax.dot_general(a, b, dims, preferred_element_type=F32)


def _split3(x):
    hi = x.astype(BF16)
    r1 = x - hi.astype(F32)
    mid = r1.astype(BF16)
    lo = (r1 - mid.astype(F32)).astype(BF16)
    return hi, mid, lo


def _dot_exact_rhs(x, mat, dims=NN, pieces=3):
    hi, mid, lo = _split3(x)
    out = _dot(hi, mat, dims) + _dot(mid, mat, dims)
    return out + _dot(lo, mat, dims) if pieces == 3 else out


def _dot_exact_lhs(mat, x, dims=NN):
    hi, mid, lo = _split3(x)
    return _dot(mat, hi, dims) + _dot(mat, mid, dims) + _dot(mat, lo, dims)


def _params(*sem):
    return pltpu.CompilerParams(dimension_semantics=sem, vmem_limit_bytes=VMEM_LIMIT)


def _mm(name, ins, terms, n_acc, grid, acc_shape, outs, epilogue, extras=(), deps=()):
    nk = grid[2]
    ni, ne, nd, no = len(ins), len(extras), len(deps), len(outs)

    def body(*refs):
        in_refs = refs[:ni]
        ex_refs = refs[ni:ni + ne]
        out_refs = refs[ni + ne + nd:ni + ne + nd + no]
        acc_refs = refs[ni + ne + nd + no:]
        parts = [None] * n_acc
        for ai, li, ri, dims in terms:
            d = _dot(in_refs[li][...], in_refs[ri][...], dims)
            parts[ai] = d if parts[ai] is None else parts[ai] + d

        def finish(accs):
            res = epilogue(accs, [e[...] for e in ex_refs])
            for o, r in zip(out_refs, res):
                o[...] = r.astype(o.dtype)

        if nk == 1:
            finish(parts)
        else:
            k = pl.program_id(2)

            @pl.when(k == 0)
            def _():
                for a, p in zip(acc_refs, parts):
                    a[...] = p

            @pl.when(k > 0)
            def _():
                for a, p in zip(acc_refs, parts):
                    a[...] += p

            @pl.when(k == nk - 1)
            def _():
                finish([a[...] for a in acc_refs])

    scratch = [] if nk == 1 else [pltpu.VMEM(acc_shape, F32) for _ in range(n_acc)]
    res = pl.pallas_call(
        body,
        name=name,
        grid=grid,
        in_specs=[s for _, s in ins] + [s for _, s in extras] + [pl.BlockSpec(memory_space=pl.ANY)] * nd,
        out_specs=[s for _, s in outs],
        out_shape=[o for o, _ in outs],
        scratch_shapes=scratch,
        compiler_params=_params("parallel", "parallel", "arbitrary"),
    )(*[a for a, _ in ins], *[a for a, _ in extras], *deps)
    return res


def _mm_rows(name, lhs, weights, dims, t, outs, epilogue, extras=(), deps=()):
    tm = _row_tile(t)
    nl, ne, nd, no = len(lhs), len(extras), len(deps), len(outs)
    ns = weights[0].shape[0]

    def body(*refs):
        lhs_refs = refs[:nl]
        w_hbm = refs[nl:2 * nl]
        ex_refs = refs[2 * nl:2 * nl + ne]
        out_refs = refs[2 * nl + ne + nd:2 * nl + ne + nd + no]
        w_vmem = refs[2 * nl + ne + nd + no:3 * nl + ne + nd + no]
        sem = refs[-1]

        @pl.when(pl.program_id(0) == 0)
        def _():
            copies = [pltpu.make_async_copy(w_hbm[p], w_vmem[p], sem.at[p]) for p in range(nl)]
            for cp in copies:
                cp.start()
            for cp in copies:
                cp.wait()

        acc = None
        for p in range(nl):
            pick = lhs[p][2]
            for j in range(ns):
                part = _dot(pick(lhs_refs[p], j), w_vmem[p][j], dims)
                acc = part if acc is None else acc + part
        res = epilogue([acc], [e[...] for e in ex_refs])
        for o, r in zip(out_refs, res):
            o[...] = r.astype(o.dtype)

    return pl.pallas_call(
        body,
        name=name,
        grid=(t // tm,),
        in_specs=[s for _, s, _ in lhs] + [pl.BlockSpec(memory_space=pl.ANY)] * nl + [s for _, s in extras]
        + [pl.BlockSpec(memory_space=pl.ANY)] * nd,
        out_specs=[s for _, s in outs],
        out_shape=[o for o, _ in outs],
        scratch_shapes=[pltpu.VMEM(w.shape, w.dtype) for w in weights] + [pltpu.SemaphoreType.DMA((nl,))],
        compiler_params=_params("arbitrary"),
    )(*[a for a, _, _ in lhs], *weights, *[a for a, _ in extras], *deps)


def _mm_shards(name, x, weights, dims, outs, epilogue, extras=(), deps=()):
    t = x.shape[0]
    tm = _row_tile(t)
    nw, ne, nd, no = len(weights), len(extras), len(deps), len(outs)
    ns = weights[0].shape[0]

    def body(*refs):
        x_ref = refs[0]
        w_hbm = refs[1:1 + nw]
        ex_refs = refs[1 + nw:1 + nw + ne]
        out_refs = refs[1 + nw + ne + nd:1 + nw + ne + nd + no]
        w_vmem = refs[1 + nw + ne + nd + no:1 + 2 * nw + ne + nd + no]
        sem = refs[-1]

        @pl.when(pl.program_id(0) == 0)
        def _():
            copies = [pltpu.make_async_copy(w_hbm[p], w_vmem[p], sem.at[p]) for p in range(nw)]
            for cp in copies:
                cp.start()
            for cp in copies:
                cp.wait()

        xv = x_ref[...]
        accs = [_dot(xv, w_vmem[p][0], dims) for p in range(nw)]
        for j in range(ns):
            nxt = [_dot(xv, w_vmem[p][j + 1], dims) for p in range(nw)] if j + 1 < ns else None
            res = epilogue(accs, [e[j] for e in ex_refs])
            for (_, _, store), o, r in zip(outs, out_refs, res):
                store(o, j, r.astype(o.dtype))
            accs = nxt

    return pl.pallas_call(
        body,
        name=name,
        grid=(t // tm,),
        in_specs=[pl.BlockSpec((tm, x.shape[1]), lambda i: (i, 0))] + [pl.BlockSpec(memory_space=pl.ANY)] * nw
        + [s for _, s in extras] + [pl.BlockSpec(memory_space=pl.ANY)] * nd,
        out_specs=[s for _, s, _ in outs],
        out_shape=[o for o, _, _ in outs],
        scratch_shapes=[pltpu.VMEM(w.shape, w.dtype) for w in weights] + [pltpu.SemaphoreType.DMA((nw,))],
        compiler_params=_params("arbitrary"),
    )(x, *weights, *[a for a, _ in extras], *deps)


def _store_shard(ref, j, value):
    ref[j] = value


def _row_tile(t):
    return 512 if t % 512 == 0 else t


def _k_tile(t):
    return t if t <= 4096 else 1024


def _rmsnorm(xv, g):
    ms = jnp.mean(xv * xv, axis=-1, keepdims=True)
    return xv * lax.rsqrt(ms + RMS_EPS) * g


def _rmsnorm_fwd(name, x, g):
    t, d = x.shape
    tm = _row_tile(t)

    def body(x_ref, g_ref, h_ref):
        h_ref[...] = _rmsnorm(x_ref[...], g_ref[...]).astype(BF16)

    return pl.pallas_call(
        body,
        name=name,
        grid=(t // tm,),
        in_specs=[pl.BlockSpec((tm, d), lambda i: (i, 0)), pl.BlockSpec((1, d), lambda i: (0, 0))],
        out_specs=pl.BlockSpec((tm, d), lambda i: (i, 0)),
        out_shape=jax.ShapeDtypeStruct((t, d), BF16),
        compiler_params=_params("parallel"),
    )(x, g)


def _norm_bwd_epilogue(copy_scale):
    def epilogue(accs, ex):
        dh = accs[0]
        xv, g, dres = ex
        ms = jnp.mean(xv * xv, axis=-1, keepdims=True)
        rstd = lax.rsqrt(ms + RMS_EPS)
        xhat = xv * rstd
        dxhat = dh * g
        dx = rstd * (dxhat - xhat * jnp.mean(dxhat * xhat, axis=-1, keepdims=True))
        out = dres + dx
        dg = jnp.sum(dh * xhat, axis=0, keepdims=True)
        if copy_scale is None:
            return out, dg
        return out, out * copy_scale, dg

    return epilogue


def _ffn_up(name, h, wg, wu, deps=()):
    t, d = h.shape
    ns, f, _ = wg.shape
    tm = _row_tile(t)

    def epilogue(accs, ex):
        a, b = accs
        sg = _sigmoid(a)
        act = a * sg
        return act, b * (sg * (1.0 + a * (1.0 - sg))), act * b

    out = (jax.ShapeDtypeStruct((ns, t, f), BF16), pl.BlockSpec((ns, tm, f), lambda i: (0, i, 0)), _store_shard)
    return _mm_shards(name, h, [wg, wu], NT, [out] * 3, epilogue, deps=deps)


def _shard_rows(arr, tm):
    ns, _, f = arr.shape
    return arr, pl.BlockSpec((ns, tm, f), lambda i: (0, i, 0)), lambda ref, j: ref[j]


def _ffn_down(name, z, wd, x, g_next, deps=()):
    _, t, _ = z.shape
    d = wd.shape[2]
    tm = _row_tile(t)
    row = pl.BlockSpec((tm, d), lambda i: (i, 0))

    def epilogue(accs, ex):
        y = ex[0] + 0.5 * accs[0]
        return y, _rmsnorm(y, ex[1])

    return _mm_rows(
        name, [_shard_rows(z, tm)], [wd], NN, t,
        outs=[(jax.ShapeDtypeStruct((t, d), F32), row), (jax.ShapeDtypeStruct((t, d), BF16), row)],
        epilogue=epilogue,
        extras=[(x, row), (g_next, pl.BlockSpec((1, d), lambda i: (0, 0)))],
        deps=deps,
    )


def _ffn_down_loss(name, z, wd, x, target):
    _, t, _ = z.shape
    d = wd.shape[2]
    tm = _row_tile(t)
    nt = t // tm
    row = pl.BlockSpec((tm, d), lambda i: (i, 0))

    def epilogue(accs, ex):
        e = ex[0] + 0.5 * accs[0] - ex[1]
        dy = e * (1.0 / d)
        return dy, 0.5 * dy, jnp.sum(e * e, axis=0, keepdims=True)

    return _mm_rows(
        name, [_shard_rows(z, tm)], [wd], NN, t,
        outs=[(jax.ShapeDtypeStruct((t, d), F32), row), (jax.ShapeDtypeStruct((t, d), BF16), row),
              (jax.ShapeDtypeStruct((nt, 1, d), F32), pl.BlockSpec((None, 1, d), lambda i: (i, 0, 0)))],
        epilogue=epilogue,
        extras=[(x, row), (target, row)],
    )


def _ffn_bwd_act(name, dout, wd, act_a, dact_b, deps=()):
    t, d = dout.shape
    ns, f, _ = wd.shape
    tm = _row_tile(t)

    def epilogue(accs, ex):
        dz = accs[0]
        return dz * ex[1].astype(F32), dz * ex[0].astype(F32)

    act = pl.BlockSpec((ns, tm, f), lambda i: (0, i, 0))
    out = (jax.ShapeDtypeStruct((ns, t, f), BF16), act, _store_shard)
    return _mm_shards(name, dout, [wd], NT, [out] * 2, epilogue, extras=[(act_a, act), (dact_b, act)], deps=deps)


def _grad_w_shardrows(name, z, dout, deps=()):
    ns, t, f = z.shape
    d = dout.shape[1]
    tk = _k_tile(t)
    return _mm(
        name,
        ins=[(z, pl.BlockSpec((None, tk, f), lambda j, n, k: (j, k, 0))),
             (dout, pl.BlockSpec((tk, d), lambda j, n, k: (k, 0)))],
        terms=[(0, 0, 1, TN)],
        n_acc=1,
        grid=(ns, 1, t // tk),
        acc_shape=(f, d),
        outs=[(jax.ShapeDtypeStruct((ns, f, d), BF16), pl.BlockSpec((None, f, d), lambda j, n, k: (j, 0, 0)))],
        epilogue=lambda accs, ex: (accs[0],),
        deps=deps,
    )[0]


def _norm_bwd_outs(t, d, tm, copy_scale):
    row = pl.BlockSpec((tm, d), lambda i: (i, 0))
    outs = [(jax.ShapeDtypeStruct((t, d), F32), row)]
    if copy_scale is not None:
        outs.append((jax.ShapeDtypeStruct((t, d), BF16), row))
    outs.append((jax.ShapeDtypeStruct((t // tm, 1, d), F32), pl.BlockSpec((None, 1, d), lambda i: (i, 0, 0))))
    return row, outs


def _ffn_bwd_in(name, da, db, wg, wu, x, g, dres, copy_scale, deps=()):
    _, t, _ = da.shape
    d = wg.shape[2]
    tm = _row_tile(t)
    row, outs = _norm_bwd_outs(t, d, tm, copy_scale)
    return _mm_rows(
        name, [_shard_rows(da, tm), _shard_rows(db, tm)], [wg, wu], NN, t,
        outs=outs,
        epilogue=_norm_bwd_epilogue(copy_scale),
        extras=[(x, row), (g, pl.BlockSpec((1, d), lambda i: (0, 0))), (dres, row)],
        deps=deps,
    )


def _in_proj(name, h, w_in):
    t, d = h.shape
    ns, _, pj = w_in.shape
    tm = _row_tile(t)
    def store(ref, j, value):
        ref[:, j * pj:(j + 1) * pj] = value

    out = (jax.ShapeDtypeStruct((t, ns * pj), F32), pl.BlockSpec((tm, ns * pj), lambda i: (i, 0)), store)
    return _mm_shards(name, h, [w_in], NN, [out], lambda accs, ex: (accs[0],))[0]


def _in_proj_bwd(name, dp, w_in, x, g, dres, copy_scale, deps=()):
    t = dp.shape[0]
    ns, d, pj = w_in.shape
    tm = _row_tile(t)
    row, outs = _norm_bwd_outs(t, d, tm, copy_scale)
    cols = (dp, pl.BlockSpec((tm, ns * pj), lambda i: (i, 0)), lambda ref, j: ref[:, j * pj:(j + 1) * pj])
    return _mm_rows(
        name, [cols], [w_in], NT, t,
        outs=outs,
        epilogue=_norm_bwd_epilogue(copy_scale),
        extras=[(x, row), (g, pl.BlockSpec((1, d), lambda i: (0, 0))), (dres, row)],
        deps=deps,
    )


def _grad_w_in(name, h, dp, ns):
    t, d = h.shape
    pj = dp.shape[1] // ns
    tk = _k_tile(t)
    return _mm(
        name,
        ins=[(h, pl.BlockSpec((tk, d), lambda j, n, k: (k, 0))),
             (dp, pl.BlockSpec((tk, pj), lambda j, n, k: (k, j)))],
        terms=[(0, 0, 1, TN)],
        n_acc=1,
        grid=(ns, 1, t // tk),
        acc_shape=(d, pj),
        outs=[(jax.ShapeDtypeStruct((ns, d, pj), BF16), pl.BlockSpec((None, d, pj), lambda j, n, k: (j, 0, 0)))],
        epilogue=lambda accs, ex: (accs[0],),
    )[0]


def _out_proj(name, mix, w_out, x, g_next):
    t, dm = mix.shape
    d = w_out.shape[1]
    tm = _row_tile(t)
    row = pl.BlockSpec((tm, d), lambda i, n, k: (i, 0))
    return _mm(
        name,
        ins=[(mix, pl.BlockSpec((tm, dm), lambda i, n, k: (i, 0))),
             (w_out, pl.BlockSpec((dm, d), lambda i, n, k: (0, 0)))],
        terms=[(0, 0, 1, NN)],
        n_acc=1,
        grid=(t // tm, 1, 1),
        acc_shape=(tm, d),
        outs=[(jax.ShapeDtypeStruct((t, d), F32), row), (jax.ShapeDtypeStruct((t, d), BF16), row)],
        epilogue=lambda accs, ex: (ex[0] + accs[0], _rmsnorm(ex[0] + accs[0], ex[1])),
        extras=[(x, row), (g_next, pl.BlockSpec((1, d), lambda i, n, k: (0, 0)))],
    )


def _out_proj_bwd(name, dx, w_out, deps=()):
    t, d = dx.shape
    dm = w_out.shape[0]
    tm = _row_tile(t)
    return _mm(
        name,
        ins=[(dx, pl.BlockSpec((tm, d), lambda i, n, k: (i, 0))),
             (w_out, pl.BlockSpec((dm, d), lambda i, n, k: (0, 0)))],
        terms=[(0, 0, 1, NT)],
        n_acc=1,
        grid=(t // tm, 1, 1),
        acc_shape=(tm, dm),
        outs=[(jax.ShapeDtypeStruct((t, dm), F32), pl.BlockSpec((tm, dm), lambda i, n, k: (i, 0)))],
        epilogue=lambda accs, ex: (accs[0],),
        deps=deps,
    )[0]


def _grad_w_out(name, mix, dx):
    t, dm = mix.shape
    d = dx.shape[1]
    tk = _k_tile(t)
    return _mm(
        name,
        ins=[(mix, pl.BlockSpec((tk, dm), lambda a, n, k: (k, 0))),
             (dx, pl.BlockSpec((tk, d), lambda a, n, k: (k, 0)))],
        terms=[(0, 0, 1, TN)],
        n_acc=1,
        grid=(1, 1, t // tk),
        acc_shape=(dm, d),
        outs=[(jax.ShapeDtypeStruct((dm, d), BF16), pl.BlockSpec((dm, d), lambda a, n, k: (0, 0)))],
        epilogue=lambda accs, ex: (accs[0],),
    )[0]


def _head_group_matrix():
    r = lax.broadcasted_iota(jnp.int32, (ATTN_W, ATTN_W), 0)
    c = lax.broadcasted_iota(jnp.int32, (ATTN_W, ATTN_W), 1)
    same = jnp.right_shift(r, 6) == jnp.right_shift(c, 6)
    return jnp.where(same, 1.0, 0.0).astype(BF16)


def _qk_prep(name, proj, gq, gk):
    b, s, _ = proj.shape
    tm = KPAD
    nb = s // tm

    def body(q_ref, k_ref, v_ref, gq_ref, gk_ref, qn_ref, kn_ref, vb_ref):
        j = pl.program_id(1)
        bd = _head_group_matrix()

        def norm(xv, g):
            ms = _dot_exact_rhs(xv * xv, bd, pieces=2) * (1.0 / ATTN_DH)
            return xv * lax.rsqrt(ms + RMS_EPS) * g

        @pl.when(j == 0)
        def _():
            kn_ref[...] = jnp.zeros_like(kn_ref)
            vb_ref[...] = jnp.zeros_like(vb_ref)

        @pl.when(j > 0)
        def _():
            qn_ref[...] = norm(q_ref[...], gq_ref[...]).astype(BF16)
            kn_ref[...] = norm(k_ref[...], gk_ref[...]).astype(BF16)
            vb_ref[...] = v_ref[...].astype(BF16)

    src_blk = lambda col: pl.BlockSpec((None, tm, ATTN_W), lambda bi, j: (bi, jnp.maximum(j - 1, 0), col))
    gspec = pl.BlockSpec((1, ATTN_W), lambda bi, j: (0, 0))
    padded = pl.BlockSpec((None, tm, ATTN_W), lambda bi, j: (bi, j, 0))
    return pl.pallas_call(
        body,
        name=name,
        grid=(b, nb + 1),
        in_specs=[src_blk(0), src_blk(1), src_blk(2), gspec, gspec],
        out_specs=[src_blk(0), padded, padded],
        out_shape=[jax.ShapeDtypeStruct((b, s, ATTN_W), BF16), jax.ShapeDtypeStruct((b, KPAD + s, ATTN_W), BF16),
                   jax.ShapeDtypeStruct((b, KPAD + s, ATTN_W), BF16)],
        compiler_params=_params("parallel", "arbitrary"),
    )(proj, proj, proj, gq, gk)


def _qk_prep_bwd(name, proj, dqn, dkn, dv, gq, gk):
    b, s, _ = proj.shape
    tm = KPAD
    nb = s // tm

    def body(q_ref, k_ref, dqn_ref, dkn_ref, dv_ref, gq_ref, gk_ref, dq_ref, dk_ref, dvb_ref, dgq_ref, dgk_ref):
        bd = _head_group_matrix()

        def bwd(xv, dy, g):
            ms = _dot_exact_rhs(xv * xv, bd, pieces=2) * (1.0 / ATTN_DH)
            rstd = lax.rsqrt(ms + RMS_EPS)
            xhat = xv * rstd
            dxhat = dy * g
            gm = _dot_exact_rhs(dxhat * xhat, bd, pieces=2) * (1.0 / ATTN_DH)
            return rstd * (dxhat - xhat * gm), jnp.sum(dy * xhat, axis=0, keepdims=True)

        dq, dgq = bwd(q_ref[...], dqn_ref[...], gq_ref[...])
        dk, dgk = bwd(k_ref[...], dkn_ref[...], gk_ref[...])
        dq_ref[...] = dq.astype(BF16)
        dk_ref[...] = dk.astype(BF16)
        dvb_ref[...] = dv_ref[...].astype(BF16)
        dgq_ref[...] = dgq
        dgk_ref[...] = dgk

    col = lambda c: pl.BlockSpec((None, tm, ATTN_W), lambda bi, j: (bi, j, c))
    past_pad = pl.BlockSpec((None, tm, ATTN_W), lambda bi, j: (bi, j + 1, 0))
    gspec = pl.BlockSpec((1, ATTN_W), lambda bi, j: (0, 0))
    pspec = pl.BlockSpec((None, 1, ATTN_W), lambda bi, j: (bi * nb + j, 0, 0))
    o_shape = jax.ShapeDtypeStruct((b, s, ATTN_W), BF16)
    p_shape = jax.ShapeDtypeStruct((b * nb, 1, ATTN_W), F32)
    return pl.pallas_call(
        body,
        name=name,
        grid=(b, nb),
        in_specs=[col(0), col(1), col(0), past_pad, past_pad, gspec, gspec],
        out_specs=[col(0)] * 3 + [pspec] * 2,
        out_shape=[o_shape] * 3 + [p_shape] * 2,
        compiler_params=_params("parallel", "parallel"),
    )(proj, proj, dqn, dkn, dv, gq, gk)


Q_CHUNKS = 4
QBLK = Q_CHUNKS * CHUNK
WIN = (LEFT_CHUNKS + Q_CHUNKS) * CHUNK
DB_W = BAND + CHUNK
MASKED = -1e30


def _band_table(bias):
    rows = [jnp.pad(bias, ((0, 0), (0, 0), (CHUNK * i, WIN - BAND - CHUNK * i)), constant_values=MASKED)
            for i in range(Q_CHUNKS)]
    return jnp.concatenate(rows, axis=1)


def _head_lanes(hh):
    lane = lax.broadcasted_iota(jnp.int32, (1, LANES), 1)
    return (lane < ATTN_DH) if hh == 0 else (lane >= ATTN_DH)


def _attn_probs(qh, kw, table, start):
    s = _dot(qh, kw, NT) * (ATTN_DH ** -0.5) + table
    col = lax.broadcasted_iota(jnp.int32, (QBLK, WIN), 1)
    s = jnp.where(col + start >= KPAD, s, MASKED)
    m = jnp.max(s, axis=-1, keepdims=True)
    p = jnp.exp(s - m)
    return p * (1.0 / jnp.sum(p, axis=-1, keepdims=True))


def _attn_fwd(name, q, k, v, table, deps=()):
    b, s, w = q.shape
    sp = k.shape[1]

    def body(q_ref, k_ref, v_ref, t_ref, *rest):
        o_ref = rest[-1]
        start = pl.multiple_of(pl.program_id(2) * QBLK, QBLK)
        kw = k_ref[pl.ds(start, WIN), :]
        vw = v_ref[pl.ds(start, WIN), :]
        q2 = q_ref[...]
        lanes = [_head_lanes(hh) for hh in range(2)]
        probs = [_attn_probs(jnp.where(mine, q2, jnp.zeros_like(q2)), kw, t_ref[hh], start).astype(BF16)
                 for hh, mine in enumerate(lanes)]
        outs = [_dot(p, vw) for p in probs]
        o_ref[...] = jnp.where(lanes[0], outs[0], outs[1]).astype(BF16)

    qspec = pl.BlockSpec((None, QBLK, LANES), lambda p, bi, i: (bi, i, p))
    kspec = pl.BlockSpec((None, sp, LANES), lambda p, bi, i: (bi, 0, p))
    return pl.pallas_call(
        body,
        name=name,
        grid=(w // LANES, b, s // QBLK),
        in_specs=[qspec, kspec, kspec, pl.BlockSpec((2, QBLK, WIN), lambda p, bi, i: (p, 0, 0))] + [ANY] * len(deps),
        out_specs=qspec,
        out_shape=jax.ShapeDtypeStruct((b, s, w), BF16),
        compiler_params=_params("parallel", "parallel", "arbitrary"),
    )(q, k, v, table, *deps)


def _attn_bwd(name, q, k, v, table, dmix):
    b, s, w = q.shape
    sp = k.shape[1]

    def body(q_ref, k_ref, v_ref, t_ref, do_ref, dq_ref, dk_ref, dv_ref, dbe_ref, dbo_ref):
        bi = pl.program_id(1)
        i = pl.program_id(2)
        start = pl.multiple_of(i * QBLK, QBLK)
        win = pl.ds(start, WIN)

        @pl.when(i == 0)
        def _():
            dk_ref[...] = jnp.zeros_like(dk_ref)
            dv_ref[...] = jnp.zeros_like(dv_ref)

        @pl.when(jnp.logical_and(i == 0, bi == 0))
        def _():
            dbe_ref[...] = jnp.zeros_like(dbe_ref)
            dbo_ref[...] = jnp.zeros_like(dbo_ref)

        kw = k_ref[win, :]
        vw = v_ref[win, :]
        q2 = q_ref[...]
        do2 = do_ref[...].astype(BF16)
        lanes = [_head_lanes(hh) for hh in range(2)]
        qh = [jnp.where(mine, q2, jnp.zeros_like(q2)) for mine in lanes]
        doh = [jnp.where(mine, do2, jnp.zeros_like(do2)) for mine in lanes]
        p = [_attn_probs(qh[hh], kw, t_ref[hh], start) for hh in range(2)]
        dp = [_dot(doh[hh], vw, NT) for hh in range(2)]
        ds = [p[hh] * (dp[hh] - jnp.sum(p[hh] * dp[hh], axis=-1, keepdims=True)) for hh in range(2)]
        dsb = [(x * (ATTN_DH ** -0.5)).astype(BF16) for x in ds]
        pb = [x.astype(BF16) for x in p]
        dq = [_dot(dsb[hh], kw) for hh in range(2)]
        dk = [_dot(dsb[hh], qh[hh], TN) for hh in range(2)]
        dv = [_dot(pb[hh], doh[hh], TN) for hh in range(2)]
        for hh in range(2):
            for qi in range(Q_CHUNKS):
                c0 = (qi // 2) * LANES
                blk = ds[hh][qi * CHUNK:(qi + 1) * CHUNK, c0:c0 + DB_W]
                if qi % 2 == 0:
                    dbe_ref[hh] += blk
                else:
                    dbo_ref[hh] += blk
        dq_ref[...] = jnp.where(lanes[0], dq[0], dq[1])
        dk_ref[win, :] += dk[0] + dk[1]
        dv_ref[win, :] += dv[0] + dv[1]

    qspec = pl.BlockSpec((None, QBLK, LANES), lambda p, bi, i: (bi, i, p))
    kspec = pl.BlockSpec((None, sp, LANES), lambda p, bi, i: (bi, 0, p))
    dbspec = pl.BlockSpec((2, CHUNK, DB_W), lambda p, bi, i: (p, 0, 0))
    db_shape = jax.ShapeDtypeStruct((ATTN_HEADS, CHUNK, DB_W), F32)
    return pl.pallas_call(
        body,
        name=name,
        grid=(w // LANES, b, s // QBLK),
        in_specs=[qspec, kspec, kspec, pl.BlockSpec((2, QBLK, WIN), lambda p, bi, i: (p, 0, 0)), qspec],
        out_specs=[qspec, kspec, kspec, dbspec, dbspec],
        out_shape=[jax.ShapeDtypeStruct((b, s, w), F32), jax.ShapeDtypeStruct((b, sp, w), F32),
                   jax.ShapeDtypeStruct((b, sp, w), F32), db_shape, db_shape],
        compiler_params=_params("arbitrary", "arbitrary", "arbitrary"),
    )(q, k, v, table, dmix)


HQ_COL = 3 * ATTN_W // HGRN_DH
HF_COL = HQ_COL + HGRN_HEADS
HI_COL = HF_COL + HGRN_HEADS
HG_COL = HI_COL + HGRN_HEADS
HGRN_ROWS = 8 * CHUNK
HEAD_LANES = [slice(hh * HGRN_DH, (hh + 1) * HGRN_DH) for hh in range(HGRN_HEADS)]


def _tri(lower):
    r = lax.broadcasted_iota(jnp.int32, (CHUNK, CHUNK), 0)
    c = lax.broadcasted_iota(jnp.int32, (CHUNK, CHUNK), 1)
    return (r >= c) if lower else (r <= c)


def _hgrn_chunk(hq, hf, lb, tril):
    sig = _sigmoid(hf)
    f = lb + (1.0 - lb) * sig
    g = jnp.log(f)
    ones_l = jnp.where(tril, 1.0, 0.0).astype(BF16)
    b = _dot_exact_lhs(ones_l, g)
    bl = jnp.sum(g, axis=0, keepdims=True)
    rows = lax.broadcasted_iota(jnp.int32, g.shape, 0)
    bm = jnp.sum(jnp.where(rows <= CHUNK // 2, g, 0.0), axis=0, keepdims=True)
    sq = _sigmoid(hq)
    q = hq * sq
    k = 1.0 - f
    return sig, f, b, bl, bm, sq, q, k


def _hgrn_fwd(name, proj, attn, lb, go, b, s):
    nc = s // CHUNK
    t = b * s
    nblk = s // HGRN_ROWS
    cpb = HGRN_ROWS // CHUNK

    def body(hq_ref, hf_ref, hi_ref, hg_ref, attn_ref, lb_ref, go_ref, mix_ref, oraw_ref, st_ref, s_scr):
        tril = _tri(True)
        gov = go_ref[...]
        mix_ref[:, 0:ATTN_W] = attn_ref[...]

        @pl.when(pl.program_id(1) == 0)
        def _():
            s_scr[...] = jnp.zeros_like(s_scr)

        def step(c, carry):
            sl = pl.ds(pl.multiple_of(c * CHUNK, CHUNK), CHUNK)
            hg = hg_ref[sl, :]
            _, _, bb, bl, bm, _, q, k = _hgrn_chunk(hq_ref[sl, :], hf_ref[sl, :], lb_ref[...], tril)
            vb = hi_ref[sl, :].astype(BF16)
            qe = (q * jnp.exp(bb - bm)).astype(BF16)
            ke = (k * jnp.exp(bm - bb)).astype(BF16)
            qb = (q * jnp.exp(bb)).astype(BF16)
            kb = (k * jnp.exp(bl - bb)).astype(BF16)
            e_last = jnp.exp(bl)
            gate = _silu(hg)
            st = [s_scr[hh] for hh in range(HGRN_HEADS)]
            a = [jnp.where(tril, _dot(qe[:, hs], ke[:, hs], NT), 0.0).astype(BF16) for hs in HEAD_LANES]
            o_state = [_dot(qb[:, hs], st[hh].astype(BF16), NT) for hh, hs in enumerate(HEAD_LANES)]
            st_next = [st[hh] * e_last[:, hs] + _dot(vb[:, hs], kb[:, hs], TN) for hh, hs in enumerate(HEAD_LANES)]
            o = [_dot(a[hh], vb[:, hs]) + o_state[hh] for hh, hs in enumerate(HEAD_LANES)]
            ro = [(oh * lax.rsqrt(jnp.mean(oh * oh, axis=-1, keepdims=True) + RMS_EPS) * gov) * gate[:, hs]
                  for oh, hs in zip(o, HEAD_LANES)]
            for hh in range(HGRN_HEADS):
                st_ref[hh, c] = st[hh]
                s_scr[hh] = st_next[hh]
            mix_ref[sl, ATTN_W:ATTN_W + HGRN_W] = jnp.concatenate(ro, axis=1).astype(BF16)
            oraw_ref[sl, :] = jnp.concatenate(o, axis=1)
            return carry

        lax.fori_loop(0, cpb, step, 0)

    col = lambda base: pl.BlockSpec((HGRN_ROWS, HGRN_W), lambda bi, i: (bi * nblk + i, base // HGRN_HEADS))
    out = pl.BlockSpec((HGRN_ROWS, HGRN_W), lambda bi, i: (bi * nblk + i, 0))
    return pl.pallas_call(
        body,
        name=name,
        grid=(b, nblk),
        in_specs=[col(HQ_COL), col(HF_COL), col(HI_COL), col(HG_COL), out,
                  pl.BlockSpec((1, HGRN_W), lambda bi, i: (0, 0)), pl.BlockSpec((1, HGRN_DH), lambda bi, i: (0, 0))],
        out_specs=[pl.BlockSpec((HGRN_ROWS, ATTN_W + HGRN_W), lambda bi, i: (bi * nblk + i, 0)), out,
                   pl.BlockSpec((None, HGRN_HEADS, cpb, HGRN_DH, HGRN_DH), lambda bi, i: (bi, 0, i, 0, 0))],
        out_shape=[jax.ShapeDtypeStruct((t, ATTN_W + HGRN_W), BF16), jax.ShapeDtypeStruct((t, HGRN_W), F32),
                   jax.ShapeDtypeStruct((b, HGRN_HEADS, nc, HGRN_DH, HGRN_DH), F32)],
        scratch_shapes=[pltpu.VMEM((HGRN_HEADS, HGRN_DH, HGRN_DH), F32)],
        compiler_params=_params("parallel", "arbitrary"),
    )(proj, proj, proj, proj, attn, lb, go)


def _hgrn_bwd(name, proj, dqkv, lb, go, oraw, states, dmix, b, s):
    t = b * s
    nblk = s // HGRN_ROWS
    cpb = HGRN_ROWS // CHUNK

    def body(hq_ref, hf_ref, hi_ref, hg_ref, dq_ref, dk_ref, dv_ref, lb_ref, go_ref, oraw_ref, st_ref, dro_ref,
             dp_ref, dlb_ref, dgo_ref, ds_scr, dlb_scr, dgo_scr):
        tril = _tri(True)
        ones_u = jnp.where(_tri(False), 1.0, 0.0).astype(BF16)
        gov = go_ref[...]
        dp_ref[:, 0:ATTN_W] = dq_ref[...]
        dp_ref[:, ATTN_W:2 * ATTN_W] = dk_ref[...]
        dp_ref[:, 2 * ATTN_W:3 * ATTN_W] = dv_ref[...]

        @pl.when(pl.program_id(1) == 0)
        def _():
            ds_scr[...] = jnp.zeros_like(ds_scr)
            dlb_scr[...] = jnp.zeros_like(dlb_scr)
            dgo_scr[...] = jnp.zeros_like(dgo_scr)

        def step(ci, carry):
            c = cpb - 1 - ci
            sl = pl.ds(pl.multiple_of(c * CHUNK, CHUNK), CHUNK)
            hq = hq_ref[sl, :]
            hg = hg_ref[sl, :]
            sig, f, bb, bl, bm, sq, q, k = _hgrn_chunk(hq, hf_ref[sl, :], lb_ref[...], tril)
            vb = hi_ref[sl, :].astype(BF16)
            ebm = jnp.exp(bb - bm)
            embm = jnp.exp(bm - bb)
            eb = jnp.exp(bb)
            ebl = jnp.exp(bl - bb)
            e_last = jnp.exp(bl)
            qe = (q * ebm).astype(BF16)
            ke = (k * embm).astype(BF16)
            qb = (q * eb).astype(BF16)
            kb = (k * ebl).astype(BF16)
            st = [st_ref[hh, c] for hh in range(HGRN_HEADS)]
            dst = [ds_scr[hh] for hh in range(HGRN_HEADS)]
            o = oraw_ref[sl, :]
            dro = dro_ref[sl, :]
            sg = _sigmoid(hg)
            gov4 = jnp.concatenate([gov] * HGRN_HEADS, axis=1)
            rstd = jnp.concatenate(
                [jnp.broadcast_to(lax.rsqrt(jnp.mean(o[:, hs] * o[:, hs], axis=-1, keepdims=True) + RMS_EPS),
                                  (CHUNK, HGRN_DH)) for hs in HEAD_LANES], axis=1)
            ohat = o * rstd
            dn = dro * (hg * sg)
            dhg = dro * (ohat * gov4) * (sg * (1.0 + hg * (1.0 - sg)))
            dgo_inc = jnp.sum(dn * ohat, axis=0, keepdims=True)
            dohat = dn * gov4
            proj_h = dohat * ohat
            pm = jnp.concatenate(
                [jnp.broadcast_to(jnp.mean(proj_h[:, hs], axis=-1, keepdims=True), (CHUNK, HGRN_DH))
                 for hs in HEAD_LANES], axis=1)
            dob = (rstd * (dohat - ohat * pm)).astype(BF16)
            stb = [x.astype(BF16) for x in st]
            dstb = [x.astype(BF16) for x in dst]
            a = [jnp.where(tril, _dot(qe[:, hs], ke[:, hs], NT), 0.0).astype(BF16) for hs in HEAD_LANES]
            dab = [jnp.where(tril, _dot(dob[:, hs], vb[:, hs], NT), 0.0).astype(BF16) for hs in HEAD_LANES]
            dqb = [_dot(dob[:, hs], stb[hh]) for hh, hs in enumerate(HEAD_LANES)]
            dkb = [_dot(vb[:, hs], dstb[hh]) for hh, hs in enumerate(HEAD_LANES)]
            dv_state = [_dot(kb[:, hs], dstb[hh], NT) for hh, hs in enumerate(HEAD_LANES)]
            dst_next = [dst[hh] * e_last[:, hs] + _dot(dob[:, hs], qb[:, hs], TN) for hh, hs in enumerate(HEAD_LANES)]
            dv = [_dot(a[hh], dob[:, hs], TN) + dv_state[hh] for hh, hs in enumerate(HEAD_LANES)]
            dqe = jnp.concatenate([_dot(dab[hh], ke[:, hs]) for hh, hs in enumerate(HEAD_LANES)], axis=1)
            dke = jnp.concatenate([_dot(dab[hh], qe[:, hs], TN) for hh, hs in enumerate(HEAD_LANES)], axis=1)
            dqb = jnp.concatenate(dqb, axis=1)
            dkb = jnp.concatenate(dkb, axis=1)
            state_term = jnp.concatenate(
                [jnp.sum(dst[hh] * st[hh], axis=0, keepdims=True) for hh in range(HGRN_HEADS)], axis=1)
            dq = dqe * ebm + dqb * eb
            dk = dke * embm + dkb * ebl
            db = (qe.astype(F32) * dqe - ke.astype(F32) * dke) + q * (dqb * eb) - k * (dkb * ebl)
            d_last = jnp.sum(k * ebl * dkb, axis=0, keepdims=True) + state_term * e_last
            dg = _dot_exact_lhs(ones_u, db) + d_last
            df = dg / f - dk
            first = HQ_COL * HGRN_DH
            dp_ref[sl, first:first + HGRN_W] = (dq * (sq * (1.0 + hq * (1.0 - sq)))).astype(BF16)
            dp_ref[sl, first + HGRN_W:first + 2 * HGRN_W] = (df * (1.0 - lb_ref[...]) * sig * (1.0 - sig)).astype(BF16)
            dp_ref[sl, first + 2 * HGRN_W:first + 3 * HGRN_W] = jnp.concatenate(dv, axis=1).astype(BF16)
            dp_ref[sl, first + 3 * HGRN_W:first + 4 * HGRN_W] = dhg.astype(BF16)
            dlb_scr[...] += jnp.sum(df * (1.0 - sig), axis=0, keepdims=True)
            dgo_scr[...] += dgo_inc
            for hh in range(HGRN_HEADS):
                ds_scr[hh] = dst_next[hh]
            return carry

        lax.fori_loop(0, cpb, step, 0)

        @pl.when(pl.program_id(1) == nblk - 1)
        def _():
            dlb_ref[...] = dlb_scr[...]
            dgo_ref[...] = dgo_scr[...]

    rows = lambda bi, i: bi * nblk + (nblk - 1 - i)
    col = lambda base: pl.BlockSpec((HGRN_ROWS, HGRN_W), lambda bi, i: (rows(bi, i), base // HGRN_HEADS))
    out = pl.BlockSpec((HGRN_ROWS, HGRN_W), lambda bi, i: (rows(bi, i), 0))
    part = pl.BlockSpec((None, 1, HGRN_W), lambda bi, i: (bi, 0, 0))
    width = HG_COL * HGRN_DH + HGRN_W
    o_shape = jax.ShapeDtypeStruct((t, width), BF16)
    p_shape = jax.ShapeDtypeStruct((b, 1, HGRN_W), F32)
    return pl.pallas_call(
        body,
        name=name,
        grid=(b, nblk),
        in_specs=[col(HQ_COL), col(HF_COL), col(HI_COL), col(HG_COL), out, out, out,
                  pl.BlockSpec((1, HGRN_W), lambda bi, i: (0, 0)), pl.BlockSpec((1, HGRN_DH), lambda bi, i: (0, 0)), out,
                  pl.BlockSpec((None, HGRN_HEADS, cpb, HGRN_DH, HGRN_DH), lambda bi, i: (bi, 0, nblk - 1 - i, 0, 0)),
                  col(ATTN_W // HGRN_DH)],
        out_specs=[pl.BlockSpec((HGRN_ROWS, width), lambda bi, i: (rows(bi, i), 0))] + [part] * 2,
        out_shape=[o_shape] + [p_shape] * 2,
        scratch_shapes=[pltpu.VMEM((HGRN_HEADS, HGRN_DH, HGRN_DH), F32), pltpu.VMEM((1, HGRN_W), F32),
                        pltpu.VMEM((1, HGRN_W), F32)],
        compiler_params=_params("parallel", "arbitrary"),
    )(proj, proj, proj, proj, *dqkv, lb, go, oraw, states, dmix)


def _small_grads(name, dg1, dgm, dg2, dgq, dgk, dbias_t, dlb, dgo, lbp):
    d = dg1.shape[1]

    def body(dg1_ref, dgm_ref, dg2_ref, dgq_ref, dgk_ref, dbias_ref, dlb_ref, dgo_ref, lbp_ref,
             g1_ref, gm_ref, g2_ref, gq_ref, gk_ref, rb_ref, lbg_ref, go_ref):
        g1_ref[...] = jnp.sum(dg1_ref[...], axis=0, keepdims=True)
        gm_ref[...] = jnp.sum(dgm_ref[...], axis=0, keepdims=True)
        g2_ref[...] = jnp.sum(dg2_ref[...], axis=0, keepdims=True)
        r = lax.broadcasted_iota(jnp.int32, (ATTN_W, ATTN_DH), 0)
        cidx = lax.broadcasted_iota(jnp.int32, (ATTN_W, ATTN_DH), 1)
        fold = jnp.where(jnp.bitwise_and(r, ATTN_DH - 1) == cidx, 1.0, 0.0).astype(BF16)
        gq_ref[...] = jnp.sum(_dot_exact_rhs(dgq_ref[...], fold), axis=0, keepdims=True)
        gk_ref[...] = jnp.sum(_dot_exact_rhs(dgk_ref[...], fold), axis=0, keepdims=True)
        gosum = jnp.sum(dgo_ref[...], axis=0, keepdims=True)
        go_ref[...] = (gosum[:, 0:HGRN_DH] + gosum[:, HGRN_DH:2 * HGRN_DH]
                       + gosum[:, 2 * HGRN_DH:3 * HGRN_DH] + gosum[:, 3 * HGRN_DH:4 * HGRN_DH])
        p0 = lbp_ref[0:1, :]
        p1 = lbp_ref[1:2, :]
        lbv = 1.0 / (1.0 + jnp.exp(p1 - p0))
        dp0 = jnp.sum(dlb_ref[...], axis=0, keepdims=True) * lbv * (1.0 - lbv)
        lbg_ref[0:1, :] = dp0
        lbg_ref[1:2, :] = -dp0
        sidx = lax.broadcasted_iota(jnp.int32, (BAND, N_REL_PAD), 0)
        ridx = lax.broadcasted_iota(jnp.int32, (BAND, N_REL_PAD), 1)

        def step(tq, acc):
            rel = jnp.clip(tq + KPAD - sidx, -REL_CLIP, REL_CLIP) + REL_CLIP
            onehot = jnp.where(rel == ridx, 1.0, 0.0).astype(BF16)
            return acc + _dot_exact_rhs(dbias_ref[tq], onehot)

        rb_ref[...] = lax.fori_loop(0, CHUNK, step, jnp.zeros((ATTN_HEADS, N_REL_PAD), F32))

    ins = [dg1, dgm, dg2, dgq, dgk, dbias_t, dlb, dgo, lbp]
    outs = [jax.ShapeDtypeStruct((1, d), F32)] * 3 + [jax.ShapeDtypeStruct((1, ATTN_DH), F32)] * 2 + [
        jax.ShapeDtypeStruct((ATTN_HEADS, N_REL_PAD), F32), jax.ShapeDtypeStruct((2, HGRN_W), F32),
        jax.ShapeDtypeStruct((1, HGRN_DH), F32)]
    vm = pl.BlockSpec(memory_space=pltpu.VMEM)
    return pl.pallas_call(
        body,
        name=name,
        in_specs=[vm] * len(ins),
        out_specs=[vm] * len(outs),
        out_shape=outs,
        compiler_params=pltpu.CompilerParams(vmem_limit_bytes=VMEM_LIMIT),
    )(*ins)


def _adam_update(w, g, m, v):
    m2 = ADAM_B1 * m + (1.0 - ADAM_B1) * g
    v2 = ADAM_B2 * v + (1.0 - ADAM_B2) * (g * g)
    m_hat = m2 / (1.0 - ADAM_B1 ** ADAM_STEP)
    v_hat = v2 / (1.0 - ADAM_B2 ** ADAM_STEP)
    delta = -ADAM_LR * (m_hat / (jnp.sqrt(v_hat) + ADAM_EPS) + ADAM_WD * w)
    return delta, m2, v2


def _rows_tile(r):
    return r if r <= 512 or r % 512 else 512


def _pair_sum(name, grad, theirs, core):
    n, half, c = theirs.shape
    tr = _rows_tile(half)
    nth = half // tr

    def body(core_ref, a_ref, b_ref, o_ref):
        o_ref[...] = (a_ref[...].astype(F32) + b_ref[...].astype(F32)).astype(o_ref.dtype)

    spec = pl.BlockSpec((None, tr, c), lambda i, j, core_ref: (i, j, 0))
    return pl.pallas_call(
        body, name=name,
        grid_spec=pltpu.PrefetchScalarGridSpec(
            num_scalar_prefetch=1, grid=(n, nth),
            in_specs=[pl.BlockSpec((None, tr, c), lambda i, j, core_ref: (i, core_ref[0] * nth + j, 0)), spec],
            out_specs=spec),
        out_shape=jax.ShapeDtypeStruct((n, half, c), BF16), compiler_params=_params("parallel", "parallel"),
    )(core, grad, theirs)


def _chip_sum(name, own, parts, chip):
    _, half, c = own.shape
    tr = _rows_tile(half)

    def body(chip_ref, own_ref, p_ref, o_ref):
        me = chip_ref[0]
        mine = own_ref[...].astype(F32)
        flip_x, flip_y, flip_xy = (p_ref[i].astype(F32) for i in range(3))
        acc = None
        for k in range(N_CHIPS):
            rel = jnp.bitwise_xor(me, k)
            term = jnp.where(rel == 0, mine, jnp.where(rel == 2, flip_x, jnp.where(rel == 1, flip_y, flip_xy)))
            acc = term if acc is None else acc + term
        o_ref[...] = acc

    return pl.pallas_call(
        body, name=name,
        grid_spec=pltpu.PrefetchScalarGridSpec(
            num_scalar_prefetch=1, grid=(half // tr,),
            in_specs=[pl.BlockSpec((None, tr, c), lambda j, chip_ref: (chip_ref[0], j, 0)),
                      pl.BlockSpec((3, tr, c), lambda j, chip_ref: (0, j, 0))],
            out_specs=pl.BlockSpec((tr, c), lambda j, chip_ref: (j, 0))),
        out_shape=jax.ShapeDtypeStruct((half, c), F32), compiler_params=_params("parallel"),
    )(chip, own, parts)


def _adamw(name, w, g_mine, g_theirs, m, v, core):
    _, r, c = w.shape
    half = r // 2
    tr = _rows_tile(half)
    nth = half // tr

    def body(core_ref, w_ref, gm_ref, gt_ref, m_ref, v_ref, g_ref, d_ref, m2_ref, v2_ref):
        g = jnp.where(pl.program_id(0) == core_ref[0], gm_ref[...], gt_ref[...])
        delta, m2, v2 = _adam_update(w_ref[...], g, m_ref[...], v_ref[...])
        g_ref[...] = g
        d_ref[...] = delta
        m2_ref[...] = m2
        v2_ref[...] = v2

    full = pl.BlockSpec((None, tr, c), lambda h, j, core_ref: (0, h * nth + j, 0))
    part = pl.BlockSpec((tr, c), lambda h, j, core_ref: (j, 0))
    shape = jax.ShapeDtypeStruct((1, r, c), F32)
    return pl.pallas_call(
        body, name=name,
        grid_spec=pltpu.PrefetchScalarGridSpec(
            num_scalar_prefetch=1, grid=(2, nth), in_specs=[full, part, part, full, full], out_specs=[full] * 4),
        out_shape=[shape] * 4, compiler_params=_params("parallel", "parallel"),
    )(core, w, g_mine, g_theirs, m, v)


def _rel_bias_table(name, rel_bias):
    padded = jnp.pad(rel_bias, ((0, 0), (0, N_REL_PAD - N_REL)))

    def body(rb_ref, o_ref):
        ridx = lax.broadcasted_iota(jnp.int32, (N_REL_PAD, BAND), 0)
        sidx = lax.broadcasted_iota(jnp.int32, (N_REL_PAD, BAND), 1)
        rb = rb_ref[...]

        def step(tq, carry):
            rel = jnp.clip(tq + KPAD - sidx, -REL_CLIP, REL_CLIP) + REL_CLIP
            onehot = jnp.where(rel == ridx, 1.0, 0.0).astype(BF16)
            o_ref[tq] = _dot_exact_rhs(rb, onehot)
            return carry

        lax.fori_loop(0, CHUNK, step, 0)

    vm = pl.BlockSpec(memory_space=pltpu.VMEM)
    table = pl.pallas_call(
        body, name=name, in_specs=[vm], out_specs=vm,
        out_shape=jax.ShapeDtypeStruct((CHUNK, ATTN_HEADS, BAND), F32),
    )(padded)
    return table.transpose(1, 0, 2)


def _adamw_small(name, w, parts, m, v):
    def body(w_ref, p_ref, m_ref, v_ref, g_ref, d_ref, m2_ref, v2_ref):
        g = p_ref[0]
        for i in range(1, N_DEV):
            g = g + p_ref[i]
        delta, m2, v2 = _adam_update(w_ref[...], g, m_ref[...], v_ref[...])
        g_ref[...] = g
        d_ref[...] = delta
        m2_ref[...] = m2
        v2_ref[...] = v2

    vm = pl.BlockSpec(memory_space=pltpu.VMEM)
    shape = jax.ShapeDtypeStruct((SMALL_ROWS, SMALL_COLS), F32)
    return pl.pallas_call(
        body, name=name, in_specs=[vm] * 4, out_specs=[vm] * 4, out_shape=[shape] * 4,
    )(w, parts, m, v)


def _position():
    return lax.axis_index("x"), lax.axis_index("y"), lax.axis_index("c")


def _other_chips(x, y):
    return [(1 - x, y), (x, 1 - y), (1 - x, 1 - y)]


ANY = pl.BlockSpec(memory_space=pl.ANY)


HBM = pl.BlockSpec(memory_space=pltpu.HBM)
SEM = pl.BlockSpec(memory_space=pltpu.SEMAPHORE)
SPLIT_COPY = pltpu.SideEffectType.DATAFLOW_SIDE_EFFECTING


def _gather_copy(shards, outs, send_sem, recv_sem, i, j):
    x, y, c = _position()
    chips = _other_chips(x, y)
    half = shards[i].shape[0] // 2
    rows = pl.ds(pl.multiple_of(c * half, 16), half)
    return pltpu.make_async_remote_copy(
        src_ref=shards[i].at[rows, :], dst_ref=outs[i].at[2 * x + y, rows, :],
        send_sem=send_sem.at[3 * i + j], recv_sem=recv_sem.at[3 * i + j],
        device_id=(chips[j][0], chips[j][1], c), device_id_type=MESH)


def _gather_start(name, shards, after):
    n = len(shards)

    def body(*refs):
        srcs, outs = refs[:n], refs[n:2 * n]
        send_sem, recv_sem = refs[2 * n + len(after)], refs[2 * n + len(after) + 1]
        token = refs[-1]
        for i in range(n):
            for j in range(3):
                _gather_copy(srcs, outs, send_sem, recv_sem, i, j).start()
        token[...] = jnp.zeros_like(token)

    full = [(N_CHIPS,) + s.shape for s in shards]
    res = pl.pallas_call(
        body,
        name=name,
        in_specs=[HBM] * (2 * n) + [ANY] * len(after),
        out_specs=[SEM, SEM] + [HBM] * (2 * n) + [pl.BlockSpec(memory_space=pltpu.VMEM)],
        out_shape=[pltpu.SemaphoreType.DMA((3 * n,)), pltpu.SemaphoreType.DMA((3 * n,))]
        + [pltpu.HBM(s.shape, s.dtype) for s in shards]
        + [pltpu.HBM(shp, s.dtype) for shp, s in zip(full, shards)]
        + [jax.ShapeDtypeStruct((8, LANES), F32)],
        input_output_aliases={i: 2 + i for i in range(2 * n)},
        compiler_params=pltpu.CompilerParams(has_side_effects=SPLIT_COPY),
    )(*[pltpu.with_memory_space_constraint(s, pltpu.HBM) for s in shards],
      *[pltpu.with_memory_space_constraint(lax.empty(shp, s.dtype), pltpu.HBM) for shp, s in zip(full, shards)],
      *after)
    return res[0], res[1], list(res[2:2 + n]), list(res[2 + n:2 + 2 * n]), res[-1]


def _gather_wait(name, send_sem, recv_sem, shards, outs, after):
    n = len(shards)

    def body(*refs):
        srcs, out_refs = refs[:n], refs[n:2 * n]
        send_ref, recv_ref = refs[2 * n], refs[2 * n + 1]
        for i in range(n):
            for j in range(3):
                copy = _gather_copy(srcs, out_refs, send_ref, recv_ref, i, j)
                copy.wait_send()
                copy.wait_recv()

    res = pl.pallas_call(
        body,
        name=name,
        in_specs=[HBM] * (2 * n) + [SEM, SEM] + [ANY] * len(after),
        out_specs=[HBM] * (2 * n),
        out_shape=[pltpu.HBM(s.shape, s.dtype) for s in shards] + [pltpu.HBM(o.shape, o.dtype) for o in outs],
        input_output_aliases={i: i for i in range(2 * n)},
        compiler_params=pltpu.CompilerParams(has_side_effects=SPLIT_COPY),
    )(*shards, *outs, send_sem, recv_sem, *after)
    return list(res[:n]), list(res[n:])


def _join_copies(srcs, ins, outs, own_send, own_recv, half_send, half_recv):
    x, y, c = _position()
    chips = _other_chips(x, y)
    copies = []
    for i in range(len(srcs)):
        copies.append(pltpu.make_async_remote_copy(
            src_ref=srcs[i], dst_ref=outs[i].at[2 * x + y], send_sem=own_send.at[i], recv_sem=own_recv.at[i],
            device_id=(x, y, 1 - c), device_id_type=MESH))
        half = srcs[i].shape[0] // 2
        rows = pl.ds(pl.multiple_of(c * half, 16), half)
        for j in range(3):
            slot = 2 * chips[j][0] + chips[j][1]
            copies.append(pltpu.make_async_remote_copy(
                src_ref=ins[i].at[slot, rows, :], dst_ref=outs[i].at[slot, rows, :],
                send_sem=half_send.at[3 * i + j], recv_sem=half_recv.at[3 * i + j],
                device_id=(x, y, 1 - c), device_id_type=MESH))
    return copies


def _gather_join(name, shards, outs):
    n = len(shards)

    def body(*refs):
        copies = _join_copies(refs[:n], refs[n:2 * n], refs[2 * n:3 * n], *refs[3 * n:])
        for cp in copies:
            cp.start()
        for cp in copies:
            cp.wait()

    return pl.pallas_call(
        body,
        name=name,
        in_specs=[ANY] * (2 * n),
        out_specs=[ANY] * n,
        out_shape=[jax.ShapeDtypeStruct(o.shape, o.dtype) for o in outs],
        input_output_aliases={n + i: i for i in range(n)},
        scratch_shapes=[pltpu.SemaphoreType.DMA((n,))] * 2 + [pltpu.SemaphoreType.DMA((3 * n,))] * 2,
    )(*shards, *outs)


def _join_start(name, shards, outs):
    n = len(shards)

    def body(*refs):
        srcs, arrs = refs[:n], refs[n:2 * n]
        sems = refs[2 * n:2 * n + 4]
        token = refs[-1]
        for cp in _join_copies(srcs, arrs, arrs, *sems):
            cp.start()
        token[...] = jnp.zeros_like(token)

    res = pl.pallas_call(
        body,
        name=name,
        in_specs=[HBM] * (2 * n),
        out_specs=[SEM] * 4 + [HBM] * (2 * n) + [pl.BlockSpec(memory_space=pltpu.VMEM)],
        out_shape=[pltpu.SemaphoreType.DMA((n,))] * 2 + [pltpu.SemaphoreType.DMA((3 * n,))] * 2
        + [pltpu.HBM(s.shape, s.dtype) for s in shards] + [pltpu.HBM(o.shape, o.dtype) for o in outs]
        + [jax.ShapeDtypeStruct((8, LANES), F32)],
        input_output_aliases={i: 4 + i for i in range(2 * n)},
        compiler_params=pltpu.CompilerParams(has_side_effects=SPLIT_COPY),
    )(*shards, *outs)
    return list(res[:4]), list(res[4:4 + n]), list(res[4 + n:4 + 2 * n]), res[-1]


def _join_wait(name, sems, shards, outs, after):
    n = len(shards)

    def body(*refs):
        srcs, arrs = refs[:n], refs[n:2 * n]
        for cp in _join_copies(srcs, arrs, arrs, *refs[2 * n:2 * n + 4]):
            cp.wait_send()
            cp.wait_recv()

    res = pl.pallas_call(
        body,
        name=name,
        in_specs=[HBM] * (2 * n) + [SEM] * 4 + [ANY] * len(after),
        out_specs=[HBM] * (2 * n),
        out_shape=[pltpu.HBM(s.shape, s.dtype) for s in shards] + [pltpu.HBM(o.shape, o.dtype) for o in outs],
        input_output_aliases={i: i for i in range(2 * n)},
        compiler_params=pltpu.CompilerParams(has_side_effects=SPLIT_COPY),
    )(*shards, *outs, *sems, *after)
    return list(res[n:])


def _pair_copy(grads, lands, send_sem, recv_sem, i):
    x, y, c = _position()
    half = grads[i].shape[1] // 2
    give = pl.ds(pl.multiple_of((1 - c) * half, 16), half)
    return pltpu.make_async_remote_copy(
        src_ref=grads[i].at[:, give, :], dst_ref=lands[i], send_sem=send_sem.at[i], recv_sem=recv_sem.at[i],
        device_id=(x, y, 1 - c), device_id_type=MESH)


def _pair_start(name, grads):
    n = len(grads)

    def body(*refs):
        srcs, lands = refs[:n], refs[n:2 * n]
        send_sem, recv_sem = refs[2 * n], refs[2 * n + 1]
        token = refs[-1]
        for i in range(n):
            _pair_copy(srcs, lands, send_sem, recv_sem, i).start()
        token[...] = jnp.zeros_like(token)

    halves = [(g.shape[0], g.shape[1] // 2, g.shape[2]) for g in grads]
    res = pl.pallas_call(
        body,
        name=name,
        in_specs=[HBM] * (2 * n),
        out_specs=[SEM, SEM] + [HBM] * (2 * n) + [pl.BlockSpec(memory_space=pltpu.VMEM)],
        out_shape=[pltpu.SemaphoreType.DMA((n,)), pltpu.SemaphoreType.DMA((n,))]
        + [pltpu.HBM(g.shape, g.dtype) for g in grads]
        + [pltpu.HBM(shp, g.dtype) for shp, g in zip(halves, grads)]
        + [jax.ShapeDtypeStruct((8, LANES), F32)],
        input_output_aliases={i: 2 + i for i in range(2 * n)},
        compiler_params=pltpu.CompilerParams(has_side_effects=SPLIT_COPY),
    )(*[pltpu.with_memory_space_constraint(g, pltpu.HBM) for g in grads],
      *[pltpu.with_memory_space_constraint(lax.empty(shp, g.dtype), pltpu.HBM) for shp, g in zip(halves, grads)])
    return res[0], res[1], list(res[2:2 + n]), list(res[2 + n:2 + 2 * n]), res[-1]


def _pair_wait(name, send_sem, recv_sem, grads, lands, after):
    n = len(grads)

    def body(*refs):
        srcs, land_refs = refs[:n], refs[n:2 * n]
        send_ref, recv_ref = refs[2 * n], refs[2 * n + 1]
        for i in range(n):
            copy = _pair_copy(srcs, land_refs, send_ref, recv_ref, i)
            copy.wait_send()
            copy.wait_recv()

    res = pl.pallas_call(
        body,
        name=name,
        in_specs=[HBM] * (2 * n) + [SEM, SEM, ANY],
        out_specs=[HBM] * (2 * n),
        out_shape=[pltpu.HBM(g.shape, g.dtype) for g in grads] + [pltpu.HBM(l.shape, l.dtype) for l in lands],
        input_output_aliases={i: i for i in range(2 * n)},
        compiler_params=pltpu.CompilerParams(has_side_effects=SPLIT_COPY),
    )(*grads, *lands, send_sem, recv_sem, after)
    return list(res[:n]), list(res[n:])


def _scatter_copy(srcs, lands, send_sem, recv_sem, i, j):
    x, y, c = _position()
    chips = _other_chips(x, y)
    return pltpu.make_async_remote_copy(
        src_ref=srcs[i].at[2 * chips[j][0] + chips[j][1]], dst_ref=lands[i].at[j],
        send_sem=send_sem.at[3 * i + j], recv_sem=recv_sem.at[3 * i + j],
        device_id=(chips[j][0], chips[j][1], c), device_id_type=MESH)


def _scatter_start(name, sums):
    n = len(sums)

    def body(*refs):
        srcs, lands = refs[:n], refs[n:2 * n]
        send_sem, recv_sem = refs[2 * n], refs[2 * n + 1]
        token = refs[-1]
        for i in range(n):
            for j in range(3):
                _scatter_copy(srcs, lands, send_sem, recv_sem, i, j).start()
        token[...] = jnp.zeros_like(token)

    land_shapes = [(3,) + s.shape[1:] for s in sums]
    res = pl.pallas_call(
        body,
        name=name,
        in_specs=[HBM] * (2 * n),
        out_specs=[SEM, SEM] + [HBM] * (2 * n) + [pl.BlockSpec(memory_space=pltpu.VMEM)],
        out_shape=[pltpu.SemaphoreType.DMA((3 * n,)), pltpu.SemaphoreType.DMA((3 * n,))]
        + [pltpu.HBM(s.shape, s.dtype) for s in sums]
        + [pltpu.HBM(shp, s.dtype) for shp, s in zip(land_shapes, sums)]
        + [jax.ShapeDtypeStruct((8, LANES), F32)],
        input_output_aliases={i: 2 + i for i in range(2 * n)},
        compiler_params=pltpu.CompilerParams(has_side_effects=SPLIT_COPY),
    )(*[pltpu.with_memory_space_constraint(s, pltpu.HBM) for s in sums],
      *[pltpu.with_memory_space_constraint(lax.empty(shp, s.dtype), pltpu.HBM) for shp, s in zip(land_shapes, sums)])
    return res[0], res[1], list(res[2:2 + n]), list(res[2 + n:2 + 2 * n]), res[-1]


def _scatter_wait(name, send_sem, recv_sem, sums, lands, after):
    n = len(sums)

    def body(*refs):
        srcs, land_refs = refs[:n], refs[n:2 * n]
        send_ref, recv_ref = refs[2 * n], refs[2 * n + 1]
        for i in range(n):
            for j in range(3):
                copy = _scatter_copy(srcs, land_refs, send_ref, recv_ref, i, j)
                copy.wait_send()
                copy.wait_recv()

    res = pl.pallas_call(
        body,
        name=name,
        in_specs=[HBM] * (2 * n) + [SEM, SEM, ANY],
        out_specs=[HBM] * (2 * n),
        out_shape=[pltpu.HBM(s.shape, s.dtype) for s in sums] + [pltpu.HBM(l.shape, l.dtype) for l in lands],
        input_output_aliases={i: i for i in range(2 * n)},
        compiler_params=pltpu.CompilerParams(has_side_effects=SPLIT_COPY),
    )(*sums, *lands, send_sem, recv_sem, after)
    return list(res[:n]), list(res[n:])


def _pair_join(name, halves, small=None):
    n = len(halves)
    if small is None:
        def body_plain(*refs):
            ins, outs = refs[:n], refs[n:2 * n]
            send_sem, recv_sem = refs[2 * n:]
            x, y, c = _position()
            swaps = [pltpu.make_async_remote_copy(
                src_ref=ins[i], dst_ref=outs[i], send_sem=send_sem.at[i], recv_sem=recv_sem.at[i],
                device_id=(x, y, 1 - c), device_id_type=MESH) for i in range(n)]
            for swap in swaps:
                swap.start()
            for swap in swaps:
                swap.wait()

        return pl.pallas_call(
            body_plain,
            name=name,
            in_specs=[ANY] * n,
            out_specs=[ANY] * n,
            out_shape=[jax.ShapeDtypeStruct(h.shape, h.dtype) for h in halves],
            scratch_shapes=[pltpu.SemaphoreType.DMA((n,))] * 2,
        )(*halves)

    def body(*refs):
        ins, small_ref = refs[:n], refs[n]
        outs, all_ref = refs[n + 1:2 * n + 1], refs[2 * n + 1]
        send_sem, recv_sem, sm_send, sm_recv, sm_local = refs[2 * n + 2:]
        x, y, c = _position()
        swaps = []
        for i in range(n):
            swap = pltpu.make_async_remote_copy(
                src_ref=ins[i], dst_ref=outs[i], send_sem=send_sem.at[i], recv_sem=recv_sem.at[i],
                device_id=(x, y, 1 - c), device_id_type=MESH)
            swap.start()
            swaps.append(swap)
        me = 4 * x + 2 * y + c
        sm_own = pltpu.make_async_copy(small_ref, all_ref.at[me], sm_local)
        sm_own.start()
        pushes, arrivals = [], []
        for mask in range(1, N_DEV):
            px, py, pc = x ^ (mask >> 2), y ^ ((mask >> 1) & 1), c ^ (mask & 1)
            pushes.append(pltpu.make_async_remote_copy(
                src_ref=small_ref, dst_ref=all_ref.at[me], send_sem=sm_send.at[mask - 1], recv_sem=sm_recv.at[mask - 1],
                device_id=(px, py, pc), device_id_type=MESH))
            arrivals.append(pltpu.make_async_remote_copy(
                src_ref=small_ref, dst_ref=all_ref.at[4 * px + 2 * py + pc], send_sem=sm_send.at[mask - 1],
                recv_sem=sm_recv.at[mask - 1], device_id=(px, py, pc), device_id_type=MESH))
        for cp in pushes:
            cp.start()
        for swap in swaps:
            swap.wait()
        for cp in arrivals:
            cp.wait_recv()
        for cp in pushes:
            cp.wait_send()
        sm_own.wait()

    res = pl.pallas_call(
        body,
        name=name,
        in_specs=[ANY] * (n + 1),
        out_specs=[ANY] * (n + 1),
        out_shape=[jax.ShapeDtypeStruct(h.shape, h.dtype) for h in halves]
        + [jax.ShapeDtypeStruct((N_DEV,) + small.shape, small.dtype)],
        scratch_shapes=[pltpu.SemaphoreType.DMA((n,))] * 2 + [pltpu.SemaphoreType.DMA((N_DEV - 1,))] * 2
        + [pltpu.SemaphoreType.DMA(())],
    )(*halves, small)
    return res[:n], res[n]


def _lower_bound(lbp):
    return jax.nn.softmax(lbp, axis=0)[0:1]


def _local_step(x, target, g1, gm, g2, gq, gk, go, rel_bias, lbp, weights, on_grads, grads_sent):
    b, s, d = x.shape
    t = b * s
    x0 = x.reshape(t, d)
    tgt = target.reshape(t, d)
    gq_t = jnp.tile(gq, (1, ATTN_HEADS))
    gk_t = jnp.tile(gk, (1, ATTN_HEADS))
    lb = _lower_bound(lbp)
    table = _band_table(_rel_bias_table("rel_bias_table", rel_bias))

    h1 = _rmsnorm_fwd("norm1", x0, g1)
    wg1, wu1, deps1 = weights["first"]((h1, table))
    a1, b1, z1 = _ffn_up("ffn1_up", h1, wg1, wu1, deps1)
    wd1, deps_mid = weights["mid"]((z1,))
    x1, h2 = _ffn_down("ffn1_down", z1, wd1, x0, gm, deps_mid)
    w_in, w_out = weights["mid_rest"]((x1,))
    ns = w_in.shape[0]
    proj = _in_proj("in_proj", h2, w_in)
    proj3 = proj.reshape(b, s, proj.shape[1])
    qn, kn, vb = _qk_prep("qk_prep", proj3, gq_t, gk_t)
    attn = _attn_fwd("attn_fwd", qn, kn, vb, table, weights["last_begin"]((qn,))).reshape(t, ATTN_W)
    mix, oraw, states = _hgrn_fwd("hgrn_fwd", proj, attn, lb, go, b, s)
    x2, h3 = _out_proj("out_proj", mix, w_out, x1, g2)
    wg2, wu2, wd2 = weights["last"]((h3,))
    a2, b2, z2 = _ffn_up("ffn2_up", h3, wg2, wu2)
    dy, dyh, sq = _ffn_down_loss("ffn2_down_loss", z2, wd2, x2, tgt)
    loss = 0.5 * jnp.sum(sq) / d

    da2, db2 = _ffn_bwd_act("ffn2_bwd_act", dyh, wd2, a2, b2)
    dwd2 = _grad_w_shardrows("ffn2_dwd", z2, dyh)
    dwg2 = _grad_w_shardrows("ffn2_dwg", da2, h3)
    dwu2 = _grad_w_shardrows("ffn2_dwu", db2, h3)
    sent2 = on_grads("ffn2", {"ffn2_w_gate": dwg2, "ffn2_w_up": dwu2, "ffn2_w_down": dwd2})
    dx2, dx2b, dg2 = _ffn_bwd_in("ffn2_bwd_in", da2, db2, wg2, wu2, x2, g2, dy, 1.0, sent2)
    sent2 = grads_sent("ffn2", dx2b)

    dwout = _grad_w_out("dw_out", mix, dx2b)
    dmix = _out_proj_bwd("out_proj_bwd", dx2b, w_out, sent2)
    dqn, dkn, dvn, dbe, dbo = _attn_bwd("attn_bwd", qn, kn, vb, table, dmix.reshape(b, s, dmix.shape[1]))
    dbias = dbe[:, :, :BAND] + dbo[:, :, CHUNK:]
    dpq, dpk, dpv, dgq, dgk = _qk_prep_bwd("qk_prep_bwd", proj3, dqn, dkn, dvn, gq_t, gk_t)
    dpq, dpk, dpv = (a.reshape(t, ATTN_W) for a in (dpq, dpk, dpv))
    dproj, dlb, dgo = _hgrn_bwd("hgrn_bwd", proj, (dpq, dpk, dpv), lb, go, oraw, states, dmix, b, s)
    dwin = _grad_w_in("dw_in", h2, dproj, ns)
    dx1, dx1h, dgm = _in_proj_bwd("in_proj_bwd", dproj, w_in, x1, gm, dx2, 0.5)

    dwd1 = _grad_w_shardrows("ffn1_dwd", z1, dx1h)
    sent_mix = on_grads("mix", {"w_in": dwin, "w_out": dwout.reshape(ns, dwout.shape[0] // ns, d),
                                "ffn1_w_down": dwd1})
    da1, db1 = _ffn_bwd_act("ffn1_bwd_act", dx1h, wd1, a1, b1, sent_mix)
    sent_mix = grads_sent("mix", da1)
    dwg1 = _grad_w_shardrows("ffn1_dwg", da1, h1, sent_mix)
    dwu1 = _grad_w_shardrows("ffn1_dwu", db1, h1)
    on_grads("ffn1", {"ffn1_w_gate": dwg1, "ffn1_w_up": dwu1})
    sent1 = grads_sent("ffn1", None)
    dx0, dg1 = _ffn_bwd_in("ffn1_bwd_in", da1, db1, wg1, wu1, x0, g1, dx1, None, sent1)

    nt = dg1.shape[0]
    sg = _small_grads(
        "small_grads", dg1.reshape(nt, d), dgm.reshape(nt, d), dg2.reshape(nt, d),
        dgq.reshape(-1, ATTN_W), dgk.reshape(-1, ATTN_W), dbias.transpose(1, 0, 2),
        dlb.reshape(b, HGRN_W), dgo.reshape(b, HGRN_W), lbp)
    g1g, gmg, g2g, gqg, gkg, rbg, lbg, gog = sg
    small = _pack_small(g1g, gmg, g2g, lbg, rbg[:, :N_REL], gqg, gkg, gog, loss)
    return dx0.reshape(b, s, d), small


LOSS_SLOT = 7 * SMALL_COLS + 2 * ATTN_DH + HGRN_DH


def _pack_small(g1, gm, g2, lbp, rel_bias, gq, gk, go, loss=None):
    flat = [g1.reshape(-1), gm.reshape(-1), g2.reshape(-1), lbp.reshape(-1), rel_bias.reshape(-1)]
    n_bias = 3 * SMALL_COLS - rel_bias.size
    heads = [gq.reshape(-1), gk.reshape(-1), go.reshape(-1)]
    heads.append(jnp.zeros((1,), F32) if loss is None else loss.reshape(1))
    n_tail = SMALL_COLS - sum(h.size for h in heads)
    return jnp.concatenate(flat + [jnp.zeros((n_bias,), F32)] + heads + [jnp.zeros((n_tail,), F32)]).reshape(
        SMALL_ROWS, SMALL_COLS)


def _unpack_small(p, d):
    flat = p.reshape(-1)
    o = 3 * d
    g1, gm, g2 = p[0:1], p[1:2], p[2:3]
    lbp = flat[o:o + 2 * HGRN_W].reshape(2, HGRN_W)
    o = 4 * SMALL_COLS
    rel = flat[o:o + ATTN_HEADS * N_REL].reshape(1, ATTN_HEADS, N_REL)
    o = 7 * SMALL_COLS
    gq = flat[o:o + ATTN_DH].reshape(1, ATTN_DH)
    gk = flat[o + ATTN_DH:o + 2 * ATTN_DH].reshape(1, ATTN_DH)
    go = flat[o + 2 * ATTN_DH:o + 2 * ATTN_DH + HGRN_DH].reshape(1, HGRN_DH)
    return g1, gm, g2, gq, gk, rel, lbp, go


def kernel(x, ffn1_norm_g, ffn1_w_gate, ffn1_w_up, ffn1_w_down, mix_norm_g, w_in, attn_q_norm_g, attn_k_norm_g, attn_rel_bias, hgrn_lower_bounds, hgrn_out_norm_g, w_out, ffn2_norm_g, ffn2_w_gate, ffn2_w_up, ffn2_w_down, loss_target, m_ffn1_norm_g, m_ffn1_w_gate, m_ffn1_w_up, m_ffn1_w_down, m_mix_norm_g, m_w_in, m_attn_q_norm_g, m_attn_k_norm_g, m_attn_rel_bias, m_hgrn_lower_bounds, m_hgrn_out_norm_g, m_w_out, m_ffn2_norm_g, m_ffn2_w_gate, m_ffn2_w_up, m_ffn2_w_down, v_ffn1_norm_g, v_ffn1_w_gate, v_ffn1_w_up, v_ffn1_w_down, v_mix_norm_g, v_w_in, v_attn_q_norm_g, v_attn_k_norm_g, v_attn_rel_bias, v_hgrn_lower_bounds, v_hgrn_out_norm_g, v_w_out, v_ffn2_norm_g, v_ffn2_w_gate, v_ffn2_w_up, v_ffn2_w_down):
    d = x.shape[-1]
    big_w = [ffn1_w_gate, ffn1_w_up, ffn1_w_down, w_in, w_out, ffn2_w_gate, ffn2_w_up, ffn2_w_down]
    big_m = [m_ffn1_w_gate, m_ffn1_w_up, m_ffn1_w_down, m_w_in, m_w_out, m_ffn2_w_gate, m_ffn2_w_up, m_ffn2_w_down]
    big_v = [v_ffn1_w_gate, v_ffn1_w_up, v_ffn1_w_down, v_w_in, v_w_out, v_ffn2_w_gate, v_ffn2_w_up, v_ffn2_w_down]
    big_names = ["ffn1_w_gate", "ffn1_w_up", "ffn1_w_down", "w_in", "w_out", "ffn2_w_gate", "ffn2_w_up", "ffn2_w_down"]
    flipped = {nm for nm in big_names if nm.endswith("gate") or nm.endswith("up")}
    flip = lambda nm, a: jnp.swapaxes(a, 1, 2) if nm in flipped else a
    big_w, big_m, big_v = ([flip(nm, a) for nm, a in zip(big_names, arrs)] for arrs in (big_w, big_m, big_v))

    shards = [w[0].astype(BF16) for w in big_w]
    start_a = _gather_start("gather_start_up1", shards[:2], ())
    start_b = _gather_start("gather_start_mid", shards[2:5], (start_a[4],))
    start_c = _gather_start("gather_start_ffn2", shards[5:], (start_b[4],))

    pending = {}

    def arrived(tag, started, after):
        send_sem, recv_sem, srcs, outs, _ = started
        return _gather_wait("gather_wait_" + tag, send_sem, recv_sem, srcs, outs, after)

    def first_weights(after):
        return (*_gather_join("gather_join_up1", *arrived("up1", start_a, after)), (start_c[4],))

    def mid_weights(after):
        srcs, outs = arrived("mid", start_b, after)
        (wd1,) = _gather_join("gather_join_wd1", srcs[:1], outs[:1])
        pending["mid"] = _join_start("join_start_mid", srcs[1:], outs[1:])
        return wd1, (pending["mid"][3],)

    def mid_rest(after):
        sems, srcs, outs, _ = pending["mid"]
        win_f, wout_f = _join_wait("join_wait_mid", sems, srcs, outs, after)
        return win_f, wout_f.reshape(wout_f.shape[0] * wout_f.shape[1], d)

    def last_begin(after):
        pending["ffn2"] = _join_start("join_start_ffn2", *arrived("ffn2", start_c, after))
        return (pending["ffn2"][3],)

    def last_weights(after):
        sems, srcs, outs, _ = pending["ffn2"]
        return _join_wait("join_wait_ffn2", sems, srcs, outs, after)

    weights = {"first": first_weights, "mid": mid_weights, "mid_rest": mid_rest, "last_begin": last_begin,
               "last": last_weights}

    core = lax.axis_index("c").astype(jnp.int32).reshape(1)
    chip = (2 * lax.axis_index("x") + lax.axis_index("y")).astype(jnp.int32).reshape(1)
    started = {}

    def on_grads(tag, grads):
        names = list(grads)
        started[tag] = (names, _pair_start("pair_start_" + tag, [grads[nm] for nm in names]))
        return (started[tag][1][4],)

    def grads_sent(tag, after):
        names, (send_sem, recv_sem, grads, lands, token) = started[tag]
        grads, theirs = _pair_wait("pair_wait_" + tag, send_sem, recv_sem, grads, lands, token if after is None else after)
        sums = [_pair_sum("pair_sum_" + nm, g, th, core) for nm, g, th in zip(names, grads, theirs)]
        started[tag] = (names, _scatter_start("scatter_start_" + tag, sums))
        return (started[tag][1][4],)

    grad_x, small_g = _local_step(
        x, loss_target, ffn1_norm_g, mix_norm_g, ffn2_norm_g, attn_q_norm_g, attn_k_norm_g, hgrn_out_norm_g,
        attn_rel_bias[0], hgrn_lower_bounds, weights, on_grads, grads_sent)

    def finish(tag, after):
        names, (send_sem, recv_sem, sums, lands, _) = started[tag]
        sums, lands = _scatter_wait("scatter_wait_" + tag, send_sem, recv_sem, sums, lands, after)
        return names, [_chip_sum("chip_sum_" + nm, sm, ld, chip) for nm, sm, ld in zip(names, sums, lands)]

    by_name = {nm: (w, m, v) for nm, w, m, v in zip(big_names, big_w, big_m, big_v)}
    updated = {}

    def update(names, halves, other_halves):
        for nm, mine, theirs in zip(names, halves, other_halves):
            w, m, v = by_name[nm]
            updated[nm] = _adamw("adamw_" + nm, w, mine, theirs, m, v, core)

    last_token = started["ffn1"][1][4]
    names_a, halves_a = finish("ffn2", last_token)
    names_m, halves_m = finish("mix", last_token)
    names_a, halves_a = names_a + names_m, halves_a + halves_m
    update(names_a, halves_a, _pair_join("pair_join_early", halves_a))
    names_b, halves_b = finish("ffn1", updated[names_a[-1]][1])
    others_b, small_all = _pair_join("pair_join_last", halves_b, small_g)
    update(names_b, halves_b, others_b)
    big_out = [updated[nm] for nm in big_names]

    pack = lambda g1, gm, g2, gq, gk, rel, lbp, go: _pack_small(g1, gm, g2, lbp, rel[0], gq, gk, go)
    small_w = pack(ffn1_norm_g, mix_norm_g, ffn2_norm_g, attn_q_norm_g, attn_k_norm_g, attn_rel_bias, hgrn_lower_bounds, hgrn_out_norm_g)
    small_m = pack(m_ffn1_norm_g, m_mix_norm_g, m_ffn2_norm_g, m_attn_q_norm_g, m_attn_k_norm_g, m_attn_rel_bias, m_hgrn_lower_bounds, m_hgrn_out_norm_g)
    small_v = pack(v_ffn1_norm_g, v_mix_norm_g, v_ffn2_norm_g, v_attn_q_norm_g, v_attn_k_norm_g, v_attn_rel_bias, v_hgrn_lower_bounds, v_hgrn_out_norm_g)
    small_res = _adamw_small("adamw_small", small_w, small_all, small_m, small_v)
    small_out = [_unpack_small(p, d) for p in small_res]
    loss = small_res[0].reshape(-1)[LOSS_SLOT]

    def assemble(kind):
        bg = [flip(nm, o[kind]) for nm, o in zip(big_names, big_out)]
        g1, gm, g2, gq, gk, rel, lbp, go = small_out[kind]
        return [g1, bg[0], bg[1], bg[2], gm, bg[3], gq, gk, rel, lbp, go, bg[4], g2, bg[5], bg[6], bg[7]]

    return (loss, grad_x, *assemble(0), *assemble(1), *assemble(2), *assemble(3))
```

```python
import functools

import jax
import jax.numpy as jnp
from jax import lax
from jax.experimental import pallas as pl
from jax.experimental.pallas import tpu as pltpu

F32 = jnp.float32
BF16 = jnp.bfloat16
MESH = pl.DeviceIdType.MESH

N_CHIPS = 4
N_DEV = 8
CHUNK = 64
ATTN_HEADS = 8
ATTN_DH = 64
ATTN_W = ATTN_HEADS * ATTN_DH
HGRN_HEADS = 4
HGRN_DH = 128
HGRN_W = HGRN_HEADS * HGRN_DH
LEFT_CHUNKS = 8
BAND = (LEFT_CHUNKS + 1) * CHUNK
KPAD = LEFT_CHUNKS * CHUNK
REL_CLIP = 128
N_REL = 2 * REL_CLIP + 1
N_REL_PAD = 384
RMS_EPS = 1e-6
LANES = 128
SMALL_ROWS = 8
SMALL_COLS = 1024

ADAM_LR = 0.001
ADAM_B1 = 0.9
ADAM_B2 = 0.999
ADAM_EPS = 1e-08
ADAM_WD = 0.01
ADAM_STEP = 10

NN = (((1,), (0,)), ((), ()))
NT = (((1,), (1,)), ((), ()))
TN = (((0,), (0,)), ((), ()))

VMEM_LIMIT = 48 * 1024 * 1024


def _sigmoid(x):
    return 1.0 / (1.0 + jnp.exp(-x))


def _silu(x):
    return x * _sigmoid(x)


def _dot(a, b, dims=NN):
    return lax.dot_general(a, b, dims, preferred_element_type=F32)


def _split3(x):
    hi = x.astype(BF16)
    r1 = x - hi.astype(F32)
    mid = r1.astype(BF16)
    lo = (r1 - mid.astype(F32)).astype(BF16)
    return hi, mid, lo


def _dot_exact_rhs(x, mat, dims=NN, pieces=3):
    hi, mid, lo = _split3(x)
    out = _dot(hi, mat, dims) + _dot(mid, mat, dims)
    return out + _dot(lo, mat, dims) if pieces == 3 else out


def _dot_exact_lhs(mat, x, dims=NN):
    hi, mid, lo = _split3(x)
    return _dot(mat, hi, dims) + _dot(mat, mid, dims) + _dot(mat, lo, dims)


def _params(*sem):
    return pltpu.CompilerParams(dimension_semantics=sem, vmem_limit_bytes=VMEM_LIMIT)


def _mm(name, ins, terms, n_acc, grid, acc_shape, outs, epilogue, extras=(), deps=()):
    nk = grid[2]
    ni, ne, nd, no = len(ins), len(extras), len(deps), len(outs)

    def body(*refs):
        in_refs = refs[:ni]
        ex_refs = refs[ni:ni + ne]
        out_refs = refs[ni + ne + nd:ni + ne + nd + no]
        acc_refs = refs[ni + ne + nd + no:]
        parts = [None] * n_acc
        for ai, li, ri, dims in terms:
            d = _dot(in_refs[li][...], in_refs[ri][...], dims)
            parts[ai] = d if parts[ai] is None else parts[ai] + d

        def finish(accs):
            res = epilogue(accs, [e[...] for e in ex_refs])
            for o, r in zip(out_refs, res):
                o[...] = r.astype(o.dtype)

        if nk == 1:
            finish(parts)
        else:
            k = pl.program_id(2)

            @pl.when(k == 0)
            def _():
                for a, p in zip(acc_refs, parts):
                    a[...] = p

            @pl.when(k > 0)
            def _():
                for a, p in zip(acc_refs, parts):
                    a[...] += p

            @pl.when(k == nk - 1)
            def _():
                finish([a[...] for a in acc_refs])

    scratch = [] if nk == 1 else [pltpu.VMEM(acc_shape, F32) for _ in range(n_acc)]
    res = pl.pallas_call(
        body,
        name=name,
        grid=grid,
        in_specs=[s for _, s in ins] + [s for _, s in extras] + [pl.BlockSpec(memory_space=pl.ANY)] * nd,
        out_specs=[s for _, s in outs],
        out_shape=[o for o, _ in outs],
        scratch_shapes=scratch,
        compiler_params=_params("parallel", "parallel", "arbitrary"),
    )(*[a for a, _ in ins], *[a for a, _ in extras], *deps)
    return res


def _mm_rows(name, lhs, weights, dims, t, outs, epilogue, extras=(), deps=()):
    tm = _row_tile(t)
    nl, ne, nd, no = len(lhs), len(extras), len(deps), len(outs)
    ns = weights[0].shape[0]

    def body(*refs):
        lhs_refs = refs[:nl]
        w_hbm = refs[nl:2 * nl]
        ex_refs = refs[2 * nl:2 * nl + ne]
        out_refs = refs[2 * nl + ne + nd:2 * nl + ne + nd + no]
        w_vmem = refs[2 * nl + ne + nd + no:3 * nl + ne + nd + no]
        sem = refs[-1]

        @pl.when(pl.program_id(0) == 0)
        def _():
            copies = [pltpu.make_async_copy(w_hbm[p], w_vmem[p], sem.at[p]) for p in range(nl)]
            for cp in copies:
                cp.start()
            for cp in copies:
                cp.wait()

        acc = None
        for p in range(nl):
            pick = lhs[p][2]
            for j in range(ns):
                part = _dot(pick(lhs_refs[p], j), w_vmem[p][j], dims)
                acc = part if acc is None else acc + part
        res = epilogue([acc], [e[...] for e in ex_refs])
        for o, r in zip(out_refs, res):
            o[...] = r.astype(o.dtype)

    return pl.pallas_call(
        body,
        name=name,
        grid=(t // tm,),
        in_specs=[s for _, s, _ in lhs] + [pl.BlockSpec(memory_space=pl.ANY)] * nl + [s for _, s in extras]
        + [pl.BlockSpec(memory_space=pl.ANY)] * nd,
        out_specs=[s for _, s in outs],
        out_shape=[o for o, _ in outs],
        scratch_shapes=[pltpu.VMEM(w.shape, w.dtype) for w in weights] + [pltpu.SemaphoreType.DMA((nl,))],
        compiler_params=_params("arbitrary"),
    )(*[a for a, _, _ in lhs], *weights, *[a for a, _ in extras], *deps)


def _mm_shards(name, x, weights, dims, outs, epilogue, extras=(), deps=()):
    t = x.shape[0]
    tm = _row_tile(t)
    nw, ne, nd, no = len(weights), len(extras), len(deps), len(outs)
    ns = weights[0].shape[0]

    def body(*refs):
        x_ref = refs[0]
        w_hbm = refs[1:1 + nw]
        ex_refs = refs[1 + nw:1 + nw + ne]
        out_refs = refs[1 + nw + ne + nd:1 + nw + ne + nd + no]
        w_vmem = refs[1 + nw + ne + nd + no:1 + 2 * nw + ne + nd + no]
        sem = refs[-1]

        @pl.when(pl.program_id(0) == 0)
        def _():
            copies = [pltpu.make_async_copy(w_hbm[p], w_vmem[p], sem.at[p]) for p in range(nw)]
            for cp in copies:
                cp.start()
            for cp in copies:
                cp.wait()

        xv = x_ref[...]
        accs = [_dot(xv, w_vmem[p][0], dims) for p in range(nw)]
        for j in range(ns):
            nxt = [_dot(xv, w_vmem[p][j + 1], dims) for p in range(nw)] if j + 1 < ns else None
            res = epilogue(accs, [e[j] for e in ex_refs])
            for (_, _, store), o, r in zip(outs, out_refs, res):
                store(o, j, r.astype(o.dtype))
            accs = nxt

    return pl.pallas_call(
        body,
        name=name,
        grid=(t // tm,),
        in_specs=[pl.BlockSpec((tm, x.shape[1]), lambda i: (i, 0))] + [pl.BlockSpec(memory_space=pl.ANY)] * nw
        + [s for _, s in extras] + [pl.BlockSpec(memory_space=pl.ANY)] * nd,
        out_specs=[s for _, s, _ in outs],
        out_shape=[o for o, _, _ in outs],
        scratch_shapes=[pltpu.VMEM(w.shape, w.dtype) for w in weights] + [pltpu.SemaphoreType.DMA((nw,))],
        compiler_params=_params("arbitrary"),
    )(x, *weights, *[a for a, _ in extras], *deps)


def _store_shard(ref, j, value):
    ref[j] = value


def _row_tile(t):
    return 512 if t % 512 == 0 else t


def _k_tile(t):
    return t if t <= 4096 else 1024


def _rmsnorm(xv, g):
    ms = jnp.mean(xv * xv, axis=-1, keepdims=True)
    return xv * lax.rsqrt(ms + RMS_EPS) * g


def _rmsnorm_fwd(name, x, g):
    t, d = x.shape
    tm = _row_tile(t)

    def body(x_ref, g_ref, h_ref):
        h_ref[...] = _rmsnorm(x_ref[...], g_ref[...]).astype(BF16)

    return pl.pallas_call(
        body,
        name=name,
        grid=(t // tm,),
        in_specs=[pl.BlockSpec((tm, d), lambda i: (i, 0)), pl.BlockSpec((1, d), lambda i: (0, 0))],
        out_specs=pl.BlockSpec((tm, d), lambda i: (i, 0)),
        out_shape=jax.ShapeDtypeStruct((t, d), BF16),
        compiler_params=_params("parallel"),
    )(x, g)


def _norm_bwd_epilogue(copy_scale):
    def epilogue(accs, ex):
        dh = accs[0]
        xv, g, dres = ex
        ms = jnp.mean(xv * xv, axis=-1, keepdims=True)
        rstd = lax.rsqrt(ms + RMS_EPS)
        xhat = xv * rstd
        dxhat = dh * g
        dx = rstd * (dxhat - xhat * jnp.mean(dxhat * xhat, axis=-1, keepdims=True))
        out = dres + dx
        dg = jnp.sum(dh * xhat, axis=0, keepdims=True)
        if copy_scale is None:
            return out, dg
        return out, out * copy_scale, dg

    return epilogue


def _ffn_up(name, h, wg, wu, deps=()):
    t, d = h.shape
    ns, f, _ = wg.shape
    tm = _row_tile(t)

    def epilogue(accs, ex):
        a, b = accs
        sg = _sigmoid(a)
        act = a * sg
        return act, b * (sg * (1.0 + a * (1.0 - sg))), act * b

    out = (jax.ShapeDtypeStruct((ns, t, f), BF16), pl.BlockSpec((ns, tm, f), lambda i: (0, i, 0)), _store_shard)
    return _mm_shards(name, h, [wg, wu], NT, [out] * 3, epilogue, deps=deps)


def _shard_rows(arr, tm):
    ns, _, f = arr.shape
    return arr, pl.BlockSpec((ns, tm, f), lambda i: (0, i, 0)), lambda ref, j: ref[j]


def _ffn_down(name, z, wd, x, g_next, deps=()):
    _, t, _ = z.shape
    d = wd.shape[2]
    tm = _row_tile(t)
    row = pl.BlockSpec((tm, d), lambda i: (i, 0))

    def epilogue(accs, ex):
        y = ex[0] + 0.5 * accs[0]
        return y, _rmsnorm(y, ex[1])

    return _mm_rows(
        name, [_shard_rows(z, tm)], [wd], NN, t,
        outs=[(jax.ShapeDtypeStruct((t, d), F32), row), (jax.ShapeDtypeStruct((t, d), BF16), row)],
        epilogue=epilogue,
        extras=[(x, row), (g_next, pl.BlockSpec((1, d), lambda i: (0, 0)))],
        deps=deps,
    )


def _ffn_down_loss(name, z, wd, x, target):
    _, t, _ = z.shape
    d = wd.shape[2]
    tm = _row_tile(t)
    nt = t // tm
    row = pl.BlockSpec((tm, d), lambda i: (i, 0))

    def epilogue(accs, ex):
        e = ex[0] + 0.5 * accs[0] - ex[1]
        dy = e * (1.0 / d)
        return dy, 0.5 * dy, jnp.sum(e * e, axis=0, keepdims=True)

    return _mm_rows(
        name, [_shard_rows(z, tm)], [wd], NN, t,
        outs=[(jax.ShapeDtypeStruct((t, d), F32), row), (jax.ShapeDtypeStruct((t, d), BF16), row),
              (jax.ShapeDtypeStruct((nt, 1, d), F32), pl.BlockSpec((None, 1, d), lambda i: (i, 0, 0)))],
        epilogue=epilogue,
        extras=[(x, row), (target, row)],
    )


def _ffn_bwd_act(name, dout, wd, act_a, dact_b, deps=()):
    t, d = dout.shape
    ns, f, _ = wd.shape
    tm = _row_tile(t)

    def epilogue(accs, ex):
        dz = accs[0]
        return dz * ex[1].astype(F32), dz * ex[0].astype(F32)

    act = pl.BlockSpec((ns, tm, f), lambda i: (0, i, 0))
    out = (jax.ShapeDtypeStruct((ns, t, f), BF16), act, _store_shard)
    return _mm_shards(name, dout, [wd], NT, [out] * 2, epilogue, extras=[(act_a, act), (dact_b, act)], deps=deps)


def _grad_w_shardrows(name, z, dout, deps=()):
    ns, t, f = z.shape
    d = dout.shape[1]
    tk = _k_tile(t)
    return _mm(
        name,
        ins=[(z, pl.BlockSpec((None, tk, f), lambda j, n, k: (j, k, 0))),
             (dout, pl.BlockSpec((tk, d), lambda j, n, k: (k, 0)))],
        terms=[(0, 0, 1, TN)],
        n_acc=1,
        grid=(ns, 1, t // tk),
        acc_shape=(f, d),
        outs=[(jax.ShapeDtypeStruct((ns, f, d), BF16), pl.BlockSpec((None, f, d), lambda j, n, k: (j, 0, 0)))],
        epilogue=lambda accs, ex: (accs[0],),
        deps=deps,
    )[0]


def _norm_bwd_outs(t, d, tm, copy_scale):
    row = pl.BlockSpec((tm, d), lambda i: (i, 0))
    outs = [(jax.ShapeDtypeStruct((t, d), F32), row)]
    if copy_scale is not None:
        outs.append((jax.ShapeDtypeStruct((t, d), BF16), row))
    outs.append((jax.ShapeDtypeStruct((t // tm, 1, d), F32), pl.BlockSpec((None, 1, d), lambda i: (i, 0, 0))))
    return row, outs


def _ffn_bwd_in(name, da, db, wg, wu, x, g, dres, copy_scale, deps=()):
    _, t, _ = da.shape
    d = wg.shape[2]
    tm = _row_tile(t)
    row, outs = _norm_bwd_outs(t, d, tm, copy_scale)
    return _mm_rows(
        name, [_shard_rows(da, tm), _shard_rows(db, tm)], [wg, wu], NN, t,
        outs=outs,
        epilogue=_norm_bwd_epilogue(copy_scale),
        extras=[(x, row), (g, pl.BlockSpec((1, d), lambda i: (0, 0))), (dres, row)],
        deps=deps,
    )


def _in_proj(name, h, w_in):
    t, d = h.shape
    ns, _, pj = w_in.shape
    tm = _row_tile(t)
    def store(ref, j, value):
        ref[:, j * pj:(j + 1) * pj] = value

    out = (jax.ShapeDtypeStruct((t, ns * pj), F32), pl.BlockSpec((tm, ns * pj), lambda i: (i, 0)), store)
    return _mm_shards(name, h, [w_in], NN, [out], lambda accs, ex: (accs[0],))[0]


def _in_proj_bwd(name, dp, w_in, x, g, dres, copy_scale, deps=()):
    t = dp.shape[0]
    ns, d, pj = w_in.shape
    tm = _row_tile(t)
    row, outs = _norm_bwd_outs(t, d, tm, copy_scale)
    cols = (dp, pl.BlockSpec((tm, ns * pj), lambda i: (i, 0)), lambda ref, j: ref[:, j * pj:(j + 1) * pj])
    return _mm_rows(
        name, [cols], [w_in], NT, t,
        outs=outs,
        epilogue=_norm_bwd_epilogue(copy_scale),
        extras=[(x, row), (g, pl.BlockSpec((1, d), lambda i: (0, 0))), (dres, row)],
        deps=deps,
    )


def _grad_w_in(name, h, dp, ns):
    t, d = h.shape
    pj = dp.shape[1] // ns
    tk = _k_tile(t)
    return _mm(
        name,
        ins=[(h, pl.BlockSpec((tk, d), lambda j, n, k: (k, 0))),
             (dp, pl.BlockSpec((tk, pj), lambda j, n, k: (k, j)))],
        terms=[(0, 0, 1, TN)],
        n_acc=1,
        grid=(ns, 1, t // tk),
        acc_shape=(d, pj),
        outs=[(jax.ShapeDtypeStruct((ns, d, pj), BF16), pl.BlockSpec((None, d, pj), lambda j, n, k: (j, 0, 0)))],
        epilogue=lambda accs, ex: (accs[0],),
    )[0]


def _out_proj(name, mix, w_out, x, g_next):
    t, dm = mix.shape
    d = w_out.shape[1]
    tm = _row_tile(t)
    row = pl.BlockSpec((tm, d), lambda i, n, k: (i, 0))
    return _mm(
        name,
        ins=[(mix, pl.BlockSpec((tm, dm), lambda i, n, k: (i, 0))),
             (w_out, pl.BlockSpec((dm, d), lambda i, n, k: (0, 0)))],
        terms=[(0, 0, 1, NN)],
        n_acc=1,
        grid=(t // tm, 1, 1),
        acc_shape=(tm, d),
        outs=[(jax.ShapeDtypeStruct((t, d), F32), row), (jax.ShapeDtypeStruct((t, d), BF16), row)],
        epilogue=lambda accs, ex: (ex[0] + accs[0], _rmsnorm(ex[0] + accs[0], ex[1])),
        extras=[(x, row), (g_next, pl.BlockSpec((1, d), lambda i, n, k: (0, 0)))],
    )


def _out_proj_bwd(name, dx, w_out, deps=()):
    t, d = dx.shape
    dm = w_out.shape[0]
    tm = _row_tile(t)
    return _mm(
        name,
        ins=[(dx, pl.BlockSpec((tm, d), lambda i, n, k: (i, 0))),
             (w_out, pl.BlockSpec((dm, d), lambda i, n, k: (0, 0)))],
        terms=[(0, 0, 1, NT)],
        n_acc=1,
        grid=(t // tm, 1, 1),
        acc_shape=(tm, dm),
        outs=[(jax.ShapeDtypeStruct((t, dm), F32), pl.BlockSpec((tm, dm), lambda i, n, k: (i, 0)))],
        epilogue=lambda accs, ex: (accs[0],),
        deps=deps,
    )[0]


def _grad_w_out(name, mix, dx):
    t, dm = mix.shape
    d = dx.shape[1]
    tk = _k_tile(t)
    return _mm(
        name,
        ins=[(mix, pl.BlockSpec((tk, dm), lambda a, n, k: (k, 0))),
             (dx, pl.BlockSpec((tk, d), lambda a, n, k: (k, 0)))],
        terms=[(0, 0, 1, TN)],
        n_acc=1,
        grid=(1, 1, t // tk),
        acc_shape=(dm, d),
        outs=[(jax.ShapeDtypeStruct((dm, d), BF16), pl.BlockSpec((dm, d), lambda a, n, k: (0, 0)))],
        epilogue=lambda accs, ex: (accs[0],),
    )[0]


def _head_group_matrix():
    r = lax.broadcasted_iota(jnp.int32, (ATTN_W, ATTN_W), 0)
    c = lax.broadcasted_iota(jnp.int32, (ATTN_W, ATTN_W), 1)
    same = jnp.right_shift(r, 6) == jnp.right_shift(c, 6)
    return jnp.where(same, 1.0, 0.0).astype(BF16)


def _qk_prep(name, proj, gq, gk):
    b, s, _ = proj.shape
    tm = KPAD
    nb = s // tm

    def body(q_ref, k_ref, v_ref, gq_ref, gk_ref, qn_ref, kn_ref, vb_ref):
        j = pl.program_id(1)
        bd = _head_group_matrix()

        def norm(xv, g):
            ms = _dot_exact_rhs(xv * xv, bd, pieces=2) * (1.0 / ATTN_DH)
            return xv * lax.rsqrt(ms + RMS_EPS) * g

        @pl.when(j == 0)
        def _():
            kn_ref[...] = jnp.zeros_like(kn_ref)
            vb_ref[...] = jnp.zeros_like(vb_ref)

        @pl.when(j > 0)
        def _():
            qn_ref[...] = norm(q_ref[...], gq_ref[...]).astype(BF16)
            kn_ref[...] = norm(k_ref[...], gk_ref[...]).astype(BF16)
            vb_ref[...] = v_ref[...].astype(BF16)

    src_blk = lambda col: pl.BlockSpec((None, tm, ATTN_W), lambda bi, j: (bi, jnp.maximum(j - 1, 0), col))
    gspec = pl.BlockSpec((1, ATTN_W), lambda bi, j: (0, 0))
    padded = pl.BlockSpec((None, tm, ATTN_W), lambda bi, j: (bi, j, 0))
    return pl.pallas_call(
        body,
        name=name,
        grid=(b, nb + 1),
        in_specs=[src_blk(0), src_blk(1), src_blk(2), gspec, gspec],
        out_specs=[src_blk(0), padded, padded],
        out_shape=[jax.ShapeDtypeStruct((b, s, ATTN_W), BF16), jax.ShapeDtypeStruct((b, KPAD + s, ATTN_W), BF16),
                   jax.ShapeDtypeStruct((b, KPAD + s, ATTN_W), BF16)],
        compiler_params=_params("parallel", "arbitrary"),
    )(proj, proj, proj, gq, gk)


def _qk_prep_bwd(name, proj, dqn, dkn, dv, gq, gk):
    b, s, _ = proj.shape
    tm = KPAD
    nb = s // tm

    def body(q_ref, k_ref, dqn_ref, dkn_ref, dv_ref, gq_ref, gk_ref, dq_ref, dk_ref, dvb_ref, dgq_ref, dgk_ref):
        bd = _head_group_matrix()

        def bwd(xv, dy, g):
            ms = _dot_exact_rhs(xv * xv, bd, pieces=2) * (1.0 / ATTN_DH)
            rstd = lax.rsqrt(ms + RMS_EPS)
            xhat = xv * rstd
            dxhat = dy * g
            gm = _dot_exact_rhs(dxhat * xhat, bd, pieces=2) * (1.0 / ATTN_DH)
            return rstd * (dxhat - xhat * gm), jnp.sum(dy * xhat, axis=0, keepdims=True)

        dq, dgq = bwd(q_ref[...], dqn_ref[...], gq_ref[...])
        dk, dgk = bwd(k_ref[...], dkn_ref[...], gk_ref[...])
        dq_ref[...] = dq.astype(BF16)
        dk_ref[...] = dk.astype(BF16)
        dvb_ref[...] = dv_ref[...].astype(BF16)
        dgq_ref[...] = dgq
        dgk_ref[...] = dgk

    col = lambda c: pl.BlockSpec((None, tm, ATTN_W), lambda bi, j: (bi, j, c))
    past_pad = pl.BlockSpec((None, tm, ATTN_W), lambda bi, j: (bi, j + 1, 0))
    gspec = pl.BlockSpec((1, ATTN_W), lambda bi, j: (0, 0))
    pspec = pl.BlockSpec((None, 1, ATTN_W), lambda bi, j: (bi * nb + j, 0, 0))
    o_shape = jax.ShapeDtypeStruct((b, s, ATTN_W), BF16)
    p_shape = jax.ShapeDtypeStruct((b * nb, 1, ATTN_W), F32)
    return pl.pallas_call(
        body,
        name=name,
        grid=(b, nb),
        in_specs=[col(0), col(1), col(0), past_pad, past_pad, gspec, gspec],
        out_specs=[col(0)] * 3 + [pspec] * 2,
        out_shape=[o_shape] * 3 + [p_shape] * 2,
        compiler_params=_params("parallel", "parallel"),
    )(proj, proj, dqn, dkn, dv, gq, gk)


Q_CHUNKS = 4
QBLK = Q_CHUNKS * CHUNK
WIN = (LEFT_CHUNKS + Q_CHUNKS) * CHUNK
DB_W = BAND + CHUNK
MASKED = -1e30


def _band_table(bias):
    rows = [jnp.pad(bias, ((0, 0), (0, 0), (CHUNK * i, WIN - BAND - CHUNK * i)), constant_values=MASKED)
            for i in range(Q_CHUNKS)]
    return jnp.concatenate(rows, axis=1)


def _head_lanes(hh):
    lane = lax.broadcasted_iota(jnp.int32, (1, LANES), 1)
    return (lane < ATTN_DH) if hh == 0 else (lane >= ATTN_DH)


def _attn_probs(qh, kw, table, start):
    s = _dot(qh, kw, NT) * (ATTN_DH ** -0.5) + table
    col = lax.broadcasted_iota(jnp.int32, (QBLK, WIN), 1)
    s = jnp.where(col + start >= KPAD, s, MASKED)
    m = jnp.max(s, axis=-1, keepdims=True)
    p = jnp.exp(s - m)
    return p * (1.0 / jnp.sum(p, axis=-1, keepdims=True))


def _attn_fwd(name, q, k, v, table, deps=()):
    b, s, w = q.shape
    sp = k.shape[1]

    def body(q_ref, k_ref, v_ref, t_ref, *rest):
        o_ref = rest[-1]
        start = pl.multiple_of(pl.program_id(2) * QBLK, QBLK)
        kw = k_ref[pl.ds(start, WIN), :]
        vw = v_ref[pl.ds(start, WIN), :]
        q2 = q_ref[...]
        lanes = [_head_lanes(hh) for hh in range(2)]
        probs = [_attn_probs(jnp.where(mine, q2, jnp.zeros_like(q2)), kw, t_ref[hh], start).astype(BF16)
                 for hh, mine in enumerate(lanes)]
        outs = [_dot(p, vw) for p in probs]
        o_ref[...] = jnp.where(lanes[0], outs[0], outs[1]).astype(BF16)

    qspec = pl.BlockSpec((None, QBLK, LANES), lambda p, bi, i: (bi, i, p))
    kspec = pl.BlockSpec((None, sp, LANES), lambda p, bi, i: (bi, 0, p))
    return pl.pallas_call(
        body,
        name=name,
        grid=(w // LANES, b, s // QBLK),
        in_specs=[qspec, kspec, kspec, pl.BlockSpec((2, QBLK, WIN), lambda p, bi, i: (p, 0, 0))] + [ANY] * len(deps),
        out_specs=qspec,
        out_shape=jax.ShapeDtypeStruct((b, s, w), BF16),
        compiler_params=_params("parallel", "parallel", "arbitrary"),
    )(q, k, v, table, *deps)


def _attn_bwd(name, q, k, v, table, dmix):
    b, s, w = q.shape
    sp = k.shape[1]

    def body(q_ref, k_ref, v_ref, t_ref, do_ref, dq_ref, dk_ref, dv_ref, dbe_ref, dbo_ref):
        bi = pl.program_id(1)
        i = pl.program_id(2)
        start = pl.multiple_of(i * QBLK, QBLK)
        win = pl.ds(start, WIN)

        @pl.when(i == 0)
        def _():
            dk_ref[...] = jnp.zeros_like(dk_ref)
            dv_ref[...] = jnp.zeros_like(dv_ref)

        @pl.when(jnp.logical_and(i == 0, bi == 0))
        def _():
            dbe_ref[...] = jnp.zeros_like(dbe_ref)
            dbo_ref[...] = jnp.zeros_like(dbo_ref)

        kw = k_ref[win, :]
        vw = v_ref[win, :]
        q2 = q_ref[...]
        do2 = do_ref[...].astype(BF16)
        lanes = [_head_lanes(hh) for hh in range(2)]
        qh = [jnp.where(mine, q2, jnp.zeros_like(q2)) for mine in lanes]
        doh = [jnp.where(mine, do2, jnp.zeros_like(do2)) for mine in lanes]
        p = [_attn_probs(qh[hh], kw, t_ref[hh], start) for hh in range(2)]
        dp = [_dot(doh[hh], vw, NT) for hh in range(2)]
        ds = [p[hh] * (dp[hh] - jnp.sum(p[hh] * dp[hh], axis=-1, keepdims=True)) for hh in range(2)]
        dsb = [(x * (ATTN_DH ** -0.5)).astype(BF16) for x in ds]
        pb = [x.astype(BF16) for x in p]
        dq = [_dot(dsb[hh], kw) for hh in range(2)]
        dk = [_dot(dsb[hh], qh[hh], TN) for hh in range(2)]
        dv = [_dot(pb[hh], doh[hh], TN) for hh in range(2)]
        for hh in range(2):
            for qi in range(Q_CHUNKS):
                c0 = (qi // 2) * LANES
                blk = ds[hh][qi * CHUNK:(qi + 1) * CHUNK, c0:c0 + DB_W]
                if qi % 2 == 0:
                    dbe_ref[hh] += blk
                else:
                    dbo_ref[hh] += blk
        dq_ref[...] = jnp.where(lanes[0], dq[0], dq[1])
        dk_ref[win, :] += dk[0] + dk[1]
        dv_ref[win, :] += dv[0] + dv[1]

    qspec = pl.BlockSpec((None, QBLK, LANES), lambda p, bi, i: (bi, i, p))
    kspec = pl.BlockSpec((None, sp, LANES), lambda p, bi, i: (bi, 0, p))
    dbspec = pl.BlockSpec((2, CHUNK, DB_W), lambda p, bi, i: (p, 0, 0))
    db_shape = jax.ShapeDtypeStruct((ATTN_HEADS, CHUNK, DB_W), F32)
    return pl.pallas_call(
        body,
        name=name,
        grid=(w // LANES, b, s // QBLK),
        in_specs=[qspec, kspec, kspec, pl.BlockSpec((2, QBLK, WIN), lambda p, bi, i: (p, 0, 0)), qspec],
        out_specs=[qspec, kspec, kspec, dbspec, dbspec],
        out_shape=[jax.ShapeDtypeStruct((b, s, w), F32), jax.ShapeDtypeStruct((b, sp, w), F32),
                   jax.ShapeDtypeStruct((b, sp, w), F32), db_shape, db_shape],
        compiler_params=_params("arbitrary", "arbitrary", "arbitrary"),
    )(q, k, v, table, dmix)


HQ_COL = 3 * ATTN_W // HGRN_DH
HF_COL = HQ_COL + HGRN_HEADS
HI_COL = HF_COL + HGRN_HEADS
HG_COL = HI_COL + HGRN_HEADS
HGRN_ROWS = 8 * CHUNK
HEAD_LANES = [slice(hh * HGRN_DH, (hh + 1) * HGRN_DH) for hh in range(HGRN_HEADS)]


def _tri(lower):
    r = lax.broadcasted_iota(jnp.int32, (CHUNK, CHUNK), 0)
    c = lax.broadcasted_iota(jnp.int32, (CHUNK, CHUNK), 1)
    return (r >= c) if lower else (r <= c)


def _hgrn_chunk(hq, hf, lb, tril):
    sig = _sigmoid(hf)
    f = lb + (1.0 - lb) * sig
    g = jnp.log(f)
    ones_l = jnp.where(tril, 1.0, 0.0).astype(BF16)
    b = _dot_exact_lhs(ones_l, g)
    bl = jnp.sum(g, axis=0, keepdims=True)
    rows = lax.broadcasted_iota(jnp.int32, g.shape, 0)
    bm = jnp.sum(jnp.where(rows <= CHUNK // 2, g, 0.0), axis=0, keepdims=True)
    sq = _sigmoid(hq)
    q = hq * sq
    k = 1.0 - f
    return sig, f, b, bl, bm, sq, q, k


def _hgrn_fwd(name, proj, attn, lb, go, b, s):
    nc = s // CHUNK
    t = b * s
    nblk = s // HGRN_ROWS
    cpb = HGRN_ROWS // CHUNK

    def body(hq_ref, hf_ref, hi_ref, hg_ref, attn_ref, lb_ref, go_ref, mix_ref, oraw_ref, st_ref, s_scr):
        tril = _tri(True)
        gov = go_ref[...]
        mix_ref[:, 0:ATTN_W] = attn_ref[...]

        @pl.when(pl.program_id(1) == 0)
        def _():
            s_scr[...] = jnp.zeros_like(s_scr)

        def step(c, carry):
            sl = pl.ds(pl.multiple_of(c * CHUNK, CHUNK), CHUNK)
            hg = hg_ref[sl, :]
            _, _, bb, bl, bm, _, q, k = _hgrn_chunk(hq_ref[sl, :], hf_ref[sl, :], lb_ref[...], tril)
            vb = hi_ref[sl, :].astype(BF16)
            qe = (q * jnp.exp(bb - bm)).astype(BF16)
            ke = (k * jnp.exp(bm - bb)).astype(BF16)
            qb = (q * jnp.exp(bb)).astype(BF16)
            kb = (k * jnp.exp(bl - bb)).astype(BF16)
            e_last = jnp.exp(bl)
            gate = _silu(hg)
            st = [s_scr[hh] for hh in range(HGRN_HEADS)]
            a = [jnp.where(tril, _dot(qe[:, hs], ke[:, hs], NT), 0.0).astype(BF16) for hs in HEAD_LANES]
            o_state = [_dot(qb[:, hs], st[hh].astype(BF16), NT) for hh, hs in enumerate(HEAD_LANES)]
            st_next = [st[hh] * e_last[:, hs] + _dot(vb[:, hs], kb[:, hs], TN) for hh, hs in enumerate(HEAD_LANES)]
            o = [_dot(a[hh], vb[:, hs]) + o_state[hh] for hh, hs in enumerate(HEAD_LANES)]
            ro = [(oh * lax.rsqrt(jnp.mean(oh * oh, axis=-1, keepdims=True) + RMS_EPS) * gov) * gate[:, hs]
                  for oh, hs in zip(o, HEAD_LANES)]
            for hh in range(HGRN_HEADS):
                st_ref[hh, c] = st[hh]
                s_scr[hh] = st_next[hh]
            mix_ref[sl, ATTN_W:ATTN_W + HGRN_W] = jnp.concatenate(ro, axis=1).astype(BF16)
            oraw_ref[sl, :] = jnp.concatenate(o, axis=1)
            return carry

        lax.fori_loop(0, cpb, step, 0)

    col = lambda base: pl.BlockSpec((HGRN_ROWS, HGRN_W), lambda bi, i: (bi * nblk + i, base // HGRN_HEADS))
    out = pl.BlockSpec((HGRN_ROWS, HGRN_W), lambda bi, i: (bi * nblk + i, 0))
    return pl.pallas_call(
        body,
        name=name,
        grid=(b, nblk),
        in_specs=[col(HQ_COL), col(HF_COL), col(HI_COL), col(HG_COL), out,
                  pl.BlockSpec((1, HGRN_W), lambda bi, i: (0, 0)), pl.BlockSpec((1, HGRN_DH), lambda bi, i: (0, 0))],
        out_specs=[pl.BlockSpec((HGRN_ROWS, ATTN_W + HGRN_W), lambda bi, i: (bi * nblk + i, 0)), out,
                   pl.BlockSpec((None, HGRN_HEADS, cpb, HGRN_DH, HGRN_DH), lambda bi, i: (bi, 0, i, 0, 0))],
        out_shape=[jax.ShapeDtypeStruct((t, ATTN_W + HGRN_W), BF16), jax.ShapeDtypeStruct((t, HGRN_W), F32),
                   jax.ShapeDtypeStruct((b, HGRN_HEADS, nc, HGRN_DH, HGRN_DH), F32)],
        scratch_shapes=[pltpu.VMEM((HGRN_HEADS, HGRN_DH, HGRN_DH), F32)],
        compiler_params=_params("parallel", "arbitrary"),
    )(proj, proj, proj, proj, attn, lb, go)


def _hgrn_bwd(name, proj, dqkv, lb, go, oraw, states, dmix, b, s):
    t = b * s
    nblk = s // HGRN_ROWS
    cpb = HGRN_ROWS // CHUNK

    def body(hq_ref, hf_ref, hi_ref, hg_ref, dq_ref, dk_ref, dv_ref, lb_ref, go_ref, oraw_ref, st_ref, dro_ref,
             dp_ref, dlb_ref, dgo_ref, ds_scr, dlb_scr, dgo_scr):
        tril = _tri(True)
        ones_u = jnp.where(_tri(False), 1.0, 0.0).astype(BF16)
        gov = go_ref[...]
        dp_ref[:, 0:ATTN_W] = dq_ref[...]
        dp_ref[:, ATTN_W:2 * ATTN_W] = dk_ref[...]
        dp_ref[:, 2 * ATTN_W:3 * ATTN_W] = dv_ref[...]

        @pl.when(pl.program_id(1) == 0)
        def _():
            ds_scr[...] = jnp.zeros_like(ds_scr)
            dlb_scr[...] = jnp.zeros_like(dlb_scr)
            dgo_scr[...] = jnp.zeros_like(dgo_scr)

        def step(ci, carry):
            c = cpb - 1 - ci
            sl = pl.ds(pl.multiple_of(c * CHUNK, CHUNK), CHUNK)
            hq = hq_ref[sl, :]
            hg = hg_ref[sl, :]
            sig, f, bb, bl, bm, sq, q, k = _hgrn_chunk(hq, hf_ref[sl, :], lb_ref[...], tril)
            vb = hi_ref[sl, :].astype(BF16)
            ebm = jnp.exp(bb - bm)
            embm = jnp.exp(bm - bb)
            eb = jnp.exp(bb)
            ebl = jnp.exp(bl - bb)
            e_last = jnp.exp(bl)
            qe = (q * ebm).astype(BF16)
            ke = (k * embm).astype(BF16)
            qb = (q * eb).astype(BF16)
            kb = (k * ebl).astype(BF16)
            st = [st_ref[hh, c] for hh in range(HGRN_HEADS)]
            dst = [ds_scr[hh] for hh in range(HGRN_HEADS)]
            o = oraw_ref[sl, :]
            dro = dro_ref[sl, :]
            sg = _sigmoid(hg)
            gov4 = jnp.concatenate([gov] * HGRN_HEADS, axis=1)
            rstd = jnp.concatenate(
                [jnp.broadcast_to(lax.rsqrt(jnp.mean(o[:, hs] * o[:, hs], axis=-1, keepdims=True) + RMS_EPS),
                                  (CHUNK, HGRN_DH)) for hs in HEAD_LANES], axis=1)
            ohat = o * rstd
            dn = dro * (hg * sg)
            dhg = dro * (ohat * gov4) * (sg * (1.0 + hg * (1.0 - sg)))
            dgo_inc = jnp.sum(dn * ohat, axis=0, keepdims=True)
            dohat = dn * gov4
            proj_h = dohat * ohat
            pm = jnp.concatenate(
                [jnp.broadcast_to(jnp.mean(proj_h[:, hs], axis=-1, keepdims=True), (CHUNK, HGRN_DH))
                 for hs in HEAD_LANES], axis=1)
            dob = (rstd * (dohat - ohat * pm)).astype(BF16)
            stb = [x.astype(BF16) for x in st]
            dstb = [x.astype(BF16) for x in dst]
            a = [jnp.where(tril, _dot(qe[:, hs], ke[:, hs], NT), 0.0).astype(BF16) for hs in HEAD_LANES]
            dab = [jnp.where(tril, _dot(dob[:, hs], vb[:, hs], NT), 0.0).astype(BF16) for hs in HEAD_LANES]
            dqb = [_dot(dob[:, hs], stb[hh]) for hh, hs in enumerate(HEAD_LANES)]
            dkb = [_dot(vb[:, hs], dstb[hh]) for hh, hs in enumerate(HEAD_LANES)]
            dv_state = [_dot(kb[:, hs], dstb[hh], NT) for hh, hs in enumerate(HEAD_LANES)]
            dst_next = [dst[hh] * e_last[:, hs] + _dot(dob[:, hs], qb[:, hs], TN) for hh, hs in enumerate(HEAD_LANES)]
            dv = [_dot(a[hh], dob[:, hs], TN) + dv_state[hh] for hh, hs in enumerate(HEAD_LANES)]
            dqe = jnp.concatenate([_dot(dab[hh], ke[:, hs]) for hh, hs in enumerate(HEAD_LANES)], axis=1)
            dke = jnp.concatenate([_dot(dab[hh], qe[:, hs], TN) for hh, hs in enumerate(HEAD_LANES)], axis=1)
            dqb = jnp.concatenate(dqb, axis=1)
            dkb = jnp.concatenate(dkb, axis=1)
            state_term = jnp.concatenate(
                [jnp.sum(dst[hh] * st[hh], axis=0, keepdims=True) for hh in range(HGRN_HEADS)], axis=1)
            dq = dqe * ebm + dqb * eb
            dk = dke * embm + dkb * ebl
            db = (qe.astype(F32) * dqe - ke.astype(F32) * dke) + q * (dqb * eb) - k * (dkb * ebl)
            d_last = jnp.sum(k * ebl * dkb, axis=0, keepdims=True) + state_term * e_last
            dg = _dot_exact_lhs(ones_u, db) + d_last
            df = dg / f - dk
            first = HQ_COL * HGRN_DH
            dp_ref[sl, first:first + HGRN_W] = (dq * (sq * (1.0 + hq * (1.0 - sq)))).astype(BF16)
            dp_ref[sl, first + HGRN_W:first + 2 * HGRN_W] = (df * (1.0 - lb_ref[...]) * sig * (1.0 - sig)).astype(BF16)
            dp_ref[sl, first + 2 * HGRN_W:first + 3 * HGRN_W] = jnp.concatenate(dv, axis=1).astype(BF16)
            dp_ref[sl, first + 3 * HGRN_W:first + 4 * HGRN_W] = dhg.astype(BF16)
            dlb_scr[...] += jnp.sum(df * (1.0 - sig), axis=0, keepdims=True)
            dgo_scr[...] += dgo_inc
            for hh in range(HGRN_HEADS):
                ds_scr[hh] = dst_next[hh]
            return carry

        lax.fori_loop(0, cpb, step, 0)

        @pl.when(pl.program_id(1) == nblk - 1)
        def _():
            dlb_ref[...] = dlb_scr[...]
            dgo_ref[...] = dgo_scr[...]

    rows = lambda bi, i: bi * nblk + (nblk - 1 - i)
    col = lambda base: pl.BlockSpec((HGRN_ROWS, HGRN_W), lambda bi, i: (rows(bi, i), base // HGRN_HEADS))
    out = pl.BlockSpec((HGRN_ROWS, HGRN_W), lambda bi, i: (rows(bi, i), 0))
    part = pl.BlockSpec((None, 1, HGRN_W), lambda bi, i: (bi, 0, 0))
    width = HG_COL * HGRN_DH + HGRN_W
    o_shape = jax.ShapeDtypeStruct((t, width), BF16)
    p_shape = jax.ShapeDtypeStruct((b, 1, HGRN_W), F32)
    return pl.pallas_call(
        body,
        name=name,
        grid=(b, nblk),
        in_specs=[col(HQ_COL), col(HF_COL), col(HI_COL), col(HG_COL), out, out, out,
                  pl.BlockSpec((1, HGRN_W), lambda bi, i: (0, 0)), pl.BlockSpec((1, HGRN_DH), lambda bi, i: (0, 0)), out,
                  pl.BlockSpec((None, HGRN_HEADS, cpb, HGRN_DH, HGRN_DH), lambda bi, i: (bi, 0, nblk - 1 - i, 0, 0)),
                  col(ATTN_W // HGRN_DH)],
        out_specs=[pl.BlockSpec((HGRN_ROWS, width), lambda bi, i: (rows(bi, i), 0))] + [part] * 2,
        out_shape=[o_shape] + [p_shape] * 2,
        scratch_shapes=[pltpu.VMEM((HGRN_HEADS, HGRN_DH, HGRN_DH), F32), pltpu.VMEM((1, HGRN_W), F32),
                        pltpu.VMEM((1, HGRN_W), F32)],
        compiler_params=_params("parallel", "arbitrary"),
    )(proj, proj, proj, proj, *dqkv, lb, go, oraw, states, dmix)


def _small_grads(name, dg1, dgm, dg2, dgq, dgk, dbias_t, dlb, dgo, lbp):
    d = dg1.shape[1]

    def body(dg1_ref, dgm_ref, dg2_ref, dgq_ref, dgk_ref, dbias_ref, dlb_ref, dgo_ref, lbp_ref,
             g1_ref, gm_ref, g2_ref, gq_ref, gk_ref, rb_ref, lbg_ref, go_ref):
        g1_ref[...] = jnp.sum(dg1_ref[...], axis=0, keepdims=True)
        gm_ref[...] = jnp.sum(dgm_ref[...], axis=0, keepdims=True)
        g2_ref[...] = jnp.sum(dg2_ref[...], axis=0, keepdims=True)
        r = lax.broadcasted_iota(jnp.int32, (ATTN_W, ATTN_DH), 0)
        cidx = lax.broadcasted_iota(jnp.int32, (ATTN_W, ATTN_DH), 1)
        fold = jnp.where(jnp.bitwise_and(r, ATTN_DH - 1) == cidx, 1.0, 0.0).astype(BF16)
        gq_ref[...] = jnp.sum(_dot_exact_rhs(dgq_ref[...], fold), axis=0, keepdims=True)
        gk_ref[...] = jnp.sum(_dot_exact_rhs(dgk_ref[...], fold), axis=0, keepdims=True)
        gosum = jnp.sum(dgo_ref[...], axis=0, keepdims=True)
        go_ref[...] = (gosum[:, 0:HGRN_DH] + gosum[:, HGRN_DH:2 * HGRN_DH]
                       + gosum[:, 2 * HGRN_DH:3 * HGRN_DH] + gosum[:, 3 * HGRN_DH:4 * HGRN_DH])
        p0 = lbp_ref[0:1, :]
        p1 = lbp_ref[1:2, :]
        lbv = 1.0 / (1.0 + jnp.exp(p1 - p0))
        dp0 = jnp.sum(dlb_ref[...], axis=0, keepdims=True) * lbv * (1.0 - lbv)
        lbg_ref[0:1, :] = dp0
        lbg_ref[1:2, :] = -dp0
        sidx = lax.broadcasted_iota(jnp.int32, (BAND, N_REL_PAD), 0)
        ridx = lax.broadcasted_iota(jnp.int32, (BAND, N_REL_PAD), 1)

        def step(tq, acc):
            rel = jnp.clip(tq + KPAD - sidx, -REL_CLIP, REL_CLIP) + REL_CLIP
            onehot = jnp.where(rel == ridx, 1.0, 0.0).astype(BF16)
            return acc + _dot_exact_rhs(dbias_ref[tq], onehot)

        rb_ref[...] = lax.fori_loop(0, CHUNK, step, jnp.zeros((ATTN_HEADS, N_REL_PAD), F32))

    ins = [dg1, dgm, dg2, dgq, dgk, dbias_t, dlb, dgo, lbp]
    outs = [jax.ShapeDtypeStruct((1, d), F32)] * 3 + [jax.ShapeDtypeStruct((1, ATTN_DH), F32)] * 2 + [
        jax.ShapeDtypeStruct((ATTN_HEADS, N_REL_PAD), F32), jax.ShapeDtypeStruct((2, HGRN_W), F32),
        jax.ShapeDtypeStruct((1, HGRN_DH), F32)]
    vm = pl.BlockSpec(memory_space=pltpu.VMEM)
    return pl.pallas_call(
        body,
        name=name,
        in_specs=[vm] * len(ins),
        out_specs=[vm] * len(outs),
        out_shape=outs,
        compiler_params=pltpu.CompilerParams(vmem_limit_bytes=VMEM_LIMIT),
    )(*ins)


def _adam_update(w, g, m, v):
    m2 = ADAM_B1 * m + (1.0 - ADAM_B1) * g
    v2 = ADAM_B2 * v + (1.0 - ADAM_B2) * (g * g)
    m_hat = m2 / (1.0 - ADAM_B1 ** ADAM_STEP)
    v_hat = v2 / (1.0 - ADAM_B2 ** ADAM_STEP)
    delta = -ADAM_LR * (m_hat / (jnp.sqrt(v_hat) + ADAM_EPS) + ADAM_WD * w)
    return delta, m2, v2


def _rows_tile(r):
    return r if r <= 512 or r % 512 else 512


def _pair_sum(name, grad, theirs, core):
    n, half, c = theirs.shape
    tr = _rows_tile(half)
    nth = half // tr

    def body(core_ref, a_ref, b_ref, o_ref):
        o_ref[...] = (a_ref[...].astype(F32) + b_ref[...].astype(F32)).astype(o_ref.dtype)

    spec = pl.BlockSpec((None, tr, c), lambda i, j, core_ref: (i, j, 0))
    return pl.pallas_call(
        body, name=name,
        grid_spec=pltpu.PrefetchScalarGridSpec(
            num_scalar_prefetch=1, grid=(n, nth),
            in_specs=[pl.BlockSpec((None, tr, c), lambda i, j, core_ref: (i, core_ref[0] * nth + j, 0)), spec],
            out_specs=spec),
        out_shape=jax.ShapeDtypeStruct((n, half, c), BF16), compiler_params=_params("parallel", "parallel"),
    )(core, grad, theirs)


def _chip_sum(name, own, parts, chip):
    _, half, c = own.shape
    tr = _rows_tile(half)

    def body(chip_ref, own_ref, p_ref, o_ref):
        me = chip_ref[0]
        mine = own_ref[...].astype(F32)
        flip_x, flip_y, flip_xy = (p_ref[i].astype(F32) for i in range(3))
        acc = None
        for k in range(N_CHIPS):
            rel = jnp.bitwise_xor(me, k)
            term = jnp.where(rel == 0, mine, jnp.where(rel == 2, flip_x, jnp.where(rel == 1, flip_y, flip_xy)))
            acc = term if acc is None else acc + term
        o_ref[...] = acc

    return pl.pallas_call(
        body, name=name,
        grid_spec=pltpu.PrefetchScalarGridSpec(
            num_scalar_prefetch=1, grid=(half // tr,),
            in_specs=[pl.BlockSpec((None, tr, c), lambda j, chip_ref: (chip_ref[0], j, 0)),
                      pl.BlockSpec((3, tr, c), lambda j, chip_ref: (0, j, 0))],
            out_specs=pl.BlockSpec((tr, c), lambda j, chip_ref: (j, 0))),
        out_shape=jax.ShapeDtypeStruct((half, c), F32), compiler_params=_params("parallel"),
    )(chip, own, parts)


def _adamw(name, w, g_mine, g_theirs, m, v, core):
    _, r, c = w.shape
    half = r // 2
    tr = _rows_tile(half)
    nth = half // tr

    def body(core_ref, w_ref, gm_ref, gt_ref, m_ref, v_ref, g_ref, d_ref, m2_ref, v2_ref):
        g = jnp.where(pl.program_id(0) == core_ref[0], gm_ref[...], gt_ref[...])
        delta, m2, v2 = _adam_update(w_ref[...], g, m_ref[...], v_ref[...])
        g_ref[...] = g
        d_ref[...] = delta
        m2_ref[...] = m2
        v2_ref[...] = v2

    full = pl.BlockSpec((None, tr, c), lambda h, j, core_ref: (0, h * nth + j, 0))
    part = pl.BlockSpec((tr, c), lambda h, j, core_ref: (j, 0))
    shape = jax.ShapeDtypeStruct((1, r, c), F32)
    return pl.pallas_call(
        body, name=name,
        grid_spec=pltpu.PrefetchScalarGridSpec(
            num_scalar_prefetch=1, grid=(2, nth), in_specs=[full, part, part, full, full], out_specs=[full] * 4),
        out_shape=[shape] * 4, compiler_params=_params("parallel", "parallel"),
    )(core, w, g_mine, g_theirs, m, v)


def _rel_bias_table(name, rel_bias):
    padded = jnp.pad(rel_bias, ((0, 0), (0, N_REL_PAD - N_REL)))

    def body(rb_ref, o_ref):
        ridx = lax.broadcasted_iota(jnp.int32, (N_REL_PAD, BAND), 0)
        sidx = lax.broadcasted_iota(jnp.int32, (N_REL_PAD, BAND), 1)
        rb = rb_ref[...]

        def step(tq, carry):
            rel = jnp.clip(tq + KPAD - sidx, -REL_CLIP, REL_CLIP) + REL_CLIP
            onehot = jnp.where(rel == ridx, 1.0, 0.0).astype(BF16)
            o_ref[tq] = _dot_exact_rhs(rb, onehot)
            return carry

        lax.fori_loop(0, CHUNK, step, 0)

    vm = pl.BlockSpec(memory_space=pltpu.VMEM)
    table = pl.pallas_call(
        body, name=name, in_specs=[vm], out_specs=vm,
        out_shape=jax.ShapeDtypeStruct((CHUNK, ATTN_HEADS, BAND), F32),
    )(padded)
    return table.transpose(1, 0, 2)


def _adamw_small(name, w, parts, m, v):
    def body(w_ref, p_ref, m_ref, v_ref, g_ref, d_ref, m2_ref, v2_ref):
        g = p_ref[0]
        for i in range(1, N_DEV):
            g = g + p_ref[i]
        delta, m2, v2 = _adam_update(w_ref[...], g, m_ref[...], v_ref[...])
        g_ref[...] = g
        d_ref[...] = delta
        m2_ref[...] = m2
        v2_ref[...] = v2

    vm = pl.BlockSpec(memory_space=pltpu.VMEM)
    shape = jax.ShapeDtypeStruct((SMALL_ROWS, SMALL_COLS), F32)
    return pl.pallas_call(
        body, name=name, in_specs=[vm] * 4, out_specs=[vm] * 4, out_shape=[shape] * 4,
    )(w, parts, m, v)


def _position():
    return lax.axis_index("x"), lax.axis_index("y"), lax.axis_index("c")


def _other_chips(x, y):
    return [(1 - x, y), (x, 1 - y), (1 - x, 1 - y)]


ANY = pl.BlockSpec(memory_space=pl.ANY)
PAIR_ID = 0


def _pair_handshake():
    x, y, c = _position()
    barrier = pltpu.get_barrier_semaphore()
    pl.semaphore_signal(barrier, inc=1, device_id=(x, y, 1 - c), device_id_type=MESH)
    pl.semaphore_wait(barrier, 1)


PAIR_CALL = pltpu.CompilerParams(collective_id=PAIR_ID)


HBM = pl.BlockSpec(memory_space=pltpu.HBM)
SEM = pl.BlockSpec(memory_space=pltpu.SEMAPHORE)
SPLIT_COPY = pltpu.SideEffectType.DATAFLOW_SIDE_EFFECTING


def _gather_copy(shards, outs, send_sem, recv_sem, i, j):
    x, y, c = _position()
    chips = _other_chips(x, y)
    half = shards[i].shape[0] // 2
    rows = pl.ds(pl.multiple_of(c * half, 16), half)
    return pltpu.make_async_remote_copy(
        src_ref=shards[i].at[rows, :], dst_ref=outs[i].at[2 * x + y, rows, :],
        send_sem=send_sem.at[3 * i + j], recv_sem=recv_sem.at[3 * i + j],
        device_id=(chips[j][0], chips[j][1], c), device_id_type=MESH)


def _gather_start(name, shards, after):
    n = len(shards)

    def body(*refs):
        srcs, outs = refs[:n], refs[n:2 * n]
        send_sem, recv_sem = refs[2 * n + len(after)], refs[2 * n + len(after) + 1]
        token = refs[-1]
        for i in range(n):
            for j in range(3):
                _gather_copy(srcs, outs, send_sem, recv_sem, i, j).start()
        token[...] = jnp.zeros_like(token)

    full = [(N_CHIPS,) + s.shape for s in shards]
    res = pl.pallas_call(
        body,
        name=name,
        in_specs=[HBM] * (2 * n) + [ANY] * len(after),
        out_specs=[SEM, SEM] + [HBM] * (2 * n) + [pl.BlockSpec(memory_space=pltpu.VMEM)],
        out_shape=[pltpu.SemaphoreType.DMA((3 * n,)), pltpu.SemaphoreType.DMA((3 * n,))]
        + [pltpu.HBM(s.shape, s.dtype) for s in shards]
        + [pltpu.HBM(shp, s.dtype) for shp, s in zip(full, shards)]
        + [jax.ShapeDtypeStruct((8, LANES), F32)],
        input_output_aliases={i: 2 + i for i in range(2 * n)},
        compiler_params=pltpu.CompilerParams(has_side_effects=SPLIT_COPY),
    )(*[pltpu.with_memory_space_constraint(s, pltpu.HBM) for s in shards],
      *[pltpu.with_memory_space_constraint(lax.empty(shp, s.dtype), pltpu.HBM) for shp, s in zip(full, shards)],
      *after)
    return res[0], res[1], list(res[2:2 + n]), list(res[2 + n:2 + 2 * n]), res[-1]


def _gather_wait(name, send_sem, recv_sem, shards, outs, after):
    n = len(shards)

    def body(*refs):
        srcs, out_refs = refs[:n], refs[n:2 * n]
        send_ref, recv_ref = refs[2 * n], refs[2 * n + 1]
        for i in range(n):
            for j in range(3):
                copy = _gather_copy(srcs, out_refs, send_ref, recv_ref, i, j)
                copy.wait_send()
                copy.wait_recv()

    res = pl.pallas_call(
        body,
        name=name,
        in_specs=[HBM] * (2 * n) + [SEM, SEM] + [ANY] * len(after),
        out_specs=[HBM] * (2 * n),
        out_shape=[pltpu.HBM(s.shape, s.dtype) for s in shards] + [pltpu.HBM(o.shape, o.dtype) for o in outs],
        input_output_aliases={i: i for i in range(2 * n)},
        compiler_params=pltpu.CompilerParams(has_side_effects=SPLIT_COPY),
    )(*shards, *outs, send_sem, recv_sem, *after)
    return list(res[:n]), list(res[n:])


def _join_copies(srcs, ins, outs, own_send, own_recv, half_send, half_recv):
    x, y, c = _position()
    chips = _other_chips(x, y)
    copies = []
    for i in range(len(srcs)):
        copies.append(pltpu.make_async_remote_copy(
            src_ref=srcs[i], dst_ref=outs[i].at[2 * x + y], send_sem=own_send.at[i], recv_sem=own_recv.at[i],
            device_id=(x, y, 1 - c), device_id_type=MESH))
        half = srcs[i].shape[0] // 2
        rows = pl.ds(pl.multiple_of(c * half, 16), half)
        for j in range(3):
            slot = 2 * chips[j][0] + chips[j][1]
            copies.append(pltpu.make_async_remote_copy(
                src_ref=ins[i].at[slot, rows, :], dst_ref=outs[i].at[slot, rows, :],
                send_sem=half_send.at[3 * i + j], recv_sem=half_recv.at[3 * i + j],
                device_id=(x, y, 1 - c), device_id_type=MESH))
    return copies


def _gather_join(name, shards, outs):
    n = len(shards)

    def body(*refs):
        _pair_handshake()
        copies = _join_copies(refs[:n], refs[n:2 * n], refs[2 * n:3 * n], *refs[3 * n:])
        for cp in copies:
            cp.start()
        for cp in copies:
            cp.wait()

    return pl.pallas_call(
        body,
        name=name,
        in_specs=[ANY] * (2 * n),
        out_specs=[ANY] * n,
        out_shape=[jax.ShapeDtypeStruct(o.shape, o.dtype) for o in outs],
        input_output_aliases={n + i: i for i in range(n)},
        scratch_shapes=[pltpu.SemaphoreType.DMA((n,))] * 2 + [pltpu.SemaphoreType.DMA((3 * n,))] * 2,
        compiler_params=PAIR_CALL,
    )(*shards, *outs)


def _join_start(name, shards, outs):
    n = len(shards)

    def body(*refs):
        _pair_handshake()
        srcs, arrs = refs[:n], refs[n:2 * n]
        sems = refs[2 * n:2 * n + 4]
        token = refs[-1]
        for cp in _join_copies(srcs, arrs, arrs, *sems):
            cp.start()
        token[...] = jnp.zeros_like(token)

    res = pl.pallas_call(
        body,
        name=name,
        in_specs=[HBM] * (2 * n),
        out_specs=[SEM] * 4 + [HBM] * (2 * n) + [pl.BlockSpec(memory_space=pltpu.VMEM)],
        out_shape=[pltpu.SemaphoreType.DMA((n,))] * 2 + [pltpu.SemaphoreType.DMA((3 * n,))] * 2
        + [pltpu.HBM(s.shape, s.dtype) for s in shards] + [pltpu.HBM(o.shape, o.dtype) for o in outs]
        + [jax.ShapeDtypeStruct((8, LANES), F32)],
        input_output_aliases={i: 4 + i for i in range(2 * n)},
        compiler_params=pltpu.CompilerParams(has_side_effects=SPLIT_COPY, collective_id=PAIR_ID),
    )(*shards, *outs)
    return list(res[:4]), list(res[4:4 + n]), list(res[4 + n:4 + 2 * n]), res[-1]


def _join_wait(name, sems, shards, outs, after):
    n = len(shards)

    def body(*refs):
        srcs, arrs = refs[:n], refs[n:2 * n]
        for cp in _join_copies(srcs, arrs, arrs, *refs[2 * n:2 * n + 4]):
            cp.wait_send()
            cp.wait_recv()

    res = pl.pallas_call(
        body,
        name=name,
        in_specs=[HBM] * (2 * n) + [SEM] * 4 + [ANY] * len(after),
        out_specs=[HBM] * (2 * n),
        out_shape=[pltpu.HBM(s.shape, s.dtype) for s in shards] + [pltpu.HBM(o.shape, o.dtype) for o in outs],
        input_output_aliases={i: i for i in range(2 * n)},
        compiler_params=pltpu.CompilerParams(has_side_effects=SPLIT_COPY),
    )(*shards, *outs, *sems, *after)
    return list(res[n:])


def _pair_copy(grads, lands, send_sem, recv_sem, i):
    x, y, c = _position()
    half = grads[i].shape[1] // 2
    give = pl.ds(pl.multiple_of((1 - c) * half, 16), half)
    return pltpu.make_async_remote_copy(
        src_ref=grads[i].at[:, give, :], dst_ref=lands[i], send_sem=send_sem.at[i], recv_sem=recv_sem.at[i],
        device_id=(x, y, 1 - c), device_id_type=MESH)


def _pair_start(name, grads):
    n = len(grads)

    def body(*refs):
        _pair_handshake()
        srcs, lands = refs[:n], refs[n:2 * n]
        send_sem, recv_sem = refs[2 * n], refs[2 * n + 1]
        token = refs[-1]
        for i in range(n):
            _pair_copy(srcs, lands, send_sem, recv_sem, i).start()
        token[...] = jnp.zeros_like(token)

    halves = [(g.shape[0], g.shape[1] // 2, g.shape[2]) for g in grads]
    res = pl.pallas_call(
        body,
        name=name,
        in_specs=[HBM] * (2 * n),
        out_specs=[SEM, SEM] + [HBM] * (2 * n) + [pl.BlockSpec(memory_space=pltpu.VMEM)],
        out_shape=[pltpu.SemaphoreType.DMA((n,)), pltpu.SemaphoreType.DMA((n,))]
        + [pltpu.HBM(g.shape, g.dtype) for g in grads]
        + [pltpu.HBM(shp, g.dtype) for shp, g in zip(halves, grads)]
        + [jax.ShapeDtypeStruct((8, LANES), F32)],
        input_output_aliases={i: 2 + i for i in range(2 * n)},
        compiler_params=pltpu.CompilerParams(has_side_effects=SPLIT_COPY, collective_id=PAIR_ID),
    )(*[pltpu.with_memory_space_constraint(g, pltpu.HBM) for g in grads],
      *[pltpu.with_memory_space_constraint(lax.empty(shp, g.dtype), pltpu.HBM) for shp, g in zip(halves, grads)])
    return res[0], res[1], list(res[2:2 + n]), list(res[2 + n:2 + 2 * n]), res[-1]


def _pair_wait(name, send_sem, recv_sem, grads, lands, after):
    n = len(grads)

    def body(*refs):
        srcs, land_refs = refs[:n], refs[n:2 * n]
        send_ref, recv_ref = refs[2 * n], refs[2 * n + 1]
        for i in range(n):
            copy = _pair_copy(srcs, land_refs, send_ref, recv_ref, i)
            copy.wait_send()
            copy.wait_recv()

    res = pl.pallas_call(
        body,
        name=name,
        in_specs=[HBM] * (2 * n) + [SEM, SEM, ANY],
        out_specs=[HBM] * (2 * n),
        out_shape=[pltpu.HBM(g.shape, g.dtype) for g in grads] + [pltpu.HBM(l.shape, l.dtype) for l in lands],
        input_output_aliases={i: i for i in range(2 * n)},
        compiler_params=pltpu.CompilerParams(has_side_effects=SPLIT_COPY),
    )(*grads, *lands, send_sem, recv_sem, after)
    return list(res[:n]), list(res[n:])


def _scatter_copy(srcs, lands, send_sem, recv_sem, i, j):
    x, y, c = _position()
    chips = _other_chips(x, y)
    return pltpu.make_async_remote_copy(
        src_ref=srcs[i].at[2 * chips[j][0] + chips[j][1]], dst_ref=lands[i].at[j],
        send_sem=send_sem.at[3 * i + j], recv_sem=recv_sem.at[3 * i + j],
        device_id=(chips[j][0], chips[j][1], c), device_id_type=MESH)


def _scatter_start(name, sums):
    n = len(sums)

    def body(*refs):
        srcs, lands = refs[:n], refs[n:2 * n]
        send_sem, recv_sem = refs[2 * n], refs[2 * n + 1]
        token = refs[-1]
        for i in range(n):
            for j in range(3):
                _scatter_copy(srcs, lands, send_sem, recv_sem, i, j).start()
        token[...] = jnp.zeros_like(token)

    land_shapes = [(3,) + s.shape[1:] for s in sums]
    res = pl.pallas_call(
        body,
        name=name,
        in_specs=[HBM] * (2 * n),
        out_specs=[SEM, SEM] + [HBM] * (2 * n) + [pl.BlockSpec(memory_space=pltpu.VMEM)],
        out_shape=[pltpu.SemaphoreType.DMA((3 * n,)), pltpu.SemaphoreType.DMA((3 * n,))]
        + [pltpu.HBM(s.shape, s.dtype) for s in sums]
        + [pltpu.HBM(shp, s.dtype) for shp, s in zip(land_shapes, sums)]
        + [jax.ShapeDtypeStruct((8, LANES), F32)],
        input_output_aliases={i: 2 + i for i in range(2 * n)},
        compiler_params=pltpu.CompilerParams(has_side_effects=SPLIT_COPY),
    )(*[pltpu.with_memory_space_constraint(s, pltpu.HBM) for s in sums],
      *[pltpu.with_memory_space_constraint(lax.empty(shp, s.dtype), pltpu.HBM) for shp, s in zip(land_shapes, sums)])
    return res[0], res[1], list(res[2:2 + n]), list(res[2 + n:2 + 2 * n]), res[-1]


def _scatter_wait(name, send_sem, recv_sem, sums, lands, after):
    n = len(sums)

    def body(*refs):
        srcs, land_refs = refs[:n], refs[n:2 * n]
        send_ref, recv_ref = refs[2 * n], refs[2 * n + 1]
        for i in range(n):
            for j in range(3):
                copy = _scatter_copy(srcs, land_refs, send_ref, recv_ref, i, j)
                copy.wait_send()
                copy.wait_recv()

    res = pl.pallas_call(
        body,
        name=name,
        in_specs=[HBM] * (2 * n) + [SEM, SEM, ANY],
        out_specs=[HBM] * (2 * n),
        out_shape=[pltpu.HBM(s.shape, s.dtype) for s in sums] + [pltpu.HBM(l.shape, l.dtype) for l in lands],
        input_output_aliases={i: i for i in range(2 * n)},
        compiler_params=pltpu.CompilerParams(has_side_effects=SPLIT_COPY),
    )(*sums, *lands, send_sem, recv_sem, after)
    return list(res[:n]), list(res[n:])


def _pair_join(name, halves, small=None):
    n = len(halves)
    if small is None:
        def body_plain(*refs):
            _pair_handshake()
            ins, outs = refs[:n], refs[n:2 * n]
            send_sem, recv_sem = refs[2 * n:]
            x, y, c = _position()
            swaps = [pltpu.make_async_remote_copy(
                src_ref=ins[i], dst_ref=outs[i], send_sem=send_sem.at[i], recv_sem=recv_sem.at[i],
                device_id=(x, y, 1 - c), device_id_type=MESH) for i in range(n)]
            for swap in swaps:
                swap.start()
            for swap in swaps:
                swap.wait()

        return pl.pallas_call(
            body_plain,
            name=name,
            in_specs=[ANY] * n,
            out_specs=[ANY] * n,
            out_shape=[jax.ShapeDtypeStruct(h.shape, h.dtype) for h in halves],
            scratch_shapes=[pltpu.SemaphoreType.DMA((n,))] * 2,
            compiler_params=PAIR_CALL,
        )(*halves)

    def body(*refs):
        ins, small_ref = refs[:n], refs[n]
        outs, all_ref = refs[n + 1:2 * n + 1], refs[2 * n + 1]
        send_sem, recv_sem, sm_send, sm_recv, sm_local = refs[2 * n + 2:]
        x, y, c = _position()
        swaps = []
        for i in range(n):
            swap = pltpu.make_async_remote_copy(
                src_ref=ins[i], dst_ref=outs[i], send_sem=send_sem.at[i], recv_sem=recv_sem.at[i],
                device_id=(x, y, 1 - c), device_id_type=MESH)
            swap.start()
            swaps.append(swap)
        me = 4 * x + 2 * y + c
        sm_own = pltpu.make_async_copy(small_ref, all_ref.at[me], sm_local)
        sm_own.start()
        pushes, arrivals = [], []
        for mask in range(1, N_DEV):
            px, py, pc = x ^ (mask >> 2), y ^ ((mask >> 1) & 1), c ^ (mask & 1)
            pushes.append(pltpu.make_async_remote_copy(
                src_ref=small_ref, dst_ref=all_ref.at[me], send_sem=sm_send.at[mask - 1], recv_sem=sm_recv.at[mask - 1],
                device_id=(px, py, pc), device_id_type=MESH))
            arrivals.append(pltpu.make_async_remote_copy(
                src_ref=small_ref, dst_ref=all_ref.at[4 * px + 2 * py + pc], send_sem=sm_send.at[mask - 1],
                recv_sem=sm_recv.at[mask - 1], device_id=(px, py, pc), device_id_type=MESH))
        for cp in pushes:
            cp.start()
        for swap in swaps:
            swap.wait()
        for cp in arrivals:
            cp.wait_recv()
        for cp in pushes:
            cp.wait_send()
        sm_own.wait()

    res = pl.pallas_call(
        body,
        name=name,
        in_specs=[ANY] * (n + 1),
        out_specs=[ANY] * (n + 1),
        out_shape=[jax.ShapeDtypeStruct(h.shape, h.dtype) for h in halves]
        + [jax.ShapeDtypeStruct((N_DEV,) + small.shape, small.dtype)],
        scratch_shapes=[pltpu.SemaphoreType.DMA((n,))] * 2 + [pltpu.SemaphoreType.DMA((N_DEV - 1,))] * 2
        + [pltpu.SemaphoreType.DMA(())],
    )(*halves, small)
    return res[:n], res[n]


def _lower_bound(lbp):
    return jax.nn.softmax(lbp, axis=0)[0:1]


def _local_step(x, target, g1, gm, g2, gq, gk, go, rel_bias, lbp, weights, on_grads, grads_sent):
    b, s, d = x.shape
    t = b * s
    x0 = x.reshape(t, d)
    tgt = target.reshape(t, d)
    gq_t = jnp.tile(gq, (1, ATTN_HEADS))
    gk_t = jnp.tile(gk, (1, ATTN_HEADS))
    lb = _lower_bound(lbp)
    table = _band_table(_rel_bias_table("rel_bias_table", rel_bias))

    h1 = _rmsnorm_fwd("norm1", x0, g1)
    wg1, wu1, deps1 = weights["first"]((h1, table))
    a1, b1, z1 = _ffn_up("ffn1_up", h1, wg1, wu1, deps1)
    wd1, deps_mid = weights["mid"]((z1,))
    x1, h2 = _ffn_down("ffn1_down", z1, wd1, x0, gm, deps_mid)
    w_in, w_out = weights["mid_rest"]((x1,))
    ns = w_in.shape[0]
    proj = _in_proj("in_proj", h2, w_in)
    proj3 = proj.reshape(b, s, proj.shape[1])
    qn, kn, vb = _qk_prep("qk_prep", proj3, gq_t, gk_t)
    attn = _attn_fwd("attn_fwd", qn, kn, vb, table, weights["last_begin"]((qn,))).reshape(t, ATTN_W)
    mix, oraw, states = _hgrn_fwd("hgrn_fwd", proj, attn, lb, go, b, s)
    x2, h3 = _out_proj("out_proj", mix, w_out, x1, g2)
    wg2, wu2, wd2 = weights["last"]((h3,))
    a2, b2, z2 = _ffn_up("ffn2_up", h3, wg2, wu2)
    dy, dyh, sq = _ffn_down_loss("ffn2_down_loss", z2, wd2, x2, tgt)
    loss = 0.5 * jnp.sum(sq) / d

    da2, db2 = _ffn_bwd_act("ffn2_bwd_act", dyh, wd2, a2, b2)
    dwd2 = _grad_w_shardrows("ffn2_dwd", z2, dyh)
    dwg2 = _grad_w_shardrows("ffn2_dwg", da2, h3)
    dwu2 = _grad_w_shardrows("ffn2_dwu", db2, h3)
    sent2 = on_grads("ffn2", {"ffn2_w_gate": dwg2, "ffn2_w_up": dwu2, "ffn2_w_down": dwd2})
    dx2, dx2b, dg2 = _ffn_bwd_in("ffn2_bwd_in", da2, db2, wg2, wu2, x2, g2, dy, 1.0, sent2)
    sent2 = grads_sent("ffn2", dx2b)

    dwout = _grad_w_out("dw_out", mix, dx2b)
    dmix = _out_proj_bwd("out_proj_bwd", dx2b, w_out, sent2)
    dqn, dkn, dvn, dbe, dbo = _attn_bwd("attn_bwd", qn, kn, vb, table, dmix.reshape(b, s, dmix.shape[1]))
    dbias = dbe[:, :, :BAND] + dbo[:, :, CHUNK:]
    dpq, dpk, dpv, dgq, dgk = _qk_prep_bwd("qk_prep_bwd", proj3, dqn, dkn, dvn, gq_t, gk_t)
    dpq, dpk, dpv = (a.reshape(t, ATTN_W) for a in (dpq, dpk, dpv))
    dproj, dlb, dgo = _hgrn_bwd("hgrn_bwd", proj, (dpq, dpk, dpv), lb, go, oraw, states, dmix, b, s)
    dwin = _grad_w_in("dw_in", h2, dproj, ns)
    dx1, dx1h, dgm = _in_proj_bwd("in_proj_bwd", dproj, w_in, x1, gm, dx2, 0.5)

    dwd1 = _grad_w_shardrows("ffn1_dwd", z1, dx1h)
    sent_mix = on_grads("mix", {"w_in": dwin, "w_out": dwout.reshape(ns, dwout.shape[0] // ns, d),
                                "ffn1_w_down": dwd1})
    da1, db1 = _ffn_bwd_act("ffn1_bwd_act", dx1h, wd1, a1, b1, sent_mix)
    sent_mix = grads_sent("mix", da1)
    dwg1 = _grad_w_shardrows("ffn1_dwg", da1, h1, sent_mix)
    dwu1 = _grad_w_shardrows("ffn1_dwu", db1, h1)
    on_grads("ffn1", {"ffn1_w_gate": dwg1, "ffn1_w_up": dwu1})
    sent1 = grads_sent("ffn1", None)
    dx0, dg1 = _ffn_bwd_in("ffn1_bwd_in", da1, db1, wg1, wu1, x0, g1, dx1, None, sent1)

    nt = dg1.shape[0]
    sg = _small_grads(
        "small_grads", dg1.reshape(nt, d), dgm.reshape(nt, d), dg2.reshape(nt, d),
        dgq.reshape(-1, ATTN_W), dgk.reshape(-1, ATTN_W), dbias.transpose(1, 0, 2),
        dlb.reshape(b, HGRN_W), dgo.reshape(b, HGRN_W), lbp)
    g1g, gmg, g2g, gqg, gkg, rbg, lbg, gog = sg
    small = _pack_small(g1g, gmg, g2g, lbg, rbg[:, :N_REL], gqg, gkg, gog, loss)
    return dx0.reshape(b, s, d), small


LOSS_SLOT = 7 * SMALL_COLS + 2 * ATTN_DH + HGRN_DH


def _pack_small(g1, gm, g2, lbp, rel_bias, gq, gk, go, loss=None):
    flat = [g1.reshape(-1), gm.reshape(-1), g2.reshape(-1), lbp.reshape(-1), rel_bias.reshape(-1)]
    n_bias = 3 * SMALL_COLS - rel_bias.size
    heads = [gq.reshape(-1), gk.reshape(-1), go.reshape(-1)]
    heads.append(jnp.zeros((1,), F32) if loss is None else loss.reshape(1))
    n_tail = SMALL_COLS - sum(h.size for h in heads)
    return jnp.concatenate(flat + [jnp.zeros((n_bias,), F32)] + heads + [jnp.zeros((n_tail,), F32)]).reshape(
        SMALL_ROWS, SMALL_COLS)


def _unpack_small(p, d):
    flat = p.reshape(-1)
    o = 3 * d
    g1, gm, g2 = p[0:1], p[1:2], p[2:3]
    lbp = flat[o:o + 2 * HGRN_W].reshape(2, HGRN_W)
    o = 4 * SMALL_COLS
    rel = flat[o:o + ATTN_HEADS * N_REL].reshape(1, ATTN_HEADS, N_REL)
    o = 7 * SMALL_COLS
    gq = flat[o:o + ATTN_DH].reshape(1, ATTN_DH)
    gk = flat[o + ATTN_DH:o + 2 * ATTN_DH].reshape(1, ATTN_DH)
    go = flat[o + 2 * ATTN_DH:o + 2 * ATTN_DH + HGRN_DH].reshape(1, HGRN_DH)
    return g1, gm, g2, gq, gk, rel, lbp, go


def kernel(x, ffn1_norm_g, ffn1_w_gate, ffn1_w_up, ffn1_w_down, mix_norm_g, w_in, attn_q_norm_g, attn_k_norm_g, attn_rel_bias, hgrn_lower_bounds, hgrn_out_norm_g, w_out, ffn2_norm_g, ffn2_w_gate, ffn2_w_up, ffn2_w_down, loss_target, m_ffn1_norm_g, m_ffn1_w_gate, m_ffn1_w_up, m_ffn1_w_down, m_mix_norm_g, m_w_in, m_attn_q_norm_g, m_attn_k_norm_g, m_attn_rel_bias, m_hgrn_lower_bounds, m_hgrn_out_norm_g, m_w_out, m_ffn2_norm_g, m_ffn2_w_gate, m_ffn2_w_up, m_ffn2_w_down, v_ffn1_norm_g, v_ffn1_w_gate, v_ffn1_w_up, v_ffn1_w_down, v_mix_norm_g, v_w_in, v_attn_q_norm_g, v_attn_k_norm_g, v_attn_rel_bias, v_hgrn_lower_bounds, v_hgrn_out_norm_g, v_w_out, v_ffn2_norm_g, v_ffn2_w_gate, v_ffn2_w_up, v_ffn2_w_down):
    d = x.shape[-1]
    big_w = [ffn1_w_gate, ffn1_w_up, ffn1_w_down, w_in, w_out, ffn2_w_gate, ffn2_w_up, ffn2_w_down]
    big_m = [m_ffn1_w_gate, m_ffn1_w_up, m_ffn1_w_down, m_w_in, m_w_out, m_ffn2_w_gate, m_ffn2_w_up, m_ffn2_w_down]
    big_v = [v_ffn1_w_gate, v_ffn1_w_up, v_ffn1_w_down, v_w_in, v_w_out, v_ffn2_w_gate, v_ffn2_w_up, v_ffn2_w_down]
    big_names = ["ffn1_w_gate", "ffn1_w_up", "ffn1_w_down", "w_in", "w_out", "ffn2_w_gate", "ffn2_w_up", "ffn2_w_down"]
    flipped = {nm for nm in big_names if nm.endswith("gate") or nm.endswith("up")}
    flip = lambda nm, a: jnp.swapaxes(a, 1, 2) if nm in flipped else a
    big_w, big_m, big_v = ([flip(nm, a) for nm, a in zip(big_names, arrs)] for arrs in (big_w, big_m, big_v))

    shards = [w[0].astype(BF16) for w in big_w]
    start_a = _gather_start("gather_start_up1", shards[:2], ())
    start_b = _gather_start("gather_start_mid", shards[2:5], (start_a[4],))
    start_c = _gather_start("gather_start_ffn2", shards[5:], (start_b[4],))

    pending = {}

    def arrived(tag, started, after):
        send_sem, recv_sem, srcs, outs, _ = started
        return _gather_wait("gather_wait_" + tag, send_sem, recv_sem, srcs, outs, after)

    def first_weights(after):
        return (*_gather_join("gather_join_up1", *arrived("up1", start_a, after)), (start_c[4],))

    def mid_weights(after):
        srcs, outs = arrived("mid", start_b, after)
        (wd1,) = _gather_join("gather_join_wd1", srcs[:1], outs[:1])
        pending["mid"] = _join_start("join_start_mid", srcs[1:], outs[1:])
        return wd1, (pending["mid"][3],)

    def mid_rest(after):
        sems, srcs, outs, _ = pending["mid"]
        win_f, wout_f = _join_wait("join_wait_mid", sems, srcs, outs, after)
        return win_f, wout_f.reshape(wout_f.shape[0] * wout_f.shape[1], d)

    def last_begin(after):
        pending["ffn2"] = _join_start("join_start_ffn2", *arrived("ffn2", start_c, after))
        return (pending["ffn2"][3],)

    def last_weights(after):
        sems, srcs, outs, _ = pending["ffn2"]
        return _join_wait("join_wait_ffn2", sems, srcs, outs, after)

    weights = {"first": first_weights, "mid": mid_weights, "mid_rest": mid_rest, "last_begin": last_begin,
               "last": last_weights}

    core = lax.axis_index("c").astype(jnp.int32).reshape(1)
    chip = (2 * lax.axis_index("x") + lax.axis_index("y")).astype(jnp.int32).reshape(1)
    started = {}

    def on_grads(tag, grads):
        names = list(grads)
        started[tag] = (names, _pair_start("pair_start_" + tag, [grads[nm] for nm in names]))
        return (started[tag][1][4],)

    def grads_sent(tag, after):
        names, (send_sem, recv_sem, grads, lands, token) = started[tag]
        grads, theirs = _pair_wait("pair_wait_" + tag, send_sem, recv_sem, grads, lands, token if after is None else after)
        sums = [_pair_sum("pair_sum_" + nm, g, th, core) for nm, g, th in zip(names, grads, theirs)]
        started[tag] = (names, _scatter_start("scatter_start_" + tag, sums))
        return (started[tag][1][4],)

    grad_x, small_g = _local_step(
        x, loss_target, ffn1_norm_g, mix_norm_g, ffn2_norm_g, attn_q_norm_g, attn_k_norm_g, hgrn_out_norm_g,
        attn_rel_bias[0], hgrn_lower_bounds, weights, on_grads, grads_sent)

    def finish(tag, after):
        names, (send_sem, recv_sem, sums, lands, _) = started[tag]
        sums, lands = _scatter_wait("scatter_wait_" + tag, send_sem, recv_sem, sums, lands, after)
        return names, [_chip_sum("chip_sum_" + nm, sm, ld, chip) for nm, sm, ld in zip(names, sums, lands)]

    by_name = {nm: (w, m, v) for nm, w, m, v in zip(big_names, big_w, big_m, big_v)}
    updated = {}

    def update(names, halves, other_halves):
        for nm, mine, theirs in zip(names, halves, other_halves):
            w, m, v = by_name[nm]
            updated[nm] = _adamw("adamw_" + nm, w, mine, theirs, m, v, core)

    last_token = started["ffn1"][1][4]
    names_a, halves_a = finish("ffn2", last_token)
    names_m, halves_m = finish("mix", last_token)
    names_a, halves_a = names_a + names_m, halves_a + halves_m
    update(names_a, halves_a, _pair_join("pair_join_early", halves_a))
    names_b, halves_b = finish("ffn1", updated[names_a[-1]][1])
    others_b, small_all = _pair_join("pair_join_last", halves_b, small_g)
    update(names_b, halves_b, others_b)
    big_out = [updated[nm] for nm in big_names]

    pack = lambda g1, gm, g2, gq, gk, rel, lbp, go: _pack_small(g1, gm, g2, lbp, rel[0], gq, gk, go)
    small_w = pack(ffn1_norm_g, mix_norm_g, ffn2_norm_g, attn_q_norm_g, attn_k_norm_g, attn_rel_bias, hgrn_lower_bounds, hgrn_out_norm_g)
    small_m = pack(m_ffn1_norm_g, m_mix_norm_g, m_ffn2_norm_g, m_attn_q_norm_g, m_attn_k_norm_g, m_attn_rel_bias, m_hgrn_lower_bounds, m_hgrn_out_norm_g)
    small_v = pack(v_ffn1_norm_g, v_mix_norm_g, v_ffn2_norm_g, v_attn_q_norm_g, v_attn_k_norm_g, v_attn_rel_bias, v_hgrn_lower_bounds, v_hgrn_out_norm_g)
    small_res = _adamw_small("adamw_small", small_w, small_all, small_m, small_v)
    small_out = [_unpack_small(p, d) for p in small_res]
    loss = small_res[0].reshape(-1)[LOSS_SLOT]

    def assemble(kind):
        bg = [flip(nm, o[kind]) for nm, o in zip(big_names, big_out)]
        g1, gm, g2, gq, gk, rel, lbp, go = small_out[kind]
        return [g1, bg[0], bg[1], bg[2], gm, bg[3], gq, gk, rel, lbp, go, bg[4], g2, bg[5], bg[6], bg[7]]

    return (loss, grad_x, *assemble(0), *assemble(1), *assemble(2), *assemble(3))
```

```python
import functools

import jax
import jax.numpy as jnp
from jax import lax
from jax.experimental import pallas as pl
from jax.experimental.pallas import tpu as pltpu

F32 = jnp.float32
BF16 = jnp.bfloat16
MESH = pl.DeviceIdType.MESH

N_CHIPS = 4
N_DEV = 8
CHUNK = 64
ATTN_HEADS = 8
ATTN_DH = 64
ATTN_W = ATTN_HEADS * ATTN_DH
HGRN_HEADS = 4
HGRN_DH = 128
HGRN_W = HGRN_HEADS * HGRN_DH
LEFT_CHUNKS = 8
BAND = (LEFT_CHUNKS + 1) * CHUNK
KPAD = LEFT_CHUNKS * CHUNK
REL_CLIP = 128
N_REL = 2 * REL_CLIP + 1
N_REL_PAD = 384
RMS_EPS = 1e-6
LANES = 128
SMALL_ROWS = 8
SMALL_COLS = 1024

ADAM_LR = 0.001
ADAM_B1 = 0.9
ADAM_B2 = 0.999
ADAM_EPS = 1e-08
ADAM_WD = 0.01
ADAM_STEP = 10

NN = (((1,), (0,)), ((), ()))
NT = (((1,), (1,)), ((), ()))
TN = (((0,), (0,)), ((), ()))

VMEM_LIMIT = 48 * 1024 * 1024


def _sigmoid(x):
    return 1.0 / (1.0 + jnp.exp(-x))


def _silu(x):
    return x * _sigmoid(x)


def _dot(a, b, dims=NN):
    return lax.dot_general(a, b, dims, preferred_element_type=F32)


def _split3(x):
    hi = x.astype(BF16)
    r1 = x - hi.astype(F32)
    mid = r1.astype(BF16)
    lo = (r1 - mid.astype(F32)).astype(BF16)
    return hi, mid, lo


def _dot_exact_rhs(x, mat, dims=NN, pieces=3):
    hi, mid, lo = _split3(x)
    out = _dot(hi, mat, dims) + _dot(mid, mat, dims)
    return out + _dot(lo, mat, dims) if pieces == 3 else out


def _dot_exact_lhs(mat, x, dims=NN):
    hi, mid, lo = _split3(x)
    return _dot(mat, hi, dims) + _dot(mat, mid, dims) + _dot(mat, lo, dims)


def _params(*sem):
    return pltpu.CompilerParams(dimension_semantics=sem, vmem_limit_bytes=VMEM_LIMIT)


def _mm(name, ins, terms, n_acc, grid, acc_shape, outs, epilogue, extras=(), deps=()):
    nk = grid[2]
    ni, ne, nd, no = len(ins), len(extras), len(deps), len(outs)

    def body(*refs):
        in_refs = refs[:ni]
        ex_refs = refs[ni:ni + ne]
        out_refs = refs[ni + ne + nd:ni + ne + nd + no]
        acc_refs = refs[ni + ne + nd + no:]
        parts = [None] * n_acc
        for ai, li, ri, dims in terms:
            d = _dot(in_refs[li][...], in_refs[ri][...], dims)
            parts[ai] = d if parts[ai] is None else parts[ai] + d

        def finish(accs):
            res = epilogue(accs, [e[...] for e in ex_refs])
            for o, r in zip(out_refs, res):
                o[...] = r.astype(o.dtype)

        if nk == 1:
            finish(parts)
        else:
            k = pl.program_id(2)

            @pl.when(k == 0)
            def _():
                for a, p in zip(acc_refs, parts):
                    a[...] = p

            @pl.when(k > 0)
            def _():
                for a, p in zip(acc_refs, parts):
                    a[...] += p

            @pl.when(k == nk - 1)
            def _():
                finish([a[...] for a in acc_refs])

    scratch = [] if nk == 1 else [pltpu.VMEM(acc_shape, F32) for _ in range(n_acc)]
    res = pl.pallas_call(
        body,
        name=name,
        grid=grid,
        in_specs=[s for _, s in ins] + [s for _, s in extras] + [pl.BlockSpec(memory_space=pl.ANY)] * nd,
        out_specs=[s for _, s in outs],
        out_shape=[o for o, _ in outs],
        scratch_shapes=scratch,
        compiler_params=_params("parallel", "parallel", "arbitrary"),
    )(*[a for a, _ in ins], *[a for a, _ in extras], *deps)
    return res


def _mm_rows(name, lhs, weights, dims, t, outs, epilogue, extras=(), deps=()):
    tm = _row_tile(t)
    nl, ne, nd, no = len(lhs), len(extras), len(deps), len(outs)
    ns = weights[0].shape[0]

    def body(*refs):
        lhs_refs = refs[:nl]
        w_hbm = refs[nl:2 * nl]
        ex_refs = refs[2 * nl:2 * nl + ne]
        out_refs = refs[2 * nl + ne + nd:2 * nl + ne + nd + no]
        w_vmem = refs[2 * nl + ne + nd + no:3 * nl + ne + nd + no]
        sem = refs[-1]

        @pl.when(pl.program_id(0) == 0)
        def _():
            copies = [pltpu.make_async_copy(w_hbm[p], w_vmem[p], sem.at[p]) for p in range(nl)]
            for cp in copies:
                cp.start()
            for cp in copies:
                cp.wait()

        acc = None
        for p in range(nl):
            pick = lhs[p][2]
            for j in range(ns):
                part = _dot(pick(lhs_refs[p], j), w_vmem[p][j], dims)
                acc = part if acc is None else acc + part
        res = epilogue([acc], [e[...] for e in ex_refs])
        for o, r in zip(out_refs, res):
            o[...] = r.astype(o.dtype)

    return pl.pallas_call(
        body,
        name=name,
        grid=(t // tm,),
        in_specs=[s for _, s, _ in lhs] + [pl.BlockSpec(memory_space=pl.ANY)] * nl + [s for _, s in extras]
        + [pl.BlockSpec(memory_space=pl.ANY)] * nd,
        out_specs=[s for _, s in outs],
        out_shape=[o for o, _ in outs],
        scratch_shapes=[pltpu.VMEM(w.shape, w.dtype) for w in weights] + [pltpu.SemaphoreType.DMA((nl,))],
        compiler_params=_params("arbitrary"),
    )(*[a for a, _, _ in lhs], *weights, *[a for a, _ in extras], *deps)


def _mm_shards(name, x, weights, dims, outs, epilogue, extras=(), deps=()):
    t = x.shape[0]
    tm = _row_tile(t)
    nw, ne, nd, no = len(weights), len(extras), len(deps), len(outs)
    ns = weights[0].shape[0]

    def body(*refs):
        x_ref = refs[0]
        w_hbm = refs[1:1 + nw]
        ex_refs = refs[1 + nw:1 + nw + ne]
        out_refs = refs[1 + nw + ne + nd:1 + nw + ne + nd + no]
        w_vmem = refs[1 + nw + ne + nd + no:1 + 2 * nw + ne + nd + no]
        sem = refs[-1]

        @pl.when(pl.program_id(0) == 0)
        def _():
            copies = [pltpu.make_async_copy(w_hbm[p], w_vmem[p], sem.at[p]) for p in range(nw)]
            for cp in copies:
                cp.start()
            for cp in copies:
                cp.wait()

        xv = x_ref[...]
        accs = [_dot(xv, w_vmem[p][0], dims) for p in range(nw)]
        for j in range(ns):
            nxt = [_dot(xv, w_vmem[p][j + 1], dims) for p in range(nw)] if j + 1 < ns else None
            res = epilogue(accs, [e[j] for e in ex_refs])
            for (_, _, store), o, r in zip(outs, out_refs, res):
                store(o, j, r.astype(o.dtype))
            accs = nxt

    return pl.pallas_call(
        body,
        name=name,
        grid=(t // tm,),
        in_specs=[pl.BlockSpec((tm, x.shape[1]), lambda i: (i, 0))] + [pl.BlockSpec(memory_space=pl.ANY)] * nw
        + [s for _, s in extras] + [pl.BlockSpec(memory_space=pl.ANY)] * nd,
        out_specs=[s for _, s, _ in outs],
        out_shape=[o for o, _, _ in outs],
        scratch_shapes=[pltpu.VMEM(w.shape, w.dtype) for w in weights] + [pltpu.SemaphoreType.DMA((nw,))],
        compiler_params=_params("arbitrary"),
    )(x, *weights, *[a for a, _ in extras], *deps)


def _store_shard(ref, j, value):
    ref[j] = value


def _row_tile(t):
    return 512 if t % 512 == 0 else t


def _k_tile(t):
    return t if t <= 4096 else 1024


def _rmsnorm(xv, g):
    ms = jnp.mean(xv * xv, axis=-1, keepdims=True)
    return xv * lax.rsqrt(ms + RMS_EPS) * g


def _rmsnorm_fwd(name, x, g):
    t, d = x.shape
    tm = _row_tile(t)

    def body(x_ref, g_ref, h_ref):
        h_ref[...] = _rmsnorm(x_ref[...], g_ref[...]).astype(BF16)

    return pl.pallas_call(
        body,
        name=name,
        grid=(t // tm,),
        in_specs=[pl.BlockSpec((tm, d), lambda i: (i, 0)), pl.BlockSpec((1, d), lambda i: (0, 0))],
        out_specs=pl.BlockSpec((tm, d), lambda i: (i, 0)),
        out_shape=jax.ShapeDtypeStruct((t, d), BF16),
        compiler_params=_params("parallel"),
    )(x, g)


def _norm_bwd_epilogue(copy_scale):
    def epilogue(accs, ex):
        dh = accs[0]
        xv, g, dres = ex
        ms = jnp.mean(xv * xv, axis=-1, keepdims=True)
        rstd = lax.rsqrt(ms + RMS_EPS)
        xhat = xv * rstd
        dxhat = dh * g
        dx = rstd * (dxhat - xhat * jnp.mean(dxhat * xhat, axis=-1, keepdims=True))
        out = dres + dx
        dg = jnp.sum(dh * xhat, axis=0, keepdims=True)
        if copy_scale is None:
            return out, dg
        return out, out * copy_scale, dg

    return epilogue


def _ffn_up(name, h, wg, wu, deps=()):
    t, d = h.shape
    ns, f, _ = wg.shape
    tm = _row_tile(t)

    def epilogue(accs, ex):
        a, b = accs
        sg = _sigmoid(a)
        act = a * sg
        return act, b * (sg * (1.0 + a * (1.0 - sg))), act * b

    out = (jax.ShapeDtypeStruct((ns, t, f), BF16), pl.BlockSpec((ns, tm, f), lambda i: (0, i, 0)), _store_shard)
    return _mm_shards(name, h, [wg, wu], NT, [out] * 3, epilogue, deps=deps)


def _shard_rows(arr, tm):
    ns, _, f = arr.shape
    return arr, pl.BlockSpec((ns, tm, f), lambda i: (0, i, 0)), lambda ref, j: ref[j]


def _ffn_down(name, z, wd, x, g_next, deps=()):
    _, t, _ = z.shape
    d = wd.shape[2]
    tm = _row_tile(t)
    row = pl.BlockSpec((tm, d), lambda i: (i, 0))

    def epilogue(accs, ex):
        y = ex[0] + 0.5 * accs[0]
        return y, _rmsnorm(y, ex[1])

    return _mm_rows(
        name, [_shard_rows(z, tm)], [wd], NN, t,
        outs=[(jax.ShapeDtypeStruct((t, d), F32), row), (jax.ShapeDtypeStruct((t, d), BF16), row)],
        epilogue=epilogue,
        extras=[(x, row), (g_next, pl.BlockSpec((1, d), lambda i: (0, 0)))],
        deps=deps,
    )


def _ffn_down_loss(name, z, wd, x, target):
    _, t, _ = z.shape
    d = wd.shape[2]
    tm = _row_tile(t)
    nt = t // tm
    row = pl.BlockSpec((tm, d), lambda i: (i, 0))

    def epilogue(accs, ex):
        e = ex[0] + 0.5 * accs[0] - ex[1]
        dy = e * (1.0 / d)
        return dy, 0.5 * dy, jnp.sum(e * e, axis=0, keepdims=True)

    return _mm_rows(
        name, [_shard_rows(z, tm)], [wd], NN, t,
        outs=[(jax.ShapeDtypeStruct((t, d), F32), row), (jax.ShapeDtypeStruct((t, d), BF16), row),
              (jax.ShapeDtypeStruct((nt, 1, d), F32), pl.BlockSpec((None, 1, d), lambda i: (i, 0, 0)))],
        epilogue=epilogue,
        extras=[(x, row), (target, row)],
    )


def _ffn_bwd_act(name, dout, wd, act_a, dact_b, deps=()):
    t, d = dout.shape
    ns, f, _ = wd.shape
    tm = _row_tile(t)

    def epilogue(accs, ex):
        dz = accs[0]
        return dz * ex[1].astype(F32), dz * ex[0].astype(F32)

    act = pl.BlockSpec((ns, tm, f), lambda i: (0, i, 0))
    out = (jax.ShapeDtypeStruct((ns, t, f), BF16), act, _store_shard)
    return _mm_shards(name, dout, [wd], NT, [out] * 2, epilogue, extras=[(act_a, act), (dact_b, act)], deps=deps)


def _grad_w_shardrows(name, z, dout, deps=()):
    ns, t, f = z.shape
    d = dout.shape[1]
    tk = _k_tile(t)
    return _mm(
        name,
        ins=[(z, pl.BlockSpec((None, tk, f), lambda j, n, k: (j, k, 0))),
             (dout, pl.BlockSpec((tk, d), lambda j, n, k: (k, 0)))],
        terms=[(0, 0, 1, TN)],
        n_acc=1,
        grid=(ns, 1, t // tk),
        acc_shape=(f, d),
        outs=[(pltpu.HBM((ns, f, d), BF16), pl.BlockSpec((None, f, d), lambda j, n, k: (j, 0, 0)))],
        epilogue=lambda accs, ex: (accs[0],),
        deps=deps,
    )[0]


def _norm_bwd_outs(t, d, tm, copy_scale):
    row = pl.BlockSpec((tm, d), lambda i: (i, 0))
    outs = [(jax.ShapeDtypeStruct((t, d), F32), row)]
    if copy_scale is not None:
        outs.append((jax.ShapeDtypeStruct((t, d), BF16), row))
    outs.append((jax.ShapeDtypeStruct((t // tm, 1, d), F32), pl.BlockSpec((None, 1, d), lambda i: (i, 0, 0))))
    return row, outs


def _ffn_bwd_in(name, da, db, wg, wu, x, g, dres, copy_scale, deps=()):
    _, t, _ = da.shape
    d = wg.shape[2]
    tm = _row_tile(t)
    row, outs = _norm_bwd_outs(t, d, tm, copy_scale)
    return _mm_rows(
        name, [_shard_rows(da, tm), _shard_rows(db, tm)], [wg, wu], NN, t,
        outs=outs,
        epilogue=_norm_bwd_epilogue(copy_scale),
        extras=[(x, row), (g, pl.BlockSpec((1, d), lambda i: (0, 0))), (dres, row)],
        deps=deps,
    )


def _in_proj(name, h, w_in):
    t, d = h.shape
    ns, _, pj = w_in.shape
    tm = _row_tile(t)
    def store(ref, j, value):
        ref[:, j * pj:(j + 1) * pj] = value

    out = (jax.ShapeDtypeStruct((t, ns * pj), F32), pl.BlockSpec((tm, ns * pj), lambda i: (i, 0)), store)
    return _mm_shards(name, h, [w_in], NN, [out], lambda accs, ex: (accs[0],))[0]


def _in_proj_bwd(name, dp, w_in, x, g, dres, copy_scale, deps=()):
    t = dp.shape[0]
    ns, d, pj = w_in.shape
    tm = _row_tile(t)
    row, outs = _norm_bwd_outs(t, d, tm, copy_scale)
    cols = (dp, pl.BlockSpec((tm, ns * pj), lambda i: (i, 0)), lambda ref, j: ref[:, j * pj:(j + 1) * pj])
    return _mm_rows(
        name, [cols], [w_in], NT, t,
        outs=outs,
        epilogue=_norm_bwd_epilogue(copy_scale),
        extras=[(x, row), (g, pl.BlockSpec((1, d), lambda i: (0, 0))), (dres, row)],
        deps=deps,
    )


def _grad_w_in(name, h, dp, ns):
    t, d = h.shape
    pj = dp.shape[1] // ns
    tk = _k_tile(t)
    return _mm(
        name,
        ins=[(h, pl.BlockSpec((tk, d), lambda j, n, k: (k, 0))),
             (dp, pl.BlockSpec((tk, pj), lambda j, n, k: (k, j)))],
        terms=[(0, 0, 1, TN)],
        n_acc=1,
        grid=(ns, 1, t // tk),
        acc_shape=(d, pj),
        outs=[(pltpu.HBM((ns, d, pj), BF16), pl.BlockSpec((None, d, pj), lambda j, n, k: (j, 0, 0)))],
        epilogue=lambda accs, ex: (accs[0],),
    )[0]


def _out_proj(name, mix, w_out, x, g_next):
    t, dm = mix.shape
    d = w_out.shape[1]
    tm = _row_tile(t)
    row = pl.BlockSpec((tm, d), lambda i, n, k: (i, 0))
    return _mm(
        name,
        ins=[(mix, pl.BlockSpec((tm, dm), lambda i, n, k: (i, 0))),
             (w_out, pl.BlockSpec((dm, d), lambda i, n, k: (0, 0)))],
        terms=[(0, 0, 1, NN)],
        n_acc=1,
        grid=(t // tm, 1, 1),
        acc_shape=(tm, d),
        outs=[(jax.ShapeDtypeStruct((t, d), F32), row), (jax.ShapeDtypeStruct((t, d), BF16), row)],
        epilogue=lambda accs, ex: (ex[0] + accs[0], _rmsnorm(ex[0] + accs[0], ex[1])),
        extras=[(x, row), (g_next, pl.BlockSpec((1, d), lambda i, n, k: (0, 0)))],
    )


def _out_proj_bwd(name, dx, w_out, deps=()):
    t, d = dx.shape
    dm = w_out.shape[0]
    tm = _row_tile(t)
    return _mm(
        name,
        ins=[(dx, pl.BlockSpec((tm, d), lambda i, n, k: (i, 0))),
             (w_out, pl.BlockSpec((dm, d), lambda i, n, k: (0, 0)))],
        terms=[(0, 0, 1, NT)],
        n_acc=1,
        grid=(t // tm, 1, 1),
        acc_shape=(tm, dm),
        outs=[(jax.ShapeDtypeStruct((t, dm), F32), pl.BlockSpec((tm, dm), lambda i, n, k: (i, 0)))],
        epilogue=lambda accs, ex: (accs[0],),
        deps=deps,
    )[0]


def _grad_w_out(name, mix, dx):
    t, dm = mix.shape
    d = dx.shape[1]
    tk = _k_tile(t)
    return _mm(
        name,
        ins=[(mix, pl.BlockSpec((tk, dm), lambda a, n, k: (k, 0))),
             (dx, pl.BlockSpec((tk, d), lambda a, n, k: (k, 0)))],
        terms=[(0, 0, 1, TN)],
        n_acc=1,
        grid=(1, 1, t // tk),
        acc_shape=(dm, d),
        outs=[(pltpu.HBM((dm, d), BF16), pl.BlockSpec((dm, d), lambda a, n, k: (0, 0)))],
        epilogue=lambda accs, ex: (accs[0],),
    )[0]


def _head_group_matrix():
    r = lax.broadcasted_iota(jnp.int32, (ATTN_W, ATTN_W), 0)
    c = lax.broadcasted_iota(jnp.int32, (ATTN_W, ATTN_W), 1)
    same = jnp.right_shift(r, 6) == jnp.right_shift(c, 6)
    return jnp.where(same, 1.0, 0.0).astype(BF16)


def _qk_prep(name, proj, gq, gk):
    b, s, _ = proj.shape
    tm = KPAD
    nb = s // tm

    def body(q_ref, k_ref, v_ref, gq_ref, gk_ref, qn_ref, kn_ref, vb_ref):
        j = pl.program_id(1)
        bd = _head_group_matrix()

        def norm(xv, g):
            ms = _dot_exact_rhs(xv * xv, bd, pieces=2) * (1.0 / ATTN_DH)
            return xv * lax.rsqrt(ms + RMS_EPS) * g

        @pl.when(j == 0)
        def _():
            kn_ref[...] = jnp.zeros_like(kn_ref)
            vb_ref[...] = jnp.zeros_like(vb_ref)

        @pl.when(j > 0)
        def _():
            qn_ref[...] = norm(q_ref[...], gq_ref[...]).astype(BF16)
            kn_ref[...] = norm(k_ref[...], gk_ref[...]).astype(BF16)
            vb_ref[...] = v_ref[...].astype(BF16)

    src_blk = lambda col: pl.BlockSpec((None, tm, ATTN_W), lambda bi, j: (bi, jnp.maximum(j - 1, 0), col))
    gspec = pl.BlockSpec((1, ATTN_W), lambda bi, j: (0, 0))
    padded = pl.BlockSpec((None, tm, ATTN_W), lambda bi, j: (bi, j, 0))
    return pl.pallas_call(
        body,
        name=name,
        grid=(b, nb + 1),
        in_specs=[src_blk(0), src_blk(1), src_blk(2), gspec, gspec],
        out_specs=[src_blk(0), padded, padded],
        out_shape=[jax.ShapeDtypeStruct((b, s, ATTN_W), BF16), jax.ShapeDtypeStruct((b, KPAD + s, ATTN_W), BF16),
                   jax.ShapeDtypeStruct((b, KPAD + s, ATTN_W), BF16)],
        compiler_params=_params("parallel", "arbitrary"),
    )(proj, proj, proj, gq, gk)


def _qk_prep_bwd(name, proj, dqn, dkn, dv, gq, gk):
    b, s, _ = proj.shape
    tm = KPAD
    nb = s // tm

    def body(q_ref, k_ref, dqn_ref, dkn_ref, dv_ref, gq_ref, gk_ref, dq_ref, dk_ref, dvb_ref, dgq_ref, dgk_ref):
        bd = _head_group_matrix()

        def bwd(xv, dy, g):
            ms = _dot_exact_rhs(xv * xv, bd, pieces=2) * (1.0 / ATTN_DH)
            rstd = lax.rsqrt(ms + RMS_EPS)
            xhat = xv * rstd
            dxhat = dy * g
            gm = _dot_exact_rhs(dxhat * xhat, bd, pieces=2) * (1.0 / ATTN_DH)
            return rstd * (dxhat - xhat * gm), jnp.sum(dy * xhat, axis=0, keepdims=True)

        dq, dgq = bwd(q_ref[...], dqn_ref[...], gq_ref[...])
        dk, dgk = bwd(k_ref[...], dkn_ref[...], gk_ref[...])
        dq_ref[...] = dq.astype(BF16)
        dk_ref[...] = dk.astype(BF16)
        dvb_ref[...] = dv_ref[...].astype(BF16)
        dgq_ref[...] = dgq
        dgk_ref[...] = dgk

    col = lambda c: pl.BlockSpec((None, tm, ATTN_W), lambda bi, j: (bi, j, c))
    past_pad = pl.BlockSpec((None, tm, ATTN_W), lambda bi, j: (bi, j + 1, 0))
    gspec = pl.BlockSpec((1, ATTN_W), lambda bi, j: (0, 0))
    pspec = pl.BlockSpec((None, 1, ATTN_W), lambda bi, j: (bi * nb + j, 0, 0))
    o_shape = jax.ShapeDtypeStruct((b, s, ATTN_W), BF16)
    p_shape = jax.ShapeDtypeStruct((b * nb, 1, ATTN_W), F32)
    return pl.pallas_call(
        body,
        name=name,
        grid=(b, nb),
        in_specs=[col(0), col(1), col(0), past_pad, past_pad, gspec, gspec],
        out_specs=[col(0)] * 3 + [pspec] * 2,
        out_shape=[o_shape] * 3 + [p_shape] * 2,
        compiler_params=_params("parallel", "parallel"),
    )(proj, proj, dqn, dkn, dv, gq, gk)


Q_CHUNKS = 4
QBLK = Q_CHUNKS * CHUNK
WIN = (LEFT_CHUNKS + Q_CHUNKS) * CHUNK
DB_W = BAND + CHUNK
MASKED = -1e30


def _band_table(bias):
    rows = [jnp.pad(bias, ((0, 0), (0, 0), (CHUNK * i, WIN - BAND - CHUNK * i)), constant_values=MASKED)
            for i in range(Q_CHUNKS)]
    return jnp.concatenate(rows, axis=1)


def _head_lanes(hh):
    lane = lax.broadcasted_iota(jnp.int32, (1, LANES), 1)
    return (lane < ATTN_DH) if hh == 0 else (lane >= ATTN_DH)


def _attn_probs(qh, kw, table, start):
    s = _dot(qh, kw, NT) * (ATTN_DH ** -0.5) + table
    col = lax.broadcasted_iota(jnp.int32, (QBLK, WIN), 1)
    s = jnp.where(col + start >= KPAD, s, MASKED)
    m = jnp.max(s, axis=-1, keepdims=True)
    p = jnp.exp(s - m)
    return p * (1.0 / jnp.sum(p, axis=-1, keepdims=True))


def _attn_fwd(name, q, k, v, table, deps=()):
    b, s, w = q.shape
    sp = k.shape[1]

    def body(q_ref, k_ref, v_ref, t_ref, *rest):
        o_ref = rest[-1]
        start = pl.multiple_of(pl.program_id(2) * QBLK, QBLK)
        kw = k_ref[pl.ds(start, WIN), :]
        vw = v_ref[pl.ds(start, WIN), :]
        q2 = q_ref[...]
        lanes = [_head_lanes(hh) for hh in range(2)]
        probs = [_attn_probs(jnp.where(mine, q2, jnp.zeros_like(q2)), kw, t_ref[hh], start).astype(BF16)
                 for hh, mine in enumerate(lanes)]
        outs = [_dot(p, vw) for p in probs]
        o_ref[...] = jnp.where(lanes[0], outs[0], outs[1]).astype(BF16)

    qspec = pl.BlockSpec((None, QBLK, LANES), lambda p, bi, i: (bi, i, p))
    kspec = pl.BlockSpec((None, sp, LANES), lambda p, bi, i: (bi, 0, p))
    return pl.pallas_call(
        body,
        name=name,
        grid=(w // LANES, b, s // QBLK),
        in_specs=[qspec, kspec, kspec, pl.BlockSpec((2, QBLK, WIN), lambda p, bi, i: (p, 0, 0))] + [ANY] * len(deps),
        out_specs=qspec,
        out_shape=jax.ShapeDtypeStruct((b, s, w), BF16),
        compiler_params=_params("parallel", "parallel", "arbitrary"),
    )(q, k, v, table, *deps)


def _attn_bwd(name, q, k, v, table, dmix):
    b, s, w = q.shape
    sp = k.shape[1]

    def body(q_ref, k_ref, v_ref, t_ref, do_ref, dq_ref, dk_ref, dv_ref, dbe_ref, dbo_ref):
        bi = pl.program_id(1)
        i = pl.program_id(2)
        start = pl.multiple_of(i * QBLK, QBLK)
        win = pl.ds(start, WIN)

        @pl.when(i == 0)
        def _():
            dk_ref[...] = jnp.zeros_like(dk_ref)
            dv_ref[...] = jnp.zeros_like(dv_ref)

        @pl.when(jnp.logical_and(i == 0, bi == 0))
        def _():
            dbe_ref[...] = jnp.zeros_like(dbe_ref)
            dbo_ref[...] = jnp.zeros_like(dbo_ref)

        kw = k_ref[win, :]
        vw = v_ref[win, :]
        q2 = q_ref[...]
        do2 = do_ref[...].astype(BF16)
        lanes = [_head_lanes(hh) for hh in range(2)]
        qh = [jnp.where(mine, q2, jnp.zeros_like(q2)) for mine in lanes]
        doh = [jnp.where(mine, do2, jnp.zeros_like(do2)) for mine in lanes]
        p = [_attn_probs(qh[hh], kw, t_ref[hh], start) for hh in range(2)]
        dp = [_dot(doh[hh], vw, NT) for hh in range(2)]
        ds = [p[hh] * (dp[hh] - jnp.sum(p[hh] * dp[hh], axis=-1, keepdims=True)) for hh in range(2)]
        dsb = [(x * (ATTN_DH ** -0.5)).astype(BF16) for x in ds]
        pb = [x.astype(BF16) for x in p]
        dq = [_dot(dsb[hh], kw) for hh in range(2)]
        dk = [_dot(dsb[hh], qh[hh], TN) for hh in range(2)]
        dv = [_dot(pb[hh], doh[hh], TN) for hh in range(2)]
        for hh in range(2):
            for qi in range(Q_CHUNKS):
                c0 = (qi // 2) * LANES
                blk = ds[hh][qi * CHUNK:(qi + 1) * CHUNK, c0:c0 + DB_W]
                if qi % 2 == 0:
                    dbe_ref[hh] += blk
                else:
                    dbo_ref[hh] += blk
        dq_ref[...] = jnp.where(lanes[0], dq[0], dq[1])
        dk_ref[win, :] += dk[0] + dk[1]
        dv_ref[win, :] += dv[0] + dv[1]

    qspec = pl.BlockSpec((None, QBLK, LANES), lambda p, bi, i: (bi, i, p))
    kspec = pl.BlockSpec((None, sp, LANES), lambda p, bi, i: (bi, 0, p))
    dbspec = pl.BlockSpec((2, CHUNK, DB_W), lambda p, bi, i: (p, 0, 0))
    db_shape = jax.ShapeDtypeStruct((ATTN_HEADS, CHUNK, DB_W), F32)
    return pl.pallas_call(
        body,
        name=name,
        grid=(w // LANES, b, s // QBLK),
        in_specs=[qspec, kspec, kspec, pl.BlockSpec((2, QBLK, WIN), lambda p, bi, i: (p, 0, 0)), qspec],
        out_specs=[qspec, kspec, kspec, dbspec, dbspec],
        out_shape=[jax.ShapeDtypeStruct((b, s, w), F32), jax.ShapeDtypeStruct((b, sp, w), F32),
                   jax.ShapeDtypeStruct((b, sp, w), F32), db_shape, db_shape],
        compiler_params=_params("arbitrary", "arbitrary", "arbitrary"),
    )(q, k, v, table, dmix)


HQ_COL = 3 * ATTN_W // HGRN_DH
HF_COL = HQ_COL + HGRN_HEADS
HI_COL = HF_COL + HGRN_HEADS
HG_COL = HI_COL + HGRN_HEADS
HGRN_ROWS = 8 * CHUNK
HEAD_LANES = [slice(hh * HGRN_DH, (hh + 1) * HGRN_DH) for hh in range(HGRN_HEADS)]


def _tri(lower):
    r = lax.broadcasted_iota(jnp.int32, (CHUNK, CHUNK), 0)
    c = lax.broadcasted_iota(jnp.int32, (CHUNK, CHUNK), 1)
    return (r >= c) if lower else (r <= c)


def _hgrn_chunk(hq, hf, lb, tril):
    sig = _sigmoid(hf)
    f = lb + (1.0 - lb) * sig
    g = jnp.log(f)
    ones_l = jnp.where(tril, 1.0, 0.0).astype(BF16)
    b = _dot_exact_lhs(ones_l, g)
    bl = jnp.sum(g, axis=0, keepdims=True)
    rows = lax.broadcasted_iota(jnp.int32, g.shape, 0)
    bm = jnp.sum(jnp.where(rows <= CHUNK // 2, g, 0.0), axis=0, keepdims=True)
    sq = _sigmoid(hq)
    q = hq * sq
    k = 1.0 - f
    return sig, f, b, bl, bm, sq, q, k


def _hgrn_fwd(name, proj, attn, lb, go, b, s):
    nc = s // CHUNK
    t = b * s
    nblk = s // HGRN_ROWS
    cpb = HGRN_ROWS // CHUNK

    def body(hq_ref, hf_ref, hi_ref, hg_ref, attn_ref, lb_ref, go_ref, mix_ref, oraw_ref, st_ref, s_scr):
        tril = _tri(True)
        gov = go_ref[...]
        mix_ref[:, 0:ATTN_W] = attn_ref[...]

        @pl.when(pl.program_id(1) == 0)
        def _():
            s_scr[...] = jnp.zeros_like(s_scr)

        def step(c, carry):
            sl = pl.ds(pl.multiple_of(c * CHUNK, CHUNK), CHUNK)
            hg = hg_ref[sl, :]
            _, _, bb, bl, bm, _, q, k = _hgrn_chunk(hq_ref[sl, :], hf_ref[sl, :], lb_ref[...], tril)
            vb = hi_ref[sl, :].astype(BF16)
            qe = (q * jnp.exp(bb - bm)).astype(BF16)
            ke = (k * jnp.exp(bm - bb)).astype(BF16)
            qb = (q * jnp.exp(bb)).astype(BF16)
            kb = (k * jnp.exp(bl - bb)).astype(BF16)
            e_last = jnp.exp(bl)
            gate = _silu(hg)
            st = [s_scr[hh] for hh in range(HGRN_HEADS)]
            a = [jnp.where(tril, _dot(qe[:, hs], ke[:, hs], NT), 0.0).astype(BF16) for hs in HEAD_LANES]
            o_state = [_dot(qb[:, hs], st[hh].astype(BF16), NT) for hh, hs in enumerate(HEAD_LANES)]
            st_next = [st[hh] * e_last[:, hs] + _dot(vb[:, hs], kb[:, hs], TN) for hh, hs in enumerate(HEAD_LANES)]
            o = [_dot(a[hh], vb[:, hs]) + o_state[hh] for hh, hs in enumerate(HEAD_LANES)]
            ro = [(oh * lax.rsqrt(jnp.mean(oh * oh, axis=-1, keepdims=True) + RMS_EPS) * gov) * gate[:, hs]
                  for oh, hs in zip(o, HEAD_LANES)]
            for hh in range(HGRN_HEADS):
                st_ref[hh, c] = st[hh]
                s_scr[hh] = st_next[hh]
            mix_ref[sl, ATTN_W:ATTN_W + HGRN_W] = jnp.concatenate(ro, axis=1).astype(BF16)
            oraw_ref[sl, :] = jnp.concatenate(o, axis=1)
            return carry

        lax.fori_loop(0, cpb, step, 0)

    col = lambda base: pl.BlockSpec((HGRN_ROWS, HGRN_W), lambda bi, i: (bi * nblk + i, base // HGRN_HEADS))
    out = pl.BlockSpec((HGRN_ROWS, HGRN_W), lambda bi, i: (bi * nblk + i, 0))
    return pl.pallas_call(
        body,
        name=name,
        grid=(b, nblk),
        in_specs=[col(HQ_COL), col(HF_COL), col(HI_COL), col(HG_COL), out,
                  pl.BlockSpec((1, HGRN_W), lambda bi, i: (0, 0)), pl.BlockSpec((1, HGRN_DH), lambda bi, i: (0, 0))],
        out_specs=[pl.BlockSpec((HGRN_ROWS, ATTN_W + HGRN_W), lambda bi, i: (bi * nblk + i, 0)), out,
                   pl.BlockSpec((None, HGRN_HEADS, cpb, HGRN_DH, HGRN_DH), lambda bi, i: (bi, 0, i, 0, 0))],
        out_shape=[jax.ShapeDtypeStruct((t, ATTN_W + HGRN_W), BF16), jax.ShapeDtypeStruct((t, HGRN_W), F32),
                   jax.ShapeDtypeStruct((b, HGRN_HEADS, nc, HGRN_DH, HGRN_DH), F32)],
        scratch_shapes=[pltpu.VMEM((HGRN_HEADS, HGRN_DH, HGRN_DH), F32)],
        compiler_params=_params("parallel", "arbitrary"),
    )(proj, proj, proj, proj, attn, lb, go)


def _hgrn_bwd(name, proj, dqkv, lb, go, oraw, states, dmix, b, s):
    t = b * s
    nblk = s // HGRN_ROWS
    cpb = HGRN_ROWS // CHUNK

    def body(hq_ref, hf_ref, hi_ref, hg_ref, dq_ref, dk_ref, dv_ref, lb_ref, go_ref, oraw_ref, st_ref, dro_ref,
             dp_ref, dlb_ref, dgo_ref, ds_scr, dlb_scr, dgo_scr):
        tril = _tri(True)
        ones_u = jnp.where(_tri(False), 1.0, 0.0).astype(BF16)
        gov = go_ref[...]
        dp_ref[:, 0:ATTN_W] = dq_ref[...]
        dp_ref[:, ATTN_W:2 * ATTN_W] = dk_ref[...]
        dp_ref[:, 2 * ATTN_W:3 * ATTN_W] = dv_ref[...]

        @pl.when(pl.program_id(1) == 0)
        def _():
            ds_scr[...] = jnp.zeros_like(ds_scr)
            dlb_scr[...] = jnp.zeros_like(dlb_scr)
            dgo_scr[...] = jnp.zeros_like(dgo_scr)

        def step(ci, carry):
            c = cpb - 1 - ci
            sl = pl.ds(pl.multiple_of(c * CHUNK, CHUNK), CHUNK)
            hq = hq_ref[sl, :]
            hg = hg_ref[sl, :]
            sig, f, bb, bl, bm, sq, q, k = _hgrn_chunk(hq, hf_ref[sl, :], lb_ref[...], tril)
            vb = hi_ref[sl, :].astype(BF16)
            ebm = jnp.exp(bb - bm)
            embm = jnp.exp(bm - bb)
            eb = jnp.exp(bb)
            ebl = jnp.exp(bl - bb)
            e_last = jnp.exp(bl)
            qe = (q * ebm).astype(BF16)
            ke = (k * embm).astype(BF16)
            qb = (q * eb).astype(BF16)
            kb = (k * ebl).astype(BF16)
            st = [st_ref[hh, c] for hh in range(HGRN_HEADS)]
            dst = [ds_scr[hh] for hh in range(HGRN_HEADS)]
            o = oraw_ref[sl, :]
            dro = dro_ref[sl, :]
            sg = _sigmoid(hg)
            gov4 = jnp.concatenate([gov] * HGRN_HEADS, axis=1)
            rstd = jnp.concatenate(
                [jnp.broadcast_to(lax.rsqrt(jnp.mean(o[:, hs] * o[:, hs], axis=-1, keepdims=True) + RMS_EPS),
                                  (CHUNK, HGRN_DH)) for hs in HEAD_LANES], axis=1)
            ohat = o * rstd
            dn = dro * (hg * sg)
            dhg = dro * (ohat * gov4) * (sg * (1.0 + hg * (1.0 - sg)))
            dgo_inc = jnp.sum(dn * ohat, axis=0, keepdims=True)
            dohat = dn * gov4
            proj_h = dohat * ohat
            pm = jnp.concatenate(
                [jnp.broadcast_to(jnp.mean(proj_h[:, hs], axis=-1, keepdims=True), (CHUNK, HGRN_DH))
                 for hs in HEAD_LANES], axis=1)
            dob = (rstd * (dohat - ohat * pm)).astype(BF16)
            stb = [x.astype(BF16) for x in st]
            dstb = [x.astype(BF16) for x in dst]
            a = [jnp.where(tril, _dot(qe[:, hs], ke[:, hs], NT), 0.0).astype(BF16) for hs in HEAD_LANES]
            dab = [jnp.where(tril, _dot(dob[:, hs], vb[:, hs], NT), 0.0).astype(BF16) for hs in HEAD_LANES]
            dqb = [_dot(dob[:, hs], stb[hh]) for hh, hs in enumerate(HEAD_LANES)]
            dkb = [_dot(vb[:, hs], dstb[hh]) for hh, hs in enumerate(HEAD_LANES)]
            dv_state = [_dot(kb[:, hs], dstb[hh], NT) for hh, hs in enumerate(HEAD_LANES)]
            dst_next = [dst[hh] * e_last[:, hs] + _dot(dob[:, hs], qb[:, hs], TN) for hh, hs in enumerate(HEAD_LANES)]
            dv = [_dot(a[hh], dob[:, hs], TN) + dv_state[hh] for hh, hs in enumerate(HEAD_LANES)]
            dqe = jnp.concatenate([_dot(dab[hh], ke[:, hs]) for hh, hs in enumerate(HEAD_LANES)], axis=1)
            dke = jnp.concatenate([_dot(dab[hh], qe[:, hs], TN) for hh, hs in enumerate(HEAD_LANES)], axis=1)
            dqb = jnp.concatenate(dqb, axis=1)
            dkb = jnp.concatenate(dkb, axis=1)
            state_term = jnp.concatenate(
                [jnp.sum(dst[hh] * st[hh], axis=0, keepdims=True) for hh in range(HGRN_HEADS)], axis=1)
            dq = dqe * ebm + dqb * eb
            dk = dke * embm + dkb * ebl
            db = (qe.astype(F32) * dqe - ke.astype(F32) * dke) + q * (dqb * eb) - k * (dkb * ebl)
            d_last = jnp.sum(k * ebl * dkb, axis=0, keepdims=True) + state_term * e_last
            dg = _dot_exact_lhs(ones_u, db) + d_last
            df = dg / f - dk
            first = HQ_COL * HGRN_DH
            dp_ref[sl, first:first + HGRN_W] = (dq * (sq * (1.0 + hq * (1.0 - sq)))).astype(BF16)
            dp_ref[sl, first + HGRN_W:first + 2 * HGRN_W] = (df * (1.0 - lb_ref[...]) * sig * (1.0 - sig)).astype(BF16)
            dp_ref[sl, first + 2 * HGRN_W:first + 3 * HGRN_W] = jnp.concatenate(dv, axis=1).astype(BF16)
            dp_ref[sl, first + 3 * HGRN_W:first + 4 * HGRN_W] = dhg.astype(BF16)
            dlb_scr[...] += jnp.sum(df * (1.0 - sig), axis=0, keepdims=True)
            dgo_scr[...] += dgo_inc
            for hh in range(HGRN_HEADS):
                ds_scr[hh] = dst_next[hh]
            return carry

        lax.fori_loop(0, cpb, step, 0)

        @pl.when(pl.program_id(1) == nblk - 1)
        def _():
            dlb_ref[...] = dlb_scr[...]
            dgo_ref[...] = dgo_scr[...]

    rows = lambda bi, i: bi * nblk + (nblk - 1 - i)
    col = lambda base: pl.BlockSpec((HGRN_ROWS, HGRN_W), lambda bi, i: (rows(bi, i), base // HGRN_HEADS))
    out = pl.BlockSpec((HGRN_ROWS, HGRN_W), lambda bi, i: (rows(bi, i), 0))
    part = pl.BlockSpec((None, 1, HGRN_W), lambda bi, i: (bi, 0, 0))
    width = HG_COL * HGRN_DH + HGRN_W
    o_shape = jax.ShapeDtypeStruct((t, width), BF16)
    p_shape = jax.ShapeDtypeStruct((b, 1, HGRN_W), F32)
    return pl.pallas_call(
        body,
        name=name,
        grid=(b, nblk),
        in_specs=[col(HQ_COL), col(HF_COL), col(HI_COL), col(HG_COL), out, out, out,
                  pl.BlockSpec((1, HGRN_W), lambda bi, i: (0, 0)), pl.BlockSpec((1, HGRN_DH), lambda bi, i: (0, 0)), out,
                  pl.BlockSpec((None, HGRN_HEADS, cpb, HGRN_DH, HGRN_DH), lambda bi, i: (bi, 0, nblk - 1 - i, 0, 0)),
                  col(ATTN_W // HGRN_DH)],
        out_specs=[pl.BlockSpec((HGRN_ROWS, width), lambda bi, i: (rows(bi, i), 0))] + [part] * 2,
        out_shape=[o_shape] + [p_shape] * 2,
        scratch_shapes=[pltpu.VMEM((HGRN_HEADS, HGRN_DH, HGRN_DH), F32), pltpu.VMEM((1, HGRN_W), F32),
                        pltpu.VMEM((1, HGRN_W), F32)],
        compiler_params=_params("parallel", "arbitrary"),
    )(proj, proj, proj, proj, *dqkv, lb, go, oraw, states, dmix)


def _small_grads(name, dg1, dgm, dg2, dgq, dgk, dbias_t, dlb, dgo, lbp):
    d = dg1.shape[1]

    def body(dg1_ref, dgm_ref, dg2_ref, dgq_ref, dgk_ref, dbias_ref, dlb_ref, dgo_ref, lbp_ref,
             g1_ref, gm_ref, g2_ref, gq_ref, gk_ref, rb_ref, lbg_ref, go_ref):
        g1_ref[...] = jnp.sum(dg1_ref[...], axis=0, keepdims=True)
        gm_ref[...] = jnp.sum(dgm_ref[...], axis=0, keepdims=True)
        g2_ref[...] = jnp.sum(dg2_ref[...], axis=0, keepdims=True)
        r = lax.broadcasted_iota(jnp.int32, (ATTN_W, ATTN_DH), 0)
        cidx = lax.broadcasted_iota(jnp.int32, (ATTN_W, ATTN_DH), 1)
        fold = jnp.where(jnp.bitwise_and(r, ATTN_DH - 1) == cidx, 1.0, 0.0).astype(BF16)
        gq_ref[...] = jnp.sum(_dot_exact_rhs(dgq_ref[...], fold), axis=0, keepdims=True)
        gk_ref[...] = jnp.sum(_dot_exact_rhs(dgk_ref[...], fold), axis=0, keepdims=True)
        gosum = jnp.sum(dgo_ref[...], axis=0, keepdims=True)
        go_ref[...] = (gosum[:, 0:HGRN_DH] + gosum[:, HGRN_DH:2 * HGRN_DH]
                       + gosum[:, 2 * HGRN_DH:3 * HGRN_DH] + gosum[:, 3 * HGRN_DH:4 * HGRN_DH])
        p0 = lbp_ref[0:1, :]
        p1 = lbp_ref[1:2, :]
        lbv = 1.0 / (1.0 + jnp.exp(p1 - p0))
        dp0 = jnp.sum(dlb_ref[...], axis=0, keepdims=True) * lbv * (1.0 - lbv)
        lbg_ref[0:1, :] = dp0
        lbg_ref[1:2, :] = -dp0
        sidx = lax.broadcasted_iota(jnp.int32, (BAND, N_REL_PAD), 0)
        ridx = lax.broadcasted_iota(jnp.int32, (BAND, N_REL_PAD), 1)

        def step(tq, acc):
            rel = jnp.clip(tq + KPAD - sidx, -REL_CLIP, REL_CLIP) + REL_CLIP
            onehot = jnp.where(rel == ridx, 1.0, 0.0).astype(BF16)
            return acc + _dot_exact_rhs(dbias_ref[tq], onehot)

        rb_ref[...] = lax.fori_loop(0, CHUNK, step, jnp.zeros((ATTN_HEADS, N_REL_PAD), F32))

    ins = [dg1, dgm, dg2, dgq, dgk, dbias_t, dlb, dgo, lbp]
    outs = [jax.ShapeDtypeStruct((1, d), F32)] * 3 + [jax.ShapeDtypeStruct((1, ATTN_DH), F32)] * 2 + [
        jax.ShapeDtypeStruct((ATTN_HEADS, N_REL_PAD), F32), jax.ShapeDtypeStruct((2, HGRN_W), F32),
        jax.ShapeDtypeStruct((1, HGRN_DH), F32)]
    vm = pl.BlockSpec(memory_space=pltpu.VMEM)
    return pl.pallas_call(
        body,
        name=name,
        in_specs=[vm] * len(ins),
        out_specs=[vm] * len(outs),
        out_shape=outs,
        compiler_params=pltpu.CompilerParams(vmem_limit_bytes=VMEM_LIMIT),
    )(*ins)


def _adam_update(w, g, m, v):
    m2 = ADAM_B1 * m + (1.0 - ADAM_B1) * g
    v2 = ADAM_B2 * v + (1.0 - ADAM_B2) * (g * g)
    m_hat = m2 / (1.0 - ADAM_B1 ** ADAM_STEP)
    v_hat = v2 / (1.0 - ADAM_B2 ** ADAM_STEP)
    delta = -ADAM_LR * (m_hat / (jnp.sqrt(v_hat) + ADAM_EPS) + ADAM_WD * w)
    return delta, m2, v2


def _rows_tile(r):
    return r if r <= 512 or r % 512 else 512


def _pair_sum(name, grad, theirs, core):
    n, half, c = theirs.shape
    tr = _rows_tile(half)
    nth = half // tr

    def body(core_ref, a_ref, b_ref, o_ref):
        o_ref[...] = (a_ref[...].astype(F32) + b_ref[...].astype(F32)).astype(o_ref.dtype)

    spec = pl.BlockSpec((None, tr, c), lambda i, j, core_ref: (i, j, 0))
    return pl.pallas_call(
        body, name=name,
        grid_spec=pltpu.PrefetchScalarGridSpec(
            num_scalar_prefetch=1, grid=(n, nth),
            in_specs=[pl.BlockSpec((None, tr, c), lambda i, j, core_ref: (i, core_ref[0] * nth + j, 0)), spec],
            out_specs=spec),
        out_shape=pltpu.HBM((n, half, c), BF16), compiler_params=_params("parallel", "parallel"),
    )(core, grad, theirs)


def _chip_sum(name, own, parts, chip):
    _, half, c = own.shape
    tr = _rows_tile(half)

    def body(chip_ref, own_ref, p_ref, o_ref):
        me = chip_ref[0]
        mine = own_ref[...].astype(F32)
        flip_x, flip_y, flip_xy = (p_ref[i].astype(F32) for i in range(3))
        acc = None
        for k in range(N_CHIPS):
            rel = jnp.bitwise_xor(me, k)
            term = jnp.where(rel == 0, mine, jnp.where(rel == 2, flip_x, jnp.where(rel == 1, flip_y, flip_xy)))
            acc = term if acc is None else acc + term
        o_ref[...] = acc

    return pl.pallas_call(
        body, name=name,
        grid_spec=pltpu.PrefetchScalarGridSpec(
            num_scalar_prefetch=1, grid=(half // tr,),
            in_specs=[pl.BlockSpec((None, tr, c), lambda j, chip_ref: (chip_ref[0], j, 0)),
                      pl.BlockSpec((3, tr, c), lambda j, chip_ref: (0, j, 0))],
            out_specs=pl.BlockSpec((tr, c), lambda j, chip_ref: (j, 0))),
        out_shape=pltpu.HBM((half, c), F32), compiler_params=_params("parallel"),
    )(chip, own, parts)


def _adamw(name, w, g_mine, g_theirs, m, v, core):
    _, r, c = w.shape
    half = r // 2
    tr = _rows_tile(half)
    nth = half // tr

    def body(core_ref, w_ref, gm_ref, gt_ref, m_ref, v_ref, g_ref, d_ref, m2_ref, v2_ref):
        g = jnp.where(pl.program_id(0) == core_ref[0], gm_ref[...], gt_ref[...])
        delta, m2, v2 = _adam_update(w_ref[...], g, m_ref[...], v_ref[...])
        g_ref[...] = g
        d_ref[...] = delta
        m2_ref[...] = m2
        v2_ref[...] = v2

    full = pl.BlockSpec((None, tr, c), lambda h, j, core_ref: (0, h * nth + j, 0))
    part = pl.BlockSpec((tr, c), lambda h, j, core_ref: (j, 0))
    shape = jax.ShapeDtypeStruct((1, r, c), F32)
    return pl.pallas_call(
        body, name=name,
        grid_spec=pltpu.PrefetchScalarGridSpec(
            num_scalar_prefetch=1, grid=(2, nth), in_specs=[full, part, part, full, full], out_specs=[full] * 4),
        out_shape=[shape] * 4, compiler_params=_params("parallel", "parallel"),
    )(core, w, g_mine, g_theirs, m, v)


def _rel_bias_table(name, rel_bias):
    padded = jnp.pad(rel_bias, ((0, 0), (0, N_REL_PAD - N_REL)))

    def body(rb_ref, o_ref):
        ridx = lax.broadcasted_iota(jnp.int32, (N_REL_PAD, BAND), 0)
        sidx = lax.broadcasted_iota(jnp.int32, (N_REL_PAD, BAND), 1)
        rb = rb_ref[...]

        def step(tq, carry):
            rel = jnp.clip(tq + KPAD - sidx, -REL_CLIP, REL_CLIP) + REL_CLIP
            onehot = jnp.where(rel == ridx, 1.0, 0.0).astype(BF16)
            o_ref[tq] = _dot_exact_rhs(rb, onehot)
            return carry

        lax.fori_loop(0, CHUNK, step, 0)

    vm = pl.BlockSpec(memory_space=pltpu.VMEM)
    table = pl.pallas_call(
        body, name=name, in_specs=[vm], out_specs=vm,
        out_shape=jax.ShapeDtypeStruct((CHUNK, ATTN_HEADS, BAND), F32),
    )(padded)
    return table.transpose(1, 0, 2)


def _adamw_small(name, w, parts, m, v):
    def body(w_ref, p_ref, m_ref, v_ref, g_ref, d_ref, m2_ref, v2_ref):
        g = p_ref[0]
        for i in range(1, N_DEV):
            g = g + p_ref[i]
        delta, m2, v2 = _adam_update(w_ref[...], g, m_ref[...], v_ref[...])
        g_ref[...] = g
        d_ref[...] = delta
        m2_ref[...] = m2
        v2_ref[...] = v2

    vm = pl.BlockSpec(memory_space=pltpu.VMEM)
    shape = jax.ShapeDtypeStruct((SMALL_ROWS, SMALL_COLS), F32)
    return pl.pallas_call(
        body, name=name, in_specs=[vm] * 4, out_specs=[vm] * 4, out_shape=[shape] * 4,
    )(w, parts, m, v)


def _position():
    return lax.axis_index("x"), lax.axis_index("y"), lax.axis_index("c")


def _other_chips(x, y):
    return [(1 - x, y), (x, 1 - y), (1 - x, 1 - y)]


ANY = pl.BlockSpec(memory_space=pl.ANY)
PAIR_ID = 0


def _pair_handshake():
    x, y, c = _position()
    barrier = pltpu.get_barrier_semaphore()
    pl.semaphore_signal(barrier, inc=1, device_id=(x, y, 1 - c), device_id_type=MESH)
    pl.semaphore_wait(barrier, 1)


PAIR_CALL = pltpu.CompilerParams(collective_id=PAIR_ID)


HBM = pl.BlockSpec(memory_space=pltpu.HBM)
SEM = pl.BlockSpec(memory_space=pltpu.SEMAPHORE)
SPLIT_COPY = pltpu.SideEffectType.DATAFLOW_SIDE_EFFECTING


def _gather_copy(shards, outs, send_sem, recv_sem, i, j):
    x, y, c = _position()
    chips = _other_chips(x, y)
    half = shards[i].shape[0] // 2
    rows = pl.ds(pl.multiple_of(c * half, 16), half)
    return pltpu.make_async_remote_copy(
        src_ref=shards[i].at[rows, :], dst_ref=outs[i].at[2 * x + y, rows, :],
        send_sem=send_sem.at[3 * i + j], recv_sem=recv_sem.at[3 * i + j],
        device_id=(chips[j][0], chips[j][1], c), device_id_type=MESH)


def _gather_start(name, shards, after):
    n = len(shards)

    def body(*refs):
        srcs, outs = refs[:n], refs[n:2 * n]
        send_sem, recv_sem = refs[2 * n + len(after)], refs[2 * n + len(after) + 1]
        token = refs[-1]
        for i in range(n):
            for j in range(3):
                _gather_copy(srcs, outs, send_sem, recv_sem, i, j).start()
        token[...] = jnp.zeros_like(token)

    full = [(N_CHIPS,) + s.shape for s in shards]
    res = pl.pallas_call(
        body,
        name=name,
        in_specs=[HBM] * (2 * n) + [ANY] * len(after),
        out_specs=[SEM, SEM] + [HBM] * (2 * n) + [pl.BlockSpec(memory_space=pltpu.VMEM)],
        out_shape=[pltpu.SemaphoreType.DMA((3 * n,)), pltpu.SemaphoreType.DMA((3 * n,))]
        + [pltpu.HBM(s.shape, s.dtype) for s in shards]
        + [pltpu.HBM(shp, s.dtype) for shp, s in zip(full, shards)]
        + [jax.ShapeDtypeStruct((8, LANES), F32)],
        input_output_aliases={i: 2 + i for i in range(2 * n)},
        compiler_params=pltpu.CompilerParams(has_side_effects=SPLIT_COPY),
    )(*[pltpu.with_memory_space_constraint(s, pltpu.HBM) for s in shards],
      *[pltpu.with_memory_space_constraint(lax.empty(shp, s.dtype), pltpu.HBM) for shp, s in zip(full, shards)],
      *after)
    return res[0], res[1], list(res[2:2 + n]), list(res[2 + n:2 + 2 * n]), res[-1]


def _gather_wait(name, send_sem, recv_sem, shards, outs, after):
    n = len(shards)

    def body(*refs):
        srcs, out_refs = refs[:n], refs[n:2 * n]
        send_ref, recv_ref = refs[2 * n], refs[2 * n + 1]
        for i in range(n):
            for j in range(3):
                copy = _gather_copy(srcs, out_refs, send_ref, recv_ref, i, j)
                copy.wait_send()
                copy.wait_recv()

    res = pl.pallas_call(
        body,
        name=name,
        in_specs=[HBM] * (2 * n) + [SEM, SEM] + [ANY] * len(after),
        out_specs=[HBM] * (2 * n),
        out_shape=[pltpu.HBM(s.shape, s.dtype) for s in shards] + [pltpu.HBM(o.shape, o.dtype) for o in outs],
        input_output_aliases={i: i for i in range(2 * n)},
        compiler_params=pltpu.CompilerParams(has_side_effects=SPLIT_COPY),
    )(*shards, *outs, send_sem, recv_sem, *after)
    return list(res[:n]), list(res[n:])


def _join_copies(srcs, ins, outs, own_send, own_recv, half_send, half_recv):
    x, y, c = _position()
    chips = _other_chips(x, y)
    copies = []
    for i in range(len(srcs)):
        copies.append(pltpu.make_async_remote_copy(
            src_ref=srcs[i], dst_ref=outs[i].at[2 * x + y], send_sem=own_send.at[i], recv_sem=own_recv.at[i],
            device_id=(x, y, 1 - c), device_id_type=MESH))
        half = srcs[i].shape[0] // 2
        rows = pl.ds(pl.multiple_of(c * half, 16), half)
        for j in range(3):
            slot = 2 * chips[j][0] + chips[j][1]
            copies.append(pltpu.make_async_remote_copy(
                src_ref=ins[i].at[slot, rows, :], dst_ref=outs[i].at[slot, rows, :],
                send_sem=half_send.at[3 * i + j], recv_sem=half_recv.at[3 * i + j],
                device_id=(x, y, 1 - c), device_id_type=MESH))
    return copies


def _gather_join(name, shards, outs):
    n = len(shards)

    def body(*refs):
        _pair_handshake()
        copies = _join_copies(refs[:n], refs[n:2 * n], refs[2 * n:3 * n], *refs[3 * n:])
        for cp in copies:
            cp.start()
        for cp in copies:
            cp.wait()

    return pl.pallas_call(
        body,
        name=name,
        in_specs=[ANY] * (2 * n),
        out_specs=[HBM] * n,
        out_shape=[pltpu.HBM(o.shape, o.dtype) for o in outs],
        input_output_aliases={n + i: i for i in range(n)},
        scratch_shapes=[pltpu.SemaphoreType.DMA((n,))] * 2 + [pltpu.SemaphoreType.DMA((3 * n,))] * 2,
        compiler_params=PAIR_CALL,
    )(*shards, *outs)


def _join_start(name, shards, outs):
    n = len(shards)

    def body(*refs):
        _pair_handshake()
        srcs, arrs = refs[:n], refs[n:2 * n]
        sems = refs[2 * n:2 * n + 4]
        token = refs[-1]
        for cp in _join_copies(srcs, arrs, arrs, *sems):
            cp.start()
        token[...] = jnp.zeros_like(token)

    res = pl.pallas_call(
        body,
        name=name,
        in_specs=[HBM] * (2 * n),
        out_specs=[SEM] * 4 + [HBM] * (2 * n) + [pl.BlockSpec(memory_space=pltpu.VMEM)],
        out_shape=[pltpu.SemaphoreType.DMA((n,))] * 2 + [pltpu.SemaphoreType.DMA((3 * n,))] * 2
        + [pltpu.HBM(s.shape, s.dtype) for s in shards] + [pltpu.HBM(o.shape, o.dtype) for o in outs]
        + [jax.ShapeDtypeStruct((8, LANES), F32)],
        input_output_aliases={i: 4 + i for i in range(2 * n)},
        compiler_params=pltpu.CompilerParams(has_side_effects=SPLIT_COPY, collective_id=PAIR_ID),
    )(*shards, *outs)
    return list(res[:4]), list(res[4:4 + n]), list(res[4 + n:4 + 2 * n]), res[-1]


def _join_wait(name, sems, shards, outs, after):
    n = len(shards)

    def body(*refs):
        srcs, arrs = refs[:n], refs[n:2 * n]
        for cp in _join_copies(srcs, arrs, arrs, *refs[2 * n:2 * n + 4]):
            cp.wait_send()
            cp.wait_recv()

    res = pl.pallas_call(
        body,
        name=name,
        in_specs=[HBM] * (2 * n) + [SEM] * 4 + [ANY] * len(after),
        out_specs=[HBM] * (2 * n),
        out_shape=[pltpu.HBM(s.shape, s.dtype) for s in shards] + [pltpu.HBM(o.shape, o.dtype) for o in outs],
        input_output_aliases={i: i for i in range(2 * n)},
        compiler_params=pltpu.CompilerParams(has_side_effects=SPLIT_COPY),
    )(*shards, *outs, *sems, *after)
    return list(res[n:])


def _pair_copy(grads, lands, send_sem, recv_sem, i):
    x, y, c = _position()
    half = grads[i].shape[1] // 2
    give = pl.ds(pl.multiple_of((1 - c) * half, 16), half)
    return pltpu.make_async_remote_copy(
        src_ref=grads[i].at[:, give, :], dst_ref=lands[i], send_sem=send_sem.at[i], recv_sem=recv_sem.at[i],
        device_id=(x, y, 1 - c), device_id_type=MESH)


def _pair_start(name, grads):
    n = len(grads)

    def body(*refs):
        _pair_handshake()
        srcs, lands = refs[:n], refs[n:2 * n]
        send_sem, recv_sem = refs[2 * n], refs[2 * n + 1]
        token = refs[-1]
        for i in range(n):
            _pair_copy(srcs, lands, send_sem, recv_sem, i).start()
        token[...] = jnp.zeros_like(token)

    halves = [(g.shape[0], g.shape[1] // 2, g.shape[2]) for g in grads]
    res = pl.pallas_call(
        body,
        name=name,
        in_specs=[HBM] * (2 * n),
        out_specs=[SEM, SEM] + [HBM] * (2 * n) + [pl.BlockSpec(memory_space=pltpu.VMEM)],
        out_shape=[pltpu.SemaphoreType.DMA((n,)), pltpu.SemaphoreType.DMA((n,))]
        + [pltpu.HBM(g.shape, g.dtype) for g in grads]
        + [pltpu.HBM(shp, g.dtype) for shp, g in zip(halves, grads)]
        + [jax.ShapeDtypeStruct((8, LANES), F32)],
        input_output_aliases={i: 2 + i for i in range(2 * n)},
        compiler_params=pltpu.CompilerParams(has_side_effects=SPLIT_COPY, collective_id=PAIR_ID),
    )(*[pltpu.with_memory_space_constraint(g, pltpu.HBM) for g in grads],
      *[pltpu.with_memory_space_constraint(lax.empty(shp, g.dtype), pltpu.HBM) for shp, g in zip(halves, grads)])
    return res[0], res[1], list(res[2:2 + n]), list(res[2 + n:2 + 2 * n]), res[-1]


def _pair_wait(name, send_sem, recv_sem, grads, lands, after):
    n = len(grads)

    def body(*refs):
        srcs, land_refs = refs[:n], refs[n:2 * n]
        send_ref, recv_ref = refs[2 * n], refs[2 * n + 1]
        for i in range(n):
            copy = _pair_copy(srcs, land_refs, send_ref, recv_ref, i)
            copy.wait_send()
            copy.wait_recv()

    res = pl.pallas_call(
        body,
        name=name,
        in_specs=[HBM] * (2 * n) + [SEM, SEM, ANY],
        out_specs=[HBM] * (2 * n),
        out_shape=[pltpu.HBM(g.shape, g.dtype) for g in grads] + [pltpu.HBM(l.shape, l.dtype) for l in lands],
        input_output_aliases={i: i for i in range(2 * n)},
        compiler_params=pltpu.CompilerParams(has_side_effects=SPLIT_COPY),
    )(*grads, *lands, send_sem, recv_sem, after)
    return list(res[:n]), list(res[n:])


def _scatter_copy(srcs, lands, send_sem, recv_sem, i, j):
    x, y, c = _position()
    chips = _other_chips(x, y)
    return pltpu.make_async_remote_copy(
        src_ref=srcs[i].at[2 * chips[j][0] + chips[j][1]], dst_ref=lands[i].at[j],
        send_sem=send_sem.at[3 * i + j], recv_sem=recv_sem.at[3 * i + j],
        device_id=(chips[j][0], chips[j][1], c), device_id_type=MESH)


def _scatter_start(name, sums):
    n = len(sums)

    def body(*refs):
        srcs, lands = refs[:n], refs[n:2 * n]
        send_sem, recv_sem = refs[2 * n], refs[2 * n + 1]
        token = refs[-1]
        for i in range(n):
            for j in range(3):
                _scatter_copy(srcs, lands, send_sem, recv_sem, i, j).start()
        token[...] = jnp.zeros_like(token)

    land_shapes = [(3,) + s.shape[1:] for s in sums]
    res = pl.pallas_call(
        body,
        name=name,
        in_specs=[HBM] * (2 * n),
        out_specs=[SEM, SEM] + [HBM] * (2 * n) + [pl.BlockSpec(memory_space=pltpu.VMEM)],
        out_shape=[pltpu.SemaphoreType.DMA((3 * n,)), pltpu.SemaphoreType.DMA((3 * n,))]
        + [pltpu.HBM(s.shape, s.dtype) for s in sums]
        + [pltpu.HBM(shp, s.dtype) for shp, s in zip(land_shapes, sums)]
        + [jax.ShapeDtypeStruct((8, LANES), F32)],
        input_output_aliases={i: 2 + i for i in range(2 * n)},
        compiler_params=pltpu.CompilerParams(has_side_effects=SPLIT_COPY),
    )(*[pltpu.with_memory_space_constraint(s, pltpu.HBM) for s in sums],
      *[pltpu.with_memory_space_constraint(lax.empty(shp, s.dtype), pltpu.HBM) for shp, s in zip(land_shapes, sums)])
    return res[0], res[1], list(res[2:2 + n]), list(res[2 + n:2 + 2 * n]), res[-1]


def _scatter_wait(name, send_sem, recv_sem, sums, lands, after):
    n = len(sums)

    def body(*refs):
        srcs, land_refs = refs[:n], refs[n:2 * n]
        send_ref, recv_ref = refs[2 * n], refs[2 * n + 1]
        for i in range(n):
            for j in range(3):
                copy = _scatter_copy(srcs, land_refs, send_ref, recv_ref, i, j)
                copy.wait_send()
                copy.wait_recv()

    res = pl.pallas_call(
        body,
        name=name,
        in_specs=[HBM] * (2 * n) + [SEM, SEM, ANY],
        out_specs=[HBM] * (2 * n),
        out_shape=[pltpu.HBM(s.shape, s.dtype) for s in sums] + [pltpu.HBM(l.shape, l.dtype) for l in lands],
        input_output_aliases={i: i for i in range(2 * n)},
        compiler_params=pltpu.CompilerParams(has_side_effects=SPLIT_COPY),
    )(*sums, *lands, send_sem, recv_sem, after)
    return list(res[:n]), list(res[n:])


def _pair_join(name, halves, small=None):
    n = len(halves)
    if small is None:
        def body_plain(*refs):
            _pair_handshake()
            ins, outs = refs[:n], refs[n:2 * n]
            send_sem, recv_sem = refs[2 * n:]
            x, y, c = _position()
            swaps = [pltpu.make_async_remote_copy(
                src_ref=ins[i], dst_ref=outs[i], send_sem=send_sem.at[i], recv_sem=recv_sem.at[i],
                device_id=(x, y, 1 - c), device_id_type=MESH) for i in range(n)]
            for swap in swaps:
                swap.start()
            for swap in swaps:
                swap.wait()

        return pl.pallas_call(
            body_plain,
            name=name,
            in_specs=[ANY] * n,
            out_specs=[ANY] * n,
            out_shape=[jax.ShapeDtypeStruct(h.shape, h.dtype) for h in halves],
            scratch_shapes=[pltpu.SemaphoreType.DMA((n,))] * 2,
            compiler_params=PAIR_CALL,
        )(*halves)

    def body(*refs):
        ins, small_ref = refs[:n], refs[n]
        outs, all_ref = refs[n + 1:2 * n + 1], refs[2 * n + 1]
        send_sem, recv_sem, sm_send, sm_recv, sm_local = refs[2 * n + 2:]
        x, y, c = _position()
        swaps = []
        for i in range(n):
            swap = pltpu.make_async_remote_copy(
                src_ref=ins[i], dst_ref=outs[i], send_sem=send_sem.at[i], recv_sem=recv_sem.at[i],
                device_id=(x, y, 1 - c), device_id_type=MESH)
            swap.start()
            swaps.append(swap)
        me = 4 * x + 2 * y + c
        sm_own = pltpu.make_async_copy(small_ref, all_ref.at[me], sm_local)
        sm_own.start()
        pushes, arrivals = [], []
        for mask in range(1, N_DEV):
            px, py, pc = x ^ (mask >> 2), y ^ ((mask >> 1) & 1), c ^ (mask & 1)
            pushes.append(pltpu.make_async_remote_copy(
                src_ref=small_ref, dst_ref=all_ref.at[me], send_sem=sm_send.at[mask - 1], recv_sem=sm_recv.at[mask - 1],
                device_id=(px, py, pc), device_id_type=MESH))
            arrivals.append(pltpu.make_async_remote_copy(
                src_ref=small_ref, dst_ref=all_ref.at[4 * px + 2 * py + pc], send_sem=sm_send.at[mask - 1],
                recv_sem=sm_recv.at[mask - 1], device_id=(px, py, pc), device_id_type=MESH))
        for cp in pushes:
            cp.start()
        for swap in swaps:
            swap.wait()
        for cp in arrivals:
            cp.wait_recv()
        for cp in pushes:
            cp.wait_send()
        sm_own.wait()

    res = pl.pallas_call(
        body,
        name=name,
        in_specs=[ANY] * (n + 1),
        out_specs=[ANY] * (n + 1),
        out_shape=[jax.ShapeDtypeStruct(h.shape, h.dtype) for h in halves]
        + [jax.ShapeDtypeStruct((N_DEV,) + small.shape, small.dtype)],
        scratch_shapes=[pltpu.SemaphoreType.DMA((n,))] * 2 + [pltpu.SemaphoreType.DMA((N_DEV - 1,))] * 2
        + [pltpu.SemaphoreType.DMA(())],
    )(*halves, small)
    return res[:n], res[n]


def _lower_bound(lbp):
    return jax.nn.softmax(lbp, axis=0)[0:1]


def _local_step(x, target, g1, gm, g2, gq, gk, go, rel_bias, lbp, weights, on_grads, grads_sent):
    b, s, d = x.shape
    t = b * s
    x0 = x.reshape(t, d)
    tgt = target.reshape(t, d)
    gq_t = jnp.tile(gq, (1, ATTN_HEADS))
    gk_t = jnp.tile(gk, (1, ATTN_HEADS))
    lb = _lower_bound(lbp)
    table = _band_table(_rel_bias_table("rel_bias_table", rel_bias))

    h1 = _rmsnorm_fwd("norm1", x0, g1)
    wg1, wu1, deps1 = weights["first"]((h1, table))
    a1, b1, z1 = _ffn_up("ffn1_up", h1, wg1, wu1, deps1)
    wd1, deps_mid = weights["mid"]((z1,))
    x1, h2 = _ffn_down("ffn1_down", z1, wd1, x0, gm, deps_mid)
    w_in, w_out = weights["mid_rest"]((x1,))
    ns = w_in.shape[0]
    proj = _in_proj("in_proj", h2, w_in)
    proj3 = proj.reshape(b, s, proj.shape[1])
    qn, kn, vb = _qk_prep("qk_prep", proj3, gq_t, gk_t)
    attn = _attn_fwd("attn_fwd", qn, kn, vb, table, weights["last_begin"]((qn,))).reshape(t, ATTN_W)
    mix, oraw, states = _hgrn_fwd("hgrn_fwd", proj, attn, lb, go, b, s)
    x2, h3 = _out_proj("out_proj", mix, w_out, x1, g2)
    wg2, wu2, wd2 = weights["last"]((h3,))
    a2, b2, z2 = _ffn_up("ffn2_up", h3, wg2, wu2)
    dy, dyh, sq = _ffn_down_loss("ffn2_down_loss", z2, wd2, x2, tgt)
    loss = 0.5 * jnp.sum(sq) / d

    da2, db2 = _ffn_bwd_act("ffn2_bwd_act", dyh, wd2, a2, b2)
    dwd2 = _grad_w_shardrows("ffn2_dwd", z2, dyh)
    dwg2 = _grad_w_shardrows("ffn2_dwg", da2, h3)
    dwu2 = _grad_w_shardrows("ffn2_dwu", db2, h3)
    sent2 = on_grads("ffn2", {"ffn2_w_gate": dwg2, "ffn2_w_up": dwu2, "ffn2_w_down": dwd2})
    dx2, dx2b, dg2 = _ffn_bwd_in("ffn2_bwd_in", da2, db2, wg2, wu2, x2, g2, dy, 1.0, sent2)
    sent2 = grads_sent("ffn2", dx2b)

    dwout = _grad_w_out("dw_out", mix, dx2b)
    dmix = _out_proj_bwd("out_proj_bwd", dx2b, w_out, sent2)
    dqn, dkn, dvn, dbe, dbo = _attn_bwd("attn_bwd", qn, kn, vb, table, dmix.reshape(b, s, dmix.shape[1]))
    dbias = dbe[:, :, :BAND] + dbo[:, :, CHUNK:]
    dpq, dpk, dpv, dgq, dgk = _qk_prep_bwd("qk_prep_bwd", proj3, dqn, dkn, dvn, gq_t, gk_t)
    dpq, dpk, dpv = (a.reshape(t, ATTN_W) for a in (dpq, dpk, dpv))
    dproj, dlb, dgo = _hgrn_bwd("hgrn_bwd", proj, (dpq, dpk, dpv), lb, go, oraw, states, dmix, b, s)
    dwin = _grad_w_in("dw_in", h2, dproj, ns)
    dx1, dx1h, dgm = _in_proj_bwd("in_proj_bwd", dproj, w_in, x1, gm, dx2, 0.5)

    dwd1 = _grad_w_shardrows("ffn1_dwd", z1, dx1h)
    sent_mix = on_grads("mix", {"w_in": dwin, "w_out": dwout.reshape(ns, dwout.shape[0] // ns, d),
                                "ffn1_w_down": dwd1})
    da1, db1 = _ffn_bwd_act("ffn1_bwd_act", dx1h, wd1, a1, b1, sent_mix)
    sent_mix = grads_sent("mix", da1)
    dwg1 = _grad_w_shardrows("ffn1_dwg", da1, h1, sent_mix)
    dwu1 = _grad_w_shardrows("ffn1_dwu", db1, h1)
    on_grads("ffn1", {"ffn1_w_gate": dwg1, "ffn1_w_up": dwu1})
    sent1 = grads_sent("ffn1", None)
    dx0, dg1 = _ffn_bwd_in("ffn1_bwd_in", da1, db1, wg1, wu1, x0, g1, dx1, None, sent1)

    nt = dg1.shape[0]
    sg = _small_grads(
        "small_grads", dg1.reshape(nt, d), dgm.reshape(nt, d), dg2.reshape(nt, d),
        dgq.reshape(-1, ATTN_W), dgk.reshape(-1, ATTN_W), dbias.transpose(1, 0, 2),
        dlb.reshape(b, HGRN_W), dgo.reshape(b, HGRN_W), lbp)
    g1g, gmg, g2g, gqg, gkg, rbg, lbg, gog = sg
    small = _pack_small(g1g, gmg, g2g, lbg, rbg[:, :N_REL], gqg, gkg, gog, loss)
    return dx0.reshape(b, s, d), small


LOSS_SLOT = 7 * SMALL_COLS + 2 * ATTN_DH + HGRN_DH


def _pack_small(g1, gm, g2, lbp, rel_bias, gq, gk, go, loss=None):
    flat = [g1.reshape(-1), gm.reshape(-1), g2.reshape(-1), lbp.reshape(-1), rel_bias.reshape(-1)]
    n_bias = 3 * SMALL_COLS - rel_bias.size
    heads = [gq.reshape(-1), gk.reshape(-1), go.reshape(-1)]
    heads.append(jnp.zeros((1,), F32) if loss is None else loss.reshape(1))
    n_tail = SMALL_COLS - sum(h.size for h in heads)
    return jnp.concatenate(flat + [jnp.zeros((n_bias,), F32)] + heads + [jnp.zeros((n_tail,), F32)]).reshape(
        SMALL_ROWS, SMALL_COLS)


def _unpack_small(p, d):
    flat = p.reshape(-1)
    o = 3 * d
    g1, gm, g2 = p[0:1], p[1:2], p[2:3]
    lbp = flat[o:o + 2 * HGRN_W].reshape(2, HGRN_W)
    o = 4 * SMALL_COLS
    rel = flat[o:o + ATTN_HEADS * N_REL].reshape(1, ATTN_HEADS, N_REL)
    o = 7 * SMALL_COLS
    gq = flat[o:o + ATTN_DH].reshape(1, ATTN_DH)
    gk = flat[o + ATTN_DH:o + 2 * ATTN_DH].reshape(1, ATTN_DH)
    go = flat[o + 2 * ATTN_DH:o + 2 * ATTN_DH + HGRN_DH].reshape(1, HGRN_DH)
    return g1, gm, g2, gq, gk, rel, lbp, go


def kernel(x, ffn1_norm_g, ffn1_w_gate, ffn1_w_up, ffn1_w_down, mix_norm_g, w_in, attn_q_norm_g, attn_k_norm_g, attn_rel_bias, hgrn_lower_bounds, hgrn_out_norm_g, w_out, ffn2_norm_g, ffn2_w_gate, ffn2_w_up, ffn2_w_down, loss_target, m_ffn1_norm_g, m_ffn1_w_gate, m_ffn1_w_up, m_ffn1_w_down, m_mix_norm_g, m_w_in, m_attn_q_norm_g, m_attn_k_norm_g, m_attn_rel_bias, m_hgrn_lower_bounds, m_hgrn_out_norm_g, m_w_out, m_ffn2_norm_g, m_ffn2_w_gate, m_ffn2_w_up, m_ffn2_w_down, v_ffn1_norm_g, v_ffn1_w_gate, v_ffn1_w_up, v_ffn1_w_down, v_mix_norm_g, v_w_in, v_attn_q_norm_g, v_attn_k_norm_g, v_attn_rel_bias, v_hgrn_lower_bounds, v_hgrn_out_norm_g, v_w_out, v_ffn2_norm_g, v_ffn2_w_gate, v_ffn2_w_up, v_ffn2_w_down):
    d = x.shape[-1]
    big_w = [ffn1_w_gate, ffn1_w_up, ffn1_w_down, w_in, w_out, ffn2_w_gate, ffn2_w_up, ffn2_w_down]
    big_m = [m_ffn1_w_gate, m_ffn1_w_up, m_ffn1_w_down, m_w_in, m_w_out, m_ffn2_w_gate, m_ffn2_w_up, m_ffn2_w_down]
    big_v = [v_ffn1_w_gate, v_ffn1_w_up, v_ffn1_w_down, v_w_in, v_w_out, v_ffn2_w_gate, v_ffn2_w_up, v_ffn2_w_down]
    big_names = ["ffn1_w_gate", "ffn1_w_up", "ffn1_w_down", "w_in", "w_out", "ffn2_w_gate", "ffn2_w_up", "ffn2_w_down"]
    flipped = {nm for nm in big_names if nm.endswith("gate") or nm.endswith("up")}
    flip = lambda nm, a: jnp.swapaxes(a, 1, 2) if nm in flipped else a
    big_w, big_m, big_v = ([flip(nm, a) for nm, a in zip(big_names, arrs)] for arrs in (big_w, big_m, big_v))

    shards = [w[0].astype(BF16) for w in big_w]
    start_a = _gather_start("gather_start_up1", shards[:2], ())
    start_b = _gather_start("gather_start_mid", shards[2:5], (start_a[4],))
    start_c = _gather_start("gather_start_ffn2", shards[5:], (start_b[4],))

    pending = {}

    def arrived(tag, started, after):
        send_sem, recv_sem, srcs, outs, _ = started
        return _gather_wait("gather_wait_" + tag, send_sem, recv_sem, srcs, outs, after)

    def first_weights(after):
        return (*_gather_join("gather_join_up1", *arrived("up1", start_a, after)), (start_c[4],))

    def mid_weights(after):
        srcs, outs = arrived("mid", start_b, after)
        (wd1,) = _gather_join("gather_join_wd1", srcs[:1], outs[:1])
        pending["mid"] = _join_start("join_start_mid", srcs[1:], outs[1:])
        return wd1, (pending["mid"][3],)

    def mid_rest(after):
        sems, srcs, outs, _ = pending["mid"]
        win_f, wout_f = _join_wait("join_wait_mid", sems, srcs, outs, after)
        return win_f, wout_f.reshape(wout_f.shape[0] * wout_f.shape[1], d)

    def last_begin(after):
        pending["ffn2"] = _join_start("join_start_ffn2", *arrived("ffn2", start_c, after))
        return (pending["ffn2"][3],)

    def last_weights(after):
        sems, srcs, outs, _ = pending["ffn2"]
        return _join_wait("join_wait_ffn2", sems, srcs, outs, after)

    weights = {"first": first_weights, "mid": mid_weights, "mid_rest": mid_rest, "last_begin": last_begin,
               "last": last_weights}

    core = lax.axis_index("c").astype(jnp.int32).reshape(1)
    chip = (2 * lax.axis_index("x") + lax.axis_index("y")).astype(jnp.int32).reshape(1)
    started = {}

    def on_grads(tag, grads):
        names = list(grads)
        started[tag] = (names, _pair_start("pair_start_" + tag, [grads[nm] for nm in names]))
        return (started[tag][1][4],)

    def grads_sent(tag, after):
        names, (send_sem, recv_sem, grads, lands, token) = started[tag]
        grads, theirs = _pair_wait("pair_wait_" + tag, send_sem, recv_sem, grads, lands, token if after is None else after)
        sums = [_pair_sum("pair_sum_" + nm, g, th, core) for nm, g, th in zip(names, grads, theirs)]
        started[tag] = (names, _scatter_start("scatter_start_" + tag, sums))
        return (started[tag][1][4],)

    grad_x, small_g = _local_step(
        x, loss_target, ffn1_norm_g, mix_norm_g, ffn2_norm_g, attn_q_norm_g, attn_k_norm_g, hgrn_out_norm_g,
        attn_rel_bias[0], hgrn_lower_bounds, weights, on_grads, grads_sent)

    def finish(tag, after):
        names, (send_sem, recv_sem, sums, lands, _) = started[tag]
        sums, lands = _scatter_wait("scatter_wait_" + tag, send_sem, recv_sem, sums, lands, after)
        return names, [_chip_sum("chip_sum_" + nm, sm, ld, chip) for nm, sm, ld in zip(names, sums, lands)]

    by_name = {nm: (w, m, v) for nm, w, m, v in zip(big_names, big_w, big_m, big_v)}
    updated = {}

    def update(names, halves, other_halves):
        for nm, mine, theirs in zip(names, halves, other_halves):
            w, m, v = by_name[nm]
            updated[nm] = _adamw("adamw_" + nm, w, mine, theirs, m, v, core)

    last_token = started["ffn1"][1][4]
    names_a, halves_a = finish("ffn2", last_token)
    names_m, halves_m = finish("mix", last_token)
    names_a, halves_a = names_a + names_m, halves_a + halves_m
    update(names_a, halves_a, _pair_join("pair_join_early", halves_a))
    names_b, halves_b = finish("ffn1", updated[names_a[-1]][1])
    others_b, small_all = _pair_join("pair_join_last", halves_b, small_g)
    update(names_b, halves_b, others_b)
    big_out = [updated[nm] for nm in big_names]

    pack = lambda g1, gm, g2, gq, gk, rel, lbp, go: _pack_small(g1, gm, g2, lbp, rel[0], gq, gk, go)
    small_w = pack(ffn1_norm_g, mix_norm_g, ffn2_norm_g, attn_q_norm_g, attn_k_norm_g, attn_rel_bias, hgrn_lower_bounds, hgrn_out_norm_g)
    small_m = pack(m_ffn1_norm_g, m_mix_norm_g, m_ffn2_norm_g, m_attn_q_norm_g, m_attn_k_norm_g, m_attn_rel_bias, m_hgrn_lower_bounds, m_hgrn_out_norm_g)
    small_v = pack(v_ffn1_norm_g, v_mix_norm_g, v_ffn2_norm_g, v_attn_q_norm_g, v_attn_k_norm_g, v_attn_rel_bias, v_hgrn_lower_bounds, v_hgrn_out_norm_g)
    small_res = _adamw_small("adamw_small", small_w, small_all, small_m, small_v)
    small_out = [_unpack_small(p, d) for p in small_res]
    loss = small_res[0].reshape(-1)[LOSS_SLOT]

    def assemble(kind):
        bg = [flip(nm, o[kind]) for nm, o in zip(big_names, big_out)]
        g1, gm, g2, gq, gk, rel, lbp, go = small_out[kind]
        return [g1, bg[0], bg[1], bg[2], gm, bg[3], gq, gk, rel, lbp, go, bg[4], g2, bg[5], bg[6], bg[7]]

    return (loss, grad_x, *assemble(0), *assemble(1), *assemble(2), *assemble(3))
```

```python
import functools

import jax
import jax.numpy as jnp
from jax import lax
from jax.experimental import pallas as pl
from jax.experimental.pallas import tpu as pltpu

F32 = jnp.float32
BF16 = jnp.bfloat16
MESH = pl.DeviceIdType.MESH

N_CHIPS = 4
N_DEV = 8
CHUNK = 64
ATTN_HEADS = 8
ATTN_DH = 64
ATTN_W = ATTN_HEADS * ATTN_DH
HGRN_HEADS = 4
HGRN_DH = 128
HGRN_W = HGRN_HEADS * HGRN_DH
LEFT_CHUNKS = 8
BAND = (LEFT_CHUNKS + 1) * CHUNK
KPAD = LEFT_CHUNKS * CHUNK
REL_CLIP = 128
N_REL = 2 * REL_CLIP + 1
N_REL_PAD = 384
RMS_EPS = 1e-6
LANES = 128
SMALL_ROWS = 8
SMALL_COLS = 1024

ADAM_LR = 0.001
ADAM_B1 = 0.9
ADAM_B2 = 0.999
ADAM_EPS = 1e-08
ADAM_WD = 0.01
ADAM_STEP = 10

NN = (((1,), (0,)), ((), ()))
NT = (((1,), (1,)), ((), ()))
TN = (((0,), (0,)), ((), ()))

VMEM_LIMIT = 48 * 1024 * 1024


def _sigmoid(x):
    return 1.0 / (1.0 + jnp.exp(-x))


def _silu(x):
    return x * _sigmoid(x)


def _dot(a, b, dims=NN):
    return lax.dot_general(a, b, dims, preferred_element_type=F32)


def _split3(x):
    hi = x.astype(BF16)
    r1 = x - hi.astype(F32)
    mid = r1.astype(BF16)
    lo = (r1 - mid.astype(F32)).astype(BF16)
    return hi, mid, lo


def _dot_exact_rhs(x, mat, dims=NN, pieces=3):
    hi, mid, lo = _split3(x)
    out = _dot(hi, mat, dims) + _dot(mid, mat, dims)
    return out + _dot(lo, mat, dims) if pieces == 3 else out


def _dot_exact_lhs(mat, x, dims=NN):
    hi, mid, lo = _split3(x)
    return _dot(mat, hi, dims) + _dot(mat, mid, dims) + _dot(mat, lo, dims)


def _params(*sem):
    return pltpu.CompilerParams(dimension_semantics=sem, vmem_limit_bytes=VMEM_LIMIT)


def _mm(name, ins, terms, n_acc, grid, acc_shape, outs, epilogue, extras=(), deps=()):
    nk = grid[2]
    ni, ne, nd, no = len(ins), len(extras), len(deps), len(outs)

    def body(*refs):
        in_refs = refs[:ni]
        ex_refs = refs[ni:ni + ne]
        out_refs = refs[ni + ne + nd:ni + ne + nd + no]
        acc_refs = refs[ni + ne + nd + no:]
        parts = [None] * n_acc
        for ai, li, ri, dims in terms:
            d = _dot(in_refs[li][...], in_refs[ri][...], dims)
            parts[ai] = d if parts[ai] is None else parts[ai] + d

        def finish(accs):
            res = epilogue(accs, [e[...] for e in ex_refs])
            for o, r in zip(out_refs, res):
                o[...] = r.astype(o.dtype)

        if nk == 1:
            finish(parts)
        else:
            k = pl.program_id(2)

            @pl.when(k == 0)
            def _():
                for a, p in zip(acc_refs, parts):
                    a[...] = p

            @pl.when(k > 0)
            def _():
                for a, p in zip(acc_refs, parts):
                    a[...] += p

            @pl.when(k == nk - 1)
            def _():
                finish([a[...] for a in acc_refs])

    scratch = [] if nk == 1 else [pltpu.VMEM(acc_shape, F32) for _ in range(n_acc)]
    res = pl.pallas_call(
        body,
        name=name,
        grid=grid,
        in_specs=[s for _, s in ins] + [s for _, s in extras] + [pl.BlockSpec(memory_space=pl.ANY)] * nd,
        out_specs=[s for _, s in outs],
        out_shape=[o for o, _ in outs],
        scratch_shapes=scratch,
        compiler_params=_params("parallel", "parallel", "arbitrary"),
    )(*[a for a, _ in ins], *[a for a, _ in extras], *deps)
    return res


def _mm_rows(name, lhs, weights, dims, t, outs, epilogue, extras=(), deps=()):
    tm = _row_tile(t)
    nl, ne, nd, no = len(lhs), len(extras), len(deps), len(outs)
    ns = weights[0].shape[0]

    def body(*refs):
        lhs_refs = refs[:nl]
        w_hbm = refs[nl:2 * nl]
        ex_refs = refs[2 * nl:2 * nl + ne]
        out_refs = refs[2 * nl + ne + nd:2 * nl + ne + nd + no]
        w_vmem = refs[2 * nl + ne + nd + no:3 * nl + ne + nd + no]
        sem = refs[-1]

        @pl.when(pl.program_id(0) == 0)
        def _():
            copies = [pltpu.make_async_copy(w_hbm[p], w_vmem[p], sem.at[p]) for p in range(nl)]
            for cp in copies:
                cp.start()
            for cp in copies:
                cp.wait()

        acc = None
        for p in range(nl):
            pick = lhs[p][2]
            for j in range(ns):
                part = _dot(pick(lhs_refs[p], j), w_vmem[p][j], dims)
                acc = part if acc is None else acc + part
        res = epilogue([acc], [e[...] for e in ex_refs])
        for o, r in zip(out_refs, res):
            o[...] = r.astype(o.dtype)

    return pl.pallas_call(
        body,
        name=name,
        grid=(t // tm,),
        in_specs=[s for _, s, _ in lhs] + [pl.BlockSpec(memory_space=pl.ANY)] * nl + [s for _, s in extras]
        + [pl.BlockSpec(memory_space=pl.ANY)] * nd,
        out_specs=[s for _, s in outs],
        out_shape=[o for o, _ in outs],
        scratch_shapes=[pltpu.VMEM(w.shape, w.dtype) for w in weights] + [pltpu.SemaphoreType.DMA((nl,))],
        compiler_params=_params("arbitrary"),
    )(*[a for a, _, _ in lhs], *weights, *[a for a, _ in extras], *deps)


def _mm_shards(name, x, weights, dims, outs, epilogue, extras=(), deps=()):
    t = x.shape[0]
    tm = _row_tile(t)
    nw, ne, nd, no = len(weights), len(extras), len(deps), len(outs)
    ns = weights[0].shape[0]

    def body(*refs):
        x_ref = refs[0]
        w_hbm = refs[1:1 + nw]
        ex_refs = refs[1 + nw:1 + nw + ne]
        out_refs = refs[1 + nw + ne + nd:1 + nw + ne + nd + no]
        w_vmem = refs[1 + nw + ne + nd + no:1 + 2 * nw + ne + nd + no]
        sem = refs[-1]

        @pl.when(pl.program_id(0) == 0)
        def _():
            copies = [pltpu.make_async_copy(w_hbm[p], w_vmem[p], sem.at[p]) for p in range(nw)]
            for cp in copies:
                cp.start()
            for cp in copies:
                cp.wait()

        xv = x_ref[...]
        accs = [_dot(xv, w_vmem[p][0], dims) for p in range(nw)]
        for j in range(ns):
            nxt = [_dot(xv, w_vmem[p][j + 1], dims) for p in range(nw)] if j + 1 < ns else None
            res = epilogue(accs, [e[j] for e in ex_refs])
            for (_, _, store), o, r in zip(outs, out_refs, res):
                store(o, j, r.astype(o.dtype))
            accs = nxt

    return pl.pallas_call(
        body,
        name=name,
        grid=(t // tm,),
        in_specs=[pl.BlockSpec((tm, x.shape[1]), lambda i: (i, 0))] + [pl.BlockSpec(memory_space=pl.ANY)] * nw
        + [s for _, s in extras] + [pl.BlockSpec(memory_space=pl.ANY)] * nd,
        out_specs=[s for _, s, _ in outs],
        out_shape=[o for o, _, _ in outs],
        scratch_shapes=[pltpu.VMEM(w.shape, w.dtype) for w in weights] + [pltpu.SemaphoreType.DMA((nw,))],
        compiler_params=_params("arbitrary"),
    )(x, *weights, *[a for a, _ in extras], *deps)


def _store_shard(ref, j, value):
    ref[j] = value


def _row_tile(t):
    return 512 if t % 512 == 0 else t


def _k_tile(t):
    return t if t <= 4096 else 1024


def _rmsnorm(xv, g):
    ms = jnp.mean(xv * xv, axis=-1, keepdims=True)
    return xv * lax.rsqrt(ms + RMS_EPS) * g


def _rmsnorm_fwd(name, x, g):
    t, d = x.shape
    tm = _row_tile(t)

    def body(x_ref, g_ref, h_ref):
        h_ref[...] = _rmsnorm(x_ref[...], g_ref[...]).astype(BF16)

    return pl.pallas_call(
        body,
        name=name,
        grid=(t // tm,),
        in_specs=[pl.BlockSpec((tm, d), lambda i: (i, 0)), pl.BlockSpec((1, d), lambda i: (0, 0))],
        out_specs=pl.BlockSpec((tm, d), lambda i: (i, 0)),
        out_shape=jax.ShapeDtypeStruct((t, d), BF16),
        compiler_params=_params("parallel"),
    )(x, g)


def _norm_bwd_epilogue(copy_scale):
    def epilogue(accs, ex):
        dh = accs[0]
        xv, g, dres = ex
        ms = jnp.mean(xv * xv, axis=-1, keepdims=True)
        rstd = lax.rsqrt(ms + RMS_EPS)
        xhat = xv * rstd
        dxhat = dh * g
        dx = rstd * (dxhat - xhat * jnp.mean(dxhat * xhat, axis=-1, keepdims=True))
        out = dres + dx
        dg = jnp.sum(dh * xhat, axis=0, keepdims=True)
        if copy_scale is None:
            return out, dg
        return out, out * copy_scale, dg

    return epilogue


def _ffn_up(name, h, wg, wu, deps=()):
    t, d = h.shape
    ns, f, _ = wg.shape
    tm = _row_tile(t)

    def epilogue(accs, ex):
        a, b = accs
        sg = _sigmoid(a)
        act = a * sg
        return act, b * (sg * (1.0 + a * (1.0 - sg))), act * b

    out = (jax.ShapeDtypeStruct((ns, t, f), BF16), pl.BlockSpec((ns, tm, f), lambda i: (0, i, 0)), _store_shard)
    return _mm_shards(name, h, [wg, wu], NT, [out] * 3, epilogue, deps=deps)


def _shard_rows(arr, tm):
    ns, _, f = arr.shape
    return arr, pl.BlockSpec((ns, tm, f), lambda i: (0, i, 0)), lambda ref, j: ref[j]


def _ffn_down(name, z, wd, x, g_next, deps=()):
    _, t, _ = z.shape
    d = wd.shape[2]
    tm = _row_tile(t)
    row = pl.BlockSpec((tm, d), lambda i: (i, 0))

    def epilogue(accs, ex):
        y = ex[0] + 0.5 * accs[0]
        return y, _rmsnorm(y, ex[1])

    return _mm_rows(
        name, [_shard_rows(z, tm)], [wd], NN, t,
        outs=[(jax.ShapeDtypeStruct((t, d), F32), row), (jax.ShapeDtypeStruct((t, d), BF16), row)],
        epilogue=epilogue,
        extras=[(x, row), (g_next, pl.BlockSpec((1, d), lambda i: (0, 0)))],
        deps=deps,
    )


def _ffn_down_loss(name, z, wd, x, target):
    _, t, _ = z.shape
    d = wd.shape[2]
    tm = _row_tile(t)
    nt = t // tm
    row = pl.BlockSpec((tm, d), lambda i: (i, 0))

    def epilogue(accs, ex):
        e = ex[0] + 0.5 * accs[0] - ex[1]
        dy = e * (1.0 / d)
        return dy, 0.5 * dy, jnp.sum(e * e, axis=0, keepdims=True)

    return _mm_rows(
        name, [_shard_rows(z, tm)], [wd], NN, t,
        outs=[(jax.ShapeDtypeStruct((t, d), F32), row), (jax.ShapeDtypeStruct((t, d), BF16), row),
              (jax.ShapeDtypeStruct((nt, 1, d), F32), pl.BlockSpec((None, 1, d), lambda i: (i, 0, 0)))],
        epilogue=epilogue,
        extras=[(x, row), (target, row)],
    )


def _ffn_bwd_act(name, dout, wd, act_a, dact_b, deps=()):
    t, d = dout.shape
    ns, f, _ = wd.shape
    tm = _row_tile(t)

    def epilogue(accs, ex):
        dz = accs[0]
        return dz * ex[1].astype(F32), dz * ex[0].astype(F32)

    act = pl.BlockSpec((ns, tm, f), lambda i: (0, i, 0))
    out = (jax.ShapeDtypeStruct((ns, t, f), BF16), act, _store_shard)
    return _mm_shards(name, dout, [wd], NT, [out] * 2, epilogue, extras=[(act_a, act), (dact_b, act)], deps=deps)


def _grad_w_shardrows(name, z, dout, deps=()):
    ns, t, f = z.shape
    d = dout.shape[1]
    tk = _k_tile(t)
    return _mm(
        name,
        ins=[(z, pl.BlockSpec((None, tk, f), lambda j, n, k: (j, k, 0))),
             (dout, pl.BlockSpec((tk, d), lambda j, n, k: (k, 0)))],
        terms=[(0, 0, 1, TN)],
        n_acc=1,
        grid=(ns, 1, t // tk),
        acc_shape=(f, d),
        outs=[(pltpu.HBM((ns, f, d), BF16), pl.BlockSpec((None, f, d), lambda j, n, k: (j, 0, 0)))],
        epilogue=lambda accs, ex: (accs[0],),
        deps=deps,
    )[0]


def _norm_bwd_outs(t, d, tm, copy_scale):
    row = pl.BlockSpec((tm, d), lambda i: (i, 0))
    outs = [(jax.ShapeDtypeStruct((t, d), F32), row)]
    if copy_scale is not None:
        outs.append((jax.ShapeDtypeStruct((t, d), BF16), row))
    outs.append((jax.ShapeDtypeStruct((t // tm, 1, d), F32), pl.BlockSpec((None, 1, d), lambda i: (i, 0, 0))))
    return row, outs


def _ffn_bwd_in(name, da, db, wg, wu, x, g, dres, copy_scale, deps=()):
    _, t, _ = da.shape
    d = wg.shape[2]
    tm = _row_tile(t)
    row, outs = _norm_bwd_outs(t, d, tm, copy_scale)
    return _mm_rows(
        name, [_shard_rows(da, tm), _shard_rows(db, tm)], [wg, wu], NN, t,
        outs=outs,
        epilogue=_norm_bwd_epilogue(copy_scale),
        extras=[(x, row), (g, pl.BlockSpec((1, d), lambda i: (0, 0))), (dres, row)],
        deps=deps,
    )


def _in_proj(name, h, w_in):
    t, d = h.shape
    ns, _, pj = w_in.shape
    tm = _row_tile(t)
    def store(ref, j, value):
        ref[:, j * pj:(j + 1) * pj] = value

    out = (jax.ShapeDtypeStruct((t, ns * pj), F32), pl.BlockSpec((tm, ns * pj), lambda i: (i, 0)), store)
    return _mm_shards(name, h, [w_in], NN, [out], lambda accs, ex: (accs[0],))[0]


def _in_proj_bwd(name, dp, w_in, x, g, dres, copy_scale, deps=()):
    t = dp.shape[0]
    ns, d, pj = w_in.shape
    tm = _row_tile(t)
    row, outs = _norm_bwd_outs(t, d, tm, copy_scale)
    cols = (dp, pl.BlockSpec((tm, ns * pj), lambda i: (i, 0)), lambda ref, j: ref[:, j * pj:(j + 1) * pj])
    return _mm_rows(
        name, [cols], [w_in], NT, t,
        outs=outs,
        epilogue=_norm_bwd_epilogue(copy_scale),
        extras=[(x, row), (g, pl.BlockSpec((1, d), lambda i: (0, 0))), (dres, row)],
        deps=deps,
    )


def _grad_w_in(name, h, dp, ns):
    t, d = h.shape
    pj = dp.shape[1] // ns
    tk = _k_tile(t)
    return _mm(
        name,
        ins=[(h, pl.BlockSpec((tk, d), lambda j, n, k: (k, 0))),
             (dp, pl.BlockSpec((tk, pj), lambda j, n, k: (k, j)))],
        terms=[(0, 0, 1, TN)],
        n_acc=1,
        grid=(ns, 1, t // tk),
        acc_shape=(d, pj),
        outs=[(pltpu.HBM((ns, d, pj), BF16), pl.BlockSpec((None, d, pj), lambda j, n, k: (j, 0, 0)))],
        epilogue=lambda accs, ex: (accs[0],),
    )[0]


def _out_proj(name, mix, w_out, x, g_next):
    t, dm = mix.shape
    d = w_out.shape[1]
    tm = _row_tile(t)
    row = pl.BlockSpec((tm, d), lambda i, n, k: (i, 0))
    return _mm(
        name,
        ins=[(mix, pl.BlockSpec((tm, dm), lambda i, n, k: (i, 0))),
             (w_out, pl.BlockSpec((dm, d), lambda i, n, k: (0, 0)))],
        terms=[(0, 0, 1, NN)],
        n_acc=1,
        grid=(t // tm, 1, 1),
        acc_shape=(tm, d),
        outs=[(jax.ShapeDtypeStruct((t, d), F32), row), (jax.ShapeDtypeStruct((t, d), BF16), row)],
        epilogue=lambda accs, ex: (ex[0] + accs[0], _rmsnorm(ex[0] + accs[0], ex[1])),
        extras=[(x, row), (g_next, pl.BlockSpec((1, d), lambda i, n, k: (0, 0)))],
    )


def _out_proj_bwd(name, dx, w_out, deps=()):
    t, d = dx.shape
    dm = w_out.shape[0]
    tm = _row_tile(t)
    return _mm(
        name,
        ins=[(dx, pl.BlockSpec((tm, d), lambda i, n, k: (i, 0))),
             (w_out, pl.BlockSpec((dm, d), lambda i, n, k: (0, 0)))],
        terms=[(0, 0, 1, NT)],
        n_acc=1,
        grid=(t // tm, 1, 1),
        acc_shape=(tm, dm),
        outs=[(jax.ShapeDtypeStruct((t, dm), F32), pl.BlockSpec((tm, dm), lambda i, n, k: (i, 0)))],
        epilogue=lambda accs, ex: (accs[0],),
        deps=deps,
    )[0]


def _grad_w_out(name, mix, dx):
    t, dm = mix.shape
    d = dx.shape[1]
    tk = _k_tile(t)
    return _mm(
        name,
        ins=[(mix, pl.BlockSpec((tk, dm), lambda a, n, k: (k, 0))),
             (dx, pl.BlockSpec((tk, d), lambda a, n, k: (k, 0)))],
        terms=[(0, 0, 1, TN)],
        n_acc=1,
        grid=(1, 1, t // tk),
        acc_shape=(dm, d),
        outs=[(pltpu.HBM((dm, d), BF16), pl.BlockSpec((dm, d), lambda a, n, k: (0, 0)))],
        epilogue=lambda accs, ex: (accs[0],),
    )[0]


def _head_group_matrix():
    r = lax.broadcasted_iota(jnp.int32, (ATTN_W, ATTN_W), 0)
    c = lax.broadcasted_iota(jnp.int32, (ATTN_W, ATTN_W), 1)
    same = jnp.right_shift(r, 6) == jnp.right_shift(c, 6)
    return jnp.where(same, 1.0, 0.0).astype(BF16)


def _qk_prep(name, proj, gq, gk):
    b, s, _ = proj.shape
    tm = KPAD
    nb = s // tm

    def body(q_ref, k_ref, v_ref, gq_ref, gk_ref, qn_ref, kn_ref, vb_ref):
        j = pl.program_id(1)
        bd = _head_group_matrix()

        def norm(xv, g):
            ms = _dot_exact_rhs(xv * xv, bd, pieces=2) * (1.0 / ATTN_DH)
            return xv * lax.rsqrt(ms + RMS_EPS) * g

        @pl.when(j == 0)
        def _():
            kn_ref[...] = jnp.zeros_like(kn_ref)
            vb_ref[...] = jnp.zeros_like(vb_ref)

        @pl.when(j > 0)
        def _():
            qn_ref[...] = (norm(q_ref[...], gq_ref[...]) * (ATTN_DH ** -0.5)).astype(BF16)
            kn_ref[...] = norm(k_ref[...], gk_ref[...]).astype(BF16)
            vb_ref[...] = v_ref[...].astype(BF16)

    src_blk = lambda col: pl.BlockSpec((None, tm, ATTN_W), lambda bi, j: (bi, jnp.maximum(j - 1, 0), col))
    gspec = pl.BlockSpec((1, ATTN_W), lambda bi, j: (0, 0))
    padded = pl.BlockSpec((None, tm, ATTN_W), lambda bi, j: (bi, j, 0))
    return pl.pallas_call(
        body,
        name=name,
        grid=(b, nb + 1),
        in_specs=[src_blk(0), src_blk(1), src_blk(2), gspec, gspec],
        out_specs=[src_blk(0), padded, padded],
        out_shape=[jax.ShapeDtypeStruct((b, s, ATTN_W), BF16), jax.ShapeDtypeStruct((b, KPAD + s, ATTN_W), BF16),
                   jax.ShapeDtypeStruct((b, KPAD + s, ATTN_W), BF16)],
        compiler_params=_params("parallel", "arbitrary"),
    )(proj, proj, proj, gq, gk)


def _qk_prep_bwd(name, proj, dqn, dkn, dv, gq, gk):
    b, s, _ = proj.shape
    tm = KPAD
    nb = s // tm

    def body(q_ref, k_ref, dqn_ref, dkn_ref, dv_ref, gq_ref, gk_ref, dq_ref, dk_ref, dvb_ref, dgq_ref, dgk_ref):
        bd = _head_group_matrix()

        def bwd(xv, dy, g):
            ms = _dot_exact_rhs(xv * xv, bd, pieces=2) * (1.0 / ATTN_DH)
            rstd = lax.rsqrt(ms + RMS_EPS)
            xhat = xv * rstd
            dxhat = dy * g
            gm = _dot_exact_rhs(dxhat * xhat, bd, pieces=2) * (1.0 / ATTN_DH)
            return rstd * (dxhat - xhat * gm), jnp.sum(dy * xhat, axis=0, keepdims=True)

        dq, dgq = bwd(q_ref[...], dqn_ref[...], gq_ref[...])
        dk, dgk = bwd(k_ref[...], dkn_ref[...], gk_ref[...])
        dq_ref[...] = dq.astype(BF16)
        dk_ref[...] = dk.astype(BF16)
        dvb_ref[...] = dv_ref[...].astype(BF16)
        dgq_ref[...] = dgq
        dgk_ref[...] = dgk

    col = lambda c: pl.BlockSpec((None, tm, ATTN_W), lambda bi, j: (bi, j, c))
    past_pad = pl.BlockSpec((None, tm, ATTN_W), lambda bi, j: (bi, j + 1, 0))
    gspec = pl.BlockSpec((1, ATTN_W), lambda bi, j: (0, 0))
    pspec = pl.BlockSpec((None, 1, ATTN_W), lambda bi, j: (bi * nb + j, 0, 0))
    o_shape = jax.ShapeDtypeStruct((b, s, ATTN_W), BF16)
    p_shape = jax.ShapeDtypeStruct((b * nb, 1, ATTN_W), F32)
    return pl.pallas_call(
        body,
        name=name,
        grid=(b, nb),
        in_specs=[col(0), col(1), col(0), past_pad, past_pad, gspec, gspec],
        out_specs=[col(0)] * 3 + [pspec] * 2,
        out_shape=[o_shape] * 3 + [p_shape] * 2,
        compiler_params=_params("parallel", "parallel"),
    )(proj, proj, dqn, dkn, dv, gq, gk)


Q_CHUNKS = 4
QBLK = Q_CHUNKS * CHUNK
WIN = (LEFT_CHUNKS + Q_CHUNKS) * CHUNK
DB_W = BAND + CHUNK
MASKED = -1e30


N_TABLES = KPAD // QBLK + 1


def _band_table(bias):
    rows = [jnp.pad(bias, ((0, 0), (0, 0), (CHUNK * i, WIN - BAND - CHUNK * i)), constant_values=MASKED)
            for i in range(Q_CHUNKS)]
    table = jnp.concatenate(rows, axis=1)
    col = jnp.arange(WIN)
    return jnp.stack([jnp.where(col + QBLK * v >= KPAD, table, MASKED) for v in range(N_TABLES)])


def _head_lanes(hh):
    lane = lax.broadcasted_iota(jnp.int32, (1, LANES), 1)
    return (lane < ATTN_DH) if hh == 0 else (lane >= ATTN_DH)


def _attn_probs(qh, kw, table):
    s = _dot(qh, kw, NT) + table
    m = jnp.max(s, axis=-1, keepdims=True)
    p = jnp.exp(s - m)
    return p * (1.0 / jnp.sum(p, axis=-1, keepdims=True))


def _attn_fwd(name, q, k, v, table, deps=()):
    b, s, w = q.shape
    sp = k.shape[1]

    def body(q_ref, k_ref, v_ref, t_ref, *rest):
        o_ref = rest[-1]
        start = pl.multiple_of(pl.program_id(2) * QBLK, QBLK)
        kw = k_ref[pl.ds(start, WIN), :]
        vw = v_ref[pl.ds(start, WIN), :]
        q2 = q_ref[...]
        lanes = [_head_lanes(hh) for hh in range(2)]
        probs = [_attn_probs(jnp.where(mine, q2, jnp.zeros_like(q2)), kw, t_ref[hh]).astype(BF16)
                 for hh, mine in enumerate(lanes)]
        outs = [_dot(p, vw) for p in probs]
        o_ref[...] = jnp.where(lanes[0], outs[0], outs[1]).astype(BF16)

    qspec = pl.BlockSpec((None, QBLK, LANES), lambda p, bi, i: (bi, i, p))
    kspec = pl.BlockSpec((None, sp, LANES), lambda p, bi, i: (bi, 0, p))
    tspec = pl.BlockSpec((None, 2, QBLK, WIN), lambda p, bi, i: (jnp.minimum(i, N_TABLES - 1), p, 0, 0))
    return pl.pallas_call(
        body,
        name=name,
        grid=(w // LANES, b, s // QBLK),
        in_specs=[qspec, kspec, kspec, tspec] + [ANY] * len(deps),
        out_specs=qspec,
        out_shape=jax.ShapeDtypeStruct((b, s, w), BF16),
        compiler_params=_params("parallel", "parallel", "arbitrary"),
    )(q, k, v, table, *deps)


def _attn_bwd(name, q, k, v, table, dmix):
    b, s, w = q.shape
    sp = k.shape[1]

    def body(q_ref, k_ref, v_ref, t_ref, do_ref, dq_ref, dk_ref, dv_ref, dbe_ref, dbo_ref):
        bi = pl.program_id(1)
        i = pl.program_id(2)
        start = pl.multiple_of(i * QBLK, QBLK)
        win = pl.ds(start, WIN)

        @pl.when(i == 0)
        def _():
            dk_ref[...] = jnp.zeros_like(dk_ref)
            dv_ref[...] = jnp.zeros_like(dv_ref)

        @pl.when(jnp.logical_and(i == 0, bi == 0))
        def _():
            dbe_ref[...] = jnp.zeros_like(dbe_ref)
            dbo_ref[...] = jnp.zeros_like(dbo_ref)

        kw = k_ref[win, :]
        vw = v_ref[win, :]
        q2 = q_ref[...]
        do2 = do_ref[...].astype(BF16)
        lanes = [_head_lanes(hh) for hh in range(2)]
        qh = [jnp.where(mine, q2, jnp.zeros_like(q2)) for mine in lanes]
        doh = [jnp.where(mine, do2, jnp.zeros_like(do2)) for mine in lanes]
        p = [_attn_probs(qh[hh], kw, t_ref[hh]) for hh in range(2)]
        dp = [_dot(doh[hh], vw, NT) for hh in range(2)]
        ds = [p[hh] * (dp[hh] - jnp.sum(p[hh] * dp[hh], axis=-1, keepdims=True)) for hh in range(2)]
        dsb = [x.astype(BF16) for x in ds]
        pb = [x.astype(BF16) for x in p]
        dq = [_dot(dsb[hh], kw) * (ATTN_DH ** -0.5) for hh in range(2)]
        dk = [_dot(dsb[hh], qh[hh], TN) for hh in range(2)]
        dv = [_dot(pb[hh], doh[hh], TN) for hh in range(2)]
        for hh in range(2):
            for qi in range(Q_CHUNKS):
                c0 = (qi // 2) * LANES
                blk = ds[hh][qi * CHUNK:(qi + 1) * CHUNK, c0:c0 + DB_W]
                if qi % 2 == 0:
                    dbe_ref[hh] += blk
                else:
                    dbo_ref[hh] += blk
        dq_ref[...] = jnp.where(lanes[0], dq[0], dq[1])
        dk_ref[win, :] += dk[0] + dk[1]
        dv_ref[win, :] += dv[0] + dv[1]

    qspec = pl.BlockSpec((None, QBLK, LANES), lambda p, bi, i: (bi, i, p))
    kspec = pl.BlockSpec((None, sp, LANES), lambda p, bi, i: (bi, 0, p))
    tspec = pl.BlockSpec((None, 2, QBLK, WIN), lambda p, bi, i: (jnp.minimum(i, N_TABLES - 1), p, 0, 0))
    dbspec = pl.BlockSpec((2, CHUNK, DB_W), lambda p, bi, i: (p, 0, 0))
    db_shape = jax.ShapeDtypeStruct((ATTN_HEADS, CHUNK, DB_W), F32)
    return pl.pallas_call(
        body,
        name=name,
        grid=(w // LANES, b, s // QBLK),
        in_specs=[qspec, kspec, kspec, tspec, qspec],
        out_specs=[qspec, kspec, kspec, dbspec, dbspec],
        out_shape=[jax.ShapeDtypeStruct((b, s, w), F32), jax.ShapeDtypeStruct((b, sp, w), F32),
                   jax.ShapeDtypeStruct((b, sp, w), F32), db_shape, db_shape],
        compiler_params=_params("arbitrary", "arbitrary", "arbitrary"),
    )(q, k, v, table, dmix)


HQ_COL = 3 * ATTN_W // HGRN_DH
HF_COL = HQ_COL + HGRN_HEADS
HI_COL = HF_COL + HGRN_HEADS
HG_COL = HI_COL + HGRN_HEADS
HGRN_ROWS = 8 * CHUNK
HEAD_LANES = [slice(hh * HGRN_DH, (hh + 1) * HGRN_DH) for hh in range(HGRN_HEADS)]


def _tri(lower):
    r = lax.broadcasted_iota(jnp.int32, (CHUNK, CHUNK), 0)
    c = lax.broadcasted_iota(jnp.int32, (CHUNK, CHUNK), 1)
    return (r >= c) if lower else (r <= c)


def _hgrn_chunk(hq, hf, lb, tril):
    sig = _sigmoid(hf)
    f = lb + (1.0 - lb) * sig
    g = jnp.log(f)
    ones_l = jnp.where(tril, 1.0, 0.0).astype(BF16)
    b = _dot_exact_lhs(ones_l, g)
    bl = jnp.sum(g, axis=0, keepdims=True)
    rows = lax.broadcasted_iota(jnp.int32, g.shape, 0)
    bm = jnp.sum(jnp.where(rows <= CHUNK // 2, g, 0.0), axis=0, keepdims=True)
    sq = _sigmoid(hq)
    q = hq * sq
    k = 1.0 - f
    return sig, f, b, bl, bm, sq, q, k


def _hgrn_fwd(name, proj, attn, lb, go, b, s):
    nc = s // CHUNK
    t = b * s
    nblk = s // HGRN_ROWS
    cpb = HGRN_ROWS // CHUNK

    def body(hq_ref, hf_ref, hi_ref, hg_ref, attn_ref, lb_ref, go_ref, mix_ref, oraw_ref, st_ref, s_scr):
        tril = _tri(True)
        gov = go_ref[...]
        mix_ref[:, 0:ATTN_W] = attn_ref[...]

        @pl.when(pl.program_id(1) == 0)
        def _():
            s_scr[...] = jnp.zeros_like(s_scr)

        def step(c, carry):
            sl = pl.ds(pl.multiple_of(c * CHUNK, CHUNK), CHUNK)
            hg = hg_ref[sl, :]
            _, _, bb, bl, bm, _, q, k = _hgrn_chunk(hq_ref[sl, :], hf_ref[sl, :], lb_ref[...], tril)
            vb = hi_ref[sl, :].astype(BF16)
            qe = (q * jnp.exp(bb - bm)).astype(BF16)
            ke = (k * jnp.exp(bm - bb)).astype(BF16)
            qb = (q * jnp.exp(bb)).astype(BF16)
            kb = (k * jnp.exp(bl - bb)).astype(BF16)
            e_last = jnp.exp(bl)
            gate = _silu(hg)
            st = [s_scr[hh] for hh in range(HGRN_HEADS)]
            a = [jnp.where(tril, _dot(qe[:, hs], ke[:, hs], NT), 0.0).astype(BF16) for hs in HEAD_LANES]
            o_state = [_dot(qb[:, hs], st[hh].astype(BF16), NT) for hh, hs in enumerate(HEAD_LANES)]
            st_next = [st[hh] * e_last[:, hs] + _dot(vb[:, hs], kb[:, hs], TN) for hh, hs in enumerate(HEAD_LANES)]
            o = [_dot(a[hh], vb[:, hs]) + o_state[hh] for hh, hs in enumerate(HEAD_LANES)]
            ro = [(oh * lax.rsqrt(jnp.mean(oh * oh, axis=-1, keepdims=True) + RMS_EPS) * gov) * gate[:, hs]
                  for oh, hs in zip(o, HEAD_LANES)]
            for hh in range(HGRN_HEADS):
                st_ref[hh, c] = st[hh]
                s_scr[hh] = st_next[hh]
            mix_ref[sl, ATTN_W:ATTN_W + HGRN_W] = jnp.concatenate(ro, axis=1).astype(BF16)
            oraw_ref[sl, :] = jnp.concatenate(o, axis=1)
            return carry

        lax.fori_loop(0, cpb, step, 0)

    col = lambda base: pl.BlockSpec((HGRN_ROWS, HGRN_W), lambda bi, i: (bi * nblk + i, base // HGRN_HEADS))
    out = pl.BlockSpec((HGRN_ROWS, HGRN_W), lambda bi, i: (bi * nblk + i, 0))
    return pl.pallas_call(
        body,
        name=name,
        grid=(b, nblk),
        in_specs=[col(HQ_COL), col(HF_COL), col(HI_COL), col(HG_COL), out,
                  pl.BlockSpec((1, HGRN_W), lambda bi, i: (0, 0)), pl.BlockSpec((1, HGRN_DH), lambda bi, i: (0, 0))],
        out_specs=[pl.BlockSpec((HGRN_ROWS, ATTN_W + HGRN_W), lambda bi, i: (bi * nblk + i, 0)), out,
                   pl.BlockSpec((None, HGRN_HEADS, cpb, HGRN_DH, HGRN_DH), lambda bi, i: (bi, 0, i, 0, 0))],
        out_shape=[jax.ShapeDtypeStruct((t, ATTN_W + HGRN_W), BF16), jax.ShapeDtypeStruct((t, HGRN_W), F32),
                   jax.ShapeDtypeStruct((b, HGRN_HEADS, nc, HGRN_DH, HGRN_DH), F32)],
        scratch_shapes=[pltpu.VMEM((HGRN_HEADS, HGRN_DH, HGRN_DH), F32)],
        compiler_params=_params("parallel", "arbitrary"),
    )(proj, proj, proj, proj, attn, lb, go)


def _hgrn_bwd(name, proj, dqkv, lb, go, oraw, states, dmix, b, s):
    t = b * s
    nblk = s // HGRN_ROWS
    cpb = HGRN_ROWS // CHUNK

    def body(hq_ref, hf_ref, hi_ref, hg_ref, dq_ref, dk_ref, dv_ref, lb_ref, go_ref, oraw_ref, st_ref, dro_ref,
             dp_ref, dlb_ref, dgo_ref, ds_scr, dlb_scr, dgo_scr):
        tril = _tri(True)
        ones_u = jnp.where(_tri(False), 1.0, 0.0).astype(BF16)
        gov = go_ref[...]
        dp_ref[:, 0:ATTN_W] = dq_ref[...]
        dp_ref[:, ATTN_W:2 * ATTN_W] = dk_ref[...]
        dp_ref[:, 2 * ATTN_W:3 * ATTN_W] = dv_ref[...]

        @pl.when(pl.program_id(1) == 0)
        def _():
            ds_scr[...] = jnp.zeros_like(ds_scr)
            dlb_scr[...] = jnp.zeros_like(dlb_scr)
            dgo_scr[...] = jnp.zeros_like(dgo_scr)

        def step(ci, carry):
            c = cpb - 1 - ci
            sl = pl.ds(pl.multiple_of(c * CHUNK, CHUNK), CHUNK)
            hq = hq_ref[sl, :]
            hg = hg_ref[sl, :]
            sig, f, bb, bl, bm, sq, q, k = _hgrn_chunk(hq, hf_ref[sl, :], lb_ref[...], tril)
            vb = hi_ref[sl, :].astype(BF16)
            ebm = jnp.exp(bb - bm)
            embm = jnp.exp(bm - bb)
            eb = jnp.exp(bb)
            ebl = jnp.exp(bl - bb)
            e_last = jnp.exp(bl)
            qe = (q * ebm).astype(BF16)
            ke = (k * embm).astype(BF16)
            qb = (q * eb).astype(BF16)
            kb = (k * ebl).astype(BF16)
            st = [st_ref[hh, c] for hh in range(HGRN_HEADS)]
            dst = [ds_scr[hh] for hh in range(HGRN_HEADS)]
            o = oraw_ref[sl, :]
            dro = dro_ref[sl, :]
            sg = _sigmoid(hg)
            gov4 = jnp.concatenate([gov] * HGRN_HEADS, axis=1)
            rstd = jnp.concatenate(
                [jnp.broadcast_to(lax.rsqrt(jnp.mean(o[:, hs] * o[:, hs], axis=-1, keepdims=True) + RMS_EPS),
                                  (CHUNK, HGRN_DH)) for hs in HEAD_LANES], axis=1)
            ohat = o * rstd
            dn = dro * (hg * sg)
            dhg = dro * (ohat * gov4) * (sg * (1.0 + hg * (1.0 - sg)))
            dgo_inc = jnp.sum(dn * ohat, axis=0, keepdims=True)
            dohat = dn * gov4
            proj_h = dohat * ohat
            pm = jnp.concatenate(
                [jnp.broadcast_to(jnp.mean(proj_h[:, hs], axis=-1, keepdims=True), (CHUNK, HGRN_DH))
                 for hs in HEAD_LANES], axis=1)
            dob = (rstd * (dohat - ohat * pm)).astype(BF16)
            stb = [x.astype(BF16) for x in st]
            dstb = [x.astype(BF16) for x in dst]
            a = [jnp.where(tril, _dot(qe[:, hs], ke[:, hs], NT), 0.0).astype(BF16) for hs in HEAD_LANES]
            dab = [jnp.where(tril, _dot(dob[:, hs], vb[:, hs], NT), 0.0).astype(BF16) for hs in HEAD_LANES]
            dqb = [_dot(dob[:, hs], stb[hh]) for hh, hs in enumerate(HEAD_LANES)]
            dkb = [_dot(vb[:, hs], dstb[hh]) for hh, hs in enumerate(HEAD_LANES)]
            dv_state = [_dot(kb[:, hs], dstb[hh], NT) for hh, hs in enumerate(HEAD_LANES)]
            dst_next = [dst[hh] * e_last[:, hs] + _dot(dob[:, hs], qb[:, hs], TN) for hh, hs in enumerate(HEAD_LANES)]
            dv = [_dot(a[hh], dob[:, hs], TN) + dv_state[hh] for hh, hs in enumerate(HEAD_LANES)]
            dqe = jnp.concatenate([_dot(dab[hh], ke[:, hs]) for hh, hs in enumerate(HEAD_LANES)], axis=1)
            dke = jnp.concatenate([_dot(dab[hh], qe[:, hs], TN) for hh, hs in enumerate(HEAD_LANES)], axis=1)
            dqb = jnp.concatenate(dqb, axis=1)
            dkb = jnp.concatenate(dkb, axis=1)
            state_term = jnp.concatenate(
                [jnp.sum(dst[hh] * st[hh], axis=0, keepdims=True) for hh in range(HGRN_HEADS)], axis=1)
            dq = dqe * ebm + dqb * eb
            dk = dke * embm + dkb * ebl
            db = (qe.astype(F32) * dqe - ke.astype(F32) * dke) + q * (dqb * eb) - k * (dkb * ebl)
            d_last = jnp.sum(k * ebl * dkb, axis=0, keepdims=True) + state_term * e_last
            dg = _dot_exact_lhs(ones_u, db) + d_last
            df = dg / f - dk
            first = HQ_COL * HGRN_DH
            dp_ref[sl, first:first + HGRN_W] = (dq * (sq * (1.0 + hq * (1.0 - sq)))).astype(BF16)
            dp_ref[sl, first + HGRN_W:first + 2 * HGRN_W] = (df * (1.0 - lb_ref[...]) * sig * (1.0 - sig)).astype(BF16)
            dp_ref[sl, first + 2 * HGRN_W:first + 3 * HGRN_W] = jnp.concatenate(dv, axis=1).astype(BF16)
            dp_ref[sl, first + 3 * HGRN_W:first + 4 * HGRN_W] = dhg.astype(BF16)
            dlb_scr[...] += jnp.sum(df * (1.0 - sig), axis=0, keepdims=True)
            dgo_scr[...] += dgo_inc
            for hh in range(HGRN_HEADS):
                ds_scr[hh] = dst_next[hh]
            return carry

        lax.fori_loop(0, cpb, step, 0)

        @pl.when(pl.program_id(1) == nblk - 1)
        def _():
            dlb_ref[...] = dlb_scr[...]
            dgo_ref[...] = dgo_scr[...]

    rows = lambda bi, i: bi * nblk + (nblk - 1 - i)
    col = lambda base: pl.BlockSpec((HGRN_ROWS, HGRN_W), lambda bi, i: (rows(bi, i), base // HGRN_HEADS))
    out = pl.BlockSpec((HGRN_ROWS, HGRN_W), lambda bi, i: (rows(bi, i), 0))
    part = pl.BlockSpec((None, 1, HGRN_W), lambda bi, i: (bi, 0, 0))
    width = HG_COL * HGRN_DH + HGRN_W
    o_shape = jax.ShapeDtypeStruct((t, width), BF16)
    p_shape = jax.ShapeDtypeStruct((b, 1, HGRN_W), F32)
    return pl.pallas_call(
        body,
        name=name,
        grid=(b, nblk),
        in_specs=[col(HQ_COL), col(HF_COL), col(HI_COL), col(HG_COL), out, out, out,
                  pl.BlockSpec((1, HGRN_W), lambda bi, i: (0, 0)), pl.BlockSpec((1, HGRN_DH), lambda bi, i: (0, 0)), out,
                  pl.BlockSpec((None, HGRN_HEADS, cpb, HGRN_DH, HGRN_DH), lambda bi, i: (bi, 0, nblk - 1 - i, 0, 0)),
                  col(ATTN_W // HGRN_DH)],
        out_specs=[pl.BlockSpec((HGRN_ROWS, width), lambda bi, i: (rows(bi, i), 0))] + [part] * 2,
        out_shape=[o_shape] + [p_shape] * 2,
        scratch_shapes=[pltpu.VMEM((HGRN_HEADS, HGRN_DH, HGRN_DH), F32), pltpu.VMEM((1, HGRN_W), F32),
                        pltpu.VMEM((1, HGRN_W), F32)],
        compiler_params=_params("parallel", "arbitrary"),
    )(proj, proj, proj, proj, *dqkv, lb, go, oraw, states, dmix)


def _small_grads(name, dg1, dgm, dg2, dgq, dgk, dbias_t, dlb, dgo, lbp):
    d = dg1.shape[1]

    def body(dg1_ref, dgm_ref, dg2_ref, dgq_ref, dgk_ref, dbias_ref, dlb_ref, dgo_ref, lbp_ref,
             g1_ref, gm_ref, g2_ref, gq_ref, gk_ref, rb_ref, lbg_ref, go_ref):
        g1_ref[...] = jnp.sum(dg1_ref[...], axis=0, keepdims=True)
        gm_ref[...] = jnp.sum(dgm_ref[...], axis=0, keepdims=True)
        g2_ref[...] = jnp.sum(dg2_ref[...], axis=0, keepdims=True)
        r = lax.broadcasted_iota(jnp.int32, (ATTN_W, ATTN_DH), 0)
        cidx = lax.broadcasted_iota(jnp.int32, (ATTN_W, ATTN_DH), 1)
        fold = jnp.where(jnp.bitwise_and(r, ATTN_DH - 1) == cidx, 1.0, 0.0).astype(BF16)
        gq_ref[...] = jnp.sum(_dot_exact_rhs(dgq_ref[...], fold), axis=0, keepdims=True)
        gk_ref[...] = jnp.sum(_dot_exact_rhs(dgk_ref[...], fold), axis=0, keepdims=True)
        gosum = jnp.sum(dgo_ref[...], axis=0, keepdims=True)
        go_ref[...] = (gosum[:, 0:HGRN_DH] + gosum[:, HGRN_DH:2 * HGRN_DH]
                       + gosum[:, 2 * HGRN_DH:3 * HGRN_DH] + gosum[:, 3 * HGRN_DH:4 * HGRN_DH])
        p0 = lbp_ref[0:1, :]
        p1 = lbp_ref[1:2, :]
        lbv = 1.0 / (1.0 + jnp.exp(p1 - p0))
        dp0 = jnp.sum(dlb_ref[...], axis=0, keepdims=True) * lbv * (1.0 - lbv)
        lbg_ref[0:1, :] = dp0
        lbg_ref[1:2, :] = -dp0
        sidx = lax.broadcasted_iota(jnp.int32, (BAND, N_REL_PAD), 0)
        ridx = lax.broadcasted_iota(jnp.int32, (BAND, N_REL_PAD), 1)

        def step(tq, acc):
            rel = jnp.clip(tq + KPAD - sidx, -REL_CLIP, REL_CLIP) + REL_CLIP
            onehot = jnp.where(rel == ridx, 1.0, 0.0).astype(BF16)
            return acc + _dot_exact_rhs(dbias_ref[tq], onehot)

        rb_ref[...] = lax.fori_loop(0, CHUNK, step, jnp.zeros((ATTN_HEADS, N_REL_PAD), F32))

    ins = [dg1, dgm, dg2, dgq, dgk, dbias_t, dlb, dgo, lbp]
    outs = [jax.ShapeDtypeStruct((1, d), F32)] * 3 + [jax.ShapeDtypeStruct((1, ATTN_DH), F32)] * 2 + [
        jax.ShapeDtypeStruct((ATTN_HEADS, N_REL_PAD), F32), jax.ShapeDtypeStruct((2, HGRN_W), F32),
        jax.ShapeDtypeStruct((1, HGRN_DH), F32)]
    vm = pl.BlockSpec(memory_space=pltpu.VMEM)
    return pl.pallas_call(
        body,
        name=name,
        in_specs=[vm] * len(ins),
        out_specs=[vm] * len(outs),
        out_shape=outs,
        compiler_params=pltpu.CompilerParams(vmem_limit_bytes=VMEM_LIMIT),
    )(*ins)


def _adam_update(w, g, m, v):
    m2 = ADAM_B1 * m + (1.0 - ADAM_B1) * g
    v2 = ADAM_B2 * v + (1.0 - ADAM_B2) * (g * g)
    m_hat = m2 / (1.0 - ADAM_B1 ** ADAM_STEP)
    v_hat = v2 / (1.0 - ADAM_B2 ** ADAM_STEP)
    delta = -ADAM_LR * (m_hat / (jnp.sqrt(v_hat) + ADAM_EPS) + ADAM_WD * w)
    return delta, m2, v2


def _rows_tile(r):
    return r if r <= 512 or r % 512 else 512


def _pair_sum(name, grad, theirs, core):
    n, half, c = theirs.shape
    tr = _rows_tile(half)
    nth = half // tr

    def body(core_ref, a_ref, b_ref, o_ref):
        o_ref[...] = (a_ref[...].astype(F32) + b_ref[...].astype(F32)).astype(o_ref.dtype)

    spec = pl.BlockSpec((None, tr, c), lambda i, j, core_ref: (i, j, 0))
    return pl.pallas_call(
        body, name=name,
        grid_spec=pltpu.PrefetchScalarGridSpec(
            num_scalar_prefetch=1, grid=(n, nth),
            in_specs=[pl.BlockSpec((None, tr, c), lambda i, j, core_ref: (i, core_ref[0] * nth + j, 0)), spec],
            out_specs=spec),
        out_shape=pltpu.HBM((n, half, c), BF16), compiler_params=_params("parallel", "parallel"),
    )(core, grad, theirs)


def _chip_sum(name, own, parts, chip):
    _, half, c = own.shape
    tr = _rows_tile(half)

    def body(chip_ref, own_ref, p_ref, o_ref):
        me = chip_ref[0]
        mine = own_ref[...].astype(F32)
        flip_x, flip_y, flip_xy = (p_ref[i].astype(F32) for i in range(3))
        acc = None
        for k in range(N_CHIPS):
            rel = jnp.bitwise_xor(me, k)
            term = jnp.where(rel == 0, mine, jnp.where(rel == 2, flip_x, jnp.where(rel == 1, flip_y, flip_xy)))
            acc = term if acc is None else acc + term
        o_ref[...] = acc

    return pl.pallas_call(
        body, name=name,
        grid_spec=pltpu.PrefetchScalarGridSpec(
            num_scalar_prefetch=1, grid=(half // tr,),
            in_specs=[pl.BlockSpec((None, tr, c), lambda j, chip_ref: (chip_ref[0], j, 0)),
                      pl.BlockSpec((3, tr, c), lambda j, chip_ref: (0, j, 0))],
            out_specs=pl.BlockSpec((tr, c), lambda j, chip_ref: (j, 0))),
        out_shape=pltpu.HBM((half, c), F32), compiler_params=_params("parallel"),
    )(chip, own, parts)


def _adamw(name, w, g_mine, g_theirs, m, v, core):
    _, r, c = w.shape
    half = r // 2
    tr = _rows_tile(half)
    nth = half // tr

    def body(core_ref, w_ref, gm_ref, gt_ref, m_ref, v_ref, g_ref, d_ref, m2_ref, v2_ref):
        g = jnp.where(pl.program_id(0) == core_ref[0], gm_ref[...], gt_ref[...])
        delta, m2, v2 = _adam_update(w_ref[...], g, m_ref[...], v_ref[...])
        g_ref[...] = g
        d_ref[...] = delta
        m2_ref[...] = m2
        v2_ref[...] = v2

    full = pl.BlockSpec((None, tr, c), lambda h, j, core_ref: (0, h * nth + j, 0))
    part = pl.BlockSpec((tr, c), lambda h, j, core_ref: (j, 0))
    shape = jax.ShapeDtypeStruct((1, r, c), F32)
    return pl.pallas_call(
        body, name=name,
        grid_spec=pltpu.PrefetchScalarGridSpec(
            num_scalar_prefetch=1, grid=(2, nth), in_specs=[full, part, part, full, full], out_specs=[full] * 4),
        out_shape=[shape] * 4, compiler_params=_params("parallel", "parallel"),
    )(core, w, g_mine, g_theirs, m, v)


def _rel_bias_table(name, rel_bias):
    padded = jnp.pad(rel_bias, ((0, 0), (0, N_REL_PAD - N_REL)))

    def body(rb_ref, o_ref):
        ridx = lax.broadcasted_iota(jnp.int32, (N_REL_PAD, BAND), 0)
        sidx = lax.broadcasted_iota(jnp.int32, (N_REL_PAD, BAND), 1)
        rb = rb_ref[...]

        def step(tq, carry):
            rel = jnp.clip(tq + KPAD - sidx, -REL_CLIP, REL_CLIP) + REL_CLIP
            onehot = jnp.where(rel == ridx, 1.0, 0.0).astype(BF16)
            o_ref[tq] = _dot_exact_rhs(rb, onehot)
            return carry

        lax.fori_loop(0, CHUNK, step, 0)

    vm = pl.BlockSpec(memory_space=pltpu.VMEM)
    table = pl.pallas_call(
        body, name=name, in_specs=[vm], out_specs=vm,
        out_shape=jax.ShapeDtypeStruct((CHUNK, ATTN_HEADS, BAND), F32),
    )(padded)
    return table.transpose(1, 0, 2)


def _adamw_small(name, w, parts, m, v):
    def body(w_ref, p_ref, m_ref, v_ref, g_ref, d_ref, m2_ref, v2_ref):
        g = p_ref[0]
        for i in range(1, N_DEV):
            g = g + p_ref[i]
        delta, m2, v2 = _adam_update(w_ref[...], g, m_ref[...], v_ref[...])
        g_ref[...] = g
        d_ref[...] = delta
        m2_ref[...] = m2
        v2_ref[...] = v2

    vm = pl.BlockSpec(memory_space=pltpu.VMEM)
    shape = jax.ShapeDtypeStruct((SMALL_ROWS, SMALL_COLS), F32)
    return pl.pallas_call(
        body, name=name, in_specs=[vm] * 4, out_specs=[vm] * 4, out_shape=[shape] * 4,
    )(w, parts, m, v)


def _position():
    return lax.axis_index("x"), lax.axis_index("y"), lax.axis_index("c")


def _other_chips(x, y):
    return [(1 - x, y), (x, 1 - y), (1 - x, 1 - y)]


ANY = pl.BlockSpec(memory_space=pl.ANY)
PAIR_ID = 0


def _pair_handshake():
    x, y, c = _position()
    barrier = pltpu.get_barrier_semaphore()
    pl.semaphore_signal(barrier, inc=1, device_id=(x, y, 1 - c), device_id_type=MESH)
    pl.semaphore_wait(barrier, 1)


PAIR_CALL = pltpu.CompilerParams(collective_id=PAIR_ID)


HBM = pl.BlockSpec(memory_space=pltpu.HBM)
SEM = pl.BlockSpec(memory_space=pltpu.SEMAPHORE)
SPLIT_COPY = pltpu.SideEffectType.DATAFLOW_SIDE_EFFECTING


def _gather_copy(shards, outs, send_sem, recv_sem, i, j):
    x, y, c = _position()
    chips = _other_chips(x, y)
    half = shards[i].shape[0] // 2
    rows = pl.ds(pl.multiple_of(c * half, 16), half)
    return pltpu.make_async_remote_copy(
        src_ref=shards[i].at[rows, :], dst_ref=outs[i].at[2 * x + y, rows, :],
        send_sem=send_sem.at[3 * i + j], recv_sem=recv_sem.at[3 * i + j],
        device_id=(chips[j][0], chips[j][1], c), device_id_type=MESH)


def _gather_start(name, shards, after):
    n = len(shards)

    def body(*refs):
        srcs, outs = refs[:n], refs[n:2 * n]
        send_sem, recv_sem = refs[2 * n + len(after)], refs[2 * n + len(after) + 1]
        token = refs[-1]
        for i in range(n):
            for j in range(3):
                _gather_copy(srcs, outs, send_sem, recv_sem, i, j).start()
        token[...] = jnp.zeros_like(token)

    full = [(N_CHIPS,) + s.shape for s in shards]
    res = pl.pallas_call(
        body,
        name=name,
        in_specs=[HBM] * (2 * n) + [ANY] * len(after),
        out_specs=[SEM, SEM] + [HBM] * (2 * n) + [pl.BlockSpec(memory_space=pltpu.VMEM)],
        out_shape=[pltpu.SemaphoreType.DMA((3 * n,)), pltpu.SemaphoreType.DMA((3 * n,))]
        + [pltpu.HBM(s.shape, s.dtype) for s in shards]
        + [pltpu.HBM(shp, s.dtype) for shp, s in zip(full, shards)]
        + [jax.ShapeDtypeStruct((8, LANES), F32)],
        input_output_aliases={i: 2 + i for i in range(2 * n)},
        compiler_params=pltpu.CompilerParams(has_side_effects=SPLIT_COPY),
    )(*[pltpu.with_memory_space_constraint(s, pltpu.HBM) for s in shards],
      *[pltpu.with_memory_space_constraint(lax.empty(shp, s.dtype), pltpu.HBM) for shp, s in zip(full, shards)],
      *after)
    return res[0], res[1], list(res[2:2 + n]), list(res[2 + n:2 + 2 * n]), res[-1]


def _gather_wait(name, send_sem, recv_sem, shards, outs, after):
    n = len(shards)

    def body(*refs):
        srcs, out_refs = refs[:n], refs[n:2 * n]
        send_ref, recv_ref = refs[2 * n], refs[2 * n + 1]
        for i in range(n):
            for j in range(3):
                copy = _gather_copy(srcs, out_refs, send_ref, recv_ref, i, j)
                copy.wait_send()
                copy.wait_recv()

    res = pl.pallas_call(
        body,
        name=name,
        in_specs=[HBM] * (2 * n) + [SEM, SEM] + [ANY] * len(after),
        out_specs=[HBM] * (2 * n),
        out_shape=[pltpu.HBM(s.shape, s.dtype) for s in shards] + [pltpu.HBM(o.shape, o.dtype) for o in outs],
        input_output_aliases={i: i for i in range(2 * n)},
        compiler_params=pltpu.CompilerParams(has_side_effects=SPLIT_COPY),
    )(*shards, *outs, send_sem, recv_sem, *after)
    return list(res[:n]), list(res[n:])


def _join_copies(srcs, ins, outs, own_send, own_recv, half_send, half_recv):
    x, y, c = _position()
    chips = _other_chips(x, y)
    copies = []
    for i in range(len(srcs)):
        copies.append(pltpu.make_async_remote_copy(
            src_ref=srcs[i], dst_ref=outs[i].at[2 * x + y], send_sem=own_send.at[i], recv_sem=own_recv.at[i],
            device_id=(x, y, 1 - c), device_id_type=MESH))
        half = srcs[i].shape[0] // 2
        rows = pl.ds(pl.multiple_of(c * half, 16), half)
        for j in range(3):
            slot = 2 * chips[j][0] + chips[j][1]
            copies.append(pltpu.make_async_remote_copy(
                src_ref=ins[i].at[slot, rows, :], dst_ref=outs[i].at[slot, rows, :],
                send_sem=half_send.at[3 * i + j], recv_sem=half_recv.at[3 * i + j],
                device_id=(x, y, 1 - c), device_id_type=MESH))
    return copies


def _gather_join(name, shards, outs):
    n = len(shards)

    def body(*refs):
        _pair_handshake()
        copies = _join_copies(refs[:n], refs[n:2 * n], refs[2 * n:3 * n], *refs[3 * n:])
        for cp in copies:
            cp.start()
        for cp in copies:
            cp.wait()

    return pl.pallas_call(
        body,
        name=name,
        in_specs=[ANY] * (2 * n),
        out_specs=[HBM] * n,
        out_shape=[pltpu.HBM(o.shape, o.dtype) for o in outs],
        input_output_aliases={n + i: i for i in range(n)},
        scratch_shapes=[pltpu.SemaphoreType.DMA((n,))] * 2 + [pltpu.SemaphoreType.DMA((3 * n,))] * 2,
        compiler_params=PAIR_CALL,
    )(*shards, *outs)


def _join_start(name, shards, outs):
    n = len(shards)

    def body(*refs):
        _pair_handshake()
        srcs, arrs = refs[:n], refs[n:2 * n]
        sems = refs[2 * n:2 * n + 4]
        token = refs[-1]
        for cp in _join_copies(srcs, arrs, arrs, *sems):
            cp.start()
        token[...] = jnp.zeros_like(token)

    res = pl.pallas_call(
        body,
        name=name,
        in_specs=[HBM] * (2 * n),
        out_specs=[SEM] * 4 + [HBM] * (2 * n) + [pl.BlockSpec(memory_space=pltpu.VMEM)],
        out_shape=[pltpu.SemaphoreType.DMA((n,))] * 2 + [pltpu.SemaphoreType.DMA((3 * n,))] * 2
        + [pltpu.HBM(s.shape, s.dtype) for s in shards] + [pltpu.HBM(o.shape, o.dtype) for o in outs]
        + [jax.ShapeDtypeStruct((8, LANES), F32)],
        input_output_aliases={i: 4 + i for i in range(2 * n)},
        compiler_params=pltpu.CompilerParams(has_side_effects=SPLIT_COPY, collective_id=PAIR_ID),
    )(*shards, *outs)
    return list(res[:4]), list(res[4:4 + n]), list(res[4 + n:4 + 2 * n]), res[-1]


def _join_wait(name, sems, shards, outs, after):
    n = len(shards)

    def body(*refs):
        srcs, arrs = refs[:n], refs[n:2 * n]
        for cp in _join_copies(srcs, arrs, arrs, *refs[2 * n:2 * n + 4]):
            cp.wait_send()
            cp.wait_recv()

    res = pl.pallas_call(
        body,
        name=name,
        in_specs=[HBM] * (2 * n) + [SEM] * 4 + [ANY] * len(after),
        out_specs=[HBM] * (2 * n),
        out_shape=[pltpu.HBM(s.shape, s.dtype) for s in shards] + [pltpu.HBM(o.shape, o.dtype) for o in outs],
        input_output_aliases={i: i for i in range(2 * n)},
        compiler_params=pltpu.CompilerParams(has_side_effects=SPLIT_COPY),
    )(*shards, *outs, *sems, *after)
    return list(res[n:])


def _pair_copy(grads, lands, send_sem, recv_sem, i):
    x, y, c = _position()
    half = grads[i].shape[1] // 2
    give = pl.ds(pl.multiple_of((1 - c) * half, 16), half)
    return pltpu.make_async_remote_copy(
        src_ref=grads[i].at[:, give, :], dst_ref=lands[i], send_sem=send_sem.at[i], recv_sem=recv_sem.at[i],
        device_id=(x, y, 1 - c), device_id_type=MESH)


def _pair_start(name, grads):
    n = len(grads)

    def body(*refs):
        _pair_handshake()
        srcs, lands = refs[:n], refs[n:2 * n]
        send_sem, recv_sem = refs[2 * n], refs[2 * n + 1]
        token = refs[-1]
        for i in range(n):
            _pair_copy(srcs, lands, send_sem, recv_sem, i).start()
        token[...] = jnp.zeros_like(token)

    halves = [(g.shape[0], g.shape[1] // 2, g.shape[2]) for g in grads]
    res = pl.pallas_call(
        body,
        name=name,
        in_specs=[HBM] * (2 * n),
        out_specs=[SEM, SEM] + [HBM] * (2 * n) + [pl.BlockSpec(memory_space=pltpu.VMEM)],
        out_shape=[pltpu.SemaphoreType.DMA((n,)), pltpu.SemaphoreType.DMA((n,))]
        + [pltpu.HBM(g.shape, g.dtype) for g in grads]
        + [pltpu.HBM(shp, g.dtype) for shp, g in zip(halves, grads)]
        + [jax.ShapeDtypeStruct((8, LANES), F32)],
        input_output_aliases={i: 2 + i for i in range(2 * n)},
        compiler_params=pltpu.CompilerParams(has_side_effects=SPLIT_COPY, collective_id=PAIR_ID),
    )(*[pltpu.with_memory_space_constraint(g, pltpu.HBM) for g in grads],
      *[pltpu.with_memory_space_constraint(lax.empty(shp, g.dtype), pltpu.HBM) for shp, g in zip(halves, grads)])
    return res[0], res[1], list(res[2:2 + n]), list(res[2 + n:2 + 2 * n]), res[-1]


def _pair_wait(name, send_sem, recv_sem, grads, lands, after):
    n = len(grads)

    def body(*refs):
        srcs, land_refs = refs[:n], refs[n:2 * n]
        send_ref, recv_ref = refs[2 * n], refs[2 * n + 1]
        for i in range(n):
            copy = _pair_copy(srcs, land_refs, send_ref, recv_ref, i)
            copy.wait_send()
            copy.wait_recv()

    res = pl.pallas_call(
        body,
        name=name,
        in_specs=[HBM] * (2 * n) + [SEM, SEM, ANY],
        out_specs=[HBM] * (2 * n),
        out_shape=[pltpu.HBM(g.shape, g.dtype) for g in grads] + [pltpu.HBM(l.shape, l.dtype) for l in lands],
        input_output_aliases={i: i for i in range(2 * n)},
        compiler_params=pltpu.CompilerParams(has_side_effects=SPLIT_COPY),
    )(*grads, *lands, send_sem, recv_sem, after)
    return list(res[:n]), list(res[n:])


def _scatter_copy(srcs, lands, send_sem, recv_sem, i, j):
    x, y, c = _position()
    chips = _other_chips(x, y)
    return pltpu.make_async_remote_copy(
        src_ref=srcs[i].at[2 * chips[j][0] + chips[j][1]], dst_ref=lands[i].at[j],
        send_sem=send_sem.at[3 * i + j], recv_sem=recv_sem.at[3 * i + j],
        device_id=(chips[j][0], chips[j][1], c), device_id_type=MESH)


def _scatter_start(name, sums):
    n = len(sums)

    def body(*refs):
        srcs, lands = refs[:n], refs[n:2 * n]
        send_sem, recv_sem = refs[2 * n], refs[2 * n + 1]
        token = refs[-1]
        for i in range(n):
            for j in range(3):
                _scatter_copy(srcs, lands, send_sem, recv_sem, i, j).start()
        token[...] = jnp.zeros_like(token)

    land_shapes = [(3,) + s.shape[1:] for s in sums]
    res = pl.pallas_call(
        body,
        name=name,
        in_specs=[HBM] * (2 * n),
        out_specs=[SEM, SEM] + [HBM] * (2 * n) + [pl.BlockSpec(memory_space=pltpu.VMEM)],
        out_shape=[pltpu.SemaphoreType.DMA((3 * n,)), pltpu.SemaphoreType.DMA((3 * n,))]
        + [pltpu.HBM(s.shape, s.dtype) for s in sums]
        + [pltpu.HBM(shp, s.dtype) for shp, s in zip(land_shapes, sums)]
        + [jax.ShapeDtypeStruct((8, LANES), F32)],
        input_output_aliases={i: 2 + i for i in range(2 * n)},
        compiler_params=pltpu.CompilerParams(has_side_effects=SPLIT_COPY),
    )(*[pltpu.with_memory_space_constraint(s, pltpu.HBM) for s in sums],
      *[pltpu.with_memory_space_constraint(lax.empty(shp, s.dtype), pltpu.HBM) for shp, s in zip(land_shapes, sums)])
    return res[0], res[1], list(res[2:2 + n]), list(res[2 + n:2 + 2 * n]), res[-1]


def _scatter_wait(name, send_sem, recv_sem, sums, lands, after):
    n = len(sums)

    def body(*refs):
        srcs, land_refs = refs[:n], refs[n:2 * n]
        send_ref, recv_ref = refs[2 * n], refs[2 * n + 1]
        for i in range(n):
            for j in range(3):
                copy = _scatter_copy(srcs, land_refs, send_ref, recv_ref, i, j)
                copy.wait_send()
                copy.wait_recv()

    res = pl.pallas_call(
        body,
        name=name,
        in_specs=[HBM] * (2 * n) + [SEM, SEM, ANY],
        out_specs=[HBM] * (2 * n),
        out_shape=[pltpu.HBM(s.shape, s.dtype) for s in sums] + [pltpu.HBM(l.shape, l.dtype) for l in lands],
        input_output_aliases={i: i for i in range(2 * n)},
        compiler_params=pltpu.CompilerParams(has_side_effects=SPLIT_COPY),
    )(*sums, *lands, send_sem, recv_sem, after)
    return list(res[:n]), list(res[n:])


def _pair_join(name, halves, small=None):
    n = len(halves)
    if small is None:
        def body_plain(*refs):
            _pair_handshake()
            ins, outs = refs[:n], refs[n:2 * n]
            send_sem, recv_sem = refs[2 * n:]
            x, y, c = _position()
            swaps = [pltpu.make_async_remote_copy(
                src_ref=ins[i], dst_ref=outs[i], send_sem=send_sem.at[i], recv_sem=recv_sem.at[i],
                device_id=(x, y, 1 - c), device_id_type=MESH) for i in range(n)]
            for swap in swaps:
                swap.start()
            for swap in swaps:
                swap.wait()

        return pl.pallas_call(
            body_plain,
            name=name,
            in_specs=[ANY] * n,
            out_specs=[ANY] * n,
            out_shape=[jax.ShapeDtypeStruct(h.shape, h.dtype) for h in halves],
            scratch_shapes=[pltpu.SemaphoreType.DMA((n,))] * 2,
            compiler_params=PAIR_CALL,
        )(*halves)

    def body(*refs):
        ins, small_ref = refs[:n], refs[n]
        outs, all_ref = refs[n + 1:2 * n + 1], refs[2 * n + 1]
        send_sem, recv_sem, sm_send, sm_recv, sm_local = refs[2 * n + 2:]
        x, y, c = _position()
        swaps = []
        for i in range(n):
            swap = pltpu.make_async_remote_copy(
                src_ref=ins[i], dst_ref=outs[i], send_sem=send_sem.at[i], recv_sem=recv_sem.at[i],
                device_id=(x, y, 1 - c), device_id_type=MESH)
            swap.start()
            swaps.append(swap)
        me = 4 * x + 2 * y + c
        sm_own = pltpu.make_async_copy(small_ref, all_ref.at[me], sm_local)
        sm_own.start()
        pushes, arrivals = [], []
        for mask in range(1, N_DEV):
            px, py, pc = x ^ (mask >> 2), y ^ ((mask >> 1) & 1), c ^ (mask & 1)
            pushes.append(pltpu.make_async_remote_copy(
                src_ref=small_ref, dst_ref=all_ref.at[me], send_sem=sm_send.at[mask - 1], recv_sem=sm_recv.at[mask - 1],
                device_id=(px, py, pc), device_id_type=MESH))
            arrivals.append(pltpu.make_async_remote_copy(
                src_ref=small_ref, dst_ref=all_ref.at[4 * px + 2 * py + pc], send_sem=sm_send.at[mask - 1],
                recv_sem=sm_recv.at[mask - 1], device_id=(px, py, pc), device_id_type=MESH))
        for cp in pushes:
            cp.start()
        for swap in swaps:
            swap.wait()
        for cp in arrivals:
            cp.wait_recv()
        for cp in pushes:
            cp.wait_send()
        sm_own.wait()

    res = pl.pallas_call(
        body,
        name=name,
        in_specs=[ANY] * (n + 1),
        out_specs=[ANY] * (n + 1),
        out_shape=[jax.ShapeDtypeStruct(h.shape, h.dtype) for h in halves]
        + [jax.ShapeDtypeStruct((N_DEV,) + small.shape, small.dtype)],
        scratch_shapes=[pltpu.SemaphoreType.DMA((n,))] * 2 + [pltpu.SemaphoreType.DMA((N_DEV - 1,))] * 2
        + [pltpu.SemaphoreType.DMA(())],
    )(*halves, small)
    return res[:n], res[n]


def _lower_bound(lbp):
    return jax.nn.softmax(lbp, axis=0)[0:1]


def _local_step(x, target, g1, gm, g2, gq, gk, go, rel_bias, lbp, weights, on_grads, grads_sent):
    b, s, d = x.shape
    t = b * s
    x0 = x.reshape(t, d)
    tgt = target.reshape(t, d)
    gq_t = jnp.tile(gq, (1, ATTN_HEADS))
    gk_t = jnp.tile(gk, (1, ATTN_HEADS))
    lb = _lower_bound(lbp)
    table = _band_table(_rel_bias_table("rel_bias_table", rel_bias))

    h1 = _rmsnorm_fwd("norm1", x0, g1)
    wg1, wu1, deps1 = weights["first"]((h1, table))
    a1, b1, z1 = _ffn_up("ffn1_up", h1, wg1, wu1, deps1)
    wd1, deps_mid = weights["mid"]((z1,))
    x1, h2 = _ffn_down("ffn1_down", z1, wd1, x0, gm, deps_mid)
    w_in, w_out = weights["mid_rest"]((x1,))
    ns = w_in.shape[0]
    proj = _in_proj("in_proj", h2, w_in)
    proj3 = proj.reshape(b, s, proj.shape[1])
    qn, kn, vb = _qk_prep("qk_prep", proj3, gq_t, gk_t)
    attn = _attn_fwd("attn_fwd", qn, kn, vb, table, weights["last_begin"]((qn,))).reshape(t, ATTN_W)
    mix, oraw, states = _hgrn_fwd("hgrn_fwd", proj, attn, lb, go, b, s)
    x2, h3 = _out_proj("out_proj", mix, w_out, x1, g2)
    wg2, wu2, wd2 = weights["last"]((h3,))
    a2, b2, z2 = _ffn_up("ffn2_up", h3, wg2, wu2)
    dy, dyh, sq = _ffn_down_loss("ffn2_down_loss", z2, wd2, x2, tgt)
    loss = 0.5 * jnp.sum(sq) / d

    da2, db2 = _ffn_bwd_act("ffn2_bwd_act", dyh, wd2, a2, b2)
    dwd2 = _grad_w_shardrows("ffn2_dwd", z2, dyh)
    dwg2 = _grad_w_shardrows("ffn2_dwg", da2, h3)
    dwu2 = _grad_w_shardrows("ffn2_dwu", db2, h3)
    sent2 = on_grads("ffn2", {"ffn2_w_gate": dwg2, "ffn2_w_up": dwu2, "ffn2_w_down": dwd2})
    dx2, dx2b, dg2 = _ffn_bwd_in("ffn2_bwd_in", da2, db2, wg2, wu2, x2, g2, dy, 1.0, sent2)
    sent2 = grads_sent("ffn2", dx2b)

    dwout = _grad_w_out("dw_out", mix, dx2b)
    dmix = _out_proj_bwd("out_proj_bwd", dx2b, w_out, sent2)
    dqn, dkn, dvn, dbe, dbo = _attn_bwd("attn_bwd", qn, kn, vb, table, dmix.reshape(b, s, dmix.shape[1]))
    dbias = dbe[:, :, :BAND] + dbo[:, :, CHUNK:]
    dpq, dpk, dpv, dgq, dgk = _qk_prep_bwd("qk_prep_bwd", proj3, dqn, dkn, dvn, gq_t, gk_t)
    dpq, dpk, dpv = (a.reshape(t, ATTN_W) for a in (dpq, dpk, dpv))
    dproj, dlb, dgo = _hgrn_bwd("hgrn_bwd", proj, (dpq, dpk, dpv), lb, go, oraw, states, dmix, b, s)
    dwin = _grad_w_in("dw_in", h2, dproj, ns)
    dx1, dx1h, dgm = _in_proj_bwd("in_proj_bwd", dproj, w_in, x1, gm, dx2, 0.5)

    dwd1 = _grad_w_shardrows("ffn1_dwd", z1, dx1h)
    sent_mix = on_grads("mix", {"w_in": dwin, "w_out": dwout.reshape(ns, dwout.shape[0] // ns, d),
                                "ffn1_w_down": dwd1})
    da1, db1 = _ffn_bwd_act("ffn1_bwd_act", dx1h, wd1, a1, b1, sent_mix)
    sent_mix = grads_sent("mix", da1)
    dwg1 = _grad_w_shardrows("ffn1_dwg", da1, h1, sent_mix)
    dwu1 = _grad_w_shardrows("ffn1_dwu", db1, h1)
    on_grads("ffn1", {"ffn1_w_gate": dwg1, "ffn1_w_up": dwu1})
    sent1 = grads_sent("ffn1", None)
    dx0, dg1 = _ffn_bwd_in("ffn1_bwd_in", da1, db1, wg1, wu1, x0, g1, dx1, None, sent1)

    nt = dg1.shape[0]
    sg = _small_grads(
        "small_grads", dg1.reshape(nt, d), dgm.reshape(nt, d), dg2.reshape(nt, d),
        dgq.reshape(-1, ATTN_W), dgk.reshape(-1, ATTN_W), dbias.transpose(1, 0, 2),
        dlb.reshape(b, HGRN_W), dgo.reshape(b, HGRN_W), lbp)
    g1g, gmg, g2g, gqg, gkg, rbg, lbg, gog = sg
    small = _pack_small(g1g, gmg, g2g, lbg, rbg[:, :N_REL], gqg, gkg, gog, loss)
    return dx0.reshape(b, s, d), small


LOSS_SLOT = 7 * SMALL_COLS + 2 * ATTN_DH + HGRN_DH


def _pack_small(g1, gm, g2, lbp, rel_bias, gq, gk, go, loss=None):
    flat = [g1.reshape(-1), gm.reshape(-1), g2.reshape(-1), lbp.reshape(-1), rel_bias.reshape(-1)]
    n_bias = 3 * SMALL_COLS - rel_bias.size
    heads = [gq.reshape(-1), gk.reshape(-1), go.reshape(-1)]
    heads.append(jnp.zeros((1,), F32) if loss is None else loss.reshape(1))
    n_tail = SMALL_COLS - sum(h.size for h in heads)
    return jnp.concatenate(flat + [jnp.zeros((n_bias,), F32)] + heads + [jnp.zeros((n_tail,), F32)]).reshape(
        SMALL_ROWS, SMALL_COLS)


def _unpack_small(p, d):
    flat = p.reshape(-1)
    o = 3 * d
    g1, gm, g2 = p[0:1], p[1:2], p[2:3]
    lbp = flat[o:o + 2 * HGRN_W].reshape(2, HGRN_W)
    o = 4 * SMALL_COLS
    rel = flat[o:o + ATTN_HEADS * N_REL].reshape(1, ATTN_HEADS, N_REL)
    o = 7 * SMALL_COLS
    gq = flat[o:o + ATTN_DH].reshape(1, ATTN_DH)
    gk = flat[o + ATTN_DH:o + 2 * ATTN_DH].reshape(1, ATTN_DH)
    go = flat[o + 2 * ATTN_DH:o + 2 * ATTN_DH + HGRN_DH].reshape(1, HGRN_DH)
    return g1, gm, g2, gq, gk, rel, lbp, go


def kernel(x, ffn1_norm_g, ffn1_w_gate, ffn1_w_up, ffn1_w_down, mix_norm_g, w_in, attn_q_norm_g, attn_k_norm_g, attn_rel_bias, hgrn_lower_bounds, hgrn_out_norm_g, w_out, ffn2_norm_g, ffn2_w_gate, ffn2_w_up, ffn2_w_down, loss_target, m_ffn1_norm_g, m_ffn1_w_gate, m_ffn1_w_up, m_ffn1_w_down, m_mix_norm_g, m_w_in, m_attn_q_norm_g, m_attn_k_norm_g, m_attn_rel_bias, m_hgrn_lower_bounds, m_hgrn_out_norm_g, m_w_out, m_ffn2_norm_g, m_ffn2_w_gate, m_ffn2_w_up, m_ffn2_w_down, v_ffn1_norm_g, v_ffn1_w_gate, v_ffn1_w_up, v_ffn1_w_down, v_mix_norm_g, v_w_in, v_attn_q_norm_g, v_attn_k_norm_g, v_attn_rel_bias, v_hgrn_lower_bounds, v_hgrn_out_norm_g, v_w_out, v_ffn2_norm_g, v_ffn2_w_gate, v_ffn2_w_up, v_ffn2_w_down):
    d = x.shape[-1]
    big_w = [ffn1_w_gate, ffn1_w_up, ffn1_w_down, w_in, w_out, ffn2_w_gate, ffn2_w_up, ffn2_w_down]
    big_m = [m_ffn1_w_gate, m_ffn1_w_up, m_ffn1_w_down, m_w_in, m_w_out, m_ffn2_w_gate, m_ffn2_w_up, m_ffn2_w_down]
    big_v = [v_ffn1_w_gate, v_ffn1_w_up, v_ffn1_w_down, v_w_in, v_w_out, v_ffn2_w_gate, v_ffn2_w_up, v_ffn2_w_down]
    big_names = ["ffn1_w_gate", "ffn1_w_up", "ffn1_w_down", "w_in", "w_out", "ffn2_w_gate", "ffn2_w_up", "ffn2_w_down"]
    flipped = {nm for nm in big_names if nm.endswith("gate") or nm.endswith("up")}
    flip = lambda nm, a: jnp.swapaxes(a, 1, 2) if nm in flipped else a
    big_w, big_m, big_v = ([flip(nm, a) for nm, a in zip(big_names, arrs)] for arrs in (big_w, big_m, big_v))

    shards = [w[0].astype(BF16) for w in big_w]
    start_a = _gather_start("gather_start_up1", shards[:2], ())
    start_b = _gather_start("gather_start_mid", shards[2:5], (start_a[4],))
    start_c = _gather_start("gather_start_ffn2", shards[5:], (start_b[4],))

    pending = {}

    def arrived(tag, started, after):
        send_sem, recv_sem, srcs, outs, _ = started
        return _gather_wait("gather_wait_" + tag, send_sem, recv_sem, srcs, outs, after)

    def first_weights(after):
        return (*_gather_join("gather_join_up1", *arrived("up1", start_a, after)), (start_c[4],))

    def mid_weights(after):
        srcs, outs = arrived("mid", start_b, after)
        (wd1,) = _gather_join("gather_join_wd1", srcs[:1], outs[:1])
        pending["mid"] = _join_start("join_start_mid", srcs[1:], outs[1:])
        return wd1, (pending["mid"][3],)

    def mid_rest(after):
        sems, srcs, outs, _ = pending["mid"]
        win_f, wout_f = _join_wait("join_wait_mid", sems, srcs, outs, after)
        return win_f, wout_f.reshape(wout_f.shape[0] * wout_f.shape[1], d)

    def last_begin(after):
        pending["ffn2"] = _join_start("join_start_ffn2", *arrived("ffn2", start_c, after))
        return (pending["ffn2"][3],)

    def last_weights(after):
        sems, srcs, outs, _ = pending["ffn2"]
        return _join_wait("join_wait_ffn2", sems, srcs, outs, after)

    weights = {"first": first_weights, "mid": mid_weights, "mid_rest": mid_rest, "last_begin": last_begin,
               "last": last_weights}

    core = lax.axis_index("c").astype(jnp.int32).reshape(1)
    chip = (2 * lax.axis_index("x") + lax.axis_index("y")).astype(jnp.int32).reshape(1)
    started = {}

    def on_grads(tag, grads):
        names = list(grads)
        started[tag] = (names, _pair_start("pair_start_" + tag, [grads[nm] for nm in names]))
        return (started[tag][1][4],)

    def grads_sent(tag, after):
        names, (send_sem, recv_sem, grads, lands, token) = started[tag]
        grads, theirs = _pair_wait("pair_wait_" + tag, send_sem, recv_sem, grads, lands, token if after is None else after)
        sums = [_pair_sum("pair_sum_" + nm, g, th, core) for nm, g, th in zip(names, grads, theirs)]
        started[tag] = (names, _scatter_start("scatter_start_" + tag, sums))
        return (started[tag][1][4],)

    grad_x, small_g = _local_step(
        x, loss_target, ffn1_norm_g, mix_norm_g, ffn2_norm_g, attn_q_norm_g, attn_k_norm_g, hgrn_out_norm_g,
        attn_rel_bias[0], hgrn_lower_bounds, weights, on_grads, grads_sent)

    def finish(tag, after):
        names, (send_sem, recv_sem, sums, lands, _) = started[tag]
        sums, lands = _scatter_wait("scatter_wait_" + tag, send_sem, recv_sem, sums, lands, after)
        return names, [_chip_sum("chip_sum_" + nm, sm, ld, chip) for nm, sm, ld in zip(names, sums, lands)]

    by_name = {nm: (w, m, v) for nm, w, m, v in zip(big_names, big_w, big_m, big_v)}
    updated = {}

    def update(names, halves, other_halves):
        for nm, mine, theirs in zip(names, halves, other_halves):
            w, m, v = by_name[nm]
            updated[nm] = _adamw("adamw_" + nm, w, mine, theirs, m, v, core)

    last_token = started["ffn1"][1][4]
    names_a, halves_a = finish("ffn2", last_token)
    names_m, halves_m = finish("mix", last_token)
    names_a, halves_a = names_a + names_m, halves_a + halves_m
    update(names_a, halves_a, _pair_join("pair_join_early", halves_a))
    names_b, halves_b = finish("ffn1", updated[names_a[-1]][1])
    others_b, small_all = _pair_join("pair_join_last", halves_b, small_g)
    update(names_b, halves_b, others_b)
    big_out = [updated[nm] for nm in big_names]

    pack = lambda g1, gm, g2, gq, gk, rel, lbp, go: _pack_small(g1, gm, g2, lbp, rel[0], gq, gk, go)
    small_w = pack(ffn1_norm_g, mix_norm_g, ffn2_norm_g, attn_q_norm_g, attn_k_norm_g, attn_rel_bias, hgrn_lower_bounds, hgrn_out_norm_g)
    small_m = pack(m_ffn1_norm_g, m_mix_norm_g, m_ffn2_norm_g, m_attn_q_norm_g, m_attn_k_norm_g, m_attn_rel_bias, m_hgrn_lower_bounds, m_hgrn_out_norm_g)
    small_v = pack(v_ffn1_norm_g, v_mix_norm_g, v_ffn2_norm_g, v_attn_q_norm_g, v_attn_k_norm_g, v_attn_rel_bias, v_hgrn_lower_bounds, v_hgrn_out_norm_g)
    small_res = _adamw_small("adamw_small", small_w, small_all, small_m, small_v)
    small_out = [_unpack_small(p, d) for p in small_res]
    loss = small_res[0].reshape(-1)[LOSS_SLOT]

    def assemble(kind):
        bg = [flip(nm, o[kind]) for nm, o in zip(big_names, big_out)]
        g1, gm, g2, gq, gk, rel, lbp, go = small_out[kind]
        return [g1, bg[0], bg[1], bg[2], gm, bg[3], gq, gk, rel, lbp, go, bg[4], g2, bg[5], bg[6], bg[7]]

    return (loss, grad_x, *assemble(0), *assemble(1), *assemble(2), *assemble(3))
```

```python
import functools

import jax
import jax.numpy as jnp
from jax import lax
from jax.experimental import pallas as pl
from jax.experimental.pallas import tpu as pltpu

F32 = jnp.float32
BF16 = jnp.bfloat16
MESH = pl.DeviceIdType.MESH

N_CHIPS = 4
N_DEV = 8
CHUNK = 64
ATTN_HEADS = 8
ATTN_DH = 64
ATTN_W = ATTN_HEADS * ATTN_DH
HGRN_HEADS = 4
HGRN_DH = 128
HGRN_W = HGRN_HEADS * HGRN_DH
LEFT_CHUNKS = 8
BAND = (LEFT_CHUNKS + 1) * CHUNK
KPAD = LEFT_CHUNKS * CHUNK
REL_CLIP = 128
N_REL = 2 * REL_CLIP + 1
N_REL_PAD = 384
RMS_EPS = 1e-6
LANES = 128
SMALL_ROWS = 8
SMALL_COLS = 1024

ADAM_LR = 0.001
ADAM_B1 = 0.9
ADAM_B2 = 0.999
ADAM_EPS = 1e-08
ADAM_WD = 0.01
ADAM_STEP = 10

NN = (((1,), (0,)), ((), ()))
NT = (((1,), (1,)), ((), ()))
TN = (((0,), (0,)), ((), ()))

VMEM_LIMIT = 48 * 1024 * 1024


def _sigmoid(x):
    return 1.0 / (1.0 + jnp.exp(-x))


def _silu(x):
    return x * _sigmoid(x)


def _dot(a, b, dims=NN):
    return lax.dot_general(a, b, dims, preferred_element_type=F32)


def _split3(x):
    hi = x.astype(BF16)
    r1 = x - hi.astype(F32)
    mid = r1.astype(BF16)
    lo = (r1 - mid.astype(F32)).astype(BF16)
    return hi, mid, lo


def _dot_exact_rhs(x, mat, dims=NN, pieces=3):
    hi, mid, lo = _split3(x)
    out = _dot(hi, mat, dims) + _dot(mid, mat, dims)
    return out + _dot(lo, mat, dims) if pieces == 3 else out


def _dot_exact_lhs(mat, x, dims=NN):
    hi, mid, lo = _split3(x)
    return _dot(mat, hi, dims) + _dot(mat, mid, dims) + _dot(mat, lo, dims)


def _params(*sem):
    return pltpu.CompilerParams(dimension_semantics=sem, vmem_limit_bytes=VMEM_LIMIT)


def _mm(name, ins, terms, n_acc, grid, acc_shape, outs, epilogue, extras=(), deps=()):
    nk = grid[2]
    ni, ne, nd, no = len(ins), len(extras), len(deps), len(outs)

    def body(*refs):
        in_refs = refs[:ni]
        ex_refs = refs[ni:ni + ne]
        out_refs = refs[ni + ne + nd:ni + ne + nd + no]
        acc_refs = refs[ni + ne + nd + no:]
        parts = [None] * n_acc
        for ai, li, ri, dims in terms:
            d = _dot(in_refs[li][...], in_refs[ri][...], dims)
            parts[ai] = d if parts[ai] is None else parts[ai] + d

        def finish(accs):
            res = epilogue(accs, [e[...] for e in ex_refs])
            for o, r in zip(out_refs, res):
                o[...] = r.astype(o.dtype)

        if nk == 1:
            finish(parts)
        else:
            k = pl.program_id(2)

            @pl.when(k == 0)
            def _():
                for a, p in zip(acc_refs, parts):
                    a[...] = p

            @pl.when(k > 0)
            def _():
                for a, p in zip(acc_refs, parts):
                    a[...] += p

            @pl.when(k == nk - 1)
            def _():
                finish([a[...] for a in acc_refs])

    scratch = [] if nk == 1 else [pltpu.VMEM(acc_shape, F32) for _ in range(n_acc)]
    res = pl.pallas_call(
        body,
        name=name,
        grid=grid,
        in_specs=[s for _, s in ins] + [s for _, s in extras] + [pl.BlockSpec(memory_space=pl.ANY)] * nd,
        out_specs=[s for _, s in outs],
        out_shape=[o for o, _ in outs],
        scratch_shapes=scratch,
        compiler_params=_params("parallel", "parallel", "arbitrary"),
    )(*[a for a, _ in ins], *[a for a, _ in extras], *deps)
    return res


def _resident_weights(w_hbm, w_vmem, sem):
    first = pl.program_id(0) == 0
    ns = w_vmem[0].shape[0]
    copies = [[pltpu.make_async_copy(h.at[j], v.at[j], sem.at[p, j]) for p, (h, v) in enumerate(zip(w_hbm, w_vmem))]
              for j in range(ns)]

    @pl.when(first)
    def _():
        for shard in copies:
            for cp in shard:
                cp.start()

    def landed(j):
        @pl.when(first)
        def _():
            for cp in copies[j]:
                cp.wait()

    return landed


def _mm_rows(name, lhs, weights, dims, t, outs, epilogue, extras=(), deps=()):
    tm = _row_tile(t)
    nl, ne, nd, no = len(lhs), len(extras), len(deps), len(outs)
    ns = weights[0].shape[0]

    def body(*refs):
        lhs_refs = refs[:nl]
        w_hbm = refs[nl:2 * nl]
        ex_refs = refs[2 * nl:2 * nl + ne]
        out_refs = refs[2 * nl + ne + nd:2 * nl + ne + nd + no]
        w_vmem = refs[2 * nl + ne + nd + no:3 * nl + ne + nd + no]
        landed = _resident_weights(w_hbm, w_vmem, refs[-1])

        acc = None
        for j in range(ns):
            landed(j)
            for p in range(nl):
                part = _dot(lhs[p][2](lhs_refs[p], j), w_vmem[p][j], dims)
                acc = part if acc is None else acc + part
        res = epilogue([acc], [e[...] for e in ex_refs])
        for o, r in zip(out_refs, res):
            o[...] = r.astype(o.dtype)

    return pl.pallas_call(
        body,
        name=name,
        grid=(t // tm,),
        in_specs=[s for _, s, _ in lhs] + [pl.BlockSpec(memory_space=pl.ANY)] * nl + [s for _, s in extras]
        + [pl.BlockSpec(memory_space=pl.ANY)] * nd,
        out_specs=[s for _, s in outs],
        out_shape=[o for o, _ in outs],
        scratch_shapes=[pltpu.VMEM(w.shape, w.dtype) for w in weights] + [pltpu.SemaphoreType.DMA((nl, ns))],
        compiler_params=_params("arbitrary"),
    )(*[a for a, _, _ in lhs], *weights, *[a for a, _ in extras], *deps)


def _mm_shards(name, x, weights, dims, outs, epilogue, extras=(), deps=()):
    t = x.shape[0]
    tm = _row_tile(t)
    nw, ne, nd, no = len(weights), len(extras), len(deps), len(outs)
    ns = weights[0].shape[0]

    def body(*refs):
        x_ref = refs[0]
        w_hbm = refs[1:1 + nw]
        ex_refs = refs[1 + nw:1 + nw + ne]
        out_refs = refs[1 + nw + ne + nd:1 + nw + ne + nd + no]
        w_vmem = refs[1 + nw + ne + nd + no:1 + 2 * nw + ne + nd + no]
        landed = _resident_weights(w_hbm, w_vmem, refs[-1])

        def shard_dots(j):
            landed(j)
            return [_dot(xv, w_vmem[p][j], dims) for p in range(nw)]

        xv = x_ref[...]
        accs = shard_dots(0)
        for j in range(ns):
            nxt = shard_dots(j + 1) if j + 1 < ns else None
            res = epilogue(accs, [e[j] for e in ex_refs])
            for (_, _, store), o, r in zip(outs, out_refs, res):
                store(o, j, r.astype(o.dtype))
            accs = nxt

    return pl.pallas_call(
        body,
        name=name,
        grid=(t // tm,),
        in_specs=[pl.BlockSpec((tm, x.shape[1]), lambda i: (i, 0))] + [pl.BlockSpec(memory_space=pl.ANY)] * nw
        + [s for _, s in extras] + [pl.BlockSpec(memory_space=pl.ANY)] * nd,
        out_specs=[s for _, s, _ in outs],
        out_shape=[o for o, _, _ in outs],
        scratch_shapes=[pltpu.VMEM(w.shape, w.dtype) for w in weights] + [pltpu.SemaphoreType.DMA((nw, ns))],
        compiler_params=_params("arbitrary"),
    )(x, *weights, *[a for a, _ in extras], *deps)


def _store_shard(ref, j, value):
    ref[j] = value


def _row_tile(t):
    return 512 if t % 512 == 0 else t


def _k_tile(t):
    return t if t <= 4096 else 1024


def _rmsnorm(xv, g):
    ms = jnp.mean(xv * xv, axis=-1, keepdims=True)
    return xv * lax.rsqrt(ms + RMS_EPS) * g


def _rmsnorm_fwd(name, x, g):
    t, d = x.shape
    tm = _row_tile(t)

    def body(x_ref, g_ref, h_ref):
        h_ref[...] = _rmsnorm(x_ref[...], g_ref[...]).astype(BF16)

    return pl.pallas_call(
        body,
        name=name,
        grid=(t // tm,),
        in_specs=[pl.BlockSpec((tm, d), lambda i: (i, 0)), pl.BlockSpec((1, d), lambda i: (0, 0))],
        out_specs=pl.BlockSpec((tm, d), lambda i: (i, 0)),
        out_shape=jax.ShapeDtypeStruct((t, d), BF16),
        compiler_params=_params("parallel"),
    )(x, g)


def _norm_bwd_epilogue(copy_scale):
    def epilogue(accs, ex):
        dh = accs[0]
        xv, g, dres = ex
        ms = jnp.mean(xv * xv, axis=-1, keepdims=True)
        rstd = lax.rsqrt(ms + RMS_EPS)
        xhat = xv * rstd
        dxhat = dh * g
        dx = rstd * (dxhat - xhat * jnp.mean(dxhat * xhat, axis=-1, keepdims=True))
        out = dres + dx
        dg = jnp.sum(dh * xhat, axis=0, keepdims=True)
        if copy_scale is None:
            return out, dg
        return out, out * copy_scale, dg

    return epilogue


def _ffn_up(name, h, wg, wu, deps=()):
    t, d = h.shape
    ns, f, _ = wg.shape
    tm = _row_tile(t)

    def epilogue(accs, ex):
        a, b = accs
        sg = _sigmoid(a)
        act = a * sg
        return act, b * (sg * (1.0 + a * (1.0 - sg))), act * b

    out = (jax.ShapeDtypeStruct((ns, t, f), BF16), pl.BlockSpec((ns, tm, f), lambda i: (0, i, 0)), _store_shard)
    return _mm_shards(name, h, [wg, wu], NT, [out] * 3, epilogue, deps=deps)


def _shard_rows(arr, tm):
    ns, _, f = arr.shape
    return arr, pl.BlockSpec((ns, tm, f), lambda i: (0, i, 0)), lambda ref, j: ref[j]


def _ffn_down(name, z, wd, x, g_next, deps=()):
    _, t, _ = z.shape
    d = wd.shape[2]
    tm = _row_tile(t)
    row = pl.BlockSpec((tm, d), lambda i: (i, 0))

    def epilogue(accs, ex):
        y = ex[0] + 0.5 * accs[0]
        return y, _rmsnorm(y, ex[1])

    return _mm_rows(
        name, [_shard_rows(z, tm)], [wd], NN, t,
        outs=[(jax.ShapeDtypeStruct((t, d), F32), row), (jax.ShapeDtypeStruct((t, d), BF16), row)],
        epilogue=epilogue,
        extras=[(x, row), (g_next, pl.BlockSpec((1, d), lambda i: (0, 0)))],
        deps=deps,
    )


def _ffn_down_loss(name, z, wd, x, target):
    _, t, _ = z.shape
    d = wd.shape[2]
    tm = _row_tile(t)
    nt = t // tm
    row = pl.BlockSpec((tm, d), lambda i: (i, 0))

    def epilogue(accs, ex):
        e = ex[0] + 0.5 * accs[0] - ex[1]
        dy = e * (1.0 / d)
        return dy, 0.5 * dy, jnp.sum(e * e, axis=0, keepdims=True)

    return _mm_rows(
        name, [_shard_rows(z, tm)], [wd], NN, t,
        outs=[(jax.ShapeDtypeStruct((t, d), F32), row), (jax.ShapeDtypeStruct((t, d), BF16), row),
              (jax.ShapeDtypeStruct((nt, 1, d), F32), pl.BlockSpec((None, 1, d), lambda i: (i, 0, 0)))],
        epilogue=epilogue,
        extras=[(x, row), (target, row)],
    )


def _ffn_bwd_act(name, dout, wd, act_a, dact_b, deps=()):
    t, d = dout.shape
    ns, f, _ = wd.shape
    tm = _row_tile(t)

    def epilogue(accs, ex):
        dz = accs[0]
        return dz * ex[1].astype(F32), dz * ex[0].astype(F32)

    act = pl.BlockSpec((ns, tm, f), lambda i: (0, i, 0))
    out = (jax.ShapeDtypeStruct((ns, t, f), BF16), act, _store_shard)
    return _mm_shards(name, dout, [wd], NT, [out] * 2, epilogue, extras=[(act_a, act), (dact_b, act)], deps=deps)


def _grad_w_shardrows(name, z, dout, deps=()):
    ns, t, f = z.shape
    d = dout.shape[1]
    tk = _k_tile(t)
    return _mm(
        name,
        ins=[(z, pl.BlockSpec((None, tk, f), lambda j, n, k: (j, k, 0))),
             (dout, pl.BlockSpec((tk, d), lambda j, n, k: (k, 0)))],
        terms=[(0, 0, 1, TN)],
        n_acc=1,
        grid=(ns, 1, t // tk),
        acc_shape=(f, d),
        outs=[(pltpu.HBM((ns, f, d), BF16), pl.BlockSpec((None, f, d), lambda j, n, k: (j, 0, 0)))],
        epilogue=lambda accs, ex: (accs[0],),
        deps=deps,
    )[0]


def _norm_bwd_outs(t, d, tm, copy_scale):
    row = pl.BlockSpec((tm, d), lambda i: (i, 0))
    outs = [(jax.ShapeDtypeStruct((t, d), F32), row)]
    if copy_scale is not None:
        outs.append((jax.ShapeDtypeStruct((t, d), BF16), row))
    outs.append((jax.ShapeDtypeStruct((t // tm, 1, d), F32), pl.BlockSpec((None, 1, d), lambda i: (i, 0, 0))))
    return row, outs


def _ffn_bwd_in(name, da, db, wg, wu, x, g, dres, copy_scale, deps=()):
    _, t, _ = da.shape
    d = wg.shape[2]
    tm = _row_tile(t)
    row, outs = _norm_bwd_outs(t, d, tm, copy_scale)
    return _mm_rows(
        name, [_shard_rows(da, tm), _shard_rows(db, tm)], [wg, wu], NN, t,
        outs=outs,
        epilogue=_norm_bwd_epilogue(copy_scale),
        extras=[(x, row), (g, pl.BlockSpec((1, d), lambda i: (0, 0))), (dres, row)],
        deps=deps,
    )


def _in_proj(name, h, w_in):
    t, d = h.shape
    ns, _, pj = w_in.shape
    tm = _row_tile(t)
    def store(ref, j, value):
        ref[:, j * pj:(j + 1) * pj] = value

    out = (jax.ShapeDtypeStruct((t, ns * pj), F32), pl.BlockSpec((tm, ns * pj), lambda i: (i, 0)), store)
    return _mm_shards(name, h, [w_in], NN, [out], lambda accs, ex: (accs[0],))[0]


def _in_proj_bwd(name, dp, w_in, x, g, dres, copy_scale, deps=()):
    t = dp.shape[0]
    ns, d, pj = w_in.shape
    tm = _row_tile(t)
    row, outs = _norm_bwd_outs(t, d, tm, copy_scale)
    cols = (dp, pl.BlockSpec((tm, ns * pj), lambda i: (i, 0)), lambda ref, j: ref[:, j * pj:(j + 1) * pj])
    return _mm_rows(
        name, [cols], [w_in], NT, t,
        outs=outs,
        epilogue=_norm_bwd_epilogue(copy_scale),
        extras=[(x, row), (g, pl.BlockSpec((1, d), lambda i: (0, 0))), (dres, row)],
        deps=deps,
    )


def _grad_w_in(name, h, dp, ns):
    t, d = h.shape
    pj = dp.shape[1] // ns
    tk = _k_tile(t)
    return _mm(
        name,
        ins=[(h, pl.BlockSpec((tk, d), lambda j, n, k: (k, 0))),
             (dp, pl.BlockSpec((tk, pj), lambda j, n, k: (k, j)))],
        terms=[(0, 0, 1, TN)],
        n_acc=1,
        grid=(ns, 1, t // tk),
        acc_shape=(d, pj),
        outs=[(pltpu.HBM((ns, d, pj), BF16), pl.BlockSpec((None, d, pj), lambda j, n, k: (j, 0, 0)))],
        epilogue=lambda accs, ex: (accs[0],),
    )[0]


def _out_proj(name, mix, w_out, x, g_next):
    t, dm = mix.shape
    d = w_out.shape[1]
    tm = _row_tile(t)
    row = pl.BlockSpec((tm, d), lambda i, n, k: (i, 0))
    return _mm(
        name,
        ins=[(mix, pl.BlockSpec((tm, dm), lambda i, n, k: (i, 0))),
             (w_out, pl.BlockSpec((dm, d), lambda i, n, k: (0, 0)))],
        terms=[(0, 0, 1, NN)],
        n_acc=1,
        grid=(t // tm, 1, 1),
        acc_shape=(tm, d),
        outs=[(jax.ShapeDtypeStruct((t, d), F32), row), (jax.ShapeDtypeStruct((t, d), BF16), row)],
        epilogue=lambda accs, ex: (ex[0] + accs[0], _rmsnorm(ex[0] + accs[0], ex[1])),
        extras=[(x, row), (g_next, pl.BlockSpec((1, d), lambda i, n, k: (0, 0)))],
    )


def _out_proj_bwd(name, dx, w_out, deps=()):
    t, d = dx.shape
    dm = w_out.shape[0]
    tm = _row_tile(t)
    return _mm(
        name,
        ins=[(dx, pl.BlockSpec((tm, d), lambda i, n, k: (i, 0))),
             (w_out, pl.BlockSpec((dm, d), lambda i, n, k: (0, 0)))],
        terms=[(0, 0, 1, NT)],
        n_acc=1,
        grid=(t // tm, 1, 1),
        acc_shape=(tm, dm),
        outs=[(jax.ShapeDtypeStruct((t, dm), F32), pl.BlockSpec((tm, dm), lambda i, n, k: (i, 0)))],
        epilogue=lambda accs, ex: (accs[0],),
        deps=deps,
    )[0]


def _grad_w_out(name, mix, dx):
    t, dm = mix.shape
    d = dx.shape[1]
    tk = _k_tile(t)
    return _mm(
        name,
        ins=[(mix, pl.BlockSpec((tk, dm), lambda a, n, k: (k, 0))),
             (dx, pl.BlockSpec((tk, d), lambda a, n, k: (k, 0)))],
        terms=[(0, 0, 1, TN)],
        n_acc=1,
        grid=(1, 1, t // tk),
        acc_shape=(dm, d),
        outs=[(pltpu.HBM((dm, d), BF16), pl.BlockSpec((dm, d), lambda a, n, k: (0, 0)))],
        epilogue=lambda accs, ex: (accs[0],),
    )[0]


def _head_group_matrix():
    r = lax.broadcasted_iota(jnp.int32, (ATTN_W, ATTN_W), 0)
    c = lax.broadcasted_iota(jnp.int32, (ATTN_W, ATTN_W), 1)
    same = jnp.right_shift(r, 6) == jnp.right_shift(c, 6)
    return jnp.where(same, 1.0, 0.0).astype(BF16)


def _qk_prep(name, proj, gq, gk):
    b, s, _ = proj.shape
    tm = KPAD
    nb = s // tm

    def body(q_ref, k_ref, v_ref, gq_ref, gk_ref, qn_ref, kn_ref, vb_ref):
        j = pl.program_id(1)
        bd = _head_group_matrix()

        def norm(xv, g):
            ms = _dot_exact_rhs(xv * xv, bd, pieces=2) * (1.0 / ATTN_DH)
            return xv * lax.rsqrt(ms + RMS_EPS) * g

        @pl.when(j == 0)
        def _():
            kn_ref[...] = jnp.zeros_like(kn_ref)
            vb_ref[...] = jnp.zeros_like(vb_ref)

        @pl.when(j > 0)
        def _():
            qn_ref[...] = norm(q_ref[...], gq_ref[...]).astype(BF16)
            kn_ref[...] = norm(k_ref[...], gk_ref[...]).astype(BF16)
            vb_ref[...] = v_ref[...].astype(BF16)

    src_blk = lambda col: pl.BlockSpec((None, tm, ATTN_W), lambda bi, j: (bi, jnp.maximum(j - 1, 0), col))
    gspec = pl.BlockSpec((1, ATTN_W), lambda bi, j: (0, 0))
    padded = pl.BlockSpec((None, tm, ATTN_W), lambda bi, j: (bi, j, 0))
    return pl.pallas_call(
        body,
        name=name,
        grid=(b, nb + 1),
        in_specs=[src_blk(0), src_blk(1), src_blk(2), gspec, gspec],
        out_specs=[src_blk(0), padded, padded],
        out_shape=[jax.ShapeDtypeStruct((b, s, ATTN_W), BF16), jax.ShapeDtypeStruct((b, KPAD + s, ATTN_W), BF16),
                   jax.ShapeDtypeStruct((b, KPAD + s, ATTN_W), BF16)],
        compiler_params=_params("parallel", "arbitrary"),
    )(proj, proj, proj, gq, gk)


def _qk_prep_bwd(name, proj, dqn, dkn, dv, gq, gk):
    b, s, _ = proj.shape
    tm = KPAD
    nb = s // tm

    def body(q_ref, k_ref, dqn_ref, dkn_ref, dv_ref, gq_ref, gk_ref, dq_ref, dk_ref, dvb_ref, dgq_ref, dgk_ref):
        bd = _head_group_matrix()

        def bwd(xv, dy, g):
            ms = _dot_exact_rhs(xv * xv, bd, pieces=2) * (1.0 / ATTN_DH)
            rstd = lax.rsqrt(ms + RMS_EPS)
            xhat = xv * rstd
            dxhat = dy * g
            gm = _dot_exact_rhs(dxhat * xhat, bd, pieces=2) * (1.0 / ATTN_DH)
            return rstd * (dxhat - xhat * gm), jnp.sum(dy * xhat, axis=0, keepdims=True)

        dq, dgq = bwd(q_ref[...], dqn_ref[...], gq_ref[...])
        dk, dgk = bwd(k_ref[...], dkn_ref[...], gk_ref[...])
        dq_ref[...] = dq.astype(BF16)
        dk_ref[...] = dk.astype(BF16)
        dvb_ref[...] = dv_ref[...].astype(BF16)
        dgq_ref[...] = dgq
        dgk_ref[...] = dgk

    col = lambda c: pl.BlockSpec((None, tm, ATTN_W), lambda bi, j: (bi, j, c))
    past_pad = pl.BlockSpec((None, tm, ATTN_W), lambda bi, j: (bi, j + 1, 0))
    gspec = pl.BlockSpec((1, ATTN_W), lambda bi, j: (0, 0))
    pspec = pl.BlockSpec((None, 1, ATTN_W), lambda bi, j: (bi * nb + j, 0, 0))
    o_shape = jax.ShapeDtypeStruct((b, s, ATTN_W), BF16)
    p_shape = jax.ShapeDtypeStruct((b * nb, 1, ATTN_W), F32)
    return pl.pallas_call(
        body,
        name=name,
        grid=(b, nb),
        in_specs=[col(0), col(1), col(0), past_pad, past_pad, gspec, gspec],
        out_specs=[col(0)] * 3 + [pspec] * 2,
        out_shape=[o_shape] * 3 + [p_shape] * 2,
        compiler_params=_params("parallel", "parallel"),
    )(proj, proj, dqn, dkn, dv, gq, gk)


Q_CHUNKS = 4
QBLK = Q_CHUNKS * CHUNK
WIN = (LEFT_CHUNKS + Q_CHUNKS) * CHUNK
DB_W = BAND + CHUNK
MASKED = -1e30


def _band_table(bias):
    rows = [jnp.pad(bias, ((0, 0), (0, 0), (CHUNK * i, WIN - BAND - CHUNK * i)), constant_values=MASKED)
            for i in range(Q_CHUNKS)]
    return jnp.concatenate(rows, axis=1)


def _head_lanes(hh):
    lane = lax.broadcasted_iota(jnp.int32, (1, LANES), 1)
    return (lane < ATTN_DH) if hh == 0 else (lane >= ATTN_DH)


def _attn_probs(qh, kw, table, start):
    s = _dot(qh, kw, NT) * (ATTN_DH ** -0.5) + table
    col = lax.broadcasted_iota(jnp.int32, (QBLK, WIN), 1)
    s = jnp.where(col + start >= KPAD, s, MASKED)
    m = jnp.max(s, axis=-1, keepdims=True)
    p = jnp.exp(s - m)
    return p * (1.0 / jnp.sum(p, axis=-1, keepdims=True))


def _attn_fwd(name, q, k, v, table, deps=()):
    b, s, w = q.shape
    sp = k.shape[1]

    def body(q_ref, k_ref, v_ref, t_ref, *rest):
        o_ref = rest[-1]
        start = pl.multiple_of(pl.program_id(2) * QBLK, QBLK)
        kw = k_ref[pl.ds(start, WIN), :]
        vw = v_ref[pl.ds(start, WIN), :]
        q2 = q_ref[...]
        lanes = [_head_lanes(hh) for hh in range(2)]
        probs = [_attn_probs(jnp.where(mine, q2, jnp.zeros_like(q2)), kw, t_ref[hh], start).astype(BF16)
                 for hh, mine in enumerate(lanes)]
        outs = [_dot(p, vw) for p in probs]
        o_ref[...] = jnp.where(lanes[0], outs[0], outs[1]).astype(BF16)

    qspec = pl.BlockSpec((None, QBLK, LANES), lambda p, bi, i: (bi, i, p))
    kspec = pl.BlockSpec((None, sp, LANES), lambda p, bi, i: (bi, 0, p))
    return pl.pallas_call(
        body,
        name=name,
        grid=(w // LANES, b, s // QBLK),
        in_specs=[qspec, kspec, kspec, pl.BlockSpec((2, QBLK, WIN), lambda p, bi, i: (p, 0, 0))] + [ANY] * len(deps),
        out_specs=qspec,
        out_shape=jax.ShapeDtypeStruct((b, s, w), BF16),
        compiler_params=_params("parallel", "parallel", "arbitrary"),
    )(q, k, v, table, *deps)


def _attn_bwd(name, q, k, v, table, dmix):
    b, s, w = q.shape
    sp = k.shape[1]

    def body(q_ref, k_ref, v_ref, t_ref, do_ref, dq_ref, dk_ref, dv_ref, dbe_ref, dbo_ref):
        bi = pl.program_id(1)
        i = pl.program_id(2)
        start = pl.multiple_of(i * QBLK, QBLK)
        win = pl.ds(start, WIN)

        @pl.when(i == 0)
        def _():
            dk_ref[...] = jnp.zeros_like(dk_ref)
            dv_ref[...] = jnp.zeros_like(dv_ref)

        @pl.when(jnp.logical_and(i == 0, bi == 0))
        def _():
            dbe_ref[...] = jnp.zeros_like(dbe_ref)
            dbo_ref[...] = jnp.zeros_like(dbo_ref)

        kw = k_ref[win, :]
        vw = v_ref[win, :]
        q2 = q_ref[...]
        do2 = do_ref[...].astype(BF16)
        lanes = [_head_lanes(hh) for hh in range(2)]
        qh = [jnp.where(mine, q2, jnp.zeros_like(q2)) for mine in lanes]
        doh = [jnp.where(mine, do2, jnp.zeros_like(do2)) for mine in lanes]
        p = [_attn_probs(qh[hh], kw, t_ref[hh], start) for hh in range(2)]
        dp = [_dot(doh[hh], vw, NT) for hh in range(2)]
        ds = [p[hh] * (dp[hh] - jnp.sum(p[hh] * dp[hh], axis=-1, keepdims=True)) for hh in range(2)]
        dsb = [(x * (ATTN_DH ** -0.5)).astype(BF16) for x in ds]
        pb = [x.astype(BF16) for x in p]
        dq = [_dot(dsb[hh], kw) for hh in range(2)]
        dk = [_dot(dsb[hh], qh[hh], TN) for hh in range(2)]
        dv = [_dot(pb[hh], doh[hh], TN) for hh in range(2)]
        for hh in range(2):
            for qi in range(Q_CHUNKS):
                c0 = (qi // 2) * LANES
                blk = ds[hh][qi * CHUNK:(qi + 1) * CHUNK, c0:c0 + DB_W]
                if qi % 2 == 0:
                    dbe_ref[hh] += blk
                else:
                    dbo_ref[hh] += blk
        dq_ref[...] = jnp.where(lanes[0], dq[0], dq[1])
        dk_ref[win, :] += dk[0] + dk[1]
        dv_ref[win, :] += dv[0] + dv[1]

    qspec = pl.BlockSpec((None, QBLK, LANES), lambda p, bi, i: (bi, i, p))
    kspec = pl.BlockSpec((None, sp, LANES), lambda p, bi, i: (bi, 0, p))
    dbspec = pl.BlockSpec((2, CHUNK, DB_W), lambda p, bi, i: (p, 0, 0))
    db_shape = jax.ShapeDtypeStruct((ATTN_HEADS, CHUNK, DB_W), F32)
    return pl.pallas_call(
        body,
        name=name,
        grid=(w // LANES, b, s // QBLK),
        in_specs=[qspec, kspec, kspec, pl.BlockSpec((2, QBLK, WIN), lambda p, bi, i: (p, 0, 0)), qspec],
        out_specs=[qspec, kspec, kspec, dbspec, dbspec],
        out_shape=[jax.ShapeDtypeStruct((b, s, w), F32), jax.ShapeDtypeStruct((b, sp, w), F32),
                   jax.ShapeDtypeStruct((b, sp, w), F32), db_shape, db_shape],
        compiler_params=_params("arbitrary", "arbitrary", "arbitrary"),
    )(q, k, v, table, dmix)


HQ_COL = 3 * ATTN_W // HGRN_DH
HF_COL = HQ_COL + HGRN_HEADS
HI_COL = HF_COL + HGRN_HEADS
HG_COL = HI_COL + HGRN_HEADS
HGRN_ROWS = 8 * CHUNK
HEAD_LANES = [slice(hh * HGRN_DH, (hh + 1) * HGRN_DH) for hh in range(HGRN_HEADS)]


def _tri(lower):
    r = lax.broadcasted_iota(jnp.int32, (CHUNK, CHUNK), 0)
    c = lax.broadcasted_iota(jnp.int32, (CHUNK, CHUNK), 1)
    return (r >= c) if lower else (r <= c)


def _hgrn_chunk(hq, hf, lb, tril):
    sig = _sigmoid(hf)
    f = lb + (1.0 - lb) * sig
    g = jnp.log(f)
    ones_l = jnp.where(tril, 1.0, 0.0).astype(BF16)
    b = _dot_exact_lhs(ones_l, g)
    bl = jnp.sum(g, axis=0, keepdims=True)
    rows = lax.broadcasted_iota(jnp.int32, g.shape, 0)
    bm = jnp.sum(jnp.where(rows <= CHUNK // 2, g, 0.0), axis=0, keepdims=True)
    sq = _sigmoid(hq)
    q = hq * sq
    k = 1.0 - f
    return sig, f, b, bl, bm, sq, q, k


def _hgrn_fwd(name, proj, attn, lb, go, b, s):
    nc = s // CHUNK
    t = b * s
    nblk = s // HGRN_ROWS
    cpb = HGRN_ROWS // CHUNK

    def body(hq_ref, hf_ref, hi_ref, hg_ref, attn_ref, lb_ref, go_ref, mix_ref, oraw_ref, st_ref, s_scr):
        tril = _tri(True)
        gov = go_ref[...]
        mix_ref[:, 0:ATTN_W] = attn_ref[...]

        @pl.when(pl.program_id(1) == 0)
        def _():
            s_scr[...] = jnp.zeros_like(s_scr)

        def step(c, carry):
            sl = pl.ds(pl.multiple_of(c * CHUNK, CHUNK), CHUNK)
            hg = hg_ref[sl, :]
            _, _, bb, bl, bm, _, q, k = _hgrn_chunk(hq_ref[sl, :], hf_ref[sl, :], lb_ref[...], tril)
            vb = hi_ref[sl, :].astype(BF16)
            qe = (q * jnp.exp(bb - bm)).astype(BF16)
            ke = (k * jnp.exp(bm - bb)).astype(BF16)
            qb = (q * jnp.exp(bb)).astype(BF16)
            kb = (k * jnp.exp(bl - bb)).astype(BF16)
            e_last = jnp.exp(bl)
            gate = _silu(hg)
            st = [s_scr[hh] for hh in range(HGRN_HEADS)]
            a = [jnp.where(tril, _dot(qe[:, hs], ke[:, hs], NT), 0.0).astype(BF16) for hs in HEAD_LANES]
            o_state = [_dot(qb[:, hs], st[hh].astype(BF16), NT) for hh, hs in enumerate(HEAD_LANES)]
            st_next = [st[hh] * e_last[:, hs] + _dot(vb[:, hs], kb[:, hs], TN) for hh, hs in enumerate(HEAD_LANES)]
            o = [_dot(a[hh], vb[:, hs]) + o_state[hh] for hh, hs in enumerate(HEAD_LANES)]
            ro = [(oh * lax.rsqrt(jnp.mean(oh * oh, axis=-1, keepdims=True) + RMS_EPS) * gov) * gate[:, hs]
                  for oh, hs in zip(o, HEAD_LANES)]
            for hh in range(HGRN_HEADS):
                st_ref[hh, c] = st[hh]
                s_scr[hh] = st_next[hh]
            mix_ref[sl, ATTN_W:ATTN_W + HGRN_W] = jnp.concatenate(ro, axis=1).astype(BF16)
            oraw_ref[sl, :] = jnp.concatenate(o, axis=1)
            return carry

        lax.fori_loop(0, cpb, step, 0)

    col = lambda base: pl.BlockSpec((HGRN_ROWS, HGRN_W), lambda bi, i: (bi * nblk + i, base // HGRN_HEADS))
    out = pl.BlockSpec((HGRN_ROWS, HGRN_W), lambda bi, i: (bi * nblk + i, 0))
    return pl.pallas_call(
        body,
        name=name,
        grid=(b, nblk),
        in_specs=[col(HQ_COL), col(HF_COL), col(HI_COL), col(HG_COL), out,
                  pl.BlockSpec((1, HGRN_W), lambda bi, i: (0, 0)), pl.BlockSpec((1, HGRN_DH), lambda bi, i: (0, 0))],
        out_specs=[pl.BlockSpec((HGRN_ROWS, ATTN_W + HGRN_W), lambda bi, i: (bi * nblk + i, 0)), out,
                   pl.BlockSpec((None, HGRN_HEADS, cpb, HGRN_DH, HGRN_DH), lambda bi, i: (bi, 0, i, 0, 0))],
        out_shape=[jax.ShapeDtypeStruct((t, ATTN_W + HGRN_W), BF16), jax.ShapeDtypeStruct((t, HGRN_W), F32),
                   jax.ShapeDtypeStruct((b, HGRN_HEADS, nc, HGRN_DH, HGRN_DH), F32)],
        scratch_shapes=[pltpu.VMEM((HGRN_HEADS, HGRN_DH, HGRN_DH), F32)],
        compiler_params=_params("parallel", "arbitrary"),
    )(proj, proj, proj, proj, attn, lb, go)


def _hgrn_bwd(name, proj, dqkv, lb, go, oraw, states, dmix, b, s):
    t = b * s
    nblk = s // HGRN_ROWS
    cpb = HGRN_ROWS // CHUNK

    def body(hq_ref, hf_ref, hi_ref, hg_ref, dq_ref, dk_ref, dv_ref, lb_ref, go_ref, oraw_ref, st_ref, dro_ref,
             dp_ref, dlb_ref, dgo_ref, ds_scr, dlb_scr, dgo_scr):
        tril = _tri(True)
        ones_u = jnp.where(_tri(False), 1.0, 0.0).astype(BF16)
        gov = go_ref[...]
        dp_ref[:, 0:ATTN_W] = dq_ref[...]
        dp_ref[:, ATTN_W:2 * ATTN_W] = dk_ref[...]
        dp_ref[:, 2 * ATTN_W:3 * ATTN_W] = dv_ref[...]

        @pl.when(pl.program_id(1) == 0)
        def _():
            ds_scr[...] = jnp.zeros_like(ds_scr)
            dlb_scr[...] = jnp.zeros_like(dlb_scr)
            dgo_scr[...] = jnp.zeros_like(dgo_scr)

        def step(ci, carry):
            c = cpb - 1 - ci
            sl = pl.ds(pl.multiple_of(c * CHUNK, CHUNK), CHUNK)
            hq = hq_ref[sl, :]
            hg = hg_ref[sl, :]
            sig, f, bb, bl, bm, sq, q, k = _hgrn_chunk(hq, hf_ref[sl, :], lb_ref[...], tril)
            vb = hi_ref[sl, :].astype(BF16)
            ebm = jnp.exp(bb - bm)
            embm = jnp.exp(bm - bb)
            eb = jnp.exp(bb)
            ebl = jnp.exp(bl - bb)
            e_last = jnp.exp(bl)
            qe = (q * ebm).astype(BF16)
            ke = (k * embm).astype(BF16)
            qb = (q * eb).astype(BF16)
            kb = (k * ebl).astype(BF16)
            st = [st_ref[hh, c] for hh in range(HGRN_HEADS)]
            dst = [ds_scr[hh] for hh in range(HGRN_HEADS)]
            o = oraw_ref[sl, :]
            dro = dro_ref[sl, :]
            sg = _sigmoid(hg)
            gov4 = jnp.concatenate([gov] * HGRN_HEADS, axis=1)
            rstd = jnp.concatenate(
                [jnp.broadcast_to(lax.rsqrt(jnp.mean(o[:, hs] * o[:, hs], axis=-1, keepdims=True) + RMS_EPS),
                                  (CHUNK, HGRN_DH)) for hs in HEAD_LANES], axis=1)
            ohat = o * rstd
            dn = dro * (hg * sg)
            dhg = dro * (ohat * gov4) * (sg * (1.0 + hg * (1.0 - sg)))
            dgo_inc = jnp.sum(dn * ohat, axis=0, keepdims=True)
            dohat = dn * gov4
            proj_h = dohat * ohat
            pm = jnp.concatenate(
                [jnp.broadcast_to(jnp.mean(proj_h[:, hs], axis=-1, keepdims=True), (CHUNK, HGRN_DH))
                 for hs in HEAD_LANES], axis=1)
            dob = (rstd * (dohat - ohat * pm)).astype(BF16)
            stb = [x.astype(BF16) for x in st]
            dstb = [x.astype(BF16) for x in dst]
            a = [jnp.where(tril, _dot(qe[:, hs], ke[:, hs], NT), 0.0).astype(BF16) for hs in HEAD_LANES]
            dab = [jnp.where(tril, _dot(dob[:, hs], vb[:, hs], NT), 0.0).astype(BF16) for hs in HEAD_LANES]
            dqb = [_dot(dob[:, hs], stb[hh]) for hh, hs in enumerate(HEAD_LANES)]
            dkb = [_dot(vb[:, hs], dstb[hh]) for hh, hs in enumerate(HEAD_LANES)]
            dv_state = [_dot(kb[:, hs], dstb[hh], NT) for hh, hs in enumerate(HEAD_LANES)]
            dst_next = [dst[hh] * e_last[:, hs] + _dot(dob[:, hs], qb[:, hs], TN) for hh, hs in enumerate(HEAD_LANES)]
            dv = [_dot(a[hh], dob[:, hs], TN) + dv_state[hh] for hh, hs in enumerate(HEAD_LANES)]
            dqe = jnp.concatenate([_dot(dab[hh], ke[:, hs]) for hh, hs in enumerate(HEAD_LANES)], axis=1)
            dke = jnp.concatenate([_dot(dab[hh], qe[:, hs], TN) for hh, hs in enumerate(HEAD_LANES)], axis=1)
            dqb = jnp.concatenate(dqb, axis=1)
            dkb = jnp.concatenate(dkb, axis=1)
            state_term = jnp.concatenate(
                [jnp.sum(dst[hh] * st[hh], axis=0, keepdims=True) for hh in range(HGRN_HEADS)], axis=1)
            dq = dqe * ebm + dqb * eb
            dk = dke * embm + dkb * ebl
            db = (qe.astype(F32) * dqe - ke.astype(F32) * dke) + q * (dqb * eb) - k * (dkb * ebl)
            d_last = jnp.sum(k * ebl * dkb, axis=0, keepdims=True) + state_term * e_last
            dg = _dot_exact_lhs(ones_u, db) + d_last
            df = dg / f - dk
            first = HQ_COL * HGRN_DH
            dp_ref[sl, first:first + HGRN_W] = (dq * (sq * (1.0 + hq * (1.0 - sq)))).astype(BF16)
            dp_ref[sl, first + HGRN_W:first + 2 * HGRN_W] = (df * (1.0 - lb_ref[...]) * sig * (1.0 - sig)).astype(BF16)
            dp_ref[sl, first + 2 * HGRN_W:first + 3 * HGRN_W] = jnp.concatenate(dv, axis=1).astype(BF16)
            dp_ref[sl, first + 3 * HGRN_W:first + 4 * HGRN_W] = dhg.astype(BF16)
            dlb_scr[...] += jnp.sum(df * (1.0 - sig), axis=0, keepdims=True)
            dgo_scr[...] += dgo_inc
            for hh in range(HGRN_HEADS):
                ds_scr[hh] = dst_next[hh]
            return carry

        lax.fori_loop(0, cpb, step, 0)

        @pl.when(pl.program_id(1) == nblk - 1)
        def _():
            dlb_ref[...] = dlb_scr[...]
            dgo_ref[...] = dgo_scr[...]

    rows = lambda bi, i: bi * nblk + (nblk - 1 - i)
    col = lambda base: pl.BlockSpec((HGRN_ROWS, HGRN_W), lambda bi, i: (rows(bi, i), base // HGRN_HEADS))
    out = pl.BlockSpec((HGRN_ROWS, HGRN_W), lambda bi, i: (rows(bi, i), 0))
    part = pl.BlockSpec((None, 1, HGRN_W), lambda bi, i: (bi, 0, 0))
    width = HG_COL * HGRN_DH + HGRN_W
    o_shape = jax.ShapeDtypeStruct((t, width), BF16)
    p_shape = jax.ShapeDtypeStruct((b, 1, HGRN_W), F32)
    return pl.pallas_call(
        body,
        name=name,
        grid=(b, nblk),
        in_specs=[col(HQ_COL), col(HF_COL), col(HI_COL), col(HG_COL), out, out, out,
                  pl.BlockSpec((1, HGRN_W), lambda bi, i: (0, 0)), pl.BlockSpec((1, HGRN_DH), lambda bi, i: (0, 0)), out,
                  pl.BlockSpec((None, HGRN_HEADS, cpb, HGRN_DH, HGRN_DH), lambda bi, i: (bi, 0, nblk - 1 - i, 0, 0)),
                  col(ATTN_W // HGRN_DH)],
        out_specs=[pl.BlockSpec((HGRN_ROWS, width), lambda bi, i: (rows(bi, i), 0))] + [part] * 2,
        out_shape=[o_shape] + [p_shape] * 2,
        scratch_shapes=[pltpu.VMEM((HGRN_HEADS, HGRN_DH, HGRN_DH), F32), pltpu.VMEM((1, HGRN_W), F32),
                        pltpu.VMEM((1, HGRN_W), F32)],
        compiler_params=_params("parallel", "arbitrary"),
    )(proj, proj, proj, proj, *dqkv, lb, go, oraw, states, dmix)


def _small_grads(name, dg1, dgm, dg2, dgq, dgk, dbias_t, dlb, dgo, lbp):
    d = dg1.shape[1]

    def body(dg1_ref, dgm_ref, dg2_ref, dgq_ref, dgk_ref, dbias_ref, dlb_ref, dgo_ref, lbp_ref,
             g1_ref, gm_ref, g2_ref, gq_ref, gk_ref, rb_ref, lbg_ref, go_ref):
        g1_ref[...] = jnp.sum(dg1_ref[...], axis=0, keepdims=True)
        gm_ref[...] = jnp.sum(dgm_ref[...], axis=0, keepdims=True)
        g2_ref[...] = jnp.sum(dg2_ref[...], axis=0, keepdims=True)
        r = lax.broadcasted_iota(jnp.int32, (ATTN_W, ATTN_DH), 0)
        cidx = lax.broadcasted_iota(jnp.int32, (ATTN_W, ATTN_DH), 1)
        fold = jnp.where(jnp.bitwise_and(r, ATTN_DH - 1) == cidx, 1.0, 0.0).astype(BF16)
        gq_ref[...] = jnp.sum(_dot_exact_rhs(dgq_ref[...], fold), axis=0, keepdims=True)
        gk_ref[...] = jnp.sum(_dot_exact_rhs(dgk_ref[...], fold), axis=0, keepdims=True)
        gosum = jnp.sum(dgo_ref[...], axis=0, keepdims=True)
        go_ref[...] = (gosum[:, 0:HGRN_DH] + gosum[:, HGRN_DH:2 * HGRN_DH]
                       + gosum[:, 2 * HGRN_DH:3 * HGRN_DH] + gosum[:, 3 * HGRN_DH:4 * HGRN_DH])
        p0 = lbp_ref[0:1, :]
        p1 = lbp_ref[1:2, :]
        lbv = 1.0 / (1.0 + jnp.exp(p1 - p0))
        dp0 = jnp.sum(dlb_ref[...], axis=0, keepdims=True) * lbv * (1.0 - lbv)
        lbg_ref[0:1, :] = dp0
        lbg_ref[1:2, :] = -dp0
        sidx = lax.broadcasted_iota(jnp.int32, (BAND, N_REL_PAD), 0)
        ridx = lax.broadcasted_iota(jnp.int32, (BAND, N_REL_PAD), 1)

        def step(tq, acc):
            rel = jnp.clip(tq + KPAD - sidx, -REL_CLIP, REL_CLIP) + REL_CLIP
            onehot = jnp.where(rel == ridx, 1.0, 0.0).astype(BF16)
            return acc + _dot_exact_rhs(dbias_ref[tq], onehot)

        rb_ref[...] = lax.fori_loop(0, CHUNK, step, jnp.zeros((ATTN_HEADS, N_REL_PAD), F32))

    ins = [dg1, dgm, dg2, dgq, dgk, dbias_t, dlb, dgo, lbp]
    outs = [jax.ShapeDtypeStruct((1, d), F32)] * 3 + [jax.ShapeDtypeStruct((1, ATTN_DH), F32)] * 2 + [
        jax.ShapeDtypeStruct((ATTN_HEADS, N_REL_PAD), F32), jax.ShapeDtypeStruct((2, HGRN_W), F32),
        jax.ShapeDtypeStruct((1, HGRN_DH), F32)]
    vm = pl.BlockSpec(memory_space=pltpu.VMEM)
    return pl.pallas_call(
        body,
        name=name,
        in_specs=[vm] * len(ins),
        out_specs=[vm] * len(outs),
        out_shape=outs,
        compiler_params=pltpu.CompilerParams(vmem_limit_bytes=VMEM_LIMIT),
    )(*ins)


def _adam_update(w, g, m, v):
    m2 = ADAM_B1 * m + (1.0 - ADAM_B1) * g
    v2 = ADAM_B2 * v + (1.0 - ADAM_B2) * (g * g)
    m_hat = m2 / (1.0 - ADAM_B1 ** ADAM_STEP)
    v_hat = v2 / (1.0 - ADAM_B2 ** ADAM_STEP)
    delta = -ADAM_LR * (m_hat / (jnp.sqrt(v_hat) + ADAM_EPS) + ADAM_WD * w)
    return delta, m2, v2


def _rows_tile(r):
    return r if r <= 512 or r % 512 else 512


def _pair_sum(name, grad, theirs, core):
    n, half, c = theirs.shape
    tr = _rows_tile(half)
    nth = half // tr

    def body(core_ref, a_ref, b_ref, o_ref):
        o_ref[...] = (a_ref[...].astype(F32) + b_ref[...].astype(F32)).astype(o_ref.dtype)

    spec = pl.BlockSpec((None, tr, c), lambda i, j, core_ref: (i, j, 0))
    return pl.pallas_call(
        body, name=name,
        grid_spec=pltpu.PrefetchScalarGridSpec(
            num_scalar_prefetch=1, grid=(n, nth),
            in_specs=[pl.BlockSpec((None, tr, c), lambda i, j, core_ref: (i, core_ref[0] * nth + j, 0)), spec],
            out_specs=spec),
        out_shape=pltpu.HBM((n, half, c), BF16), compiler_params=_params("parallel", "parallel"),
    )(core, grad, theirs)


def _chip_sum(name, own, parts, chip):
    _, half, c = own.shape
    tr = _rows_tile(half)

    def body(chip_ref, own_ref, p_ref, o_ref):
        me = chip_ref[0]
        mine = own_ref[...].astype(F32)
        flip_x, flip_y, flip_xy = (p_ref[i].astype(F32) for i in range(3))
        acc = None
        for k in range(N_CHIPS):
            rel = jnp.bitwise_xor(me, k)
            term = jnp.where(rel == 0, mine, jnp.where(rel == 2, flip_x, jnp.where(rel == 1, flip_y, flip_xy)))
            acc = term if acc is None else acc + term
        o_ref[...] = acc

    return pl.pallas_call(
        body, name=name,
        grid_spec=pltpu.PrefetchScalarGridSpec(
            num_scalar_prefetch=1, grid=(half // tr,),
            in_specs=[pl.BlockSpec((None, tr, c), lambda j, chip_ref: (chip_ref[0], j, 0)),
                      pl.BlockSpec((3, tr, c), lambda j, chip_ref: (0, j, 0))],
            out_specs=pl.BlockSpec((tr, c), lambda j, chip_ref: (j, 0))),
        out_shape=pltpu.HBM((half, c), F32), compiler_params=_params("parallel"),
    )(chip, own, parts)


def _adamw(name, w, g_mine, g_theirs, m, v, core):
    _, r, c = w.shape
    half = r // 2
    tr = _rows_tile(half)
    nth = half // tr

    def body(core_ref, w_ref, gm_ref, gt_ref, m_ref, v_ref, g_ref, d_ref, m2_ref, v2_ref):
        g = jnp.where(pl.program_id(0) == core_ref[0], gm_ref[...], gt_ref[...])
        delta, m2, v2 = _adam_update(w_ref[...], g, m_ref[...], v_ref[...])
        g_ref[...] = g
        d_ref[...] = delta
        m2_ref[...] = m2
        v2_ref[...] = v2

    full = pl.BlockSpec((None, tr, c), lambda h, j, core_ref: (0, h * nth + j, 0))
    part = pl.BlockSpec((tr, c), lambda h, j, core_ref: (j, 0))
    shape = jax.ShapeDtypeStruct((1, r, c), F32)
    return pl.pallas_call(
        body, name=name,
        grid_spec=pltpu.PrefetchScalarGridSpec(
            num_scalar_prefetch=1, grid=(2, nth), in_specs=[full, part, part, full, full], out_specs=[full] * 4),
        out_shape=[shape] * 4, compiler_params=_params("parallel", "parallel"),
    )(core, w, g_mine, g_theirs, m, v)


def _rel_bias_table(name, rel_bias):
    padded = jnp.pad(rel_bias, ((0, 0), (0, N_REL_PAD - N_REL)))

    def body(rb_ref, o_ref):
        ridx = lax.broadcasted_iota(jnp.int32, (N_REL_PAD, BAND), 0)
        sidx = lax.broadcasted_iota(jnp.int32, (N_REL_PAD, BAND), 1)
        rb = rb_ref[...]

        def step(tq, carry):
            rel = jnp.clip(tq + KPAD - sidx, -REL_CLIP, REL_CLIP) + REL_CLIP
            onehot = jnp.where(rel == ridx, 1.0, 0.0).astype(BF16)
            o_ref[tq] = _dot_exact_rhs(rb, onehot)
            return carry

        lax.fori_loop(0, CHUNK, step, 0)

    vm = pl.BlockSpec(memory_space=pltpu.VMEM)
    table = pl.pallas_call(
        body, name=name, in_specs=[vm], out_specs=vm,
        out_shape=jax.ShapeDtypeStruct((CHUNK, ATTN_HEADS, BAND), F32),
    )(padded)
    return table.transpose(1, 0, 2)


def _adamw_small(name, w, parts, m, v):
    def body(w_ref, p_ref, m_ref, v_ref, g_ref, d_ref, m2_ref, v2_ref):
        g = p_ref[0]
        for i in range(1, N_DEV):
            g = g + p_ref[i]
        delta, m2, v2 = _adam_update(w_ref[...], g, m_ref[...], v_ref[...])
        g_ref[...] = g
        d_ref[...] = delta
        m2_ref[...] = m2
        v2_ref[...] = v2

    vm = pl.BlockSpec(memory_space=pltpu.VMEM)
    shape = jax.ShapeDtypeStruct((SMALL_ROWS, SMALL_COLS), F32)
    return pl.pallas_call(
        body, name=name, in_specs=[vm] * 4, out_specs=[vm] * 4, out_shape=[shape] * 4,
    )(w, parts, m, v)


def _position():
    return lax.axis_index("x"), lax.axis_index("y"), lax.axis_index("c")


def _other_chips(x, y):
    return [(1 - x, y), (x, 1 - y), (1 - x, 1 - y)]


ANY = pl.BlockSpec(memory_space=pl.ANY)
PAIR_ID = 0


def _pair_handshake():
    x, y, c = _position()
    barrier = pltpu.get_barrier_semaphore()
    pl.semaphore_signal(barrier, inc=1, device_id=(x, y, 1 - c), device_id_type=MESH)
    pl.semaphore_wait(barrier, 1)


PAIR_CALL = pltpu.CompilerParams(collective_id=PAIR_ID)


HBM = pl.BlockSpec(memory_space=pltpu.HBM)
SEM = pl.BlockSpec(memory_space=pltpu.SEMAPHORE)
SPLIT_COPY = pltpu.SideEffectType.DATAFLOW_SIDE_EFFECTING


def _gather_copy(shards, outs, send_sem, recv_sem, i, j):
    x, y, c = _position()
    chips = _other_chips(x, y)
    half = shards[i].shape[0] // 2
    rows = pl.ds(pl.multiple_of(c * half, 16), half)
    return pltpu.make_async_remote_copy(
        src_ref=shards[i].at[rows, :], dst_ref=outs[i].at[2 * x + y, rows, :],
        send_sem=send_sem.at[3 * i + j], recv_sem=recv_sem.at[3 * i + j],
        device_id=(chips[j][0], chips[j][1], c), device_id_type=MESH)


def _gather_start(name, shards, after):
    n = len(shards)

    def body(*refs):
        srcs, outs = refs[:n], refs[n:2 * n]
        send_sem, recv_sem = refs[2 * n + len(after)], refs[2 * n + len(after) + 1]
        token = refs[-1]
        for i in range(n):
            for j in range(3):
                _gather_copy(srcs, outs, send_sem, recv_sem, i, j).start()
        token[...] = jnp.zeros_like(token)

    full = [(N_CHIPS,) + s.shape for s in shards]
    res = pl.pallas_call(
        body,
        name=name,
        in_specs=[HBM] * (2 * n) + [ANY] * len(after),
        out_specs=[SEM, SEM] + [HBM] * (2 * n) + [pl.BlockSpec(memory_space=pltpu.VMEM)],
        out_shape=[pltpu.SemaphoreType.DMA((3 * n,)), pltpu.SemaphoreType.DMA((3 * n,))]
        + [pltpu.HBM(s.shape, s.dtype) for s in shards]
        + [pltpu.HBM(shp, s.dtype) for shp, s in zip(full, shards)]
        + [jax.ShapeDtypeStruct((8, LANES), F32)],
        input_output_aliases={i: 2 + i for i in range(2 * n)},
        compiler_params=pltpu.CompilerParams(has_side_effects=SPLIT_COPY),
    )(*[pltpu.with_memory_space_constraint(s, pltpu.HBM) for s in shards],
      *[pltpu.with_memory_space_constraint(lax.empty(shp, s.dtype), pltpu.HBM) for shp, s in zip(full, shards)],
      *after)
    return res[0], res[1], list(res[2:2 + n]), list(res[2 + n:2 + 2 * n]), res[-1]


def _gather_wait(name, send_sem, recv_sem, shards, outs, after):
    n = len(shards)

    def body(*refs):
        srcs, out_refs = refs[:n], refs[n:2 * n]
        send_ref, recv_ref = refs[2 * n], refs[2 * n + 1]
        for i in range(n):
            for j in range(3):
                copy = _gather_copy(srcs, out_refs, send_ref, recv_ref, i, j)
                copy.wait_send()
                copy.wait_recv()

    res = pl.pallas_call(
        body,
        name=name,
        in_specs=[HBM] * (2 * n) + [SEM, SEM] + [ANY] * len(after),
        out_specs=[HBM] * (2 * n),
        out_shape=[pltpu.HBM(s.shape, s.dtype) for s in shards] + [pltpu.HBM(o.shape, o.dtype) for o in outs],
        input_output_aliases={i: i for i in range(2 * n)},
        compiler_params=pltpu.CompilerParams(has_side_effects=SPLIT_COPY),
    )(*shards, *outs, send_sem, recv_sem, *after)
    return list(res[:n]), list(res[n:])


def _join_copies(srcs, ins, outs, own_send, own_recv, half_send, half_recv):
    x, y, c = _position()
    chips = _other_chips(x, y)
    copies = []
    for i in range(len(srcs)):
        copies.append(pltpu.make_async_remote_copy(
            src_ref=srcs[i], dst_ref=outs[i].at[2 * x + y], send_sem=own_send.at[i], recv_sem=own_recv.at[i],
            device_id=(x, y, 1 - c), device_id_type=MESH))
        half = srcs[i].shape[0] // 2
        rows = pl.ds(pl.multiple_of(c * half, 16), half)
        for j in range(3):
            slot = 2 * chips[j][0] + chips[j][1]
            copies.append(pltpu.make_async_remote_copy(
                src_ref=ins[i].at[slot, rows, :], dst_ref=outs[i].at[slot, rows, :],
                send_sem=half_send.at[3 * i + j], recv_sem=half_recv.at[3 * i + j],
                device_id=(x, y, 1 - c), device_id_type=MESH))
    return copies


def _gather_join(name, shards, outs):
    n = len(shards)

    def body(*refs):
        _pair_handshake()
        copies = _join_copies(refs[:n], refs[n:2 * n], refs[2 * n:3 * n], *refs[3 * n:])
        for cp in copies:
            cp.start()
        for cp in copies:
            cp.wait()

    return pl.pallas_call(
        body,
        name=name,
        in_specs=[ANY] * (2 * n),
        out_specs=[HBM] * n,
        out_shape=[pltpu.HBM(o.shape, o.dtype) for o in outs],
        input_output_aliases={n + i: i for i in range(n)},
        scratch_shapes=[pltpu.SemaphoreType.DMA((n,))] * 2 + [pltpu.SemaphoreType.DMA((3 * n,))] * 2,
        compiler_params=PAIR_CALL,
    )(*shards, *outs)


def _join_start(name, shards, outs):
    n = len(shards)

    def body(*refs):
        _pair_handshake()
        srcs, arrs = refs[:n], refs[n:2 * n]
        sems = refs[2 * n:2 * n + 4]
        token = refs[-1]
        for cp in _join_copies(srcs, arrs, arrs, *sems):
            cp.start()
        token[...] = jnp.zeros_like(token)

    res = pl.pallas_call(
        body,
        name=name,
        in_specs=[HBM] * (2 * n),
        out_specs=[SEM] * 4 + [HBM] * (2 * n) + [pl.BlockSpec(memory_space=pltpu.VMEM)],
        out_shape=[pltpu.SemaphoreType.DMA((n,))] * 2 + [pltpu.SemaphoreType.DMA((3 * n,))] * 2
        + [pltpu.HBM(s.shape, s.dtype) for s in shards] + [pltpu.HBM(o.shape, o.dtype) for o in outs]
        + [jax.ShapeDtypeStruct((8, LANES), F32)],
        input_output_aliases={i: 4 + i for i in range(2 * n)},
        compiler_params=pltpu.CompilerParams(has_side_effects=SPLIT_COPY, collective_id=PAIR_ID),
    )(*shards, *outs)
    return list(res[:4]), list(res[4:4 + n]), list(res[4 + n:4 + 2 * n]), res[-1]


def _join_wait(name, sems, shards, outs, after):
    n = len(shards)

    def body(*refs):
        srcs, arrs = refs[:n], refs[n:2 * n]
        for cp in _join_copies(srcs, arrs, arrs, *refs[2 * n:2 * n + 4]):
            cp.wait_send()
            cp.wait_recv()

    res = pl.pallas_call(
        body,
        name=name,
        in_specs=[HBM] * (2 * n) + [SEM] * 4 + [ANY] * len(after),
        out_specs=[HBM] * (2 * n),
        out_shape=[pltpu.HBM(s.shape, s.dtype) for s in shards] + [pltpu.HBM(o.shape, o.dtype) for o in outs],
        input_output_aliases={i: i for i in range(2 * n)},
        compiler_params=pltpu.CompilerParams(has_side_effects=SPLIT_COPY),
    )(*shards, *outs, *sems, *after)
    return list(res[n:])


def _pair_copy(grads, lands, send_sem, recv_sem, i):
    x, y, c = _position()
    half = grads[i].shape[1] // 2
    give = pl.ds(pl.multiple_of((1 - c) * half, 16), half)
    return pltpu.make_async_remote_copy(
        src_ref=grads[i].at[:, give, :], dst_ref=lands[i], send_sem=send_sem.at[i], recv_sem=recv_sem.at[i],
        device_id=(x, y, 1 - c), device_id_type=MESH)


def _pair_start(name, grads):
    n = len(grads)

    def body(*refs):
        _pair_handshake()
        srcs, lands = refs[:n], refs[n:2 * n]
        send_sem, recv_sem = refs[2 * n], refs[2 * n + 1]
        token = refs[-1]
        for i in range(n):
            _pair_copy(srcs, lands, send_sem, recv_sem, i).start()
        token[...] = jnp.zeros_like(token)

    halves = [(g.shape[0], g.shape[1] // 2, g.shape[2]) for g in grads]
    res = pl.pallas_call(
        body,
        name=name,
        in_specs=[HBM] * (2 * n),
        out_specs=[SEM, SEM] + [HBM] * (2 * n) + [pl.BlockSpec(memory_space=pltpu.VMEM)],
        out_shape=[pltpu.SemaphoreType.DMA((n,)), pltpu.SemaphoreType.DMA((n,))]
        + [pltpu.HBM(g.shape, g.dtype) for g in grads]
        + [pltpu.HBM(shp, g.dtype) for shp, g in zip(halves, grads)]
        + [jax.ShapeDtypeStruct((8, LANES), F32)],
        input_output_aliases={i: 2 + i for i in range(2 * n)},
        compiler_params=pltpu.CompilerParams(has_side_effects=SPLIT_COPY, collective_id=PAIR_ID),
    )(*[pltpu.with_memory_space_constraint(g, pltpu.HBM) for g in grads],
      *[pltpu.with_memory_space_constraint(lax.empty(shp, g.dtype), pltpu.HBM) for shp, g in zip(halves, grads)])
    return res[0], res[1], list(res[2:2 + n]), list(res[2 + n:2 + 2 * n]), res[-1]


def _pair_wait(name, send_sem, recv_sem, grads, lands, after):
    n = len(grads)

    def body(*refs):
        srcs, land_refs = refs[:n], refs[n:2 * n]
        send_ref, recv_ref = refs[2 * n], refs[2 * n + 1]
        for i in range(n):
            copy = _pair_copy(srcs, land_refs, send_ref, recv_ref, i)
            copy.wait_send()
            copy.wait_recv()

    res = pl.pallas_call(
        body,
        name=name,
        in_specs=[HBM] * (2 * n) + [SEM, SEM, ANY],
        out_specs=[HBM] * (2 * n),
        out_shape=[pltpu.HBM(g.shape, g.dtype) for g in grads] + [pltpu.HBM(l.shape, l.dtype) for l in lands],
        input_output_aliases={i: i for i in range(2 * n)},
        compiler_params=pltpu.CompilerParams(has_side_effects=SPLIT_COPY),
    )(*grads, *lands, send_sem, recv_sem, after)
    return list(res[:n]), list(res[n:])


def _scatter_copy(srcs, lands, send_sem, recv_sem, i, j):
    x, y, c = _position()
    chips = _other_chips(x, y)
    return pltpu.make_async_remote_copy(
        src_ref=srcs[i].at[2 * chips[j][0] + chips[j][1]], dst_ref=lands[i].at[j],
        send_sem=send_sem.at[3 * i + j], recv_sem=recv_sem.at[3 * i + j],
        device_id=(chips[j][0], chips[j][1], c), device_id_type=MESH)


def _scatter_start(name, sums):
    n = len(sums)

    def body(*refs):
        srcs, lands = refs[:n], refs[n:2 * n]
        send_sem, recv_sem = refs[2 * n], refs[2 * n + 1]
        token = refs[-1]
        for i in range(n):
            for j in range(3):
                _scatter_copy(srcs, lands, send_sem, recv_sem, i, j).start()
        token[...] = jnp.zeros_like(token)

    land_shapes = [(3,) + s.shape[1:] for s in sums]
    res = pl.pallas_call(
        body,
        name=name,
        in_specs=[HBM] * (2 * n),
        out_specs=[SEM, SEM] + [HBM] * (2 * n) + [pl.BlockSpec(memory_space=pltpu.VMEM)],
        out_shape=[pltpu.SemaphoreType.DMA((3 * n,)), pltpu.SemaphoreType.DMA((3 * n,))]
        + [pltpu.HBM(s.shape, s.dtype) for s in sums]
        + [pltpu.HBM(shp, s.dtype) for shp, s in zip(land_shapes, sums)]
        + [jax.ShapeDtypeStruct((8, LANES), F32)],
        input_output_aliases={i: 2 + i for i in range(2 * n)},
        compiler_params=pltpu.CompilerParams(has_side_effects=SPLIT_COPY),
    )(*[pltpu.with_memory_space_constraint(s, pltpu.HBM) for s in sums],
      *[pltpu.with_memory_space_constraint(lax.empty(shp, s.dtype), pltpu.HBM) for shp, s in zip(land_shapes, sums)])
    return res[0], res[1], list(res[2:2 + n]), list(res[2 + n:2 + 2 * n]), res[-1]


def _scatter_wait(name, send_sem, recv_sem, sums, lands, after):
    n = len(sums)

    def body(*refs):
        srcs, land_refs = refs[:n], refs[n:2 * n]
        send_ref, recv_ref = refs[2 * n], refs[2 * n + 1]
        for i in range(n):
            for j in range(3):
                copy = _scatter_copy(srcs, land_refs, send_ref, recv_ref, i, j)
                copy.wait_send()
                copy.wait_recv()

    res = pl.pallas_call(
        body,
        name=name,
        in_specs=[HBM] * (2 * n) + [SEM, SEM, ANY],
        out_specs=[HBM] * (2 * n),
        out_shape=[pltpu.HBM(s.shape, s.dtype) for s in sums] + [pltpu.HBM(l.shape, l.dtype) for l in lands],
        input_output_aliases={i: i for i in range(2 * n)},
        compiler_params=pltpu.CompilerParams(has_side_effects=SPLIT_COPY),
    )(*sums, *lands, send_sem, recv_sem, after)
    return list(res[:n]), list(res[n:])


def _pair_join(name, halves, small=None):
    n = len(halves)
    if small is None:
        def body_plain(*refs):
            _pair_handshake()
            ins, outs = refs[:n], refs[n:2 * n]
            send_sem, recv_sem = refs[2 * n:]
            x, y, c = _position()
            swaps = [pltpu.make_async_remote_copy(
                src_ref=ins[i], dst_ref=outs[i], send_sem=send_sem.at[i], recv_sem=recv_sem.at[i],
                device_id=(x, y, 1 - c), device_id_type=MESH) for i in range(n)]
            for swap in swaps:
                swap.start()
            for swap in swaps:
                swap.wait()

        return pl.pallas_call(
            body_plain,
            name=name,
            in_specs=[ANY] * n,
            out_specs=[ANY] * n,
            out_shape=[jax.ShapeDtypeStruct(h.shape, h.dtype) for h in halves],
            scratch_shapes=[pltpu.SemaphoreType.DMA((n,))] * 2,
            compiler_params=PAIR_CALL,
        )(*halves)

    def body(*refs):
        ins, small_ref = refs[:n], refs[n]
        outs, all_ref = refs[n + 1:2 * n + 1], refs[2 * n + 1]
        send_sem, recv_sem, sm_send, sm_recv, sm_local = refs[2 * n + 2:]
        x, y, c = _position()
        swaps = []
        for i in range(n):
            swap = pltpu.make_async_remote_copy(
                src_ref=ins[i], dst_ref=outs[i], send_sem=send_sem.at[i], recv_sem=recv_sem.at[i],
                device_id=(x, y, 1 - c), device_id_type=MESH)
            swap.start()
            swaps.append(swap)
        me = 4 * x + 2 * y + c
        sm_own = pltpu.make_async_copy(small_ref, all_ref.at[me], sm_local)
        sm_own.start()
        pushes, arrivals = [], []
        for mask in range(1, N_DEV):
            px, py, pc = x ^ (mask >> 2), y ^ ((mask >> 1) & 1), c ^ (mask & 1)
            pushes.append(pltpu.make_async_remote_copy(
                src_ref=small_ref, dst_ref=all_ref.at[me], send_sem=sm_send.at[mask - 1], recv_sem=sm_recv.at[mask - 1],
                device_id=(px, py, pc), device_id_type=MESH))
            arrivals.append(pltpu.make_async_remote_copy(
                src_ref=small_ref, dst_ref=all_ref.at[4 * px + 2 * py + pc], send_sem=sm_send.at[mask - 1],
                recv_sem=sm_recv.at[mask - 1], device_id=(px, py, pc), device_id_type=MESH))
        for cp in pushes:
            cp.start()
        for swap in swaps:
            swap.wait()
        for cp in arrivals:
            cp.wait_recv()
        for cp in pushes:
            cp.wait_send()
        sm_own.wait()

    res = pl.pallas_call(
        body,
        name=name,
        in_specs=[ANY] * (n + 1),
        out_specs=[ANY] * (n + 1),
        out_shape=[jax.ShapeDtypeStruct(h.shape, h.dtype) for h in halves]
        + [jax.ShapeDtypeStruct((N_DEV,) + small.shape, small.dtype)],
        scratch_shapes=[pltpu.SemaphoreType.DMA((n,))] * 2 + [pltpu.SemaphoreType.DMA((N_DEV - 1,))] * 2
        + [pltpu.SemaphoreType.DMA(())],
    )(*halves, small)
    return res[:n], res[n]


def _lower_bound(lbp):
    return jax.nn.softmax(lbp, axis=0)[0:1]


def _local_step(x, target, g1, gm, g2, gq, gk, go, rel_bias, lbp, weights, on_grads, grads_sent):
    b, s, d = x.shape
    t = b * s
    x0 = x.reshape(t, d)
    tgt = target.reshape(t, d)
    gq_t = jnp.tile(gq, (1, ATTN_HEADS))
    gk_t = jnp.tile(gk, (1, ATTN_HEADS))
    lb = _lower_bound(lbp)
    table = _band_table(_rel_bias_table("rel_bias_table", rel_bias))

    h1 = _rmsnorm_fwd("norm1", x0, g1)
    wg1, wu1, deps1 = weights["first"]((h1, table))
    a1, b1, z1 = _ffn_up("ffn1_up", h1, wg1, wu1, deps1)
    wd1, deps_mid = weights["mid"]((z1,))
    x1, h2 = _ffn_down("ffn1_down", z1, wd1, x0, gm, deps_mid)
    w_in, w_out = weights["mid_rest"]((x1,))
    ns = w_in.shape[0]
    proj = _in_proj("in_proj", h2, w_in)
    proj3 = proj.reshape(b, s, proj.shape[1])
    qn, kn, vb = _qk_prep("qk_prep", proj3, gq_t, gk_t)
    attn = _attn_fwd("attn_fwd", qn, kn, vb, table, weights["last_begin"]((qn,))).reshape(t, ATTN_W)
    mix, oraw, states = _hgrn_fwd("hgrn_fwd", proj, attn, lb, go, b, s)
    x2, h3 = _out_proj("out_proj", mix, w_out, x1, g2)
    wg2, wu2, wd2 = weights["last"]((h3,))
    a2, b2, z2 = _ffn_up("ffn2_up", h3, wg2, wu2)
    dy, dyh, sq = _ffn_down_loss("ffn2_down_loss", z2, wd2, x2, tgt)
    loss = 0.5 * jnp.sum(sq) / d

    da2, db2 = _ffn_bwd_act("ffn2_bwd_act", dyh, wd2, a2, b2)
    dwd2 = _grad_w_shardrows("ffn2_dwd", z2, dyh)
    dwg2 = _grad_w_shardrows("ffn2_dwg", da2, h3)
    dwu2 = _grad_w_shardrows("ffn2_dwu", db2, h3)
    sent2 = on_grads("ffn2", {"ffn2_w_gate": dwg2, "ffn2_w_up": dwu2, "ffn2_w_down": dwd2})
    dx2, dx2b, dg2 = _ffn_bwd_in("ffn2_bwd_in", da2, db2, wg2, wu2, x2, g2, dy, 1.0, sent2)
    sent2 = grads_sent("ffn2", dx2b)

    dwout = _grad_w_out("dw_out", mix, dx2b)
    dmix = _out_proj_bwd("out_proj_bwd", dx2b, w_out, sent2)
    dqn, dkn, dvn, dbe, dbo = _attn_bwd("attn_bwd", qn, kn, vb, table, dmix.reshape(b, s, dmix.shape[1]))
    dbias = dbe[:, :, :BAND] + dbo[:, :, CHUNK:]
    dpq, dpk, dpv, dgq, dgk = _qk_prep_bwd("qk_prep_bwd", proj3, dqn, dkn, dvn, gq_t, gk_t)
    dpq, dpk, dpv = (a.reshape(t, ATTN_W) for a in (dpq, dpk, dpv))
    dproj, dlb, dgo = _hgrn_bwd("hgrn_bwd", proj, (dpq, dpk, dpv), lb, go, oraw, states, dmix, b, s)
    dwin = _grad_w_in("dw_in", h2, dproj, ns)
    dx1, dx1h, dgm = _in_proj_bwd("in_proj_bwd", dproj, w_in, x1, gm, dx2, 0.5)

    dwd1 = _grad_w_shardrows("ffn1_dwd", z1, dx1h)
    sent_mix = on_grads("mix", {"w_in": dwin, "w_out": dwout.reshape(ns, dwout.shape[0] // ns, d),
                                "ffn1_w_down": dwd1})
    da1, db1 = _ffn_bwd_act("ffn1_bwd_act", dx1h, wd1, a1, b1, sent_mix)
    sent_mix = grads_sent("mix", da1)
    dwg1 = _grad_w_shardrows("ffn1_dwg", da1, h1, sent_mix)
    dwu1 = _grad_w_shardrows("ffn1_dwu", db1, h1)
    on_grads("ffn1", {"ffn1_w_gate": dwg1, "ffn1_w_up": dwu1})
    sent1 = grads_sent("ffn1", None)
    dx0, dg1 = _ffn_bwd_in("ffn1_bwd_in", da1, db1, wg1, wu1, x0, g1, dx1, None, sent1)

    nt = dg1.shape[0]
    sg = _small_grads(
        "small_grads", dg1.reshape(nt, d), dgm.reshape(nt, d), dg2.reshape(nt, d),
        dgq.reshape(-1, ATTN_W), dgk.reshape(-1, ATTN_W), dbias.transpose(1, 0, 2),
        dlb.reshape(b, HGRN_W), dgo.reshape(b, HGRN_W), lbp)
    g1g, gmg, g2g, gqg, gkg, rbg, lbg, gog = sg
    small = _pack_small(g1g, gmg, g2g, lbg, rbg[:, :N_REL], gqg, gkg, gog, loss)
    return dx0.reshape(b, s, d), small


LOSS_SLOT = 7 * SMALL_COLS + 2 * ATTN_DH + HGRN_DH


def _pack_small(g1, gm, g2, lbp, rel_bias, gq, gk, go, loss=None):
    flat = [g1.reshape(-1), gm.reshape(-1), g2.reshape(-1), lbp.reshape(-1), rel_bias.reshape(-1)]
    n_bias = 3 * SMALL_COLS - rel_bias.size
    heads = [gq.reshape(-1), gk.reshape(-1), go.reshape(-1)]
    heads.append(jnp.zeros((1,), F32) if loss is None else loss.reshape(1))
    n_tail = SMALL_COLS - sum(h.size for h in heads)
    return jnp.concatenate(flat + [jnp.zeros((n_bias,), F32)] + heads + [jnp.zeros((n_tail,), F32)]).reshape(
        SMALL_ROWS, SMALL_COLS)


def _unpack_small(p, d):
    flat = p.reshape(-1)
    o = 3 * d
    g1, gm, g2 = p[0:1], p[1:2], p[2:3]
    lbp = flat[o:o + 2 * HGRN_W].reshape(2, HGRN_W)
    o = 4 * SMALL_COLS
    rel = flat[o:o + ATTN_HEADS * N_REL].reshape(1, ATTN_HEADS, N_REL)
    o = 7 * SMALL_COLS
    gq = flat[o:o + ATTN_DH].reshape(1, ATTN_DH)
    gk = flat[o + ATTN_DH:o + 2 * ATTN_DH].reshape(1, ATTN_DH)
    go = flat[o + 2 * ATTN_DH:o + 2 * ATTN_DH + HGRN_DH].reshape(1, HGRN_DH)
    return g1, gm, g2, gq, gk, rel, lbp, go


def kernel(x, ffn1_norm_g, ffn1_w_gate, ffn1_w_up, ffn1_w_down, mix_norm_g, w_in, attn_q_norm_g, attn_k_norm_g, attn_rel_bias, hgrn_lower_bounds, hgrn_out_norm_g, w_out, ffn2_norm_g, ffn2_w_gate, ffn2_w_up, ffn2_w_down, loss_target, m_ffn1_norm_g, m_ffn1_w_gate, m_ffn1_w_up, m_ffn1_w_down, m_mix_norm_g, m_w_in, m_attn_q_norm_g, m_attn_k_norm_g, m_attn_rel_bias, m_hgrn_lower_bounds, m_hgrn_out_norm_g, m_w_out, m_ffn2_norm_g, m_ffn2_w_gate, m_ffn2_w_up, m_ffn2_w_down, v_ffn1_norm_g, v_ffn1_w_gate, v_ffn1_w_up, v_ffn1_w_down, v_mix_norm_g, v_w_in, v_attn_q_norm_g, v_attn_k_norm_g, v_attn_rel_bias, v_hgrn_lower_bounds, v_hgrn_out_norm_g, v_w_out, v_ffn2_norm_g, v_ffn2_w_gate, v_ffn2_w_up, v_ffn2_w_down):
    d = x.shape[-1]
    big_w = [ffn1_w_gate, ffn1_w_up, ffn1_w_down, w_in, w_out, ffn2_w_gate, ffn2_w_up, ffn2_w_down]
    big_m = [m_ffn1_w_gate, m_ffn1_w_up, m_ffn1_w_down, m_w_in, m_w_out, m_ffn2_w_gate, m_ffn2_w_up, m_ffn2_w_down]
    big_v = [v_ffn1_w_gate, v_ffn1_w_up, v_ffn1_w_down, v_w_in, v_w_out, v_ffn2_w_gate, v_ffn2_w_up, v_ffn2_w_down]
    big_names = ["ffn1_w_gate", "ffn1_w_up", "ffn1_w_down", "w_in", "w_out", "ffn2_w_gate", "ffn2_w_up", "ffn2_w_down"]
    flipped = {nm for nm in big_names if nm.endswith("gate") or nm.endswith("up")}
    flip = lambda nm, a: jnp.swapaxes(a, 1, 2) if nm in flipped else a
    big_w, big_m, big_v = ([flip(nm, a) for nm, a in zip(big_names, arrs)] for arrs in (big_w, big_m, big_v))

    shards = [w[0].astype(BF16) for w in big_w]
    start_a = _gather_start("gather_start_up1", shards[:2], ())
    start_b = _gather_start("gather_start_mid", shards[2:5], (start_a[4],))
    start_c = _gather_start("gather_start_ffn2", shards[5:], (start_b[4],))

    pending = {}

    def arrived(tag, started, after):
        send_sem, recv_sem, srcs, outs, _ = started
        return _gather_wait("gather_wait_" + tag, send_sem, recv_sem, srcs, outs, after)

    def first_weights(after):
        return (*_gather_join("gather_join_up1", *arrived("up1", start_a, after)), (start_c[4],))

    def mid_weights(after):
        srcs, outs = arrived("mid", start_b, after)
        (wd1,) = _gather_join("gather_join_wd1", srcs[:1], outs[:1])
        pending["mid"] = _join_start("join_start_mid", srcs[1:], outs[1:])
        return wd1, (pending["mid"][3],)

    def mid_rest(after):
        sems, srcs, outs, _ = pending["mid"]
        win_f, wout_f = _join_wait("join_wait_mid", sems, srcs, outs, after)
        return win_f, wout_f.reshape(wout_f.shape[0] * wout_f.shape[1], d)

    def last_begin(after):
        pending["ffn2"] = _join_start("join_start_ffn2", *arrived("ffn2", start_c, after))
        return (pending["ffn2"][3],)

    def last_weights(after):
        sems, srcs, outs, _ = pending["ffn2"]
        return _join_wait("join_wait_ffn2", sems, srcs, outs, after)

    weights = {"first": first_weights, "mid": mid_weights, "mid_rest": mid_rest, "last_begin": last_begin,
               "last": last_weights}

    core = lax.axis_index("c").astype(jnp.int32).reshape(1)
    chip = (2 * lax.axis_index("x") + lax.axis_index("y")).astype(jnp.int32).reshape(1)
    started = {}

    def on_grads(tag, grads):
        names = list(grads)
        started[tag] = (names, _pair_start("pair_start_" + tag, [grads[nm] for nm in names]))
        return (started[tag][1][4],)

    def grads_sent(tag, after):
        names, (send_sem, recv_sem, grads, lands, token) = started[tag]
        grads, theirs = _pair_wait("pair_wait_" + tag, send_sem, recv_sem, grads, lands, token if after is None else after)
        sums = [_pair_sum("pair_sum_" + nm, g, th, core) for nm, g, th in zip(names, grads, theirs)]
        started[tag] = (names, _scatter_start("scatter_start_" + tag, sums))
        return (started[tag][1][4],)

    grad_x, small_g = _local_step(
        x, loss_target, ffn1_norm_g, mix_norm_g, ffn2_norm_g, attn_q_norm_g, attn_k_norm_g, hgrn_out_norm_g,
        attn_rel_bias[0], hgrn_lower_bounds, weights, on_grads, grads_sent)

    def finish(tag, after):
        names, (send_sem, recv_sem, sums, lands, _) = started[tag]
        sums, lands = _scatter_wait("scatter_wait_" + tag, send_sem, recv_sem, sums, lands, after)
        return names, [_chip_sum("chip_sum_" + nm, sm, ld, chip) for nm, sm, ld in zip(names, sums, lands)]

    by_name = {nm: (w, m, v) for nm, w, m, v in zip(big_names, big_w, big_m, big_v)}
    updated = {}

    def update(names, halves, other_halves):
        for nm, mine, theirs in zip(names, halves, other_halves):
            w, m, v = by_name[nm]
            updated[nm] = _adamw("adamw_" + nm, w, mine, theirs, m, v, core)

    last_token = started["ffn1"][1][4]
    names_a, halves_a = finish("ffn2", last_token)
    names_m, halves_m = finish("mix", last_token)
    names_a, halves_a = names_a + names_m, halves_a + halves_m
    update(names_a, halves_a, _pair_join("pair_join_early", halves_a))
    names_b, halves_b = finish("ffn1", updated[names_a[-1]][1])
    others_b, small_all = _pair_join("pair_join_last", halves_b, small_g)
    update(names_b, halves_b, others_b)
    big_out = [updated[nm] for nm in big_names]

    pack = lambda g1, gm, g2, gq, gk, rel, lbp, go: _pack_small(g1, gm, g2, lbp, rel[0], gq, gk, go)
    small_w = pack(ffn1_norm_g, mix_norm_g, ffn2_norm_g, attn_q_norm_g, attn_k_norm_g, attn_rel_bias, hgrn_lower_bounds, hgrn_out_norm_g)
    small_m = pack(m_ffn1_norm_g, m_mix_norm_g, m_ffn2_norm_g, m_attn_q_norm_g, m_attn_k_norm_g, m_attn_rel_bias, m_hgrn_lower_bounds, m_hgrn_out_norm_g)
    small_v = pack(v_ffn1_norm_g, v_mix_norm_g, v_ffn2_norm_g, v_attn_q_norm_g, v_attn_k_norm_g, v_attn_rel_bias, v_hgrn_lower_bounds, v_hgrn_out_norm_g)
    small_res = _adamw_small("adamw_small", small_w, small_all, small_m, small_v)
    small_out = [_unpack_small(p, d) for p in small_res]
    loss = small_res[0].reshape(-1)[LOSS_SLOT]

    def assemble(kind):
        bg = [flip(nm, o[kind]) for nm, o in zip(big_names, big_out)]
        g1, gm, g2, gq, gk, rel, lbp, go = small_out[kind]
        return [g1, bg[0], bg[1], bg[2], gm, bg[3], gq, gk, rel, lbp, go, bg[4], g2, bg[5], bg[6], bg[7]]

    return (loss, grad_x, *assemble(0), *assemble(1), *assemble(2), *assemble(3))
```

```python
import functools

import jax
import jax.numpy as jnp
from jax import lax
from jax.experimental import pallas as pl
from jax.experimental.pallas import tpu as pltpu

F32 = jnp.float32
BF16 = jnp.bfloat16
MESH = pl.DeviceIdType.MESH

N_CHIPS = 4
N_DEV = 8
CHUNK = 64
ATTN_HEADS = 8
ATTN_DH = 64
ATTN_W = ATTN_HEADS * ATTN_DH
HGRN_HEADS = 4
HGRN_DH = 128
HGRN_W = HGRN_HEADS * HGRN_DH
LEFT_CHUNKS = 8
BAND = (LEFT_CHUNKS + 1) * CHUNK
KPAD = LEFT_CHUNKS * CHUNK
REL_CLIP = 128
N_REL = 2 * REL_CLIP + 1
N_REL_PAD = 384
RMS_EPS = 1e-6
LANES = 128
SMALL_ROWS = 8
SMALL_COLS = 1024

ADAM_LR = 0.001
ADAM_B1 = 0.9
ADAM_B2 = 0.999
ADAM_EPS = 1e-08
ADAM_WD = 0.01
ADAM_STEP = 10

NN = (((1,), (0,)), ((), ()))
NT = (((1,), (1,)), ((), ()))
TN = (((0,), (0,)), ((), ()))

VMEM_LIMIT = 48 * 1024 * 1024
MXU_WIDTH = 256
COL_CHUNK = 3 * MXU_WIDTH


def _sigmoid(x):
    return 1.0 / (1.0 + jnp.exp(-x))


def _silu(x):
    return x * _sigmoid(x)


def _dot(a, b, dims=NN):
    return lax.dot_general(a, b, dims, preferred_element_type=F32)


def _split3(x):
    hi = x.astype(BF16)
    r1 = x - hi.astype(F32)
    mid = r1.astype(BF16)
    lo = (r1 - mid.astype(F32)).astype(BF16)
    return hi, mid, lo


def _dot_exact_rhs(x, mat, dims=NN, pieces=3):
    hi, mid, lo = _split3(x)
    out = _dot(hi, mat, dims) + _dot(mid, mat, dims)
    return out + _dot(lo, mat, dims) if pieces == 3 else out


def _dot_exact_lhs(mat, x, dims=NN):
    hi, mid, lo = _split3(x)
    return _dot(mat, hi, dims) + _dot(mat, mid, dims) + _dot(mat, lo, dims)


def _params(*sem):
    return pltpu.CompilerParams(dimension_semantics=sem, vmem_limit_bytes=VMEM_LIMIT)


def _mm(name, ins, terms, n_acc, grid, acc_shape, outs, epilogue, extras=(), deps=()):
    nk = grid[2]
    ni, ne, nd, no = len(ins), len(extras), len(deps), len(outs)

    def body(*refs):
        in_refs = refs[:ni]
        ex_refs = refs[ni:ni + ne]
        out_refs = refs[ni + ne + nd:ni + ne + nd + no]
        acc_refs = refs[ni + ne + nd + no:]
        parts = [None] * n_acc
        for ai, li, ri, dims in terms:
            d = _dot(in_refs[li][...], in_refs[ri][...], dims)
            parts[ai] = d if parts[ai] is None else parts[ai] + d

        def finish(accs):
            res = epilogue(accs, [e[...] for e in ex_refs])
            for o, r in zip(out_refs, res):
                o[...] = r.astype(o.dtype)

        if nk == 1:
            finish(parts)
        else:
            k = pl.program_id(2)

            @pl.when(k == 0)
            def _():
                for a, p in zip(acc_refs, parts):
                    a[...] = p

            @pl.when(k > 0)
            def _():
                for a, p in zip(acc_refs, parts):
                    a[...] += p

            @pl.when(k == nk - 1)
            def _():
                finish([a[...] for a in acc_refs])

    scratch = [] if nk == 1 else [pltpu.VMEM(acc_shape, F32) for _ in range(n_acc)]
    res = pl.pallas_call(
        body,
        name=name,
        grid=grid,
        in_specs=[s for _, s in ins] + [s for _, s in extras] + [pl.BlockSpec(memory_space=pl.ANY)] * nd,
        out_specs=[s for _, s in outs],
        out_shape=[o for o, _ in outs],
        scratch_shapes=scratch,
        compiler_params=_params("parallel", "parallel", "arbitrary"),
    )(*[a for a, _ in ins], *[a for a, _ in extras], *deps)
    return res


def _mm_rows(name, lhs, weights, dims, t, outs, epilogue, extras=(), deps=()):
    tm = _row_tile(t)
    nl, ne, nd, no = len(lhs), len(extras), len(deps), len(outs)
    ns = weights[0].shape[0]

    def body(*refs):
        lhs_refs = refs[:nl]
        w_hbm = refs[nl:2 * nl]
        ex_refs = refs[2 * nl:2 * nl + ne]
        out_refs = refs[2 * nl + ne + nd:2 * nl + ne + nd + no]
        w_vmem = refs[2 * nl + ne + nd + no:3 * nl + ne + nd + no]
        sem = refs[-1]

        @pl.when(pl.program_id(0) == 0)
        def _():
            copies = [pltpu.make_async_copy(w_hbm[p], w_vmem[p], sem.at[p]) for p in range(nl)]
            for cp in copies:
                cp.start()
            for cp in copies:
                cp.wait()

        acc = None
        for p in range(nl):
            pick = lhs[p][2]
            for j in range(ns):
                part = _dot(pick(lhs_refs[p], j), w_vmem[p][j], dims)
                acc = part if acc is None else acc + part
        res = epilogue([acc], [e[...] for e in ex_refs])
        for o, r in zip(out_refs, res):
            o[...] = r.astype(o.dtype)

    return pl.pallas_call(
        body,
        name=name,
        grid=(t // tm,),
        in_specs=[s for _, s, _ in lhs] + [pl.BlockSpec(memory_space=pl.ANY)] * nl + [s for _, s in extras]
        + [pl.BlockSpec(memory_space=pl.ANY)] * nd,
        out_specs=[s for _, s in outs],
        out_shape=[o for o, _ in outs],
        scratch_shapes=[pltpu.VMEM(w.shape, w.dtype) for w in weights] + [pltpu.SemaphoreType.DMA((nl,))],
        compiler_params=_params("arbitrary"),
    )(*[a for a, _, _ in lhs], *weights, *[a for a, _ in extras], *deps)


def _mm_shards(name, x, weights, dims, outs, epilogue, extras=(), deps=()):
    t = x.shape[0]
    tm = _row_tile(t)
    nw, ne, nd, no = len(weights), len(extras), len(deps), len(outs)
    ns = weights[0].shape[0]

    def body(*refs):
        x_ref = refs[0]
        w_hbm = refs[1:1 + nw]
        ex_refs = refs[1 + nw:1 + nw + ne]
        out_refs = refs[1 + nw + ne + nd:1 + nw + ne + nd + no]
        w_vmem = refs[1 + nw + ne + nd + no:1 + 2 * nw + ne + nd + no]
        sem = refs[-1]

        @pl.when(pl.program_id(0) == 0)
        def _():
            copies = [pltpu.make_async_copy(w_hbm[p], w_vmem[p], sem.at[p]) for p in range(nw)]
            for cp in copies:
                cp.start()
            for cp in copies:
                cp.wait()

        xv = x_ref[...]
        accs = [_dot(xv, w_vmem[p][0], dims) for p in range(nw)]
        for j in range(ns):
            nxt = [_dot(xv, w_vmem[p][j + 1], dims) for p in range(nw)] if j + 1 < ns else None
            res = epilogue(accs, [e[j] for e in ex_refs])
            for (_, _, store), o, r in zip(outs, out_refs, res):
                store(o, j, r.astype(o.dtype))
            accs = nxt

    return pl.pallas_call(
        body,
        name=name,
        grid=(t // tm,),
        in_specs=[pl.BlockSpec((tm, x.shape[1]), lambda i: (i, 0))] + [pl.BlockSpec(memory_space=pl.ANY)] * nw
        + [s for _, s in extras] + [pl.BlockSpec(memory_space=pl.ANY)] * nd,
        out_specs=[s for _, s, _ in outs],
        out_shape=[o for o, _, _ in outs],
        scratch_shapes=[pltpu.VMEM(w.shape, w.dtype) for w in weights] + [pltpu.SemaphoreType.DMA((nw,))],
        compiler_params=_params("arbitrary"),
    )(x, *weights, *[a for a, _ in extras], *deps)


def _col_chunks(f):
    return [(c, min(COL_CHUNK, f - c)) for c in range(0, f, COL_CHUNK)]


def _mm_cols(name, x, weights, n_out, epilogue, extras=(), deps=()):
    t, k = x.shape
    f = weights[0].shape[0]
    tm = _row_tile(t)
    chunks = _col_chunks(f)
    nw, ne, nd = len(weights), len(extras), len(deps)

    def body(*refs):
        x_ref = refs[0]
        w_hbm = refs[1:1 + nw]
        ex_refs = refs[1 + nw:1 + nw + ne]
        out_refs = refs[1 + nw + ne + nd:1 + nw + ne + nd + n_out]
        w_vmem = refs[1 + nw + ne + nd + n_out:1 + 2 * nw + ne + nd + n_out]
        sem = refs[-1]

        @pl.when(pl.program_id(0) == 0)
        def _():
            copies = [pltpu.make_async_copy(w_hbm[p], w_vmem[p], sem.at[p]) for p in range(nw)]
            for cp in copies:
                cp.start()
            for cp in copies:
                cp.wait()

        xv = x_ref[...]

        def dots(c):
            c0, cw = chunks[c]
            return [_dot(xv, w[c0:c0 + cw, :], NT) for w in w_vmem]

        accs = dots(0)
        for c, (c0, cw) in enumerate(chunks):
            nxt = dots(c + 1) if c + 1 < len(chunks) else None
            res = epilogue(accs, [e[:, c0:c0 + cw] for e in ex_refs])
            for o, r in zip(out_refs, res):
                o[:, c0:c0 + cw] = r.astype(o.dtype)
            accs = nxt

    act = pl.BlockSpec((tm, f), lambda i: (i, 0))
    return pl.pallas_call(
        body,
        name=name,
        grid=(t // tm,),
        in_specs=[pl.BlockSpec((tm, k), lambda i: (i, 0))] + [pl.BlockSpec(memory_space=pl.ANY)] * nw + [act] * ne
        + [pl.BlockSpec(memory_space=pl.ANY)] * nd,
        out_specs=[act] * n_out,
        out_shape=[jax.ShapeDtypeStruct((t, f), BF16)] * n_out,
        scratch_shapes=[pltpu.VMEM(w.shape, w.dtype) for w in weights] + [pltpu.SemaphoreType.DMA((nw,))],
        compiler_params=_params("arbitrary"),
    )(x, *weights, *extras, *deps)


def _row_tile(t):
    return 512 if t % 512 == 0 else t


def _k_tile(t):
    return t if t <= 4096 else 1024


def _grad_k_tile(t):
    return 1024 if t % 1024 == 0 else t


def _rmsnorm(xv, g):
    ms = jnp.mean(xv * xv, axis=-1, keepdims=True)
    return xv * lax.rsqrt(ms + RMS_EPS) * g


def _rmsnorm_fwd(name, x, g):
    t, d = x.shape
    tm = _row_tile(t)

    def body(x_ref, g_ref, h_ref):
        h_ref[...] = _rmsnorm(x_ref[...], g_ref[...]).astype(BF16)

    return pl.pallas_call(
        body,
        name=name,
        grid=(t // tm,),
        in_specs=[pl.BlockSpec((tm, d), lambda i: (i, 0)), pl.BlockSpec((1, d), lambda i: (0, 0))],
        out_specs=pl.BlockSpec((tm, d), lambda i: (i, 0)),
        out_shape=jax.ShapeDtypeStruct((t, d), BF16),
        compiler_params=_params("parallel"),
    )(x, g)


def _norm_bwd_epilogue(copy_scale):
    def epilogue(accs, ex):
        dh = accs[0]
        xv, g, dres = ex
        ms = jnp.mean(xv * xv, axis=-1, keepdims=True)
        rstd = lax.rsqrt(ms + RMS_EPS)
        xhat = xv * rstd
        dxhat = dh * g
        dx = rstd * (dxhat - xhat * jnp.mean(dxhat * xhat, axis=-1, keepdims=True))
        out = dres + dx
        dg = jnp.sum(dh * xhat, axis=0, keepdims=True)
        if copy_scale is None:
            return out, dg
        return out, out * copy_scale, dg

    return epilogue


def _merged(w):
    return w.reshape(1, -1, w.shape[-1])


def _ffn_up(name, h, wg, wu, deps=()):
    def epilogue(accs, ex):
        a, b = accs
        sg = _sigmoid(a)
        act = a * sg
        return act, b * (sg * (1.0 + a * (1.0 - sg))), act * b

    return _mm_cols(name, h, [_merged(wg)[0], _merged(wu)[0]], 3, epilogue, deps=deps)


def _whole_rows(arr, tm):
    return arr, pl.BlockSpec((tm, arr.shape[1]), lambda i: (i, 0)), lambda ref, j: ref[...]


def _ffn_down(name, z, wd, x, g_next, deps=()):
    t = z.shape[0]
    d = wd.shape[2]
    tm = _row_tile(t)
    row = pl.BlockSpec((tm, d), lambda i: (i, 0))

    def epilogue(accs, ex):
        y = ex[0] + 0.5 * accs[0]
        return y, _rmsnorm(y, ex[1])

    return _mm_rows(
        name, [_whole_rows(z, tm)], [_merged(wd)], NN, t,
        outs=[(jax.ShapeDtypeStruct((t, d), F32), row), (jax.ShapeDtypeStruct((t, d), BF16), row)],
        epilogue=epilogue,
        extras=[(x, row), (g_next, pl.BlockSpec((1, d), lambda i: (0, 0)))],
        deps=deps,
    )


def _ffn_down_loss(name, z, wd, x, target):
    t = z.shape[0]
    d = wd.shape[2]
    tm = _row_tile(t)
    nt = t // tm
    row = pl.BlockSpec((tm, d), lambda i: (i, 0))

    def epilogue(accs, ex):
        e = ex[0] + 0.5 * accs[0] - ex[1]
        dy = e * (1.0 / d)
        return dy, 0.5 * dy, jnp.sum(e * e, axis=0, keepdims=True)

    return _mm_rows(
        name, [_whole_rows(z, tm)], [_merged(wd)], NN, t,
        outs=[(jax.ShapeDtypeStruct((t, d), F32), row), (jax.ShapeDtypeStruct((t, d), BF16), row),
              (jax.ShapeDtypeStruct((nt, 1, d), F32), pl.BlockSpec((None, 1, d), lambda i: (i, 0, 0)))],
        epilogue=epilogue,
        extras=[(x, row), (target, row)],
    )


def _ffn_bwd_act(name, dout, wd, act_a, dact_b, deps=()):
    def epilogue(accs, ex):
        dz = accs[0]
        return dz * ex[1].astype(F32), dz * ex[0].astype(F32)

    return _mm_cols(name, dout, [_merged(wd)[0]], 2, epilogue, extras=[act_a, dact_b], deps=deps)


def _grad_w_cols(name, z, dout, deps=()):
    t, f = z.shape
    d = dout.shape[1]
    tk = _grad_k_tile(t)
    fh = f // 2
    dw = _mm(
        name,
        ins=[(z, pl.BlockSpec((tk, fh), lambda j, n, k: (k, j))),
             (dout, pl.BlockSpec((tk, d), lambda j, n, k: (k, 0)))],
        terms=[(0, 0, 1, TN)],
        n_acc=1,
        grid=(2, 1, t // tk),
        acc_shape=(fh, d),
        outs=[(pltpu.HBM((f, d), BF16), pl.BlockSpec((fh, d), lambda j, n, k: (j, 0)))],
        epilogue=lambda accs, ex: (accs[0],),
        deps=deps,
    )[0]
    return dw.reshape(N_CHIPS, f // N_CHIPS, d)


def _norm_bwd_outs(t, d, tm, copy_scale):
    row = pl.BlockSpec((tm, d), lambda i: (i, 0))
    outs = [(jax.ShapeDtypeStruct((t, d), F32), row)]
    if copy_scale is not None:
        outs.append((jax.ShapeDtypeStruct((t, d), BF16), row))
    outs.append((jax.ShapeDtypeStruct((t // tm, 1, d), F32), pl.BlockSpec((None, 1, d), lambda i: (i, 0, 0))))
    return row, outs


def _ffn_bwd_in(name, da, db, wg, wu, x, g, dres, copy_scale, deps=()):
    t = da.shape[0]
    d = wg.shape[2]
    tm = _row_tile(t)
    row, outs = _norm_bwd_outs(t, d, tm, copy_scale)
    return _mm_rows(
        name, [_whole_rows(da, tm), _whole_rows(db, tm)], [_merged(wg), _merged(wu)], NN, t,
        outs=outs,
        epilogue=_norm_bwd_epilogue(copy_scale),
        extras=[(x, row), (g, pl.BlockSpec((1, d), lambda i: (0, 0))), (dres, row)],
        deps=deps,
    )


def _in_proj(name, h, w_in):
    t, d = h.shape
    ns, _, pj = w_in.shape
    tm = _row_tile(t)
    def store(ref, j, value):
        ref[:, j * pj:(j + 1) * pj] = value

    out = (jax.ShapeDtypeStruct((t, ns * pj), F32), pl.BlockSpec((tm, ns * pj), lambda i: (i, 0)), store)
    return _mm_shards(name, h, [w_in], NN, [out], lambda accs, ex: (accs[0],))[0]


def _in_proj_bwd(name, dp, w_in, x, g, dres, copy_scale, deps=()):
    t = dp.shape[0]
    ns, d, pj = w_in.shape
    tm = _row_tile(t)
    row, outs = _norm_bwd_outs(t, d, tm, copy_scale)
    cols = (dp, pl.BlockSpec((tm, ns * pj), lambda i: (i, 0)), lambda ref, j: ref[:, j * pj:(j + 1) * pj])
    return _mm_rows(
        name, [cols], [w_in], NT, t,
        outs=outs,
        epilogue=_norm_bwd_epilogue(copy_scale),
        extras=[(x, row), (g, pl.BlockSpec((1, d), lambda i: (0, 0))), (dres, row)],
        deps=deps,
    )


def _grad_w_in(name, h, dp, ns):
    t, d = h.shape
    pj = dp.shape[1] // ns
    tk = _k_tile(t)
    return _mm(
        name,
        ins=[(h, pl.BlockSpec((tk, d), lambda j, n, k: (k, 0))),
             (dp, pl.BlockSpec((tk, pj), lambda j, n, k: (k, j)))],
        terms=[(0, 0, 1, TN)],
        n_acc=1,
        grid=(ns, 1, t // tk),
        acc_shape=(d, pj),
        outs=[(pltpu.HBM((ns, d, pj), BF16), pl.BlockSpec((None, d, pj), lambda j, n, k: (j, 0, 0)))],
        epilogue=lambda accs, ex: (accs[0],),
    )[0]


def _out_proj(name, mix, w_out, x, g_next):
    t, dm = mix.shape
    d = w_out.shape[1]
    tm = _row_tile(t)
    row = pl.BlockSpec((tm, d), lambda i, n, k: (i, 0))
    return _mm(
        name,
        ins=[(mix, pl.BlockSpec((tm, dm), lambda i, n, k: (i, 0))),
             (w_out, pl.BlockSpec((dm, d), lambda i, n, k: (0, 0)))],
        terms=[(0, 0, 1, NN)],
        n_acc=1,
        grid=(t // tm, 1, 1),
        acc_shape=(tm, d),
        outs=[(jax.ShapeDtypeStruct((t, d), F32), row), (jax.ShapeDtypeStruct((t, d), BF16), row)],
        epilogue=lambda accs, ex: (ex[0] + accs[0], _rmsnorm(ex[0] + accs[0], ex[1])),
        extras=[(x, row), (g_next, pl.BlockSpec((1, d), lambda i, n, k: (0, 0)))],
    )


def _out_proj_bwd(name, dx, w_out, deps=()):
    t, d = dx.shape
    dm = w_out.shape[0]
    tm = _row_tile(t)
    return _mm(
        name,
        ins=[(dx, pl.BlockSpec((tm, d), lambda i, n, k: (i, 0))),
             (w_out, pl.BlockSpec((dm, d), lambda i, n, k: (0, 0)))],
        terms=[(0, 0, 1, NT)],
        n_acc=1,
        grid=(t // tm, 1, 1),
        acc_shape=(tm, dm),
        outs=[(jax.ShapeDtypeStruct((t, dm), F32), pl.BlockSpec((tm, dm), lambda i, n, k: (i, 0)))],
        epilogue=lambda accs, ex: (accs[0],),
        deps=deps,
    )[0]


def _grad_w_out(name, mix, dx):
    t, dm = mix.shape
    d = dx.shape[1]
    tk = _k_tile(t)
    return _mm(
        name,
        ins=[(mix, pl.BlockSpec((tk, dm), lambda a, n, k: (k, 0))),
             (dx, pl.BlockSpec((tk, d), lambda a, n, k: (k, 0)))],
        terms=[(0, 0, 1, TN)],
        n_acc=1,
        grid=(1, 1, t // tk),
        acc_shape=(dm, d),
        outs=[(pltpu.HBM((dm, d), BF16), pl.BlockSpec((dm, d), lambda a, n, k: (0, 0)))],
        epilogue=lambda accs, ex: (accs[0],),
    )[0]


def _head_group_matrix():
    r = lax.broadcasted_iota(jnp.int32, (ATTN_W, ATTN_W), 0)
    c = lax.broadcasted_iota(jnp.int32, (ATTN_W, ATTN_W), 1)
    same = jnp.right_shift(r, 6) == jnp.right_shift(c, 6)
    return jnp.where(same, 1.0, 0.0).astype(BF16)


def _qk_prep(name, proj, gq, gk):
    b, s, _ = proj.shape
    tm = KPAD
    nb = s // tm

    def body(q_ref, k_ref, v_ref, gq_ref, gk_ref, qn_ref, kn_ref, vb_ref):
        j = pl.program_id(1)
        bd = _head_group_matrix()

        def norm(xv, g):
            ms = _dot_exact_rhs(xv * xv, bd, pieces=2) * (1.0 / ATTN_DH)
            return xv * lax.rsqrt(ms + RMS_EPS) * g

        @pl.when(j == 0)
        def _():
            kn_ref[...] = jnp.zeros_like(kn_ref)
            vb_ref[...] = jnp.zeros_like(vb_ref)

        @pl.when(j > 0)
        def _():
            qn_ref[...] = norm(q_ref[...], gq_ref[...]).astype(BF16)
            kn_ref[...] = norm(k_ref[...], gk_ref[...]).astype(BF16)
            vb_ref[...] = v_ref[...].astype(BF16)

    src_blk = lambda col: pl.BlockSpec((None, tm, ATTN_W), lambda bi, j: (bi, jnp.maximum(j - 1, 0), col))
    gspec = pl.BlockSpec((1, ATTN_W), lambda bi, j: (0, 0))
    padded = pl.BlockSpec((None, tm, ATTN_W), lambda bi, j: (bi, j, 0))
    return pl.pallas_call(
        body,
        name=name,
        grid=(b, nb + 1),
        in_specs=[src_blk(0), src_blk(1), src_blk(2), gspec, gspec],
        out_specs=[src_blk(0), padded, padded],
        out_shape=[jax.ShapeDtypeStruct((b, s, ATTN_W), BF16), jax.ShapeDtypeStruct((b, KPAD + s, ATTN_W), BF16),
                   jax.ShapeDtypeStruct((b, KPAD + s, ATTN_W), BF16)],
        compiler_params=_params("parallel", "arbitrary"),
    )(proj, proj, proj, gq, gk)


def _qk_prep_bwd(name, proj, dqn, dkn, dv, gq, gk):
    b, s, _ = proj.shape
    tm = KPAD
    nb = s // tm

    def body(q_ref, k_ref, dqn_ref, dkn_ref, dv_ref, gq_ref, gk_ref, dq_ref, dk_ref, dvb_ref, dgq_ref, dgk_ref):
        bd = _head_group_matrix()

        def bwd(xv, dy, g):
            ms = _dot_exact_rhs(xv * xv, bd, pieces=2) * (1.0 / ATTN_DH)
            rstd = lax.rsqrt(ms + RMS_EPS)
            xhat = xv * rstd
            dxhat = dy * g
            gm = _dot_exact_rhs(dxhat * xhat, bd, pieces=2) * (1.0 / ATTN_DH)
            return rstd * (dxhat - xhat * gm), jnp.sum(dy * xhat, axis=0, keepdims=True)

        dq, dgq = bwd(q_ref[...], dqn_ref[...], gq_ref[...])
        dk, dgk = bwd(k_ref[...], dkn_ref[...], gk_ref[...])
        dq_ref[...] = dq.astype(BF16)
        dk_ref[...] = dk.astype(BF16)
        dvb_ref[...] = dv_ref[...].astype(BF16)
        dgq_ref[...] = dgq
        dgk_ref[...] = dgk

    col = lambda c: pl.BlockSpec((None, tm, ATTN_W), lambda bi, j: (bi, j, c))
    past_pad = pl.BlockSpec((None, tm, ATTN_W), lambda bi, j: (bi, j + 1, 0))
    gspec = pl.BlockSpec((1, ATTN_W), lambda bi, j: (0, 0))
    pspec = pl.BlockSpec((None, 1, ATTN_W), lambda bi, j: (bi * nb + j, 0, 0))
    o_shape = jax.ShapeDtypeStruct((b, s, ATTN_W), BF16)
    p_shape = jax.ShapeDtypeStruct((b * nb, 1, ATTN_W), F32)
    return pl.pallas_call(
        body,
        name=name,
        grid=(b, nb),
        in_specs=[col(0), col(1), col(0), past_pad, past_pad, gspec, gspec],
        out_specs=[col(0)] * 3 + [pspec] * 2,
        out_shape=[o_shape] * 3 + [p_shape] * 2,
        compiler_params=_params("parallel", "parallel"),
    )(proj, proj, dqn, dkn, dv, gq, gk)


Q_CHUNKS = 4
QBLK = Q_CHUNKS * CHUNK
WIN = (LEFT_CHUNKS + Q_CHUNKS) * CHUNK
DB_W = BAND + CHUNK
MASKED = -1e30


def _band_table(bias):
    rows = [jnp.pad(bias, ((0, 0), (0, 0), (CHUNK * i, WIN - BAND - CHUNK * i)), constant_values=MASKED)
            for i in range(Q_CHUNKS)]
    return jnp.concatenate(rows, axis=1)


def _head_lanes(hh):
    lane = lax.broadcasted_iota(jnp.int32, (1, LANES), 1)
    return (lane < ATTN_DH) if hh == 0 else (lane >= ATTN_DH)


def _attn_probs(qh, kw, table, start):
    s = _dot(qh, kw, NT) * (ATTN_DH ** -0.5) + table
    col = lax.broadcasted_iota(jnp.int32, (QBLK, WIN), 1)
    s = jnp.where(col + start >= KPAD, s, MASKED)
    m = jnp.max(s, axis=-1, keepdims=True)
    p = jnp.exp(s - m)
    return p * (1.0 / jnp.sum(p, axis=-1, keepdims=True))


def _attn_fwd(name, q, k, v, table, deps=()):
    b, s, w = q.shape
    sp = k.shape[1]

    def body(q_ref, k_ref, v_ref, t_ref, *rest):
        o_ref = rest[-1]
        start = pl.multiple_of(pl.program_id(2) * QBLK, QBLK)
        kw = k_ref[pl.ds(start, WIN), :]
        vw = v_ref[pl.ds(start, WIN), :]
        q2 = q_ref[...]
        lanes = [_head_lanes(hh) for hh in range(2)]
        probs = [_attn_probs(jnp.where(mine, q2, jnp.zeros_like(q2)), kw, t_ref[hh], start).astype(BF16)
                 for hh, mine in enumerate(lanes)]
        outs = [_dot(p, vw) for p in probs]
        o_ref[...] = jnp.where(lanes[0], outs[0], outs[1]).astype(BF16)

    qspec = pl.BlockSpec((None, QBLK, LANES), lambda p, bi, i: (bi, i, p))
    kspec = pl.BlockSpec((None, sp, LANES), lambda p, bi, i: (bi, 0, p))
    return pl.pallas_call(
        body,
        name=name,
        grid=(w // LANES, b, s // QBLK),
        in_specs=[qspec, kspec, kspec, pl.BlockSpec((2, QBLK, WIN), lambda p, bi, i: (p, 0, 0))] + [ANY] * len(deps),
        out_specs=qspec,
        out_shape=jax.ShapeDtypeStruct((b, s, w), BF16),
        compiler_params=_params("parallel", "parallel", "arbitrary"),
    )(q, k, v, table, *deps)


def _attn_bwd(name, q, k, v, table, dmix):
    b, s, w = q.shape
    sp = k.shape[1]

    def body(q_ref, k_ref, v_ref, t_ref, do_ref, dq_ref, dk_ref, dv_ref, dbe_ref, dbo_ref):
        bi = pl.program_id(1)
        i = pl.program_id(2)
        start = pl.multiple_of(i * QBLK, QBLK)
        win = pl.ds(start, WIN)

        @pl.when(i == 0)
        def _():
            dk_ref[...] = jnp.zeros_like(dk_ref)
            dv_ref[...] = jnp.zeros_like(dv_ref)

        @pl.when(jnp.logical_and(i == 0, bi == 0))
        def _():
            dbe_ref[...] = jnp.zeros_like(dbe_ref)
            dbo_ref[...] = jnp.zeros_like(dbo_ref)

        kw = k_ref[win, :]
        vw = v_ref[win, :]
        q2 = q_ref[...]
        do2 = do_ref[...].astype(BF16)
        lanes = [_head_lanes(hh) for hh in range(2)]
        qh = [jnp.where(mine, q2, jnp.zeros_like(q2)) for mine in lanes]
        doh = [jnp.where(mine, do2, jnp.zeros_like(do2)) for mine in lanes]
        p = [_attn_probs(qh[hh], kw, t_ref[hh], start) for hh in range(2)]
        dp = [_dot(doh[hh], vw, NT) for hh in range(2)]
        ds = [p[hh] * (dp[hh] - jnp.sum(p[hh] * dp[hh], axis=-1, keepdims=True)) for hh in range(2)]
        dsb = [(x * (ATTN_DH ** -0.5)).astype(BF16) for x in ds]
        pb = [x.astype(BF16) for x in p]
        dq = [_dot(dsb[hh], kw) for hh in range(2)]
        dk = [_dot(dsb[hh], qh[hh], TN) for hh in range(2)]
        dv = [_dot(pb[hh], doh[hh], TN) for hh in range(2)]
        for hh in range(2):
            for qi in range(Q_CHUNKS):
                c0 = (qi // 2) * LANES
                blk = ds[hh][qi * CHUNK:(qi + 1) * CHUNK, c0:c0 + DB_W]
                if qi % 2 == 0:
                    dbe_ref[hh] += blk
                else:
                    dbo_ref[hh] += blk
        dq_ref[...] = jnp.where(lanes[0], dq[0], dq[1])
        dk_ref[win, :] += dk[0] + dk[1]
        dv_ref[win, :] += dv[0] + dv[1]

    qspec = pl.BlockSpec((None, QBLK, LANES), lambda p, bi, i: (bi, i, p))
    kspec = pl.BlockSpec((None, sp, LANES), lambda p, bi, i: (bi, 0, p))
    dbspec = pl.BlockSpec((2, CHUNK, DB_W), lambda p, bi, i: (p, 0, 0))
    db_shape = jax.ShapeDtypeStruct((ATTN_HEADS, CHUNK, DB_W), F32)
    return pl.pallas_call(
        body,
        name=name,
        grid=(w // LANES, b, s // QBLK),
        in_specs=[qspec, kspec, kspec, pl.BlockSpec((2, QBLK, WIN), lambda p, bi, i: (p, 0, 0)), qspec],
        out_specs=[qspec, kspec, kspec, dbspec, dbspec],
        out_shape=[jax.ShapeDtypeStruct((b, s, w), F32), jax.ShapeDtypeStruct((b, sp, w), F32),
                   jax.ShapeDtypeStruct((b, sp, w), F32), db_shape, db_shape],
        compiler_params=_params("arbitrary", "arbitrary", "arbitrary"),
    )(q, k, v, table, dmix)


HQ_COL = 3 * ATTN_W // HGRN_DH
HF_COL = HQ_COL + HGRN_HEADS
HI_COL = HF_COL + HGRN_HEADS
HG_COL = HI_COL + HGRN_HEADS
HGRN_ROWS = 8 * CHUNK
HEAD_LANES = [slice(hh * HGRN_DH, (hh + 1) * HGRN_DH) for hh in range(HGRN_HEADS)]


def _tri(lower):
    r = lax.broadcasted_iota(jnp.int32, (CHUNK, CHUNK), 0)
    c = lax.broadcasted_iota(jnp.int32, (CHUNK, CHUNK), 1)
    return (r >= c) if lower else (r <= c)


def _hgrn_chunk(hq, hf, lb, tril):
    sig = _sigmoid(hf)
    f = lb + (1.0 - lb) * sig
    g = jnp.log(f)
    ones_l = jnp.where(tril, 1.0, 0.0).astype(BF16)
    b = _dot_exact_lhs(ones_l, g)
    bl = jnp.sum(g, axis=0, keepdims=True)
    rows = lax.broadcasted_iota(jnp.int32, g.shape, 0)
    bm = jnp.sum(jnp.where(rows <= CHUNK // 2, g, 0.0), axis=0, keepdims=True)
    sq = _sigmoid(hq)
    q = hq * sq
    k = 1.0 - f
    return sig, f, b, bl, bm, sq, q, k


def _hgrn_fwd(name, proj, attn, lb, go, b, s):
    nc = s // CHUNK
    t = b * s
    nblk = s // HGRN_ROWS
    cpb = HGRN_ROWS // CHUNK

    def body(hq_ref, hf_ref, hi_ref, hg_ref, attn_ref, lb_ref, go_ref, mix_ref, oraw_ref, st_ref, s_scr):
        tril = _tri(True)
        gov = go_ref[...]
        mix_ref[:, 0:ATTN_W] = attn_ref[...]

        @pl.when(pl.program_id(1) == 0)
        def _():
            s_scr[...] = jnp.zeros_like(s_scr)

        def step(c, carry):
            sl = pl.ds(pl.multiple_of(c * CHUNK, CHUNK), CHUNK)
            hg = hg_ref[sl, :]
            _, _, bb, bl, bm, _, q, k = _hgrn_chunk(hq_ref[sl, :], hf_ref[sl, :], lb_ref[...], tril)
            vb = hi_ref[sl, :].astype(BF16)
            qe = (q * jnp.exp(bb - bm)).astype(BF16)
            ke = (k * jnp.exp(bm - bb)).astype(BF16)
            qb = (q * jnp.exp(bb)).astype(BF16)
            kb = (k * jnp.exp(bl - bb)).astype(BF16)
            e_last = jnp.exp(bl)
            gate = _silu(hg)
            st = [s_scr[hh] for hh in range(HGRN_HEADS)]
            a = [jnp.where(tril, _dot(qe[:, hs], ke[:, hs], NT), 0.0).astype(BF16) for hs in HEAD_LANES]
            o_state = [_dot(qb[:, hs], st[hh].astype(BF16), NT) for hh, hs in enumerate(HEAD_LANES)]
            st_next = [st[hh] * e_last[:, hs] + _dot(vb[:, hs], kb[:, hs], TN) for hh, hs in enumerate(HEAD_LANES)]
            o = [_dot(a[hh], vb[:, hs]) + o_state[hh] for hh, hs in enumerate(HEAD_LANES)]
            ro = [(oh * lax.rsqrt(jnp.mean(oh * oh, axis=-1, keepdims=True) + RMS_EPS) * gov) * gate[:, hs]
                  for oh, hs in zip(o, HEAD_LANES)]
            for hh in range(HGRN_HEADS):
                st_ref[hh, c] = st[hh]
                s_scr[hh] = st_next[hh]
            mix_ref[sl, ATTN_W:ATTN_W + HGRN_W] = jnp.concatenate(ro, axis=1).astype(BF16)
            oraw_ref[sl, :] = jnp.concatenate(o, axis=1)
            return carry

        lax.fori_loop(0, cpb, step, 0)

    col = lambda base: pl.BlockSpec((HGRN_ROWS, HGRN_W), lambda bi, i: (bi * nblk + i, base // HGRN_HEADS))
    out = pl.BlockSpec((HGRN_ROWS, HGRN_W), lambda bi, i: (bi * nblk + i, 0))
    return pl.pallas_call(
        body,
        name=name,
        grid=(b, nblk),
        in_specs=[col(HQ_COL), col(HF_COL), col(HI_COL), col(HG_COL), out,
                  pl.BlockSpec((1, HGRN_W), lambda bi, i: (0, 0)), pl.BlockSpec((1, HGRN_DH), lambda bi, i: (0, 0))],
        out_specs=[pl.BlockSpec((HGRN_ROWS, ATTN_W + HGRN_W), lambda bi, i: (bi * nblk + i, 0)), out,
                   pl.BlockSpec((None, HGRN_HEADS, cpb, HGRN_DH, HGRN_DH), lambda bi, i: (bi, 0, i, 0, 0))],
        out_shape=[jax.ShapeDtypeStruct((t, ATTN_W + HGRN_W), BF16), jax.ShapeDtypeStruct((t, HGRN_W), F32),
                   jax.ShapeDtypeStruct((b, HGRN_HEADS, nc, HGRN_DH, HGRN_DH), F32)],
        scratch_shapes=[pltpu.VMEM((HGRN_HEADS, HGRN_DH, HGRN_DH), F32)],
        compiler_params=_params("parallel", "arbitrary"),
    )(proj, proj, proj, proj, attn, lb, go)


def _hgrn_bwd(name, proj, dqkv, lb, go, oraw, states, dmix, b, s):
    t = b * s
    nblk = s // HGRN_ROWS
    cpb = HGRN_ROWS // CHUNK

    def body(hq_ref, hf_ref, hi_ref, hg_ref, dq_ref, dk_ref, dv_ref, lb_ref, go_ref, oraw_ref, st_ref, dro_ref,
             dp_ref, dlb_ref, dgo_ref, ds_scr, dlb_scr, dgo_scr):
        tril = _tri(True)
        ones_u = jnp.where(_tri(False), 1.0, 0.0).astype(BF16)
        gov = go_ref[...]
        dp_ref[:, 0:ATTN_W] = dq_ref[...]
        dp_ref[:, ATTN_W:2 * ATTN_W] = dk_ref[...]
        dp_ref[:, 2 * ATTN_W:3 * ATTN_W] = dv_ref[...]

        @pl.when(pl.program_id(1) == 0)
        def _():
            ds_scr[...] = jnp.zeros_like(ds_scr)
            dlb_scr[...] = jnp.zeros_like(dlb_scr)
            dgo_scr[...] = jnp.zeros_like(dgo_scr)

        def step(ci, carry):
            c = cpb - 1 - ci
            sl = pl.ds(pl.multiple_of(c * CHUNK, CHUNK), CHUNK)
            hq = hq_ref[sl, :]
            hg = hg_ref[sl, :]
            sig, f, bb, bl, bm, sq, q, k = _hgrn_chunk(hq, hf_ref[sl, :], lb_ref[...], tril)
            vb = hi_ref[sl, :].astype(BF16)
            ebm = jnp.exp(bb - bm)
            embm = jnp.exp(bm - bb)
            eb = jnp.exp(bb)
            ebl = jnp.exp(bl - bb)
            e_last = jnp.exp(bl)
            qe = (q * ebm).astype(BF16)
            ke = (k * embm).astype(BF16)
            qb = (q * eb).astype(BF16)
            kb = (k * ebl).astype(BF16)
            st = [st_ref[hh, c] for hh in range(HGRN_HEADS)]
            dst = [ds_scr[hh] for hh in range(HGRN_HEADS)]
            o = oraw_ref[sl, :]
            dro = dro_ref[sl, :]
            sg = _sigmoid(hg)
            gov4 = jnp.concatenate([gov] * HGRN_HEADS, axis=1)
            rstd = jnp.concatenate(
                [jnp.broadcast_to(lax.rsqrt(jnp.mean(o[:, hs] * o[:, hs], axis=-1, keepdims=True) + RMS_EPS),
                                  (CHUNK, HGRN_DH)) for hs in HEAD_LANES], axis=1)
            ohat = o * rstd
            dn = dro * (hg * sg)
            dhg = dro * (ohat * gov4) * (sg * (1.0 + hg * (1.0 - sg)))
            dgo_inc = jnp.sum(dn * ohat, axis=0, keepdims=True)
            dohat = dn * gov4
            proj_h = dohat * ohat
            pm = jnp.concatenate(
                [jnp.broadcast_to(jnp.mean(proj_h[:, hs], axis=-1, keepdims=True), (CHUNK, HGRN_DH))
                 for hs in HEAD_LANES], axis=1)
            dob = (rstd * (dohat - ohat * pm)).astype(BF16)
            stb = [x.astype(BF16) for x in st]
            dstb = [x.astype(BF16) for x in dst]
            a = [jnp.where(tril, _dot(qe[:, hs], ke[:, hs], NT), 0.0).astype(BF16) for hs in HEAD_LANES]
            dab = [jnp.where(tril, _dot(dob[:, hs], vb[:, hs], NT), 0.0).astype(BF16) for hs in HEAD_LANES]
            dqb = [_dot(dob[:, hs], stb[hh]) for hh, hs in enumerate(HEAD_LANES)]
            dkb = [_dot(vb[:, hs], dstb[hh]) for hh, hs in enumerate(HEAD_LANES)]
            dv_state = [_dot(kb[:, hs], dstb[hh], NT) for hh, hs in enumerate(HEAD_LANES)]
            dst_next = [dst[hh] * e_last[:, hs] + _dot(dob[:, hs], qb[:, hs], TN) for hh, hs in enumerate(HEAD_LANES)]
            dv = [_dot(a[hh], dob[:, hs], TN) + dv_state[hh] for hh, hs in enumerate(HEAD_LANES)]
            dqe = jnp.concatenate([_dot(dab[hh], ke[:, hs]) for hh, hs in enumerate(HEAD_LANES)], axis=1)
            dke = jnp.concatenate([_dot(dab[hh], qe[:, hs], TN) for hh, hs in enumerate(HEAD_LANES)], axis=1)
            dqb = jnp.concatenate(dqb, axis=1)
            dkb = jnp.concatenate(dkb, axis=1)
            state_term = jnp.concatenate(
                [jnp.sum(dst[hh] * st[hh], axis=0, keepdims=True) for hh in range(HGRN_HEADS)], axis=1)
            dq = dqe * ebm + dqb * eb
            dk = dke * embm + dkb * ebl
            db = (qe.astype(F32) * dqe - ke.astype(F32) * dke) + q * (dqb * eb) - k * (dkb * ebl)
            d_last = jnp.sum(k * ebl * dkb, axis=0, keepdims=True) + state_term * e_last
            dg = _dot_exact_lhs(ones_u, db) + d_last
            df = dg / f - dk
            first = HQ_COL * HGRN_DH
            dp_ref[sl, first:first + HGRN_W] = (dq * (sq * (1.0 + hq * (1.0 - sq)))).astype(BF16)
            dp_ref[sl, first + HGRN_W:first + 2 * HGRN_W] = (df * (1.0 - lb_ref[...]) * sig * (1.0 - sig)).astype(BF16)
            dp_ref[sl, first + 2 * HGRN_W:first + 3 * HGRN_W] = jnp.concatenate(dv, axis=1).astype(BF16)
            dp_ref[sl, first + 3 * HGRN_W:first + 4 * HGRN_W] = dhg.astype(BF16)
            dlb_scr[...] += jnp.sum(df * (1.0 - sig), axis=0, keepdims=True)
            dgo_scr[...] += dgo_inc
            for hh in range(HGRN_HEADS):
                ds_scr[hh] = dst_next[hh]
            return carry

        lax.fori_loop(0, cpb, step, 0)

        @pl.when(pl.program_id(1) == nblk - 1)
        def _():
            dlb_ref[...] = dlb_scr[...]
            dgo_ref[...] = dgo_scr[...]

    rows = lambda bi, i: bi * nblk + (nblk - 1 - i)
    col = lambda base: pl.BlockSpec((HGRN_ROWS, HGRN_W), lambda bi, i: (rows(bi, i), base // HGRN_HEADS))
    out = pl.BlockSpec((HGRN_ROWS, HGRN_W), lambda bi, i: (rows(bi, i), 0))
    part = pl.BlockSpec((None, 1, HGRN_W), lambda bi, i: (bi, 0, 0))
    width = HG_COL * HGRN_DH + HGRN_W
    o_shape = jax.ShapeDtypeStruct((t, width), BF16)
    p_shape = jax.ShapeDtypeStruct((b, 1, HGRN_W), F32)
    return pl.pallas_call(
        body,
        name=name,
        grid=(b, nblk),
        in_specs=[col(HQ_COL), col(HF_COL), col(HI_COL), col(HG_COL), out, out, out,
                  pl.BlockSpec((1, HGRN_W), lambda bi, i: (0, 0)), pl.BlockSpec((1, HGRN_DH), lambda bi, i: (0, 0)), out,
                  pl.BlockSpec((None, HGRN_HEADS, cpb, HGRN_DH, HGRN_DH), lambda bi, i: (bi, 0, nblk - 1 - i, 0, 0)),
                  col(ATTN_W // HGRN_DH)],
        out_specs=[pl.BlockSpec((HGRN_ROWS, width), lambda bi, i: (rows(bi, i), 0))] + [part] * 2,
        out_shape=[o_shape] + [p_shape] * 2,
        scratch_shapes=[pltpu.VMEM((HGRN_HEADS, HGRN_DH, HGRN_DH), F32), pltpu.VMEM((1, HGRN_W), F32),
                        pltpu.VMEM((1, HGRN_W), F32)],
        compiler_params=_params("parallel", "arbitrary"),
    )(proj, proj, proj, proj, *dqkv, lb, go, oraw, states, dmix)


def _small_grads(name, dg1, dgm, dg2, dgq, dgk, dbias_t, dlb, dgo, lbp):
    d = dg1.shape[1]

    def body(dg1_ref, dgm_ref, dg2_ref, dgq_ref, dgk_ref, dbias_ref, dlb_ref, dgo_ref, lbp_ref,
             g1_ref, gm_ref, g2_ref, gq_ref, gk_ref, rb_ref, lbg_ref, go_ref):
        g1_ref[...] = jnp.sum(dg1_ref[...], axis=0, keepdims=True)
        gm_ref[...] = jnp.sum(dgm_ref[...], axis=0, keepdims=True)
        g2_ref[...] = jnp.sum(dg2_ref[...], axis=0, keepdims=True)
        r = lax.broadcasted_iota(jnp.int32, (ATTN_W, ATTN_DH), 0)
        cidx = lax.broadcasted_iota(jnp.int32, (ATTN_W, ATTN_DH), 1)
        fold = jnp.where(jnp.bitwise_and(r, ATTN_DH - 1) == cidx, 1.0, 0.0).astype(BF16)
        gq_ref[...] = jnp.sum(_dot_exact_rhs(dgq_ref[...], fold), axis=0, keepdims=True)
        gk_ref[...] = jnp.sum(_dot_exact_rhs(dgk_ref[...], fold), axis=0, keepdims=True)
        gosum = jnp.sum(dgo_ref[...], axis=0, keepdims=True)
        go_ref[...] = (gosum[:, 0:HGRN_DH] + gosum[:, HGRN_DH:2 * HGRN_DH]
                       + gosum[:, 2 * HGRN_DH:3 * HGRN_DH] + gosum[:, 3 * HGRN_DH:4 * HGRN_DH])
        p0 = lbp_ref[0:1, :]
        p1 = lbp_ref[1:2, :]
        lbv = 1.0 / (1.0 + jnp.exp(p1 - p0))
        dp0 = jnp.sum(dlb_ref[...], axis=0, keepdims=True) * lbv * (1.0 - lbv)
        lbg_ref[0:1, :] = dp0
        lbg_ref[1:2, :] = -dp0
        sidx = lax.broadcasted_iota(jnp.int32, (BAND, N_REL_PAD), 0)
        ridx = lax.broadcasted_iota(jnp.int32, (BAND, N_REL_PAD), 1)

        def step(tq, acc):
            rel = jnp.clip(tq + KPAD - sidx, -REL_CLIP, REL_CLIP) + REL_CLIP
            onehot = jnp.where(rel == ridx, 1.0, 0.0).astype(BF16)
            return acc + _dot_exact_rhs(dbias_ref[tq], onehot)

        rb_ref[...] = lax.fori_loop(0, CHUNK, step, jnp.zeros((ATTN_HEADS, N_REL_PAD), F32))

    ins = [dg1, dgm, dg2, dgq, dgk, dbias_t, dlb, dgo, lbp]
    outs = [jax.ShapeDtypeStruct((1, d), F32)] * 3 + [jax.ShapeDtypeStruct((1, ATTN_DH), F32)] * 2 + [
        jax.ShapeDtypeStruct((ATTN_HEADS, N_REL_PAD), F32), jax.ShapeDtypeStruct((2, HGRN_W), F32),
        jax.ShapeDtypeStruct((1, HGRN_DH), F32)]
    vm = pl.BlockSpec(memory_space=pltpu.VMEM)
    return pl.pallas_call(
        body,
        name=name,
        in_specs=[vm] * len(ins),
        out_specs=[vm] * len(outs),
        out_shape=outs,
        compiler_params=pltpu.CompilerParams(vmem_limit_bytes=VMEM_LIMIT),
    )(*ins)


def _adam_update(w, g, m, v):
    m2 = ADAM_B1 * m + (1.0 - ADAM_B1) * g
    v2 = ADAM_B2 * v + (1.0 - ADAM_B2) * (g * g)
    m_hat = m2 / (1.0 - ADAM_B1 ** ADAM_STEP)
    v_hat = v2 / (1.0 - ADAM_B2 ** ADAM_STEP)
    delta = -ADAM_LR * (m_hat / (jnp.sqrt(v_hat) + ADAM_EPS) + ADAM_WD * w)
    return delta, m2, v2


def _rows_tile(r):
    return r if r <= 512 or r % 512 else 512


def _pair_sum(name, grad, theirs, core):
    n, half, c = theirs.shape
    tr = _rows_tile(half)
    nth = half // tr

    def body(core_ref, a_ref, b_ref, o_ref):
        o_ref[...] = (a_ref[...].astype(F32) + b_ref[...].astype(F32)).astype(o_ref.dtype)

    spec = pl.BlockSpec((None, tr, c), lambda i, j, core_ref: (i, j, 0))
    return pl.pallas_call(
        body, name=name,
        grid_spec=pltpu.PrefetchScalarGridSpec(
            num_scalar_prefetch=1, grid=(n, nth),
            in_specs=[pl.BlockSpec((None, tr, c), lambda i, j, core_ref: (i, core_ref[0] * nth + j, 0)), spec],
            out_specs=spec),
        out_shape=pltpu.HBM((n, half, c), BF16), compiler_params=_params("parallel", "parallel"),
    )(core, grad, theirs)


def _chip_sum(name, own, parts, chip):
    _, half, c = own.shape
    tr = _rows_tile(half)

    def body(chip_ref, own_ref, p_ref, o_ref):
        me = chip_ref[0]
        mine = own_ref[...].astype(F32)
        flip_x, flip_y, flip_xy = (p_ref[i].astype(F32) for i in range(3))
        acc = None
        for k in range(N_CHIPS):
            rel = jnp.bitwise_xor(me, k)
            term = jnp.where(rel == 0, mine, jnp.where(rel == 2, flip_x, jnp.where(rel == 1, flip_y, flip_xy)))
            acc = term if acc is None else acc + term
        o_ref[...] = acc

    return pl.pallas_call(
        body, name=name,
        grid_spec=pltpu.PrefetchScalarGridSpec(
            num_scalar_prefetch=1, grid=(half // tr,),
            in_specs=[pl.BlockSpec((None, tr, c), lambda j, chip_ref: (chip_ref[0], j, 0)),
                      pl.BlockSpec((3, tr, c), lambda j, chip_ref: (0, j, 0))],
            out_specs=pl.BlockSpec((tr, c), lambda j, chip_ref: (j, 0))),
        out_shape=pltpu.HBM((half, c), F32), compiler_params=_params("parallel"),
    )(chip, own, parts)


def _adamw(name, w, g_mine, g_theirs, m, v, core):
    _, r, c = w.shape
    half = r // 2
    tr = _rows_tile(half)
    nth = half // tr

    def body(core_ref, w_ref, gm_ref, gt_ref, m_ref, v_ref, g_ref, d_ref, m2_ref, v2_ref):
        g = jnp.where(pl.program_id(0) == core_ref[0], gm_ref[...], gt_ref[...])
        delta, m2, v2 = _adam_update(w_ref[...], g, m_ref[...], v_ref[...])
        g_ref[...] = g
        d_ref[...] = delta
        m2_ref[...] = m2
        v2_ref[...] = v2

    full = pl.BlockSpec((None, tr, c), lambda h, j, core_ref: (0, h * nth + j, 0))
    part = pl.BlockSpec((tr, c), lambda h, j, core_ref: (j, 0))
    shape = jax.ShapeDtypeStruct((1, r, c), F32)
    return pl.pallas_call(
        body, name=name,
        grid_spec=pltpu.PrefetchScalarGridSpec(
            num_scalar_prefetch=1, grid=(2, nth), in_specs=[full, part, part, full, full], out_specs=[full] * 4),
        out_shape=[shape] * 4, compiler_params=_params("parallel", "parallel"),
    )(core, w, g_mine, g_theirs, m, v)


def _rel_bias_table(name, rel_bias):
    padded = jnp.pad(rel_bias, ((0, 0), (0, N_REL_PAD - N_REL)))

    def body(rb_ref, o_ref):
        ridx = lax.broadcasted_iota(jnp.int32, (N_REL_PAD, BAND), 0)
        sidx = lax.broadcasted_iota(jnp.int32, (N_REL_PAD, BAND), 1)
        rb = rb_ref[...]

        def step(tq, carry):
            rel = jnp.clip(tq + KPAD - sidx, -REL_CLIP, REL_CLIP) + REL_CLIP
            onehot = jnp.where(rel == ridx, 1.0, 0.0).astype(BF16)
            o_ref[tq] = _dot_exact_rhs(rb, onehot)
            return carry

        lax.fori_loop(0, CHUNK, step, 0)

    vm = pl.BlockSpec(memory_space=pltpu.VMEM)
    table = pl.pallas_call(
        body, name=name, in_specs=[vm], out_specs=vm,
        out_shape=jax.ShapeDtypeStruct((CHUNK, ATTN_HEADS, BAND), F32),
    )(padded)
    return table.transpose(1, 0, 2)


def _adamw_small(name, w, parts, m, v):
    def body(w_ref, p_ref, m_ref, v_ref, g_ref, d_ref, m2_ref, v2_ref):
        g = p_ref[0]
        for i in range(1, N_DEV):
            g = g + p_ref[i]
        delta, m2, v2 = _adam_update(w_ref[...], g, m_ref[...], v_ref[...])
        g_ref[...] = g
        d_ref[...] = delta
        m2_ref[...] = m2
        v2_ref[...] = v2

    vm = pl.BlockSpec(memory_space=pltpu.VMEM)
    shape = jax.ShapeDtypeStruct((SMALL_ROWS, SMALL_COLS), F32)
    return pl.pallas_call(
        body, name=name, in_specs=[vm] * 4, out_specs=[vm] * 4, out_shape=[shape] * 4,
    )(w, parts, m, v)


def _position():
    return lax.axis_index("x"), lax.axis_index("y"), lax.axis_index("c")


def _other_chips(x, y):
    return [(1 - x, y), (x, 1 - y), (1 - x, 1 - y)]


ANY = pl.BlockSpec(memory_space=pl.ANY)
PAIR_ID = 0


def _pair_handshake():
    x, y, c = _position()
    barrier = pltpu.get_barrier_semaphore()
    pl.semaphore_signal(barrier, inc=1, device_id=(x, y, 1 - c), device_id_type=MESH)
    pl.semaphore_wait(barrier, 1)


PAIR_CALL = pltpu.CompilerParams(collective_id=PAIR_ID)


HBM = pl.BlockSpec(memory_space=pltpu.HBM)
SEM = pl.BlockSpec(memory_space=pltpu.SEMAPHORE)
SPLIT_COPY = pltpu.SideEffectType.DATAFLOW_SIDE_EFFECTING


def _gather_copy(shards, outs, send_sem, recv_sem, i, j):
    x, y, c = _position()
    chips = _other_chips(x, y)
    half = shards[i].shape[0] // 2
    rows = pl.ds(pl.multiple_of(c * half, 16), half)
    return pltpu.make_async_remote_copy(
        src_ref=shards[i].at[rows, :], dst_ref=outs[i].at[2 * x + y, rows, :],
        send_sem=send_sem.at[3 * i + j], recv_sem=recv_sem.at[3 * i + j],
        device_id=(chips[j][0], chips[j][1], c), device_id_type=MESH)


def _gather_start(name, shards, after):
    n = len(shards)

    def body(*refs):
        srcs, outs = refs[:n], refs[n:2 * n]
        send_sem, recv_sem = refs[2 * n + len(after)], refs[2 * n + len(after) + 1]
        token = refs[-1]
        for i in range(n):
            for j in range(3):
                _gather_copy(srcs, outs, send_sem, recv_sem, i, j).start()
        token[...] = jnp.zeros_like(token)

    full = [(N_CHIPS,) + s.shape for s in shards]
    res = pl.pallas_call(
        body,
        name=name,
        in_specs=[HBM] * (2 * n) + [ANY] * len(after),
        out_specs=[SEM, SEM] + [HBM] * (2 * n) + [pl.BlockSpec(memory_space=pltpu.VMEM)],
        out_shape=[pltpu.SemaphoreType.DMA((3 * n,)), pltpu.SemaphoreType.DMA((3 * n,))]
        + [pltpu.HBM(s.shape, s.dtype) for s in shards]
        + [pltpu.HBM(shp, s.dtype) for shp, s in zip(full, shards)]
        + [jax.ShapeDtypeStruct((8, LANES), F32)],
        input_output_aliases={i: 2 + i for i in range(2 * n)},
        compiler_params=pltpu.CompilerParams(has_side_effects=SPLIT_COPY),
    )(*[pltpu.with_memory_space_constraint(s, pltpu.HBM) for s in shards],
      *[pltpu.with_memory_space_constraint(lax.empty(shp, s.dtype), pltpu.HBM) for shp, s in zip(full, shards)],
      *after)
    return res[0], res[1], list(res[2:2 + n]), list(res[2 + n:2 + 2 * n]), res[-1]


def _gather_wait(name, send_sem, recv_sem, shards, outs, after):
    n = len(shards)

    def body(*refs):
        srcs, out_refs = refs[:n], refs[n:2 * n]
        send_ref, recv_ref = refs[2 * n], refs[2 * n + 1]
        for i in range(n):
            for j in range(3):
                copy = _gather_copy(srcs, out_refs, send_ref, recv_ref, i, j)
                copy.wait_send()
                copy.wait_recv()

    res = pl.pallas_call(
        body,
        name=name,
        in_specs=[HBM] * (2 * n) + [SEM, SEM] + [ANY] * len(after),
        out_specs=[HBM] * (2 * n),
        out_shape=[pltpu.HBM(s.shape, s.dtype) for s in shards] + [pltpu.HBM(o.shape, o.dtype) for o in outs],
        input_output_aliases={i: i for i in range(2 * n)},
        compiler_params=pltpu.CompilerParams(has_side_effects=SPLIT_COPY),
    )(*shards, *outs, send_sem, recv_sem, *after)
    return list(res[:n]), list(res[n:])


def _join_copies(srcs, ins, outs, own_send, own_recv, half_send, half_recv):
    x, y, c = _position()
    chips = _other_chips(x, y)
    copies = []
    for i in range(len(srcs)):
        copies.append(pltpu.make_async_remote_copy(
            src_ref=srcs[i], dst_ref=outs[i].at[2 * x + y], send_sem=own_send.at[i], recv_sem=own_recv.at[i],
            device_id=(x, y, 1 - c), device_id_type=MESH))
        half = srcs[i].shape[0] // 2
        rows = pl.ds(pl.multiple_of(c * half, 16), half)
        for j in range(3):
            slot = 2 * chips[j][0] + chips[j][1]
            copies.append(pltpu.make_async_remote_copy(
                src_ref=ins[i].at[slot, rows, :], dst_ref=outs[i].at[slot, rows, :],
                send_sem=half_send.at[3 * i + j], recv_sem=half_recv.at[3 * i + j],
                device_id=(x, y, 1 - c), device_id_type=MESH))
    return copies


def _gather_join(name, shards, outs):
    n = len(shards)

    def body(*refs):
        _pair_handshake()
        copies = _join_copies(refs[:n], refs[n:2 * n], refs[2 * n:3 * n], *refs[3 * n:])
        for cp in copies:
            cp.start()
        for cp in copies:
            cp.wait()

    return pl.pallas_call(
        body,
        name=name,
        in_specs=[ANY] * (2 * n),
        out_specs=[HBM] * n,
        out_shape=[pltpu.HBM(o.shape, o.dtype) for o in outs],
        input_output_aliases={n + i: i for i in range(n)},
        scratch_shapes=[pltpu.SemaphoreType.DMA((n,))] * 2 + [pltpu.SemaphoreType.DMA((3 * n,))] * 2,
        compiler_params=PAIR_CALL,
    )(*shards, *outs)


def _join_start(name, shards, outs):
    n = len(shards)

    def body(*refs):
        _pair_handshake()
        srcs, arrs = refs[:n], refs[n:2 * n]
        sems = refs[2 * n:2 * n + 4]
        token = refs[-1]
        for cp in _join_copies(srcs, arrs, arrs, *sems):
            cp.start()
        token[...] = jnp.zeros_like(token)

    res = pl.pallas_call(
        body,
        name=name,
        in_specs=[HBM] * (2 * n),
        out_specs=[SEM] * 4 + [HBM] * (2 * n) + [pl.BlockSpec(memory_space=pltpu.VMEM)],
        out_shape=[pltpu.SemaphoreType.DMA((n,))] * 2 + [pltpu.SemaphoreType.DMA((3 * n,))] * 2
        + [pltpu.HBM(s.shape, s.dtype) for s in shards] + [pltpu.HBM(o.shape, o.dtype) for o in outs]
        + [jax.ShapeDtypeStruct((8, LANES), F32)],
        input_output_aliases={i: 4 + i for i in range(2 * n)},
        compiler_params=pltpu.CompilerParams(has_side_effects=SPLIT_COPY, collective_id=PAIR_ID),
    )(*shards, *outs)
    return list(res[:4]), list(res[4:4 + n]), list(res[4 + n:4 + 2 * n]), res[-1]


def _join_wait(name, sems, shards, outs, after):
    n = len(shards)

    def body(*refs):
        srcs, arrs = refs[:n], refs[n:2 * n]
        for cp in _join_copies(srcs, arrs, arrs, *refs[2 * n:2 * n + 4]):
            cp.wait_send()
            cp.wait_recv()

    res = pl.pallas_call(
        body,
        name=name,
        in_specs=[HBM] * (2 * n) + [SEM] * 4 + [ANY] * len(after),
        out_specs=[HBM] * (2 * n),
        out_shape=[pltpu.HBM(s.shape, s.dtype) for s in shards] + [pltpu.HBM(o.shape, o.dtype) for o in outs],
        input_output_aliases={i: i for i in range(2 * n)},
        compiler_params=pltpu.CompilerParams(has_side_effects=SPLIT_COPY),
    )(*shards, *outs, *sems, *after)
    return list(res[n:])


def _pair_copy(grads, lands, send_sem, recv_sem, i):
    x, y, c = _position()
    half = grads[i].shape[1] // 2
    give = pl.ds(pl.multiple_of((1 - c) * half, 16), half)
    return pltpu.make_async_remote_copy(
        src_ref=grads[i].at[:, give, :], dst_ref=lands[i], send_sem=send_sem.at[i], recv_sem=recv_sem.at[i],
        device_id=(x, y, 1 - c), device_id_type=MESH)


def _pair_start(name, grads):
    n = len(grads)

    def body(*refs):
        _pair_handshake()
        srcs, lands = refs[:n], refs[n:2 * n]
        send_sem, recv_sem = refs[2 * n], refs[2 * n + 1]
        token = refs[-1]
        for i in range(n):
            _pair_copy(srcs, lands, send_sem, recv_sem, i).start()
        token[...] = jnp.zeros_like(token)

    halves = [(g.shape[0], g.shape[1] // 2, g.shape[2]) for g in grads]
    res = pl.pallas_call(
        body,
        name=name,
        in_specs=[HBM] * (2 * n),
        out_specs=[SEM, SEM] + [HBM] * (2 * n) + [pl.BlockSpec(memory_space=pltpu.VMEM)],
        out_shape=[pltpu.SemaphoreType.DMA((n,)), pltpu.SemaphoreType.DMA((n,))]
        + [pltpu.HBM(g.shape, g.dtype) for g in grads]
        + [pltpu.HBM(shp, g.dtype) for shp, g in zip(halves, grads)]
        + [jax.ShapeDtypeStruct((8, LANES), F32)],
        input_output_aliases={i: 2 + i for i in range(2 * n)},
        compiler_params=pltpu.CompilerParams(has_side_effects=SPLIT_COPY, collective_id=PAIR_ID),
    )(*[pltpu.with_memory_space_constraint(g, pltpu.HBM) for g in grads],
      *[pltpu.with_memory_space_constraint(lax.empty(shp, g.dtype), pltpu.HBM) for shp, g in zip(halves, grads)])
    return res[0], res[1], list(res[2:2 + n]), list(res[2 + n:2 + 2 * n]), res[-1]


def _pair_wait(name, send_sem, recv_sem, grads, lands, after):
    n = len(grads)

    def body(*refs):
        srcs, land_refs = refs[:n], refs[n:2 * n]
        send_ref, recv_ref = refs[2 * n], refs[2 * n + 1]
        for i in range(n):
            copy = _pair_copy(srcs, land_refs, send_ref, recv_ref, i)
            copy.wait_send()
            copy.wait_recv()

    res = pl.pallas_call(
        body,
        name=name,
        in_specs=[HBM] * (2 * n) + [SEM, SEM, ANY],
        out_specs=[HBM] * (2 * n),
        out_shape=[pltpu.HBM(g.shape, g.dtype) for g in grads] + [pltpu.HBM(l.shape, l.dtype) for l in lands],
        input_output_aliases={i: i for i in range(2 * n)},
        compiler_params=pltpu.CompilerParams(has_side_effects=SPLIT_COPY),
    )(*grads, *lands, send_sem, recv_sem, after)
    return list(res[:n]), list(res[n:])


def _scatter_copy(srcs, lands, send_sem, recv_sem, i, j):
    x, y, c = _position()
    chips = _other_chips(x, y)
    return pltpu.make_async_remote_copy(
        src_ref=srcs[i].at[2 * chips[j][0] + chips[j][1]], dst_ref=lands[i].at[j],
        send_sem=send_sem.at[3 * i + j], recv_sem=recv_sem.at[3 * i + j],
        device_id=(chips[j][0], chips[j][1], c), device_id_type=MESH)


def _scatter_start(name, sums):
    n = len(sums)

    def body(*refs):
        srcs, lands = refs[:n], refs[n:2 * n]
        send_sem, recv_sem = refs[2 * n], refs[2 * n + 1]
        token = refs[-1]
        for i in range(n):
            for j in range(3):
                _scatter_copy(srcs, lands, send_sem, recv_sem, i, j).start()
        token[...] = jnp.zeros_like(token)

    land_shapes = [(3,) + s.shape[1:] for s in sums]
    res = pl.pallas_call(
        body,
        name=name,
        in_specs=[HBM] * (2 * n),
        out_specs=[SEM, SEM] + [HBM] * (2 * n) + [pl.BlockSpec(memory_space=pltpu.VMEM)],
        out_shape=[pltpu.SemaphoreType.DMA((3 * n,)), pltpu.SemaphoreType.DMA((3 * n,))]
        + [pltpu.HBM(s.shape, s.dtype) for s in sums]
        + [pltpu.HBM(shp, s.dtype) for shp, s in zip(land_shapes, sums)]
        + [jax.ShapeDtypeStruct((8, LANES), F32)],
        input_output_aliases={i: 2 + i for i in range(2 * n)},
        compiler_params=pltpu.CompilerParams(has_side_effects=SPLIT_COPY),
    )(*[pltpu.with_memory_space_constraint(s, pltpu.HBM) for s in sums],
      *[pltpu.with_memory_space_constraint(lax.empty(shp, s.dtype), pltpu.HBM) for shp, s in zip(land_shapes, sums)])
    return res[0], res[1], list(res[2:2 + n]), list(res[2 + n:2 + 2 * n]), res[-1]


def _scatter_wait(name, send_sem, recv_sem, sums, lands, after):
    n = len(sums)

    def body(*refs):
        srcs, land_refs = refs[:n], refs[n:2 * n]
        send_ref, recv_ref = refs[2 * n], refs[2 * n + 1]
        for i in range(n):
            for j in range(3):
                copy = _scatter_copy(srcs, land_refs, send_ref, recv_ref, i, j)
                copy.wait_send()
                copy.wait_recv()

    res = pl.pallas_call(
        body,
        name=name,
        in_specs=[HBM] * (2 * n) + [SEM, SEM, ANY],
        out_specs=[HBM] * (2 * n),
        out_shape=[pltpu.HBM(s.shape, s.dtype) for s in sums] + [pltpu.HBM(l.shape, l.dtype) for l in lands],
        input_output_aliases={i: i for i in range(2 * n)},
        compiler_params=pltpu.CompilerParams(has_side_effects=SPLIT_COPY),
    )(*sums, *lands, send_sem, recv_sem, after)
    return list(res[:n]), list(res[n:])


def _pair_join(name, halves, small=None):
    n = len(halves)
    if small is None:
        def body_plain(*refs):
            _pair_handshake()
            ins, outs = refs[:n], refs[n:2 * n]
            send_sem, recv_sem = refs[2 * n:]
            x, y, c = _position()
            swaps = [pltpu.make_async_remote_copy(
                src_ref=ins[i], dst_ref=outs[i], send_sem=send_sem.at[i], recv_sem=recv_sem.at[i],
                device_id=(x, y, 1 - c), device_id_type=MESH) for i in range(n)]
            for swap in swaps:
                swap.start()
            for swap in swaps:
                swap.wait()

        return pl.pallas_call(
            body_plain,
            name=name,
            in_specs=[ANY] * n,
            out_specs=[ANY] * n,
            out_shape=[jax.ShapeDtypeStruct(h.shape, h.dtype) for h in halves],
            scratch_shapes=[pltpu.SemaphoreType.DMA((n,))] * 2,
            compiler_params=PAIR_CALL,
        )(*halves)

    def body(*refs):
        ins, small_ref = refs[:n], refs[n]
        outs, all_ref = refs[n + 1:2 * n + 1], refs[2 * n + 1]
        send_sem, recv_sem, sm_send, sm_recv, sm_local = refs[2 * n + 2:]
        x, y, c = _position()
        swaps = []
        for i in range(n):
            swap = pltpu.make_async_remote_copy(
                src_ref=ins[i], dst_ref=outs[i], send_sem=send_sem.at[i], recv_sem=recv_sem.at[i],
                device_id=(x, y, 1 - c), device_id_type=MESH)
            swap.start()
            swaps.append(swap)
        me = 4 * x + 2 * y + c
        sm_own = pltpu.make_async_copy(small_ref, all_ref.at[me], sm_local)
        sm_own.start()
        pushes, arrivals = [], []
        for mask in range(1, N_DEV):
            px, py, pc = x ^ (mask >> 2), y ^ ((mask >> 1) & 1), c ^ (mask & 1)
            pushes.append(pltpu.make_async_remote_copy(
                src_ref=small_ref, dst_ref=all_ref.at[me], send_sem=sm_send.at[mask - 1], recv_sem=sm_recv.at[mask - 1],
                device_id=(px, py, pc), device_id_type=MESH))
            arrivals.append(pltpu.make_async_remote_copy(
                src_ref=small_ref, dst_ref=all_ref.at[4 * px + 2 * py + pc], send_sem=sm_send.at[mask - 1],
                recv_sem=sm_recv.at[mask - 1], device_id=(px, py, pc), device_id_type=MESH))
        for cp in pushes:
            cp.start()
        for swap in swaps:
            swap.wait()
        for cp in arrivals:
            cp.wait_recv()
        for cp in pushes:
            cp.wait_send()
        sm_own.wait()

    res = pl.pallas_call(
        body,
        name=name,
        in_specs=[ANY] * (n + 1),
        out_specs=[ANY] * (n + 1),
        out_shape=[jax.ShapeDtypeStruct(h.shape, h.dtype) for h in halves]
        + [jax.ShapeDtypeStruct((N_DEV,) + small.shape, small.dtype)],
        scratch_shapes=[pltpu.SemaphoreType.DMA((n,))] * 2 + [pltpu.SemaphoreType.DMA((N_DEV - 1,))] * 2
        + [pltpu.SemaphoreType.DMA(())],
    )(*halves, small)
    return res[:n], res[n]


def _lower_bound(lbp):
    return jax.nn.softmax(lbp, axis=0)[0:1]


def _local_step(x, target, g1, gm, g2, gq, gk, go, rel_bias, lbp, weights, on_grads, grads_sent):
    b, s, d = x.shape
    t = b * s
    x0 = x.reshape(t, d)
    tgt = target.reshape(t, d)
    gq_t = jnp.tile(gq, (1, ATTN_HEADS))
    gk_t = jnp.tile(gk, (1, ATTN_HEADS))
    lb = _lower_bound(lbp)
    table = _band_table(_rel_bias_table("rel_bias_table", rel_bias))

    h1 = _rmsnorm_fwd("norm1", x0, g1)
    wg1, wu1, deps1 = weights["first"]((h1, table))
    a1, b1, z1 = _ffn_up("ffn1_up", h1, wg1, wu1, deps1)
    wd1, deps_mid = weights["mid"]((z1,))
    x1, h2 = _ffn_down("ffn1_down", z1, wd1, x0, gm, deps_mid)
    w_in, w_out = weights["mid_rest"]((x1,))
    ns = w_in.shape[0]
    proj = _in_proj("in_proj", h2, w_in)
    proj3 = proj.reshape(b, s, proj.shape[1])
    qn, kn, vb = _qk_prep("qk_prep", proj3, gq_t, gk_t)
    attn = _attn_fwd("attn_fwd", qn, kn, vb, table, weights["last_begin"]((qn,))).reshape(t, ATTN_W)
    mix, oraw, states = _hgrn_fwd("hgrn_fwd", proj, attn, lb, go, b, s)
    x2, h3 = _out_proj("out_proj", mix, w_out, x1, g2)
    wg2, wu2, wd2 = weights["last"]((h3,))
    a2, b2, z2 = _ffn_up("ffn2_up", h3, wg2, wu2)
    dy, dyh, sq = _ffn_down_loss("ffn2_down_loss", z2, wd2, x2, tgt)
    loss = 0.5 * jnp.sum(sq) / d

    da2, db2 = _ffn_bwd_act("ffn2_bwd_act", dyh, wd2, a2, b2)
    dwd2 = _grad_w_cols("ffn2_dwd", z2, dyh)
    dwg2 = _grad_w_cols("ffn2_dwg", da2, h3)
    dwu2 = _grad_w_cols("ffn2_dwu", db2, h3)
    sent2 = on_grads("ffn2", {"ffn2_w_gate": dwg2, "ffn2_w_up": dwu2, "ffn2_w_down": dwd2})
    dx2, dx2b, dg2 = _ffn_bwd_in("ffn2_bwd_in", da2, db2, wg2, wu2, x2, g2, dy, 1.0, sent2)
    sent2 = grads_sent("ffn2", dx2b)

    dwout = _grad_w_out("dw_out", mix, dx2b)
    dmix = _out_proj_bwd("out_proj_bwd", dx2b, w_out, sent2)
    dqn, dkn, dvn, dbe, dbo = _attn_bwd("attn_bwd", qn, kn, vb, table, dmix.reshape(b, s, dmix.shape[1]))
    dbias = dbe[:, :, :BAND] + dbo[:, :, CHUNK:]
    dpq, dpk, dpv, dgq, dgk = _qk_prep_bwd("qk_prep_bwd", proj3, dqn, dkn, dvn, gq_t, gk_t)
    dpq, dpk, dpv = (a.reshape(t, ATTN_W) for a in (dpq, dpk, dpv))
    dproj, dlb, dgo = _hgrn_bwd("hgrn_bwd", proj, (dpq, dpk, dpv), lb, go, oraw, states, dmix, b, s)
    dwin = _grad_w_in("dw_in", h2, dproj, ns)
    dx1, dx1h, dgm = _in_proj_bwd("in_proj_bwd", dproj, w_in, x1, gm, dx2, 0.5)

    dwd1 = _grad_w_cols("ffn1_dwd", z1, dx1h)
    sent_mix = on_grads("mix", {"w_in": dwin, "w_out": dwout.reshape(ns, dwout.shape[0] // ns, d),
                                "ffn1_w_down": dwd1})
    da1, db1 = _ffn_bwd_act("ffn1_bwd_act", dx1h, wd1, a1, b1, sent_mix)
    sent_mix = grads_sent("mix", da1)
    dwg1 = _grad_w_cols("ffn1_dwg", da1, h1, sent_mix)
    dwu1 = _grad_w_cols("ffn1_dwu", db1, h1)
    on_grads("ffn1", {"ffn1_w_gate": dwg1, "ffn1_w_up": dwu1})
    sent1 = grads_sent("ffn1", None)
    dx0, dg1 = _ffn_bwd_in("ffn1_bwd_in", da1, db1, wg1, wu1, x0, g1, dx1, None, sent1)

    nt = dg1.shape[0]
    sg = _small_grads(
        "small_grads", dg1.reshape(nt, d), dgm.reshape(nt, d), dg2.reshape(nt, d),
        dgq.reshape(-1, ATTN_W), dgk.reshape(-1, ATTN_W), dbias.transpose(1, 0, 2),
        dlb.reshape(b, HGRN_W), dgo.reshape(b, HGRN_W), lbp)
    g1g, gmg, g2g, gqg, gkg, rbg, lbg, gog = sg
    small = _pack_small(g1g, gmg, g2g, lbg, rbg[:, :N_REL], gqg, gkg, gog, loss)
    return dx0.reshape(b, s, d), small


LOSS_SLOT = 7 * SMALL_COLS + 2 * ATTN_DH + HGRN_DH


def _pack_small(g1, gm, g2, lbp, rel_bias, gq, gk, go, loss=None):
    flat = [g1.reshape(-1), gm.reshape(-1), g2.reshape(-1), lbp.reshape(-1), rel_bias.reshape(-1)]
    n_bias = 3 * SMALL_COLS - rel_bias.size
    heads = [gq.reshape(-1), gk.reshape(-1), go.reshape(-1)]
    heads.append(jnp.zeros((1,), F32) if loss is None else loss.reshape(1))
    n_tail = SMALL_COLS - sum(h.size for h in heads)
    return jnp.concatenate(flat + [jnp.zeros((n_bias,), F32)] + heads + [jnp.zeros((n_tail,), F32)]).reshape(
        SMALL_ROWS, SMALL_COLS)


def _unpack_small(p, d):
    flat = p.reshape(-1)
    o = 3 * d
    g1, gm, g2 = p[0:1], p[1:2], p[2:3]
    lbp = flat[o:o + 2 * HGRN_W].reshape(2, HGRN_W)
    o = 4 * SMALL_COLS
    rel = flat[o:o + ATTN_HEADS * N_REL].reshape(1, ATTN_HEADS, N_REL)
    o = 7 * SMALL_COLS
    gq = flat[o:o + ATTN_DH].reshape(1, ATTN_DH)
    gk = flat[o + ATTN_DH:o + 2 * ATTN_DH].reshape(1, ATTN_DH)
    go = flat[o + 2 * ATTN_DH:o + 2 * ATTN_DH + HGRN_DH].reshape(1, HGRN_DH)
    return g1, gm, g2, gq, gk, rel, lbp, go


def kernel(x, ffn1_norm_g, ffn1_w_gate, ffn1_w_up, ffn1_w_down, mix_norm_g, w_in, attn_q_norm_g, attn_k_norm_g, attn_rel_bias, hgrn_lower_bounds, hgrn_out_norm_g, w_out, ffn2_norm_g, ffn2_w_gate, ffn2_w_up, ffn2_w_down, loss_target, m_ffn1_norm_g, m_ffn1_w_gate, m_ffn1_w_up, m_ffn1_w_down, m_mix_norm_g, m_w_in, m_attn_q_norm_g, m_attn_k_norm_g, m_attn_rel_bias, m_hgrn_lower_bounds, m_hgrn_out_norm_g, m_w_out, m_ffn2_norm_g, m_ffn2_w_gate, m_ffn2_w_up, m_ffn2_w_down, v_ffn1_norm_g, v_ffn1_w_gate, v_ffn1_w_up, v_ffn1_w_down, v_mix_norm_g, v_w_in, v_attn_q_norm_g, v_attn_k_norm_g, v_attn_rel_bias, v_hgrn_lower_bounds, v_hgrn_out_norm_g, v_w_out, v_ffn2_norm_g, v_ffn2_w_gate, v_ffn2_w_up, v_ffn2_w_down):
    d = x.shape[-1]
    big_w = [ffn1_w_gate, ffn1_w_up, ffn1_w_down, w_in, w_out, ffn2_w_gate, ffn2_w_up, ffn2_w_down]
    big_m = [m_ffn1_w_gate, m_ffn1_w_up, m_ffn1_w_down, m_w_in, m_w_out, m_ffn2_w_gate, m_ffn2_w_up, m_ffn2_w_down]
    big_v = [v_ffn1_w_gate, v_ffn1_w_up, v_ffn1_w_down, v_w_in, v_w_out, v_ffn2_w_gate, v_ffn2_w_up, v_ffn2_w_down]
    big_names = ["ffn1_w_gate", "ffn1_w_up", "ffn1_w_down", "w_in", "w_out", "ffn2_w_gate", "ffn2_w_up", "ffn2_w_down"]
    flipped = {nm for nm in big_names if nm.endswith("gate") or nm.endswith("up")}
    flip = lambda nm, a: jnp.swapaxes(a, 1, 2) if nm in flipped else a
    big_w, big_m, big_v = ([flip(nm, a) for nm, a in zip(big_names, arrs)] for arrs in (big_w, big_m, big_v))

    shards = [w[0].astype(BF16) for w in big_w]
    start_a = _gather_start("gather_start_up1", shards[:2], ())
    start_b = _gather_start("gather_start_mid", shards[2:5], (start_a[4],))
    start_c = _gather_start("gather_start_ffn2", shards[5:], (start_b[4],))

    pending = {}

    def arrived(tag, started, after):
        send_sem, recv_sem, srcs, outs, _ = started
        return _gather_wait("gather_wait_" + tag, send_sem, recv_sem, srcs, outs, after)

    def first_weights(after):
        return (*_gather_join("gather_join_up1", *arrived("up1", start_a, after)), (start_c[4],))

    def mid_weights(after):
        srcs, outs = arrived("mid", start_b, after)
        (wd1,) = _gather_join("gather_join_wd1", srcs[:1], outs[:1])
        pending["mid"] = _join_start("join_start_mid", srcs[1:], outs[1:])
        return wd1, (pending["mid"][3],)

    def mid_rest(after):
        sems, srcs, outs, _ = pending["mid"]
        win_f, wout_f = _join_wait("join_wait_mid", sems, srcs, outs, after)
        return win_f, wout_f.reshape(wout_f.shape[0] * wout_f.shape[1], d)

    def last_begin(after):
        pending["ffn2"] = _join_start("join_start_ffn2", *arrived("ffn2", start_c, after))
        return (pending["ffn2"][3],)

    def last_weights(after):
        sems, srcs, outs, _ = pending["ffn2"]
        return _join_wait("join_wait_ffn2", sems, srcs, outs, after)

    weights = {"first": first_weights, "mid": mid_weights, "mid_rest": mid_rest, "last_begin": last_begin,
               "last": last_weights}

    core = lax.axis_index("c").astype(jnp.int32).reshape(1)
    chip = (2 * lax.axis_index("x") + lax.axis_index("y")).astype(jnp.int32).reshape(1)
    started = {}

    def on_grads(tag, grads):
        names = list(grads)
        started[tag] = (names, _pair_start("pair_start_" + tag, [grads[nm] for nm in names]))
        return (started[tag][1][4],)

    def grads_sent(tag, after):
        names, (send_sem, recv_sem, grads, lands, token) = started[tag]
        grads, theirs = _pair_wait("pair_wait_" + tag, send_sem, recv_sem, grads, lands, token if after is None else after)
        sums = [_pair_sum("pair_sum_" + nm, g, th, core) for nm, g, th in zip(names, grads, theirs)]
        started[tag] = (names, _scatter_start("scatter_start_" + tag, sums))
        return (started[tag][1][4],)

    grad_x, small_g = _local_step(
        x, loss_target, ffn1_norm_g, mix_norm_g, ffn2_norm_g, attn_q_norm_g, attn_k_norm_g, hgrn_out_norm_g,
        attn_rel_bias[0], hgrn_lower_bounds, weights, on_grads, grads_sent)

    def finish(tag, after):
        names, (send_sem, recv_sem, sums, lands, _) = started[tag]
        sums, lands = _scatter_wait("scatter_wait_" + tag, send_sem, recv_sem, sums, lands, after)
        return names, [_chip_sum("chip_sum_" + nm, sm, ld, chip) for nm, sm, ld in zip(names, sums, lands)]

    by_name = {nm: (w, m, v) for nm, w, m, v in zip(big_names, big_w, big_m, big_v)}
    updated = {}

    def update(names, halves, other_halves):
        for nm, mine, theirs in zip(names, halves, other_halves):
            w, m, v = by_name[nm]
            updated[nm] = _adamw("adamw_" + nm, w, mine, theirs, m, v, core)

    last_token = started["ffn1"][1][4]
    names_a, halves_a = finish("ffn2", last_token)
    names_m, halves_m = finish("mix", last_token)
    names_a, halves_a = names_a + names_m, halves_a + halves_m
    update(names_a, halves_a, _pair_join("pair_join_early", halves_a))
    names_b, halves_b = finish("ffn1", updated[names_a[-1]][1])
    others_b, small_all = _pair_join("pair_join_last", halves_b, small_g)
    update(names_b, halves_b, others_b)
    big_out = [updated[nm] for nm in big_names]

    pack = lambda g1, gm, g2, gq, gk, rel, lbp, go: _pack_small(g1, gm, g2, lbp, rel[0], gq, gk, go)
    small_w = pack(ffn1_norm_g, mix_norm_g, ffn2_norm_g, attn_q_norm_g, attn_k_norm_g, attn_rel_bias, hgrn_lower_bounds, hgrn_out_norm_g)
    small_m = pack(m_ffn1_norm_g, m_mix_norm_g, m_ffn2_norm_g, m_attn_q_norm_g, m_attn_k_norm_g, m_attn_rel_bias, m_hgrn_lower_bounds, m_hgrn_out_norm_g)
    small_v = pack(v_ffn1_norm_g, v_mix_norm_g, v_ffn2_norm_g, v_attn_q_norm_g, v_attn_k_norm_g, v_attn_rel_bias, v_hgrn_lower_bounds, v_hgrn_out_norm_g)
    small_res = _adamw_small("adamw_small", small_w, small_all, small_m, small_v)
    small_out = [_unpack_small(p, d) for p in small_res]
    loss = small_res[0].reshape(-1)[LOSS_SLOT]

    def assemble(kind):
        bg = [flip(nm, o[kind]) for nm, o in zip(big_names, big_out)]
        g1, gm, g2, gq, gk, rel, lbp, go = small_out[kind]
        return [g1, bg[0], bg[1], bg[2], gm, bg[3], gq, gk, rel, lbp, go, bg[4], g2, bg[5], bg[6], bg[7]]

    return (loss, grad_x, *assemble(0), *assemble(1), *assemble(2), *assemble(3))
```

```python
import functools

import jax
import jax.numpy as jnp
from jax import lax
from jax.experimental import pallas as pl
from jax.experimental.pallas import tpu as pltpu

F32 = jnp.float32
BF16 = jnp.bfloat16
MESH = pl.DeviceIdType.MESH

N_CHIPS = 4
N_DEV = 8
CHUNK = 64
ATTN_HEADS = 8
ATTN_DH = 64
ATTN_W = ATTN_HEADS * ATTN_DH
HGRN_HEADS = 4
HGRN_DH = 128
HGRN_W = HGRN_HEADS * HGRN_DH
LEFT_CHUNKS = 8
BAND = (LEFT_CHUNKS + 1) * CHUNK
KPAD = LEFT_CHUNKS * CHUNK
REL_CLIP = 128
N_REL = 2 * REL_CLIP + 1
N_REL_PAD = 384
RMS_EPS = 1e-6
LANES = 128
SMALL_ROWS = 8
SMALL_COLS = 1024

ADAM_LR = 0.001
ADAM_B1 = 0.9
ADAM_B2 = 0.999
ADAM_EPS = 1e-08
ADAM_WD = 0.01
ADAM_STEP = 10

NN = (((1,), (0,)), ((), ()))
NT = (((1,), (1,)), ((), ()))
TN = (((0,), (0,)), ((), ()))

VMEM_LIMIT = 48 * 1024 * 1024
MXU_WIDTH = 256
COL_CHUNK = 3 * MXU_WIDTH


def _sigmoid(x):
    return 1.0 / (1.0 + jnp.exp(-x))


def _silu(x):
    return x * _sigmoid(x)


def _dot(a, b, dims=NN):
    return lax.dot_general(a, b, dims, preferred_element_type=F32)


def _split3(x):
    hi = x.astype(BF16)
    r1 = x - hi.astype(F32)
    mid = r1.astype(BF16)
    lo = (r1 - mid.astype(F32)).astype(BF16)
    return hi, mid, lo


def _dot_exact_rhs(x, mat, dims=NN, pieces=3):
    hi, mid, lo = _split3(x)
    out = _dot(hi, mat, dims) + _dot(mid, mat, dims)
    return out + _dot(lo, mat, dims) if pieces == 3 else out


def _dot_exact_lhs(mat, x, dims=NN):
    hi, mid, lo = _split3(x)
    return _dot(mat, hi, dims) + _dot(mat, mid, dims) + _dot(mat, lo, dims)


def _params(*sem):
    return pltpu.CompilerParams(dimension_semantics=sem, vmem_limit_bytes=VMEM_LIMIT)


def _mm(name, ins, terms, n_acc, grid, acc_shape, outs, epilogue, extras=(), deps=()):
    nk = grid[2]
    ni, ne, nd, no = len(ins), len(extras), len(deps), len(outs)

    def body(*refs):
        in_refs = refs[:ni]
        ex_refs = refs[ni:ni + ne]
        out_refs = refs[ni + ne + nd:ni + ne + nd + no]
        acc_refs = refs[ni + ne + nd + no:]
        parts = [None] * n_acc
        for ai, li, ri, dims in terms:
            d = _dot(in_refs[li][...], in_refs[ri][...], dims)
            parts[ai] = d if parts[ai] is None else parts[ai] + d

        def finish(accs):
            res = epilogue(accs, [e[...] for e in ex_refs])
            for o, r in zip(out_refs, res):
                o[...] = r.astype(o.dtype)

        if nk == 1:
            finish(parts)
        else:
            k = pl.program_id(2)

            @pl.when(k == 0)
            def _():
                for a in acc_refs:
                    a[...] = jnp.zeros_like(a)

            sums = [a[...] + p for a, p in zip(acc_refs, parts)]
            for a, s in zip(acc_refs, sums):
                a[...] = s
            finish(sums)

    scratch = [] if nk == 1 else [pltpu.VMEM(acc_shape, F32) for _ in range(n_acc)]
    res = pl.pallas_call(
        body,
        name=name,
        grid=grid,
        in_specs=[s for _, s in ins] + [s for _, s in extras] + [pl.BlockSpec(memory_space=pl.ANY)] * nd,
        out_specs=[s for _, s in outs],
        out_shape=[o for o, _ in outs],
        scratch_shapes=scratch,
        compiler_params=_params("parallel", "parallel", "arbitrary"),
    )(*[a for a, _ in ins], *[a for a, _ in extras], *deps)
    return res


def _mm_rows(name, lhs, weights, dims, t, outs, epilogue, extras=(), deps=()):
    tm = _row_tile(t)
    nl, ne, nd, no = len(lhs), len(extras), len(deps), len(outs)
    ns = weights[0].shape[0]

    def body(*refs):
        lhs_refs = refs[:nl]
        w_hbm = refs[nl:2 * nl]
        ex_refs = refs[2 * nl:2 * nl + ne]
        out_refs = refs[2 * nl + ne + nd:2 * nl + ne + nd + no]
        w_vmem = refs[2 * nl + ne + nd + no:3 * nl + ne + nd + no]
        sem = refs[-1]

        @pl.when(pl.program_id(0) == 0)
        def _():
            copies = [pltpu.make_async_copy(w_hbm[p], w_vmem[p], sem.at[p]) for p in range(nl)]
            for cp in copies:
                cp.start()
            for cp in copies:
                cp.wait()

        acc = None
        for p in range(nl):
            pick = lhs[p][2]
            for j in range(ns):
                part = _dot(pick(lhs_refs[p], j), w_vmem[p][j], dims)
                acc = part if acc is None else acc + part
        res = epilogue([acc], [e[...] for e in ex_refs])
        for o, r in zip(out_refs, res):
            o[...] = r.astype(o.dtype)

    return pl.pallas_call(
        body,
        name=name,
        grid=(t // tm,),
        in_specs=[s for _, s, _ in lhs] + [pl.BlockSpec(memory_space=pl.ANY)] * nl + [s for _, s in extras]
        + [pl.BlockSpec(memory_space=pl.ANY)] * nd,
        out_specs=[s for _, s in outs],
        out_shape=[o for o, _ in outs],
        scratch_shapes=[pltpu.VMEM(w.shape, w.dtype) for w in weights] + [pltpu.SemaphoreType.DMA((nl,))],
        compiler_params=_params("arbitrary"),
    )(*[a for a, _, _ in lhs], *weights, *[a for a, _ in extras], *deps)


def _mm_shards(name, x, weights, dims, outs, epilogue, extras=(), deps=()):
    t = x.shape[0]
    tm = _row_tile(t)
    nw, ne, nd, no = len(weights), len(extras), len(deps), len(outs)
    ns = weights[0].shape[0]

    def body(*refs):
        x_ref = refs[0]
        w_hbm = refs[1:1 + nw]
        ex_refs = refs[1 + nw:1 + nw + ne]
        out_refs = refs[1 + nw + ne + nd:1 + nw + ne + nd + no]
        w_vmem = refs[1 + nw + ne + nd + no:1 + 2 * nw + ne + nd + no]
        sem = refs[-1]

        @pl.when(pl.program_id(0) == 0)
        def _():
            copies = [pltpu.make_async_copy(w_hbm[p], w_vmem[p], sem.at[p]) for p in range(nw)]
            for cp in copies:
                cp.start()
            for cp in copies:
                cp.wait()

        xv = x_ref[...]
        accs = [_dot(xv, w_vmem[p][0], dims) for p in range(nw)]
        for j in range(ns):
            nxt = [_dot(xv, w_vmem[p][j + 1], dims) for p in range(nw)] if j + 1 < ns else None
            res = epilogue(accs, [e[j] for e in ex_refs])
            for (_, _, store), o, r in zip(outs, out_refs, res):
                store(o, j, r.astype(o.dtype))
            accs = nxt

    return pl.pallas_call(
        body,
        name=name,
        grid=(t // tm,),
        in_specs=[pl.BlockSpec((tm, x.shape[1]), lambda i: (i, 0))] + [pl.BlockSpec(memory_space=pl.ANY)] * nw
        + [s for _, s in extras] + [pl.BlockSpec(memory_space=pl.ANY)] * nd,
        out_specs=[s for _, s, _ in outs],
        out_shape=[o for o, _, _ in outs],
        scratch_shapes=[pltpu.VMEM(w.shape, w.dtype) for w in weights] + [pltpu.SemaphoreType.DMA((nw,))],
        compiler_params=_params("arbitrary"),
    )(x, *weights, *[a for a, _ in extras], *deps)


def _col_chunks(f):
    return [(c, min(COL_CHUNK, f - c)) for c in range(0, f, COL_CHUNK)]


def _mm_cols(name, x, weights, n_out, epilogue, extras=(), deps=()):
    t, k = x.shape
    f = weights[0].shape[0]
    tm = _row_tile(t)
    chunks = _col_chunks(f)
    nw, ne, nd = len(weights), len(extras), len(deps)

    def body(*refs):
        x_ref = refs[0]
        w_hbm = refs[1:1 + nw]
        ex_refs = refs[1 + nw:1 + nw + ne]
        out_refs = refs[1 + nw + ne + nd:1 + nw + ne + nd + n_out]
        w_vmem = refs[1 + nw + ne + nd + n_out:1 + 2 * nw + ne + nd + n_out]
        sem = refs[-1]

        @pl.when(pl.program_id(0) == 0)
        def _():
            copies = [pltpu.make_async_copy(w_hbm[p], w_vmem[p], sem.at[p]) for p in range(nw)]
            for cp in copies:
                cp.start()
            for cp in copies:
                cp.wait()

        xv = x_ref[...]

        def dots(c):
            c0, cw = chunks[c]
            return [_dot(xv, w[c0:c0 + cw, :], NT) for w in w_vmem]

        accs = dots(0)
        for c, (c0, cw) in enumerate(chunks):
            nxt = dots(c + 1) if c + 1 < len(chunks) else None
            res = epilogue(accs, [e[:, c0:c0 + cw] for e in ex_refs])
            for o, r in zip(out_refs, res):
                o[:, c0:c0 + cw] = r.astype(o.dtype)
            accs = nxt

    act = pl.BlockSpec((tm, f), lambda i: (i, 0))
    return pl.pallas_call(
        body,
        name=name,
        grid=(t // tm,),
        in_specs=[pl.BlockSpec((tm, k), lambda i: (i, 0))] + [pl.BlockSpec(memory_space=pl.ANY)] * nw + [act] * ne
        + [pl.BlockSpec(memory_space=pl.ANY)] * nd,
        out_specs=[act] * n_out,
        out_shape=[jax.ShapeDtypeStruct((t, f), BF16)] * n_out,
        scratch_shapes=[pltpu.VMEM(w.shape, w.dtype) for w in weights] + [pltpu.SemaphoreType.DMA((nw,))],
        compiler_params=_params("arbitrary"),
    )(x, *weights, *extras, *deps)


def _row_tile(t):
    return 512 if t % 512 == 0 else t


def _k_tile(t):
    return t if t <= 4096 else 1024


def _grad_k_tile(t):
    return 1024 if t % 1024 == 0 else t


def _rmsnorm(xv, g):
    ms = jnp.mean(xv * xv, axis=-1, keepdims=True)
    return xv * lax.rsqrt(ms + RMS_EPS) * g


def _rmsnorm_fwd(name, x, g):
    t, d = x.shape
    tm = _row_tile(t)

    def body(x_ref, g_ref, h_ref):
        h_ref[...] = _rmsnorm(x_ref[...], g_ref[...]).astype(BF16)

    return pl.pallas_call(
        body,
        name=name,
        grid=(t // tm,),
        in_specs=[pl.BlockSpec((tm, d), lambda i: (i, 0)), pl.BlockSpec((1, d), lambda i: (0, 0))],
        out_specs=pl.BlockSpec((tm, d), lambda i: (i, 0)),
        out_shape=jax.ShapeDtypeStruct((t, d), BF16),
        compiler_params=_params("parallel"),
    )(x, g)


def _norm_bwd_epilogue(copy_scale):
    def epilogue(accs, ex):
        dh = accs[0]
        xv, g, dres = ex
        ms = jnp.mean(xv * xv, axis=-1, keepdims=True)
        rstd = lax.rsqrt(ms + RMS_EPS)
        xhat = xv * rstd
        dxhat = dh * g
        dx = rstd * (dxhat - xhat * jnp.mean(dxhat * xhat, axis=-1, keepdims=True))
        out = dres + dx
        dg = jnp.sum(dh * xhat, axis=0, keepdims=True)
        if copy_scale is None:
            return out, dg
        return out, out * copy_scale, dg

    return epilogue


def _merged(w):
    return w.reshape(1, -1, w.shape[-1])


def _ffn_up(name, h, wg, wu, deps=()):
    def epilogue(accs, ex):
        a, b = accs
        sg = _sigmoid(a)
        act = a * sg
        return act, b * (sg * (1.0 + a * (1.0 - sg))), act * b

    return _mm_cols(name, h, [_merged(wg)[0], _merged(wu)[0]], 3, epilogue, deps=deps)


def _whole_rows(arr, tm):
    return arr, pl.BlockSpec((tm, arr.shape[1]), lambda i: (i, 0)), lambda ref, j: ref[...]


def _ffn_down(name, z, wd, x, g_next, deps=()):
    t = z.shape[0]
    d = wd.shape[2]
    tm = _row_tile(t)
    row = pl.BlockSpec((tm, d), lambda i: (i, 0))

    def epilogue(accs, ex):
        y = ex[0] + 0.5 * accs[0]
        return y, _rmsnorm(y, ex[1])

    return _mm_rows(
        name, [_whole_rows(z, tm)], [_merged(wd)], NN, t,
        outs=[(jax.ShapeDtypeStruct((t, d), F32), row), (jax.ShapeDtypeStruct((t, d), BF16), row)],
        epilogue=epilogue,
        extras=[(x, row), (g_next, pl.BlockSpec((1, d), lambda i: (0, 0)))],
        deps=deps,
    )


def _ffn_down_loss(name, z, wd, x, target):
    t = z.shape[0]
    d = wd.shape[2]
    tm = _row_tile(t)
    nt = t // tm
    row = pl.BlockSpec((tm, d), lambda i: (i, 0))

    def epilogue(accs, ex):
        e = ex[0] + 0.5 * accs[0] - ex[1]
        dy = e * (1.0 / d)
        return dy, 0.5 * dy, jnp.sum(e * e, axis=0, keepdims=True)

    return _mm_rows(
        name, [_whole_rows(z, tm)], [_merged(wd)], NN, t,
        outs=[(jax.ShapeDtypeStruct((t, d), F32), row), (jax.ShapeDtypeStruct((t, d), BF16), row),
              (jax.ShapeDtypeStruct((nt, 1, d), F32), pl.BlockSpec((None, 1, d), lambda i: (i, 0, 0)))],
        epilogue=epilogue,
        extras=[(x, row), (target, row)],
    )


def _ffn_bwd_act(name, dout, wd, act_a, dact_b, deps=()):
    def epilogue(accs, ex):
        dz = accs[0]
        return dz * ex[1].astype(F32), dz * ex[0].astype(F32)

    return _mm_cols(name, dout, [_merged(wd)[0]], 2, epilogue, extras=[act_a, dact_b], deps=deps)


def _grad_w_cols(name, z, dout, deps=()):
    t, f = z.shape
    d = dout.shape[1]
    tk = _grad_k_tile(t)
    fh = f // 2
    dw = _mm(
        name,
        ins=[(z, pl.BlockSpec((tk, fh), lambda j, n, k: (k, j))),
             (dout, pl.BlockSpec((tk, d), lambda j, n, k: (k, 0)))],
        terms=[(0, 0, 1, TN)],
        n_acc=1,
        grid=(2, 1, t // tk),
        acc_shape=(fh, d),
        outs=[(pltpu.HBM((f, d), BF16), pl.BlockSpec((fh, d), lambda j, n, k: (j, 0)))],
        epilogue=lambda accs, ex: (accs[0],),
        deps=deps,
    )[0]
    return dw.reshape(N_CHIPS, f // N_CHIPS, d)


def _norm_bwd_outs(t, d, tm, copy_scale):
    row = pl.BlockSpec((tm, d), lambda i: (i, 0))
    outs = [(jax.ShapeDtypeStruct((t, d), F32), row)]
    if copy_scale is not None:
        outs.append((jax.ShapeDtypeStruct((t, d), BF16), row))
    outs.append((jax.ShapeDtypeStruct((t // tm, 1, d), F32), pl.BlockSpec((None, 1, d), lambda i: (i, 0, 0))))
    return row, outs


def _ffn_bwd_in(name, da, db, wg, wu, x, g, dres, copy_scale, deps=()):
    t = da.shape[0]
    d = wg.shape[2]
    tm = _row_tile(t)
    row, outs = _norm_bwd_outs(t, d, tm, copy_scale)
    return _mm_rows(
        name, [_whole_rows(da, tm), _whole_rows(db, tm)], [_merged(wg), _merged(wu)], NN, t,
        outs=outs,
        epilogue=_norm_bwd_epilogue(copy_scale),
        extras=[(x, row), (g, pl.BlockSpec((1, d), lambda i: (0, 0))), (dres, row)],
        deps=deps,
    )


def _in_proj(name, h, w_in):
    t, d = h.shape
    ns, _, pj = w_in.shape
    tm = _row_tile(t)
    def store(ref, j, value):
        ref[:, j * pj:(j + 1) * pj] = value

    out = (jax.ShapeDtypeStruct((t, ns * pj), F32), pl.BlockSpec((tm, ns * pj), lambda i: (i, 0)), store)
    return _mm_shards(name, h, [w_in], NN, [out], lambda accs, ex: (accs[0],))[0]


def _in_proj_bwd(name, dp, w_in, x, g, dres, copy_scale, deps=()):
    t = dp.shape[0]
    ns, d, pj = w_in.shape
    tm = _row_tile(t)
    row, outs = _norm_bwd_outs(t, d, tm, copy_scale)
    cols = (dp, pl.BlockSpec((tm, ns * pj), lambda i: (i, 0)), lambda ref, j: ref[:, j * pj:(j + 1) * pj])
    return _mm_rows(
        name, [cols], [w_in], NT, t,
        outs=outs,
        epilogue=_norm_bwd_epilogue(copy_scale),
        extras=[(x, row), (g, pl.BlockSpec((1, d), lambda i: (0, 0))), (dres, row)],
        deps=deps,
    )


def _grad_w_in(name, h, dp, ns):
    t, d = h.shape
    pj = dp.shape[1] // ns
    tk = _k_tile(t)
    return _mm(
        name,
        ins=[(h, pl.BlockSpec((tk, d), lambda j, n, k: (k, 0))),
             (dp, pl.BlockSpec((tk, pj), lambda j, n, k: (k, j)))],
        terms=[(0, 0, 1, TN)],
        n_acc=1,
        grid=(ns, 1, t // tk),
        acc_shape=(d, pj),
        outs=[(pltpu.HBM((ns, d, pj), BF16), pl.BlockSpec((None, d, pj), lambda j, n, k: (j, 0, 0)))],
        epilogue=lambda accs, ex: (accs[0],),
    )[0]


def _out_proj(name, mix, w_out, x, g_next):
    t, dm = mix.shape
    d = w_out.shape[1]
    tm = _row_tile(t)
    row = pl.BlockSpec((tm, d), lambda i, n, k: (i, 0))
    return _mm(
        name,
        ins=[(mix, pl.BlockSpec((tm, dm), lambda i, n, k: (i, 0))),
             (w_out, pl.BlockSpec((dm, d), lambda i, n, k: (0, 0)))],
        terms=[(0, 0, 1, NN)],
        n_acc=1,
        grid=(t // tm, 1, 1),
        acc_shape=(tm, d),
        outs=[(jax.ShapeDtypeStruct((t, d), F32), row), (jax.ShapeDtypeStruct((t, d), BF16), row)],
        epilogue=lambda accs, ex: (ex[0] + accs[0], _rmsnorm(ex[0] + accs[0], ex[1])),
        extras=[(x, row), (g_next, pl.BlockSpec((1, d), lambda i, n, k: (0, 0)))],
    )


def _out_proj_bwd(name, dx, w_out, deps=()):
    t, d = dx.shape
    dm = w_out.shape[0]
    tm = _row_tile(t)
    return _mm(
        name,
        ins=[(dx, pl.BlockSpec((tm, d), lambda i, n, k: (i, 0))),
             (w_out, pl.BlockSpec((dm, d), lambda i, n, k: (0, 0)))],
        terms=[(0, 0, 1, NT)],
        n_acc=1,
        grid=(t // tm, 1, 1),
        acc_shape=(tm, dm),
        outs=[(jax.ShapeDtypeStruct((t, dm), F32), pl.BlockSpec((tm, dm), lambda i, n, k: (i, 0)))],
        epilogue=lambda accs, ex: (accs[0],),
        deps=deps,
    )[0]


def _grad_w_out(name, mix, dx):
    t, dm = mix.shape
    d = dx.shape[1]
    tk = _k_tile(t)
    return _mm(
        name,
        ins=[(mix, pl.BlockSpec((tk, dm), lambda a, n, k: (k, 0))),
             (dx, pl.BlockSpec((tk, d), lambda a, n, k: (k, 0)))],
        terms=[(0, 0, 1, TN)],
        n_acc=1,
        grid=(1, 1, t // tk),
        acc_shape=(dm, d),
        outs=[(pltpu.HBM((dm, d), BF16), pl.BlockSpec((dm, d), lambda a, n, k: (0, 0)))],
        epilogue=lambda accs, ex: (accs[0],),
    )[0]


def _head_group_matrix():
    r = lax.broadcasted_iota(jnp.int32, (ATTN_W, ATTN_W), 0)
    c = lax.broadcasted_iota(jnp.int32, (ATTN_W, ATTN_W), 1)
    same = jnp.right_shift(r, 6) == jnp.right_shift(c, 6)
    return jnp.where(same, 1.0, 0.0).astype(BF16)


def _qk_prep(name, proj, gq, gk):
    b, s, _ = proj.shape
    tm = KPAD
    nb = s // tm

    def body(q_ref, k_ref, v_ref, gq_ref, gk_ref, qn_ref, kn_ref, vb_ref):
        j = pl.program_id(1)
        bd = _head_group_matrix()

        def norm(xv, g):
            ms = _dot_exact_rhs(xv * xv, bd, pieces=2) * (1.0 / ATTN_DH)
            return xv * lax.rsqrt(ms + RMS_EPS) * g

        @pl.when(j == 0)
        def _():
            kn_ref[...] = jnp.zeros_like(kn_ref)
            vb_ref[...] = jnp.zeros_like(vb_ref)

        @pl.when(j > 0)
        def _():
            qn_ref[...] = norm(q_ref[...], gq_ref[...]).astype(BF16)
            kn_ref[...] = norm(k_ref[...], gk_ref[...]).astype(BF16)
            vb_ref[...] = v_ref[...].astype(BF16)

    src_blk = lambda col: pl.BlockSpec((None, tm, ATTN_W), lambda bi, j: (bi, jnp.maximum(j - 1, 0), col))
    gspec = pl.BlockSpec((1, ATTN_W), lambda bi, j: (0, 0))
    padded = pl.BlockSpec((None, tm, ATTN_W), lambda bi, j: (bi, j, 0))
    return pl.pallas_call(
        body,
        name=name,
        grid=(b, nb + 1),
        in_specs=[src_blk(0), src_blk(1), src_blk(2), gspec, gspec],
        out_specs=[src_blk(0), padded, padded],
        out_shape=[jax.ShapeDtypeStruct((b, s, ATTN_W), BF16), jax.ShapeDtypeStruct((b, KPAD + s, ATTN_W), BF16),
                   jax.ShapeDtypeStruct((b, KPAD + s, ATTN_W), BF16)],
        compiler_params=_params("parallel", "arbitrary"),
    )(proj, proj, proj, gq, gk)


def _qk_prep_bwd(name, proj, dqn, dkn, dv, gq, gk):
    b, s, _ = proj.shape
    tm = KPAD
    nb = s // tm

    def body(q_ref, k_ref, dqn_ref, dkn_ref, dv_ref, gq_ref, gk_ref, dq_ref, dk_ref, dvb_ref, dgq_ref, dgk_ref):
        bd = _head_group_matrix()

        def bwd(xv, dy, g):
            ms = _dot_exact_rhs(xv * xv, bd, pieces=2) * (1.0 / ATTN_DH)
            rstd = lax.rsqrt(ms + RMS_EPS)
            xhat = xv * rstd
            dxhat = dy * g
            gm = _dot_exact_rhs(dxhat * xhat, bd, pieces=2) * (1.0 / ATTN_DH)
            return rstd * (dxhat - xhat * gm), jnp.sum(dy * xhat, axis=0, keepdims=True)

        dq, dgq = bwd(q_ref[...], dqn_ref[...], gq_ref[...])
        dk, dgk = bwd(k_ref[...], dkn_ref[...], gk_ref[...])
        dq_ref[...] = dq.astype(BF16)
        dk_ref[...] = dk.astype(BF16)
        dvb_ref[...] = dv_ref[...].astype(BF16)
        dgq_ref[...] = dgq
        dgk_ref[...] = dgk

    col = lambda c: pl.BlockSpec((None, tm, ATTN_W), lambda bi, j: (bi, j, c))
    past_pad = pl.BlockSpec((None, tm, ATTN_W), lambda bi, j: (bi, j + 1, 0))
    gspec = pl.BlockSpec((1, ATTN_W), lambda bi, j: (0, 0))
    pspec = pl.BlockSpec((None, 1, ATTN_W), lambda bi, j: (bi * nb + j, 0, 0))
    o_shape = jax.ShapeDtypeStruct((b, s, ATTN_W), BF16)
    p_shape = jax.ShapeDtypeStruct((b * nb, 1, ATTN_W), F32)
    return pl.pallas_call(
        body,
        name=name,
        grid=(b, nb),
        in_specs=[col(0), col(1), col(0), past_pad, past_pad, gspec, gspec],
        out_specs=[col(0)] * 3 + [pspec] * 2,
        out_shape=[o_shape] * 3 + [p_shape] * 2,
        compiler_params=_params("parallel", "parallel"),
    )(proj, proj, dqn, dkn, dv, gq, gk)


Q_CHUNKS = 4
QBLK = Q_CHUNKS * CHUNK
WIN = (LEFT_CHUNKS + Q_CHUNKS) * CHUNK
DB_W = BAND + CHUNK
MASKED = -1e30


def _band_table(bias):
    rows = [jnp.pad(bias, ((0, 0), (0, 0), (CHUNK * i, WIN - BAND - CHUNK * i)), constant_values=MASKED)
            for i in range(Q_CHUNKS)]
    return jnp.concatenate(rows, axis=1)


def _head_lanes(hh):
    lane = lax.broadcasted_iota(jnp.int32, (1, LANES), 1)
    return (lane < ATTN_DH) if hh == 0 else (lane >= ATTN_DH)


def _attn_probs(qh, kw, table, start):
    s = _dot(qh, kw, NT) * (ATTN_DH ** -0.5) + table
    col = lax.broadcasted_iota(jnp.int32, (QBLK, WIN), 1)
    s = jnp.where(col + start >= KPAD, s, MASKED)
    m = jnp.max(s, axis=-1, keepdims=True)
    p = jnp.exp(s - m)
    return p * (1.0 / jnp.sum(p, axis=-1, keepdims=True))


def _attn_fwd(name, q, k, v, table, deps=()):
    b, s, w = q.shape
    sp = k.shape[1]

    def body(q_ref, k_ref, v_ref, t_ref, *rest):
        o_ref = rest[-1]
        start = pl.multiple_of(pl.program_id(2) * QBLK, QBLK)
        kw = k_ref[pl.ds(start, WIN), :]
        vw = v_ref[pl.ds(start, WIN), :]
        q2 = q_ref[...]
        lanes = [_head_lanes(hh) for hh in range(2)]
        probs = [_attn_probs(jnp.where(mine, q2, jnp.zeros_like(q2)), kw, t_ref[hh], start).astype(BF16)
                 for hh, mine in enumerate(lanes)]
        outs = [_dot(p, vw) for p in probs]
        o_ref[...] = jnp.where(lanes[0], outs[0], outs[1]).astype(BF16)

    qspec = pl.BlockSpec((None, QBLK, LANES), lambda p, bi, i: (bi, i, p))
    kspec = pl.BlockSpec((None, sp, LANES), lambda p, bi, i: (bi, 0, p))
    return pl.pallas_call(
        body,
        name=name,
        grid=(w // LANES, b, s // QBLK),
        in_specs=[qspec, kspec, kspec, pl.BlockSpec((2, QBLK, WIN), lambda p, bi, i: (p, 0, 0))] + [ANY] * len(deps),
        out_specs=qspec,
        out_shape=jax.ShapeDtypeStruct((b, s, w), BF16),
        compiler_params=_params("parallel", "parallel", "arbitrary"),
    )(q, k, v, table, *deps)


def _attn_bwd(name, q, k, v, table, dmix):
    b, s, w = q.shape
    sp = k.shape[1]

    def body(q_ref, k_ref, v_ref, t_ref, do_ref, dq_ref, dk_ref, dv_ref, dbe_ref, dbo_ref):
        bi = pl.program_id(1)
        i = pl.program_id(2)
        start = pl.multiple_of(i * QBLK, QBLK)
        win = pl.ds(start, WIN)

        @pl.when(i == 0)
        def _():
            dk_ref[...] = jnp.zeros_like(dk_ref)
            dv_ref[...] = jnp.zeros_like(dv_ref)

        @pl.when(jnp.logical_and(i == 0, bi == 0))
        def _():
            dbe_ref[...] = jnp.zeros_like(dbe_ref)
            dbo_ref[...] = jnp.zeros_like(dbo_ref)

        kw = k_ref[win, :]
        vw = v_ref[win, :]
        q2 = q_ref[...]
        do2 = do_ref[...].astype(BF16)
        lanes = [_head_lanes(hh) for hh in range(2)]
        qh = [jnp.where(mine, q2, jnp.zeros_like(q2)) for mine in lanes]
        doh = [jnp.where(mine, do2, jnp.zeros_like(do2)) for mine in lanes]
        p = [_attn_probs(qh[hh], kw, t_ref[hh], start) for hh in range(2)]
        dp = [_dot(doh[hh], vw, NT) for hh in range(2)]
        ds = [p[hh] * (dp[hh] - jnp.sum(p[hh] * dp[hh], axis=-1, keepdims=True)) for hh in range(2)]
        dsb = [(x * (ATTN_DH ** -0.5)).astype(BF16) for x in ds]
        pb = [x.astype(BF16) for x in p]
        dq = [_dot(dsb[hh], kw) for hh in range(2)]
        dk = [_dot(dsb[hh], qh[hh], TN) for hh in range(2)]
        dv = [_dot(pb[hh], doh[hh], TN) for hh in range(2)]
        for hh in range(2):
            for qi in range(Q_CHUNKS):
                c0 = (qi // 2) * LANES
                blk = ds[hh][qi * CHUNK:(qi + 1) * CHUNK, c0:c0 + DB_W]
                if qi % 2 == 0:
                    dbe_ref[hh] += blk
                else:
                    dbo_ref[hh] += blk
        dq_ref[...] = jnp.where(lanes[0], dq[0], dq[1])
        dk_ref[win, :] += dk[0] + dk[1]
        dv_ref[win, :] += dv[0] + dv[1]

    qspec = pl.BlockSpec((None, QBLK, LANES), lambda p, bi, i: (bi, i, p))
    kspec = pl.BlockSpec((None, sp, LANES), lambda p, bi, i: (bi, 0, p))
    dbspec = pl.BlockSpec((2, CHUNK, DB_W), lambda p, bi, i: (p, 0, 0))
    db_shape = jax.ShapeDtypeStruct((ATTN_HEADS, CHUNK, DB_W), F32)
    return pl.pallas_call(
        body,
        name=name,
        grid=(w // LANES, b, s // QBLK),
        in_specs=[qspec, kspec, kspec, pl.BlockSpec((2, QBLK, WIN), lambda p, bi, i: (p, 0, 0)), qspec],
        out_specs=[qspec, kspec, kspec, dbspec, dbspec],
        out_shape=[jax.ShapeDtypeStruct((b, s, w), F32), jax.ShapeDtypeStruct((b, sp, w), F32),
                   jax.ShapeDtypeStruct((b, sp, w), F32), db_shape, db_shape],
        compiler_params=_params("arbitrary", "arbitrary", "arbitrary"),
    )(q, k, v, table, dmix)


HQ_COL = 3 * ATTN_W // HGRN_DH
HF_COL = HQ_COL + HGRN_HEADS
HI_COL = HF_COL + HGRN_HEADS
HG_COL = HI_COL + HGRN_HEADS
HGRN_ROWS = 8 * CHUNK
HEAD_LANES = [slice(hh * HGRN_DH, (hh + 1) * HGRN_DH) for hh in range(HGRN_HEADS)]


def _tri(lower):
    r = lax.broadcasted_iota(jnp.int32, (CHUNK, CHUNK), 0)
    c = lax.broadcasted_iota(jnp.int32, (CHUNK, CHUNK), 1)
    return (r >= c) if lower else (r <= c)


def _hgrn_chunk(hq, hf, lb, tril):
    sig = _sigmoid(hf)
    f = lb + (1.0 - lb) * sig
    g = jnp.log(f)
    ones_l = jnp.where(tril, 1.0, 0.0).astype(BF16)
    b = _dot_exact_lhs(ones_l, g)
    bl = jnp.sum(g, axis=0, keepdims=True)
    rows = lax.broadcasted_iota(jnp.int32, g.shape, 0)
    bm = jnp.sum(jnp.where(rows <= CHUNK // 2, g, 0.0), axis=0, keepdims=True)
    sq = _sigmoid(hq)
    q = hq * sq
    k = 1.0 - f
    return sig, f, b, bl, bm, sq, q, k


def _hgrn_fwd(name, proj, attn, lb, go, b, s):
    nc = s // CHUNK
    t = b * s
    nblk = s // HGRN_ROWS
    cpb = HGRN_ROWS // CHUNK

    def body(hq_ref, hf_ref, hi_ref, hg_ref, attn_ref, lb_ref, go_ref, mix_ref, oraw_ref, st_ref, s_scr):
        tril = _tri(True)
        gov = go_ref[...]
        mix_ref[:, 0:ATTN_W] = attn_ref[...]

        @pl.when(pl.program_id(1) == 0)
        def _():
            s_scr[...] = jnp.zeros_like(s_scr)

        def step(c, carry):
            sl = pl.ds(pl.multiple_of(c * CHUNK, CHUNK), CHUNK)
            hg = hg_ref[sl, :]
            _, _, bb, bl, bm, _, q, k = _hgrn_chunk(hq_ref[sl, :], hf_ref[sl, :], lb_ref[...], tril)
            vb = hi_ref[sl, :].astype(BF16)
            qe = (q * jnp.exp(bb - bm)).astype(BF16)
            ke = (k * jnp.exp(bm - bb)).astype(BF16)
            qb = (q * jnp.exp(bb)).astype(BF16)
            kb = (k * jnp.exp(bl - bb)).astype(BF16)
            e_last = jnp.exp(bl)
            gate = _silu(hg)
            st = [s_scr[hh] for hh in range(HGRN_HEADS)]
            a = [jnp.where(tril, _dot(qe[:, hs], ke[:, hs], NT), 0.0).astype(BF16) for hs in HEAD_LANES]
            o_state = [_dot(qb[:, hs], st[hh].astype(BF16), NT) for hh, hs in enumerate(HEAD_LANES)]
            st_next = [st[hh] * e_last[:, hs] + _dot(vb[:, hs], kb[:, hs], TN) for hh, hs in enumerate(HEAD_LANES)]
            o = [_dot(a[hh], vb[:, hs]) + o_state[hh] for hh, hs in enumerate(HEAD_LANES)]
            ro = [(oh * lax.rsqrt(jnp.mean(oh * oh, axis=-1, keepdims=True) + RMS_EPS) * gov) * gate[:, hs]
                  for oh, hs in zip(o, HEAD_LANES)]
            for hh in range(HGRN_HEADS):
                st_ref[hh, c] = st[hh]
                s_scr[hh] = st_next[hh]
            mix_ref[sl, ATTN_W:ATTN_W + HGRN_W] = jnp.concatenate(ro, axis=1).astype(BF16)
            oraw_ref[sl, :] = jnp.concatenate(o, axis=1)
            return carry

        lax.fori_loop(0, cpb, step, 0)

    col = lambda base: pl.BlockSpec((HGRN_ROWS, HGRN_W), lambda bi, i: (bi * nblk + i, base // HGRN_HEADS))
    out = pl.BlockSpec((HGRN_ROWS, HGRN_W), lambda bi, i: (bi * nblk + i, 0))
    return pl.pallas_call(
        body,
        name=name,
        grid=(b, nblk),
        in_specs=[col(HQ_COL), col(HF_COL), col(HI_COL), col(HG_COL), out,
                  pl.BlockSpec((1, HGRN_W), lambda bi, i: (0, 0)), pl.BlockSpec((1, HGRN_DH), lambda bi, i: (0, 0))],
        out_specs=[pl.BlockSpec((HGRN_ROWS, ATTN_W + HGRN_W), lambda bi, i: (bi * nblk + i, 0)), out,
                   pl.BlockSpec((None, HGRN_HEADS, cpb, HGRN_DH, HGRN_DH), lambda bi, i: (bi, 0, i, 0, 0))],
        out_shape=[jax.ShapeDtypeStruct((t, ATTN_W + HGRN_W), BF16), jax.ShapeDtypeStruct((t, HGRN_W), F32),
                   jax.ShapeDtypeStruct((b, HGRN_HEADS, nc, HGRN_DH, HGRN_DH), F32)],
        scratch_shapes=[pltpu.VMEM((HGRN_HEADS, HGRN_DH, HGRN_DH), F32)],
        compiler_params=_params("parallel", "arbitrary"),
    )(proj, proj, proj, proj, attn, lb, go)


def _hgrn_bwd(name, proj, dqkv, lb, go, oraw, states, dmix, b, s):
    t = b * s
    nblk = s // HGRN_ROWS
    cpb = HGRN_ROWS // CHUNK

    def body(hq_ref, hf_ref, hi_ref, hg_ref, dq_ref, dk_ref, dv_ref, lb_ref, go_ref, oraw_ref, st_ref, dro_ref,
             dp_ref, dlb_ref, dgo_ref, ds_scr, dlb_scr, dgo_scr):
        tril = _tri(True)
        ones_u = jnp.where(_tri(False), 1.0, 0.0).astype(BF16)
        gov = go_ref[...]
        dp_ref[:, 0:ATTN_W] = dq_ref[...]
        dp_ref[:, ATTN_W:2 * ATTN_W] = dk_ref[...]
        dp_ref[:, 2 * ATTN_W:3 * ATTN_W] = dv_ref[...]

        @pl.when(pl.program_id(1) == 0)
        def _():
            ds_scr[...] = jnp.zeros_like(ds_scr)
            dlb_scr[...] = jnp.zeros_like(dlb_scr)
            dgo_scr[...] = jnp.zeros_like(dgo_scr)

        def step(ci, carry):
            c = cpb - 1 - ci
            sl = pl.ds(pl.multiple_of(c * CHUNK, CHUNK), CHUNK)
            hq = hq_ref[sl, :]
            hg = hg_ref[sl, :]
            sig, f, bb, bl, bm, sq, q, k = _hgrn_chunk(hq, hf_ref[sl, :], lb_ref[...], tril)
            vb = hi_ref[sl, :].astype(BF16)
            ebm = jnp.exp(bb - bm)
            embm = jnp.exp(bm - bb)
            eb = jnp.exp(bb)
            ebl = jnp.exp(bl - bb)
            e_last = jnp.exp(bl)
            qe = (q * ebm).astype(BF16)
            ke = (k * embm).astype(BF16)
            qb = (q * eb).astype(BF16)
            kb = (k * ebl).astype(BF16)
            st = [st_ref[hh, c] for hh in range(HGRN_HEADS)]
            dst = [ds_scr[hh] for hh in range(HGRN_HEADS)]
            o = oraw_ref[sl, :]
            dro = dro_ref[sl, :]
            sg = _sigmoid(hg)
            gov4 = jnp.concatenate([gov] * HGRN_HEADS, axis=1)
            rstd = jnp.concatenate(
                [jnp.broadcast_to(lax.rsqrt(jnp.mean(o[:, hs] * o[:, hs], axis=-1, keepdims=True) + RMS_EPS),
                                  (CHUNK, HGRN_DH)) for hs in HEAD_LANES], axis=1)
            ohat = o * rstd
            dn = dro * (hg * sg)
            dhg = dro * (ohat * gov4) * (sg * (1.0 + hg * (1.0 - sg)))
            dgo_inc = jnp.sum(dn * ohat, axis=0, keepdims=True)
            dohat = dn * gov4
            proj_h = dohat * ohat
            pm = jnp.concatenate(
                [jnp.broadcast_to(jnp.mean(proj_h[:, hs], axis=-1, keepdims=True), (CHUNK, HGRN_DH))
                 for hs in HEAD_LANES], axis=1)
            dob = (rstd * (dohat - ohat * pm)).astype(BF16)
            stb = [x.astype(BF16) for x in st]
            dstb = [x.astype(BF16) for x in dst]
            a = [jnp.where(tril, _dot(qe[:, hs], ke[:, hs], NT), 0.0).astype(BF16) for hs in HEAD_LANES]
            dab = [jnp.where(tril, _dot(dob[:, hs], vb[:, hs], NT), 0.0).astype(BF16) for hs in HEAD_LANES]
            dqb = [_dot(dob[:, hs], stb[hh]) for hh, hs in enumerate(HEAD_LANES)]
            dkb = [_dot(vb[:, hs], dstb[hh]) for hh, hs in enumerate(HEAD_LANES)]
            dv_state = [_dot(kb[:, hs], dstb[hh], NT) for hh, hs in enumerate(HEAD_LANES)]
            dst_next = [dst[hh] * e_last[:, hs] + _dot(dob[:, hs], qb[:, hs], TN) for hh, hs in enumerate(HEAD_LANES)]
            dv = [_dot(a[hh], dob[:, hs], TN) + dv_state[hh] for hh, hs in enumerate(HEAD_LANES)]
            dqe = jnp.concatenate([_dot(dab[hh], ke[:, hs]) for hh, hs in enumerate(HEAD_LANES)], axis=1)
            dke = jnp.concatenate([_dot(dab[hh], qe[:, hs], TN) for hh, hs in enumerate(HEAD_LANES)], axis=1)
            dqb = jnp.concatenate(dqb, axis=1)
            dkb = jnp.concatenate(dkb, axis=1)
            state_term = jnp.concatenate(
                [jnp.sum(dst[hh] * st[hh], axis=0, keepdims=True) for hh in range(HGRN_HEADS)], axis=1)
            dq = dqe * ebm + dqb * eb
            dk = dke * embm + dkb * ebl
            db = (qe.astype(F32) * dqe - ke.astype(F32) * dke) + q * (dqb * eb) - k * (dkb * ebl)
            d_last = jnp.sum(k * ebl * dkb, axis=0, keepdims=True) + state_term * e_last
            dg = _dot_exact_lhs(ones_u, db) + d_last
            df = dg / f - dk
            first = HQ_COL * HGRN_DH
            dp_ref[sl, first:first + HGRN_W] = (dq * (sq * (1.0 + hq * (1.0 - sq)))).astype(BF16)
            dp_ref[sl, first + HGRN_W:first + 2 * HGRN_W] = (df * (1.0 - lb_ref[...]) * sig * (1.0 - sig)).astype(BF16)
            dp_ref[sl, first + 2 * HGRN_W:first + 3 * HGRN_W] = jnp.concatenate(dv, axis=1).astype(BF16)
            dp_ref[sl, first + 3 * HGRN_W:first + 4 * HGRN_W] = dhg.astype(BF16)
            dlb_scr[...] += jnp.sum(df * (1.0 - sig), axis=0, keepdims=True)
            dgo_scr[...] += dgo_inc
            for hh in range(HGRN_HEADS):
                ds_scr[hh] = dst_next[hh]
            return carry

        lax.fori_loop(0, cpb, step, 0)

        @pl.when(pl.program_id(1) == nblk - 1)
        def _():
            dlb_ref[...] = dlb_scr[...]
            dgo_ref[...] = dgo_scr[...]

    rows = lambda bi, i: bi * nblk + (nblk - 1 - i)
    col = lambda base: pl.BlockSpec((HGRN_ROWS, HGRN_W), lambda bi, i: (rows(bi, i), base // HGRN_HEADS))
    out = pl.BlockSpec((HGRN_ROWS, HGRN_W), lambda bi, i: (rows(bi, i), 0))
    part = pl.BlockSpec((None, 1, HGRN_W), lambda bi, i: (bi, 0, 0))
    width = HG_COL * HGRN_DH + HGRN_W
    o_shape = jax.ShapeDtypeStruct((t, width), BF16)
    p_shape = jax.ShapeDtypeStruct((b, 1, HGRN_W), F32)
    return pl.pallas_call(
        body,
        name=name,
        grid=(b, nblk),
        in_specs=[col(HQ_COL), col(HF_COL), col(HI_COL), col(HG_COL), out, out, out,
                  pl.BlockSpec((1, HGRN_W), lambda bi, i: (0, 0)), pl.BlockSpec((1, HGRN_DH), lambda bi, i: (0, 0)), out,
                  pl.BlockSpec((None, HGRN_HEADS, cpb, HGRN_DH, HGRN_DH), lambda bi, i: (bi, 0, nblk - 1 - i, 0, 0)),
                  col(ATTN_W // HGRN_DH)],
        out_specs=[pl.BlockSpec((HGRN_ROWS, width), lambda bi, i: (rows(bi, i), 0))] + [part] * 2,
        out_shape=[o_shape] + [p_shape] * 2,
        scratch_shapes=[pltpu.VMEM((HGRN_HEADS, HGRN_DH, HGRN_DH), F32), pltpu.VMEM((1, HGRN_W), F32),
                        pltpu.VMEM((1, HGRN_W), F32)],
        compiler_params=_params("parallel", "arbitrary"),
    )(proj, proj, proj, proj, *dqkv, lb, go, oraw, states, dmix)


def _small_grads(name, dg1, dgm, dg2, dgq, dgk, dbias_t, dlb, dgo, lbp):
    d = dg1.shape[1]

    def body(dg1_ref, dgm_ref, dg2_ref, dgq_ref, dgk_ref, dbias_ref, dlb_ref, dgo_ref, lbp_ref,
             g1_ref, gm_ref, g2_ref, gq_ref, gk_ref, rb_ref, lbg_ref, go_ref):
        g1_ref[...] = jnp.sum(dg1_ref[...], axis=0, keepdims=True)
        gm_ref[...] = jnp.sum(dgm_ref[...], axis=0, keepdims=True)
        g2_ref[...] = jnp.sum(dg2_ref[...], axis=0, keepdims=True)
        r = lax.broadcasted_iota(jnp.int32, (ATTN_W, ATTN_DH), 0)
        cidx = lax.broadcasted_iota(jnp.int32, (ATTN_W, ATTN_DH), 1)
        fold = jnp.where(jnp.bitwise_and(r, ATTN_DH - 1) == cidx, 1.0, 0.0).astype(BF16)
        gq_ref[...] = jnp.sum(_dot_exact_rhs(dgq_ref[...], fold), axis=0, keepdims=True)
        gk_ref[...] = jnp.sum(_dot_exact_rhs(dgk_ref[...], fold), axis=0, keepdims=True)
        gosum = jnp.sum(dgo_ref[...], axis=0, keepdims=True)
        go_ref[...] = (gosum[:, 0:HGRN_DH] + gosum[:, HGRN_DH:2 * HGRN_DH]
                       + gosum[:, 2 * HGRN_DH:3 * HGRN_DH] + gosum[:, 3 * HGRN_DH:4 * HGRN_DH])
        p0 = lbp_ref[0:1, :]
        p1 = lbp_ref[1:2, :]
        lbv = 1.0 / (1.0 + jnp.exp(p1 - p0))
        dp0 = jnp.sum(dlb_ref[...], axis=0, keepdims=True) * lbv * (1.0 - lbv)
        lbg_ref[0:1, :] = dp0
        lbg_ref[1:2, :] = -dp0
        sidx = lax.broadcasted_iota(jnp.int32, (BAND, N_REL_PAD), 0)
        ridx = lax.broadcasted_iota(jnp.int32, (BAND, N_REL_PAD), 1)

        def step(tq, acc):
            rel = jnp.clip(tq + KPAD - sidx, -REL_CLIP, REL_CLIP) + REL_CLIP
            onehot = jnp.where(rel == ridx, 1.0, 0.0).astype(BF16)
            return acc + _dot_exact_rhs(dbias_ref[tq], onehot)

        rb_ref[...] = lax.fori_loop(0, CHUNK, step, jnp.zeros((ATTN_HEADS, N_REL_PAD), F32))

    ins = [dg1, dgm, dg2, dgq, dgk, dbias_t, dlb, dgo, lbp]
    outs = [jax.ShapeDtypeStruct((1, d), F32)] * 3 + [jax.ShapeDtypeStruct((1, ATTN_DH), F32)] * 2 + [
        jax.ShapeDtypeStruct((ATTN_HEADS, N_REL_PAD), F32), jax.ShapeDtypeStruct((2, HGRN_W), F32),
        jax.ShapeDtypeStruct((1, HGRN_DH), F32)]
    vm = pl.BlockSpec(memory_space=pltpu.VMEM)
    return pl.pallas_call(
        body,
        name=name,
        in_specs=[vm] * len(ins),
        out_specs=[vm] * len(outs),
        out_shape=outs,
        compiler_params=pltpu.CompilerParams(vmem_limit_bytes=VMEM_LIMIT),
    )(*ins)


def _adam_update(w, g, m, v):
    m2 = ADAM_B1 * m + (1.0 - ADAM_B1) * g
    v2 = ADAM_B2 * v + (1.0 - ADAM_B2) * (g * g)
    m_hat = m2 / (1.0 - ADAM_B1 ** ADAM_STEP)
    v_hat = v2 / (1.0 - ADAM_B2 ** ADAM_STEP)
    delta = -ADAM_LR * (m_hat / (jnp.sqrt(v_hat) + ADAM_EPS) + ADAM_WD * w)
    return delta, m2, v2


def _rows_tile(r):
    return r if r <= 512 or r % 512 else 512


def _pair_sum(name, grad, theirs, core):
    n, half, c = theirs.shape
    tr = _rows_tile(half)
    nth = half // tr

    def body(core_ref, a_ref, b_ref, o_ref):
        o_ref[...] = (a_ref[...].astype(F32) + b_ref[...].astype(F32)).astype(o_ref.dtype)

    spec = pl.BlockSpec((None, tr, c), lambda i, j, core_ref: (i, j, 0))
    return pl.pallas_call(
        body, name=name,
        grid_spec=pltpu.PrefetchScalarGridSpec(
            num_scalar_prefetch=1, grid=(n, nth),
            in_specs=[pl.BlockSpec((None, tr, c), lambda i, j, core_ref: (i, core_ref[0] * nth + j, 0)), spec],
            out_specs=spec),
        out_shape=pltpu.HBM((n, half, c), BF16), compiler_params=_params("parallel", "parallel"),
    )(core, grad, theirs)


def _chip_sum(name, own, parts, chip):
    _, half, c = own.shape
    tr = _rows_tile(half)

    def body(chip_ref, own_ref, p_ref, o_ref):
        me = chip_ref[0]
        mine = own_ref[...].astype(F32)
        flip_x, flip_y, flip_xy = (p_ref[i].astype(F32) for i in range(3))
        acc = None
        for k in range(N_CHIPS):
            rel = jnp.bitwise_xor(me, k)
            term = jnp.where(rel == 0, mine, jnp.where(rel == 2, flip_x, jnp.where(rel == 1, flip_y, flip_xy)))
            acc = term if acc is None else acc + term
        o_ref[...] = acc

    return pl.pallas_call(
        body, name=name,
        grid_spec=pltpu.PrefetchScalarGridSpec(
            num_scalar_prefetch=1, grid=(half // tr,),
            in_specs=[pl.BlockSpec((None, tr, c), lambda j, chip_ref: (chip_ref[0], j, 0)),
                      pl.BlockSpec((3, tr, c), lambda j, chip_ref: (0, j, 0))],
            out_specs=pl.BlockSpec((tr, c), lambda j, chip_ref: (j, 0))),
        out_shape=pltpu.HBM((half, c), F32), compiler_params=_params("parallel"),
    )(chip, own, parts)


def _adamw(name, w, g_mine, g_theirs, m, v, core):
    _, r, c = w.shape
    half = r // 2
    tr = _rows_tile(half)
    nth = half // tr

    def body(core_ref, w_ref, gm_ref, gt_ref, m_ref, v_ref, g_ref, d_ref, m2_ref, v2_ref):
        g = jnp.where(pl.program_id(0) == core_ref[0], gm_ref[...], gt_ref[...])
        delta, m2, v2 = _adam_update(w_ref[...], g, m_ref[...], v_ref[...])
        g_ref[...] = g
        d_ref[...] = delta
        m2_ref[...] = m2
        v2_ref[...] = v2

    full = pl.BlockSpec((None, tr, c), lambda h, j, core_ref: (0, h * nth + j, 0))
    part = pl.BlockSpec((tr, c), lambda h, j, core_ref: (j, 0))
    shape = jax.ShapeDtypeStruct((1, r, c), F32)
    return pl.pallas_call(
        body, name=name,
        grid_spec=pltpu.PrefetchScalarGridSpec(
            num_scalar_prefetch=1, grid=(2, nth), in_specs=[full, part, part, full, full], out_specs=[full] * 4),
        out_shape=[shape] * 4, compiler_params=_params("parallel", "parallel"),
    )(core, w, g_mine, g_theirs, m, v)


def _rel_bias_table(name, rel_bias):
    padded = jnp.pad(rel_bias, ((0, 0), (0, N_REL_PAD - N_REL)))

    def body(rb_ref, o_ref):
        ridx = lax.broadcasted_iota(jnp.int32, (N_REL_PAD, BAND), 0)
        sidx = lax.broadcasted_iota(jnp.int32, (N_REL_PAD, BAND), 1)
        rb = rb_ref[...]

        def step(tq, carry):
            rel = jnp.clip(tq + KPAD - sidx, -REL_CLIP, REL_CLIP) + REL_CLIP
            onehot = jnp.where(rel == ridx, 1.0, 0.0).astype(BF16)
            o_ref[tq] = _dot_exact_rhs(rb, onehot)
            return carry

        lax.fori_loop(0, CHUNK, step, 0)

    vm = pl.BlockSpec(memory_space=pltpu.VMEM)
    table = pl.pallas_call(
        body, name=name, in_specs=[vm], out_specs=vm,
        out_shape=jax.ShapeDtypeStruct((CHUNK, ATTN_HEADS, BAND), F32),
    )(padded)
    return table.transpose(1, 0, 2)


def _adamw_small(name, w, parts, m, v):
    def body(w_ref, p_ref, m_ref, v_ref, g_ref, d_ref, m2_ref, v2_ref):
        g = p_ref[0]
        for i in range(1, N_DEV):
            g = g + p_ref[i]
        delta, m2, v2 = _adam_update(w_ref[...], g, m_ref[...], v_ref[...])
        g_ref[...] = g
        d_ref[...] = delta
        m2_ref[...] = m2
        v2_ref[...] = v2

    vm = pl.BlockSpec(memory_space=pltpu.VMEM)
    shape = jax.ShapeDtypeStruct((SMALL_ROWS, SMALL_COLS), F32)
    return pl.pallas_call(
        body, name=name, in_specs=[vm] * 4, out_specs=[vm] * 4, out_shape=[shape] * 4,
    )(w, parts, m, v)


def _position():
    return lax.axis_index("x"), lax.axis_index("y"), lax.axis_index("c")


def _other_chips(x, y):
    return [(1 - x, y), (x, 1 - y), (1 - x, 1 - y)]


ANY = pl.BlockSpec(memory_space=pl.ANY)
PAIR_ID = 0


def _pair_handshake():
    x, y, c = _position()
    barrier = pltpu.get_barrier_semaphore()
    pl.semaphore_signal(barrier, inc=1, device_id=(x, y, 1 - c), device_id_type=MESH)
    pl.semaphore_wait(barrier, 1)


PAIR_CALL = pltpu.CompilerParams(collective_id=PAIR_ID)


HBM = pl.BlockSpec(memory_space=pltpu.HBM)
SEM = pl.BlockSpec(memory_space=pltpu.SEMAPHORE)
SPLIT_COPY = pltpu.SideEffectType.DATAFLOW_SIDE_EFFECTING


def _gather_copy(shards, outs, send_sem, recv_sem, i, j):
    x, y, c = _position()
    chips = _other_chips(x, y)
    half = shards[i].shape[0] // 2
    rows = pl.ds(pl.multiple_of(c * half, 16), half)
    return pltpu.make_async_remote_copy(
        src_ref=shards[i].at[rows, :], dst_ref=outs[i].at[2 * x + y, rows, :],
        send_sem=send_sem.at[3 * i + j], recv_sem=recv_sem.at[3 * i + j],
        device_id=(chips[j][0], chips[j][1], c), device_id_type=MESH)


def _gather_start(name, shards, after):
    n = len(shards)

    def body(*refs):
        srcs, outs = refs[:n], refs[n:2 * n]
        send_sem, recv_sem = refs[2 * n + len(after)], refs[2 * n + len(after) + 1]
        token = refs[-1]
        for i in range(n):
            for j in range(3):
                _gather_copy(srcs, outs, send_sem, recv_sem, i, j).start()
        token[...] = jnp.zeros_like(token)

    full = [(N_CHIPS,) + s.shape for s in shards]
    res = pl.pallas_call(
        body,
        name=name,
        in_specs=[HBM] * (2 * n) + [ANY] * len(after),
        out_specs=[SEM, SEM] + [HBM] * (2 * n) + [pl.BlockSpec(memory_space=pltpu.VMEM)],
        out_shape=[pltpu.SemaphoreType.DMA((3 * n,)), pltpu.SemaphoreType.DMA((3 * n,))]
        + [pltpu.HBM(s.shape, s.dtype) for s in shards]
        + [pltpu.HBM(shp, s.dtype) for shp, s in zip(full, shards)]
        + [jax.ShapeDtypeStruct((8, LANES), F32)],
        input_output_aliases={i: 2 + i for i in range(2 * n)},
        compiler_params=pltpu.CompilerParams(has_side_effects=SPLIT_COPY),
    )(*[pltpu.with_memory_space_constraint(s, pltpu.HBM) for s in shards],
      *[pltpu.with_memory_space_constraint(lax.empty(shp, s.dtype), pltpu.HBM) for shp, s in zip(full, shards)],
      *after)
    return res[0], res[1], list(res[2:2 + n]), list(res[2 + n:2 + 2 * n]), res[-1]


def _gather_wait(name, send_sem, recv_sem, shards, outs, after):
    n = len(shards)

    def body(*refs):
        srcs, out_refs = refs[:n], refs[n:2 * n]
        send_ref, recv_ref = refs[2 * n], refs[2 * n + 1]
        for i in range(n):
            for j in range(3):
                copy = _gather_copy(srcs, out_refs, send_ref, recv_ref, i, j)
                copy.wait_send()
                copy.wait_recv()

    res = pl.pallas_call(
        body,
        name=name,
        in_specs=[HBM] * (2 * n) + [SEM, SEM] + [ANY] * len(after),
        out_specs=[HBM] * (2 * n),
        out_shape=[pltpu.HBM(s.shape, s.dtype) for s in shards] + [pltpu.HBM(o.shape, o.dtype) for o in outs],
        input_output_aliases={i: i for i in range(2 * n)},
        compiler_params=pltpu.CompilerParams(has_side_effects=SPLIT_COPY),
    )(*shards, *outs, send_sem, recv_sem, *after)
    return list(res[:n]), list(res[n:])


def _join_copies(srcs, ins, outs, own_send, own_recv, half_send, half_recv):
    x, y, c = _position()
    chips = _other_chips(x, y)
    copies = []
    for i in range(len(srcs)):
        copies.append(pltpu.make_async_remote_copy(
            src_ref=srcs[i], dst_ref=outs[i].at[2 * x + y], send_sem=own_send.at[i], recv_sem=own_recv.at[i],
            device_id=(x, y, 1 - c), device_id_type=MESH))
        half = srcs[i].shape[0] // 2
        rows = pl.ds(pl.multiple_of(c * half, 16), half)
        for j in range(3):
            slot = 2 * chips[j][0] + chips[j][1]
            copies.append(pltpu.make_async_remote_copy(
                src_ref=ins[i].at[slot, rows, :], dst_ref=outs[i].at[slot, rows, :],
                send_sem=half_send.at[3 * i + j], recv_sem=half_recv.at[3 * i + j],
                device_id=(x, y, 1 - c), device_id_type=MESH))
    return copies


def _gather_join(name, shards, outs):
    n = len(shards)

    def body(*refs):
        _pair_handshake()
        copies = _join_copies(refs[:n], refs[n:2 * n], refs[2 * n:3 * n], *refs[3 * n:])
        for cp in copies:
            cp.start()
        for cp in copies:
            cp.wait()

    return pl.pallas_call(
        body,
        name=name,
        in_specs=[ANY] * (2 * n),
        out_specs=[HBM] * n,
        out_shape=[pltpu.HBM(o.shape, o.dtype) for o in outs],
        input_output_aliases={n + i: i for i in range(n)},
        scratch_shapes=[pltpu.SemaphoreType.DMA((n,))] * 2 + [pltpu.SemaphoreType.DMA((3 * n,))] * 2,
        compiler_params=PAIR_CALL,
    )(*shards, *outs)


def _join_start(name, shards, outs):
    n = len(shards)

    def body(*refs):
        _pair_handshake()
        srcs, arrs = refs[:n], refs[n:2 * n]
        sems = refs[2 * n:2 * n + 4]
        token = refs[-1]
        for cp in _join_copies(srcs, arrs, arrs, *sems):
            cp.start()
        token[...] = jnp.zeros_like(token)

    res = pl.pallas_call(
        body,
        name=name,
        in_specs=[HBM] * (2 * n),
        out_specs=[SEM] * 4 + [HBM] * (2 * n) + [pl.BlockSpec(memory_space=pltpu.VMEM)],
        out_shape=[pltpu.SemaphoreType.DMA((n,))] * 2 + [pltpu.SemaphoreType.DMA((3 * n,))] * 2
        + [pltpu.HBM(s.shape, s.dtype) for s in shards] + [pltpu.HBM(o.shape, o.dtype) for o in outs]
        + [jax.ShapeDtypeStruct((8, LANES), F32)],
        input_output_aliases={i: 4 + i for i in range(2 * n)},
        compiler_params=pltpu.CompilerParams(has_side_effects=SPLIT_COPY, collective_id=PAIR_ID),
    )(*shards, *outs)
    return list(res[:4]), list(res[4:4 + n]), list(res[4 + n:4 + 2 * n]), res[-1]


def _join_wait(name, sems, shards, outs, after):
    n = len(shards)

    def body(*refs):
        srcs, arrs = refs[:n], refs[n:2 * n]
        for cp in _join_copies(srcs, arrs, arrs, *refs[2 * n:2 * n + 4]):
            cp.wait_send()
            cp.wait_recv()

    res = pl.pallas_call(
        body,
        name=name,
        in_specs=[HBM] * (2 * n) + [SEM] * 4 + [ANY] * len(after),
        out_specs=[HBM] * (2 * n),
        out_shape=[pltpu.HBM(s.shape, s.dtype) for s in shards] + [pltpu.HBM(o.shape, o.dtype) for o in outs],
        input_output_aliases={i: i for i in range(2 * n)},
        compiler_params=pltpu.CompilerParams(has_side_effects=SPLIT_COPY),
    )(*shards, *outs, *sems, *after)
    return list(res[n:])


def _pair_copy(grads, lands, send_sem, recv_sem, i):
    x, y, c = _position()
    half = grads[i].shape[1] // 2
    give = pl.ds(pl.multiple_of((1 - c) * half, 16), half)
    return pltpu.make_async_remote_copy(
        src_ref=grads[i].at[:, give, :], dst_ref=lands[i], send_sem=send_sem.at[i], recv_sem=recv_sem.at[i],
        device_id=(x, y, 1 - c), device_id_type=MESH)


def _pair_start(name, grads):
    n = len(grads)

    def body(*refs):
        _pair_handshake()
        srcs, lands = refs[:n], refs[n:2 * n]
        send_sem, recv_sem = refs[2 * n], refs[2 * n + 1]
        token = refs[-1]
        for i in range(n):
            _pair_copy(srcs, lands, send_sem, recv_sem, i).start()
        token[...] = jnp.zeros_like(token)

    halves = [(g.shape[0], g.shape[1] // 2, g.shape[2]) for g in grads]
    res = pl.pallas_call(
        body,
        name=name,
        in_specs=[HBM] * (2 * n),
        out_specs=[SEM, SEM] + [HBM] * (2 * n) + [pl.BlockSpec(memory_space=pltpu.VMEM)],
        out_shape=[pltpu.SemaphoreType.DMA((n,)), pltpu.SemaphoreType.DMA((n,))]
        + [pltpu.HBM(g.shape, g.dtype) for g in grads]
        + [pltpu.HBM(shp, g.dtype) for shp, g in zip(halves, grads)]
        + [jax.ShapeDtypeStruct((8, LANES), F32)],
        input_output_aliases={i: 2 + i for i in range(2 * n)},
        compiler_params=pltpu.CompilerParams(has_side_effects=SPLIT_COPY, collective_id=PAIR_ID),
    )(*[pltpu.with_memory_space_constraint(g, pltpu.HBM) for g in grads],
      *[pltpu.with_memory_space_constraint(lax.empty(shp, g.dtype), pltpu.HBM) for shp, g in zip(halves, grads)])
    return res[0], res[1], list(res[2:2 + n]), list(res[2 + n:2 + 2 * n]), res[-1]


def _pair_wait(name, send_sem, recv_sem, grads, lands, after):
    n = len(grads)

    def body(*refs):
        srcs, land_refs = refs[:n], refs[n:2 * n]
        send_ref, recv_ref = refs[2 * n], refs[2 * n + 1]
        for i in range(n):
            copy = _pair_copy(srcs, land_refs, send_ref, recv_ref, i)
            copy.wait_send()
            copy.wait_recv()

    res = pl.pallas_call(
        body,
        name=name,
        in_specs=[HBM] * (2 * n) + [SEM, SEM, ANY],
        out_specs=[HBM] * (2 * n),
        out_shape=[pltpu.HBM(g.shape, g.dtype) for g in grads] + [pltpu.HBM(l.shape, l.dtype) for l in lands],
        input_output_aliases={i: i for i in range(2 * n)},
        compiler_params=pltpu.CompilerParams(has_side_effects=SPLIT_COPY),
    )(*grads, *lands, send_sem, recv_sem, after)
    return list(res[:n]), list(res[n:])


def _scatter_copy(srcs, lands, send_sem, recv_sem, i, j):
    x, y, c = _position()
    chips = _other_chips(x, y)
    return pltpu.make_async_remote_copy(
        src_ref=srcs[i].at[2 * chips[j][0] + chips[j][1]], dst_ref=lands[i].at[j],
        send_sem=send_sem.at[3 * i + j], recv_sem=recv_sem.at[3 * i + j],
        device_id=(chips[j][0], chips[j][1], c), device_id_type=MESH)


def _scatter_start(name, sums):
    n = len(sums)

    def body(*refs):
        srcs, lands = refs[:n], refs[n:2 * n]
        send_sem, recv_sem = refs[2 * n], refs[2 * n + 1]
        token = refs[-1]
        for i in range(n):
            for j in range(3):
                _scatter_copy(srcs, lands, send_sem, recv_sem, i, j).start()
        token[...] = jnp.zeros_like(token)

    land_shapes = [(3,) + s.shape[1:] for s in sums]
    res = pl.pallas_call(
        body,
        name=name,
        in_specs=[HBM] * (2 * n),
        out_specs=[SEM, SEM] + [HBM] * (2 * n) + [pl.BlockSpec(memory_space=pltpu.VMEM)],
        out_shape=[pltpu.SemaphoreType.DMA((3 * n,)), pltpu.SemaphoreType.DMA((3 * n,))]
        + [pltpu.HBM(s.shape, s.dtype) for s in sums]
        + [pltpu.HBM(shp, s.dtype) for shp, s in zip(land_shapes, sums)]
        + [jax.ShapeDtypeStruct((8, LANES), F32)],
        input_output_aliases={i: 2 + i for i in range(2 * n)},
        compiler_params=pltpu.CompilerParams(has_side_effects=SPLIT_COPY),
    )(*[pltpu.with_memory_space_constraint(s, pltpu.HBM) for s in sums],
      *[pltpu.with_memory_space_constraint(lax.empty(shp, s.dtype), pltpu.HBM) for shp, s in zip(land_shapes, sums)])
    return res[0], res[1], list(res[2:2 + n]), list(res[2 + n:2 + 2 * n]), res[-1]


def _scatter_wait(name, send_sem, recv_sem, sums, lands, after):
    n = len(sums)

    def body(*refs):
        srcs, land_refs = refs[:n], refs[n:2 * n]
        send_ref, recv_ref = refs[2 * n], refs[2 * n + 1]
        for i in range(n):
            for j in range(3):
                copy = _scatter_copy(srcs, land_refs, send_ref, recv_ref, i, j)
                copy.wait_send()
                copy.wait_recv()

    res = pl.pallas_call(
        body,
        name=name,
        in_specs=[HBM] * (2 * n) + [SEM, SEM, ANY],
        out_specs=[HBM] * (2 * n),
        out_shape=[pltpu.HBM(s.shape, s.dtype) for s in sums] + [pltpu.HBM(l.shape, l.dtype) for l in lands],
        input_output_aliases={i: i for i in range(2 * n)},
        compiler_params=pltpu.CompilerParams(has_side_effects=SPLIT_COPY),
    )(*sums, *lands, send_sem, recv_sem, after)
    return list(res[:n]), list(res[n:])


def _pair_join(name, halves, small=None):
    n = len(halves)
    if small is None:
        def body_plain(*refs):
            _pair_handshake()
            ins, outs = refs[:n], refs[n:2 * n]
            send_sem, recv_sem = refs[2 * n:]
            x, y, c = _position()
            swaps = [pltpu.make_async_remote_copy(
                src_ref=ins[i], dst_ref=outs[i], send_sem=send_sem.at[i], recv_sem=recv_sem.at[i],
                device_id=(x, y, 1 - c), device_id_type=MESH) for i in range(n)]
            for swap in swaps:
                swap.start()
            for swap in swaps:
                swap.wait()

        return pl.pallas_call(
            body_plain,
            name=name,
            in_specs=[ANY] * n,
            out_specs=[ANY] * n,
            out_shape=[jax.ShapeDtypeStruct(h.shape, h.dtype) for h in halves],
            scratch_shapes=[pltpu.SemaphoreType.DMA((n,))] * 2,
            compiler_params=PAIR_CALL,
        )(*halves)

    def body(*refs):
        ins, small_ref = refs[:n], refs[n]
        outs, all_ref = refs[n + 1:2 * n + 1], refs[2 * n + 1]
        send_sem, recv_sem, sm_send, sm_recv, sm_local = refs[2 * n + 2:]
        x, y, c = _position()
        swaps = []
        for i in range(n):
            swap = pltpu.make_async_remote_copy(
                src_ref=ins[i], dst_ref=outs[i], send_sem=send_sem.at[i], recv_sem=recv_sem.at[i],
                device_id=(x, y, 1 - c), device_id_type=MESH)
            swap.start()
            swaps.append(swap)
        me = 4 * x + 2 * y + c
        sm_own = pltpu.make_async_copy(small_ref, all_ref.at[me], sm_local)
        sm_own.start()
        pushes, arrivals = [], []
        for mask in range(1, N_DEV):
            px, py, pc = x ^ (mask >> 2), y ^ ((mask >> 1) & 1), c ^ (mask & 1)
            pushes.append(pltpu.make_async_remote_copy(
                src_ref=small_ref, dst_ref=all_ref.at[me], send_sem=sm_send.at[mask - 1], recv_sem=sm_recv.at[mask - 1],
                device_id=(px, py, pc), device_id_type=MESH))
            arrivals.append(pltpu.make_async_remote_copy(
                src_ref=small_ref, dst_ref=all_ref.at[4 * px + 2 * py + pc], send_sem=sm_send.at[mask - 1],
                recv_sem=sm_recv.at[mask - 1], device_id=(px, py, pc), device_id_type=MESH))
        for cp in pushes:
            cp.start()
        for swap in swaps:
            swap.wait()
        for cp in arrivals:
            cp.wait_recv()
        for cp in pushes:
            cp.wait_send()
        sm_own.wait()

    res = pl.pallas_call(
        body,
        name=name,
        in_specs=[ANY] * (n + 1),
        out_specs=[ANY] * (n + 1),
        out_shape=[jax.ShapeDtypeStruct(h.shape, h.dtype) for h in halves]
        + [jax.ShapeDtypeStruct((N_DEV,) + small.shape, small.dtype)],
        scratch_shapes=[pltpu.SemaphoreType.DMA((n,))] * 2 + [pltpu.SemaphoreType.DMA((N_DEV - 1,))] * 2
        + [pltpu.SemaphoreType.DMA(())],
    )(*halves, small)
    return res[:n], res[n]


def _lower_bound(lbp):
    return jax.nn.softmax(lbp, axis=0)[0:1]


def _local_step(x, target, g1, gm, g2, gq, gk, go, rel_bias, lbp, weights, on_grads, grads_sent):
    b, s, d = x.shape
    t = b * s
    x0 = x.reshape(t, d)
    tgt = target.reshape(t, d)
    gq_t = jnp.tile(gq, (1, ATTN_HEADS))
    gk_t = jnp.tile(gk, (1, ATTN_HEADS))
    lb = _lower_bound(lbp)
    table = _band_table(_rel_bias_table("rel_bias_table", rel_bias))

    h1 = _rmsnorm_fwd("norm1", x0, g1)
    wg1, wu1, deps1 = weights["first"]((h1, table))
    a1, b1, z1 = _ffn_up("ffn1_up", h1, wg1, wu1, deps1)
    wd1, deps_mid = weights["mid"]((z1,))
    x1, h2 = _ffn_down("ffn1_down", z1, wd1, x0, gm, deps_mid)
    w_in, w_out = weights["mid_rest"]((x1,))
    ns = w_in.shape[0]
    proj = _in_proj("in_proj", h2, w_in)
    proj3 = proj.reshape(b, s, proj.shape[1])
    qn, kn, vb = _qk_prep("qk_prep", proj3, gq_t, gk_t)
    attn = _attn_fwd("attn_fwd", qn, kn, vb, table, weights["last_begin"]((qn,))).reshape(t, ATTN_W)
    mix, oraw, states = _hgrn_fwd("hgrn_fwd", proj, attn, lb, go, b, s)
    x2, h3 = _out_proj("out_proj", mix, w_out, x1, g2)
    wg2, wu2, wd2 = weights["last"]((h3,))
    a2, b2, z2 = _ffn_up("ffn2_up", h3, wg2, wu2)
    dy, dyh, sq = _ffn_down_loss("ffn2_down_loss", z2, wd2, x2, tgt)
    loss = 0.5 * jnp.sum(sq) / d

    da2, db2 = _ffn_bwd_act("ffn2_bwd_act", dyh, wd2, a2, b2)
    dwd2 = _grad_w_cols("ffn2_dwd", z2, dyh)
    dwg2 = _grad_w_cols("ffn2_dwg", da2, h3)
    dwu2 = _grad_w_cols("ffn2_dwu", db2, h3)
    sent2 = on_grads("ffn2", {"ffn2_w_gate": dwg2, "ffn2_w_up": dwu2, "ffn2_w_down": dwd2})
    dx2, dx2b, dg2 = _ffn_bwd_in("ffn2_bwd_in", da2, db2, wg2, wu2, x2, g2, dy, 1.0, sent2)
    sent2 = grads_sent("ffn2", dx2b)

    dwout = _grad_w_out("dw_out", mix, dx2b)
    dmix = _out_proj_bwd("out_proj_bwd", dx2b, w_out, sent2)
    dqn, dkn, dvn, dbe, dbo = _attn_bwd("attn_bwd", qn, kn, vb, table, dmix.reshape(b, s, dmix.shape[1]))
    dbias = dbe[:, :, :BAND] + dbo[:, :, CHUNK:]
    dpq, dpk, dpv, dgq, dgk = _qk_prep_bwd("qk_prep_bwd", proj3, dqn, dkn, dvn, gq_t, gk_t)
    dpq, dpk, dpv = (a.reshape(t, ATTN_W) for a in (dpq, dpk, dpv))
    dproj, dlb, dgo = _hgrn_bwd("hgrn_bwd", proj, (dpq, dpk, dpv), lb, go, oraw, states, dmix, b, s)
    dwin = _grad_w_in("dw_in", h2, dproj, ns)
    dx1, dx1h, dgm = _in_proj_bwd("in_proj_bwd", dproj, w_in, x1, gm, dx2, 0.5)

    dwd1 = _grad_w_cols("ffn1_dwd", z1, dx1h)
    sent_mix = on_grads("mix", {"w_in": dwin, "w_out": dwout.reshape(ns, dwout.shape[0] // ns, d),
                                "ffn1_w_down": dwd1})
    da1, db1 = _ffn_bwd_act("ffn1_bwd_act", dx1h, wd1, a1, b1, sent_mix)
    sent_mix = grads_sent("mix", da1)
    dwg1 = _grad_w_cols("ffn1_dwg", da1, h1, sent_mix)
    dwu1 = _grad_w_cols("ffn1_dwu", db1, h1)
    on_grads("ffn1", {"ffn1_w_gate": dwg1, "ffn1_w_up": dwu1})
    sent1 = grads_sent("ffn1", None)
    dx0, dg1 = _ffn_bwd_in("ffn1_bwd_in", da1, db1, wg1, wu1, x0, g1, dx1, None, sent1)

    nt = dg1.shape[0]
    sg = _small_grads(
        "small_grads", dg1.reshape(nt, d), dgm.reshape(nt, d), dg2.reshape(nt, d),
        dgq.reshape(-1, ATTN_W), dgk.reshape(-1, ATTN_W), dbias.transpose(1, 0, 2),
        dlb.reshape(b, HGRN_W), dgo.reshape(b, HGRN_W), lbp)
    g1g, gmg, g2g, gqg, gkg, rbg, lbg, gog = sg
    small = _pack_small(g1g, gmg, g2g, lbg, rbg[:, :N_REL], gqg, gkg, gog, loss)
    return dx0.reshape(b, s, d), small


LOSS_SLOT = 7 * SMALL_COLS + 2 * ATTN_DH + HGRN_DH


def _pack_small(g1, gm, g2, lbp, rel_bias, gq, gk, go, loss=None):
    flat = [g1.reshape(-1), gm.reshape(-1), g2.reshape(-1), lbp.reshape(-1), rel_bias.reshape(-1)]
    n_bias = 3 * SMALL_COLS - rel_bias.size
    heads = [gq.reshape(-1), gk.reshape(-1), go.reshape(-1)]
    heads.append(jnp.zeros((1,), F32) if loss is None else loss.reshape(1))
    n_tail = SMALL_COLS - sum(h.size for h in heads)
    return jnp.concatenate(flat + [jnp.zeros((n_bias,), F32)] + heads + [jnp.zeros((n_tail,), F32)]).reshape(
        SMALL_ROWS, SMALL_COLS)


def _unpack_small(p, d):
    flat = p.reshape(-1)
    o = 3 * d
    g1, gm, g2 = p[0:1], p[1:2], p[2:3]
    lbp = flat[o:o + 2 * HGRN_W].reshape(2, HGRN_W)
    o = 4 * SMALL_COLS
    rel = flat[o:o + ATTN_HEADS * N_REL].reshape(1, ATTN_HEADS, N_REL)
    o = 7 * SMALL_COLS
    gq = flat[o:o + ATTN_DH].reshape(1, ATTN_DH)
    gk = flat[o + ATTN_DH:o + 2 * ATTN_DH].reshape(1, ATTN_DH)
    go = flat[o + 2 * ATTN_DH:o + 2 * ATTN_DH + HGRN_DH].reshape(1, HGRN_DH)
    return g1, gm, g2, gq, gk, rel, lbp, go


def kernel(x, ffn1_norm_g, ffn1_w_gate, ffn1_w_up, ffn1_w_down, mix_norm_g, w_in, attn_q_norm_g, attn_k_norm_g, attn_rel_bias, hgrn_lower_bounds, hgrn_out_norm_g, w_out, ffn2_norm_g, ffn2_w_gate, ffn2_w_up, ffn2_w_down, loss_target, m_ffn1_norm_g, m_ffn1_w_gate, m_ffn1_w_up, m_ffn1_w_down, m_mix_norm_g, m_w_in, m_attn_q_norm_g, m_attn_k_norm_g, m_attn_rel_bias, m_hgrn_lower_bounds, m_hgrn_out_norm_g, m_w_out, m_ffn2_norm_g, m_ffn2_w_gate, m_ffn2_w_up, m_ffn2_w_down, v_ffn1_norm_g, v_ffn1_w_gate, v_ffn1_w_up, v_ffn1_w_down, v_mix_norm_g, v_w_in, v_attn_q_norm_g, v_attn_k_norm_g, v_attn_rel_bias, v_hgrn_lower_bounds, v_hgrn_out_norm_g, v_w_out, v_ffn2_norm_g, v_ffn2_w_gate, v_ffn2_w_up, v_ffn2_w_down):
    d = x.shape[-1]
    big_w = [ffn1_w_gate, ffn1_w_up, ffn1_w_down, w_in, w_out, ffn2_w_gate, ffn2_w_up, ffn2_w_down]
    big_m = [m_ffn1_w_gate, m_ffn1_w_up, m_ffn1_w_down, m_w_in, m_w_out, m_ffn2_w_gate, m_ffn2_w_up, m_ffn2_w_down]
    big_v = [v_ffn1_w_gate, v_ffn1_w_up, v_ffn1_w_down, v_w_in, v_w_out, v_ffn2_w_gate, v_ffn2_w_up, v_ffn2_w_down]
    big_names = ["ffn1_w_gate", "ffn1_w_up", "ffn1_w_down", "w_in", "w_out", "ffn2_w_gate", "ffn2_w_up", "ffn2_w_down"]
    flipped = {nm for nm in big_names if nm.endswith("gate") or nm.endswith("up")}
    flip = lambda nm, a: jnp.swapaxes(a, 1, 2) if nm in flipped else a
    big_w, big_m, big_v = ([flip(nm, a) for nm, a in zip(big_names, arrs)] for arrs in (big_w, big_m, big_v))

    shards = [w[0].astype(BF16) for w in big_w]
    start_a = _gather_start("gather_start_up1", shards[:2], ())
    start_b = _gather_start("gather_start_mid", shards[2:5], (start_a[4],))
    start_c = _gather_start("gather_start_ffn2", shards[5:], (start_b[4],))

    pending = {}

    def arrived(tag, started, after):
        send_sem, recv_sem, srcs, outs, _ = started
        return _gather_wait("gather_wait_" + tag, send_sem, recv_sem, srcs, outs, after)

    def first_weights(after):
        return (*_gather_join("gather_join_up1", *arrived("up1", start_a, after)), (start_c[4],))

    def mid_weights(after):
        srcs, outs = arrived("mid", start_b, after)
        (wd1,) = _gather_join("gather_join_wd1", srcs[:1], outs[:1])
        pending["mid"] = _join_start("join_start_mid", srcs[1:], outs[1:])
        return wd1, (pending["mid"][3],)

    def mid_rest(after):
        sems, srcs, outs, _ = pending["mid"]
        win_f, wout_f = _join_wait("join_wait_mid", sems, srcs, outs, after)
        return win_f, wout_f.reshape(wout_f.shape[0] * wout_f.shape[1], d)

    def last_begin(after):
        pending["ffn2"] = _join_start("join_start_ffn2", *arrived("ffn2", start_c, after))
        return (pending["ffn2"][3],)

    def last_weights(after):
        sems, srcs, outs, _ = pending["ffn2"]
        return _join_wait("join_wait_ffn2", sems, srcs, outs, after)

    weights = {"first": first_weights, "mid": mid_weights, "mid_rest": mid_rest, "last_begin": last_begin,
               "last": last_weights}

    core = lax.axis_index("c").astype(jnp.int32).reshape(1)
    chip = (2 * lax.axis_index("x") + lax.axis_index("y")).astype(jnp.int32).reshape(1)
    started = {}

    def on_grads(tag, grads):
        names = list(grads)
        started[tag] = (names, _pair_start("pair_start_" + tag, [grads[nm] for nm in names]))
        return (started[tag][1][4],)

    def grads_sent(tag, after):
        names, (send_sem, recv_sem, grads, lands, token) = started[tag]
        grads, theirs = _pair_wait("pair_wait_" + tag, send_sem, recv_sem, grads, lands, token if after is None else after)
        sums = [_pair_sum("pair_sum_" + nm, g, th, core) for nm, g, th in zip(names, grads, theirs)]
        started[tag] = (names, _scatter_start("scatter_start_" + tag, sums))
        return (started[tag][1][4],)

    grad_x, small_g = _local_step(
        x, loss_target, ffn1_norm_g, mix_norm_g, ffn2_norm_g, attn_q_norm_g, attn_k_norm_g, hgrn_out_norm_g,
        attn_rel_bias[0], hgrn_lower_bounds, weights, on_grads, grads_sent)

    def finish(tag, after):
        names, (send_sem, recv_sem, sums, lands, _) = started[tag]
        sums, lands = _scatter_wait("scatter_wait_" + tag, send_sem, recv_sem, sums, lands, after)
        return names, [_chip_sum("chip_sum_" + nm, sm, ld, chip) for nm, sm, ld in zip(names, sums, lands)]

    by_name = {nm: (w, m, v) for nm, w, m, v in zip(big_names, big_w, big_m, big_v)}
    updated = {}

    def update(names, halves, other_halves):
        for nm, mine, theirs in zip(names, halves, other_halves):
            w, m, v = by_name[nm]
            updated[nm] = _adamw("adamw_" + nm, w, mine, theirs, m, v, core)

    last_token = started["ffn1"][1][4]
    names_a, halves_a = finish("ffn2", last_token)
    names_m, halves_m = finish("mix", last_token)
    names_a, halves_a = names_a + names_m, halves_a + halves_m
    update(names_a, halves_a, _pair_join("pair_join_early", halves_a))
    names_b, halves_b = finish("ffn1", updated[names_a[-1]][1])
    others_b, small_all = _pair_join("pair_join_last", halves_b, small_g)
    update(names_b, halves_b, others_b)
    big_out = [updated[nm] for nm in big_names]

    pack = lambda g1, gm, g2, gq, gk, rel, lbp, go: _pack_small(g1, gm, g2, lbp, rel[0], gq, gk, go)
    small_w = pack(ffn1_norm_g, mix_norm_g, ffn2_norm_g, attn_q_norm_g, attn_k_norm_g, attn_rel_bias, hgrn_lower_bounds, hgrn_out_norm_g)
    small_m = pack(m_ffn1_norm_g, m_mix_norm_g, m_ffn2_norm_g, m_attn_q_norm_g, m_attn_k_norm_g, m_attn_rel_bias, m_hgrn_lower_bounds, m_hgrn_out_norm_g)
    small_v = pack(v_ffn1_norm_g, v_mix_norm_g, v_ffn2_norm_g, v_attn_q_norm_g, v_attn_k_norm_g, v_attn_rel_bias, v_hgrn_lower_bounds, v_hgrn_out_norm_g)
    small_res = _adamw_small("adamw_small", small_w, small_all, small_m, small_v)
    small_out = [_unpack_small(p, d) for p in small_res]
    loss = small_res[0].reshape(-1)[LOSS_SLOT]

    def assemble(kind):
        bg = [flip(nm, o[kind]) for nm, o in zip(big_names, big_out)]
        g1, gm, g2, gq, gk, rel, lbp, go = small_out[kind]
        return [g1, bg[0], bg[1], bg[2], gm, bg[3], gq, gk, rel, lbp, go, bg[4], g2, bg[5], bg[6], bg[7]]

    return (loss, grad_x, *assemble(0), *assemble(1), *assemble(2), *assemble(3))
```

```python
import functools

import jax
import jax.numpy as jnp
from jax import lax
from jax.experimental import pallas as pl
from jax.experimental.pallas import tpu as pltpu

F32 = jnp.float32
BF16 = jnp.bfloat16
MESH = pl.DeviceIdType.MESH

N_CHIPS = 4
N_DEV = 8
CHUNK = 64
ATTN_HEADS = 8
ATTN_DH = 64
ATTN_W = ATTN_HEADS * ATTN_DH
HGRN_HEADS = 4
HGRN_DH = 128
HGRN_W = HGRN_HEADS * HGRN_DH
LEFT_CHUNKS = 8
BAND = (LEFT_CHUNKS + 1) * CHUNK
KPAD = LEFT_CHUNKS * CHUNK
REL_CLIP = 128
N_REL = 2 * REL_CLIP + 1
N_REL_PAD = 384
RMS_EPS = 1e-6
LANES = 128
SMALL_ROWS = 8
SMALL_COLS = 1024

ADAM_LR = 0.001
ADAM_B1 = 0.9
ADAM_B2 = 0.999
ADAM_EPS = 1e-08
ADAM_WD = 0.01
ADAM_STEP = 10

NN = (((1,), (0,)), ((), ()))
NT = (((1,), (1,)), ((), ()))
TN = (((0,), (0,)), ((), ()))

VMEM_LIMIT = 48 * 1024 * 1024
MXU_WIDTH = 256
COL_CHUNK = 3 * MXU_WIDTH


def _sigmoid(x):
    return 1.0 / (1.0 + jnp.exp(-x))


def _silu(x):
    return x * _sigmoid(x)


def _dot(a, b, dims=NN):
    return lax.dot_general(a, b, dims, preferred_element_type=F32)


def _split3(x):
    hi = x.astype(BF16)
    r1 = x - hi.astype(F32)
    mid = r1.astype(BF16)
    lo = (r1 - mid.astype(F32)).astype(BF16)
    return hi, mid, lo


def _dot_exact_rhs(x, mat, dims=NN, pieces=3):
    hi, mid, lo = _split3(x)
    out = _dot(hi, mat, dims) + _dot(mid, mat, dims)
    return out + _dot(lo, mat, dims) if pieces == 3 else out


def _dot_exact_lhs(mat, x, dims=NN):
    hi, mid, lo = _split3(x)
    return _dot(mat, hi, dims) + _dot(mat, mid, dims) + _dot(mat, lo, dims)


def _params(*sem):
    return pltpu.CompilerParams(dimension_semantics=sem, vmem_limit_bytes=VMEM_LIMIT)


def _mm(name, ins, terms, n_acc, grid, acc_shape, outs, epilogue, extras=(), deps=()):
    nk = grid[2]
    ni, ne, nd, no = len(ins), len(extras), len(deps), len(outs)

    def body(*refs):
        in_refs = refs[:ni]
        ex_refs = refs[ni:ni + ne]
        out_refs = refs[ni + ne + nd:ni + ne + nd + no]
        acc_refs = refs[ni + ne + nd + no:]

        def products():
            parts = [None] * n_acc
            for ai, li, ri, dims in terms:
                d = _dot(in_refs[li][...], in_refs[ri][...], dims)
                parts[ai] = d if parts[ai] is None else parts[ai] + d
            return parts

        def finish(accs):
            res = epilogue(accs, [e[...] for e in ex_refs])
            for o, r in zip(out_refs, res):
                o[...] = r.astype(o.dtype)

        if nk == 1:
            finish(products())
        else:
            k = pl.program_id(2)

            @pl.when(k == 0)
            def _():
                for a, p in zip(acc_refs, products()):
                    a[...] = p

            if nk > 2:
                @pl.when(jnp.logical_and(k > 0, k < nk - 1))
                def _():
                    for a, p in zip(acc_refs, products()):
                        a[...] += p

            @pl.when(k == nk - 1)
            def _():
                finish([a[...] + p for a, p in zip(acc_refs, products())])

    scratch = [] if nk == 1 else [pltpu.VMEM(acc_shape, F32) for _ in range(n_acc)]
    res = pl.pallas_call(
        body,
        name=name,
        grid=grid,
        in_specs=[s for _, s in ins] + [s for _, s in extras] + [pl.BlockSpec(memory_space=pl.ANY)] * nd,
        out_specs=[s for _, s in outs],
        out_shape=[o for o, _ in outs],
        scratch_shapes=scratch,
        compiler_params=_params("parallel", "parallel", "arbitrary"),
    )(*[a for a, _ in ins], *[a for a, _ in extras], *deps)
    return res


def _mm_rows(name, lhs, weights, dims, t, outs, epilogue, extras=(), deps=()):
    tm = _row_tile(t)
    nl, ne, nd, no = len(lhs), len(extras), len(deps), len(outs)
    ns = weights[0].shape[0]

    def body(*refs):
        lhs_refs = refs[:nl]
        w_hbm = refs[nl:2 * nl]
        ex_refs = refs[2 * nl:2 * nl + ne]
        out_refs = refs[2 * nl + ne + nd:2 * nl + ne + nd + no]
        w_vmem = refs[2 * nl + ne + nd + no:3 * nl + ne + nd + no]
        sem = refs[-1]

        @pl.when(pl.program_id(0) == 0)
        def _():
            copies = [pltpu.make_async_copy(w_hbm[p], w_vmem[p], sem.at[p]) for p in range(nl)]
            for cp in copies:
                cp.start()
            for cp in copies:
                cp.wait()

        acc = None
        for p in range(nl):
            pick = lhs[p][2]
            for j in range(ns):
                part = _dot(pick(lhs_refs[p], j), w_vmem[p][j], dims)
                acc = part if acc is None else acc + part
        res = epilogue([acc], [e[...] for e in ex_refs])
        for o, r in zip(out_refs, res):
            o[...] = r.astype(o.dtype)

    return pl.pallas_call(
        body,
        name=name,
        grid=(t // tm,),
        in_specs=[s for _, s, _ in lhs] + [pl.BlockSpec(memory_space=pl.ANY)] * nl + [s for _, s in extras]
        + [pl.BlockSpec(memory_space=pl.ANY)] * nd,
        out_specs=[s for _, s in outs],
        out_shape=[o for o, _ in outs],
        scratch_shapes=[pltpu.VMEM(w.shape, w.dtype) for w in weights] + [pltpu.SemaphoreType.DMA((nl,))],
        compiler_params=_params("arbitrary"),
    )(*[a for a, _, _ in lhs], *weights, *[a for a, _ in extras], *deps)


def _mm_shards(name, x, weights, dims, outs, epilogue, extras=(), deps=()):
    t = x.shape[0]
    tm = _row_tile(t)
    nw, ne, nd, no = len(weights), len(extras), len(deps), len(outs)
    ns = weights[0].shape[0]

    def body(*refs):
        x_ref = refs[0]
        w_hbm = refs[1:1 + nw]
        ex_refs = refs[1 + nw:1 + nw + ne]
        out_refs = refs[1 + nw + ne + nd:1 + nw + ne + nd + no]
        w_vmem = refs[1 + nw + ne + nd + no:1 + 2 * nw + ne + nd + no]
        sem = refs[-1]

        @pl.when(pl.program_id(0) == 0)
        def _():
            copies = [pltpu.make_async_copy(w_hbm[p], w_vmem[p], sem.at[p]) for p in range(nw)]
            for cp in copies:
                cp.start()
            for cp in copies:
                cp.wait()

        xv = x_ref[...]
        accs = [_dot(xv, w_vmem[p][0], dims) for p in range(nw)]
        for j in range(ns):
            nxt = [_dot(xv, w_vmem[p][j + 1], dims) for p in range(nw)] if j + 1 < ns else None
            res = epilogue(accs, [e[j] for e in ex_refs])
            for (_, _, store), o, r in zip(outs, out_refs, res):
                store(o, j, r.astype(o.dtype))
            accs = nxt

    return pl.pallas_call(
        body,
        name=name,
        grid=(t // tm,),
        in_specs=[pl.BlockSpec((tm, x.shape[1]), lambda i: (i, 0))] + [pl.BlockSpec(memory_space=pl.ANY)] * nw
        + [s for _, s in extras] + [pl.BlockSpec(memory_space=pl.ANY)] * nd,
        out_specs=[s for _, s, _ in outs],
        out_shape=[o for o, _, _ in outs],
        scratch_shapes=[pltpu.VMEM(w.shape, w.dtype) for w in weights] + [pltpu.SemaphoreType.DMA((nw,))],
        compiler_params=_params("arbitrary"),
    )(x, *weights, *[a for a, _ in extras], *deps)


def _col_chunks(f):
    return [(c, min(COL_CHUNK, f - c)) for c in range(0, f, COL_CHUNK)]


def _mm_cols(name, x, weights, n_out, epilogue, extras=(), deps=()):
    t, k = x.shape
    f = weights[0].shape[0]
    tm = _row_tile(t)
    chunks = _col_chunks(f)
    nw, ne, nd = len(weights), len(extras), len(deps)

    def body(*refs):
        x_ref = refs[0]
        w_hbm = refs[1:1 + nw]
        ex_refs = refs[1 + nw:1 + nw + ne]
        out_refs = refs[1 + nw + ne + nd:1 + nw + ne + nd + n_out]
        w_vmem = refs[1 + nw + ne + nd + n_out:1 + 2 * nw + ne + nd + n_out]
        sem = refs[-1]

        @pl.when(pl.program_id(0) == 0)
        def _():
            copies = [pltpu.make_async_copy(w_hbm[p], w_vmem[p], sem.at[p]) for p in range(nw)]
            for cp in copies:
                cp.start()
            for cp in copies:
                cp.wait()

        xv = x_ref[...]

        def dots(c):
            c0, cw = chunks[c]
            return [_dot(xv, w[c0:c0 + cw, :], NT) for w in w_vmem]

        accs = dots(0)
        for c, (c0, cw) in enumerate(chunks):
            nxt = dots(c + 1) if c + 1 < len(chunks) else None
            res = epilogue(accs, [e[:, c0:c0 + cw] for e in ex_refs])
            for o, r in zip(out_refs, res):
                o[:, c0:c0 + cw] = r.astype(o.dtype)
            accs = nxt

    act = pl.BlockSpec((tm, f), lambda i: (i, 0))
    return pl.pallas_call(
        body,
        name=name,
        grid=(t // tm,),
        in_specs=[pl.BlockSpec((tm, k), lambda i: (i, 0))] + [pl.BlockSpec(memory_space=pl.ANY)] * nw + [act] * ne
        + [pl.BlockSpec(memory_space=pl.ANY)] * nd,
        out_specs=[act] * n_out,
        out_shape=[jax.ShapeDtypeStruct((t, f), BF16)] * n_out,
        scratch_shapes=[pltpu.VMEM(w.shape, w.dtype) for w in weights] + [pltpu.SemaphoreType.DMA((nw,))],
        compiler_params=_params("arbitrary"),
    )(x, *weights, *extras, *deps)


def _row_tile(t):
    return 512 if t % 512 == 0 else t


def _k_tile(t):
    return t if t <= 4096 else 1024


def _grad_k_tile(t):
    return 2048 if t % 2048 == 0 else t


def _rmsnorm(xv, g):
    ms = jnp.mean(xv * xv, axis=-1, keepdims=True)
    return xv * lax.rsqrt(ms + RMS_EPS) * g


def _rmsnorm_fwd(name, x, g):
    t, d = x.shape
    tm = _row_tile(t)

    def body(x_ref, g_ref, h_ref):
        h_ref[...] = _rmsnorm(x_ref[...], g_ref[...]).astype(BF16)

    return pl.pallas_call(
        body,
        name=name,
        grid=(t // tm,),
        in_specs=[pl.BlockSpec((tm, d), lambda i: (i, 0)), pl.BlockSpec((1, d), lambda i: (0, 0))],
        out_specs=pl.BlockSpec((tm, d), lambda i: (i, 0)),
        out_shape=jax.ShapeDtypeStruct((t, d), BF16),
        compiler_params=_params("parallel"),
    )(x, g)


def _norm_bwd_epilogue(copy_scale):
    def epilogue(accs, ex):
        dh = accs[0]
        xv, g, dres = ex
        ms = jnp.mean(xv * xv, axis=-1, keepdims=True)
        rstd = lax.rsqrt(ms + RMS_EPS)
        xhat = xv * rstd
        dxhat = dh * g
        dx = rstd * (dxhat - xhat * jnp.mean(dxhat * xhat, axis=-1, keepdims=True))
        out = dres + dx
        dg = jnp.sum(dh * xhat, axis=0, keepdims=True)
        if copy_scale is None:
            return out, dg
        return out, out * copy_scale, dg

    return epilogue


def _merged(w):
    return w.reshape(1, -1, w.shape[-1])


def _ffn_up(name, h, wg, wu, deps=()):
    def epilogue(accs, ex):
        a, b = accs
        sg = _sigmoid(a)
        act = a * sg
        return act, b * (sg * (1.0 + a * (1.0 - sg))), act * b

    return _mm_cols(name, h, [_merged(wg)[0], _merged(wu)[0]], 3, epilogue, deps=deps)


def _whole_rows(arr, tm):
    return arr, pl.BlockSpec((tm, arr.shape[1]), lambda i: (i, 0)), lambda ref, j: ref[...]


def _ffn_down(name, z, wd, x, g_next, deps=()):
    t = z.shape[0]
    d = wd.shape[2]
    tm = _row_tile(t)
    row = pl.BlockSpec((tm, d), lambda i: (i, 0))

    def epilogue(accs, ex):
        y = ex[0] + 0.5 * accs[0]
        return y, _rmsnorm(y, ex[1])

    return _mm_rows(
        name, [_whole_rows(z, tm)], [_merged(wd)], NN, t,
        outs=[(jax.ShapeDtypeStruct((t, d), F32), row), (jax.ShapeDtypeStruct((t, d), BF16), row)],
        epilogue=epilogue,
        extras=[(x, row), (g_next, pl.BlockSpec((1, d), lambda i: (0, 0)))],
        deps=deps,
    )


def _ffn_down_loss(name, z, wd, x, target):
    t = z.shape[0]
    d = wd.shape[2]
    tm = _row_tile(t)
    nt = t // tm
    row = pl.BlockSpec((tm, d), lambda i: (i, 0))

    def epilogue(accs, ex):
        e = ex[0] + 0.5 * accs[0] - ex[1]
        dy = e * (1.0 / d)
        return dy, 0.5 * dy, jnp.sum(e * e, axis=0, keepdims=True)

    return _mm_rows(
        name, [_whole_rows(z, tm)], [_merged(wd)], NN, t,
        outs=[(jax.ShapeDtypeStruct((t, d), F32), row), (jax.ShapeDtypeStruct((t, d), BF16), row),
              (jax.ShapeDtypeStruct((nt, 1, d), F32), pl.BlockSpec((None, 1, d), lambda i: (i, 0, 0)))],
        epilogue=epilogue,
        extras=[(x, row), (target, row)],
    )


def _ffn_bwd_act(name, dout, wd, act_a, dact_b, deps=()):
    def epilogue(accs, ex):
        dz = accs[0]
        return dz * ex[1].astype(F32), dz * ex[0].astype(F32)

    return _mm_cols(name, dout, [_merged(wd)[0]], 2, epilogue, extras=[act_a, dact_b], deps=deps)


def _grad_w_cols(name, z, dout, deps=()):
    t, f = z.shape
    d = dout.shape[1]
    tk = _grad_k_tile(t)
    fh = f // 2
    dw = _mm(
        name,
        ins=[(z, pl.BlockSpec((tk, fh), lambda j, n, k: (k, j))),
             (dout, pl.BlockSpec((tk, d), lambda j, n, k: (k, 0)))],
        terms=[(0, 0, 1, TN)],
        n_acc=1,
        grid=(2, 1, t // tk),
        acc_shape=(fh, d),
        outs=[(pltpu.HBM((f, d), BF16), pl.BlockSpec((fh, d), lambda j, n, k: (j, 0)))],
        epilogue=lambda accs, ex: (accs[0],),
        deps=deps,
    )[0]
    return dw.reshape(N_CHIPS, f // N_CHIPS, d)


def _norm_bwd_outs(t, d, tm, copy_scale):
    row = pl.BlockSpec((tm, d), lambda i: (i, 0))
    outs = [(jax.ShapeDtypeStruct((t, d), F32), row)]
    if copy_scale is not None:
        outs.append((jax.ShapeDtypeStruct((t, d), BF16), row))
    outs.append((jax.ShapeDtypeStruct((t // tm, 1, d), F32), pl.BlockSpec((None, 1, d), lambda i: (i, 0, 0))))
    return row, outs


def _ffn_bwd_in(name, da, db, wg, wu, x, g, dres, copy_scale, deps=()):
    t = da.shape[0]
    d = wg.shape[2]
    tm = _row_tile(t)
    row, outs = _norm_bwd_outs(t, d, tm, copy_scale)
    return _mm_rows(
        name, [_whole_rows(da, tm), _whole_rows(db, tm)], [_merged(wg), _merged(wu)], NN, t,
        outs=outs,
        epilogue=_norm_bwd_epilogue(copy_scale),
        extras=[(x, row), (g, pl.BlockSpec((1, d), lambda i: (0, 0))), (dres, row)],
        deps=deps,
    )


def _in_proj(name, h, w_in):
    t, d = h.shape
    ns, _, pj = w_in.shape
    tm = _row_tile(t)
    def store(ref, j, value):
        ref[:, j * pj:(j + 1) * pj] = value

    out = (jax.ShapeDtypeStruct((t, ns * pj), F32), pl.BlockSpec((tm, ns * pj), lambda i: (i, 0)), store)
    return _mm_shards(name, h, [w_in], NN, [out], lambda accs, ex: (accs[0],))[0]


def _in_proj_bwd(name, dp, w_in, x, g, dres, copy_scale, deps=()):
    t = dp.shape[0]
    ns, d, pj = w_in.shape
    tm = _row_tile(t)
    row, outs = _norm_bwd_outs(t, d, tm, copy_scale)
    cols = (dp, pl.BlockSpec((tm, ns * pj), lambda i: (i, 0)), lambda ref, j: ref[:, j * pj:(j + 1) * pj])
    return _mm_rows(
        name, [cols], [w_in], NT, t,
        outs=outs,
        epilogue=_norm_bwd_epilogue(copy_scale),
        extras=[(x, row), (g, pl.BlockSpec((1, d), lambda i: (0, 0))), (dres, row)],
        deps=deps,
    )


def _grad_w_in(name, h, dp, ns):
    t, d = h.shape
    pj = dp.shape[1] // ns
    tk = _k_tile(t)
    return _mm(
        name,
        ins=[(h, pl.BlockSpec((tk, d), lambda j, n, k: (k, 0))),
             (dp, pl.BlockSpec((tk, pj), lambda j, n, k: (k, j)))],
        terms=[(0, 0, 1, TN)],
        n_acc=1,
        grid=(ns, 1, t // tk),
        acc_shape=(d, pj),
        outs=[(pltpu.HBM((ns, d, pj), BF16), pl.BlockSpec((None, d, pj), lambda j, n, k: (j, 0, 0)))],
        epilogue=lambda accs, ex: (accs[0],),
    )[0]


def _out_proj(name, mix, w_out, x, g_next):
    t, dm = mix.shape
    d = w_out.shape[1]
    tm = _row_tile(t)
    row = pl.BlockSpec((tm, d), lambda i, n, k: (i, 0))
    return _mm(
        name,
        ins=[(mix, pl.BlockSpec((tm, dm), lambda i, n, k: (i, 0))),
             (w_out, pl.BlockSpec((dm, d), lambda i, n, k: (0, 0)))],
        terms=[(0, 0, 1, NN)],
        n_acc=1,
        grid=(t // tm, 1, 1),
        acc_shape=(tm, d),
        outs=[(jax.ShapeDtypeStruct((t, d), F32), row), (jax.ShapeDtypeStruct((t, d), BF16), row)],
        epilogue=lambda accs, ex: (ex[0] + accs[0], _rmsnorm(ex[0] + accs[0], ex[1])),
        extras=[(x, row), (g_next, pl.BlockSpec((1, d), lambda i, n, k: (0, 0)))],
    )


def _out_proj_bwd(name, dx, w_out, deps=()):
    t, d = dx.shape
    dm = w_out.shape[0]
    tm = _row_tile(t)
    return _mm(
        name,
        ins=[(dx, pl.BlockSpec((tm, d), lambda i, n, k: (i, 0))),
             (w_out, pl.BlockSpec((dm, d), lambda i, n, k: (0, 0)))],
        terms=[(0, 0, 1, NT)],
        n_acc=1,
        grid=(t // tm, 1, 1),
        acc_shape=(tm, dm),
        outs=[(jax.ShapeDtypeStruct((t, dm), F32), pl.BlockSpec((tm, dm), lambda i, n, k: (i, 0)))],
        epilogue=lambda accs, ex: (accs[0],),
        deps=deps,
    )[0]


def _grad_w_out(name, mix, dx):
    t, dm = mix.shape
    d = dx.shape[1]
    tk = _k_tile(t)
    return _mm(
        name,
        ins=[(mix, pl.BlockSpec((tk, dm), lambda a, n, k: (k, 0))),
             (dx, pl.BlockSpec((tk, d), lambda a, n, k: (k, 0)))],
        terms=[(0, 0, 1, TN)],
        n_acc=1,
        grid=(1, 1, t // tk),
        acc_shape=(dm, d),
        outs=[(pltpu.HBM((dm, d), BF16), pl.BlockSpec((dm, d), lambda a, n, k: (0, 0)))],
        epilogue=lambda accs, ex: (accs[0],),
    )[0]


def _head_group_matrix():
    r = lax.broadcasted_iota(jnp.int32, (ATTN_W, ATTN_W), 0)
    c = lax.broadcasted_iota(jnp.int32, (ATTN_W, ATTN_W), 1)
    same = jnp.right_shift(r, 6) == jnp.right_shift(c, 6)
    return jnp.where(same, 1.0, 0.0).astype(BF16)


def _qk_prep(name, proj, gq, gk):
    b, s, _ = proj.shape
    tm = KPAD
    nb = s // tm

    def body(q_ref, k_ref, v_ref, gq_ref, gk_ref, qn_ref, kn_ref, vb_ref):
        j = pl.program_id(1)
        bd = _head_group_matrix()

        def norm(xv, g):
            ms = _dot_exact_rhs(xv * xv, bd, pieces=2) * (1.0 / ATTN_DH)
            return xv * lax.rsqrt(ms + RMS_EPS) * g

        @pl.when(j == 0)
        def _():
            kn_ref[...] = jnp.zeros_like(kn_ref)
            vb_ref[...] = jnp.zeros_like(vb_ref)

        @pl.when(j > 0)
        def _():
            qn_ref[...] = norm(q_ref[...], gq_ref[...]).astype(BF16)
            kn_ref[...] = norm(k_ref[...], gk_ref[...]).astype(BF16)
            vb_ref[...] = v_ref[...].astype(BF16)

    src_blk = lambda col: pl.BlockSpec((None, tm, ATTN_W), lambda bi, j: (bi, jnp.maximum(j - 1, 0), col))
    gspec = pl.BlockSpec((1, ATTN_W), lambda bi, j: (0, 0))
    padded = pl.BlockSpec((None, tm, ATTN_W), lambda bi, j: (bi, j, 0))
    return pl.pallas_call(
        body,
        name=name,
        grid=(b, nb + 1),
        in_specs=[src_blk(0), src_blk(1), src_blk(2), gspec, gspec],
        out_specs=[src_blk(0), padded, padded],
        out_shape=[jax.ShapeDtypeStruct((b, s, ATTN_W), BF16), jax.ShapeDtypeStruct((b, KPAD + s, ATTN_W), BF16),
                   jax.ShapeDtypeStruct((b, KPAD + s, ATTN_W), BF16)],
        compiler_params=_params("parallel", "arbitrary"),
    )(proj, proj, proj, gq, gk)


def _qk_prep_bwd(name, proj, dqn, dkn, dv, gq, gk):
    b, s, _ = proj.shape
    tm = KPAD
    nb = s // tm

    def body(q_ref, k_ref, dqn_ref, dkn_ref, dv_ref, gq_ref, gk_ref, dq_ref, dk_ref, dvb_ref, dgq_ref, dgk_ref):
        bd = _head_group_matrix()

        def bwd(xv, dy, g):
            ms = _dot_exact_rhs(xv * xv, bd, pieces=2) * (1.0 / ATTN_DH)
            rstd = lax.rsqrt(ms + RMS_EPS)
            xhat = xv * rstd
            dxhat = dy * g
            gm = _dot_exact_rhs(dxhat * xhat, bd, pieces=2) * (1.0 / ATTN_DH)
            return rstd * (dxhat - xhat * gm), jnp.sum(dy * xhat, axis=0, keepdims=True)

        dq, dgq = bwd(q_ref[...], dqn_ref[...], gq_ref[...])
        dk, dgk = bwd(k_ref[...], dkn_ref[...], gk_ref[...])
        dq_ref[...] = dq.astype(BF16)
        dk_ref[...] = dk.astype(BF16)
        dvb_ref[...] = dv_ref[...].astype(BF16)
        dgq_ref[...] = dgq
        dgk_ref[...] = dgk

    col = lambda c: pl.BlockSpec((None, tm, ATTN_W), lambda bi, j: (bi, j, c))
    past_pad = pl.BlockSpec((None, tm, ATTN_W), lambda bi, j: (bi, j + 1, 0))
    gspec = pl.BlockSpec((1, ATTN_W), lambda bi, j: (0, 0))
    pspec = pl.BlockSpec((None, 1, ATTN_W), lambda bi, j: (bi * nb + j, 0, 0))
    o_shape = jax.ShapeDtypeStruct((b, s, ATTN_W), BF16)
    p_shape = jax.ShapeDtypeStruct((b * nb, 1, ATTN_W), F32)
    return pl.pallas_call(
        body,
        name=name,
        grid=(b, nb),
        in_specs=[col(0), col(1), col(0), past_pad, past_pad, gspec, gspec],
        out_specs=[col(0)] * 3 + [pspec] * 2,
        out_shape=[o_shape] * 3 + [p_shape] * 2,
        compiler_params=_params("parallel", "parallel"),
    )(proj, proj, dqn, dkn, dv, gq, gk)


Q_CHUNKS = 4
QBLK = Q_CHUNKS * CHUNK
WIN = (LEFT_CHUNKS + Q_CHUNKS) * CHUNK
DB_W = BAND + CHUNK
MASKED = -1e30


def _band_table(bias):
    rows = [jnp.pad(bias, ((0, 0), (0, 0), (CHUNK * i, WIN - BAND - CHUNK * i)), constant_values=MASKED)
            for i in range(Q_CHUNKS)]
    return jnp.concatenate(rows, axis=1)


def _head_lanes(hh):
    lane = lax.broadcasted_iota(jnp.int32, (1, LANES), 1)
    return (lane < ATTN_DH) if hh == 0 else (lane >= ATTN_DH)


def _attn_probs(qh, kw, table, start):
    s = _dot(qh, kw, NT) * (ATTN_DH ** -0.5) + table
    col = lax.broadcasted_iota(jnp.int32, (QBLK, WIN), 1)
    s = jnp.where(col + start >= KPAD, s, MASKED)
    m = jnp.max(s, axis=-1, keepdims=True)
    p = jnp.exp(s - m)
    return p * (1.0 / jnp.sum(p, axis=-1, keepdims=True))


def _attn_fwd(name, q, k, v, table, deps=()):
    b, s, w = q.shape
    sp = k.shape[1]

    def body(q_ref, k_ref, v_ref, t_ref, *rest):
        o_ref = rest[-1]
        start = pl.multiple_of(pl.program_id(2) * QBLK, QBLK)
        kw = k_ref[pl.ds(start, WIN), :]
        vw = v_ref[pl.ds(start, WIN), :]
        q2 = q_ref[...]
        lanes = [_head_lanes(hh) for hh in range(2)]
        probs = [_attn_probs(jnp.where(mine, q2, jnp.zeros_like(q2)), kw, t_ref[hh], start).astype(BF16)
                 for hh, mine in enumerate(lanes)]
        outs = [_dot(p, vw) for p in probs]
        o_ref[...] = jnp.where(lanes[0], outs[0], outs[1]).astype(BF16)

    qspec = pl.BlockSpec((None, QBLK, LANES), lambda p, bi, i: (bi, i, p))
    kspec = pl.BlockSpec((None, sp, LANES), lambda p, bi, i: (bi, 0, p))
    return pl.pallas_call(
        body,
        name=name,
        grid=(w // LANES, b, s // QBLK),
        in_specs=[qspec, kspec, kspec, pl.BlockSpec((2, QBLK, WIN), lambda p, bi, i: (p, 0, 0))] + [ANY] * len(deps),
        out_specs=qspec,
        out_shape=jax.ShapeDtypeStruct((b, s, w), BF16),
        compiler_params=_params("parallel", "parallel", "arbitrary"),
    )(q, k, v, table, *deps)


def _attn_bwd(name, q, k, v, table, dmix):
    b, s, w = q.shape
    sp = k.shape[1]

    def body(q_ref, k_ref, v_ref, t_ref, do_ref, dq_ref, dk_ref, dv_ref, dbe_ref, dbo_ref):
        bi = pl.program_id(1)
        i = pl.program_id(2)
        start = pl.multiple_of(i * QBLK, QBLK)
        win = pl.ds(start, WIN)

        @pl.when(i == 0)
        def _():
            dk_ref[...] = jnp.zeros_like(dk_ref)
            dv_ref[...] = jnp.zeros_like(dv_ref)

        @pl.when(jnp.logical_and(i == 0, bi == 0))
        def _():
            dbe_ref[...] = jnp.zeros_like(dbe_ref)
            dbo_ref[...] = jnp.zeros_like(dbo_ref)

        kw = k_ref[win, :]
        vw = v_ref[win, :]
        q2 = q_ref[...]
        do2 = do_ref[...].astype(BF16)
        lanes = [_head_lanes(hh) for hh in range(2)]
        qh = [jnp.where(mine, q2, jnp.zeros_like(q2)) for mine in lanes]
        doh = [jnp.where(mine, do2, jnp.zeros_like(do2)) for mine in lanes]
        p = [_attn_probs(qh[hh], kw, t_ref[hh], start) for hh in range(2)]
        dp = [_dot(doh[hh], vw, NT) for hh in range(2)]
        ds = [p[hh] * (dp[hh] - jnp.sum(p[hh] * dp[hh], axis=-1, keepdims=True)) for hh in range(2)]
        dsb = [(x * (ATTN_DH ** -0.5)).astype(BF16) for x in ds]
        pb = [x.astype(BF16) for x in p]
        dq = [_dot(dsb[hh], kw) for hh in range(2)]
        dk = [_dot(dsb[hh], qh[hh], TN) for hh in range(2)]
        dv = [_dot(pb[hh], doh[hh], TN) for hh in range(2)]
        for hh in range(2):
            for qi in range(Q_CHUNKS):
                c0 = (qi // 2) * LANES
                blk = ds[hh][qi * CHUNK:(qi + 1) * CHUNK, c0:c0 + DB_W]
                if qi % 2 == 0:
                    dbe_ref[hh] += blk
                else:
                    dbo_ref[hh] += blk
        dq_ref[...] = jnp.where(lanes[0], dq[0], dq[1])
        dk_ref[win, :] += dk[0] + dk[1]
        dv_ref[win, :] += dv[0] + dv[1]

    qspec = pl.BlockSpec((None, QBLK, LANES), lambda p, bi, i: (bi, i, p))
    kspec = pl.BlockSpec((None, sp, LANES), lambda p, bi, i: (bi, 0, p))
    dbspec = pl.BlockSpec((2, CHUNK, DB_W), lambda p, bi, i: (p, 0, 0))
    db_shape = jax.ShapeDtypeStruct((ATTN_HEADS, CHUNK, DB_W), F32)
    return pl.pallas_call(
        body,
        name=name,
        grid=(w // LANES, b, s // QBLK),
        in_specs=[qspec, kspec, kspec, pl.BlockSpec((2, QBLK, WIN), lambda p, bi, i: (p, 0, 0)), qspec],
        out_specs=[qspec, kspec, kspec, dbspec, dbspec],
        out_shape=[jax.ShapeDtypeStruct((b, s, w), F32), jax.ShapeDtypeStruct((b, sp, w), F32),
                   jax.ShapeDtypeStruct((b, sp, w), F32), db_shape, db_shape],
        compiler_params=_params("arbitrary", "arbitrary", "arbitrary"),
    )(q, k, v, table, dmix)


HQ_COL = 3 * ATTN_W // HGRN_DH
HF_COL = HQ_COL + HGRN_HEADS
HI_COL = HF_COL + HGRN_HEADS
HG_COL = HI_COL + HGRN_HEADS
HGRN_ROWS = 8 * CHUNK
HEAD_LANES = [slice(hh * HGRN_DH, (hh + 1) * HGRN_DH) for hh in range(HGRN_HEADS)]


def _tri(lower):
    r = lax.broadcasted_iota(jnp.int32, (CHUNK, CHUNK), 0)
    c = lax.broadcasted_iota(jnp.int32, (CHUNK, CHUNK), 1)
    return (r >= c) if lower else (r <= c)


def _hgrn_chunk(hq, hf, lb, tril):
    sig = _sigmoid(hf)
    f = lb + (1.0 - lb) * sig
    g = jnp.log(f)
    ones_l = jnp.where(tril, 1.0, 0.0).astype(BF16)
    b = _dot_exact_lhs(ones_l, g)
    bl = jnp.sum(g, axis=0, keepdims=True)
    rows = lax.broadcasted_iota(jnp.int32, g.shape, 0)
    bm = jnp.sum(jnp.where(rows <= CHUNK // 2, g, 0.0), axis=0, keepdims=True)
    sq = _sigmoid(hq)
    q = hq * sq
    k = 1.0 - f
    return sig, f, b, bl, bm, sq, q, k


def _hgrn_fwd(name, proj, attn, lb, go, b, s):
    nc = s // CHUNK
    t = b * s
    nblk = s // HGRN_ROWS
    cpb = HGRN_ROWS // CHUNK

    def body(hq_ref, hf_ref, hi_ref, hg_ref, attn_ref, lb_ref, go_ref, mix_ref, oraw_ref, st_ref, s_scr):
        tril = _tri(True)
        gov = go_ref[...]
        mix_ref[:, 0:ATTN_W] = attn_ref[...]

        @pl.when(pl.program_id(1) == 0)
        def _():
            s_scr[...] = jnp.zeros_like(s_scr)

        def step(c, carry):
            sl = pl.ds(pl.multiple_of(c * CHUNK, CHUNK), CHUNK)
            hg = hg_ref[sl, :]
            _, _, bb, bl, bm, _, q, k = _hgrn_chunk(hq_ref[sl, :], hf_ref[sl, :], lb_ref[...], tril)
            vb = hi_ref[sl, :].astype(BF16)
            qe = (q * jnp.exp(bb - bm)).astype(BF16)
            ke = (k * jnp.exp(bm - bb)).astype(BF16)
            qb = (q * jnp.exp(bb)).astype(BF16)
            kb = (k * jnp.exp(bl - bb)).astype(BF16)
            e_last = jnp.exp(bl)
            gate = _silu(hg)
            st = [s_scr[hh] for hh in range(HGRN_HEADS)]
            a = [jnp.where(tril, _dot(qe[:, hs], ke[:, hs], NT), 0.0).astype(BF16) for hs in HEAD_LANES]
            o_state = [_dot(qb[:, hs], st[hh].astype(BF16), NT) for hh, hs in enumerate(HEAD_LANES)]
            st_next = [st[hh] * e_last[:, hs] + _dot(vb[:, hs], kb[:, hs], TN) for hh, hs in enumerate(HEAD_LANES)]
            o = [_dot(a[hh], vb[:, hs]) + o_state[hh] for hh, hs in enumerate(HEAD_LANES)]
            ro = [(oh * lax.rsqrt(jnp.mean(oh * oh, axis=-1, keepdims=True) + RMS_EPS) * gov) * gate[:, hs]
                  for oh, hs in zip(o, HEAD_LANES)]
            for hh in range(HGRN_HEADS):
                st_ref[hh, c] = st[hh]
                s_scr[hh] = st_next[hh]
            mix_ref[sl, ATTN_W:ATTN_W + HGRN_W] = jnp.concatenate(ro, axis=1).astype(BF16)
            oraw_ref[sl, :] = jnp.concatenate(o, axis=1)
            return carry

        lax.fori_loop(0, cpb, step, 0)

    col = lambda base: pl.BlockSpec((HGRN_ROWS, HGRN_W), lambda bi, i: (bi * nblk + i, base // HGRN_HEADS))
    out = pl.BlockSpec((HGRN_ROWS, HGRN_W), lambda bi, i: (bi * nblk + i, 0))
    return pl.pallas_call(
        body,
        name=name,
        grid=(b, nblk),
        in_specs=[col(HQ_COL), col(HF_COL), col(HI_COL), col(HG_COL), out,
                  pl.BlockSpec((1, HGRN_W), lambda bi, i: (0, 0)), pl.BlockSpec((1, HGRN_DH), lambda bi, i: (0, 0))],
        out_specs=[pl.BlockSpec((HGRN_ROWS, ATTN_W + HGRN_W), lambda bi, i: (bi * nblk + i, 0)), out,
                   pl.BlockSpec((None, HGRN_HEADS, cpb, HGRN_DH, HGRN_DH), lambda bi, i: (bi, 0, i, 0, 0))],
        out_shape=[jax.ShapeDtypeStruct((t, ATTN_W + HGRN_W), BF16), jax.ShapeDtypeStruct((t, HGRN_W), F32),
                   jax.ShapeDtypeStruct((b, HGRN_HEADS, nc, HGRN_DH, HGRN_DH), F32)],
        scratch_shapes=[pltpu.VMEM((HGRN_HEADS, HGRN_DH, HGRN_DH), F32)],
        compiler_params=_params("parallel", "arbitrary"),
    )(proj, proj, proj, proj, attn, lb, go)


def _hgrn_bwd(name, proj, dqkv, lb, go, oraw, states, dmix, b, s):
    t = b * s
    nblk = s // HGRN_ROWS
    cpb = HGRN_ROWS // CHUNK

    def body(hq_ref, hf_ref, hi_ref, hg_ref, dq_ref, dk_ref, dv_ref, lb_ref, go_ref, oraw_ref, st_ref, dro_ref,
             dp_ref, dlb_ref, dgo_ref, ds_scr, dlb_scr, dgo_scr):
        tril = _tri(True)
        ones_u = jnp.where(_tri(False), 1.0, 0.0).astype(BF16)
        gov = go_ref[...]
        dp_ref[:, 0:ATTN_W] = dq_ref[...]
        dp_ref[:, ATTN_W:2 * ATTN_W] = dk_ref[...]
        dp_ref[:, 2 * ATTN_W:3 * ATTN_W] = dv_ref[...]

        @pl.when(pl.program_id(1) == 0)
        def _():
            ds_scr[...] = jnp.zeros_like(ds_scr)
            dlb_scr[...] = jnp.zeros_like(dlb_scr)
            dgo_scr[...] = jnp.zeros_like(dgo_scr)

        def step(ci, carry):
            c = cpb - 1 - ci
            sl = pl.ds(pl.multiple_of(c * CHUNK, CHUNK), CHUNK)
            hq = hq_ref[sl, :]
            hg = hg_ref[sl, :]
            sig, f, bb, bl, bm, sq, q, k = _hgrn_chunk(hq, hf_ref[sl, :], lb_ref[...], tril)
            vb = hi_ref[sl, :].astype(BF16)
            ebm = jnp.exp(bb - bm)
            embm = jnp.exp(bm - bb)
            eb = jnp.exp(bb)
            ebl = jnp.exp(bl - bb)
            e_last = jnp.exp(bl)
            qe = (q * ebm).astype(BF16)
            ke = (k * embm).astype(BF16)
            qb = (q * eb).astype(BF16)
            kb = (k * ebl).astype(BF16)
            st = [st_ref[hh, c] for hh in range(HGRN_HEADS)]
            dst = [ds_scr[hh] for hh in range(HGRN_HEADS)]
            o = oraw_ref[sl, :]
            dro = dro_ref[sl, :]
            sg = _sigmoid(hg)
            gov4 = jnp.concatenate([gov] * HGRN_HEADS, axis=1)
            rstd = jnp.concatenate(
                [jnp.broadcast_to(lax.rsqrt(jnp.mean(o[:, hs] * o[:, hs], axis=-1, keepdims=True) + RMS_EPS),
                                  (CHUNK, HGRN_DH)) for hs in HEAD_LANES], axis=1)
            ohat = o * rstd
            dn = dro * (hg * sg)
            dhg = dro * (ohat * gov4) * (sg * (1.0 + hg * (1.0 - sg)))
            dgo_inc = jnp.sum(dn * ohat, axis=0, keepdims=True)
            dohat = dn * gov4
            proj_h = dohat * ohat
            pm = jnp.concatenate(
                [jnp.broadcast_to(jnp.mean(proj_h[:, hs], axis=-1, keepdims=True), (CHUNK, HGRN_DH))
                 for hs in HEAD_LANES], axis=1)
            dob = (rstd * (dohat - ohat * pm)).astype(BF16)
            stb = [x.astype(BF16) for x in st]
            dstb = [x.astype(BF16) for x in dst]
            a = [jnp.where(tril, _dot(qe[:, hs], ke[:, hs], NT), 0.0).astype(BF16) for hs in HEAD_LANES]
            dab = [jnp.where(tril, _dot(dob[:, hs], vb[:, hs], NT), 0.0).astype(BF16) for hs in HEAD_LANES]
            dqb = [_dot(dob[:, hs], stb[hh]) for hh, hs in enumerate(HEAD_LANES)]
            dkb = [_dot(vb[:, hs], dstb[hh]) for hh, hs in enumerate(HEAD_LANES)]
            dv_state = [_dot(kb[:, hs], dstb[hh], NT) for hh, hs in enumerate(HEAD_LANES)]
            dst_next = [dst[hh] * e_last[:, hs] + _dot(dob[:, hs], qb[:, hs], TN) for hh, hs in enumerate(HEAD_LANES)]
            dv = [_dot(a[hh], dob[:, hs], TN) + dv_state[hh] for hh, hs in enumerate(HEAD_LANES)]
            dqe = jnp.concatenate([_dot(dab[hh], ke[:, hs]) for hh, hs in enumerate(HEAD_LANES)], axis=1)
            dke = jnp.concatenate([_dot(dab[hh], qe[:, hs], TN) for hh, hs in enumerate(HEAD_LANES)], axis=1)
            dqb = jnp.concatenate(dqb, axis=1)
            dkb = jnp.concatenate(dkb, axis=1)
            state_term = jnp.concatenate(
                [jnp.sum(dst[hh] * st[hh], axis=0, keepdims=True) for hh in range(HGRN_HEADS)], axis=1)
            dq = dqe * ebm + dqb * eb
            dk = dke * embm + dkb * ebl
            db = (qe.astype(F32) * dqe - ke.astype(F32) * dke) + q * (dqb * eb) - k * (dkb * ebl)
            d_last = jnp.sum(k * ebl * dkb, axis=0, keepdims=True) + state_term * e_last
            dg = _dot_exact_lhs(ones_u, db) + d_last
            df = dg / f - dk
            first = HQ_COL * HGRN_DH
            dp_ref[sl, first:first + HGRN_W] = (dq * (sq * (1.0 + hq * (1.0 - sq)))).astype(BF16)
            dp_ref[sl, first + HGRN_W:first + 2 * HGRN_W] = (df * (1.0 - lb_ref[...]) * sig * (1.0 - sig)).astype(BF16)
            dp_ref[sl, first + 2 * HGRN_W:first + 3 * HGRN_W] = jnp.concatenate(dv, axis=1).astype(BF16)
            dp_ref[sl, first + 3 * HGRN_W:first + 4 * HGRN_W] = dhg.astype(BF16)
            dlb_scr[...] += jnp.sum(df * (1.0 - sig), axis=0, keepdims=True)
            dgo_scr[...] += dgo_inc
            for hh in range(HGRN_HEADS):
                ds_scr[hh] = dst_next[hh]
            return carry

        lax.fori_loop(0, cpb, step, 0)

        @pl.when(pl.program_id(1) == nblk - 1)
        def _():
            dlb_ref[...] = dlb_scr[...]
            dgo_ref[...] = dgo_scr[...]

    rows = lambda bi, i: bi * nblk + (nblk - 1 - i)
    col = lambda base: pl.BlockSpec((HGRN_ROWS, HGRN_W), lambda bi, i: (rows(bi, i), base // HGRN_HEADS))
    out = pl.BlockSpec((HGRN_ROWS, HGRN_W), lambda bi, i: (rows(bi, i), 0))
    part = pl.BlockSpec((None, 1, HGRN_W), lambda bi, i: (bi, 0, 0))
    width = HG_COL * HGRN_DH + HGRN_W
    o_shape = jax.ShapeDtypeStruct((t, width), BF16)
    p_shape = jax.ShapeDtypeStruct((b, 1, HGRN_W), F32)
    return pl.pallas_call(
        body,
        name=name,
        grid=(b, nblk),
        in_specs=[col(HQ_COL), col(HF_COL), col(HI_COL), col(HG_COL), out, out, out,
                  pl.BlockSpec((1, HGRN_W), lambda bi, i: (0, 0)), pl.BlockSpec((1, HGRN_DH), lambda bi, i: (0, 0)), out,
                  pl.BlockSpec((None, HGRN_HEADS, cpb, HGRN_DH, HGRN_DH), lambda bi, i: (bi, 0, nblk - 1 - i, 0, 0)),
                  col(ATTN_W // HGRN_DH)],
        out_specs=[pl.BlockSpec((HGRN_ROWS, width), lambda bi, i: (rows(bi, i), 0))] + [part] * 2,
        out_shape=[o_shape] + [p_shape] * 2,
        scratch_shapes=[pltpu.VMEM((HGRN_HEADS, HGRN_DH, HGRN_DH), F32), pltpu.VMEM((1, HGRN_W), F32),
                        pltpu.VMEM((1, HGRN_W), F32)],
        compiler_params=_params("parallel", "arbitrary"),
    )(proj, proj, proj, proj, *dqkv, lb, go, oraw, states, dmix)


def _small_grads(name, dg1, dgm, dg2, dgq, dgk, dbias_t, dlb, dgo, lbp):
    d = dg1.shape[1]

    def body(dg1_ref, dgm_ref, dg2_ref, dgq_ref, dgk_ref, dbias_ref, dlb_ref, dgo_ref, lbp_ref,
             g1_ref, gm_ref, g2_ref, gq_ref, gk_ref, rb_ref, lbg_ref, go_ref):
        g1_ref[...] = jnp.sum(dg1_ref[...], axis=0, keepdims=True)
        gm_ref[...] = jnp.sum(dgm_ref[...], axis=0, keepdims=True)
        g2_ref[...] = jnp.sum(dg2_ref[...], axis=0, keepdims=True)
        r = lax.broadcasted_iota(jnp.int32, (ATTN_W, ATTN_DH), 0)
        cidx = lax.broadcasted_iota(jnp.int32, (ATTN_W, ATTN_DH), 1)
        fold = jnp.where(jnp.bitwise_and(r, ATTN_DH - 1) == cidx, 1.0, 0.0).astype(BF16)
        gq_ref[...] = jnp.sum(_dot_exact_rhs(dgq_ref[...], fold), axis=0, keepdims=True)
        gk_ref[...] = jnp.sum(_dot_exact_rhs(dgk_ref[...], fold), axis=0, keepdims=True)
        gosum = jnp.sum(dgo_ref[...], axis=0, keepdims=True)
        go_ref[...] = (gosum[:, 0:HGRN_DH] + gosum[:, HGRN_DH:2 * HGRN_DH]
                       + gosum[:, 2 * HGRN_DH:3 * HGRN_DH] + gosum[:, 3 * HGRN_DH:4 * HGRN_DH])
        p0 = lbp_ref[0:1, :]
        p1 = lbp_ref[1:2, :]
        lbv = 1.0 / (1.0 + jnp.exp(p1 - p0))
        dp0 = jnp.sum(dlb_ref[...], axis=0, keepdims=True) * lbv * (1.0 - lbv)
        lbg_ref[0:1, :] = dp0
        lbg_ref[1:2, :] = -dp0
        sidx = lax.broadcasted_iota(jnp.int32, (BAND, N_REL_PAD), 0)
        ridx = lax.broadcasted_iota(jnp.int32, (BAND, N_REL_PAD), 1)

        def step(tq, acc):
            rel = jnp.clip(tq + KPAD - sidx, -REL_CLIP, REL_CLIP) + REL_CLIP
            onehot = jnp.where(rel == ridx, 1.0, 0.0).astype(BF16)
            return acc + _dot_exact_rhs(dbias_ref[tq], onehot)

        rb_ref[...] = lax.fori_loop(0, CHUNK, step, jnp.zeros((ATTN_HEADS, N_REL_PAD), F32))

    ins = [dg1, dgm, dg2, dgq, dgk, dbias_t, dlb, dgo, lbp]
    outs = [jax.ShapeDtypeStruct((1, d), F32)] * 3 + [jax.ShapeDtypeStruct((1, ATTN_DH), F32)] * 2 + [
        jax.ShapeDtypeStruct((ATTN_HEADS, N_REL_PAD), F32), jax.ShapeDtypeStruct((2, HGRN_W), F32),
        jax.ShapeDtypeStruct((1, HGRN_DH), F32)]
    vm = pl.BlockSpec(memory_space=pltpu.VMEM)
    return pl.pallas_call(
        body,
        name=name,
        in_specs=[vm] * len(ins),
        out_specs=[vm] * len(outs),
        out_shape=outs,
        compiler_params=pltpu.CompilerParams(vmem_limit_bytes=VMEM_LIMIT),
    )(*ins)


def _adam_update(w, g, m, v):
    m2 = ADAM_B1 * m + (1.0 - ADAM_B1) * g
    v2 = ADAM_B2 * v + (1.0 - ADAM_B2) * (g * g)
    m_hat = m2 / (1.0 - ADAM_B1 ** ADAM_STEP)
    v_hat = v2 / (1.0 - ADAM_B2 ** ADAM_STEP)
    delta = -ADAM_LR * (m_hat / (jnp.sqrt(v_hat) + ADAM_EPS) + ADAM_WD * w)
    return delta, m2, v2


def _rows_tile(r):
    return r if r <= 512 or r % 512 else 512


def _pair_sum(name, grad, theirs, core):
    n, half, c = theirs.shape
    tr = _rows_tile(half)
    nth = half // tr

    def body(core_ref, a_ref, b_ref, o_ref):
        o_ref[...] = (a_ref[...].astype(F32) + b_ref[...].astype(F32)).astype(o_ref.dtype)

    spec = pl.BlockSpec((None, tr, c), lambda i, j, core_ref: (i, j, 0))
    return pl.pallas_call(
        body, name=name,
        grid_spec=pltpu.PrefetchScalarGridSpec(
            num_scalar_prefetch=1, grid=(n, nth),
            in_specs=[pl.BlockSpec((None, tr, c), lambda i, j, core_ref: (i, core_ref[0] * nth + j, 0)), spec],
            out_specs=spec),
        out_shape=pltpu.HBM((n, half, c), BF16), compiler_params=_params("parallel", "parallel"),
    )(core, grad, theirs)


def _chip_sum(name, own, parts, chip):
    _, half, c = own.shape
    tr = _rows_tile(half)

    def body(chip_ref, own_ref, p_ref, o_ref):
        me = chip_ref[0]
        mine = own_ref[...].astype(F32)
        flip_x, flip_y, flip_xy = (p_ref[i].astype(F32) for i in range(3))
        acc = None
        for k in range(N_CHIPS):
            rel = jnp.bitwise_xor(me, k)
            term = jnp.where(rel == 0, mine, jnp.where(rel == 2, flip_x, jnp.where(rel == 1, flip_y, flip_xy)))
            acc = term if acc is None else acc + term
        o_ref[...] = acc

    return pl.pallas_call(
        body, name=name,
        grid_spec=pltpu.PrefetchScalarGridSpec(
            num_scalar_prefetch=1, grid=(half // tr,),
            in_specs=[pl.BlockSpec((None, tr, c), lambda j, chip_ref: (chip_ref[0], j, 0)),
                      pl.BlockSpec((3, tr, c), lambda j, chip_ref: (0, j, 0))],
            out_specs=pl.BlockSpec((tr, c), lambda j, chip_ref: (j, 0))),
        out_shape=pltpu.HBM((half, c), F32), compiler_params=_params("parallel"),
    )(chip, own, parts)


def _adamw(name, w, g_mine, g_theirs, m, v, core):
    _, r, c = w.shape
    half = r // 2
    tr = _rows_tile(half)
    nth = half // tr

    def body(core_ref, w_ref, gm_ref, gt_ref, m_ref, v_ref, g_ref, d_ref, m2_ref, v2_ref):
        g = jnp.where(pl.program_id(0) == core_ref[0], gm_ref[...], gt_ref[...])
        delta, m2, v2 = _adam_update(w_ref[...], g, m_ref[...], v_ref[...])
        g_ref[...] = g
        d_ref[...] = delta
        m2_ref[...] = m2
        v2_ref[...] = v2

    full = pl.BlockSpec((None, tr, c), lambda h, j, core_ref: (0, h * nth + j, 0))
    part = pl.BlockSpec((tr, c), lambda h, j, core_ref: (j, 0))
    shape = jax.ShapeDtypeStruct((1, r, c), F32)
    return pl.pallas_call(
        body, name=name,
        grid_spec=pltpu.PrefetchScalarGridSpec(
            num_scalar_prefetch=1, grid=(2, nth), in_specs=[full, part, part, full, full], out_specs=[full] * 4),
        out_shape=[shape] * 4, compiler_params=_params("parallel", "parallel"),
    )(core, w, g_mine, g_theirs, m, v)


def _rel_bias_table(name, rel_bias):
    padded = jnp.pad(rel_bias, ((0, 0), (0, N_REL_PAD - N_REL)))

    def body(rb_ref, o_ref):
        ridx = lax.broadcasted_iota(jnp.int32, (N_REL_PAD, BAND), 0)
        sidx = lax.broadcasted_iota(jnp.int32, (N_REL_PAD, BAND), 1)
        rb = rb_ref[...]

        def step(tq, carry):
            rel = jnp.clip(tq + KPAD - sidx, -REL_CLIP, REL_CLIP) + REL_CLIP
            onehot = jnp.where(rel == ridx, 1.0, 0.0).astype(BF16)
            o_ref[tq] = _dot_exact_rhs(rb, onehot)
            return carry

        lax.fori_loop(0, CHUNK, step, 0)

    vm = pl.BlockSpec(memory_space=pltpu.VMEM)
    table = pl.pallas_call(
        body, name=name, in_specs=[vm], out_specs=vm,
        out_shape=jax.ShapeDtypeStruct((CHUNK, ATTN_HEADS, BAND), F32),
    )(padded)
    return table.transpose(1, 0, 2)


def _adamw_small(name, w, parts, m, v):
    def body(w_ref, p_ref, m_ref, v_ref, g_ref, d_ref, m2_ref, v2_ref):
        g = p_ref[0]
        for i in range(1, N_DEV):
            g = g + p_ref[i]
        delta, m2, v2 = _adam_update(w_ref[...], g, m_ref[...], v_ref[...])
        g_ref[...] = g
        d_ref[...] = delta
        m2_ref[...] = m2
        v2_ref[...] = v2

    vm = pl.BlockSpec(memory_space=pltpu.VMEM)
    shape = jax.ShapeDtypeStruct((SMALL_ROWS, SMALL_COLS), F32)
    return pl.pallas_call(
        body, name=name, in_specs=[vm] * 4, out_specs=[vm] * 4, out_shape=[shape] * 4,
    )(w, parts, m, v)


def _position():
    return lax.axis_index("x"), lax.axis_index("y"), lax.axis_index("c")


def _other_chips(x, y):
    return [(1 - x, y), (x, 1 - y), (1 - x, 1 - y)]


ANY = pl.BlockSpec(memory_space=pl.ANY)
PAIR_ID = 0


def _pair_handshake():
    x, y, c = _position()
    barrier = pltpu.get_barrier_semaphore()
    pl.semaphore_signal(barrier, inc=1, device_id=(x, y, 1 - c), device_id_type=MESH)
    pl.semaphore_wait(barrier, 1)


PAIR_CALL = pltpu.CompilerParams(collective_id=PAIR_ID)


HBM = pl.BlockSpec(memory_space=pltpu.HBM)
SEM = pl.BlockSpec(memory_space=pltpu.SEMAPHORE)
SPLIT_COPY = pltpu.SideEffectType.DATAFLOW_SIDE_EFFECTING


def _gather_copy(shards, outs, send_sem, recv_sem, i, j):
    x, y, c = _position()
    chips = _other_chips(x, y)
    half = shards[i].shape[0] // 2
    rows = pl.ds(pl.multiple_of(c * half, 16), half)
    return pltpu.make_async_remote_copy(
        src_ref=shards[i].at[rows, :], dst_ref=outs[i].at[2 * x + y, rows, :],
        send_sem=send_sem.at[3 * i + j], recv_sem=recv_sem.at[3 * i + j],
        device_id=(chips[j][0], chips[j][1], c), device_id_type=MESH)


def _gather_start(name, shards, after):
    n = len(shards)

    def body(*refs):
        srcs, outs = refs[:n], refs[n:2 * n]
        send_sem, recv_sem = refs[2 * n + len(after)], refs[2 * n + len(after) + 1]
        token = refs[-1]
        for i in range(n):
            for j in range(3):
                _gather_copy(srcs, outs, send_sem, recv_sem, i, j).start()
        token[...] = jnp.zeros_like(token)

    full = [(N_CHIPS,) + s.shape for s in shards]
    res = pl.pallas_call(
        body,
        name=name,
        in_specs=[HBM] * (2 * n) + [ANY] * len(after),
        out_specs=[SEM, SEM] + [HBM] * (2 * n) + [pl.BlockSpec(memory_space=pltpu.VMEM)],
        out_shape=[pltpu.SemaphoreType.DMA((3 * n,)), pltpu.SemaphoreType.DMA((3 * n,))]
        + [pltpu.HBM(s.shape, s.dtype) for s in shards]
        + [pltpu.HBM(shp, s.dtype) for shp, s in zip(full, shards)]
        + [jax.ShapeDtypeStruct((8, LANES), F32)],
        input_output_aliases={i: 2 + i for i in range(2 * n)},
        compiler_params=pltpu.CompilerParams(has_side_effects=SPLIT_COPY),
    )(*[pltpu.with_memory_space_constraint(s, pltpu.HBM) for s in shards],
      *[pltpu.with_memory_space_constraint(lax.empty(shp, s.dtype), pltpu.HBM) for shp, s in zip(full, shards)],
      *after)
    return res[0], res[1], list(res[2:2 + n]), list(res[2 + n:2 + 2 * n]), res[-1]


def _gather_wait(name, send_sem, recv_sem, shards, outs, after):
    n = len(shards)

    def body(*refs):
        srcs, out_refs = refs[:n], refs[n:2 * n]
        send_ref, recv_ref = refs[2 * n], refs[2 * n + 1]
        for i in range(n):
            for j in range(3):
                copy = _gather_copy(srcs, out_refs, send_ref, recv_ref, i, j)
                copy.wait_send()
                copy.wait_recv()

    res = pl.pallas_call(
        body,
        name=name,
        in_specs=[HBM] * (2 * n) + [SEM, SEM] + [ANY] * len(after),
        out_specs=[HBM] * (2 * n),
        out_shape=[pltpu.HBM(s.shape, s.dtype) for s in shards] + [pltpu.HBM(o.shape, o.dtype) for o in outs],
        input_output_aliases={i: i for i in range(2 * n)},
        compiler_params=pltpu.CompilerParams(has_side_effects=SPLIT_COPY),
    )(*shards, *outs, send_sem, recv_sem, *after)
    return list(res[:n]), list(res[n:])


def _join_copies(srcs, ins, outs, own_send, own_recv, half_send, half_recv):
    x, y, c = _position()
    chips = _other_chips(x, y)
    copies = []
    for i in range(len(srcs)):
        copies.append(pltpu.make_async_remote_copy(
            src_ref=srcs[i], dst_ref=outs[i].at[2 * x + y], send_sem=own_send.at[i], recv_sem=own_recv.at[i],
            device_id=(x, y, 1 - c), device_id_type=MESH))
        half = srcs[i].shape[0] // 2
        rows = pl.ds(pl.multiple_of(c * half, 16), half)
        for j in range(3):
            slot = 2 * chips[j][0] + chips[j][1]
            copies.append(pltpu.make_async_remote_copy(
                src_ref=ins[i].at[slot, rows, :], dst_ref=outs[i].at[slot, rows, :],
                send_sem=half_send.at[3 * i + j], recv_sem=half_recv.at[3 * i + j],
                device_id=(x, y, 1 - c), device_id_type=MESH))
    return copies


def _gather_join(name, shards, outs):
    n = len(shards)

    def body(*refs):
        _pair_handshake()
        copies = _join_copies(refs[:n], refs[n:2 * n], refs[2 * n:3 * n], *refs[3 * n:])
        for cp in copies:
            cp.start()
        for cp in copies:
            cp.wait()

    return pl.pallas_call(
        body,
        name=name,
        in_specs=[ANY] * (2 * n),
        out_specs=[HBM] * n,
        out_shape=[pltpu.HBM(o.shape, o.dtype) for o in outs],
        input_output_aliases={n + i: i for i in range(n)},
        scratch_shapes=[pltpu.SemaphoreType.DMA((n,))] * 2 + [pltpu.SemaphoreType.DMA((3 * n,))] * 2,
        compiler_params=PAIR_CALL,
    )(*shards, *outs)


def _join_start(name, shards, outs):
    n = len(shards)

    def body(*refs):
        _pair_handshake()
        srcs, arrs = refs[:n], refs[n:2 * n]
        sems = refs[2 * n:2 * n + 4]
        token = refs[-1]
        for cp in _join_copies(srcs, arrs, arrs, *sems):
            cp.start()
        token[...] = jnp.zeros_like(token)

    res = pl.pallas_call(
        body,
        name=name,
        in_specs=[HBM] * (2 * n),
        out_specs=[SEM] * 4 + [HBM] * (2 * n) + [pl.BlockSpec(memory_space=pltpu.VMEM)],
        out_shape=[pltpu.SemaphoreType.DMA((n,))] * 2 + [pltpu.SemaphoreType.DMA((3 * n,))] * 2
        + [pltpu.HBM(s.shape, s.dtype) for s in shards] + [pltpu.HBM(o.shape, o.dtype) for o in outs]
        + [jax.ShapeDtypeStruct((8, LANES), F32)],
        input_output_aliases={i: 4 + i for i in range(2 * n)},
        compiler_params=pltpu.CompilerParams(has_side_effects=SPLIT_COPY, collective_id=PAIR_ID),
    )(*shards, *outs)
    return list(res[:4]), list(res[4:4 + n]), list(res[4 + n:4 + 2 * n]), res[-1]


def _join_wait(name, sems, shards, outs, after):
    n = len(shards)

    def body(*refs):
        srcs, arrs = refs[:n], refs[n:2 * n]
        for cp in _join_copies(srcs, arrs, arrs, *refs[2 * n:2 * n + 4]):
            cp.wait_send()
            cp.wait_recv()

    res = pl.pallas_call(
        body,
        name=name,
        in_specs=[HBM] * (2 * n) + [SEM] * 4 + [ANY] * len(after),
        out_specs=[HBM] * (2 * n),
        out_shape=[pltpu.HBM(s.shape, s.dtype) for s in shards] + [pltpu.HBM(o.shape, o.dtype) for o in outs],
        input_output_aliases={i: i for i in range(2 * n)},
        compiler_params=pltpu.CompilerParams(has_side_effects=SPLIT_COPY),
    )(*shards, *outs, *sems, *after)
    return list(res[n:])


def _pair_copy(grads, lands, send_sem, recv_sem, i):
    x, y, c = _position()
    half = grads[i].shape[1] // 2
    give = pl.ds(pl.multiple_of((1 - c) * half, 16), half)
    return pltpu.make_async_remote_copy(
        src_ref=grads[i].at[:, give, :], dst_ref=lands[i], send_sem=send_sem.at[i], recv_sem=recv_sem.at[i],
        device_id=(x, y, 1 - c), device_id_type=MESH)


def _pair_start(name, grads):
    n = len(grads)

    def body(*refs):
        _pair_handshake()
        srcs, lands = refs[:n], refs[n:2 * n]
        send_sem, recv_sem = refs[2 * n], refs[2 * n + 1]
        token = refs[-1]
        for i in range(n):
            _pair_copy(srcs, lands, send_sem, recv_sem, i).start()
        token[...] = jnp.zeros_like(token)

    halves = [(g.shape[0], g.shape[1] // 2, g.shape[2]) for g in grads]
    res = pl.pallas_call(
        body,
        name=name,
        in_specs=[HBM] * (2 * n),
        out_specs=[SEM, SEM] + [HBM] * (2 * n) + [pl.BlockSpec(memory_space=pltpu.VMEM)],
        out_shape=[pltpu.SemaphoreType.DMA((n,)), pltpu.SemaphoreType.DMA((n,))]
        + [pltpu.HBM(g.shape, g.dtype) for g in grads]
        + [pltpu.HBM(shp, g.dtype) for shp, g in zip(halves, grads)]
        + [jax.ShapeDtypeStruct((8, LANES), F32)],
        input_output_aliases={i: 2 + i for i in range(2 * n)},
        compiler_params=pltpu.CompilerParams(has_side_effects=SPLIT_COPY, collective_id=PAIR_ID),
    )(*[pltpu.with_memory_space_constraint(g, pltpu.HBM) for g in grads],
      *[pltpu.with_memory_space_constraint(lax.empty(shp, g.dtype), pltpu.HBM) for shp, g in zip(halves, grads)])
    return res[0], res[1], list(res[2:2 + n]), list(res[2 + n:2 + 2 * n]), res[-1]


def _pair_wait(name, send_sem, recv_sem, grads, lands, after):
    n = len(grads)

    def body(*refs):
        srcs, land_refs = refs[:n], refs[n:2 * n]
        send_ref, recv_ref = refs[2 * n], refs[2 * n + 1]
        for i in range(n):
            copy = _pair_copy(srcs, land_refs, send_ref, recv_ref, i)
            copy.wait_send()
            copy.wait_recv()

    res = pl.pallas_call(
        body,
        name=name,
        in_specs=[HBM] * (2 * n) + [SEM, SEM, ANY],
        out_specs=[HBM] * (2 * n),
        out_shape=[pltpu.HBM(g.shape, g.dtype) for g in grads] + [pltpu.HBM(l.shape, l.dtype) for l in lands],
        input_output_aliases={i: i for i in range(2 * n)},
        compiler_params=pltpu.CompilerParams(has_side_effects=SPLIT_COPY),
    )(*grads, *lands, send_sem, recv_sem, after)
    return list(res[:n]), list(res[n:])


def _scatter_copy(srcs, lands, send_sem, recv_sem, i, j):
    x, y, c = _position()
    chips = _other_chips(x, y)
    return pltpu.make_async_remote_copy(
        src_ref=srcs[i].at[2 * chips[j][0] + chips[j][1]], dst_ref=lands[i].at[j],
        send_sem=send_sem.at[3 * i + j], recv_sem=recv_sem.at[3 * i + j],
        device_id=(chips[j][0], chips[j][1], c), device_id_type=MESH)


def _scatter_start(name, sums):
    n = len(sums)

    def body(*refs):
        srcs, lands = refs[:n], refs[n:2 * n]
        send_sem, recv_sem = refs[2 * n], refs[2 * n + 1]
        token = refs[-1]
        for i in range(n):
            for j in range(3):
                _scatter_copy(srcs, lands, send_sem, recv_sem, i, j).start()
        token[...] = jnp.zeros_like(token)

    land_shapes = [(3,) + s.shape[1:] for s in sums]
    res = pl.pallas_call(
        body,
        name=name,
        in_specs=[HBM] * (2 * n),
        out_specs=[SEM, SEM] + [HBM] * (2 * n) + [pl.BlockSpec(memory_space=pltpu.VMEM)],
        out_shape=[pltpu.SemaphoreType.DMA((3 * n,)), pltpu.SemaphoreType.DMA((3 * n,))]
        + [pltpu.HBM(s.shape, s.dtype) for s in sums]
        + [pltpu.HBM(shp, s.dtype) for shp, s in zip(land_shapes, sums)]
        + [jax.ShapeDtypeStruct((8, LANES), F32)],
        input_output_aliases={i: 2 + i for i in range(2 * n)},
        compiler_params=pltpu.CompilerParams(has_side_effects=SPLIT_COPY),
    )(*[pltpu.with_memory_space_constraint(s, pltpu.HBM) for s in sums],
      *[pltpu.with_memory_space_constraint(lax.empty(shp, s.dtype), pltpu.HBM) for shp, s in zip(land_shapes, sums)])
    return res[0], res[1], list(res[2:2 + n]), list(res[2 + n:2 + 2 * n]), res[-1]


def _scatter_wait(name, send_sem, recv_sem, sums, lands, after):
    n = len(sums)

    def body(*refs):
        srcs, land_refs = refs[:n], refs[n:2 * n]
        send_ref, recv_ref = refs[2 * n], refs[2 * n + 1]
        for i in range(n):
            for j in range(3):
                copy = _scatter_copy(srcs, land_refs, send_ref, recv_ref, i, j)
                copy.wait_send()
                copy.wait_recv()

    res = pl.pallas_call(
        body,
        name=name,
        in_specs=[HBM] * (2 * n) + [SEM, SEM, ANY],
        out_specs=[HBM] * (2 * n),
        out_shape=[pltpu.HBM(s.shape, s.dtype) for s in sums] + [pltpu.HBM(l.shape, l.dtype) for l in lands],
        input_output_aliases={i: i for i in range(2 * n)},
        compiler_params=pltpu.CompilerParams(has_side_effects=SPLIT_COPY),
    )(*sums, *lands, send_sem, recv_sem, after)
    return list(res[:n]), list(res[n:])


def _pair_join(name, halves, small=None):
    n = len(halves)
    if small is None:
        def body_plain(*refs):
            _pair_handshake()
            ins, outs = refs[:n], refs[n:2 * n]
            send_sem, recv_sem = refs[2 * n:]
            x, y, c = _position()
            swaps = [pltpu.make_async_remote_copy(
                src_ref=ins[i], dst_ref=outs[i], send_sem=send_sem.at[i], recv_sem=recv_sem.at[i],
                device_id=(x, y, 1 - c), device_id_type=MESH) for i in range(n)]
            for swap in swaps:
                swap.start()
            for swap in swaps:
                swap.wait()

        return pl.pallas_call(
            body_plain,
            name=name,
            in_specs=[ANY] * n,
            out_specs=[ANY] * n,
            out_shape=[jax.ShapeDtypeStruct(h.shape, h.dtype) for h in halves],
            scratch_shapes=[pltpu.SemaphoreType.DMA((n,))] * 2,
            compiler_params=PAIR_CALL,
        )(*halves)

    def body(*refs):
        ins, small_ref = refs[:n], refs[n]
        outs, all_ref = refs[n + 1:2 * n + 1], refs[2 * n + 1]
        send_sem, recv_sem, sm_send, sm_recv, sm_local = refs[2 * n + 2:]
        x, y, c = _position()
        swaps = []
        for i in range(n):
            swap = pltpu.make_async_remote_copy(
                src_ref=ins[i], dst_ref=outs[i], send_sem=send_sem.at[i], recv_sem=recv_sem.at[i],
                device_id=(x, y, 1 - c), device_id_type=MESH)
            swap.start()
            swaps.append(swap)
        me = 4 * x + 2 * y + c
        sm_own = pltpu.make_async_copy(small_ref, all_ref.at[me], sm_local)
        sm_own.start()
        pushes, arrivals = [], []
        for mask in range(1, N_DEV):
            px, py, pc = x ^ (mask >> 2), y ^ ((mask >> 1) & 1), c ^ (mask & 1)
            pushes.append(pltpu.make_async_remote_copy(
                src_ref=small_ref, dst_ref=all_ref.at[me], send_sem=sm_send.at[mask - 1], recv_sem=sm_recv.at[mask - 1],
                device_id=(px, py, pc), device_id_type=MESH))
            arrivals.append(pltpu.make_async_remote_copy(
                src_ref=small_ref, dst_ref=all_ref.at[4 * px + 2 * py + pc], send_sem=sm_send.at[mask - 1],
                recv_sem=sm_recv.at[mask - 1], device_id=(px, py, pc), device_id_type=MESH))
        for cp in pushes:
            cp.start()
        for swap in swaps:
            swap.wait()
        for cp in arrivals:
            cp.wait_recv()
        for cp in pushes:
            cp.wait_send()
        sm_own.wait()

    res = pl.pallas_call(
        body,
        name=name,
        in_specs=[ANY] * (n + 1),
        out_specs=[ANY] * (n + 1),
        out_shape=[jax.ShapeDtypeStruct(h.shape, h.dtype) for h in halves]
        + [jax.ShapeDtypeStruct((N_DEV,) + small.shape, small.dtype)],
        scratch_shapes=[pltpu.SemaphoreType.DMA((n,))] * 2 + [pltpu.SemaphoreType.DMA((N_DEV - 1,))] * 2
        + [pltpu.SemaphoreType.DMA(())],
    )(*halves, small)
    return res[:n], res[n]


def _lower_bound(lbp):
    return jax.nn.softmax(lbp, axis=0)[0:1]


def _local_step(x, target, g1, gm, g2, gq, gk, go, rel_bias, lbp, weights, on_grads, grads_sent):
    b, s, d = x.shape
    t = b * s
    x0 = x.reshape(t, d)
    tgt = target.reshape(t, d)
    gq_t = jnp.tile(gq, (1, ATTN_HEADS))
    gk_t = jnp.tile(gk, (1, ATTN_HEADS))
    lb = _lower_bound(lbp)
    table = _band_table(_rel_bias_table("rel_bias_table", rel_bias))

    h1 = _rmsnorm_fwd("norm1", x0, g1)
    wg1, wu1, deps1 = weights["first"]((h1, table))
    a1, b1, z1 = _ffn_up("ffn1_up", h1, wg1, wu1, deps1)
    wd1, deps_mid = weights["mid"]((z1,))
    x1, h2 = _ffn_down("ffn1_down", z1, wd1, x0, gm, deps_mid)
    w_in, w_out = weights["mid_rest"]((x1,))
    ns = w_in.shape[0]
    proj = _in_proj("in_proj", h2, w_in)
    proj3 = proj.reshape(b, s, proj.shape[1])
    qn, kn, vb = _qk_prep("qk_prep", proj3, gq_t, gk_t)
    attn = _attn_fwd("attn_fwd", qn, kn, vb, table, weights["last_begin"]((qn,))).reshape(t, ATTN_W)
    mix, oraw, states = _hgrn_fwd("hgrn_fwd", proj, attn, lb, go, b, s)
    x2, h3 = _out_proj("out_proj", mix, w_out, x1, g2)
    wg2, wu2, wd2 = weights["last"]((h3,))
    a2, b2, z2 = _ffn_up("ffn2_up", h3, wg2, wu2)
    dy, dyh, sq = _ffn_down_loss("ffn2_down_loss", z2, wd2, x2, tgt)
    loss = 0.5 * jnp.sum(sq) / d

    da2, db2 = _ffn_bwd_act("ffn2_bwd_act", dyh, wd2, a2, b2)
    dwd2 = _grad_w_cols("ffn2_dwd", z2, dyh)
    dwg2 = _grad_w_cols("ffn2_dwg", da2, h3)
    dwu2 = _grad_w_cols("ffn2_dwu", db2, h3)
    sent2 = on_grads("ffn2", {"ffn2_w_gate": dwg2, "ffn2_w_up": dwu2, "ffn2_w_down": dwd2})
    dx2, dx2b, dg2 = _ffn_bwd_in("ffn2_bwd_in", da2, db2, wg2, wu2, x2, g2, dy, 1.0, sent2)
    sent2 = grads_sent("ffn2", dx2b)

    dwout = _grad_w_out("dw_out", mix, dx2b)
    dmix = _out_proj_bwd("out_proj_bwd", dx2b, w_out, sent2)
    dqn, dkn, dvn, dbe, dbo = _attn_bwd("attn_bwd", qn, kn, vb, table, dmix.reshape(b, s, dmix.shape[1]))
    dbias = dbe[:, :, :BAND] + dbo[:, :, CHUNK:]
    dpq, dpk, dpv, dgq, dgk = _qk_prep_bwd("qk_prep_bwd", proj3, dqn, dkn, dvn, gq_t, gk_t)
    dpq, dpk, dpv = (a.reshape(t, ATTN_W) for a in (dpq, dpk, dpv))
    dproj, dlb, dgo = _hgrn_bwd("hgrn_bwd", proj, (dpq, dpk, dpv), lb, go, oraw, states, dmix, b, s)
    dwin = _grad_w_in("dw_in", h2, dproj, ns)
    dx1, dx1h, dgm = _in_proj_bwd("in_proj_bwd", dproj, w_in, x1, gm, dx2, 0.5)

    dwd1 = _grad_w_cols("ffn1_dwd", z1, dx1h)
    sent_mix = on_grads("mix", {"w_in": dwin, "w_out": dwout.reshape(ns, dwout.shape[0] // ns, d),
                                "ffn1_w_down": dwd1})
    da1, db1 = _ffn_bwd_act("ffn1_bwd_act", dx1h, wd1, a1, b1, sent_mix)
    sent_mix = grads_sent("mix", da1)
    dwg1 = _grad_w_cols("ffn1_dwg", da1, h1, sent_mix)
    dwu1 = _grad_w_cols("ffn1_dwu", db1, h1)
    on_grads("ffn1", {"ffn1_w_gate": dwg1, "ffn1_w_up": dwu1})
    sent1 = grads_sent("ffn1", None)
    dx0, dg1 = _ffn_bwd_in("ffn1_bwd_in", da1, db1, wg1, wu1, x0, g1, dx1, None, sent1)

    nt = dg1.shape[0]
    sg = _small_grads(
        "small_grads", dg1.reshape(nt, d), dgm.reshape(nt, d), dg2.reshape(nt, d),
        dgq.reshape(-1, ATTN_W), dgk.reshape(-1, ATTN_W), dbias.transpose(1, 0, 2),
        dlb.reshape(b, HGRN_W), dgo.reshape(b, HGRN_W), lbp)
    g1g, gmg, g2g, gqg, gkg, rbg, lbg, gog = sg
    small = _pack_small(g1g, gmg, g2g, lbg, rbg[:, :N_REL], gqg, gkg, gog, loss)
    return dx0.reshape(b, s, d), small


LOSS_SLOT = 7 * SMALL_COLS + 2 * ATTN_DH + HGRN_DH


def _pack_small(g1, gm, g2, lbp, rel_bias, gq, gk, go, loss=None):
    flat = [g1.reshape(-1), gm.reshape(-1), g2.reshape(-1), lbp.reshape(-1), rel_bias.reshape(-1)]
    n_bias = 3 * SMALL_COLS - rel_bias.size
    heads = [gq.reshape(-1), gk.reshape(-1), go.reshape(-1)]
    heads.append(jnp.zeros((1,), F32) if loss is None else loss.reshape(1))
    n_tail = SMALL_COLS - sum(h.size for h in heads)
    return jnp.concatenate(flat + [jnp.zeros((n_bias,), F32)] + heads + [jnp.zeros((n_tail,), F32)]).reshape(
        SMALL_ROWS, SMALL_COLS)


def _unpack_small(p, d):
    flat = p.reshape(-1)
    o = 3 * d
    g1, gm, g2 = p[0:1], p[1:2], p[2:3]
    lbp = flat[o:o + 2 * HGRN_W].reshape(2, HGRN_W)
    o = 4 * SMALL_COLS
    rel = flat[o:o + ATTN_HEADS * N_REL].reshape(1, ATTN_HEADS, N_REL)
    o = 7 * SMALL_COLS
    gq = flat[o:o + ATTN_DH].reshape(1, ATTN_DH)
    gk = flat[o + ATTN_DH:o + 2 * ATTN_DH].reshape(1, ATTN_DH)
    go = flat[o + 2 * ATTN_DH:o + 2 * ATTN_DH + HGRN_DH].reshape(1, HGRN_DH)
    return g1, gm, g2, gq, gk, rel, lbp, go


def kernel(x, ffn1_norm_g, ffn1_w_gate, ffn1_w_up, ffn1_w_down, mix_norm_g, w_in, attn_q_norm_g, attn_k_norm_g, attn_rel_bias, hgrn_lower_bounds, hgrn_out_norm_g, w_out, ffn2_norm_g, ffn2_w_gate, ffn2_w_up, ffn2_w_down, loss_target, m_ffn1_norm_g, m_ffn1_w_gate, m_ffn1_w_up, m_ffn1_w_down, m_mix_norm_g, m_w_in, m_attn_q_norm_g, m_attn_k_norm_g, m_attn_rel_bias, m_hgrn_lower_bounds, m_hgrn_out_norm_g, m_w_out, m_ffn2_norm_g, m_ffn2_w_gate, m_ffn2_w_up, m_ffn2_w_down, v_ffn1_norm_g, v_ffn1_w_gate, v_ffn1_w_up, v_ffn1_w_down, v_mix_norm_g, v_w_in, v_attn_q_norm_g, v_attn_k_norm_g, v_attn_rel_bias, v_hgrn_lower_bounds, v_hgrn_out_norm_g, v_w_out, v_ffn2_norm_g, v_ffn2_w_gate, v_ffn2_w_up, v_ffn2_w_down):
    d = x.shape[-1]
    big_w = [ffn1_w_gate, ffn1_w_up, ffn1_w_down, w_in, w_out, ffn2_w_gate, ffn2_w_up, ffn2_w_down]
    big_m = [m_ffn1_w_gate, m_ffn1_w_up, m_ffn1_w_down, m_w_in, m_w_out, m_ffn2_w_gate, m_ffn2_w_up, m_ffn2_w_down]
    big_v = [v_ffn1_w_gate, v_ffn1_w_up, v_ffn1_w_down, v_w_in, v_w_out, v_ffn2_w_gate, v_ffn2_w_up, v_ffn2_w_down]
    big_names = ["ffn1_w_gate", "ffn1_w_up", "ffn1_w_down", "w_in", "w_out", "ffn2_w_gate", "ffn2_w_up", "ffn2_w_down"]
    flipped = {nm for nm in big_names if nm.endswith("gate") or nm.endswith("up")}
    flip = lambda nm, a: jnp.swapaxes(a, 1, 2) if nm in flipped else a
    big_w, big_m, big_v = ([flip(nm, a) for nm, a in zip(big_names, arrs)] for arrs in (big_w, big_m, big_v))

    shards = [w[0].astype(BF16) for w in big_w]
    start_a = _gather_start("gather_start_up1", shards[:2], ())
    start_b = _gather_start("gather_start_mid", shards[2:5], (start_a[4],))
    start_c = _gather_start("gather_start_ffn2", shards[5:], (start_b[4],))

    pending = {}

    def arrived(tag, started, after):
        send_sem, recv_sem, srcs, outs, _ = started
        return _gather_wait("gather_wait_" + tag, send_sem, recv_sem, srcs, outs, after)

    def first_weights(after):
        return (*_gather_join("gather_join_up1", *arrived("up1", start_a, after)), (start_c[4],))

    def mid_weights(after):
        srcs, outs = arrived("mid", start_b, after)
        (wd1,) = _gather_join("gather_join_wd1", srcs[:1], outs[:1])
        pending["mid"] = _join_start("join_start_mid", srcs[1:], outs[1:])
        return wd1, (pending["mid"][3],)

    def mid_rest(after):
        sems, srcs, outs, _ = pending["mid"]
        win_f, wout_f = _join_wait("join_wait_mid", sems, srcs, outs, after)
        return win_f, wout_f.reshape(wout_f.shape[0] * wout_f.shape[1], d)

    def last_begin(after):
        pending["ffn2"] = _join_start("join_start_ffn2", *arrived("ffn2", start_c, after))
        return (pending["ffn2"][3],)

    def last_weights(after):
        sems, srcs, outs, _ = pending["ffn2"]
        return _join_wait("join_wait_ffn2", sems, srcs, outs, after)

    weights = {"first": first_weights, "mid": mid_weights, "mid_rest": mid_rest, "last_begin": last_begin,
               "last": last_weights}

    core = lax.axis_index("c").astype(jnp.int32).reshape(1)
    chip = (2 * lax.axis_index("x") + lax.axis_index("y")).astype(jnp.int32).reshape(1)
    started = {}

    def on_grads(tag, grads):
        names = list(grads)
        started[tag] = (names, _pair_start("pair_start_" + tag, [grads[nm] for nm in names]))
        return (started[tag][1][4],)

    def grads_sent(tag, after):
        names, (send_sem, recv_sem, grads, lands, token) = started[tag]
        grads, theirs = _pair_wait("pair_wait_" + tag, send_sem, recv_sem, grads, lands, token if after is None else after)
        sums = [_pair_sum("pair_sum_" + nm, g, th, core) for nm, g, th in zip(names, grads, theirs)]
        started[tag] = (names, _scatter_start("scatter_start_" + tag, sums))
        return (started[tag][1][4],)

    grad_x, small_g = _local_step(
        x, loss_target, ffn1_norm_g, mix_norm_g, ffn2_norm_g, attn_q_norm_g, attn_k_norm_g, hgrn_out_norm_g,
        attn_rel_bias[0], hgrn_lower_bounds, weights, on_grads, grads_sent)

    def finish(tag, after):
        names, (send_sem, recv_sem, sums, lands, _) = started[tag]
        sums, lands = _scatter_wait("scatter_wait_" + tag, send_sem, recv_sem, sums, lands, after)
        return names, [_chip_sum("chip_sum_" + nm, sm, ld, chip) for nm, sm, ld in zip(names, sums, lands)]

    by_name = {nm: (w, m, v) for nm, w, m, v in zip(big_names, big_w, big_m, big_v)}
    updated = {}

    def update(names, halves, other_halves):
        for nm, mine, theirs in zip(names, halves, other_halves):
            w, m, v = by_name[nm]
            updated[nm] = _adamw("adamw_" + nm, w, mine, theirs, m, v, core)

    last_token = started["ffn1"][1][4]
    names_a, halves_a = finish("ffn2", last_token)
    names_m, halves_m = finish("mix", last_token)
    names_a, halves_a = names_a + names_m, halves_a + halves_m
    update(names_a, halves_a, _pair_join("pair_join_early", halves_a))
    names_b, halves_b = finish("ffn1", updated[names_a[-1]][1])
    others_b, small_all = _pair_join("pair_join_last", halves_b, small_g)
    update(names_b, halves_b, others_b)
    big_out = [updated[nm] for nm in big_names]

    pack = lambda g1, gm, g2, gq, gk, rel, lbp, go: _pack_small(g1, gm, g2, lbp, rel[0], gq, gk, go)
    small_w = pack(ffn1_norm_g, mix_norm_g, ffn2_norm_g, attn_q_norm_g, attn_k_norm_g, attn_rel_bias, hgrn_lower_bounds, hgrn_out_norm_g)
    small_m = pack(m_ffn1_norm_g, m_mix_norm_g, m_ffn2_norm_g, m_attn_q_norm_g, m_attn_k_norm_g, m_attn_rel_bias, m_hgrn_lower_bounds, m_hgrn_out_norm_g)
    small_v = pack(v_ffn1_norm_g, v_mix_norm_g, v_ffn2_norm_g, v_attn_q_norm_g, v_attn_k_norm_g, v_attn_rel_bias, v_hgrn_lower_bounds, v_hgrn_out_norm_g)
    small_res = _adamw_small("adamw_small", small_w, small_all, small_m, small_v)
    small_out = [_unpack_small(p, d) for p in small_res]
    loss = small_res[0].reshape(-1)[LOSS_SLOT]

    def assemble(kind):
        bg = [flip(nm, o[kind]) for nm, o in zip(big_names, big_out)]
        g1, gm, g2, gq, gk, rel, lbp, go = small_out[kind]
        return [g1, bg[0], bg[1], bg[2], gm, bg[3], gq, gk, rel, lbp, go, bg[4], g2, bg[5], bg[6], bg[7]]

    return (loss, grad_x, *assemble(0), *assemble(1), *assemble(2), *assemble(3))
```

```python
import functools

import jax
import jax.numpy as jnp
from jax import lax
from jax.experimental import pallas as pl
from jax.experimental.pallas import tpu as pltpu

F32 = jnp.float32
BF16 = jnp.bfloat16
MESH = pl.DeviceIdType.MESH

N_CHIPS = 4
N_DEV = 8
CHUNK = 64
ATTN_HEADS = 8
ATTN_DH = 64
ATTN_W = ATTN_HEADS * ATTN_DH
HGRN_HEADS = 4
HGRN_DH = 128
HGRN_W = HGRN_HEADS * HGRN_DH
LEFT_CHUNKS = 8
BAND = (LEFT_CHUNKS + 1) * CHUNK
KPAD = LEFT_CHUNKS * CHUNK
REL_CLIP = 128
N_REL = 2 * REL_CLIP + 1
N_REL_PAD = 384
RMS_EPS = 1e-6
LANES = 128
SMALL_ROWS = 8
SMALL_COLS = 1024

ADAM_LR = 0.001
ADAM_B1 = 0.9
ADAM_B2 = 0.999
ADAM_EPS = 1e-08
ADAM_WD = 0.01
ADAM_STEP = 10

NN = (((1,), (0,)), ((), ()))
NT = (((1,), (1,)), ((), ()))
TN = (((0,), (0,)), ((), ()))

VMEM_LIMIT = 48 * 1024 * 1024
MXU_WIDTH = 256
COL_CHUNK = 3 * MXU_WIDTH


def _sigmoid(x):
    return 1.0 / (1.0 + jnp.exp(-x))


def _silu(x):
    return x * _sigmoid(x)


def _dot(a, b, dims=NN):
    return lax.dot_general(a, b, dims, preferred_element_type=F32)


def _split3(x):
    hi = x.astype(BF16)
    r1 = x - hi.astype(F32)
    mid = r1.astype(BF16)
    lo = (r1 - mid.astype(F32)).astype(BF16)
    return hi, mid, lo


def _dot_exact_rhs(x, mat, dims=NN, pieces=3):
    hi, mid, lo = _split3(x)
    out = _dot(hi, mat, dims) + _dot(mid, mat, dims)
    return out + _dot(lo, mat, dims) if pieces == 3 else out


def _dot_exact_lhs(mat, x, dims=NN):
    hi, mid, lo = _split3(x)
    return _dot(mat, hi, dims) + _dot(mat, mid, dims) + _dot(mat, lo, dims)


def _params(*sem):
    return pltpu.CompilerParams(dimension_semantics=sem, vmem_limit_bytes=VMEM_LIMIT)


def _mm(name, ins, terms, n_acc, grid, acc_shape, outs, epilogue, extras=(), deps=()):
    nk = grid[2]
    ni, ne, nd, no = len(ins), len(extras), len(deps), len(outs)

    def body(*refs):
        in_refs = refs[:ni]
        ex_refs = refs[ni:ni + ne]
        out_refs = refs[ni + ne + nd:ni + ne + nd + no]
        acc_refs = refs[ni + ne + nd + no:]

        def products():
            parts = [None] * n_acc
            for ai, li, ri, dims in terms:
                d = _dot(in_refs[li][...], in_refs[ri][...], dims)
                parts[ai] = d if parts[ai] is None else parts[ai] + d
            return parts

        def finish(accs):
            res = epilogue(accs, [e[...] for e in ex_refs])
            for o, r in zip(out_refs, res):
                o[...] = r.astype(o.dtype)

        if nk == 1:
            finish(products())
        else:
            k = pl.program_id(2)

            @pl.when(k == 0)
            def _():
                for a, p in zip(acc_refs, products()):
                    a[...] = p

            if nk > 2:
                @pl.when(jnp.logical_and(k > 0, k < nk - 1))
                def _():
                    for a, p in zip(acc_refs, products()):
                        a[...] += p

            @pl.when(k == nk - 1)
            def _():
                finish([a[...] + p for a, p in zip(acc_refs, products())])

    scratch = [] if nk == 1 else [pltpu.VMEM(acc_shape, F32) for _ in range(n_acc)]
    res = pl.pallas_call(
        body,
        name=name,
        grid=grid,
        in_specs=[s for _, s in ins] + [s for _, s in extras] + [pl.BlockSpec(memory_space=pl.ANY)] * nd,
        out_specs=[s for _, s in outs],
        out_shape=[o for o, _ in outs],
        scratch_shapes=scratch,
        compiler_params=_params("parallel", "parallel", "arbitrary"),
    )(*[a for a, _ in ins], *[a for a, _ in extras], *deps)
    return res


def _staged_shape(w):
    return w.shape if len(w.shape) == 2 else (w.shape[1], w.shape[0] * w.shape[2])


def _stage_weights(w_hbm, w_vmem, sem):
    @pl.when(pl.program_id(0) == 0)
    def _():
        copies = []
        for p, (h, v) in enumerate(zip(w_hbm, w_vmem)):
            if len(h.shape) == 2:
                copies.append(pltpu.make_async_copy(h, v, sem.at[p, 0]))
            else:
                pj = h.shape[2]
                copies += [pltpu.make_async_copy(h.at[j], v.at[:, pl.ds(j * pj, pj)], sem.at[p, j])
                           for j in range(h.shape[0])]
        for cp in copies:
            cp.start()
        for cp in copies:
            cp.wait()


def _staging_scratch(weights):
    return [pltpu.VMEM(_staged_shape(w), w.dtype) for w in weights] + [pltpu.SemaphoreType.DMA((len(weights), N_CHIPS))]


def _mm_rows(name, lhs, weights, dims, t, outs, epilogue, extras=(), deps=()):
    tm = _row_tile(t)
    nl, ne, nd, no = len(lhs), len(extras), len(deps), len(outs)

    def body(*refs):
        lhs_refs = refs[:nl]
        w_hbm = refs[nl:2 * nl]
        ex_refs = refs[2 * nl:2 * nl + ne]
        out_refs = refs[2 * nl + ne + nd:2 * nl + ne + nd + no]
        w_vmem = refs[2 * nl + ne + nd + no:3 * nl + ne + nd + no]
        _stage_weights(w_hbm, w_vmem, refs[-1])

        acc = None
        for p in range(nl):
            part = _dot(lhs_refs[p][...], w_vmem[p][...], dims)
            acc = part if acc is None else acc + part
        res = epilogue([acc], [e[...] for e in ex_refs])
        for o, r in zip(out_refs, res):
            o[...] = r.astype(o.dtype)

    return pl.pallas_call(
        body,
        name=name,
        grid=(t // tm,),
        in_specs=[s for _, s in lhs] + [pl.BlockSpec(memory_space=pl.ANY)] * nl + [s for _, s in extras]
        + [pl.BlockSpec(memory_space=pl.ANY)] * nd,
        out_specs=[s for _, s in outs],
        out_shape=[o for o, _ in outs],
        scratch_shapes=_staging_scratch(weights),
        compiler_params=_params("arbitrary"),
    )(*[a for a, _ in lhs], *weights, *[a for a, _ in extras], *deps)


def _col_chunks(f):
    return [(c, min(COL_CHUNK, f - c)) for c in range(0, f, COL_CHUNK)]


def _mm_cols(name, x, weights, dims, n_out, epilogue, extras=(), deps=(), out_dtype=BF16):
    t, k = x.shape
    f = _staged_shape(weights[0])[0 if dims == NT else 1]
    tm = _row_tile(t)
    chunks = _col_chunks(f)
    nw, ne, nd = len(weights), len(extras), len(deps)

    def body(*refs):
        x_ref = refs[0]
        w_hbm = refs[1:1 + nw]
        ex_refs = refs[1 + nw:1 + nw + ne]
        out_refs = refs[1 + nw + ne + nd:1 + nw + ne + nd + n_out]
        w_vmem = refs[1 + nw + ne + nd + n_out:1 + 2 * nw + ne + nd + n_out]
        _stage_weights(w_hbm, w_vmem, refs[-1])

        xv = x_ref[...]

        def dots(c):
            c0, cw = chunks[c]
            return [_dot(xv, w[c0:c0 + cw, :] if dims == NT else w[:, c0:c0 + cw], dims) for w in w_vmem]

        accs = dots(0)
        for c, (c0, cw) in enumerate(chunks):
            nxt = dots(c + 1) if c + 1 < len(chunks) else None
            res = epilogue(accs, [e[:, c0:c0 + cw] for e in ex_refs])
            for o, r in zip(out_refs, res):
                o[:, c0:c0 + cw] = r.astype(o.dtype)
            accs = nxt

    act = pl.BlockSpec((tm, f), lambda i: (i, 0))
    return pl.pallas_call(
        body,
        name=name,
        grid=(t // tm,),
        in_specs=[pl.BlockSpec((tm, k), lambda i: (i, 0))] + [pl.BlockSpec(memory_space=pl.ANY)] * nw + [act] * ne
        + [pl.BlockSpec(memory_space=pl.ANY)] * nd,
        out_specs=[act] * n_out,
        out_shape=[jax.ShapeDtypeStruct((t, f), out_dtype)] * n_out,
        scratch_shapes=_staging_scratch(weights),
        compiler_params=_params("arbitrary"),
    )(x, *weights, *extras, *deps)


def _row_tile(t):
    return 512 if t % 512 == 0 else t


def _k_tile(t):
    return t if t <= 4096 else 1024


def _grad_k_tile(t):
    return 2048 if t % 2048 == 0 else t


def _rmsnorm(xv, g):
    ms = jnp.mean(xv * xv, axis=-1, keepdims=True)
    return xv * lax.rsqrt(ms + RMS_EPS) * g


def _rmsnorm_fwd(name, x, g):
    t, d = x.shape
    tm = _row_tile(t)

    def body(x_ref, g_ref, h_ref):
        h_ref[...] = _rmsnorm(x_ref[...], g_ref[...]).astype(BF16)

    return pl.pallas_call(
        body,
        name=name,
        grid=(t // tm,),
        in_specs=[pl.BlockSpec((tm, d), lambda i: (i, 0)), pl.BlockSpec((1, d), lambda i: (0, 0))],
        out_specs=pl.BlockSpec((tm, d), lambda i: (i, 0)),
        out_shape=jax.ShapeDtypeStruct((t, d), BF16),
        compiler_params=_params("parallel"),
    )(x, g)


def _norm_bwd_epilogue(copy_scale):
    def epilogue(accs, ex):
        dh = accs[0]
        xv, g, dres = ex
        ms = jnp.mean(xv * xv, axis=-1, keepdims=True)
        rstd = lax.rsqrt(ms + RMS_EPS)
        xhat = xv * rstd
        dxhat = dh * g
        dx = rstd * (dxhat - xhat * jnp.mean(dxhat * xhat, axis=-1, keepdims=True))
        out = dres + dx
        dg = jnp.sum(dh * xhat, axis=0, keepdims=True)
        if copy_scale is None:
            return out, dg
        return out, out * copy_scale, dg

    return epilogue


def _merged(w):
    return w.reshape(-1, w.shape[-1])


def _ffn_up(name, h, wg, wu, deps=()):
    def epilogue(accs, ex):
        a, b = accs
        sg = _sigmoid(a)
        act = a * sg
        return act, b * (sg * (1.0 + a * (1.0 - sg))), act * b

    return _mm_cols(name, h, [_merged(wg), _merged(wu)], NT, 3, epilogue, deps=deps)


def _whole_rows(arr, tm):
    return arr, pl.BlockSpec((tm, arr.shape[1]), lambda i: (i, 0))


def _ffn_down(name, z, wd, x, g_next, deps=()):
    t = z.shape[0]
    d = wd.shape[2]
    tm = _row_tile(t)
    row = pl.BlockSpec((tm, d), lambda i: (i, 0))

    def epilogue(accs, ex):
        y = ex[0] + 0.5 * accs[0]
        return y, _rmsnorm(y, ex[1])

    return _mm_rows(
        name, [_whole_rows(z, tm)], [_merged(wd)], NN, t,
        outs=[(jax.ShapeDtypeStruct((t, d), F32), row), (jax.ShapeDtypeStruct((t, d), BF16), row)],
        epilogue=epilogue,
        extras=[(x, row), (g_next, pl.BlockSpec((1, d), lambda i: (0, 0)))],
        deps=deps,
    )


def _ffn_down_loss(name, z, wd, x, target):
    t = z.shape[0]
    d = wd.shape[2]
    tm = _row_tile(t)
    nt = t // tm
    row = pl.BlockSpec((tm, d), lambda i: (i, 0))

    def epilogue(accs, ex):
        e = ex[0] + 0.5 * accs[0] - ex[1]
        dy = e * (1.0 / d)
        return dy, 0.5 * dy, jnp.sum(e * e, axis=0, keepdims=True)

    return _mm_rows(
        name, [_whole_rows(z, tm)], [_merged(wd)], NN, t,
        outs=[(jax.ShapeDtypeStruct((t, d), F32), row), (jax.ShapeDtypeStruct((t, d), BF16), row),
              (jax.ShapeDtypeStruct((nt, 1, d), F32), pl.BlockSpec((None, 1, d), lambda i: (i, 0, 0)))],
        epilogue=epilogue,
        extras=[(x, row), (target, row)],
    )


def _ffn_bwd_act(name, dout, wd, act_a, dact_b, deps=()):
    def epilogue(accs, ex):
        dz = accs[0]
        return dz * ex[1].astype(F32), dz * ex[0].astype(F32)

    return _mm_cols(name, dout, [_merged(wd)], NT, 2, epilogue, extras=[act_a, dact_b], deps=deps)


def _grad_w_cols(name, z, dout, deps=()):
    t, f = z.shape
    d = dout.shape[1]
    tk = _grad_k_tile(t)
    fh = f // 2
    dw = _mm(
        name,
        ins=[(z, pl.BlockSpec((tk, fh), lambda j, n, k: (k, j))),
             (dout, pl.BlockSpec((tk, d), lambda j, n, k: (k, 0)))],
        terms=[(0, 0, 1, TN)],
        n_acc=1,
        grid=(2, 1, t // tk),
        acc_shape=(fh, d),
        outs=[(pltpu.HBM((f, d), BF16), pl.BlockSpec((fh, d), lambda j, n, k: (j, 0)))],
        epilogue=lambda accs, ex: (accs[0],),
        deps=deps,
    )[0]
    return dw.reshape(N_CHIPS, f // N_CHIPS, d)


def _norm_bwd_outs(t, d, tm, copy_scale):
    row = pl.BlockSpec((tm, d), lambda i: (i, 0))
    outs = [(jax.ShapeDtypeStruct((t, d), F32), row)]
    if copy_scale is not None:
        outs.append((jax.ShapeDtypeStruct((t, d), BF16), row))
    outs.append((jax.ShapeDtypeStruct((t // tm, 1, d), F32), pl.BlockSpec((None, 1, d), lambda i: (i, 0, 0))))
    return row, outs


def _ffn_bwd_in(name, da, db, wg, wu, x, g, dres, copy_scale, deps=()):
    t = da.shape[0]
    d = wg.shape[2]
    tm = _row_tile(t)
    row, outs = _norm_bwd_outs(t, d, tm, copy_scale)
    return _mm_rows(
        name, [_whole_rows(da, tm), _whole_rows(db, tm)], [_merged(wg), _merged(wu)], NN, t,
        outs=outs,
        epilogue=_norm_bwd_epilogue(copy_scale),
        extras=[(x, row), (g, pl.BlockSpec((1, d), lambda i: (0, 0))), (dres, row)],
        deps=deps,
    )


def _in_proj(name, h, w_in):
    return _mm_cols(name, h, [w_in], NN, 1, lambda accs, ex: (accs[0],), out_dtype=F32)[0]


def _in_proj_bwd(name, dp, w_in, x, g, dres, copy_scale, deps=()):
    t = dp.shape[0]
    d = w_in.shape[1]
    tm = _row_tile(t)
    row, outs = _norm_bwd_outs(t, d, tm, copy_scale)
    return _mm_rows(
        name, [_whole_rows(dp, tm)], [w_in], NT, t,
        outs=outs,
        epilogue=_norm_bwd_epilogue(copy_scale),
        extras=[(x, row), (g, pl.BlockSpec((1, d), lambda i: (0, 0))), (dres, row)],
        deps=deps,
    )


def _grad_w_in(name, h, dp, ns):
    t, d = h.shape
    pj = dp.shape[1] // ns
    tk = _k_tile(t)
    return _mm(
        name,
        ins=[(h, pl.BlockSpec((tk, d), lambda j, n, k: (k, 0))),
             (dp, pl.BlockSpec((tk, pj), lambda j, n, k: (k, j)))],
        terms=[(0, 0, 1, TN)],
        n_acc=1,
        grid=(ns, 1, t // tk),
        acc_shape=(d, pj),
        outs=[(pltpu.HBM((ns, d, pj), BF16), pl.BlockSpec((None, d, pj), lambda j, n, k: (j, 0, 0)))],
        epilogue=lambda accs, ex: (accs[0],),
    )[0]


def _out_proj(name, mix, w_out, x, g_next):
    t, dm = mix.shape
    d = w_out.shape[1]
    tm = _row_tile(t)
    row = pl.BlockSpec((tm, d), lambda i, n, k: (i, 0))
    return _mm(
        name,
        ins=[(mix, pl.BlockSpec((tm, dm), lambda i, n, k: (i, 0))),
             (w_out, pl.BlockSpec((dm, d), lambda i, n, k: (0, 0)))],
        terms=[(0, 0, 1, NN)],
        n_acc=1,
        grid=(t // tm, 1, 1),
        acc_shape=(tm, d),
        outs=[(jax.ShapeDtypeStruct((t, d), F32), row), (jax.ShapeDtypeStruct((t, d), BF16), row)],
        epilogue=lambda accs, ex: (ex[0] + accs[0], _rmsnorm(ex[0] + accs[0], ex[1])),
        extras=[(x, row), (g_next, pl.BlockSpec((1, d), lambda i, n, k: (0, 0)))],
    )


def _out_proj_bwd(name, dx, w_out, deps=()):
    t, d = dx.shape
    dm = w_out.shape[0]
    tm = _row_tile(t)
    return _mm(
        name,
        ins=[(dx, pl.BlockSpec((tm, d), lambda i, n, k: (i, 0))),
             (w_out, pl.BlockSpec((dm, d), lambda i, n, k: (0, 0)))],
        terms=[(0, 0, 1, NT)],
        n_acc=1,
        grid=(t // tm, 1, 1),
        acc_shape=(tm, dm),
        outs=[(jax.ShapeDtypeStruct((t, dm), F32), pl.BlockSpec((tm, dm), lambda i, n, k: (i, 0)))],
        epilogue=lambda accs, ex: (accs[0],),
        deps=deps,
    )[0]


def _grad_w_out(name, mix, dx):
    t, dm = mix.shape
    d = dx.shape[1]
    tk = _k_tile(t)
    return _mm(
        name,
        ins=[(mix, pl.BlockSpec((tk, dm), lambda a, n, k: (k, 0))),
             (dx, pl.BlockSpec((tk, d), lambda a, n, k: (k, 0)))],
        terms=[(0, 0, 1, TN)],
        n_acc=1,
        grid=(1, 1, t // tk),
        acc_shape=(dm, d),
        outs=[(pltpu.HBM((dm, d), BF16), pl.BlockSpec((dm, d), lambda a, n, k: (0, 0)))],
        epilogue=lambda accs, ex: (accs[0],),
    )[0]


def _head_group_matrix():
    r = lax.broadcasted_iota(jnp.int32, (ATTN_W, ATTN_W), 0)
    c = lax.broadcasted_iota(jnp.int32, (ATTN_W, ATTN_W), 1)
    same = jnp.right_shift(r, 6) == jnp.right_shift(c, 6)
    return jnp.where(same, 1.0, 0.0).astype(BF16)


def _qk_prep(name, proj, gq, gk):
    b, s, _ = proj.shape
    tm = KPAD
    nb = s // tm

    def body(q_ref, k_ref, v_ref, gq_ref, gk_ref, qn_ref, kn_ref, vb_ref):
        j = pl.program_id(1)
        bd = _head_group_matrix()

        def norm(xv, g):
            ms = _dot_exact_rhs(xv * xv, bd, pieces=2) * (1.0 / ATTN_DH)
            return xv * lax.rsqrt(ms + RMS_EPS) * g

        @pl.when(j == 0)
        def _():
            kn_ref[...] = jnp.zeros_like(kn_ref)
            vb_ref[...] = jnp.zeros_like(vb_ref)

        @pl.when(j > 0)
        def _():
            qn_ref[...] = norm(q_ref[...], gq_ref[...]).astype(BF16)
            kn_ref[...] = norm(k_ref[...], gk_ref[...]).astype(BF16)
            vb_ref[...] = v_ref[...].astype(BF16)

    src_blk = lambda col: pl.BlockSpec((None, tm, ATTN_W), lambda bi, j: (bi, jnp.maximum(j - 1, 0), col))
    gspec = pl.BlockSpec((1, ATTN_W), lambda bi, j: (0, 0))
    padded = pl.BlockSpec((None, tm, ATTN_W), lambda bi, j: (bi, j, 0))
    return pl.pallas_call(
        body,
        name=name,
        grid=(b, nb + 1),
        in_specs=[src_blk(0), src_blk(1), src_blk(2), gspec, gspec],
        out_specs=[src_blk(0), padded, padded],
        out_shape=[jax.ShapeDtypeStruct((b, s, ATTN_W), BF16), jax.ShapeDtypeStruct((b, KPAD + s, ATTN_W), BF16),
                   jax.ShapeDtypeStruct((b, KPAD + s, ATTN_W), BF16)],
        compiler_params=_params("parallel", "arbitrary"),
    )(proj, proj, proj, gq, gk)


def _qk_prep_bwd(name, proj, dqn, dkn, dv, gq, gk):
    b, s, _ = proj.shape
    tm = KPAD
    nb = s // tm

    def body(q_ref, k_ref, dqn_ref, dkn_ref, dv_ref, gq_ref, gk_ref, dq_ref, dk_ref, dvb_ref, dgq_ref, dgk_ref):
        bd = _head_group_matrix()

        def bwd(xv, dy, g):
            ms = _dot_exact_rhs(xv * xv, bd, pieces=2) * (1.0 / ATTN_DH)
            rstd = lax.rsqrt(ms + RMS_EPS)
            xhat = xv * rstd
            dxhat = dy * g
            gm = _dot_exact_rhs(dxhat * xhat, bd, pieces=2) * (1.0 / ATTN_DH)
            return rstd * (dxhat - xhat * gm), jnp.sum(dy * xhat, axis=0, keepdims=True)

        dq, dgq = bwd(q_ref[...], dqn_ref[...], gq_ref[...])
        dk, dgk = bwd(k_ref[...], dkn_ref[...], gk_ref[...])
        dq_ref[...] = dq.astype(BF16)
        dk_ref[...] = dk.astype(BF16)
        dvb_ref[...] = dv_ref[...].astype(BF16)
        dgq_ref[...] = dgq
        dgk_ref[...] = dgk

    col = lambda c: pl.BlockSpec((None, tm, ATTN_W), lambda bi, j: (bi, j, c))
    past_pad = pl.BlockSpec((None, tm, ATTN_W), lambda bi, j: (bi, j + 1, 0))
    gspec = pl.BlockSpec((1, ATTN_W), lambda bi, j: (0, 0))
    pspec = pl.BlockSpec((None, 1, ATTN_W), lambda bi, j: (bi * nb + j, 0, 0))
    o_shape = jax.ShapeDtypeStruct((b, s, ATTN_W), BF16)
    p_shape = jax.ShapeDtypeStruct((b * nb, 1, ATTN_W), F32)
    return pl.pallas_call(
        body,
        name=name,
        grid=(b, nb),
        in_specs=[col(0), col(1), col(0), past_pad, past_pad, gspec, gspec],
        out_specs=[col(0)] * 3 + [pspec] * 2,
        out_shape=[o_shape] * 3 + [p_shape] * 2,
        compiler_params=_params("parallel", "parallel"),
    )(proj, proj, dqn, dkn, dv, gq, gk)


Q_CHUNKS = 4
QBLK = Q_CHUNKS * CHUNK
WIN = (LEFT_CHUNKS + Q_CHUNKS) * CHUNK
DB_W = BAND + CHUNK
MASKED = -1e30


def _band_table(bias):
    rows = [jnp.pad(bias, ((0, 0), (0, 0), (CHUNK * i, WIN - BAND - CHUNK * i)), constant_values=MASKED)
            for i in range(Q_CHUNKS)]
    return jnp.concatenate(rows, axis=1)


def _head_lanes(hh):
    lane = lax.broadcasted_iota(jnp.int32, (1, LANES), 1)
    return (lane < ATTN_DH) if hh == 0 else (lane >= ATTN_DH)


def _attn_probs(qh, kw, table, start):
    s = _dot(qh, kw, NT) * (ATTN_DH ** -0.5) + table
    col = lax.broadcasted_iota(jnp.int32, (QBLK, WIN), 1)
    s = jnp.where(col + start >= KPAD, s, MASKED)
    m = jnp.max(s, axis=-1, keepdims=True)
    p = jnp.exp(s - m)
    return p * (1.0 / jnp.sum(p, axis=-1, keepdims=True))


def _attn_fwd(name, q, k, v, table, deps=()):
    b, s, w = q.shape
    sp = k.shape[1]

    def body(q_ref, k_ref, v_ref, t_ref, *rest):
        o_ref = rest[-1]
        start = pl.multiple_of(pl.program_id(2) * QBLK, QBLK)
        kw = k_ref[pl.ds(start, WIN), :]
        vw = v_ref[pl.ds(start, WIN), :]
        q2 = q_ref[...]
        lanes = [_head_lanes(hh) for hh in range(2)]
        probs = [_attn_probs(jnp.where(mine, q2, jnp.zeros_like(q2)), kw, t_ref[hh], start).astype(BF16)
                 for hh, mine in enumerate(lanes)]
        outs = [_dot(p, vw) for p in probs]
        o_ref[...] = jnp.where(lanes[0], outs[0], outs[1]).astype(BF16)

    qspec = pl.BlockSpec((None, QBLK, LANES), lambda p, bi, i: (bi, i, p))
    kspec = pl.BlockSpec((None, sp, LANES), lambda p, bi, i: (bi, 0, p))
    return pl.pallas_call(
        body,
        name=name,
        grid=(w // LANES, b, s // QBLK),
        in_specs=[qspec, kspec, kspec, pl.BlockSpec((2, QBLK, WIN), lambda p, bi, i: (p, 0, 0))] + [ANY] * len(deps),
        out_specs=qspec,
        out_shape=jax.ShapeDtypeStruct((b, s, w), BF16),
        compiler_params=_params("parallel", "parallel", "arbitrary"),
    )(q, k, v, table, *deps)


def _attn_bwd(name, q, k, v, table, dmix):
    b, s, w = q.shape
    sp = k.shape[1]

    def body(q_ref, k_ref, v_ref, t_ref, do_ref, dq_ref, dk_ref, dv_ref, dbe_ref, dbo_ref):
        bi = pl.program_id(1)
        i = pl.program_id(2)
        start = pl.multiple_of(i * QBLK, QBLK)
        win = pl.ds(start, WIN)

        @pl.when(i == 0)
        def _():
            dk_ref[...] = jnp.zeros_like(dk_ref)
            dv_ref[...] = jnp.zeros_like(dv_ref)

        @pl.when(jnp.logical_and(i == 0, bi == 0))
        def _():
            dbe_ref[...] = jnp.zeros_like(dbe_ref)
            dbo_ref[...] = jnp.zeros_like(dbo_ref)

        kw = k_ref[win, :]
        vw = v_ref[win, :]
        q2 = q_ref[...]
        do2 = do_ref[...].astype(BF16)
        lanes = [_head_lanes(hh) for hh in range(2)]
        qh = [jnp.where(mine, q2, jnp.zeros_like(q2)) for mine in lanes]
        doh = [jnp.where(mine, do2, jnp.zeros_like(do2)) for mine in lanes]
        p = [_attn_probs(qh[hh], kw, t_ref[hh], start) for hh in range(2)]
        dp = [_dot(doh[hh], vw, NT) for hh in range(2)]
        ds = [p[hh] * (dp[hh] - jnp.sum(p[hh] * dp[hh], axis=-1, keepdims=True)) for hh in range(2)]
        dsb = [(x * (ATTN_DH ** -0.5)).astype(BF16) for x in ds]
        pb = [x.astype(BF16) for x in p]
        dq = [_dot(dsb[hh], kw) for hh in range(2)]
        dk = [_dot(dsb[hh], qh[hh], TN) for hh in range(2)]
        dv = [_dot(pb[hh], doh[hh], TN) for hh in range(2)]
        for hh in range(2):
            for qi in range(Q_CHUNKS):
                c0 = (qi // 2) * LANES
                blk = ds[hh][qi * CHUNK:(qi + 1) * CHUNK, c0:c0 + DB_W]
                if qi % 2 == 0:
                    dbe_ref[hh] += blk
                else:
                    dbo_ref[hh] += blk
        dq_ref[...] = jnp.where(lanes[0], dq[0], dq[1])
        dk_ref[win, :] += dk[0] + dk[1]
        dv_ref[win, :] += dv[0] + dv[1]

    qspec = pl.BlockSpec((None, QBLK, LANES), lambda p, bi, i: (bi, i, p))
    kspec = pl.BlockSpec((None, sp, LANES), lambda p, bi, i: (bi, 0, p))
    dbspec = pl.BlockSpec((2, CHUNK, DB_W), lambda p, bi, i: (p, 0, 0))
    db_shape = jax.ShapeDtypeStruct((ATTN_HEADS, CHUNK, DB_W), F32)
    return pl.pallas_call(
        body,
        name=name,
        grid=(w // LANES, b, s // QBLK),
        in_specs=[qspec, kspec, kspec, pl.BlockSpec((2, QBLK, WIN), lambda p, bi, i: (p, 0, 0)), qspec],
        out_specs=[qspec, kspec, kspec, dbspec, dbspec],
        out_shape=[jax.ShapeDtypeStruct((b, s, w), F32), jax.ShapeDtypeStruct((b, sp, w), F32),
                   jax.ShapeDtypeStruct((b, sp, w), F32), db_shape, db_shape],
        compiler_params=_params("arbitrary", "arbitrary", "arbitrary"),
    )(q, k, v, table, dmix)


HQ_COL = 3 * ATTN_W // HGRN_DH
HF_COL = HQ_COL + HGRN_HEADS
HI_COL = HF_COL + HGRN_HEADS
HG_COL = HI_COL + HGRN_HEADS
HGRN_ROWS = 8 * CHUNK
HEAD_LANES = [slice(hh * HGRN_DH, (hh + 1) * HGRN_DH) for hh in range(HGRN_HEADS)]


def _tri(lower):
    r = lax.broadcasted_iota(jnp.int32, (CHUNK, CHUNK), 0)
    c = lax.broadcasted_iota(jnp.int32, (CHUNK, CHUNK), 1)
    return (r >= c) if lower else (r <= c)


def _hgrn_chunk(hq, hf, lb, tril):
    sig = _sigmoid(hf)
    f = lb + (1.0 - lb) * sig
    g = jnp.log(f)
    ones_l = jnp.where(tril, 1.0, 0.0).astype(BF16)
    b = _dot_exact_lhs(ones_l, g)
    bl = jnp.sum(g, axis=0, keepdims=True)
    rows = lax.broadcasted_iota(jnp.int32, g.shape, 0)
    bm = jnp.sum(jnp.where(rows <= CHUNK // 2, g, 0.0), axis=0, keepdims=True)
    sq = _sigmoid(hq)
    q = hq * sq
    k = 1.0 - f
    return sig, f, b, bl, bm, sq, q, k


def _hgrn_fwd(name, proj, attn, lb, go, b, s):
    nc = s // CHUNK
    t = b * s
    nblk = s // HGRN_ROWS
    cpb = HGRN_ROWS // CHUNK

    def body(hq_ref, hf_ref, hi_ref, hg_ref, attn_ref, lb_ref, go_ref, mix_ref, oraw_ref, st_ref, s_scr):
        tril = _tri(True)
        gov = go_ref[...]
        mix_ref[:, 0:ATTN_W] = attn_ref[...]

        @pl.when(pl.program_id(1) == 0)
        def _():
            s_scr[...] = jnp.zeros_like(s_scr)

        def step(c, carry):
            sl = pl.ds(pl.multiple_of(c * CHUNK, CHUNK), CHUNK)
            hg = hg_ref[sl, :]
            _, _, bb, bl, bm, _, q, k = _hgrn_chunk(hq_ref[sl, :], hf_ref[sl, :], lb_ref[...], tril)
            vb = hi_ref[sl, :].astype(BF16)
            qe = (q * jnp.exp(bb - bm)).astype(BF16)
            ke = (k * jnp.exp(bm - bb)).astype(BF16)
            qb = (q * jnp.exp(bb)).astype(BF16)
            kb = (k * jnp.exp(bl - bb)).astype(BF16)
            e_last = jnp.exp(bl)
            gate = _silu(hg)
            st = [s_scr[hh] for hh in range(HGRN_HEADS)]
            a = [jnp.where(tril, _dot(qe[:, hs], ke[:, hs], NT), 0.0).astype(BF16) for hs in HEAD_LANES]
            o_state = [_dot(qb[:, hs], st[hh].astype(BF16), NT) for hh, hs in enumerate(HEAD_LANES)]
            st_next = [st[hh] * e_last[:, hs] + _dot(vb[:, hs], kb[:, hs], TN) for hh, hs in enumerate(HEAD_LANES)]
            o = [_dot(a[hh], vb[:, hs]) + o_state[hh] for hh, hs in enumerate(HEAD_LANES)]
            ro = [(oh * lax.rsqrt(jnp.mean(oh * oh, axis=-1, keepdims=True) + RMS_EPS) * gov) * gate[:, hs]
                  for oh, hs in zip(o, HEAD_LANES)]
            for hh in range(HGRN_HEADS):
                st_ref[hh, c] = st[hh]
                s_scr[hh] = st_next[hh]
            mix_ref[sl, ATTN_W:ATTN_W + HGRN_W] = jnp.concatenate(ro, axis=1).astype(BF16)
            oraw_ref[sl, :] = jnp.concatenate(o, axis=1)
            return carry

        lax.fori_loop(0, cpb, step, 0)

    col = lambda base: pl.BlockSpec((HGRN_ROWS, HGRN_W), lambda bi, i: (bi * nblk + i, base // HGRN_HEADS))
    out = pl.BlockSpec((HGRN_ROWS, HGRN_W), lambda bi, i: (bi * nblk + i, 0))
    return pl.pallas_call(
        body,
        name=name,
        grid=(b, nblk),
        in_specs=[col(HQ_COL), col(HF_COL), col(HI_COL), col(HG_COL), out,
                  pl.BlockSpec((1, HGRN_W), lambda bi, i: (0, 0)), pl.BlockSpec((1, HGRN_DH), lambda bi, i: (0, 0))],
        out_specs=[pl.BlockSpec((HGRN_ROWS, ATTN_W + HGRN_W), lambda bi, i: (bi * nblk + i, 0)), out,
                   pl.BlockSpec((None, HGRN_HEADS, cpb, HGRN_DH, HGRN_DH), lambda bi, i: (bi, 0, i, 0, 0))],
        out_shape=[jax.ShapeDtypeStruct((t, ATTN_W + HGRN_W), BF16), jax.ShapeDtypeStruct((t, HGRN_W), F32),
                   jax.ShapeDtypeStruct((b, HGRN_HEADS, nc, HGRN_DH, HGRN_DH), F32)],
        scratch_shapes=[pltpu.VMEM((HGRN_HEADS, HGRN_DH, HGRN_DH), F32)],
        compiler_params=_params("parallel", "arbitrary"),
    )(proj, proj, proj, proj, attn, lb, go)


def _hgrn_bwd(name, proj, dqkv, lb, go, oraw, states, dmix, b, s):
    t = b * s
    nblk = s // HGRN_ROWS
    cpb = HGRN_ROWS // CHUNK

    def body(hq_ref, hf_ref, hi_ref, hg_ref, dq_ref, dk_ref, dv_ref, lb_ref, go_ref, oraw_ref, st_ref, dro_ref,
             dp_ref, dlb_ref, dgo_ref, ds_scr, dlb_scr, dgo_scr):
        tril = _tri(True)
        ones_u = jnp.where(_tri(False), 1.0, 0.0).astype(BF16)
        gov = go_ref[...]
        dp_ref[:, 0:ATTN_W] = dq_ref[...]
        dp_ref[:, ATTN_W:2 * ATTN_W] = dk_ref[...]
        dp_ref[:, 2 * ATTN_W:3 * ATTN_W] = dv_ref[...]

        @pl.when(pl.program_id(1) == 0)
        def _():
            ds_scr[...] = jnp.zeros_like(ds_scr)
            dlb_scr[...] = jnp.zeros_like(dlb_scr)
            dgo_scr[...] = jnp.zeros_like(dgo_scr)

        def step(ci, carry):
            c = cpb - 1 - ci
            sl = pl.ds(pl.multiple_of(c * CHUNK, CHUNK), CHUNK)
            hq = hq_ref[sl, :]
            hg = hg_ref[sl, :]
            sig, f, bb, bl, bm, sq, q, k = _hgrn_chunk(hq, hf_ref[sl, :], lb_ref[...], tril)
            vb = hi_ref[sl, :].astype(BF16)
            ebm = jnp.exp(bb - bm)
            embm = jnp.exp(bm - bb)
            eb = jnp.exp(bb)
            ebl = jnp.exp(bl - bb)
            e_last = jnp.exp(bl)
            qe = (q * ebm).astype(BF16)
            ke = (k * embm).astype(BF16)
            qb = (q * eb).astype(BF16)
            kb = (k * ebl).astype(BF16)
            st = [st_ref[hh, c] for hh in range(HGRN_HEADS)]
            dst = [ds_scr[hh] for hh in range(HGRN_HEADS)]
            o = oraw_ref[sl, :]
            dro = dro_ref[sl, :]
            sg = _sigmoid(hg)
            gov4 = jnp.concatenate([gov] * HGRN_HEADS, axis=1)
            rstd = jnp.concatenate(
                [jnp.broadcast_to(lax.rsqrt(jnp.mean(o[:, hs] * o[:, hs], axis=-1, keepdims=True) + RMS_EPS),
                                  (CHUNK, HGRN_DH)) for hs in HEAD_LANES], axis=1)
            ohat = o * rstd
            dn = dro * (hg * sg)
            dhg = dro * (ohat * gov4) * (sg * (1.0 + hg * (1.0 - sg)))
            dgo_inc = jnp.sum(dn * ohat, axis=0, keepdims=True)
            dohat = dn * gov4
            proj_h = dohat * ohat
            pm = jnp.concatenate(
                [jnp.broadcast_to(jnp.mean(proj_h[:, hs], axis=-1, keepdims=True), (CHUNK, HGRN_DH))
                 for hs in HEAD_LANES], axis=1)
            dob = (rstd * (dohat - ohat * pm)).astype(BF16)
            stb = [x.astype(BF16) for x in st]
            dstb = [x.astype(BF16) for x in dst]
            a = [jnp.where(tril, _dot(qe[:, hs], ke[:, hs], NT), 0.0).astype(BF16) for hs in HEAD_LANES]
            dab = [jnp.where(tril, _dot(dob[:, hs], vb[:, hs], NT), 0.0).astype(BF16) for hs in HEAD_LANES]
            dqb = [_dot(dob[:, hs], stb[hh]) for hh, hs in enumerate(HEAD_LANES)]
            dkb = [_dot(vb[:, hs], dstb[hh]) for hh, hs in enumerate(HEAD_LANES)]
            dv_state = [_dot(kb[:, hs], dstb[hh], NT) for hh, hs in enumerate(HEAD_LANES)]
            dst_next = [dst[hh] * e_last[:, hs] + _dot(dob[:, hs], qb[:, hs], TN) for hh, hs in enumerate(HEAD_LANES)]
            dv = [_dot(a[hh], dob[:, hs], TN) + dv_state[hh] for hh, hs in enumerate(HEAD_LANES)]
            dqe = jnp.concatenate([_dot(dab[hh], ke[:, hs]) for hh, hs in enumerate(HEAD_LANES)], axis=1)
            dke = jnp.concatenate([_dot(dab[hh], qe[:, hs], TN) for hh, hs in enumerate(HEAD_LANES)], axis=1)
            dqb = jnp.concatenate(dqb, axis=1)
            dkb = jnp.concatenate(dkb, axis=1)
            state_term = jnp.concatenate(
                [jnp.sum(dst[hh] * st[hh], axis=0, keepdims=True) for hh in range(HGRN_HEADS)], axis=1)
            dq = dqe * ebm + dqb * eb
            dk = dke * embm + dkb * ebl
            db = (qe.astype(F32) * dqe - ke.astype(F32) * dke) + q * (dqb * eb) - k * (dkb * ebl)
            d_last = jnp.sum(k * ebl * dkb, axis=0, keepdims=True) + state_term * e_last
            dg = _dot_exact_lhs(ones_u, db) + d_last
            df = dg / f - dk
            first = HQ_COL * HGRN_DH
            dp_ref[sl, first:first + HGRN_W] = (dq * (sq * (1.0 + hq * (1.0 - sq)))).astype(BF16)
            dp_ref[sl, first + HGRN_W:first + 2 * HGRN_W] = (df * (1.0 - lb_ref[...]) * sig * (1.0 - sig)).astype(BF16)
            dp_ref[sl, first + 2 * HGRN_W:first + 3 * HGRN_W] = jnp.concatenate(dv, axis=1).astype(BF16)
            dp_ref[sl, first + 3 * HGRN_W:first + 4 * HGRN_W] = dhg.astype(BF16)
            dlb_scr[...] += jnp.sum(df * (1.0 - sig), axis=0, keepdims=True)
            dgo_scr[...] += dgo_inc
            for hh in range(HGRN_HEADS):
                ds_scr[hh] = dst_next[hh]
            return carry

        lax.fori_loop(0, cpb, step, 0)

        @pl.when(pl.program_id(1) == nblk - 1)
        def _():
            dlb_ref[...] = dlb_scr[...]
            dgo_ref[...] = dgo_scr[...]

    rows = lambda bi, i: bi * nblk + (nblk - 1 - i)
    col = lambda base: pl.BlockSpec((HGRN_ROWS, HGRN_W), lambda bi, i: (rows(bi, i), base // HGRN_HEADS))
    out = pl.BlockSpec((HGRN_ROWS, HGRN_W), lambda bi, i: (rows(bi, i), 0))
    part = pl.BlockSpec((None, 1, HGRN_W), lambda bi, i: (bi, 0, 0))
    width = HG_COL * HGRN_DH + HGRN_W
    o_shape = jax.ShapeDtypeStruct((t, width), BF16)
    p_shape = jax.ShapeDtypeStruct((b, 1, HGRN_W), F32)
    return pl.pallas_call(
        body,
        name=name,
        grid=(b, nblk),
        in_specs=[col(HQ_COL), col(HF_COL), col(HI_COL), col(HG_COL), out, out, out,
                  pl.BlockSpec((1, HGRN_W), lambda bi, i: (0, 0)), pl.BlockSpec((1, HGRN_DH), lambda bi, i: (0, 0)), out,
                  pl.BlockSpec((None, HGRN_HEADS, cpb, HGRN_DH, HGRN_DH), lambda bi, i: (bi, 0, nblk - 1 - i, 0, 0)),
                  col(ATTN_W // HGRN_DH)],
        out_specs=[pl.BlockSpec((HGRN_ROWS, width), lambda bi, i: (rows(bi, i), 0))] + [part] * 2,
        out_shape=[o_shape] + [p_shape] * 2,
        scratch_shapes=[pltpu.VMEM((HGRN_HEADS, HGRN_DH, HGRN_DH), F32), pltpu.VMEM((1, HGRN_W), F32),
                        pltpu.VMEM((1, HGRN_W), F32)],
        compiler_params=_params("parallel", "arbitrary"),
    )(proj, proj, proj, proj, *dqkv, lb, go, oraw, states, dmix)


def _small_grads(name, dg1, dgm, dg2, dgq, dgk, dbias_t, dlb, dgo, lbp):
    d = dg1.shape[1]

    def body(dg1_ref, dgm_ref, dg2_ref, dgq_ref, dgk_ref, dbias_ref, dlb_ref, dgo_ref, lbp_ref,
             g1_ref, gm_ref, g2_ref, gq_ref, gk_ref, rb_ref, lbg_ref, go_ref):
        g1_ref[...] = jnp.sum(dg1_ref[...], axis=0, keepdims=True)
        gm_ref[...] = jnp.sum(dgm_ref[...], axis=0, keepdims=True)
        g2_ref[...] = jnp.sum(dg2_ref[...], axis=0, keepdims=True)
        r = lax.broadcasted_iota(jnp.int32, (ATTN_W, ATTN_DH), 0)
        cidx = lax.broadcasted_iota(jnp.int32, (ATTN_W, ATTN_DH), 1)
        fold = jnp.where(jnp.bitwise_and(r, ATTN_DH - 1) == cidx, 1.0, 0.0).astype(BF16)
        gq_ref[...] = jnp.sum(_dot_exact_rhs(dgq_ref[...], fold), axis=0, keepdims=True)
        gk_ref[...] = jnp.sum(_dot_exact_rhs(dgk_ref[...], fold), axis=0, keepdims=True)
        gosum = jnp.sum(dgo_ref[...], axis=0, keepdims=True)
        go_ref[...] = (gosum[:, 0:HGRN_DH] + gosum[:, HGRN_DH:2 * HGRN_DH]
                       + gosum[:, 2 * HGRN_DH:3 * HGRN_DH] + gosum[:, 3 * HGRN_DH:4 * HGRN_DH])
        p0 = lbp_ref[0:1, :]
        p1 = lbp_ref[1:2, :]
        lbv = 1.0 / (1.0 + jnp.exp(p1 - p0))
        dp0 = jnp.sum(dlb_ref[...], axis=0, keepdims=True) * lbv * (1.0 - lbv)
        lbg_ref[0:1, :] = dp0
        lbg_ref[1:2, :] = -dp0
        sidx = lax.broadcasted_iota(jnp.int32, (BAND, N_REL_PAD), 0)
        ridx = lax.broadcasted_iota(jnp.int32, (BAND, N_REL_PAD), 1)

        def step(tq, acc):
            rel = jnp.clip(tq + KPAD - sidx, -REL_CLIP, REL_CLIP) + REL_CLIP
            onehot = jnp.where(rel == ridx, 1.0, 0.0).astype(BF16)
            return acc + _dot_exact_rhs(dbias_ref[tq], onehot)

        rb_ref[...] = lax.fori_loop(0, CHUNK, step, jnp.zeros((ATTN_HEADS, N_REL_PAD), F32))

    ins = [dg1, dgm, dg2, dgq, dgk, dbias_t, dlb, dgo, lbp]
    outs = [jax.ShapeDtypeStruct((1, d), F32)] * 3 + [jax.ShapeDtypeStruct((1, ATTN_DH), F32)] * 2 + [
        jax.ShapeDtypeStruct((ATTN_HEADS, N_REL_PAD), F32), jax.ShapeDtypeStruct((2, HGRN_W), F32),
        jax.ShapeDtypeStruct((1, HGRN_DH), F32)]
    vm = pl.BlockSpec(memory_space=pltpu.VMEM)
    return pl.pallas_call(
        body,
        name=name,
        in_specs=[vm] * len(ins),
        out_specs=[vm] * len(outs),
        out_shape=outs,
        compiler_params=pltpu.CompilerParams(vmem_limit_bytes=VMEM_LIMIT),
    )(*ins)


def _adam_update(w, g, m, v):
    m2 = ADAM_B1 * m + (1.0 - ADAM_B1) * g
    v2 = ADAM_B2 * v + (1.0 - ADAM_B2) * (g * g)
    m_hat = m2 / (1.0 - ADAM_B1 ** ADAM_STEP)
    v_hat = v2 / (1.0 - ADAM_B2 ** ADAM_STEP)
    delta = -ADAM_LR * (m_hat / (jnp.sqrt(v_hat) + ADAM_EPS) + ADAM_WD * w)
    return delta, m2, v2


def _rows_tile(r):
    return r if r <= 512 or r % 512 else 512


def _pair_sum(name, grad, theirs, core):
    n, half, c = theirs.shape
    tr = _rows_tile(half)
    nth = half // tr

    def body(core_ref, a_ref, b_ref, o_ref):
        o_ref[...] = (a_ref[...].astype(F32) + b_ref[...].astype(F32)).astype(o_ref.dtype)

    spec = pl.BlockSpec((None, tr, c), lambda i, j, core_ref: (i, j, 0))
    return pl.pallas_call(
        body, name=name,
        grid_spec=pltpu.PrefetchScalarGridSpec(
            num_scalar_prefetch=1, grid=(n, nth),
            in_specs=[pl.BlockSpec((None, tr, c), lambda i, j, core_ref: (i, core_ref[0] * nth + j, 0)), spec],
            out_specs=spec),
        out_shape=pltpu.HBM((n, half, c), BF16), compiler_params=_params("parallel", "parallel"),
    )(core, grad, theirs)


def _chip_sum(name, own, parts, chip):
    _, half, c = own.shape
    tr = _rows_tile(half)

    def body(chip_ref, own_ref, p_ref, o_ref):
        me = chip_ref[0]
        mine = own_ref[...].astype(F32)
        flip_x, flip_y, flip_xy = (p_ref[i].astype(F32) for i in range(3))
        acc = None
        for k in range(N_CHIPS):
            rel = jnp.bitwise_xor(me, k)
            term = jnp.where(rel == 0, mine, jnp.where(rel == 2, flip_x, jnp.where(rel == 1, flip_y, flip_xy)))
            acc = term if acc is None else acc + term
        o_ref[...] = acc

    return pl.pallas_call(
        body, name=name,
        grid_spec=pltpu.PrefetchScalarGridSpec(
            num_scalar_prefetch=1, grid=(half // tr,),
            in_specs=[pl.BlockSpec((None, tr, c), lambda j, chip_ref: (chip_ref[0], j, 0)),
                      pl.BlockSpec((3, tr, c), lambda j, chip_ref: (0, j, 0))],
            out_specs=pl.BlockSpec((tr, c), lambda j, chip_ref: (j, 0))),
        out_shape=pltpu.HBM((half, c), F32), compiler_params=_params("parallel"),
    )(chip, own, parts)


def _adamw(name, w, g_mine, g_theirs, m, v, core):
    _, r, c = w.shape
    half = r // 2
    tr = _rows_tile(half)
    nth = half // tr

    def body(core_ref, w_ref, gm_ref, gt_ref, m_ref, v_ref, g_ref, d_ref, m2_ref, v2_ref):
        g = jnp.where(pl.program_id(0) == core_ref[0], gm_ref[...], gt_ref[...])
        delta, m2, v2 = _adam_update(w_ref[...], g, m_ref[...], v_ref[...])
        g_ref[...] = g
        d_ref[...] = delta
        m2_ref[...] = m2
        v2_ref[...] = v2

    full = pl.BlockSpec((None, tr, c), lambda h, j, core_ref: (0, h * nth + j, 0))
    part = pl.BlockSpec((tr, c), lambda h, j, core_ref: (j, 0))
    shape = jax.ShapeDtypeStruct((1, r, c), F32)
    return pl.pallas_call(
        body, name=name,
        grid_spec=pltpu.PrefetchScalarGridSpec(
            num_scalar_prefetch=1, grid=(2, nth), in_specs=[full, part, part, full, full], out_specs=[full] * 4),
        out_shape=[shape] * 4, compiler_params=_params("parallel", "parallel"),
    )(core, w, g_mine, g_theirs, m, v)


def _rel_bias_table(name, rel_bias):
    padded = jnp.pad(rel_bias, ((0, 0), (0, N_REL_PAD - N_REL)))

    def body(rb_ref, o_ref):
        ridx = lax.broadcasted_iota(jnp.int32, (N_REL_PAD, BAND), 0)
        sidx = lax.broadcasted_iota(jnp.int32, (N_REL_PAD, BAND), 1)
        rb = rb_ref[...]

        def step(tq, carry):
            rel = jnp.clip(tq + KPAD - sidx, -REL_CLIP, REL_CLIP) + REL_CLIP
            onehot = jnp.where(rel == ridx, 1.0, 0.0).astype(BF16)
            o_ref[tq] = _dot_exact_rhs(rb, onehot)
            return carry

        lax.fori_loop(0, CHUNK, step, 0)

    vm = pl.BlockSpec(memory_space=pltpu.VMEM)
    table = pl.pallas_call(
        body, name=name, in_specs=[vm], out_specs=vm,
        out_shape=jax.ShapeDtypeStruct((CHUNK, ATTN_HEADS, BAND), F32),
    )(padded)
    return table.transpose(1, 0, 2)


def _adamw_small(name, w, parts, m, v):
    def body(w_ref, p_ref, m_ref, v_ref, g_ref, d_ref, m2_ref, v2_ref):
        g = p_ref[0]
        for i in range(1, N_DEV):
            g = g + p_ref[i]
        delta, m2, v2 = _adam_update(w_ref[...], g, m_ref[...], v_ref[...])
        g_ref[...] = g
        d_ref[...] = delta
        m2_ref[...] = m2
        v2_ref[...] = v2

    vm = pl.BlockSpec(memory_space=pltpu.VMEM)
    shape = jax.ShapeDtypeStruct((SMALL_ROWS, SMALL_COLS), F32)
    return pl.pallas_call(
        body, name=name, in_specs=[vm] * 4, out_specs=[vm] * 4, out_shape=[shape] * 4,
    )(w, parts, m, v)


def _position():
    return lax.axis_index("x"), lax.axis_index("y"), lax.axis_index("c")


def _other_chips(x, y):
    return [(1 - x, y), (x, 1 - y), (1 - x, 1 - y)]


ANY = pl.BlockSpec(memory_space=pl.ANY)
PAIR_ID = 0


def _pair_handshake():
    x, y, c = _position()
    barrier = pltpu.get_barrier_semaphore()
    pl.semaphore_signal(barrier, inc=1, device_id=(x, y, 1 - c), device_id_type=MESH)
    pl.semaphore_wait(barrier, 1)


PAIR_CALL = pltpu.CompilerParams(collective_id=PAIR_ID)


HBM = pl.BlockSpec(memory_space=pltpu.HBM)
SEM = pl.BlockSpec(memory_space=pltpu.SEMAPHORE)
SPLIT_COPY = pltpu.SideEffectType.DATAFLOW_SIDE_EFFECTING


def _gather_copy(shards, outs, send_sem, recv_sem, i, j):
    x, y, c = _position()
    chips = _other_chips(x, y)
    half = shards[i].shape[0] // 2
    rows = pl.ds(pl.multiple_of(c * half, 16), half)
    return pltpu.make_async_remote_copy(
        src_ref=shards[i].at[rows, :], dst_ref=outs[i].at[2 * x + y, rows, :],
        send_sem=send_sem.at[3 * i + j], recv_sem=recv_sem.at[3 * i + j],
        device_id=(chips[j][0], chips[j][1], c), device_id_type=MESH)


def _gather_start(name, shards, after):
    n = len(shards)

    def body(*refs):
        srcs, outs = refs[:n], refs[n:2 * n]
        send_sem, recv_sem = refs[2 * n + len(after)], refs[2 * n + len(after) + 1]
        token = refs[-1]
        for i in range(n):
            for j in range(3):
                _gather_copy(srcs, outs, send_sem, recv_sem, i, j).start()
        token[...] = jnp.zeros_like(token)

    full = [(N_CHIPS,) + s.shape for s in shards]
    res = pl.pallas_call(
        body,
        name=name,
        in_specs=[HBM] * (2 * n) + [ANY] * len(after),
        out_specs=[SEM, SEM] + [HBM] * (2 * n) + [pl.BlockSpec(memory_space=pltpu.VMEM)],
        out_shape=[pltpu.SemaphoreType.DMA((3 * n,)), pltpu.SemaphoreType.DMA((3 * n,))]
        + [pltpu.HBM(s.shape, s.dtype) for s in shards]
        + [pltpu.HBM(shp, s.dtype) for shp, s in zip(full, shards)]
        + [jax.ShapeDtypeStruct((8, LANES), F32)],
        input_output_aliases={i: 2 + i for i in range(2 * n)},
        compiler_params=pltpu.CompilerParams(has_side_effects=SPLIT_COPY),
    )(*[pltpu.with_memory_space_constraint(s, pltpu.HBM) for s in shards],
      *[pltpu.with_memory_space_constraint(lax.empty(shp, s.dtype), pltpu.HBM) for shp, s in zip(full, shards)],
      *after)
    return res[0], res[1], list(res[2:2 + n]), list(res[2 + n:2 + 2 * n]), res[-1]


def _gather_wait(name, send_sem, recv_sem, shards, outs, after):
    n = len(shards)

    def body(*refs):
        srcs, out_refs = refs[:n], refs[n:2 * n]
        send_ref, recv_ref = refs[2 * n], refs[2 * n + 1]
        for i in range(n):
            for j in range(3):
                copy = _gather_copy(srcs, out_refs, send_ref, recv_ref, i, j)
                copy.wait_send()
                copy.wait_recv()

    res = pl.pallas_call(
        body,
        name=name,
        in_specs=[HBM] * (2 * n) + [SEM, SEM] + [ANY] * len(after),
        out_specs=[HBM] * (2 * n),
        out_shape=[pltpu.HBM(s.shape, s.dtype) for s in shards] + [pltpu.HBM(o.shape, o.dtype) for o in outs],
        input_output_aliases={i: i for i in range(2 * n)},
        compiler_params=pltpu.CompilerParams(has_side_effects=SPLIT_COPY),
    )(*shards, *outs, send_sem, recv_sem, *after)
    return list(res[:n]), list(res[n:])


def _join_copies(srcs, ins, outs, own_send, own_recv, half_send, half_recv):
    x, y, c = _position()
    chips = _other_chips(x, y)
    copies = []
    for i in range(len(srcs)):
        copies.append(pltpu.make_async_remote_copy(
            src_ref=srcs[i], dst_ref=outs[i].at[2 * x + y], send_sem=own_send.at[i], recv_sem=own_recv.at[i],
            device_id=(x, y, 1 - c), device_id_type=MESH))
        half = srcs[i].shape[0] // 2
        rows = pl.ds(pl.multiple_of(c * half, 16), half)
        for j in range(3):
            slot = 2 * chips[j][0] + chips[j][1]
            copies.append(pltpu.make_async_remote_copy(
                src_ref=ins[i].at[slot, rows, :], dst_ref=outs[i].at[slot, rows, :],
                send_sem=half_send.at[3 * i + j], recv_sem=half_recv.at[3 * i + j],
                device_id=(x, y, 1 - c), device_id_type=MESH))
    return copies


def _gather_join(name, shards, outs):
    n = len(shards)

    def body(*refs):
        _pair_handshake()
        copies = _join_copies(refs[:n], refs[n:2 * n], refs[2 * n:3 * n], *refs[3 * n:])
        for cp in copies:
            cp.start()
        for cp in copies:
            cp.wait()

    return pl.pallas_call(
        body,
        name=name,
        in_specs=[ANY] * (2 * n),
        out_specs=[HBM] * n,
        out_shape=[pltpu.HBM(o.shape, o.dtype) for o in outs],
        input_output_aliases={n + i: i for i in range(n)},
        scratch_shapes=[pltpu.SemaphoreType.DMA((n,))] * 2 + [pltpu.SemaphoreType.DMA((3 * n,))] * 2,
        compiler_params=PAIR_CALL,
    )(*shards, *outs)


def _join_start(name, shards, outs):
    n = len(shards)

    def body(*refs):
        _pair_handshake()
        srcs, arrs = refs[:n], refs[n:2 * n]
        sems = refs[2 * n:2 * n + 4]
        token = refs[-1]
        for cp in _join_copies(srcs, arrs, arrs, *sems):
            cp.start()
        token[...] = jnp.zeros_like(token)

    res = pl.pallas_call(
        body,
        name=name,
        in_specs=[HBM] * (2 * n),
        out_specs=[SEM] * 4 + [HBM] * (2 * n) + [pl.BlockSpec(memory_space=pltpu.VMEM)],
        out_shape=[pltpu.SemaphoreType.DMA((n,))] * 2 + [pltpu.SemaphoreType.DMA((3 * n,))] * 2
        + [pltpu.HBM(s.shape, s.dtype) for s in shards] + [pltpu.HBM(o.shape, o.dtype) for o in outs]
        + [jax.ShapeDtypeStruct((8, LANES), F32)],
        input_output_aliases={i: 4 + i for i in range(2 * n)},
        compiler_params=pltpu.CompilerParams(has_side_effects=SPLIT_COPY, collective_id=PAIR_ID),
    )(*shards, *outs)
    return list(res[:4]), list(res[4:4 + n]), list(res[4 + n:4 + 2 * n]), res[-1]


def _join_wait(name, sems, shards, outs, after):
    n = len(shards)

    def body(*refs):
        srcs, arrs = refs[:n], refs[n:2 * n]
        for cp in _join_copies(srcs, arrs, arrs, *refs[2 * n:2 * n + 4]):
            cp.wait_send()
            cp.wait_recv()

    res = pl.pallas_call(
        body,
        name=name,
        in_specs=[HBM] * (2 * n) + [SEM] * 4 + [ANY] * len(after),
        out_specs=[HBM] * (2 * n),
        out_shape=[pltpu.HBM(s.shape, s.dtype) for s in shards] + [pltpu.HBM(o.shape, o.dtype) for o in outs],
        input_output_aliases={i: i for i in range(2 * n)},
        compiler_params=pltpu.CompilerParams(has_side_effects=SPLIT_COPY),
    )(*shards, *outs, *sems, *after)
    return list(res[n:])


def _pair_copy(grads, lands, send_sem, recv_sem, i):
    x, y, c = _position()
    half = grads[i].shape[1] // 2
    give = pl.ds(pl.multiple_of((1 - c) * half, 16), half)
    return pltpu.make_async_remote_copy(
        src_ref=grads[i].at[:, give, :], dst_ref=lands[i], send_sem=send_sem.at[i], recv_sem=recv_sem.at[i],
        device_id=(x, y, 1 - c), device_id_type=MESH)


def _pair_start(name, grads):
    n = len(grads)

    def body(*refs):
        _pair_handshake()
        srcs, lands = refs[:n], refs[n:2 * n]
        send_sem, recv_sem = refs[2 * n], refs[2 * n + 1]
        token = refs[-1]
        for i in range(n):
            _pair_copy(srcs, lands, send_sem, recv_sem, i).start()
        token[...] = jnp.zeros_like(token)

    halves = [(g.shape[0], g.shape[1] // 2, g.shape[2]) for g in grads]
    res = pl.pallas_call(
        body,
        name=name,
        in_specs=[HBM] * (2 * n),
        out_specs=[SEM, SEM] + [HBM] * (2 * n) + [pl.BlockSpec(memory_space=pltpu.VMEM)],
        out_shape=[pltpu.SemaphoreType.DMA((n,)), pltpu.SemaphoreType.DMA((n,))]
        + [pltpu.HBM(g.shape, g.dtype) for g in grads]
        + [pltpu.HBM(shp, g.dtype) for shp, g in zip(halves, grads)]
        + [jax.ShapeDtypeStruct((8, LANES), F32)],
        input_output_aliases={i: 2 + i for i in range(2 * n)},
        compiler_params=pltpu.CompilerParams(has_side_effects=SPLIT_COPY, collective_id=PAIR_ID),
    )(*[pltpu.with_memory_space_constraint(g, pltpu.HBM) for g in grads],
      *[pltpu.with_memory_space_constraint(lax.empty(shp, g.dtype), pltpu.HBM) for shp, g in zip(halves, grads)])
    return res[0], res[1], list(res[2:2 + n]), list(res[2 + n:2 + 2 * n]), res[-1]


def _pair_wait(name, send_sem, recv_sem, grads, lands, after):
    n = len(grads)

    def body(*refs):
        srcs, land_refs = refs[:n], refs[n:2 * n]
        send_ref, recv_ref = refs[2 * n], refs[2 * n + 1]
        for i in range(n):
            copy = _pair_copy(srcs, land_refs, send_ref, recv_ref, i)
            copy.wait_send()
            copy.wait_recv()

    res = pl.pallas_call(
        body,
        name=name,
        in_specs=[HBM] * (2 * n) + [SEM, SEM, ANY],
        out_specs=[HBM] * (2 * n),
        out_shape=[pltpu.HBM(g.shape, g.dtype) for g in grads] + [pltpu.HBM(l.shape, l.dtype) for l in lands],
        input_output_aliases={i: i for i in range(2 * n)},
        compiler_params=pltpu.CompilerParams(has_side_effects=SPLIT_COPY),
    )(*grads, *lands, send_sem, recv_sem, after)
    return list(res[:n]), list(res[n:])


def _scatter_copy(srcs, lands, send_sem, recv_sem, i, j):
    x, y, c = _position()
    chips = _other_chips(x, y)
    return pltpu.make_async_remote_copy(
        src_ref=srcs[i].at[2 * chips[j][0] + chips[j][1]], dst_ref=lands[i].at[j],
        send_sem=send_sem.at[3 * i + j], recv_sem=recv_sem.at[3 * i + j],
        device_id=(chips[j][0], chips[j][1], c), device_id_type=MESH)


def _scatter_start(name, sums):
    n = len(sums)

    def body(*refs):
        srcs, lands = refs[:n], refs[n:2 * n]
        send_sem, recv_sem = refs[2 * n], refs[2 * n + 1]
        token = refs[-1]
        for i in range(n):
            for j in range(3):
                _scatter_copy(srcs, lands, send_sem, recv_sem, i, j).start()
        token[...] = jnp.zeros_like(token)

    land_shapes = [(3,) + s.shape[1:] for s in sums]
    res = pl.pallas_call(
        body,
        name=name,
        in_specs=[HBM] * (2 * n),
        out_specs=[SEM, SEM] + [HBM] * (2 * n) + [pl.BlockSpec(memory_space=pltpu.VMEM)],
        out_shape=[pltpu.SemaphoreType.DMA((3 * n,)), pltpu.SemaphoreType.DMA((3 * n,))]
        + [pltpu.HBM(s.shape, s.dtype) for s in sums]
        + [pltpu.HBM(shp, s.dtype) for shp, s in zip(land_shapes, sums)]
        + [jax.ShapeDtypeStruct((8, LANES), F32)],
        input_output_aliases={i: 2 + i for i in range(2 * n)},
        compiler_params=pltpu.CompilerParams(has_side_effects=SPLIT_COPY),
    )(*[pltpu.with_memory_space_constraint(s, pltpu.HBM) for s in sums],
      *[pltpu.with_memory_space_constraint(lax.empty(shp, s.dtype), pltpu.HBM) for shp, s in zip(land_shapes, sums)])
    return res[0], res[1], list(res[2:2 + n]), list(res[2 + n:2 + 2 * n]), res[-1]


def _scatter_wait(name, send_sem, recv_sem, sums, lands, after):
    n = len(sums)

    def body(*refs):
        srcs, land_refs = refs[:n], refs[n:2 * n]
        send_ref, recv_ref = refs[2 * n], refs[2 * n + 1]
        for i in range(n):
            for j in range(3):
                copy = _scatter_copy(srcs, land_refs, send_ref, recv_ref, i, j)
                copy.wait_send()
                copy.wait_recv()

    res = pl.pallas_call(
        body,
        name=name,
        in_specs=[HBM] * (2 * n) + [SEM, SEM, ANY],
        out_specs=[HBM] * (2 * n),
        out_shape=[pltpu.HBM(s.shape, s.dtype) for s in sums] + [pltpu.HBM(l.shape, l.dtype) for l in lands],
        input_output_aliases={i: i for i in range(2 * n)},
        compiler_params=pltpu.CompilerParams(has_side_effects=SPLIT_COPY),
    )(*sums, *lands, send_sem, recv_sem, after)
    return list(res[:n]), list(res[n:])


def _pair_join(name, halves, small=None):
    n = len(halves)
    if small is None:
        def body_plain(*refs):
            _pair_handshake()
            ins, outs = refs[:n], refs[n:2 * n]
            send_sem, recv_sem = refs[2 * n:]
            x, y, c = _position()
            swaps = [pltpu.make_async_remote_copy(
                src_ref=ins[i], dst_ref=outs[i], send_sem=send_sem.at[i], recv_sem=recv_sem.at[i],
                device_id=(x, y, 1 - c), device_id_type=MESH) for i in range(n)]
            for swap in swaps:
                swap.start()
            for swap in swaps:
                swap.wait()

        return pl.pallas_call(
            body_plain,
            name=name,
            in_specs=[ANY] * n,
            out_specs=[ANY] * n,
            out_shape=[jax.ShapeDtypeStruct(h.shape, h.dtype) for h in halves],
            scratch_shapes=[pltpu.SemaphoreType.DMA((n,))] * 2,
            compiler_params=PAIR_CALL,
        )(*halves)

    def body(*refs):
        ins, small_ref = refs[:n], refs[n]
        outs, all_ref = refs[n + 1:2 * n + 1], refs[2 * n + 1]
        send_sem, recv_sem, sm_send, sm_recv, sm_local = refs[2 * n + 2:]
        x, y, c = _position()
        swaps = []
        for i in range(n):
            swap = pltpu.make_async_remote_copy(
                src_ref=ins[i], dst_ref=outs[i], send_sem=send_sem.at[i], recv_sem=recv_sem.at[i],
                device_id=(x, y, 1 - c), device_id_type=MESH)
            swap.start()
            swaps.append(swap)
        me = 4 * x + 2 * y + c
        sm_own = pltpu.make_async_copy(small_ref, all_ref.at[me], sm_local)
        sm_own.start()
        pushes, arrivals = [], []
        for mask in range(1, N_DEV):
            px, py, pc = x ^ (mask >> 2), y ^ ((mask >> 1) & 1), c ^ (mask & 1)
            pushes.append(pltpu.make_async_remote_copy(
                src_ref=small_ref, dst_ref=all_ref.at[me], send_sem=sm_send.at[mask - 1], recv_sem=sm_recv.at[mask - 1],
                device_id=(px, py, pc), device_id_type=MESH))
            arrivals.append(pltpu.make_async_remote_copy(
                src_ref=small_ref, dst_ref=all_ref.at[4 * px + 2 * py + pc], send_sem=sm_send.at[mask - 1],
                recv_sem=sm_recv.at[mask - 1], device_id=(px, py, pc), device_id_type=MESH))
        for cp in pushes:
            cp.start()
        for swap in swaps:
            swap.wait()
        for cp in arrivals:
            cp.wait_recv()
        for cp in pushes:
            cp.wait_send()
        sm_own.wait()

    res = pl.pallas_call(
        body,
        name=name,
        in_specs=[ANY] * (n + 1),
        out_specs=[ANY] * (n + 1),
        out_shape=[jax.ShapeDtypeStruct(h.shape, h.dtype) for h in halves]
        + [jax.ShapeDtypeStruct((N_DEV,) + small.shape, small.dtype)],
        scratch_shapes=[pltpu.SemaphoreType.DMA((n,))] * 2 + [pltpu.SemaphoreType.DMA((N_DEV - 1,))] * 2
        + [pltpu.SemaphoreType.DMA(())],
    )(*halves, small)
    return res[:n], res[n]


def _lower_bound(lbp):
    return jax.nn.softmax(lbp, axis=0)[0:1]


def _local_step(x, target, g1, gm, g2, gq, gk, go, rel_bias, lbp, weights, on_grads, grads_sent):
    b, s, d = x.shape
    t = b * s
    x0 = x.reshape(t, d)
    tgt = target.reshape(t, d)
    gq_t = jnp.tile(gq, (1, ATTN_HEADS))
    gk_t = jnp.tile(gk, (1, ATTN_HEADS))
    lb = _lower_bound(lbp)
    table = _band_table(_rel_bias_table("rel_bias_table", rel_bias))

    h1 = _rmsnorm_fwd("norm1", x0, g1)
    wg1, wu1, deps1 = weights["first"]((h1, table))
    a1, b1, z1 = _ffn_up("ffn1_up", h1, wg1, wu1, deps1)
    wd1, deps_mid = weights["mid"]((z1,))
    x1, h2 = _ffn_down("ffn1_down", z1, wd1, x0, gm, deps_mid)
    w_in, w_out = weights["mid_rest"]((x1,))
    ns = w_in.shape[0]
    proj = _in_proj("in_proj", h2, w_in)
    proj3 = proj.reshape(b, s, proj.shape[1])
    qn, kn, vb = _qk_prep("qk_prep", proj3, gq_t, gk_t)
    attn = _attn_fwd("attn_fwd", qn, kn, vb, table, weights["last_begin"]((qn,))).reshape(t, ATTN_W)
    mix, oraw, states = _hgrn_fwd("hgrn_fwd", proj, attn, lb, go, b, s)
    x2, h3 = _out_proj("out_proj", mix, w_out, x1, g2)
    wg2, wu2, wd2 = weights["last"]((h3,))
    a2, b2, z2 = _ffn_up("ffn2_up", h3, wg2, wu2)
    dy, dyh, sq = _ffn_down_loss("ffn2_down_loss", z2, wd2, x2, tgt)
    loss = 0.5 * jnp.sum(sq) / d

    da2, db2 = _ffn_bwd_act("ffn2_bwd_act", dyh, wd2, a2, b2)
    dwd2 = _grad_w_cols("ffn2_dwd", z2, dyh)
    dwg2 = _grad_w_cols("ffn2_dwg", da2, h3)
    dwu2 = _grad_w_cols("ffn2_dwu", db2, h3)
    sent2 = on_grads("ffn2", {"ffn2_w_gate": dwg2, "ffn2_w_up": dwu2, "ffn2_w_down": dwd2})
    dx2, dx2b, dg2 = _ffn_bwd_in("ffn2_bwd_in", da2, db2, wg2, wu2, x2, g2, dy, 1.0, sent2)
    sent2 = grads_sent("ffn2", dx2b)

    dwout = _grad_w_out("dw_out", mix, dx2b)
    dmix = _out_proj_bwd("out_proj_bwd", dx2b, w_out, sent2)
    dqn, dkn, dvn, dbe, dbo = _attn_bwd("attn_bwd", qn, kn, vb, table, dmix.reshape(b, s, dmix.shape[1]))
    dbias = dbe[:, :, :BAND] + dbo[:, :, CHUNK:]
    dpq, dpk, dpv, dgq, dgk = _qk_prep_bwd("qk_prep_bwd", proj3, dqn, dkn, dvn, gq_t, gk_t)
    dpq, dpk, dpv = (a.reshape(t, ATTN_W) for a in (dpq, dpk, dpv))
    dproj, dlb, dgo = _hgrn_bwd("hgrn_bwd", proj, (dpq, dpk, dpv), lb, go, oraw, states, dmix, b, s)
    dwin = _grad_w_in("dw_in", h2, dproj, ns)
    dx1, dx1h, dgm = _in_proj_bwd("in_proj_bwd", dproj, w_in, x1, gm, dx2, 0.5)

    dwd1 = _grad_w_cols("ffn1_dwd", z1, dx1h)
    sent_mix = on_grads("mix", {"w_in": dwin, "w_out": dwout.reshape(ns, dwout.shape[0] // ns, d),
                                "ffn1_w_down": dwd1})
    da1, db1 = _ffn_bwd_act("ffn1_bwd_act", dx1h, wd1, a1, b1, sent_mix)
    sent_mix = grads_sent("mix", da1)
    dwg1 = _grad_w_cols("ffn1_dwg", da1, h1, sent_mix)
    dwu1 = _grad_w_cols("ffn1_dwu", db1, h1)
    on_grads("ffn1", {"ffn1_w_gate": dwg1, "ffn1_w_up": dwu1})
    sent1 = grads_sent("ffn1", None)
    dx0, dg1 = _ffn_bwd_in("ffn1_bwd_in", da1, db1, wg1, wu1, x0, g1, dx1, None, sent1)

    nt = dg1.shape[0]
    sg = _small_grads(
        "small_grads", dg1.reshape(nt, d), dgm.reshape(nt, d), dg2.reshape(nt, d),
        dgq.reshape(-1, ATTN_W), dgk.reshape(-1, ATTN_W), dbias.transpose(1, 0, 2),
        dlb.reshape(b, HGRN_W), dgo.reshape(b, HGRN_W), lbp)
    g1g, gmg, g2g, gqg, gkg, rbg, lbg, gog = sg
    small = _pack_small(g1g, gmg, g2g, lbg, rbg[:, :N_REL], gqg, gkg, gog, loss)
    return dx0.reshape(b, s, d), small


LOSS_SLOT = 7 * SMALL_COLS + 2 * ATTN_DH + HGRN_DH


def _pack_small(g1, gm, g2, lbp, rel_bias, gq, gk, go, loss=None):
    flat = [g1.reshape(-1), gm.reshape(-1), g2.reshape(-1), lbp.reshape(-1), rel_bias.reshape(-1)]
    n_bias = 3 * SMALL_COLS - rel_bias.size
    heads = [gq.reshape(-1), gk.reshape(-1), go.reshape(-1)]
    heads.append(jnp.zeros((1,), F32) if loss is None else loss.reshape(1))
    n_tail = SMALL_COLS - sum(h.size for h in heads)
    return jnp.concatenate(flat + [jnp.zeros((n_bias,), F32)] + heads + [jnp.zeros((n_tail,), F32)]).reshape(
        SMALL_ROWS, SMALL_COLS)


def _unpack_small(p, d):
    flat = p.reshape(-1)
    o = 3 * d
    g1, gm, g2 = p[0:1], p[1:2], p[2:3]
    lbp = flat[o:o + 2 * HGRN_W].reshape(2, HGRN_W)
    o = 4 * SMALL_COLS
    rel = flat[o:o + ATTN_HEADS * N_REL].reshape(1, ATTN_HEADS, N_REL)
    o = 7 * SMALL_COLS
    gq = flat[o:o + ATTN_DH].reshape(1, ATTN_DH)
    gk = flat[o + ATTN_DH:o + 2 * ATTN_DH].reshape(1, ATTN_DH)
    go = flat[o + 2 * ATTN_DH:o + 2 * ATTN_DH + HGRN_DH].reshape(1, HGRN_DH)
    return g1, gm, g2, gq, gk, rel, lbp, go


def kernel(x, ffn1_norm_g, ffn1_w_gate, ffn1_w_up, ffn1_w_down, mix_norm_g, w_in, attn_q_norm_g, attn_k_norm_g, attn_rel_bias, hgrn_lower_bounds, hgrn_out_norm_g, w_out, ffn2_norm_g, ffn2_w_gate, ffn2_w_up, ffn2_w_down, loss_target, m_ffn1_norm_g, m_ffn1_w_gate, m_ffn1_w_up, m_ffn1_w_down, m_mix_norm_g, m_w_in, m_attn_q_norm_g, m_attn_k_norm_g, m_attn_rel_bias, m_hgrn_lower_bounds, m_hgrn_out_norm_g, m_w_out, m_ffn2_norm_g, m_ffn2_w_gate, m_ffn2_w_up, m_ffn2_w_down, v_ffn1_norm_g, v_ffn1_w_gate, v_ffn1_w_up, v_ffn1_w_down, v_mix_norm_g, v_w_in, v_attn_q_norm_g, v_attn_k_norm_g, v_attn_rel_bias, v_hgrn_lower_bounds, v_hgrn_out_norm_g, v_w_out, v_ffn2_norm_g, v_ffn2_w_gate, v_ffn2_w_up, v_ffn2_w_down):
    d = x.shape[-1]
    big_w = [ffn1_w_gate, ffn1_w_up, ffn1_w_down, w_in, w_out, ffn2_w_gate, ffn2_w_up, ffn2_w_down]
    big_m = [m_ffn1_w_gate, m_ffn1_w_up, m_ffn1_w_down, m_w_in, m_w_out, m_ffn2_w_gate, m_ffn2_w_up, m_ffn2_w_down]
    big_v = [v_ffn1_w_gate, v_ffn1_w_up, v_ffn1_w_down, v_w_in, v_w_out, v_ffn2_w_gate, v_ffn2_w_up, v_ffn2_w_down]
    big_names = ["ffn1_w_gate", "ffn1_w_up", "ffn1_w_down", "w_in", "w_out", "ffn2_w_gate", "ffn2_w_up", "ffn2_w_down"]
    flipped = {nm for nm in big_names if nm.endswith("gate") or nm.endswith("up")}
    flip = lambda nm, a: jnp.swapaxes(a, 1, 2) if nm in flipped else a
    big_w, big_m, big_v = ([flip(nm, a) for nm, a in zip(big_names, arrs)] for arrs in (big_w, big_m, big_v))

    shards = [w[0].astype(BF16) for w in big_w]
    start_a = _gather_start("gather_start_up1", shards[:2], ())
    start_b = _gather_start("gather_start_mid", shards[2:5], (start_a[4],))
    start_c = _gather_start("gather_start_ffn2", shards[5:], (start_b[4],))

    pending = {}

    def arrived(tag, started, after):
        send_sem, recv_sem, srcs, outs, _ = started
        return _gather_wait("gather_wait_" + tag, send_sem, recv_sem, srcs, outs, after)

    def first_weights(after):
        return (*_gather_join("gather_join_up1", *arrived("up1", start_a, after)), (start_c[4],))

    def mid_weights(after):
        srcs, outs = arrived("mid", start_b, after)
        (wd1,) = _gather_join("gather_join_wd1", srcs[:1], outs[:1])
        pending["mid"] = _join_start("join_start_mid", srcs[1:], outs[1:])
        return wd1, (pending["mid"][3],)

    def mid_rest(after):
        sems, srcs, outs, _ = pending["mid"]
        win_f, wout_f = _join_wait("join_wait_mid", sems, srcs, outs, after)
        return win_f, wout_f.reshape(wout_f.shape[0] * wout_f.shape[1], d)

    def last_begin(after):
        pending["ffn2"] = _join_start("join_start_ffn2", *arrived("ffn2", start_c, after))
        return (pending["ffn2"][3],)

    def last_weights(after):
        sems, srcs, outs, _ = pending["ffn2"]
        return _join_wait("join_wait_ffn2", sems, srcs, outs, after)

    weights = {"first": first_weights, "mid": mid_weights, "mid_rest": mid_rest, "last_begin": last_begin,
               "last": last_weights}

    core = lax.axis_index("c").astype(jnp.int32).reshape(1)
    chip = (2 * lax.axis_index("x") + lax.axis_index("y")).astype(jnp.int32).reshape(1)
    started = {}

    def on_grads(tag, grads):
        names = list(grads)
        started[tag] = (names, _pair_start("pair_start_" + tag, [grads[nm] for nm in names]))
        return (started[tag][1][4],)

    def grads_sent(tag, after):
        names, (send_sem, recv_sem, grads, lands, token) = started[tag]
        grads, theirs = _pair_wait("pair_wait_" + tag, send_sem, recv_sem, grads, lands, token if after is None else after)
        sums = [_pair_sum("pair_sum_" + nm, g, th, core) for nm, g, th in zip(names, grads, theirs)]
        started[tag] = (names, _scatter_start("scatter_start_" + tag, sums))
        return (started[tag][1][4],)

    grad_x, small_g = _local_step(
        x, loss_target, ffn1_norm_g, mix_norm_g, ffn2_norm_g, attn_q_norm_g, attn_k_norm_g, hgrn_out_norm_g,
        attn_rel_bias[0], hgrn_lower_bounds, weights, on_grads, grads_sent)

    def finish(tag, after):
        names, (send_sem, recv_sem, sums, lands, _) = started[tag]
        sums, lands = _scatter_wait("scatter_wait_" + tag, send_sem, recv_sem, sums, lands, after)
        return names, [_chip_sum("chip_sum_" + nm, sm, ld, chip) for nm, sm, ld in zip(names, sums, lands)]

    by_name = {nm: (w, m, v) for nm, w, m, v in zip(big_names, big_w, big_m, big_v)}
    updated = {}

    def update(names, halves, other_halves):
        for nm, mine, theirs in zip(names, halves, other_halves):
            w, m, v = by_name[nm]
            updated[nm] = _adamw("adamw_" + nm, w, mine, theirs, m, v, core)

    last_token = started["ffn1"][1][4]
    names_a, halves_a = finish("ffn2", last_token)
    names_m, halves_m = finish("mix", last_token)
    names_a, halves_a = names_a + names_m, halves_a + halves_m
    update(names_a, halves_a, _pair_join("pair_join_early", halves_a))
    names_b, halves_b = finish("ffn1", updated[names_a[-1]][1])
    others_b, small_all = _pair_join("pair_join_last", halves_b, small_g)
    update(names_b, halves_b, others_b)
    big_out = [updated[nm] for nm in big_names]

    pack = lambda g1, gm, g2, gq, gk, rel, lbp, go: _pack_small(g1, gm, g2, lbp, rel[0], gq, gk, go)
    small_w = pack(ffn1_norm_g, mix_norm_g, ffn2_norm_g, attn_q_norm_g, attn_k_norm_g, attn_rel_bias, hgrn_lower_bounds, hgrn_out_norm_g)
    small_m = pack(m_ffn1_norm_g, m_mix_norm_g, m_ffn2_norm_g, m_attn_q_norm_g, m_attn_k_norm_g, m_attn_rel_bias, m_hgrn_lower_bounds, m_hgrn_out_norm_g)
    small_v = pack(v_ffn1_norm_g, v_mix_norm_g, v_ffn2_norm_g, v_attn_q_norm_g, v_attn_k_norm_g, v_attn_rel_bias, v_hgrn_lower_bounds, v_hgrn_out_norm_g)
    small_res = _adamw_small("adamw_small", small_w, small_all, small_m, small_v)
    small_out = [_unpack_small(p, d) for p in small_res]
    loss = small_res[0].reshape(-1)[LOSS_SLOT]

    def assemble(kind):
        bg = [flip(nm, o[kind]) for nm, o in zip(big_names, big_out)]
        g1, gm, g2, gq, gk, rel, lbp, go = small_out[kind]
        return [g1, bg[0], bg[1], bg[2], gm, bg[3], gq, gk, rel, lbp, go, bg[4], g2, bg[5], bg[6], bg[7]]

    return (loss, grad_x, *assemble(0), *assemble(1), *assemble(2), *assemble(3))
```

```python
import functools

import jax
import jax.numpy as jnp
from jax import lax
from jax.experimental import pallas as pl
from jax.experimental.pallas import tpu as pltpu

F32 = jnp.float32
BF16 = jnp.bfloat16
MESH = pl.DeviceIdType.MESH

N_CHIPS = 4
N_DEV = 8
CHUNK = 64
ATTN_HEADS = 8
ATTN_DH = 64
ATTN_W = ATTN_HEADS * ATTN_DH
HGRN_HEADS = 4
HGRN_DH = 128
HGRN_W = HGRN_HEADS * HGRN_DH
LEFT_CHUNKS = 8
BAND = (LEFT_CHUNKS + 1) * CHUNK
KPAD = LEFT_CHUNKS * CHUNK
REL_CLIP = 128
N_REL = 2 * REL_CLIP + 1
N_REL_PAD = 384
RMS_EPS = 1e-6
LANES = 128
SMALL_ROWS = 8
SMALL_COLS = 1024

ADAM_LR = 0.001
ADAM_B1 = 0.9
ADAM_B2 = 0.999
ADAM_EPS = 1e-08
ADAM_WD = 0.01
ADAM_STEP = 10

NN = (((1,), (0,)), ((), ()))
NT = (((1,), (1,)), ((), ()))
TN = (((0,), (0,)), ((), ()))

VMEM_LIMIT = 48 * 1024 * 1024
MXU_WIDTH = 256
COL_CHUNK = 3 * MXU_WIDTH


def _sigmoid(x):
    return 1.0 / (1.0 + jnp.exp(-x))


def _silu(x):
    return x * _sigmoid(x)


def _dot(a, b, dims=NN):
    return lax.dot_general(a, b, dims, preferred_element_type=F32)


def _split3(x):
    hi = x.astype(BF16)
    r1 = x - hi.astype(F32)
    mid = r1.astype(BF16)
    lo = (r1 - mid.astype(F32)).astype(BF16)
    return hi, mid, lo


def _dot_exact_rhs(x, mat, dims=NN, pieces=3):
    hi, mid, lo = _split3(x)
    out = _dot(hi, mat, dims) + _dot(mid, mat, dims)
    return out + _dot(lo, mat, dims) if pieces == 3 else out


def _dot_exact_lhs(mat, x, dims=NN):
    hi, mid, lo = _split3(x)
    return _dot(mat, hi, dims) + _dot(mat, mid, dims) + _dot(mat, lo, dims)


def _params(*sem):
    return pltpu.CompilerParams(dimension_semantics=sem, vmem_limit_bytes=VMEM_LIMIT)


def _mm(name, ins, terms, n_acc, grid, acc_shape, outs, epilogue, extras=(), deps=()):
    nk = grid[2]
    ni, ne, nd, no = len(ins), len(extras), len(deps), len(outs)

    def body(*refs):
        in_refs = refs[:ni]
        ex_refs = refs[ni:ni + ne]
        out_refs = refs[ni + ne + nd:ni + ne + nd + no]
        acc_refs = refs[ni + ne + nd + no:]

        def products():
            parts = [None] * n_acc
            for ai, li, ri, dims in terms:
                d = _dot(in_refs[li][...], in_refs[ri][...], dims)
                parts[ai] = d if parts[ai] is None else parts[ai] + d
            return parts

        def finish(accs):
            res = epilogue(accs, [e[...] for e in ex_refs])
            for o, r in zip(out_refs, res):
                o[...] = r.astype(o.dtype)

        if nk == 1:
            finish(products())
        else:
            k = pl.program_id(2)

            @pl.when(k == 0)
            def _():
                for a, p in zip(acc_refs, products()):
                    a[...] = p

            if nk > 2:
                @pl.when(jnp.logical_and(k > 0, k < nk - 1))
                def _():
                    for a, p in zip(acc_refs, products()):
                        a[...] += p

            @pl.when(k == nk - 1)
            def _():
                finish([a[...] + p for a, p in zip(acc_refs, products())])

    scratch = [] if nk == 1 else [pltpu.VMEM(acc_shape, F32) for _ in range(n_acc)]
    res = pl.pallas_call(
        body,
        name=name,
        grid=grid,
        in_specs=[s for _, s in ins] + [s for _, s in extras] + [pl.BlockSpec(memory_space=pl.ANY)] * nd,
        out_specs=[s for _, s in outs],
        out_shape=[o for o, _ in outs],
        scratch_shapes=scratch,
        compiler_params=_params("parallel", "parallel", "arbitrary"),
    )(*[a for a, _ in ins], *[a for a, _ in extras], *deps)
    return res


def _staged_shape(w):
    return w.shape if len(w.shape) == 2 else (w.shape[1], w.shape[0] * w.shape[2])


def _stage_weights(w_hbm, w_vmem, sem):
    @pl.when(pl.program_id(0) == 0)
    def _():
        copies = []
        for p, (h, v) in enumerate(zip(w_hbm, w_vmem)):
            if len(h.shape) == 2:
                copies.append(pltpu.make_async_copy(h, v, sem.at[p, 0]))
            else:
                pj = h.shape[2]
                copies += [pltpu.make_async_copy(h.at[j], v.at[:, pl.ds(j * pj, pj)], sem.at[p, j])
                           for j in range(h.shape[0])]
        for cp in copies:
            cp.start()
        for cp in copies:
            cp.wait()


def _staging_scratch(weights):
    return [pltpu.VMEM(_staged_shape(w), w.dtype) for w in weights] + [pltpu.SemaphoreType.DMA((len(weights), N_CHIPS))]


def _mm_rows(name, lhs, weights, dims, t, outs, epilogue, extras=(), deps=()):
    tm = _row_tile(t)
    nl, ne, nd, no = len(lhs), len(extras), len(deps), len(outs)

    def body(*refs):
        lhs_refs = refs[:nl]
        w_hbm = refs[nl:2 * nl]
        ex_refs = refs[2 * nl:2 * nl + ne]
        out_refs = refs[2 * nl + ne + nd:2 * nl + ne + nd + no]
        w_vmem = refs[2 * nl + ne + nd + no:3 * nl + ne + nd + no]
        _stage_weights(w_hbm, w_vmem, refs[-1])

        acc = None
        for p in range(nl):
            part = _dot(lhs_refs[p][...], w_vmem[p][...], dims)
            acc = part if acc is None else acc + part
        res = epilogue([acc], [e[...] for e in ex_refs])
        for o, r in zip(out_refs, res):
            o[...] = r.astype(o.dtype)

    return pl.pallas_call(
        body,
        name=name,
        grid=(t // tm,),
        in_specs=[s for _, s in lhs] + [pl.BlockSpec(memory_space=pl.ANY)] * nl + [s for _, s in extras]
        + [pl.BlockSpec(memory_space=pl.ANY)] * nd,
        out_specs=[s for _, s in outs],
        out_shape=[o for o, _ in outs],
        scratch_shapes=_staging_scratch(weights),
        compiler_params=_params("arbitrary"),
    )(*[a for a, _ in lhs], *weights, *[a for a, _ in extras], *deps)


def _col_chunks(f):
    return [(c, min(COL_CHUNK, f - c)) for c in range(0, f, COL_CHUNK)]


def _mm_cols(name, x, weights, dims, n_out, epilogue, extras=(), deps=(), out_dtype=BF16):
    t, k = x.shape
    f = _staged_shape(weights[0])[0 if dims == NT else 1]
    tm = _row_tile(t)
    chunks = _col_chunks(f)
    nw, ne, nd = len(weights), len(extras), len(deps)

    def body(*refs):
        x_ref = refs[0]
        w_hbm = refs[1:1 + nw]
        ex_refs = refs[1 + nw:1 + nw + ne]
        out_refs = refs[1 + nw + ne + nd:1 + nw + ne + nd + n_out]
        w_vmem = refs[1 + nw + ne + nd + n_out:1 + 2 * nw + ne + nd + n_out]
        _stage_weights(w_hbm, w_vmem, refs[-1])

        xv = x_ref[...]

        def dots(c):
            c0, cw = chunks[c]
            return [_dot(xv, w[c0:c0 + cw, :] if dims == NT else w[:, c0:c0 + cw], dims) for w in w_vmem]

        accs = dots(0)
        for c, (c0, cw) in enumerate(chunks):
            nxt = dots(c + 1) if c + 1 < len(chunks) else None
            res = epilogue(accs, [e[:, c0:c0 + cw] for e in ex_refs])
            for o, r in zip(out_refs, res):
                o[:, c0:c0 + cw] = r.astype(o.dtype)
            accs = nxt

    act = pl.BlockSpec((tm, f), lambda i: (i, 0))
    return pl.pallas_call(
        body,
        name=name,
        grid=(t // tm,),
        in_specs=[pl.BlockSpec((tm, k), lambda i: (i, 0))] + [pl.BlockSpec(memory_space=pl.ANY)] * nw + [act] * ne
        + [pl.BlockSpec(memory_space=pl.ANY)] * nd,
        out_specs=[act] * n_out,
        out_shape=[jax.ShapeDtypeStruct((t, f), out_dtype)] * n_out,
        scratch_shapes=_staging_scratch(weights),
        compiler_params=_params("arbitrary"),
    )(x, *weights, *extras, *deps)


def _row_tile(t):
    return 512 if t % 512 == 0 else t


def _k_tile(t):
    return t if t <= 4096 else 1024


def _grad_k_tile(t):
    return 2048 if t % 2048 == 0 else t


def _rmsnorm(xv, g):
    ms = jnp.mean(xv * xv, axis=-1, keepdims=True)
    return xv * lax.rsqrt(ms + RMS_EPS) * g


def _rmsnorm_fwd(name, x, g):
    t, d = x.shape
    tm = _row_tile(t)

    def body(x_ref, g_ref, h_ref):
        h_ref[...] = _rmsnorm(x_ref[...], g_ref[...]).astype(BF16)

    return pl.pallas_call(
        body,
        name=name,
        grid=(t // tm,),
        in_specs=[pl.BlockSpec((tm, d), lambda i: (i, 0)), pl.BlockSpec((1, d), lambda i: (0, 0))],
        out_specs=pl.BlockSpec((tm, d), lambda i: (i, 0)),
        out_shape=jax.ShapeDtypeStruct((t, d), BF16),
        compiler_params=_params("parallel"),
    )(x, g)


def _norm_bwd_epilogue(copy_scale):
    def epilogue(accs, ex):
        dh = accs[0]
        xv, g, dres = ex
        ms = jnp.mean(xv * xv, axis=-1, keepdims=True)
        rstd = lax.rsqrt(ms + RMS_EPS)
        xhat = xv * rstd
        dxhat = dh * g
        dx = rstd * (dxhat - xhat * jnp.mean(dxhat * xhat, axis=-1, keepdims=True))
        out = dres + dx
        dg = jnp.sum(dh * xhat, axis=0, keepdims=True)
        if copy_scale is None:
            return out, dg
        return out, out * copy_scale, dg

    return epilogue


def _merged(w):
    return w.reshape(-1, w.shape[-1])


def _ffn_up(name, h, wg, wu, deps=()):
    def epilogue(accs, ex):
        a, b = accs
        sg = _sigmoid(a)
        act = a * sg
        return act, b * (sg * (1.0 + a * (1.0 - sg))), act * b

    return _mm_cols(name, h, [_merged(wg), _merged(wu)], NT, 3, epilogue, deps=deps)


def _whole_rows(arr, tm):
    return arr, pl.BlockSpec((tm, arr.shape[1]), lambda i: (i, 0))


def _ffn_down(name, z, wd, x, g_next, deps=()):
    t = z.shape[0]
    d = wd.shape[2]
    tm = _row_tile(t)
    row = pl.BlockSpec((tm, d), lambda i: (i, 0))

    def epilogue(accs, ex):
        y = ex[0] + 0.5 * accs[0]
        return y, _rmsnorm(y, ex[1])

    return _mm_rows(
        name, [_whole_rows(z, tm)], [_merged(wd)], NN, t,
        outs=[(jax.ShapeDtypeStruct((t, d), F32), row), (jax.ShapeDtypeStruct((t, d), BF16), row)],
        epilogue=epilogue,
        extras=[(x, row), (g_next, pl.BlockSpec((1, d), lambda i: (0, 0)))],
        deps=deps,
    )


def _ffn_down_loss(name, z, wd, x, target):
    t = z.shape[0]
    d = wd.shape[2]
    tm = _row_tile(t)
    nt = t // tm
    row = pl.BlockSpec((tm, d), lambda i: (i, 0))

    def epilogue(accs, ex):
        e = ex[0] + 0.5 * accs[0] - ex[1]
        dy = e * (1.0 / d)
        return dy, 0.5 * dy, jnp.sum(e * e, axis=0, keepdims=True)

    return _mm_rows(
        name, [_whole_rows(z, tm)], [_merged(wd)], NN, t,
        outs=[(jax.ShapeDtypeStruct((t, d), F32), row), (jax.ShapeDtypeStruct((t, d), BF16), row),
              (jax.ShapeDtypeStruct((nt, 1, d), F32), pl.BlockSpec((None, 1, d), lambda i: (i, 0, 0)))],
        epilogue=epilogue,
        extras=[(x, row), (target, row)],
    )


def _ffn_bwd_act(name, dout, wd, act_a, dact_b, deps=()):
    def epilogue(accs, ex):
        dz = accs[0]
        return dz * ex[1].astype(F32), dz * ex[0].astype(F32)

    return _mm_cols(name, dout, [_merged(wd)], NT, 2, epilogue, extras=[act_a, dact_b], deps=deps)


def _grad_w_cols(name, z, dout, deps=()):
    t, f = z.shape
    d = dout.shape[1]
    tk = _grad_k_tile(t)
    fh = f // 2
    dw = _mm(
        name,
        ins=[(z, pl.BlockSpec((tk, fh), lambda j, n, k: (k, j))),
             (dout, pl.BlockSpec((tk, d), lambda j, n, k: (k, 0)))],
        terms=[(0, 0, 1, TN)],
        n_acc=1,
        grid=(2, 1, t // tk),
        acc_shape=(fh, d),
        outs=[(pltpu.HBM((f, d), BF16), pl.BlockSpec((fh, d), lambda j, n, k: (j, 0)))],
        epilogue=lambda accs, ex: (accs[0],),
        deps=deps,
    )[0]
    return dw.reshape(N_CHIPS, f // N_CHIPS, d)


def _norm_bwd_outs(t, d, tm, copy_scale):
    row = pl.BlockSpec((tm, d), lambda i: (i, 0))
    outs = [(jax.ShapeDtypeStruct((t, d), F32), row)]
    if copy_scale is not None:
        outs.append((jax.ShapeDtypeStruct((t, d), BF16), row))
    outs.append((jax.ShapeDtypeStruct((t // tm, 1, d), F32), pl.BlockSpec((None, 1, d), lambda i: (i, 0, 0))))
    return row, outs


def _ffn_bwd_in(name, da, db, wg, wu, x, g, dres, copy_scale, deps=()):
    t = da.shape[0]
    d = wg.shape[2]
    tm = _row_tile(t)
    row, outs = _norm_bwd_outs(t, d, tm, copy_scale)
    return _mm_rows(
        name, [_whole_rows(da, tm), _whole_rows(db, tm)], [_merged(wg), _merged(wu)], NN, t,
        outs=outs,
        epilogue=_norm_bwd_epilogue(copy_scale),
        extras=[(x, row), (g, pl.BlockSpec((1, d), lambda i: (0, 0))), (dres, row)],
        deps=deps,
    )


def _in_proj(name, h, w_in):
    return _mm_cols(name, h, [w_in], NN, 1, lambda accs, ex: (accs[0],), out_dtype=F32)[0]


def _in_proj_bwd(name, dp, w_in, x, g, dres, copy_scale, deps=()):
    t = dp.shape[0]
    d = w_in.shape[1]
    tm = _row_tile(t)
    row, outs = _norm_bwd_outs(t, d, tm, copy_scale)
    return _mm_rows(
        name, [_whole_rows(dp, tm)], [w_in], NT, t,
        outs=outs,
        epilogue=_norm_bwd_epilogue(copy_scale),
        extras=[(x, row), (g, pl.BlockSpec((1, d), lambda i: (0, 0))), (dres, row)],
        deps=deps,
    )


def _grad_w_in(name, h, dp, ns):
    t, d = h.shape
    pj = dp.shape[1] // ns
    tk = _k_tile(t)
    return _mm(
        name,
        ins=[(h, pl.BlockSpec((tk, d), lambda j, n, k: (k, 0))),
             (dp, pl.BlockSpec((tk, pj), lambda j, n, k: (k, j)))],
        terms=[(0, 0, 1, TN)],
        n_acc=1,
        grid=(ns, 1, t // tk),
        acc_shape=(d, pj),
        outs=[(pltpu.HBM((ns, d, pj), BF16), pl.BlockSpec((None, d, pj), lambda j, n, k: (j, 0, 0)))],
        epilogue=lambda accs, ex: (accs[0],),
    )[0]


def _out_proj(name, mix, w_out, x, g_next):
    t, dm = mix.shape
    d = w_out.shape[1]
    tm = _row_tile(t)
    row = pl.BlockSpec((tm, d), lambda i, n, k: (i, 0))
    return _mm(
        name,
        ins=[(mix, pl.BlockSpec((tm, dm), lambda i, n, k: (i, 0))),
             (w_out, pl.BlockSpec((dm, d), lambda i, n, k: (0, 0)))],
        terms=[(0, 0, 1, NN)],
        n_acc=1,
        grid=(t // tm, 1, 1),
        acc_shape=(tm, d),
        outs=[(jax.ShapeDtypeStruct((t, d), F32), row), (jax.ShapeDtypeStruct((t, d), BF16), row)],
        epilogue=lambda accs, ex: (ex[0] + accs[0], _rmsnorm(ex[0] + accs[0], ex[1])),
        extras=[(x, row), (g_next, pl.BlockSpec((1, d), lambda i, n, k: (0, 0)))],
    )


def _out_proj_bwd(name, dx, w_out, deps=()):
    t, d = dx.shape
    dm = w_out.shape[0]
    tm = _row_tile(t)
    return _mm(
        name,
        ins=[(dx, pl.BlockSpec((tm, d), lambda i, n, k: (i, 0))),
             (w_out, pl.BlockSpec((dm, d), lambda i, n, k: (0, 0)))],
        terms=[(0, 0, 1, NT)],
        n_acc=1,
        grid=(t // tm, 1, 1),
        acc_shape=(tm, dm),
        outs=[(jax.ShapeDtypeStruct((t, dm), F32), pl.BlockSpec((tm, dm), lambda i, n, k: (i, 0)))],
        epilogue=lambda accs, ex: (accs[0],),
        deps=deps,
    )[0]


def _grad_w_out(name, mix, dx):
    t, dm = mix.shape
    d = dx.shape[1]
    tk = _k_tile(t)
    return _mm(
        name,
        ins=[(mix, pl.BlockSpec((tk, dm), lambda a, n, k: (k, 0))),
             (dx, pl.BlockSpec((tk, d), lambda a, n, k: (k, 0)))],
        terms=[(0, 0, 1, TN)],
        n_acc=1,
        grid=(1, 1, t // tk),
        acc_shape=(dm, d),
        outs=[(pltpu.HBM((dm, d), BF16), pl.BlockSpec((dm, d), lambda a, n, k: (0, 0)))],
        epilogue=lambda accs, ex: (accs[0],),
    )[0]


def _head_group_matrix():
    r = lax.broadcasted_iota(jnp.int32, (ATTN_W, ATTN_W), 0)
    c = lax.broadcasted_iota(jnp.int32, (ATTN_W, ATTN_W), 1)
    same = jnp.right_shift(r, 6) == jnp.right_shift(c, 6)
    return jnp.where(same, 1.0, 0.0).astype(BF16)


def _qk_prep(name, proj, gq, gk):
    b, s, _ = proj.shape
    tm = KPAD
    nb = s // tm

    def body(q_ref, k_ref, v_ref, gq_ref, gk_ref, qn_ref, kn_ref, vb_ref):
        j = pl.program_id(1)
        bd = _head_group_matrix()

        def norm(xv, g):
            ms = _dot_exact_rhs(xv * xv, bd, pieces=2) * (1.0 / ATTN_DH)
            return xv * lax.rsqrt(ms + RMS_EPS) * g

        @pl.when(j == 0)
        def _():
            kn_ref[...] = jnp.zeros_like(kn_ref)
            vb_ref[...] = jnp.zeros_like(vb_ref)

        @pl.when(j > 0)
        def _():
            qn_ref[...] = norm(q_ref[...], gq_ref[...]).astype(BF16)
            kn_ref[...] = norm(k_ref[...], gk_ref[...]).astype(BF16)
            vb_ref[...] = v_ref[...].astype(BF16)

    src_blk = lambda col: pl.BlockSpec((None, tm, ATTN_W), lambda bi, j: (bi, jnp.maximum(j - 1, 0), col))
    gspec = pl.BlockSpec((1, ATTN_W), lambda bi, j: (0, 0))
    padded = pl.BlockSpec((None, tm, ATTN_W), lambda bi, j: (bi, j, 0))
    return pl.pallas_call(
        body,
        name=name,
        grid=(b, nb + 1),
        in_specs=[src_blk(0), src_blk(1), src_blk(2), gspec, gspec],
        out_specs=[src_blk(0), padded, padded],
        out_shape=[jax.ShapeDtypeStruct((b, s, ATTN_W), BF16), jax.ShapeDtypeStruct((b, KPAD + s, ATTN_W), BF16),
                   jax.ShapeDtypeStruct((b, KPAD + s, ATTN_W), BF16)],
        compiler_params=_params("parallel", "arbitrary"),
    )(proj, proj, proj, gq, gk)


def _qk_prep_bwd(name, proj, dqn, dkn, dv, gq, gk):
    b, s, _ = proj.shape
    tm = KPAD
    nb = s // tm

    def body(q_ref, k_ref, dqn_ref, dkn_ref, dv_ref, gq_ref, gk_ref, dq_ref, dk_ref, dvb_ref, dgq_ref, dgk_ref):
        bd = _head_group_matrix()

        def bwd(xv, dy, g):
            ms = _dot_exact_rhs(xv * xv, bd, pieces=2) * (1.0 / ATTN_DH)
            rstd = lax.rsqrt(ms + RMS_EPS)
            xhat = xv * rstd
            dxhat = dy * g
            gm = _dot_exact_rhs(dxhat * xhat, bd, pieces=2) * (1.0 / ATTN_DH)
            return rstd * (dxhat - xhat * gm), jnp.sum(dy * xhat, axis=0, keepdims=True)

        dq, dgq = bwd(q_ref[...], dqn_ref[...], gq_ref[...])
        dk, dgk = bwd(k_ref[...], dkn_ref[...], gk_ref[...])
        dq_ref[...] = dq.astype(BF16)
        dk_ref[...] = dk.astype(BF16)
        dvb_ref[...] = dv_ref[...].astype(BF16)
        dgq_ref[...] = dgq
        dgk_ref[...] = dgk

    col = lambda c: pl.BlockSpec((None, tm, ATTN_W), lambda bi, j: (bi, j, c))
    past_pad = pl.BlockSpec((None, tm, ATTN_W), lambda bi, j: (bi, j + 1, 0))
    gspec = pl.BlockSpec((1, ATTN_W), lambda bi, j: (0, 0))
    pspec = pl.BlockSpec((None, 1, ATTN_W), lambda bi, j: (bi * nb + j, 0, 0))
    o_shape = jax.ShapeDtypeStruct((b, s, ATTN_W), BF16)
    p_shape = jax.ShapeDtypeStruct((b * nb, 1, ATTN_W), F32)
    return pl.pallas_call(
        body,
        name=name,
        grid=(b, nb),
        in_specs=[col(0), col(1), col(0), past_pad, past_pad, gspec, gspec],
        out_specs=[col(0)] * 3 + [pspec] * 2,
        out_shape=[o_shape] * 3 + [p_shape] * 2,
        compiler_params=_params("parallel", "parallel"),
    )(proj, proj, dqn, dkn, dv, gq, gk)


Q_CHUNKS = 4
QBLK = Q_CHUNKS * CHUNK
WIN = (LEFT_CHUNKS + Q_CHUNKS) * CHUNK
DB_W = BAND + CHUNK
MASKED = -1e30


def _band_table(bias):
    rows = [jnp.pad(bias, ((0, 0), (0, 0), (CHUNK * i, WIN - BAND - CHUNK * i)), constant_values=MASKED)
            for i in range(Q_CHUNKS)]
    return jnp.concatenate(rows, axis=1)


def _head_lanes(hh):
    lane = lax.broadcasted_iota(jnp.int32, (1, LANES), 1)
    return (lane < ATTN_DH) if hh == 0 else (lane >= ATTN_DH)


def _attn_probs(qh, kw, table, start):
    s = _dot(qh, kw, NT) * (ATTN_DH ** -0.5) + table
    col = lax.broadcasted_iota(jnp.int32, (QBLK, WIN), 1)
    s = jnp.where(col + start >= KPAD, s, MASKED)
    m = jnp.max(s, axis=-1, keepdims=True)
    p = jnp.exp(s - m)
    return p * (1.0 / jnp.sum(p, axis=-1, keepdims=True))


def _attn_fwd(name, q, k, v, table, deps=()):
    b, s, w = q.shape
    sp = k.shape[1]

    def body(q_ref, k_ref, v_ref, t_ref, *rest):
        o_ref = rest[-1]
        start = pl.multiple_of(pl.program_id(2) * QBLK, QBLK)
        kw = k_ref[pl.ds(start, WIN), :]
        vw = v_ref[pl.ds(start, WIN), :]
        q2 = q_ref[...]
        lanes = [_head_lanes(hh) for hh in range(2)]
        probs = [_attn_probs(jnp.where(mine, q2, jnp.zeros_like(q2)), kw, t_ref[hh], start).astype(BF16)
                 for hh, mine in enumerate(lanes)]
        outs = [_dot(p, vw) for p in probs]
        o_ref[...] = jnp.where(lanes[0], outs[0], outs[1]).astype(BF16)

    qspec = pl.BlockSpec((None, QBLK, LANES), lambda p, bi, i: (bi, i, p))
    kspec = pl.BlockSpec((None, sp, LANES), lambda p, bi, i: (bi, 0, p))
    return pl.pallas_call(
        body,
        name=name,
        grid=(w // LANES, b, s // QBLK),
        in_specs=[qspec, kspec, kspec, pl.BlockSpec((2, QBLK, WIN), lambda p, bi, i: (p, 0, 0))] + [ANY] * len(deps),
        out_specs=qspec,
        out_shape=jax.ShapeDtypeStruct((b, s, w), BF16),
        compiler_params=_params("parallel", "parallel", "arbitrary"),
    )(q, k, v, table, *deps)


def _attn_bwd(name, q, k, v, table, dmix):
    b, s, w = q.shape
    sp = k.shape[1]

    def body(q_ref, k_ref, v_ref, t_ref, do_ref, dq_ref, dk_ref, dv_ref, dbe_ref, dbo_ref):
        bi = pl.program_id(1)
        i = pl.program_id(2)
        start = pl.multiple_of(i * QBLK, QBLK)
        win = pl.ds(start, WIN)

        @pl.when(i == 0)
        def _():
            dk_ref[...] = jnp.zeros_like(dk_ref)
            dv_ref[...] = jnp.zeros_like(dv_ref)

        @pl.when(jnp.logical_and(i == 0, bi == 0))
        def _():
            dbe_ref[...] = jnp.zeros_like(dbe_ref)
            dbo_ref[...] = jnp.zeros_like(dbo_ref)

        kw = k_ref[win, :]
        vw = v_ref[win, :]
        q2 = q_ref[...]
        do2 = do_ref[...].astype(BF16)
        lanes = [_head_lanes(hh) for hh in range(2)]
        qh = [jnp.where(mine, q2, jnp.zeros_like(q2)) for mine in lanes]
        doh = [jnp.where(mine, do2, jnp.zeros_like(do2)) for mine in lanes]
        p = [_attn_probs(qh[hh], kw, t_ref[hh], start) for hh in range(2)]
        dp = [_dot(doh[hh], vw, NT) for hh in range(2)]
        ds = [p[hh] * (dp[hh] - jnp.sum(p[hh] * dp[hh], axis=-1, keepdims=True)) for hh in range(2)]
        dsb = [(x * (ATTN_DH ** -0.5)).astype(BF16) for x in ds]
        pb = [x.astype(BF16) for x in p]
        dq = [_dot(dsb[hh], kw) for hh in range(2)]
        dk = [_dot(dsb[hh], qh[hh], TN) for hh in range(2)]
        dv = [_dot(pb[hh], doh[hh], TN) for hh in range(2)]
        for hh in range(2):
            for qi in range(Q_CHUNKS):
                c0 = (qi // 2) * LANES
                blk = ds[hh][qi * CHUNK:(qi + 1) * CHUNK, c0:c0 + DB_W]
                if qi % 2 == 0:
                    dbe_ref[hh] += blk
                else:
                    dbo_ref[hh] += blk
        dq_ref[...] = jnp.where(lanes[0], dq[0], dq[1])
        dk_ref[win, :] += dk[0] + dk[1]
        dv_ref[win, :] += dv[0] + dv[1]

    qspec = pl.BlockSpec((None, QBLK, LANES), lambda p, bi, i: (bi, i, p))
    kspec = pl.BlockSpec((None, sp, LANES), lambda p, bi, i: (bi, 0, p))
    dbspec = pl.BlockSpec((2, CHUNK, DB_W), lambda p, bi, i: (p, 0, 0))
    db_shape = jax.ShapeDtypeStruct((ATTN_HEADS, CHUNK, DB_W), F32)
    return pl.pallas_call(
        body,
        name=name,
        grid=(w // LANES, b, s // QBLK),
        in_specs=[qspec, kspec, kspec, pl.BlockSpec((2, QBLK, WIN), lambda p, bi, i: (p, 0, 0)), qspec],
        out_specs=[qspec, kspec, kspec, dbspec, dbspec],
        out_shape=[jax.ShapeDtypeStruct((b, s, w), F32), jax.ShapeDtypeStruct((b, sp, w), F32),
                   jax.ShapeDtypeStruct((b, sp, w), F32), db_shape, db_shape],
        compiler_params=_params("arbitrary", "arbitrary", "arbitrary"),
    )(q, k, v, table, dmix)


HQ_COL = 3 * ATTN_W // HGRN_DH
HF_COL = HQ_COL + HGRN_HEADS
HI_COL = HF_COL + HGRN_HEADS
HG_COL = HI_COL + HGRN_HEADS
HGRN_ROWS = 8 * CHUNK
HEAD_LANES = [slice(hh * HGRN_DH, (hh + 1) * HGRN_DH) for hh in range(HGRN_HEADS)]


def _tri(lower):
    r = lax.broadcasted_iota(jnp.int32, (CHUNK, CHUNK), 0)
    c = lax.broadcasted_iota(jnp.int32, (CHUNK, CHUNK), 1)
    return (r >= c) if lower else (r <= c)


def _hgrn_chunk(hq, hf, lb, tril):
    sig = _sigmoid(hf)
    f = lb + (1.0 - lb) * sig
    g = jnp.log(f)
    ones_l = jnp.where(tril, 1.0, 0.0).astype(BF16)
    b = _dot_exact_lhs(ones_l, g)
    bl = jnp.sum(g, axis=0, keepdims=True)
    rows = lax.broadcasted_iota(jnp.int32, g.shape, 0)
    bm = jnp.sum(jnp.where(rows <= CHUNK // 2, g, 0.0), axis=0, keepdims=True)
    sq = _sigmoid(hq)
    q = hq * sq
    k = 1.0 - f
    return sig, f, b, bl, bm, sq, q, k


def _hgrn_fwd(name, proj, attn, lb, go, b, s):
    nc = s // CHUNK
    t = b * s
    nblk = s // HGRN_ROWS
    cpb = HGRN_ROWS // CHUNK

    def body(hq_ref, hf_ref, hi_ref, hg_ref, attn_ref, lb_ref, go_ref, mix_ref, oraw_ref, st_ref, s_scr):
        tril = _tri(True)
        gov = go_ref[...]
        mix_ref[:, 0:ATTN_W] = attn_ref[...]

        @pl.when(pl.program_id(1) == 0)
        def _():
            s_scr[...] = jnp.zeros_like(s_scr)

        def step(c, carry):
            sl = pl.ds(pl.multiple_of(c * CHUNK, CHUNK), CHUNK)
            hg = hg_ref[sl, :]
            _, _, bb, bl, bm, _, q, k = _hgrn_chunk(hq_ref[sl, :], hf_ref[sl, :], lb_ref[...], tril)
            vb = hi_ref[sl, :].astype(BF16)
            qe = (q * jnp.exp(bb - bm)).astype(BF16)
            ke = (k * jnp.exp(bm - bb)).astype(BF16)
            qb = (q * jnp.exp(bb)).astype(BF16)
            kb = (k * jnp.exp(bl - bb)).astype(BF16)
            e_last = jnp.exp(bl)
            gate = _silu(hg)
            st = [s_scr[hh] for hh in range(HGRN_HEADS)]
            a = [jnp.where(tril, _dot(qe[:, hs], ke[:, hs], NT), 0.0).astype(BF16) for hs in HEAD_LANES]
            o_state = [_dot(qb[:, hs], st[hh].astype(BF16), NT) for hh, hs in enumerate(HEAD_LANES)]
            st_next = [st[hh] * e_last[:, hs] + _dot(vb[:, hs], kb[:, hs], TN) for hh, hs in enumerate(HEAD_LANES)]
            o = [_dot(a[hh], vb[:, hs]) + o_state[hh] for hh, hs in enumerate(HEAD_LANES)]
            ro = [(oh * lax.rsqrt(jnp.mean(oh * oh, axis=-1, keepdims=True) + RMS_EPS) * gov) * gate[:, hs]
                  for oh, hs in zip(o, HEAD_LANES)]
            for hh in range(HGRN_HEADS):
                st_ref[hh, c] = st[hh]
                s_scr[hh] = st_next[hh]
            mix_ref[sl, ATTN_W:ATTN_W + HGRN_W] = jnp.concatenate(ro, axis=1).astype(BF16)
            oraw_ref[sl, :] = jnp.concatenate(o, axis=1)
            return carry

        lax.fori_loop(0, cpb, step, 0)

    col = lambda base: pl.BlockSpec((HGRN_ROWS, HGRN_W), lambda bi, i: (bi * nblk + i, base // HGRN_HEADS))
    out = pl.BlockSpec((HGRN_ROWS, HGRN_W), lambda bi, i: (bi * nblk + i, 0))
    return pl.pallas_call(
        body,
        name=name,
        grid=(b, nblk),
        in_specs=[col(HQ_COL), col(HF_COL), col(HI_COL), col(HG_COL), out,
                  pl.BlockSpec((1, HGRN_W), lambda bi, i: (0, 0)), pl.BlockSpec((1, HGRN_DH), lambda bi, i: (0, 0))],
        out_specs=[pl.BlockSpec((HGRN_ROWS, ATTN_W + HGRN_W), lambda bi, i: (bi * nblk + i, 0)), out,
                   pl.BlockSpec((None, HGRN_HEADS, cpb, HGRN_DH, HGRN_DH), lambda bi, i: (bi, 0, i, 0, 0))],
        out_shape=[jax.ShapeDtypeStruct((t, ATTN_W + HGRN_W), BF16), jax.ShapeDtypeStruct((t, HGRN_W), F32),
                   jax.ShapeDtypeStruct((b, HGRN_HEADS, nc, HGRN_DH, HGRN_DH), F32)],
        scratch_shapes=[pltpu.VMEM((HGRN_HEADS, HGRN_DH, HGRN_DH), F32)],
        compiler_params=_params("parallel", "arbitrary"),
    )(proj, proj, proj, proj, attn, lb, go)


def _hgrn_bwd(name, proj, dqkv, lb, go, oraw, states, dmix, b, s):
    t = b * s
    nblk = s // HGRN_ROWS
    cpb = HGRN_ROWS // CHUNK

    def body(hq_ref, hf_ref, hi_ref, hg_ref, dq_ref, dk_ref, dv_ref, lb_ref, go_ref, oraw_ref, st_ref, dro_ref,
             dp_ref, dlb_ref, dgo_ref, ds_scr, dlb_scr, dgo_scr):
        tril = _tri(True)
        ones_u = jnp.where(_tri(False), 1.0, 0.0).astype(BF16)
        gov = go_ref[...]
        dp_ref[:, 0:ATTN_W] = dq_ref[...]
        dp_ref[:, ATTN_W:2 * ATTN_W] = dk_ref[...]
        dp_ref[:, 2 * ATTN_W:3 * ATTN_W] = dv_ref[...]

        @pl.when(pl.program_id(1) == 0)
        def _():
            ds_scr[...] = jnp.zeros_like(ds_scr)
            dlb_scr[...] = jnp.zeros_like(dlb_scr)
            dgo_scr[...] = jnp.zeros_like(dgo_scr)

        def step(ci, carry):
            c = cpb - 1 - ci
            sl = pl.ds(pl.multiple_of(c * CHUNK, CHUNK), CHUNK)
            hq = hq_ref[sl, :]
            hg = hg_ref[sl, :]
            sig, f, bb, bl, bm, sq, q, k = _hgrn_chunk(hq, hf_ref[sl, :], lb_ref[...], tril)
            vb = hi_ref[sl, :].astype(BF16)
            ebm = jnp.exp(bb - bm)
            embm = jnp.exp(bm - bb)
            eb = jnp.exp(bb)
            ebl = jnp.exp(bl - bb)
            e_last = jnp.exp(bl)
            qe = (q * ebm).astype(BF16)
            ke = (k * embm).astype(BF16)
            qb = (q * eb).astype(BF16)
            kb = (k * ebl).astype(BF16)
            st = [st_ref[hh, c] for hh in range(HGRN_HEADS)]
            dst = [ds_scr[hh] for hh in range(HGRN_HEADS)]
            o = oraw_ref[sl, :]
            dro = dro_ref[sl, :]
            sg = _sigmoid(hg)
            gov4 = jnp.concatenate([gov] * HGRN_HEADS, axis=1)
            rstd = jnp.concatenate(
                [jnp.broadcast_to(lax.rsqrt(jnp.mean(o[:, hs] * o[:, hs], axis=-1, keepdims=True) + RMS_EPS),
                                  (CHUNK, HGRN_DH)) for hs in HEAD_LANES], axis=1)
            ohat = o * rstd
            dn = dro * (hg * sg)
            dhg = dro * (ohat * gov4) * (sg * (1.0 + hg * (1.0 - sg)))
            dgo_inc = jnp.sum(dn * ohat, axis=0, keepdims=True)
            dohat = dn * gov4
            proj_h = dohat * ohat
            pm = jnp.concatenate(
                [jnp.broadcast_to(jnp.mean(proj_h[:, hs], axis=-1, keepdims=True), (CHUNK, HGRN_DH))
                 for hs in HEAD_LANES], axis=1)
            dob = (rstd * (dohat - ohat * pm)).astype(BF16)
            stb = [x.astype(BF16) for x in st]
            dstb = [x.astype(BF16) for x in dst]
            a = [jnp.where(tril, _dot(qe[:, hs], ke[:, hs], NT), 0.0).astype(BF16) for hs in HEAD_LANES]
            dab = [jnp.where(tril, _dot(dob[:, hs], vb[:, hs], NT), 0.0).astype(BF16) for hs in HEAD_LANES]
            dqb = [_dot(dob[:, hs], stb[hh]) for hh, hs in enumerate(HEAD_LANES)]
            dkb = [_dot(vb[:, hs], dstb[hh]) for hh, hs in enumerate(HEAD_LANES)]
            dv_state = [_dot(kb[:, hs], dstb[hh], NT) for hh, hs in enumerate(HEAD_LANES)]
            dst_next = [dst[hh] * e_last[:, hs] + _dot(dob[:, hs], qb[:, hs], TN) for hh, hs in enumerate(HEAD_LANES)]
            dv = [_dot(a[hh], dob[:, hs], TN) + dv_state[hh] for hh, hs in enumerate(HEAD_LANES)]
            dqe = jnp.concatenate([_dot(dab[hh], ke[:, hs]) for hh, hs in enumerate(HEAD_LANES)], axis=1)
            dke = jnp.concatenate([_dot(dab[hh], qe[:, hs], TN) for hh, hs in enumerate(HEAD_LANES)], axis=1)
            dqb = jnp.concatenate(dqb, axis=1)
            dkb = jnp.concatenate(dkb, axis=1)
            state_term = jnp.concatenate(
                [jnp.sum(dst[hh] * st[hh], axis=0, keepdims=True) for hh in range(HGRN_HEADS)], axis=1)
            dq = dqe * ebm + dqb * eb
            dk = dke * embm + dkb * ebl
            db = (qe.astype(F32) * dqe - ke.astype(F32) * dke) + q * (dqb * eb) - k * (dkb * ebl)
            d_last = jnp.sum(k * ebl * dkb, axis=0, keepdims=True) + state_term * e_last
            dg = _dot_exact_lhs(ones_u, db) + d_last
            df = dg / f - dk
            first = HQ_COL * HGRN_DH
            dp_ref[sl, first:first + HGRN_W] = (dq * (sq * (1.0 + hq * (1.0 - sq)))).astype(BF16)
            dp_ref[sl, first + HGRN_W:first + 2 * HGRN_W] = (df * (1.0 - lb_ref[...]) * sig * (1.0 - sig)).astype(BF16)
            dp_ref[sl, first + 2 * HGRN_W:first + 3 * HGRN_W] = jnp.concatenate(dv, axis=1).astype(BF16)
            dp_ref[sl, first + 3 * HGRN_W:first + 4 * HGRN_W] = dhg.astype(BF16)
            dlb_scr[...] += jnp.sum(df * (1.0 - sig), axis=0, keepdims=True)
            dgo_scr[...] += dgo_inc
            for hh in range(HGRN_HEADS):
                ds_scr[hh] = dst_next[hh]
            return carry

        lax.fori_loop(0, cpb, step, 0)

        @pl.when(pl.program_id(1) == nblk - 1)
        def _():
            dlb_ref[...] = dlb_scr[...]
            dgo_ref[...] = dgo_scr[...]

    rows = lambda bi, i: bi * nblk + (nblk - 1 - i)
    col = lambda base: pl.BlockSpec((HGRN_ROWS, HGRN_W), lambda bi, i: (rows(bi, i), base // HGRN_HEADS))
    out = pl.BlockSpec((HGRN_ROWS, HGRN_W), lambda bi, i: (rows(bi, i), 0))
    part = pl.BlockSpec((None, 1, HGRN_W), lambda bi, i: (bi, 0, 0))
    width = HG_COL * HGRN_DH + HGRN_W
    o_shape = jax.ShapeDtypeStruct((t, width), BF16)
    p_shape = jax.ShapeDtypeStruct((b, 1, HGRN_W), F32)
    return pl.pallas_call(
        body,
        name=name,
        grid=(b, nblk),
        in_specs=[col(HQ_COL), col(HF_COL), col(HI_COL), col(HG_COL), out, out, out,
                  pl.BlockSpec((1, HGRN_W), lambda bi, i: (0, 0)), pl.BlockSpec((1, HGRN_DH), lambda bi, i: (0, 0)), out,
                  pl.BlockSpec((None, HGRN_HEADS, cpb, HGRN_DH, HGRN_DH), lambda bi, i: (bi, 0, nblk - 1 - i, 0, 0)),
                  col(ATTN_W // HGRN_DH)],
        out_specs=[pl.BlockSpec((HGRN_ROWS, width), lambda bi, i: (rows(bi, i), 0))] + [part] * 2,
        out_shape=[o_shape] + [p_shape] * 2,
        scratch_shapes=[pltpu.VMEM((HGRN_HEADS, HGRN_DH, HGRN_DH), F32), pltpu.VMEM((1, HGRN_W), F32),
                        pltpu.VMEM((1, HGRN_W), F32)],
        compiler_params=_params("parallel", "arbitrary"),
    )(proj, proj, proj, proj, *dqkv, lb, go, oraw, states, dmix)


def _small_grads(name, dg1, dgm, dg2, dgq, dgk, dbe_t, dbo_t, dlb, dgo, lbp):
    d = dg1.shape[1]

    def body(dg1_ref, dgm_ref, dg2_ref, dgq_ref, dgk_ref, dbe_ref, dbo_ref, dlb_ref, dgo_ref, lbp_ref,
             g1_ref, gm_ref, g2_ref, gq_ref, gk_ref, rb_ref, lbg_ref, go_ref):
        g1_ref[...] = jnp.sum(dg1_ref[...], axis=0, keepdims=True)
        gm_ref[...] = jnp.sum(dgm_ref[...], axis=0, keepdims=True)
        g2_ref[...] = jnp.sum(dg2_ref[...], axis=0, keepdims=True)
        r = lax.broadcasted_iota(jnp.int32, (ATTN_W, ATTN_DH), 0)
        cidx = lax.broadcasted_iota(jnp.int32, (ATTN_W, ATTN_DH), 1)
        fold = jnp.where(jnp.bitwise_and(r, ATTN_DH - 1) == cidx, 1.0, 0.0).astype(BF16)
        gq_ref[...] = jnp.sum(_dot_exact_rhs(dgq_ref[...], fold), axis=0, keepdims=True)
        gk_ref[...] = jnp.sum(_dot_exact_rhs(dgk_ref[...], fold), axis=0, keepdims=True)
        gosum = jnp.sum(dgo_ref[...], axis=0, keepdims=True)
        go_ref[...] = (gosum[:, 0:HGRN_DH] + gosum[:, HGRN_DH:2 * HGRN_DH]
                       + gosum[:, 2 * HGRN_DH:3 * HGRN_DH] + gosum[:, 3 * HGRN_DH:4 * HGRN_DH])
        p0 = lbp_ref[0:1, :]
        p1 = lbp_ref[1:2, :]
        lbv = 1.0 / (1.0 + jnp.exp(p1 - p0))
        dp0 = jnp.sum(dlb_ref[...], axis=0, keepdims=True) * lbv * (1.0 - lbv)
        lbg_ref[0:1, :] = dp0
        lbg_ref[1:2, :] = -dp0
        acc = dbe_ref[CHUNK - 1] + pltpu.roll(dbo_ref[CHUNK - 1], DB_W - CHUNK, 1)
        for tq in range(CHUNK - 1):
            acc = acc + pltpu.roll(dbe_ref[tq], CHUNK - 1 - tq, 1) + pltpu.roll(dbo_ref[tq], DB_W - 1 - tq, 1)
        jidx = lax.broadcasted_iota(jnp.int32, (DB_W, N_REL_PAD), 0)
        ridx = lax.broadcasted_iota(jnp.int32, (DB_W, N_REL_PAD), 1)
        rel = jnp.clip(KPAD + CHUNK - 1 - jidx, -REL_CLIP, REL_CLIP) + REL_CLIP
        rb_ref[...] = _dot_exact_rhs(acc, jnp.where(rel == ridx, 1.0, 0.0).astype(BF16))

    ins = [dg1, dgm, dg2, dgq, dgk, dbe_t, dbo_t, dlb, dgo, lbp]
    outs = [jax.ShapeDtypeStruct((1, d), F32)] * 3 + [jax.ShapeDtypeStruct((1, ATTN_DH), F32)] * 2 + [
        jax.ShapeDtypeStruct((ATTN_HEADS, N_REL_PAD), F32), jax.ShapeDtypeStruct((2, HGRN_W), F32),
        jax.ShapeDtypeStruct((1, HGRN_DH), F32)]
    vm = pl.BlockSpec(memory_space=pltpu.VMEM)
    return pl.pallas_call(
        body,
        name=name,
        in_specs=[vm] * len(ins),
        out_specs=[vm] * len(outs),
        out_shape=outs,
        compiler_params=pltpu.CompilerParams(vmem_limit_bytes=VMEM_LIMIT),
    )(*ins)


def _adam_update(w, g, m, v):
    m2 = ADAM_B1 * m + (1.0 - ADAM_B1) * g
    v2 = ADAM_B2 * v + (1.0 - ADAM_B2) * (g * g)
    m_hat = m2 / (1.0 - ADAM_B1 ** ADAM_STEP)
    v_hat = v2 / (1.0 - ADAM_B2 ** ADAM_STEP)
    delta = -ADAM_LR * (m_hat / (jnp.sqrt(v_hat) + ADAM_EPS) + ADAM_WD * w)
    return delta, m2, v2


def _rows_tile(r):
    return r if r <= 512 or r % 512 else 512


def _pair_sum(name, grad, theirs, core):
    n, half, c = theirs.shape
    tr = _rows_tile(half)
    nth = half // tr

    def body(core_ref, a_ref, b_ref, o_ref):
        o_ref[...] = (a_ref[...].astype(F32) + b_ref[...].astype(F32)).astype(o_ref.dtype)

    spec = pl.BlockSpec((None, tr, c), lambda i, j, core_ref: (i, j, 0))
    return pl.pallas_call(
        body, name=name,
        grid_spec=pltpu.PrefetchScalarGridSpec(
            num_scalar_prefetch=1, grid=(n, nth),
            in_specs=[pl.BlockSpec((None, tr, c), lambda i, j, core_ref: (i, core_ref[0] * nth + j, 0)), spec],
            out_specs=spec),
        out_shape=pltpu.HBM((n, half, c), BF16), compiler_params=_params("parallel", "parallel"),
    )(core, grad, theirs)


def _chip_sum(name, own, parts, chip):
    _, half, c = own.shape
    tr = _rows_tile(half)

    def body(chip_ref, own_ref, p_ref, o_ref):
        me = chip_ref[0]
        mine = own_ref[...].astype(F32)
        flip_x, flip_y, flip_xy = (p_ref[i].astype(F32) for i in range(3))
        acc = None
        for k in range(N_CHIPS):
            rel = jnp.bitwise_xor(me, k)
            term = jnp.where(rel == 0, mine, jnp.where(rel == 2, flip_x, jnp.where(rel == 1, flip_y, flip_xy)))
            acc = term if acc is None else acc + term
        o_ref[...] = acc

    return pl.pallas_call(
        body, name=name,
        grid_spec=pltpu.PrefetchScalarGridSpec(
            num_scalar_prefetch=1, grid=(half // tr,),
            in_specs=[pl.BlockSpec((None, tr, c), lambda j, chip_ref: (chip_ref[0], j, 0)),
                      pl.BlockSpec((3, tr, c), lambda j, chip_ref: (0, j, 0))],
            out_specs=pl.BlockSpec((tr, c), lambda j, chip_ref: (j, 0))),
        out_shape=pltpu.HBM((half, c), F32), compiler_params=_params("parallel"),
    )(chip, own, parts)


def _adamw(name, w, g_mine, g_theirs, m, v, core):
    _, r, c = w.shape
    half = r // 2
    tr = _rows_tile(half)
    nth = half // tr

    def body(core_ref, w_ref, gm_ref, gt_ref, m_ref, v_ref, g_ref, d_ref, m2_ref, v2_ref):
        g = jnp.where(pl.program_id(0) == core_ref[0], gm_ref[...], gt_ref[...])
        delta, m2, v2 = _adam_update(w_ref[...], g, m_ref[...], v_ref[...])
        g_ref[...] = g
        d_ref[...] = delta
        m2_ref[...] = m2
        v2_ref[...] = v2

    full = pl.BlockSpec((None, tr, c), lambda h, j, core_ref: (0, h * nth + j, 0))
    part = pl.BlockSpec((tr, c), lambda h, j, core_ref: (j, 0))
    shape = jax.ShapeDtypeStruct((1, r, c), F32)
    return pl.pallas_call(
        body, name=name,
        grid_spec=pltpu.PrefetchScalarGridSpec(
            num_scalar_prefetch=1, grid=(2, nth), in_specs=[full, part, part, full, full], out_specs=[full] * 4),
        out_shape=[shape] * 4, compiler_params=_params("parallel", "parallel"),
    )(core, w, g_mine, g_theirs, m, v)


def _rel_bias_table(name, rel_bias):
    padded = jnp.pad(rel_bias, ((0, 0), (0, N_REL_PAD - N_REL)))

    def body(rb_ref, o_ref):
        ridx = lax.broadcasted_iota(jnp.int32, (N_REL_PAD, BAND), 0)
        sidx = lax.broadcasted_iota(jnp.int32, (N_REL_PAD, BAND), 1)
        rb = rb_ref[...]

        def step(tq, carry):
            rel = jnp.clip(tq + KPAD - sidx, -REL_CLIP, REL_CLIP) + REL_CLIP
            onehot = jnp.where(rel == ridx, 1.0, 0.0).astype(BF16)
            o_ref[tq] = _dot_exact_rhs(rb, onehot)
            return carry

        lax.fori_loop(0, CHUNK, step, 0)

    vm = pl.BlockSpec(memory_space=pltpu.VMEM)
    table = pl.pallas_call(
        body, name=name, in_specs=[vm], out_specs=vm,
        out_shape=jax.ShapeDtypeStruct((CHUNK, ATTN_HEADS, BAND), F32),
    )(padded)
    return table.transpose(1, 0, 2)


def _adamw_small(name, w, parts, m, v):
    def body(w_ref, p_ref, m_ref, v_ref, g_ref, d_ref, m2_ref, v2_ref):
        g = p_ref[0]
        for i in range(1, N_DEV):
            g = g + p_ref[i]
        delta, m2, v2 = _adam_update(w_ref[...], g, m_ref[...], v_ref[...])
        g_ref[...] = g
        d_ref[...] = delta
        m2_ref[...] = m2
        v2_ref[...] = v2

    vm = pl.BlockSpec(memory_space=pltpu.VMEM)
    shape = jax.ShapeDtypeStruct((SMALL_ROWS, SMALL_COLS), F32)
    return pl.pallas_call(
        body, name=name, in_specs=[vm] * 4, out_specs=[vm] * 4, out_shape=[shape] * 4,
    )(w, parts, m, v)


def _position():
    return lax.axis_index("x"), lax.axis_index("y"), lax.axis_index("c")


def _other_chips(x, y):
    return [(1 - x, y), (x, 1 - y), (1 - x, 1 - y)]


ANY = pl.BlockSpec(memory_space=pl.ANY)
PAIR_ID = 0


def _pair_handshake():
    x, y, c = _position()
    barrier = pltpu.get_barrier_semaphore()
    pl.semaphore_signal(barrier, inc=1, device_id=(x, y, 1 - c), device_id_type=MESH)
    pl.semaphore_wait(barrier, 1)


PAIR_CALL = pltpu.CompilerParams(collective_id=PAIR_ID)


HBM = pl.BlockSpec(memory_space=pltpu.HBM)
SEM = pl.BlockSpec(memory_space=pltpu.SEMAPHORE)
SPLIT_COPY = pltpu.SideEffectType.DATAFLOW_SIDE_EFFECTING


def _gather_copy(shards, outs, send_sem, recv_sem, i, j):
    x, y, c = _position()
    chips = _other_chips(x, y)
    half = shards[i].shape[0] // 2
    rows = pl.ds(pl.multiple_of(c * half, 16), half)
    return pltpu.make_async_remote_copy(
        src_ref=shards[i].at[rows, :], dst_ref=outs[i].at[2 * x + y, rows, :],
        send_sem=send_sem.at[3 * i + j], recv_sem=recv_sem.at[3 * i + j],
        device_id=(chips[j][0], chips[j][1], c), device_id_type=MESH)


def _gather_start(name, shards, after):
    n = len(shards)

    def body(*refs):
        srcs, outs = refs[:n], refs[n:2 * n]
        send_sem, recv_sem = refs[2 * n + len(after)], refs[2 * n + len(after) + 1]
        token = refs[-1]
        for i in range(n):
            for j in range(3):
                _gather_copy(srcs, outs, send_sem, recv_sem, i, j).start()
        token[...] = jnp.zeros_like(token)

    full = [(N_CHIPS,) + s.shape for s in shards]
    res = pl.pallas_call(
        body,
        name=name,
        in_specs=[HBM] * (2 * n) + [ANY] * len(after),
        out_specs=[SEM, SEM] + [HBM] * (2 * n) + [pl.BlockSpec(memory_space=pltpu.VMEM)],
        out_shape=[pltpu.SemaphoreType.DMA((3 * n,)), pltpu.SemaphoreType.DMA((3 * n,))]
        + [pltpu.HBM(s.shape, s.dtype) for s in shards]
        + [pltpu.HBM(shp, s.dtype) for shp, s in zip(full, shards)]
        + [jax.ShapeDtypeStruct((8, LANES), F32)],
        input_output_aliases={i: 2 + i for i in range(2 * n)},
        compiler_params=pltpu.CompilerParams(has_side_effects=SPLIT_COPY),
    )(*[pltpu.with_memory_space_constraint(s, pltpu.HBM) for s in shards],
      *[pltpu.with_memory_space_constraint(lax.empty(shp, s.dtype), pltpu.HBM) for shp, s in zip(full, shards)],
      *after)
    return res[0], res[1], list(res[2:2 + n]), list(res[2 + n:2 + 2 * n]), res[-1]


def _gather_wait(name, send_sem, recv_sem, shards, outs, after):
    n = len(shards)

    def body(*refs):
        srcs, out_refs = refs[:n], refs[n:2 * n]
        send_ref, recv_ref = refs[2 * n], refs[2 * n + 1]
        for i in range(n):
            for j in range(3):
                copy = _gather_copy(srcs, out_refs, send_ref, recv_ref, i, j)
                copy.wait_send()
                copy.wait_recv()

    res = pl.pallas_call(
        body,
        name=name,
        in_specs=[HBM] * (2 * n) + [SEM, SEM] + [ANY] * len(after),
        out_specs=[HBM] * (2 * n),
        out_shape=[pltpu.HBM(s.shape, s.dtype) for s in shards] + [pltpu.HBM(o.shape, o.dtype) for o in outs],
        input_output_aliases={i: i for i in range(2 * n)},
        compiler_params=pltpu.CompilerParams(has_side_effects=SPLIT_COPY),
    )(*shards, *outs, send_sem, recv_sem, *after)
    return list(res[:n]), list(res[n:])


def _join_copies(srcs, ins, outs, own_send, own_recv, half_send, half_recv):
    x, y, c = _position()
    chips = _other_chips(x, y)
    copies = []
    for i in range(len(srcs)):
        copies.append(pltpu.make_async_remote_copy(
            src_ref=srcs[i], dst_ref=outs[i].at[2 * x + y], send_sem=own_send.at[i], recv_sem=own_recv.at[i],
            device_id=(x, y, 1 - c), device_id_type=MESH))
        half = srcs[i].shape[0] // 2
        rows = pl.ds(pl.multiple_of(c * half, 16), half)
        for j in range(3):
            slot = 2 * chips[j][0] + chips[j][1]
            copies.append(pltpu.make_async_remote_copy(
                src_ref=ins[i].at[slot, rows, :], dst_ref=outs[i].at[slot, rows, :],
                send_sem=half_send.at[3 * i + j], recv_sem=half_recv.at[3 * i + j],
                device_id=(x, y, 1 - c), device_id_type=MESH))
    return copies


def _gather_join(name, shards, outs):
    n = len(shards)

    def body(*refs):
        _pair_handshake()
        copies = _join_copies(refs[:n], refs[n:2 * n], refs[2 * n:3 * n], *refs[3 * n:])
        for cp in copies:
            cp.start()
        for cp in copies:
            cp.wait()

    return pl.pallas_call(
        body,
        name=name,
        in_specs=[ANY] * (2 * n),
        out_specs=[HBM] * n,
        out_shape=[pltpu.HBM(o.shape, o.dtype) for o in outs],
        input_output_aliases={n + i: i for i in range(n)},
        scratch_shapes=[pltpu.SemaphoreType.DMA((n,))] * 2 + [pltpu.SemaphoreType.DMA((3 * n,))] * 2,
        compiler_params=PAIR_CALL,
    )(*shards, *outs)


def _join_start(name, shards, outs):
    n = len(shards)

    def body(*refs):
        _pair_handshake()
        srcs, arrs = refs[:n], refs[n:2 * n]
        sems = refs[2 * n:2 * n + 4]
        token = refs[-1]
        for cp in _join_copies(srcs, arrs, arrs, *sems):
            cp.start()
        token[...] = jnp.zeros_like(token)

    res = pl.pallas_call(
        body,
        name=name,
        in_specs=[HBM] * (2 * n),
        out_specs=[SEM] * 4 + [HBM] * (2 * n) + [pl.BlockSpec(memory_space=pltpu.VMEM)],
        out_shape=[pltpu.SemaphoreType.DMA((n,))] * 2 + [pltpu.SemaphoreType.DMA((3 * n,))] * 2
        + [pltpu.HBM(s.shape, s.dtype) for s in shards] + [pltpu.HBM(o.shape, o.dtype) for o in outs]
        + [jax.ShapeDtypeStruct((8, LANES), F32)],
        input_output_aliases={i: 4 + i for i in range(2 * n)},
        compiler_params=pltpu.CompilerParams(has_side_effects=SPLIT_COPY, collective_id=PAIR_ID),
    )(*shards, *outs)
    return list(res[:4]), list(res[4:4 + n]), list(res[4 + n:4 + 2 * n]), res[-1]


def _join_wait(name, sems, shards, outs, after):
    n = len(shards)

    def body(*refs):
        srcs, arrs = refs[:n], refs[n:2 * n]
        for cp in _join_copies(srcs, arrs, arrs, *refs[2 * n:2 * n + 4]):
            cp.wait_send()
            cp.wait_recv()

    res = pl.pallas_call(
        body,
        name=name,
        in_specs=[HBM] * (2 * n) + [SEM] * 4 + [ANY] * len(after),
        out_specs=[HBM] * (2 * n),
        out_shape=[pltpu.HBM(s.shape, s.dtype) for s in shards] + [pltpu.HBM(o.shape, o.dtype) for o in outs],
        input_output_aliases={i: i for i in range(2 * n)},
        compiler_params=pltpu.CompilerParams(has_side_effects=SPLIT_COPY),
    )(*shards, *outs, *sems, *after)
    return list(res[n:])


def _pair_copy(grads, lands, send_sem, recv_sem, i):
    x, y, c = _position()
    half = grads[i].shape[1] // 2
    give = pl.ds(pl.multiple_of((1 - c) * half, 16), half)
    return pltpu.make_async_remote_copy(
        src_ref=grads[i].at[:, give, :], dst_ref=lands[i], send_sem=send_sem.at[i], recv_sem=recv_sem.at[i],
        device_id=(x, y, 1 - c), device_id_type=MESH)


def _pair_start(name, grads):
    n = len(grads)

    def body(*refs):
        _pair_handshake()
        srcs, lands = refs[:n], refs[n:2 * n]
        send_sem, recv_sem = refs[2 * n], refs[2 * n + 1]
        token = refs[-1]
        for i in range(n):
            _pair_copy(srcs, lands, send_sem, recv_sem, i).start()
        token[...] = jnp.zeros_like(token)

    halves = [(g.shape[0], g.shape[1] // 2, g.shape[2]) for g in grads]
    res = pl.pallas_call(
        body,
        name=name,
        in_specs=[HBM] * (2 * n),
        out_specs=[SEM, SEM] + [HBM] * (2 * n) + [pl.BlockSpec(memory_space=pltpu.VMEM)],
        out_shape=[pltpu.SemaphoreType.DMA((n,)), pltpu.SemaphoreType.DMA((n,))]
        + [pltpu.HBM(g.shape, g.dtype) for g in grads]
        + [pltpu.HBM(shp, g.dtype) for shp, g in zip(halves, grads)]
        + [jax.ShapeDtypeStruct((8, LANES), F32)],
        input_output_aliases={i: 2 + i for i in range(2 * n)},
        compiler_params=pltpu.CompilerParams(has_side_effects=SPLIT_COPY, collective_id=PAIR_ID),
    )(*[pltpu.with_memory_space_constraint(g, pltpu.HBM) for g in grads],
      *[pltpu.with_memory_space_constraint(lax.empty(shp, g.dtype), pltpu.HBM) for shp, g in zip(halves, grads)])
    return res[0], res[1], list(res[2:2 + n]), list(res[2 + n:2 + 2 * n]), res[-1]


def _pair_wait(name, send_sem, recv_sem, grads, lands, after):
    n = len(grads)

    def body(*refs):
        srcs, land_refs = refs[:n], refs[n:2 * n]
        send_ref, recv_ref = refs[2 * n], refs[2 * n + 1]
        for i in range(n):
            copy = _pair_copy(srcs, land_refs, send_ref, recv_ref, i)
            copy.wait_send()
            copy.wait_recv()

    res = pl.pallas_call(
        body,
        name=name,
        in_specs=[HBM] * (2 * n) + [SEM, SEM, ANY],
        out_specs=[HBM] * (2 * n),
        out_shape=[pltpu.HBM(g.shape, g.dtype) for g in grads] + [pltpu.HBM(l.shape, l.dtype) for l in lands],
        input_output_aliases={i: i for i in range(2 * n)},
        compiler_params=pltpu.CompilerParams(has_side_effects=SPLIT_COPY),
    )(*grads, *lands, send_sem, recv_sem, after)
    return list(res[:n]), list(res[n:])


def _scatter_copy(srcs, lands, send_sem, recv_sem, i, j):
    x, y, c = _position()
    chips = _other_chips(x, y)
    return pltpu.make_async_remote_copy(
        src_ref=srcs[i].at[2 * chips[j][0] + chips[j][1]], dst_ref=lands[i].at[j],
        send_sem=send_sem.at[3 * i + j], recv_sem=recv_sem.at[3 * i + j],
        device_id=(chips[j][0], chips[j][1], c), device_id_type=MESH)


def _scatter_start(name, sums):
    n = len(sums)

    def body(*refs):
        srcs, lands = refs[:n], refs[n:2 * n]
        send_sem, recv_sem = refs[2 * n], refs[2 * n + 1]
        token = refs[-1]
        for i in range(n):
            for j in range(3):
                _scatter_copy(srcs, lands, send_sem, recv_sem, i, j).start()
        token[...] = jnp.zeros_like(token)

    land_shapes = [(3,) + s.shape[1:] for s in sums]
    res = pl.pallas_call(
        body,
        name=name,
        in_specs=[HBM] * (2 * n),
        out_specs=[SEM, SEM] + [HBM] * (2 * n) + [pl.BlockSpec(memory_space=pltpu.VMEM)],
        out_shape=[pltpu.SemaphoreType.DMA((3 * n,)), pltpu.SemaphoreType.DMA((3 * n,))]
        + [pltpu.HBM(s.shape, s.dtype) for s in sums]
        + [pltpu.HBM(shp, s.dtype) for shp, s in zip(land_shapes, sums)]
        + [jax.ShapeDtypeStruct((8, LANES), F32)],
        input_output_aliases={i: 2 + i for i in range(2 * n)},
        compiler_params=pltpu.CompilerParams(has_side_effects=SPLIT_COPY),
    )(*[pltpu.with_memory_space_constraint(s, pltpu.HBM) for s in sums],
      *[pltpu.with_memory_space_constraint(lax.empty(shp, s.dtype), pltpu.HBM) for shp, s in zip(land_shapes, sums)])
    return res[0], res[1], list(res[2:2 + n]), list(res[2 + n:2 + 2 * n]), res[-1]


def _scatter_wait(name, send_sem, recv_sem, sums, lands, after):
    n = len(sums)

    def body(*refs):
        srcs, land_refs = refs[:n], refs[n:2 * n]
        send_ref, recv_ref = refs[2 * n], refs[2 * n + 1]
        for i in range(n):
            for j in range(3):
                copy = _scatter_copy(srcs, land_refs, send_ref, recv_ref, i, j)
                copy.wait_send()
                copy.wait_recv()

    res = pl.pallas_call(
        body,
        name=name,
        in_specs=[HBM] * (2 * n) + [SEM, SEM, ANY],
        out_specs=[HBM] * (2 * n),
        out_shape=[pltpu.HBM(s.shape, s.dtype) for s in sums] + [pltpu.HBM(l.shape, l.dtype) for l in lands],
        input_output_aliases={i: i for i in range(2 * n)},
        compiler_params=pltpu.CompilerParams(has_side_effects=SPLIT_COPY),
    )(*sums, *lands, send_sem, recv_sem, after)
    return list(res[:n]), list(res[n:])


def _pair_join(name, halves, small=None):
    n = len(halves)
    if small is None:
        def body_plain(*refs):
            _pair_handshake()
            ins, outs = refs[:n], refs[n:2 * n]
            send_sem, recv_sem = refs[2 * n:]
            x, y, c = _position()
            swaps = [pltpu.make_async_remote_copy(
                src_ref=ins[i], dst_ref=outs[i], send_sem=send_sem.at[i], recv_sem=recv_sem.at[i],
                device_id=(x, y, 1 - c), device_id_type=MESH) for i in range(n)]
            for swap in swaps:
                swap.start()
            for swap in swaps:
                swap.wait()

        return pl.pallas_call(
            body_plain,
            name=name,
            in_specs=[ANY] * n,
            out_specs=[ANY] * n,
            out_shape=[jax.ShapeDtypeStruct(h.shape, h.dtype) for h in halves],
            scratch_shapes=[pltpu.SemaphoreType.DMA((n,))] * 2,
            compiler_params=PAIR_CALL,
        )(*halves)

    def body(*refs):
        ins, small_ref = refs[:n], refs[n]
        outs, all_ref = refs[n + 1:2 * n + 1], refs[2 * n + 1]
        send_sem, recv_sem, sm_send, sm_recv, sm_local = refs[2 * n + 2:]
        x, y, c = _position()
        swaps = []
        for i in range(n):
            swap = pltpu.make_async_remote_copy(
                src_ref=ins[i], dst_ref=outs[i], send_sem=send_sem.at[i], recv_sem=recv_sem.at[i],
                device_id=(x, y, 1 - c), device_id_type=MESH)
            swap.start()
            swaps.append(swap)
        me = 4 * x + 2 * y + c
        sm_own = pltpu.make_async_copy(small_ref, all_ref.at[me], sm_local)
        sm_own.start()
        pushes, arrivals = [], []
        for mask in range(1, N_DEV):
            px, py, pc = x ^ (mask >> 2), y ^ ((mask >> 1) & 1), c ^ (mask & 1)
            pushes.append(pltpu.make_async_remote_copy(
                src_ref=small_ref, dst_ref=all_ref.at[me], send_sem=sm_send.at[mask - 1], recv_sem=sm_recv.at[mask - 1],
                device_id=(px, py, pc), device_id_type=MESH))
            arrivals.append(pltpu.make_async_remote_copy(
                src_ref=small_ref, dst_ref=all_ref.at[4 * px + 2 * py + pc], send_sem=sm_send.at[mask - 1],
                recv_sem=sm_recv.at[mask - 1], device_id=(px, py, pc), device_id_type=MESH))
        for cp in pushes:
            cp.start()
        for swap in swaps:
            swap.wait()
        for cp in arrivals:
            cp.wait_recv()
        for cp in pushes:
            cp.wait_send()
        sm_own.wait()

    res = pl.pallas_call(
        body,
        name=name,
        in_specs=[ANY] * (n + 1),
        out_specs=[ANY] * (n + 1),
        out_shape=[jax.ShapeDtypeStruct(h.shape, h.dtype) for h in halves]
        + [jax.ShapeDtypeStruct((N_DEV,) + small.shape, small.dtype)],
        scratch_shapes=[pltpu.SemaphoreType.DMA((n,))] * 2 + [pltpu.SemaphoreType.DMA((N_DEV - 1,))] * 2
        + [pltpu.SemaphoreType.DMA(())],
    )(*halves, small)
    return res[:n], res[n]


def _lower_bound(lbp):
    return jax.nn.softmax(lbp, axis=0)[0:1]


def _local_step(x, target, g1, gm, g2, gq, gk, go, rel_bias, lbp, weights, on_grads, grads_sent):
    b, s, d = x.shape
    t = b * s
    x0 = x.reshape(t, d)
    tgt = target.reshape(t, d)
    gq_t = jnp.tile(gq, (1, ATTN_HEADS))
    gk_t = jnp.tile(gk, (1, ATTN_HEADS))
    lb = _lower_bound(lbp)
    table = _band_table(_rel_bias_table("rel_bias_table", rel_bias))

    h1 = _rmsnorm_fwd("norm1", x0, g1)
    wg1, wu1, deps1 = weights["first"]((h1, table))
    a1, b1, z1 = _ffn_up("ffn1_up", h1, wg1, wu1, deps1)
    wd1, deps_mid = weights["mid"]((z1,))
    x1, h2 = _ffn_down("ffn1_down", z1, wd1, x0, gm, deps_mid)
    w_in, w_out = weights["mid_rest"]((x1,))
    ns = w_in.shape[0]
    proj = _in_proj("in_proj", h2, w_in)
    proj3 = proj.reshape(b, s, proj.shape[1])
    qn, kn, vb = _qk_prep("qk_prep", proj3, gq_t, gk_t)
    attn = _attn_fwd("attn_fwd", qn, kn, vb, table, weights["last_begin"]((qn,))).reshape(t, ATTN_W)
    mix, oraw, states = _hgrn_fwd("hgrn_fwd", proj, attn, lb, go, b, s)
    x2, h3 = _out_proj("out_proj", mix, w_out, x1, g2)
    wg2, wu2, wd2 = weights["last"]((h3,))
    a2, b2, z2 = _ffn_up("ffn2_up", h3, wg2, wu2)
    dy, dyh, sq = _ffn_down_loss("ffn2_down_loss", z2, wd2, x2, tgt)
    loss = 0.5 * jnp.sum(sq) / d

    da2, db2 = _ffn_bwd_act("ffn2_bwd_act", dyh, wd2, a2, b2)
    dwd2 = _grad_w_cols("ffn2_dwd", z2, dyh)
    dwg2 = _grad_w_cols("ffn2_dwg", da2, h3)
    dwu2 = _grad_w_cols("ffn2_dwu", db2, h3)
    sent2 = on_grads("ffn2", {"ffn2_w_gate": dwg2, "ffn2_w_up": dwu2, "ffn2_w_down": dwd2})
    dx2, dx2b, dg2 = _ffn_bwd_in("ffn2_bwd_in", da2, db2, wg2, wu2, x2, g2, dy, 1.0, sent2)
    sent2 = grads_sent("ffn2", dx2b)

    dwout = _grad_w_out("dw_out", mix, dx2b)
    dmix = _out_proj_bwd("out_proj_bwd", dx2b, w_out, sent2)
    dqn, dkn, dvn, dbe, dbo = _attn_bwd("attn_bwd", qn, kn, vb, table, dmix.reshape(b, s, dmix.shape[1]))
    dpq, dpk, dpv, dgq, dgk = _qk_prep_bwd("qk_prep_bwd", proj3, dqn, dkn, dvn, gq_t, gk_t)
    dpq, dpk, dpv = (a.reshape(t, ATTN_W) for a in (dpq, dpk, dpv))
    dproj, dlb, dgo = _hgrn_bwd("hgrn_bwd", proj, (dpq, dpk, dpv), lb, go, oraw, states, dmix, b, s)
    dwin = _grad_w_in("dw_in", h2, dproj, ns)
    dx1, dx1h, dgm = _in_proj_bwd("in_proj_bwd", dproj, w_in, x1, gm, dx2, 0.5)

    dwd1 = _grad_w_cols("ffn1_dwd", z1, dx1h)
    sent_mix = on_grads("mix", {"w_in": dwin, "w_out": dwout.reshape(ns, dwout.shape[0] // ns, d),
                                "ffn1_w_down": dwd1})
    da1, db1 = _ffn_bwd_act("ffn1_bwd_act", dx1h, wd1, a1, b1, sent_mix)
    sent_mix = grads_sent("mix", da1)
    dwg1 = _grad_w_cols("ffn1_dwg", da1, h1, sent_mix)
    dwu1 = _grad_w_cols("ffn1_dwu", db1, h1)
    on_grads("ffn1", {"ffn1_w_gate": dwg1, "ffn1_w_up": dwu1})
    sent1 = grads_sent("ffn1", None)
    dx0, dg1 = _ffn_bwd_in("ffn1_bwd_in", da1, db1, wg1, wu1, x0, g1, dx1, None, sent1)

    nt = dg1.shape[0]
    sg = _small_grads(
        "small_grads", dg1.reshape(nt, d), dgm.reshape(nt, d), dg2.reshape(nt, d),
        dgq.reshape(-1, ATTN_W), dgk.reshape(-1, ATTN_W), dbe.transpose(1, 0, 2), dbo.transpose(1, 0, 2),
        dlb.reshape(b, HGRN_W), dgo.reshape(b, HGRN_W), lbp)
    g1g, gmg, g2g, gqg, gkg, rbg, lbg, gog = sg
    small = _pack_small(g1g, gmg, g2g, lbg, rbg[:, :N_REL], gqg, gkg, gog, loss)
    return dx0.reshape(b, s, d), small


LOSS_SLOT = 7 * SMALL_COLS + 2 * ATTN_DH + HGRN_DH


def _pack_small(g1, gm, g2, lbp, rel_bias, gq, gk, go, loss=None):
    flat = [g1.reshape(-1), gm.reshape(-1), g2.reshape(-1), lbp.reshape(-1), rel_bias.reshape(-1)]
    n_bias = 3 * SMALL_COLS - rel_bias.size
    heads = [gq.reshape(-1), gk.reshape(-1), go.reshape(-1)]
    heads.append(jnp.zeros((1,), F32) if loss is None else loss.reshape(1))
    n_tail = SMALL_COLS - sum(h.size for h in heads)
    return jnp.concatenate(flat + [jnp.zeros((n_bias,), F32)] + heads + [jnp.zeros((n_tail,), F32)]).reshape(
        SMALL_ROWS, SMALL_COLS)


def _unpack_small(p, d):
    flat = p.reshape(-1)
    o = 3 * d
    g1, gm, g2 = p[0:1], p[1:2], p[2:3]
    lbp = flat[o:o + 2 * HGRN_W].reshape(2, HGRN_W)
    o = 4 * SMALL_COLS
    rel = flat[o:o + ATTN_HEADS * N_REL].reshape(1, ATTN_HEADS, N_REL)
    o = 7 * SMALL_COLS
    gq = flat[o:o + ATTN_DH].reshape(1, ATTN_DH)
    gk = flat[o + ATTN_DH:o + 2 * ATTN_DH].reshape(1, ATTN_DH)
    go = flat[o + 2 * ATTN_DH:o + 2 * ATTN_DH + HGRN_DH].reshape(1, HGRN_DH)
    return g1, gm, g2, gq, gk, rel, lbp, go


def kernel(x, ffn1_norm_g, ffn1_w_gate, ffn1_w_up, ffn1_w_down, mix_norm_g, w_in, attn_q_norm_g, attn_k_norm_g, attn_rel_bias, hgrn_lower_bounds, hgrn_out_norm_g, w_out, ffn2_norm_g, ffn2_w_gate, ffn2_w_up, ffn2_w_down, loss_target, m_ffn1_norm_g, m_ffn1_w_gate, m_ffn1_w_up, m_ffn1_w_down, m_mix_norm_g, m_w_in, m_attn_q_norm_g, m_attn_k_norm_g, m_attn_rel_bias, m_hgrn_lower_bounds, m_hgrn_out_norm_g, m_w_out, m_ffn2_norm_g, m_ffn2_w_gate, m_ffn2_w_up, m_ffn2_w_down, v_ffn1_norm_g, v_ffn1_w_gate, v_ffn1_w_up, v_ffn1_w_down, v_mix_norm_g, v_w_in, v_attn_q_norm_g, v_attn_k_norm_g, v_attn_rel_bias, v_hgrn_lower_bounds, v_hgrn_out_norm_g, v_w_out, v_ffn2_norm_g, v_ffn2_w_gate, v_ffn2_w_up, v_ffn2_w_down):
    d = x.shape[-1]
    big_w = [ffn1_w_gate, ffn1_w_up, ffn1_w_down, w_in, w_out, ffn2_w_gate, ffn2_w_up, ffn2_w_down]
    big_m = [m_ffn1_w_gate, m_ffn1_w_up, m_ffn1_w_down, m_w_in, m_w_out, m_ffn2_w_gate, m_ffn2_w_up, m_ffn2_w_down]
    big_v = [v_ffn1_w_gate, v_ffn1_w_up, v_ffn1_w_down, v_w_in, v_w_out, v_ffn2_w_gate, v_ffn2_w_up, v_ffn2_w_down]
    big_names = ["ffn1_w_gate", "ffn1_w_up", "ffn1_w_down", "w_in", "w_out", "ffn2_w_gate", "ffn2_w_up", "ffn2_w_down"]
    flipped = {nm for nm in big_names if nm.endswith("gate") or nm.endswith("up")}
    flip = lambda nm, a: jnp.swapaxes(a, 1, 2) if nm in flipped else a
    big_w, big_m, big_v = ([flip(nm, a) for nm, a in zip(big_names, arrs)] for arrs in (big_w, big_m, big_v))

    shards = [w[0].astype(BF16) for w in big_w]
    start_a = _gather_start("gather_start_up1", shards[:2], ())
    start_b = _gather_start("gather_start_mid", shards[2:5], (start_a[4],))
    start_c = _gather_start("gather_start_ffn2", shards[5:], (start_b[4],))

    pending = {}

    def arrived(tag, started, after):
        send_sem, recv_sem, srcs, outs, _ = started
        return _gather_wait("gather_wait_" + tag, send_sem, recv_sem, srcs, outs, after)

    def first_weights(after):
        return (*_gather_join("gather_join_up1", *arrived("up1", start_a, after)), (start_c[4],))

    def mid_weights(after):
        srcs, outs = arrived("mid", start_b, after)
        (wd1,) = _gather_join("gather_join_wd1", srcs[:1], outs[:1])
        pending["mid"] = _join_start("join_start_mid", srcs[1:], outs[1:])
        return wd1, (pending["mid"][3],)

    def mid_rest(after):
        sems, srcs, outs, _ = pending["mid"]
        win_f, wout_f = _join_wait("join_wait_mid", sems, srcs, outs, after)
        return win_f, wout_f.reshape(wout_f.shape[0] * wout_f.shape[1], d)

    def last_begin(after):
        pending["ffn2"] = _join_start("join_start_ffn2", *arrived("ffn2", start_c, after))
        return (pending["ffn2"][3],)

    def last_weights(after):
        sems, srcs, outs, _ = pending["ffn2"]
        return _join_wait("join_wait_ffn2", sems, srcs, outs, after)

    weights = {"first": first_weights, "mid": mid_weights, "mid_rest": mid_rest, "last_begin": last_begin,
               "last": last_weights}

    core = lax.axis_index("c").astype(jnp.int32).reshape(1)
    chip = (2 * lax.axis_index("x") + lax.axis_index("y")).astype(jnp.int32).reshape(1)
    started = {}

    def on_grads(tag, grads):
        names = list(grads)
        started[tag] = (names, _pair_start("pair_start_" + tag, [grads[nm] for nm in names]))
        return (started[tag][1][4],)

    def grads_sent(tag, after):
        names, (send_sem, recv_sem, grads, lands, token) = started[tag]
        grads, theirs = _pair_wait("pair_wait_" + tag, send_sem, recv_sem, grads, lands, token if after is None else after)
        sums = [_pair_sum("pair_sum_" + nm, g, th, core) for nm, g, th in zip(names, grads, theirs)]
        started[tag] = (names, _scatter_start("scatter_start_" + tag, sums))
        return (started[tag][1][4],)

    grad_x, small_g = _local_step(
        x, loss_target, ffn1_norm_g, mix_norm_g, ffn2_norm_g, attn_q_norm_g, attn_k_norm_g, hgrn_out_norm_g,
        attn_rel_bias[0], hgrn_lower_bounds, weights, on_grads, grads_sent)

    def finish(tag, after):
        names, (send_sem, recv_sem, sums, lands, _) = started[tag]
        sums, lands = _scatter_wait("scatter_wait_" + tag, send_sem, recv_sem, sums, lands, after)
        return names, [_chip_sum("chip_sum_" + nm, sm, ld, chip) for nm, sm, ld in zip(names, sums, lands)]

    by_name = {nm: (w, m, v) for nm, w, m, v in zip(big_names, big_w, big_m, big_v)}
    updated = {}

    def update(names, halves, other_halves):
        for nm, mine, theirs in zip(names, halves, other_halves):
            w, m, v = by_name[nm]
            updated[nm] = _adamw("adamw_" + nm, w, mine, theirs, m, v, core)

    last_token = started["ffn1"][1][4]
    names_a, halves_a = finish("ffn2", last_token)
    names_m, halves_m = finish("mix", last_token)
    names_a, halves_a = names_a + names_m, halves_a + halves_m
    update(names_a, halves_a, _pair_join("pair_join_early", halves_a))
    names_b, halves_b = finish("ffn1", updated[names_a[-1]][1])
    others_b, small_all = _pair_join("pair_join_last", halves_b, small_g)
    update(names_b, halves_b, others_b)
    big_out = [updated[nm] for nm in big_names]

    pack = lambda g1, gm, g2, gq, gk, rel, lbp, go: _pack_small(g1, gm, g2, lbp, rel[0], gq, gk, go)
    small_w = pack(ffn1_norm_g, mix_norm_g, ffn2_norm_g, attn_q_norm_g, attn_k_norm_g, attn_rel_bias, hgrn_lower_bounds, hgrn_out_norm_g)
    small_m = pack(m_ffn1_norm_g, m_mix_norm_g, m_ffn2_norm_g, m_attn_q_norm_g, m_attn_k_norm_g, m_attn_rel_bias, m_hgrn_lower_bounds, m_hgrn_out_norm_g)
    small_v = pack(v_ffn1_norm_g, v_mix_norm_g, v_ffn2_norm_g, v_attn_q_norm_g, v_attn_k_norm_g, v_attn_rel_bias, v_hgrn_lower_bounds, v_hgrn_out_norm_g)
    small_res = _adamw_small("adamw_small", small_w, small_all, small_m, small_v)
    small_out = [_unpack_small(p, d) for p in small_res]
    loss = small_res[0].reshape(-1)[LOSS_SLOT]

    def assemble(kind):
        bg = [flip(nm, o[kind]) for nm, o in zip(big_names, big_out)]
        g1, gm, g2, gq, gk, rel, lbp, go = small_out[kind]
        return [g1, bg[0], bg[1], bg[2], gm, bg[3], gq, gk, rel, lbp, go, bg[4], g2, bg[5], bg[6], bg[7]]

    return (loss, grad_x, *assemble(0), *assemble(1), *assemble(2), *assemble(3))
```

```python
import functools

import jax
import jax.numpy as jnp
from jax import lax
from jax.experimental import pallas as pl
from jax.experimental.pallas import tpu as pltpu

F32 = jnp.float32
BF16 = jnp.bfloat16
MESH = pl.DeviceIdType.MESH

N_CHIPS = 4
N_DEV = 8
CHUNK = 64
ATTN_HEADS = 8
ATTN_DH = 64
ATTN_W = ATTN_HEADS * ATTN_DH
HGRN_HEADS = 4
HGRN_DH = 128
HGRN_W = HGRN_HEADS * HGRN_DH
LEFT_CHUNKS = 8
BAND = (LEFT_CHUNKS + 1) * CHUNK
KPAD = LEFT_CHUNKS * CHUNK
REL_CLIP = 128
N_REL = 2 * REL_CLIP + 1
N_REL_PAD = 384
RMS_EPS = 1e-6
LANES = 128
SMALL_ROWS = 8
SMALL_COLS = 1024

ADAM_LR = 0.001
ADAM_B1 = 0.9
ADAM_B2 = 0.999
ADAM_EPS = 1e-08
ADAM_WD = 0.01
ADAM_STEP = 10

NN = (((1,), (0,)), ((), ()))
NT = (((1,), (1,)), ((), ()))
TN = (((0,), (0,)), ((), ()))

VMEM_LIMIT = 48 * 1024 * 1024
MXU_WIDTH = 256
COL_CHUNK = 3 * MXU_WIDTH


def _sigmoid(x):
    return 1.0 / (1.0 + jnp.exp(-x))


def _silu(x):
    return x * _sigmoid(x)


def _dot(a, b, dims=NN):
    return lax.dot_general(a, b, dims, preferred_element_type=F32)


def _split3(x):
    hi = x.astype(BF16)
    r1 = x - hi.astype(F32)
    mid = r1.astype(BF16)
    lo = (r1 - mid.astype(F32)).astype(BF16)
    return hi, mid, lo


def _dot_exact_rhs(x, mat, dims=NN, pieces=3):
    hi, mid, lo = _split3(x)
    out = _dot(hi, mat, dims) + _dot(mid, mat, dims)
    return out + _dot(lo, mat, dims) if pieces == 3 else out


def _dot_exact_lhs(mat, x, dims=NN):
    hi, mid, lo = _split3(x)
    return _dot(mat, hi, dims) + _dot(mat, mid, dims) + _dot(mat, lo, dims)


def _params(*sem):
    return pltpu.CompilerParams(dimension_semantics=sem, vmem_limit_bytes=VMEM_LIMIT)


def _mm(name, ins, terms, n_acc, grid, acc_shape, outs, epilogue, extras=(), deps=()):
    nk = grid[2]
    ni, ne, nd, no = len(ins), len(extras), len(deps), len(outs)

    def body(*refs):
        in_refs = refs[:ni]
        ex_refs = refs[ni:ni + ne]
        out_refs = refs[ni + ne + nd:ni + ne + nd + no]
        acc_refs = refs[ni + ne + nd + no:]

        def products():
            parts = [None] * n_acc
            for ai, li, ri, dims in terms:
                d = _dot(in_refs[li][...], in_refs[ri][...], dims)
                parts[ai] = d if parts[ai] is None else parts[ai] + d
            return parts

        def finish(accs):
            res = epilogue(accs, [e[...] for e in ex_refs])
            for o, r in zip(out_refs, res):
                o[...] = r.astype(o.dtype)

        if nk == 1:
            finish(products())
        else:
            k = pl.program_id(2)

            @pl.when(k == 0)
            def _():
                for a, p in zip(acc_refs, products()):
                    a[...] = p

            if nk > 2:
                @pl.when(jnp.logical_and(k > 0, k < nk - 1))
                def _():
                    for a, p in zip(acc_refs, products()):
                        a[...] += p

            @pl.when(k == nk - 1)
            def _():
                finish([a[...] + p for a, p in zip(acc_refs, products())])

    scratch = [] if nk == 1 else [pltpu.VMEM(acc_shape, F32) for _ in range(n_acc)]
    res = pl.pallas_call(
        body,
        name=name,
        grid=grid,
        in_specs=[s for _, s in ins] + [s for _, s in extras] + [pl.BlockSpec(memory_space=pl.ANY)] * nd,
        out_specs=[s for _, s in outs],
        out_shape=[o for o, _ in outs],
        scratch_shapes=scratch,
        compiler_params=_params("parallel", "parallel", "arbitrary"),
    )(*[a for a, _ in ins], *[a for a, _ in extras], *deps)
    return res


def _staged_shape(w):
    return w.shape if len(w.shape) == 2 else (w.shape[1], w.shape[0] * w.shape[2])


def _stage_weights(w_hbm, w_vmem, sem):
    @pl.when(pl.program_id(0) == 0)
    def _():
        copies = []
        for p, (h, v) in enumerate(zip(w_hbm, w_vmem)):
            if len(h.shape) == 2:
                copies.append(pltpu.make_async_copy(h, v, sem.at[p, 0]))
            else:
                pj = h.shape[2]
                copies += [pltpu.make_async_copy(h.at[j], v.at[:, pl.ds(j * pj, pj)], sem.at[p, j])
                           for j in range(h.shape[0])]
        for cp in copies:
            cp.start()
        for cp in copies:
            cp.wait()


def _staging_scratch(weights):
    return [pltpu.VMEM(_staged_shape(w), w.dtype) for w in weights] + [pltpu.SemaphoreType.DMA((len(weights), N_CHIPS))]


def _mm_rows(name, lhs, weights, dims, t, outs, epilogue, extras=(), deps=()):
    tm = _row_tile(t)
    nl, ne, nd, no = len(lhs), len(extras), len(deps), len(outs)

    def body(*refs):
        lhs_refs = refs[:nl]
        w_hbm = refs[nl:2 * nl]
        ex_refs = refs[2 * nl:2 * nl + ne]
        out_refs = refs[2 * nl + ne + nd:2 * nl + ne + nd + no]
        w_vmem = refs[2 * nl + ne + nd + no:3 * nl + ne + nd + no]
        _stage_weights(w_hbm, w_vmem, refs[-1])

        acc = None
        for p in range(nl):
            part = _dot(lhs_refs[p][...], w_vmem[p][...], dims)
            acc = part if acc is None else acc + part
        res = epilogue([acc], [e[...] for e in ex_refs])
        for o, r in zip(out_refs, res):
            o[...] = r.astype(o.dtype)

    return pl.pallas_call(
        body,
        name=name,
        grid=(t // tm,),
        in_specs=[s for _, s in lhs] + [pl.BlockSpec(memory_space=pl.ANY)] * nl + [s for _, s in extras]
        + [pl.BlockSpec(memory_space=pl.ANY)] * nd,
        out_specs=[s for _, s in outs],
        out_shape=[o for o, _ in outs],
        scratch_shapes=_staging_scratch(weights),
        compiler_params=_params("arbitrary"),
    )(*[a for a, _ in lhs], *weights, *[a for a, _ in extras], *deps)


def _col_chunks(f):
    return [(c, min(COL_CHUNK, f - c)) for c in range(0, f, COL_CHUNK)]


def _mm_cols(name, x, weights, dims, n_out, epilogue, extras=(), deps=(), out_dtype=BF16):
    t, k = x.shape
    f = _staged_shape(weights[0])[0 if dims == NT else 1]
    tm = _row_tile(t)
    chunks = _col_chunks(f)
    nw, ne, nd = len(weights), len(extras), len(deps)

    def body(*refs):
        x_ref = refs[0]
        w_hbm = refs[1:1 + nw]
        ex_refs = refs[1 + nw:1 + nw + ne]
        out_refs = refs[1 + nw + ne + nd:1 + nw + ne + nd + n_out]
        w_vmem = refs[1 + nw + ne + nd + n_out:1 + 2 * nw + ne + nd + n_out]
        _stage_weights(w_hbm, w_vmem, refs[-1])

        xv = x_ref[...]

        def dots(c):
            c0, cw = chunks[c]
            return [_dot(xv, w[c0:c0 + cw, :] if dims == NT else w[:, c0:c0 + cw], dims) for w in w_vmem]

        accs = dots(0)
        for c, (c0, cw) in enumerate(chunks):
            nxt = dots(c + 1) if c + 1 < len(chunks) else None
            res = epilogue(accs, [e[:, c0:c0 + cw] for e in ex_refs])
            for o, r in zip(out_refs, res):
                o[:, c0:c0 + cw] = r.astype(o.dtype)
            accs = nxt

    act = pl.BlockSpec((tm, f), lambda i: (i, 0))
    return pl.pallas_call(
        body,
        name=name,
        grid=(t // tm,),
        in_specs=[pl.BlockSpec((tm, k), lambda i: (i, 0))] + [pl.BlockSpec(memory_space=pl.ANY)] * nw + [act] * ne
        + [pl.BlockSpec(memory_space=pl.ANY)] * nd,
        out_specs=[act] * n_out,
        out_shape=[jax.ShapeDtypeStruct((t, f), out_dtype)] * n_out,
        scratch_shapes=_staging_scratch(weights),
        compiler_params=_params("arbitrary"),
    )(x, *weights, *extras, *deps)


def _row_tile(t):
    return 512 if t % 512 == 0 else t


def _k_tile(t):
    return t if t <= 4096 else 1024


def _grad_k_tile(t):
    return 2048 if t % 2048 == 0 else t


def _rmsnorm(xv, g):
    ms = jnp.mean(xv * xv, axis=-1, keepdims=True)
    return xv * lax.rsqrt(ms + RMS_EPS) * g


def _rmsnorm_fwd(name, x, g):
    t, d = x.shape
    tm = _row_tile(t)

    def body(x_ref, g_ref, h_ref):
        h_ref[...] = _rmsnorm(x_ref[...], g_ref[...]).astype(BF16)

    return pl.pallas_call(
        body,
        name=name,
        grid=(t // tm,),
        in_specs=[pl.BlockSpec((tm, d), lambda i: (i, 0)), pl.BlockSpec((1, d), lambda i: (0, 0))],
        out_specs=pl.BlockSpec((tm, d), lambda i: (i, 0)),
        out_shape=jax.ShapeDtypeStruct((t, d), BF16),
        compiler_params=_params("parallel"),
    )(x, g)


def _norm_bwd_epilogue(copy_scale):
    def epilogue(accs, ex):
        dh = accs[0]
        xv, g, dres = ex
        ms = jnp.mean(xv * xv, axis=-1, keepdims=True)
        rstd = lax.rsqrt(ms + RMS_EPS)
        xhat = xv * rstd
        dxhat = dh * g
        dx = rstd * (dxhat - xhat * jnp.mean(dxhat * xhat, axis=-1, keepdims=True))
        out = dres + dx
        dg = jnp.sum(dh * xhat, axis=0, keepdims=True)
        if copy_scale is None:
            return out, dg
        return out, out * copy_scale, dg

    return epilogue


def _merged(w):
    return w.reshape(-1, w.shape[-1])


def _ffn_up(name, h, wg, wu, deps=()):
    def epilogue(accs, ex):
        a, b = accs
        sg = _sigmoid(a)
        act = a * sg
        return act, b * (sg * (1.0 + a * (1.0 - sg))), act * b

    return _mm_cols(name, h, [_merged(wg), _merged(wu)], NT, 3, epilogue, deps=deps)


def _whole_rows(arr, tm):
    return arr, pl.BlockSpec((tm, arr.shape[1]), lambda i: (i, 0))


def _ffn_down(name, z, wd, x, g_next, deps=()):
    t = z.shape[0]
    d = wd.shape[2]
    tm = _row_tile(t)
    row = pl.BlockSpec((tm, d), lambda i: (i, 0))

    def epilogue(accs, ex):
        y = ex[0] + 0.5 * accs[0]
        return y, _rmsnorm(y, ex[1])

    return _mm_rows(
        name, [_whole_rows(z, tm)], [_merged(wd)], NN, t,
        outs=[(jax.ShapeDtypeStruct((t, d), F32), row), (jax.ShapeDtypeStruct((t, d), BF16), row)],
        epilogue=epilogue,
        extras=[(x, row), (g_next, pl.BlockSpec((1, d), lambda i: (0, 0)))],
        deps=deps,
    )


def _ffn_down_loss(name, z, wd, x, target):
    t = z.shape[0]
    d = wd.shape[2]
    tm = _row_tile(t)
    nt = t // tm
    row = pl.BlockSpec((tm, d), lambda i: (i, 0))

    def epilogue(accs, ex):
        e = ex[0] + 0.5 * accs[0] - ex[1]
        dy = e * (1.0 / d)
        return dy, 0.5 * dy, jnp.sum(e * e, axis=0, keepdims=True)

    return _mm_rows(
        name, [_whole_rows(z, tm)], [_merged(wd)], NN, t,
        outs=[(jax.ShapeDtypeStruct((t, d), F32), row), (jax.ShapeDtypeStruct((t, d), BF16), row),
              (jax.ShapeDtypeStruct((nt, 1, d), F32), pl.BlockSpec((None, 1, d), lambda i: (i, 0, 0)))],
        epilogue=epilogue,
        extras=[(x, row), (target, row)],
    )


def _ffn_bwd_act(name, dout, wd, act_a, dact_b, deps=()):
    def epilogue(accs, ex):
        dz = accs[0]
        return dz * ex[1].astype(F32), dz * ex[0].astype(F32)

    return _mm_cols(name, dout, [_merged(wd)], NT, 2, epilogue, extras=[act_a, dact_b], deps=deps)


def _grad_w_cols(name, z, dout, deps=()):
    t, f = z.shape
    d = dout.shape[1]
    tk = _grad_k_tile(t)
    fh = f // 2
    dw = _mm(
        name,
        ins=[(z, pl.BlockSpec((tk, fh), lambda j, n, k: (k, j))),
             (dout, pl.BlockSpec((tk, d), lambda j, n, k: (k, 0)))],
        terms=[(0, 0, 1, TN)],
        n_acc=1,
        grid=(2, 1, t // tk),
        acc_shape=(fh, d),
        outs=[(pltpu.HBM((f, d), BF16), pl.BlockSpec((fh, d), lambda j, n, k: (j, 0)))],
        epilogue=lambda accs, ex: (accs[0],),
        deps=deps,
    )[0]
    return dw.reshape(N_CHIPS, f // N_CHIPS, d)


def _norm_bwd_outs(t, d, tm, copy_scale):
    row = pl.BlockSpec((tm, d), lambda i: (i, 0))
    outs = [(jax.ShapeDtypeStruct((t, d), F32), row)]
    if copy_scale is not None:
        outs.append((jax.ShapeDtypeStruct((t, d), BF16), row))
    outs.append((jax.ShapeDtypeStruct((t // tm, 1, d), F32), pl.BlockSpec((None, 1, d), lambda i: (i, 0, 0))))
    return row, outs


def _ffn_bwd_in(name, da, db, wg, wu, x, g, dres, copy_scale, deps=()):
    t = da.shape[0]
    d = wg.shape[2]
    tm = _row_tile(t)
    row, outs = _norm_bwd_outs(t, d, tm, copy_scale)
    return _mm_rows(
        name, [_whole_rows(da, tm), _whole_rows(db, tm)], [_merged(wg), _merged(wu)], NN, t,
        outs=outs,
        epilogue=_norm_bwd_epilogue(copy_scale),
        extras=[(x, row), (g, pl.BlockSpec((1, d), lambda i: (0, 0))), (dres, row)],
        deps=deps,
    )


def _in_proj(name, h, w_in):
    return _mm_cols(name, h, [w_in], NN, 1, lambda accs, ex: (accs[0],), out_dtype=F32)[0]


def _in_proj_bwd(name, dp, w_in, x, g, dres, copy_scale, deps=()):
    t = dp.shape[0]
    d = w_in.shape[1]
    tm = _row_tile(t)
    row, outs = _norm_bwd_outs(t, d, tm, copy_scale)
    return _mm_rows(
        name, [_whole_rows(dp, tm)], [w_in], NT, t,
        outs=outs,
        epilogue=_norm_bwd_epilogue(copy_scale),
        extras=[(x, row), (g, pl.BlockSpec((1, d), lambda i: (0, 0))), (dres, row)],
        deps=deps,
    )


def _grad_w_in(name, h, dp, ns):
    t, d = h.shape
    pj = dp.shape[1] // ns
    tk = _k_tile(t)
    return _mm(
        name,
        ins=[(h, pl.BlockSpec((tk, d), lambda j, n, k: (k, 0))),
             (dp, pl.BlockSpec((tk, pj), lambda j, n, k: (k, j)))],
        terms=[(0, 0, 1, TN)],
        n_acc=1,
        grid=(ns, 1, t // tk),
        acc_shape=(d, pj),
        outs=[(pltpu.HBM((ns, d, pj), BF16), pl.BlockSpec((None, d, pj), lambda j, n, k: (j, 0, 0)))],
        epilogue=lambda accs, ex: (accs[0],),
    )[0]


def _out_proj(name, mix, w_out, x, g_next):
    t, dm = mix.shape
    d = w_out.shape[1]
    tm = _row_tile(t)
    row = pl.BlockSpec((tm, d), lambda i, n, k: (i, 0))
    return _mm(
        name,
        ins=[(mix, pl.BlockSpec((tm, dm), lambda i, n, k: (i, 0))),
             (w_out, pl.BlockSpec((dm, d), lambda i, n, k: (0, 0)))],
        terms=[(0, 0, 1, NN)],
        n_acc=1,
        grid=(t // tm, 1, 1),
        acc_shape=(tm, d),
        outs=[(jax.ShapeDtypeStruct((t, d), F32), row), (jax.ShapeDtypeStruct((t, d), BF16), row)],
        epilogue=lambda accs, ex: (ex[0] + accs[0], _rmsnorm(ex[0] + accs[0], ex[1])),
        extras=[(x, row), (g_next, pl.BlockSpec((1, d), lambda i, n, k: (0, 0)))],
    )


def _out_proj_bwd(name, dx, w_out, deps=()):
    t, d = dx.shape
    dm = w_out.shape[0]
    tm = _row_tile(t)
    return _mm(
        name,
        ins=[(dx, pl.BlockSpec((tm, d), lambda i, n, k: (i, 0))),
             (w_out, pl.BlockSpec((dm, d), lambda i, n, k: (0, 0)))],
        terms=[(0, 0, 1, NT)],
        n_acc=1,
        grid=(t // tm, 1, 1),
        acc_shape=(tm, dm),
        outs=[(jax.ShapeDtypeStruct((t, dm), F32), pl.BlockSpec((tm, dm), lambda i, n, k: (i, 0)))],
        epilogue=lambda accs, ex: (accs[0],),
        deps=deps,
    )[0]


def _grad_w_out(name, mix, dx):
    t, dm = mix.shape
    d = dx.shape[1]
    tk = _k_tile(t)
    return _mm(
        name,
        ins=[(mix, pl.BlockSpec((tk, dm), lambda a, n, k: (k, 0))),
             (dx, pl.BlockSpec((tk, d), lambda a, n, k: (k, 0)))],
        terms=[(0, 0, 1, TN)],
        n_acc=1,
        grid=(1, 1, t // tk),
        acc_shape=(dm, d),
        outs=[(pltpu.HBM((dm, d), BF16), pl.BlockSpec((dm, d), lambda a, n, k: (0, 0)))],
        epilogue=lambda accs, ex: (accs[0],),
    )[0]


def _head_group_matrix():
    r = lax.broadcasted_iota(jnp.int32, (ATTN_W, ATTN_W), 0)
    c = lax.broadcasted_iota(jnp.int32, (ATTN_W, ATTN_W), 1)
    same = jnp.right_shift(r, 6) == jnp.right_shift(c, 6)
    return jnp.where(same, 1.0, 0.0).astype(BF16)


def _qk_prep(name, proj, gq, gk):
    b, s, _ = proj.shape
    tm = KPAD
    nb = s // tm

    def body(q_ref, k_ref, v_ref, gq_ref, gk_ref, qn_ref, kn_ref, vb_ref):
        j = pl.program_id(1)
        bd = _head_group_matrix()

        def norm(xv, g):
            ms = _dot_exact_rhs(xv * xv, bd, pieces=2) * (1.0 / ATTN_DH)
            return xv * lax.rsqrt(ms + RMS_EPS) * g

        @pl.when(j == 0)
        def _():
            kn_ref[...] = jnp.zeros_like(kn_ref)
            vb_ref[...] = jnp.zeros_like(vb_ref)

        @pl.when(j > 0)
        def _():
            qn_ref[...] = norm(q_ref[...], gq_ref[...]).astype(BF16)
            kn_ref[...] = norm(k_ref[...], gk_ref[...]).astype(BF16)
            vb_ref[...] = v_ref[...].astype(BF16)

    src_blk = lambda col: pl.BlockSpec((None, tm, ATTN_W), lambda bi, j: (bi, jnp.maximum(j - 1, 0), col))
    gspec = pl.BlockSpec((1, ATTN_W), lambda bi, j: (0, 0))
    padded = pl.BlockSpec((None, tm, ATTN_W), lambda bi, j: (bi, j, 0))
    return pl.pallas_call(
        body,
        name=name,
        grid=(b, nb + 1),
        in_specs=[src_blk(0), src_blk(1), src_blk(2), gspec, gspec],
        out_specs=[src_blk(0), padded, padded],
        out_shape=[jax.ShapeDtypeStruct((b, s, ATTN_W), BF16), jax.ShapeDtypeStruct((b, KPAD + s, ATTN_W), BF16),
                   jax.ShapeDtypeStruct((b, KPAD + s, ATTN_W), BF16)],
        compiler_params=_params("parallel", "arbitrary"),
    )(proj, proj, proj, gq, gk)


def _qk_prep_bwd(name, proj, dqn, dkn, dv, gq, gk):
    b, s, _ = proj.shape
    tm = KPAD
    nb = s // tm

    def body(q_ref, k_ref, dqn_ref, dkn_ref, dv_ref, gq_ref, gk_ref, dq_ref, dk_ref, dvb_ref, dgq_ref, dgk_ref):
        bd = _head_group_matrix()

        def bwd(xv, dy, g):
            ms = _dot_exact_rhs(xv * xv, bd, pieces=2) * (1.0 / ATTN_DH)
            rstd = lax.rsqrt(ms + RMS_EPS)
            xhat = xv * rstd
            dxhat = dy * g
            gm = _dot_exact_rhs(dxhat * xhat, bd, pieces=2) * (1.0 / ATTN_DH)
            return rstd * (dxhat - xhat * gm), jnp.sum(dy * xhat, axis=0, keepdims=True)

        dq, dgq = bwd(q_ref[...], dqn_ref[...], gq_ref[...])
        dk, dgk = bwd(k_ref[...], dkn_ref[...], gk_ref[...])
        dq_ref[...] = dq.astype(BF16)
        dk_ref[...] = dk.astype(BF16)
        dvb_ref[...] = dv_ref[...].astype(BF16)
        dgq_ref[...] = dgq
        dgk_ref[...] = dgk

    col = lambda c: pl.BlockSpec((None, tm, ATTN_W), lambda bi, j: (bi, j, c))
    past_pad = pl.BlockSpec((None, tm, ATTN_W), lambda bi, j: (bi, j + 1, 0))
    gspec = pl.BlockSpec((1, ATTN_W), lambda bi, j: (0, 0))
    pspec = pl.BlockSpec((None, 1, ATTN_W), lambda bi, j: (bi * nb + j, 0, 0))
    o_shape = jax.ShapeDtypeStruct((b, s, ATTN_W), BF16)
    p_shape = jax.ShapeDtypeStruct((b * nb, 1, ATTN_W), F32)
    return pl.pallas_call(
        body,
        name=name,
        grid=(b, nb),
        in_specs=[col(0), col(1), col(0), past_pad, past_pad, gspec, gspec],
        out_specs=[col(0)] * 3 + [pspec] * 2,
        out_shape=[o_shape] * 3 + [p_shape] * 2,
        compiler_params=_params("parallel", "parallel"),
    )(proj, proj, dqn, dkn, dv, gq, gk)


Q_CHUNKS = 4
QBLK = Q_CHUNKS * CHUNK
WIN = (LEFT_CHUNKS + Q_CHUNKS) * CHUNK
DB_W = BAND + CHUNK
MASKED = -1e30


def _band_table(bias):
    rows = [jnp.pad(bias, ((0, 0), (0, 0), (CHUNK * i, WIN - BAND - CHUNK * i)), constant_values=MASKED)
            for i in range(Q_CHUNKS)]
    return jnp.concatenate(rows, axis=1)


def _head_lanes(hh):
    lane = lax.broadcasted_iota(jnp.int32, (1, LANES), 1)
    return (lane < ATTN_DH) if hh == 0 else (lane >= ATTN_DH)


def _attn_probs(qh, kw, table, start):
    s = _dot(qh, kw, NT) * (ATTN_DH ** -0.5) + table
    col = lax.broadcasted_iota(jnp.int32, (QBLK, WIN), 1)
    s = jnp.where(col + start >= KPAD, s, MASKED)
    m = jnp.max(s, axis=-1, keepdims=True)
    p = jnp.exp(s - m)
    return p * (1.0 / jnp.sum(p, axis=-1, keepdims=True))


def _attn_fwd(name, q, k, v, table, deps=()):
    b, s, w = q.shape
    sp = k.shape[1]

    def body(q_ref, k_ref, v_ref, t_ref, *rest):
        o_ref = rest[-1]
        start = pl.multiple_of(pl.program_id(2) * QBLK, QBLK)
        kw = k_ref[pl.ds(start, WIN), :]
        vw = v_ref[pl.ds(start, WIN), :]
        q2 = q_ref[...]
        lanes = [_head_lanes(hh) for hh in range(2)]
        probs = [_attn_probs(jnp.where(mine, q2, jnp.zeros_like(q2)), kw, t_ref[hh], start).astype(BF16)
                 for hh, mine in enumerate(lanes)]
        outs = [_dot(p, vw) for p in probs]
        o_ref[...] = jnp.where(lanes[0], outs[0], outs[1]).astype(BF16)

    qspec = pl.BlockSpec((None, QBLK, LANES), lambda p, bi, i: (bi, i, p))
    kspec = pl.BlockSpec((None, sp, LANES), lambda p, bi, i: (bi, 0, p))
    return pl.pallas_call(
        body,
        name=name,
        grid=(w // LANES, b, s // QBLK),
        in_specs=[qspec, kspec, kspec, pl.BlockSpec((2, QBLK, WIN), lambda p, bi, i: (p, 0, 0))] + [ANY] * len(deps),
        out_specs=qspec,
        out_shape=jax.ShapeDtypeStruct((b, s, w), BF16),
        compiler_params=_params("parallel", "parallel", "arbitrary"),
    )(q, k, v, table, *deps)


def _attn_bwd(name, q, k, v, table, dmix):
    b, s, w = q.shape
    sp = k.shape[1]

    def body(q_ref, k_ref, v_ref, t_ref, do_ref, dq_ref, dk_ref, dv_ref, dbe_ref, dbo_ref):
        bi = pl.program_id(1)
        i = pl.program_id(2)
        start = pl.multiple_of(i * QBLK, QBLK)
        win = pl.ds(start, WIN)

        @pl.when(i == 0)
        def _():
            dk_ref[...] = jnp.zeros_like(dk_ref)
            dv_ref[...] = jnp.zeros_like(dv_ref)

        @pl.when(jnp.logical_and(i == 0, bi == 0))
        def _():
            dbe_ref[...] = jnp.zeros_like(dbe_ref)
            dbo_ref[...] = jnp.zeros_like(dbo_ref)

        kw = k_ref[win, :]
        vw = v_ref[win, :]
        q2 = q_ref[...]
        do2 = do_ref[...].astype(BF16)
        lanes = [_head_lanes(hh) for hh in range(2)]
        qh = [jnp.where(mine, q2, jnp.zeros_like(q2)) for mine in lanes]
        doh = [jnp.where(mine, do2, jnp.zeros_like(do2)) for mine in lanes]
        p = [_attn_probs(qh[hh], kw, t_ref[hh], start) for hh in range(2)]
        dp = [_dot(doh[hh], vw, NT) for hh in range(2)]
        ds = [p[hh] * (dp[hh] - jnp.sum(p[hh] * dp[hh], axis=-1, keepdims=True)) for hh in range(2)]
        dsb = [(x * (ATTN_DH ** -0.5)).astype(BF16) for x in ds]
        pb = [x.astype(BF16) for x in p]
        dq = [_dot(dsb[hh], kw) for hh in range(2)]
        dk = [_dot(dsb[hh], qh[hh], TN) for hh in range(2)]
        dv = [_dot(pb[hh], doh[hh], TN) for hh in range(2)]
        for hh in range(2):
            for qi in range(Q_CHUNKS):
                c0 = (qi // 2) * LANES
                blk = ds[hh][qi * CHUNK:(qi + 1) * CHUNK, c0:c0 + DB_W]
                if qi % 2 == 0:
                    dbe_ref[hh] += blk
                else:
                    dbo_ref[hh] += blk
        dq_ref[...] = jnp.where(lanes[0], dq[0], dq[1])
        dk_ref[win, :] += dk[0] + dk[1]
        dv_ref[win, :] += dv[0] + dv[1]

    qspec = pl.BlockSpec((None, QBLK, LANES), lambda p, bi, i: (bi, i, p))
    kspec = pl.BlockSpec((None, sp, LANES), lambda p, bi, i: (bi, 0, p))
    dbspec = pl.BlockSpec((2, CHUNK, DB_W), lambda p, bi, i: (p, 0, 0))
    db_shape = jax.ShapeDtypeStruct((ATTN_HEADS, CHUNK, DB_W), F32)
    return pl.pallas_call(
        body,
        name=name,
        grid=(w // LANES, b, s // QBLK),
        in_specs=[qspec, kspec, kspec, pl.BlockSpec((2, QBLK, WIN), lambda p, bi, i: (p, 0, 0)), qspec],
        out_specs=[qspec, kspec, kspec, dbspec, dbspec],
        out_shape=[jax.ShapeDtypeStruct((b, s, w), F32), jax.ShapeDtypeStruct((b, sp, w), F32),
                   jax.ShapeDtypeStruct((b, sp, w), F32), db_shape, db_shape],
        compiler_params=_params("arbitrary", "arbitrary", "arbitrary"),
    )(q, k, v, table, dmix)


HQ_COL = 3 * ATTN_W // HGRN_DH
HF_COL = HQ_COL + HGRN_HEADS
HI_COL = HF_COL + HGRN_HEADS
HG_COL = HI_COL + HGRN_HEADS
HGRN_ROWS = 8 * CHUNK
HEAD_LANES = [slice(hh * HGRN_DH, (hh + 1) * HGRN_DH) for hh in range(HGRN_HEADS)]


def _tri(lower):
    r = lax.broadcasted_iota(jnp.int32, (CHUNK, CHUNK), 0)
    c = lax.broadcasted_iota(jnp.int32, (CHUNK, CHUNK), 1)
    return (r >= c) if lower else (r <= c)


def _hgrn_chunk(hq, hf, lb, tril):
    sig = _sigmoid(hf)
    f = lb + (1.0 - lb) * sig
    g = jnp.log(f)
    ones_l = jnp.where(tril, 1.0, 0.0).astype(BF16)
    b = _dot_exact_lhs(ones_l, g)
    bl = jnp.sum(g, axis=0, keepdims=True)
    rows = lax.broadcasted_iota(jnp.int32, g.shape, 0)
    bm = jnp.sum(jnp.where(rows <= CHUNK // 2, g, 0.0), axis=0, keepdims=True)
    sq = _sigmoid(hq)
    q = hq * sq
    k = 1.0 - f
    return sig, f, b, bl, bm, sq, q, k


def _hgrn_fwd(name, proj, attn, lb, go, b, s):
    nc = s // CHUNK
    t = b * s
    nblk = s // HGRN_ROWS
    cpb = HGRN_ROWS // CHUNK

    def body(hq_ref, hf_ref, hi_ref, hg_ref, attn_ref, lb_ref, go_ref, mix_ref, oraw_ref, st_ref, s_scr):
        tril = _tri(True)
        gov = go_ref[...]
        mix_ref[:, 0:ATTN_W] = attn_ref[...]

        @pl.when(pl.program_id(1) == 0)
        def _():
            s_scr[...] = jnp.zeros_like(s_scr)

        def step(c, carry):
            sl = pl.ds(pl.multiple_of(c * CHUNK, CHUNK), CHUNK)
            hg = hg_ref[sl, :]
            _, _, bb, bl, bm, _, q, k = _hgrn_chunk(hq_ref[sl, :], hf_ref[sl, :], lb_ref[...], tril)
            vb = hi_ref[sl, :].astype(BF16)
            qe = (q * jnp.exp(bb - bm)).astype(BF16)
            ke = (k * jnp.exp(bm - bb)).astype(BF16)
            qb = (q * jnp.exp(bb)).astype(BF16)
            kb = (k * jnp.exp(bl - bb)).astype(BF16)
            e_last = jnp.exp(bl)
            gate = _silu(hg)
            st = [s_scr[hh] for hh in range(HGRN_HEADS)]
            a = [jnp.where(tril, _dot(qe[:, hs], ke[:, hs], NT), 0.0).astype(BF16) for hs in HEAD_LANES]
            o_state = [_dot(qb[:, hs], st[hh].astype(BF16), NT) for hh, hs in enumerate(HEAD_LANES)]
            st_next = [st[hh] * e_last[:, hs] + _dot(vb[:, hs], kb[:, hs], TN) for hh, hs in enumerate(HEAD_LANES)]
            o = [_dot(a[hh], vb[:, hs]) + o_state[hh] for hh, hs in enumerate(HEAD_LANES)]
            ro = [(oh * lax.rsqrt(jnp.mean(oh * oh, axis=-1, keepdims=True) + RMS_EPS) * gov) * gate[:, hs]
                  for oh, hs in zip(o, HEAD_LANES)]
            for hh in range(HGRN_HEADS):
                st_ref[hh, c] = st[hh]
                s_scr[hh] = st_next[hh]
            mix_ref[sl, ATTN_W:ATTN_W + HGRN_W] = jnp.concatenate(ro, axis=1).astype(BF16)
            oraw_ref[sl, :] = jnp.concatenate(o, axis=1)
            return carry

        lax.fori_loop(0, cpb, step, 0)

    col = lambda base: pl.BlockSpec((HGRN_ROWS, HGRN_W), lambda bi, i: (bi * nblk + i, base // HGRN_HEADS))
    out = pl.BlockSpec((HGRN_ROWS, HGRN_W), lambda bi, i: (bi * nblk + i, 0))
    return pl.pallas_call(
        body,
        name=name,
        grid=(b, nblk),
        in_specs=[col(HQ_COL), col(HF_COL), col(HI_COL), col(HG_COL), out,
                  pl.BlockSpec((1, HGRN_W), lambda bi, i: (0, 0)), pl.BlockSpec((1, HGRN_DH), lambda bi, i: (0, 0))],
        out_specs=[pl.BlockSpec((HGRN_ROWS, ATTN_W + HGRN_W), lambda bi, i: (bi * nblk + i, 0)), out,
                   pl.BlockSpec((None, HGRN_HEADS, cpb, HGRN_DH, HGRN_DH), lambda bi, i: (bi, 0, i, 0, 0))],
        out_shape=[jax.ShapeDtypeStruct((t, ATTN_W + HGRN_W), BF16), jax.ShapeDtypeStruct((t, HGRN_W), F32),
                   jax.ShapeDtypeStruct((b, HGRN_HEADS, nc, HGRN_DH, HGRN_DH), F32)],
        scratch_shapes=[pltpu.VMEM((HGRN_HEADS, HGRN_DH, HGRN_DH), F32)],
        compiler_params=_params("parallel", "arbitrary"),
    )(proj, proj, proj, proj, attn, lb, go)


def _hgrn_bwd(name, proj, dqkv, lb, go, oraw, states, dmix, b, s):
    t = b * s
    nblk = s // HGRN_ROWS
    cpb = HGRN_ROWS // CHUNK

    def body(hq_ref, hf_ref, hi_ref, hg_ref, dq_ref, dk_ref, dv_ref, lb_ref, go_ref, oraw_ref, st_ref, dro_ref,
             dp_ref, dlb_ref, dgo_ref, ds_scr, dlb_scr, dgo_scr):
        tril = _tri(True)
        ones_u = jnp.where(_tri(False), 1.0, 0.0).astype(BF16)
        gov = go_ref[...]
        dp_ref[:, 0:ATTN_W] = dq_ref[...]
        dp_ref[:, ATTN_W:2 * ATTN_W] = dk_ref[...]
        dp_ref[:, 2 * ATTN_W:3 * ATTN_W] = dv_ref[...]

        @pl.when(pl.program_id(1) == 0)
        def _():
            ds_scr[...] = jnp.zeros_like(ds_scr)
            dlb_scr[...] = jnp.zeros_like(dlb_scr)
            dgo_scr[...] = jnp.zeros_like(dgo_scr)

        def step(ci, carry):
            c = cpb - 1 - ci
            sl = pl.ds(pl.multiple_of(c * CHUNK, CHUNK), CHUNK)
            hq = hq_ref[sl, :]
            hg = hg_ref[sl, :]
            sig, f, bb, bl, bm, sq, q, k = _hgrn_chunk(hq, hf_ref[sl, :], lb_ref[...], tril)
            vb = hi_ref[sl, :].astype(BF16)
            ebm = jnp.exp(bb - bm)
            embm = jnp.exp(bm - bb)
            eb = jnp.exp(bb)
            ebl = jnp.exp(bl - bb)
            e_last = jnp.exp(bl)
            qe = (q * ebm).astype(BF16)
            ke = (k * embm).astype(BF16)
            qb = (q * eb).astype(BF16)
            kb = (k * ebl).astype(BF16)
            st = [st_ref[hh, c] for hh in range(HGRN_HEADS)]
            dst = [ds_scr[hh] for hh in range(HGRN_HEADS)]
            o = oraw_ref[sl, :]
            dro = dro_ref[sl, :]
            sg = _sigmoid(hg)
            gov4 = jnp.concatenate([gov] * HGRN_HEADS, axis=1)
            rstd = jnp.concatenate(
                [jnp.broadcast_to(lax.rsqrt(jnp.mean(o[:, hs] * o[:, hs], axis=-1, keepdims=True) + RMS_EPS),
                                  (CHUNK, HGRN_DH)) for hs in HEAD_LANES], axis=1)
            ohat = o * rstd
            dn = dro * (hg * sg)
            dhg = dro * (ohat * gov4) * (sg * (1.0 + hg * (1.0 - sg)))
            dgo_inc = jnp.sum(dn * ohat, axis=0, keepdims=True)
            dohat = dn * gov4
            proj_h = dohat * ohat
            pm = jnp.concatenate(
                [jnp.broadcast_to(jnp.mean(proj_h[:, hs], axis=-1, keepdims=True), (CHUNK, HGRN_DH))
                 for hs in HEAD_LANES], axis=1)
            dob = (rstd * (dohat - ohat * pm)).astype(BF16)
            stb = [x.astype(BF16) for x in st]
            dstb = [x.astype(BF16) for x in dst]
            a = [jnp.where(tril, _dot(qe[:, hs], ke[:, hs], NT), 0.0).astype(BF16) for hs in HEAD_LANES]
            dab = [jnp.where(tril, _dot(dob[:, hs], vb[:, hs], NT), 0.0).astype(BF16) for hs in HEAD_LANES]
            dqb = [_dot(dob[:, hs], stb[hh]) for hh, hs in enumerate(HEAD_LANES)]
            dkb = [_dot(vb[:, hs], dstb[hh]) for hh, hs in enumerate(HEAD_LANES)]
            dv_state = [_dot(kb[:, hs], dstb[hh], NT) for hh, hs in enumerate(HEAD_LANES)]
            dst_next = [dst[hh] * e_last[:, hs] + _dot(dob[:, hs], qb[:, hs], TN) for hh, hs in enumerate(HEAD_LANES)]
            dv = [_dot(a[hh], dob[:, hs], TN) + dv_state[hh] for hh, hs in enumerate(HEAD_LANES)]
            dqe = jnp.concatenate([_dot(dab[hh], ke[:, hs]) for hh, hs in enumerate(HEAD_LANES)], axis=1)
            dke = jnp.concatenate([_dot(dab[hh], qe[:, hs], TN) for hh, hs in enumerate(HEAD_LANES)], axis=1)
            dqb = jnp.concatenate(dqb, axis=1)
            dkb = jnp.concatenate(dkb, axis=1)
            state_term = jnp.concatenate(
                [jnp.sum(dst[hh] * st[hh], axis=0, keepdims=True) for hh in range(HGRN_HEADS)], axis=1)
            dq = dqe * ebm + dqb * eb
            dk = dke * embm + dkb * ebl
            db = (qe.astype(F32) * dqe - ke.astype(F32) * dke) + q * (dqb * eb) - k * (dkb * ebl)
            d_last = jnp.sum(k * ebl * dkb, axis=0, keepdims=True) + state_term * e_last
            dg = _dot_exact_lhs(ones_u, db) + d_last
            df = dg / f - dk
            first = HQ_COL * HGRN_DH
            dp_ref[sl, first:first + HGRN_W] = (dq * (sq * (1.0 + hq * (1.0 - sq)))).astype(BF16)
            dp_ref[sl, first + HGRN_W:first + 2 * HGRN_W] = (df * (1.0 - lb_ref[...]) * sig * (1.0 - sig)).astype(BF16)
            dp_ref[sl, first + 2 * HGRN_W:first + 3 * HGRN_W] = jnp.concatenate(dv, axis=1).astype(BF16)
            dp_ref[sl, first + 3 * HGRN_W:first + 4 * HGRN_W] = dhg.astype(BF16)
            dlb_scr[...] += jnp.sum(df * (1.0 - sig), axis=0, keepdims=True)
            dgo_scr[...] += dgo_inc
            for hh in range(HGRN_HEADS):
                ds_scr[hh] = dst_next[hh]
            return carry

        lax.fori_loop(0, cpb, step, 0)

        @pl.when(pl.program_id(1) == nblk - 1)
        def _():
            dlb_ref[...] = dlb_scr[...]
            dgo_ref[...] = dgo_scr[...]

    rows = lambda bi, i: bi * nblk + (nblk - 1 - i)
    col = lambda base: pl.BlockSpec((HGRN_ROWS, HGRN_W), lambda bi, i: (rows(bi, i), base // HGRN_HEADS))
    out = pl.BlockSpec((HGRN_ROWS, HGRN_W), lambda bi, i: (rows(bi, i), 0))
    part = pl.BlockSpec((None, 1, HGRN_W), lambda bi, i: (bi, 0, 0))
    width = HG_COL * HGRN_DH + HGRN_W
    o_shape = jax.ShapeDtypeStruct((t, width), BF16)
    p_shape = jax.ShapeDtypeStruct((b, 1, HGRN_W), F32)
    return pl.pallas_call(
        body,
        name=name,
        grid=(b, nblk),
        in_specs=[col(HQ_COL), col(HF_COL), col(HI_COL), col(HG_COL), out, out, out,
                  pl.BlockSpec((1, HGRN_W), lambda bi, i: (0, 0)), pl.BlockSpec((1, HGRN_DH), lambda bi, i: (0, 0)), out,
                  pl.BlockSpec((None, HGRN_HEADS, cpb, HGRN_DH, HGRN_DH), lambda bi, i: (bi, 0, nblk - 1 - i, 0, 0)),
                  col(ATTN_W // HGRN_DH)],
        out_specs=[pl.BlockSpec((HGRN_ROWS, width), lambda bi, i: (rows(bi, i), 0))] + [part] * 2,
        out_shape=[o_shape] + [p_shape] * 2,
        scratch_shapes=[pltpu.VMEM((HGRN_HEADS, HGRN_DH, HGRN_DH), F32), pltpu.VMEM((1, HGRN_W), F32),
                        pltpu.VMEM((1, HGRN_W), F32)],
        compiler_params=_params("parallel", "arbitrary"),
    )(proj, proj, proj, proj, *dqkv, lb, go, oraw, states, dmix)


def _small_grads(name, dg1, dgm, dg2, dgq, dgk, dbe_t, dbo_t, dlb, dgo, lbp):
    d = dg1.shape[1]

    def body(dg1_ref, dgm_ref, dg2_ref, dgq_ref, dgk_ref, dbe_ref, dbo_ref, dlb_ref, dgo_ref, lbp_ref,
             g1_ref, gm_ref, g2_ref, gq_ref, gk_ref, rb_ref, lbg_ref, go_ref):
        g1_ref[...] = jnp.sum(dg1_ref[...], axis=0, keepdims=True)
        gm_ref[...] = jnp.sum(dgm_ref[...], axis=0, keepdims=True)
        g2_ref[...] = jnp.sum(dg2_ref[...], axis=0, keepdims=True)
        r = lax.broadcasted_iota(jnp.int32, (ATTN_W, ATTN_DH), 0)
        cidx = lax.broadcasted_iota(jnp.int32, (ATTN_W, ATTN_DH), 1)
        fold = jnp.where(jnp.bitwise_and(r, ATTN_DH - 1) == cidx, 1.0, 0.0).astype(BF16)
        gq_ref[...] = jnp.sum(_dot_exact_rhs(dgq_ref[...], fold), axis=0, keepdims=True)
        gk_ref[...] = jnp.sum(_dot_exact_rhs(dgk_ref[...], fold), axis=0, keepdims=True)
        gosum = jnp.sum(dgo_ref[...], axis=0, keepdims=True)
        go_ref[...] = (gosum[:, 0:HGRN_DH] + gosum[:, HGRN_DH:2 * HGRN_DH]
                       + gosum[:, 2 * HGRN_DH:3 * HGRN_DH] + gosum[:, 3 * HGRN_DH:4 * HGRN_DH])
        p0 = lbp_ref[0:1, :]
        p1 = lbp_ref[1:2, :]
        lbv = 1.0 / (1.0 + jnp.exp(p1 - p0))
        dp0 = jnp.sum(dlb_ref[...], axis=0, keepdims=True) * lbv * (1.0 - lbv)
        lbg_ref[0:1, :] = dp0
        lbg_ref[1:2, :] = -dp0
        acc = dbe_ref[CHUNK - 1] + pltpu.roll(dbo_ref[CHUNK - 1], DB_W - CHUNK, 1)
        for tq in range(CHUNK - 1):
            acc = acc + pltpu.roll(dbe_ref[tq], CHUNK - 1 - tq, 1) + pltpu.roll(dbo_ref[tq], DB_W - 1 - tq, 1)
        jidx = lax.broadcasted_iota(jnp.int32, (DB_W, N_REL_PAD), 0)
        ridx = lax.broadcasted_iota(jnp.int32, (DB_W, N_REL_PAD), 1)
        rel = jnp.clip(KPAD + CHUNK - 1 - jidx, -REL_CLIP, REL_CLIP) + REL_CLIP
        rb_ref[...] = _dot_exact_rhs(acc, jnp.where(rel == ridx, 1.0, 0.0).astype(BF16))

    ins = [dg1, dgm, dg2, dgq, dgk, dbe_t, dbo_t, dlb, dgo, lbp]
    outs = [jax.ShapeDtypeStruct((1, d), F32)] * 3 + [jax.ShapeDtypeStruct((1, ATTN_DH), F32)] * 2 + [
        jax.ShapeDtypeStruct((ATTN_HEADS, N_REL_PAD), F32), jax.ShapeDtypeStruct((2, HGRN_W), F32),
        jax.ShapeDtypeStruct((1, HGRN_DH), F32)]
    vm = pl.BlockSpec(memory_space=pltpu.VMEM)
    return pl.pallas_call(
        body,
        name=name,
        in_specs=[vm] * len(ins),
        out_specs=[vm] * len(outs),
        out_shape=outs,
        compiler_params=pltpu.CompilerParams(vmem_limit_bytes=VMEM_LIMIT),
    )(*ins)


def _adam_update(w, g, m, v):
    m2 = ADAM_B1 * m + (1.0 - ADAM_B1) * g
    v2 = ADAM_B2 * v + (1.0 - ADAM_B2) * (g * g)
    m_hat = m2 / (1.0 - ADAM_B1 ** ADAM_STEP)
    v_hat = v2 / (1.0 - ADAM_B2 ** ADAM_STEP)
    delta = -ADAM_LR * (m_hat / (jnp.sqrt(v_hat) + ADAM_EPS) + ADAM_WD * w)
    return delta, m2, v2


def _rows_tile(r):
    return r if r <= 512 or r % 512 else 512


def _pair_sum(name, grad, theirs, core):
    n, half, c = theirs.shape
    tr = _rows_tile(half)
    nth = half // tr

    def body(core_ref, a_ref, b_ref, o_ref):
        o_ref[...] = (a_ref[...].astype(F32) + b_ref[...].astype(F32)).astype(o_ref.dtype)

    spec = pl.BlockSpec((None, tr, c), lambda i, j, core_ref: (i, j, 0))
    return pl.pallas_call(
        body, name=name,
        grid_spec=pltpu.PrefetchScalarGridSpec(
            num_scalar_prefetch=1, grid=(n, nth),
            in_specs=[pl.BlockSpec((None, tr, c), lambda i, j, core_ref: (i, core_ref[0] * nth + j, 0)), spec],
            out_specs=spec),
        out_shape=pltpu.HBM((n, half, c), BF16), compiler_params=_params("parallel", "parallel"),
    )(core, grad, theirs)


def _chip_sum(name, own, parts, chip):
    _, half, c = own.shape
    tr = _rows_tile(half)

    def body(chip_ref, own_ref, p_ref, o_ref):
        me = chip_ref[0]
        mine = own_ref[...].astype(F32)
        flip_x, flip_y, flip_xy = (p_ref[i].astype(F32) for i in range(3))
        acc = None
        for k in range(N_CHIPS):
            rel = jnp.bitwise_xor(me, k)
            term = jnp.where(rel == 0, mine, jnp.where(rel == 2, flip_x, jnp.where(rel == 1, flip_y, flip_xy)))
            acc = term if acc is None else acc + term
        o_ref[...] = acc

    return pl.pallas_call(
        body, name=name,
        grid_spec=pltpu.PrefetchScalarGridSpec(
            num_scalar_prefetch=1, grid=(half // tr,),
            in_specs=[pl.BlockSpec((None, tr, c), lambda j, chip_ref: (chip_ref[0], j, 0)),
                      pl.BlockSpec((3, tr, c), lambda j, chip_ref: (0, j, 0))],
            out_specs=pl.BlockSpec((tr, c), lambda j, chip_ref: (j, 0))),
        out_shape=pltpu.HBM((half, c), F32), compiler_params=_params("parallel"),
    )(chip, own, parts)


def _adamw(name, w, g_mine, g_theirs, m, v, core):
    _, r, c = w.shape
    half = r // 2
    tr = _rows_tile(half)
    nth = half // tr

    def body(core_ref, w_ref, gm_ref, gt_ref, m_ref, v_ref, g_ref, d_ref, m2_ref, v2_ref):
        g = jnp.where(pl.program_id(0) == core_ref[0], gm_ref[...], gt_ref[...])
        delta, m2, v2 = _adam_update(w_ref[...], g, m_ref[...], v_ref[...])
        g_ref[...] = g
        d_ref[...] = delta
        m2_ref[...] = m2
        v2_ref[...] = v2

    full = pl.BlockSpec((None, tr, c), lambda h, j, core_ref: (0, h * nth + j, 0))
    part = pl.BlockSpec((tr, c), lambda h, j, core_ref: (j, 0))
    shape = jax.ShapeDtypeStruct((1, r, c), F32)
    return pl.pallas_call(
        body, name=name,
        grid_spec=pltpu.PrefetchScalarGridSpec(
            num_scalar_prefetch=1, grid=(2, nth), in_specs=[full, part, part, full, full], out_specs=[full] * 4),
        out_shape=[shape] * 4, compiler_params=_params("parallel", "parallel"),
    )(core, w, g_mine, g_theirs, m, v)


def _rel_bias_table(name, rel_bias):
    padded = jnp.pad(rel_bias, ((0, 0), (0, N_REL_PAD - N_REL)))

    def body(rb_ref, o_ref):
        ridx = lax.broadcasted_iota(jnp.int32, (N_REL_PAD, BAND), 0)
        sidx = lax.broadcasted_iota(jnp.int32, (N_REL_PAD, BAND), 1)
        rb = rb_ref[...]

        def step(tq, carry):
            rel = jnp.clip(tq + KPAD - sidx, -REL_CLIP, REL_CLIP) + REL_CLIP
            onehot = jnp.where(rel == ridx, 1.0, 0.0).astype(BF16)
            o_ref[tq] = _dot_exact_rhs(rb, onehot)
            return carry

        lax.fori_loop(0, CHUNK, step, 0)

    vm = pl.BlockSpec(memory_space=pltpu.VMEM)
    table = pl.pallas_call(
        body, name=name, in_specs=[vm], out_specs=vm,
        out_shape=jax.ShapeDtypeStruct((CHUNK, ATTN_HEADS, BAND), F32),
    )(padded)
    return table.transpose(1, 0, 2)


def _adamw_small(name, w, parts, m, v):
    def body(w_ref, p_ref, m_ref, v_ref, g_ref, d_ref, m2_ref, v2_ref):
        g = p_ref[0]
        for i in range(1, N_DEV):
            g = g + p_ref[i]
        delta, m2, v2 = _adam_update(w_ref[...], g, m_ref[...], v_ref[...])
        g_ref[...] = g
        d_ref[...] = delta
        m2_ref[...] = m2
        v2_ref[...] = v2

    vm = pl.BlockSpec(memory_space=pltpu.VMEM)
    shape = jax.ShapeDtypeStruct((SMALL_ROWS, SMALL_COLS), F32)
    return pl.pallas_call(
        body, name=name, in_specs=[vm] * 4, out_specs=[vm] * 4, out_shape=[shape] * 4,
    )(w, parts, m, v)


def _position():
    return lax.axis_index("x"), lax.axis_index("y"), lax.axis_index("c")


def _other_chips(x, y):
    return [(1 - x, y), (x, 1 - y), (1 - x, 1 - y)]


ANY = pl.BlockSpec(memory_space=pl.ANY)
PAIR_ID = 0


def _pair_handshake():
    x, y, c = _position()
    barrier = pltpu.get_barrier_semaphore()
    pl.semaphore_signal(barrier, inc=1, device_id=(x, y, 1 - c), device_id_type=MESH)
    pl.semaphore_wait(barrier, 1)


PAIR_CALL = pltpu.CompilerParams(collective_id=PAIR_ID)


HBM = pl.BlockSpec(memory_space=pltpu.HBM)
SEM = pl.BlockSpec(memory_space=pltpu.SEMAPHORE)
SPLIT_COPY = pltpu.SideEffectType.DATAFLOW_SIDE_EFFECTING


def _gather_copy(shards, outs, send_sem, recv_sem, i, j):
    x, y, c = _position()
    chips = _other_chips(x, y)
    half = shards[i].shape[0] // 2
    rows = pl.ds(pl.multiple_of(c * half, 16), half)
    return pltpu.make_async_remote_copy(
        src_ref=shards[i].at[rows, :], dst_ref=outs[i].at[2 * x + y, rows, :],
        send_sem=send_sem.at[3 * i + j], recv_sem=recv_sem.at[3 * i + j],
        device_id=(chips[j][0], chips[j][1], c), device_id_type=MESH)


def _gather_start(name, shards, after):
    n = len(shards)

    def body(*refs):
        srcs, outs = refs[:n], refs[n:2 * n]
        send_sem, recv_sem = refs[2 * n + len(after)], refs[2 * n + len(after) + 1]
        token = refs[-1]
        for i in range(n):
            for j in range(3):
                _gather_copy(srcs, outs, send_sem, recv_sem, i, j).start()
        token[...] = jnp.zeros_like(token)

    full = [(N_CHIPS,) + s.shape for s in shards]
    res = pl.pallas_call(
        body,
        name=name,
        in_specs=[HBM] * (2 * n) + [ANY] * len(after),
        out_specs=[SEM, SEM] + [HBM] * (2 * n) + [pl.BlockSpec(memory_space=pltpu.VMEM)],
        out_shape=[pltpu.SemaphoreType.DMA((3 * n,)), pltpu.SemaphoreType.DMA((3 * n,))]
        + [pltpu.HBM(s.shape, s.dtype) for s in shards]
        + [pltpu.HBM(shp, s.dtype) for shp, s in zip(full, shards)]
        + [jax.ShapeDtypeStruct((8, LANES), F32)],
        input_output_aliases={i: 2 + i for i in range(2 * n)},
        compiler_params=pltpu.CompilerParams(has_side_effects=SPLIT_COPY),
    )(*[pltpu.with_memory_space_constraint(s, pltpu.HBM) for s in shards],
      *[pltpu.with_memory_space_constraint(lax.empty(shp, s.dtype), pltpu.HBM) for shp, s in zip(full, shards)],
      *after)
    return res[0], res[1], list(res[2:2 + n]), list(res[2 + n:2 + 2 * n]), res[-1]


def _gather_wait(name, send_sem, recv_sem, shards, outs, after):
    n = len(shards)

    def body(*refs):
        srcs, out_refs = refs[:n], refs[n:2 * n]
        send_ref, recv_ref = refs[2 * n], refs[2 * n + 1]
        for i in range(n):
            for j in range(3):
                copy = _gather_copy(srcs, out_refs, send_ref, recv_ref, i, j)
                copy.wait_send()
                copy.wait_recv()

    res = pl.pallas_call(
        body,
        name=name,
        in_specs=[HBM] * (2 * n) + [SEM, SEM] + [ANY] * len(after),
        out_specs=[HBM] * (2 * n),
        out_shape=[pltpu.HBM(s.shape, s.dtype) for s in shards] + [pltpu.HBM(o.shape, o.dtype) for o in outs],
        input_output_aliases={i: i for i in range(2 * n)},
        compiler_params=pltpu.CompilerParams(has_side_effects=SPLIT_COPY),
    )(*shards, *outs, send_sem, recv_sem, *after)
    return list(res[:n]), list(res[n:])


def _join_copies(srcs, ins, outs, own_send, own_recv, half_send, half_recv):
    x, y, c = _position()
    chips = _other_chips(x, y)
    copies = []
    for i in range(len(srcs)):
        copies.append(pltpu.make_async_remote_copy(
            src_ref=srcs[i], dst_ref=outs[i].at[2 * x + y], send_sem=own_send.at[i], recv_sem=own_recv.at[i],
            device_id=(x, y, 1 - c), device_id_type=MESH))
        half = srcs[i].shape[0] // 2
        rows = pl.ds(pl.multiple_of(c * half, 16), half)
        for j in range(3):
            slot = 2 * chips[j][0] + chips[j][1]
            copies.append(pltpu.make_async_remote_copy(
                src_ref=ins[i].at[slot, rows, :], dst_ref=outs[i].at[slot, rows, :],
                send_sem=half_send.at[3 * i + j], recv_sem=half_recv.at[3 * i + j],
                device_id=(x, y, 1 - c), device_id_type=MESH))
    return copies


def _gather_join(name, shards, outs):
    n = len(shards)

    def body(*refs):
        _pair_handshake()
        copies = _join_copies(refs[:n], refs[n:2 * n], refs[2 * n:3 * n], *refs[3 * n:])
        for cp in copies:
            cp.start()
        for cp in copies:
            cp.wait()

    return pl.pallas_call(
        body,
        name=name,
        in_specs=[ANY] * (2 * n),
        out_specs=[HBM] * n,
        out_shape=[pltpu.HBM(o.shape, o.dtype) for o in outs],
        input_output_aliases={n + i: i for i in range(n)},
        scratch_shapes=[pltpu.SemaphoreType.DMA((n,))] * 2 + [pltpu.SemaphoreType.DMA((3 * n,))] * 2,
        compiler_params=PAIR_CALL,
    )(*shards, *outs)


def _join_start(name, shards, outs):
    n = len(shards)

    def body(*refs):
        _pair_handshake()
        srcs, arrs = refs[:n], refs[n:2 * n]
        sems = refs[2 * n:2 * n + 4]
        token = refs[-1]
        for cp in _join_copies(srcs, arrs, arrs, *sems):
            cp.start()
        token[...] = jnp.zeros_like(token)

    res = pl.pallas_call(
        body,
        name=name,
        in_specs=[HBM] * (2 * n),
        out_specs=[SEM] * 4 + [HBM] * (2 * n) + [pl.BlockSpec(memory_space=pltpu.VMEM)],
        out_shape=[pltpu.SemaphoreType.DMA((n,))] * 2 + [pltpu.SemaphoreType.DMA((3 * n,))] * 2
        + [pltpu.HBM(s.shape, s.dtype) for s in shards] + [pltpu.HBM(o.shape, o.dtype) for o in outs]
        + [jax.ShapeDtypeStruct((8, LANES), F32)],
        input_output_aliases={i: 4 + i for i in range(2 * n)},
        compiler_params=pltpu.CompilerParams(has_side_effects=SPLIT_COPY, collective_id=PAIR_ID),
    )(*shards, *outs)
    return list(res[:4]), list(res[4:4 + n]), list(res[4 + n:4 + 2 * n]), res[-1]


def _join_wait(name, sems, shards, outs, after):
    n = len(shards)

    def body(*refs):
        srcs, arrs = refs[:n], refs[n:2 * n]
        for cp in _join_copies(srcs, arrs, arrs, *refs[2 * n:2 * n + 4]):
            cp.wait_send()
            cp.wait_recv()

    res = pl.pallas_call(
        body,
        name=name,
        in_specs=[HBM] * (2 * n) + [SEM] * 4 + [ANY] * len(after),
        out_specs=[HBM] * (2 * n),
        out_shape=[pltpu.HBM(s.shape, s.dtype) for s in shards] + [pltpu.HBM(o.shape, o.dtype) for o in outs],
        input_output_aliases={i: i for i in range(2 * n)},
        compiler_params=pltpu.CompilerParams(has_side_effects=SPLIT_COPY),
    )(*shards, *outs, *sems, *after)
    return list(res[n:])


def _pair_copy(grads, lands, send_sem, recv_sem, i, whole):
    x, y, c = _position()
    if whole:
        src = grads[i]
    else:
        half = grads[i].shape[1] // 2
        src = grads[i].at[:, pl.ds(pl.multiple_of((1 - c) * half, 16), half), :]
    return pltpu.make_async_remote_copy(
        src_ref=src, dst_ref=lands[i], send_sem=send_sem.at[i], recv_sem=recv_sem.at[i],
        device_id=(x, y, 1 - c), device_id_type=MESH)


def _pair_start(name, grads, whole=False):
    n = len(grads)

    def body(*refs):
        _pair_handshake()
        srcs, lands = refs[:n], refs[n:2 * n]
        send_sem, recv_sem = refs[2 * n], refs[2 * n + 1]
        token = refs[-1]
        for i in range(n):
            _pair_copy(srcs, lands, send_sem, recv_sem, i, whole).start()
        token[...] = jnp.zeros_like(token)

    halves = [g.shape if whole else (g.shape[0], g.shape[1] // 2, g.shape[2]) for g in grads]
    res = pl.pallas_call(
        body,
        name=name,
        in_specs=[HBM] * (2 * n),
        out_specs=[SEM, SEM] + [HBM] * (2 * n) + [pl.BlockSpec(memory_space=pltpu.VMEM)],
        out_shape=[pltpu.SemaphoreType.DMA((n,)), pltpu.SemaphoreType.DMA((n,))]
        + [pltpu.HBM(g.shape, g.dtype) for g in grads]
        + [pltpu.HBM(shp, g.dtype) for shp, g in zip(halves, grads)]
        + [jax.ShapeDtypeStruct((8, LANES), F32)],
        input_output_aliases={i: 2 + i for i in range(2 * n)},
        compiler_params=pltpu.CompilerParams(has_side_effects=SPLIT_COPY, collective_id=PAIR_ID),
    )(*[pltpu.with_memory_space_constraint(g, pltpu.HBM) for g in grads],
      *[pltpu.with_memory_space_constraint(lax.empty(shp, g.dtype), pltpu.HBM) for shp, g in zip(halves, grads)])
    return res[0], res[1], list(res[2:2 + n]), list(res[2 + n:2 + 2 * n]), res[-1]


def _pair_wait(name, send_sem, recv_sem, grads, lands, after, whole=False):
    n = len(grads)

    def body(*refs):
        srcs, land_refs = refs[:n], refs[n:2 * n]
        send_ref, recv_ref = refs[2 * n], refs[2 * n + 1]
        for i in range(n):
            copy = _pair_copy(srcs, land_refs, send_ref, recv_ref, i, whole)
            copy.wait_send()
            copy.wait_recv()

    res = pl.pallas_call(
        body,
        name=name,
        in_specs=[HBM] * (2 * n) + [SEM, SEM, ANY],
        out_specs=[HBM] * (2 * n),
        out_shape=[pltpu.HBM(g.shape, g.dtype) for g in grads] + [pltpu.HBM(l.shape, l.dtype) for l in lands],
        input_output_aliases={i: i for i in range(2 * n)},
        compiler_params=pltpu.CompilerParams(has_side_effects=SPLIT_COPY),
    )(*grads, *lands, send_sem, recv_sem, after)
    return list(res[:n]), list(res[n:])


def _scatter_copy(srcs, lands, send_sem, recv_sem, i, j):
    x, y, c = _position()
    chips = _other_chips(x, y)
    return pltpu.make_async_remote_copy(
        src_ref=srcs[i].at[2 * chips[j][0] + chips[j][1]], dst_ref=lands[i].at[j],
        send_sem=send_sem.at[3 * i + j], recv_sem=recv_sem.at[3 * i + j],
        device_id=(chips[j][0], chips[j][1], c), device_id_type=MESH)


def _scatter_start(name, sums):
    n = len(sums)

    def body(*refs):
        srcs, lands = refs[:n], refs[n:2 * n]
        send_sem, recv_sem = refs[2 * n], refs[2 * n + 1]
        token = refs[-1]
        for i in range(n):
            for j in range(3):
                _scatter_copy(srcs, lands, send_sem, recv_sem, i, j).start()
        token[...] = jnp.zeros_like(token)

    land_shapes = [(3,) + s.shape[1:] for s in sums]
    res = pl.pallas_call(
        body,
        name=name,
        in_specs=[HBM] * (2 * n),
        out_specs=[SEM, SEM] + [HBM] * (2 * n) + [pl.BlockSpec(memory_space=pltpu.VMEM)],
        out_shape=[pltpu.SemaphoreType.DMA((3 * n,)), pltpu.SemaphoreType.DMA((3 * n,))]
        + [pltpu.HBM(s.shape, s.dtype) for s in sums]
        + [pltpu.HBM(shp, s.dtype) for shp, s in zip(land_shapes, sums)]
        + [jax.ShapeDtypeStruct((8, LANES), F32)],
        input_output_aliases={i: 2 + i for i in range(2 * n)},
        compiler_params=pltpu.CompilerParams(has_side_effects=SPLIT_COPY),
    )(*[pltpu.with_memory_space_constraint(s, pltpu.HBM) for s in sums],
      *[pltpu.with_memory_space_constraint(lax.empty(shp, s.dtype), pltpu.HBM) for shp, s in zip(land_shapes, sums)])
    return res[0], res[1], list(res[2:2 + n]), list(res[2 + n:2 + 2 * n]), res[-1]


def _scatter_wait(name, send_sem, recv_sem, sums, lands, after):
    n = len(sums)

    def body(*refs):
        srcs, land_refs = refs[:n], refs[n:2 * n]
        send_ref, recv_ref = refs[2 * n], refs[2 * n + 1]
        for i in range(n):
            for j in range(3):
                copy = _scatter_copy(srcs, land_refs, send_ref, recv_ref, i, j)
                copy.wait_send()
                copy.wait_recv()

    res = pl.pallas_call(
        body,
        name=name,
        in_specs=[HBM] * (2 * n) + [SEM, SEM, ANY],
        out_specs=[HBM] * (2 * n),
        out_shape=[pltpu.HBM(s.shape, s.dtype) for s in sums] + [pltpu.HBM(l.shape, l.dtype) for l in lands],
        input_output_aliases={i: i for i in range(2 * n)},
        compiler_params=pltpu.CompilerParams(has_side_effects=SPLIT_COPY),
    )(*sums, *lands, send_sem, recv_sem, after)
    return list(res[:n]), list(res[n:])


def _pair_join(name, halves, small):
    n = len(halves)

    def body(*refs):
        ins, small_ref = refs[:n], refs[n]
        outs, all_ref = refs[n + 1:2 * n + 1], refs[2 * n + 1]
        send_sem, recv_sem, sm_send, sm_recv, sm_local = refs[2 * n + 2:]
        x, y, c = _position()
        swaps = []
        for i in range(n):
            swap = pltpu.make_async_remote_copy(
                src_ref=ins[i], dst_ref=outs[i], send_sem=send_sem.at[i], recv_sem=recv_sem.at[i],
                device_id=(x, y, 1 - c), device_id_type=MESH)
            swap.start()
            swaps.append(swap)
        me = 4 * x + 2 * y + c
        sm_own = pltpu.make_async_copy(small_ref, all_ref.at[me], sm_local)
        sm_own.start()
        pushes, arrivals = [], []
        for mask in range(1, N_DEV):
            px, py, pc = x ^ (mask >> 2), y ^ ((mask >> 1) & 1), c ^ (mask & 1)
            pushes.append(pltpu.make_async_remote_copy(
                src_ref=small_ref, dst_ref=all_ref.at[me], send_sem=sm_send.at[mask - 1], recv_sem=sm_recv.at[mask - 1],
                device_id=(px, py, pc), device_id_type=MESH))
            arrivals.append(pltpu.make_async_remote_copy(
                src_ref=small_ref, dst_ref=all_ref.at[4 * px + 2 * py + pc], send_sem=sm_send.at[mask - 1],
                recv_sem=sm_recv.at[mask - 1], device_id=(px, py, pc), device_id_type=MESH))
        for cp in pushes:
            cp.start()
        for swap in swaps:
            swap.wait()
        for cp in arrivals:
            cp.wait_recv()
        for cp in pushes:
            cp.wait_send()
        sm_own.wait()

    res = pl.pallas_call(
        body,
        name=name,
        in_specs=[ANY] * (n + 1),
        out_specs=[ANY] * (n + 1),
        out_shape=[jax.ShapeDtypeStruct(h.shape, h.dtype) for h in halves]
        + [jax.ShapeDtypeStruct((N_DEV,) + small.shape, small.dtype)],
        scratch_shapes=[pltpu.SemaphoreType.DMA((n,))] * 2 + [pltpu.SemaphoreType.DMA((N_DEV - 1,))] * 2
        + [pltpu.SemaphoreType.DMA(())],
    )(*halves, small)
    return res[:n], res[n]


def _lower_bound(lbp):
    return jax.nn.softmax(lbp, axis=0)[0:1]


def _local_step(x, target, g1, gm, g2, gq, gk, go, rel_bias, lbp, weights, on_grads, grads_sent):
    b, s, d = x.shape
    t = b * s
    x0 = x.reshape(t, d)
    tgt = target.reshape(t, d)
    gq_t = jnp.tile(gq, (1, ATTN_HEADS))
    gk_t = jnp.tile(gk, (1, ATTN_HEADS))
    lb = _lower_bound(lbp)
    table = _band_table(_rel_bias_table("rel_bias_table", rel_bias))

    h1 = _rmsnorm_fwd("norm1", x0, g1)
    wg1, wu1, deps1 = weights["first"]((h1, table))
    a1, b1, z1 = _ffn_up("ffn1_up", h1, wg1, wu1, deps1)
    wd1, deps_mid = weights["mid"]((z1,))
    x1, h2 = _ffn_down("ffn1_down", z1, wd1, x0, gm, deps_mid)
    w_in, w_out = weights["mid_rest"]((x1,))
    ns = w_in.shape[0]
    proj = _in_proj("in_proj", h2, w_in)
    proj3 = proj.reshape(b, s, proj.shape[1])
    qn, kn, vb = _qk_prep("qk_prep", proj3, gq_t, gk_t)
    attn = _attn_fwd("attn_fwd", qn, kn, vb, table, weights["last_begin"]((qn,))).reshape(t, ATTN_W)
    mix, oraw, states = _hgrn_fwd("hgrn_fwd", proj, attn, lb, go, b, s)
    x2, h3 = _out_proj("out_proj", mix, w_out, x1, g2)
    wg2, wu2, wd2 = weights["last"]((h3,))
    a2, b2, z2 = _ffn_up("ffn2_up", h3, wg2, wu2)
    dy, dyh, sq = _ffn_down_loss("ffn2_down_loss", z2, wd2, x2, tgt)
    loss = 0.5 * jnp.sum(sq) / d

    da2, db2 = _ffn_bwd_act("ffn2_bwd_act", dyh, wd2, a2, b2)
    dwd2 = _grad_w_cols("ffn2_dwd", z2, dyh)
    dwg2 = _grad_w_cols("ffn2_dwg", da2, h3)
    dwu2 = _grad_w_cols("ffn2_dwu", db2, h3)
    sent2 = on_grads("ffn2", {"ffn2_w_gate": dwg2, "ffn2_w_up": dwu2, "ffn2_w_down": dwd2})
    dx2, dx2b, dg2 = _ffn_bwd_in("ffn2_bwd_in", da2, db2, wg2, wu2, x2, g2, dy, 1.0, sent2)
    sent2 = grads_sent("ffn2", dx2b)

    dwout = _grad_w_out("dw_out", mix, dx2b)
    dmix = _out_proj_bwd("out_proj_bwd", dx2b, w_out, sent2)
    dqn, dkn, dvn, dbe, dbo = _attn_bwd("attn_bwd", qn, kn, vb, table, dmix.reshape(b, s, dmix.shape[1]))
    dpq, dpk, dpv, dgq, dgk = _qk_prep_bwd("qk_prep_bwd", proj3, dqn, dkn, dvn, gq_t, gk_t)
    dpq, dpk, dpv = (a.reshape(t, ATTN_W) for a in (dpq, dpk, dpv))
    dproj, dlb, dgo = _hgrn_bwd("hgrn_bwd", proj, (dpq, dpk, dpv), lb, go, oraw, states, dmix, b, s)
    dwin = _grad_w_in("dw_in", h2, dproj, ns)
    dx1, dx1h, dgm = _in_proj_bwd("in_proj_bwd", dproj, w_in, x1, gm, dx2, 0.5)

    dwd1 = _grad_w_cols("ffn1_dwd", z1, dx1h)
    sent_mix = on_grads("mix", {"w_in": dwin, "w_out": dwout.reshape(ns, dwout.shape[0] // ns, d),
                                "ffn1_w_down": dwd1})
    da1, db1 = _ffn_bwd_act("ffn1_bwd_act", dx1h, wd1, a1, b1, sent_mix)
    sent_mix = grads_sent("mix", da1)
    dwg1 = _grad_w_cols("ffn1_dwg", da1, h1, sent_mix)
    dwu1 = _grad_w_cols("ffn1_dwu", db1, h1)
    on_grads("ffn1", {"ffn1_w_gate": dwg1, "ffn1_w_up": dwu1})
    sent1 = grads_sent("ffn1", None)
    dx0, dg1 = _ffn_bwd_in("ffn1_bwd_in", da1, db1, wg1, wu1, x0, g1, dx1, None, sent1)

    nt = dg1.shape[0]
    sg = _small_grads(
        "small_grads", dg1.reshape(nt, d), dgm.reshape(nt, d), dg2.reshape(nt, d),
        dgq.reshape(-1, ATTN_W), dgk.reshape(-1, ATTN_W), dbe.transpose(1, 0, 2), dbo.transpose(1, 0, 2),
        dlb.reshape(b, HGRN_W), dgo.reshape(b, HGRN_W), lbp)
    g1g, gmg, g2g, gqg, gkg, rbg, lbg, gog = sg
    small = _pack_small(g1g, gmg, g2g, lbg, rbg[:, :N_REL], gqg, gkg, gog, loss)
    return dx0.reshape(b, s, d), small


LOSS_SLOT = 7 * SMALL_COLS + 2 * ATTN_DH + HGRN_DH


def _pack_small(g1, gm, g2, lbp, rel_bias, gq, gk, go, loss=None):
    flat = [g1.reshape(-1), gm.reshape(-1), g2.reshape(-1), lbp.reshape(-1), rel_bias.reshape(-1)]
    n_bias = 3 * SMALL_COLS - rel_bias.size
    heads = [gq.reshape(-1), gk.reshape(-1), go.reshape(-1)]
    heads.append(jnp.zeros((1,), F32) if loss is None else loss.reshape(1))
    n_tail = SMALL_COLS - sum(h.size for h in heads)
    return jnp.concatenate(flat + [jnp.zeros((n_bias,), F32)] + heads + [jnp.zeros((n_tail,), F32)]).reshape(
        SMALL_ROWS, SMALL_COLS)


def _unpack_small(p, d):
    flat = p.reshape(-1)
    o = 3 * d
    g1, gm, g2 = p[0:1], p[1:2], p[2:3]
    lbp = flat[o:o + 2 * HGRN_W].reshape(2, HGRN_W)
    o = 4 * SMALL_COLS
    rel = flat[o:o + ATTN_HEADS * N_REL].reshape(1, ATTN_HEADS, N_REL)
    o = 7 * SMALL_COLS
    gq = flat[o:o + ATTN_DH].reshape(1, ATTN_DH)
    gk = flat[o + ATTN_DH:o + 2 * ATTN_DH].reshape(1, ATTN_DH)
    go = flat[o + 2 * ATTN_DH:o + 2 * ATTN_DH + HGRN_DH].reshape(1, HGRN_DH)
    return g1, gm, g2, gq, gk, rel, lbp, go


def kernel(x, ffn1_norm_g, ffn1_w_gate, ffn1_w_up, ffn1_w_down, mix_norm_g, w_in, attn_q_norm_g, attn_k_norm_g, attn_rel_bias, hgrn_lower_bounds, hgrn_out_norm_g, w_out, ffn2_norm_g, ffn2_w_gate, ffn2_w_up, ffn2_w_down, loss_target, m_ffn1_norm_g, m_ffn1_w_gate, m_ffn1_w_up, m_ffn1_w_down, m_mix_norm_g, m_w_in, m_attn_q_norm_g, m_attn_k_norm_g, m_attn_rel_bias, m_hgrn_lower_bounds, m_hgrn_out_norm_g, m_w_out, m_ffn2_norm_g, m_ffn2_w_gate, m_ffn2_w_up, m_ffn2_w_down, v_ffn1_norm_g, v_ffn1_w_gate, v_ffn1_w_up, v_ffn1_w_down, v_mix_norm_g, v_w_in, v_attn_q_norm_g, v_attn_k_norm_g, v_attn_rel_bias, v_hgrn_lower_bounds, v_hgrn_out_norm_g, v_w_out, v_ffn2_norm_g, v_ffn2_w_gate, v_ffn2_w_up, v_ffn2_w_down):
    d = x.shape[-1]
    big_w = [ffn1_w_gate, ffn1_w_up, ffn1_w_down, w_in, w_out, ffn2_w_gate, ffn2_w_up, ffn2_w_down]
    big_m = [m_ffn1_w_gate, m_ffn1_w_up, m_ffn1_w_down, m_w_in, m_w_out, m_ffn2_w_gate, m_ffn2_w_up, m_ffn2_w_down]
    big_v = [v_ffn1_w_gate, v_ffn1_w_up, v_ffn1_w_down, v_w_in, v_w_out, v_ffn2_w_gate, v_ffn2_w_up, v_ffn2_w_down]
    big_names = ["ffn1_w_gate", "ffn1_w_up", "ffn1_w_down", "w_in", "w_out", "ffn2_w_gate", "ffn2_w_up", "ffn2_w_down"]
    flipped = {nm for nm in big_names if nm.endswith("gate") or nm.endswith("up")}
    flip = lambda nm, a: jnp.swapaxes(a, 1, 2) if nm in flipped else a
    big_w, big_m, big_v = ([flip(nm, a) for nm, a in zip(big_names, arrs)] for arrs in (big_w, big_m, big_v))

    shards = [w[0].astype(BF16) for w in big_w]
    start_a = _gather_start("gather_start_up1", shards[:2], ())
    start_b = _gather_start("gather_start_mid", shards[2:5], (start_a[4],))
    start_c = _gather_start("gather_start_ffn2", shards[5:], (start_b[4],))

    pending = {}

    def arrived(tag, started, after):
        send_sem, recv_sem, srcs, outs, _ = started
        return _gather_wait("gather_wait_" + tag, send_sem, recv_sem, srcs, outs, after)

    def first_weights(after):
        return (*_gather_join("gather_join_up1", *arrived("up1", start_a, after)), (start_c[4],))

    def mid_weights(after):
        srcs, outs = arrived("mid", start_b, after)
        (wd1,) = _gather_join("gather_join_wd1", srcs[:1], outs[:1])
        pending["mid"] = _join_start("join_start_mid", srcs[1:], outs[1:])
        return wd1, (pending["mid"][3],)

    def mid_rest(after):
        sems, srcs, outs, _ = pending["mid"]
        win_f, wout_f = _join_wait("join_wait_mid", sems, srcs, outs, after)
        return win_f, wout_f.reshape(wout_f.shape[0] * wout_f.shape[1], d)

    def last_begin(after):
        pending["ffn2"] = _join_start("join_start_ffn2", *arrived("ffn2", start_c, after))
        return (pending["ffn2"][3],)

    def last_weights(after):
        sems, srcs, outs, _ = pending["ffn2"]
        return _join_wait("join_wait_ffn2", sems, srcs, outs, after)

    weights = {"first": first_weights, "mid": mid_weights, "mid_rest": mid_rest, "last_begin": last_begin,
               "last": last_weights}

    core = lax.axis_index("c").astype(jnp.int32).reshape(1)
    chip = (2 * lax.axis_index("x") + lax.axis_index("y")).astype(jnp.int32).reshape(1)
    started = {}

    def on_grads(tag, grads):
        names = list(grads)
        started[tag] = (names, _pair_start("pair_start_" + tag, [grads[nm] for nm in names]))
        return (started[tag][1][4],)

    def grads_sent(tag, after):
        names, (send_sem, recv_sem, grads, lands, token) = started[tag]
        grads, theirs = _pair_wait("pair_wait_" + tag, send_sem, recv_sem, grads, lands, token if after is None else after)
        sums = [_pair_sum("pair_sum_" + nm, g, th, core) for nm, g, th in zip(names, grads, theirs)]
        started[tag] = (names, _scatter_start("scatter_start_" + tag, sums))
        token = started[tag][1][4]
        if tag == "mix":
            token = reduced("ffn2", token)
        return (token,)

    def finish(tag, after):
        names, (send_sem, recv_sem, sums, lands, _) = started[tag]
        sums, lands = _scatter_wait("scatter_wait_" + tag, send_sem, recv_sem, sums, lands, after)
        return names, [_chip_sum("chip_sum_" + nm, sm, ld, chip) for nm, sm, ld in zip(names, sums, lands)]

    swaps = {}

    def reduced(tag, after):
        names, halves = finish(tag, after)
        swaps[tag] = (names, _pair_start("swap_start_" + tag, halves, whole=True))
        return swaps[tag][1][4]

    def swapped(tag, after):
        names, (send_sem, recv_sem, halves, lands, token) = swaps[tag]
        mine, theirs = _pair_wait("swap_wait_" + tag, send_sem, recv_sem, halves, lands, after, whole=True)
        return names, mine, theirs

    grad_x, small_g = _local_step(
        x, loss_target, ffn1_norm_g, mix_norm_g, ffn2_norm_g, attn_q_norm_g, attn_k_norm_g, hgrn_out_norm_g,
        attn_rel_bias[0], hgrn_lower_bounds, weights, on_grads, grads_sent)

    by_name = {nm: (w, m, v) for nm, w, m, v in zip(big_names, big_w, big_m, big_v)}
    updated = {}

    def update(names, halves, other_halves):
        for nm, mine, theirs in zip(names, halves, other_halves):
            w, m, v = by_name[nm]
            updated[nm] = _adamw("adamw_" + nm, w, mine, theirs, m, v, core)

    mix_token = reduced("mix", started["ffn1"][1][4])
    update(*swapped("ffn2", mix_token))
    names_m, halves_m, others_m = swapped("mix", updated["ffn2_w_down"][1])
    update(names_m, halves_m, others_m)
    names_b, halves_b = finish("ffn1", updated[names_m[-1]][1])
    others_b, small_all = _pair_join("pair_join_last", halves_b, small_g)
    update(names_b, halves_b, others_b)
    big_out = [updated[nm] for nm in big_names]

    pack = lambda g1, gm, g2, gq, gk, rel, lbp, go: _pack_small(g1, gm, g2, lbp, rel[0], gq, gk, go)
    small_w = pack(ffn1_norm_g, mix_norm_g, ffn2_norm_g, attn_q_norm_g, attn_k_norm_g, attn_rel_bias, hgrn_lower_bounds, hgrn_out_norm_g)
    small_m = pack(m_ffn1_norm_g, m_mix_norm_g, m_ffn2_norm_g, m_attn_q_norm_g, m_attn_k_norm_g, m_attn_rel_bias, m_hgrn_lower_bounds, m_hgrn_out_norm_g)
    small_v = pack(v_ffn1_norm_g, v_mix_norm_g, v_ffn2_norm_g, v_attn_q_norm_g, v_attn_k_norm_g, v_attn_rel_bias, v_hgrn_lower_bounds, v_hgrn_out_norm_g)
    small_res = _adamw_small("adamw_small", small_w, small_all, small_m, small_v)
    small_out = [_unpack_small(p, d) for p in small_res]
    loss = small_res[0].reshape(-1)[LOSS_SLOT]

    def assemble(kind):
        bg = [flip(nm, o[kind]) for nm, o in zip(big_names, big_out)]
        g1, gm, g2, gq, gk, rel, lbp, go = small_out[kind]
        return [g1, bg[0], bg[1], bg[2], gm, bg[3], gq, gk, rel, lbp, go, bg[4], g2, bg[5], bg[6], bg[7]]

    return (loss, grad_x, *assemble(0), *assemble(1), *assemble(2), *assemble(3))
```

```python
import functools

import jax
import jax.numpy as jnp
from jax import lax
from jax.experimental import pallas as pl
from jax.experimental.pallas import tpu as pltpu

F32 = jnp.float32
BF16 = jnp.bfloat16
MESH = pl.DeviceIdType.MESH

N_CHIPS = 4
N_DEV = 8
CHUNK = 64
ATTN_HEADS = 8
ATTN_DH = 64
ATTN_W = ATTN_HEADS * ATTN_DH
HGRN_HEADS = 4
HGRN_DH = 128
HGRN_W = HGRN_HEADS * HGRN_DH
LEFT_CHUNKS = 8
BAND = (LEFT_CHUNKS + 1) * CHUNK
KPAD = LEFT_CHUNKS * CHUNK
REL_CLIP = 128
N_REL = 2 * REL_CLIP + 1
N_REL_PAD = 384
RMS_EPS = 1e-6
LANES = 128
SMALL_ROWS = 8
SMALL_COLS = 1024

ADAM_LR = 0.001
ADAM_B1 = 0.9
ADAM_B2 = 0.999
ADAM_EPS = 1e-08
ADAM_WD = 0.01
ADAM_STEP = 10

NN = (((1,), (0,)), ((), ()))
NT = (((1,), (1,)), ((), ()))
TN = (((0,), (0,)), ((), ()))

VMEM_LIMIT = 48 * 1024 * 1024
MXU_WIDTH = 256
COL_CHUNK = 3 * MXU_WIDTH


def _sigmoid(x):
    return 1.0 / (1.0 + jnp.exp(-x))


def _silu(x):
    return x * _sigmoid(x)


def _dot(a, b, dims=NN):
    return lax.dot_general(a, b, dims, preferred_element_type=F32)


def _split3(x):
    hi = x.astype(BF16)
    r1 = x - hi.astype(F32)
    mid = r1.astype(BF16)
    lo = (r1 - mid.astype(F32)).astype(BF16)
    return hi, mid, lo


def _dot_exact_rhs(x, mat, dims=NN, pieces=3):
    hi, mid, lo = _split3(x)
    out = _dot(hi, mat, dims) + _dot(mid, mat, dims)
    return out + _dot(lo, mat, dims) if pieces == 3 else out


def _dot_exact_lhs(mat, x, dims=NN):
    hi, mid, lo = _split3(x)
    return _dot(mat, hi, dims) + _dot(mat, mid, dims) + _dot(mat, lo, dims)


def _params(*sem):
    return pltpu.CompilerParams(dimension_semantics=sem, vmem_limit_bytes=VMEM_LIMIT)


def _mm(name, ins, terms, n_acc, grid, acc_shape, outs, epilogue, extras=(), deps=()):
    nk = grid[2]
    ni, ne, nd, no = len(ins), len(extras), len(deps), len(outs)

    def body(*refs):
        in_refs = refs[:ni]
        ex_refs = refs[ni:ni + ne]
        out_refs = refs[ni + ne + nd:ni + ne + nd + no]
        acc_refs = refs[ni + ne + nd + no:]

        def products():
            parts = [None] * n_acc
            for ai, li, ri, dims in terms:
                d = _dot(in_refs[li][...], in_refs[ri][...], dims)
                parts[ai] = d if parts[ai] is None else parts[ai] + d
            return parts

        def finish(accs):
            res = epilogue(accs, [e[...] for e in ex_refs])
            for o, r in zip(out_refs, res):
                o[...] = r.astype(o.dtype)

        if nk == 1:
            finish(products())
        else:
            k = pl.program_id(2)

            @pl.when(k == 0)
            def _():
                for a, p in zip(acc_refs, products()):
                    a[...] = p

            if nk > 2:
                @pl.when(jnp.logical_and(k > 0, k < nk - 1))
                def _():
                    for a, p in zip(acc_refs, products()):
                        a[...] += p

            @pl.when(k == nk - 1)
            def _():
                finish([a[...] + p for a, p in zip(acc_refs, products())])

    scratch = [] if nk == 1 else [pltpu.VMEM(acc_shape, F32) for _ in range(n_acc)]
    res = pl.pallas_call(
        body,
        name=name,
        grid=grid,
        in_specs=[s for _, s in ins] + [s for _, s in extras] + [pl.BlockSpec(memory_space=pl.ANY)] * nd,
        out_specs=[s for _, s in outs],
        out_shape=[o for o, _ in outs],
        scratch_shapes=scratch,
        compiler_params=_params("parallel", "parallel", "arbitrary"),
    )(*[a for a, _ in ins], *[a for a, _ in extras], *deps)
    return res


def _staged_shape(w):
    return w.shape if len(w.shape) == 2 else (w.shape[1], w.shape[0] * w.shape[2])


def _stage_weights(w_hbm, w_vmem, sem):
    @pl.when(pl.program_id(0) == 0)
    def _():
        copies = []
        for p, (h, v) in enumerate(zip(w_hbm, w_vmem)):
            if len(h.shape) == 2:
                copies.append(pltpu.make_async_copy(h, v, sem.at[p, 0]))
            else:
                pj = h.shape[2]
                copies += [pltpu.make_async_copy(h.at[j], v.at[:, pl.ds(j * pj, pj)], sem.at[p, j])
                           for j in range(h.shape[0])]
        for cp in copies:
            cp.start()
        for cp in copies:
            cp.wait()


def _staging_scratch(weights):
    return [pltpu.VMEM(_staged_shape(w), w.dtype) for w in weights] + [pltpu.SemaphoreType.DMA((len(weights), N_CHIPS))]


def _mm_rows(name, lhs, weights, dims, t, outs, epilogue, extras=(), deps=()):
    tm = _row_tile(t)
    nl, ne, nd, no = len(lhs), len(extras), len(deps), len(outs)

    def body(*refs):
        lhs_refs = refs[:nl]
        w_hbm = refs[nl:2 * nl]
        ex_refs = refs[2 * nl:2 * nl + ne]
        out_refs = refs[2 * nl + ne + nd:2 * nl + ne + nd + no]
        w_vmem = refs[2 * nl + ne + nd + no:3 * nl + ne + nd + no]
        _stage_weights(w_hbm, w_vmem, refs[-1])

        acc = None
        for p in range(nl):
            part = _dot(lhs_refs[p][...], w_vmem[p][...], dims)
            acc = part if acc is None else acc + part
        res = epilogue([acc], [e[...] for e in ex_refs])
        for o, r in zip(out_refs, res):
            o[...] = r.astype(o.dtype)

    return pl.pallas_call(
        body,
        name=name,
        grid=(t // tm,),
        in_specs=[s for _, s in lhs] + [pl.BlockSpec(memory_space=pl.ANY)] * nl + [s for _, s in extras]
        + [pl.BlockSpec(memory_space=pl.ANY)] * nd,
        out_specs=[s for _, s in outs],
        out_shape=[o for o, _ in outs],
        scratch_shapes=_staging_scratch(weights),
        compiler_params=_params("arbitrary"),
    )(*[a for a, _ in lhs], *weights, *[a for a, _ in extras], *deps)


def _col_chunks(f):
    return [(c, min(COL_CHUNK, f - c)) for c in range(0, f, COL_CHUNK)]


def _mm_cols(name, x, weights, dims, n_out, epilogue, extras=(), deps=(), out_dtype=BF16):
    t, k = x.shape
    f = _staged_shape(weights[0])[0 if dims == NT else 1]
    tm = _row_tile(t)
    chunks = _col_chunks(f)
    nw, ne, nd = len(weights), len(extras), len(deps)

    def body(*refs):
        x_ref = refs[0]
        w_hbm = refs[1:1 + nw]
        ex_refs = refs[1 + nw:1 + nw + ne]
        out_refs = refs[1 + nw + ne + nd:1 + nw + ne + nd + n_out]
        w_vmem = refs[1 + nw + ne + nd + n_out:1 + 2 * nw + ne + nd + n_out]
        _stage_weights(w_hbm, w_vmem, refs[-1])

        xv = x_ref[...]

        def dots(c):
            c0, cw = chunks[c]
            return [_dot(xv, w[c0:c0 + cw, :] if dims == NT else w[:, c0:c0 + cw], dims) for w in w_vmem]

        accs = dots(0)
        for c, (c0, cw) in enumerate(chunks):
            nxt = dots(c + 1) if c + 1 < len(chunks) else None
            res = epilogue(accs, [e[:, c0:c0 + cw] for e in ex_refs])
            for o, r in zip(out_refs, res):
                o[:, c0:c0 + cw] = r.astype(o.dtype)
            accs = nxt

    act = pl.BlockSpec((tm, f), lambda i: (i, 0))
    return pl.pallas_call(
        body,
        name=name,
        grid=(t // tm,),
        in_specs=[pl.BlockSpec((tm, k), lambda i: (i, 0))] + [pl.BlockSpec(memory_space=pl.ANY)] * nw + [act] * ne
        + [pl.BlockSpec(memory_space=pl.ANY)] * nd,
        out_specs=[act] * n_out,
        out_shape=[jax.ShapeDtypeStruct((t, f), out_dtype)] * n_out,
        scratch_shapes=_staging_scratch(weights),
        compiler_params=_params("arbitrary"),
    )(x, *weights, *extras, *deps)


def _row_tile(t):
    return 512 if t % 512 == 0 else t


def _k_tile(t):
    return t if t <= 4096 else 1024


def _grad_k_tile(t):
    return 2048 if t % 2048 == 0 else t


def _rmsnorm(xv, g):
    ms = jnp.mean(xv * xv, axis=-1, keepdims=True)
    return xv * lax.rsqrt(ms + RMS_EPS) * g


def _rmsnorm_fwd(name, x, g):
    t, d = x.shape
    tm = _row_tile(t)

    def body(x_ref, g_ref, h_ref):
        h_ref[...] = _rmsnorm(x_ref[...], g_ref[...]).astype(BF16)

    return pl.pallas_call(
        body,
        name=name,
        grid=(t // tm,),
        in_specs=[pl.BlockSpec((tm, d), lambda i: (i, 0)), pl.BlockSpec((1, d), lambda i: (0, 0))],
        out_specs=pl.BlockSpec((tm, d), lambda i: (i, 0)),
        out_shape=jax.ShapeDtypeStruct((t, d), BF16),
        compiler_params=_params("parallel"),
    )(x, g)


def _norm_bwd_epilogue(copy_scale):
    def epilogue(accs, ex):
        dh = accs[0]
        xv, g, dres = ex
        ms = jnp.mean(xv * xv, axis=-1, keepdims=True)
        rstd = lax.rsqrt(ms + RMS_EPS)
        xhat = xv * rstd
        dxhat = dh * g
        dx = rstd * (dxhat - xhat * jnp.mean(dxhat * xhat, axis=-1, keepdims=True))
        out = dres + dx
        dg = jnp.sum(dh * xhat, axis=0, keepdims=True)
        if copy_scale is None:
            return out, dg
        return out, out * copy_scale, dg

    return epilogue


def _merged(w):
    return w.reshape(-1, w.shape[-1])


def _ffn_up(name, h, wg, wu, deps=()):
    def epilogue(accs, ex):
        a, b = accs
        sg = _sigmoid(a)
        act = a * sg
        return act, b * (sg * (1.0 + a * (1.0 - sg))), act * b

    return _mm_cols(name, h, [_merged(wg), _merged(wu)], NT, 3, epilogue, deps=deps)


def _whole_rows(arr, tm):
    return arr, pl.BlockSpec((tm, arr.shape[1]), lambda i: (i, 0))


def _ffn_down(name, z, wd, x, g_next, deps=()):
    t = z.shape[0]
    d = wd.shape[2]
    tm = _row_tile(t)
    row = pl.BlockSpec((tm, d), lambda i: (i, 0))

    def epilogue(accs, ex):
        y = ex[0] + 0.5 * accs[0]
        return y, _rmsnorm(y, ex[1])

    return _mm_rows(
        name, [_whole_rows(z, tm)], [_merged(wd)], NN, t,
        outs=[(jax.ShapeDtypeStruct((t, d), F32), row), (jax.ShapeDtypeStruct((t, d), BF16), row)],
        epilogue=epilogue,
        extras=[(x, row), (g_next, pl.BlockSpec((1, d), lambda i: (0, 0)))],
        deps=deps,
    )


def _ffn_down_loss(name, z, wd, x, target):
    t = z.shape[0]
    d = wd.shape[2]
    tm = _row_tile(t)
    nt = t // tm
    row = pl.BlockSpec((tm, d), lambda i: (i, 0))

    def epilogue(accs, ex):
        e = ex[0] + 0.5 * accs[0] - ex[1]
        dy = e * (1.0 / d)
        return dy, 0.5 * dy, jnp.sum(e * e, axis=0, keepdims=True)

    return _mm_rows(
        name, [_whole_rows(z, tm)], [_merged(wd)], NN, t,
        outs=[(jax.ShapeDtypeStruct((t, d), F32), row), (jax.ShapeDtypeStruct((t, d), BF16), row),
              (jax.ShapeDtypeStruct((nt, 1, d), F32), pl.BlockSpec((None, 1, d), lambda i: (i, 0, 0)))],
        epilogue=epilogue,
        extras=[(x, row), (target, row)],
    )


def _ffn_bwd_act(name, dout, wd, act_a, dact_b, deps=()):
    def epilogue(accs, ex):
        dz = accs[0]
        return dz * ex[1].astype(F32), dz * ex[0].astype(F32)

    return _mm_cols(name, dout, [_merged(wd)], NT, 2, epilogue, extras=[act_a, dact_b], deps=deps)


def _grad_w_cols(name, z, dout, deps=()):
    t, f = z.shape
    d = dout.shape[1]
    tk = _grad_k_tile(t)
    fh = f // 2
    dw = _mm(
        name,
        ins=[(z, pl.BlockSpec((tk, fh), lambda j, n, k: (k, j))),
             (dout, pl.BlockSpec((tk, d), lambda j, n, k: (k, 0)))],
        terms=[(0, 0, 1, TN)],
        n_acc=1,
        grid=(2, 1, t // tk),
        acc_shape=(fh, d),
        outs=[(pltpu.HBM((f, d), BF16), pl.BlockSpec((fh, d), lambda j, n, k: (j, 0)))],
        epilogue=lambda accs, ex: (accs[0],),
        deps=deps,
    )[0]
    return dw.reshape(N_CHIPS, f // N_CHIPS, d)


def _norm_bwd_outs(t, d, tm, copy_scale):
    row = pl.BlockSpec((tm, d), lambda i: (i, 0))
    outs = [(jax.ShapeDtypeStruct((t, d), F32), row)]
    if copy_scale is not None:
        outs.append((jax.ShapeDtypeStruct((t, d), BF16), row))
    outs.append((jax.ShapeDtypeStruct((t // tm, 1, d), F32), pl.BlockSpec((None, 1, d), lambda i: (i, 0, 0))))
    return row, outs


def _ffn_bwd_in(name, da, db, wg, wu, x, g, dres, copy_scale, deps=()):
    t = da.shape[0]
    d = wg.shape[2]
    tm = _row_tile(t)
    row, outs = _norm_bwd_outs(t, d, tm, copy_scale)
    return _mm_rows(
        name, [_whole_rows(da, tm), _whole_rows(db, tm)], [_merged(wg), _merged(wu)], NN, t,
        outs=outs,
        epilogue=_norm_bwd_epilogue(copy_scale),
        extras=[(x, row), (g, pl.BlockSpec((1, d), lambda i: (0, 0))), (dres, row)],
        deps=deps,
    )


def _in_proj(name, h, w_in):
    return _mm_cols(name, h, [w_in], NN, 1, lambda accs, ex: (accs[0],), out_dtype=F32)[0]


def _in_proj_bwd(name, dp, w_in, x, g, dres, copy_scale, deps=()):
    t = dp.shape[0]
    d = w_in.shape[1]
    tm = _row_tile(t)
    row, outs = _norm_bwd_outs(t, d, tm, copy_scale)
    return _mm_rows(
        name, [_whole_rows(dp, tm)], [w_in], NT, t,
        outs=outs,
        epilogue=_norm_bwd_epilogue(copy_scale),
        extras=[(x, row), (g, pl.BlockSpec((1, d), lambda i: (0, 0))), (dres, row)],
        deps=deps,
    )


def _grad_w_in(name, h, dp, ns):
    t, d = h.shape
    pj = dp.shape[1] // ns
    tk = _k_tile(t)
    return _mm(
        name,
        ins=[(h, pl.BlockSpec((tk, d), lambda j, n, k: (k, 0))),
             (dp, pl.BlockSpec((tk, pj), lambda j, n, k: (k, j)))],
        terms=[(0, 0, 1, TN)],
        n_acc=1,
        grid=(ns, 1, t // tk),
        acc_shape=(d, pj),
        outs=[(pltpu.HBM((ns, d, pj), BF16), pl.BlockSpec((None, d, pj), lambda j, n, k: (j, 0, 0)))],
        epilogue=lambda accs, ex: (accs[0],),
    )[0]


def _out_proj(name, mix, w_out, x, g_next):
    t, dm = mix.shape
    d = w_out.shape[1]
    tm = _row_tile(t)
    row = pl.BlockSpec((tm, d), lambda i, n, k: (i, 0))
    return _mm(
        name,
        ins=[(mix, pl.BlockSpec((tm, dm), lambda i, n, k: (i, 0))),
             (w_out, pl.BlockSpec((dm, d), lambda i, n, k: (0, 0)))],
        terms=[(0, 0, 1, NN)],
        n_acc=1,
        grid=(t // tm, 1, 1),
        acc_shape=(tm, d),
        outs=[(jax.ShapeDtypeStruct((t, d), F32), row), (jax.ShapeDtypeStruct((t, d), BF16), row)],
        epilogue=lambda accs, ex: (ex[0] + accs[0], _rmsnorm(ex[0] + accs[0], ex[1])),
        extras=[(x, row), (g_next, pl.BlockSpec((1, d), lambda i, n, k: (0, 0)))],
    )


def _out_proj_bwd(name, dx, w_out, deps=()):
    t, d = dx.shape
    dm = w_out.shape[0]
    tm = _row_tile(t)
    return _mm(
        name,
        ins=[(dx, pl.BlockSpec((tm, d), lambda i, n, k: (i, 0))),
             (w_out, pl.BlockSpec((dm, d), lambda i, n, k: (0, 0)))],
        terms=[(0, 0, 1, NT)],
        n_acc=1,
        grid=(t // tm, 1, 1),
        acc_shape=(tm, dm),
        outs=[(jax.ShapeDtypeStruct((t, dm), F32), pl.BlockSpec((tm, dm), lambda i, n, k: (i, 0)))],
        epilogue=lambda accs, ex: (accs[0],),
        deps=deps,
    )[0]


def _grad_w_out(name, mix, dx):
    t, dm = mix.shape
    d = dx.shape[1]
    tk = _k_tile(t)
    return _mm(
        name,
        ins=[(mix, pl.BlockSpec((tk, dm), lambda a, n, k: (k, 0))),
             (dx, pl.BlockSpec((tk, d), lambda a, n, k: (k, 0)))],
        terms=[(0, 0, 1, TN)],
        n_acc=1,
        grid=(1, 1, t // tk),
        acc_shape=(dm, d),
        outs=[(pltpu.HBM((dm, d), BF16), pl.BlockSpec((dm, d), lambda a, n, k: (0, 0)))],
        epilogue=lambda accs, ex: (accs[0],),
    )[0]


def _head_group_matrix():
    r = lax.broadcasted_iota(jnp.int32, (ATTN_W, ATTN_W), 0)
    c = lax.broadcasted_iota(jnp.int32, (ATTN_W, ATTN_W), 1)
    same = jnp.right_shift(r, 6) == jnp.right_shift(c, 6)
    return jnp.where(same, 1.0, 0.0).astype(BF16)


def _qk_prep(name, proj, gq, gk):
    b, s, _ = proj.shape
    tm = KPAD
    nb = s // tm

    def body(q_ref, k_ref, v_ref, gq_ref, gk_ref, qn_ref, kn_ref, vb_ref):
        j = pl.program_id(1)
        bd = _head_group_matrix()

        def norm(xv, g):
            ms = _dot_exact_rhs(xv * xv, bd, pieces=2) * (1.0 / ATTN_DH)
            return xv * lax.rsqrt(ms + RMS_EPS) * g

        @pl.when(j == 0)
        def _():
            kn_ref[...] = jnp.zeros_like(kn_ref)
            vb_ref[...] = jnp.zeros_like(vb_ref)

        @pl.when(j > 0)
        def _():
            qn_ref[...] = norm(q_ref[...], gq_ref[...]).astype(BF16)
            kn_ref[...] = norm(k_ref[...], gk_ref[...]).astype(BF16)
            vb_ref[...] = v_ref[...].astype(BF16)

    src_blk = lambda col: pl.BlockSpec((None, tm, ATTN_W), lambda bi, j: (bi, jnp.maximum(j - 1, 0), col))
    gspec = pl.BlockSpec((1, ATTN_W), lambda bi, j: (0, 0))
    padded = pl.BlockSpec((None, tm, ATTN_W), lambda bi, j: (bi, j, 0))
    return pl.pallas_call(
        body,
        name=name,
        grid=(b, nb + 1),
        in_specs=[src_blk(0), src_blk(1), src_blk(2), gspec, gspec],
        out_specs=[src_blk(0), padded, padded],
        out_shape=[jax.ShapeDtypeStruct((b, s, ATTN_W), BF16), jax.ShapeDtypeStruct((b, KPAD + s, ATTN_W), BF16),
                   jax.ShapeDtypeStruct((b, KPAD + s, ATTN_W), BF16)],
        compiler_params=_params("parallel", "arbitrary"),
    )(proj, proj, proj, gq, gk)


def _qk_prep_bwd(name, proj, dqn, dkn, dv, gq, gk):
    b, s, _ = proj.shape
    tm = KPAD
    nb = s // tm

    def body(q_ref, k_ref, dqn_ref, dkn_ref, dv_ref, gq_ref, gk_ref, dq_ref, dk_ref, dvb_ref, dgq_ref, dgk_ref):
        bd = _head_group_matrix()

        def bwd(xv, dy, g):
            ms = _dot_exact_rhs(xv * xv, bd, pieces=2) * (1.0 / ATTN_DH)
            rstd = lax.rsqrt(ms + RMS_EPS)
            xhat = xv * rstd
            dxhat = dy * g
            gm = _dot_exact_rhs(dxhat * xhat, bd, pieces=2) * (1.0 / ATTN_DH)
            return rstd * (dxhat - xhat * gm), jnp.sum(dy * xhat, axis=0, keepdims=True)

        dq, dgq = bwd(q_ref[...], dqn_ref[...], gq_ref[...])
        dk, dgk = bwd(k_ref[...], dkn_ref[...], gk_ref[...])
        dq_ref[...] = dq.astype(BF16)
        dk_ref[...] = dk.astype(BF16)
        dvb_ref[...] = dv_ref[...].astype(BF16)
        dgq_ref[...] = dgq
        dgk_ref[...] = dgk

    col = lambda c: pl.BlockSpec((None, tm, ATTN_W), lambda bi, j: (bi, j, c))
    past_pad = pl.BlockSpec((None, tm, ATTN_W), lambda bi, j: (bi, j + 1, 0))
    gspec = pl.BlockSpec((1, ATTN_W), lambda bi, j: (0, 0))
    pspec = pl.BlockSpec((None, 1, ATTN_W), lambda bi, j: (bi * nb + j, 0, 0))
    o_shape = jax.ShapeDtypeStruct((b, s, ATTN_W), BF16)
    p_shape = jax.ShapeDtypeStruct((b * nb, 1, ATTN_W), F32)
    return pl.pallas_call(
        body,
        name=name,
        grid=(b, nb),
        in_specs=[col(0), col(1), col(0), past_pad, past_pad, gspec, gspec],
        out_specs=[col(0)] * 3 + [pspec] * 2,
        out_shape=[o_shape] * 3 + [p_shape] * 2,
        compiler_params=_params("parallel", "parallel"),
    )(proj, proj, dqn, dkn, dv, gq, gk)


Q_CHUNKS = 4
QBLK = Q_CHUNKS * CHUNK
WIN = (LEFT_CHUNKS + Q_CHUNKS) * CHUNK
DB_W = BAND + CHUNK
MASKED = -1e30
FWD_BLOCKS = 4


def _band_table(bias):
    rows = [jnp.pad(bias, ((0, 0), (0, 0), (CHUNK * i, WIN - BAND - CHUNK * i)), constant_values=MASKED)
            for i in range(Q_CHUNKS)]
    return jnp.concatenate(rows, axis=1)


def _head_lanes(hh):
    lane = lax.broadcasted_iota(jnp.int32, (1, LANES), 1)
    return (lane < ATTN_DH) if hh == 0 else (lane >= ATTN_DH)


def _attn_probs(qh, kw, table, start):
    s = _dot(qh, kw, NT) * (ATTN_DH ** -0.5) + table
    col = lax.broadcasted_iota(jnp.int32, (QBLK, WIN), 1)
    s = jnp.where(col + start >= KPAD, s, MASKED)
    m = jnp.max(s, axis=-1, keepdims=True)
    p = jnp.exp(s - m)
    return p * (1.0 / jnp.sum(p, axis=-1, keepdims=True))


def _attn_fwd(name, q, k, v, table, deps=()):
    b, s, w = q.shape
    sp = k.shape[1]

    def body(q_ref, k_ref, v_ref, t_ref, *rest):
        o_ref = rest[-1]
        lanes = [_head_lanes(hh) for hh in range(2)]
        starts = [pl.multiple_of((pl.program_id(2) * FWD_BLOCKS + j) * QBLK, QBLK) for j in range(FWD_BLOCKS)]
        kws = [k_ref[pl.ds(st, WIN), :] for st in starts]
        vws = [v_ref[pl.ds(st, WIN), :] for st in starts]
        q2s = [q_ref[j * QBLK:(j + 1) * QBLK, :] for j in range(FWD_BLOCKS)]
        probs = [[_attn_probs(jnp.where(mine, q2s[j], jnp.zeros_like(q2s[j])), kws[j], t_ref[hh], starts[j]).astype(BF16)
                  for hh, mine in enumerate(lanes)] for j in range(FWD_BLOCKS)]
        for j in range(FWD_BLOCKS):
            outs = [_dot(p, vws[j]) for p in probs[j]]
            o_ref[j * QBLK:(j + 1) * QBLK, :] = jnp.where(lanes[0], outs[0], outs[1]).astype(BF16)

    qspec = pl.BlockSpec((None, FWD_BLOCKS * QBLK, LANES), lambda p, bi, i: (bi, i, p))
    kspec = pl.BlockSpec((None, sp, LANES), lambda p, bi, i: (bi, 0, p))
    return pl.pallas_call(
        body,
        name=name,
        grid=(w // LANES, b, s // (FWD_BLOCKS * QBLK)),
        in_specs=[qspec, kspec, kspec, pl.BlockSpec((2, QBLK, WIN), lambda p, bi, i: (p, 0, 0))] + [ANY] * len(deps),
        out_specs=qspec,
        out_shape=jax.ShapeDtypeStruct((b, s, w), BF16),
        compiler_params=_params("parallel", "parallel", "arbitrary"),
    )(q, k, v, table, *deps)


def _attn_bwd(name, q, k, v, table, dmix):
    b, s, w = q.shape
    sp = k.shape[1]

    def body(q_ref, k_ref, v_ref, t_ref, do_ref, dq_ref, dk_ref, dv_ref, dbe_ref, dbo_ref):
        bi = pl.program_id(1)
        i = pl.program_id(2)
        start = pl.multiple_of(i * QBLK, QBLK)
        win = pl.ds(start, WIN)

        @pl.when(i == 0)
        def _():
            dk_ref[...] = jnp.zeros_like(dk_ref)
            dv_ref[...] = jnp.zeros_like(dv_ref)

        @pl.when(jnp.logical_and(i == 0, bi == 0))
        def _():
            dbe_ref[...] = jnp.zeros_like(dbe_ref)
            dbo_ref[...] = jnp.zeros_like(dbo_ref)

        kw = k_ref[win, :]
        vw = v_ref[win, :]
        q2 = q_ref[...]
        do2 = do_ref[...].astype(BF16)
        lanes = [_head_lanes(hh) for hh in range(2)]
        qh = [jnp.where(mine, q2, jnp.zeros_like(q2)) for mine in lanes]
        doh = [jnp.where(mine, do2, jnp.zeros_like(do2)) for mine in lanes]
        p = [_attn_probs(qh[hh], kw, t_ref[hh], start) for hh in range(2)]
        dp = [_dot(doh[hh], vw, NT) for hh in range(2)]
        ds = [p[hh] * (dp[hh] - jnp.sum(p[hh] * dp[hh], axis=-1, keepdims=True)) for hh in range(2)]
        dsb = [(x * (ATTN_DH ** -0.5)).astype(BF16) for x in ds]
        pb = [x.astype(BF16) for x in p]
        dq = [_dot(dsb[hh], kw) for hh in range(2)]
        dk = [_dot(dsb[hh], qh[hh], TN) for hh in range(2)]
        dv = [_dot(pb[hh], doh[hh], TN) for hh in range(2)]
        for hh in range(2):
            for qi in range(Q_CHUNKS):
                c0 = (qi // 2) * LANES
                blk = ds[hh][qi * CHUNK:(qi + 1) * CHUNK, c0:c0 + DB_W]
                if qi % 2 == 0:
                    dbe_ref[hh] += blk
                else:
                    dbo_ref[hh] += blk
        dq_ref[...] = jnp.where(lanes[0], dq[0], dq[1])
        dk_ref[win, :] += dk[0] + dk[1]
        dv_ref[win, :] += dv[0] + dv[1]

    qspec = pl.BlockSpec((None, QBLK, LANES), lambda p, bi, i: (bi, i, p))
    kspec = pl.BlockSpec((None, sp, LANES), lambda p, bi, i: (bi, 0, p))
    dbspec = pl.BlockSpec((2, CHUNK, DB_W), lambda p, bi, i: (p, 0, 0))
    db_shape = jax.ShapeDtypeStruct((ATTN_HEADS, CHUNK, DB_W), F32)
    return pl.pallas_call(
        body,
        name=name,
        grid=(w // LANES, b, s // QBLK),
        in_specs=[qspec, kspec, kspec, pl.BlockSpec((2, QBLK, WIN), lambda p, bi, i: (p, 0, 0)), qspec],
        out_specs=[qspec, kspec, kspec, dbspec, dbspec],
        out_shape=[jax.ShapeDtypeStruct((b, s, w), F32), jax.ShapeDtypeStruct((b, sp, w), F32),
                   jax.ShapeDtypeStruct((b, sp, w), F32), db_shape, db_shape],
        compiler_params=_params("arbitrary", "arbitrary", "arbitrary"),
    )(q, k, v, table, dmix)


HQ_COL = 3 * ATTN_W // HGRN_DH
HF_COL = HQ_COL + HGRN_HEADS
HI_COL = HF_COL + HGRN_HEADS
HG_COL = HI_COL + HGRN_HEADS
HGRN_ROWS = 8 * CHUNK
HEAD_LANES = [slice(hh * HGRN_DH, (hh + 1) * HGRN_DH) for hh in range(HGRN_HEADS)]


def _tri(lower):
    r = lax.broadcasted_iota(jnp.int32, (CHUNK, CHUNK), 0)
    c = lax.broadcasted_iota(jnp.int32, (CHUNK, CHUNK), 1)
    return (r >= c) if lower else (r <= c)


def _hgrn_chunk(hq, hf, lb, tril):
    sig = _sigmoid(hf)
    f = lb + (1.0 - lb) * sig
    g = jnp.log(f)
    ones_l = jnp.where(tril, 1.0, 0.0).astype(BF16)
    b = _dot_exact_lhs(ones_l, g)
    bl = jnp.sum(g, axis=0, keepdims=True)
    rows = lax.broadcasted_iota(jnp.int32, g.shape, 0)
    bm = jnp.sum(jnp.where(rows <= CHUNK // 2, g, 0.0), axis=0, keepdims=True)
    sq = _sigmoid(hq)
    q = hq * sq
    k = 1.0 - f
    return sig, f, b, bl, bm, sq, q, k


def _hgrn_fwd(name, proj, attn, lb, go, b, s):
    nc = s // CHUNK
    t = b * s
    nblk = s // HGRN_ROWS
    cpb = HGRN_ROWS // CHUNK

    def body(hq_ref, hf_ref, hi_ref, hg_ref, attn_ref, lb_ref, go_ref, mix_ref, oraw_ref, st_ref, s_scr):
        tril = _tri(True)
        gov = go_ref[...]
        mix_ref[:, 0:ATTN_W] = attn_ref[...]

        @pl.when(pl.program_id(1) == 0)
        def _():
            s_scr[...] = jnp.zeros_like(s_scr)

        def step(c, carry):
            sl = pl.ds(pl.multiple_of(c * CHUNK, CHUNK), CHUNK)
            hg = hg_ref[sl, :]
            _, _, bb, bl, bm, _, q, k = _hgrn_chunk(hq_ref[sl, :], hf_ref[sl, :], lb_ref[...], tril)
            vb = hi_ref[sl, :].astype(BF16)
            qe = (q * jnp.exp(bb - bm)).astype(BF16)
            ke = (k * jnp.exp(bm - bb)).astype(BF16)
            qb = (q * jnp.exp(bb)).astype(BF16)
            kb = (k * jnp.exp(bl - bb)).astype(BF16)
            e_last = jnp.exp(bl)
            gate = _silu(hg)
            st = [s_scr[hh] for hh in range(HGRN_HEADS)]
            a = [jnp.where(tril, _dot(qe[:, hs], ke[:, hs], NT), 0.0).astype(BF16) for hs in HEAD_LANES]
            o_state = [_dot(qb[:, hs], st[hh].astype(BF16), NT) for hh, hs in enumerate(HEAD_LANES)]
            st_next = [st[hh] * e_last[:, hs] + _dot(vb[:, hs], kb[:, hs], TN) for hh, hs in enumerate(HEAD_LANES)]
            o = [_dot(a[hh], vb[:, hs]) + o_state[hh] for hh, hs in enumerate(HEAD_LANES)]
            ro = [(oh * lax.rsqrt(jnp.mean(oh * oh, axis=-1, keepdims=True) + RMS_EPS) * gov) * gate[:, hs]
                  for oh, hs in zip(o, HEAD_LANES)]
            for hh in range(HGRN_HEADS):
                st_ref[hh, c] = st[hh]
                s_scr[hh] = st_next[hh]
            mix_ref[sl, ATTN_W:ATTN_W + HGRN_W] = jnp.concatenate(ro, axis=1).astype(BF16)
            oraw_ref[sl, :] = jnp.concatenate(o, axis=1)
            return carry

        lax.fori_loop(0, cpb, step, 0)

    col = lambda base: pl.BlockSpec((HGRN_ROWS, HGRN_W), lambda bi, i: (bi * nblk + i, base // HGRN_HEADS))
    out = pl.BlockSpec((HGRN_ROWS, HGRN_W), lambda bi, i: (bi * nblk + i, 0))
    return pl.pallas_call(
        body,
        name=name,
        grid=(b, nblk),
        in_specs=[col(HQ_COL), col(HF_COL), col(HI_COL), col(HG_COL), out,
                  pl.BlockSpec((1, HGRN_W), lambda bi, i: (0, 0)), pl.BlockSpec((1, HGRN_DH), lambda bi, i: (0, 0))],
        out_specs=[pl.BlockSpec((HGRN_ROWS, ATTN_W + HGRN_W), lambda bi, i: (bi * nblk + i, 0)), out,
                   pl.BlockSpec((None, HGRN_HEADS, cpb, HGRN_DH, HGRN_DH), lambda bi, i: (bi, 0, i, 0, 0))],
        out_shape=[jax.ShapeDtypeStruct((t, ATTN_W + HGRN_W), BF16), jax.ShapeDtypeStruct((t, HGRN_W), F32),
                   jax.ShapeDtypeStruct((b, HGRN_HEADS, nc, HGRN_DH, HGRN_DH), F32)],
        scratch_shapes=[pltpu.VMEM((HGRN_HEADS, HGRN_DH, HGRN_DH), F32)],
        compiler_params=_params("parallel", "arbitrary"),
    )(proj, proj, proj, proj, attn, lb, go)


def _hgrn_bwd(name, proj, dqkv, lb, go, oraw, states, dmix, b, s):
    t = b * s
    nblk = s // HGRN_ROWS
    cpb = HGRN_ROWS // CHUNK

    def body(hq_ref, hf_ref, hi_ref, hg_ref, dq_ref, dk_ref, dv_ref, lb_ref, go_ref, oraw_ref, st_ref, dro_ref,
             dp_ref, dlb_ref, dgo_ref, ds_scr, dlb_scr, dgo_scr):
        tril = _tri(True)
        ones_u = jnp.where(_tri(False), 1.0, 0.0).astype(BF16)
        gov = go_ref[...]
        dp_ref[:, 0:ATTN_W] = dq_ref[...]
        dp_ref[:, ATTN_W:2 * ATTN_W] = dk_ref[...]
        dp_ref[:, 2 * ATTN_W:3 * ATTN_W] = dv_ref[...]

        @pl.when(pl.program_id(1) == 0)
        def _():
            ds_scr[...] = jnp.zeros_like(ds_scr)
            dlb_scr[...] = jnp.zeros_like(dlb_scr)
            dgo_scr[...] = jnp.zeros_like(dgo_scr)

        def step(ci, carry):
            c = cpb - 1 - ci
            sl = pl.ds(pl.multiple_of(c * CHUNK, CHUNK), CHUNK)
            hq = hq_ref[sl, :]
            hg = hg_ref[sl, :]
            sig, f, bb, bl, bm, sq, q, k = _hgrn_chunk(hq, hf_ref[sl, :], lb_ref[...], tril)
            vb = hi_ref[sl, :].astype(BF16)
            ebm = jnp.exp(bb - bm)
            embm = jnp.exp(bm - bb)
            eb = jnp.exp(bb)
            ebl = jnp.exp(bl - bb)
            e_last = jnp.exp(bl)
            qe = (q * ebm).astype(BF16)
            ke = (k * embm).astype(BF16)
            qb = (q * eb).astype(BF16)
            kb = (k * ebl).astype(BF16)
            st = [st_ref[hh, c] for hh in range(HGRN_HEADS)]
            dst = [ds_scr[hh] for hh in range(HGRN_HEADS)]
            o = oraw_ref[sl, :]
            dro = dro_ref[sl, :]
            sg = _sigmoid(hg)
            gov4 = jnp.concatenate([gov] * HGRN_HEADS, axis=1)
            rstd = jnp.concatenate(
                [jnp.broadcast_to(lax.rsqrt(jnp.mean(o[:, hs] * o[:, hs], axis=-1, keepdims=True) + RMS_EPS),
                                  (CHUNK, HGRN_DH)) for hs in HEAD_LANES], axis=1)
            ohat = o * rstd
            dn = dro * (hg * sg)
            dhg = dro * (ohat * gov4) * (sg * (1.0 + hg * (1.0 - sg)))
            dgo_inc = jnp.sum(dn * ohat, axis=0, keepdims=True)
            dohat = dn * gov4
            proj_h = dohat * ohat
            pm = jnp.concatenate(
                [jnp.broadcast_to(jnp.mean(proj_h[:, hs], axis=-1, keepdims=True), (CHUNK, HGRN_DH))
                 for hs in HEAD_LANES], axis=1)
            dob = (rstd * (dohat - ohat * pm)).astype(BF16)
            stb = [x.astype(BF16) for x in st]
            dstb = [x.astype(BF16) for x in dst]
            a = [jnp.where(tril, _dot(qe[:, hs], ke[:, hs], NT), 0.0).astype(BF16) for hs in HEAD_LANES]
            dab = [jnp.where(tril, _dot(dob[:, hs], vb[:, hs], NT), 0.0).astype(BF16) for hs in HEAD_LANES]
            dqb = [_dot(dob[:, hs], stb[hh]) for hh, hs in enumerate(HEAD_LANES)]
            dkb = [_dot(vb[:, hs], dstb[hh]) for hh, hs in enumerate(HEAD_LANES)]
            dv_state = [_dot(kb[:, hs], dstb[hh], NT) for hh, hs in enumerate(HEAD_LANES)]
            dst_next = [dst[hh] * e_last[:, hs] + _dot(dob[:, hs], qb[:, hs], TN) for hh, hs in enumerate(HEAD_LANES)]
            dv = [_dot(a[hh], dob[:, hs], TN) + dv_state[hh] for hh, hs in enumerate(HEAD_LANES)]
            dqe = jnp.concatenate([_dot(dab[hh], ke[:, hs]) for hh, hs in enumerate(HEAD_LANES)], axis=1)
            dke = jnp.concatenate([_dot(dab[hh], qe[:, hs], TN) for hh, hs in enumerate(HEAD_LANES)], axis=1)
            dqb = jnp.concatenate(dqb, axis=1)
            dkb = jnp.concatenate(dkb, axis=1)
            state_term = jnp.concatenate(
                [jnp.sum(dst[hh] * st[hh], axis=0, keepdims=True) for hh in range(HGRN_HEADS)], axis=1)
            dq = dqe * ebm + dqb * eb
            dk = dke * embm + dkb * ebl
            db = (qe.astype(F32) * dqe - ke.astype(F32) * dke) + q * (dqb * eb) - k * (dkb * ebl)
            d_last = jnp.sum(k * ebl * dkb, axis=0, keepdims=True) + state_term * e_last
            dg = _dot_exact_lhs(ones_u, db) + d_last
            df = dg / f - dk
            first = HQ_COL * HGRN_DH
            dp_ref[sl, first:first + HGRN_W] = (dq * (sq * (1.0 + hq * (1.0 - sq)))).astype(BF16)
            dp_ref[sl, first + HGRN_W:first + 2 * HGRN_W] = (df * (1.0 - lb_ref[...]) * sig * (1.0 - sig)).astype(BF16)
            dp_ref[sl, first + 2 * HGRN_W:first + 3 * HGRN_W] = jnp.concatenate(dv, axis=1).astype(BF16)
            dp_ref[sl, first + 3 * HGRN_W:first + 4 * HGRN_W] = dhg.astype(BF16)
            dlb_scr[...] += jnp.sum(df * (1.0 - sig), axis=0, keepdims=True)
            dgo_scr[...] += dgo_inc
            for hh in range(HGRN_HEADS):
                ds_scr[hh] = dst_next[hh]
            return carry

        lax.fori_loop(0, cpb, step, 0)

        @pl.when(pl.program_id(1) == nblk - 1)
        def _():
            dlb_ref[...] = dlb_scr[...]
            dgo_ref[...] = dgo_scr[...]

    rows = lambda bi, i: bi * nblk + (nblk - 1 - i)
    col = lambda base: pl.BlockSpec((HGRN_ROWS, HGRN_W), lambda bi, i: (rows(bi, i), base // HGRN_HEADS))
    out = pl.BlockSpec((HGRN_ROWS, HGRN_W), lambda bi, i: (rows(bi, i), 0))
    part = pl.BlockSpec((None, 1, HGRN_W), lambda bi, i: (bi, 0, 0))
    width = HG_COL * HGRN_DH + HGRN_W
    o_shape = jax.ShapeDtypeStruct((t, width), BF16)
    p_shape = jax.ShapeDtypeStruct((b, 1, HGRN_W), F32)
    return pl.pallas_call(
        body,
        name=name,
        grid=(b, nblk),
        in_specs=[col(HQ_COL), col(HF_COL), col(HI_COL), col(HG_COL), out, out, out,
                  pl.BlockSpec((1, HGRN_W), lambda bi, i: (0, 0)), pl.BlockSpec((1, HGRN_DH), lambda bi, i: (0, 0)), out,
                  pl.BlockSpec((None, HGRN_HEADS, cpb, HGRN_DH, HGRN_DH), lambda bi, i: (bi, 0, nblk - 1 - i, 0, 0)),
                  col(ATTN_W // HGRN_DH)],
        out_specs=[pl.BlockSpec((HGRN_ROWS, width), lambda bi, i: (rows(bi, i), 0))] + [part] * 2,
        out_shape=[o_shape] + [p_shape] * 2,
        scratch_shapes=[pltpu.VMEM((HGRN_HEADS, HGRN_DH, HGRN_DH), F32), pltpu.VMEM((1, HGRN_W), F32),
                        pltpu.VMEM((1, HGRN_W), F32)],
        compiler_params=_params("parallel", "arbitrary"),
    )(proj, proj, proj, proj, *dqkv, lb, go, oraw, states, dmix)


def _small_grads(name, dg1, dgm, dg2, dgq, dgk, dbe_t, dbo_t, dlb, dgo, lbp):
    d = dg1.shape[1]

    def body(dg1_ref, dgm_ref, dg2_ref, dgq_ref, dgk_ref, dbe_ref, dbo_ref, dlb_ref, dgo_ref, lbp_ref,
             g1_ref, gm_ref, g2_ref, gq_ref, gk_ref, rb_ref, lbg_ref, go_ref):
        g1_ref[...] = jnp.sum(dg1_ref[...], axis=0, keepdims=True)
        gm_ref[...] = jnp.sum(dgm_ref[...], axis=0, keepdims=True)
        g2_ref[...] = jnp.sum(dg2_ref[...], axis=0, keepdims=True)
        r = lax.broadcasted_iota(jnp.int32, (ATTN_W, ATTN_DH), 0)
        cidx = lax.broadcasted_iota(jnp.int32, (ATTN_W, ATTN_DH), 1)
        fold = jnp.where(jnp.bitwise_and(r, ATTN_DH - 1) == cidx, 1.0, 0.0).astype(BF16)
        gq_ref[...] = jnp.sum(_dot_exact_rhs(dgq_ref[...], fold), axis=0, keepdims=True)
        gk_ref[...] = jnp.sum(_dot_exact_rhs(dgk_ref[...], fold), axis=0, keepdims=True)
        gosum = jnp.sum(dgo_ref[...], axis=0, keepdims=True)
        go_ref[...] = (gosum[:, 0:HGRN_DH] + gosum[:, HGRN_DH:2 * HGRN_DH]
                       + gosum[:, 2 * HGRN_DH:3 * HGRN_DH] + gosum[:, 3 * HGRN_DH:4 * HGRN_DH])
        p0 = lbp_ref[0:1, :]
        p1 = lbp_ref[1:2, :]
        lbv = 1.0 / (1.0 + jnp.exp(p1 - p0))
        dp0 = jnp.sum(dlb_ref[...], axis=0, keepdims=True) * lbv * (1.0 - lbv)
        lbg_ref[0:1, :] = dp0
        lbg_ref[1:2, :] = -dp0
        acc = dbe_ref[CHUNK - 1] + pltpu.roll(dbo_ref[CHUNK - 1], DB_W - CHUNK, 1)
        for tq in range(CHUNK - 1):
            acc = acc + pltpu.roll(dbe_ref[tq], CHUNK - 1 - tq, 1) + pltpu.roll(dbo_ref[tq], DB_W - 1 - tq, 1)
        jidx = lax.broadcasted_iota(jnp.int32, (DB_W, N_REL_PAD), 0)
        ridx = lax.broadcasted_iota(jnp.int32, (DB_W, N_REL_PAD), 1)
        rel = jnp.clip(KPAD + CHUNK - 1 - jidx, -REL_CLIP, REL_CLIP) + REL_CLIP
        rb_ref[...] = _dot_exact_rhs(acc, jnp.where(rel == ridx, 1.0, 0.0).astype(BF16))

    ins = [dg1, dgm, dg2, dgq, dgk, dbe_t, dbo_t, dlb, dgo, lbp]
    outs = [jax.ShapeDtypeStruct((1, d), F32)] * 3 + [jax.ShapeDtypeStruct((1, ATTN_DH), F32)] * 2 + [
        jax.ShapeDtypeStruct((ATTN_HEADS, N_REL_PAD), F32), jax.ShapeDtypeStruct((2, HGRN_W), F32),
        jax.ShapeDtypeStruct((1, HGRN_DH), F32)]
    vm = pl.BlockSpec(memory_space=pltpu.VMEM)
    return pl.pallas_call(
        body,
        name=name,
        in_specs=[vm] * len(ins),
        out_specs=[vm] * len(outs),
        out_shape=outs,
        compiler_params=pltpu.CompilerParams(vmem_limit_bytes=VMEM_LIMIT),
    )(*ins)


def _adam_update(w, g, m, v):
    m2 = ADAM_B1 * m + (1.0 - ADAM_B1) * g
    v2 = ADAM_B2 * v + (1.0 - ADAM_B2) * (g * g)
    m_hat = m2 / (1.0 - ADAM_B1 ** ADAM_STEP)
    v_hat = v2 / (1.0 - ADAM_B2 ** ADAM_STEP)
    delta = -ADAM_LR * (m_hat / (jnp.sqrt(v_hat) + ADAM_EPS) + ADAM_WD * w)
    return delta, m2, v2


def _rows_tile(r):
    return r if r <= 512 or r % 512 else 512


def _pair_sum(name, grad, theirs, core):
    n, half, c = theirs.shape
    tr = _rows_tile(half)
    nth = half // tr

    def body(core_ref, a_ref, b_ref, o_ref):
        o_ref[...] = (a_ref[...].astype(F32) + b_ref[...].astype(F32)).astype(o_ref.dtype)

    spec = pl.BlockSpec((None, tr, c), lambda i, j, core_ref: (i, j, 0))
    return pl.pallas_call(
        body, name=name,
        grid_spec=pltpu.PrefetchScalarGridSpec(
            num_scalar_prefetch=1, grid=(n, nth),
            in_specs=[pl.BlockSpec((None, tr, c), lambda i, j, core_ref: (i, core_ref[0] * nth + j, 0)), spec],
            out_specs=spec),
        out_shape=pltpu.HBM((n, half, c), BF16), compiler_params=_params("parallel", "parallel"),
    )(core, grad, theirs)


def _chip_sum(name, own, parts, chip):
    _, half, c = own.shape
    tr = _rows_tile(half)

    def body(chip_ref, own_ref, p_ref, o_ref):
        me = chip_ref[0]
        mine = own_ref[...].astype(F32)
        flip_x, flip_y, flip_xy = (p_ref[i].astype(F32) for i in range(3))
        acc = None
        for k in range(N_CHIPS):
            rel = jnp.bitwise_xor(me, k)
            term = jnp.where(rel == 0, mine, jnp.where(rel == 2, flip_x, jnp.where(rel == 1, flip_y, flip_xy)))
            acc = term if acc is None else acc + term
        o_ref[...] = acc

    return pl.pallas_call(
        body, name=name,
        grid_spec=pltpu.PrefetchScalarGridSpec(
            num_scalar_prefetch=1, grid=(half // tr,),
            in_specs=[pl.BlockSpec((None, tr, c), lambda j, chip_ref: (chip_ref[0], j, 0)),
                      pl.BlockSpec((3, tr, c), lambda j, chip_ref: (0, j, 0))],
            out_specs=pl.BlockSpec((tr, c), lambda j, chip_ref: (j, 0))),
        out_shape=pltpu.HBM((half, c), F32), compiler_params=_params("parallel"),
    )(chip, own, parts)


def _adamw(name, w, g_mine, g_theirs, m, v, core):
    _, r, c = w.shape
    half = r // 2
    tr = _rows_tile(half)
    nth = half // tr

    def body(core_ref, w_ref, gm_ref, gt_ref, m_ref, v_ref, g_ref, d_ref, m2_ref, v2_ref):
        g = jnp.where(pl.program_id(0) == core_ref[0], gm_ref[...], gt_ref[...])
        delta, m2, v2 = _adam_update(w_ref[...], g, m_ref[...], v_ref[...])
        g_ref[...] = g
        d_ref[...] = delta
        m2_ref[...] = m2
        v2_ref[...] = v2

    full = pl.BlockSpec((None, tr, c), lambda h, j, core_ref: (0, h * nth + j, 0))
    part = pl.BlockSpec((tr, c), lambda h, j, core_ref: (j, 0))
    shape = jax.ShapeDtypeStruct((1, r, c), F32)
    return pl.pallas_call(
        body, name=name,
        grid_spec=pltpu.PrefetchScalarGridSpec(
            num_scalar_prefetch=1, grid=(2, nth), in_specs=[full, part, part, full, full], out_specs=[full] * 4),
        out_shape=[shape] * 4, compiler_params=_params("parallel", "parallel"),
    )(core, w, g_mine, g_theirs, m, v)


def _rel_bias_table(name, rel_bias):
    padded = jnp.pad(rel_bias, ((0, 0), (0, N_REL_PAD - N_REL)))

    def body(rb_ref, o_ref):
        ridx = lax.broadcasted_iota(jnp.int32, (N_REL_PAD, BAND), 0)
        sidx = lax.broadcasted_iota(jnp.int32, (N_REL_PAD, BAND), 1)
        rb = rb_ref[...]

        def step(tq, carry):
            rel = jnp.clip(tq + KPAD - sidx, -REL_CLIP, REL_CLIP) + REL_CLIP
            onehot = jnp.where(rel == ridx, 1.0, 0.0).astype(BF16)
            o_ref[tq] = _dot_exact_rhs(rb, onehot)
            return carry

        lax.fori_loop(0, CHUNK, step, 0)

    vm = pl.BlockSpec(memory_space=pltpu.VMEM)
    table = pl.pallas_call(
        body, name=name, in_specs=[vm], out_specs=vm,
        out_shape=jax.ShapeDtypeStruct((CHUNK, ATTN_HEADS, BAND), F32),
    )(padded)
    return table.transpose(1, 0, 2)


def _adamw_small(name, w, parts, m, v):
    def body(w_ref, p_ref, m_ref, v_ref, g_ref, d_ref, m2_ref, v2_ref):
        g = p_ref[0]
        for i in range(1, N_DEV):
            g = g + p_ref[i]
        delta, m2, v2 = _adam_update(w_ref[...], g, m_ref[...], v_ref[...])
        g_ref[...] = g
        d_ref[...] = delta
        m2_ref[...] = m2
        v2_ref[...] = v2

    vm = pl.BlockSpec(memory_space=pltpu.VMEM)
    shape = jax.ShapeDtypeStruct((SMALL_ROWS, SMALL_COLS), F32)
    return pl.pallas_call(
        body, name=name, in_specs=[vm] * 4, out_specs=[vm] * 4, out_shape=[shape] * 4,
    )(w, parts, m, v)


def _position():
    return lax.axis_index("x"), lax.axis_index("y"), lax.axis_index("c")


def _other_chips(x, y):
    return [(1 - x, y), (x, 1 - y), (1 - x, 1 - y)]


ANY = pl.BlockSpec(memory_space=pl.ANY)
PAIR_ID = 0


def _pair_handshake():
    x, y, c = _position()
    barrier = pltpu.get_barrier_semaphore()
    pl.semaphore_signal(barrier, inc=1, device_id=(x, y, 1 - c), device_id_type=MESH)
    pl.semaphore_wait(barrier, 1)


PAIR_CALL = pltpu.CompilerParams(collective_id=PAIR_ID)


HBM = pl.BlockSpec(memory_space=pltpu.HBM)
SEM = pl.BlockSpec(memory_space=pltpu.SEMAPHORE)
SPLIT_COPY = pltpu.SideEffectType.DATAFLOW_SIDE_EFFECTING


def _gather_copy(shards, outs, send_sem, recv_sem, i, j):
    x, y, c = _position()
    chips = _other_chips(x, y)
    half = shards[i].shape[0] // 2
    rows = pl.ds(pl.multiple_of(c * half, 16), half)
    return pltpu.make_async_remote_copy(
        src_ref=shards[i].at[rows, :], dst_ref=outs[i].at[2 * x + y, rows, :],
        send_sem=send_sem.at[3 * i + j], recv_sem=recv_sem.at[3 * i + j],
        device_id=(chips[j][0], chips[j][1], c), device_id_type=MESH)


def _gather_start(name, shards, after):
    n = len(shards)

    def body(*refs):
        srcs, outs = refs[:n], refs[n:2 * n]
        send_sem, recv_sem = refs[2 * n + len(after)], refs[2 * n + len(after) + 1]
        token = refs[-1]
        for i in range(n):
            for j in range(3):
                _gather_copy(srcs, outs, send_sem, recv_sem, i, j).start()
        token[...] = jnp.zeros_like(token)

    full = [(N_CHIPS,) + s.shape for s in shards]
    res = pl.pallas_call(
        body,
        name=name,
        in_specs=[HBM] * (2 * n) + [ANY] * len(after),
        out_specs=[SEM, SEM] + [HBM] * (2 * n) + [pl.BlockSpec(memory_space=pltpu.VMEM)],
        out_shape=[pltpu.SemaphoreType.DMA((3 * n,)), pltpu.SemaphoreType.DMA((3 * n,))]
        + [pltpu.HBM(s.shape, s.dtype) for s in shards]
        + [pltpu.HBM(shp, s.dtype) for shp, s in zip(full, shards)]
        + [jax.ShapeDtypeStruct((8, LANES), F32)],
        input_output_aliases={i: 2 + i for i in range(2 * n)},
        compiler_params=pltpu.CompilerParams(has_side_effects=SPLIT_COPY),
    )(*[pltpu.with_memory_space_constraint(s, pltpu.HBM) for s in shards],
      *[pltpu.with_memory_space_constraint(lax.empty(shp, s.dtype), pltpu.HBM) for shp, s in zip(full, shards)],
      *after)
    return res[0], res[1], list(res[2:2 + n]), list(res[2 + n:2 + 2 * n]), res[-1]


def _gather_wait(name, send_sem, recv_sem, shards, outs, after):
    n = len(shards)

    def body(*refs):
        srcs, out_refs = refs[:n], refs[n:2 * n]
        send_ref, recv_ref = refs[2 * n], refs[2 * n + 1]
        for i in range(n):
            for j in range(3):
                copy = _gather_copy(srcs, out_refs, send_ref, recv_ref, i, j)
                copy.wait_send()
                copy.wait_recv()

    res = pl.pallas_call(
        body,
        name=name,
        in_specs=[HBM] * (2 * n) + [SEM, SEM] + [ANY] * len(after),
        out_specs=[HBM] * (2 * n),
        out_shape=[pltpu.HBM(s.shape, s.dtype) for s in shards] + [pltpu.HBM(o.shape, o.dtype) for o in outs],
        input_output_aliases={i: i for i in range(2 * n)},
        compiler_params=pltpu.CompilerParams(has_side_effects=SPLIT_COPY),
    )(*shards, *outs, send_sem, recv_sem, *after)
    return list(res[:n]), list(res[n:])


def _join_copies(srcs, ins, outs, own_send, own_recv, half_send, half_recv):
    x, y, c = _position()
    chips = _other_chips(x, y)
    copies = []
    for i in range(len(srcs)):
        copies.append(pltpu.make_async_remote_copy(
            src_ref=srcs[i], dst_ref=outs[i].at[2 * x + y], send_sem=own_send.at[i], recv_sem=own_recv.at[i],
            device_id=(x, y, 1 - c), device_id_type=MESH))
        half = srcs[i].shape[0] // 2
        rows = pl.ds(pl.multiple_of(c * half, 16), half)
        for j in range(3):
            slot = 2 * chips[j][0] + chips[j][1]
            copies.append(pltpu.make_async_remote_copy(
                src_ref=ins[i].at[slot, rows, :], dst_ref=outs[i].at[slot, rows, :],
                send_sem=half_send.at[3 * i + j], recv_sem=half_recv.at[3 * i + j],
                device_id=(x, y, 1 - c), device_id_type=MESH))
    return copies


def _gather_join(name, shards, outs):
    n = len(shards)

    def body(*refs):
        _pair_handshake()
        copies = _join_copies(refs[:n], refs[n:2 * n], refs[2 * n:3 * n], *refs[3 * n:])
        for cp in copies:
            cp.start()
        for cp in copies:
            cp.wait()

    return pl.pallas_call(
        body,
        name=name,
        in_specs=[ANY] * (2 * n),
        out_specs=[HBM] * n,
        out_shape=[pltpu.HBM(o.shape, o.dtype) for o in outs],
        input_output_aliases={n + i: i for i in range(n)},
        scratch_shapes=[pltpu.SemaphoreType.DMA((n,))] * 2 + [pltpu.SemaphoreType.DMA((3 * n,))] * 2,
        compiler_params=PAIR_CALL,
    )(*shards, *outs)


def _join_start(name, shards, outs):
    n = len(shards)

    def body(*refs):
        _pair_handshake()
        srcs, arrs = refs[:n], refs[n:2 * n]
        sems = refs[2 * n:2 * n + 4]
        token = refs[-1]
        for cp in _join_copies(srcs, arrs, arrs, *sems):
            cp.start()
        token[...] = jnp.zeros_like(token)

    res = pl.pallas_call(
        body,
        name=name,
        in_specs=[HBM] * (2 * n),
        out_specs=[SEM] * 4 + [HBM] * (2 * n) + [pl.BlockSpec(memory_space=pltpu.VMEM)],
        out_shape=[pltpu.SemaphoreType.DMA((n,))] * 2 + [pltpu.SemaphoreType.DMA((3 * n,))] * 2
        + [pltpu.HBM(s.shape, s.dtype) for s in shards] + [pltpu.HBM(o.shape, o.dtype) for o in outs]
        + [jax.ShapeDtypeStruct((8, LANES), F32)],
        input_output_aliases={i: 4 + i for i in range(2 * n)},
        compiler_params=pltpu.CompilerParams(has_side_effects=SPLIT_COPY, collective_id=PAIR_ID),
    )(*shards, *outs)
    return list(res[:4]), list(res[4:4 + n]), list(res[4 + n:4 + 2 * n]), res[-1]


def _join_wait(name, sems, shards, outs, after):
    n = len(shards)

    def body(*refs):
        srcs, arrs = refs[:n], refs[n:2 * n]
        for cp in _join_copies(srcs, arrs, arrs, *refs[2 * n:2 * n + 4]):
            cp.wait_send()
            cp.wait_recv()

    res = pl.pallas_call(
        body,
        name=name,
        in_specs=[HBM] * (2 * n) + [SEM] * 4 + [ANY] * len(after),
        out_specs=[HBM] * (2 * n),
        out_shape=[pltpu.HBM(s.shape, s.dtype) for s in shards] + [pltpu.HBM(o.shape, o.dtype) for o in outs],
        input_output_aliases={i: i for i in range(2 * n)},
        compiler_params=pltpu.CompilerParams(has_side_effects=SPLIT_COPY),
    )(*shards, *outs, *sems, *after)
    return list(res[n:])


def _pair_copy(grads, lands, send_sem, recv_sem, i):
    x, y, c = _position()
    half = grads[i].shape[1] // 2
    give = pl.ds(pl.multiple_of((1 - c) * half, 16), half)
    return pltpu.make_async_remote_copy(
        src_ref=grads[i].at[:, give, :], dst_ref=lands[i], send_sem=send_sem.at[i], recv_sem=recv_sem.at[i],
        device_id=(x, y, 1 - c), device_id_type=MESH)


def _pair_start(name, grads):
    n = len(grads)

    def body(*refs):
        _pair_handshake()
        srcs, lands = refs[:n], refs[n:2 * n]
        send_sem, recv_sem = refs[2 * n], refs[2 * n + 1]
        token = refs[-1]
        for i in range(n):
            _pair_copy(srcs, lands, send_sem, recv_sem, i).start()
        token[...] = jnp.zeros_like(token)

    halves = [(g.shape[0], g.shape[1] // 2, g.shape[2]) for g in grads]
    res = pl.pallas_call(
        body,
        name=name,
        in_specs=[HBM] * (2 * n),
        out_specs=[SEM, SEM] + [HBM] * (2 * n) + [pl.BlockSpec(memory_space=pltpu.VMEM)],
        out_shape=[pltpu.SemaphoreType.DMA((n,)), pltpu.SemaphoreType.DMA((n,))]
        + [pltpu.HBM(g.shape, g.dtype) for g in grads]
        + [pltpu.HBM(shp, g.dtype) for shp, g in zip(halves, grads)]
        + [jax.ShapeDtypeStruct((8, LANES), F32)],
        input_output_aliases={i: 2 + i for i in range(2 * n)},
        compiler_params=pltpu.CompilerParams(has_side_effects=SPLIT_COPY, collective_id=PAIR_ID),
    )(*[pltpu.with_memory_space_constraint(g, pltpu.HBM) for g in grads],
      *[pltpu.with_memory_space_constraint(lax.empty(shp, g.dtype), pltpu.HBM) for shp, g in zip(halves, grads)])
    return res[0], res[1], list(res[2:2 + n]), list(res[2 + n:2 + 2 * n]), res[-1]


def _pair_wait(name, send_sem, recv_sem, grads, lands, after):
    n = len(grads)

    def body(*refs):
        srcs, land_refs = refs[:n], refs[n:2 * n]
        send_ref, recv_ref = refs[2 * n], refs[2 * n + 1]
        for i in range(n):
            copy = _pair_copy(srcs, land_refs, send_ref, recv_ref, i)
            copy.wait_send()
            copy.wait_recv()

    res = pl.pallas_call(
        body,
        name=name,
        in_specs=[HBM] * (2 * n) + [SEM, SEM, ANY],
        out_specs=[HBM] * (2 * n),
        out_shape=[pltpu.HBM(g.shape, g.dtype) for g in grads] + [pltpu.HBM(l.shape, l.dtype) for l in lands],
        input_output_aliases={i: i for i in range(2 * n)},
        compiler_params=pltpu.CompilerParams(has_side_effects=SPLIT_COPY),
    )(*grads, *lands, send_sem, recv_sem, after)
    return list(res[:n]), list(res[n:])


def _scatter_copy(srcs, lands, send_sem, recv_sem, i, j):
    x, y, c = _position()
    chips = _other_chips(x, y)
    return pltpu.make_async_remote_copy(
        src_ref=srcs[i].at[2 * chips[j][0] + chips[j][1]], dst_ref=lands[i].at[j],
        send_sem=send_sem.at[3 * i + j], recv_sem=recv_sem.at[3 * i + j],
        device_id=(chips[j][0], chips[j][1], c), device_id_type=MESH)


def _scatter_start(name, sums):
    n = len(sums)

    def body(*refs):
        srcs, lands = refs[:n], refs[n:2 * n]
        send_sem, recv_sem = refs[2 * n], refs[2 * n + 1]
        token = refs[-1]
        for i in range(n):
            for j in range(3):
                _scatter_copy(srcs, lands, send_sem, recv_sem, i, j).start()
        token[...] = jnp.zeros_like(token)

    land_shapes = [(3,) + s.shape[1:] for s in sums]
    res = pl.pallas_call(
        body,
        name=name,
        in_specs=[HBM] * (2 * n),
        out_specs=[SEM, SEM] + [HBM] * (2 * n) + [pl.BlockSpec(memory_space=pltpu.VMEM)],
        out_shape=[pltpu.SemaphoreType.DMA((3 * n,)), pltpu.SemaphoreType.DMA((3 * n,))]
        + [pltpu.HBM(s.shape, s.dtype) for s in sums]
        + [pltpu.HBM(shp, s.dtype) for shp, s in zip(land_shapes, sums)]
        + [jax.ShapeDtypeStruct((8, LANES), F32)],
        input_output_aliases={i: 2 + i for i in range(2 * n)},
        compiler_params=pltpu.CompilerParams(has_side_effects=SPLIT_COPY),
    )(*[pltpu.with_memory_space_constraint(s, pltpu.HBM) for s in sums],
      *[pltpu.with_memory_space_constraint(lax.empty(shp, s.dtype), pltpu.HBM) for shp, s in zip(land_shapes, sums)])
    return res[0], res[1], list(res[2:2 + n]), list(res[2 + n:2 + 2 * n]), res[-1]


def _scatter_wait(name, send_sem, recv_sem, sums, lands, after):
    n = len(sums)

    def body(*refs):
        srcs, land_refs = refs[:n], refs[n:2 * n]
        send_ref, recv_ref = refs[2 * n], refs[2 * n + 1]
        for i in range(n):
            for j in range(3):
                copy = _scatter_copy(srcs, land_refs, send_ref, recv_ref, i, j)
                copy.wait_send()
                copy.wait_recv()

    res = pl.pallas_call(
        body,
        name=name,
        in_specs=[HBM] * (2 * n) + [SEM, SEM, ANY],
        out_specs=[HBM] * (2 * n),
        out_shape=[pltpu.HBM(s.shape, s.dtype) for s in sums] + [pltpu.HBM(l.shape, l.dtype) for l in lands],
        input_output_aliases={i: i for i in range(2 * n)},
        compiler_params=pltpu.CompilerParams(has_side_effects=SPLIT_COPY),
    )(*sums, *lands, send_sem, recv_sem, after)
    return list(res[:n]), list(res[n:])


def _pair_join(name, halves, small=None):
    n = len(halves)
    if small is None:
        def body_plain(*refs):
            _pair_handshake()
            ins, outs = refs[:n], refs[n:2 * n]
            send_sem, recv_sem = refs[2 * n:]
            x, y, c = _position()
            swaps = [pltpu.make_async_remote_copy(
                src_ref=ins[i], dst_ref=outs[i], send_sem=send_sem.at[i], recv_sem=recv_sem.at[i],
                device_id=(x, y, 1 - c), device_id_type=MESH) for i in range(n)]
            for swap in swaps:
                swap.start()
            for swap in swaps:
                swap.wait()

        return pl.pallas_call(
            body_plain,
            name=name,
            in_specs=[ANY] * n,
            out_specs=[ANY] * n,
            out_shape=[jax.ShapeDtypeStruct(h.shape, h.dtype) for h in halves],
            scratch_shapes=[pltpu.SemaphoreType.DMA((n,))] * 2,
            compiler_params=PAIR_CALL,
        )(*halves)

    def body(*refs):
        ins, small_ref = refs[:n], refs[n]
        outs, all_ref = refs[n + 1:2 * n + 1], refs[2 * n + 1]
        send_sem, recv_sem, sm_send, sm_recv, sm_local = refs[2 * n + 2:]
        x, y, c = _position()
        swaps = []
        for i in range(n):
            swap = pltpu.make_async_remote_copy(
                src_ref=ins[i], dst_ref=outs[i], send_sem=send_sem.at[i], recv_sem=recv_sem.at[i],
                device_id=(x, y, 1 - c), device_id_type=MESH)
            swap.start()
            swaps.append(swap)
        me = 4 * x + 2 * y + c
        sm_own = pltpu.make_async_copy(small_ref, all_ref.at[me], sm_local)
        sm_own.start()
        pushes, arrivals = [], []
        for mask in range(1, N_DEV):
            px, py, pc = x ^ (mask >> 2), y ^ ((mask >> 1) & 1), c ^ (mask & 1)
            pushes.append(pltpu.make_async_remote_copy(
                src_ref=small_ref, dst_ref=all_ref.at[me], send_sem=sm_send.at[mask - 1], recv_sem=sm_recv.at[mask - 1],
                device_id=(px, py, pc), device_id_type=MESH))
            arrivals.append(pltpu.make_async_remote_copy(
                src_ref=small_ref, dst_ref=all_ref.at[4 * px + 2 * py + pc], send_sem=sm_send.at[mask - 1],
                recv_sem=sm_recv.at[mask - 1], device_id=(px, py, pc), device_id_type=MESH))
        for cp in pushes:
            cp.start()
        for swap in swaps:
            swap.wait()
        for cp in arrivals:
            cp.wait_recv()
        for cp in pushes:
            cp.wait_send()
        sm_own.wait()

    res = pl.pallas_call(
        body,
        name=name,
        in_specs=[ANY] * (n + 1),
        out_specs=[ANY] * (n + 1),
        out_shape=[jax.ShapeDtypeStruct(h.shape, h.dtype) for h in halves]
        + [jax.ShapeDtypeStruct((N_DEV,) + small.shape, small.dtype)],
        scratch_shapes=[pltpu.SemaphoreType.DMA((n,))] * 2 + [pltpu.SemaphoreType.DMA((N_DEV - 1,))] * 2
        + [pltpu.SemaphoreType.DMA(())],
    )(*halves, small)
    return res[:n], res[n]


def _lower_bound(lbp):
    return jax.nn.softmax(lbp, axis=0)[0:1]


def _local_step(x, target, g1, gm, g2, gq, gk, go, rel_bias, lbp, weights, on_grads, grads_sent):
    b, s, d = x.shape
    t = b * s
    x0 = x.reshape(t, d)
    tgt = target.reshape(t, d)
    gq_t = jnp.tile(gq, (1, ATTN_HEADS))
    gk_t = jnp.tile(gk, (1, ATTN_HEADS))
    lb = _lower_bound(lbp)
    table = _band_table(_rel_bias_table("rel_bias_table", rel_bias))

    h1 = _rmsnorm_fwd("norm1", x0, g1)
    wg1, wu1, deps1 = weights["first"]((h1, table))
    a1, b1, z1 = _ffn_up("ffn1_up", h1, wg1, wu1, deps1)
    wd1, deps_mid = weights["mid"]((z1,))
    x1, h2 = _ffn_down("ffn1_down", z1, wd1, x0, gm, deps_mid)
    w_in, w_out = weights["mid_rest"]((x1,))
    ns = w_in.shape[0]
    proj = _in_proj("in_proj", h2, w_in)
    proj3 = proj.reshape(b, s, proj.shape[1])
    qn, kn, vb = _qk_prep("qk_prep", proj3, gq_t, gk_t)
    attn = _attn_fwd("attn_fwd", qn, kn, vb, table, weights["last_begin"]((qn,))).reshape(t, ATTN_W)
    mix, oraw, states = _hgrn_fwd("hgrn_fwd", proj, attn, lb, go, b, s)
    x2, h3 = _out_proj("out_proj", mix, w_out, x1, g2)
    wg2, wu2, wd2 = weights["last"]((h3,))
    a2, b2, z2 = _ffn_up("ffn2_up", h3, wg2, wu2)
    dy, dyh, sq = _ffn_down_loss("ffn2_down_loss", z2, wd2, x2, tgt)
    loss = 0.5 * jnp.sum(sq) / d

    da2, db2 = _ffn_bwd_act("ffn2_bwd_act", dyh, wd2, a2, b2)
    dwd2 = _grad_w_cols("ffn2_dwd", z2, dyh)
    dwg2 = _grad_w_cols("ffn2_dwg", da2, h3)
    dwu2 = _grad_w_cols("ffn2_dwu", db2, h3)
    sent2 = on_grads("ffn2", {"ffn2_w_gate": dwg2, "ffn2_w_up": dwu2, "ffn2_w_down": dwd2})
    dx2, dx2b, dg2 = _ffn_bwd_in("ffn2_bwd_in", da2, db2, wg2, wu2, x2, g2, dy, 1.0, sent2)
    sent2 = grads_sent("ffn2", dx2b)

    dwout = _grad_w_out("dw_out", mix, dx2b)
    dmix = _out_proj_bwd("out_proj_bwd", dx2b, w_out, sent2)
    dqn, dkn, dvn, dbe, dbo = _attn_bwd("attn_bwd", qn, kn, vb, table, dmix.reshape(b, s, dmix.shape[1]))
    dpq, dpk, dpv, dgq, dgk = _qk_prep_bwd("qk_prep_bwd", proj3, dqn, dkn, dvn, gq_t, gk_t)
    dpq, dpk, dpv = (a.reshape(t, ATTN_W) for a in (dpq, dpk, dpv))
    dproj, dlb, dgo = _hgrn_bwd("hgrn_bwd", proj, (dpq, dpk, dpv), lb, go, oraw, states, dmix, b, s)
    dwin = _grad_w_in("dw_in", h2, dproj, ns)
    dx1, dx1h, dgm = _in_proj_bwd("in_proj_bwd", dproj, w_in, x1, gm, dx2, 0.5)

    dwd1 = _grad_w_cols("ffn1_dwd", z1, dx1h)
    sent_mix = on_grads("mix", {"w_in": dwin, "w_out": dwout.reshape(ns, dwout.shape[0] // ns, d),
                                "ffn1_w_down": dwd1})
    da1, db1 = _ffn_bwd_act("ffn1_bwd_act", dx1h, wd1, a1, b1, sent_mix)
    sent_mix = grads_sent("mix", da1)
    dwg1 = _grad_w_cols("ffn1_dwg", da1, h1, sent_mix)
    dwu1 = _grad_w_cols("ffn1_dwu", db1, h1)
    on_grads("ffn1", {"ffn1_w_gate": dwg1, "ffn1_w_up": dwu1})
    sent1 = grads_sent("ffn1", None)
    dx0, dg1 = _ffn_bwd_in("ffn1_bwd_in", da1, db1, wg1, wu1, x0, g1, dx1, None, sent1)

    nt = dg1.shape[0]
    sg = _small_grads(
        "small_grads", dg1.reshape(nt, d), dgm.reshape(nt, d), dg2.reshape(nt, d),
        dgq.reshape(-1, ATTN_W), dgk.reshape(-1, ATTN_W), dbe.transpose(1, 0, 2), dbo.transpose(1, 0, 2),
        dlb.reshape(b, HGRN_W), dgo.reshape(b, HGRN_W), lbp)
    g1g, gmg, g2g, gqg, gkg, rbg, lbg, gog = sg
    small = _pack_small(g1g, gmg, g2g, lbg, rbg[:, :N_REL], gqg, gkg, gog, loss)
    return dx0.reshape(b, s, d), small


LOSS_SLOT = 7 * SMALL_COLS + 2 * ATTN_DH + HGRN_DH


def _pack_small(g1, gm, g2, lbp, rel_bias, gq, gk, go, loss=None):
    flat = [g1.reshape(-1), gm.reshape(-1), g2.reshape(-1), lbp.reshape(-1), rel_bias.reshape(-1)]
    n_bias = 3 * SMALL_COLS - rel_bias.size
    heads = [gq.reshape(-1), gk.reshape(-1), go.reshape(-1)]
    heads.append(jnp.zeros((1,), F32) if loss is None else loss.reshape(1))
    n_tail = SMALL_COLS - sum(h.size for h in heads)
    return jnp.concatenate(flat + [jnp.zeros((n_bias,), F32)] + heads + [jnp.zeros((n_tail,), F32)]).reshape(
        SMALL_ROWS, SMALL_COLS)


def _unpack_small(p, d):
    flat = p.reshape(-1)
    o = 3 * d
    g1, gm, g2 = p[0:1], p[1:2], p[2:3]
    lbp = flat[o:o + 2 * HGRN_W].reshape(2, HGRN_W)
    o = 4 * SMALL_COLS
    rel = flat[o:o + ATTN_HEADS * N_REL].reshape(1, ATTN_HEADS, N_REL)
    o = 7 * SMALL_COLS
    gq = flat[o:o + ATTN_DH].reshape(1, ATTN_DH)
    gk = flat[o + ATTN_DH:o + 2 * ATTN_DH].reshape(1, ATTN_DH)
    go = flat[o + 2 * ATTN_DH:o + 2 * ATTN_DH + HGRN_DH].reshape(1, HGRN_DH)
    return g1, gm, g2, gq, gk, rel, lbp, go


def kernel(x, ffn1_norm_g, ffn1_w_gate, ffn1_w_up, ffn1_w_down, mix_norm_g, w_in, attn_q_norm_g, attn_k_norm_g, attn_rel_bias, hgrn_lower_bounds, hgrn_out_norm_g, w_out, ffn2_norm_g, ffn2_w_gate, ffn2_w_up, ffn2_w_down, loss_target, m_ffn1_norm_g, m_ffn1_w_gate, m_ffn1_w_up, m_ffn1_w_down, m_mix_norm_g, m_w_in, m_attn_q_norm_g, m_attn_k_norm_g, m_attn_rel_bias, m_hgrn_lower_bounds, m_hgrn_out_norm_g, m_w_out, m_ffn2_norm_g, m_ffn2_w_gate, m_ffn2_w_up, m_ffn2_w_down, v_ffn1_norm_g, v_ffn1_w_gate, v_ffn1_w_up, v_ffn1_w_down, v_mix_norm_g, v_w_in, v_attn_q_norm_g, v_attn_k_norm_g, v_attn_rel_bias, v_hgrn_lower_bounds, v_hgrn_out_norm_g, v_w_out, v_ffn2_norm_g, v_ffn2_w_gate, v_ffn2_w_up, v_ffn2_w_down):
    d = x.shape[-1]
    big_w = [ffn1_w_gate, ffn1_w_up, ffn1_w_down, w_in, w_out, ffn2_w_gate, ffn2_w_up, ffn2_w_down]
    big_m = [m_ffn1_w_gate, m_ffn1_w_up, m_ffn1_w_down, m_w_in, m_w_out, m_ffn2_w_gate, m_ffn2_w_up, m_ffn2_w_down]
    big_v = [v_ffn1_w_gate, v_ffn1_w_up, v_ffn1_w_down, v_w_in, v_w_out, v_ffn2_w_gate, v_ffn2_w_up, v_ffn2_w_down]
    big_names = ["ffn1_w_gate", "ffn1_w_up", "ffn1_w_down", "w_in", "w_out", "ffn2_w_gate", "ffn2_w_up", "ffn2_w_down"]
    flipped = {nm for nm in big_names if nm.endswith("gate") or nm.endswith("up")}
    flip = lambda nm, a: jnp.swapaxes(a, 1, 2) if nm in flipped else a
    big_w, big_m, big_v = ([flip(nm, a) for nm, a in zip(big_names, arrs)] for arrs in (big_w, big_m, big_v))

    shards = [w[0].astype(BF16) for w in big_w]
    start_a = _gather_start("gather_start_up1", shards[:2], ())
    start_b = _gather_start("gather_start_mid", shards[2:5], (start_a[4],))
    start_c = _gather_start("gather_start_ffn2", shards[5:], (start_b[4],))

    pending = {}

    def arrived(tag, started, after):
        send_sem, recv_sem, srcs, outs, _ = started
        return _gather_wait("gather_wait_" + tag, send_sem, recv_sem, srcs, outs, after)

    def first_weights(after):
        return (*_gather_join("gather_join_up1", *arrived("up1", start_a, after)), (start_c[4],))

    def mid_weights(after):
        srcs, outs = arrived("mid", start_b, after)
        (wd1,) = _gather_join("gather_join_wd1", srcs[:1], outs[:1])
        pending["mid"] = _join_start("join_start_mid", srcs[1:], outs[1:])
        return wd1, (pending["mid"][3],)

    def mid_rest(after):
        sems, srcs, outs, _ = pending["mid"]
        win_f, wout_f = _join_wait("join_wait_mid", sems, srcs, outs, after)
        return win_f, wout_f.reshape(wout_f.shape[0] * wout_f.shape[1], d)

    def last_begin(after):
        pending["ffn2"] = _join_start("join_start_ffn2", *arrived("ffn2", start_c, after))
        return (pending["ffn2"][3],)

    def last_weights(after):
        sems, srcs, outs, _ = pending["ffn2"]
        return _join_wait("join_wait_ffn2", sems, srcs, outs, after)

    weights = {"first": first_weights, "mid": mid_weights, "mid_rest": mid_rest, "last_begin": last_begin,
               "last": last_weights}

    core = lax.axis_index("c").astype(jnp.int32).reshape(1)
    chip = (2 * lax.axis_index("x") + lax.axis_index("y")).astype(jnp.int32).reshape(1)
    started = {}

    def on_grads(tag, grads):
        names = list(grads)
        started[tag] = (names, _pair_start("pair_start_" + tag, [grads[nm] for nm in names]))
        return (started[tag][1][4],)

    def grads_sent(tag, after):
        names, (send_sem, recv_sem, grads, lands, token) = started[tag]
        grads, theirs = _pair_wait("pair_wait_" + tag, send_sem, recv_sem, grads, lands, token if after is None else after)
        sums = [_pair_sum("pair_sum_" + nm, g, th, core) for nm, g, th in zip(names, grads, theirs)]
        started[tag] = (names, _scatter_start("scatter_start_" + tag, sums))
        return (started[tag][1][4],)

    grad_x, small_g = _local_step(
        x, loss_target, ffn1_norm_g, mix_norm_g, ffn2_norm_g, attn_q_norm_g, attn_k_norm_g, hgrn_out_norm_g,
        attn_rel_bias[0], hgrn_lower_bounds, weights, on_grads, grads_sent)

    def finish(tag, after):
        names, (send_sem, recv_sem, sums, lands, _) = started[tag]
        sums, lands = _scatter_wait("scatter_wait_" + tag, send_sem, recv_sem, sums, lands, after)
        return names, [_chip_sum("chip_sum_" + nm, sm, ld, chip) for nm, sm, ld in zip(names, sums, lands)]

    by_name = {nm: (w, m, v) for nm, w, m, v in zip(big_names, big_w, big_m, big_v)}
    updated = {}

    def update(names, halves, other_halves):
        for nm, mine, theirs in zip(names, halves, other_halves):
            w, m, v = by_name[nm]
            updated[nm] = _adamw("adamw_" + nm, w, mine, theirs, m, v, core)

    last_token = started["ffn1"][1][4]
    names_a, halves_a = finish("ffn2", last_token)
    names_m, halves_m = finish("mix", last_token)
    names_a, halves_a = names_a + names_m, halves_a + halves_m
    update(names_a, halves_a, _pair_join("pair_join_early", halves_a))
    names_b, halves_b = finish("ffn1", updated[names_a[-1]][1])
    others_b, small_all = _pair_join("pair_join_last", halves_b, small_g)
    update(names_b, halves_b, others_b)
    big_out = [updated[nm] for nm in big_names]

    pack = lambda g1, gm, g2, gq, gk, rel, lbp, go: _pack_small(g1, gm, g2, lbp, rel[0], gq, gk, go)
    small_w = pack(ffn1_norm_g, mix_norm_g, ffn2_norm_g, attn_q_norm_g, attn_k_norm_g, attn_rel_bias, hgrn_lower_bounds, hgrn_out_norm_g)
    small_m = pack(m_ffn1_norm_g, m_mix_norm_g, m_ffn2_norm_g, m_attn_q_norm_g, m_attn_k_norm_g, m_attn_rel_bias, m_hgrn_lower_bounds, m_hgrn_out_norm_g)
    small_v = pack(v_ffn1_norm_g, v_mix_norm_g, v_ffn2_norm_g, v_attn_q_norm_g, v_attn_k_norm_g, v_attn_rel_bias, v_hgrn_lower_bounds, v_hgrn_out_norm_g)
    small_res = _adamw_small("adamw_small", small_w, small_all, small_m, small_v)
    small_out = [_unpack_small(p, d) for p in small_res]
    loss = small_res[0].reshape(-1)[LOSS_SLOT]

    def assemble(kind):
        bg = [flip(nm, o[kind]) for nm, o in zip(big_names, big_out)]
        g1, gm, g2, gq, gk, rel, lbp, go = small_out[kind]
        return [g1, bg[0], bg[1], bg[2], gm, bg[3], gq, gk, rel, lbp, go, bg[4], g2, bg[5], bg[6], bg[7]]

    return (loss, grad_x, *assemble(0), *assemble(1), *assemble(2), *assemble(3))
```

```python
import functools

import jax
import jax.numpy as jnp
from jax import lax
from jax.experimental import pallas as pl
from jax.experimental.pallas import tpu as pltpu

F32 = jnp.float32
BF16 = jnp.bfloat16
MESH = pl.DeviceIdType.MESH

N_CHIPS = 4
N_DEV = 8
CHUNK = 64
ATTN_HEADS = 8
ATTN_DH = 64
ATTN_W = ATTN_HEADS * ATTN_DH
HGRN_HEADS = 4
HGRN_DH = 128
HGRN_W = HGRN_HEADS * HGRN_DH
LEFT_CHUNKS = 8
BAND = (LEFT_CHUNKS + 1) * CHUNK
KPAD = LEFT_CHUNKS * CHUNK
REL_CLIP = 128
N_REL = 2 * REL_CLIP + 1
N_REL_PAD = 384
RMS_EPS = 1e-6
LANES = 128
SMALL_ROWS = 8
SMALL_COLS = 1024

ADAM_LR = 0.001
ADAM_B1 = 0.9
ADAM_B2 = 0.999
ADAM_EPS = 1e-08
ADAM_WD = 0.01
ADAM_STEP = 10

NN = (((1,), (0,)), ((), ()))
NT = (((1,), (1,)), ((), ()))
TN = (((0,), (0,)), ((), ()))

VMEM_LIMIT = 48 * 1024 * 1024
MXU_WIDTH = 256
COL_CHUNK = 3 * MXU_WIDTH


def _sigmoid(x):
    return 1.0 / (1.0 + jnp.exp(-x))


def _silu(x):
    return x * _sigmoid(x)


def _dot(a, b, dims=NN):
    return lax.dot_general(a, b, dims, preferred_element_type=F32)


def _split3(x):
    hi = x.astype(BF16)
    r1 = x - hi.astype(F32)
    mid = r1.astype(BF16)
    lo = (r1 - mid.astype(F32)).astype(BF16)
    return hi, mid, lo


def _dot_exact_rhs(x, mat, dims=NN, pieces=3):
    hi, mid, lo = _split3(x)
    out = _dot(hi, mat, dims) + _dot(mid, mat, dims)
    return out + _dot(lo, mat, dims) if pieces == 3 else out


def _dot_exact_lhs(mat, x, dims=NN):
    hi, mid, lo = _split3(x)
    return _dot(mat, hi, dims) + _dot(mat, mid, dims) + _dot(mat, lo, dims)


def _params(*sem):
    return pltpu.CompilerParams(dimension_semantics=sem, vmem_limit_bytes=VMEM_LIMIT)


def _mm(name, ins, terms, n_acc, grid, acc_shape, outs, epilogue, extras=(), deps=()):
    nk = grid[2]
    ni, ne, nd, no = len(ins), len(extras), len(deps), len(outs)

    def body(*refs):
        in_refs = refs[:ni]
        ex_refs = refs[ni:ni + ne]
        out_refs = refs[ni + ne + nd:ni + ne + nd + no]
        acc_refs = refs[ni + ne + nd + no:]

        def products():
            parts = [None] * n_acc
            for ai, li, ri, dims in terms:
                d = _dot(in_refs[li][...], in_refs[ri][...], dims)
                parts[ai] = d if parts[ai] is None else parts[ai] + d
            return parts

        def finish(accs):
            res = epilogue(accs, [e[...] for e in ex_refs])
            for o, r in zip(out_refs, res):
                o[...] = r.astype(o.dtype)

        if nk == 1:
            finish(products())
        else:
            k = pl.program_id(2)

            @pl.when(k == 0)
            def _():
                for a, p in zip(acc_refs, products()):
                    a[...] = p

            if nk > 2:
                @pl.when(jnp.logical_and(k > 0, k < nk - 1))
                def _():
                    for a, p in zip(acc_refs, products()):
                        a[...] += p

            @pl.when(k == nk - 1)
            def _():
                finish([a[...] + p for a, p in zip(acc_refs, products())])

    scratch = [] if nk == 1 else [pltpu.VMEM(acc_shape, F32) for _ in range(n_acc)]
    res = pl.pallas_call(
        body,
        name=name,
        grid=grid,
        in_specs=[s for _, s in ins] + [s for _, s in extras] + [pl.BlockSpec(memory_space=pl.ANY)] * nd,
        out_specs=[s for _, s in outs],
        out_shape=[o for o, _ in outs],
        scratch_shapes=scratch,
        compiler_params=_params("parallel", "parallel", "arbitrary"),
    )(*[a for a, _ in ins], *[a for a, _ in extras], *deps)
    return res


def _staged_shape(w):
    return w.shape if len(w.shape) == 2 else (w.shape[1], w.shape[0] * w.shape[2])


def _stage_weights(w_hbm, w_vmem, sem):
    @pl.when(pl.program_id(0) == 0)
    def _():
        copies = []
        for p, (h, v) in enumerate(zip(w_hbm, w_vmem)):
            if len(h.shape) == 2:
                copies.append(pltpu.make_async_copy(h, v, sem.at[p, 0]))
            else:
                pj = h.shape[2]
                copies += [pltpu.make_async_copy(h.at[j], v.at[:, pl.ds(j * pj, pj)], sem.at[p, j])
                           for j in range(h.shape[0])]
        for cp in copies:
            cp.start()
        for cp in copies:
            cp.wait()


def _staging_scratch(weights):
    return [pltpu.VMEM(_staged_shape(w), w.dtype) for w in weights] + [pltpu.SemaphoreType.DMA((len(weights), N_CHIPS))]


def _mm_rows(name, lhs, weights, dims, t, outs, epilogue, extras=(), deps=()):
    tm = _row_tile(t)
    nl, ne, nd, no = len(lhs), len(extras), len(deps), len(outs)

    def body(*refs):
        lhs_refs = refs[:nl]
        w_hbm = refs[nl:2 * nl]
        ex_refs = refs[2 * nl:2 * nl + ne]
        out_refs = refs[2 * nl + ne + nd:2 * nl + ne + nd + no]
        w_vmem = refs[2 * nl + ne + nd + no:3 * nl + ne + nd + no]
        _stage_weights(w_hbm, w_vmem, refs[-1])

        acc = None
        for p in range(nl):
            part = _dot(lhs_refs[p][...], w_vmem[p][...], dims)
            acc = part if acc is None else acc + part
        res = epilogue([acc], [e[...] for e in ex_refs])
        for o, r in zip(out_refs, res):
            o[...] = r.astype(o.dtype)

    return pl.pallas_call(
        body,
        name=name,
        grid=(t // tm,),
        in_specs=[s for _, s in lhs] + [pl.BlockSpec(memory_space=pl.ANY)] * nl + [s for _, s in extras]
        + [pl.BlockSpec(memory_space=pl.ANY)] * nd,
        out_specs=[s for _, s in outs],
        out_shape=[o for o, _ in outs],
        scratch_shapes=_staging_scratch(weights),
        compiler_params=_params("arbitrary"),
    )(*[a for a, _ in lhs], *weights, *[a for a, _ in extras], *deps)


def _col_chunks(f):
    return [(c, min(COL_CHUNK, f - c)) for c in range(0, f, COL_CHUNK)]


def _mm_cols(name, x, weights, dims, n_out, epilogue, extras=(), deps=(), out_dtype=BF16):
    t, k = x.shape
    f = _staged_shape(weights[0])[0 if dims == NT else 1]
    tm = _row_tile(t)
    chunks = _col_chunks(f)
    nw, ne, nd = len(weights), len(extras), len(deps)

    def body(*refs):
        x_ref = refs[0]
        w_hbm = refs[1:1 + nw]
        ex_refs = refs[1 + nw:1 + nw + ne]
        out_refs = refs[1 + nw + ne + nd:1 + nw + ne + nd + n_out]
        w_vmem = refs[1 + nw + ne + nd + n_out:1 + 2 * nw + ne + nd + n_out]
        _stage_weights(w_hbm, w_vmem, refs[-1])

        xv = x_ref[...]

        def dots(c):
            c0, cw = chunks[c]
            return [_dot(xv, w[c0:c0 + cw, :] if dims == NT else w[:, c0:c0 + cw], dims) for w in w_vmem]

        accs = dots(0)
        for c, (c0, cw) in enumerate(chunks):
            nxt = dots(c + 1) if c + 1 < len(chunks) else None
            res = epilogue(accs, [e[:, c0:c0 + cw] for e in ex_refs])
            for o, r in zip(out_refs, res):
                o[:, c0:c0 + cw] = r.astype(o.dtype)
            accs = nxt

    act = pl.BlockSpec((tm, f), lambda i: (i, 0))
    return pl.pallas_call(
        body,
        name=name,
        grid=(t // tm,),
        in_specs=[pl.BlockSpec((tm, k), lambda i: (i, 0))] + [pl.BlockSpec(memory_space=pl.ANY)] * nw + [act] * ne
        + [pl.BlockSpec(memory_space=pl.ANY)] * nd,
        out_specs=[act] * n_out,
        out_shape=[jax.ShapeDtypeStruct((t, f), out_dtype)] * n_out,
        scratch_shapes=_staging_scratch(weights),
        compiler_params=_params("arbitrary"),
    )(x, *weights, *extras, *deps)


def _row_tile(t):
    return 512 if t % 512 == 0 else t


def _k_tile(t):
    return t if t <= 4096 else 1024


def _grad_k_tile(t):
    return 2048 if t % 2048 == 0 else t


def _rmsnorm(xv, g):
    ms = jnp.mean(xv * xv, axis=-1, keepdims=True)
    return xv * lax.rsqrt(ms + RMS_EPS) * g


def _rmsnorm_fwd(name, x, g):
    t, d = x.shape
    tm = _row_tile(t)

    def body(x_ref, g_ref, h_ref):
        h_ref[...] = _rmsnorm(x_ref[...], g_ref[...]).astype(BF16)

    return pl.pallas_call(
        body,
        name=name,
        grid=(t // tm,),
        in_specs=[pl.BlockSpec((tm, d), lambda i: (i, 0)), pl.BlockSpec((1, d), lambda i: (0, 0))],
        out_specs=pl.BlockSpec((tm, d), lambda i: (i, 0)),
        out_shape=jax.ShapeDtypeStruct((t, d), BF16),
        compiler_params=_params("parallel"),
    )(x, g)


def _norm_bwd_epilogue(copy_scale):
    def epilogue(accs, ex):
        dh = accs[0]
        xv, g, dres = ex
        ms = jnp.mean(xv * xv, axis=-1, keepdims=True)
        rstd = lax.rsqrt(ms + RMS_EPS)
        xhat = xv * rstd
        dxhat = dh * g
        dx = rstd * (dxhat - xhat * jnp.mean(dxhat * xhat, axis=-1, keepdims=True))
        out = dres + dx
        dg = jnp.sum(dh * xhat, axis=0, keepdims=True)
        if copy_scale is None:
            return out, dg
        return out, out * copy_scale, dg

    return epilogue


def _merged(w):
    return w.reshape(-1, w.shape[-1])


def _ffn_up(name, h, wg, wu, deps=()):
    def epilogue(accs, ex):
        a, b = accs
        sg = _sigmoid(a)
        act = a * sg
        return act, b * (sg * (1.0 + a * (1.0 - sg))), act * b

    return _mm_cols(name, h, [_merged(wg), _merged(wu)], NT, 3, epilogue, deps=deps)


def _whole_rows(arr, tm):
    return arr, pl.BlockSpec((tm, arr.shape[1]), lambda i: (i, 0))


def _ffn_down(name, z, wd, x, g_next, deps=()):
    t = z.shape[0]
    d = wd.shape[2]
    tm = _row_tile(t)
    row = pl.BlockSpec((tm, d), lambda i: (i, 0))

    def epilogue(accs, ex):
        y = ex[0] + 0.5 * accs[0]
        return y, _rmsnorm(y, ex[1])

    return _mm_rows(
        name, [_whole_rows(z, tm)], [_merged(wd)], NN, t,
        outs=[(jax.ShapeDtypeStruct((t, d), F32), row), (jax.ShapeDtypeStruct((t, d), BF16), row)],
        epilogue=epilogue,
        extras=[(x, row), (g_next, pl.BlockSpec((1, d), lambda i: (0, 0)))],
        deps=deps,
    )


def _ffn_down_loss(name, z, wd, x, target):
    t = z.shape[0]
    d = wd.shape[2]
    tm = _row_tile(t)
    nt = t // tm
    row = pl.BlockSpec((tm, d), lambda i: (i, 0))

    def epilogue(accs, ex):
        e = ex[0] + 0.5 * accs[0] - ex[1]
        dy = e * (1.0 / d)
        return dy, 0.5 * dy, jnp.sum(e * e, axis=0, keepdims=True)

    return _mm_rows(
        name, [_whole_rows(z, tm)], [_merged(wd)], NN, t,
        outs=[(jax.ShapeDtypeStruct((t, d), F32), row), (jax.ShapeDtypeStruct((t, d), BF16), row),
              (jax.ShapeDtypeStruct((nt, 1, d), F32), pl.BlockSpec((None, 1, d), lambda i: (i, 0, 0)))],
        epilogue=epilogue,
        extras=[(x, row), (target, row)],
    )


def _ffn_bwd_act(name, dout, wd, act_a, dact_b, deps=()):
    def epilogue(accs, ex):
        dz = accs[0]
        return dz * ex[1].astype(F32), dz * ex[0].astype(F32)

    return _mm_cols(name, dout, [_merged(wd)], NT, 2, epilogue, extras=[act_a, dact_b], deps=deps)


def _grad_w_cols(name, z, dout, deps=()):
    t, f = z.shape
    d = dout.shape[1]
    tk = _grad_k_tile(t)
    fh = f // 2
    dw = _mm(
        name,
        ins=[(z, pl.BlockSpec((tk, fh), lambda j, n, k: (k, j))),
             (dout, pl.BlockSpec((tk, d), lambda j, n, k: (k, 0)))],
        terms=[(0, 0, 1, TN)],
        n_acc=1,
        grid=(2, 1, t // tk),
        acc_shape=(fh, d),
        outs=[(pltpu.HBM((f, d), BF16), pl.BlockSpec((fh, d), lambda j, n, k: (j, 0)))],
        epilogue=lambda accs, ex: (accs[0],),
        deps=deps,
    )[0]
    return dw.reshape(N_CHIPS, f // N_CHIPS, d)


def _norm_bwd_outs(t, d, tm, copy_scale):
    row = pl.BlockSpec((tm, d), lambda i: (i, 0))
    outs = [(jax.ShapeDtypeStruct((t, d), F32), row)]
    if copy_scale is not None:
        outs.append((jax.ShapeDtypeStruct((t, d), BF16), row))
    outs.append((jax.ShapeDtypeStruct((t // tm, 1, d), F32), pl.BlockSpec((None, 1, d), lambda i: (i, 0, 0))))
    return row, outs


def _ffn_bwd_in(name, da, db, wg, wu, x, g, dres, copy_scale, deps=()):
    t = da.shape[0]
    d = wg.shape[2]
    tm = _row_tile(t)
    row, outs = _norm_bwd_outs(t, d, tm, copy_scale)
    return _mm_rows(
        name, [_whole_rows(da, tm), _whole_rows(db, tm)], [_merged(wg), _merged(wu)], NN, t,
        outs=outs,
        epilogue=_norm_bwd_epilogue(copy_scale),
        extras=[(x, row), (g, pl.BlockSpec((1, d), lambda i: (0, 0))), (dres, row)],
        deps=deps,
    )


def _in_proj(name, h, w_in):
    return _mm_cols(name, h, [w_in], NN, 1, lambda accs, ex: (accs[0],), out_dtype=F32)[0]


def _in_proj_bwd(name, dp, w_in, x, g, dres, copy_scale, deps=()):
    t = dp.shape[0]
    d = w_in.shape[1]
    tm = _row_tile(t)
    row, outs = _norm_bwd_outs(t, d, tm, copy_scale)
    return _mm_rows(
        name, [_whole_rows(dp, tm)], [w_in], NT, t,
        outs=outs,
        epilogue=_norm_bwd_epilogue(copy_scale),
        extras=[(x, row), (g, pl.BlockSpec((1, d), lambda i: (0, 0))), (dres, row)],
        deps=deps,
    )


def _grad_w_in(name, h, dp, ns):
    t, d = h.shape
    pj = dp.shape[1] // ns
    tk = _k_tile(t)
    return _mm(
        name,
        ins=[(h, pl.BlockSpec((tk, d), lambda j, n, k: (k, 0))),
             (dp, pl.BlockSpec((tk, pj), lambda j, n, k: (k, j)))],
        terms=[(0, 0, 1, TN)],
        n_acc=1,
        grid=(ns, 1, t // tk),
        acc_shape=(d, pj),
        outs=[(pltpu.HBM((ns, d, pj), BF16), pl.BlockSpec((None, d, pj), lambda j, n, k: (j, 0, 0)))],
        epilogue=lambda accs, ex: (accs[0],),
    )[0]


def _out_proj(name, mix, w_out, x, g_next):
    t, dm = mix.shape
    d = w_out.shape[1]
    tm = _row_tile(t)
    row = pl.BlockSpec((tm, d), lambda i, n, k: (i, 0))
    return _mm(
        name,
        ins=[(mix, pl.BlockSpec((tm, dm), lambda i, n, k: (i, 0))),
             (w_out, pl.BlockSpec((dm, d), lambda i, n, k: (0, 0)))],
        terms=[(0, 0, 1, NN)],
        n_acc=1,
        grid=(t // tm, 1, 1),
        acc_shape=(tm, d),
        outs=[(jax.ShapeDtypeStruct((t, d), F32), row), (jax.ShapeDtypeStruct((t, d), BF16), row)],
        epilogue=lambda accs, ex: (ex[0] + accs[0], _rmsnorm(ex[0] + accs[0], ex[1])),
        extras=[(x, row), (g_next, pl.BlockSpec((1, d), lambda i, n, k: (0, 0)))],
    )


def _out_proj_bwd(name, dx, w_out, deps=()):
    t, d = dx.shape
    dm = w_out.shape[0]
    tm = _row_tile(t)
    return _mm(
        name,
        ins=[(dx, pl.BlockSpec((tm, d), lambda i, n, k: (i, 0))),
             (w_out, pl.BlockSpec((dm, d), lambda i, n, k: (0, 0)))],
        terms=[(0, 0, 1, NT)],
        n_acc=1,
        grid=(t // tm, 1, 1),
        acc_shape=(tm, dm),
        outs=[(jax.ShapeDtypeStruct((t, dm), F32), pl.BlockSpec((tm, dm), lambda i, n, k: (i, 0)))],
        epilogue=lambda accs, ex: (accs[0],),
        deps=deps,
    )[0]


def _grad_w_out(name, mix, dx):
    t, dm = mix.shape
    d = dx.shape[1]
    tk = _k_tile(t)
    return _mm(
        name,
        ins=[(mix, pl.BlockSpec((tk, dm), lambda a, n, k: (k, 0))),
             (dx, pl.BlockSpec((tk, d), lambda a, n, k: (k, 0)))],
        terms=[(0, 0, 1, TN)],
        n_acc=1,
        grid=(1, 1, t // tk),
        acc_shape=(dm, d),
        outs=[(pltpu.HBM((dm, d), BF16), pl.BlockSpec((dm, d), lambda a, n, k: (0, 0)))],
        epilogue=lambda accs, ex: (accs[0],),
    )[0]


def _head_group_matrix():
    r = lax.broadcasted_iota(jnp.int32, (ATTN_W, ATTN_W), 0)
    c = lax.broadcasted_iota(jnp.int32, (ATTN_W, ATTN_W), 1)
    same = jnp.right_shift(r, 6) == jnp.right_shift(c, 6)
    return jnp.where(same, 1.0, 0.0).astype(BF16)


def _qk_prep(name, proj, gq, gk):
    b, s, _ = proj.shape
    tm = KPAD
    nb = s // tm

    def body(q_ref, k_ref, v_ref, gq_ref, gk_ref, qn_ref, kn_ref, vb_ref):
        j = pl.program_id(1)
        bd = _head_group_matrix()

        def norm(xv, g):
            ms = _dot_exact_rhs(xv * xv, bd, pieces=2) * (1.0 / ATTN_DH)
            return xv * lax.rsqrt(ms + RMS_EPS) * g

        @pl.when(j == 0)
        def _():
            kn_ref[...] = jnp.zeros_like(kn_ref)
            vb_ref[...] = jnp.zeros_like(vb_ref)

        @pl.when(j > 0)
        def _():
            qn_ref[...] = norm(q_ref[...], gq_ref[...]).astype(BF16)
            kn_ref[...] = norm(k_ref[...], gk_ref[...]).astype(BF16)
            vb_ref[...] = v_ref[...].astype(BF16)

    src_blk = lambda col: pl.BlockSpec((None, tm, ATTN_W), lambda bi, j: (bi, jnp.maximum(j - 1, 0), col))
    gspec = pl.BlockSpec((1, ATTN_W), lambda bi, j: (0, 0))
    padded = pl.BlockSpec((None, tm, ATTN_W), lambda bi, j: (bi, j, 0))
    return pl.pallas_call(
        body,
        name=name,
        grid=(b, nb + 1),
        in_specs=[src_blk(0), src_blk(1), src_blk(2), gspec, gspec],
        out_specs=[src_blk(0), padded, padded],
        out_shape=[jax.ShapeDtypeStruct((b, s, ATTN_W), BF16), jax.ShapeDtypeStruct((b, KPAD + s, ATTN_W), BF16),
                   jax.ShapeDtypeStruct((b, KPAD + s, ATTN_W), BF16)],
        compiler_params=_params("parallel", "arbitrary"),
    )(proj, proj, proj, gq, gk)


def _qk_prep_bwd(name, proj, dqn, dkn, dv, gq, gk):
    b, s, _ = proj.shape
    tm = KPAD
    nb = s // tm

    def body(q_ref, k_ref, dqn_ref, dkn_ref, dv_ref, gq_ref, gk_ref, dq_ref, dk_ref, dvb_ref, dgq_ref, dgk_ref):
        bd = _head_group_matrix()

        def bwd(xv, dy, g):
            ms = _dot_exact_rhs(xv * xv, bd, pieces=2) * (1.0 / ATTN_DH)
            rstd = lax.rsqrt(ms + RMS_EPS)
            xhat = xv * rstd
            dxhat = dy * g
            gm = _dot_exact_rhs(dxhat * xhat, bd, pieces=2) * (1.0 / ATTN_DH)
            return rstd * (dxhat - xhat * gm), jnp.sum(dy * xhat, axis=0, keepdims=True)

        dq, dgq = bwd(q_ref[...], dqn_ref[...], gq_ref[...])
        dk, dgk = bwd(k_ref[...], dkn_ref[...], gk_ref[...])
        dq_ref[...] = dq.astype(BF16)
        dk_ref[...] = dk.astype(BF16)
        dvb_ref[...] = dv_ref[...].astype(BF16)
        dgq_ref[...] = dgq
        dgk_ref[...] = dgk

    col = lambda c: pl.BlockSpec((None, tm, ATTN_W), lambda bi, j: (bi, j, c))
    past_pad = pl.BlockSpec((None, tm, ATTN_W), lambda bi, j: (bi, j + 1, 0))
    gspec = pl.BlockSpec((1, ATTN_W), lambda bi, j: (0, 0))
    pspec = pl.BlockSpec((None, 1, ATTN_W), lambda bi, j: (bi * nb + j, 0, 0))
    o_shape = jax.ShapeDtypeStruct((b, s, ATTN_W), BF16)
    p_shape = jax.ShapeDtypeStruct((b * nb, 1, ATTN_W), F32)
    return pl.pallas_call(
        body,
        name=name,
        grid=(b, nb),
        in_specs=[col(0), col(1), col(0), past_pad, past_pad, gspec, gspec],
        out_specs=[col(0)] * 3 + [pspec] * 2,
        out_shape=[o_shape] * 3 + [p_shape] * 2,
        compiler_params=_params("parallel", "parallel"),
    )(proj, proj, dqn, dkn, dv, gq, gk)


Q_CHUNKS = 4
QBLK = Q_CHUNKS * CHUNK
WIN = (LEFT_CHUNKS + Q_CHUNKS) * CHUNK
DB_W = BAND + CHUNK
MASKED = -1e30
FWD_BLOCKS = 8


def _band_table(bias):
    rows = [jnp.pad(bias, ((0, 0), (0, 0), (CHUNK * i, WIN - BAND - CHUNK * i)), constant_values=MASKED)
            for i in range(Q_CHUNKS)]
    return jnp.concatenate(rows, axis=1)


def _head_lanes(hh):
    lane = lax.broadcasted_iota(jnp.int32, (1, LANES), 1)
    return (lane < ATTN_DH) if hh == 0 else (lane >= ATTN_DH)


def _attn_probs(qh, kw, table, start):
    s = _dot(qh, kw, NT) * (ATTN_DH ** -0.5) + table
    col = lax.broadcasted_iota(jnp.int32, (QBLK, WIN), 1)
    s = jnp.where(col + start >= KPAD, s, MASKED)
    m = jnp.max(s, axis=-1, keepdims=True)
    p = jnp.exp(s - m)
    return p * (1.0 / jnp.sum(p, axis=-1, keepdims=True))


def _attn_fwd(name, q, k, v, table, deps=()):
    b, s, w = q.shape
    sp = k.shape[1]

    def body(q_ref, k_ref, v_ref, t_ref, *rest):
        o_ref = rest[-1]
        lanes = [_head_lanes(hh) for hh in range(2)]
        starts = [pl.multiple_of((pl.program_id(2) * FWD_BLOCKS + j) * QBLK, QBLK) for j in range(FWD_BLOCKS)]
        kws = [k_ref[pl.ds(st, WIN), :] for st in starts]
        vws = [v_ref[pl.ds(st, WIN), :] for st in starts]
        q2s = [q_ref[j * QBLK:(j + 1) * QBLK, :] for j in range(FWD_BLOCKS)]
        probs = [[_attn_probs(jnp.where(mine, q2s[j], jnp.zeros_like(q2s[j])), kws[j], t_ref[hh], starts[j]).astype(BF16)
                  for hh, mine in enumerate(lanes)] for j in range(FWD_BLOCKS)]
        for j in range(FWD_BLOCKS):
            outs = [_dot(p, vws[j]) for p in probs[j]]
            o_ref[j * QBLK:(j + 1) * QBLK, :] = jnp.where(lanes[0], outs[0], outs[1]).astype(BF16)

    qspec = pl.BlockSpec((None, FWD_BLOCKS * QBLK, LANES), lambda p, bi, i: (bi, i, p))
    kspec = pl.BlockSpec((None, sp, LANES), lambda p, bi, i: (bi, 0, p))
    return pl.pallas_call(
        body,
        name=name,
        grid=(w // LANES, b, s // (FWD_BLOCKS * QBLK)),
        in_specs=[qspec, kspec, kspec, pl.BlockSpec((2, QBLK, WIN), lambda p, bi, i: (p, 0, 0))] + [ANY] * len(deps),
        out_specs=qspec,
        out_shape=jax.ShapeDtypeStruct((b, s, w), BF16),
        compiler_params=_params("parallel", "parallel", "arbitrary"),
    )(q, k, v, table, *deps)


def _attn_bwd(name, q, k, v, table, dmix):
    b, s, w = q.shape
    sp = k.shape[1]

    def body(q_ref, k_ref, v_ref, t_ref, do_ref, dq_ref, dk_ref, dv_ref, dbe_ref, dbo_ref):
        bi = pl.program_id(1)
        i = pl.program_id(2)
        start = pl.multiple_of(i * QBLK, QBLK)
        win = pl.ds(start, WIN)

        @pl.when(i == 0)
        def _():
            dk_ref[...] = jnp.zeros_like(dk_ref)
            dv_ref[...] = jnp.zeros_like(dv_ref)

        @pl.when(jnp.logical_and(i == 0, bi == 0))
        def _():
            dbe_ref[...] = jnp.zeros_like(dbe_ref)
            dbo_ref[...] = jnp.zeros_like(dbo_ref)

        kw = k_ref[win, :]
        vw = v_ref[win, :]
        q2 = q_ref[...]
        do2 = do_ref[...].astype(BF16)
        lanes = [_head_lanes(hh) for hh in range(2)]
        qh = [jnp.where(mine, q2, jnp.zeros_like(q2)) for mine in lanes]
        doh = [jnp.where(mine, do2, jnp.zeros_like(do2)) for mine in lanes]
        p = [_attn_probs(qh[hh], kw, t_ref[hh], start) for hh in range(2)]
        dp = [_dot(doh[hh], vw, NT) for hh in range(2)]
        ds = [p[hh] * (dp[hh] - jnp.sum(p[hh] * dp[hh], axis=-1, keepdims=True)) for hh in range(2)]
        dsb = [(x * (ATTN_DH ** -0.5)).astype(BF16) for x in ds]
        pb = [x.astype(BF16) for x in p]
        dq = [_dot(dsb[hh], kw) for hh in range(2)]
        dk = [_dot(dsb[hh], qh[hh], TN) for hh in range(2)]
        dv = [_dot(pb[hh], doh[hh], TN) for hh in range(2)]
        for hh in range(2):
            for qi in range(Q_CHUNKS):
                c0 = (qi // 2) * LANES
                blk = ds[hh][qi * CHUNK:(qi + 1) * CHUNK, c0:c0 + DB_W]
                if qi % 2 == 0:
                    dbe_ref[hh] += blk
                else:
                    dbo_ref[hh] += blk
        dq_ref[...] = jnp.where(lanes[0], dq[0], dq[1])
        dk_ref[win, :] += dk[0] + dk[1]
        dv_ref[win, :] += dv[0] + dv[1]

    qspec = pl.BlockSpec((None, QBLK, LANES), lambda p, bi, i: (bi, i, p))
    kspec = pl.BlockSpec((None, sp, LANES), lambda p, bi, i: (bi, 0, p))
    dbspec = pl.BlockSpec((2, CHUNK, DB_W), lambda p, bi, i: (p, 0, 0))
    db_shape = jax.ShapeDtypeStruct((ATTN_HEADS, CHUNK, DB_W), F32)
    return pl.pallas_call(
        body,
        name=name,
        grid=(w // LANES, b, s // QBLK),
        in_specs=[qspec, kspec, kspec, pl.BlockSpec((2, QBLK, WIN), lambda p, bi, i: (p, 0, 0)), qspec],
        out_specs=[qspec, kspec, kspec, dbspec, dbspec],
        out_shape=[jax.ShapeDtypeStruct((b, s, w), F32), jax.ShapeDtypeStruct((b, sp, w), F32),
                   jax.ShapeDtypeStruct((b, sp, w), F32), db_shape, db_shape],
        compiler_params=_params("arbitrary", "arbitrary", "arbitrary"),
    )(q, k, v, table, dmix)


HQ_COL = 3 * ATTN_W // HGRN_DH
HF_COL = HQ_COL + HGRN_HEADS
HI_COL = HF_COL + HGRN_HEADS
HG_COL = HI_COL + HGRN_HEADS
HGRN_ROWS = 8 * CHUNK
HGRN_UNROLL = 8
HEAD_LANES = [slice(hh * HGRN_DH, (hh + 1) * HGRN_DH) for hh in range(HGRN_HEADS)]


def _tri(lower):
    r = lax.broadcasted_iota(jnp.int32, (CHUNK, CHUNK), 0)
    c = lax.broadcasted_iota(jnp.int32, (CHUNK, CHUNK), 1)
    return (r >= c) if lower else (r <= c)


def _hgrn_chunk(hq, hf, lb, tril):
    sig = _sigmoid(hf)
    f = lb + (1.0 - lb) * sig
    g = jnp.log(f)
    ones_l = jnp.where(tril, 1.0, 0.0).astype(BF16)
    b = _dot_exact_lhs(ones_l, g)
    bl = jnp.sum(g, axis=0, keepdims=True)
    rows = lax.broadcasted_iota(jnp.int32, g.shape, 0)
    bm = jnp.sum(jnp.where(rows <= CHUNK // 2, g, 0.0), axis=0, keepdims=True)
    sq = _sigmoid(hq)
    q = hq * sq
    k = 1.0 - f
    return sig, f, b, bl, bm, sq, q, k


def _hgrn_fwd(name, proj, attn, lb, go, b, s):
    nc = s // CHUNK
    t = b * s
    nblk = s // HGRN_ROWS
    cpb = HGRN_ROWS // CHUNK

    def body(hq_ref, hf_ref, hi_ref, hg_ref, attn_ref, lb_ref, go_ref, mix_ref, oraw_ref, st_ref, s_scr):
        tril = _tri(True)
        gov = go_ref[...]
        mix_ref[:, 0:ATTN_W] = attn_ref[...]

        @pl.when(pl.program_id(1) == 0)
        def _():
            s_scr[...] = jnp.zeros_like(s_scr)

        def step(c, carry):
            sl = pl.ds(pl.multiple_of(c * CHUNK, CHUNK), CHUNK)
            hg = hg_ref[sl, :]
            _, _, bb, bl, bm, _, q, k = _hgrn_chunk(hq_ref[sl, :], hf_ref[sl, :], lb_ref[...], tril)
            vb = hi_ref[sl, :].astype(BF16)
            qe = (q * jnp.exp(bb - bm)).astype(BF16)
            ke = (k * jnp.exp(bm - bb)).astype(BF16)
            qb = (q * jnp.exp(bb)).astype(BF16)
            kb = (k * jnp.exp(bl - bb)).astype(BF16)
            e_last = jnp.exp(bl)
            gate = _silu(hg)
            st = [s_scr[hh] for hh in range(HGRN_HEADS)]
            a = [jnp.where(tril, _dot(qe[:, hs], ke[:, hs], NT), 0.0).astype(BF16) for hs in HEAD_LANES]
            o_state = [_dot(qb[:, hs], st[hh].astype(BF16), NT) for hh, hs in enumerate(HEAD_LANES)]
            st_next = [st[hh] * e_last[:, hs] + _dot(vb[:, hs], kb[:, hs], TN) for hh, hs in enumerate(HEAD_LANES)]
            o = [_dot(a[hh], vb[:, hs]) + o_state[hh] for hh, hs in enumerate(HEAD_LANES)]
            ro = [(oh * lax.rsqrt(jnp.mean(oh * oh, axis=-1, keepdims=True) + RMS_EPS) * gov) * gate[:, hs]
                  for oh, hs in zip(o, HEAD_LANES)]
            for hh in range(HGRN_HEADS):
                st_ref[hh, c] = st[hh]
                s_scr[hh] = st_next[hh]
            mix_ref[sl, ATTN_W:ATTN_W + HGRN_W] = jnp.concatenate(ro, axis=1).astype(BF16)
            oraw_ref[sl, :] = jnp.concatenate(o, axis=1)
            return carry

        lax.fori_loop(0, cpb, step, 0, unroll=HGRN_UNROLL)

    col = lambda base: pl.BlockSpec((HGRN_ROWS, HGRN_W), lambda bi, i: (bi * nblk + i, base // HGRN_HEADS))
    out = pl.BlockSpec((HGRN_ROWS, HGRN_W), lambda bi, i: (bi * nblk + i, 0))
    return pl.pallas_call(
        body,
        name=name,
        grid=(b, nblk),
        in_specs=[col(HQ_COL), col(HF_COL), col(HI_COL), col(HG_COL), out,
                  pl.BlockSpec((1, HGRN_W), lambda bi, i: (0, 0)), pl.BlockSpec((1, HGRN_DH), lambda bi, i: (0, 0))],
        out_specs=[pl.BlockSpec((HGRN_ROWS, ATTN_W + HGRN_W), lambda bi, i: (bi * nblk + i, 0)), out,
                   pl.BlockSpec((None, HGRN_HEADS, cpb, HGRN_DH, HGRN_DH), lambda bi, i: (bi, 0, i, 0, 0))],
        out_shape=[jax.ShapeDtypeStruct((t, ATTN_W + HGRN_W), BF16), jax.ShapeDtypeStruct((t, HGRN_W), F32),
                   jax.ShapeDtypeStruct((b, HGRN_HEADS, nc, HGRN_DH, HGRN_DH), F32)],
        scratch_shapes=[pltpu.VMEM((HGRN_HEADS, HGRN_DH, HGRN_DH), F32)],
        compiler_params=_params("parallel", "arbitrary"),
    )(proj, proj, proj, proj, attn, lb, go)


def _hgrn_bwd(name, proj, dqkv, lb, go, oraw, states, dmix, b, s):
    t = b * s
    nblk = s // HGRN_ROWS
    cpb = HGRN_ROWS // CHUNK

    def body(hq_ref, hf_ref, hi_ref, hg_ref, dq_ref, dk_ref, dv_ref, lb_ref, go_ref, oraw_ref, st_ref, dro_ref,
             dp_ref, dlb_ref, dgo_ref, ds_scr, dlb_scr, dgo_scr):
        tril = _tri(True)
        ones_u = jnp.where(_tri(False), 1.0, 0.0).astype(BF16)
        gov = go_ref[...]
        dp_ref[:, 0:ATTN_W] = dq_ref[...]
        dp_ref[:, ATTN_W:2 * ATTN_W] = dk_ref[...]
        dp_ref[:, 2 * ATTN_W:3 * ATTN_W] = dv_ref[...]

        @pl.when(pl.program_id(1) == 0)
        def _():
            ds_scr[...] = jnp.zeros_like(ds_scr)
            dlb_scr[...] = jnp.zeros_like(dlb_scr)
            dgo_scr[...] = jnp.zeros_like(dgo_scr)

        def step(ci, carry):
            c = cpb - 1 - ci
            sl = pl.ds(pl.multiple_of(c * CHUNK, CHUNK), CHUNK)
            hq = hq_ref[sl, :]
            hg = hg_ref[sl, :]
            sig, f, bb, bl, bm, sq, q, k = _hgrn_chunk(hq, hf_ref[sl, :], lb_ref[...], tril)
            vb = hi_ref[sl, :].astype(BF16)
            ebm = jnp.exp(bb - bm)
            embm = jnp.exp(bm - bb)
            eb = jnp.exp(bb)
            ebl = jnp.exp(bl - bb)
            e_last = jnp.exp(bl)
            qe = (q * ebm).astype(BF16)
            ke = (k * embm).astype(BF16)
            qb = (q * eb).astype(BF16)
            kb = (k * ebl).astype(BF16)
            st = [st_ref[hh, c] for hh in range(HGRN_HEADS)]
            dst = [ds_scr[hh] for hh in range(HGRN_HEADS)]
            o = oraw_ref[sl, :]
            dro = dro_ref[sl, :]
            sg = _sigmoid(hg)
            gov4 = jnp.concatenate([gov] * HGRN_HEADS, axis=1)
            rstd = jnp.concatenate(
                [jnp.broadcast_to(lax.rsqrt(jnp.mean(o[:, hs] * o[:, hs], axis=-1, keepdims=True) + RMS_EPS),
                                  (CHUNK, HGRN_DH)) for hs in HEAD_LANES], axis=1)
            ohat = o * rstd
            dn = dro * (hg * sg)
            dhg = dro * (ohat * gov4) * (sg * (1.0 + hg * (1.0 - sg)))
            dgo_inc = jnp.sum(dn * ohat, axis=0, keepdims=True)
            dohat = dn * gov4
            proj_h = dohat * ohat
            pm = jnp.concatenate(
                [jnp.broadcast_to(jnp.mean(proj_h[:, hs], axis=-1, keepdims=True), (CHUNK, HGRN_DH))
                 for hs in HEAD_LANES], axis=1)
            dob = (rstd * (dohat - ohat * pm)).astype(BF16)
            stb = [x.astype(BF16) for x in st]
            dstb = [x.astype(BF16) for x in dst]
            a = [jnp.where(tril, _dot(qe[:, hs], ke[:, hs], NT), 0.0).astype(BF16) for hs in HEAD_LANES]
            dab = [jnp.where(tril, _dot(dob[:, hs], vb[:, hs], NT), 0.0).astype(BF16) for hs in HEAD_LANES]
            dqb = [_dot(dob[:, hs], stb[hh]) for hh, hs in enumerate(HEAD_LANES)]
            dkb = [_dot(vb[:, hs], dstb[hh]) for hh, hs in enumerate(HEAD_LANES)]
            dv_state = [_dot(kb[:, hs], dstb[hh], NT) for hh, hs in enumerate(HEAD_LANES)]
            dst_next = [dst[hh] * e_last[:, hs] + _dot(dob[:, hs], qb[:, hs], TN) for hh, hs in enumerate(HEAD_LANES)]
            dv = [_dot(a[hh], dob[:, hs], TN) + dv_state[hh] for hh, hs in enumerate(HEAD_LANES)]
            dqe = jnp.concatenate([_dot(dab[hh], ke[:, hs]) for hh, hs in enumerate(HEAD_LANES)], axis=1)
            dke = jnp.concatenate([_dot(dab[hh], qe[:, hs], TN) for hh, hs in enumerate(HEAD_LANES)], axis=1)
            dqb = jnp.concatenate(dqb, axis=1)
            dkb = jnp.concatenate(dkb, axis=1)
            state_term = jnp.concatenate(
                [jnp.sum(dst[hh] * st[hh], axis=0, keepdims=True) for hh in range(HGRN_HEADS)], axis=1)
            dq = dqe * ebm + dqb * eb
            dk = dke * embm + dkb * ebl
            db = (qe.astype(F32) * dqe - ke.astype(F32) * dke) + q * (dqb * eb) - k * (dkb * ebl)
            d_last = jnp.sum(k * ebl * dkb, axis=0, keepdims=True) + state_term * e_last
            dg = _dot_exact_lhs(ones_u, db) + d_last
            df = dg / f - dk
            first = HQ_COL * HGRN_DH
            dp_ref[sl, first:first + HGRN_W] = (dq * (sq * (1.0 + hq * (1.0 - sq)))).astype(BF16)
            dp_ref[sl, first + HGRN_W:first + 2 * HGRN_W] = (df * (1.0 - lb_ref[...]) * sig * (1.0 - sig)).astype(BF16)
            dp_ref[sl, first + 2 * HGRN_W:first + 3 * HGRN_W] = jnp.concatenate(dv, axis=1).astype(BF16)
            dp_ref[sl, first + 3 * HGRN_W:first + 4 * HGRN_W] = dhg.astype(BF16)
            dlb_scr[...] += jnp.sum(df * (1.0 - sig), axis=0, keepdims=True)
            dgo_scr[...] += dgo_inc
            for hh in range(HGRN_HEADS):
                ds_scr[hh] = dst_next[hh]
            return carry

        lax.fori_loop(0, cpb, step, 0, unroll=HGRN_UNROLL)

        @pl.when(pl.program_id(1) == nblk - 1)
        def _():
            dlb_ref[...] = dlb_scr[...]
            dgo_ref[...] = dgo_scr[...]

    rows = lambda bi, i: bi * nblk + (nblk - 1 - i)
    col = lambda base: pl.BlockSpec((HGRN_ROWS, HGRN_W), lambda bi, i: (rows(bi, i), base // HGRN_HEADS))
    out = pl.BlockSpec((HGRN_ROWS, HGRN_W), lambda bi, i: (rows(bi, i), 0))
    part = pl.BlockSpec((None, 1, HGRN_W), lambda bi, i: (bi, 0, 0))
    width = HG_COL * HGRN_DH + HGRN_W
    o_shape = jax.ShapeDtypeStruct((t, width), BF16)
    p_shape = jax.ShapeDtypeStruct((b, 1, HGRN_W), F32)
    return pl.pallas_call(
        body,
        name=name,
        grid=(b, nblk),
        in_specs=[col(HQ_COL), col(HF_COL), col(HI_COL), col(HG_COL), out, out, out,
                  pl.BlockSpec((1, HGRN_W), lambda bi, i: (0, 0)), pl.BlockSpec((1, HGRN_DH), lambda bi, i: (0, 0)), out,
                  pl.BlockSpec((None, HGRN_HEADS, cpb, HGRN_DH, HGRN_DH), lambda bi, i: (bi, 0, nblk - 1 - i, 0, 0)),
                  col(ATTN_W // HGRN_DH)],
        out_specs=[pl.BlockSpec((HGRN_ROWS, width), lambda bi, i: (rows(bi, i), 0))] + [part] * 2,
        out_shape=[o_shape] + [p_shape] * 2,
        scratch_shapes=[pltpu.VMEM((HGRN_HEADS, HGRN_DH, HGRN_DH), F32), pltpu.VMEM((1, HGRN_W), F32),
                        pltpu.VMEM((1, HGRN_W), F32)],
        compiler_params=_params("parallel", "arbitrary"),
    )(proj, proj, proj, proj, *dqkv, lb, go, oraw, states, dmix)


def _small_grads(name, dg1, dgm, dg2, dgq, dgk, dbe_t, dbo_t, dlb, dgo, lbp):
    d = dg1.shape[1]

    def body(dg1_ref, dgm_ref, dg2_ref, dgq_ref, dgk_ref, dbe_ref, dbo_ref, dlb_ref, dgo_ref, lbp_ref,
             g1_ref, gm_ref, g2_ref, gq_ref, gk_ref, rb_ref, lbg_ref, go_ref):
        g1_ref[...] = jnp.sum(dg1_ref[...], axis=0, keepdims=True)
        gm_ref[...] = jnp.sum(dgm_ref[...], axis=0, keepdims=True)
        g2_ref[...] = jnp.sum(dg2_ref[...], axis=0, keepdims=True)
        r = lax.broadcasted_iota(jnp.int32, (ATTN_W, ATTN_DH), 0)
        cidx = lax.broadcasted_iota(jnp.int32, (ATTN_W, ATTN_DH), 1)
        fold = jnp.where(jnp.bitwise_and(r, ATTN_DH - 1) == cidx, 1.0, 0.0).astype(BF16)
        gq_ref[...] = jnp.sum(_dot_exact_rhs(dgq_ref[...], fold), axis=0, keepdims=True)
        gk_ref[...] = jnp.sum(_dot_exact_rhs(dgk_ref[...], fold), axis=0, keepdims=True)
        gosum = jnp.sum(dgo_ref[...], axis=0, keepdims=True)
        go_ref[...] = (gosum[:, 0:HGRN_DH] + gosum[:, HGRN_DH:2 * HGRN_DH]
                       + gosum[:, 2 * HGRN_DH:3 * HGRN_DH] + gosum[:, 3 * HGRN_DH:4 * HGRN_DH])
        p0 = lbp_ref[0:1, :]
        p1 = lbp_ref[1:2, :]
        lbv = 1.0 / (1.0 + jnp.exp(p1 - p0))
        dp0 = jnp.sum(dlb_ref[...], axis=0, keepdims=True) * lbv * (1.0 - lbv)
        lbg_ref[0:1, :] = dp0
        lbg_ref[1:2, :] = -dp0
        acc = dbe_ref[CHUNK - 1] + pltpu.roll(dbo_ref[CHUNK - 1], DB_W - CHUNK, 1)
        for tq in range(CHUNK - 1):
            acc = acc + pltpu.roll(dbe_ref[tq], CHUNK - 1 - tq, 1) + pltpu.roll(dbo_ref[tq], DB_W - 1 - tq, 1)
        jidx = lax.broadcasted_iota(jnp.int32, (DB_W, N_REL_PAD), 0)
        ridx = lax.broadcasted_iota(jnp.int32, (DB_W, N_REL_PAD), 1)
        rel = jnp.clip(KPAD + CHUNK - 1 - jidx, -REL_CLIP, REL_CLIP) + REL_CLIP
        rb_ref[...] = _dot_exact_rhs(acc, jnp.where(rel == ridx, 1.0, 0.0).astype(BF16))

    ins = [dg1, dgm, dg2, dgq, dgk, dbe_t, dbo_t, dlb, dgo, lbp]
    outs = [jax.ShapeDtypeStruct((1, d), F32)] * 3 + [jax.ShapeDtypeStruct((1, ATTN_DH), F32)] * 2 + [
        jax.ShapeDtypeStruct((ATTN_HEADS, N_REL_PAD), F32), jax.ShapeDtypeStruct((2, HGRN_W), F32),
        jax.ShapeDtypeStruct((1, HGRN_DH), F32)]
    vm = pl.BlockSpec(memory_space=pltpu.VMEM)
    return pl.pallas_call(
        body,
        name=name,
        in_specs=[vm] * len(ins),
        out_specs=[vm] * len(outs),
        out_shape=outs,
        compiler_params=pltpu.CompilerParams(vmem_limit_bytes=VMEM_LIMIT),
    )(*ins)


def _adam_update(w, g, m, v):
    m2 = ADAM_B1 * m + (1.0 - ADAM_B1) * g
    v2 = ADAM_B2 * v + (1.0 - ADAM_B2) * (g * g)
    m_hat = m2 / (1.0 - ADAM_B1 ** ADAM_STEP)
    v_hat = v2 / (1.0 - ADAM_B2 ** ADAM_STEP)
    delta = -ADAM_LR * (m_hat / (jnp.sqrt(v_hat) + ADAM_EPS) + ADAM_WD * w)
    return delta, m2, v2


def _rows_tile(r):
    return r if r <= 512 or r % 512 else 512


def _pair_sum(name, grad, theirs, core):
    n, half, c = theirs.shape
    tr = _rows_tile(half)
    nth = half // tr

    def body(core_ref, a_ref, b_ref, o_ref):
        o_ref[...] = (a_ref[...].astype(F32) + b_ref[...].astype(F32)).astype(o_ref.dtype)

    spec = pl.BlockSpec((None, tr, c), lambda i, j, core_ref: (i, j, 0))
    return pl.pallas_call(
        body, name=name,
        grid_spec=pltpu.PrefetchScalarGridSpec(
            num_scalar_prefetch=1, grid=(n, nth),
            in_specs=[pl.BlockSpec((None, tr, c), lambda i, j, core_ref: (i, core_ref[0] * nth + j, 0)), spec],
            out_specs=spec),
        out_shape=pltpu.HBM((n, half, c), BF16), compiler_params=_params("parallel", "parallel"),
    )(core, grad, theirs)


def _chip_sum(name, own, parts, chip):
    _, half, c = own.shape
    tr = _rows_tile(half)

    def body(chip_ref, own_ref, p_ref, o_ref):
        me = chip_ref[0]
        mine = own_ref[...].astype(F32)
        flip_x, flip_y, flip_xy = (p_ref[i].astype(F32) for i in range(3))
        acc = None
        for k in range(N_CHIPS):
            rel = jnp.bitwise_xor(me, k)
            term = jnp.where(rel == 0, mine, jnp.where(rel == 2, flip_x, jnp.where(rel == 1, flip_y, flip_xy)))
            acc = term if acc is None else acc + term
        o_ref[...] = acc

    return pl.pallas_call(
        body, name=name,
        grid_spec=pltpu.PrefetchScalarGridSpec(
            num_scalar_prefetch=1, grid=(half // tr,),
            in_specs=[pl.BlockSpec((None, tr, c), lambda j, chip_ref: (chip_ref[0], j, 0)),
                      pl.BlockSpec((3, tr, c), lambda j, chip_ref: (0, j, 0))],
            out_specs=pl.BlockSpec((tr, c), lambda j, chip_ref: (j, 0))),
        out_shape=pltpu.HBM((half, c), F32), compiler_params=_params("parallel"),
    )(chip, own, parts)


def _adamw(name, w, g_mine, g_theirs, m, v, core):
    _, r, c = w.shape
    half = r // 2
    tr = _rows_tile(half)
    nth = half // tr

    def body(core_ref, w_ref, gm_ref, gt_ref, m_ref, v_ref, g_ref, d_ref, m2_ref, v2_ref):
        g = jnp.where(pl.program_id(0) == core_ref[0], gm_ref[...], gt_ref[...])
        delta, m2, v2 = _adam_update(w_ref[...], g, m_ref[...], v_ref[...])
        g_ref[...] = g
        d_ref[...] = delta
        m2_ref[...] = m2
        v2_ref[...] = v2

    full = pl.BlockSpec((None, tr, c), lambda h, j, core_ref: (0, h * nth + j, 0))
    part = pl.BlockSpec((tr, c), lambda h, j, core_ref: (j, 0))
    shape = jax.ShapeDtypeStruct((1, r, c), F32)
    return pl.pallas_call(
        body, name=name,
        grid_spec=pltpu.PrefetchScalarGridSpec(
            num_scalar_prefetch=1, grid=(2, nth), in_specs=[full, part, part, full, full], out_specs=[full] * 4),
        out_shape=[shape] * 4, compiler_params=_params("parallel", "parallel"),
    )(core, w, g_mine, g_theirs, m, v)


def _rel_bias_table(name, rel_bias):
    padded = jnp.pad(rel_bias, ((0, 0), (0, N_REL_PAD - N_REL)))

    def body(rb_ref, o_ref):
        ridx = lax.broadcasted_iota(jnp.int32, (N_REL_PAD, BAND), 0)
        sidx = lax.broadcasted_iota(jnp.int32, (N_REL_PAD, BAND), 1)
        rb = rb_ref[...]

        def step(tq, carry):
            rel = jnp.clip(tq + KPAD - sidx, -REL_CLIP, REL_CLIP) + REL_CLIP
            onehot = jnp.where(rel == ridx, 1.0, 0.0).astype(BF16)
            o_ref[tq] = _dot_exact_rhs(rb, onehot)
            return carry

        lax.fori_loop(0, CHUNK, step, 0)

    vm = pl.BlockSpec(memory_space=pltpu.VMEM)
    table = pl.pallas_call(
        body, name=name, in_specs=[vm], out_specs=vm,
        out_shape=jax.ShapeDtypeStruct((CHUNK, ATTN_HEADS, BAND), F32),
    )(padded)
    return table.transpose(1, 0, 2)


def _adamw_small(name, w, parts, m, v):
    def body(w_ref, p_ref, m_ref, v_ref, g_ref, d_ref, m2_ref, v2_ref):
        g = p_ref[0]
        for i in range(1, N_DEV):
            g = g + p_ref[i]
        delta, m2, v2 = _adam_update(w_ref[...], g, m_ref[...], v_ref[...])
        g_ref[...] = g
        d_ref[...] = delta
        m2_ref[...] = m2
        v2_ref[...] = v2

    vm = pl.BlockSpec(memory_space=pltpu.VMEM)
    shape = jax.ShapeDtypeStruct((SMALL_ROWS, SMALL_COLS), F32)
    return pl.pallas_call(
        body, name=name, in_specs=[vm] * 4, out_specs=[vm] * 4, out_shape=[shape] * 4,
    )(w, parts, m, v)


def _position():
    return lax.axis_index("x"), lax.axis_index("y"), lax.axis_index("c")


def _other_chips(x, y):
    return [(1 - x, y), (x, 1 - y), (1 - x, 1 - y)]


ANY = pl.BlockSpec(memory_space=pl.ANY)
PAIR_ID = 0


def _pair_handshake():
    x, y, c = _position()
    barrier = pltpu.get_barrier_semaphore()
    pl.semaphore_signal(barrier, inc=1, device_id=(x, y, 1 - c), device_id_type=MESH)
    pl.semaphore_wait(barrier, 1)


PAIR_CALL = pltpu.CompilerParams(collective_id=PAIR_ID)


HBM = pl.BlockSpec(memory_space=pltpu.HBM)
SEM = pl.BlockSpec(memory_space=pltpu.SEMAPHORE)
SPLIT_COPY = pltpu.SideEffectType.DATAFLOW_SIDE_EFFECTING


def _gather_copy(shards, outs, send_sem, recv_sem, i, j):
    x, y, c = _position()
    chips = _other_chips(x, y)
    half = shards[i].shape[0] // 2
    rows = pl.ds(pl.multiple_of(c * half, 16), half)
    return pltpu.make_async_remote_copy(
        src_ref=shards[i].at[rows, :], dst_ref=outs[i].at[2 * x + y, rows, :],
        send_sem=send_sem.at[3 * i + j], recv_sem=recv_sem.at[3 * i + j],
        device_id=(chips[j][0], chips[j][1], c), device_id_type=MESH)


def _gather_start(name, shards, after):
    n = len(shards)

    def body(*refs):
        srcs, outs = refs[:n], refs[n:2 * n]
        send_sem, recv_sem = refs[2 * n + len(after)], refs[2 * n + len(after) + 1]
        token = refs[-1]
        for i in range(n):
            for j in range(3):
                _gather_copy(srcs, outs, send_sem, recv_sem, i, j).start()
        token[...] = jnp.zeros_like(token)

    full = [(N_CHIPS,) + s.shape for s in shards]
    res = pl.pallas_call(
        body,
        name=name,
        in_specs=[HBM] * (2 * n) + [ANY] * len(after),
        out_specs=[SEM, SEM] + [HBM] * (2 * n) + [pl.BlockSpec(memory_space=pltpu.VMEM)],
        out_shape=[pltpu.SemaphoreType.DMA((3 * n,)), pltpu.SemaphoreType.DMA((3 * n,))]
        + [pltpu.HBM(s.shape, s.dtype) for s in shards]
        + [pltpu.HBM(shp, s.dtype) for shp, s in zip(full, shards)]
        + [jax.ShapeDtypeStruct((8, LANES), F32)],
        input_output_aliases={i: 2 + i for i in range(2 * n)},
        compiler_params=pltpu.CompilerParams(has_side_effects=SPLIT_COPY),
    )(*[pltpu.with_memory_space_constraint(s, pltpu.HBM) for s in shards],
      *[pltpu.with_memory_space_constraint(lax.empty(shp, s.dtype), pltpu.HBM) for shp, s in zip(full, shards)],
      *after)
    return res[0], res[1], list(res[2:2 + n]), list(res[2 + n:2 + 2 * n]), res[-1]


def _gather_wait(name, send_sem, recv_sem, shards, outs, after):
    n = len(shards)

    def body(*refs):
        srcs, out_refs = refs[:n], refs[n:2 * n]
        send_ref, recv_ref = refs[2 * n], refs[2 * n + 1]
        for i in range(n):
            for j in range(3):
                copy = _gather_copy(srcs, out_refs, send_ref, recv_ref, i, j)
                copy.wait_send()
                copy.wait_recv()

    res = pl.pallas_call(
        body,
        name=name,
        in_specs=[HBM] * (2 * n) + [SEM, SEM] + [ANY] * len(after),
        out_specs=[HBM] * (2 * n),
        out_shape=[pltpu.HBM(s.shape, s.dtype) for s in shards] + [pltpu.HBM(o.shape, o.dtype) for o in outs],
        input_output_aliases={i: i for i in range(2 * n)},
        compiler_params=pltpu.CompilerParams(has_side_effects=SPLIT_COPY),
    )(*shards, *outs, send_sem, recv_sem, *after)
    return list(res[:n]), list(res[n:])


def _join_copies(srcs, ins, outs, own_send, own_recv, half_send, half_recv):
    x, y, c = _position()
    chips = _other_chips(x, y)
    copies = []
    for i in range(len(srcs)):
        copies.append(pltpu.make_async_remote_copy(
            src_ref=srcs[i], dst_ref=outs[i].at[2 * x + y], send_sem=own_send.at[i], recv_sem=own_recv.at[i],
            device_id=(x, y, 1 - c), device_id_type=MESH))
        half = srcs[i].shape[0] // 2
        rows = pl.ds(pl.multiple_of(c * half, 16), half)
        for j in range(3):
            slot = 2 * chips[j][0] + chips[j][1]
            copies.append(pltpu.make_async_remote_copy(
                src_ref=ins[i].at[slot, rows, :], dst_ref=outs[i].at[slot, rows, :],
                send_sem=half_send.at[3 * i + j], recv_sem=half_recv.at[3 * i + j],
                device_id=(x, y, 1 - c), device_id_type=MESH))
    return copies


def _gather_join(name, shards, outs):
    n = len(shards)

    def body(*refs):
        _pair_handshake()
        copies = _join_copies(refs[:n], refs[n:2 * n], refs[2 * n:3 * n], *refs[3 * n:])
        for cp in copies:
            cp.start()
        for cp in copies:
            cp.wait()

    return pl.pallas_call(
        body,
        name=name,
        in_specs=[ANY] * (2 * n),
        out_specs=[HBM] * n,
        out_shape=[pltpu.HBM(o.shape, o.dtype) for o in outs],
        input_output_aliases={n + i: i for i in range(n)},
        scratch_shapes=[pltpu.SemaphoreType.DMA((n,))] * 2 + [pltpu.SemaphoreType.DMA((3 * n,))] * 2,
        compiler_params=PAIR_CALL,
    )(*shards, *outs)


def _join_start(name, shards, outs):
    n = len(shards)

    def body(*refs):
        _pair_handshake()
        srcs, arrs = refs[:n], refs[n:2 * n]
        sems = refs[2 * n:2 * n + 4]
        token = refs[-1]
        for cp in _join_copies(srcs, arrs, arrs, *sems):
            cp.start()
        token[...] = jnp.zeros_like(token)

    res = pl.pallas_call(
        body,
        name=name,
        in_specs=[HBM] * (2 * n),
        out_specs=[SEM] * 4 + [HBM] * (2 * n) + [pl.BlockSpec(memory_space=pltpu.VMEM)],
        out_shape=[pltpu.SemaphoreType.DMA((n,))] * 2 + [pltpu.SemaphoreType.DMA((3 * n,))] * 2
        + [pltpu.HBM(s.shape, s.dtype) for s in shards] + [pltpu.HBM(o.shape, o.dtype) for o in outs]
        + [jax.ShapeDtypeStruct((8, LANES), F32)],
        input_output_aliases={i: 4 + i for i in range(2 * n)},
        compiler_params=pltpu.CompilerParams(has_side_effects=SPLIT_COPY, collective_id=PAIR_ID),
    )(*shards, *outs)
    return list(res[:4]), list(res[4:4 + n]), list(res[4 + n:4 + 2 * n]), res[-1]


def _join_wait(name, sems, shards, outs, after):
    n = len(shards)

    def body(*refs):
        srcs, arrs = refs[:n], refs[n:2 * n]
        for cp in _join_copies(srcs, arrs, arrs, *refs[2 * n:2 * n + 4]):
            cp.wait_send()
            cp.wait_recv()

    res = pl.pallas_call(
        body,
        name=name,
        in_specs=[HBM] * (2 * n) + [SEM] * 4 + [ANY] * len(after),
        out_specs=[HBM] * (2 * n),
        out_shape=[pltpu.HBM(s.shape, s.dtype) for s in shards] + [pltpu.HBM(o.shape, o.dtype) for o in outs],
        input_output_aliases={i: i for i in range(2 * n)},
        compiler_params=pltpu.CompilerParams(has_side_effects=SPLIT_COPY),
    )(*shards, *outs, *sems, *after)
    return list(res[n:])


def _pair_copy(grads, lands, send_sem, recv_sem, i):
    x, y, c = _position()
    half = grads[i].shape[1] // 2
    give = pl.ds(pl.multiple_of((1 - c) * half, 16), half)
    return pltpu.make_async_remote_copy(
        src_ref=grads[i].at[:, give, :], dst_ref=lands[i], send_sem=send_sem.at[i], recv_sem=recv_sem.at[i],
        device_id=(x, y, 1 - c), device_id_type=MESH)


def _pair_start(name, grads):
    n = len(grads)

    def body(*refs):
        _pair_handshake()
        srcs, lands = refs[:n], refs[n:2 * n]
        send_sem, recv_sem = refs[2 * n], refs[2 * n + 1]
        token = refs[-1]
        for i in range(n):
            _pair_copy(srcs, lands, send_sem, recv_sem, i).start()
        token[...] = jnp.zeros_like(token)

    halves = [(g.shape[0], g.shape[1] // 2, g.shape[2]) for g in grads]
    res = pl.pallas_call(
        body,
        name=name,
        in_specs=[HBM] * (2 * n),
        out_specs=[SEM, SEM] + [HBM] * (2 * n) + [pl.BlockSpec(memory_space=pltpu.VMEM)],
        out_shape=[pltpu.SemaphoreType.DMA((n,)), pltpu.SemaphoreType.DMA((n,))]
        + [pltpu.HBM(g.shape, g.dtype) for g in grads]
        + [pltpu.HBM(shp, g.dtype) for shp, g in zip(halves, grads)]
        + [jax.ShapeDtypeStruct((8, LANES), F32)],
        input_output_aliases={i: 2 + i for i in range(2 * n)},
        compiler_params=pltpu.CompilerParams(has_side_effects=SPLIT_COPY, collective_id=PAIR_ID),
    )(*[pltpu.with_memory_space_constraint(g, pltpu.HBM) for g in grads],
      *[pltpu.with_memory_space_constraint(lax.empty(shp, g.dtype), pltpu.HBM) for shp, g in zip(halves, grads)])
    return res[0], res[1], list(res[2:2 + n]), list(res[2 + n:2 + 2 * n]), res[-1]


def _pair_wait(name, send_sem, recv_sem, grads, lands, after):
    n = len(grads)

    def body(*refs):
        srcs, land_refs = refs[:n], refs[n:2 * n]
        send_ref, recv_ref = refs[2 * n], refs[2 * n + 1]
        for i in range(n):
            copy = _pair_copy(srcs, land_refs, send_ref, recv_ref, i)
            copy.wait_send()
            copy.wait_recv()

    res = pl.pallas_call(
        body,
        name=name,
        in_specs=[HBM] * (2 * n) + [SEM, SEM, ANY],
        out_specs=[HBM] * (2 * n),
        out_shape=[pltpu.HBM(g.shape, g.dtype) for g in grads] + [pltpu.HBM(l.shape, l.dtype) for l in lands],
        input_output_aliases={i: i for i in range(2 * n)},
        compiler_params=pltpu.CompilerParams(has_side_effects=SPLIT_COPY),
    )(*grads, *lands, send_sem, recv_sem, after)
    return list(res[:n]), list(res[n:])


def _scatter_copy(srcs, lands, send_sem, recv_sem, i, j):
    x, y, c = _position()
    chips = _other_chips(x, y)
    return pltpu.make_async_remote_copy(
        src_ref=srcs[i].at[2 * chips[j][0] + chips[j][1]], dst_ref=lands[i].at[j],
        send_sem=send_sem.at[3 * i + j], recv_sem=recv_sem.at[3 * i + j],
        device_id=(chips[j][0], chips[j][1], c), device_id_type=MESH)


def _scatter_start(name, sums):
    n = len(sums)

    def body(*refs):
        srcs, lands = refs[:n], refs[n:2 * n]
        send_sem, recv_sem = refs[2 * n], refs[2 * n + 1]
        token = refs[-1]
        for i in range(n):
            for j in range(3):
                _scatter_copy(srcs, lands, send_sem, recv_sem, i, j).start()
        token[...] = jnp.zeros_like(token)

    land_shapes = [(3,) + s.shape[1:] for s in sums]
    res = pl.pallas_call(
        body,
        name=name,
        in_specs=[HBM] * (2 * n),
        out_specs=[SEM, SEM] + [HBM] * (2 * n) + [pl.BlockSpec(memory_space=pltpu.VMEM)],
        out_shape=[pltpu.SemaphoreType.DMA((3 * n,)), pltpu.SemaphoreType.DMA((3 * n,))]
        + [pltpu.HBM(s.shape, s.dtype) for s in sums]
        + [pltpu.HBM(shp, s.dtype) for shp, s in zip(land_shapes, sums)]
        + [jax.ShapeDtypeStruct((8, LANES), F32)],
        input_output_aliases={i: 2 + i for i in range(2 * n)},
        compiler_params=pltpu.CompilerParams(has_side_effects=SPLIT_COPY),
    )(*[pltpu.with_memory_space_constraint(s, pltpu.HBM) for s in sums],
      *[pltpu.with_memory_space_constraint(lax.empty(shp, s.dtype), pltpu.HBM) for shp, s in zip(land_shapes, sums)])
    return res[0], res[1], list(res[2:2 + n]), list(res[2 + n:2 + 2 * n]), res[-1]


def _scatter_wait(name, send_sem, recv_sem, sums, lands, after):
    n = len(sums)

    def body(*refs):
        srcs, land_refs = refs[:n], refs[n:2 * n]
        send_ref, recv_ref = refs[2 * n], refs[2 * n + 1]
        for i in range(n):
            for j in range(3):
                copy = _scatter_copy(srcs, land_refs, send_ref, recv_ref, i, j)
                copy.wait_send()
                copy.wait_recv()

    res = pl.pallas_call(
        body,
        name=name,
        in_specs=[HBM] * (2 * n) + [SEM, SEM, ANY],
        out_specs=[HBM] * (2 * n),
        out_shape=[pltpu.HBM(s.shape, s.dtype) for s in sums] + [pltpu.HBM(l.shape, l.dtype) for l in lands],
        input_output_aliases={i: i for i in range(2 * n)},
        compiler_params=pltpu.CompilerParams(has_side_effects=SPLIT_COPY),
    )(*sums, *lands, send_sem, recv_sem, after)
    return list(res[:n]), list(res[n:])


def _pair_join(name, halves, small=None):
    n = len(halves)
    if small is None:
        def body_plain(*refs):
            _pair_handshake()
            ins, outs = refs[:n], refs[n:2 * n]
            send_sem, recv_sem = refs[2 * n:]
            x, y, c = _position()
            swaps = [pltpu.make_async_remote_copy(
                src_ref=ins[i], dst_ref=outs[i], send_sem=send_sem.at[i], recv_sem=recv_sem.at[i],
                device_id=(x, y, 1 - c), device_id_type=MESH) for i in range(n)]
            for swap in swaps:
                swap.start()
            for swap in swaps:
                swap.wait()

        return pl.pallas_call(
            body_plain,
            name=name,
            in_specs=[ANY] * n,
            out_specs=[ANY] * n,
            out_shape=[jax.ShapeDtypeStruct(h.shape, h.dtype) for h in halves],
            scratch_shapes=[pltpu.SemaphoreType.DMA((n,))] * 2,
            compiler_params=PAIR_CALL,
        )(*halves)

    def body(*refs):
        ins, small_ref = refs[:n], refs[n]
        outs, all_ref = refs[n + 1:2 * n + 1], refs[2 * n + 1]
        send_sem, recv_sem, sm_send, sm_recv, sm_local = refs[2 * n + 2:]
        x, y, c = _position()
        swaps = []
        for i in range(n):
            swap = pltpu.make_async_remote_copy(
                src_ref=ins[i], dst_ref=outs[i], send_sem=send_sem.at[i], recv_sem=recv_sem.at[i],
                device_id=(x, y, 1 - c), device_id_type=MESH)
            swap.start()
            swaps.append(swap)
        me = 4 * x + 2 * y + c
        sm_own = pltpu.make_async_copy(small_ref, all_ref.at[me], sm_local)
        sm_own.start()
        pushes, arrivals = [], []
        for mask in range(1, N_DEV):
            px, py, pc = x ^ (mask >> 2), y ^ ((mask >> 1) & 1), c ^ (mask & 1)
            pushes.append(pltpu.make_async_remote_copy(
                src_ref=small_ref, dst_ref=all_ref.at[me], send_sem=sm_send.at[mask - 1], recv_sem=sm_recv.at[mask - 1],
                device_id=(px, py, pc), device_id_type=MESH))
            arrivals.append(pltpu.make_async_remote_copy(
                src_ref=small_ref, dst_ref=all_ref.at[4 * px + 2 * py + pc], send_sem=sm_send.at[mask - 1],
                recv_sem=sm_recv.at[mask - 1], device_id=(px, py, pc), device_id_type=MESH))
        for cp in pushes:
            cp.start()
        for swap in swaps:
            swap.wait()
        for cp in arrivals:
            cp.wait_recv()
        for cp in pushes:
            cp.wait_send()
        sm_own.wait()

    res = pl.pallas_call(
        body,
        name=name,
        in_specs=[ANY] * (n + 1),
        out_specs=[ANY] * (n + 1),
        out_shape=[jax.ShapeDtypeStruct(h.shape, h.dtype) for h in halves]
        + [jax.ShapeDtypeStruct((N_DEV,) + small.shape, small.dtype)],
        scratch_shapes=[pltpu.SemaphoreType.DMA((n,))] * 2 + [pltpu.SemaphoreType.DMA((N_DEV - 1,))] * 2
        + [pltpu.SemaphoreType.DMA(())],
    )(*halves, small)
    return res[:n], res[n]


def _lower_bound(lbp):
    return jax.nn.softmax(lbp, axis=0)[0:1]


def _local_step(x, target, g1, gm, g2, gq, gk, go, rel_bias, lbp, weights, on_grads, grads_sent):
    b, s, d = x.shape
    t = b * s
    x0 = x.reshape(t, d)
    tgt = target.reshape(t, d)
    gq_t = jnp.tile(gq, (1, ATTN_HEADS))
    gk_t = jnp.tile(gk, (1, ATTN_HEADS))
    lb = _lower_bound(lbp)
    table = _band_table(_rel_bias_table("rel_bias_table", rel_bias))

    h1 = _rmsnorm_fwd("norm1", x0, g1)
    wg1, wu1, deps1 = weights["first"]((h1, table))
    a1, b1, z1 = _ffn_up("ffn1_up", h1, wg1, wu1, deps1)
    wd1, deps_mid = weights["mid"]((z1,))
    x1, h2 = _ffn_down("ffn1_down", z1, wd1, x0, gm, deps_mid)
    w_in, w_out = weights["mid_rest"]((x1,))
    ns = w_in.shape[0]
    proj = _in_proj("in_proj", h2, w_in)
    proj3 = proj.reshape(b, s, proj.shape[1])
    qn, kn, vb = _qk_prep("qk_prep", proj3, gq_t, gk_t)
    attn = _attn_fwd("attn_fwd", qn, kn, vb, table, weights["last_begin"]((qn,))).reshape(t, ATTN_W)
    mix, oraw, states = _hgrn_fwd("hgrn_fwd", proj, attn, lb, go, b, s)
    x2, h3 = _out_proj("out_proj", mix, w_out, x1, g2)
    wg2, wu2, wd2 = weights["last"]((h3,))
    a2, b2, z2 = _ffn_up("ffn2_up", h3, wg2, wu2)
    dy, dyh, sq = _ffn_down_loss("ffn2_down_loss", z2, wd2, x2, tgt)
    loss = 0.5 * jnp.sum(sq) / d

    da2, db2 = _ffn_bwd_act("ffn2_bwd_act", dyh, wd2, a2, b2)
    dwd2 = _grad_w_cols("ffn2_dwd", z2, dyh)
    dwg2 = _grad_w_cols("ffn2_dwg", da2, h3)
    dwu2 = _grad_w_cols("ffn2_dwu", db2, h3)
    sent2 = on_grads("ffn2", {"ffn2_w_gate": dwg2, "ffn2_w_up": dwu2, "ffn2_w_down": dwd2})
    dx2, dx2b, dg2 = _ffn_bwd_in("ffn2_bwd_in", da2, db2, wg2, wu2, x2, g2, dy, 1.0, sent2)
    sent2 = grads_sent("ffn2", dx2b)

    dwout = _grad_w_out("dw_out", mix, dx2b)
    dmix = _out_proj_bwd("out_proj_bwd", dx2b, w_out, sent2)
    dqn, dkn, dvn, dbe, dbo = _attn_bwd("attn_bwd", qn, kn, vb, table, dmix.reshape(b, s, dmix.shape[1]))
    dpq, dpk, dpv, dgq, dgk = _qk_prep_bwd("qk_prep_bwd", proj3, dqn, dkn, dvn, gq_t, gk_t)
    dpq, dpk, dpv = (a.reshape(t, ATTN_W) for a in (dpq, dpk, dpv))
    dproj, dlb, dgo = _hgrn_bwd("hgrn_bwd", proj, (dpq, dpk, dpv), lb, go, oraw, states, dmix, b, s)
    dwin = _grad_w_in("dw_in", h2, dproj, ns)
    dx1, dx1h, dgm = _in_proj_bwd("in_proj_bwd", dproj, w_in, x1, gm, dx2, 0.5)

    dwd1 = _grad_w_cols("ffn1_dwd", z1, dx1h)
    sent_mix = on_grads("mix", {"w_in": dwin, "w_out": dwout.reshape(ns, dwout.shape[0] // ns, d),
                                "ffn1_w_down": dwd1})
    da1, db1 = _ffn_bwd_act("ffn1_bwd_act", dx1h, wd1, a1, b1, sent_mix)
    sent_mix = grads_sent("mix", da1)
    dwg1 = _grad_w_cols("ffn1_dwg", da1, h1, sent_mix)
    dwu1 = _grad_w_cols("ffn1_dwu", db1, h1)
    on_grads("ffn1", {"ffn1_w_gate": dwg1, "ffn1_w_up": dwu1})
    sent1 = grads_sent("ffn1", None)
    dx0, dg1 = _ffn_bwd_in("ffn1_bwd_in", da1, db1, wg1, wu1, x0, g1, dx1, None, sent1)

    nt = dg1.shape[0]
    sg = _small_grads(
        "small_grads", dg1.reshape(nt, d), dgm.reshape(nt, d), dg2.reshape(nt, d),
        dgq.reshape(-1, ATTN_W), dgk.reshape(-1, ATTN_W), dbe.transpose(1, 0, 2), dbo.transpose(1, 0, 2),
        dlb.reshape(b, HGRN_W), dgo.reshape(b, HGRN_W), lbp)
    g1g, gmg, g2g, gqg, gkg, rbg, lbg, gog = sg
    small = _pack_small(g1g, gmg, g2g, lbg, rbg[:, :N_REL], gqg, gkg, gog, loss)
    return dx0.reshape(b, s, d), small


LOSS_SLOT = 7 * SMALL_COLS + 2 * ATTN_DH + HGRN_DH


def _pack_small(g1, gm, g2, lbp, rel_bias, gq, gk, go, loss=None):
    flat = [g1.reshape(-1), gm.reshape(-1), g2.reshape(-1), lbp.reshape(-1), rel_bias.reshape(-1)]
    n_bias = 3 * SMALL_COLS - rel_bias.size
    heads = [gq.reshape(-1), gk.reshape(-1), go.reshape(-1)]
    heads.append(jnp.zeros((1,), F32) if loss is None else loss.reshape(1))
    n_tail = SMALL_COLS - sum(h.size for h in heads)
    return jnp.concatenate(flat + [jnp.zeros((n_bias,), F32)] + heads + [jnp.zeros((n_tail,), F32)]).reshape(
        SMALL_ROWS, SMALL_COLS)


def _unpack_small(p, d):
    flat = p.reshape(-1)
    o = 3 * d
    g1, gm, g2 = p[0:1], p[1:2], p[2:3]
    lbp = flat[o:o + 2 * HGRN_W].reshape(2, HGRN_W)
    o = 4 * SMALL_COLS
    rel = flat[o:o + ATTN_HEADS * N_REL].reshape(1, ATTN_HEADS, N_REL)
    o = 7 * SMALL_COLS
    gq = flat[o:o + ATTN_DH].reshape(1, ATTN_DH)
    gk = flat[o + ATTN_DH:o + 2 * ATTN_DH].reshape(1, ATTN_DH)
    go = flat[o + 2 * ATTN_DH:o + 2 * ATTN_DH + HGRN_DH].reshape(1, HGRN_DH)
    return g1, gm, g2, gq, gk, rel, lbp, go


def kernel(x, ffn1_norm_g, ffn1_w_gate, ffn1_w_up, ffn1_w_down, mix_norm_g, w_in, attn_q_norm_g, attn_k_norm_g, attn_rel_bias, hgrn_lower_bounds, hgrn_out_norm_g, w_out, ffn2_norm_g, ffn2_w_gate, ffn2_w_up, ffn2_w_down, loss_target, m_ffn1_norm_g, m_ffn1_w_gate, m_ffn1_w_up, m_ffn1_w_down, m_mix_norm_g, m_w_in, m_attn_q_norm_g, m_attn_k_norm_g, m_attn_rel_bias, m_hgrn_lower_bounds, m_hgrn_out_norm_g, m_w_out, m_ffn2_norm_g, m_ffn2_w_gate, m_ffn2_w_up, m_ffn2_w_down, v_ffn1_norm_g, v_ffn1_w_gate, v_ffn1_w_up, v_ffn1_w_down, v_mix_norm_g, v_w_in, v_attn_q_norm_g, v_attn_k_norm_g, v_attn_rel_bias, v_hgrn_lower_bounds, v_hgrn_out_norm_g, v_w_out, v_ffn2_norm_g, v_ffn2_w_gate, v_ffn2_w_up, v_ffn2_w_down):
    d = x.shape[-1]
    big_w = [ffn1_w_gate, ffn1_w_up, ffn1_w_down, w_in, w_out, ffn2_w_gate, ffn2_w_up, ffn2_w_down]
    big_m = [m_ffn1_w_gate, m_ffn1_w_up, m_ffn1_w_down, m_w_in, m_w_out, m_ffn2_w_gate, m_ffn2_w_up, m_ffn2_w_down]
    big_v = [v_ffn1_w_gate, v_ffn1_w_up, v_ffn1_w_down, v_w_in, v_w_out, v_ffn2_w_gate, v_ffn2_w_up, v_ffn2_w_down]
    big_names = ["ffn1_w_gate", "ffn1_w_up", "ffn1_w_down", "w_in", "w_out", "ffn2_w_gate", "ffn2_w_up", "ffn2_w_down"]
    flipped = {nm for nm in big_names if nm.endswith("gate") or nm.endswith("up")}
    flip = lambda nm, a: jnp.swapaxes(a, 1, 2) if nm in flipped else a
    big_w, big_m, big_v = ([flip(nm, a) for nm, a in zip(big_names, arrs)] for arrs in (big_w, big_m, big_v))

    shards = [w[0].astype(BF16) for w in big_w]
    start_a = _gather_start("gather_start_up1", shards[:2], ())
    start_b = _gather_start("gather_start_mid", shards[2:5], (start_a[4],))
    start_c = _gather_start("gather_start_ffn2", shards[5:], (start_b[4],))

    pending = {}

    def arrived(tag, started, after):
        send_sem, recv_sem, srcs, outs, _ = started
        return _gather_wait("gather_wait_" + tag, send_sem, recv_sem, srcs, outs, after)

    def first_weights(after):
        return (*_gather_join("gather_join_up1", *arrived("up1", start_a, after)), (start_c[4],))

    def mid_weights(after):
        srcs, outs = arrived("mid", start_b, after)
        (wd1,) = _gather_join("gather_join_wd1", srcs[:1], outs[:1])
        pending["mid"] = _join_start("join_start_mid", srcs[1:], outs[1:])
        return wd1, (pending["mid"][3],)

    def mid_rest(after):
        sems, srcs, outs, _ = pending["mid"]
        win_f, wout_f = _join_wait("join_wait_mid", sems, srcs, outs, after)
        return win_f, wout_f.reshape(wout_f.shape[0] * wout_f.shape[1], d)

    def last_begin(after):
        pending["ffn2"] = _join_start("join_start_ffn2", *arrived("ffn2", start_c, after))
        return (pending["ffn2"][3],)

    def last_weights(after):
        sems, srcs, outs, _ = pending["ffn2"]
        return _join_wait("join_wait_ffn2", sems, srcs, outs, after)

    weights = {"first": first_weights, "mid": mid_weights, "mid_rest": mid_rest, "last_begin": last_begin,
               "last": last_weights}

    core = lax.axis_index("c").astype(jnp.int32).reshape(1)
    chip = (2 * lax.axis_index("x") + lax.axis_index("y")).astype(jnp.int32).reshape(1)
    started = {}

    def on_grads(tag, grads):
        names = list(grads)
        started[tag] = (names, _pair_start("pair_start_" + tag, [grads[nm] for nm in names]))
        return (started[tag][1][4],)

    def grads_sent(tag, after):
        names, (send_sem, recv_sem, grads, lands, token) = started[tag]
        grads, theirs = _pair_wait("pair_wait_" + tag, send_sem, recv_sem, grads, lands, token if after is None else after)
        sums = [_pair_sum("pair_sum_" + nm, g, th, core) for nm, g, th in zip(names, grads, theirs)]
        started[tag] = (names, _scatter_start("scatter_start_" + tag, sums))
        return (started[tag][1][4],)

    grad_x, small_g = _local_step(
        x, loss_target, ffn1_norm_g, mix_norm_g, ffn2_norm_g, attn_q_norm_g, attn_k_norm_g, hgrn_out_norm_g,
        attn_rel_bias[0], hgrn_lower_bounds, weights, on_grads, grads_sent)

    def finish(tag, after):
        names, (send_sem, recv_sem, sums, lands, _) = started[tag]
        sums, lands = _scatter_wait("scatter_wait_" + tag, send_sem, recv_sem, sums, lands, after)
        return names, [_chip_sum("chip_sum_" + nm, sm, ld, chip) for nm, sm, ld in zip(names, sums, lands)]

    by_name = {nm: (w, m, v) for nm, w, m, v in zip(big_names, big_w, big_m, big_v)}
    updated = {}

    def update(names, halves, other_halves):
        for nm, mine, theirs in zip(names, halves, other_halves):
            w, m, v = by_name[nm]
            updated[nm] = _adamw("adamw_" + nm, w, mine, theirs, m, v, core)

    last_token = started["ffn1"][1][4]
    names_a, halves_a = finish("ffn2", last_token)
    names_m, halves_m = finish("mix", last_token)
    names_a, halves_a = names_a + names_m, halves_a + halves_m
    update(names_a, halves_a, _pair_join("pair_join_early", halves_a))
    names_b, halves_b = finish("ffn1", updated[names_a[-1]][1])
    others_b, small_all = _pair_join("pair_join_last", halves_b, small_g)
    update(names_b, halves_b, others_b)
    big_out = [updated[nm] for nm in big_names]

    pack = lambda g1, gm, g2, gq, gk, rel, lbp, go: _pack_small(g1, gm, g2, lbp, rel[0], gq, gk, go)
    small_w = pack(ffn1_norm_g, mix_norm_g, ffn2_norm_g, attn_q_norm_g, attn_k_norm_g, attn_rel_bias, hgrn_lower_bounds, hgrn_out_norm_g)
    small_m = pack(m_ffn1_norm_g, m_mix_norm_g, m_ffn2_norm_g, m_attn_q_norm_g, m_attn_k_norm_g, m_attn_rel_bias, m_hgrn_lower_bounds, m_hgrn_out_norm_g)
    small_v = pack(v_ffn1_norm_g, v_mix_norm_g, v_ffn2_norm_g, v_attn_q_norm_g, v_attn_k_norm_g, v_attn_rel_bias, v_hgrn_lower_bounds, v_hgrn_out_norm_g)
    small_res = _adamw_small("adamw_small", small_w, small_all, small_m, small_v)
    small_out = [_unpack_small(p, d) for p in small_res]
    loss = small_res[0].reshape(-1)[LOSS_SLOT]

    def assemble(kind):
        bg = [flip(nm, o[kind]) for nm, o in zip(big_names, big_out)]
        g1, gm, g2, gq, gk, rel, lbp, go = small_out[kind]
        return [g1, bg[0], bg[1], bg[2], gm, bg[3], gq, gk, rel, lbp, go, bg[4], g2, bg[5], bg[6], bg[7]]

    return (loss, grad_x, *assemble(0), *assemble(1), *assemble(2), *assemble(3))
```

```python
import functools

import jax
import jax.numpy as jnp
from jax import lax
from jax.experimental import pallas as pl
from jax.experimental.pallas import tpu as pltpu

F32 = jnp.float32
BF16 = jnp.bfloat16
MESH = pl.DeviceIdType.MESH

N_CHIPS = 4
N_DEV = 8
CHUNK = 64
ATTN_HEADS = 8
ATTN_DH = 64
ATTN_W = ATTN_HEADS * ATTN_DH
HGRN_HEADS = 4
HGRN_DH = 128
HGRN_W = HGRN_HEADS * HGRN_DH
LEFT_CHUNKS = 8
BAND = (LEFT_CHUNKS + 1) * CHUNK
KPAD = LEFT_CHUNKS * CHUNK
REL_CLIP = 128
N_REL = 2 * REL_CLIP + 1
N_REL_PAD = 384
RMS_EPS = 1e-6
LANES = 128
SMALL_ROWS = 8
SMALL_COLS = 1024

ADAM_LR = 0.001
ADAM_B1 = 0.9
ADAM_B2 = 0.999
ADAM_EPS = 1e-08
ADAM_WD = 0.01
ADAM_STEP = 10

NN = (((1,), (0,)), ((), ()))
NT = (((1,), (1,)), ((), ()))
TN = (((0,), (0,)), ((), ()))

VMEM_LIMIT = 48 * 1024 * 1024
MXU_WIDTH = 256
COL_CHUNK = 3 * MXU_WIDTH


def _sigmoid(x):
    return 1.0 / (1.0 + jnp.exp(-x))


def _silu(x):
    return x * _sigmoid(x)


def _dot(a, b, dims=NN):
    return lax.dot_general(a, b, dims, preferred_element_type=F32)


def _split3(x):
    hi = x.astype(BF16)
    r1 = x - hi.astype(F32)
    mid = r1.astype(BF16)
    lo = (r1 - mid.astype(F32)).astype(BF16)
    return hi, mid, lo


def _dot_exact_rhs(x, mat, dims=NN, pieces=3):
    hi, mid, lo = _split3(x)
    out = _dot(hi, mat, dims) + _dot(mid, mat, dims)
    return out + _dot(lo, mat, dims) if pieces == 3 else out


def _dot_exact_lhs(mat, x, dims=NN):
    hi, mid, lo = _split3(x)
    return _dot(mat, hi, dims) + _dot(mat, mid, dims) + _dot(mat, lo, dims)


def _params(*sem):
    return pltpu.CompilerParams(dimension_semantics=sem, vmem_limit_bytes=VMEM_LIMIT)


def _mm(name, ins, terms, n_acc, grid, acc_shape, outs, epilogue, extras=(), deps=()):
    nk = grid[2]
    ni, ne, nd, no = len(ins), len(extras), len(deps), len(outs)

    def body(*refs):
        in_refs = refs[:ni]
        ex_refs = refs[ni:ni + ne]
        out_refs = refs[ni + ne + nd:ni + ne + nd + no]
        acc_refs = refs[ni + ne + nd + no:]

        def products():
            parts = [None] * n_acc
            for ai, li, ri, dims in terms:
                d = _dot(in_refs[li][...], in_refs[ri][...], dims)
                parts[ai] = d if parts[ai] is None else parts[ai] + d
            return parts

        def finish(accs):
            res = epilogue(accs, [e[...] for e in ex_refs])
            for o, r in zip(out_refs, res):
                o[...] = r.astype(o.dtype)

        if nk == 1:
            finish(products())
        else:
            k = pl.program_id(2)

            @pl.when(k == 0)
            def _():
                for a, p in zip(acc_refs, products()):
                    a[...] = p

            if nk > 2:
                @pl.when(jnp.logical_and(k > 0, k < nk - 1))
                def _():
                    for a, p in zip(acc_refs, products()):
                        a[...] += p

            @pl.when(k == nk - 1)
            def _():
                finish([a[...] + p for a, p in zip(acc_refs, products())])

    scratch = [] if nk == 1 else [pltpu.VMEM(acc_shape, F32) for _ in range(n_acc)]
    res = pl.pallas_call(
        body,
        name=name,
        grid=grid,
        in_specs=[s for _, s in ins] + [s for _, s in extras] + [pl.BlockSpec(memory_space=pl.ANY)] * nd,
        out_specs=[s for _, s in outs],
        out_shape=[o for o, _ in outs],
        scratch_shapes=scratch,
        compiler_params=_params("parallel", "parallel", "arbitrary"),
    )(*[a for a, _ in ins], *[a for a, _ in extras], *deps)
    return res


def _staged_shape(w):
    return w.shape if len(w.shape) == 2 else (w.shape[1], w.shape[0] * w.shape[2])


def _stage_weights(w_hbm, w_vmem, sem):
    @pl.when(pl.program_id(0) == 0)
    def _():
        copies = []
        for p, (h, v) in enumerate(zip(w_hbm, w_vmem)):
            if len(h.shape) == 2:
                copies.append(pltpu.make_async_copy(h, v, sem.at[p, 0]))
            else:
                pj = h.shape[2]
                copies += [pltpu.make_async_copy(h.at[j], v.at[:, pl.ds(j * pj, pj)], sem.at[p, j])
                           for j in range(h.shape[0])]
        for cp in copies:
            cp.start()
        for cp in copies:
            cp.wait()


def _staging_scratch(weights):
    return [pltpu.VMEM(_staged_shape(w), w.dtype) for w in weights] + [pltpu.SemaphoreType.DMA((len(weights), N_CHIPS))]


def _mm_rows(name, lhs, weights, dims, t, outs, epilogue, extras=(), deps=()):
    tm = _row_tile(t)
    nl, ne, nd, no = len(lhs), len(extras), len(deps), len(outs)

    def body(*refs):
        lhs_refs = refs[:nl]
        w_hbm = refs[nl:2 * nl]
        ex_refs = refs[2 * nl:2 * nl + ne]
        out_refs = refs[2 * nl + ne + nd:2 * nl + ne + nd + no]
        w_vmem = refs[2 * nl + ne + nd + no:3 * nl + ne + nd + no]
        _stage_weights(w_hbm, w_vmem, refs[-1])

        acc = None
        for p in range(nl):
            part = _dot(lhs_refs[p][...], w_vmem[p][...], dims)
            acc = part if acc is None else acc + part
        res = epilogue([acc], [e[...] for e in ex_refs])
        for o, r in zip(out_refs, res):
            o[...] = r.astype(o.dtype)

    return pl.pallas_call(
        body,
        name=name,
        grid=(t // tm,),
        in_specs=[s for _, s in lhs] + [pl.BlockSpec(memory_space=pl.ANY)] * nl + [s for _, s in extras]
        + [pl.BlockSpec(memory_space=pl.ANY)] * nd,
        out_specs=[s for _, s in outs],
        out_shape=[o for o, _ in outs],
        scratch_shapes=_staging_scratch(weights),
        compiler_params=_params("arbitrary"),
    )(*[a for a, _ in lhs], *weights, *[a for a, _ in extras], *deps)


def _col_chunks(f):
    return [(c, min(COL_CHUNK, f - c)) for c in range(0, f, COL_CHUNK)]


def _mm_cols(name, x, weights, dims, n_out, epilogue, extras=(), deps=(), out_dtype=BF16):
    t, k = x.shape
    f = _staged_shape(weights[0])[0 if dims == NT else 1]
    tm = _row_tile(t)
    chunks = _col_chunks(f)
    nw, ne, nd = len(weights), len(extras), len(deps)

    def body(*refs):
        x_ref = refs[0]
        w_hbm = refs[1:1 + nw]
        ex_refs = refs[1 + nw:1 + nw + ne]
        out_refs = refs[1 + nw + ne + nd:1 + nw + ne + nd + n_out]
        w_vmem = refs[1 + nw + ne + nd + n_out:1 + 2 * nw + ne + nd + n_out]
        _stage_weights(w_hbm, w_vmem, refs[-1])

        xv = x_ref[...]

        def dots(c):
            c0, cw = chunks[c]
            return [_dot(xv, w[c0:c0 + cw, :] if dims == NT else w[:, c0:c0 + cw], dims) for w in w_vmem]

        accs = dots(0)
        for c, (c0, cw) in enumerate(chunks):
            nxt = dots(c + 1) if c + 1 < len(chunks) else None
            res = epilogue(accs, [e[:, c0:c0 + cw] for e in ex_refs])
            for o, r in zip(out_refs, res):
                o[:, c0:c0 + cw] = r.astype(o.dtype)
            accs = nxt

    act = pl.BlockSpec((tm, f), lambda i: (i, 0))
    return pl.pallas_call(
        body,
        name=name,
        grid=(t // tm,),
        in_specs=[pl.BlockSpec((tm, k), lambda i: (i, 0))] + [pl.BlockSpec(memory_space=pl.ANY)] * nw + [act] * ne
        + [pl.BlockSpec(memory_space=pl.ANY)] * nd,
        out_specs=[act] * n_out,
        out_shape=[jax.ShapeDtypeStruct((t, f), out_dtype)] * n_out,
        scratch_shapes=_staging_scratch(weights),
        compiler_params=_params("arbitrary"),
    )(x, *weights, *extras, *deps)


def _row_tile(t):
    return 512 if t % 512 == 0 else t


def _k_tile(t):
    return t if t <= 4096 else 1024


def _grad_k_tile(t):
    return 2048 if t % 2048 == 0 else t


def _rmsnorm(xv, g):
    ms = jnp.mean(xv * xv, axis=-1, keepdims=True)
    return xv * lax.rsqrt(ms + RMS_EPS) * g


def _rmsnorm_fwd(name, x, g):
    t, d = x.shape
    tm = _row_tile(t)

    def body(x_ref, g_ref, h_ref):
        h_ref[...] = _rmsnorm(x_ref[...], g_ref[...]).astype(BF16)

    return pl.pallas_call(
        body,
        name=name,
        grid=(t // tm,),
        in_specs=[pl.BlockSpec((tm, d), lambda i: (i, 0)), pl.BlockSpec((1, d), lambda i: (0, 0))],
        out_specs=pl.BlockSpec((tm, d), lambda i: (i, 0)),
        out_shape=jax.ShapeDtypeStruct((t, d), BF16),
        compiler_params=_params("parallel"),
    )(x, g)


def _norm_bwd_epilogue(copy_scale):
    def epilogue(accs, ex):
        dh = accs[0]
        xv, g, dres = ex
        ms = jnp.mean(xv * xv, axis=-1, keepdims=True)
        rstd = lax.rsqrt(ms + RMS_EPS)
        xhat = xv * rstd
        dxhat = dh * g
        dx = rstd * (dxhat - xhat * jnp.mean(dxhat * xhat, axis=-1, keepdims=True))
        out = dres + dx
        dg = jnp.sum(dh * xhat, axis=0, keepdims=True)
        if copy_scale is None:
            return out, dg
        return out, out * copy_scale, dg

    return epilogue


def _merged(w):
    return w.reshape(-1, w.shape[-1])


def _ffn_up(name, h, wg, wu, deps=()):
    def epilogue(accs, ex):
        a, b = accs
        sg = _sigmoid(a)
        act = a * sg
        return act, b * (sg * (1.0 + a * (1.0 - sg))), act * b

    return _mm_cols(name, h, [_merged(wg), _merged(wu)], NT, 3, epilogue, deps=deps)


def _whole_rows(arr, tm):
    return arr, pl.BlockSpec((tm, arr.shape[1]), lambda i: (i, 0))


def _ffn_down(name, z, wd, x, g_next, deps=()):
    t = z.shape[0]
    d = wd.shape[2]
    tm = _row_tile(t)
    row = pl.BlockSpec((tm, d), lambda i: (i, 0))

    def epilogue(accs, ex):
        y = ex[0] + 0.5 * accs[0]
        return y, _rmsnorm(y, ex[1])

    return _mm_rows(
        name, [_whole_rows(z, tm)], [_merged(wd)], NN, t,
        outs=[(jax.ShapeDtypeStruct((t, d), F32), row), (jax.ShapeDtypeStruct((t, d), BF16), row)],
        epilogue=epilogue,
        extras=[(x, row), (g_next, pl.BlockSpec((1, d), lambda i: (0, 0)))],
        deps=deps,
    )


def _ffn_down_loss(name, z, wd, x, target):
    t = z.shape[0]
    d = wd.shape[2]
    tm = _row_tile(t)
    nt = t // tm
    row = pl.BlockSpec((tm, d), lambda i: (i, 0))

    def epilogue(accs, ex):
        e = ex[0] + 0.5 * accs[0] - ex[1]
        dy = e * (1.0 / d)
        return dy, 0.5 * dy, jnp.sum(e * e, axis=0, keepdims=True)

    return _mm_rows(
        name, [_whole_rows(z, tm)], [_merged(wd)], NN, t,
        outs=[(jax.ShapeDtypeStruct((t, d), F32), row), (jax.ShapeDtypeStruct((t, d), BF16), row),
              (jax.ShapeDtypeStruct((nt, 1, d), F32), pl.BlockSpec((None, 1, d), lambda i: (i, 0, 0)))],
        epilogue=epilogue,
        extras=[(x, row), (target, row)],
    )


def _ffn_bwd_act(name, dout, wd, act_a, dact_b, deps=()):
    def epilogue(accs, ex):
        dz = accs[0]
        return dz * ex[1].astype(F32), dz * ex[0].astype(F32)

    return _mm_cols(name, dout, [_merged(wd)], NT, 2, epilogue, extras=[act_a, dact_b], deps=deps)


def _grad_w_cols(name, z, dout, deps=()):
    t, f = z.shape
    d = dout.shape[1]
    tk = _grad_k_tile(t)
    fh = f // 2
    dw = _mm(
        name,
        ins=[(z, pl.BlockSpec((tk, fh), lambda j, n, k: (k, j))),
             (dout, pl.BlockSpec((tk, d), lambda j, n, k: (k, 0)))],
        terms=[(0, 0, 1, TN)],
        n_acc=1,
        grid=(2, 1, t // tk),
        acc_shape=(fh, d),
        outs=[(pltpu.HBM((f, d), BF16), pl.BlockSpec((fh, d), lambda j, n, k: (j, 0)))],
        epilogue=lambda accs, ex: (accs[0],),
        deps=deps,
    )[0]
    return dw.reshape(N_CHIPS, f // N_CHIPS, d)


def _norm_bwd_outs(t, d, tm, copy_scale):
    row = pl.BlockSpec((tm, d), lambda i: (i, 0))
    outs = [(jax.ShapeDtypeStruct((t, d), F32), row)]
    if copy_scale is not None:
        outs.append((jax.ShapeDtypeStruct((t, d), BF16), row))
    outs.append((jax.ShapeDtypeStruct((t // tm, 1, d), F32), pl.BlockSpec((None, 1, d), lambda i: (i, 0, 0))))
    return row, outs


def _ffn_bwd_in(name, da, db, wg, wu, x, g, dres, copy_scale, deps=()):
    t = da.shape[0]
    d = wg.shape[2]
    tm = _row_tile(t)
    row, outs = _norm_bwd_outs(t, d, tm, copy_scale)
    return _mm_rows(
        name, [_whole_rows(da, tm), _whole_rows(db, tm)], [_merged(wg), _merged(wu)], NN, t,
        outs=outs,
        epilogue=_norm_bwd_epilogue(copy_scale),
        extras=[(x, row), (g, pl.BlockSpec((1, d), lambda i: (0, 0))), (dres, row)],
        deps=deps,
    )


def _in_proj(name, h, w_in):
    return _mm_cols(name, h, [w_in], NN, 1, lambda accs, ex: (accs[0],), out_dtype=F32)[0]


def _in_proj_bwd(name, dp, w_in, x, g, dres, copy_scale, deps=()):
    t = dp.shape[0]
    d = w_in.shape[1]
    tm = _row_tile(t)
    row, outs = _norm_bwd_outs(t, d, tm, copy_scale)
    return _mm_rows(
        name, [_whole_rows(dp, tm)], [w_in], NT, t,
        outs=outs,
        epilogue=_norm_bwd_epilogue(copy_scale),
        extras=[(x, row), (g, pl.BlockSpec((1, d), lambda i: (0, 0))), (dres, row)],
        deps=deps,
    )


def _grad_w_in(name, h, dp, ns):
    t, d = h.shape
    pj = dp.shape[1] // ns
    tk = _k_tile(t)
    return _mm(
        name,
        ins=[(h, pl.BlockSpec((tk, d), lambda j, n, k: (k, 0))),
             (dp, pl.BlockSpec((tk, pj), lambda j, n, k: (k, j)))],
        terms=[(0, 0, 1, TN)],
        n_acc=1,
        grid=(ns, 1, t // tk),
        acc_shape=(d, pj),
        outs=[(pltpu.HBM((ns, d, pj), BF16), pl.BlockSpec((None, d, pj), lambda j, n, k: (j, 0, 0)))],
        epilogue=lambda accs, ex: (accs[0],),
    )[0]


def _out_proj(name, mix, w_out, x, g_next):
    t, dm = mix.shape
    d = w_out.shape[1]
    tm = _row_tile(t)
    row = pl.BlockSpec((tm, d), lambda i, n, k: (i, 0))
    return _mm(
        name,
        ins=[(mix, pl.BlockSpec((tm, dm), lambda i, n, k: (i, 0))),
             (w_out, pl.BlockSpec((dm, d), lambda i, n, k: (0, 0)))],
        terms=[(0, 0, 1, NN)],
        n_acc=1,
        grid=(t // tm, 1, 1),
        acc_shape=(tm, d),
        outs=[(jax.ShapeDtypeStruct((t, d), F32), row), (jax.ShapeDtypeStruct((t, d), BF16), row)],
        epilogue=lambda accs, ex: (ex[0] + accs[0], _rmsnorm(ex[0] + accs[0], ex[1])),
        extras=[(x, row), (g_next, pl.BlockSpec((1, d), lambda i, n, k: (0, 0)))],
    )


def _out_proj_bwd(name, dx, w_out, deps=()):
    t, d = dx.shape
    dm = w_out.shape[0]
    tm = _row_tile(t)
    return _mm(
        name,
        ins=[(dx, pl.BlockSpec((tm, d), lambda i, n, k: (i, 0))),
             (w_out, pl.BlockSpec((dm, d), lambda i, n, k: (0, 0)))],
        terms=[(0, 0, 1, NT)],
        n_acc=1,
        grid=(t // tm, 1, 1),
        acc_shape=(tm, dm),
        outs=[(jax.ShapeDtypeStruct((t, dm), F32), pl.BlockSpec((tm, dm), lambda i, n, k: (i, 0)))],
        epilogue=lambda accs, ex: (accs[0],),
        deps=deps,
    )[0]


def _grad_w_out(name, mix, dx):
    t, dm = mix.shape
    d = dx.shape[1]
    tk = _k_tile(t)
    return _mm(
        name,
        ins=[(mix, pl.BlockSpec((tk, dm), lambda a, n, k: (k, 0))),
             (dx, pl.BlockSpec((tk, d), lambda a, n, k: (k, 0)))],
        terms=[(0, 0, 1, TN)],
        n_acc=1,
        grid=(1, 1, t // tk),
        acc_shape=(dm, d),
        outs=[(pltpu.HBM((dm, d), BF16), pl.BlockSpec((dm, d), lambda a, n, k: (0, 0)))],
        epilogue=lambda accs, ex: (accs[0],),
    )[0]


def _head_group_matrix():
    r = lax.broadcasted_iota(jnp.int32, (ATTN_W, ATTN_W), 0)
    c = lax.broadcasted_iota(jnp.int32, (ATTN_W, ATTN_W), 1)
    same = jnp.right_shift(r, 6) == jnp.right_shift(c, 6)
    return jnp.where(same, 1.0, 0.0).astype(BF16)


def _qk_prep(name, proj, gq, gk):
    b, s, _ = proj.shape
    tm = KPAD
    nb = s // tm

    def body(q_ref, k_ref, v_ref, gq_ref, gk_ref, qn_ref, kn_ref, vb_ref):
        j = pl.program_id(1)
        bd = _head_group_matrix()

        def norm(xv, g):
            ms = _dot_exact_rhs(xv * xv, bd, pieces=2) * (1.0 / ATTN_DH)
            return xv * lax.rsqrt(ms + RMS_EPS) * g

        @pl.when(j == 0)
        def _():
            kn_ref[...] = jnp.zeros_like(kn_ref)
            vb_ref[...] = jnp.zeros_like(vb_ref)

        @pl.when(j > 0)
        def _():
            qn_ref[...] = norm(q_ref[...], gq_ref[...]).astype(BF16)
            kn_ref[...] = norm(k_ref[...], gk_ref[...]).astype(BF16)
            vb_ref[...] = v_ref[...].astype(BF16)

    src_blk = lambda col: pl.BlockSpec((None, tm, ATTN_W), lambda bi, j: (bi, jnp.maximum(j - 1, 0), col))
    gspec = pl.BlockSpec((1, ATTN_W), lambda bi, j: (0, 0))
    padded = pl.BlockSpec((None, tm, ATTN_W), lambda bi, j: (bi, j, 0))
    return pl.pallas_call(
        body,
        name=name,
        grid=(b, nb + 1),
        in_specs=[src_blk(0), src_blk(1), src_blk(2), gspec, gspec],
        out_specs=[src_blk(0), padded, padded],
        out_shape=[jax.ShapeDtypeStruct((b, s, ATTN_W), BF16), jax.ShapeDtypeStruct((b, KPAD + s, ATTN_W), BF16),
                   jax.ShapeDtypeStruct((b, KPAD + s, ATTN_W), BF16)],
        compiler_params=_params("parallel", "arbitrary"),
    )(proj, proj, proj, gq, gk)


def _qk_prep_bwd(name, proj, dqn, dkn, dv, gq, gk):
    b, s, _ = proj.shape
    tm = KPAD
    nb = s // tm

    def body(q_ref, k_ref, dqn_ref, dkn_ref, dv_ref, gq_ref, gk_ref, dq_ref, dk_ref, dvb_ref, dgq_ref, dgk_ref):
        bd = _head_group_matrix()

        def bwd(xv, dy, g):
            ms = _dot_exact_rhs(xv * xv, bd, pieces=2) * (1.0 / ATTN_DH)
            rstd = lax.rsqrt(ms + RMS_EPS)
            xhat = xv * rstd
            dxhat = dy * g
            gm = _dot_exact_rhs(dxhat * xhat, bd, pieces=2) * (1.0 / ATTN_DH)
            return rstd * (dxhat - xhat * gm), jnp.sum(dy * xhat, axis=0, keepdims=True)

        dq, dgq = bwd(q_ref[...], dqn_ref[...], gq_ref[...])
        dk, dgk = bwd(k_ref[...], dkn_ref[...], gk_ref[...])
        dq_ref[...] = dq.astype(BF16)
        dk_ref[...] = dk.astype(BF16)
        dvb_ref[...] = dv_ref[...].astype(BF16)
        dgq_ref[...] = dgq
        dgk_ref[...] = dgk

    col = lambda c: pl.BlockSpec((None, tm, ATTN_W), lambda bi, j: (bi, j, c))
    past_pad = pl.BlockSpec((None, tm, ATTN_W), lambda bi, j: (bi, j + 1, 0))
    gspec = pl.BlockSpec((1, ATTN_W), lambda bi, j: (0, 0))
    pspec = pl.BlockSpec((None, 1, ATTN_W), lambda bi, j: (bi * nb + j, 0, 0))
    o_shape = jax.ShapeDtypeStruct((b, s, ATTN_W), BF16)
    p_shape = jax.ShapeDtypeStruct((b * nb, 1, ATTN_W), F32)
    return pl.pallas_call(
        body,
        name=name,
        grid=(b, nb),
        in_specs=[col(0), col(1), col(0), past_pad, past_pad, gspec, gspec],
        out_specs=[col(0)] * 3 + [pspec] * 2,
        out_shape=[o_shape] * 3 + [p_shape] * 2,
        compiler_params=_params("parallel", "parallel"),
    )(proj, proj, dqn, dkn, dv, gq, gk)


Q_CHUNKS = 4
QBLK = Q_CHUNKS * CHUNK
WIN = (LEFT_CHUNKS + Q_CHUNKS) * CHUNK
DB_W = BAND + CHUNK
MASKED = -1e30
FWD_BLOCKS = 8
BWD_BLOCKS = 2


def _band_table(bias):
    rows = [jnp.pad(bias, ((0, 0), (0, 0), (CHUNK * i, WIN - BAND - CHUNK * i)), constant_values=MASKED)
            for i in range(Q_CHUNKS)]
    return jnp.concatenate(rows, axis=1)


def _head_lanes(hh):
    lane = lax.broadcasted_iota(jnp.int32, (1, LANES), 1)
    return (lane < ATTN_DH) if hh == 0 else (lane >= ATTN_DH)


def _attn_probs(qh, kw, table, start):
    s = _dot(qh, kw, NT) * (ATTN_DH ** -0.5) + table
    col = lax.broadcasted_iota(jnp.int32, (QBLK, WIN), 1)
    s = jnp.where(col + start >= KPAD, s, MASKED)
    m = jnp.max(s, axis=-1, keepdims=True)
    p = jnp.exp(s - m)
    return p * (1.0 / jnp.sum(p, axis=-1, keepdims=True))


def _attn_fwd(name, q, k, v, table, deps=()):
    b, s, w = q.shape
    sp = k.shape[1]

    def body(q_ref, k_ref, v_ref, t_ref, *rest):
        o_ref = rest[-1]
        lanes = [_head_lanes(hh) for hh in range(2)]
        starts = [pl.multiple_of((pl.program_id(2) * FWD_BLOCKS + j) * QBLK, QBLK) for j in range(FWD_BLOCKS)]
        kws = [k_ref[pl.ds(st, WIN), :] for st in starts]
        vws = [v_ref[pl.ds(st, WIN), :] for st in starts]
        q2s = [q_ref[j * QBLK:(j + 1) * QBLK, :] for j in range(FWD_BLOCKS)]
        probs = [[_attn_probs(jnp.where(mine, q2s[j], jnp.zeros_like(q2s[j])), kws[j], t_ref[hh], starts[j]).astype(BF16)
                  for hh, mine in enumerate(lanes)] for j in range(FWD_BLOCKS)]
        for j in range(FWD_BLOCKS):
            outs = [_dot(p, vws[j]) for p in probs[j]]
            o_ref[j * QBLK:(j + 1) * QBLK, :] = jnp.where(lanes[0], outs[0], outs[1]).astype(BF16)

    qspec = pl.BlockSpec((None, FWD_BLOCKS * QBLK, LANES), lambda p, bi, i: (bi, i, p))
    kspec = pl.BlockSpec((None, sp, LANES), lambda p, bi, i: (bi, 0, p))
    return pl.pallas_call(
        body,
        name=name,
        grid=(w // LANES, b, s // (FWD_BLOCKS * QBLK)),
        in_specs=[qspec, kspec, kspec, pl.BlockSpec((2, QBLK, WIN), lambda p, bi, i: (p, 0, 0))] + [ANY] * len(deps),
        out_specs=qspec,
        out_shape=jax.ShapeDtypeStruct((b, s, w), BF16),
        compiler_params=_params("parallel", "parallel", "arbitrary"),
    )(q, k, v, table, *deps)


def _attn_bwd(name, q, k, v, table, dmix):
    b, s, w = q.shape
    sp = k.shape[1]

    def body(q_ref, k_ref, v_ref, t_ref, do_ref, dq_ref, dk_ref, dv_ref, dbe_ref, dbo_ref):
        bi = pl.program_id(1)
        i = pl.program_id(2)

        @pl.when(i == 0)
        def _():
            dk_ref[...] = jnp.zeros_like(dk_ref)
            dv_ref[...] = jnp.zeros_like(dv_ref)

        @pl.when(jnp.logical_and(i == 0, bi == 0))
        def _():
            dbe_ref[...] = jnp.zeros_like(dbe_ref)
            dbo_ref[...] = jnp.zeros_like(dbo_ref)

        lanes = [_head_lanes(hh) for hh in range(2)]

        def scores(j):
            start = pl.multiple_of((i * BWD_BLOCKS + j) * QBLK, QBLK)
            win = pl.ds(start, WIN)
            kw = k_ref[win, :]
            vw = v_ref[win, :]
            q2 = q_ref[j * QBLK:(j + 1) * QBLK, :]
            do2 = do_ref[j * QBLK:(j + 1) * QBLK, :].astype(BF16)
            qh = [jnp.where(mine, q2, jnp.zeros_like(q2)) for mine in lanes]
            doh = [jnp.where(mine, do2, jnp.zeros_like(do2)) for mine in lanes]
            p = [_attn_probs(qh[hh], kw, t_ref[hh], start) for hh in range(2)]
            dp = [_dot(doh[hh], vw, NT) for hh in range(2)]
            return win, kw, qh, doh, p, dp

        def gradients(j, win, kw, qh, doh, p, dp):
            ds = [p[hh] * (dp[hh] - jnp.sum(p[hh] * dp[hh], axis=-1, keepdims=True)) for hh in range(2)]
            dsb = [(x * (ATTN_DH ** -0.5)).astype(BF16) for x in ds]
            pb = [x.astype(BF16) for x in p]
            dq = [_dot(dsb[hh], kw) for hh in range(2)]
            dk = [_dot(dsb[hh], qh[hh], TN) for hh in range(2)]
            dv = [_dot(pb[hh], doh[hh], TN) for hh in range(2)]
            for hh in range(2):
                for qi in range(Q_CHUNKS):
                    c0 = (qi // 2) * LANES
                    blk = ds[hh][qi * CHUNK:(qi + 1) * CHUNK, c0:c0 + DB_W]
                    if qi % 2 == 0:
                        dbe_ref[hh] += blk
                    else:
                        dbo_ref[hh] += blk
            dq_ref[j * QBLK:(j + 1) * QBLK, :] = jnp.where(lanes[0], dq[0], dq[1])
            dk_ref[win, :] += dk[0] + dk[1]
            dv_ref[win, :] += dv[0] + dv[1]

        staged = scores(0)
        for j in range(BWD_BLOCKS):
            upcoming = scores(j + 1) if j + 1 < BWD_BLOCKS else None
            gradients(j, *staged)
            staged = upcoming

    qspec = pl.BlockSpec((None, BWD_BLOCKS * QBLK, LANES), lambda p, bi, i: (bi, i, p))
    kspec = pl.BlockSpec((None, sp, LANES), lambda p, bi, i: (bi, 0, p))
    dbspec = pl.BlockSpec((2, CHUNK, DB_W), lambda p, bi, i: (p, 0, 0))
    db_shape = jax.ShapeDtypeStruct((ATTN_HEADS, CHUNK, DB_W), F32)
    return pl.pallas_call(
        body,
        name=name,
        grid=(w // LANES, b, s // (BWD_BLOCKS * QBLK)),
        in_specs=[qspec, kspec, kspec, pl.BlockSpec((2, QBLK, WIN), lambda p, bi, i: (p, 0, 0)), qspec],
        out_specs=[qspec, kspec, kspec, dbspec, dbspec],
        out_shape=[jax.ShapeDtypeStruct((b, s, w), F32), jax.ShapeDtypeStruct((b, sp, w), F32),
                   jax.ShapeDtypeStruct((b, sp, w), F32), db_shape, db_shape],
        compiler_params=_params("arbitrary", "arbitrary", "arbitrary"),
    )(q, k, v, table, dmix)


HQ_COL = 3 * ATTN_W // HGRN_DH
HF_COL = HQ_COL + HGRN_HEADS
HI_COL = HF_COL + HGRN_HEADS
HG_COL = HI_COL + HGRN_HEADS
HGRN_ROWS = 8 * CHUNK
HGRN_UNROLL = 8
HEAD_LANES = [slice(hh * HGRN_DH, (hh + 1) * HGRN_DH) for hh in range(HGRN_HEADS)]


def _tri(lower):
    r = lax.broadcasted_iota(jnp.int32, (CHUNK, CHUNK), 0)
    c = lax.broadcasted_iota(jnp.int32, (CHUNK, CHUNK), 1)
    return (r >= c) if lower else (r <= c)


def _hgrn_chunk(hq, hf, lb, tril):
    sig = _sigmoid(hf)
    f = lb + (1.0 - lb) * sig
    g = jnp.log(f)
    ones_l = jnp.where(tril, 1.0, 0.0).astype(BF16)
    b = _dot_exact_lhs(ones_l, g)
    bl = jnp.sum(g, axis=0, keepdims=True)
    rows = lax.broadcasted_iota(jnp.int32, g.shape, 0)
    bm = jnp.sum(jnp.where(rows <= CHUNK // 2, g, 0.0), axis=0, keepdims=True)
    sq = _sigmoid(hq)
    q = hq * sq
    k = 1.0 - f
    return sig, f, b, bl, bm, sq, q, k


def _hgrn_fwd(name, proj, attn, lb, go, b, s):
    nc = s // CHUNK
    t = b * s
    nblk = s // HGRN_ROWS
    cpb = HGRN_ROWS // CHUNK

    def body(hq_ref, hf_ref, hi_ref, hg_ref, attn_ref, lb_ref, go_ref, mix_ref, oraw_ref, st_ref, s_scr):
        tril = _tri(True)
        gov = go_ref[...]
        mix_ref[:, 0:ATTN_W] = attn_ref[...]

        @pl.when(pl.program_id(1) == 0)
        def _():
            s_scr[...] = jnp.zeros_like(s_scr)

        def step(c, carry):
            sl = pl.ds(pl.multiple_of(c * CHUNK, CHUNK), CHUNK)
            hg = hg_ref[sl, :]
            _, _, bb, bl, bm, _, q, k = _hgrn_chunk(hq_ref[sl, :], hf_ref[sl, :], lb_ref[...], tril)
            vb = hi_ref[sl, :].astype(BF16)
            qe = (q * jnp.exp(bb - bm)).astype(BF16)
            ke = (k * jnp.exp(bm - bb)).astype(BF16)
            qb = (q * jnp.exp(bb)).astype(BF16)
            kb = (k * jnp.exp(bl - bb)).astype(BF16)
            e_last = jnp.exp(bl)
            gate = _silu(hg)
            st = [s_scr[hh] for hh in range(HGRN_HEADS)]
            a = [jnp.where(tril, _dot(qe[:, hs], ke[:, hs], NT), 0.0).astype(BF16) for hs in HEAD_LANES]
            o_state = [_dot(qb[:, hs], st[hh].astype(BF16), NT) for hh, hs in enumerate(HEAD_LANES)]
            st_next = [st[hh] * e_last[:, hs] + _dot(vb[:, hs], kb[:, hs], TN) for hh, hs in enumerate(HEAD_LANES)]
            o = [_dot(a[hh], vb[:, hs]) + o_state[hh] for hh, hs in enumerate(HEAD_LANES)]
            ro = [(oh * lax.rsqrt(jnp.mean(oh * oh, axis=-1, keepdims=True) + RMS_EPS) * gov) * gate[:, hs]
                  for oh, hs in zip(o, HEAD_LANES)]
            for hh in range(HGRN_HEADS):
                st_ref[hh, c] = st[hh]
                s_scr[hh] = st_next[hh]
            mix_ref[sl, ATTN_W:ATTN_W + HGRN_W] = jnp.concatenate(ro, axis=1).astype(BF16)
            oraw_ref[sl, :] = jnp.concatenate(o, axis=1)
            return carry

        lax.fori_loop(0, cpb, step, 0, unroll=HGRN_UNROLL)

    col = lambda base: pl.BlockSpec((HGRN_ROWS, HGRN_W), lambda bi, i: (bi * nblk + i, base // HGRN_HEADS))
    out = pl.BlockSpec((HGRN_ROWS, HGRN_W), lambda bi, i: (bi * nblk + i, 0))
    return pl.pallas_call(
        body,
        name=name,
        grid=(b, nblk),
        in_specs=[col(HQ_COL), col(HF_COL), col(HI_COL), col(HG_COL), out,
                  pl.BlockSpec((1, HGRN_W), lambda bi, i: (0, 0)), pl.BlockSpec((1, HGRN_DH), lambda bi, i: (0, 0))],
        out_specs=[pl.BlockSpec((HGRN_ROWS, ATTN_W + HGRN_W), lambda bi, i: (bi * nblk + i, 0)), out,
                   pl.BlockSpec((None, HGRN_HEADS, cpb, HGRN_DH, HGRN_DH), lambda bi, i: (bi, 0, i, 0, 0))],
        out_shape=[jax.ShapeDtypeStruct((t, ATTN_W + HGRN_W), BF16), jax.ShapeDtypeStruct((t, HGRN_W), F32),
                   jax.ShapeDtypeStruct((b, HGRN_HEADS, nc, HGRN_DH, HGRN_DH), F32)],
        scratch_shapes=[pltpu.VMEM((HGRN_HEADS, HGRN_DH, HGRN_DH), F32)],
        compiler_params=_params("parallel", "arbitrary"),
    )(proj, proj, proj, proj, attn, lb, go)


def _hgrn_bwd(name, proj, dqkv, lb, go, oraw, states, dmix, b, s):
    t = b * s
    nblk = s // HGRN_ROWS
    cpb = HGRN_ROWS // CHUNK

    def body(hq_ref, hf_ref, hi_ref, hg_ref, dq_ref, dk_ref, dv_ref, lb_ref, go_ref, oraw_ref, st_ref, dro_ref,
             dp_ref, dlb_ref, dgo_ref, ds_scr, dlb_scr, dgo_scr):
        tril = _tri(True)
        ones_u = jnp.where(_tri(False), 1.0, 0.0).astype(BF16)
        gov = go_ref[...]
        dp_ref[:, 0:ATTN_W] = dq_ref[...]
        dp_ref[:, ATTN_W:2 * ATTN_W] = dk_ref[...]
        dp_ref[:, 2 * ATTN_W:3 * ATTN_W] = dv_ref[...]

        @pl.when(pl.program_id(1) == 0)
        def _():
            ds_scr[...] = jnp.zeros_like(ds_scr)
            dlb_scr[...] = jnp.zeros_like(dlb_scr)
            dgo_scr[...] = jnp.zeros_like(dgo_scr)

        def step(ci, carry):
            c = cpb - 1 - ci
            sl = pl.ds(pl.multiple_of(c * CHUNK, CHUNK), CHUNK)
            hq = hq_ref[sl, :]
            hg = hg_ref[sl, :]
            sig, f, bb, bl, bm, sq, q, k = _hgrn_chunk(hq, hf_ref[sl, :], lb_ref[...], tril)
            vb = hi_ref[sl, :].astype(BF16)
            ebm = jnp.exp(bb - bm)
            embm = jnp.exp(bm - bb)
            eb = jnp.exp(bb)
            ebl = jnp.exp(bl - bb)
            e_last = jnp.exp(bl)
            qe = (q * ebm).astype(BF16)
            ke = (k * embm).astype(BF16)
            qb = (q * eb).astype(BF16)
            kb = (k * ebl).astype(BF16)
            st = [st_ref[hh, c] for hh in range(HGRN_HEADS)]
            dst = [ds_scr[hh] for hh in range(HGRN_HEADS)]
            o = oraw_ref[sl, :]
            dro = dro_ref[sl, :]
            sg = _sigmoid(hg)
            gov4 = jnp.concatenate([gov] * HGRN_HEADS, axis=1)
            rstd = jnp.concatenate(
                [jnp.broadcast_to(lax.rsqrt(jnp.mean(o[:, hs] * o[:, hs], axis=-1, keepdims=True) + RMS_EPS),
                                  (CHUNK, HGRN_DH)) for hs in HEAD_LANES], axis=1)
            ohat = o * rstd
            dn = dro * (hg * sg)
            dhg = dro * (ohat * gov4) * (sg * (1.0 + hg * (1.0 - sg)))
            dgo_inc = jnp.sum(dn * ohat, axis=0, keepdims=True)
            dohat = dn * gov4
            proj_h = dohat * ohat
            pm = jnp.concatenate(
                [jnp.broadcast_to(jnp.mean(proj_h[:, hs], axis=-1, keepdims=True), (CHUNK, HGRN_DH))
                 for hs in HEAD_LANES], axis=1)
            dob = (rstd * (dohat - ohat * pm)).astype(BF16)
            stb = [x.astype(BF16) for x in st]
            dstb = [x.astype(BF16) for x in dst]
            a = [jnp.where(tril, _dot(qe[:, hs], ke[:, hs], NT), 0.0).astype(BF16) for hs in HEAD_LANES]
            dab = [jnp.where(tril, _dot(dob[:, hs], vb[:, hs], NT), 0.0).astype(BF16) for hs in HEAD_LANES]
            dqb = [_dot(dob[:, hs], stb[hh]) for hh, hs in enumerate(HEAD_LANES)]
            dkb = [_dot(vb[:, hs], dstb[hh]) for hh, hs in enumerate(HEAD_LANES)]
            dv_state = [_dot(kb[:, hs], dstb[hh], NT) for hh, hs in enumerate(HEAD_LANES)]
            dst_next = [dst[hh] * e_last[:, hs] + _dot(dob[:, hs], qb[:, hs], TN) for hh, hs in enumerate(HEAD_LANES)]
            dv = [_dot(a[hh], dob[:, hs], TN) + dv_state[hh] for hh, hs in enumerate(HEAD_LANES)]
            dqe = jnp.concatenate([_dot(dab[hh], ke[:, hs]) for hh, hs in enumerate(HEAD_LANES)], axis=1)
            dke = jnp.concatenate([_dot(dab[hh], qe[:, hs], TN) for hh, hs in enumerate(HEAD_LANES)], axis=1)
            dqb = jnp.concatenate(dqb, axis=1)
            dkb = jnp.concatenate(dkb, axis=1)
            state_term = jnp.concatenate(
                [jnp.sum(dst[hh] * st[hh], axis=0, keepdims=True) for hh in range(HGRN_HEADS)], axis=1)
            dq = dqe * ebm + dqb * eb
            dk = dke * embm + dkb * ebl
            db = (qe.astype(F32) * dqe - ke.astype(F32) * dke) + q * (dqb * eb) - k * (dkb * ebl)
            d_last = jnp.sum(k * ebl * dkb, axis=0, keepdims=True) + state_term * e_last
            dg = _dot_exact_lhs(ones_u, db) + d_last
            df = dg / f - dk
            first = HQ_COL * HGRN_DH
            dp_ref[sl, first:first + HGRN_W] = (dq * (sq * (1.0 + hq * (1.0 - sq)))).astype(BF16)
            dp_ref[sl, first + HGRN_W:first + 2 * HGRN_W] = (df * (1.0 - lb_ref[...]) * sig * (1.0 - sig)).astype(BF16)
            dp_ref[sl, first + 2 * HGRN_W:first + 3 * HGRN_W] = jnp.concatenate(dv, axis=1).astype(BF16)
            dp_ref[sl, first + 3 * HGRN_W:first + 4 * HGRN_W] = dhg.astype(BF16)
            dlb_scr[...] += jnp.sum(df * (1.0 - sig), axis=0, keepdims=True)
            dgo_scr[...] += dgo_inc
            for hh in range(HGRN_HEADS):
                ds_scr[hh] = dst_next[hh]
            return carry

        lax.fori_loop(0, cpb, step, 0, unroll=HGRN_UNROLL)

        @pl.when(pl.program_id(1) == nblk - 1)
        def _():
            dlb_ref[...] = dlb_scr[...]
            dgo_ref[...] = dgo_scr[...]

    rows = lambda bi, i: bi * nblk + (nblk - 1 - i)
    col = lambda base: pl.BlockSpec((HGRN_ROWS, HGRN_W), lambda bi, i: (rows(bi, i), base // HGRN_HEADS))
    out = pl.BlockSpec((HGRN_ROWS, HGRN_W), lambda bi, i: (rows(bi, i), 0))
    part = pl.BlockSpec((None, 1, HGRN_W), lambda bi, i: (bi, 0, 0))
    width = HG_COL * HGRN_DH + HGRN_W
    o_shape = jax.ShapeDtypeStruct((t, width), BF16)
    p_shape = jax.ShapeDtypeStruct((b, 1, HGRN_W), F32)
    return pl.pallas_call(
        body,
        name=name,
        grid=(b, nblk),
        in_specs=[col(HQ_COL), col(HF_COL), col(HI_COL), col(HG_COL), out, out, out,
                  pl.BlockSpec((1, HGRN_W), lambda bi, i: (0, 0)), pl.BlockSpec((1, HGRN_DH), lambda bi, i: (0, 0)), out,
                  pl.BlockSpec((None, HGRN_HEADS, cpb, HGRN_DH, HGRN_DH), lambda bi, i: (bi, 0, nblk - 1 - i, 0, 0)),
                  col(ATTN_W // HGRN_DH)],
        out_specs=[pl.BlockSpec((HGRN_ROWS, width), lambda bi, i: (rows(bi, i), 0))] + [part] * 2,
        out_shape=[o_shape] + [p_shape] * 2,
        scratch_shapes=[pltpu.VMEM((HGRN_HEADS, HGRN_DH, HGRN_DH), F32), pltpu.VMEM((1, HGRN_W), F32),
                        pltpu.VMEM((1, HGRN_W), F32)],
        compiler_params=_params("parallel", "arbitrary"),
    )(proj, proj, proj, proj, *dqkv, lb, go, oraw, states, dmix)


def _small_grads(name, dg1, dgm, dg2, dgq, dgk, dbe_t, dbo_t, dlb, dgo, lbp):
    d = dg1.shape[1]

    def body(dg1_ref, dgm_ref, dg2_ref, dgq_ref, dgk_ref, dbe_ref, dbo_ref, dlb_ref, dgo_ref, lbp_ref,
             g1_ref, gm_ref, g2_ref, gq_ref, gk_ref, rb_ref, lbg_ref, go_ref):
        g1_ref[...] = jnp.sum(dg1_ref[...], axis=0, keepdims=True)
        gm_ref[...] = jnp.sum(dgm_ref[...], axis=0, keepdims=True)
        g2_ref[...] = jnp.sum(dg2_ref[...], axis=0, keepdims=True)
        r = lax.broadcasted_iota(jnp.int32, (ATTN_W, ATTN_DH), 0)
        cidx = lax.broadcasted_iota(jnp.int32, (ATTN_W, ATTN_DH), 1)
        fold = jnp.where(jnp.bitwise_and(r, ATTN_DH - 1) == cidx, 1.0, 0.0).astype(BF16)
        gq_ref[...] = jnp.sum(_dot_exact_rhs(dgq_ref[...], fold), axis=0, keepdims=True)
        gk_ref[...] = jnp.sum(_dot_exact_rhs(dgk_ref[...], fold), axis=0, keepdims=True)
        gosum = jnp.sum(dgo_ref[...], axis=0, keepdims=True)
        go_ref[...] = (gosum[:, 0:HGRN_DH] + gosum[:, HGRN_DH:2 * HGRN_DH]
                       + gosum[:, 2 * HGRN_DH:3 * HGRN_DH] + gosum[:, 3 * HGRN_DH:4 * HGRN_DH])
        p0 = lbp_ref[0:1, :]
        p1 = lbp_ref[1:2, :]
        lbv = 1.0 / (1.0 + jnp.exp(p1 - p0))
        dp0 = jnp.sum(dlb_ref[...], axis=0, keepdims=True) * lbv * (1.0 - lbv)
        lbg_ref[0:1, :] = dp0
        lbg_ref[1:2, :] = -dp0
        acc = dbe_ref[CHUNK - 1] + pltpu.roll(dbo_ref[CHUNK - 1], DB_W - CHUNK, 1)
        for tq in range(CHUNK - 1):
            acc = acc + pltpu.roll(dbe_ref[tq], CHUNK - 1 - tq, 1) + pltpu.roll(dbo_ref[tq], DB_W - 1 - tq, 1)
        jidx = lax.broadcasted_iota(jnp.int32, (DB_W, N_REL_PAD), 0)
        ridx = lax.broadcasted_iota(jnp.int32, (DB_W, N_REL_PAD), 1)
        rel = jnp.clip(KPAD + CHUNK - 1 - jidx, -REL_CLIP, REL_CLIP) + REL_CLIP
        rb_ref[...] = _dot_exact_rhs(acc, jnp.where(rel == ridx, 1.0, 0.0).astype(BF16))

    ins = [dg1, dgm, dg2, dgq, dgk, dbe_t, dbo_t, dlb, dgo, lbp]
    outs = [jax.ShapeDtypeStruct((1, d), F32)] * 3 + [jax.ShapeDtypeStruct((1, ATTN_DH), F32)] * 2 + [
        jax.ShapeDtypeStruct((ATTN_HEADS, N_REL_PAD), F32), jax.ShapeDtypeStruct((2, HGRN_W), F32),
        jax.ShapeDtypeStruct((1, HGRN_DH), F32)]
    vm = pl.BlockSpec(memory_space=pltpu.VMEM)
    return pl.pallas_call(
        body,
        name=name,
        in_specs=[vm] * len(ins),
        out_specs=[vm] * len(outs),
        out_shape=outs,
        compiler_params=pltpu.CompilerParams(vmem_limit_bytes=VMEM_LIMIT),
    )(*ins)


def _adam_update(w, g, m, v):
    m2 = ADAM_B1 * m + (1.0 - ADAM_B1) * g
    v2 = ADAM_B2 * v + (1.0 - ADAM_B2) * (g * g)
    m_hat = m2 / (1.0 - ADAM_B1 ** ADAM_STEP)
    v_hat = v2 / (1.0 - ADAM_B2 ** ADAM_STEP)
    delta = -ADAM_LR * (m_hat / (jnp.sqrt(v_hat) + ADAM_EPS) + ADAM_WD * w)
    return delta, m2, v2


def _rows_tile(r):
    return r if r <= 512 or r % 512 else 512


def _pair_sum(name, grad, theirs, core):
    n, half, c = theirs.shape
    tr = _rows_tile(half)
    nth = half // tr

    def body(core_ref, a_ref, b_ref, o_ref):
        o_ref[...] = (a_ref[...].astype(F32) + b_ref[...].astype(F32)).astype(o_ref.dtype)

    spec = pl.BlockSpec((None, tr, c), lambda i, j, core_ref: (i, j, 0))
    return pl.pallas_call(
        body, name=name,
        grid_spec=pltpu.PrefetchScalarGridSpec(
            num_scalar_prefetch=1, grid=(n, nth),
            in_specs=[pl.BlockSpec((None, tr, c), lambda i, j, core_ref: (i, core_ref[0] * nth + j, 0)), spec],
            out_specs=spec),
        out_shape=pltpu.HBM((n, half, c), BF16), compiler_params=_params("parallel", "parallel"),
    )(core, grad, theirs)


def _chip_sum(name, own, parts, chip):
    _, half, c = own.shape
    tr = _rows_tile(half)

    def body(chip_ref, own_ref, p_ref, o_ref):
        me = chip_ref[0]
        mine = own_ref[...].astype(F32)
        flip_x, flip_y, flip_xy = (p_ref[i].astype(F32) for i in range(3))
        acc = None
        for k in range(N_CHIPS):
            rel = jnp.bitwise_xor(me, k)
            term = jnp.where(rel == 0, mine, jnp.where(rel == 2, flip_x, jnp.where(rel == 1, flip_y, flip_xy)))
            acc = term if acc is None else acc + term
        o_ref[...] = acc

    return pl.pallas_call(
        body, name=name,
        grid_spec=pltpu.PrefetchScalarGridSpec(
            num_scalar_prefetch=1, grid=(half // tr,),
            in_specs=[pl.BlockSpec((None, tr, c), lambda j, chip_ref: (chip_ref[0], j, 0)),
                      pl.BlockSpec((3, tr, c), lambda j, chip_ref: (0, j, 0))],
            out_specs=pl.BlockSpec((tr, c), lambda j, chip_ref: (j, 0))),
        out_shape=pltpu.HBM((half, c), F32), compiler_params=_params("parallel"),
    )(chip, own, parts)


def _adamw(name, w, g_mine, g_theirs, m, v, core):
    _, r, c = w.shape
    half = r // 2
    tr = _rows_tile(half)
    nth = half // tr

    def body(core_ref, w_ref, gm_ref, gt_ref, m_ref, v_ref, g_ref, d_ref, m2_ref, v2_ref):
        g = jnp.where(pl.program_id(0) == core_ref[0], gm_ref[...], gt_ref[...])
        delta, m2, v2 = _adam_update(w_ref[...], g, m_ref[...], v_ref[...])
        g_ref[...] = g
        d_ref[...] = delta
        m2_ref[...] = m2
        v2_ref[...] = v2

    full = pl.BlockSpec((None, tr, c), lambda h, j, core_ref: (0, h * nth + j, 0))
    part = pl.BlockSpec((tr, c), lambda h, j, core_ref: (j, 0))
    shape = jax.ShapeDtypeStruct((1, r, c), F32)
    return pl.pallas_call(
        body, name=name,
        grid_spec=pltpu.PrefetchScalarGridSpec(
            num_scalar_prefetch=1, grid=(2, nth), in_specs=[full, part, part, full, full], out_specs=[full] * 4),
        out_shape=[shape] * 4, compiler_params=_params("parallel", "parallel"),
    )(core, w, g_mine, g_theirs, m, v)


def _rel_bias_table(name, rel_bias):
    padded = jnp.pad(rel_bias, ((0, 0), (0, N_REL_PAD - N_REL)))

    def body(rb_ref, o_ref):
        ridx = lax.broadcasted_iota(jnp.int32, (N_REL_PAD, BAND), 0)
        sidx = lax.broadcasted_iota(jnp.int32, (N_REL_PAD, BAND), 1)
        rb = rb_ref[...]

        def step(tq, carry):
            rel = jnp.clip(tq + KPAD - sidx, -REL_CLIP, REL_CLIP) + REL_CLIP
            onehot = jnp.where(rel == ridx, 1.0, 0.0).astype(BF16)
            o_ref[tq] = _dot_exact_rhs(rb, onehot)
            return carry

        lax.fori_loop(0, CHUNK, step, 0)

    vm = pl.BlockSpec(memory_space=pltpu.VMEM)
    table = pl.pallas_call(
        body, name=name, in_specs=[vm], out_specs=vm,
        out_shape=jax.ShapeDtypeStruct((CHUNK, ATTN_HEADS, BAND), F32),
    )(padded)
    return table.transpose(1, 0, 2)


def _adamw_small(name, w, parts, m, v):
    def body(w_ref, p_ref, m_ref, v_ref, g_ref, d_ref, m2_ref, v2_ref):
        g = p_ref[0]
        for i in range(1, N_DEV):
            g = g + p_ref[i]
        delta, m2, v2 = _adam_update(w_ref[...], g, m_ref[...], v_ref[...])
        g_ref[...] = g
        d_ref[...] = delta
        m2_ref[...] = m2
        v2_ref[...] = v2

    vm = pl.BlockSpec(memory_space=pltpu.VMEM)
    shape = jax.ShapeDtypeStruct((SMALL_ROWS, SMALL_COLS), F32)
    return pl.pallas_call(
        body, name=name, in_specs=[vm] * 4, out_specs=[vm] * 4, out_shape=[shape] * 4,
    )(w, parts, m, v)


def _position():
    return lax.axis_index("x"), lax.axis_index("y"), lax.axis_index("c")


def _other_chips(x, y):
    return [(1 - x, y), (x, 1 - y), (1 - x, 1 - y)]


ANY = pl.BlockSpec(memory_space=pl.ANY)
PAIR_ID = 0


def _pair_handshake():
    x, y, c = _position()
    barrier = pltpu.get_barrier_semaphore()
    pl.semaphore_signal(barrier, inc=1, device_id=(x, y, 1 - c), device_id_type=MESH)
    pl.semaphore_wait(barrier, 1)


PAIR_CALL = pltpu.CompilerParams(collective_id=PAIR_ID)


HBM = pl.BlockSpec(memory_space=pltpu.HBM)
SEM = pl.BlockSpec(memory_space=pltpu.SEMAPHORE)
SPLIT_COPY = pltpu.SideEffectType.DATAFLOW_SIDE_EFFECTING


def _gather_copy(shards, outs, send_sem, recv_sem, i, j):
    x, y, c = _position()
    chips = _other_chips(x, y)
    half = shards[i].shape[0] // 2
    rows = pl.ds(pl.multiple_of(c * half, 16), half)
    return pltpu.make_async_remote_copy(
        src_ref=shards[i].at[rows, :], dst_ref=outs[i].at[2 * x + y, rows, :],
        send_sem=send_sem.at[3 * i + j], recv_sem=recv_sem.at[3 * i + j],
        device_id=(chips[j][0], chips[j][1], c), device_id_type=MESH)


def _gather_start(name, shards, after):
    n = len(shards)

    def body(*refs):
        srcs, outs = refs[:n], refs[n:2 * n]
        send_sem, recv_sem = refs[2 * n + len(after)], refs[2 * n + len(after) + 1]
        token = refs[-1]
        for i in range(n):
            for j in range(3):
                _gather_copy(srcs, outs, send_sem, recv_sem, i, j).start()
        token[...] = jnp.zeros_like(token)

    full = [(N_CHIPS,) + s.shape for s in shards]
    res = pl.pallas_call(
        body,
        name=name,
        in_specs=[HBM] * (2 * n) + [ANY] * len(after),
        out_specs=[SEM, SEM] + [HBM] * (2 * n) + [pl.BlockSpec(memory_space=pltpu.VMEM)],
        out_shape=[pltpu.SemaphoreType.DMA((3 * n,)), pltpu.SemaphoreType.DMA((3 * n,))]
        + [pltpu.HBM(s.shape, s.dtype) for s in shards]
        + [pltpu.HBM(shp, s.dtype) for shp, s in zip(full, shards)]
        + [jax.ShapeDtypeStruct((8, LANES), F32)],
        input_output_aliases={i: 2 + i for i in range(2 * n)},
        compiler_params=pltpu.CompilerParams(has_side_effects=SPLIT_COPY),
    )(*[pltpu.with_memory_space_constraint(s, pltpu.HBM) for s in shards],
      *[pltpu.with_memory_space_constraint(lax.empty(shp, s.dtype), pltpu.HBM) for shp, s in zip(full, shards)],
      *after)
    return res[0], res[1], list(res[2:2 + n]), list(res[2 + n:2 + 2 * n]), res[-1]


def _gather_wait(name, send_sem, recv_sem, shards, outs, after):
    n = len(shards)

    def body(*refs):
        srcs, out_refs = refs[:n], refs[n:2 * n]
        send_ref, recv_ref = refs[2 * n], refs[2 * n + 1]
        for i in range(n):
            for j in range(3):
                copy = _gather_copy(srcs, out_refs, send_ref, recv_ref, i, j)
                copy.wait_send()
                copy.wait_recv()

    res = pl.pallas_call(
        body,
        name=name,
        in_specs=[HBM] * (2 * n) + [SEM, SEM] + [ANY] * len(after),
        out_specs=[HBM] * (2 * n),
        out_shape=[pltpu.HBM(s.shape, s.dtype) for s in shards] + [pltpu.HBM(o.shape, o.dtype) for o in outs],
        input_output_aliases={i: i for i in range(2 * n)},
        compiler_params=pltpu.CompilerParams(has_side_effects=SPLIT_COPY),
    )(*shards, *outs, send_sem, recv_sem, *after)
    return list(res[:n]), list(res[n:])


def _join_copies(srcs, ins, outs, own_send, own_recv, half_send, half_recv):
    x, y, c = _position()
    chips = _other_chips(x, y)
    copies = []
    for i in range(len(srcs)):
        copies.append(pltpu.make_async_remote_copy(
            src_ref=srcs[i], dst_ref=outs[i].at[2 * x + y], send_sem=own_send.at[i], recv_sem=own_recv.at[i],
            device_id=(x, y, 1 - c), device_id_type=MESH))
        half = srcs[i].shape[0] // 2
        rows = pl.ds(pl.multiple_of(c * half, 16), half)
        for j in range(3):
            slot = 2 * chips[j][0] + chips[j][1]
            copies.append(pltpu.make_async_remote_copy(
                src_ref=ins[i].at[slot, rows, :], dst_ref=outs[i].at[slot, rows, :],
                send_sem=half_send.at[3 * i + j], recv_sem=half_recv.at[3 * i + j],
                device_id=(x, y, 1 - c), device_id_type=MESH))
    return copies


def _gather_join(name, shards, outs):
    n = len(shards)

    def body(*refs):
        _pair_handshake()
        copies = _join_copies(refs[:n], refs[n:2 * n], refs[2 * n:3 * n], *refs[3 * n:])
        for cp in copies:
            cp.start()
        for cp in copies:
            cp.wait()

    return pl.pallas_call(
        body,
        name=name,
        in_specs=[ANY] * (2 * n),
        out_specs=[HBM] * n,
        out_shape=[pltpu.HBM(o.shape, o.dtype) for o in outs],
        input_output_aliases={n + i: i for i in range(n)},
        scratch_shapes=[pltpu.SemaphoreType.DMA((n,))] * 2 + [pltpu.SemaphoreType.DMA((3 * n,))] * 2,
        compiler_params=PAIR_CALL,
    )(*shards, *outs)


def _join_start(name, shards, outs):
    n = len(shards)

    def body(*refs):
        _pair_handshake()
        srcs, arrs = refs[:n], refs[n:2 * n]
        sems = refs[2 * n:2 * n + 4]
        token = refs[-1]
        for cp in _join_copies(srcs, arrs, arrs, *sems):
            cp.start()
        token[...] = jnp.zeros_like(token)

    res = pl.pallas_call(
        body,
        name=name,
        in_specs=[HBM] * (2 * n),
        out_specs=[SEM] * 4 + [HBM] * (2 * n) + [pl.BlockSpec(memory_space=pltpu.VMEM)],
        out_shape=[pltpu.SemaphoreType.DMA((n,))] * 2 + [pltpu.SemaphoreType.DMA((3 * n,))] * 2
        + [pltpu.HBM(s.shape, s.dtype) for s in shards] + [pltpu.HBM(o.shape, o.dtype) for o in outs]
        + [jax.ShapeDtypeStruct((8, LANES), F32)],
        input_output_aliases={i: 4 + i for i in range(2 * n)},
        compiler_params=pltpu.CompilerParams(has_side_effects=SPLIT_COPY, collective_id=PAIR_ID),
    )(*shards, *outs)
    return list(res[:4]), list(res[4:4 + n]), list(res[4 + n:4 + 2 * n]), res[-1]


def _join_wait(name, sems, shards, outs, after):
    n = len(shards)

    def body(*refs):
        srcs, arrs = refs[:n], refs[n:2 * n]
        for cp in _join_copies(srcs, arrs, arrs, *refs[2 * n:2 * n + 4]):
            cp.wait_send()
            cp.wait_recv()

    res = pl.pallas_call(
        body,
        name=name,
        in_specs=[HBM] * (2 * n) + [SEM] * 4 + [ANY] * len(after),
        out_specs=[HBM] * (2 * n),
        out_shape=[pltpu.HBM(s.shape, s.dtype) for s in shards] + [pltpu.HBM(o.shape, o.dtype) for o in outs],
        input_output_aliases={i: i for i in range(2 * n)},
        compiler_params=pltpu.CompilerParams(has_side_effects=SPLIT_COPY),
    )(*shards, *outs, *sems, *after)
    return list(res[n:])


def _pair_copy(grads, lands, send_sem, recv_sem, i):
    x, y, c = _position()
    half = grads[i].shape[1] // 2
    give = pl.ds(pl.multiple_of((1 - c) * half, 16), half)
    return pltpu.make_async_remote_copy(
        src_ref=grads[i].at[:, give, :], dst_ref=lands[i], send_sem=send_sem.at[i], recv_sem=recv_sem.at[i],
        device_id=(x, y, 1 - c), device_id_type=MESH)


def _pair_start(name, grads):
    n = len(grads)

    def body(*refs):
        _pair_handshake()
        srcs, lands = refs[:n], refs[n:2 * n]
        send_sem, recv_sem = refs[2 * n], refs[2 * n + 1]
        token = refs[-1]
        for i in range(n):
            _pair_copy(srcs, lands, send_sem, recv_sem, i).start()
        token[...] = jnp.zeros_like(token)

    halves = [(g.shape[0], g.shape[1] // 2, g.shape[2]) for g in grads]
    res = pl.pallas_call(
        body,
        name=name,
        in_specs=[HBM] * (2 * n),
        out_specs=[SEM, SEM] + [HBM] * (2 * n) + [pl.BlockSpec(memory_space=pltpu.VMEM)],
        out_shape=[pltpu.SemaphoreType.DMA((n,)), pltpu.SemaphoreType.DMA((n,))]
        + [pltpu.HBM(g.shape, g.dtype) for g in grads]
        + [pltpu.HBM(shp, g.dtype) for shp, g in zip(halves, grads)]
        + [jax.ShapeDtypeStruct((8, LANES), F32)],
        input_output_aliases={i: 2 + i for i in range(2 * n)},
        compiler_params=pltpu.CompilerParams(has_side_effects=SPLIT_COPY, collective_id=PAIR_ID),
    )(*[pltpu.with_memory_space_constraint(g, pltpu.HBM) for g in grads],
      *[pltpu.with_memory_space_constraint(lax.empty(shp, g.dtype), pltpu.HBM) for shp, g in zip(halves, grads)])
    return res[0], res[1], list(res[2:2 + n]), list(res[2 + n:2 + 2 * n]), res[-1]


def _pair_wait(name, send_sem, recv_sem, grads, lands, after):
    n = len(grads)

    def body(*refs):
        srcs, land_refs = refs[:n], refs[n:2 * n]
        send_ref, recv_ref = refs[2 * n], refs[2 * n + 1]
        for i in range(n):
            copy = _pair_copy(srcs, land_refs, send_ref, recv_ref, i)
            copy.wait_send()
            copy.wait_recv()

    res = pl.pallas_call(
        body,
        name=name,
        in_specs=[HBM] * (2 * n) + [SEM, SEM, ANY],
        out_specs=[HBM] * (2 * n),
        out_shape=[pltpu.HBM(g.shape, g.dtype) for g in grads] + [pltpu.HBM(l.shape, l.dtype) for l in lands],
        input_output_aliases={i: i for i in range(2 * n)},
        compiler_params=pltpu.CompilerParams(has_side_effects=SPLIT_COPY),
    )(*grads, *lands, send_sem, recv_sem, after)
    return list(res[:n]), list(res[n:])


def _scatter_copy(srcs, lands, send_sem, recv_sem, i, j):
    x, y, c = _position()
    chips = _other_chips(x, y)
    return pltpu.make_async_remote_copy(
        src_ref=srcs[i].at[2 * chips[j][0] + chips[j][1]], dst_ref=lands[i].at[j],
        send_sem=send_sem.at[3 * i + j], recv_sem=recv_sem.at[3 * i + j],
        device_id=(chips[j][0], chips[j][1], c), device_id_type=MESH)


def _scatter_start(name, sums):
    n = len(sums)

    def body(*refs):
        srcs, lands = refs[:n], refs[n:2 * n]
        send_sem, recv_sem = refs[2 * n], refs[2 * n + 1]
        token = refs[-1]
        for i in range(n):
            for j in range(3):
                _scatter_copy(srcs, lands, send_sem, recv_sem, i, j).start()
        token[...] = jnp.zeros_like(token)

    land_shapes = [(3,) + s.shape[1:] for s in sums]
    res = pl.pallas_call(
        body,
        name=name,
        in_specs=[HBM] * (2 * n),
        out_specs=[SEM, SEM] + [HBM] * (2 * n) + [pl.BlockSpec(memory_space=pltpu.VMEM)],
        out_shape=[pltpu.SemaphoreType.DMA((3 * n,)), pltpu.SemaphoreType.DMA((3 * n,))]
        + [pltpu.HBM(s.shape, s.dtype) for s in sums]
        + [pltpu.HBM(shp, s.dtype) for shp, s in zip(land_shapes, sums)]
        + [jax.ShapeDtypeStruct((8, LANES), F32)],
        input_output_aliases={i: 2 + i for i in range(2 * n)},
        compiler_params=pltpu.CompilerParams(has_side_effects=SPLIT_COPY),
    )(*[pltpu.with_memory_space_constraint(s, pltpu.HBM) for s in sums],
      *[pltpu.with_memory_space_constraint(lax.empty(shp, s.dtype), pltpu.HBM) for shp, s in zip(land_shapes, sums)])
    return res[0], res[1], list(res[2:2 + n]), list(res[2 + n:2 + 2 * n]), res[-1]


def _scatter_wait(name, send_sem, recv_sem, sums, lands, after):
    n = len(sums)

    def body(*refs):
        srcs, land_refs = refs[:n], refs[n:2 * n]
        send_ref, recv_ref = refs[2 * n], refs[2 * n + 1]
        for i in range(n):
            for j in range(3):
                copy = _scatter_copy(srcs, land_refs, send_ref, recv_ref, i, j)
                copy.wait_send()
                copy.wait_recv()

    res = pl.pallas_call(
        body,
        name=name,
        in_specs=[HBM] * (2 * n) + [SEM, SEM, ANY],
        out_specs=[HBM] * (2 * n),
        out_shape=[pltpu.HBM(s.shape, s.dtype) for s in sums] + [pltpu.HBM(l.shape, l.dtype) for l in lands],
        input_output_aliases={i: i for i in range(2 * n)},
        compiler_params=pltpu.CompilerParams(has_side_effects=SPLIT_COPY),
    )(*sums, *lands, send_sem, recv_sem, after)
    return list(res[:n]), list(res[n:])


def _pair_join(name, halves, small=None):
    n = len(halves)
    if small is None:
        def body_plain(*refs):
            _pair_handshake()
            ins, outs = refs[:n], refs[n:2 * n]
            send_sem, recv_sem = refs[2 * n:]
            x, y, c = _position()
            swaps = [pltpu.make_async_remote_copy(
                src_ref=ins[i], dst_ref=outs[i], send_sem=send_sem.at[i], recv_sem=recv_sem.at[i],
                device_id=(x, y, 1 - c), device_id_type=MESH) for i in range(n)]
            for swap in swaps:
                swap.start()
            for swap in swaps:
                swap.wait()

        return pl.pallas_call(
            body_plain,
            name=name,
            in_specs=[ANY] * n,
            out_specs=[ANY] * n,
            out_shape=[jax.ShapeDtypeStruct(h.shape, h.dtype) for h in halves],
            scratch_shapes=[pltpu.SemaphoreType.DMA((n,))] * 2,
            compiler_params=PAIR_CALL,
        )(*halves)

    def body(*refs):
        ins, small_ref = refs[:n], refs[n]
        outs, all_ref = refs[n + 1:2 * n + 1], refs[2 * n + 1]
        send_sem, recv_sem, sm_send, sm_recv, sm_local = refs[2 * n + 2:]
        x, y, c = _position()
        swaps = []
        for i in range(n):
            swap = pltpu.make_async_remote_copy(
                src_ref=ins[i], dst_ref=outs[i], send_sem=send_sem.at[i], recv_sem=recv_sem.at[i],
                device_id=(x, y, 1 - c), device_id_type=MESH)
            swap.start()
            swaps.append(swap)
        me = 4 * x + 2 * y + c
        sm_own = pltpu.make_async_copy(small_ref, all_ref.at[me], sm_local)
        sm_own.start()
        pushes, arrivals = [], []
        for mask in range(1, N_DEV):
            px, py, pc = x ^ (mask >> 2), y ^ ((mask >> 1) & 1), c ^ (mask & 1)
            pushes.append(pltpu.make_async_remote_copy(
                src_ref=small_ref, dst_ref=all_ref.at[me], send_sem=sm_send.at[mask - 1], recv_sem=sm_recv.at[mask - 1],
                device_id=(px, py, pc), device_id_type=MESH))
            arrivals.append(pltpu.make_async_remote_copy(
                src_ref=small_ref, dst_ref=all_ref.at[4 * px + 2 * py + pc], send_sem=sm_send.at[mask - 1],
                recv_sem=sm_recv.at[mask - 1], device_id=(px, py, pc), device_id_type=MESH))
        for cp in pushes:
            cp.start()
        for swap in swaps:
            swap.wait()
        for cp in arrivals:
            cp.wait_recv()
        for cp in pushes:
            cp.wait_send()
        sm_own.wait()

    res = pl.pallas_call(
        body,
        name=name,
        in_specs=[ANY] * (n + 1),
        out_specs=[ANY] * (n + 1),
        out_shape=[jax.ShapeDtypeStruct(h.shape, h.dtype) for h in halves]
        + [jax.ShapeDtypeStruct((N_DEV,) + small.shape, small.dtype)],
        scratch_shapes=[pltpu.SemaphoreType.DMA((n,))] * 2 + [pltpu.SemaphoreType.DMA((N_DEV - 1,))] * 2
        + [pltpu.SemaphoreType.DMA(())],
    )(*halves, small)
    return res[:n], res[n]


def _lower_bound(lbp):
    return jax.nn.softmax(lbp, axis=0)[0:1]


def _local_step(x, target, g1, gm, g2, gq, gk, go, rel_bias, lbp, weights, on_grads, grads_sent):
    b, s, d = x.shape
    t = b * s
    x0 = x.reshape(t, d)
    tgt = target.reshape(t, d)
    gq_t = jnp.tile(gq, (1, ATTN_HEADS))
    gk_t = jnp.tile(gk, (1, ATTN_HEADS))
    lb = _lower_bound(lbp)
    table = _band_table(_rel_bias_table("rel_bias_table", rel_bias))

    h1 = _rmsnorm_fwd("norm1", x0, g1)
    wg1, wu1, deps1 = weights["first"]((h1, table))
    a1, b1, z1 = _ffn_up("ffn1_up", h1, wg1, wu1, deps1)
    wd1, deps_mid = weights["mid"]((z1,))
    x1, h2 = _ffn_down("ffn1_down", z1, wd1, x0, gm, deps_mid)
    w_in, w_out = weights["mid_rest"]((x1,))
    ns = w_in.shape[0]
    proj = _in_proj("in_proj", h2, w_in)
    proj3 = proj.reshape(b, s, proj.shape[1])
    qn, kn, vb = _qk_prep("qk_prep", proj3, gq_t, gk_t)
    attn = _attn_fwd("attn_fwd", qn, kn, vb, table, weights["last_begin"]((qn,))).reshape(t, ATTN_W)
    mix, oraw, states = _hgrn_fwd("hgrn_fwd", proj, attn, lb, go, b, s)
    x2, h3 = _out_proj("out_proj", mix, w_out, x1, g2)
    wg2, wu2, wd2 = weights["last"]((h3,))
    a2, b2, z2 = _ffn_up("ffn2_up", h3, wg2, wu2)
    dy, dyh, sq = _ffn_down_loss("ffn2_down_loss", z2, wd2, x2, tgt)
    loss = 0.5 * jnp.sum(sq) / d

    da2, db2 = _ffn_bwd_act("ffn2_bwd_act", dyh, wd2, a2, b2)
    dwd2 = _grad_w_cols("ffn2_dwd", z2, dyh)
    dwg2 = _grad_w_cols("ffn2_dwg", da2, h3)
    dwu2 = _grad_w_cols("ffn2_dwu", db2, h3)
    sent2 = on_grads("ffn2", {"ffn2_w_gate": dwg2, "ffn2_w_up": dwu2, "ffn2_w_down": dwd2})
    dx2, dx2b, dg2 = _ffn_bwd_in("ffn2_bwd_in", da2, db2, wg2, wu2, x2, g2, dy, 1.0, sent2)
    sent2 = grads_sent("ffn2", dx2b)

    dwout = _grad_w_out("dw_out", mix, dx2b)
    dmix = _out_proj_bwd("out_proj_bwd", dx2b, w_out, sent2)
    dqn, dkn, dvn, dbe, dbo = _attn_bwd("attn_bwd", qn, kn, vb, table, dmix.reshape(b, s, dmix.shape[1]))
    dpq, dpk, dpv, dgq, dgk = _qk_prep_bwd("qk_prep_bwd", proj3, dqn, dkn, dvn, gq_t, gk_t)
    dpq, dpk, dpv = (a.reshape(t, ATTN_W) for a in (dpq, dpk, dpv))
    dproj, dlb, dgo = _hgrn_bwd("hgrn_bwd", proj, (dpq, dpk, dpv), lb, go, oraw, states, dmix, b, s)
    dwin = _grad_w_in("dw_in", h2, dproj, ns)
    dx1, dx1h, dgm = _in_proj_bwd("in_proj_bwd", dproj, w_in, x1, gm, dx2, 0.5)

    dwd1 = _grad_w_cols("ffn1_dwd", z1, dx1h)
    sent_mix = on_grads("mix", {"w_in": dwin, "w_out": dwout.reshape(ns, dwout.shape[0] // ns, d),
                                "ffn1_w_down": dwd1})
    da1, db1 = _ffn_bwd_act("ffn1_bwd_act", dx1h, wd1, a1, b1, sent_mix)
    sent_mix = grads_sent("mix", da1)
    dwg1 = _grad_w_cols("ffn1_dwg", da1, h1, sent_mix)
    dwu1 = _grad_w_cols("ffn1_dwu", db1, h1)
    on_grads("ffn1", {"ffn1_w_gate": dwg1, "ffn1_w_up": dwu1})
    sent1 = grads_sent("ffn1", None)
    dx0, dg1 = _ffn_bwd_in("ffn1_bwd_in", da1, db1, wg1, wu1, x0, g1, dx1, None, sent1)

    nt = dg1.shape[0]
    sg = _small_grads(
        "small_grads", dg1.reshape(nt, d), dgm.reshape(nt, d), dg2.reshape(nt, d),
        dgq.reshape(-1, ATTN_W), dgk.reshape(-1, ATTN_W), dbe.transpose(1, 0, 2), dbo.transpose(1, 0, 2),
        dlb.reshape(b, HGRN_W), dgo.reshape(b, HGRN_W), lbp)
    g1g, gmg, g2g, gqg, gkg, rbg, lbg, gog = sg
    small = _pack_small(g1g, gmg, g2g, lbg, rbg[:, :N_REL], gqg, gkg, gog, loss)
    return dx0.reshape(b, s, d), small


LOSS_SLOT = 7 * SMALL_COLS + 2 * ATTN_DH + HGRN_DH


def _pack_small(g1, gm, g2, lbp, rel_bias, gq, gk, go, loss=None):
    flat = [g1.reshape(-1), gm.reshape(-1), g2.reshape(-1), lbp.reshape(-1), rel_bias.reshape(-1)]
    n_bias = 3 * SMALL_COLS - rel_bias.size
    heads = [gq.reshape(-1), gk.reshape(-1), go.reshape(-1)]
    heads.append(jnp.zeros((1,), F32) if loss is None else loss.reshape(1))
    n_tail = SMALL_COLS - sum(h.size for h in heads)
    return jnp.concatenate(flat + [jnp.zeros((n_bias,), F32)] + heads + [jnp.zeros((n_tail,), F32)]).reshape(
        SMALL_ROWS, SMALL_COLS)


def _unpack_small(p, d):
    flat = p.reshape(-1)
    o = 3 * d
    g1, gm, g2 = p[0:1], p[1:2], p[2:3]
    lbp = flat[o:o + 2 * HGRN_W].reshape(2, HGRN_W)
    o = 4 * SMALL_COLS
    rel = flat[o:o + ATTN_HEADS * N_REL].reshape(1, ATTN_HEADS, N_REL)
    o = 7 * SMALL_COLS
    gq = flat[o:o + ATTN_DH].reshape(1, ATTN_DH)
    gk = flat[o + ATTN_DH:o + 2 * ATTN_DH].reshape(1, ATTN_DH)
    go = flat[o + 2 * ATTN_DH:o + 2 * ATTN_DH + HGRN_DH].reshape(1, HGRN_DH)
    return g1, gm, g2, gq, gk, rel, lbp, go


def kernel(x, ffn1_norm_g, ffn1_w_gate, ffn1_w_up, ffn1_w_down, mix_norm_g, w_in, attn_q_norm_g, attn_k_norm_g, attn_rel_bias, hgrn_lower_bounds, hgrn_out_norm_g, w_out, ffn2_norm_g, ffn2_w_gate, ffn2_w_up, ffn2_w_down, loss_target, m_ffn1_norm_g, m_ffn1_w_gate, m_ffn1_w_up, m_ffn1_w_down, m_mix_norm_g, m_w_in, m_attn_q_norm_g, m_attn_k_norm_g, m_attn_rel_bias, m_hgrn_lower_bounds, m_hgrn_out_norm_g, m_w_out, m_ffn2_norm_g, m_ffn2_w_gate, m_ffn2_w_up, m_ffn2_w_down, v_ffn1_norm_g, v_ffn1_w_gate, v_ffn1_w_up, v_ffn1_w_down, v_mix_norm_g, v_w_in, v_attn_q_norm_g, v_attn_k_norm_g, v_attn_rel_bias, v_hgrn_lower_bounds, v_hgrn_out_norm_g, v_w_out, v_ffn2_norm_g, v_ffn2_w_gate, v_ffn2_w_up, v_ffn2_w_down):
    d = x.shape[-1]
    big_w = [ffn1_w_gate, ffn1_w_up, ffn1_w_down, w_in, w_out, ffn2_w_gate, ffn2_w_up, ffn2_w_down]
    big_m = [m_ffn1_w_gate, m_ffn1_w_up, m_ffn1_w_down, m_w_in, m_w_out, m_ffn2_w_gate, m_ffn2_w_up, m_ffn2_w_down]
    big_v = [v_ffn1_w_gate, v_ffn1_w_up, v_ffn1_w_down, v_w_in, v_w_out, v_ffn2_w_gate, v_ffn2_w_up, v_ffn2_w_down]
    big_names = ["ffn1_w_gate", "ffn1_w_up", "ffn1_w_down", "w_in", "w_out", "ffn2_w_gate", "ffn2_w_up", "ffn2_w_down"]
    flipped = {nm for nm in big_names if nm.endswith("gate") or nm.endswith("up")}
    flip = lambda nm, a: jnp.swapaxes(a, 1, 2) if nm in flipped else a
    big_w, big_m, big_v = ([flip(nm, a) for nm, a in zip(big_names, arrs)] for arrs in (big_w, big_m, big_v))

    shards = [w[0].astype(BF16) for w in big_w]
    start_a = _gather_start("gather_start_up1", shards[:2], ())
    start_b = _gather_start("gather_start_mid", shards[2:5], (start_a[4],))
    start_c = _gather_start("gather_start_ffn2", shards[5:], (start_b[4],))

    pending = {}

    def arrived(tag, started, after):
        send_sem, recv_sem, srcs, outs, _ = started
        return _gather_wait("gather_wait_" + tag, send_sem, recv_sem, srcs, outs, after)

    def first_weights(after):
        return (*_gather_join("gather_join_up1", *arrived("up1", start_a, after)), (start_c[4],))

    def mid_weights(after):
        srcs, outs = arrived("mid", start_b, after)
        (wd1,) = _gather_join("gather_join_wd1", srcs[:1], outs[:1])
        pending["mid"] = _join_start("join_start_mid", srcs[1:], outs[1:])
        return wd1, (pending["mid"][3],)

    def mid_rest(after):
        sems, srcs, outs, _ = pending["mid"]
        win_f, wout_f = _join_wait("join_wait_mid", sems, srcs, outs, after)
        return win_f, wout_f.reshape(wout_f.shape[0] * wout_f.shape[1], d)

    def last_begin(after):
        pending["ffn2"] = _join_start("join_start_ffn2", *arrived("ffn2", start_c, after))
        return (pending["ffn2"][3],)

    def last_weights(after):
        sems, srcs, outs, _ = pending["ffn2"]
        return _join_wait("join_wait_ffn2", sems, srcs, outs, after)

    weights = {"first": first_weights, "mid": mid_weights, "mid_rest": mid_rest, "last_begin": last_begin,
               "last": last_weights}

    core = lax.axis_index("c").astype(jnp.int32).reshape(1)
    chip = (2 * lax.axis_index("x") + lax.axis_index("y")).astype(jnp.int32).reshape(1)
    started = {}

    def on_grads(tag, grads):
        names = list(grads)
        started[tag] = (names, _pair_start("pair_start_" + tag, [grads[nm] for nm in names]))
        return (started[tag][1][4],)

    def grads_sent(tag, after):
        names, (send_sem, recv_sem, grads, lands, token) = started[tag]
        grads, theirs = _pair_wait("pair_wait_" + tag, send_sem, recv_sem, grads, lands, token if after is None else after)
        sums = [_pair_sum("pair_sum_" + nm, g, th, core) for nm, g, th in zip(names, grads, theirs)]
        started[tag] = (names, _scatter_start("scatter_start_" + tag, sums))
        return (started[tag][1][4],)

    grad_x, small_g = _local_step(
        x, loss_target, ffn1_norm_g, mix_norm_g, ffn2_norm_g, attn_q_norm_g, attn_k_norm_g, hgrn_out_norm_g,
        attn_rel_bias[0], hgrn_lower_bounds, weights, on_grads, grads_sent)

    def finish(tag, after):
        names, (send_sem, recv_sem, sums, lands, _) = started[tag]
        sums, lands = _scatter_wait("scatter_wait_" + tag, send_sem, recv_sem, sums, lands, after)
        return names, [_chip_sum("chip_sum_" + nm, sm, ld, chip) for nm, sm, ld in zip(names, sums, lands)]

    by_name = {nm: (w, m, v) for nm, w, m, v in zip(big_names, big_w, big_m, big_v)}
    updated = {}

    def update(names, halves, other_halves):
        for nm, mine, theirs in zip(names, halves, other_halves):
            w, m, v = by_name[nm]
            updated[nm] = _adamw("adamw_" + nm, w, mine, theirs, m, v, core)

    last_token = started["ffn1"][1][4]
    names_a, halves_a = finish("ffn2", last_token)
    names_m, halves_m = finish("mix", last_token)
    names_a, halves_a = names_a + names_m, halves_a + halves_m
    update(names_a, halves_a, _pair_join("pair_join_early", halves_a))
    names_b, halves_b = finish("ffn1", updated[names_a[-1]][1])
    others_b, small_all = _pair_join("pair_join_last", halves_b, small_g)
    update(names_b, halves_b, others_b)
    big_out = [updated[nm] for nm in big_names]

    pack = lambda g1, gm, g2, gq, gk, rel, lbp, go: _pack_small(g1, gm, g2, lbp, rel[0], gq, gk, go)
    small_w = pack(ffn1_norm_g, mix_norm_g, ffn2_norm_g, attn_q_norm_g, attn_k_norm_g, attn_rel_bias, hgrn_lower_bounds, hgrn_out_norm_g)
    small_m = pack(m_ffn1_norm_g, m_mix_norm_g, m_ffn2_norm_g, m_attn_q_norm_g, m_attn_k_norm_g, m_attn_rel_bias, m_hgrn_lower_bounds, m_hgrn_out_norm_g)
    small_v = pack(v_ffn1_norm_g, v_mix_norm_g, v_ffn2_norm_g, v_attn_q_norm_g, v_attn_k_norm_g, v_attn_rel_bias, v_hgrn_lower_bounds, v_hgrn_out_norm_g)
    small_res = _adamw_small("adamw_small", small_w, small_all, small_m, small_v)
    small_out = [_unpack_small(p, d) for p in small_res]
    loss = small_res[0].reshape(-1)[LOSS_SLOT]

    def assemble(kind):
        bg = [flip(nm, o[kind]) for nm, o in zip(big_names, big_out)]
        g1, gm, g2, gq, gk, rel, lbp, go = small_out[kind]
        return [g1, bg[0], bg[1], bg[2], gm, bg[3], gq, gk, rel, lbp, go, bg[4], g2, bg[5], bg[6], bg[7]]

    return (loss, grad_x, *assemble(0), *assemble(1), *assemble(2), *assemble(3))
```

```python
import functools

import jax
import jax.numpy as jnp
from jax import lax
from jax.experimental import pallas as pl
from jax.experimental.pallas import tpu as pltpu

F32 = jnp.float32
BF16 = jnp.bfloat16
MESH = pl.DeviceIdType.MESH

N_CHIPS = 4
N_DEV = 8
CHUNK = 64
ATTN_HEADS = 8
ATTN_DH = 64
ATTN_W = ATTN_HEADS * ATTN_DH
HGRN_HEADS = 4
HGRN_DH = 128
HGRN_W = HGRN_HEADS * HGRN_DH
LEFT_CHUNKS = 8
BAND = (LEFT_CHUNKS + 1) * CHUNK
KPAD = LEFT_CHUNKS * CHUNK
REL_CLIP = 128
N_REL = 2 * REL_CLIP + 1
N_REL_PAD = 384
RMS_EPS = 1e-6
LANES = 128
SMALL_ROWS = 8
SMALL_COLS = 1024

ADAM_LR = 0.001
ADAM_B1 = 0.9
ADAM_B2 = 0.999
ADAM_EPS = 1e-08
ADAM_WD = 0.01
ADAM_STEP = 10

NN = (((1,), (0,)), ((), ()))
NT = (((1,), (1,)), ((), ()))
TN = (((0,), (0,)), ((), ()))

VMEM_LIMIT = 48 * 1024 * 1024
MXU_WIDTH = 256
COL_CHUNK = 3 * MXU_WIDTH


def _sigmoid(x):
    return 1.0 / (1.0 + jnp.exp(-x))


def _silu(x):
    return x * _sigmoid(x)


def _dot(a, b, dims=NN):
    return lax.dot_general(a, b, dims, preferred_element_type=F32)


def _split3(x):
    hi = x.astype(BF16)
    r1 = x - hi.astype(F32)
    mid = r1.astype(BF16)
    lo = (r1 - mid.astype(F32)).astype(BF16)
    return hi, mid, lo


def _dot_exact_rhs(x, mat, dims=NN, pieces=3):
    hi, mid, lo = _split3(x)
    out = _dot(hi, mat, dims) + _dot(mid, mat, dims)
    return out + _dot(lo, mat, dims) if pieces == 3 else out


def _dot_exact_lhs(mat, x, dims=NN):
    hi, mid, lo = _split3(x)
    return _dot(mat, hi, dims) + _dot(mat, mid, dims) + _dot(mat, lo, dims)


def _params(*sem):
    return pltpu.CompilerParams(dimension_semantics=sem, vmem_limit_bytes=VMEM_LIMIT)


def _mm(name, ins, terms, n_acc, grid, acc_shape, outs, epilogue, extras=(), deps=()):
    nk = grid[2]
    ni, ne, nd, no = len(ins), len(extras), len(deps), len(outs)

    def body(*refs):
        in_refs = refs[:ni]
        ex_refs = refs[ni:ni + ne]
        out_refs = refs[ni + ne + nd:ni + ne + nd + no]
        acc_refs = refs[ni + ne + nd + no:]

        def products():
            parts = [None] * n_acc
            for ai, li, ri, dims in terms:
                d = _dot(in_refs[li][...], in_refs[ri][...], dims)
                parts[ai] = d if parts[ai] is None else parts[ai] + d
            return parts

        def finish(accs):
            res = epilogue(accs, [e[...] for e in ex_refs])
            for o, r in zip(out_refs, res):
                o[...] = r.astype(o.dtype)

        if nk == 1:
            finish(products())
        else:
            k = pl.program_id(2)

            @pl.when(k == 0)
            def _():
                for a, p in zip(acc_refs, products()):
                    a[...] = p

            if nk > 2:
                @pl.when(jnp.logical_and(k > 0, k < nk - 1))
                def _():
                    for a, p in zip(acc_refs, products()):
                        a[...] += p

            @pl.when(k == nk - 1)
            def _():
                finish([a[...] + p for a, p in zip(acc_refs, products())])

    scratch = [] if nk == 1 else [pltpu.VMEM(acc_shape, F32) for _ in range(n_acc)]
    res = pl.pallas_call(
        body,
        name=name,
        grid=grid,
        in_specs=[s for _, s in ins] + [s for _, s in extras] + [pl.BlockSpec(memory_space=pl.ANY)] * nd,
        out_specs=[s for _, s in outs],
        out_shape=[o for o, _ in outs],
        scratch_shapes=scratch,
        compiler_params=_params("parallel", "parallel", "arbitrary"),
    )(*[a for a, _ in ins], *[a for a, _ in extras], *deps)
    return res


def _staged_shape(w):
    return w.shape if len(w.shape) == 2 else (w.shape[1], w.shape[0] * w.shape[2])


def _stage_weights(w_hbm, w_vmem, sem):
    @pl.when(pl.program_id(0) == 0)
    def _():
        copies = []
        for p, (h, v) in enumerate(zip(w_hbm, w_vmem)):
            if len(h.shape) == 2:
                copies.append(pltpu.make_async_copy(h, v, sem.at[p, 0]))
            else:
                pj = h.shape[2]
                copies += [pltpu.make_async_copy(h.at[j], v.at[:, pl.ds(j * pj, pj)], sem.at[p, j])
                           for j in range(h.shape[0])]
        for cp in copies:
            cp.start()
        for cp in copies:
            cp.wait()


def _staging_scratch(weights):
    return [pltpu.VMEM(_staged_shape(w), w.dtype) for w in weights] + [pltpu.SemaphoreType.DMA((len(weights), N_CHIPS))]


def _mm_rows(name, lhs, weights, dims, t, outs, epilogue, extras=(), deps=()):
    tm = _row_tile(t)
    nl, ne, nd, no = len(lhs), len(extras), len(deps), len(outs)

    def body(*refs):
        lhs_refs = refs[:nl]
        w_hbm = refs[nl:2 * nl]
        ex_refs = refs[2 * nl:2 * nl + ne]
        out_refs = refs[2 * nl + ne + nd:2 * nl + ne + nd + no]
        w_vmem = refs[2 * nl + ne + nd + no:3 * nl + ne + nd + no]
        _stage_weights(w_hbm, w_vmem, refs[-1])

        acc = None
        for p in range(nl):
            part = _dot(lhs_refs[p][...], w_vmem[p][...], dims)
            acc = part if acc is None else acc + part
        res = epilogue([acc], [e[...] for e in ex_refs])
        for o, r in zip(out_refs, res):
            o[...] = r.astype(o.dtype)

    return pl.pallas_call(
        body,
        name=name,
        grid=(t // tm,),
        in_specs=[s for _, s in lhs] + [pl.BlockSpec(memory_space=pl.ANY)] * nl + [s for _, s in extras]
        + [pl.BlockSpec(memory_space=pl.ANY)] * nd,
        out_specs=[s for _, s in outs],
        out_shape=[o for o, _ in outs],
        scratch_shapes=_staging_scratch(weights),
        compiler_params=_params("arbitrary"),
    )(*[a for a, _ in lhs], *weights, *[a for a, _ in extras], *deps)


def _col_chunks(f):
    return [(c, min(COL_CHUNK, f - c)) for c in range(0, f, COL_CHUNK)]


def _mm_cols(name, x, weights, dims, n_out, epilogue, extras=(), deps=(), out_dtype=BF16):
    t, k = x.shape
    f = _staged_shape(weights[0])[0 if dims == NT else 1]
    tm = _row_tile(t)
    chunks = _col_chunks(f)
    nw, ne, nd = len(weights), len(extras), len(deps)

    def body(*refs):
        x_ref = refs[0]
        w_hbm = refs[1:1 + nw]
        ex_refs = refs[1 + nw:1 + nw + ne]
        out_refs = refs[1 + nw + ne + nd:1 + nw + ne + nd + n_out]
        w_vmem = refs[1 + nw + ne + nd + n_out:1 + 2 * nw + ne + nd + n_out]
        _stage_weights(w_hbm, w_vmem, refs[-1])

        xv = x_ref[...]

        def dots(c):
            c0, cw = chunks[c]
            return [_dot(xv, w[c0:c0 + cw, :] if dims == NT else w[:, c0:c0 + cw], dims) for w in w_vmem]

        accs = dots(0)
        for c, (c0, cw) in enumerate(chunks):
            nxt = dots(c + 1) if c + 1 < len(chunks) else None
            res = epilogue(accs, [e[:, c0:c0 + cw] for e in ex_refs])
            for o, r in zip(out_refs, res):
                o[:, c0:c0 + cw] = r.astype(o.dtype)
            accs = nxt

    act = pl.BlockSpec((tm, f), lambda i: (i, 0))
    return pl.pallas_call(
        body,
        name=name,
        grid=(t // tm,),
        in_specs=[pl.BlockSpec((tm, k), lambda i: (i, 0))] + [pl.BlockSpec(memory_space=pl.ANY)] * nw + [act] * ne
        + [pl.BlockSpec(memory_space=pl.ANY)] * nd,
        out_specs=[act] * n_out,
        out_shape=[jax.ShapeDtypeStruct((t, f), out_dtype)] * n_out,
        scratch_shapes=_staging_scratch(weights),
        compiler_params=_params("arbitrary"),
    )(x, *weights, *extras, *deps)


def _row_tile(t):
    return 512 if t % 512 == 0 else t


def _k_tile(t):
    return t if t <= 4096 else 1024


def _grad_k_tile(t):
    return 2048 if t % 2048 == 0 else t


def _rmsnorm(xv, g):
    ms = jnp.mean(xv * xv, axis=-1, keepdims=True)
    return xv * lax.rsqrt(ms + RMS_EPS) * g


def _rmsnorm_fwd(name, x, g):
    t, d = x.shape
    tm = _row_tile(t)

    def body(x_ref, g_ref, h_ref):
        h_ref[...] = _rmsnorm(x_ref[...], g_ref[...]).astype(BF16)

    return pl.pallas_call(
        body,
        name=name,
        grid=(t // tm,),
        in_specs=[pl.BlockSpec((tm, d), lambda i: (i, 0)), pl.BlockSpec((1, d), lambda i: (0, 0))],
        out_specs=pl.BlockSpec((tm, d), lambda i: (i, 0)),
        out_shape=jax.ShapeDtypeStruct((t, d), BF16),
        compiler_params=_params("parallel"),
    )(x, g)


def _norm_bwd_epilogue(copy_scale):
    def epilogue(accs, ex):
        dh = accs[0]
        xv, g, dres = ex
        ms = jnp.mean(xv * xv, axis=-1, keepdims=True)
        rstd = lax.rsqrt(ms + RMS_EPS)
        xhat = xv * rstd
        dxhat = dh * g
        dx = rstd * (dxhat - xhat * jnp.mean(dxhat * xhat, axis=-1, keepdims=True))
        out = dres + dx
        dg = jnp.sum(dh * xhat, axis=0, keepdims=True)
        if copy_scale is None:
            return out, dg
        return out, out * copy_scale, dg

    return epilogue


def _merged(w):
    return w.reshape(-1, w.shape[-1])


def _ffn_up(name, h, wg, wu, deps=()):
    def epilogue(accs, ex):
        a, b = accs
        sg = _sigmoid(a)
        act = a * sg
        return act, b * (sg * (1.0 + a * (1.0 - sg))), act * b

    return _mm_cols(name, h, [_merged(wg), _merged(wu)], NT, 3, epilogue, deps=deps)


def _whole_rows(arr, tm):
    return arr, pl.BlockSpec((tm, arr.shape[1]), lambda i: (i, 0))


def _ffn_down(name, z, wd, x, g_next, deps=()):
    t = z.shape[0]
    d = wd.shape[2]
    tm = _row_tile(t)
    row = pl.BlockSpec((tm, d), lambda i: (i, 0))

    def epilogue(accs, ex):
        y = ex[0] + 0.5 * accs[0]
        return y, _rmsnorm(y, ex[1])

    return _mm_rows(
        name, [_whole_rows(z, tm)], [_merged(wd)], NN, t,
        outs=[(jax.ShapeDtypeStruct((t, d), F32), row), (jax.ShapeDtypeStruct((t, d), BF16), row)],
        epilogue=epilogue,
        extras=[(x, row), (g_next, pl.BlockSpec((1, d), lambda i: (0, 0)))],
        deps=deps,
    )


def _ffn_down_loss(name, z, wd, x, target):
    t = z.shape[0]
    d = wd.shape[2]
    tm = _row_tile(t)
    nt = t // tm
    row = pl.BlockSpec((tm, d), lambda i: (i, 0))

    def epilogue(accs, ex):
        e = ex[0] + 0.5 * accs[0] - ex[1]
        dy = e * (1.0 / d)
        return dy, 0.5 * dy, jnp.sum(e * e, axis=0, keepdims=True)

    return _mm_rows(
        name, [_whole_rows(z, tm)], [_merged(wd)], NN, t,
        outs=[(jax.ShapeDtypeStruct((t, d), F32), row), (jax.ShapeDtypeStruct((t, d), BF16), row),
              (jax.ShapeDtypeStruct((nt, 1, d), F32), pl.BlockSpec((None, 1, d), lambda i: (i, 0, 0)))],
        epilogue=epilogue,
        extras=[(x, row), (target, row)],
    )


def _ffn_bwd_act(name, dout, wd, act_a, dact_b, deps=()):
    def epilogue(accs, ex):
        dz = accs[0]
        return dz * ex[1].astype(F32), dz * ex[0].astype(F32)

    return _mm_cols(name, dout, [_merged(wd)], NT, 2, epilogue, extras=[act_a, dact_b], deps=deps)


def _grad_w_cols(name, z, dout, deps=()):
    t, f = z.shape
    d = dout.shape[1]
    tk = _grad_k_tile(t)
    fh = f // 2
    dw = _mm(
        name,
        ins=[(z, pl.BlockSpec((tk, fh), lambda j, n, k: (k, j))),
             (dout, pl.BlockSpec((tk, d), lambda j, n, k: (k, 0)))],
        terms=[(0, 0, 1, TN)],
        n_acc=1,
        grid=(2, 1, t // tk),
        acc_shape=(fh, d),
        outs=[(pltpu.HBM((f, d), BF16), pl.BlockSpec((fh, d), lambda j, n, k: (j, 0)))],
        epilogue=lambda accs, ex: (accs[0],),
        deps=deps,
    )[0]
    return dw.reshape(N_CHIPS, f // N_CHIPS, d)


def _norm_bwd_outs(t, d, tm, copy_scale):
    row = pl.BlockSpec((tm, d), lambda i: (i, 0))
    outs = [(jax.ShapeDtypeStruct((t, d), F32), row)]
    if copy_scale is not None:
        outs.append((jax.ShapeDtypeStruct((t, d), BF16), row))
    outs.append((jax.ShapeDtypeStruct((t // tm, 1, d), F32), pl.BlockSpec((None, 1, d), lambda i: (i, 0, 0))))
    return row, outs


def _ffn_bwd_in(name, da, db, wg, wu, x, g, dres, copy_scale, deps=()):
    t = da.shape[0]
    d = wg.shape[2]
    tm = _row_tile(t)
    row, outs = _norm_bwd_outs(t, d, tm, copy_scale)
    return _mm_rows(
        name, [_whole_rows(da, tm), _whole_rows(db, tm)], [_merged(wg), _merged(wu)], NN, t,
        outs=outs,
        epilogue=_norm_bwd_epilogue(copy_scale),
        extras=[(x, row), (g, pl.BlockSpec((1, d), lambda i: (0, 0))), (dres, row)],
        deps=deps,
    )


def _in_proj(name, h, w_in):
    return _mm_cols(name, h, [w_in], NN, 1, lambda accs, ex: (accs[0],), out_dtype=F32)[0]


def _in_proj_bwd(name, dp, w_in, x, g, dres, copy_scale, deps=()):
    t = dp.shape[0]
    d = w_in.shape[1]
    tm = _row_tile(t)
    row, outs = _norm_bwd_outs(t, d, tm, copy_scale)
    return _mm_rows(
        name, [_whole_rows(dp, tm)], [w_in], NT, t,
        outs=outs,
        epilogue=_norm_bwd_epilogue(copy_scale),
        extras=[(x, row), (g, pl.BlockSpec((1, d), lambda i: (0, 0))), (dres, row)],
        deps=deps,
    )


def _grad_w_in(name, h, dp, ns):
    t, d = h.shape
    pj = dp.shape[1] // ns
    tk = 1024 if t % 1024 == 0 else t
    return _mm(
        name,
        ins=[(h, pl.BlockSpec((tk, d), lambda j, n, k: (k, 0))),
             (dp, pl.BlockSpec((tk, 2 * pj), lambda j, n, k: (k, j)))],
        terms=[(0, 0, 1, TN)],
        n_acc=1,
        grid=(ns // 2, 1, t // tk),
        acc_shape=(d, 2 * pj),
        outs=[(pltpu.HBM((ns, d, pj), BF16), pl.BlockSpec((2, d, pj), lambda j, n, k: (j, 0, 0)))],
        epilogue=lambda accs, ex: (jnp.stack([accs[0][:, :pj], accs[0][:, pj:]]),),
    )[0]


def _out_proj(name, mix, w_out, x, g_next):
    t, dm = mix.shape
    d = w_out.shape[1]
    tm = _row_tile(t)
    row = pl.BlockSpec((tm, d), lambda i, n, k: (i, 0))
    return _mm(
        name,
        ins=[(mix, pl.BlockSpec((tm, dm), lambda i, n, k: (i, 0))),
             (w_out, pl.BlockSpec((dm, d), lambda i, n, k: (0, 0)))],
        terms=[(0, 0, 1, NN)],
        n_acc=1,
        grid=(t // tm, 1, 1),
        acc_shape=(tm, d),
        outs=[(jax.ShapeDtypeStruct((t, d), F32), row), (jax.ShapeDtypeStruct((t, d), BF16), row)],
        epilogue=lambda accs, ex: (ex[0] + accs[0], _rmsnorm(ex[0] + accs[0], ex[1])),
        extras=[(x, row), (g_next, pl.BlockSpec((1, d), lambda i, n, k: (0, 0)))],
    )


def _out_proj_bwd(name, dx, w_out, deps=()):
    t, d = dx.shape
    dm = w_out.shape[0]
    tm = _row_tile(t)
    return _mm(
        name,
        ins=[(dx, pl.BlockSpec((tm, d), lambda i, n, k: (i, 0))),
             (w_out, pl.BlockSpec((dm, d), lambda i, n, k: (0, 0)))],
        terms=[(0, 0, 1, NT)],
        n_acc=1,
        grid=(t // tm, 1, 1),
        acc_shape=(tm, dm),
        outs=[(jax.ShapeDtypeStruct((t, dm), F32), pl.BlockSpec((tm, dm), lambda i, n, k: (i, 0)))],
        epilogue=lambda accs, ex: (accs[0],),
        deps=deps,
    )[0]


def _grad_w_out(name, mix, dx):
    t, dm = mix.shape
    d = dx.shape[1]
    tk = _k_tile(t)
    return _mm(
        name,
        ins=[(mix, pl.BlockSpec((tk, dm), lambda a, n, k: (k, 0))),
             (dx, pl.BlockSpec((tk, d), lambda a, n, k: (k, 0)))],
        terms=[(0, 0, 1, TN)],
        n_acc=1,
        grid=(1, 1, t // tk),
        acc_shape=(dm, d),
        outs=[(pltpu.HBM((dm, d), BF16), pl.BlockSpec((dm, d), lambda a, n, k: (0, 0)))],
        epilogue=lambda accs, ex: (accs[0],),
    )[0]


def _head_group_matrix():
    r = lax.broadcasted_iota(jnp.int32, (ATTN_W, ATTN_W), 0)
    c = lax.broadcasted_iota(jnp.int32, (ATTN_W, ATTN_W), 1)
    same = jnp.right_shift(r, 6) == jnp.right_shift(c, 6)
    return jnp.where(same, 1.0, 0.0).astype(BF16)


def _qk_prep(name, proj, gq, gk):
    b, s, _ = proj.shape
    tm = KPAD
    nb = s // tm

    def body(q_ref, k_ref, v_ref, gq_ref, gk_ref, qn_ref, kn_ref, vb_ref):
        j = pl.program_id(1)
        bd = _head_group_matrix()

        def norm(xv, g):
            ms = _dot_exact_rhs(xv * xv, bd, pieces=2) * (1.0 / ATTN_DH)
            return xv * lax.rsqrt(ms + RMS_EPS) * g

        @pl.when(j == 0)
        def _():
            kn_ref[...] = jnp.zeros_like(kn_ref)
            vb_ref[...] = jnp.zeros_like(vb_ref)

        @pl.when(j > 0)
        def _():
            qn_ref[...] = norm(q_ref[...], gq_ref[...]).astype(BF16)
            kn_ref[...] = norm(k_ref[...], gk_ref[...]).astype(BF16)
            vb_ref[...] = v_ref[...].astype(BF16)

    src_blk = lambda col: pl.BlockSpec((None, tm, ATTN_W), lambda bi, j: (bi, jnp.maximum(j - 1, 0), col))
    gspec = pl.BlockSpec((1, ATTN_W), lambda bi, j: (0, 0))
    padded = pl.BlockSpec((None, tm, ATTN_W), lambda bi, j: (bi, j, 0))
    return pl.pallas_call(
        body,
        name=name,
        grid=(b, nb + 1),
        in_specs=[src_blk(0), src_blk(1), src_blk(2), gspec, gspec],
        out_specs=[src_blk(0), padded, padded],
        out_shape=[jax.ShapeDtypeStruct((b, s, ATTN_W), BF16), jax.ShapeDtypeStruct((b, KPAD + s, ATTN_W), BF16),
                   jax.ShapeDtypeStruct((b, KPAD + s, ATTN_W), BF16)],
        compiler_params=_params("parallel", "arbitrary"),
    )(proj, proj, proj, gq, gk)


def _qk_prep_bwd(name, proj, dqn, dkn, dv, gq, gk):
    b, s, _ = proj.shape
    tm = KPAD
    nb = s // tm

    def body(q_ref, k_ref, dqn_ref, dkn_ref, dv_ref, gq_ref, gk_ref, dq_ref, dk_ref, dvb_ref, dgq_ref, dgk_ref):
        bd = _head_group_matrix()

        def bwd(xv, dy, g):
            ms = _dot_exact_rhs(xv * xv, bd, pieces=2) * (1.0 / ATTN_DH)
            rstd = lax.rsqrt(ms + RMS_EPS)
            xhat = xv * rstd
            dxhat = dy * g
            gm = _dot_exact_rhs(dxhat * xhat, bd, pieces=2) * (1.0 / ATTN_DH)
            return rstd * (dxhat - xhat * gm), jnp.sum(dy * xhat, axis=0, keepdims=True)

        dq, dgq = bwd(q_ref[...], dqn_ref[...], gq_ref[...])
        dk, dgk = bwd(k_ref[...], dkn_ref[...], gk_ref[...])
        dq_ref[...] = dq.astype(BF16)
        dk_ref[...] = dk.astype(BF16)
        dvb_ref[...] = dv_ref[...].astype(BF16)
        dgq_ref[...] = dgq
        dgk_ref[...] = dgk

    col = lambda c: pl.BlockSpec((None, tm, ATTN_W), lambda bi, j: (bi, j, c))
    past_pad = pl.BlockSpec((None, tm, ATTN_W), lambda bi, j: (bi, j + 1, 0))
    gspec = pl.BlockSpec((1, ATTN_W), lambda bi, j: (0, 0))
    pspec = pl.BlockSpec((None, 1, ATTN_W), lambda bi, j: (bi * nb + j, 0, 0))
    o_shape = jax.ShapeDtypeStruct((b, s, ATTN_W), BF16)
    p_shape = jax.ShapeDtypeStruct((b * nb, 1, ATTN_W), F32)
    return pl.pallas_call(
        body,
        name=name,
        grid=(b, nb),
        in_specs=[col(0), col(1), col(0), past_pad, past_pad, gspec, gspec],
        out_specs=[col(0)] * 3 + [pspec] * 2,
        out_shape=[o_shape] * 3 + [p_shape] * 2,
        compiler_params=_params("parallel", "parallel"),
    )(proj, proj, dqn, dkn, dv, gq, gk)


Q_CHUNKS = 4
QBLK = Q_CHUNKS * CHUNK
WIN = (LEFT_CHUNKS + Q_CHUNKS) * CHUNK
DB_W = BAND + CHUNK
MASKED = -1e30
FWD_BLOCKS = 8
BWD_BLOCKS = 2


def _band_table(bias):
    rows = [jnp.pad(bias, ((0, 0), (0, 0), (CHUNK * i, WIN - BAND - CHUNK * i)), constant_values=MASKED)
            for i in range(Q_CHUNKS)]
    return jnp.concatenate(rows, axis=1)


def _head_lanes(hh):
    lane = lax.broadcasted_iota(jnp.int32, (1, LANES), 1)
    return (lane < ATTN_DH) if hh == 0 else (lane >= ATTN_DH)


def _attn_probs(qh, kw, table, start):
    s = _dot(qh, kw, NT) * (ATTN_DH ** -0.5) + table
    col = lax.broadcasted_iota(jnp.int32, (QBLK, WIN), 1)
    s = jnp.where(col + start >= KPAD, s, MASKED)
    m = jnp.max(s, axis=-1, keepdims=True)
    p = jnp.exp(s - m)
    return p * (1.0 / jnp.sum(p, axis=-1, keepdims=True))


def _attn_fwd(name, q, k, v, table, deps=()):
    b, s, w = q.shape
    sp = k.shape[1]

    def body(q_ref, k_ref, v_ref, t_ref, *rest):
        o_ref = rest[-1]
        lanes = [_head_lanes(hh) for hh in range(2)]
        starts = [pl.multiple_of((pl.program_id(2) * FWD_BLOCKS + j) * QBLK, QBLK) for j in range(FWD_BLOCKS)]
        kws = [k_ref[pl.ds(st, WIN), :] for st in starts]
        vws = [v_ref[pl.ds(st, WIN), :] for st in starts]
        q2s = [q_ref[j * QBLK:(j + 1) * QBLK, :] for j in range(FWD_BLOCKS)]
        probs = [[_attn_probs(jnp.where(mine, q2s[j], jnp.zeros_like(q2s[j])), kws[j], t_ref[hh], starts[j]).astype(BF16)
                  for hh, mine in enumerate(lanes)] for j in range(FWD_BLOCKS)]
        for j in range(FWD_BLOCKS):
            outs = [_dot(p, vws[j]) for p in probs[j]]
            o_ref[j * QBLK:(j + 1) * QBLK, :] = jnp.where(lanes[0], outs[0], outs[1]).astype(BF16)

    qspec = pl.BlockSpec((None, FWD_BLOCKS * QBLK, LANES), lambda p, bi, i: (bi, i, p))
    kspec = pl.BlockSpec((None, sp, LANES), lambda p, bi, i: (bi, 0, p))
    return pl.pallas_call(
        body,
        name=name,
        grid=(w // LANES, b, s // (FWD_BLOCKS * QBLK)),
        in_specs=[qspec, kspec, kspec, pl.BlockSpec((2, QBLK, WIN), lambda p, bi, i: (p, 0, 0))] + [ANY] * len(deps),
        out_specs=qspec,
        out_shape=jax.ShapeDtypeStruct((b, s, w), BF16),
        compiler_params=_params("parallel", "parallel", "arbitrary"),
    )(q, k, v, table, *deps)


def _attn_bwd(name, q, k, v, table, dmix):
    b, s, w = q.shape
    sp = k.shape[1]

    def body(q_ref, k_ref, v_ref, t_ref, do_ref, dq_ref, dk_ref, dv_ref, dbe_ref, dbo_ref):
        bi = pl.program_id(1)
        i = pl.program_id(2)

        @pl.when(i == 0)
        def _():
            dk_ref[...] = jnp.zeros_like(dk_ref)
            dv_ref[...] = jnp.zeros_like(dv_ref)

        @pl.when(jnp.logical_and(i == 0, bi == 0))
        def _():
            dbe_ref[...] = jnp.zeros_like(dbe_ref)
            dbo_ref[...] = jnp.zeros_like(dbo_ref)

        lanes = [_head_lanes(hh) for hh in range(2)]

        def scores(j):
            start = pl.multiple_of((i * BWD_BLOCKS + j) * QBLK, QBLK)
            win = pl.ds(start, WIN)
            kw = k_ref[win, :]
            vw = v_ref[win, :]
            q2 = q_ref[j * QBLK:(j + 1) * QBLK, :]
            do2 = do_ref[j * QBLK:(j + 1) * QBLK, :].astype(BF16)
            qh = [jnp.where(mine, q2, jnp.zeros_like(q2)) for mine in lanes]
            doh = [jnp.where(mine, do2, jnp.zeros_like(do2)) for mine in lanes]
            p = [_attn_probs(qh[hh], kw, t_ref[hh], start) for hh in range(2)]
            dp = [_dot(doh[hh], vw, NT) for hh in range(2)]
            return win, kw, qh, doh, p, dp

        def gradients(j, win, kw, qh, doh, p, dp):
            ds = [p[hh] * (dp[hh] - jnp.sum(p[hh] * dp[hh], axis=-1, keepdims=True)) for hh in range(2)]
            dsb = [(x * (ATTN_DH ** -0.5)).astype(BF16) for x in ds]
            pb = [x.astype(BF16) for x in p]
            dq = [_dot(dsb[hh], kw) for hh in range(2)]
            dk = [_dot(dsb[hh], qh[hh], TN) for hh in range(2)]
            dv = [_dot(pb[hh], doh[hh], TN) for hh in range(2)]
            for hh in range(2):
                for qi in range(Q_CHUNKS):
                    c0 = (qi // 2) * LANES
                    blk = ds[hh][qi * CHUNK:(qi + 1) * CHUNK, c0:c0 + DB_W]
                    if qi % 2 == 0:
                        dbe_ref[hh] += blk
                    else:
                        dbo_ref[hh] += blk
            dq_ref[j * QBLK:(j + 1) * QBLK, :] = jnp.where(lanes[0], dq[0], dq[1])
            dk_ref[win, :] += dk[0] + dk[1]
            dv_ref[win, :] += dv[0] + dv[1]

        staged = scores(0)
        for j in range(BWD_BLOCKS):
            upcoming = scores(j + 1) if j + 1 < BWD_BLOCKS else None
            gradients(j, *staged)
            staged = upcoming

    qspec = pl.BlockSpec((None, BWD_BLOCKS * QBLK, LANES), lambda p, bi, i: (bi, i, p))
    kspec = pl.BlockSpec((None, sp, LANES), lambda p, bi, i: (bi, 0, p))
    dbspec = pl.BlockSpec((2, CHUNK, DB_W), lambda p, bi, i: (p, 0, 0))
    db_shape = jax.ShapeDtypeStruct((ATTN_HEADS, CHUNK, DB_W), F32)
    return pl.pallas_call(
        body,
        name=name,
        grid=(w // LANES, b, s // (BWD_BLOCKS * QBLK)),
        in_specs=[qspec, kspec, kspec, pl.BlockSpec((2, QBLK, WIN), lambda p, bi, i: (p, 0, 0)), qspec],
        out_specs=[qspec, kspec, kspec, dbspec, dbspec],
        out_shape=[jax.ShapeDtypeStruct((b, s, w), F32), jax.ShapeDtypeStruct((b, sp, w), F32),
                   jax.ShapeDtypeStruct((b, sp, w), F32), db_shape, db_shape],
        compiler_params=_params("arbitrary", "arbitrary", "arbitrary"),
    )(q, k, v, table, dmix)


HQ_COL = 3 * ATTN_W // HGRN_DH
HF_COL = HQ_COL + HGRN_HEADS
HI_COL = HF_COL + HGRN_HEADS
HG_COL = HI_COL + HGRN_HEADS
HGRN_ROWS = 8 * CHUNK
HGRN_UNROLL = 8
HEAD_LANES = [slice(hh * HGRN_DH, (hh + 1) * HGRN_DH) for hh in range(HGRN_HEADS)]


def _tri(lower):
    r = lax.broadcasted_iota(jnp.int32, (CHUNK, CHUNK), 0)
    c = lax.broadcasted_iota(jnp.int32, (CHUNK, CHUNK), 1)
    return (r >= c) if lower else (r <= c)


def _hgrn_chunk(hq, hf, lb, tril):
    sig = _sigmoid(hf)
    f = lb + (1.0 - lb) * sig
    g = jnp.log(f)
    ones_l = jnp.where(tril, 1.0, 0.0).astype(BF16)
    b = _dot_exact_lhs(ones_l, g)
    bl = jnp.sum(g, axis=0, keepdims=True)
    rows = lax.broadcasted_iota(jnp.int32, g.shape, 0)
    bm = jnp.sum(jnp.where(rows <= CHUNK // 2, g, 0.0), axis=0, keepdims=True)
    sq = _sigmoid(hq)
    q = hq * sq
    k = 1.0 - f
    return sig, f, b, bl, bm, sq, q, k


def _hgrn_fwd(name, proj, attn, lb, go, b, s):
    nc = s // CHUNK
    t = b * s
    nblk = s // HGRN_ROWS
    cpb = HGRN_ROWS // CHUNK

    def body(hq_ref, hf_ref, hi_ref, hg_ref, attn_ref, lb_ref, go_ref, mix_ref, oraw_ref, st_ref, s_scr):
        tril = _tri(True)
        gov = go_ref[...]
        mix_ref[:, 0:ATTN_W] = attn_ref[...]

        @pl.when(pl.program_id(1) == 0)
        def _():
            s_scr[...] = jnp.zeros_like(s_scr)

        def step(c, carry):
            sl = pl.ds(pl.multiple_of(c * CHUNK, CHUNK), CHUNK)
            hg = hg_ref[sl, :]
            _, _, bb, bl, bm, _, q, k = _hgrn_chunk(hq_ref[sl, :], hf_ref[sl, :], lb_ref[...], tril)
            vb = hi_ref[sl, :].astype(BF16)
            qe = (q * jnp.exp(bb - bm)).astype(BF16)
            ke = (k * jnp.exp(bm - bb)).astype(BF16)
            qb = (q * jnp.exp(bb)).astype(BF16)
            kb = (k * jnp.exp(bl - bb)).astype(BF16)
            e_last = jnp.exp(bl)
            gate = _silu(hg)
            st = [s_scr[hh] for hh in range(HGRN_HEADS)]
            a = [jnp.where(tril, _dot(qe[:, hs], ke[:, hs], NT), 0.0).astype(BF16) for hs in HEAD_LANES]
            o_state = [_dot(qb[:, hs], st[hh].astype(BF16), NT) for hh, hs in enumerate(HEAD_LANES)]
            st_next = [st[hh] * e_last[:, hs] + _dot(vb[:, hs], kb[:, hs], TN) for hh, hs in enumerate(HEAD_LANES)]
            o = [_dot(a[hh], vb[:, hs]) + o_state[hh] for hh, hs in enumerate(HEAD_LANES)]
            ro = [(oh * lax.rsqrt(jnp.mean(oh * oh, axis=-1, keepdims=True) + RMS_EPS) * gov) * gate[:, hs]
                  for oh, hs in zip(o, HEAD_LANES)]
            for hh in range(HGRN_HEADS):
                st_ref[hh, c] = st[hh]
                s_scr[hh] = st_next[hh]
            mix_ref[sl, ATTN_W:ATTN_W + HGRN_W] = jnp.concatenate(ro, axis=1).astype(BF16)
            oraw_ref[sl, :] = jnp.concatenate(o, axis=1)
            return carry

        lax.fori_loop(0, cpb, step, 0, unroll=HGRN_UNROLL)

    col = lambda base: pl.BlockSpec((HGRN_ROWS, HGRN_W), lambda bi, i: (bi * nblk + i, base // HGRN_HEADS))
    out = pl.BlockSpec((HGRN_ROWS, HGRN_W), lambda bi, i: (bi * nblk + i, 0))
    return pl.pallas_call(
        body,
        name=name,
        grid=(b, nblk),
        in_specs=[col(HQ_COL), col(HF_COL), col(HI_COL), col(HG_COL), out,
                  pl.BlockSpec((1, HGRN_W), lambda bi, i: (0, 0)), pl.BlockSpec((1, HGRN_DH), lambda bi, i: (0, 0))],
        out_specs=[pl.BlockSpec((HGRN_ROWS, ATTN_W + HGRN_W), lambda bi, i: (bi * nblk + i, 0)), out,
                   pl.BlockSpec((None, HGRN_HEADS, cpb, HGRN_DH, HGRN_DH), lambda bi, i: (bi, 0, i, 0, 0))],
        out_shape=[jax.ShapeDtypeStruct((t, ATTN_W + HGRN_W), BF16), jax.ShapeDtypeStruct((t, HGRN_W), F32),
                   jax.ShapeDtypeStruct((b, HGRN_HEADS, nc, HGRN_DH, HGRN_DH), F32)],
        scratch_shapes=[pltpu.VMEM((HGRN_HEADS, HGRN_DH, HGRN_DH), F32)],
        compiler_params=_params("parallel", "arbitrary"),
    )(proj, proj, proj, proj, attn, lb, go)


def _hgrn_bwd(name, proj, dqkv, lb, go, oraw, states, dmix, b, s):
    t = b * s
    nblk = s // HGRN_ROWS
    cpb = HGRN_ROWS // CHUNK

    def body(hq_ref, hf_ref, hi_ref, hg_ref, dq_ref, dk_ref, dv_ref, lb_ref, go_ref, oraw_ref, st_ref, dro_ref,
             dp_ref, dlb_ref, dgo_ref, ds_scr, dlb_scr, dgo_scr):
        tril = _tri(True)
        ones_u = jnp.where(_tri(False), 1.0, 0.0).astype(BF16)
        gov = go_ref[...]
        dp_ref[:, 0:ATTN_W] = dq_ref[...]
        dp_ref[:, ATTN_W:2 * ATTN_W] = dk_ref[...]
        dp_ref[:, 2 * ATTN_W:3 * ATTN_W] = dv_ref[...]

        @pl.when(pl.program_id(1) == 0)
        def _():
            ds_scr[...] = jnp.zeros_like(ds_scr)
            dlb_scr[...] = jnp.zeros_like(dlb_scr)
            dgo_scr[...] = jnp.zeros_like(dgo_scr)

        def step(ci, carry):
            c = cpb - 1 - ci
            sl = pl.ds(pl.multiple_of(c * CHUNK, CHUNK), CHUNK)
            hq = hq_ref[sl, :]
            hg = hg_ref[sl, :]
            sig, f, bb, bl, bm, sq, q, k = _hgrn_chunk(hq, hf_ref[sl, :], lb_ref[...], tril)
            vb = hi_ref[sl, :].astype(BF16)
            ebm = jnp.exp(bb - bm)
            embm = jnp.exp(bm - bb)
            eb = jnp.exp(bb)
            ebl = jnp.exp(bl - bb)
            e_last = jnp.exp(bl)
            qe = (q * ebm).astype(BF16)
            ke = (k * embm).astype(BF16)
            qb = (q * eb).astype(BF16)
            kb = (k * ebl).astype(BF16)
            st = [st_ref[hh, c] for hh in range(HGRN_HEADS)]
            dst = [ds_scr[hh] for hh in range(HGRN_HEADS)]
            o = oraw_ref[sl, :]
            dro = dro_ref[sl, :]
            sg = _sigmoid(hg)
            gov4 = jnp.concatenate([gov] * HGRN_HEADS, axis=1)
            rstd = jnp.concatenate(
                [jnp.broadcast_to(lax.rsqrt(jnp.mean(o[:, hs] * o[:, hs], axis=-1, keepdims=True) + RMS_EPS),
                                  (CHUNK, HGRN_DH)) for hs in HEAD_LANES], axis=1)
            ohat = o * rstd
            dn = dro * (hg * sg)
            dhg = dro * (ohat * gov4) * (sg * (1.0 + hg * (1.0 - sg)))
            dgo_inc = jnp.sum(dn * ohat, axis=0, keepdims=True)
            dohat = dn * gov4
            proj_h = dohat * ohat
            pm = jnp.concatenate(
                [jnp.broadcast_to(jnp.mean(proj_h[:, hs], axis=-1, keepdims=True), (CHUNK, HGRN_DH))
                 for hs in HEAD_LANES], axis=1)
            dob = (rstd * (dohat - ohat * pm)).astype(BF16)
            stb = [x.astype(BF16) for x in st]
            dstb = [x.astype(BF16) for x in dst]
            a = [jnp.where(tril, _dot(qe[:, hs], ke[:, hs], NT), 0.0).astype(BF16) for hs in HEAD_LANES]
            dab = [jnp.where(tril, _dot(dob[:, hs], vb[:, hs], NT), 0.0).astype(BF16) for hs in HEAD_LANES]
            dqb = [_dot(dob[:, hs], stb[hh]) for hh, hs in enumerate(HEAD_LANES)]
            dkb = [_dot(vb[:, hs], dstb[hh]) for hh, hs in enumerate(HEAD_LANES)]
            dv_state = [_dot(kb[:, hs], dstb[hh], NT) for hh, hs in enumerate(HEAD_LANES)]
            dst_next = [dst[hh] * e_last[:, hs] + _dot(dob[:, hs], qb[:, hs], TN) for hh, hs in enumerate(HEAD_LANES)]
            dv = [_dot(a[hh], dob[:, hs], TN) + dv_state[hh] for hh, hs in enumerate(HEAD_LANES)]
            dqe = jnp.concatenate([_dot(dab[hh], ke[:, hs]) for hh, hs in enumerate(HEAD_LANES)], axis=1)
            dke = jnp.concatenate([_dot(dab[hh], qe[:, hs], TN) for hh, hs in enumerate(HEAD_LANES)], axis=1)
            dqb = jnp.concatenate(dqb, axis=1)
            dkb = jnp.concatenate(dkb, axis=1)
            state_term = jnp.concatenate(
                [jnp.sum(dst[hh] * st[hh], axis=0, keepdims=True) for hh in range(HGRN_HEADS)], axis=1)
            dq = dqe * ebm + dqb * eb
            dk = dke * embm + dkb * ebl
            db = (qe.astype(F32) * dqe - ke.astype(F32) * dke) + q * (dqb * eb) - k * (dkb * ebl)
            d_last = jnp.sum(k * ebl * dkb, axis=0, keepdims=True) + state_term * e_last
            dg = _dot_exact_lhs(ones_u, db) + d_last
            df = dg / f - dk
            first = HQ_COL * HGRN_DH
            dp_ref[sl, first:first + HGRN_W] = (dq * (sq * (1.0 + hq * (1.0 - sq)))).astype(BF16)
            dp_ref[sl, first + HGRN_W:first + 2 * HGRN_W] = (df * (1.0 - lb_ref[...]) * sig * (1.0 - sig)).astype(BF16)
            dp_ref[sl, first + 2 * HGRN_W:first + 3 * HGRN_W] = jnp.concatenate(dv, axis=1).astype(BF16)
            dp_ref[sl, first + 3 * HGRN_W:first + 4 * HGRN_W] = dhg.astype(BF16)
            dlb_scr[...] += jnp.sum(df * (1.0 - sig), axis=0, keepdims=True)
            dgo_scr[...] += dgo_inc
            for hh in range(HGRN_HEADS):
                ds_scr[hh] = dst_next[hh]
            return carry

        lax.fori_loop(0, cpb, step, 0, unroll=HGRN_UNROLL)

        @pl.when(pl.program_id(1) == nblk - 1)
        def _():
            dlb_ref[...] = dlb_scr[...]
            dgo_ref[...] = dgo_scr[...]

    rows = lambda bi, i: bi * nblk + (nblk - 1 - i)
    col = lambda base: pl.BlockSpec((HGRN_ROWS, HGRN_W), lambda bi, i: (rows(bi, i), base // HGRN_HEADS))
    out = pl.BlockSpec((HGRN_ROWS, HGRN_W), lambda bi, i: (rows(bi, i), 0))
    part = pl.BlockSpec((None, 1, HGRN_W), lambda bi, i: (bi, 0, 0))
    width = HG_COL * HGRN_DH + HGRN_W
    o_shape = jax.ShapeDtypeStruct((t, width), BF16)
    p_shape = jax.ShapeDtypeStruct((b, 1, HGRN_W), F32)
    return pl.pallas_call(
        body,
        name=name,
        grid=(b, nblk),
        in_specs=[col(HQ_COL), col(HF_COL), col(HI_COL), col(HG_COL), out, out, out,
                  pl.BlockSpec((1, HGRN_W), lambda bi, i: (0, 0)), pl.BlockSpec((1, HGRN_DH), lambda bi, i: (0, 0)), out,
                  pl.BlockSpec((None, HGRN_HEADS, cpb, HGRN_DH, HGRN_DH), lambda bi, i: (bi, 0, nblk - 1 - i, 0, 0)),
                  col(ATTN_W // HGRN_DH)],
        out_specs=[pl.BlockSpec((HGRN_ROWS, width), lambda bi, i: (rows(bi, i), 0))] + [part] * 2,
        out_shape=[o_shape] + [p_shape] * 2,
        scratch_shapes=[pltpu.VMEM((HGRN_HEADS, HGRN_DH, HGRN_DH), F32), pltpu.VMEM((1, HGRN_W), F32),
                        pltpu.VMEM((1, HGRN_W), F32)],
        compiler_params=_params("parallel", "arbitrary"),
    )(proj, proj, proj, proj, *dqkv, lb, go, oraw, states, dmix)


def _small_grads(name, dg1, dgm, dg2, dgq, dgk, dbe_t, dbo_t, dlb, dgo, lbp):
    d = dg1.shape[1]

    def body(dg1_ref, dgm_ref, dg2_ref, dgq_ref, dgk_ref, dbe_ref, dbo_ref, dlb_ref, dgo_ref, lbp_ref,
             g1_ref, gm_ref, g2_ref, gq_ref, gk_ref, rb_ref, lbg_ref, go_ref):
        g1_ref[...] = jnp.sum(dg1_ref[...], axis=0, keepdims=True)
        gm_ref[...] = jnp.sum(dgm_ref[...], axis=0, keepdims=True)
        g2_ref[...] = jnp.sum(dg2_ref[...], axis=0, keepdims=True)
        r = lax.broadcasted_iota(jnp.int32, (ATTN_W, ATTN_DH), 0)
        cidx = lax.broadcasted_iota(jnp.int32, (ATTN_W, ATTN_DH), 1)
        fold = jnp.where(jnp.bitwise_and(r, ATTN_DH - 1) == cidx, 1.0, 0.0).astype(BF16)
        gq_ref[...] = jnp.sum(_dot_exact_rhs(dgq_ref[...], fold), axis=0, keepdims=True)
        gk_ref[...] = jnp.sum(_dot_exact_rhs(dgk_ref[...], fold), axis=0, keepdims=True)
        gosum = jnp.sum(dgo_ref[...], axis=0, keepdims=True)
        go_ref[...] = (gosum[:, 0:HGRN_DH] + gosum[:, HGRN_DH:2 * HGRN_DH]
                       + gosum[:, 2 * HGRN_DH:3 * HGRN_DH] + gosum[:, 3 * HGRN_DH:4 * HGRN_DH])
        p0 = lbp_ref[0:1, :]
        p1 = lbp_ref[1:2, :]
        lbv = 1.0 / (1.0 + jnp.exp(p1 - p0))
        dp0 = jnp.sum(dlb_ref[...], axis=0, keepdims=True) * lbv * (1.0 - lbv)
        lbg_ref[0:1, :] = dp0
        lbg_ref[1:2, :] = -dp0
        acc = dbe_ref[CHUNK - 1] + pltpu.roll(dbo_ref[CHUNK - 1], DB_W - CHUNK, 1)
        for tq in range(CHUNK - 1):
            acc = acc + pltpu.roll(dbe_ref[tq], CHUNK - 1 - tq, 1) + pltpu.roll(dbo_ref[tq], DB_W - 1 - tq, 1)
        jidx = lax.broadcasted_iota(jnp.int32, (DB_W, N_REL_PAD), 0)
        ridx = lax.broadcasted_iota(jnp.int32, (DB_W, N_REL_PAD), 1)
        rel = jnp.clip(KPAD + CHUNK - 1 - jidx, -REL_CLIP, REL_CLIP) + REL_CLIP
        rb_ref[...] = _dot_exact_rhs(acc, jnp.where(rel == ridx, 1.0, 0.0).astype(BF16))

    ins = [dg1, dgm, dg2, dgq, dgk, dbe_t, dbo_t, dlb, dgo, lbp]
    outs = [jax.ShapeDtypeStruct((1, d), F32)] * 3 + [jax.ShapeDtypeStruct((1, ATTN_DH), F32)] * 2 + [
        jax.ShapeDtypeStruct((ATTN_HEADS, N_REL_PAD), F32), jax.ShapeDtypeStruct((2, HGRN_W), F32),
        jax.ShapeDtypeStruct((1, HGRN_DH), F32)]
    vm = pl.BlockSpec(memory_space=pltpu.VMEM)
    return pl.pallas_call(
        body,
        name=name,
        in_specs=[vm] * len(ins),
        out_specs=[vm] * len(outs),
        out_shape=outs,
        compiler_params=pltpu.CompilerParams(vmem_limit_bytes=VMEM_LIMIT),
    )(*ins)


def _adam_update(w, g, m, v):
    m2 = ADAM_B1 * m + (1.0 - ADAM_B1) * g
    v2 = ADAM_B2 * v + (1.0 - ADAM_B2) * (g * g)
    m_hat = m2 / (1.0 - ADAM_B1 ** ADAM_STEP)
    v_hat = v2 / (1.0 - ADAM_B2 ** ADAM_STEP)
    delta = -ADAM_LR * (m_hat / (jnp.sqrt(v_hat) + ADAM_EPS) + ADAM_WD * w)
    return delta, m2, v2


def _rows_tile(r):
    return r if r <= 512 or r % 512 else 512


def _pair_sum(name, grad, theirs, core):
    n, half, c = theirs.shape
    tr = _rows_tile(half)
    nth = half // tr

    def body(core_ref, a_ref, b_ref, o_ref):
        o_ref[...] = (a_ref[...].astype(F32) + b_ref[...].astype(F32)).astype(o_ref.dtype)

    spec = pl.BlockSpec((None, tr, c), lambda i, j, core_ref: (i, j, 0))
    return pl.pallas_call(
        body, name=name,
        grid_spec=pltpu.PrefetchScalarGridSpec(
            num_scalar_prefetch=1, grid=(n, nth),
            in_specs=[pl.BlockSpec((None, tr, c), lambda i, j, core_ref: (i, core_ref[0] * nth + j, 0)), spec],
            out_specs=spec),
        out_shape=pltpu.HBM((n, half, c), BF16), compiler_params=_params("parallel", "parallel"),
    )(core, grad, theirs)


def _chip_sum(name, own, parts, chip):
    _, half, c = own.shape
    tr = _rows_tile(half)

    def body(chip_ref, own_ref, p_ref, o_ref):
        me = chip_ref[0]
        mine = own_ref[...].astype(F32)
        flip_x, flip_y, flip_xy = (p_ref[i].astype(F32) for i in range(3))
        acc = None
        for k in range(N_CHIPS):
            rel = jnp.bitwise_xor(me, k)
            term = jnp.where(rel == 0, mine, jnp.where(rel == 2, flip_x, jnp.where(rel == 1, flip_y, flip_xy)))
            acc = term if acc is None else acc + term
        o_ref[...] = acc

    return pl.pallas_call(
        body, name=name,
        grid_spec=pltpu.PrefetchScalarGridSpec(
            num_scalar_prefetch=1, grid=(half // tr,),
            in_specs=[pl.BlockSpec((None, tr, c), lambda j, chip_ref: (chip_ref[0], j, 0)),
                      pl.BlockSpec((3, tr, c), lambda j, chip_ref: (0, j, 0))],
            out_specs=pl.BlockSpec((tr, c), lambda j, chip_ref: (j, 0))),
        out_shape=pltpu.HBM((half, c), F32), compiler_params=_params("parallel"),
    )(chip, own, parts)


def _adamw(name, w, g_mine, g_theirs, m, v, core):
    _, r, c = w.shape
    half = r // 2
    tr = _rows_tile(half)
    nth = half // tr

    def body(core_ref, w_ref, gm_ref, gt_ref, m_ref, v_ref, g_ref, d_ref, m2_ref, v2_ref):
        g = jnp.where(pl.program_id(0) == core_ref[0], gm_ref[...], gt_ref[...])
        delta, m2, v2 = _adam_update(w_ref[...], g, m_ref[...], v_ref[...])
        g_ref[...] = g
        d_ref[...] = delta
        m2_ref[...] = m2
        v2_ref[...] = v2

    full = pl.BlockSpec((None, tr, c), lambda h, j, core_ref: (0, h * nth + j, 0))
    part = pl.BlockSpec((tr, c), lambda h, j, core_ref: (j, 0))
    shape = jax.ShapeDtypeStruct((1, r, c), F32)
    return pl.pallas_call(
        body, name=name,
        grid_spec=pltpu.PrefetchScalarGridSpec(
            num_scalar_prefetch=1, grid=(2, nth), in_specs=[full, part, part, full, full], out_specs=[full] * 4),
        out_shape=[shape] * 4, compiler_params=_params("parallel", "parallel"),
    )(core, w, g_mine, g_theirs, m, v)


def _rel_bias_table(name, rel_bias):
    padded = jnp.pad(rel_bias, ((0, 0), (0, N_REL_PAD - N_REL)))

    def body(rb_ref, o_ref):
        ridx = lax.broadcasted_iota(jnp.int32, (N_REL_PAD, BAND), 0)
        sidx = lax.broadcasted_iota(jnp.int32, (N_REL_PAD, BAND), 1)
        rb = rb_ref[...]

        def step(tq, carry):
            rel = jnp.clip(tq + KPAD - sidx, -REL_CLIP, REL_CLIP) + REL_CLIP
            onehot = jnp.where(rel == ridx, 1.0, 0.0).astype(BF16)
            o_ref[tq] = _dot_exact_rhs(rb, onehot)
            return carry

        lax.fori_loop(0, CHUNK, step, 0)

    vm = pl.BlockSpec(memory_space=pltpu.VMEM)
    table = pl.pallas_call(
        body, name=name, in_specs=[vm], out_specs=vm,
        out_shape=jax.ShapeDtypeStruct((CHUNK, ATTN_HEADS, BAND), F32),
    )(padded)
    return table.transpose(1, 0, 2)


def _adamw_small(name, w, parts, m, v):
    def body(w_ref, p_ref, m_ref, v_ref, g_ref, d_ref, m2_ref, v2_ref):
        g = p_ref[0]
        for i in range(1, N_DEV):
            g = g + p_ref[i]
        delta, m2, v2 = _adam_update(w_ref[...], g, m_ref[...], v_ref[...])
        g_ref[...] = g
        d_ref[...] = delta
        m2_ref[...] = m2
        v2_ref[...] = v2

    vm = pl.BlockSpec(memory_space=pltpu.VMEM)
    shape = jax.ShapeDtypeStruct((SMALL_ROWS, SMALL_COLS), F32)
    return pl.pallas_call(
        body, name=name, in_specs=[vm] * 4, out_specs=[vm] * 4, out_shape=[shape] * 4,
    )(w, parts, m, v)


def _position():
    return lax.axis_index("x"), lax.axis_index("y"), lax.axis_index("c")


def _other_chips(x, y):
    return [(1 - x, y), (x, 1 - y), (1 - x, 1 - y)]


ANY = pl.BlockSpec(memory_space=pl.ANY)
PAIR_ID = 0


def _pair_handshake():
    x, y, c = _position()
    barrier = pltpu.get_barrier_semaphore()
    pl.semaphore_signal(barrier, inc=1, device_id=(x, y, 1 - c), device_id_type=MESH)
    pl.semaphore_wait(barrier, 1)


PAIR_CALL = pltpu.CompilerParams(collective_id=PAIR_ID)


HBM = pl.BlockSpec(memory_space=pltpu.HBM)
SEM = pl.BlockSpec(memory_space=pltpu.SEMAPHORE)
SPLIT_COPY = pltpu.SideEffectType.DATAFLOW_SIDE_EFFECTING


def _gather_copy(shards, outs, send_sem, recv_sem, i, j):
    x, y, c = _position()
    chips = _other_chips(x, y)
    half = shards[i].shape[0] // 2
    rows = pl.ds(pl.multiple_of(c * half, 16), half)
    return pltpu.make_async_remote_copy(
        src_ref=shards[i].at[rows, :], dst_ref=outs[i].at[2 * x + y, rows, :],
        send_sem=send_sem.at[3 * i + j], recv_sem=recv_sem.at[3 * i + j],
        device_id=(chips[j][0], chips[j][1], c), device_id_type=MESH)


def _gather_start(name, shards, after):
    n = len(shards)

    def body(*refs):
        srcs, outs = refs[:n], refs[n:2 * n]
        send_sem, recv_sem = refs[2 * n + len(after)], refs[2 * n + len(after) + 1]
        token = refs[-1]
        for i in range(n):
            for j in range(3):
                _gather_copy(srcs, outs, send_sem, recv_sem, i, j).start()
        token[...] = jnp.zeros_like(token)

    full = [(N_CHIPS,) + s.shape for s in shards]
    res = pl.pallas_call(
        body,
        name=name,
        in_specs=[HBM] * (2 * n) + [ANY] * len(after),
        out_specs=[SEM, SEM] + [HBM] * (2 * n) + [pl.BlockSpec(memory_space=pltpu.VMEM)],
        out_shape=[pltpu.SemaphoreType.DMA((3 * n,)), pltpu.SemaphoreType.DMA((3 * n,))]
        + [pltpu.HBM(s.shape, s.dtype) for s in shards]
        + [pltpu.HBM(shp, s.dtype) for shp, s in zip(full, shards)]
        + [jax.ShapeDtypeStruct((8, LANES), F32)],
        input_output_aliases={i: 2 + i for i in range(2 * n)},
        compiler_params=pltpu.CompilerParams(has_side_effects=SPLIT_COPY),
    )(*[pltpu.with_memory_space_constraint(s, pltpu.HBM) for s in shards],
      *[pltpu.with_memory_space_constraint(lax.empty(shp, s.dtype), pltpu.HBM) for shp, s in zip(full, shards)],
      *after)
    return res[0], res[1], list(res[2:2 + n]), list(res[2 + n:2 + 2 * n]), res[-1]


def _gather_wait(name, send_sem, recv_sem, shards, outs, after):
    n = len(shards)

    def body(*refs):
        srcs, out_refs = refs[:n], refs[n:2 * n]
        send_ref, recv_ref = refs[2 * n], refs[2 * n + 1]
        for i in range(n):
            for j in range(3):
                copy = _gather_copy(srcs, out_refs, send_ref, recv_ref, i, j)
                copy.wait_send()
                copy.wait_recv()

    res = pl.pallas_call(
        body,
        name=name,
        in_specs=[HBM] * (2 * n) + [SEM, SEM] + [ANY] * len(after),
        out_specs=[HBM] * (2 * n),
        out_shape=[pltpu.HBM(s.shape, s.dtype) for s in shards] + [pltpu.HBM(o.shape, o.dtype) for o in outs],
        input_output_aliases={i: i for i in range(2 * n)},
        compiler_params=pltpu.CompilerParams(has_side_effects=SPLIT_COPY),
    )(*shards, *outs, send_sem, recv_sem, *after)
    return list(res[:n]), list(res[n:])


def _join_copies(srcs, ins, outs, own_send, own_recv, half_send, half_recv):
    x, y, c = _position()
    chips = _other_chips(x, y)
    copies = []
    for i in range(len(srcs)):
        copies.append(pltpu.make_async_remote_copy(
            src_ref=srcs[i], dst_ref=outs[i].at[2 * x + y], send_sem=own_send.at[i], recv_sem=own_recv.at[i],
            device_id=(x, y, 1 - c), device_id_type=MESH))
        half = srcs[i].shape[0] // 2
        rows = pl.ds(pl.multiple_of(c * half, 16), half)
        for j in range(3):
            slot = 2 * chips[j][0] + chips[j][1]
            copies.append(pltpu.make_async_remote_copy(
                src_ref=ins[i].at[slot, rows, :], dst_ref=outs[i].at[slot, rows, :],
                send_sem=half_send.at[3 * i + j], recv_sem=half_recv.at[3 * i + j],
                device_id=(x, y, 1 - c), device_id_type=MESH))
    return copies


def _gather_join(name, shards, outs):
    n = len(shards)

    def body(*refs):
        _pair_handshake()
        copies = _join_copies(refs[:n], refs[n:2 * n], refs[2 * n:3 * n], *refs[3 * n:])
        for cp in copies:
            cp.start()
        for cp in copies:
            cp.wait()

    return pl.pallas_call(
        body,
        name=name,
        in_specs=[ANY] * (2 * n),
        out_specs=[HBM] * n,
        out_shape=[pltpu.HBM(o.shape, o.dtype) for o in outs],
        input_output_aliases={n + i: i for i in range(n)},
        scratch_shapes=[pltpu.SemaphoreType.DMA((n,))] * 2 + [pltpu.SemaphoreType.DMA((3 * n,))] * 2,
        compiler_params=PAIR_CALL,
    )(*shards, *outs)


def _join_start(name, shards, outs):
    n = len(shards)

    def body(*refs):
        _pair_handshake()
        srcs, arrs = refs[:n], refs[n:2 * n]
        sems = refs[2 * n:2 * n + 4]
        token = refs[-1]
        for cp in _join_copies(srcs, arrs, arrs, *sems):
            cp.start()
        token[...] = jnp.zeros_like(token)

    res = pl.pallas_call(
        body,
        name=name,
        in_specs=[HBM] * (2 * n),
        out_specs=[SEM] * 4 + [HBM] * (2 * n) + [pl.BlockSpec(memory_space=pltpu.VMEM)],
        out_shape=[pltpu.SemaphoreType.DMA((n,))] * 2 + [pltpu.SemaphoreType.DMA((3 * n,))] * 2
        + [pltpu.HBM(s.shape, s.dtype) for s in shards] + [pltpu.HBM(o.shape, o.dtype) for o in outs]
        + [jax.ShapeDtypeStruct((8, LANES), F32)],
        input_output_aliases={i: 4 + i for i in range(2 * n)},
        compiler_params=pltpu.CompilerParams(has_side_effects=SPLIT_COPY, collective_id=PAIR_ID),
    )(*shards, *outs)
    return list(res[:4]), list(res[4:4 + n]), list(res[4 + n:4 + 2 * n]), res[-1]


def _join_wait(name, sems, shards, outs, after):
    n = len(shards)

    def body(*refs):
        srcs, arrs = refs[:n], refs[n:2 * n]
        for cp in _join_copies(srcs, arrs, arrs, *refs[2 * n:2 * n + 4]):
            cp.wait_send()
            cp.wait_recv()

    res = pl.pallas_call(
        body,
        name=name,
        in_specs=[HBM] * (2 * n) + [SEM] * 4 + [ANY] * len(after),
        out_specs=[HBM] * (2 * n),
        out_shape=[pltpu.HBM(s.shape, s.dtype) for s in shards] + [pltpu.HBM(o.shape, o.dtype) for o in outs],
        input_output_aliases={i: i for i in range(2 * n)},
        compiler_params=pltpu.CompilerParams(has_side_effects=SPLIT_COPY),
    )(*shards, *outs, *sems, *after)
    return list(res[n:])


def _pair_copy(grads, lands, send_sem, recv_sem, i):
    x, y, c = _position()
    half = grads[i].shape[1] // 2
    give = pl.ds(pl.multiple_of((1 - c) * half, 16), half)
    return pltpu.make_async_remote_copy(
        src_ref=grads[i].at[:, give, :], dst_ref=lands[i], send_sem=send_sem.at[i], recv_sem=recv_sem.at[i],
        device_id=(x, y, 1 - c), device_id_type=MESH)


def _pair_start(name, grads):
    n = len(grads)

    def body(*refs):
        _pair_handshake()
        srcs, lands = refs[:n], refs[n:2 * n]
        send_sem, recv_sem = refs[2 * n], refs[2 * n + 1]
        token = refs[-1]
        for i in range(n):
            _pair_copy(srcs, lands, send_sem, recv_sem, i).start()
        token[...] = jnp.zeros_like(token)

    halves = [(g.shape[0], g.shape[1] // 2, g.shape[2]) for g in grads]
    res = pl.pallas_call(
        body,
        name=name,
        in_specs=[HBM] * (2 * n),
        out_specs=[SEM, SEM] + [HBM] * (2 * n) + [pl.BlockSpec(memory_space=pltpu.VMEM)],
        out_shape=[pltpu.SemaphoreType.DMA((n,)), pltpu.SemaphoreType.DMA((n,))]
        + [pltpu.HBM(g.shape, g.dtype) for g in grads]
        + [pltpu.HBM(shp, g.dtype) for shp, g in zip(halves, grads)]
        + [jax.ShapeDtypeStruct((8, LANES), F32)],
        input_output_aliases={i: 2 + i for i in range(2 * n)},
        compiler_params=pltpu.CompilerParams(has_side_effects=SPLIT_COPY, collective_id=PAIR_ID),
    )(*[pltpu.with_memory_space_constraint(g, pltpu.HBM) for g in grads],
      *[pltpu.with_memory_space_constraint(lax.empty(shp, g.dtype), pltpu.HBM) for shp, g in zip(halves, grads)])
    return res[0], res[1], list(res[2:2 + n]), list(res[2 + n:2 + 2 * n]), res[-1]


def _pair_wait(name, send_sem, recv_sem, grads, lands, after):
    n = len(grads)

    def body(*refs):
        srcs, land_refs = refs[:n], refs[n:2 * n]
        send_ref, recv_ref = refs[2 * n], refs[2 * n + 1]
        for i in range(n):
            copy = _pair_copy(srcs, land_refs, send_ref, recv_ref, i)
            copy.wait_send()
            copy.wait_recv()

    res = pl.pallas_call(
        body,
        name=name,
        in_specs=[HBM] * (2 * n) + [SEM, SEM, ANY],
        out_specs=[HBM] * (2 * n),
        out_shape=[pltpu.HBM(g.shape, g.dtype) for g in grads] + [pltpu.HBM(l.shape, l.dtype) for l in lands],
        input_output_aliases={i: i for i in range(2 * n)},
        compiler_params=pltpu.CompilerParams(has_side_effects=SPLIT_COPY),
    )(*grads, *lands, send_sem, recv_sem, after)
    return list(res[:n]), list(res[n:])


def _scatter_copy(srcs, lands, send_sem, recv_sem, i, j):
    x, y, c = _position()
    chips = _other_chips(x, y)
    return pltpu.make_async_remote_copy(
        src_ref=srcs[i].at[2 * chips[j][0] + chips[j][1]], dst_ref=lands[i].at[j],
        send_sem=send_sem.at[3 * i + j], recv_sem=recv_sem.at[3 * i + j],
        device_id=(chips[j][0], chips[j][1], c), device_id_type=MESH)


def _scatter_start(name, sums):
    n = len(sums)

    def body(*refs):
        srcs, lands = refs[:n], refs[n:2 * n]
        send_sem, recv_sem = refs[2 * n], refs[2 * n + 1]
        token = refs[-1]
        for i in range(n):
            for j in range(3):
                _scatter_copy(srcs, lands, send_sem, recv_sem, i, j).start()
        token[...] = jnp.zeros_like(token)

    land_shapes = [(3,) + s.shape[1:] for s in sums]
    res = pl.pallas_call(
        body,
        name=name,
        in_specs=[HBM] * (2 * n),
        out_specs=[SEM, SEM] + [HBM] * (2 * n) + [pl.BlockSpec(memory_space=pltpu.VMEM)],
        out_shape=[pltpu.SemaphoreType.DMA((3 * n,)), pltpu.SemaphoreType.DMA((3 * n,))]
        + [pltpu.HBM(s.shape, s.dtype) for s in sums]
        + [pltpu.HBM(shp, s.dtype) for shp, s in zip(land_shapes, sums)]
        + [jax.ShapeDtypeStruct((8, LANES), F32)],
        input_output_aliases={i: 2 + i for i in range(2 * n)},
        compiler_params=pltpu.CompilerParams(has_side_effects=SPLIT_COPY),
    )(*[pltpu.with_memory_space_constraint(s, pltpu.HBM) for s in sums],
      *[pltpu.with_memory_space_constraint(lax.empty(shp, s.dtype), pltpu.HBM) for shp, s in zip(land_shapes, sums)])
    return res[0], res[1], list(res[2:2 + n]), list(res[2 + n:2 + 2 * n]), res[-1]


def _scatter_wait(name, send_sem, recv_sem, sums, lands, after):
    n = len(sums)

    def body(*refs):
        srcs, land_refs = refs[:n], refs[n:2 * n]
        send_ref, recv_ref = refs[2 * n], refs[2 * n + 1]
        for i in range(n):
            for j in range(3):
                copy = _scatter_copy(srcs, land_refs, send_ref, recv_ref, i, j)
                copy.wait_send()
                copy.wait_recv()

    res = pl.pallas_call(
        body,
        name=name,
        in_specs=[HBM] * (2 * n) + [SEM, SEM, ANY],
        out_specs=[HBM] * (2 * n),
        out_shape=[pltpu.HBM(s.shape, s.dtype) for s in sums] + [pltpu.HBM(l.shape, l.dtype) for l in lands],
        input_output_aliases={i: i for i in range(2 * n)},
        compiler_params=pltpu.CompilerParams(has_side_effects=SPLIT_COPY),
    )(*sums, *lands, send_sem, recv_sem, after)
    return list(res[:n]), list(res[n:])


def _pair_join(name, halves, small=None):
    n = len(halves)
    if small is None:
        def body_plain(*refs):
            _pair_handshake()
            ins, outs = refs[:n], refs[n:2 * n]
            send_sem, recv_sem = refs[2 * n:]
            x, y, c = _position()
            swaps = [pltpu.make_async_remote_copy(
                src_ref=ins[i], dst_ref=outs[i], send_sem=send_sem.at[i], recv_sem=recv_sem.at[i],
                device_id=(x, y, 1 - c), device_id_type=MESH) for i in range(n)]
            for swap in swaps:
                swap.start()
            for swap in swaps:
                swap.wait()

        return pl.pallas_call(
            body_plain,
            name=name,
            in_specs=[ANY] * n,
            out_specs=[ANY] * n,
            out_shape=[jax.ShapeDtypeStruct(h.shape, h.dtype) for h in halves],
            scratch_shapes=[pltpu.SemaphoreType.DMA((n,))] * 2,
            compiler_params=PAIR_CALL,
        )(*halves)

    def body(*refs):
        ins, small_ref = refs[:n], refs[n]
        outs, all_ref = refs[n + 1:2 * n + 1], refs[2 * n + 1]
        send_sem, recv_sem, sm_send, sm_recv, sm_local = refs[2 * n + 2:]
        x, y, c = _position()
        swaps = []
        for i in range(n):
            swap = pltpu.make_async_remote_copy(
                src_ref=ins[i], dst_ref=outs[i], send_sem=send_sem.at[i], recv_sem=recv_sem.at[i],
                device_id=(x, y, 1 - c), device_id_type=MESH)
            swap.start()
            swaps.append(swap)
        me = 4 * x + 2 * y + c
        sm_own = pltpu.make_async_copy(small_ref, all_ref.at[me], sm_local)
        sm_own.start()
        pushes, arrivals = [], []
        for mask in range(1, N_DEV):
            px, py, pc = x ^ (mask >> 2), y ^ ((mask >> 1) & 1), c ^ (mask & 1)
            pushes.append(pltpu.make_async_remote_copy(
                src_ref=small_ref, dst_ref=all_ref.at[me], send_sem=sm_send.at[mask - 1], recv_sem=sm_recv.at[mask - 1],
                device_id=(px, py, pc), device_id_type=MESH))
            arrivals.append(pltpu.make_async_remote_copy(
                src_ref=small_ref, dst_ref=all_ref.at[4 * px + 2 * py + pc], send_sem=sm_send.at[mask - 1],
                recv_sem=sm_recv.at[mask - 1], device_id=(px, py, pc), device_id_type=MESH))
        for cp in pushes:
            cp.start()
        for swap in swaps:
            swap.wait()
        for cp in arrivals:
            cp.wait_recv()
        for cp in pushes:
            cp.wait_send()
        sm_own.wait()

    res = pl.pallas_call(
        body,
        name=name,
        in_specs=[ANY] * (n + 1),
        out_specs=[ANY] * (n + 1),
        out_shape=[jax.ShapeDtypeStruct(h.shape, h.dtype) for h in halves]
        + [jax.ShapeDtypeStruct((N_DEV,) + small.shape, small.dtype)],
        scratch_shapes=[pltpu.SemaphoreType.DMA((n,))] * 2 + [pltpu.SemaphoreType.DMA((N_DEV - 1,))] * 2
        + [pltpu.SemaphoreType.DMA(())],
    )(*halves, small)
    return res[:n], res[n]


def _lower_bound(lbp):
    return jax.nn.softmax(lbp, axis=0)[0:1]


def _local_step(x, target, g1, gm, g2, gq, gk, go, rel_bias, lbp, weights, on_grads, grads_sent):
    b, s, d = x.shape
    t = b * s
    x0 = x.reshape(t, d)
    tgt = target.reshape(t, d)
    gq_t = jnp.tile(gq, (1, ATTN_HEADS))
    gk_t = jnp.tile(gk, (1, ATTN_HEADS))
    lb = _lower_bound(lbp)
    table = _band_table(_rel_bias_table("rel_bias_table", rel_bias))

    h1 = _rmsnorm_fwd("norm1", x0, g1)
    wg1, wu1, deps1 = weights["first"]((h1, table))
    a1, b1, z1 = _ffn_up("ffn1_up", h1, wg1, wu1, deps1)
    wd1, deps_mid = weights["mid"]((z1,))
    x1, h2 = _ffn_down("ffn1_down", z1, wd1, x0, gm, deps_mid)
    w_in, w_out = weights["mid_rest"]((x1,))
    ns = w_in.shape[0]
    proj = _in_proj("in_proj", h2, w_in)
    proj3 = proj.reshape(b, s, proj.shape[1])
    qn, kn, vb = _qk_prep("qk_prep", proj3, gq_t, gk_t)
    attn = _attn_fwd("attn_fwd", qn, kn, vb, table, weights["last_begin"]((qn,))).reshape(t, ATTN_W)
    mix, oraw, states = _hgrn_fwd("hgrn_fwd", proj, attn, lb, go, b, s)
    x2, h3 = _out_proj("out_proj", mix, w_out, x1, g2)
    wg2, wu2, wd2 = weights["last"]((h3,))
    a2, b2, z2 = _ffn_up("ffn2_up", h3, wg2, wu2)
    dy, dyh, sq = _ffn_down_loss("ffn2_down_loss", z2, wd2, x2, tgt)
    loss = 0.5 * jnp.sum(sq) / d

    da2, db2 = _ffn_bwd_act("ffn2_bwd_act", dyh, wd2, a2, b2)
    dwd2 = _grad_w_cols("ffn2_dwd", z2, dyh)
    dwg2 = _grad_w_cols("ffn2_dwg", da2, h3)
    dwu2 = _grad_w_cols("ffn2_dwu", db2, h3)
    sent2 = on_grads("ffn2", {"ffn2_w_gate": dwg2, "ffn2_w_up": dwu2, "ffn2_w_down": dwd2})
    dx2, dx2b, dg2 = _ffn_bwd_in("ffn2_bwd_in", da2, db2, wg2, wu2, x2, g2, dy, 1.0, sent2)
    sent2 = grads_sent("ffn2", dx2b)

    dwout = _grad_w_out("dw_out", mix, dx2b)
    dmix = _out_proj_bwd("out_proj_bwd", dx2b, w_out, sent2)
    dqn, dkn, dvn, dbe, dbo = _attn_bwd("attn_bwd", qn, kn, vb, table, dmix.reshape(b, s, dmix.shape[1]))
    dpq, dpk, dpv, dgq, dgk = _qk_prep_bwd("qk_prep_bwd", proj3, dqn, dkn, dvn, gq_t, gk_t)
    dpq, dpk, dpv = (a.reshape(t, ATTN_W) for a in (dpq, dpk, dpv))
    dproj, dlb, dgo = _hgrn_bwd("hgrn_bwd", proj, (dpq, dpk, dpv), lb, go, oraw, states, dmix, b, s)
    dwin = _grad_w_in("dw_in", h2, dproj, ns)
    dx1, dx1h, dgm = _in_proj_bwd("in_proj_bwd", dproj, w_in, x1, gm, dx2, 0.5)

    dwd1 = _grad_w_cols("ffn1_dwd", z1, dx1h)
    sent_mix = on_grads("mix", {"w_in": dwin, "w_out": dwout.reshape(ns, dwout.shape[0] // ns, d),
                                "ffn1_w_down": dwd1})
    da1, db1 = _ffn_bwd_act("ffn1_bwd_act", dx1h, wd1, a1, b1, sent_mix)
    sent_mix = grads_sent("mix", da1)
    dwg1 = _grad_w_cols("ffn1_dwg", da1, h1, sent_mix)
    dwu1 = _grad_w_cols("ffn1_dwu", db1, h1)
    on_grads("ffn1", {"ffn1_w_gate": dwg1, "ffn1_w_up": dwu1})
    sent1 = grads_sent("ffn1", None)
    dx0, dg1 = _ffn_bwd_in("ffn1_bwd_in", da1, db1, wg1, wu1, x0, g1, dx1, None, sent1)

    nt = dg1.shape[0]
    sg = _small_grads(
        "small_grads", dg1.reshape(nt, d), dgm.reshape(nt, d), dg2.reshape(nt, d),
        dgq.reshape(-1, ATTN_W), dgk.reshape(-1, ATTN_W), dbe.transpose(1, 0, 2), dbo.transpose(1, 0, 2),
        dlb.reshape(b, HGRN_W), dgo.reshape(b, HGRN_W), lbp)
    g1g, gmg, g2g, gqg, gkg, rbg, lbg, gog = sg
    small = _pack_small(g1g, gmg, g2g, lbg, rbg[:, :N_REL], gqg, gkg, gog, loss)
    return dx0.reshape(b, s, d), small


LOSS_SLOT = 7 * SMALL_COLS + 2 * ATTN_DH + HGRN_DH


def _pack_small(g1, gm, g2, lbp, rel_bias, gq, gk, go, loss=None):
    flat = [g1.reshape(-1), gm.reshape(-1), g2.reshape(-1), lbp.reshape(-1), rel_bias.reshape(-1)]
    n_bias = 3 * SMALL_COLS - rel_bias.size
    heads = [gq.reshape(-1), gk.reshape(-1), go.reshape(-1)]
    heads.append(jnp.zeros((1,), F32) if loss is None else loss.reshape(1))
    n_tail = SMALL_COLS - sum(h.size for h in heads)
    return jnp.concatenate(flat + [jnp.zeros((n_bias,), F32)] + heads + [jnp.zeros((n_tail,), F32)]).reshape(
        SMALL_ROWS, SMALL_COLS)


def _unpack_small(p, d):
    flat = p.reshape(-1)
    o = 3 * d
    g1, gm, g2 = p[0:1], p[1:2], p[2:3]
    lbp = flat[o:o + 2 * HGRN_W].reshape(2, HGRN_W)
    o = 4 * SMALL_COLS
    rel = flat[o:o + ATTN_HEADS * N_REL].reshape(1, ATTN_HEADS, N_REL)
    o = 7 * SMALL_COLS
    gq = flat[o:o + ATTN_DH].reshape(1, ATTN_DH)
    gk = flat[o + ATTN_DH:o + 2 * ATTN_DH].reshape(1, ATTN_DH)
    go = flat[o + 2 * ATTN_DH:o + 2 * ATTN_DH + HGRN_DH].reshape(1, HGRN_DH)
    return g1, gm, g2, gq, gk, rel, lbp, go


def kernel(x, ffn1_norm_g, ffn1_w_gate, ffn1_w_up, ffn1_w_down, mix_norm_g, w_in, attn_q_norm_g, attn_k_norm_g, attn_rel_bias, hgrn_lower_bounds, hgrn_out_norm_g, w_out, ffn2_norm_g, ffn2_w_gate, ffn2_w_up, ffn2_w_down, loss_target, m_ffn1_norm_g, m_ffn1_w_gate, m_ffn1_w_up, m_ffn1_w_down, m_mix_norm_g, m_w_in, m_attn_q_norm_g, m_attn_k_norm_g, m_attn_rel_bias, m_hgrn_lower_bounds, m_hgrn_out_norm_g, m_w_out, m_ffn2_norm_g, m_ffn2_w_gate, m_ffn2_w_up, m_ffn2_w_down, v_ffn1_norm_g, v_ffn1_w_gate, v_ffn1_w_up, v_ffn1_w_down, v_mix_norm_g, v_w_in, v_attn_q_norm_g, v_attn_k_norm_g, v_attn_rel_bias, v_hgrn_lower_bounds, v_hgrn_out_norm_g, v_w_out, v_ffn2_norm_g, v_ffn2_w_gate, v_ffn2_w_up, v_ffn2_w_down):
    d = x.shape[-1]
    big_w = [ffn1_w_gate, ffn1_w_up, ffn1_w_down, w_in, w_out, ffn2_w_gate, ffn2_w_up, ffn2_w_down]
    big_m = [m_ffn1_w_gate, m_ffn1_w_up, m_ffn1_w_down, m_w_in, m_w_out, m_ffn2_w_gate, m_ffn2_w_up, m_ffn2_w_down]
    big_v = [v_ffn1_w_gate, v_ffn1_w_up, v_ffn1_w_down, v_w_in, v_w_out, v_ffn2_w_gate, v_ffn2_w_up, v_ffn2_w_down]
    big_names = ["ffn1_w_gate", "ffn1_w_up", "ffn1_w_down", "w_in", "w_out", "ffn2_w_gate", "ffn2_w_up", "ffn2_w_down"]
    flipped = {nm for nm in big_names if nm.endswith("gate") or nm.endswith("up")}
    flip = lambda nm, a: jnp.swapaxes(a, 1, 2) if nm in flipped else a
    big_w, big_m, big_v = ([flip(nm, a) for nm, a in zip(big_names, arrs)] for arrs in (big_w, big_m, big_v))

    shards = [w[0].astype(BF16) for w in big_w]
    start_a = _gather_start("gather_start_up1", shards[:2], ())
    start_b = _gather_start("gather_start_mid", shards[2:5], (start_a[4],))
    start_c = _gather_start("gather_start_ffn2", shards[5:], (start_b[4],))

    pending = {}

    def arrived(tag, started, after):
        send_sem, recv_sem, srcs, outs, _ = started
        return _gather_wait("gather_wait_" + tag, send_sem, recv_sem, srcs, outs, after)

    def first_weights(after):
        return (*_gather_join("gather_join_up1", *arrived("up1", start_a, after)), (start_c[4],))

    def mid_weights(after):
        srcs, outs = arrived("mid", start_b, after)
        (wd1,) = _gather_join("gather_join_wd1", srcs[:1], outs[:1])
        pending["mid"] = _join_start("join_start_mid", srcs[1:], outs[1:])
        return wd1, (pending["mid"][3],)

    def mid_rest(after):
        sems, srcs, outs, _ = pending["mid"]
        win_f, wout_f = _join_wait("join_wait_mid", sems, srcs, outs, after)
        return win_f, wout_f.reshape(wout_f.shape[0] * wout_f.shape[1], d)

    def last_begin(after):
        pending["ffn2"] = _join_start("join_start_ffn2", *arrived("ffn2", start_c, after))
        return (pending["ffn2"][3],)

    def last_weights(after):
        sems, srcs, outs, _ = pending["ffn2"]
        return _join_wait("join_wait_ffn2", sems, srcs, outs, after)

    weights = {"first": first_weights, "mid": mid_weights, "mid_rest": mid_rest, "last_begin": last_begin,
               "last": last_weights}

    core = lax.axis_index("c").astype(jnp.int32).reshape(1)
    chip = (2 * lax.axis_index("x") + lax.axis_index("y")).astype(jnp.int32).reshape(1)
    started = {}

    def on_grads(tag, grads):
        names = list(grads)
        started[tag] = (names, _pair_start("pair_start_" + tag, [grads[nm] for nm in names]))
        return (started[tag][1][4],)

    def grads_sent(tag, after):
        names, (send_sem, recv_sem, grads, lands, token) = started[tag]
        grads, theirs = _pair_wait("pair_wait_" + tag, send_sem, recv_sem, grads, lands, token if after is None else after)
        sums = [_pair_sum("pair_sum_" + nm, g, th, core) for nm, g, th in zip(names, grads, theirs)]
        started[tag] = (names, _scatter_start("scatter_start_" + tag, sums))
        return (started[tag][1][4],)

    grad_x, small_g = _local_step(
        x, loss_target, ffn1_norm_g, mix_norm_g, ffn2_norm_g, attn_q_norm_g, attn_k_norm_g, hgrn_out_norm_g,
        attn_rel_bias[0], hgrn_lower_bounds, weights, on_grads, grads_sent)

    def finish(tag, after):
        names, (send_sem, recv_sem, sums, lands, _) = started[tag]
        sums, lands = _scatter_wait("scatter_wait_" + tag, send_sem, recv_sem, sums, lands, after)
        return names, [_chip_sum("chip_sum_" + nm, sm, ld, chip) for nm, sm, ld in zip(names, sums, lands)]

    by_name = {nm: (w, m, v) for nm, w, m, v in zip(big_names, big_w, big_m, big_v)}
    updated = {}

    def update(names, halves, other_halves):
        for nm, mine, theirs in zip(names, halves, other_halves):
            w, m, v = by_name[nm]
            updated[nm] = _adamw("adamw_" + nm, w, mine, theirs, m, v, core)

    last_token = started["ffn1"][1][4]
    names_a, halves_a = finish("ffn2", last_token)
    names_m, halves_m = finish("mix", last_token)
    names_a, halves_a = names_a + names_m, halves_a + halves_m
    update(names_a, halves_a, _pair_join("pair_join_early", halves_a))
    names_b, halves_b = finish("ffn1", updated[names_a[-1]][1])
    others_b, small_all = _pair_join("pair_join_last", halves_b, small_g)
    update(names_b, halves_b, others_b)
    big_out = [updated[nm] for nm in big_names]

    pack = lambda g1, gm, g2, gq, gk, rel, lbp, go: _pack_small(g1, gm, g2, lbp, rel[0], gq, gk, go)
    small_w = pack(ffn1_norm_g, mix_norm_g, ffn2_norm_g, attn_q_norm_g, attn_k_norm_g, attn_rel_bias, hgrn_lower_bounds, hgrn_out_norm_g)
    small_m = pack(m_ffn1_norm_g, m_mix_norm_g, m_ffn2_norm_g, m_attn_q_norm_g, m_attn_k_norm_g, m_attn_rel_bias, m_hgrn_lower_bounds, m_hgrn_out_norm_g)
    small_v = pack(v_ffn1_norm_g, v_mix_norm_g, v_ffn2_norm_g, v_attn_q_norm_g, v_attn_k_norm_g, v_attn_rel_bias, v_hgrn_lower_bounds, v_hgrn_out_norm_g)
    small_res = _adamw_small("adamw_small", small_w, small_all, small_m, small_v)
    small_out = [_unpack_small(p, d) for p in small_res]
    loss = small_res[0].reshape(-1)[LOSS_SLOT]

    def assemble(kind):
        bg = [flip(nm, o[kind]) for nm, o in zip(big_names, big_out)]
        g1, gm, g2, gq, gk, rel, lbp, go = small_out[kind]
        return [g1, bg[0], bg[1], bg[2], gm, bg[3], gq, gk, rel, lbp, go, bg[4], g2, bg[5], bg[6], bg[7]]

    return (loss, grad_x, *assemble(0), *assemble(1), *assemble(2), *assemble(3))
```

```python
import functools

import jax
import jax.numpy as jnp
from jax import lax
from jax.experimental import pallas as pl
from jax.experimental.pallas import tpu as pltpu

F32 = jnp.float32
BF16 = jnp.bfloat16
MESH = pl.DeviceIdType.MESH

N_CHIPS = 4
N_DEV = 8
CHUNK = 64
ATTN_HEADS = 8
ATTN_DH = 64
ATTN_W = ATTN_HEADS * ATTN_DH
HGRN_HEADS = 4
HGRN_DH = 128
HGRN_W = HGRN_HEADS * HGRN_DH
LEFT_CHUNKS = 8
BAND = (LEFT_CHUNKS + 1) * CHUNK
KPAD = LEFT_CHUNKS * CHUNK
REL_CLIP = 128
N_REL = 2 * REL_CLIP + 1
N_REL_PAD = 384
RMS_EPS = 1e-6
LANES = 128
SMALL_ROWS = 8
SMALL_COLS = 1024

ADAM_LR = 0.001
ADAM_B1 = 0.9
ADAM_B2 = 0.999
ADAM_EPS = 1e-08
ADAM_WD = 0.01
ADAM_STEP = 10

NN = (((1,), (0,)), ((), ()))
NT = (((1,), (1,)), ((), ()))
TN = (((0,), (0,)), ((), ()))

VMEM_LIMIT = 48 * 1024 * 1024
MXU_WIDTH = 256
COL_CHUNK = 3 * MXU_WIDTH


def _sigmoid(x):
    return 1.0 / (1.0 + jnp.exp(-x))


def _silu(x):
    return x * _sigmoid(x)


def _dot(a, b, dims=NN):
    return lax.dot_general(a, b, dims, preferred_element_type=F32)


def _split3(x):
    hi = x.astype(BF16)
    r1 = x - hi.astype(F32)
    mid = r1.astype(BF16)
    lo = (r1 - mid.astype(F32)).astype(BF16)
    return hi, mid, lo


def _dot_exact_rhs(x, mat, dims=NN, pieces=3):
    hi, mid, lo = _split3(x)
    out = _dot(hi, mat, dims) + _dot(mid, mat, dims)
    return out + _dot(lo, mat, dims) if pieces == 3 else out


def _dot_exact_lhs(mat, x, dims=NN):
    hi, mid, lo = _split3(x)
    return _dot(mat, hi, dims) + _dot(mat, mid, dims) + _dot(mat, lo, dims)


def _params(*sem):
    return pltpu.CompilerParams(dimension_semantics=sem, vmem_limit_bytes=VMEM_LIMIT)


def _mm(name, ins, terms, n_acc, grid, acc_shape, outs, epilogue, extras=(), deps=()):
    nk = grid[2]
    ni, ne, nd, no = len(ins), len(extras), len(deps), len(outs)

    def body(*refs):
        in_refs = refs[:ni]
        ex_refs = refs[ni:ni + ne]
        out_refs = refs[ni + ne + nd:ni + ne + nd + no]
        acc_refs = refs[ni + ne + nd + no:]

        def products():
            parts = [None] * n_acc
            for ai, li, ri, dims in terms:
                d = _dot(in_refs[li][...], in_refs[ri][...], dims)
                parts[ai] = d if parts[ai] is None else parts[ai] + d
            return parts

        def finish(accs):
            res = epilogue(accs, [e[...] for e in ex_refs])
            for o, r in zip(out_refs, res):
                o[...] = r.astype(o.dtype)

        if nk == 1:
            finish(products())
        else:
            k = pl.program_id(2)

            @pl.when(k == 0)
            def _():
                for a, p in zip(acc_refs, products()):
                    a[...] = p

            if nk > 2:
                @pl.when(jnp.logical_and(k > 0, k < nk - 1))
                def _():
                    for a, p in zip(acc_refs, products()):
                        a[...] += p

            @pl.when(k == nk - 1)
            def _():
                finish([a[...] + p for a, p in zip(acc_refs, products())])

    scratch = [] if nk == 1 else [pltpu.VMEM(acc_shape, F32) for _ in range(n_acc)]
    res = pl.pallas_call(
        body,
        name=name,
        grid=grid,
        in_specs=[s for _, s in ins] + [s for _, s in extras] + [pl.BlockSpec(memory_space=pl.ANY)] * nd,
        out_specs=[s for _, s in outs],
        out_shape=[o for o, _ in outs],
        scratch_shapes=scratch,
        compiler_params=_params("parallel", "parallel", "arbitrary"),
    )(*[a for a, _ in ins], *[a for a, _ in extras], *deps)
    return res


def _staged_shape(w):
    return w.shape if len(w.shape) == 2 else (w.shape[1], w.shape[0] * w.shape[2])


def _stage_weights(w_hbm, w_vmem, sem):
    @pl.when(pl.program_id(0) == 0)
    def _():
        copies = []
        for p, (h, v) in enumerate(zip(w_hbm, w_vmem)):
            if len(h.shape) == 2:
                copies.append(pltpu.make_async_copy(h, v, sem.at[p, 0]))
            else:
                pj = h.shape[2]
                copies += [pltpu.make_async_copy(h.at[j], v.at[:, pl.ds(j * pj, pj)], sem.at[p, j])
                           for j in range(h.shape[0])]
        for cp in copies:
            cp.start()
        for cp in copies:
            cp.wait()


def _staging_scratch(weights):
    return [pltpu.VMEM(_staged_shape(w), w.dtype) for w in weights] + [pltpu.SemaphoreType.DMA((len(weights), N_CHIPS))]


def _mm_rows(name, lhs, weights, dims, t, outs, epilogue, extras=(), deps=()):
    tm = _row_tile(t)
    nl, ne, nd, no = len(lhs), len(extras), len(deps), len(outs)

    def body(*refs):
        lhs_refs = refs[:nl]
        w_hbm = refs[nl:2 * nl]
        ex_refs = refs[2 * nl:2 * nl + ne]
        out_refs = refs[2 * nl + ne + nd:2 * nl + ne + nd + no]
        w_vmem = refs[2 * nl + ne + nd + no:3 * nl + ne + nd + no]
        _stage_weights(w_hbm, w_vmem, refs[-1])

        acc = None
        for p in range(nl):
            part = _dot(lhs_refs[p][...], w_vmem[p][...], dims)
            acc = part if acc is None else acc + part
        res = epilogue([acc], [e[...] for e in ex_refs])
        for o, r in zip(out_refs, res):
            o[...] = r.astype(o.dtype)

    return pl.pallas_call(
        body,
        name=name,
        grid=(t // tm,),
        in_specs=[s for _, s in lhs] + [pl.BlockSpec(memory_space=pl.ANY)] * nl + [s for _, s in extras]
        + [pl.BlockSpec(memory_space=pl.ANY)] * nd,
        out_specs=[s for _, s in outs],
        out_shape=[o for o, _ in outs],
        scratch_shapes=_staging_scratch(weights),
        compiler_params=_params("arbitrary"),
    )(*[a for a, _ in lhs], *weights, *[a for a, _ in extras], *deps)


def _col_chunks(f):
    return [(c, min(COL_CHUNK, f - c)) for c in range(0, f, COL_CHUNK)]


def _mm_cols(name, x, weights, dims, n_out, epilogue, extras=(), deps=(), out_dtype=BF16):
    t, k = x.shape
    f = _staged_shape(weights[0])[0 if dims == NT else 1]
    tm = _row_tile(t)
    chunks = _col_chunks(f)
    nw, ne, nd = len(weights), len(extras), len(deps)

    def body(*refs):
        x_ref = refs[0]
        w_hbm = refs[1:1 + nw]
        ex_refs = refs[1 + nw:1 + nw + ne]
        out_refs = refs[1 + nw + ne + nd:1 + nw + ne + nd + n_out]
        w_vmem = refs[1 + nw + ne + nd + n_out:1 + 2 * nw + ne + nd + n_out]
        _stage_weights(w_hbm, w_vmem, refs[-1])

        xv = x_ref[...]

        def dots(c):
            c0, cw = chunks[c]
            return [_dot(xv, w[c0:c0 + cw, :] if dims == NT else w[:, c0:c0 + cw], dims) for w in w_vmem]

        accs = dots(0)
        for c, (c0, cw) in enumerate(chunks):
            nxt = dots(c + 1) if c + 1 < len(chunks) else None
            res = epilogue(accs, [e[:, c0:c0 + cw] for e in ex_refs])
            for o, r in zip(out_refs, res):
                o[:, c0:c0 + cw] = r.astype(o.dtype)
            accs = nxt

    act = pl.BlockSpec((tm, f), lambda i: (i, 0))
    return pl.pallas_call(
        body,
        name=name,
        grid=(t // tm,),
        in_specs=[pl.BlockSpec((tm, k), lambda i: (i, 0))] + [pl.BlockSpec(memory_space=pl.ANY)] * nw + [act] * ne
        + [pl.BlockSpec(memory_space=pl.ANY)] * nd,
        out_specs=[act] * n_out,
        out_shape=[jax.ShapeDtypeStruct((t, f), out_dtype)] * n_out,
        scratch_shapes=_staging_scratch(weights),
        compiler_params=_params("arbitrary"),
    )(x, *weights, *extras, *deps)


def _row_tile(t):
    return 512 if t % 512 == 0 else t


def _k_tile(t):
    return t if t <= 4096 else 1024


def _grad_k_tile(t):
    return 2048 if t % 2048 == 0 else t


def _rmsnorm(xv, g):
    ms = jnp.mean(xv * xv, axis=-1, keepdims=True)
    return xv * lax.rsqrt(ms + RMS_EPS) * g


def _rmsnorm_fwd(name, x, g):
    t, d = x.shape
    tm = _row_tile(t)

    def body(x_ref, g_ref, h_ref):
        h_ref[...] = _rmsnorm(x_ref[...], g_ref[...]).astype(BF16)

    return pl.pallas_call(
        body,
        name=name,
        grid=(t // tm,),
        in_specs=[pl.BlockSpec((tm, d), lambda i: (i, 0)), pl.BlockSpec((1, d), lambda i: (0, 0))],
        out_specs=pl.BlockSpec((tm, d), lambda i: (i, 0)),
        out_shape=jax.ShapeDtypeStruct((t, d), BF16),
        compiler_params=_params("parallel"),
    )(x, g)


def _norm_bwd_epilogue(copy_scale):
    def epilogue(accs, ex):
        dh = accs[0]
        xv, g, dres = ex
        ms = jnp.mean(xv * xv, axis=-1, keepdims=True)
        rstd = lax.rsqrt(ms + RMS_EPS)
        xhat = xv * rstd
        dxhat = dh * g
        dx = rstd * (dxhat - xhat * jnp.mean(dxhat * xhat, axis=-1, keepdims=True))
        out = dres + dx
        dg = jnp.sum(dh * xhat, axis=0, keepdims=True)
        if copy_scale is None:
            return out, dg
        return out, out * copy_scale, dg

    return epilogue


def _merged(w):
    return w.reshape(-1, w.shape[-1])


def _ffn_up(name, h, wg, wu, deps=()):
    def epilogue(accs, ex):
        a, b = accs
        sg = _sigmoid(a)
        act = a * sg
        return act, b * (sg * (1.0 + a * (1.0 - sg))), act * b

    return _mm_cols(name, h, [_merged(wg), _merged(wu)], NT, 3, epilogue, deps=deps)


def _whole_rows(arr, tm):
    return arr, pl.BlockSpec((tm, arr.shape[1]), lambda i: (i, 0))


def _ffn_down(name, z, wd, x, g_next, deps=()):
    t = z.shape[0]
    d = wd.shape[2]
    tm = _row_tile(t)
    row = pl.BlockSpec((tm, d), lambda i: (i, 0))

    def epilogue(accs, ex):
        y = ex[0] + 0.5 * accs[0]
        return y, _rmsnorm(y, ex[1])

    return _mm_rows(
        name, [_whole_rows(z, tm)], [_merged(wd)], NN, t,
        outs=[(jax.ShapeDtypeStruct((t, d), F32), row), (jax.ShapeDtypeStruct((t, d), BF16), row)],
        epilogue=epilogue,
        extras=[(x, row), (g_next, pl.BlockSpec((1, d), lambda i: (0, 0)))],
        deps=deps,
    )


def _ffn_down_loss(name, z, wd, x, target):
    t = z.shape[0]
    d = wd.shape[2]
    tm = _row_tile(t)
    nt = t // tm
    row = pl.BlockSpec((tm, d), lambda i: (i, 0))

    def epilogue(accs, ex):
        e = ex[0] + 0.5 * accs[0] - ex[1]
        dy = e * (1.0 / d)
        return dy, 0.5 * dy, jnp.sum(e * e, axis=0, keepdims=True)

    return _mm_rows(
        name, [_whole_rows(z, tm)], [_merged(wd)], NN, t,
        outs=[(jax.ShapeDtypeStruct((t, d), F32), row), (jax.ShapeDtypeStruct((t, d), BF16), row),
              (jax.ShapeDtypeStruct((nt, 1, d), F32), pl.BlockSpec((None, 1, d), lambda i: (i, 0, 0)))],
        epilogue=epilogue,
        extras=[(x, row), (target, row)],
    )


def _ffn_bwd_act(name, dout, wd, act_a, dact_b, deps=()):
    def epilogue(accs, ex):
        dz = accs[0]
        return dz * ex[1].astype(F32), dz * ex[0].astype(F32)

    return _mm_cols(name, dout, [_merged(wd)], NT, 2, epilogue, extras=[act_a, dact_b], deps=deps)


def _grad_w_cols(name, z, dout, deps=()):
    t, f = z.shape
    d = dout.shape[1]
    tk = _grad_k_tile(t)
    fh = f // 2
    dw = _mm(
        name,
        ins=[(z, pl.BlockSpec((tk, fh), lambda j, n, k: (k, j))),
             (dout, pl.BlockSpec((tk, d), lambda j, n, k: (k, 0)))],
        terms=[(0, 0, 1, TN)],
        n_acc=1,
        grid=(2, 1, t // tk),
        acc_shape=(fh, d),
        outs=[(pltpu.HBM((f, d), BF16), pl.BlockSpec((fh, d), lambda j, n, k: (j, 0)))],
        epilogue=lambda accs, ex: (accs[0],),
        deps=deps,
    )[0]
    return dw.reshape(N_CHIPS, f // N_CHIPS, d)


def _norm_bwd_outs(t, d, tm, copy_scale):
    row = pl.BlockSpec((tm, d), lambda i: (i, 0))
    outs = [(jax.ShapeDtypeStruct((t, d), F32), row)]
    if copy_scale is not None:
        outs.append((jax.ShapeDtypeStruct((t, d), BF16), row))
    outs.append((jax.ShapeDtypeStruct((t // tm, 1, d), F32), pl.BlockSpec((None, 1, d), lambda i: (i, 0, 0))))
    return row, outs


def _ffn_bwd_in(name, da, db, wg, wu, x, g, dres, copy_scale, deps=()):
    t = da.shape[0]
    d = wg.shape[2]
    tm = _row_tile(t)
    row, outs = _norm_bwd_outs(t, d, tm, copy_scale)
    return _mm_rows(
        name, [_whole_rows(da, tm), _whole_rows(db, tm)], [_merged(wg), _merged(wu)], NN, t,
        outs=outs,
        epilogue=_norm_bwd_epilogue(copy_scale),
        extras=[(x, row), (g, pl.BlockSpec((1, d), lambda i: (0, 0))), (dres, row)],
        deps=deps,
    )


def _in_proj(name, h, w_in):
    return _mm_cols(name, h, [w_in], NN, 1, lambda accs, ex: (accs[0],), out_dtype=F32)[0]


def _in_proj_bwd(name, dp, w_in, x, g, dres, copy_scale, deps=()):
    t = dp.shape[0]
    d = w_in.shape[1]
    tm = _row_tile(t)
    row, outs = _norm_bwd_outs(t, d, tm, copy_scale)
    return _mm_rows(
        name, [_whole_rows(dp, tm)], [w_in], NT, t,
        outs=outs,
        epilogue=_norm_bwd_epilogue(copy_scale),
        extras=[(x, row), (g, pl.BlockSpec((1, d), lambda i: (0, 0))), (dres, row)],
        deps=deps,
    )


def _grad_w_in(name, h, dp, ns):
    t, d = h.shape
    pj = dp.shape[1] // ns
    tk = 1024 if t % 1024 == 0 else t
    return _mm(
        name,
        ins=[(h, pl.BlockSpec((tk, d), lambda j, n, k: (k, 0))),
             (dp, pl.BlockSpec((tk, 2 * pj), lambda j, n, k: (k, j)))],
        terms=[(0, 0, 1, TN)],
        n_acc=1,
        grid=(ns // 2, 1, t // tk),
        acc_shape=(d, 2 * pj),
        outs=[(pltpu.HBM((ns, d, pj), BF16), pl.BlockSpec((2, d, pj), lambda j, n, k: (j, 0, 0)))],
        epilogue=lambda accs, ex: (jnp.stack([accs[0][:, :pj], accs[0][:, pj:]]),),
    )[0]


def _out_proj(name, mix, w_out, x, g_next):
    t, dm = mix.shape
    d = w_out.shape[1]
    tm = _row_tile(t)
    row = pl.BlockSpec((tm, d), lambda i, n, k: (i, 0))
    return _mm(
        name,
        ins=[(mix, pl.BlockSpec((tm, dm), lambda i, n, k: (i, 0))),
             (w_out, pl.BlockSpec((dm, d), lambda i, n, k: (0, 0)))],
        terms=[(0, 0, 1, NN)],
        n_acc=1,
        grid=(t // tm, 1, 1),
        acc_shape=(tm, d),
        outs=[(jax.ShapeDtypeStruct((t, d), F32), row), (jax.ShapeDtypeStruct((t, d), BF16), row)],
        epilogue=lambda accs, ex: (ex[0] + accs[0], _rmsnorm(ex[0] + accs[0], ex[1])),
        extras=[(x, row), (g_next, pl.BlockSpec((1, d), lambda i, n, k: (0, 0)))],
    )


def _out_proj_bwd(name, dx, w_out, deps=()):
    t, d = dx.shape
    dm = w_out.shape[0]
    tm = _row_tile(t)
    return _mm(
        name,
        ins=[(dx, pl.BlockSpec((tm, d), lambda i, n, k: (i, 0))),
             (w_out, pl.BlockSpec((dm, d), lambda i, n, k: (0, 0)))],
        terms=[(0, 0, 1, NT)],
        n_acc=1,
        grid=(t // tm, 1, 1),
        acc_shape=(tm, dm),
        outs=[(jax.ShapeDtypeStruct((t, dm), F32), pl.BlockSpec((tm, dm), lambda i, n, k: (i, 0)))],
        epilogue=lambda accs, ex: (accs[0],),
        deps=deps,
    )[0]


def _grad_w_out(name, mix, dx):
    t, dm = mix.shape
    d = dx.shape[1]
    tk = _k_tile(t)
    return _mm(
        name,
        ins=[(mix, pl.BlockSpec((tk, dm), lambda a, n, k: (k, 0))),
             (dx, pl.BlockSpec((tk, d), lambda a, n, k: (k, 0)))],
        terms=[(0, 0, 1, TN)],
        n_acc=1,
        grid=(1, 1, t // tk),
        acc_shape=(dm, d),
        outs=[(pltpu.HBM((dm, d), BF16), pl.BlockSpec((dm, d), lambda a, n, k: (0, 0)))],
        epilogue=lambda accs, ex: (accs[0],),
    )[0]


def _head_group_matrix():
    r = lax.broadcasted_iota(jnp.int32, (MXU_WIDTH, MXU_WIDTH), 0)
    c = lax.broadcasted_iota(jnp.int32, (MXU_WIDTH, MXU_WIDTH), 1)
    same = jnp.right_shift(r, 6) == jnp.right_shift(c, 6)
    return jnp.where(same, 1.0, 0.0).astype(BF16)


def _head_sums(x, bd):
    return jnp.concatenate(
        [_dot_exact_rhs(x[:, c:c + MXU_WIDTH], bd, pieces=2) for c in range(0, x.shape[1], MXU_WIDTH)], axis=1)


def _qk_prep(name, proj, gq, gk):
    b, s, _ = proj.shape
    tm = KPAD
    nb = s // tm

    def body(q_ref, k_ref, v_ref, gq_ref, gk_ref, qn_ref, kn_ref, vb_ref):
        j = pl.program_id(1)
        bd = _head_group_matrix()

        def norm(xv, g):
            ms = _head_sums(xv * xv, bd) * (1.0 / ATTN_DH)
            return xv * lax.rsqrt(ms + RMS_EPS) * g

        @pl.when(j == 0)
        def _():
            kn_ref[...] = jnp.zeros_like(kn_ref)
            vb_ref[...] = jnp.zeros_like(vb_ref)

        @pl.when(j > 0)
        def _():
            qn_ref[...] = norm(q_ref[...], gq_ref[...]).astype(BF16)
            kn_ref[...] = norm(k_ref[...], gk_ref[...]).astype(BF16)
            vb_ref[...] = v_ref[...].astype(BF16)

    src_blk = lambda col: pl.BlockSpec((None, tm, ATTN_W), lambda bi, j: (bi, jnp.maximum(j - 1, 0), col))
    gspec = pl.BlockSpec((1, ATTN_W), lambda bi, j: (0, 0))
    padded = pl.BlockSpec((None, tm, ATTN_W), lambda bi, j: (bi, j, 0))
    return pl.pallas_call(
        body,
        name=name,
        grid=(b, nb + 1),
        in_specs=[src_blk(0), src_blk(1), src_blk(2), gspec, gspec],
        out_specs=[src_blk(0), padded, padded],
        out_shape=[jax.ShapeDtypeStruct((b, s, ATTN_W), BF16), jax.ShapeDtypeStruct((b, KPAD + s, ATTN_W), BF16),
                   jax.ShapeDtypeStruct((b, KPAD + s, ATTN_W), BF16)],
        compiler_params=_params("parallel", "arbitrary"),
    )(proj, proj, proj, gq, gk)


def _qk_prep_bwd(name, proj, dqn, dkn, dv, gq, gk):
    b, s, _ = proj.shape
    tm = KPAD
    nb = s // tm

    def body(q_ref, k_ref, dqn_ref, dkn_ref, dv_ref, gq_ref, gk_ref, dq_ref, dk_ref, dvb_ref, dgq_ref, dgk_ref):
        bd = _head_group_matrix()

        def bwd(xv, dy, g):
            ms = _head_sums(xv * xv, bd) * (1.0 / ATTN_DH)
            rstd = lax.rsqrt(ms + RMS_EPS)
            xhat = xv * rstd
            dxhat = dy * g
            gm = _head_sums(dxhat * xhat, bd) * (1.0 / ATTN_DH)
            return rstd * (dxhat - xhat * gm), jnp.sum(dy * xhat, axis=0, keepdims=True)

        dq, dgq = bwd(q_ref[...], dqn_ref[...], gq_ref[...])
        dk, dgk = bwd(k_ref[...], dkn_ref[...], gk_ref[...])
        dq_ref[...] = dq.astype(BF16)
        dk_ref[...] = dk.astype(BF16)
        dvb_ref[...] = dv_ref[...].astype(BF16)
        dgq_ref[...] = dgq
        dgk_ref[...] = dgk

    col = lambda c: pl.BlockSpec((None, tm, ATTN_W), lambda bi, j: (bi, j, c))
    past_pad = pl.BlockSpec((None, tm, ATTN_W), lambda bi, j: (bi, j + 1, 0))
    gspec = pl.BlockSpec((1, ATTN_W), lambda bi, j: (0, 0))
    pspec = pl.BlockSpec((None, 1, ATTN_W), lambda bi, j: (bi * nb + j, 0, 0))
    o_shape = jax.ShapeDtypeStruct((b, s, ATTN_W), BF16)
    p_shape = jax.ShapeDtypeStruct((b * nb, 1, ATTN_W), F32)
    return pl.pallas_call(
        body,
        name=name,
        grid=(b, nb),
        in_specs=[col(0), col(1), col(0), past_pad, past_pad, gspec, gspec],
        out_specs=[col(0)] * 3 + [pspec] * 2,
        out_shape=[o_shape] * 3 + [p_shape] * 2,
        compiler_params=_params("parallel", "parallel"),
    )(proj, proj, dqn, dkn, dv, gq, gk)


Q_CHUNKS = 4
QBLK = Q_CHUNKS * CHUNK
WIN = (LEFT_CHUNKS + Q_CHUNKS) * CHUNK
DB_W = BAND + CHUNK
MASKED = -1e30
FWD_BLOCKS = 8
BWD_BLOCKS = 2


def _band_table(bias):
    rows = [jnp.pad(bias, ((0, 0), (0, 0), (CHUNK * i, WIN - BAND - CHUNK * i)), constant_values=MASKED)
            for i in range(Q_CHUNKS)]
    return jnp.concatenate(rows, axis=1)


def _head_lanes(hh):
    lane = lax.broadcasted_iota(jnp.int32, (1, LANES), 1)
    return (lane < ATTN_DH) if hh == 0 else (lane >= ATTN_DH)


def _attn_probs(qh, kw, table, start):
    s = _dot(qh, kw, NT) * (ATTN_DH ** -0.5) + table
    col = lax.broadcasted_iota(jnp.int32, (QBLK, WIN), 1)
    s = jnp.where(col + start >= KPAD, s, MASKED)
    m = jnp.max(s, axis=-1, keepdims=True)
    p = jnp.exp(s - m)
    return p * (1.0 / jnp.sum(p, axis=-1, keepdims=True))


def _attn_fwd(name, q, k, v, table, deps=()):
    b, s, w = q.shape
    sp = k.shape[1]

    def body(q_ref, k_ref, v_ref, t_ref, *rest):
        o_ref = rest[-1]
        lanes = [_head_lanes(hh) for hh in range(2)]
        starts = [pl.multiple_of((pl.program_id(2) * FWD_BLOCKS + j) * QBLK, QBLK) for j in range(FWD_BLOCKS)]
        kws = [k_ref[pl.ds(st, WIN), :] for st in starts]
        vws = [v_ref[pl.ds(st, WIN), :] for st in starts]
        q2s = [q_ref[j * QBLK:(j + 1) * QBLK, :] for j in range(FWD_BLOCKS)]
        probs = [[_attn_probs(jnp.where(mine, q2s[j], jnp.zeros_like(q2s[j])), kws[j], t_ref[hh], starts[j]).astype(BF16)
                  for hh, mine in enumerate(lanes)] for j in range(FWD_BLOCKS)]
        for j in range(FWD_BLOCKS):
            outs = [_dot(p, vws[j]) for p in probs[j]]
            o_ref[j * QBLK:(j + 1) * QBLK, :] = jnp.where(lanes[0], outs[0], outs[1]).astype(BF16)

    qspec = pl.BlockSpec((None, FWD_BLOCKS * QBLK, LANES), lambda p, bi, i: (bi, i, p))
    kspec = pl.BlockSpec((None, sp, LANES), lambda p, bi, i: (bi, 0, p))
    return pl.pallas_call(
        body,
        name=name,
        grid=(w // LANES, b, s // (FWD_BLOCKS * QBLK)),
        in_specs=[qspec, kspec, kspec, pl.BlockSpec((2, QBLK, WIN), lambda p, bi, i: (p, 0, 0))] + [ANY] * len(deps),
        out_specs=qspec,
        out_shape=jax.ShapeDtypeStruct((b, s, w), BF16),
        compiler_params=_params("parallel", "parallel", "arbitrary"),
    )(q, k, v, table, *deps)


def _attn_bwd(name, q, k, v, table, dmix):
    b, s, w = q.shape
    sp = k.shape[1]

    def body(q_ref, k_ref, v_ref, t_ref, do_ref, dq_ref, dk_ref, dv_ref, dbe_ref, dbo_ref):
        bi = pl.program_id(1)
        i = pl.program_id(2)

        @pl.when(i == 0)
        def _():
            dk_ref[...] = jnp.zeros_like(dk_ref)
            dv_ref[...] = jnp.zeros_like(dv_ref)

        @pl.when(jnp.logical_and(i == 0, bi == 0))
        def _():
            dbe_ref[...] = jnp.zeros_like(dbe_ref)
            dbo_ref[...] = jnp.zeros_like(dbo_ref)

        lanes = [_head_lanes(hh) for hh in range(2)]

        def scores(j):
            start = pl.multiple_of((i * BWD_BLOCKS + j) * QBLK, QBLK)
            win = pl.ds(start, WIN)
            kw = k_ref[win, :]
            vw = v_ref[win, :]
            q2 = q_ref[j * QBLK:(j + 1) * QBLK, :]
            do2 = do_ref[j * QBLK:(j + 1) * QBLK, :].astype(BF16)
            qh = [jnp.where(mine, q2, jnp.zeros_like(q2)) for mine in lanes]
            doh = [jnp.where(mine, do2, jnp.zeros_like(do2)) for mine in lanes]
            p = [_attn_probs(qh[hh], kw, t_ref[hh], start) for hh in range(2)]
            dp = [_dot(doh[hh], vw, NT) for hh in range(2)]
            return win, kw, qh, doh, p, dp

        def gradients(j, win, kw, qh, doh, p, dp):
            ds = [p[hh] * (dp[hh] - jnp.sum(p[hh] * dp[hh], axis=-1, keepdims=True)) for hh in range(2)]
            dsb = [(x * (ATTN_DH ** -0.5)).astype(BF16) for x in ds]
            pb = [x.astype(BF16) for x in p]
            dq = [_dot(dsb[hh], kw) for hh in range(2)]
            dk = [_dot(dsb[hh], qh[hh], TN) for hh in range(2)]
            dv = [_dot(pb[hh], doh[hh], TN) for hh in range(2)]
            for hh in range(2):
                for qi in range(Q_CHUNKS):
                    c0 = (qi // 2) * LANES
                    blk = ds[hh][qi * CHUNK:(qi + 1) * CHUNK, c0:c0 + DB_W]
                    if qi % 2 == 0:
                        dbe_ref[hh] += blk
                    else:
                        dbo_ref[hh] += blk
            dq_ref[j * QBLK:(j + 1) * QBLK, :] = jnp.where(lanes[0], dq[0], dq[1])
            dk_ref[win, :] += dk[0] + dk[1]
            dv_ref[win, :] += dv[0] + dv[1]

        staged = scores(0)
        for j in range(BWD_BLOCKS):
            upcoming = scores(j + 1) if j + 1 < BWD_BLOCKS else None
            gradients(j, *staged)
            staged = upcoming

    qspec = pl.BlockSpec((None, BWD_BLOCKS * QBLK, LANES), lambda p, bi, i: (bi, i, p))
    kspec = pl.BlockSpec((None, sp, LANES), lambda p, bi, i: (bi, 0, p))
    dbspec = pl.BlockSpec((2, CHUNK, DB_W), lambda p, bi, i: (p, 0, 0))
    db_shape = jax.ShapeDtypeStruct((ATTN_HEADS, CHUNK, DB_W), F32)
    return pl.pallas_call(
        body,
        name=name,
        grid=(w // LANES, b, s // (BWD_BLOCKS * QBLK)),
        in_specs=[qspec, kspec, kspec, pl.BlockSpec((2, QBLK, WIN), lambda p, bi, i: (p, 0, 0)), qspec],
        out_specs=[qspec, kspec, kspec, dbspec, dbspec],
        out_shape=[jax.ShapeDtypeStruct((b, s, w), F32), jax.ShapeDtypeStruct((b, sp, w), F32),
                   jax.ShapeDtypeStruct((b, sp, w), F32), db_shape, db_shape],
        compiler_params=_params("arbitrary", "arbitrary", "arbitrary"),
    )(q, k, v, table, dmix)


HQ_COL = 3 * ATTN_W // HGRN_DH
HF_COL = HQ_COL + HGRN_HEADS
HI_COL = HF_COL + HGRN_HEADS
HG_COL = HI_COL + HGRN_HEADS
HGRN_ROWS = 8 * CHUNK
HGRN_UNROLL = 8
HEAD_LANES = [slice(hh * HGRN_DH, (hh + 1) * HGRN_DH) for hh in range(HGRN_HEADS)]


def _tri(lower):
    r = lax.broadcasted_iota(jnp.int32, (CHUNK, CHUNK), 0)
    c = lax.broadcasted_iota(jnp.int32, (CHUNK, CHUNK), 1)
    return (r >= c) if lower else (r <= c)


def _hgrn_chunk(hq, hf, lb, tril):
    sig = _sigmoid(hf)
    f = lb + (1.0 - lb) * sig
    g = jnp.log(f)
    ones_l = jnp.where(tril, 1.0, 0.0).astype(BF16)
    b = _dot_exact_lhs(ones_l, g)
    bl = jnp.sum(g, axis=0, keepdims=True)
    rows = lax.broadcasted_iota(jnp.int32, g.shape, 0)
    bm = jnp.sum(jnp.where(rows <= CHUNK // 2, g, 0.0), axis=0, keepdims=True)
    sq = _sigmoid(hq)
    q = hq * sq
    k = 1.0 - f
    return sig, f, b, bl, bm, sq, q, k


def _hgrn_fwd(name, proj, attn, lb, go, b, s):
    nc = s // CHUNK
    t = b * s
    nblk = s // HGRN_ROWS
    cpb = HGRN_ROWS // CHUNK

    def body(hq_ref, hf_ref, hi_ref, hg_ref, attn_ref, lb_ref, go_ref, mix_ref, oraw_ref, st_ref, s_scr):
        tril = _tri(True)
        gov = go_ref[...]
        mix_ref[:, 0:ATTN_W] = attn_ref[...]

        @pl.when(pl.program_id(1) == 0)
        def _():
            s_scr[...] = jnp.zeros_like(s_scr)

        def step(c, carry):
            sl = pl.ds(pl.multiple_of(c * CHUNK, CHUNK), CHUNK)
            hg = hg_ref[sl, :]
            _, _, bb, bl, bm, _, q, k = _hgrn_chunk(hq_ref[sl, :], hf_ref[sl, :], lb_ref[...], tril)
            vb = hi_ref[sl, :].astype(BF16)
            qe = (q * jnp.exp(bb - bm)).astype(BF16)
            ke = (k * jnp.exp(bm - bb)).astype(BF16)
            qb = (q * jnp.exp(bb)).astype(BF16)
            kb = (k * jnp.exp(bl - bb)).astype(BF16)
            e_last = jnp.exp(bl)
            gate = _silu(hg)
            st = [s_scr[hh] for hh in range(HGRN_HEADS)]
            a = [jnp.where(tril, _dot(qe[:, hs], ke[:, hs], NT), 0.0).astype(BF16) for hs in HEAD_LANES]
            o_state = [_dot(qb[:, hs], st[hh].astype(BF16), NT) for hh, hs in enumerate(HEAD_LANES)]
            st_next = [st[hh] * e_last[:, hs] + _dot(vb[:, hs], kb[:, hs], TN) for hh, hs in enumerate(HEAD_LANES)]
            o = [_dot(a[hh], vb[:, hs]) + o_state[hh] for hh, hs in enumerate(HEAD_LANES)]
            ro = [(oh * lax.rsqrt(jnp.mean(oh * oh, axis=-1, keepdims=True) + RMS_EPS) * gov) * gate[:, hs]
                  for oh, hs in zip(o, HEAD_LANES)]
            for hh in range(HGRN_HEADS):
                st_ref[hh, c] = st[hh]
                s_scr[hh] = st_next[hh]
            mix_ref[sl, ATTN_W:ATTN_W + HGRN_W] = jnp.concatenate(ro, axis=1).astype(BF16)
            oraw_ref[sl, :] = jnp.concatenate(o, axis=1)
            return carry

        lax.fori_loop(0, cpb, step, 0, unroll=HGRN_UNROLL)

    col = lambda base: pl.BlockSpec((HGRN_ROWS, HGRN_W), lambda bi, i: (bi * nblk + i, base // HGRN_HEADS))
    out = pl.BlockSpec((HGRN_ROWS, HGRN_W), lambda bi, i: (bi * nblk + i, 0))
    return pl.pallas_call(
        body,
        name=name,
        grid=(b, nblk),
        in_specs=[col(HQ_COL), col(HF_COL), col(HI_COL), col(HG_COL), out,
                  pl.BlockSpec((1, HGRN_W), lambda bi, i: (0, 0)), pl.BlockSpec((1, HGRN_DH), lambda bi, i: (0, 0))],
        out_specs=[pl.BlockSpec((HGRN_ROWS, ATTN_W + HGRN_W), lambda bi, i: (bi * nblk + i, 0)), out,
                   pl.BlockSpec((None, HGRN_HEADS, cpb, HGRN_DH, HGRN_DH), lambda bi, i: (bi, 0, i, 0, 0))],
        out_shape=[jax.ShapeDtypeStruct((t, ATTN_W + HGRN_W), BF16), jax.ShapeDtypeStruct((t, HGRN_W), F32),
                   jax.ShapeDtypeStruct((b, HGRN_HEADS, nc, HGRN_DH, HGRN_DH), F32)],
        scratch_shapes=[pltpu.VMEM((HGRN_HEADS, HGRN_DH, HGRN_DH), F32)],
        compiler_params=_params("parallel", "arbitrary"),
    )(proj, proj, proj, proj, attn, lb, go)


def _hgrn_bwd(name, proj, dqkv, lb, go, oraw, states, dmix, b, s):
    t = b * s
    nblk = s // HGRN_ROWS
    cpb = HGRN_ROWS // CHUNK

    def body(hq_ref, hf_ref, hi_ref, hg_ref, dq_ref, dk_ref, dv_ref, lb_ref, go_ref, oraw_ref, st_ref, dro_ref,
             dp_ref, dlb_ref, dgo_ref, ds_scr, dlb_scr, dgo_scr):
        tril = _tri(True)
        ones_u = jnp.where(_tri(False), 1.0, 0.0).astype(BF16)
        gov = go_ref[...]
        dp_ref[:, 0:ATTN_W] = dq_ref[...]
        dp_ref[:, ATTN_W:2 * ATTN_W] = dk_ref[...]
        dp_ref[:, 2 * ATTN_W:3 * ATTN_W] = dv_ref[...]

        @pl.when(pl.program_id(1) == 0)
        def _():
            ds_scr[...] = jnp.zeros_like(ds_scr)
            dlb_scr[...] = jnp.zeros_like(dlb_scr)
            dgo_scr[...] = jnp.zeros_like(dgo_scr)

        def step(ci, carry):
            c = cpb - 1 - ci
            sl = pl.ds(pl.multiple_of(c * CHUNK, CHUNK), CHUNK)
            hq = hq_ref[sl, :]
            hg = hg_ref[sl, :]
            sig, f, bb, bl, bm, sq, q, k = _hgrn_chunk(hq, hf_ref[sl, :], lb_ref[...], tril)
            vb = hi_ref[sl, :].astype(BF16)
            ebm = jnp.exp(bb - bm)
            embm = jnp.exp(bm - bb)
            eb = jnp.exp(bb)
            ebl = jnp.exp(bl - bb)
            e_last = jnp.exp(bl)
            qe = (q * ebm).astype(BF16)
            ke = (k * embm).astype(BF16)
            qb = (q * eb).astype(BF16)
            kb = (k * ebl).astype(BF16)
            st = [st_ref[hh, c] for hh in range(HGRN_HEADS)]
            dst = [ds_scr[hh] for hh in range(HGRN_HEADS)]
            o = oraw_ref[sl, :]
            dro = dro_ref[sl, :]
            sg = _sigmoid(hg)
            gov4 = jnp.concatenate([gov] * HGRN_HEADS, axis=1)
            rstd = jnp.concatenate(
                [jnp.broadcast_to(lax.rsqrt(jnp.mean(o[:, hs] * o[:, hs], axis=-1, keepdims=True) + RMS_EPS),
                                  (CHUNK, HGRN_DH)) for hs in HEAD_LANES], axis=1)
            ohat = o * rstd
            dn = dro * (hg * sg)
            dhg = dro * (ohat * gov4) * (sg * (1.0 + hg * (1.0 - sg)))
            dgo_inc = jnp.sum(dn * ohat, axis=0, keepdims=True)
            dohat = dn * gov4
            proj_h = dohat * ohat
            pm = jnp.concatenate(
                [jnp.broadcast_to(jnp.mean(proj_h[:, hs], axis=-1, keepdims=True), (CHUNK, HGRN_DH))
                 for hs in HEAD_LANES], axis=1)
            dob = (rstd * (dohat - ohat * pm)).astype(BF16)
            stb = [x.astype(BF16) for x in st]
            dstb = [x.astype(BF16) for x in dst]
            a = [jnp.where(tril, _dot(qe[:, hs], ke[:, hs], NT), 0.0).astype(BF16) for hs in HEAD_LANES]
            dab = [jnp.where(tril, _dot(dob[:, hs], vb[:, hs], NT), 0.0).astype(BF16) for hs in HEAD_LANES]
            dqb = [_dot(dob[:, hs], stb[hh]) for hh, hs in enumerate(HEAD_LANES)]
            dkb = [_dot(vb[:, hs], dstb[hh]) for hh, hs in enumerate(HEAD_LANES)]
            dv_state = [_dot(kb[:, hs], dstb[hh], NT) for hh, hs in enumerate(HEAD_LANES)]
            dst_next = [dst[hh] * e_last[:, hs] + _dot(dob[:, hs], qb[:, hs], TN) for hh, hs in enumerate(HEAD_LANES)]
            dv = [_dot(a[hh], dob[:, hs], TN) + dv_state[hh] for hh, hs in enumerate(HEAD_LANES)]
            dqe = jnp.concatenate([_dot(dab[hh], ke[:, hs]) for hh, hs in enumerate(HEAD_LANES)], axis=1)
            dke = jnp.concatenate([_dot(dab[hh], qe[:, hs], TN) for hh, hs in enumerate(HEAD_LANES)], axis=1)
            dqb = jnp.concatenate(dqb, axis=1)
            dkb = jnp.concatenate(dkb, axis=1)
            state_term = jnp.concatenate(
                [jnp.sum(dst[hh] * st[hh], axis=0, keepdims=True) for hh in range(HGRN_HEADS)], axis=1)
            dq = dqe * ebm + dqb * eb
            dk = dke * embm + dkb * ebl
            db = (qe.astype(F32) * dqe - ke.astype(F32) * dke) + q * (dqb * eb) - k * (dkb * ebl)
            d_last = jnp.sum(k * ebl * dkb, axis=0, keepdims=True) + state_term * e_last
            dg = _dot_exact_lhs(ones_u, db) + d_last
            df = dg / f - dk
            first = HQ_COL * HGRN_DH
            dp_ref[sl, first:first + HGRN_W] = (dq * (sq * (1.0 + hq * (1.0 - sq)))).astype(BF16)
            dp_ref[sl, first + HGRN_W:first + 2 * HGRN_W] = (df * (1.0 - lb_ref[...]) * sig * (1.0 - sig)).astype(BF16)
            dp_ref[sl, first + 2 * HGRN_W:first + 3 * HGRN_W] = jnp.concatenate(dv, axis=1).astype(BF16)
            dp_ref[sl, first + 3 * HGRN_W:first + 4 * HGRN_W] = dhg.astype(BF16)
            dlb_scr[...] += jnp.sum(df * (1.0 - sig), axis=0, keepdims=True)
            dgo_scr[...] += dgo_inc
            for hh in range(HGRN_HEADS):
                ds_scr[hh] = dst_next[hh]
            return carry

        lax.fori_loop(0, cpb, step, 0, unroll=HGRN_UNROLL)

        @pl.when(pl.program_id(1) == nblk - 1)
        def _():
            dlb_ref[...] = dlb_scr[...]
            dgo_ref[...] = dgo_scr[...]

    rows = lambda bi, i: bi * nblk + (nblk - 1 - i)
    col = lambda base: pl.BlockSpec((HGRN_ROWS, HGRN_W), lambda bi, i: (rows(bi, i), base // HGRN_HEADS))
    out = pl.BlockSpec((HGRN_ROWS, HGRN_W), lambda bi, i: (rows(bi, i), 0))
    part = pl.BlockSpec((None, 1, HGRN_W), lambda bi, i: (bi, 0, 0))
    width = HG_COL * HGRN_DH + HGRN_W
    o_shape = jax.ShapeDtypeStruct((t, width), BF16)
    p_shape = jax.ShapeDtypeStruct((b, 1, HGRN_W), F32)
    return pl.pallas_call(
        body,
        name=name,
        grid=(b, nblk),
        in_specs=[col(HQ_COL), col(HF_COL), col(HI_COL), col(HG_COL), out, out, out,
                  pl.BlockSpec((1, HGRN_W), lambda bi, i: (0, 0)), pl.BlockSpec((1, HGRN_DH), lambda bi, i: (0, 0)), out,
                  pl.BlockSpec((None, HGRN_HEADS, cpb, HGRN_DH, HGRN_DH), lambda bi, i: (bi, 0, nblk - 1 - i, 0, 0)),
                  col(ATTN_W // HGRN_DH)],
        out_specs=[pl.BlockSpec((HGRN_ROWS, width), lambda bi, i: (rows(bi, i), 0))] + [part] * 2,
        out_shape=[o_shape] + [p_shape] * 2,
        scratch_shapes=[pltpu.VMEM((HGRN_HEADS, HGRN_DH, HGRN_DH), F32), pltpu.VMEM((1, HGRN_W), F32),
                        pltpu.VMEM((1, HGRN_W), F32)],
        compiler_params=_params("parallel", "arbitrary"),
    )(proj, proj, proj, proj, *dqkv, lb, go, oraw, states, dmix)


def _small_grads(name, dg1, dgm, dg2, dgq, dgk, dbe_t, dbo_t, dlb, dgo, lbp):
    d = dg1.shape[1]

    def body(dg1_ref, dgm_ref, dg2_ref, dgq_ref, dgk_ref, dbe_ref, dbo_ref, dlb_ref, dgo_ref, lbp_ref,
             g1_ref, gm_ref, g2_ref, gq_ref, gk_ref, rb_ref, lbg_ref, go_ref):
        g1_ref[...] = jnp.sum(dg1_ref[...], axis=0, keepdims=True)
        gm_ref[...] = jnp.sum(dgm_ref[...], axis=0, keepdims=True)
        g2_ref[...] = jnp.sum(dg2_ref[...], axis=0, keepdims=True)
        r = lax.broadcasted_iota(jnp.int32, (ATTN_W, ATTN_DH), 0)
        cidx = lax.broadcasted_iota(jnp.int32, (ATTN_W, ATTN_DH), 1)
        fold = jnp.where(jnp.bitwise_and(r, ATTN_DH - 1) == cidx, 1.0, 0.0).astype(BF16)
        gq_ref[...] = jnp.sum(_dot_exact_rhs(dgq_ref[...], fold), axis=0, keepdims=True)
        gk_ref[...] = jnp.sum(_dot_exact_rhs(dgk_ref[...], fold), axis=0, keepdims=True)
        gosum = jnp.sum(dgo_ref[...], axis=0, keepdims=True)
        go_ref[...] = (gosum[:, 0:HGRN_DH] + gosum[:, HGRN_DH:2 * HGRN_DH]
                       + gosum[:, 2 * HGRN_DH:3 * HGRN_DH] + gosum[:, 3 * HGRN_DH:4 * HGRN_DH])
        p0 = lbp_ref[0:1, :]
        p1 = lbp_ref[1:2, :]
        lbv = 1.0 / (1.0 + jnp.exp(p1 - p0))
        dp0 = jnp.sum(dlb_ref[...], axis=0, keepdims=True) * lbv * (1.0 - lbv)
        lbg_ref[0:1, :] = dp0
        lbg_ref[1:2, :] = -dp0
        acc = dbe_ref[CHUNK - 1] + pltpu.roll(dbo_ref[CHUNK - 1], DB_W - CHUNK, 1)
        for tq in range(CHUNK - 1):
            acc = acc + pltpu.roll(dbe_ref[tq], CHUNK - 1 - tq, 1) + pltpu.roll(dbo_ref[tq], DB_W - 1 - tq, 1)
        jidx = lax.broadcasted_iota(jnp.int32, (DB_W, N_REL_PAD), 0)
        ridx = lax.broadcasted_iota(jnp.int32, (DB_W, N_REL_PAD), 1)
        rel = jnp.clip(KPAD + CHUNK - 1 - jidx, -REL_CLIP, REL_CLIP) + REL_CLIP
        rb_ref[...] = _dot_exact_rhs(acc, jnp.where(rel == ridx, 1.0, 0.0).astype(BF16))

    ins = [dg1, dgm, dg2, dgq, dgk, dbe_t, dbo_t, dlb, dgo, lbp]
    outs = [jax.ShapeDtypeStruct((1, d), F32)] * 3 + [jax.ShapeDtypeStruct((1, ATTN_DH), F32)] * 2 + [
        jax.ShapeDtypeStruct((ATTN_HEADS, N_REL_PAD), F32), jax.ShapeDtypeStruct((2, HGRN_W), F32),
        jax.ShapeDtypeStruct((1, HGRN_DH), F32)]
    vm = pl.BlockSpec(memory_space=pltpu.VMEM)
    return pl.pallas_call(
        body,
        name=name,
        in_specs=[vm] * len(ins),
        out_specs=[vm] * len(outs),
        out_shape=outs,
        compiler_params=pltpu.CompilerParams(vmem_limit_bytes=VMEM_LIMIT),
    )(*ins)


def _adam_update(w, g, m, v):
    m2 = ADAM_B1 * m + (1.0 - ADAM_B1) * g
    v2 = ADAM_B2 * v + (1.0 - ADAM_B2) * (g * g)
    m_hat = m2 / (1.0 - ADAM_B1 ** ADAM_STEP)
    v_hat = v2 / (1.0 - ADAM_B2 ** ADAM_STEP)
    delta = -ADAM_LR * (m_hat / (jnp.sqrt(v_hat) + ADAM_EPS) + ADAM_WD * w)
    return delta, m2, v2


def _rows_tile(r):
    return r if r <= 512 or r % 512 else 512


def _pair_sum(name, grad, theirs, core):
    n, half, c = theirs.shape
    tr = _rows_tile(half)
    nth = half // tr

    def body(core_ref, a_ref, b_ref, o_ref):
        o_ref[...] = (a_ref[...].astype(F32) + b_ref[...].astype(F32)).astype(o_ref.dtype)

    spec = pl.BlockSpec((None, tr, c), lambda i, j, core_ref: (i, j, 0))
    return pl.pallas_call(
        body, name=name,
        grid_spec=pltpu.PrefetchScalarGridSpec(
            num_scalar_prefetch=1, grid=(n, nth),
            in_specs=[pl.BlockSpec((None, tr, c), lambda i, j, core_ref: (i, core_ref[0] * nth + j, 0)), spec],
            out_specs=spec),
        out_shape=pltpu.HBM((n, half, c), BF16), compiler_params=_params("parallel", "parallel"),
    )(core, grad, theirs)


def _chip_sum(name, own, parts, chip):
    _, half, c = own.shape
    tr = _rows_tile(half)

    def body(chip_ref, own_ref, p_ref, o_ref):
        me = chip_ref[0]
        mine = own_ref[...].astype(F32)
        flip_x, flip_y, flip_xy = (p_ref[i].astype(F32) for i in range(3))
        acc = None
        for k in range(N_CHIPS):
            rel = jnp.bitwise_xor(me, k)
            term = jnp.where(rel == 0, mine, jnp.where(rel == 2, flip_x, jnp.where(rel == 1, flip_y, flip_xy)))
            acc = term if acc is None else acc + term
        o_ref[...] = acc

    return pl.pallas_call(
        body, name=name,
        grid_spec=pltpu.PrefetchScalarGridSpec(
            num_scalar_prefetch=1, grid=(half // tr,),
            in_specs=[pl.BlockSpec((None, tr, c), lambda j, chip_ref: (chip_ref[0], j, 0)),
                      pl.BlockSpec((3, tr, c), lambda j, chip_ref: (0, j, 0))],
            out_specs=pl.BlockSpec((tr, c), lambda j, chip_ref: (j, 0))),
        out_shape=pltpu.HBM((half, c), F32), compiler_params=_params("parallel"),
    )(chip, own, parts)


def _adamw(name, w, g_mine, g_theirs, m, v, core):
    _, r, c = w.shape
    half = r // 2
    tr = _rows_tile(half)
    nth = half // tr

    def body(core_ref, w_ref, gm_ref, gt_ref, m_ref, v_ref, g_ref, d_ref, m2_ref, v2_ref):
        g = jnp.where(pl.program_id(0) == core_ref[0], gm_ref[...], gt_ref[...])
        delta, m2, v2 = _adam_update(w_ref[...], g, m_ref[...], v_ref[...])
        g_ref[...] = g
        d_ref[...] = delta
        m2_ref[...] = m2
        v2_ref[...] = v2

    full = pl.BlockSpec((None, tr, c), lambda h, j, core_ref: (0, h * nth + j, 0))
    part = pl.BlockSpec((tr, c), lambda h, j, core_ref: (j, 0))
    shape = jax.ShapeDtypeStruct((1, r, c), F32)
    return pl.pallas_call(
        body, name=name,
        grid_spec=pltpu.PrefetchScalarGridSpec(
            num_scalar_prefetch=1, grid=(2, nth), in_specs=[full, part, part, full, full], out_specs=[full] * 4),
        out_shape=[shape] * 4, compiler_params=_params("parallel", "parallel"),
    )(core, w, g_mine, g_theirs, m, v)


def _rel_bias_table(name, rel_bias):
    padded = jnp.pad(rel_bias, ((0, 0), (0, N_REL_PAD - N_REL)))

    def body(rb_ref, o_ref):
        ridx = lax.broadcasted_iota(jnp.int32, (N_REL_PAD, BAND), 0)
        sidx = lax.broadcasted_iota(jnp.int32, (N_REL_PAD, BAND), 1)
        rb = rb_ref[...]

        def step(tq, carry):
            rel = jnp.clip(tq + KPAD - sidx, -REL_CLIP, REL_CLIP) + REL_CLIP
            onehot = jnp.where(rel == ridx, 1.0, 0.0).astype(BF16)
            o_ref[tq] = _dot_exact_rhs(rb, onehot)
            return carry

        lax.fori_loop(0, CHUNK, step, 0)

    vm = pl.BlockSpec(memory_space=pltpu.VMEM)
    table = pl.pallas_call(
        body, name=name, in_specs=[vm], out_specs=vm,
        out_shape=jax.ShapeDtypeStruct((CHUNK, ATTN_HEADS, BAND), F32),
    )(padded)
    return table.transpose(1, 0, 2)


def _adamw_small(name, w, parts, m, v):
    def body(w_ref, p_ref, m_ref, v_ref, g_ref, d_ref, m2_ref, v2_ref):
        g = p_ref[0]
        for i in range(1, N_DEV):
            g = g + p_ref[i]
        delta, m2, v2 = _adam_update(w_ref[...], g, m_ref[...], v_ref[...])
        g_ref[...] = g
        d_ref[...] = delta
        m2_ref[...] = m2
        v2_ref[...] = v2

    vm = pl.BlockSpec(memory_space=pltpu.VMEM)
    shape = jax.ShapeDtypeStruct((SMALL_ROWS, SMALL_COLS), F32)
    return pl.pallas_call(
        body, name=name, in_specs=[vm] * 4, out_specs=[vm] * 4, out_shape=[shape] * 4,
    )(w, parts, m, v)


def _position():
    return lax.axis_index("x"), lax.axis_index("y"), lax.axis_index("c")


def _other_chips(x, y):
    return [(1 - x, y), (x, 1 - y), (1 - x, 1 - y)]


ANY = pl.BlockSpec(memory_space=pl.ANY)
PAIR_ID = 0


def _pair_handshake():
    x, y, c = _position()
    barrier = pltpu.get_barrier_semaphore()
    pl.semaphore_signal(barrier, inc=1, device_id=(x, y, 1 - c), device_id_type=MESH)
    pl.semaphore_wait(barrier, 1)


PAIR_CALL = pltpu.CompilerParams(collective_id=PAIR_ID)


HBM = pl.BlockSpec(memory_space=pltpu.HBM)
SEM = pl.BlockSpec(memory_space=pltpu.SEMAPHORE)
SPLIT_COPY = pltpu.SideEffectType.DATAFLOW_SIDE_EFFECTING


def _gather_copy(shards, outs, send_sem, recv_sem, i, j):
    x, y, c = _position()
    chips = _other_chips(x, y)
    half = shards[i].shape[0] // 2
    rows = pl.ds(pl.multiple_of(c * half, 16), half)
    return pltpu.make_async_remote_copy(
        src_ref=shards[i].at[rows, :], dst_ref=outs[i].at[2 * x + y, rows, :],
        send_sem=send_sem.at[3 * i + j], recv_sem=recv_sem.at[3 * i + j],
        device_id=(chips[j][0], chips[j][1], c), device_id_type=MESH)


def _gather_start(name, shards, after):
    n = len(shards)

    def body(*refs):
        srcs, outs = refs[:n], refs[n:2 * n]
        send_sem, recv_sem = refs[2 * n + len(after)], refs[2 * n + len(after) + 1]
        token = refs[-1]
        for i in range(n):
            for j in range(3):
                _gather_copy(srcs, outs, send_sem, recv_sem, i, j).start()
        token[...] = jnp.zeros_like(token)

    full = [(N_CHIPS,) + s.shape for s in shards]
    res = pl.pallas_call(
        body,
        name=name,
        in_specs=[HBM] * (2 * n) + [ANY] * len(after),
        out_specs=[SEM, SEM] + [HBM] * (2 * n) + [pl.BlockSpec(memory_space=pltpu.VMEM)],
        out_shape=[pltpu.SemaphoreType.DMA((3 * n,)), pltpu.SemaphoreType.DMA((3 * n,))]
        + [pltpu.HBM(s.shape, s.dtype) for s in shards]
        + [pltpu.HBM(shp, s.dtype) for shp, s in zip(full, shards)]
        + [jax.ShapeDtypeStruct((8, LANES), F32)],
        input_output_aliases={i: 2 + i for i in range(2 * n)},
        compiler_params=pltpu.CompilerParams(has_side_effects=SPLIT_COPY),
    )(*[pltpu.with_memory_space_constraint(s, pltpu.HBM) for s in shards],
      *[pltpu.with_memory_space_constraint(lax.empty(shp, s.dtype), pltpu.HBM) for shp, s in zip(full, shards)],
      *after)
    return res[0], res[1], list(res[2:2 + n]), list(res[2 + n:2 + 2 * n]), res[-1]


def _gather_wait(name, send_sem, recv_sem, shards, outs, after):
    n = len(shards)

    def body(*refs):
        srcs, out_refs = refs[:n], refs[n:2 * n]
        send_ref, recv_ref = refs[2 * n], refs[2 * n + 1]
        for i in range(n):
            for j in range(3):
                copy = _gather_copy(srcs, out_refs, send_ref, recv_ref, i, j)
                copy.wait_send()
                copy.wait_recv()

    res = pl.pallas_call(
        body,
        name=name,
        in_specs=[HBM] * (2 * n) + [SEM, SEM] + [ANY] * len(after),
        out_specs=[HBM] * (2 * n),
        out_shape=[pltpu.HBM(s.shape, s.dtype) for s in shards] + [pltpu.HBM(o.shape, o.dtype) for o in outs],
        input_output_aliases={i: i for i in range(2 * n)},
        compiler_params=pltpu.CompilerParams(has_side_effects=SPLIT_COPY),
    )(*shards, *outs, send_sem, recv_sem, *after)
    return list(res[:n]), list(res[n:])


def _join_copies(srcs, ins, outs, own_send, own_recv, half_send, half_recv):
    x, y, c = _position()
    chips = _other_chips(x, y)
    copies = []
    for i in range(len(srcs)):
        copies.append(pltpu.make_async_remote_copy(
            src_ref=srcs[i], dst_ref=outs[i].at[2 * x + y], send_sem=own_send.at[i], recv_sem=own_recv.at[i],
            device_id=(x, y, 1 - c), device_id_type=MESH))
        half = srcs[i].shape[0] // 2
        rows = pl.ds(pl.multiple_of(c * half, 16), half)
        for j in range(3):
            slot = 2 * chips[j][0] + chips[j][1]
            copies.append(pltpu.make_async_remote_copy(
                src_ref=ins[i].at[slot, rows, :], dst_ref=outs[i].at[slot, rows, :],
                send_sem=half_send.at[3 * i + j], recv_sem=half_recv.at[3 * i + j],
                device_id=(x, y, 1 - c), device_id_type=MESH))
    return copies


def _gather_join(name, shards, outs):
    n = len(shards)

    def body(*refs):
        _pair_handshake()
        copies = _join_copies(refs[:n], refs[n:2 * n], refs[2 * n:3 * n], *refs[3 * n:])
        for cp in copies:
            cp.start()
        for cp in copies:
            cp.wait()

    return pl.pallas_call(
        body,
        name=name,
        in_specs=[ANY] * (2 * n),
        out_specs=[HBM] * n,
        out_shape=[pltpu.HBM(o.shape, o.dtype) for o in outs],
        input_output_aliases={n + i: i for i in range(n)},
        scratch_shapes=[pltpu.SemaphoreType.DMA((n,))] * 2 + [pltpu.SemaphoreType.DMA((3 * n,))] * 2,
        compiler_params=PAIR_CALL,
    )(*shards, *outs)


def _join_start(name, shards, outs):
    n = len(shards)

    def body(*refs):
        _pair_handshake()
        srcs, arrs = refs[:n], refs[n:2 * n]
        sems = refs[2 * n:2 * n + 4]
        token = refs[-1]
        for cp in _join_copies(srcs, arrs, arrs, *sems):
            cp.start()
        token[...] = jnp.zeros_like(token)

    res = pl.pallas_call(
        body,
        name=name,
        in_specs=[HBM] * (2 * n),
        out_specs=[SEM] * 4 + [HBM] * (2 * n) + [pl.BlockSpec(memory_space=pltpu.VMEM)],
        out_shape=[pltpu.SemaphoreType.DMA((n,))] * 2 + [pltpu.SemaphoreType.DMA((3 * n,))] * 2
        + [pltpu.HBM(s.shape, s.dtype) for s in shards] + [pltpu.HBM(o.shape, o.dtype) for o in outs]
        + [jax.ShapeDtypeStruct((8, LANES), F32)],
        input_output_aliases={i: 4 + i for i in range(2 * n)},
        compiler_params=pltpu.CompilerParams(has_side_effects=SPLIT_COPY, collective_id=PAIR_ID),
    )(*shards, *outs)
    return list(res[:4]), list(res[4:4 + n]), list(res[4 + n:4 + 2 * n]), res[-1]


def _join_wait(name, sems, shards, outs, after):
    n = len(shards)

    def body(*refs):
        srcs, arrs = refs[:n], refs[n:2 * n]
        for cp in _join_copies(srcs, arrs, arrs, *refs[2 * n:2 * n + 4]):
            cp.wait_send()
            cp.wait_recv()

    res = pl.pallas_call(
        body,
        name=name,
        in_specs=[HBM] * (2 * n) + [SEM] * 4 + [ANY] * len(after),
        out_specs=[HBM] * (2 * n),
        out_shape=[pltpu.HBM(s.shape, s.dtype) for s in shards] + [pltpu.HBM(o.shape, o.dtype) for o in outs],
        input_output_aliases={i: i for i in range(2 * n)},
        compiler_params=pltpu.CompilerParams(has_side_effects=SPLIT_COPY),
    )(*shards, *outs, *sems, *after)
    return list(res[n:])


def _pair_copy(grads, lands, send_sem, recv_sem, i):
    x, y, c = _position()
    half = grads[i].shape[1] // 2
    give = pl.ds(pl.multiple_of((1 - c) * half, 16), half)
    return pltpu.make_async_remote_copy(
        src_ref=grads[i].at[:, give, :], dst_ref=lands[i], send_sem=send_sem.at[i], recv_sem=recv_sem.at[i],
        device_id=(x, y, 1 - c), device_id_type=MESH)


def _pair_start(name, grads):
    n = len(grads)

    def body(*refs):
        _pair_handshake()
        srcs, lands = refs[:n], refs[n:2 * n]
        send_sem, recv_sem = refs[2 * n], refs[2 * n + 1]
        token = refs[-1]
        for i in range(n):
            _pair_copy(srcs, lands, send_sem, recv_sem, i).start()
        token[...] = jnp.zeros_like(token)

    halves = [(g.shape[0], g.shape[1] // 2, g.shape[2]) for g in grads]
    res = pl.pallas_call(
        body,
        name=name,
        in_specs=[HBM] * (2 * n),
        out_specs=[SEM, SEM] + [HBM] * (2 * n) + [pl.BlockSpec(memory_space=pltpu.VMEM)],
        out_shape=[pltpu.SemaphoreType.DMA((n,)), pltpu.SemaphoreType.DMA((n,))]
        + [pltpu.HBM(g.shape, g.dtype) for g in grads]
        + [pltpu.HBM(shp, g.dtype) for shp, g in zip(halves, grads)]
        + [jax.ShapeDtypeStruct((8, LANES), F32)],
        input_output_aliases={i: 2 + i for i in range(2 * n)},
        compiler_params=pltpu.CompilerParams(has_side_effects=SPLIT_COPY, collective_id=PAIR_ID),
    )(*[pltpu.with_memory_space_constraint(g, pltpu.HBM) for g in grads],
      *[pltpu.with_memory_space_constraint(lax.empty(shp, g.dtype), pltpu.HBM) for shp, g in zip(halves, grads)])
    return res[0], res[1], list(res[2:2 + n]), list(res[2 + n:2 + 2 * n]), res[-1]


def _pair_wait(name, send_sem, recv_sem, grads, lands, after):
    n = len(grads)

    def body(*refs):
        srcs, land_refs = refs[:n], refs[n:2 * n]
        send_ref, recv_ref = refs[2 * n], refs[2 * n + 1]
        for i in range(n):
            copy = _pair_copy(srcs, land_refs, send_ref, recv_ref, i)
            copy.wait_send()
            copy.wait_recv()

    res = pl.pallas_call(
        body,
        name=name,
        in_specs=[HBM] * (2 * n) + [SEM, SEM, ANY],
        out_specs=[HBM] * (2 * n),
        out_shape=[pltpu.HBM(g.shape, g.dtype) for g in grads] + [pltpu.HBM(l.shape, l.dtype) for l in lands],
        input_output_aliases={i: i for i in range(2 * n)},
        compiler_params=pltpu.CompilerParams(has_side_effects=SPLIT_COPY),
    )(*grads, *lands, send_sem, recv_sem, after)
    return list(res[:n]), list(res[n:])


def _scatter_copy(srcs, lands, send_sem, recv_sem, i, j):
    x, y, c = _position()
    chips = _other_chips(x, y)
    return pltpu.make_async_remote_copy(
        src_ref=srcs[i].at[2 * chips[j][0] + chips[j][1]], dst_ref=lands[i].at[j],
        send_sem=send_sem.at[3 * i + j], recv_sem=recv_sem.at[3 * i + j],
        device_id=(chips[j][0], chips[j][1], c), device_id_type=MESH)


def _scatter_start(name, sums):
    n = len(sums)

    def body(*refs):
        srcs, lands = refs[:n], refs[n:2 * n]
        send_sem, recv_sem = refs[2 * n], refs[2 * n + 1]
        token = refs[-1]
        for i in range(n):
            for j in range(3):
                _scatter_copy(srcs, lands, send_sem, recv_sem, i, j).start()
        token[...] = jnp.zeros_like(token)

    land_shapes = [(3,) + s.shape[1:] for s in sums]
    res = pl.pallas_call(
        body,
        name=name,
        in_specs=[HBM] * (2 * n),
        out_specs=[SEM, SEM] + [HBM] * (2 * n) + [pl.BlockSpec(memory_space=pltpu.VMEM)],
        out_shape=[pltpu.SemaphoreType.DMA((3 * n,)), pltpu.SemaphoreType.DMA((3 * n,))]
        + [pltpu.HBM(s.shape, s.dtype) for s in sums]
        + [pltpu.HBM(shp, s.dtype) for shp, s in zip(land_shapes, sums)]
        + [jax.ShapeDtypeStruct((8, LANES), F32)],
        input_output_aliases={i: 2 + i for i in range(2 * n)},
        compiler_params=pltpu.CompilerParams(has_side_effects=SPLIT_COPY),
    )(*[pltpu.with_memory_space_constraint(s, pltpu.HBM) for s in sums],
      *[pltpu.with_memory_space_constraint(lax.empty(shp, s.dtype), pltpu.HBM) for shp, s in zip(land_shapes, sums)])
    return res[0], res[1], list(res[2:2 + n]), list(res[2 + n:2 + 2 * n]), res[-1]


def _scatter_wait(name, send_sem, recv_sem, sums, lands, after):
    n = len(sums)

    def body(*refs):
        srcs, land_refs = refs[:n], refs[n:2 * n]
        send_ref, recv_ref = refs[2 * n], refs[2 * n + 1]
        for i in range(n):
            for j in range(3):
                copy = _scatter_copy(srcs, land_refs, send_ref, recv_ref, i, j)
                copy.wait_send()
                copy.wait_recv()

    res = pl.pallas_call(
        body,
        name=name,
        in_specs=[HBM] * (2 * n) + [SEM, SEM, ANY],
        out_specs=[HBM] * (2 * n),
        out_shape=[pltpu.HBM(s.shape, s.dtype) for s in sums] + [pltpu.HBM(l.shape, l.dtype) for l in lands],
        input_output_aliases={i: i for i in range(2 * n)},
        compiler_params=pltpu.CompilerParams(has_side_effects=SPLIT_COPY),
    )(*sums, *lands, send_sem, recv_sem, after)
    return list(res[:n]), list(res[n:])


def _pair_join(name, halves, small=None):
    n = len(halves)
    if small is None:
        def body_plain(*refs):
            _pair_handshake()
            ins, outs = refs[:n], refs[n:2 * n]
            send_sem, recv_sem = refs[2 * n:]
            x, y, c = _position()
            swaps = [pltpu.make_async_remote_copy(
                src_ref=ins[i], dst_ref=outs[i], send_sem=send_sem.at[i], recv_sem=recv_sem.at[i],
                device_id=(x, y, 1 - c), device_id_type=MESH) for i in range(n)]
            for swap in swaps:
                swap.start()
            for swap in swaps:
                swap.wait()

        return pl.pallas_call(
            body_plain,
            name=name,
            in_specs=[ANY] * n,
            out_specs=[ANY] * n,
            out_shape=[jax.ShapeDtypeStruct(h.shape, h.dtype) for h in halves],
            scratch_shapes=[pltpu.SemaphoreType.DMA((n,))] * 2,
            compiler_params=PAIR_CALL,
        )(*halves)

    def body(*refs):
        ins, small_ref = refs[:n], refs[n]
        outs, all_ref = refs[n + 1:2 * n + 1], refs[2 * n + 1]
        send_sem, recv_sem, sm_send, sm_recv, sm_local = refs[2 * n + 2:]
        x, y, c = _position()
        swaps = []
        for i in range(n):
            swap = pltpu.make_async_remote_copy(
                src_ref=ins[i], dst_ref=outs[i], send_sem=send_sem.at[i], recv_sem=recv_sem.at[i],
                device_id=(x, y, 1 - c), device_id_type=MESH)
            swap.start()
            swaps.append(swap)
        me = 4 * x + 2 * y + c
        sm_own = pltpu.make_async_copy(small_ref, all_ref.at[me], sm_local)
        sm_own.start()
        pushes, arrivals = [], []
        for mask in range(1, N_DEV):
            px, py, pc = x ^ (mask >> 2), y ^ ((mask >> 1) & 1), c ^ (mask & 1)
            pushes.append(pltpu.make_async_remote_copy(
                src_ref=small_ref, dst_ref=all_ref.at[me], send_sem=sm_send.at[mask - 1], recv_sem=sm_recv.at[mask - 1],
                device_id=(px, py, pc), device_id_type=MESH))
            arrivals.append(pltpu.make_async_remote_copy(
                src_ref=small_ref, dst_ref=all_ref.at[4 * px + 2 * py + pc], send_sem=sm_send.at[mask - 1],
                recv_sem=sm_recv.at[mask - 1], device_id=(px, py, pc), device_id_type=MESH))
        for cp in pushes:
            cp.start()
        for swap in swaps:
            swap.wait()
        for cp in arrivals:
            cp.wait_recv()
        for cp in pushes:
            cp.wait_send()
        sm_own.wait()

    res = pl.pallas_call(
        body,
        name=name,
        in_specs=[ANY] * (n + 1),
        out_specs=[ANY] * (n + 1),
        out_shape=[jax.ShapeDtypeStruct(h.shape, h.dtype) for h in halves]
        + [jax.ShapeDtypeStruct((N_DEV,) + small.shape, small.dtype)],
        scratch_shapes=[pltpu.SemaphoreType.DMA((n,))] * 2 + [pltpu.SemaphoreType.DMA((N_DEV - 1,))] * 2
        + [pltpu.SemaphoreType.DMA(())],
    )(*halves, small)
    return res[:n], res[n]


def _lower_bound(lbp):
    return jax.nn.softmax(lbp, axis=0)[0:1]


def _local_step(x, target, g1, gm, g2, gq, gk, go, rel_bias, lbp, weights, on_grads, grads_sent):
    b, s, d = x.shape
    t = b * s
    x0 = x.reshape(t, d)
    tgt = target.reshape(t, d)
    gq_t = jnp.tile(gq, (1, ATTN_HEADS))
    gk_t = jnp.tile(gk, (1, ATTN_HEADS))
    lb = _lower_bound(lbp)
    table = _band_table(_rel_bias_table("rel_bias_table", rel_bias))

    h1 = _rmsnorm_fwd("norm1", x0, g1)
    wg1, wu1, deps1 = weights["first"]((h1, table))
    a1, b1, z1 = _ffn_up("ffn1_up", h1, wg1, wu1, deps1)
    wd1, deps_mid = weights["mid"]((z1,))
    x1, h2 = _ffn_down("ffn1_down", z1, wd1, x0, gm, deps_mid)
    w_in, w_out = weights["mid_rest"]((x1,))
    ns = w_in.shape[0]
    proj = _in_proj("in_proj", h2, w_in)
    proj3 = proj.reshape(b, s, proj.shape[1])
    qn, kn, vb = _qk_prep("qk_prep", proj3, gq_t, gk_t)
    attn = _attn_fwd("attn_fwd", qn, kn, vb, table, weights["last_begin"]((qn,))).reshape(t, ATTN_W)
    mix, oraw, states = _hgrn_fwd("hgrn_fwd", proj, attn, lb, go, b, s)
    x2, h3 = _out_proj("out_proj", mix, w_out, x1, g2)
    wg2, wu2, wd2 = weights["last"]((h3,))
    a2, b2, z2 = _ffn_up("ffn2_up", h3, wg2, wu2)
    dy, dyh, sq = _ffn_down_loss("ffn2_down_loss", z2, wd2, x2, tgt)
    loss = 0.5 * jnp.sum(sq) / d

    da2, db2 = _ffn_bwd_act("ffn2_bwd_act", dyh, wd2, a2, b2)
    dwd2 = _grad_w_cols("ffn2_dwd", z2, dyh)
    dwg2 = _grad_w_cols("ffn2_dwg", da2, h3)
    dwu2 = _grad_w_cols("ffn2_dwu", db2, h3)
    sent2 = on_grads("ffn2", {"ffn2_w_gate": dwg2, "ffn2_w_up": dwu2, "ffn2_w_down": dwd2})
    dx2, dx2b, dg2 = _ffn_bwd_in("ffn2_bwd_in", da2, db2, wg2, wu2, x2, g2, dy, 1.0, sent2)
    sent2 = grads_sent("ffn2", dx2b)

    dwout = _grad_w_out("dw_out", mix, dx2b)
    dmix = _out_proj_bwd("out_proj_bwd", dx2b, w_out, sent2)
    dqn, dkn, dvn, dbe, dbo = _attn_bwd("attn_bwd", qn, kn, vb, table, dmix.reshape(b, s, dmix.shape[1]))
    dpq, dpk, dpv, dgq, dgk = _qk_prep_bwd("qk_prep_bwd", proj3, dqn, dkn, dvn, gq_t, gk_t)
    dpq, dpk, dpv = (a.reshape(t, ATTN_W) for a in (dpq, dpk, dpv))
    dproj, dlb, dgo = _hgrn_bwd("hgrn_bwd", proj, (dpq, dpk, dpv), lb, go, oraw, states, dmix, b, s)
    dwin = _grad_w_in("dw_in", h2, dproj, ns)
    dx1, dx1h, dgm = _in_proj_bwd("in_proj_bwd", dproj, w_in, x1, gm, dx2, 0.5)

    dwd1 = _grad_w_cols("ffn1_dwd", z1, dx1h)
    sent_mix = on_grads("mix", {"w_in": dwin, "w_out": dwout.reshape(ns, dwout.shape[0] // ns, d),
                                "ffn1_w_down": dwd1})
    da1, db1 = _ffn_bwd_act("ffn1_bwd_act", dx1h, wd1, a1, b1, sent_mix)
    sent_mix = grads_sent("mix", da1)
    dwg1 = _grad_w_cols("ffn1_dwg", da1, h1, sent_mix)
    dwu1 = _grad_w_cols("ffn1_dwu", db1, h1)
    on_grads("ffn1", {"ffn1_w_gate": dwg1, "ffn1_w_up": dwu1})
    sent1 = grads_sent("ffn1", None)
    dx0, dg1 = _ffn_bwd_in("ffn1_bwd_in", da1, db1, wg1, wu1, x0, g1, dx1, None, sent1)

    nt = dg1.shape[0]
    sg = _small_grads(
        "small_grads", dg1.reshape(nt, d), dgm.reshape(nt, d), dg2.reshape(nt, d),
        dgq.reshape(-1, ATTN_W), dgk.reshape(-1, ATTN_W), dbe.transpose(1, 0, 2), dbo.transpose(1, 0, 2),
        dlb.reshape(b, HGRN_W), dgo.reshape(b, HGRN_W), lbp)
    g1g, gmg, g2g, gqg, gkg, rbg, lbg, gog = sg
    small = _pack_small(g1g, gmg, g2g, lbg, rbg[:, :N_REL], gqg, gkg, gog, loss)
    return dx0.reshape(b, s, d), small


LOSS_SLOT = 7 * SMALL_COLS + 2 * ATTN_DH + HGRN_DH


def _pack_small(g1, gm, g2, lbp, rel_bias, gq, gk, go, loss=None):
    flat = [g1.reshape(-1), gm.reshape(-1), g2.reshape(-1), lbp.reshape(-1), rel_bias.reshape(-1)]
    n_bias = 3 * SMALL_COLS - rel_bias.size
    heads = [gq.reshape(-1), gk.reshape(-1), go.reshape(-1)]
    heads.append(jnp.zeros((1,), F32) if loss is None else loss.reshape(1))
    n_tail = SMALL_COLS - sum(h.size for h in heads)
    return jnp.concatenate(flat + [jnp.zeros((n_bias,), F32)] + heads + [jnp.zeros((n_tail,), F32)]).reshape(
        SMALL_ROWS, SMALL_COLS)


def _unpack_small(p, d):
    flat = p.reshape(-1)
    o = 3 * d
    g1, gm, g2 = p[0:1], p[1:2], p[2:3]
    lbp = flat[o:o + 2 * HGRN_W].reshape(2, HGRN_W)
    o = 4 * SMALL_COLS
    rel = flat[o:o + ATTN_HEADS * N_REL].reshape(1, ATTN_HEADS, N_REL)
    o = 7 * SMALL_COLS
    gq = flat[o:o + ATTN_DH].reshape(1, ATTN_DH)
    gk = flat[o + ATTN_DH:o + 2 * ATTN_DH].reshape(1, ATTN_DH)
    go = flat[o + 2 * ATTN_DH:o + 2 * ATTN_DH + HGRN_DH].reshape(1, HGRN_DH)
    return g1, gm, g2, gq, gk, rel, lbp, go


def kernel(x, ffn1_norm_g, ffn1_w_gate, ffn1_w_up, ffn1_w_down, mix_norm_g, w_in, attn_q_norm_g, attn_k_norm_g, attn_rel_bias, hgrn_lower_bounds, hgrn_out_norm_g, w_out, ffn2_norm_g, ffn2_w_gate, ffn2_w_up, ffn2_w_down, loss_target, m_ffn1_norm_g, m_ffn1_w_gate, m_ffn1_w_up, m_ffn1_w_down, m_mix_norm_g, m_w_in, m_attn_q_norm_g, m_attn_k_norm_g, m_attn_rel_bias, m_hgrn_lower_bounds, m_hgrn_out_norm_g, m_w_out, m_ffn2_norm_g, m_ffn2_w_gate, m_ffn2_w_up, m_ffn2_w_down, v_ffn1_norm_g, v_ffn1_w_gate, v_ffn1_w_up, v_ffn1_w_down, v_mix_norm_g, v_w_in, v_attn_q_norm_g, v_attn_k_norm_g, v_attn_rel_bias, v_hgrn_lower_bounds, v_hgrn_out_norm_g, v_w_out, v_ffn2_norm_g, v_ffn2_w_gate, v_ffn2_w_up, v_ffn2_w_down):
    d = x.shape[-1]
    big_w = [ffn1_w_gate, ffn1_w_up, ffn1_w_down, w_in, w_out, ffn2_w_gate, ffn2_w_up, ffn2_w_down]
    big_m = [m_ffn1_w_gate, m_ffn1_w_up, m_ffn1_w_down, m_w_in, m_w_out, m_ffn2_w_gate, m_ffn2_w_up, m_ffn2_w_down]
    big_v = [v_ffn1_w_gate, v_ffn1_w_up, v_ffn1_w_down, v_w_in, v_w_out, v_ffn2_w_gate, v_ffn2_w_up, v_ffn2_w_down]
    big_names = ["ffn1_w_gate", "ffn1_w_up", "ffn1_w_down", "w_in", "w_out", "ffn2_w_gate", "ffn2_w_up", "ffn2_w_down"]
    flipped = {nm for nm in big_names if nm.endswith("gate") or nm.endswith("up")}
    flip = lambda nm, a: jnp.swapaxes(a, 1, 2) if nm in flipped else a
    big_w, big_m, big_v = ([flip(nm, a) for nm, a in zip(big_names, arrs)] for arrs in (big_w, big_m, big_v))

    shards = [w[0].astype(BF16) for w in big_w]
    start_a = _gather_start("gather_start_up1", shards[:2], ())
    start_b = _gather_start("gather_start_mid", shards[2:5], (start_a[4],))
    start_c = _gather_start("gather_start_ffn2", shards[5:], (start_b[4],))

    pending = {}

    def arrived(tag, started, after):
        send_sem, recv_sem, srcs, outs, _ = started
        return _gather_wait("gather_wait_" + tag, send_sem, recv_sem, srcs, outs, after)

    def first_weights(after):
        return (*_gather_join("gather_join_up1", *arrived("up1", start_a, after)), (start_c[4],))

    def mid_weights(after):
        srcs, outs = arrived("mid", start_b, after)
        (wd1,) = _gather_join("gather_join_wd1", srcs[:1], outs[:1])
        pending["mid"] = _join_start("join_start_mid", srcs[1:], outs[1:])
        return wd1, (pending["mid"][3],)

    def mid_rest(after):
        sems, srcs, outs, _ = pending["mid"]
        win_f, wout_f = _join_wait("join_wait_mid", sems, srcs, outs, after)
        return win_f, wout_f.reshape(wout_f.shape[0] * wout_f.shape[1], d)

    def last_begin(after):
        pending["ffn2"] = _join_start("join_start_ffn2", *arrived("ffn2", start_c, after))
        return (pending["ffn2"][3],)

    def last_weights(after):
        sems, srcs, outs, _ = pending["ffn2"]
        return _join_wait("join_wait_ffn2", sems, srcs, outs, after)

    weights = {"first": first_weights, "mid": mid_weights, "mid_rest": mid_rest, "last_begin": last_begin,
               "last": last_weights}

    core = lax.axis_index("c").astype(jnp.int32).reshape(1)
    chip = (2 * lax.axis_index("x") + lax.axis_index("y")).astype(jnp.int32).reshape(1)
    started = {}

    def on_grads(tag, grads):
        names = list(grads)
        started[tag] = (names, _pair_start("pair_start_" + tag, [grads[nm] for nm in names]))
        return (started[tag][1][4],)

    def grads_sent(tag, after):
        names, (send_sem, recv_sem, grads, lands, token) = started[tag]
        grads, theirs = _pair_wait("pair_wait_" + tag, send_sem, recv_sem, grads, lands, token if after is None else after)
        sums = [_pair_sum("pair_sum_" + nm, g, th, core) for nm, g, th in zip(names, grads, theirs)]
        started[tag] = (names, _scatter_start("scatter_start_" + tag, sums))
        return (started[tag][1][4],)

    grad_x, small_g = _local_step(
        x, loss_target, ffn1_norm_g, mix_norm_g, ffn2_norm_g, attn_q_norm_g, attn_k_norm_g, hgrn_out_norm_g,
        attn_rel_bias[0], hgrn_lower_bounds, weights, on_grads, grads_sent)

    def finish(tag, after):
        names, (send_sem, recv_sem, sums, lands, _) = started[tag]
        sums, lands = _scatter_wait("scatter_wait_" + tag, send_sem, recv_sem, sums, lands, after)
        return names, [_chip_sum("chip_sum_" + nm, sm, ld, chip) for nm, sm, ld in zip(names, sums, lands)]

    by_name = {nm: (w, m, v) for nm, w, m, v in zip(big_names, big_w, big_m, big_v)}
    updated = {}

    def update(names, halves, other_halves):
        for nm, mine, theirs in zip(names, halves, other_halves):
            w, m, v = by_name[nm]
            updated[nm] = _adamw("adamw_" + nm, w, mine, theirs, m, v, core)

    last_token = started["ffn1"][1][4]
    names_a, halves_a = finish("ffn2", last_token)
    names_m, halves_m = finish("mix", last_token)
    names_a, halves_a = names_a + names_m, halves_a + halves_m
    update(names_a, halves_a, _pair_join("pair_join_early", halves_a))
    names_b, halves_b = finish("ffn1", updated[names_a[-1]][1])
    others_b, small_all = _pair_join("pair_join_last", halves_b, small_g)
    update(names_b, halves_b, others_b)
    big_out = [updated[nm] for nm in big_names]

    pack = lambda g1, gm, g2, gq, gk, rel, lbp, go: _pack_small(g1, gm, g2, lbp, rel[0], gq, gk, go)
    small_w = pack(ffn1_norm_g, mix_norm_g, ffn2_norm_g, attn_q_norm_g, attn_k_norm_g, attn_rel_bias, hgrn_lower_bounds, hgrn_out_norm_g)
    small_m = pack(m_ffn1_norm_g, m_mix_norm_g, m_ffn2_norm_g, m_attn_q_norm_g, m_attn_k_norm_g, m_attn_rel_bias, m_hgrn_lower_bounds, m_hgrn_out_norm_g)
    small_v = pack(v_ffn1_norm_g, v_mix_norm_g, v_ffn2_norm_g, v_attn_q_norm_g, v_attn_k_norm_g, v_attn_rel_bias, v_hgrn_lower_bounds, v_hgrn_out_norm_g)
    small_res = _adamw_small("adamw_small", small_w, small_all, small_m, small_v)
    small_out = [_unpack_small(p, d) for p in small_res]
    loss = small_res[0].reshape(-1)[LOSS_SLOT]

    def assemble(kind):
        bg = [flip(nm, o[kind]) for nm, o in zip(big_names, big_out)]
        g1, gm, g2, gq, gk, rel, lbp, go = small_out[kind]
        return [g1, bg[0], bg[1], bg[2], gm, bg[3], gq, gk, rel, lbp, go, bg[4], g2, bg[5], bg[6], bg[7]]

    return (loss, grad_x, *assemble(0), *assemble(1), *assemble(2), *assemble(3))
```

```python
import functools

import jax
import jax.numpy as jnp
from jax import lax
from jax.experimental import pallas as pl
from jax.experimental.pallas import tpu as pltpu

F32 = jnp.float32
BF16 = jnp.bfloat16
MESH = pl.DeviceIdType.MESH

N_CHIPS = 4
N_DEV = 8
CHUNK = 64
ATTN_HEADS = 8
ATTN_DH = 64
ATTN_W = ATTN_HEADS * ATTN_DH
HGRN_HEADS = 4
HGRN_DH = 128
HGRN_W = HGRN_HEADS * HGRN_DH
LEFT_CHUNKS = 8
BAND = (LEFT_CHUNKS + 1) * CHUNK
KPAD = LEFT_CHUNKS * CHUNK
REL_CLIP = 128
N_REL = 2 * REL_CLIP + 1
N_REL_PAD = 384
RMS_EPS = 1e-6
LANES = 128
SMALL_ROWS = 8
SMALL_COLS = 1024

ADAM_LR = 0.001
ADAM_B1 = 0.9
ADAM_B2 = 0.999
ADAM_EPS = 1e-08
ADAM_WD = 0.01
ADAM_STEP = 10

NN = (((1,), (0,)), ((), ()))
NT = (((1,), (1,)), ((), ()))
TN = (((0,), (0,)), ((), ()))

VMEM_LIMIT = 48 * 1024 * 1024
MXU_WIDTH = 256
COL_CHUNK = 3 * MXU_WIDTH


def _sigmoid(x):
    return 1.0 / (1.0 + jnp.exp(-x))


def _silu(x):
    return x * _sigmoid(x)


def _dot(a, b, dims=NN):
    return lax.dot_general(a, b, dims, preferred_element_type=F32)


def _split3(x):
    hi = x.astype(BF16)
    r1 = x - hi.astype(F32)
    mid = r1.astype(BF16)
    lo = (r1 - mid.astype(F32)).astype(BF16)
    return hi, mid, lo


def _dot_exact_rhs(x, mat, dims=NN, pieces=3):
    hi, mid, lo = _split3(x)
    out = _dot(hi, mat, dims) + _dot(mid, mat, dims)
    return out + _dot(lo, mat, dims) if pieces == 3 else out


def _dot_exact_lhs(mat, x, dims=NN):
    hi, mid, lo = _split3(x)
    return _dot(mat, hi, dims) + _dot(mat, mid, dims) + _dot(mat, lo, dims)


def _params(*sem):
    return pltpu.CompilerParams(dimension_semantics=sem, vmem_limit_bytes=VMEM_LIMIT)


def _mm(name, ins, terms, n_acc, grid, acc_shape, outs, epilogue, extras=(), deps=()):
    nk = grid[2]
    ni, ne, nd, no = len(ins), len(extras), len(deps), len(outs)

    def body(*refs):
        in_refs = refs[:ni]
        ex_refs = refs[ni:ni + ne]
        out_refs = refs[ni + ne + nd:ni + ne + nd + no]
        acc_refs = refs[ni + ne + nd + no:]

        def products():
            parts = [None] * n_acc
            for ai, li, ri, dims in terms:
                d = _dot(in_refs[li][...], in_refs[ri][...], dims)
                parts[ai] = d if parts[ai] is None else parts[ai] + d
            return parts

        def finish(accs):
            res = epilogue(accs, [e[...] for e in ex_refs])
            for o, r in zip(out_refs, res):
                o[...] = r.astype(o.dtype)

        if nk == 1:
            finish(products())
        else:
            k = pl.program_id(2)

            @pl.when(k == 0)
            def _():
                for a, p in zip(acc_refs, products()):
                    a[...] = p

            if nk > 2:
                @pl.when(jnp.logical_and(k > 0, k < nk - 1))
                def _():
                    for a, p in zip(acc_refs, products()):
                        a[...] += p

            @pl.when(k == nk - 1)
            def _():
                finish([a[...] + p for a, p in zip(acc_refs, products())])

    scratch = [] if nk == 1 else [pltpu.VMEM(acc_shape, F32) for _ in range(n_acc)]
    res = pl.pallas_call(
        body,
        name=name,
        grid=grid,
        in_specs=[s for _, s in ins] + [s for _, s in extras] + [pl.BlockSpec(memory_space=pl.ANY)] * nd,
        out_specs=[s for _, s in outs],
        out_shape=[o for o, _ in outs],
        scratch_shapes=scratch,
        compiler_params=_params("parallel", "parallel", "arbitrary"),
    )(*[a for a, _ in ins], *[a for a, _ in extras], *deps)
    return res


def _staged_shape(w):
    return w.shape if len(w.shape) == 2 else (w.shape[1], w.shape[0] * w.shape[2])


def _stage_weights(w_hbm, w_vmem, sem):
    @pl.when(pl.program_id(0) == 0)
    def _():
        copies = []
        for p, (h, v) in enumerate(zip(w_hbm, w_vmem)):
            if len(h.shape) == 2:
                copies.append(pltpu.make_async_copy(h, v, sem.at[p, 0]))
            else:
                pj = h.shape[2]
                copies += [pltpu.make_async_copy(h.at[j], v.at[:, pl.ds(j * pj, pj)], sem.at[p, j])
                           for j in range(h.shape[0])]
        for cp in copies:
            cp.start()
        for cp in copies:
            cp.wait()


def _staging_scratch(weights):
    return [pltpu.VMEM(_staged_shape(w), w.dtype) for w in weights] + [pltpu.SemaphoreType.DMA((len(weights), N_CHIPS))]


def _mm_rows(name, lhs, weights, dims, t, outs, epilogue, extras=(), deps=()):
    tm = _row_tile(t)
    nl, ne, nd, no = len(lhs), len(extras), len(deps), len(outs)
    halves = [slice(0, tm // 2), slice(tm // 2, tm)] if nl == 1 and tm % 32 == 0 else [slice(0, tm)]

    def body(*refs):
        lhs_refs = refs[:nl]
        w_hbm = refs[nl:2 * nl]
        ex_refs = refs[2 * nl:2 * nl + ne]
        out_refs = refs[2 * nl + ne + nd:2 * nl + ne + nd + no]
        w_vmem = refs[2 * nl + ne + nd + no:3 * nl + ne + nd + no]
        _stage_weights(w_hbm, w_vmem, refs[-1])

        accs = []
        for rows in halves:
            acc = None
            for p in range(nl):
                part = _dot(lhs_refs[p][rows, :], w_vmem[p][...], dims)
                acc = part if acc is None else acc + part
            accs.append(acc)
        for h, (rows, acc) in enumerate(zip(halves, accs)):
            res = epilogue([acc], [e[rows, :] if e.shape[0] == tm else e[...] for e in ex_refs])
            for o, r in zip(out_refs, res):
                if o.shape[0] == tm:
                    o[rows, :] = r.astype(o.dtype)
                elif h == 0:
                    o[...] = r.astype(o.dtype)
                else:
                    o[...] += r.astype(o.dtype)

    return pl.pallas_call(
        body,
        name=name,
        grid=(t // tm,),
        in_specs=[s for _, s in lhs] + [pl.BlockSpec(memory_space=pl.ANY)] * nl + [s for _, s in extras]
        + [pl.BlockSpec(memory_space=pl.ANY)] * nd,
        out_specs=[s for _, s in outs],
        out_shape=[o for o, _ in outs],
        scratch_shapes=_staging_scratch(weights),
        compiler_params=_params("arbitrary"),
    )(*[a for a, _ in lhs], *weights, *[a for a, _ in extras], *deps)


def _col_chunks(f):
    return [(c, min(COL_CHUNK, f - c)) for c in range(0, f, COL_CHUNK)]


def _mm_cols(name, x, weights, dims, n_out, epilogue, extras=(), deps=(), out_dtype=BF16):
    t, k = x.shape
    f = _staged_shape(weights[0])[0 if dims == NT else 1]
    tm = _row_tile(t)
    chunks = _col_chunks(f)
    nw, ne, nd = len(weights), len(extras), len(deps)

    def body(*refs):
        x_ref = refs[0]
        w_hbm = refs[1:1 + nw]
        ex_refs = refs[1 + nw:1 + nw + ne]
        out_refs = refs[1 + nw + ne + nd:1 + nw + ne + nd + n_out]
        w_vmem = refs[1 + nw + ne + nd + n_out:1 + 2 * nw + ne + nd + n_out]
        _stage_weights(w_hbm, w_vmem, refs[-1])

        xv = x_ref[...]

        def dots(c):
            c0, cw = chunks[c]
            return [_dot(xv, w[c0:c0 + cw, :] if dims == NT else w[:, c0:c0 + cw], dims) for w in w_vmem]

        accs = dots(0)
        for c, (c0, cw) in enumerate(chunks):
            nxt = dots(c + 1) if c + 1 < len(chunks) else None
            res = epilogue(accs, [e[:, c0:c0 + cw] for e in ex_refs])
            for o, r in zip(out_refs, res):
                o[:, c0:c0 + cw] = r.astype(o.dtype)
            accs = nxt

    act = pl.BlockSpec((tm, f), lambda i: (i, 0))
    return pl.pallas_call(
        body,
        name=name,
        grid=(t // tm,),
        in_specs=[pl.BlockSpec((tm, k), lambda i: (i, 0))] + [pl.BlockSpec(memory_space=pl.ANY)] * nw + [act] * ne
        + [pl.BlockSpec(memory_space=pl.ANY)] * nd,
        out_specs=[act] * n_out,
        out_shape=[jax.ShapeDtypeStruct((t, f), out_dtype)] * n_out,
        scratch_shapes=_staging_scratch(weights),
        compiler_params=_params("arbitrary"),
    )(x, *weights, *extras, *deps)


def _row_tile(t):
    return 512 if t % 512 == 0 else t


def _k_tile(t):
    return t if t <= 4096 else 1024


def _grad_k_tile(t):
    return 2048 if t % 2048 == 0 else t


def _rmsnorm(xv, g):
    ms = jnp.mean(xv * xv, axis=-1, keepdims=True)
    return xv * lax.rsqrt(ms + RMS_EPS) * g


def _rmsnorm_fwd(name, x, g):
    t, d = x.shape
    tm = _row_tile(t)

    def body(x_ref, g_ref, h_ref):
        h_ref[...] = _rmsnorm(x_ref[...], g_ref[...]).astype(BF16)

    return pl.pallas_call(
        body,
        name=name,
        grid=(t // tm,),
        in_specs=[pl.BlockSpec((tm, d), lambda i: (i, 0)), pl.BlockSpec((1, d), lambda i: (0, 0))],
        out_specs=pl.BlockSpec((tm, d), lambda i: (i, 0)),
        out_shape=jax.ShapeDtypeStruct((t, d), BF16),
        compiler_params=_params("parallel"),
    )(x, g)


def _norm_bwd_epilogue(copy_scale):
    def epilogue(accs, ex):
        dh = accs[0]
        xv, g, dres = ex
        ms = jnp.mean(xv * xv, axis=-1, keepdims=True)
        rstd = lax.rsqrt(ms + RMS_EPS)
        xhat = xv * rstd
        dxhat = dh * g
        dx = rstd * (dxhat - xhat * jnp.mean(dxhat * xhat, axis=-1, keepdims=True))
        out = dres + dx
        dg = jnp.sum(dh * xhat, axis=0, keepdims=True)
        if copy_scale is None:
            return out, dg
        return out, out * copy_scale, dg

    return epilogue


def _merged(w):
    return w.reshape(-1, w.shape[-1])


def _ffn_up(name, h, wg, wu, deps=()):
    def epilogue(accs, ex):
        a, b = accs
        sg = _sigmoid(a)
        act = a * sg
        return act, b * (sg * (1.0 + a * (1.0 - sg))), act * b

    return _mm_cols(name, h, [_merged(wg), _merged(wu)], NT, 3, epilogue, deps=deps)


def _whole_rows(arr, tm):
    return arr, pl.BlockSpec((tm, arr.shape[1]), lambda i: (i, 0))


def _ffn_down(name, z, wd, x, g_next, deps=()):
    t = z.shape[0]
    d = wd.shape[2]
    tm = _row_tile(t)
    row = pl.BlockSpec((tm, d), lambda i: (i, 0))

    def epilogue(accs, ex):
        y = ex[0] + 0.5 * accs[0]
        return y, _rmsnorm(y, ex[1])

    return _mm_rows(
        name, [_whole_rows(z, tm)], [_merged(wd)], NN, t,
        outs=[(jax.ShapeDtypeStruct((t, d), F32), row), (jax.ShapeDtypeStruct((t, d), BF16), row)],
        epilogue=epilogue,
        extras=[(x, row), (g_next, pl.BlockSpec((1, d), lambda i: (0, 0)))],
        deps=deps,
    )


def _ffn_down_loss(name, z, wd, x, target):
    t = z.shape[0]
    d = wd.shape[2]
    tm = _row_tile(t)
    nt = t // tm
    row = pl.BlockSpec((tm, d), lambda i: (i, 0))

    def epilogue(accs, ex):
        e = ex[0] + 0.5 * accs[0] - ex[1]
        dy = e * (1.0 / d)
        return dy, 0.5 * dy, jnp.sum(e * e, axis=0, keepdims=True)

    return _mm_rows(
        name, [_whole_rows(z, tm)], [_merged(wd)], NN, t,
        outs=[(jax.ShapeDtypeStruct((t, d), F32), row), (jax.ShapeDtypeStruct((t, d), BF16), row),
              (jax.ShapeDtypeStruct((nt, 1, d), F32), pl.BlockSpec((None, 1, d), lambda i: (i, 0, 0)))],
        epilogue=epilogue,
        extras=[(x, row), (target, row)],
    )


def _ffn_bwd_act(name, dout, wd, act_a, dact_b, deps=()):
    def epilogue(accs, ex):
        dz = accs[0]
        return dz * ex[1].astype(F32), dz * ex[0].astype(F32)

    return _mm_cols(name, dout, [_merged(wd)], NT, 2, epilogue, extras=[act_a, dact_b], deps=deps)


def _grad_w_cols(name, z, dout, deps=()):
    t, f = z.shape
    d = dout.shape[1]
    tk = _grad_k_tile(t)
    fh = f // 2
    dw = _mm(
        name,
        ins=[(z, pl.BlockSpec((tk, fh), lambda j, n, k: (k, j))),
             (dout, pl.BlockSpec((tk, d), lambda j, n, k: (k, 0)))],
        terms=[(0, 0, 1, TN)],
        n_acc=1,
        grid=(2, 1, t // tk),
        acc_shape=(fh, d),
        outs=[(pltpu.HBM((f, d), BF16), pl.BlockSpec((fh, d), lambda j, n, k: (j, 0)))],
        epilogue=lambda accs, ex: (accs[0],),
        deps=deps,
    )[0]
    return dw.reshape(N_CHIPS, f // N_CHIPS, d)


def _norm_bwd_outs(t, d, tm, copy_scale):
    row = pl.BlockSpec((tm, d), lambda i: (i, 0))
    outs = [(jax.ShapeDtypeStruct((t, d), F32), row)]
    if copy_scale is not None:
        outs.append((jax.ShapeDtypeStruct((t, d), BF16), row))
    outs.append((jax.ShapeDtypeStruct((t // tm, 1, d), F32), pl.BlockSpec((None, 1, d), lambda i: (i, 0, 0))))
    return row, outs


def _ffn_bwd_in(name, da, db, wg, wu, x, g, dres, copy_scale, deps=()):
    t = da.shape[0]
    d = wg.shape[2]
    tm = _row_tile(t)
    row, outs = _norm_bwd_outs(t, d, tm, copy_scale)
    return _mm_rows(
        name, [_whole_rows(da, tm), _whole_rows(db, tm)], [_merged(wg), _merged(wu)], NN, t,
        outs=outs,
        epilogue=_norm_bwd_epilogue(copy_scale),
        extras=[(x, row), (g, pl.BlockSpec((1, d), lambda i: (0, 0))), (dres, row)],
        deps=deps,
    )


def _in_proj(name, h, w_in):
    return _mm_cols(name, h, [w_in], NN, 1, lambda accs, ex: (accs[0],), out_dtype=F32)[0]


def _in_proj_bwd(name, dp, w_in, x, g, dres, copy_scale, deps=()):
    t = dp.shape[0]
    d = w_in.shape[1]
    tm = _row_tile(t)
    row, outs = _norm_bwd_outs(t, d, tm, copy_scale)
    return _mm_rows(
        name, [_whole_rows(dp, tm)], [w_in], NT, t,
        outs=outs,
        epilogue=_norm_bwd_epilogue(copy_scale),
        extras=[(x, row), (g, pl.BlockSpec((1, d), lambda i: (0, 0))), (dres, row)],
        deps=deps,
    )


def _grad_w_in(name, h, dp, ns):
    t, d = h.shape
    pj = dp.shape[1] // ns
    tk = 1024 if t % 1024 == 0 else t
    return _mm(
        name,
        ins=[(h, pl.BlockSpec((tk, d), lambda j, n, k: (k, 0))),
             (dp, pl.BlockSpec((tk, 2 * pj), lambda j, n, k: (k, j)))],
        terms=[(0, 0, 1, TN)],
        n_acc=1,
        grid=(ns // 2, 1, t // tk),
        acc_shape=(d, 2 * pj),
        outs=[(pltpu.HBM((ns, d, pj), BF16), pl.BlockSpec((2, d, pj), lambda j, n, k: (j, 0, 0)))],
        epilogue=lambda accs, ex: (jnp.stack([accs[0][:, :pj], accs[0][:, pj:]]),),
    )[0]


def _out_proj(name, mix, w_out, x, g_next):
    t, dm = mix.shape
    d = w_out.shape[1]
    tm = _row_tile(t)
    row = pl.BlockSpec((tm, d), lambda i, n, k: (i, 0))
    return _mm(
        name,
        ins=[(mix, pl.BlockSpec((tm, dm), lambda i, n, k: (i, 0))),
             (w_out, pl.BlockSpec((dm, d), lambda i, n, k: (0, 0)))],
        terms=[(0, 0, 1, NN)],
        n_acc=1,
        grid=(t // tm, 1, 1),
        acc_shape=(tm, d),
        outs=[(jax.ShapeDtypeStruct((t, d), F32), row), (jax.ShapeDtypeStruct((t, d), BF16), row)],
        epilogue=lambda accs, ex: (ex[0] + accs[0], _rmsnorm(ex[0] + accs[0], ex[1])),
        extras=[(x, row), (g_next, pl.BlockSpec((1, d), lambda i, n, k: (0, 0)))],
    )


def _out_proj_bwd(name, dx, w_out, deps=()):
    t, d = dx.shape
    dm = w_out.shape[0]
    tm = _row_tile(t)
    return _mm(
        name,
        ins=[(dx, pl.BlockSpec((tm, d), lambda i, n, k: (i, 0))),
             (w_out, pl.BlockSpec((dm, d), lambda i, n, k: (0, 0)))],
        terms=[(0, 0, 1, NT)],
        n_acc=1,
        grid=(t // tm, 1, 1),
        acc_shape=(tm, dm),
        outs=[(jax.ShapeDtypeStruct((t, dm), F32), pl.BlockSpec((tm, dm), lambda i, n, k: (i, 0)))],
        epilogue=lambda accs, ex: (accs[0],),
        deps=deps,
    )[0]


def _grad_w_out(name, mix, dx):
    t, dm = mix.shape
    d = dx.shape[1]
    tk = _k_tile(t)
    return _mm(
        name,
        ins=[(mix, pl.BlockSpec((tk, dm), lambda a, n, k: (k, 0))),
             (dx, pl.BlockSpec((tk, d), lambda a, n, k: (k, 0)))],
        terms=[(0, 0, 1, TN)],
        n_acc=1,
        grid=(1, 1, t // tk),
        acc_shape=(dm, d),
        outs=[(pltpu.HBM((dm, d), BF16), pl.BlockSpec((dm, d), lambda a, n, k: (0, 0)))],
        epilogue=lambda accs, ex: (accs[0],),
    )[0]


def _head_group_matrix():
    r = lax.broadcasted_iota(jnp.int32, (MXU_WIDTH, MXU_WIDTH), 0)
    c = lax.broadcasted_iota(jnp.int32, (MXU_WIDTH, MXU_WIDTH), 1)
    same = jnp.right_shift(r, 6) == jnp.right_shift(c, 6)
    return jnp.where(same, 1.0, 0.0).astype(BF16)


def _head_sums(x, bd):
    return jnp.concatenate(
        [_dot_exact_rhs(x[:, c:c + MXU_WIDTH], bd, pieces=2) for c in range(0, x.shape[1], MXU_WIDTH)], axis=1)


def _qk_prep(name, proj, gq, gk):
    b, s, _ = proj.shape
    tm = KPAD
    nb = s // tm

    def body(q_ref, k_ref, v_ref, gq_ref, gk_ref, qn_ref, kn_ref, vb_ref):
        j = pl.program_id(1)
        bd = _head_group_matrix()

        def norm(xv, g):
            ms = _head_sums(xv * xv, bd) * (1.0 / ATTN_DH)
            return xv * lax.rsqrt(ms + RMS_EPS) * g

        @pl.when(j == 0)
        def _():
            kn_ref[...] = jnp.zeros_like(kn_ref)
            vb_ref[...] = jnp.zeros_like(vb_ref)

        @pl.when(j > 0)
        def _():
            qn_ref[...] = norm(q_ref[...], gq_ref[...]).astype(BF16)
            kn_ref[...] = norm(k_ref[...], gk_ref[...]).astype(BF16)
            vb_ref[...] = v_ref[...].astype(BF16)

    src_blk = lambda col: pl.BlockSpec((None, tm, ATTN_W), lambda bi, j: (bi, jnp.maximum(j - 1, 0), col))
    gspec = pl.BlockSpec((1, ATTN_W), lambda bi, j: (0, 0))
    padded = pl.BlockSpec((None, tm, ATTN_W), lambda bi, j: (bi, j, 0))
    return pl.pallas_call(
        body,
        name=name,
        grid=(b, nb + 1),
        in_specs=[src_blk(0), src_blk(1), src_blk(2), gspec, gspec],
        out_specs=[src_blk(0), padded, padded],
        out_shape=[jax.ShapeDtypeStruct((b, s, ATTN_W), BF16), jax.ShapeDtypeStruct((b, KPAD + s, ATTN_W), BF16),
                   jax.ShapeDtypeStruct((b, KPAD + s, ATTN_W), BF16)],
        compiler_params=_params("parallel", "arbitrary"),
    )(proj, proj, proj, gq, gk)


def _qk_prep_bwd(name, proj, dqn, dkn, dv, gq, gk):
    b, s, _ = proj.shape
    tm = KPAD
    nb = s // tm

    def body(q_ref, k_ref, dqn_ref, dkn_ref, dv_ref, gq_ref, gk_ref, dq_ref, dk_ref, dvb_ref, dgq_ref, dgk_ref):
        bd = _head_group_matrix()

        def bwd(xv, dy, g):
            ms = _head_sums(xv * xv, bd) * (1.0 / ATTN_DH)
            rstd = lax.rsqrt(ms + RMS_EPS)
            xhat = xv * rstd
            dxhat = dy * g
            gm = _head_sums(dxhat * xhat, bd) * (1.0 / ATTN_DH)
            return rstd * (dxhat - xhat * gm), jnp.sum(dy * xhat, axis=0, keepdims=True)

        dq, dgq = bwd(q_ref[...], dqn_ref[...], gq_ref[...])
        dk, dgk = bwd(k_ref[...], dkn_ref[...], gk_ref[...])
        dq_ref[...] = dq.astype(BF16)
        dk_ref[...] = dk.astype(BF16)
        dvb_ref[...] = dv_ref[...].astype(BF16)
        dgq_ref[...] = dgq
        dgk_ref[...] = dgk

    col = lambda c: pl.BlockSpec((None, tm, ATTN_W), lambda bi, j: (bi, j, c))
    past_pad = pl.BlockSpec((None, tm, ATTN_W), lambda bi, j: (bi, j + 1, 0))
    gspec = pl.BlockSpec((1, ATTN_W), lambda bi, j: (0, 0))
    pspec = pl.BlockSpec((None, 1, ATTN_W), lambda bi, j: (bi * nb + j, 0, 0))
    o_shape = jax.ShapeDtypeStruct((b, s, ATTN_W), BF16)
    p_shape = jax.ShapeDtypeStruct((b * nb, 1, ATTN_W), F32)
    return pl.pallas_call(
        body,
        name=name,
        grid=(b, nb),
        in_specs=[col(0), col(1), col(0), past_pad, past_pad, gspec, gspec],
        out_specs=[col(0)] * 3 + [pspec] * 2,
        out_shape=[o_shape] * 3 + [p_shape] * 2,
        compiler_params=_params("parallel", "parallel"),
    )(proj, proj, dqn, dkn, dv, gq, gk)


Q_CHUNKS = 4
QBLK = Q_CHUNKS * CHUNK
WIN = (LEFT_CHUNKS + Q_CHUNKS) * CHUNK
DB_W = BAND + CHUNK
MASKED = -1e30
FWD_BLOCKS = 8
BWD_BLOCKS = 2


def _band_table(bias):
    rows = [jnp.pad(bias, ((0, 0), (0, 0), (CHUNK * i, WIN - BAND - CHUNK * i)), constant_values=MASKED)
            for i in range(Q_CHUNKS)]
    return jnp.concatenate(rows, axis=1)


def _head_lanes(hh):
    lane = lax.broadcasted_iota(jnp.int32, (1, LANES), 1)
    return (lane < ATTN_DH) if hh == 0 else (lane >= ATTN_DH)


def _attn_probs(qh, kw, table, start):
    s = _dot(qh, kw, NT) * (ATTN_DH ** -0.5) + table
    col = lax.broadcasted_iota(jnp.int32, (QBLK, WIN), 1)
    s = jnp.where(col + start >= KPAD, s, MASKED)
    m = jnp.max(s, axis=-1, keepdims=True)
    p = jnp.exp(s - m)
    return p * (1.0 / jnp.sum(p, axis=-1, keepdims=True))


def _attn_fwd(name, q, k, v, table, deps=()):
    b, s, w = q.shape
    sp = k.shape[1]

    def body(q_ref, k_ref, v_ref, t_ref, *rest):
        o_ref = rest[-1]
        lanes = [_head_lanes(hh) for hh in range(2)]
        starts = [pl.multiple_of((pl.program_id(2) * FWD_BLOCKS + j) * QBLK, QBLK) for j in range(FWD_BLOCKS)]
        kws = [k_ref[pl.ds(st, WIN), :] for st in starts]
        vws = [v_ref[pl.ds(st, WIN), :] for st in starts]
        q2s = [q_ref[j * QBLK:(j + 1) * QBLK, :] for j in range(FWD_BLOCKS)]
        probs = [[_attn_probs(jnp.where(mine, q2s[j], jnp.zeros_like(q2s[j])), kws[j], t_ref[hh], starts[j]).astype(BF16)
                  for hh, mine in enumerate(lanes)] for j in range(FWD_BLOCKS)]
        for j in range(FWD_BLOCKS):
            outs = [_dot(p, vws[j]) for p in probs[j]]
            o_ref[j * QBLK:(j + 1) * QBLK, :] = jnp.where(lanes[0], outs[0], outs[1]).astype(BF16)

    qspec = pl.BlockSpec((None, FWD_BLOCKS * QBLK, LANES), lambda p, bi, i: (bi, i, p))
    kspec = pl.BlockSpec((None, sp, LANES), lambda p, bi, i: (bi, 0, p))
    return pl.pallas_call(
        body,
        name=name,
        grid=(w // LANES, b, s // (FWD_BLOCKS * QBLK)),
        in_specs=[qspec, kspec, kspec, pl.BlockSpec((2, QBLK, WIN), lambda p, bi, i: (p, 0, 0))] + [ANY] * len(deps),
        out_specs=qspec,
        out_shape=jax.ShapeDtypeStruct((b, s, w), BF16),
        compiler_params=_params("parallel", "parallel", "arbitrary"),
    )(q, k, v, table, *deps)


def _attn_bwd(name, q, k, v, table, dmix):
    b, s, w = q.shape
    sp = k.shape[1]

    def body(q_ref, k_ref, v_ref, t_ref, do_ref, dq_ref, dk_ref, dv_ref, dbe_ref, dbo_ref):
        bi = pl.program_id(1)
        i = pl.program_id(2)

        @pl.when(i == 0)
        def _():
            dk_ref[...] = jnp.zeros_like(dk_ref)
            dv_ref[...] = jnp.zeros_like(dv_ref)

        @pl.when(jnp.logical_and(i == 0, bi == 0))
        def _():
            dbe_ref[...] = jnp.zeros_like(dbe_ref)
            dbo_ref[...] = jnp.zeros_like(dbo_ref)

        lanes = [_head_lanes(hh) for hh in range(2)]

        def scores(j):
            start = pl.multiple_of((i * BWD_BLOCKS + j) * QBLK, QBLK)
            win = pl.ds(start, WIN)
            kw = k_ref[win, :]
            vw = v_ref[win, :]
            q2 = q_ref[j * QBLK:(j + 1) * QBLK, :]
            do2 = do_ref[j * QBLK:(j + 1) * QBLK, :].astype(BF16)
            qh = [jnp.where(mine, q2, jnp.zeros_like(q2)) for mine in lanes]
            doh = [jnp.where(mine, do2, jnp.zeros_like(do2)) for mine in lanes]
            p = [_attn_probs(qh[hh], kw, t_ref[hh], start) for hh in range(2)]
            dp = [_dot(doh[hh], vw, NT) for hh in range(2)]
            return win, kw, qh, doh, p, dp

        def gradients(j, win, kw, qh, doh, p, dp):
            ds = [p[hh] * (dp[hh] - jnp.sum(p[hh] * dp[hh], axis=-1, keepdims=True)) for hh in range(2)]
            dsb = [(x * (ATTN_DH ** -0.5)).astype(BF16) for x in ds]
            pb = [x.astype(BF16) for x in p]
            dq = [_dot(dsb[hh], kw) for hh in range(2)]
            dk = [_dot(dsb[hh], qh[hh], TN) for hh in range(2)]
            dv = [_dot(pb[hh], doh[hh], TN) for hh in range(2)]
            for hh in range(2):
                for qi in range(Q_CHUNKS):
                    c0 = (qi // 2) * LANES
                    blk = ds[hh][qi * CHUNK:(qi + 1) * CHUNK, c0:c0 + DB_W]
                    if qi % 2 == 0:
                        dbe_ref[hh] += blk
                    else:
                        dbo_ref[hh] += blk
            dq_ref[j * QBLK:(j + 1) * QBLK, :] = jnp.where(lanes[0], dq[0], dq[1])
            dk_ref[win, :] += dk[0] + dk[1]
            dv_ref[win, :] += dv[0] + dv[1]

        staged = scores(0)
        for j in range(BWD_BLOCKS):
            upcoming = scores(j + 1) if j + 1 < BWD_BLOCKS else None
            gradients(j, *staged)
            staged = upcoming

    qspec = pl.BlockSpec((None, BWD_BLOCKS * QBLK, LANES), lambda p, bi, i: (bi, i, p))
    kspec = pl.BlockSpec((None, sp, LANES), lambda p, bi, i: (bi, 0, p))
    dbspec = pl.BlockSpec((2, CHUNK, DB_W), lambda p, bi, i: (p, 0, 0))
    db_shape = jax.ShapeDtypeStruct((ATTN_HEADS, CHUNK, DB_W), F32)
    return pl.pallas_call(
        body,
        name=name,
        grid=(w // LANES, b, s // (BWD_BLOCKS * QBLK)),
        in_specs=[qspec, kspec, kspec, pl.BlockSpec((2, QBLK, WIN), lambda p, bi, i: (p, 0, 0)), qspec],
        out_specs=[qspec, kspec, kspec, dbspec, dbspec],
        out_shape=[jax.ShapeDtypeStruct((b, s, w), F32), jax.ShapeDtypeStruct((b, sp, w), F32),
                   jax.ShapeDtypeStruct((b, sp, w), F32), db_shape, db_shape],
        compiler_params=_params("arbitrary", "arbitrary", "arbitrary"),
    )(q, k, v, table, dmix)


HQ_COL = 3 * ATTN_W // HGRN_DH
HF_COL = HQ_COL + HGRN_HEADS
HI_COL = HF_COL + HGRN_HEADS
HG_COL = HI_COL + HGRN_HEADS
HGRN_ROWS = 8 * CHUNK
HGRN_UNROLL = 8
HEAD_LANES = [slice(hh * HGRN_DH, (hh + 1) * HGRN_DH) for hh in range(HGRN_HEADS)]


def _tri(lower):
    r = lax.broadcasted_iota(jnp.int32, (CHUNK, CHUNK), 0)
    c = lax.broadcasted_iota(jnp.int32, (CHUNK, CHUNK), 1)
    return (r >= c) if lower else (r <= c)


def _hgrn_chunk(hq, hf, lb, tril):
    sig = _sigmoid(hf)
    f = lb + (1.0 - lb) * sig
    g = jnp.log(f)
    ones_l = jnp.where(tril, 1.0, 0.0).astype(BF16)
    b = _dot_exact_lhs(ones_l, g)
    bl = jnp.sum(g, axis=0, keepdims=True)
    rows = lax.broadcasted_iota(jnp.int32, g.shape, 0)
    bm = jnp.sum(jnp.where(rows <= CHUNK // 2, g, 0.0), axis=0, keepdims=True)
    sq = _sigmoid(hq)
    q = hq * sq
    k = 1.0 - f
    return sig, f, b, bl, bm, sq, q, k


def _hgrn_fwd(name, proj, attn, lb, go, b, s):
    nc = s // CHUNK
    t = b * s
    nblk = s // HGRN_ROWS
    cpb = HGRN_ROWS // CHUNK

    def body(hq_ref, hf_ref, hi_ref, hg_ref, attn_ref, lb_ref, go_ref, mix_ref, oraw_ref, st_ref, s_scr):
        tril = _tri(True)
        gov = go_ref[...]
        mix_ref[:, 0:ATTN_W] = attn_ref[...]

        @pl.when(pl.program_id(1) == 0)
        def _():
            s_scr[...] = jnp.zeros_like(s_scr)

        def step(c, carry):
            sl = pl.ds(pl.multiple_of(c * CHUNK, CHUNK), CHUNK)
            hg = hg_ref[sl, :]
            _, _, bb, bl, bm, _, q, k = _hgrn_chunk(hq_ref[sl, :], hf_ref[sl, :], lb_ref[...], tril)
            vb = hi_ref[sl, :].astype(BF16)
            qe = (q * jnp.exp(bb - bm)).astype(BF16)
            ke = (k * jnp.exp(bm - bb)).astype(BF16)
            qb = (q * jnp.exp(bb)).astype(BF16)
            kb = (k * jnp.exp(bl - bb)).astype(BF16)
            e_last = jnp.exp(bl)
            gate = _silu(hg)
            st = [s_scr[hh] for hh in range(HGRN_HEADS)]
            a = [jnp.where(tril, _dot(qe[:, hs], ke[:, hs], NT), 0.0).astype(BF16) for hs in HEAD_LANES]
            o_state = [_dot(qb[:, hs], st[hh].astype(BF16), NT) for hh, hs in enumerate(HEAD_LANES)]
            st_next = [st[hh] * e_last[:, hs] + _dot(vb[:, hs], kb[:, hs], TN) for hh, hs in enumerate(HEAD_LANES)]
            o = [_dot(a[hh], vb[:, hs]) + o_state[hh] for hh, hs in enumerate(HEAD_LANES)]
            ro = [(oh * lax.rsqrt(jnp.mean(oh * oh, axis=-1, keepdims=True) + RMS_EPS) * gov) * gate[:, hs]
                  for oh, hs in zip(o, HEAD_LANES)]
            for hh in range(HGRN_HEADS):
                st_ref[hh, c] = st[hh]
                s_scr[hh] = st_next[hh]
            mix_ref[sl, ATTN_W:ATTN_W + HGRN_W] = jnp.concatenate(ro, axis=1).astype(BF16)
            oraw_ref[sl, :] = jnp.concatenate(o, axis=1)
            return carry

        lax.fori_loop(0, cpb, step, 0, unroll=HGRN_UNROLL)

    col = lambda base: pl.BlockSpec((HGRN_ROWS, HGRN_W), lambda bi, i: (bi * nblk + i, base // HGRN_HEADS))
    out = pl.BlockSpec((HGRN_ROWS, HGRN_W), lambda bi, i: (bi * nblk + i, 0))
    return pl.pallas_call(
        body,
        name=name,
        grid=(b, nblk),
        in_specs=[col(HQ_COL), col(HF_COL), col(HI_COL), col(HG_COL), out,
                  pl.BlockSpec((1, HGRN_W), lambda bi, i: (0, 0)), pl.BlockSpec((1, HGRN_DH), lambda bi, i: (0, 0))],
        out_specs=[pl.BlockSpec((HGRN_ROWS, ATTN_W + HGRN_W), lambda bi, i: (bi * nblk + i, 0)), out,
                   pl.BlockSpec((None, HGRN_HEADS, cpb, HGRN_DH, HGRN_DH), lambda bi, i: (bi, 0, i, 0, 0))],
        out_shape=[jax.ShapeDtypeStruct((t, ATTN_W + HGRN_W), BF16), jax.ShapeDtypeStruct((t, HGRN_W), F32),
                   jax.ShapeDtypeStruct((b, HGRN_HEADS, nc, HGRN_DH, HGRN_DH), F32)],
        scratch_shapes=[pltpu.VMEM((HGRN_HEADS, HGRN_DH, HGRN_DH), F32)],
        compiler_params=_params("parallel", "arbitrary"),
    )(proj, proj, proj, proj, attn, lb, go)


def _hgrn_bwd(name, proj, dqkv, lb, go, oraw, states, dmix, b, s):
    t = b * s
    nblk = s // HGRN_ROWS
    cpb = HGRN_ROWS // CHUNK

    def body(hq_ref, hf_ref, hi_ref, hg_ref, dq_ref, dk_ref, dv_ref, lb_ref, go_ref, oraw_ref, st_ref, dro_ref,
             dp_ref, dlb_ref, dgo_ref, ds_scr, dlb_scr, dgo_scr):
        tril = _tri(True)
        ones_u = jnp.where(_tri(False), 1.0, 0.0).astype(BF16)
        gov = go_ref[...]
        dp_ref[:, 0:ATTN_W] = dq_ref[...]
        dp_ref[:, ATTN_W:2 * ATTN_W] = dk_ref[...]
        dp_ref[:, 2 * ATTN_W:3 * ATTN_W] = dv_ref[...]

        @pl.when(pl.program_id(1) == 0)
        def _():
            ds_scr[...] = jnp.zeros_like(ds_scr)
            dlb_scr[...] = jnp.zeros_like(dlb_scr)
            dgo_scr[...] = jnp.zeros_like(dgo_scr)

        def step(ci, carry):
            c = cpb - 1 - ci
            sl = pl.ds(pl.multiple_of(c * CHUNK, CHUNK), CHUNK)
            hq = hq_ref[sl, :]
            hg = hg_ref[sl, :]
            sig, f, bb, bl, bm, sq, q, k = _hgrn_chunk(hq, hf_ref[sl, :], lb_ref[...], tril)
            vb = hi_ref[sl, :].astype(BF16)
            ebm = jnp.exp(bb - bm)
            embm = jnp.exp(bm - bb)
            eb = jnp.exp(bb)
            ebl = jnp.exp(bl - bb)
            e_last = jnp.exp(bl)
            qe = (q * ebm).astype(BF16)
            ke = (k * embm).astype(BF16)
            qb = (q * eb).astype(BF16)
            kb = (k * ebl).astype(BF16)
            st = [st_ref[hh, c] for hh in range(HGRN_HEADS)]
            dst = [ds_scr[hh] for hh in range(HGRN_HEADS)]
            o = oraw_ref[sl, :]
            dro = dro_ref[sl, :]
            sg = _sigmoid(hg)
            gov4 = jnp.concatenate([gov] * HGRN_HEADS, axis=1)
            rstd = jnp.concatenate(
                [jnp.broadcast_to(lax.rsqrt(jnp.mean(o[:, hs] * o[:, hs], axis=-1, keepdims=True) + RMS_EPS),
                                  (CHUNK, HGRN_DH)) for hs in HEAD_LANES], axis=1)
            ohat = o * rstd
            dn = dro * (hg * sg)
            dhg = dro * (ohat * gov4) * (sg * (1.0 + hg * (1.0 - sg)))
            dgo_inc = jnp.sum(dn * ohat, axis=0, keepdims=True)
            dohat = dn * gov4
            proj_h = dohat * ohat
            pm = jnp.concatenate(
                [jnp.broadcast_to(jnp.mean(proj_h[:, hs], axis=-1, keepdims=True), (CHUNK, HGRN_DH))
                 for hs in HEAD_LANES], axis=1)
            dob = (rstd * (dohat - ohat * pm)).astype(BF16)
            stb = [x.astype(BF16) for x in st]
            dstb = [x.astype(BF16) for x in dst]
            a = [jnp.where(tril, _dot(qe[:, hs], ke[:, hs], NT), 0.0).astype(BF16) for hs in HEAD_LANES]
            dab = [jnp.where(tril, _dot(dob[:, hs], vb[:, hs], NT), 0.0).astype(BF16) for hs in HEAD_LANES]
            dqb = [_dot(dob[:, hs], stb[hh]) for hh, hs in enumerate(HEAD_LANES)]
            dkb = [_dot(vb[:, hs], dstb[hh]) for hh, hs in enumerate(HEAD_LANES)]
            dv_state = [_dot(kb[:, hs], dstb[hh], NT) for hh, hs in enumerate(HEAD_LANES)]
            dst_next = [dst[hh] * e_last[:, hs] + _dot(dob[:, hs], qb[:, hs], TN) for hh, hs in enumerate(HEAD_LANES)]
            dv = [_dot(a[hh], dob[:, hs], TN) + dv_state[hh] for hh, hs in enumerate(HEAD_LANES)]
            dqe = jnp.concatenate([_dot(dab[hh], ke[:, hs]) for hh, hs in enumerate(HEAD_LANES)], axis=1)
            dke = jnp.concatenate([_dot(dab[hh], qe[:, hs], TN) for hh, hs in enumerate(HEAD_LANES)], axis=1)
            dqb = jnp.concatenate(dqb, axis=1)
            dkb = jnp.concatenate(dkb, axis=1)
            state_term = jnp.concatenate(
                [jnp.sum(dst[hh] * st[hh], axis=0, keepdims=True) for hh in range(HGRN_HEADS)], axis=1)
            dq = dqe * ebm + dqb * eb
            dk = dke * embm + dkb * ebl
            db = (qe.astype(F32) * dqe - ke.astype(F32) * dke) + q * (dqb * eb) - k * (dkb * ebl)
            d_last = jnp.sum(k * ebl * dkb, axis=0, keepdims=True) + state_term * e_last
            dg = _dot_exact_lhs(ones_u, db) + d_last
            df = dg / f - dk
            first = HQ_COL * HGRN_DH
            dp_ref[sl, first:first + HGRN_W] = (dq * (sq * (1.0 + hq * (1.0 - sq)))).astype(BF16)
            dp_ref[sl, first + HGRN_W:first + 2 * HGRN_W] = (df * (1.0 - lb_ref[...]) * sig * (1.0 - sig)).astype(BF16)
            dp_ref[sl, first + 2 * HGRN_W:first + 3 * HGRN_W] = jnp.concatenate(dv, axis=1).astype(BF16)
            dp_ref[sl, first + 3 * HGRN_W:first + 4 * HGRN_W] = dhg.astype(BF16)
            dlb_scr[...] += jnp.sum(df * (1.0 - sig), axis=0, keepdims=True)
            dgo_scr[...] += dgo_inc
            for hh in range(HGRN_HEADS):
                ds_scr[hh] = dst_next[hh]
            return carry

        lax.fori_loop(0, cpb, step, 0, unroll=HGRN_UNROLL)

        @pl.when(pl.program_id(1) == nblk - 1)
        def _():
            dlb_ref[...] = dlb_scr[...]
            dgo_ref[...] = dgo_scr[...]

    rows = lambda bi, i: bi * nblk + (nblk - 1 - i)
    col = lambda base: pl.BlockSpec((HGRN_ROWS, HGRN_W), lambda bi, i: (rows(bi, i), base // HGRN_HEADS))
    out = pl.BlockSpec((HGRN_ROWS, HGRN_W), lambda bi, i: (rows(bi, i), 0))
    part = pl.BlockSpec((None, 1, HGRN_W), lambda bi, i: (bi, 0, 0))
    width = HG_COL * HGRN_DH + HGRN_W
    o_shape = jax.ShapeDtypeStruct((t, width), BF16)
    p_shape = jax.ShapeDtypeStruct((b, 1, HGRN_W), F32)
    return pl.pallas_call(
        body,
        name=name,
        grid=(b, nblk),
        in_specs=[col(HQ_COL), col(HF_COL), col(HI_COL), col(HG_COL), out, out, out,
                  pl.BlockSpec((1, HGRN_W), lambda bi, i: (0, 0)), pl.BlockSpec((1, HGRN_DH), lambda bi, i: (0, 0)), out,
                  pl.BlockSpec((None, HGRN_HEADS, cpb, HGRN_DH, HGRN_DH), lambda bi, i: (bi, 0, nblk - 1 - i, 0, 0)),
                  col(ATTN_W // HGRN_DH)],
        out_specs=[pl.BlockSpec((HGRN_ROWS, width), lambda bi, i: (rows(bi, i), 0))] + [part] * 2,
        out_shape=[o_shape] + [p_shape] * 2,
        scratch_shapes=[pltpu.VMEM((HGRN_HEADS, HGRN_DH, HGRN_DH), F32), pltpu.VMEM((1, HGRN_W), F32),
                        pltpu.VMEM((1, HGRN_W), F32)],
        compiler_params=_params("parallel", "arbitrary"),
    )(proj, proj, proj, proj, *dqkv, lb, go, oraw, states, dmix)


def _small_grads(name, dg1, dgm, dg2, dgq, dgk, dbe_t, dbo_t, dlb, dgo, lbp):
    d = dg1.shape[1]

    def body(dg1_ref, dgm_ref, dg2_ref, dgq_ref, dgk_ref, dbe_ref, dbo_ref, dlb_ref, dgo_ref, lbp_ref,
             g1_ref, gm_ref, g2_ref, gq_ref, gk_ref, rb_ref, lbg_ref, go_ref):
        g1_ref[...] = jnp.sum(dg1_ref[...], axis=0, keepdims=True)
        gm_ref[...] = jnp.sum(dgm_ref[...], axis=0, keepdims=True)
        g2_ref[...] = jnp.sum(dg2_ref[...], axis=0, keepdims=True)
        r = lax.broadcasted_iota(jnp.int32, (ATTN_W, ATTN_DH), 0)
        cidx = lax.broadcasted_iota(jnp.int32, (ATTN_W, ATTN_DH), 1)
        fold = jnp.where(jnp.bitwise_and(r, ATTN_DH - 1) == cidx, 1.0, 0.0).astype(BF16)
        gq_ref[...] = jnp.sum(_dot_exact_rhs(dgq_ref[...], fold), axis=0, keepdims=True)
        gk_ref[...] = jnp.sum(_dot_exact_rhs(dgk_ref[...], fold), axis=0, keepdims=True)
        gosum = jnp.sum(dgo_ref[...], axis=0, keepdims=True)
        go_ref[...] = (gosum[:, 0:HGRN_DH] + gosum[:, HGRN_DH:2 * HGRN_DH]
                       + gosum[:, 2 * HGRN_DH:3 * HGRN_DH] + gosum[:, 3 * HGRN_DH:4 * HGRN_DH])
        p0 = lbp_ref[0:1, :]
        p1 = lbp_ref[1:2, :]
        lbv = 1.0 / (1.0 + jnp.exp(p1 - p0))
        dp0 = jnp.sum(dlb_ref[...], axis=0, keepdims=True) * lbv * (1.0 - lbv)
        lbg_ref[0:1, :] = dp0
        lbg_ref[1:2, :] = -dp0
        acc = dbe_ref[CHUNK - 1] + pltpu.roll(dbo_ref[CHUNK - 1], DB_W - CHUNK, 1)
        for tq in range(CHUNK - 1):
            acc = acc + pltpu.roll(dbe_ref[tq], CHUNK - 1 - tq, 1) + pltpu.roll(dbo_ref[tq], DB_W - 1 - tq, 1)
        jidx = lax.broadcasted_iota(jnp.int32, (DB_W, N_REL_PAD), 0)
        ridx = lax.broadcasted_iota(jnp.int32, (DB_W, N_REL_PAD), 1)
        rel = jnp.clip(KPAD + CHUNK - 1 - jidx, -REL_CLIP, REL_CLIP) + REL_CLIP
        rb_ref[...] = _dot_exact_rhs(acc, jnp.where(rel == ridx, 1.0, 0.0).astype(BF16))

    ins = [dg1, dgm, dg2, dgq, dgk, dbe_t, dbo_t, dlb, dgo, lbp]
    outs = [jax.ShapeDtypeStruct((1, d), F32)] * 3 + [jax.ShapeDtypeStruct((1, ATTN_DH), F32)] * 2 + [
        jax.ShapeDtypeStruct((ATTN_HEADS, N_REL_PAD), F32), jax.ShapeDtypeStruct((2, HGRN_W), F32),
        jax.ShapeDtypeStruct((1, HGRN_DH), F32)]
    vm = pl.BlockSpec(memory_space=pltpu.VMEM)
    return pl.pallas_call(
        body,
        name=name,
        in_specs=[vm] * len(ins),
        out_specs=[vm] * len(outs),
        out_shape=outs,
        compiler_params=pltpu.CompilerParams(vmem_limit_bytes=VMEM_LIMIT),
    )(*ins)


def _adam_update(w, g, m, v):
    m2 = ADAM_B1 * m + (1.0 - ADAM_B1) * g
    v2 = ADAM_B2 * v + (1.0 - ADAM_B2) * (g * g)
    m_hat = m2 / (1.0 - ADAM_B1 ** ADAM_STEP)
    v_hat = v2 / (1.0 - ADAM_B2 ** ADAM_STEP)
    delta = -ADAM_LR * (m_hat / (jnp.sqrt(v_hat) + ADAM_EPS) + ADAM_WD * w)
    return delta, m2, v2


def _rows_tile(r):
    return r if r <= 512 or r % 512 else 512


def _pair_sum(name, grad, theirs, core):
    n, half, c = theirs.shape
    tr = _rows_tile(half)
    nth = half // tr

    def body(core_ref, a_ref, b_ref, o_ref):
        o_ref[...] = (a_ref[...].astype(F32) + b_ref[...].astype(F32)).astype(o_ref.dtype)

    spec = pl.BlockSpec((None, tr, c), lambda i, j, core_ref: (i, j, 0))
    return pl.pallas_call(
        body, name=name,
        grid_spec=pltpu.PrefetchScalarGridSpec(
            num_scalar_prefetch=1, grid=(n, nth),
            in_specs=[pl.BlockSpec((None, tr, c), lambda i, j, core_ref: (i, core_ref[0] * nth + j, 0)), spec],
            out_specs=spec),
        out_shape=pltpu.HBM((n, half, c), BF16), compiler_params=_params("parallel", "parallel"),
    )(core, grad, theirs)


def _chip_sum(name, own, parts, chip):
    _, half, c = own.shape
    tr = _rows_tile(half)

    def body(chip_ref, own_ref, p_ref, o_ref):
        me = chip_ref[0]
        mine = own_ref[...].astype(F32)
        flip_x, flip_y, flip_xy = (p_ref[i].astype(F32) for i in range(3))
        acc = None
        for k in range(N_CHIPS):
            rel = jnp.bitwise_xor(me, k)
            term = jnp.where(rel == 0, mine, jnp.where(rel == 2, flip_x, jnp.where(rel == 1, flip_y, flip_xy)))
            acc = term if acc is None else acc + term
        o_ref[...] = acc

    return pl.pallas_call(
        body, name=name,
        grid_spec=pltpu.PrefetchScalarGridSpec(
            num_scalar_prefetch=1, grid=(half // tr,),
            in_specs=[pl.BlockSpec((None, tr, c), lambda j, chip_ref: (chip_ref[0], j, 0)),
                      pl.BlockSpec((3, tr, c), lambda j, chip_ref: (0, j, 0))],
            out_specs=pl.BlockSpec((tr, c), lambda j, chip_ref: (j, 0))),
        out_shape=pltpu.HBM((half, c), F32), compiler_params=_params("parallel"),
    )(chip, own, parts)


def _adamw(name, w, g_mine, g_theirs, m, v, core):
    _, r, c = w.shape
    half = r // 2
    tr = _rows_tile(half)
    nth = half // tr

    def body(core_ref, w_ref, gm_ref, gt_ref, m_ref, v_ref, g_ref, d_ref, m2_ref, v2_ref):
        g = jnp.where(pl.program_id(0) == core_ref[0], gm_ref[...], gt_ref[...])
        delta, m2, v2 = _adam_update(w_ref[...], g, m_ref[...], v_ref[...])
        g_ref[...] = g
        d_ref[...] = delta
        m2_ref[...] = m2
        v2_ref[...] = v2

    full = pl.BlockSpec((None, tr, c), lambda h, j, core_ref: (0, h * nth + j, 0))
    part = pl.BlockSpec((tr, c), lambda h, j, core_ref: (j, 0))
    shape = jax.ShapeDtypeStruct((1, r, c), F32)
    return pl.pallas_call(
        body, name=name,
        grid_spec=pltpu.PrefetchScalarGridSpec(
            num_scalar_prefetch=1, grid=(2, nth), in_specs=[full, part, part, full, full], out_specs=[full] * 4),
        out_shape=[shape] * 4, compiler_params=_params("parallel", "parallel"),
    )(core, w, g_mine, g_theirs, m, v)


def _rel_bias_table(name, rel_bias):
    padded = jnp.pad(rel_bias, ((0, 0), (0, N_REL_PAD - N_REL)))

    def body(rb_ref, o_ref):
        ridx = lax.broadcasted_iota(jnp.int32, (N_REL_PAD, BAND), 0)
        sidx = lax.broadcasted_iota(jnp.int32, (N_REL_PAD, BAND), 1)
        rb = rb_ref[...]

        def step(tq, carry):
            rel = jnp.clip(tq + KPAD - sidx, -REL_CLIP, REL_CLIP) + REL_CLIP
            onehot = jnp.where(rel == ridx, 1.0, 0.0).astype(BF16)
            o_ref[tq] = _dot_exact_rhs(rb, onehot)
            return carry

        lax.fori_loop(0, CHUNK, step, 0)

    vm = pl.BlockSpec(memory_space=pltpu.VMEM)
    table = pl.pallas_call(
        body, name=name, in_specs=[vm], out_specs=vm,
        out_shape=jax.ShapeDtypeStruct((CHUNK, ATTN_HEADS, BAND), F32),
    )(padded)
    return table.transpose(1, 0, 2)


def _adamw_small(name, w, parts, m, v):
    def body(w_ref, p_ref, m_ref, v_ref, g_ref, d_ref, m2_ref, v2_ref):
        g = p_ref[0]
        for i in range(1, N_DEV):
            g = g + p_ref[i]
        delta, m2, v2 = _adam_update(w_ref[...], g, m_ref[...], v_ref[...])
        g_ref[...] = g
        d_ref[...] = delta
        m2_ref[...] = m2
        v2_ref[...] = v2

    vm = pl.BlockSpec(memory_space=pltpu.VMEM)
    shape = jax.ShapeDtypeStruct((SMALL_ROWS, SMALL_COLS), F32)
    return pl.pallas_call(
        body, name=name, in_specs=[vm] * 4, out_specs=[vm] * 4, out_shape=[shape] * 4,
    )(w, parts, m, v)


def _position():
    return lax.axis_index("x"), lax.axis_index("y"), lax.axis_index("c")


def _other_chips(x, y):
    return [(1 - x, y), (x, 1 - y), (1 - x, 1 - y)]


ANY = pl.BlockSpec(memory_space=pl.ANY)
PAIR_ID = 0


def _pair_handshake():
    x, y, c = _position()
    barrier = pltpu.get_barrier_semaphore()
    pl.semaphore_signal(barrier, inc=1, device_id=(x, y, 1 - c), device_id_type=MESH)
    pl.semaphore_wait(barrier, 1)


PAIR_CALL = pltpu.CompilerParams(collective_id=PAIR_ID)


HBM = pl.BlockSpec(memory_space=pltpu.HBM)
SEM = pl.BlockSpec(memory_space=pltpu.SEMAPHORE)
SPLIT_COPY = pltpu.SideEffectType.DATAFLOW_SIDE_EFFECTING


def _gather_copy(shards, outs, send_sem, recv_sem, i, j):
    x, y, c = _position()
    chips = _other_chips(x, y)
    half = shards[i].shape[0] // 2
    rows = pl.ds(pl.multiple_of(c * half, 16), half)
    return pltpu.make_async_remote_copy(
        src_ref=shards[i].at[rows, :], dst_ref=outs[i].at[2 * x + y, rows, :],
        send_sem=send_sem.at[3 * i + j], recv_sem=recv_sem.at[3 * i + j],
        device_id=(chips[j][0], chips[j][1], c), device_id_type=MESH)


def _gather_start(name, shards, after):
    n = len(shards)

    def body(*refs):
        srcs, outs = refs[:n], refs[n:2 * n]
        send_sem, recv_sem = refs[2 * n + len(after)], refs[2 * n + len(after) + 1]
        token = refs[-1]
        for i in range(n):
            for j in range(3):
                _gather_copy(srcs, outs, send_sem, recv_sem, i, j).start()
        token[...] = jnp.zeros_like(token)

    full = [(N_CHIPS,) + s.shape for s in shards]
    res = pl.pallas_call(
        body,
        name=name,
        in_specs=[HBM] * (2 * n) + [ANY] * len(after),
        out_specs=[SEM, SEM] + [HBM] * (2 * n) + [pl.BlockSpec(memory_space=pltpu.VMEM)],
        out_shape=[pltpu.SemaphoreType.DMA((3 * n,)), pltpu.SemaphoreType.DMA((3 * n,))]
        + [pltpu.HBM(s.shape, s.dtype) for s in shards]
        + [pltpu.HBM(shp, s.dtype) for shp, s in zip(full, shards)]
        + [jax.ShapeDtypeStruct((8, LANES), F32)],
        input_output_aliases={i: 2 + i for i in range(2 * n)},
        compiler_params=pltpu.CompilerParams(has_side_effects=SPLIT_COPY),
    )(*[pltpu.with_memory_space_constraint(s, pltpu.HBM) for s in shards],
      *[pltpu.with_memory_space_constraint(lax.empty(shp, s.dtype), pltpu.HBM) for shp, s in zip(full, shards)],
      *after)
    return res[0], res[1], list(res[2:2 + n]), list(res[2 + n:2 + 2 * n]), res[-1]


def _gather_wait(name, send_sem, recv_sem, shards, outs, after):
    n = len(shards)

    def body(*refs):
        srcs, out_refs = refs[:n], refs[n:2 * n]
        send_ref, recv_ref = refs[2 * n], refs[2 * n + 1]
        for i in range(n):
            for j in range(3):
                copy = _gather_copy(srcs, out_refs, send_ref, recv_ref, i, j)
                copy.wait_send()
                copy.wait_recv()

    res = pl.pallas_call(
        body,
        name=name,
        in_specs=[HBM] * (2 * n) + [SEM, SEM] + [ANY] * len(after),
        out_specs=[HBM] * (2 * n),
        out_shape=[pltpu.HBM(s.shape, s.dtype) for s in shards] + [pltpu.HBM(o.shape, o.dtype) for o in outs],
        input_output_aliases={i: i for i in range(2 * n)},
        compiler_params=pltpu.CompilerParams(has_side_effects=SPLIT_COPY),
    )(*shards, *outs, send_sem, recv_sem, *after)
    return list(res[:n]), list(res[n:])


def _join_copies(srcs, ins, outs, own_send, own_recv, half_send, half_recv):
    x, y, c = _position()
    chips = _other_chips(x, y)
    copies = []
    for i in range(len(srcs)):
        copies.append(pltpu.make_async_remote_copy(
            src_ref=srcs[i], dst_ref=outs[i].at[2 * x + y], send_sem=own_send.at[i], recv_sem=own_recv.at[i],
            device_id=(x, y, 1 - c), device_id_type=MESH))
        half = srcs[i].shape[0] // 2
        rows = pl.ds(pl.multiple_of(c * half, 16), half)
        for j in range(3):
            slot = 2 * chips[j][0] + chips[j][1]
            copies.append(pltpu.make_async_remote_copy(
                src_ref=ins[i].at[slot, rows, :], dst_ref=outs[i].at[slot, rows, :],
                send_sem=half_send.at[3 * i + j], recv_sem=half_recv.at[3 * i + j],
                device_id=(x, y, 1 - c), device_id_type=MESH))
    return copies


def _gather_join(name, shards, outs):
    n = len(shards)

    def body(*refs):
        _pair_handshake()
        copies = _join_copies(refs[:n], refs[n:2 * n], refs[2 * n:3 * n], *refs[3 * n:])
        for cp in copies:
            cp.start()
        for cp in copies:
            cp.wait()

    return pl.pallas_call(
        body,
        name=name,
        in_specs=[ANY] * (2 * n),
        out_specs=[HBM] * n,
        out_shape=[pltpu.HBM(o.shape, o.dtype) for o in outs],
        input_output_aliases={n + i: i for i in range(n)},
        scratch_shapes=[pltpu.SemaphoreType.DMA((n,))] * 2 + [pltpu.SemaphoreType.DMA((3 * n,))] * 2,
        compiler_params=PAIR_CALL,
    )(*shards, *outs)


def _join_start(name, shards, outs):
    n = len(shards)

    def body(*refs):
        _pair_handshake()
        srcs, arrs = refs[:n], refs[n:2 * n]
        sems = refs[2 * n:2 * n + 4]
        token = refs[-1]
        for cp in _join_copies(srcs, arrs, arrs, *sems):
            cp.start()
        token[...] = jnp.zeros_like(token)

    res = pl.pallas_call(
        body,
        name=name,
        in_specs=[HBM] * (2 * n),
        out_specs=[SEM] * 4 + [HBM] * (2 * n) + [pl.BlockSpec(memory_space=pltpu.VMEM)],
        out_shape=[pltpu.SemaphoreType.DMA((n,))] * 2 + [pltpu.SemaphoreType.DMA((3 * n,))] * 2
        + [pltpu.HBM(s.shape, s.dtype) for s in shards] + [pltpu.HBM(o.shape, o.dtype) for o in outs]
        + [jax.ShapeDtypeStruct((8, LANES), F32)],
        input_output_aliases={i: 4 + i for i in range(2 * n)},
        compiler_params=pltpu.CompilerParams(has_side_effects=SPLIT_COPY, collective_id=PAIR_ID),
    )(*shards, *outs)
    return list(res[:4]), list(res[4:4 + n]), list(res[4 + n:4 + 2 * n]), res[-1]


def _join_wait(name, sems, shards, outs, after):
    n = len(shards)

    def body(*refs):
        srcs, arrs = refs[:n], refs[n:2 * n]
        for cp in _join_copies(srcs, arrs, arrs, *refs[2 * n:2 * n + 4]):
            cp.wait_send()
            cp.wait_recv()

    res = pl.pallas_call(
        body,
        name=name,
        in_specs=[HBM] * (2 * n) + [SEM] * 4 + [ANY] * len(after),
        out_specs=[HBM] * (2 * n),
        out_shape=[pltpu.HBM(s.shape, s.dtype) for s in shards] + [pltpu.HBM(o.shape, o.dtype) for o in outs],
        input_output_aliases={i: i for i in range(2 * n)},
        compiler_params=pltpu.CompilerParams(has_side_effects=SPLIT_COPY),
    )(*shards, *outs, *sems, *after)
    return list(res[n:])


def _pair_copy(grads, lands, send_sem, recv_sem, i):
    x, y, c = _position()
    half = grads[i].shape[1] // 2
    give = pl.ds(pl.multiple_of((1 - c) * half, 16), half)
    return pltpu.make_async_remote_copy(
        src_ref=grads[i].at[:, give, :], dst_ref=lands[i], send_sem=send_sem.at[i], recv_sem=recv_sem.at[i],
        device_id=(x, y, 1 - c), device_id_type=MESH)


def _pair_start(name, grads):
    n = len(grads)

    def body(*refs):
        _pair_handshake()
        srcs, lands = refs[:n], refs[n:2 * n]
        send_sem, recv_sem = refs[2 * n], refs[2 * n + 1]
        token = refs[-1]
        for i in range(n):
            _pair_copy(srcs, lands, send_sem, recv_sem, i).start()
        token[...] = jnp.zeros_like(token)

    halves = [(g.shape[0], g.shape[1] // 2, g.shape[2]) for g in grads]
    res = pl.pallas_call(
        body,
        name=name,
        in_specs=[HBM] * (2 * n),
        out_specs=[SEM, SEM] + [HBM] * (2 * n) + [pl.BlockSpec(memory_space=pltpu.VMEM)],
        out_shape=[pltpu.SemaphoreType.DMA((n,)), pltpu.SemaphoreType.DMA((n,))]
        + [pltpu.HBM(g.shape, g.dtype) for g in grads]
        + [pltpu.HBM(shp, g.dtype) for shp, g in zip(halves, grads)]
        + [jax.ShapeDtypeStruct((8, LANES), F32)],
        input_output_aliases={i: 2 + i for i in range(2 * n)},
        compiler_params=pltpu.CompilerParams(has_side_effects=SPLIT_COPY, collective_id=PAIR_ID),
    )(*[pltpu.with_memory_space_constraint(g, pltpu.HBM) for g in grads],
      *[pltpu.with_memory_space_constraint(lax.empty(shp, g.dtype), pltpu.HBM) for shp, g in zip(halves, grads)])
    return res[0], res[1], list(res[2:2 + n]), list(res[2 + n:2 + 2 * n]), res[-1]


def _pair_wait(name, send_sem, recv_sem, grads, lands, after):
    n = len(grads)

    def body(*refs):
        srcs, land_refs = refs[:n], refs[n:2 * n]
        send_ref, recv_ref = refs[2 * n], refs[2 * n + 1]
        for i in range(n):
            copy = _pair_copy(srcs, land_refs, send_ref, recv_ref, i)
            copy.wait_send()
            copy.wait_recv()

    res = pl.pallas_call(
        body,
        name=name,
        in_specs=[HBM] * (2 * n) + [SEM, SEM, ANY],
        out_specs=[HBM] * (2 * n),
        out_shape=[pltpu.HBM(g.shape, g.dtype) for g in grads] + [pltpu.HBM(l.shape, l.dtype) for l in lands],
        input_output_aliases={i: i for i in range(2 * n)},
        compiler_params=pltpu.CompilerParams(has_side_effects=SPLIT_COPY),
    )(*grads, *lands, send_sem, recv_sem, after)
    return list(res[:n]), list(res[n:])


def _scatter_copy(srcs, lands, send_sem, recv_sem, i, j):
    x, y, c = _position()
    chips = _other_chips(x, y)
    return pltpu.make_async_remote_copy(
        src_ref=srcs[i].at[2 * chips[j][0] + chips[j][1]], dst_ref=lands[i].at[j],
        send_sem=send_sem.at[3 * i + j], recv_sem=recv_sem.at[3 * i + j],
        device_id=(chips[j][0], chips[j][1], c), device_id_type=MESH)


def _scatter_start(name, sums):
    n = len(sums)

    def body(*refs):
        srcs, lands = refs[:n], refs[n:2 * n]
        send_sem, recv_sem = refs[2 * n], refs[2 * n + 1]
        token = refs[-1]
        for i in range(n):
            for j in range(3):
                _scatter_copy(srcs, lands, send_sem, recv_sem, i, j).start()
        token[...] = jnp.zeros_like(token)

    land_shapes = [(3,) + s.shape[1:] for s in sums]
    res = pl.pallas_call(
        body,
        name=name,
        in_specs=[HBM] * (2 * n),
        out_specs=[SEM, SEM] + [HBM] * (2 * n) + [pl.BlockSpec(memory_space=pltpu.VMEM)],
        out_shape=[pltpu.SemaphoreType.DMA((3 * n,)), pltpu.SemaphoreType.DMA((3 * n,))]
        + [pltpu.HBM(s.shape, s.dtype) for s in sums]
        + [pltpu.HBM(shp, s.dtype) for shp, s in zip(land_shapes, sums)]
        + [jax.ShapeDtypeStruct((8, LANES), F32)],
        input_output_aliases={i: 2 + i for i in range(2 * n)},
        compiler_params=pltpu.CompilerParams(has_side_effects=SPLIT_COPY),
    )(*[pltpu.with_memory_space_constraint(s, pltpu.HBM) for s in sums],
      *[pltpu.with_memory_space_constraint(lax.empty(shp, s.dtype), pltpu.HBM) for shp, s in zip(land_shapes, sums)])
    return res[0], res[1], list(res[2:2 + n]), list(res[2 + n:2 + 2 * n]), res[-1]


def _scatter_wait(name, send_sem, recv_sem, sums, lands, after):
    n = len(sums)

    def body(*refs):
        srcs, land_refs = refs[:n], refs[n:2 * n]
        send_ref, recv_ref = refs[2 * n], refs[2 * n + 1]
        for i in range(n):
            for j in range(3):
                copy = _scatter_copy(srcs, land_refs, send_ref, recv_ref, i, j)
                copy.wait_send()
                copy.wait_recv()

    res = pl.pallas_call(
        body,
        name=name,
        in_specs=[HBM] * (2 * n) + [SEM, SEM, ANY],
        out_specs=[HBM] * (2 * n),
        out_shape=[pltpu.HBM(s.shape, s.dtype) for s in sums] + [pltpu.HBM(l.shape, l.dtype) for l in lands],
        input_output_aliases={i: i for i in range(2 * n)},
        compiler_params=pltpu.CompilerParams(has_side_effects=SPLIT_COPY),
    )(*sums, *lands, send_sem, recv_sem, after)
    return list(res[:n]), list(res[n:])


def _pair_join(name, halves, small=None):
    n = len(halves)
    if small is None:
        def body_plain(*refs):
            _pair_handshake()
            ins, outs = refs[:n], refs[n:2 * n]
            send_sem, recv_sem = refs[2 * n:]
            x, y, c = _position()
            swaps = [pltpu.make_async_remote_copy(
                src_ref=ins[i], dst_ref=outs[i], send_sem=send_sem.at[i], recv_sem=recv_sem.at[i],
                device_id=(x, y, 1 - c), device_id_type=MESH) for i in range(n)]
            for swap in swaps:
                swap.start()
            for swap in swaps:
                swap.wait()

        return pl.pallas_call(
            body_plain,
            name=name,
            in_specs=[ANY] * n,
            out_specs=[ANY] * n,
            out_shape=[jax.ShapeDtypeStruct(h.shape, h.dtype) for h in halves],
            scratch_shapes=[pltpu.SemaphoreType.DMA((n,))] * 2,
            compiler_params=PAIR_CALL,
        )(*halves)

    def body(*refs):
        ins, small_ref = refs[:n], refs[n]
        outs, all_ref = refs[n + 1:2 * n + 1], refs[2 * n + 1]
        send_sem, recv_sem, sm_send, sm_recv, sm_local = refs[2 * n + 2:]
        x, y, c = _position()
        swaps = []
        for i in range(n):
            swap = pltpu.make_async_remote_copy(
                src_ref=ins[i], dst_ref=outs[i], send_sem=send_sem.at[i], recv_sem=recv_sem.at[i],
                device_id=(x, y, 1 - c), device_id_type=MESH)
            swap.start()
            swaps.append(swap)
        me = 4 * x + 2 * y + c
        sm_own = pltpu.make_async_copy(small_ref, all_ref.at[me], sm_local)
        sm_own.start()
        pushes, arrivals = [], []
        for mask in range(1, N_DEV):
            px, py, pc = x ^ (mask >> 2), y ^ ((mask >> 1) & 1), c ^ (mask & 1)
            pushes.append(pltpu.make_async_remote_copy(
                src_ref=small_ref, dst_ref=all_ref.at[me], send_sem=sm_send.at[mask - 1], recv_sem=sm_recv.at[mask - 1],
                device_id=(px, py, pc), device_id_type=MESH))
            arrivals.append(pltpu.make_async_remote_copy(
                src_ref=small_ref, dst_ref=all_ref.at[4 * px + 2 * py + pc], send_sem=sm_send.at[mask - 1],
                recv_sem=sm_recv.at[mask - 1], device_id=(px, py, pc), device_id_type=MESH))
        for cp in pushes:
            cp.start()
        for swap in swaps:
            swap.wait()
        for cp in arrivals:
            cp.wait_recv()
        for cp in pushes:
            cp.wait_send()
        sm_own.wait()

    res = pl.pallas_call(
        body,
        name=name,
        in_specs=[ANY] * (n + 1),
        out_specs=[ANY] * (n + 1),
        out_shape=[jax.ShapeDtypeStruct(h.shape, h.dtype) for h in halves]
        + [jax.ShapeDtypeStruct((N_DEV,) + small.shape, small.dtype)],
        scratch_shapes=[pltpu.SemaphoreType.DMA((n,))] * 2 + [pltpu.SemaphoreType.DMA((N_DEV - 1,))] * 2
        + [pltpu.SemaphoreType.DMA(())],
    )(*halves, small)
    return res[:n], res[n]


def _lower_bound(lbp):
    return jax.nn.softmax(lbp, axis=0)[0:1]


def _local_step(x, target, g1, gm, g2, gq, gk, go, rel_bias, lbp, weights, on_grads, grads_sent):
    b, s, d = x.shape
    t = b * s
    x0 = x.reshape(t, d)
    tgt = target.reshape(t, d)
    gq_t = jnp.tile(gq, (1, ATTN_HEADS))
    gk_t = jnp.tile(gk, (1, ATTN_HEADS))
    lb = _lower_bound(lbp)
    table = _band_table(_rel_bias_table("rel_bias_table", rel_bias))

    h1 = _rmsnorm_fwd("norm1", x0, g1)
    wg1, wu1, deps1 = weights["first"]((h1, table))
    a1, b1, z1 = _ffn_up("ffn1_up", h1, wg1, wu1, deps1)
    wd1, deps_mid = weights["mid"]((z1,))
    x1, h2 = _ffn_down("ffn1_down", z1, wd1, x0, gm, deps_mid)
    w_in, w_out = weights["mid_rest"]((x1,))
    ns = w_in.shape[0]
    proj = _in_proj("in_proj", h2, w_in)
    proj3 = proj.reshape(b, s, proj.shape[1])
    qn, kn, vb = _qk_prep("qk_prep", proj3, gq_t, gk_t)
    attn = _attn_fwd("attn_fwd", qn, kn, vb, table, weights["last_begin"]((qn,))).reshape(t, ATTN_W)
    mix, oraw, states = _hgrn_fwd("hgrn_fwd", proj, attn, lb, go, b, s)
    x2, h3 = _out_proj("out_proj", mix, w_out, x1, g2)
    wg2, wu2, wd2 = weights["last"]((h3,))
    a2, b2, z2 = _ffn_up("ffn2_up", h3, wg2, wu2)
    dy, dyh, sq = _ffn_down_loss("ffn2_down_loss", z2, wd2, x2, tgt)
    loss = 0.5 * jnp.sum(sq) / d

    da2, db2 = _ffn_bwd_act("ffn2_bwd_act", dyh, wd2, a2, b2)
    dwd2 = _grad_w_cols("ffn2_dwd", z2, dyh)
    dwg2 = _grad_w_cols("ffn2_dwg", da2, h3)
    dwu2 = _grad_w_cols("ffn2_dwu", db2, h3)
    sent2 = on_grads("ffn2", {"ffn2_w_gate": dwg2, "ffn2_w_up": dwu2, "ffn2_w_down": dwd2})
    dx2, dx2b, dg2 = _ffn_bwd_in("ffn2_bwd_in", da2, db2, wg2, wu2, x2, g2, dy, 1.0, sent2)
    sent2 = grads_sent("ffn2", dx2b)

    dwout = _grad_w_out("dw_out", mix, dx2b)
    dmix = _out_proj_bwd("out_proj_bwd", dx2b, w_out, sent2)
    dqn, dkn, dvn, dbe, dbo = _attn_bwd("attn_bwd", qn, kn, vb, table, dmix.reshape(b, s, dmix.shape[1]))
    dpq, dpk, dpv, dgq, dgk = _qk_prep_bwd("qk_prep_bwd", proj3, dqn, dkn, dvn, gq_t, gk_t)
    dpq, dpk, dpv = (a.reshape(t, ATTN_W) for a in (dpq, dpk, dpv))
    dproj, dlb, dgo = _hgrn_bwd("hgrn_bwd", proj, (dpq, dpk, dpv), lb, go, oraw, states, dmix, b, s)
    dwin = _grad_w_in("dw_in", h2, dproj, ns)
    dx1, dx1h, dgm = _in_proj_bwd("in_proj_bwd", dproj, w_in, x1, gm, dx2, 0.5)

    dwd1 = _grad_w_cols("ffn1_dwd", z1, dx1h)
    sent_mix = on_grads("mix", {"w_in": dwin, "w_out": dwout.reshape(ns, dwout.shape[0] // ns, d),
                                "ffn1_w_down": dwd1})
    da1, db1 = _ffn_bwd_act("ffn1_bwd_act", dx1h, wd1, a1, b1, sent_mix)
    sent_mix = grads_sent("mix", da1)
    dwg1 = _grad_w_cols("ffn1_dwg", da1, h1, sent_mix)
    dwu1 = _grad_w_cols("ffn1_dwu", db1, h1)
    on_grads("ffn1", {"ffn1_w_gate": dwg1, "ffn1_w_up": dwu1})
    sent1 = grads_sent("ffn1", None)
    dx0, dg1 = _ffn_bwd_in("ffn1_bwd_in", da1, db1, wg1, wu1, x0, g1, dx1, None, sent1)

    nt = dg1.shape[0]
    sg = _small_grads(
        "small_grads", dg1.reshape(nt, d), dgm.reshape(nt, d), dg2.reshape(nt, d),
        dgq.reshape(-1, ATTN_W), dgk.reshape(-1, ATTN_W), dbe.transpose(1, 0, 2), dbo.transpose(1, 0, 2),
        dlb.reshape(b, HGRN_W), dgo.reshape(b, HGRN_W), lbp)
    g1g, gmg, g2g, gqg, gkg, rbg, lbg, gog = sg
    small = _pack_small(g1g, gmg, g2g, lbg, rbg[:, :N_REL], gqg, gkg, gog, loss)
    return dx0.reshape(b, s, d), small


LOSS_SLOT = 7 * SMALL_COLS + 2 * ATTN_DH + HGRN_DH


def _pack_small(g1, gm, g2, lbp, rel_bias, gq, gk, go, loss=None):
    flat = [g1.reshape(-1), gm.reshape(-1), g2.reshape(-1), lbp.reshape(-1), rel_bias.reshape(-1)]
    n_bias = 3 * SMALL_COLS - rel_bias.size
    heads = [gq.reshape(-1), gk.reshape(-1), go.reshape(-1)]
    heads.append(jnp.zeros((1,), F32) if loss is None else loss.reshape(1))
    n_tail = SMALL_COLS - sum(h.size for h in heads)
    return jnp.concatenate(flat + [jnp.zeros((n_bias,), F32)] + heads + [jnp.zeros((n_tail,), F32)]).reshape(
        SMALL_ROWS, SMALL_COLS)


def _unpack_small(p, d):
    flat = p.reshape(-1)
    o = 3 * d
    g1, gm, g2 = p[0:1], p[1:2], p[2:3]
    lbp = flat[o:o + 2 * HGRN_W].reshape(2, HGRN_W)
    o = 4 * SMALL_COLS
    rel = flat[o:o + ATTN_HEADS * N_REL].reshape(1, ATTN_HEADS, N_REL)
    o = 7 * SMALL_COLS
    gq = flat[o:o + ATTN_DH].reshape(1, ATTN_DH)
    gk = flat[o + ATTN_DH:o + 2 * ATTN_DH].reshape(1, ATTN_DH)
    go = flat[o + 2 * ATTN_DH:o + 2 * ATTN_DH + HGRN_DH].reshape(1, HGRN_DH)
    return g1, gm, g2, gq, gk, rel, lbp, go


def kernel(x, ffn1_norm_g, ffn1_w_gate, ffn1_w_up, ffn1_w_down, mix_norm_g, w_in, attn_q_norm_g, attn_k_norm_g, attn_rel_bias, hgrn_lower_bounds, hgrn_out_norm_g, w_out, ffn2_norm_g, ffn2_w_gate, ffn2_w_up, ffn2_w_down, loss_target, m_ffn1_norm_g, m_ffn1_w_gate, m_ffn1_w_up, m_ffn1_w_down, m_mix_norm_g, m_w_in, m_attn_q_norm_g, m_attn_k_norm_g, m_attn_rel_bias, m_hgrn_lower_bounds, m_hgrn_out_norm_g, m_w_out, m_ffn2_norm_g, m_ffn2_w_gate, m_ffn2_w_up, m_ffn2_w_down, v_ffn1_norm_g, v_ffn1_w_gate, v_ffn1_w_up, v_ffn1_w_down, v_mix_norm_g, v_w_in, v_attn_q_norm_g, v_attn_k_norm_g, v_attn_rel_bias, v_hgrn_lower_bounds, v_hgrn_out_norm_g, v_w_out, v_ffn2_norm_g, v_ffn2_w_gate, v_ffn2_w_up, v_ffn2_w_down):
    d = x.shape[-1]
    big_w = [ffn1_w_gate, ffn1_w_up, ffn1_w_down, w_in, w_out, ffn2_w_gate, ffn2_w_up, ffn2_w_down]
    big_m = [m_ffn1_w_gate, m_ffn1_w_up, m_ffn1_w_down, m_w_in, m_w_out, m_ffn2_w_gate, m_ffn2_w_up, m_ffn2_w_down]
    big_v = [v_ffn1_w_gate, v_ffn1_w_up, v_ffn1_w_down, v_w_in, v_w_out, v_ffn2_w_gate, v_ffn2_w_up, v_ffn2_w_down]
    big_names = ["ffn1_w_gate", "ffn1_w_up", "ffn1_w_down", "w_in", "w_out", "ffn2_w_gate", "ffn2_w_up", "ffn2_w_down"]
    flipped = {nm for nm in big_names if nm.endswith("gate") or nm.endswith("up")}
    flip = lambda nm, a: jnp.swapaxes(a, 1, 2) if nm in flipped else a
    big_w, big_m, big_v = ([flip(nm, a) for nm, a in zip(big_names, arrs)] for arrs in (big_w, big_m, big_v))

    shards = [w[0].astype(BF16) for w in big_w]
    start_a = _gather_start("gather_start_up1", shards[:2], ())
    start_b = _gather_start("gather_start_mid", shards[2:5], (start_a[4],))
    start_c = _gather_start("gather_start_ffn2", shards[5:], (start_b[4],))

    pending = {}

    def arrived(tag, started, after):
        send_sem, recv_sem, srcs, outs, _ = started
        return _gather_wait("gather_wait_" + tag, send_sem, recv_sem, srcs, outs, after)

    def first_weights(after):
        return (*_gather_join("gather_join_up1", *arrived("up1", start_a, after)), (start_c[4],))

    def mid_weights(after):
        srcs, outs = arrived("mid", start_b, after)
        (wd1,) = _gather_join("gather_join_wd1", srcs[:1], outs[:1])
        pending["mid"] = _join_start("join_start_mid", srcs[1:], outs[1:])
        return wd1, (pending["mid"][3],)

    def mid_rest(after):
        sems, srcs, outs, _ = pending["mid"]
        win_f, wout_f = _join_wait("join_wait_mid", sems, srcs, outs, after)
        return win_f, wout_f.reshape(wout_f.shape[0] * wout_f.shape[1], d)

    def last_begin(after):
        pending["ffn2"] = _join_start("join_start_ffn2", *arrived("ffn2", start_c, after))
        return (pending["ffn2"][3],)

    def last_weights(after):
        sems, srcs, outs, _ = pending["ffn2"]
        return _join_wait("join_wait_ffn2", sems, srcs, outs, after)

    weights = {"first": first_weights, "mid": mid_weights, "mid_rest": mid_rest, "last_begin": last_begin,
               "last": last_weights}

    core = lax.axis_index("c").astype(jnp.int32).reshape(1)
    chip = (2 * lax.axis_index("x") + lax.axis_index("y")).astype(jnp.int32).reshape(1)
    started = {}

    def on_grads(tag, grads):
        names = list(grads)
        started[tag] = (names, _pair_start("pair_start_" + tag, [grads[nm] for nm in names]))
        return (started[tag][1][4],)

    def grads_sent(tag, after):
        names, (send_sem, recv_sem, grads, lands, token) = started[tag]
        grads, theirs = _pair_wait("pair_wait_" + tag, send_sem, recv_sem, grads, lands, token if after is None else after)
        sums = [_pair_sum("pair_sum_" + nm, g, th, core) for nm, g, th in zip(names, grads, theirs)]
        started[tag] = (names, _scatter_start("scatter_start_" + tag, sums))
        return (started[tag][1][4],)

    grad_x, small_g = _local_step(
        x, loss_target, ffn1_norm_g, mix_norm_g, ffn2_norm_g, attn_q_norm_g, attn_k_norm_g, hgrn_out_norm_g,
        attn_rel_bias[0], hgrn_lower_bounds, weights, on_grads, grads_sent)

    def finish(tag, after):
        names, (send_sem, recv_sem, sums, lands, _) = started[tag]
        sums, lands = _scatter_wait("scatter_wait_" + tag, send_sem, recv_sem, sums, lands, after)
        return names, [_chip_sum("chip_sum_" + nm, sm, ld, chip) for nm, sm, ld in zip(names, sums, lands)]

    by_name = {nm: (w, m, v) for nm, w, m, v in zip(big_names, big_w, big_m, big_v)}
    updated = {}

    def update(names, halves, other_halves):
        for nm, mine, theirs in zip(names, halves, other_halves):
            w, m, v = by_name[nm]
            updated[nm] = _adamw("adamw_" + nm, w, mine, theirs, m, v, core)

    last_token = started["ffn1"][1][4]
    names_a, halves_a = finish("ffn2", last_token)
    names_m, halves_m = finish("mix", last_token)
    names_a, halves_a = names_a + names_m, halves_a + halves_m
    update(names_a, halves_a, _pair_join("pair_join_early", halves_a))
    names_b, halves_b = finish("ffn1", updated[names_a[-1]][1])
    others_b, small_all = _pair_join("pair_join_last", halves_b, small_g)
    update(names_b, halves_b, others_b)
    big_out = [updated[nm] for nm in big_names]

    pack = lambda g1, gm, g2, gq, gk, rel, lbp, go: _pack_small(g1, gm, g2, lbp, rel[0], gq, gk, go)
    small_w = pack(ffn1_norm_g, mix_norm_g, ffn2_norm_g, attn_q_norm_g, attn_k_norm_g, attn_rel_bias, hgrn_lower_bounds, hgrn_out_norm_g)
    small_m = pack(m_ffn1_norm_g, m_mix_norm_g, m_ffn2_norm_g, m_attn_q_norm_g, m_attn_k_norm_g, m_attn_rel_bias, m_hgrn_lower_bounds, m_hgrn_out_norm_g)
    small_v = pack(v_ffn1_norm_g, v_mix_norm_g, v_ffn2_norm_g, v_attn_q_norm_g, v_attn_k_norm_g, v_attn_rel_bias, v_hgrn_lower_bounds, v_hgrn_out_norm_g)
    small_res = _adamw_small("adamw_small", small_w, small_all, small_m, small_v)
    small_out = [_unpack_small(p, d) for p in small_res]
    loss = small_res[0].reshape(-1)[LOSS_SLOT]

    def assemble(kind):
        bg = [flip(nm, o[kind]) for nm, o in zip(big_names, big_out)]
        g1, gm, g2, gq, gk, rel, lbp, go = small_out[kind]
        return [g1, bg[0], bg[1], bg[2], gm, bg[3], gq, gk, rel, lbp, go, bg[4], g2, bg[5], bg[6], bg[7]]

    return (loss, grad_x, *assemble(0), *assemble(1), *assemble(2), *assemble(3))
```

```python
import functools

import jax
import jax.numpy as jnp
from jax import lax
from jax.experimental import pallas as pl
from jax.experimental.pallas import tpu as pltpu

F32 = jnp.float32
BF16 = jnp.bfloat16
MESH = pl.DeviceIdType.MESH

N_CHIPS = 4
N_DEV = 8
CHUNK = 64
ATTN_HEADS = 8
ATTN_DH = 64
ATTN_W = ATTN_HEADS * ATTN_DH
HGRN_HEADS = 4
HGRN_DH = 128
HGRN_W = HGRN_HEADS * HGRN_DH
LEFT_CHUNKS = 8
BAND = (LEFT_CHUNKS + 1) * CHUNK
KPAD = LEFT_CHUNKS * CHUNK
REL_CLIP = 128
N_REL = 2 * REL_CLIP + 1
N_REL_PAD = 384
RMS_EPS = 1e-6
LANES = 128
SMALL_ROWS = 8
SMALL_COLS = 1024

ADAM_LR = 0.001
ADAM_B1 = 0.9
ADAM_B2 = 0.999
ADAM_EPS = 1e-08
ADAM_WD = 0.01
ADAM_STEP = 10

NN = (((1,), (0,)), ((), ()))
NT = (((1,), (1,)), ((), ()))
TN = (((0,), (0,)), ((), ()))

VMEM_LIMIT = 48 * 1024 * 1024
MXU_WIDTH = 256
COL_CHUNK = 3 * MXU_WIDTH


def _sigmoid(x):
    return 1.0 / (1.0 + jnp.exp(-x))


def _silu(x):
    return x * _sigmoid(x)


def _dot(a, b, dims=NN):
    return lax.dot_general(a, b, dims, preferred_element_type=F32)


def _split3(x):
    hi = x.astype(BF16)
    r1 = x - hi.astype(F32)
    mid = r1.astype(BF16)
    lo = (r1 - mid.astype(F32)).astype(BF16)
    return hi, mid, lo


def _dot_exact_rhs(x, mat, dims=NN, pieces=3):
    hi, mid, lo = _split3(x)
    out = _dot(hi, mat, dims) + _dot(mid, mat, dims)
    return out + _dot(lo, mat, dims) if pieces == 3 else out


def _dot_exact_lhs(mat, x, dims=NN):
    hi, mid, lo = _split3(x)
    return _dot(mat, hi, dims) + _dot(mat, mid, dims) + _dot(mat, lo, dims)


def _params(*sem):
    return pltpu.CompilerParams(dimension_semantics=sem, vmem_limit_bytes=VMEM_LIMIT)


def _mm(name, ins, terms, n_acc, grid, acc_shape, outs, epilogue, extras=(), deps=()):
    nk = grid[2]
    ni, ne, nd, no = len(ins), len(extras), len(deps), len(outs)

    def body(*refs):
        in_refs = refs[:ni]
        ex_refs = refs[ni:ni + ne]
        out_refs = refs[ni + ne + nd:ni + ne + nd + no]
        acc_refs = refs[ni + ne + nd + no:]

        def products():
            parts = [None] * n_acc
            for ai, li, ri, dims in terms:
                d = _dot(in_refs[li][...], in_refs[ri][...], dims)
                parts[ai] = d if parts[ai] is None else parts[ai] + d
            return parts

        def finish(accs):
            res = epilogue(accs, [e[...] for e in ex_refs])
            for o, r in zip(out_refs, res):
                o[...] = r.astype(o.dtype)

        if nk == 1:
            finish(products())
        else:
            k = pl.program_id(2)

            @pl.when(k == 0)
            def _():
                for a, p in zip(acc_refs, products()):
                    a[...] = p

            if nk > 2:
                @pl.when(jnp.logical_and(k > 0, k < nk - 1))
                def _():
                    for a, p in zip(acc_refs, products()):
                        a[...] += p

            @pl.when(k == nk - 1)
            def _():
                finish([a[...] + p for a, p in zip(acc_refs, products())])

    scratch = [] if nk == 1 else [pltpu.VMEM(acc_shape, F32) for _ in range(n_acc)]
    res = pl.pallas_call(
        body,
        name=name,
        grid=grid,
        in_specs=[s for _, s in ins] + [s for _, s in extras] + [pl.BlockSpec(memory_space=pl.ANY)] * nd,
        out_specs=[s for _, s in outs],
        out_shape=[o for o, _ in outs],
        scratch_shapes=scratch,
        compiler_params=_params("parallel", "parallel", "arbitrary"),
    )(*[a for a, _ in ins], *[a for a, _ in extras], *deps)
    return res


def _staged_shape(w):
    return w.shape if len(w.shape) == 2 else (w.shape[1], w.shape[0] * w.shape[2])


def _stage_weights(w_hbm, w_vmem, sem):
    @pl.when(pl.program_id(0) == 0)
    def _():
        copies = []
        for p, (h, v) in enumerate(zip(w_hbm, w_vmem)):
            if len(h.shape) == 2:
                copies.append(pltpu.make_async_copy(h, v, sem.at[p, 0]))
            else:
                pj = h.shape[2]
                copies += [pltpu.make_async_copy(h.at[j], v.at[:, pl.ds(j * pj, pj)], sem.at[p, j])
                           for j in range(h.shape[0])]
        for cp in copies:
            cp.start()
        for cp in copies:
            cp.wait()


def _staging_scratch(weights):
    return [pltpu.VMEM(_staged_shape(w), w.dtype) for w in weights] + [pltpu.SemaphoreType.DMA((len(weights), N_CHIPS))]


def _mm_rows(name, lhs, weights, dims, t, outs, epilogue, extras=(), deps=()):
    tm = _row_tile(t)
    nl, ne, nd, no = len(lhs), len(extras), len(deps), len(outs)

    def body(*refs):
        lhs_refs = refs[:nl]
        w_hbm = refs[nl:2 * nl]
        ex_refs = refs[2 * nl:2 * nl + ne]
        out_refs = refs[2 * nl + ne + nd:2 * nl + ne + nd + no]
        w_vmem = refs[2 * nl + ne + nd + no:3 * nl + ne + nd + no]
        _stage_weights(w_hbm, w_vmem, refs[-1])

        acc = None
        for p in range(nl):
            part = _dot(lhs_refs[p][...], w_vmem[p][...], dims)
            acc = part if acc is None else acc + part
        res = epilogue([acc], [e[...] for e in ex_refs])
        for o, r in zip(out_refs, res):
            o[...] = r.astype(o.dtype)

    return pl.pallas_call(
        body,
        name=name,
        grid=(t // tm,),
        in_specs=[s for _, s in lhs] + [pl.BlockSpec(memory_space=pl.ANY)] * nl + [s for _, s in extras]
        + [pl.BlockSpec(memory_space=pl.ANY)] * nd,
        out_specs=[s for _, s in outs],
        out_shape=[o for o, _ in outs],
        scratch_shapes=_staging_scratch(weights),
        compiler_params=_params("arbitrary"),
    )(*[a for a, _ in lhs], *weights, *[a for a, _ in extras], *deps)


def _col_chunks(f):
    return [(c, min(COL_CHUNK, f - c)) for c in range(0, f, COL_CHUNK)]


def _mm_cols(name, x, weights, dims, n_out, epilogue, extras=(), deps=(), out_dtype=BF16):
    t, k = x.shape
    f = _staged_shape(weights[0])[0 if dims == NT else 1]
    tm = _row_tile(t)
    chunks = _col_chunks(f)
    nw, ne, nd = len(weights), len(extras), len(deps)

    def body(*refs):
        x_ref = refs[0]
        w_hbm = refs[1:1 + nw]
        ex_refs = refs[1 + nw:1 + nw + ne]
        out_refs = refs[1 + nw + ne + nd:1 + nw + ne + nd + n_out]
        w_vmem = refs[1 + nw + ne + nd + n_out:1 + 2 * nw + ne + nd + n_out]
        _stage_weights(w_hbm, w_vmem, refs[-1])

        xv = x_ref[...]

        def dots(c):
            c0, cw = chunks[c]
            return [_dot(xv, w[c0:c0 + cw, :] if dims == NT else w[:, c0:c0 + cw], dims) for w in w_vmem]

        accs = dots(0)
        for c, (c0, cw) in enumerate(chunks):
            nxt = dots(c + 1) if c + 1 < len(chunks) else None
            res = epilogue(accs, [e[:, c0:c0 + cw] for e in ex_refs])
            for o, r in zip(out_refs, res):
                o[:, c0:c0 + cw] = r.astype(o.dtype)
            accs = nxt

    act = pl.BlockSpec((tm, f), lambda i: (i, 0))
    return pl.pallas_call(
        body,
        name=name,
        grid=(t // tm,),
        in_specs=[pl.BlockSpec((tm, k), lambda i: (i, 0))] + [pl.BlockSpec(memory_space=pl.ANY)] * nw + [act] * ne
        + [pl.BlockSpec(memory_space=pl.ANY)] * nd,
        out_specs=[act] * n_out,
        out_shape=[jax.ShapeDtypeStruct((t, f), out_dtype)] * n_out,
        scratch_shapes=_staging_scratch(weights),
        compiler_params=_params("arbitrary"),
    )(x, *weights, *extras, *deps)


def _row_tile(t):
    return 512 if t % 512 == 0 else t


def _k_tile(t):
    return t if t <= 4096 else 1024


def _grad_k_tile(t):
    return 2048 if t % 2048 == 0 else t


def _rmsnorm(xv, g):
    ms = jnp.mean(xv * xv, axis=-1, keepdims=True)
    return xv * lax.rsqrt(ms + RMS_EPS) * g


def _rmsnorm_fwd(name, x, g):
    t, d = x.shape
    tm = _row_tile(t)

    def body(x_ref, g_ref, h_ref):
        h_ref[...] = _rmsnorm(x_ref[...], g_ref[...]).astype(BF16)

    return pl.pallas_call(
        body,
        name=name,
        grid=(t // tm,),
        in_specs=[pl.BlockSpec((tm, d), lambda i: (i, 0)), pl.BlockSpec((1, d), lambda i: (0, 0))],
        out_specs=pl.BlockSpec((tm, d), lambda i: (i, 0)),
        out_shape=jax.ShapeDtypeStruct((t, d), BF16),
        compiler_params=_params("parallel"),
    )(x, g)


def _norm_bwd_epilogue(copy_scale):
    def epilogue(accs, ex):
        dh = accs[0]
        xv, g, dres = ex
        ms = jnp.mean(xv * xv, axis=-1, keepdims=True)
        rstd = lax.rsqrt(ms + RMS_EPS)
        xhat = xv * rstd
        dxhat = dh * g
        dx = rstd * (dxhat - xhat * jnp.mean(dxhat * xhat, axis=-1, keepdims=True))
        out = dres + dx
        dg = jnp.sum(dh * xhat, axis=0, keepdims=True)
        if copy_scale is None:
            return out, dg
        return out, out * copy_scale, dg

    return epilogue


def _merged(w):
    return w.reshape(-1, w.shape[-1])


def _ffn_up(name, h, wg, wu, deps=()):
    def epilogue(accs, ex):
        a, b = accs
        sg = _sigmoid(a)
        act = a * sg
        return act, b * (sg * (1.0 + a * (1.0 - sg))), act * b

    return _mm_cols(name, h, [_merged(wg), _merged(wu)], NT, 3, epilogue, deps=deps)


def _whole_rows(arr, tm):
    return arr, pl.BlockSpec((tm, arr.shape[1]), lambda i: (i, 0))


def _ffn_down(name, z, wd, x, g_next, deps=()):
    t = z.shape[0]
    d = wd.shape[2]
    tm = _row_tile(t)
    row = pl.BlockSpec((tm, d), lambda i: (i, 0))

    def epilogue(accs, ex):
        y = ex[0] + 0.5 * accs[0]
        return y, _rmsnorm(y, ex[1])

    return _mm_rows(
        name, [_whole_rows(z, tm)], [_merged(wd)], NN, t,
        outs=[(jax.ShapeDtypeStruct((t, d), F32), row), (jax.ShapeDtypeStruct((t, d), BF16), row)],
        epilogue=epilogue,
        extras=[(x, row), (g_next, pl.BlockSpec((1, d), lambda i: (0, 0)))],
        deps=deps,
    )


def _ffn_down_loss(name, z, wd, x, target):
    t = z.shape[0]
    d = wd.shape[2]
    tm = _row_tile(t)
    nt = t // tm
    row = pl.BlockSpec((tm, d), lambda i: (i, 0))

    def epilogue(accs, ex):
        e = ex[0] + 0.5 * accs[0] - ex[1]
        dy = e * (1.0 / d)
        return dy, 0.5 * dy, jnp.sum(e * e, axis=0, keepdims=True)

    return _mm_rows(
        name, [_whole_rows(z, tm)], [_merged(wd)], NN, t,
        outs=[(jax.ShapeDtypeStruct((t, d), F32), row), (jax.ShapeDtypeStruct((t, d), BF16), row),
              (jax.ShapeDtypeStruct((nt, 1, d), F32), pl.BlockSpec((None, 1, d), lambda i: (i, 0, 0)))],
        epilogue=epilogue,
        extras=[(x, row), (target, row)],
    )


def _ffn_bwd_act(name, dout, wd, act_a, dact_b, deps=()):
    def epilogue(accs, ex):
        dz = accs[0]
        return dz * ex[1].astype(F32), dz * ex[0].astype(F32)

    return _mm_cols(name, dout, [_merged(wd)], NT, 2, epilogue, extras=[act_a, dact_b], deps=deps)


def _grad_w_cols(name, z, dout, deps=()):
    t, f = z.shape
    d = dout.shape[1]
    tk = _grad_k_tile(t)
    fh = f // 2
    dw = _mm(
        name,
        ins=[(z, pl.BlockSpec((tk, fh), lambda j, n, k: (k, j))),
             (dout, pl.BlockSpec((tk, d), lambda j, n, k: (k, 0)))],
        terms=[(0, 0, 1, TN)],
        n_acc=1,
        grid=(2, 1, t // tk),
        acc_shape=(fh, d),
        outs=[(pltpu.HBM((f, d), BF16), pl.BlockSpec((fh, d), lambda j, n, k: (j, 0)))],
        epilogue=lambda accs, ex: (accs[0],),
        deps=deps,
    )[0]
    return dw.reshape(N_CHIPS, f // N_CHIPS, d)


def _norm_bwd_outs(t, d, tm, copy_scale):
    row = pl.BlockSpec((tm, d), lambda i: (i, 0))
    outs = [(jax.ShapeDtypeStruct((t, d), F32), row)]
    if copy_scale is not None:
        outs.append((jax.ShapeDtypeStruct((t, d), BF16), row))
    outs.append((jax.ShapeDtypeStruct((t // tm, 1, d), F32), pl.BlockSpec((None, 1, d), lambda i: (i, 0, 0))))
    return row, outs


def _ffn_bwd_in(name, da, db, wg, wu, x, g, dres, copy_scale, deps=()):
    t = da.shape[0]
    d = wg.shape[2]
    tm = _row_tile(t)
    row, outs = _norm_bwd_outs(t, d, tm, copy_scale)
    return _mm_rows(
        name, [_whole_rows(da, tm), _whole_rows(db, tm)], [_merged(wg), _merged(wu)], NN, t,
        outs=outs,
        epilogue=_norm_bwd_epilogue(copy_scale),
        extras=[(x, row), (g, pl.BlockSpec((1, d), lambda i: (0, 0))), (dres, row)],
        deps=deps,
    )


def _in_proj(name, h, w_in):
    return _mm_cols(name, h, [w_in], NN, 1, lambda accs, ex: (accs[0],), out_dtype=F32)[0]


def _in_proj_bwd(name, dp, w_in, x, g, dres, copy_scale, deps=()):
    t = dp.shape[0]
    d = w_in.shape[1]
    tm = _row_tile(t)
    row, outs = _norm_bwd_outs(t, d, tm, copy_scale)
    return _mm_rows(
        name, [_whole_rows(dp, tm)], [w_in], NT, t,
        outs=outs,
        epilogue=_norm_bwd_epilogue(copy_scale),
        extras=[(x, row), (g, pl.BlockSpec((1, d), lambda i: (0, 0))), (dres, row)],
        deps=deps,
    )


def _grad_w_in(name, h, dp, ns):
    t, d = h.shape
    pj = dp.shape[1] // ns
    tk = 1024 if t % 1024 == 0 else t
    return _mm(
        name,
        ins=[(h, pl.BlockSpec((tk, d), lambda j, n, k: (k, 0))),
             (dp, pl.BlockSpec((tk, 2 * pj), lambda j, n, k: (k, j)))],
        terms=[(0, 0, 1, TN)],
        n_acc=1,
        grid=(ns // 2, 1, t // tk),
        acc_shape=(d, 2 * pj),
        outs=[(pltpu.HBM((ns, d, pj), BF16), pl.BlockSpec((2, d, pj), lambda j, n, k: (j, 0, 0)))],
        epilogue=lambda accs, ex: (jnp.stack([accs[0][:, :pj], accs[0][:, pj:]]),),
    )[0]


def _out_proj(name, mix, w_out, x, g_next):
    t, dm = mix.shape
    d = w_out.shape[1]
    tm = _row_tile(t)
    row = pl.BlockSpec((tm, d), lambda i, n, k: (i, 0))
    return _mm(
        name,
        ins=[(mix, pl.BlockSpec((tm, dm), lambda i, n, k: (i, 0))),
             (w_out, pl.BlockSpec((dm, d), lambda i, n, k: (0, 0)))],
        terms=[(0, 0, 1, NN)],
        n_acc=1,
        grid=(t // tm, 1, 1),
        acc_shape=(tm, d),
        outs=[(jax.ShapeDtypeStruct((t, d), F32), row), (jax.ShapeDtypeStruct((t, d), BF16), row)],
        epilogue=lambda accs, ex: (ex[0] + accs[0], _rmsnorm(ex[0] + accs[0], ex[1])),
        extras=[(x, row), (g_next, pl.BlockSpec((1, d), lambda i, n, k: (0, 0)))],
    )


def _out_proj_bwd(name, dx, w_out, deps=()):
    t, d = dx.shape
    dm = w_out.shape[0]
    tm = _row_tile(t)
    return _mm(
        name,
        ins=[(dx, pl.BlockSpec((tm, d), lambda i, n, k: (i, 0))),
             (w_out, pl.BlockSpec((dm, d), lambda i, n, k: (0, 0)))],
        terms=[(0, 0, 1, NT)],
        n_acc=1,
        grid=(t // tm, 1, 1),
        acc_shape=(tm, dm),
        outs=[(jax.ShapeDtypeStruct((t, dm), F32), pl.BlockSpec((tm, dm), lambda i, n, k: (i, 0)))],
        epilogue=lambda accs, ex: (accs[0],),
        deps=deps,
    )[0]


def _grad_w_out(name, mix, dx):
    t, dm = mix.shape
    d = dx.shape[1]
    tk = _k_tile(t)
    return _mm(
        name,
        ins=[(mix, pl.BlockSpec((tk, dm), lambda a, n, k: (k, 0))),
             (dx, pl.BlockSpec((tk, d), lambda a, n, k: (k, 0)))],
        terms=[(0, 0, 1, TN)],
        n_acc=1,
        grid=(1, 1, t // tk),
        acc_shape=(dm, d),
        outs=[(pltpu.HBM((dm, d), BF16), pl.BlockSpec((dm, d), lambda a, n, k: (0, 0)))],
        epilogue=lambda accs, ex: (accs[0],),
    )[0]


def _head_group_matrix():
    r = lax.broadcasted_iota(jnp.int32, (MXU_WIDTH, MXU_WIDTH), 0)
    c = lax.broadcasted_iota(jnp.int32, (MXU_WIDTH, MXU_WIDTH), 1)
    same = jnp.right_shift(r, 6) == jnp.right_shift(c, 6)
    return jnp.where(same, 1.0, 0.0).astype(BF16)


def _head_sums(x, bd):
    return jnp.concatenate(
        [_dot_exact_rhs(x[:, c:c + MXU_WIDTH], bd, pieces=2) for c in range(0, x.shape[1], MXU_WIDTH)], axis=1)


def _qk_prep(name, proj, gq, gk):
    b, s, _ = proj.shape
    tm = KPAD
    nb = s // tm

    def body(q_ref, k_ref, v_ref, gq_ref, gk_ref, qn_ref, kn_ref, vb_ref):
        j = pl.program_id(1)
        bd = _head_group_matrix()

        def norm(xv, g):
            ms = _head_sums(xv * xv, bd) * (1.0 / ATTN_DH)
            return xv * lax.rsqrt(ms + RMS_EPS) * g

        @pl.when(j == 0)
        def _():
            kn_ref[...] = jnp.zeros_like(kn_ref)
            vb_ref[...] = jnp.zeros_like(vb_ref)

        @pl.when(j > 0)
        def _():
            qn_ref[...] = (norm(q_ref[...], gq_ref[...]) * (ATTN_DH ** -0.5)).astype(BF16)
            kn_ref[...] = norm(k_ref[...], gk_ref[...]).astype(BF16)
            vb_ref[...] = v_ref[...].astype(BF16)

    src_blk = lambda col: pl.BlockSpec((None, tm, ATTN_W), lambda bi, j: (bi, jnp.maximum(j - 1, 0), col))
    gspec = pl.BlockSpec((1, ATTN_W), lambda bi, j: (0, 0))
    padded = pl.BlockSpec((None, tm, ATTN_W), lambda bi, j: (bi, j, 0))
    return pl.pallas_call(
        body,
        name=name,
        grid=(b, nb + 1),
        in_specs=[src_blk(0), src_blk(1), src_blk(2), gspec, gspec],
        out_specs=[src_blk(0), padded, padded],
        out_shape=[jax.ShapeDtypeStruct((b, s, ATTN_W), BF16), jax.ShapeDtypeStruct((b, KPAD + s, ATTN_W), BF16),
                   jax.ShapeDtypeStruct((b, KPAD + s, ATTN_W), BF16)],
        compiler_params=_params("parallel", "arbitrary"),
    )(proj, proj, proj, gq, gk)


def _qk_prep_bwd(name, proj, dqn, dkn, dv, gq, gk):
    b, s, _ = proj.shape
    tm = KPAD
    nb = s // tm

    def body(q_ref, k_ref, dqn_ref, dkn_ref, dv_ref, gq_ref, gk_ref, dq_ref, dk_ref, dvb_ref, dgq_ref, dgk_ref):
        bd = _head_group_matrix()

        def bwd(xv, dy, g):
            ms = _head_sums(xv * xv, bd) * (1.0 / ATTN_DH)
            rstd = lax.rsqrt(ms + RMS_EPS)
            xhat = xv * rstd
            dxhat = dy * g
            gm = _head_sums(dxhat * xhat, bd) * (1.0 / ATTN_DH)
            return rstd * (dxhat - xhat * gm), jnp.sum(dy * xhat, axis=0, keepdims=True)

        dq, dgq = bwd(q_ref[...], dqn_ref[...], gq_ref[...])
        dk, dgk = bwd(k_ref[...], dkn_ref[...], gk_ref[...])
        dq_ref[...] = dq.astype(BF16)
        dk_ref[...] = dk.astype(BF16)
        dvb_ref[...] = dv_ref[...].astype(BF16)
        dgq_ref[...] = dgq
        dgk_ref[...] = dgk

    col = lambda c: pl.BlockSpec((None, tm, ATTN_W), lambda bi, j: (bi, j, c))
    past_pad = pl.BlockSpec((None, tm, ATTN_W), lambda bi, j: (bi, j + 1, 0))
    gspec = pl.BlockSpec((1, ATTN_W), lambda bi, j: (0, 0))
    pspec = pl.BlockSpec((None, 1, ATTN_W), lambda bi, j: (bi * nb + j, 0, 0))
    o_shape = jax.ShapeDtypeStruct((b, s, ATTN_W), BF16)
    p_shape = jax.ShapeDtypeStruct((b * nb, 1, ATTN_W), F32)
    return pl.pallas_call(
        body,
        name=name,
        grid=(b, nb),
        in_specs=[col(0), col(1), col(0), past_pad, past_pad, gspec, gspec],
        out_specs=[col(0)] * 3 + [pspec] * 2,
        out_shape=[o_shape] * 3 + [p_shape] * 2,
        compiler_params=_params("parallel", "parallel"),
    )(proj, proj, dqn, dkn, dv, gq, gk)


Q_CHUNKS = 4
QBLK = Q_CHUNKS * CHUNK
WIN = (LEFT_CHUNKS + Q_CHUNKS) * CHUNK
DB_W = BAND + CHUNK
MASKED = -1e30
FWD_BLOCKS = 8
BWD_BLOCKS = 2


def _band_table(bias):
    rows = [jnp.pad(bias, ((0, 0), (0, 0), (CHUNK * i, WIN - BAND - CHUNK * i)), constant_values=MASKED)
            for i in range(Q_CHUNKS)]
    return jnp.concatenate(rows, axis=1)


def _head_lanes(hh):
    lane = lax.broadcasted_iota(jnp.int32, (1, LANES), 1)
    return (lane < ATTN_DH) if hh == 0 else (lane >= ATTN_DH)


def _attn_probs(qh, kw, table, start):
    s = _dot(qh, kw, NT) + table
    col = lax.broadcasted_iota(jnp.int32, (QBLK, WIN), 1)
    s = jnp.where(col + start >= KPAD, s, MASKED)
    m = jnp.max(s, axis=-1, keepdims=True)
    p = jnp.exp(s - m)
    return p * (1.0 / jnp.sum(p, axis=-1, keepdims=True))


def _attn_fwd(name, q, k, v, table, deps=()):
    b, s, w = q.shape
    sp = k.shape[1]

    def body(q_ref, k_ref, v_ref, t_ref, *rest):
        o_ref = rest[-1]
        lanes = [_head_lanes(hh) for hh in range(2)]
        starts = [pl.multiple_of((pl.program_id(2) * FWD_BLOCKS + j) * QBLK, QBLK) for j in range(FWD_BLOCKS)]
        kws = [k_ref[pl.ds(st, WIN), :] for st in starts]
        vws = [v_ref[pl.ds(st, WIN), :] for st in starts]
        q2s = [q_ref[j * QBLK:(j + 1) * QBLK, :] for j in range(FWD_BLOCKS)]
        probs = [[_attn_probs(jnp.where(mine, q2s[j], jnp.zeros_like(q2s[j])), kws[j], t_ref[hh], starts[j]).astype(BF16)
                  for hh, mine in enumerate(lanes)] for j in range(FWD_BLOCKS)]
        for j in range(FWD_BLOCKS):
            outs = [_dot(p, vws[j]) for p in probs[j]]
            o_ref[j * QBLK:(j + 1) * QBLK, :] = jnp.where(lanes[0], outs[0], outs[1]).astype(BF16)

    qspec = pl.BlockSpec((None, FWD_BLOCKS * QBLK, LANES), lambda p, bi, i: (bi, i, p))
    kspec = pl.BlockSpec((None, sp, LANES), lambda p, bi, i: (bi, 0, p))
    return pl.pallas_call(
        body,
        name=name,
        grid=(w // LANES, b, s // (FWD_BLOCKS * QBLK)),
        in_specs=[qspec, kspec, kspec, pl.BlockSpec((2, QBLK, WIN), lambda p, bi, i: (p, 0, 0))] + [ANY] * len(deps),
        out_specs=qspec,
        out_shape=jax.ShapeDtypeStruct((b, s, w), BF16),
        compiler_params=_params("parallel", "parallel", "arbitrary"),
    )(q, k, v, table, *deps)


def _attn_bwd(name, q, k, v, table, dmix):
    b, s, w = q.shape
    sp = k.shape[1]

    def body(q_ref, k_ref, v_ref, t_ref, do_ref, dq_ref, dk_ref, dv_ref, dbe_ref, dbo_ref):
        bi = pl.program_id(1)
        i = pl.program_id(2)

        @pl.when(i == 0)
        def _():
            dk_ref[...] = jnp.zeros_like(dk_ref)
            dv_ref[...] = jnp.zeros_like(dv_ref)

        @pl.when(jnp.logical_and(i == 0, bi == 0))
        def _():
            dbe_ref[...] = jnp.zeros_like(dbe_ref)
            dbo_ref[...] = jnp.zeros_like(dbo_ref)

        lanes = [_head_lanes(hh) for hh in range(2)]

        def scores(j):
            start = pl.multiple_of((i * BWD_BLOCKS + j) * QBLK, QBLK)
            win = pl.ds(start, WIN)
            kw = k_ref[win, :]
            vw = v_ref[win, :]
            q2 = q_ref[j * QBLK:(j + 1) * QBLK, :]
            do2 = do_ref[j * QBLK:(j + 1) * QBLK, :].astype(BF16)
            qh = [jnp.where(mine, q2, jnp.zeros_like(q2)) for mine in lanes]
            doh = [jnp.where(mine, do2, jnp.zeros_like(do2)) for mine in lanes]
            p = [_attn_probs(qh[hh], kw, t_ref[hh], start) for hh in range(2)]
            dp = [_dot(doh[hh], vw, NT) for hh in range(2)]
            return win, kw, qh, doh, p, dp

        def gradients(j, win, kw, qh, doh, p, dp):
            ds = [p[hh] * (dp[hh] - jnp.sum(p[hh] * dp[hh], axis=-1, keepdims=True)) for hh in range(2)]
            dsb = [x.astype(BF16) for x in ds]
            pb = [x.astype(BF16) for x in p]
            dq = [_dot(dsb[hh], kw) * (ATTN_DH ** -0.5) for hh in range(2)]
            dk = [_dot(dsb[hh], qh[hh], TN) for hh in range(2)]
            dv = [_dot(pb[hh], doh[hh], TN) for hh in range(2)]
            for hh in range(2):
                for qi in range(Q_CHUNKS):
                    c0 = (qi // 2) * LANES
                    blk = ds[hh][qi * CHUNK:(qi + 1) * CHUNK, c0:c0 + DB_W]
                    if qi % 2 == 0:
                        dbe_ref[hh] += blk
                    else:
                        dbo_ref[hh] += blk
            dq_ref[j * QBLK:(j + 1) * QBLK, :] = jnp.where(lanes[0], dq[0], dq[1])
            dk_ref[win, :] += dk[0] + dk[1]
            dv_ref[win, :] += dv[0] + dv[1]

        staged = scores(0)
        for j in range(BWD_BLOCKS):
            upcoming = scores(j + 1) if j + 1 < BWD_BLOCKS else None
            gradients(j, *staged)
            staged = upcoming

    qspec = pl.BlockSpec((None, BWD_BLOCKS * QBLK, LANES), lambda p, bi, i: (bi, i, p))
    kspec = pl.BlockSpec((None, sp, LANES), lambda p, bi, i: (bi, 0, p))
    dbspec = pl.BlockSpec((2, CHUNK, DB_W), lambda p, bi, i: (p, 0, 0))
    db_shape = jax.ShapeDtypeStruct((ATTN_HEADS, CHUNK, DB_W), F32)
    return pl.pallas_call(
        body,
        name=name,
        grid=(w // LANES, b, s // (BWD_BLOCKS * QBLK)),
        in_specs=[qspec, kspec, kspec, pl.BlockSpec((2, QBLK, WIN), lambda p, bi, i: (p, 0, 0)), qspec],
        out_specs=[qspec, kspec, kspec, dbspec, dbspec],
        out_shape=[jax.ShapeDtypeStruct((b, s, w), F32), jax.ShapeDtypeStruct((b, sp, w), F32),
                   jax.ShapeDtypeStruct((b, sp, w), F32), db_shape, db_shape],
        compiler_params=_params("arbitrary", "arbitrary", "arbitrary"),
    )(q, k, v, table, dmix)


HQ_COL = 3 * ATTN_W // HGRN_DH
HF_COL = HQ_COL + HGRN_HEADS
HI_COL = HF_COL + HGRN_HEADS
HG_COL = HI_COL + HGRN_HEADS
HGRN_ROWS = 8 * CHUNK
HGRN_UNROLL = 8
HEAD_LANES = [slice(hh * HGRN_DH, (hh + 1) * HGRN_DH) for hh in range(HGRN_HEADS)]


def _tri(lower):
    r = lax.broadcasted_iota(jnp.int32, (CHUNK, CHUNK), 0)
    c = lax.broadcasted_iota(jnp.int32, (CHUNK, CHUNK), 1)
    return (r >= c) if lower else (r <= c)


def _hgrn_chunk(hq, hf, lb, tril):
    sig = _sigmoid(hf)
    f = lb + (1.0 - lb) * sig
    g = jnp.log(f)
    ones_l = jnp.where(tril, 1.0, 0.0).astype(BF16)
    b = _dot_exact_lhs(ones_l, g)
    bl = jnp.sum(g, axis=0, keepdims=True)
    rows = lax.broadcasted_iota(jnp.int32, g.shape, 0)
    bm = jnp.sum(jnp.where(rows <= CHUNK // 2, g, 0.0), axis=0, keepdims=True)
    sq = _sigmoid(hq)
    q = hq * sq
    k = 1.0 - f
    return sig, f, b, bl, bm, sq, q, k


def _hgrn_fwd(name, proj, attn, lb, go, b, s):
    nc = s // CHUNK
    t = b * s
    nblk = s // HGRN_ROWS
    cpb = HGRN_ROWS // CHUNK

    def body(hq_ref, hf_ref, hi_ref, hg_ref, attn_ref, lb_ref, go_ref, mix_ref, oraw_ref, st_ref, s_scr):
        tril = _tri(True)
        gov = go_ref[...]
        mix_ref[:, 0:ATTN_W] = attn_ref[...]

        @pl.when(pl.program_id(1) == 0)
        def _():
            s_scr[...] = jnp.zeros_like(s_scr)

        def step(c, carry):
            sl = pl.ds(pl.multiple_of(c * CHUNK, CHUNK), CHUNK)
            hg = hg_ref[sl, :]
            _, _, bb, bl, bm, _, q, k = _hgrn_chunk(hq_ref[sl, :], hf_ref[sl, :], lb_ref[...], tril)
            vb = hi_ref[sl, :].astype(BF16)
            qe = (q * jnp.exp(bb - bm)).astype(BF16)
            ke = (k * jnp.exp(bm - bb)).astype(BF16)
            qb = (q * jnp.exp(bb)).astype(BF16)
            kb = (k * jnp.exp(bl - bb)).astype(BF16)
            e_last = jnp.exp(bl)
            gate = _silu(hg)
            st = [s_scr[hh] for hh in range(HGRN_HEADS)]
            a = [jnp.where(tril, _dot(qe[:, hs], ke[:, hs], NT), 0.0).astype(BF16) for hs in HEAD_LANES]
            o_state = [_dot(qb[:, hs], st[hh].astype(BF16), NT) for hh, hs in enumerate(HEAD_LANES)]
            st_next = [st[hh] * e_last[:, hs] + _dot(vb[:, hs], kb[:, hs], TN) for hh, hs in enumerate(HEAD_LANES)]
            o = [_dot(a[hh], vb[:, hs]) + o_state[hh] for hh, hs in enumerate(HEAD_LANES)]
            ro = [(oh * lax.rsqrt(jnp.mean(oh * oh, axis=-1, keepdims=True) + RMS_EPS) * gov) * gate[:, hs]
                  for oh, hs in zip(o, HEAD_LANES)]
            for hh in range(HGRN_HEADS):
                st_ref[hh, c] = st[hh]
                s_scr[hh] = st_next[hh]
            mix_ref[sl, ATTN_W:ATTN_W + HGRN_W] = jnp.concatenate(ro, axis=1).astype(BF16)
            oraw_ref[sl, :] = jnp.concatenate(o, axis=1)
            return carry

        lax.fori_loop(0, cpb, step, 0, unroll=HGRN_UNROLL)

    col = lambda base: pl.BlockSpec((HGRN_ROWS, HGRN_W), lambda bi, i: (bi * nblk + i, base // HGRN_HEADS))
    out = pl.BlockSpec((HGRN_ROWS, HGRN_W), lambda bi, i: (bi * nblk + i, 0))
    return pl.pallas_call(
        body,
        name=name,
        grid=(b, nblk),
        in_specs=[col(HQ_COL), col(HF_COL), col(HI_COL), col(HG_COL), out,
                  pl.BlockSpec((1, HGRN_W), lambda bi, i: (0, 0)), pl.BlockSpec((1, HGRN_DH), lambda bi, i: (0, 0))],
        out_specs=[pl.BlockSpec((HGRN_ROWS, ATTN_W + HGRN_W), lambda bi, i: (bi * nblk + i, 0)), out,
                   pl.BlockSpec((None, HGRN_HEADS, cpb, HGRN_DH, HGRN_DH), lambda bi, i: (bi, 0, i, 0, 0))],
        out_shape=[jax.ShapeDtypeStruct((t, ATTN_W + HGRN_W), BF16), jax.ShapeDtypeStruct((t, HGRN_W), F32),
                   jax.ShapeDtypeStruct((b, HGRN_HEADS, nc, HGRN_DH, HGRN_DH), F32)],
        scratch_shapes=[pltpu.VMEM((HGRN_HEADS, HGRN_DH, HGRN_DH), F32)],
        compiler_params=_params("parallel", "arbitrary"),
    )(proj, proj, proj, proj, attn, lb, go)


def _hgrn_bwd(name, proj, dqkv, lb, go, oraw, states, dmix, b, s):
    t = b * s
    nblk = s // HGRN_ROWS
    cpb = HGRN_ROWS // CHUNK

    def body(hq_ref, hf_ref, hi_ref, hg_ref, dq_ref, dk_ref, dv_ref, lb_ref, go_ref, oraw_ref, st_ref, dro_ref,
             dp_ref, dlb_ref, dgo_ref, ds_scr, dlb_scr, dgo_scr):
        tril = _tri(True)
        ones_u = jnp.where(_tri(False), 1.0, 0.0).astype(BF16)
        gov = go_ref[...]
        dp_ref[:, 0:ATTN_W] = dq_ref[...]
        dp_ref[:, ATTN_W:2 * ATTN_W] = dk_ref[...]
        dp_ref[:, 2 * ATTN_W:3 * ATTN_W] = dv_ref[...]

        @pl.when(pl.program_id(1) == 0)
        def _():
            ds_scr[...] = jnp.zeros_like(ds_scr)
            dlb_scr[...] = jnp.zeros_like(dlb_scr)
            dgo_scr[...] = jnp.zeros_like(dgo_scr)

        def step(ci, carry):
            c = cpb - 1 - ci
            sl = pl.ds(pl.multiple_of(c * CHUNK, CHUNK), CHUNK)
            hq = hq_ref[sl, :]
            hg = hg_ref[sl, :]
            sig, f, bb, bl, bm, sq, q, k = _hgrn_chunk(hq, hf_ref[sl, :], lb_ref[...], tril)
            vb = hi_ref[sl, :].astype(BF16)
            ebm = jnp.exp(bb - bm)
            embm = jnp.exp(bm - bb)
            eb = jnp.exp(bb)
            ebl = jnp.exp(bl - bb)
            e_last = jnp.exp(bl)
            qe = (q * ebm).astype(BF16)
            ke = (k * embm).astype(BF16)
            qb = (q * eb).astype(BF16)
            kb = (k * ebl).astype(BF16)
            st = [st_ref[hh, c] for hh in range(HGRN_HEADS)]
            dst = [ds_scr[hh] for hh in range(HGRN_HEADS)]
            o = oraw_ref[sl, :]
            dro = dro_ref[sl, :]
            sg = _sigmoid(hg)
            gov4 = jnp.concatenate([gov] * HGRN_HEADS, axis=1)
            rstd = jnp.concatenate(
                [jnp.broadcast_to(lax.rsqrt(jnp.mean(o[:, hs] * o[:, hs], axis=-1, keepdims=True) + RMS_EPS),
                                  (CHUNK, HGRN_DH)) for hs in HEAD_LANES], axis=1)
            ohat = o * rstd
            dn = dro * (hg * sg)
            dhg = dro * (ohat * gov4) * (sg * (1.0 + hg * (1.0 - sg)))
            dgo_inc = jnp.sum(dn * ohat, axis=0, keepdims=True)
            dohat = dn * gov4
            proj_h = dohat * ohat
            pm = jnp.concatenate(
                [jnp.broadcast_to(jnp.mean(proj_h[:, hs], axis=-1, keepdims=True), (CHUNK, HGRN_DH))
                 for hs in HEAD_LANES], axis=1)
            dob = (rstd * (dohat - ohat * pm)).astype(BF16)
            stb = [x.astype(BF16) for x in st]
            dstb = [x.astype(BF16) for x in dst]
            a = [jnp.where(tril, _dot(qe[:, hs], ke[:, hs], NT), 0.0).astype(BF16) for hs in HEAD_LANES]
            dab = [jnp.where(tril, _dot(dob[:, hs], vb[:, hs], NT), 0.0).astype(BF16) for hs in HEAD_LANES]
            dqb = [_dot(dob[:, hs], stb[hh]) for hh, hs in enumerate(HEAD_LANES)]
            dkb = [_dot(vb[:, hs], dstb[hh]) for hh, hs in enumerate(HEAD_LANES)]
            dv_state = [_dot(kb[:, hs], dstb[hh], NT) for hh, hs in enumerate(HEAD_LANES)]
            dst_next = [dst[hh] * e_last[:, hs] + _dot(dob[:, hs], qb[:, hs], TN) for hh, hs in enumerate(HEAD_LANES)]
            dv = [_dot(a[hh], dob[:, hs], TN) + dv_state[hh] for hh, hs in enumerate(HEAD_LANES)]
            dqe = jnp.concatenate([_dot(dab[hh], ke[:, hs]) for hh, hs in enumerate(HEAD_LANES)], axis=1)
            dke = jnp.concatenate([_dot(dab[hh], qe[:, hs], TN) for hh, hs in enumerate(HEAD_LANES)], axis=1)
            dqb = jnp.concatenate(dqb, axis=1)
            dkb = jnp.concatenate(dkb, axis=1)
            state_term = jnp.concatenate(
                [jnp.sum(dst[hh] * st[hh], axis=0, keepdims=True) for hh in range(HGRN_HEADS)], axis=1)
            dq = dqe * ebm + dqb * eb
            dk = dke * embm + dkb * ebl
            db = (qe.astype(F32) * dqe - ke.astype(F32) * dke) + q * (dqb * eb) - k * (dkb * ebl)
            d_last = jnp.sum(k * ebl * dkb, axis=0, keepdims=True) + state_term * e_last
            dg = _dot_exact_lhs(ones_u, db) + d_last
            df = dg / f - dk
            first = HQ_COL * HGRN_DH
            dp_ref[sl, first:first + HGRN_W] = (dq * (sq * (1.0 + hq * (1.0 - sq)))).astype(BF16)
            dp_ref[sl, first + HGRN_W:first + 2 * HGRN_W] = (df * (1.0 - lb_ref[...]) * sig * (1.0 - sig)).astype(BF16)
            dp_ref[sl, first + 2 * HGRN_W:first + 3 * HGRN_W] = jnp.concatenate(dv, axis=1).astype(BF16)
            dp_ref[sl, first + 3 * HGRN_W:first + 4 * HGRN_W] = dhg.astype(BF16)
            dlb_scr[...] += jnp.sum(df * (1.0 - sig), axis=0, keepdims=True)
            dgo_scr[...] += dgo_inc
            for hh in range(HGRN_HEADS):
                ds_scr[hh] = dst_next[hh]
            return carry

        lax.fori_loop(0, cpb, step, 0, unroll=HGRN_UNROLL)

        @pl.when(pl.program_id(1) == nblk - 1)
        def _():
            dlb_ref[...] = dlb_scr[...]
            dgo_ref[...] = dgo_scr[...]

    rows = lambda bi, i: bi * nblk + (nblk - 1 - i)
    col = lambda base: pl.BlockSpec((HGRN_ROWS, HGRN_W), lambda bi, i: (rows(bi, i), base // HGRN_HEADS))
    out = pl.BlockSpec((HGRN_ROWS, HGRN_W), lambda bi, i: (rows(bi, i), 0))
    part = pl.BlockSpec((None, 1, HGRN_W), lambda bi, i: (bi, 0, 0))
    width = HG_COL * HGRN_DH + HGRN_W
    o_shape = jax.ShapeDtypeStruct((t, width), BF16)
    p_shape = jax.ShapeDtypeStruct((b, 1, HGRN_W), F32)
    return pl.pallas_call(
        body,
        name=name,
        grid=(b, nblk),
        in_specs=[col(HQ_COL), col(HF_COL), col(HI_COL), col(HG_COL), out, out, out,
                  pl.BlockSpec((1, HGRN_W), lambda bi, i: (0, 0)), pl.BlockSpec((1, HGRN_DH), lambda bi, i: (0, 0)), out,
                  pl.BlockSpec((None, HGRN_HEADS, cpb, HGRN_DH, HGRN_DH), lambda bi, i: (bi, 0, nblk - 1 - i, 0, 0)),
                  col(ATTN_W // HGRN_DH)],
        out_specs=[pl.BlockSpec((HGRN_ROWS, width), lambda bi, i: (rows(bi, i), 0))] + [part] * 2,
        out_shape=[o_shape] + [p_shape] * 2,
        scratch_shapes=[pltpu.VMEM((HGRN_HEADS, HGRN_DH, HGRN_DH), F32), pltpu.VMEM((1, HGRN_W), F32),
                        pltpu.VMEM((1, HGRN_W), F32)],
        compiler_params=_params("parallel", "arbitrary"),
    )(proj, proj, proj, proj, *dqkv, lb, go, oraw, states, dmix)


def _small_grads(name, dg1, dgm, dg2, dgq, dgk, dbe_t, dbo_t, dlb, dgo, lbp):
    d = dg1.shape[1]

    def body(dg1_ref, dgm_ref, dg2_ref, dgq_ref, dgk_ref, dbe_ref, dbo_ref, dlb_ref, dgo_ref, lbp_ref,
             g1_ref, gm_ref, g2_ref, gq_ref, gk_ref, rb_ref, lbg_ref, go_ref):
        g1_ref[...] = jnp.sum(dg1_ref[...], axis=0, keepdims=True)
        gm_ref[...] = jnp.sum(dgm_ref[...], axis=0, keepdims=True)
        g2_ref[...] = jnp.sum(dg2_ref[...], axis=0, keepdims=True)
        r = lax.broadcasted_iota(jnp.int32, (ATTN_W, ATTN_DH), 0)
        cidx = lax.broadcasted_iota(jnp.int32, (ATTN_W, ATTN_DH), 1)
        fold = jnp.where(jnp.bitwise_and(r, ATTN_DH - 1) == cidx, 1.0, 0.0).astype(BF16)
        gq_ref[...] = jnp.sum(_dot_exact_rhs(dgq_ref[...], fold), axis=0, keepdims=True)
        gk_ref[...] = jnp.sum(_dot_exact_rhs(dgk_ref[...], fold), axis=0, keepdims=True)
        gosum = jnp.sum(dgo_ref[...], axis=0, keepdims=True)
        go_ref[...] = (gosum[:, 0:HGRN_DH] + gosum[:, HGRN_DH:2 * HGRN_DH]
                       + gosum[:, 2 * HGRN_DH:3 * HGRN_DH] + gosum[:, 3 * HGRN_DH:4 * HGRN_DH])
        p0 = lbp_ref[0:1, :]
        p1 = lbp_ref[1:2, :]
        lbv = 1.0 / (1.0 + jnp.exp(p1 - p0))
        dp0 = jnp.sum(dlb_ref[...], axis=0, keepdims=True) * lbv * (1.0 - lbv)
        lbg_ref[0:1, :] = dp0
        lbg_ref[1:2, :] = -dp0
        acc = dbe_ref[CHUNK - 1] + pltpu.roll(dbo_ref[CHUNK - 1], DB_W - CHUNK, 1)
        for tq in range(CHUNK - 1):
            acc = acc + pltpu.roll(dbe_ref[tq], CHUNK - 1 - tq, 1) + pltpu.roll(dbo_ref[tq], DB_W - 1 - tq, 1)
        jidx = lax.broadcasted_iota(jnp.int32, (DB_W, N_REL_PAD), 0)
        ridx = lax.broadcasted_iota(jnp.int32, (DB_W, N_REL_PAD), 1)
        rel = jnp.clip(KPAD + CHUNK - 1 - jidx, -REL_CLIP, REL_CLIP) + REL_CLIP
        rb_ref[...] = _dot_exact_rhs(acc, jnp.where(rel == ridx, 1.0, 0.0).astype(BF16))

    ins = [dg1, dgm, dg2, dgq, dgk, dbe_t, dbo_t, dlb, dgo, lbp]
    outs = [jax.ShapeDtypeStruct((1, d), F32)] * 3 + [jax.ShapeDtypeStruct((1, ATTN_DH), F32)] * 2 + [
        jax.ShapeDtypeStruct((ATTN_HEADS, N_REL_PAD), F32), jax.ShapeDtypeStruct((2, HGRN_W), F32),
        jax.ShapeDtypeStruct((1, HGRN_DH), F32)]
    vm = pl.BlockSpec(memory_space=pltpu.VMEM)
    return pl.pallas_call(
        body,
        name=name,
        in_specs=[vm] * len(ins),
        out_specs=[vm] * len(outs),
        out_shape=outs,
        compiler_params=pltpu.CompilerParams(vmem_limit_bytes=VMEM_LIMIT),
    )(*ins)


def _adam_update(w, g, m, v):
    m2 = ADAM_B1 * m + (1.0 - ADAM_B1) * g
    v2 = ADAM_B2 * v + (1.0 - ADAM_B2) * (g * g)
    m_hat = m2 / (1.0 - ADAM_B1 ** ADAM_STEP)
    v_hat = v2 / (1.0 - ADAM_B2 ** ADAM_STEP)
    delta = -ADAM_LR * (m_hat / (jnp.sqrt(v_hat) + ADAM_EPS) + ADAM_WD * w)
    return delta, m2, v2


def _rows_tile(r):
    return r if r <= 512 or r % 512 else 512


def _pair_sum(name, grad, theirs, core):
    n, half, c = theirs.shape
    tr = _rows_tile(half)
    nth = half // tr

    def body(core_ref, a_ref, b_ref, o_ref):
        o_ref[...] = (a_ref[...].astype(F32) + b_ref[...].astype(F32)).astype(o_ref.dtype)

    spec = pl.BlockSpec((None, tr, c), lambda i, j, core_ref: (i, j, 0))
    return pl.pallas_call(
        body, name=name,
        grid_spec=pltpu.PrefetchScalarGridSpec(
            num_scalar_prefetch=1, grid=(n, nth),
            in_specs=[pl.BlockSpec((None, tr, c), lambda i, j, core_ref: (i, core_ref[0] * nth + j, 0)), spec],
            out_specs=spec),
        out_shape=pltpu.HBM((n, half, c), BF16), compiler_params=_params("parallel", "parallel"),
    )(core, grad, theirs)


def _chip_sum(name, own, parts, chip):
    _, half, c = own.shape
    tr = _rows_tile(half)

    def body(chip_ref, own_ref, p_ref, o_ref):
        me = chip_ref[0]
        mine = own_ref[...].astype(F32)
        flip_x, flip_y, flip_xy = (p_ref[i].astype(F32) for i in range(3))
        acc = None
        for k in range(N_CHIPS):
            rel = jnp.bitwise_xor(me, k)
            term = jnp.where(rel == 0, mine, jnp.where(rel == 2, flip_x, jnp.where(rel == 1, flip_y, flip_xy)))
            acc = term if acc is None else acc + term
        o_ref[...] = acc

    return pl.pallas_call(
        body, name=name,
        grid_spec=pltpu.PrefetchScalarGridSpec(
            num_scalar_prefetch=1, grid=(half // tr,),
            in_specs=[pl.BlockSpec((None, tr, c), lambda j, chip_ref: (chip_ref[0], j, 0)),
                      pl.BlockSpec((3, tr, c), lambda j, chip_ref: (0, j, 0))],
            out_specs=pl.BlockSpec((tr, c), lambda j, chip_ref: (j, 0))),
        out_shape=pltpu.HBM((half, c), F32), compiler_params=_params("parallel"),
    )(chip, own, parts)


def _adamw(name, w, g_mine, g_theirs, m, v, core):
    _, r, c = w.shape
    half = r // 2
    tr = _rows_tile(half)
    nth = half // tr

    def body(core_ref, w_ref, gm_ref, gt_ref, m_ref, v_ref, g_ref, d_ref, m2_ref, v2_ref):
        g = jnp.where(pl.program_id(0) == core_ref[0], gm_ref[...], gt_ref[...])
        delta, m2, v2 = _adam_update(w_ref[...], g, m_ref[...], v_ref[...])
        g_ref[...] = g
        d_ref[...] = delta
        m2_ref[...] = m2
        v2_ref[...] = v2

    full = pl.BlockSpec((None, tr, c), lambda h, j, core_ref: (0, h * nth + j, 0))
    part = pl.BlockSpec((tr, c), lambda h, j, core_ref: (j, 0))
    shape = jax.ShapeDtypeStruct((1, r, c), F32)
    return pl.pallas_call(
        body, name=name,
        grid_spec=pltpu.PrefetchScalarGridSpec(
            num_scalar_prefetch=1, grid=(2, nth), in_specs=[full, part, part, full, full], out_specs=[full] * 4),
        out_shape=[shape] * 4, compiler_params=_params("parallel", "parallel"),
    )(core, w, g_mine, g_theirs, m, v)


def _rel_bias_table(name, rel_bias):
    padded = jnp.pad(rel_bias, ((0, 0), (0, N_REL_PAD - N_REL)))

    def body(rb_ref, o_ref):
        ridx = lax.broadcasted_iota(jnp.int32, (N_REL_PAD, BAND), 0)
        sidx = lax.broadcasted_iota(jnp.int32, (N_REL_PAD, BAND), 1)
        rb = rb_ref[...]

        def step(tq, carry):
            rel = jnp.clip(tq + KPAD - sidx, -REL_CLIP, REL_CLIP) + REL_CLIP
            onehot = jnp.where(rel == ridx, 1.0, 0.0).astype(BF16)
            o_ref[tq] = _dot_exact_rhs(rb, onehot)
            return carry

        lax.fori_loop(0, CHUNK, step, 0)

    vm = pl.BlockSpec(memory_space=pltpu.VMEM)
    table = pl.pallas_call(
        body, name=name, in_specs=[vm], out_specs=vm,
        out_shape=jax.ShapeDtypeStruct((CHUNK, ATTN_HEADS, BAND), F32),
    )(padded)
    return table.transpose(1, 0, 2)


def _adamw_small(name, w, parts, m, v):
    def body(w_ref, p_ref, m_ref, v_ref, g_ref, d_ref, m2_ref, v2_ref):
        g = p_ref[0]
        for i in range(1, N_DEV):
            g = g + p_ref[i]
        delta, m2, v2 = _adam_update(w_ref[...], g, m_ref[...], v_ref[...])
        g_ref[...] = g
        d_ref[...] = delta
        m2_ref[...] = m2
        v2_ref[...] = v2

    vm = pl.BlockSpec(memory_space=pltpu.VMEM)
    shape = jax.ShapeDtypeStruct((SMALL_ROWS, SMALL_COLS), F32)
    return pl.pallas_call(
        body, name=name, in_specs=[vm] * 4, out_specs=[vm] * 4, out_shape=[shape] * 4,
    )(w, parts, m, v)


def _position():
    return lax.axis_index("x"), lax.axis_index("y"), lax.axis_index("c")


def _other_chips(x, y):
    return [(1 - x, y), (x, 1 - y), (1 - x, 1 - y)]


ANY = pl.BlockSpec(memory_space=pl.ANY)
PAIR_ID = 0


def _pair_handshake():
    x, y, c = _position()
    barrier = pltpu.get_barrier_semaphore()
    pl.semaphore_signal(barrier, inc=1, device_id=(x, y, 1 - c), device_id_type=MESH)
    pl.semaphore_wait(barrier, 1)


PAIR_CALL = pltpu.CompilerParams(collective_id=PAIR_ID)


HBM = pl.BlockSpec(memory_space=pltpu.HBM)
SEM = pl.BlockSpec(memory_space=pltpu.SEMAPHORE)
SPLIT_COPY = pltpu.SideEffectType.DATAFLOW_SIDE_EFFECTING


def _gather_copy(shards, outs, send_sem, recv_sem, i, j):
    x, y, c = _position()
    chips = _other_chips(x, y)
    half = shards[i].shape[0] // 2
    rows = pl.ds(pl.multiple_of(c * half, 16), half)
    return pltpu.make_async_remote_copy(
        src_ref=shards[i].at[rows, :], dst_ref=outs[i].at[2 * x + y, rows, :],
        send_sem=send_sem.at[3 * i + j], recv_sem=recv_sem.at[3 * i + j],
        device_id=(chips[j][0], chips[j][1], c), device_id_type=MESH)


def _gather_start(name, shards, after):
    n = len(shards)

    def body(*refs):
        srcs, outs = refs[:n], refs[n:2 * n]
        send_sem, recv_sem = refs[2 * n + len(after)], refs[2 * n + len(after) + 1]
        token = refs[-1]
        for i in range(n):
            for j in range(3):
                _gather_copy(srcs, outs, send_sem, recv_sem, i, j).start()
        token[...] = jnp.zeros_like(token)

    full = [(N_CHIPS,) + s.shape for s in shards]
    res = pl.pallas_call(
        body,
        name=name,
        in_specs=[HBM] * (2 * n) + [ANY] * len(after),
        out_specs=[SEM, SEM] + [HBM] * (2 * n) + [pl.BlockSpec(memory_space=pltpu.VMEM)],
        out_shape=[pltpu.SemaphoreType.DMA((3 * n,)), pltpu.SemaphoreType.DMA((3 * n,))]
        + [pltpu.HBM(s.shape, s.dtype) for s in shards]
        + [pltpu.HBM(shp, s.dtype) for shp, s in zip(full, shards)]
        + [jax.ShapeDtypeStruct((8, LANES), F32)],
        input_output_aliases={i: 2 + i for i in range(2 * n)},
        compiler_params=pltpu.CompilerParams(has_side_effects=SPLIT_COPY),
    )(*[pltpu.with_memory_space_constraint(s, pltpu.HBM) for s in shards],
      *[pltpu.with_memory_space_constraint(lax.empty(shp, s.dtype), pltpu.HBM) for shp, s in zip(full, shards)],
      *after)
    return res[0], res[1], list(res[2:2 + n]), list(res[2 + n:2 + 2 * n]), res[-1]


def _gather_wait(name, send_sem, recv_sem, shards, outs, after):
    n = len(shards)

    def body(*refs):
        srcs, out_refs = refs[:n], refs[n:2 * n]
        send_ref, recv_ref = refs[2 * n], refs[2 * n + 1]
        for i in range(n):
            for j in range(3):
                copy = _gather_copy(srcs, out_refs, send_ref, recv_ref, i, j)
                copy.wait_send()
                copy.wait_recv()

    res = pl.pallas_call(
        body,
        name=name,
        in_specs=[HBM] * (2 * n) + [SEM, SEM] + [ANY] * len(after),
        out_specs=[HBM] * (2 * n),
        out_shape=[pltpu.HBM(s.shape, s.dtype) for s in shards] + [pltpu.HBM(o.shape, o.dtype) for o in outs],
        input_output_aliases={i: i for i in range(2 * n)},
        compiler_params=pltpu.CompilerParams(has_side_effects=SPLIT_COPY),
    )(*shards, *outs, send_sem, recv_sem, *after)
    return list(res[:n]), list(res[n:])


def _join_copies(srcs, ins, outs, own_send, own_recv, half_send, half_recv):
    x, y, c = _position()
    chips = _other_chips(x, y)
    copies = []
    for i in range(len(srcs)):
        copies.append(pltpu.make_async_remote_copy(
            src_ref=srcs[i], dst_ref=outs[i].at[2 * x + y], send_sem=own_send.at[i], recv_sem=own_recv.at[i],
            device_id=(x, y, 1 - c), device_id_type=MESH))
        half = srcs[i].shape[0] // 2
        rows = pl.ds(pl.multiple_of(c * half, 16), half)
        for j in range(3):
            slot = 2 * chips[j][0] + chips[j][1]
            copies.append(pltpu.make_async_remote_copy(
                src_ref=ins[i].at[slot, rows, :], dst_ref=outs[i].at[slot, rows, :],
                send_sem=half_send.at[3 * i + j], recv_sem=half_recv.at[3 * i + j],
                device_id=(x, y, 1 - c), device_id_type=MESH))
    return copies


def _gather_join(name, shards, outs):
    n = len(shards)

    def body(*refs):
        _pair_handshake()
        copies = _join_copies(refs[:n], refs[n:2 * n], refs[2 * n:3 * n], *refs[3 * n:])
        for cp in copies:
            cp.start()
        for cp in copies:
            cp.wait()

    return pl.pallas_call(
        body,
        name=name,
        in_specs=[ANY] * (2 * n),
        out_specs=[HBM] * n,
        out_shape=[pltpu.HBM(o.shape, o.dtype) for o in outs],
        input_output_aliases={n + i: i for i in range(n)},
        scratch_shapes=[pltpu.SemaphoreType.DMA((n,))] * 2 + [pltpu.SemaphoreType.DMA((3 * n,))] * 2,
        compiler_params=PAIR_CALL,
    )(*shards, *outs)


def _join_start(name, shards, outs):
    n = len(shards)

    def body(*refs):
        _pair_handshake()
        srcs, arrs = refs[:n], refs[n:2 * n]
        sems = refs[2 * n:2 * n + 4]
        token = refs[-1]
        for cp in _join_copies(srcs, arrs, arrs, *sems):
            cp.start()
        token[...] = jnp.zeros_like(token)

    res = pl.pallas_call(
        body,
        name=name,
        in_specs=[HBM] * (2 * n),
        out_specs=[SEM] * 4 + [HBM] * (2 * n) + [pl.BlockSpec(memory_space=pltpu.VMEM)],
        out_shape=[pltpu.SemaphoreType.DMA((n,))] * 2 + [pltpu.SemaphoreType.DMA((3 * n,))] * 2
        + [pltpu.HBM(s.shape, s.dtype) for s in shards] + [pltpu.HBM(o.shape, o.dtype) for o in outs]
        + [jax.ShapeDtypeStruct((8, LANES), F32)],
        input_output_aliases={i: 4 + i for i in range(2 * n)},
        compiler_params=pltpu.CompilerParams(has_side_effects=SPLIT_COPY, collective_id=PAIR_ID),
    )(*shards, *outs)
    return list(res[:4]), list(res[4:4 + n]), list(res[4 + n:4 + 2 * n]), res[-1]


def _join_wait(name, sems, shards, outs, after):
    n = len(shards)

    def body(*refs):
        srcs, arrs = refs[:n], refs[n:2 * n]
        for cp in _join_copies(srcs, arrs, arrs, *refs[2 * n:2 * n + 4]):
            cp.wait_send()
            cp.wait_recv()

    res = pl.pallas_call(
        body,
        name=name,
        in_specs=[HBM] * (2 * n) + [SEM] * 4 + [ANY] * len(after),
        out_specs=[HBM] * (2 * n),
        out_shape=[pltpu.HBM(s.shape, s.dtype) for s in shards] + [pltpu.HBM(o.shape, o.dtype) for o in outs],
        input_output_aliases={i: i for i in range(2 * n)},
        compiler_params=pltpu.CompilerParams(has_side_effects=SPLIT_COPY),
    )(*shards, *outs, *sems, *after)
    return list(res[n:])


def _pair_copy(grads, lands, send_sem, recv_sem, i):
    x, y, c = _position()
    half = grads[i].shape[1] // 2
    give = pl.ds(pl.multiple_of((1 - c) * half, 16), half)
    return pltpu.make_async_remote_copy(
        src_ref=grads[i].at[:, give, :], dst_ref=lands[i], send_sem=send_sem.at[i], recv_sem=recv_sem.at[i],
        device_id=(x, y, 1 - c), device_id_type=MESH)


def _pair_start(name, grads):
    n = len(grads)

    def body(*refs):
        _pair_handshake()
        srcs, lands = refs[:n], refs[n:2 * n]
        send_sem, recv_sem = refs[2 * n], refs[2 * n + 1]
        token = refs[-1]
        for i in range(n):
            _pair_copy(srcs, lands, send_sem, recv_sem, i).start()
        token[...] = jnp.zeros_like(token)

    halves = [(g.shape[0], g.shape[1] // 2, g.shape[2]) for g in grads]
    res = pl.pallas_call(
        body,
        name=name,
        in_specs=[HBM] * (2 * n),
        out_specs=[SEM, SEM] + [HBM] * (2 * n) + [pl.BlockSpec(memory_space=pltpu.VMEM)],
        out_shape=[pltpu.SemaphoreType.DMA((n,)), pltpu.SemaphoreType.DMA((n,))]
        + [pltpu.HBM(g.shape, g.dtype) for g in grads]
        + [pltpu.HBM(shp, g.dtype) for shp, g in zip(halves, grads)]
        + [jax.ShapeDtypeStruct((8, LANES), F32)],
        input_output_aliases={i: 2 + i for i in range(2 * n)},
        compiler_params=pltpu.CompilerParams(has_side_effects=SPLIT_COPY, collective_id=PAIR_ID),
    )(*[pltpu.with_memory_space_constraint(g, pltpu.HBM) for g in grads],
      *[pltpu.with_memory_space_constraint(lax.empty(shp, g.dtype), pltpu.HBM) for shp, g in zip(halves, grads)])
    return res[0], res[1], list(res[2:2 + n]), list(res[2 + n:2 + 2 * n]), res[-1]


def _pair_wait(name, send_sem, recv_sem, grads, lands, after):
    n = len(grads)

    def body(*refs):
        srcs, land_refs = refs[:n], refs[n:2 * n]
        send_ref, recv_ref = refs[2 * n], refs[2 * n + 1]
        for i in range(n):
            copy = _pair_copy(srcs, land_refs, send_ref, recv_ref, i)
            copy.wait_send()
            copy.wait_recv()

    res = pl.pallas_call(
        body,
        name=name,
        in_specs=[HBM] * (2 * n) + [SEM, SEM, ANY],
        out_specs=[HBM] * (2 * n),
        out_shape=[pltpu.HBM(g.shape, g.dtype) for g in grads] + [pltpu.HBM(l.shape, l.dtype) for l in lands],
        input_output_aliases={i: i for i in range(2 * n)},
        compiler_params=pltpu.CompilerParams(has_side_effects=SPLIT_COPY),
    )(*grads, *lands, send_sem, recv_sem, after)
    return list(res[:n]), list(res[n:])


def _scatter_copy(srcs, lands, send_sem, recv_sem, i, j):
    x, y, c = _position()
    chips = _other_chips(x, y)
    return pltpu.make_async_remote_copy(
        src_ref=srcs[i].at[2 * chips[j][0] + chips[j][1]], dst_ref=lands[i].at[j],
        send_sem=send_sem.at[3 * i + j], recv_sem=recv_sem.at[3 * i + j],
        device_id=(chips[j][0], chips[j][1], c), device_id_type=MESH)


def _scatter_start(name, sums):
    n = len(sums)

    def body(*refs):
        srcs, lands = refs[:n], refs[n:2 * n]
        send_sem, recv_sem = refs[2 * n], refs[2 * n + 1]
        token = refs[-1]
        for i in range(n):
            for j in range(3):
                _scatter_copy(srcs, lands, send_sem, recv_sem, i, j).start()
        token[...] = jnp.zeros_like(token)

    land_shapes = [(3,) + s.shape[1:] for s in sums]
    res = pl.pallas_call(
        body,
        name=name,
        in_specs=[HBM] * (2 * n),
        out_specs=[SEM, SEM] + [HBM] * (2 * n) + [pl.BlockSpec(memory_space=pltpu.VMEM)],
        out_shape=[pltpu.SemaphoreType.DMA((3 * n,)), pltpu.SemaphoreType.DMA((3 * n,))]
        + [pltpu.HBM(s.shape, s.dtype) for s in sums]
        + [pltpu.HBM(shp, s.dtype) for shp, s in zip(land_shapes, sums)]
        + [jax.ShapeDtypeStruct((8, LANES), F32)],
        input_output_aliases={i: 2 + i for i in range(2 * n)},
        compiler_params=pltpu.CompilerParams(has_side_effects=SPLIT_COPY),
    )(*[pltpu.with_memory_space_constraint(s, pltpu.HBM) for s in sums],
      *[pltpu.with_memory_space_constraint(lax.empty(shp, s.dtype), pltpu.HBM) for shp, s in zip(land_shapes, sums)])
    return res[0], res[1], list(res[2:2 + n]), list(res[2 + n:2 + 2 * n]), res[-1]


def _scatter_wait(name, send_sem, recv_sem, sums, lands, after):
    n = len(sums)

    def body(*refs):
        srcs, land_refs = refs[:n], refs[n:2 * n]
        send_ref, recv_ref = refs[2 * n], refs[2 * n + 1]
        for i in range(n):
            for j in range(3):
                copy = _scatter_copy(srcs, land_refs, send_ref, recv_ref, i, j)
                copy.wait_send()
                copy.wait_recv()

    res = pl.pallas_call(
        body,
        name=name,
        in_specs=[HBM] * (2 * n) + [SEM, SEM, ANY],
        out_specs=[HBM] * (2 * n),
        out_shape=[pltpu.HBM(s.shape, s.dtype) for s in sums] + [pltpu.HBM(l.shape, l.dtype) for l in lands],
        input_output_aliases={i: i for i in range(2 * n)},
        compiler_params=pltpu.CompilerParams(has_side_effects=SPLIT_COPY),
    )(*sums, *lands, send_sem, recv_sem, after)
    return list(res[:n]), list(res[n:])


def _pair_join(name, halves, small=None):
    n = len(halves)
    if small is None:
        def body_plain(*refs):
            _pair_handshake()
            ins, outs = refs[:n], refs[n:2 * n]
            send_sem, recv_sem = refs[2 * n:]
            x, y, c = _position()
            swaps = [pltpu.make_async_remote_copy(
                src_ref=ins[i], dst_ref=outs[i], send_sem=send_sem.at[i], recv_sem=recv_sem.at[i],
                device_id=(x, y, 1 - c), device_id_type=MESH) for i in range(n)]
            for swap in swaps:
                swap.start()
            for swap in swaps:
                swap.wait()

        return pl.pallas_call(
            body_plain,
            name=name,
            in_specs=[ANY] * n,
            out_specs=[ANY] * n,
            out_shape=[jax.ShapeDtypeStruct(h.shape, h.dtype) for h in halves],
            scratch_shapes=[pltpu.SemaphoreType.DMA((n,))] * 2,
            compiler_params=PAIR_CALL,
        )(*halves)

    def body(*refs):
        ins, small_ref = refs[:n], refs[n]
        outs, all_ref = refs[n + 1:2 * n + 1], refs[2 * n + 1]
        send_sem, recv_sem, sm_send, sm_recv, sm_local = refs[2 * n + 2:]
        x, y, c = _position()
        swaps = []
        for i in range(n):
            swap = pltpu.make_async_remote_copy(
                src_ref=ins[i], dst_ref=outs[i], send_sem=send_sem.at[i], recv_sem=recv_sem.at[i],
                device_id=(x, y, 1 - c), device_id_type=MESH)
            swap.start()
            swaps.append(swap)
        me = 4 * x + 2 * y + c
        sm_own = pltpu.make_async_copy(small_ref, all_ref.at[me], sm_local)
        sm_own.start()
        pushes, arrivals = [], []
        for mask in range(1, N_DEV):
            px, py, pc = x ^ (mask >> 2), y ^ ((mask >> 1) & 1), c ^ (mask & 1)
            pushes.append(pltpu.make_async_remote_copy(
                src_ref=small_ref, dst_ref=all_ref.at[me], send_sem=sm_send.at[mask - 1], recv_sem=sm_recv.at[mask - 1],
                device_id=(px, py, pc), device_id_type=MESH))
            arrivals.append(pltpu.make_async_remote_copy(
                src_ref=small_ref, dst_ref=all_ref.at[4 * px + 2 * py + pc], send_sem=sm_send.at[mask - 1],
                recv_sem=sm_recv.at[mask - 1], device_id=(px, py, pc), device_id_type=MESH))
        for cp in pushes:
            cp.start()
        for swap in swaps:
            swap.wait()
        for cp in arrivals:
            cp.wait_recv()
        for cp in pushes:
            cp.wait_send()
        sm_own.wait()

    res = pl.pallas_call(
        body,
        name=name,
        in_specs=[ANY] * (n + 1),
        out_specs=[ANY] * (n + 1),
        out_shape=[jax.ShapeDtypeStruct(h.shape, h.dtype) for h in halves]
        + [jax.ShapeDtypeStruct((N_DEV,) + small.shape, small.dtype)],
        scratch_shapes=[pltpu.SemaphoreType.DMA((n,))] * 2 + [pltpu.SemaphoreType.DMA((N_DEV - 1,))] * 2
        + [pltpu.SemaphoreType.DMA(())],
    )(*halves, small)
    return res[:n], res[n]


def _lower_bound(lbp):
    return jax.nn.softmax(lbp, axis=0)[0:1]


def _local_step(x, target, g1, gm, g2, gq, gk, go, rel_bias, lbp, weights, on_grads, grads_sent):
    b, s, d = x.shape
    t = b * s
    x0 = x.reshape(t, d)
    tgt = target.reshape(t, d)
    gq_t = jnp.tile(gq, (1, ATTN_HEADS))
    gk_t = jnp.tile(gk, (1, ATTN_HEADS))
    lb = _lower_bound(lbp)
    table = _band_table(_rel_bias_table("rel_bias_table", rel_bias))

    h1 = _rmsnorm_fwd("norm1", x0, g1)
    wg1, wu1, deps1 = weights["first"]((h1, table))
    a1, b1, z1 = _ffn_up("ffn1_up", h1, wg1, wu1, deps1)
    wd1, deps_mid = weights["mid"]((z1,))
    x1, h2 = _ffn_down("ffn1_down", z1, wd1, x0, gm, deps_mid)
    w_in, w_out = weights["mid_rest"]((x1,))
    ns = w_in.shape[0]
    proj = _in_proj("in_proj", h2, w_in)
    proj3 = proj.reshape(b, s, proj.shape[1])
    qn, kn, vb = _qk_prep("qk_prep", proj3, gq_t, gk_t)
    attn = _attn_fwd("attn_fwd", qn, kn, vb, table, weights["last_begin"]((qn,))).reshape(t, ATTN_W)
    mix, oraw, states = _hgrn_fwd("hgrn_fwd", proj, attn, lb, go, b, s)
    x2, h3 = _out_proj("out_proj", mix, w_out, x1, g2)
    wg2, wu2, wd2 = weights["last"]((h3,))
    a2, b2, z2 = _ffn_up("ffn2_up", h3, wg2, wu2)
    dy, dyh, sq = _ffn_down_loss("ffn2_down_loss", z2, wd2, x2, tgt)
    loss = 0.5 * jnp.sum(sq) / d

    da2, db2 = _ffn_bwd_act("ffn2_bwd_act", dyh, wd2, a2, b2)
    dwd2 = _grad_w_cols("ffn2_dwd", z2, dyh)
    dwg2 = _grad_w_cols("ffn2_dwg", da2, h3)
    dwu2 = _grad_w_cols("ffn2_dwu", db2, h3)
    sent2 = on_grads("ffn2", {"ffn2_w_gate": dwg2, "ffn2_w_up": dwu2, "ffn2_w_down": dwd2})
    dx2, dx2b, dg2 = _ffn_bwd_in("ffn2_bwd_in", da2, db2, wg2, wu2, x2, g2, dy, 1.0, sent2)
    sent2 = grads_sent("ffn2", dx2b)

    dwout = _grad_w_out("dw_out", mix, dx2b)
    dmix = _out_proj_bwd("out_proj_bwd", dx2b, w_out, sent2)
    dqn, dkn, dvn, dbe, dbo = _attn_bwd("attn_bwd", qn, kn, vb, table, dmix.reshape(b, s, dmix.shape[1]))
    dpq, dpk, dpv, dgq, dgk = _qk_prep_bwd("qk_prep_bwd", proj3, dqn, dkn, dvn, gq_t, gk_t)
    dpq, dpk, dpv = (a.reshape(t, ATTN_W) for a in (dpq, dpk, dpv))
    dproj, dlb, dgo = _hgrn_bwd("hgrn_bwd", proj, (dpq, dpk, dpv), lb, go, oraw, states, dmix, b, s)
    dwin = _grad_w_in("dw_in", h2, dproj, ns)
    dx1, dx1h, dgm = _in_proj_bwd("in_proj_bwd", dproj, w_in, x1, gm, dx2, 0.5)

    dwd1 = _grad_w_cols("ffn1_dwd", z1, dx1h)
    sent_mix = on_grads("mix", {"w_in": dwin, "w_out": dwout.reshape(ns, dwout.shape[0] // ns, d),
                                "ffn1_w_down": dwd1})
    da1, db1 = _ffn_bwd_act("ffn1_bwd_act", dx1h, wd1, a1, b1, sent_mix)
    sent_mix = grads_sent("mix", da1)
    dwg1 = _grad_w_cols("ffn1_dwg", da1, h1, sent_mix)
    dwu1 = _grad_w_cols("ffn1_dwu", db1, h1)
    on_grads("ffn1", {"ffn1_w_gate": dwg1, "ffn1_w_up": dwu1})
    sent1 = grads_sent("ffn1", None)
    dx0, dg1 = _ffn_bwd_in("ffn1_bwd_in", da1, db1, wg1, wu1, x0, g1, dx1, None, sent1)

    nt = dg1.shape[0]
    sg = _small_grads(
        "small_grads", dg1.reshape(nt, d), dgm.reshape(nt, d), dg2.reshape(nt, d),
        dgq.reshape(-1, ATTN_W), dgk.reshape(-1, ATTN_W), dbe.transpose(1, 0, 2), dbo.transpose(1, 0, 2),
        dlb.reshape(b, HGRN_W), dgo.reshape(b, HGRN_W), lbp)
    g1g, gmg, g2g, gqg, gkg, rbg, lbg, gog = sg
    small = _pack_small(g1g, gmg, g2g, lbg, rbg[:, :N_REL], gqg, gkg, gog, loss)
    return dx0.reshape(b, s, d), small


LOSS_SLOT = 7 * SMALL_COLS + 2 * ATTN_DH + HGRN_DH


def _pack_small(g1, gm, g2, lbp, rel_bias, gq, gk, go, loss=None):
    flat = [g1.reshape(-1), gm.reshape(-1), g2.reshape(-1), lbp.reshape(-1), rel_bias.reshape(-1)]
    n_bias = 3 * SMALL_COLS - rel_bias.size
    heads = [gq.reshape(-1), gk.reshape(-1), go.reshape(-1)]
    heads.append(jnp.zeros((1,), F32) if loss is None else loss.reshape(1))
    n_tail = SMALL_COLS - sum(h.size for h in heads)
    return jnp.concatenate(flat + [jnp.zeros((n_bias,), F32)] + heads + [jnp.zeros((n_tail,), F32)]).reshape(
        SMALL_ROWS, SMALL_COLS)


def _unpack_small(p, d):
    flat = p.reshape(-1)
    o = 3 * d
    g1, gm, g2 = p[0:1], p[1:2], p[2:3]
    lbp = flat[o:o + 2 * HGRN_W].reshape(2, HGRN_W)
    o = 4 * SMALL_COLS
    rel = flat[o:o + ATTN_HEADS * N_REL].reshape(1, ATTN_HEADS, N_REL)
    o = 7 * SMALL_COLS
    gq = flat[o:o + ATTN_DH].reshape(1, ATTN_DH)
    gk = flat[o + ATTN_DH:o + 2 * ATTN_DH].reshape(1, ATTN_DH)
    go = flat[o + 2 * ATTN_DH:o + 2 * ATTN_DH + HGRN_DH].reshape(1, HGRN_DH)
    return g1, gm, g2, gq, gk, rel, lbp, go


def kernel(x, ffn1_norm_g, ffn1_w_gate, ffn1_w_up, ffn1_w_down, mix_norm_g, w_in, attn_q_norm_g, attn_k_norm_g, attn_rel_bias, hgrn_lower_bounds, hgrn_out_norm_g, w_out, ffn2_norm_g, ffn2_w_gate, ffn2_w_up, ffn2_w_down, loss_target, m_ffn1_norm_g, m_ffn1_w_gate, m_ffn1_w_up, m_ffn1_w_down, m_mix_norm_g, m_w_in, m_attn_q_norm_g, m_attn_k_norm_g, m_attn_rel_bias, m_hgrn_lower_bounds, m_hgrn_out_norm_g, m_w_out, m_ffn2_norm_g, m_ffn2_w_gate, m_ffn2_w_up, m_ffn2_w_down, v_ffn1_norm_g, v_ffn1_w_gate, v_ffn1_w_up, v_ffn1_w_down, v_mix_norm_g, v_w_in, v_attn_q_norm_g, v_attn_k_norm_g, v_attn_rel_bias, v_hgrn_lower_bounds, v_hgrn_out_norm_g, v_w_out, v_ffn2_norm_g, v_ffn2_w_gate, v_ffn2_w_up, v_ffn2_w_down):
    d = x.shape[-1]
    big_w = [ffn1_w_gate, ffn1_w_up, ffn1_w_down, w_in, w_out, ffn2_w_gate, ffn2_w_up, ffn2_w_down]
    big_m = [m_ffn1_w_gate, m_ffn1_w_up, m_ffn1_w_down, m_w_in, m_w_out, m_ffn2_w_gate, m_ffn2_w_up, m_ffn2_w_down]
    big_v = [v_ffn1_w_gate, v_ffn1_w_up, v_ffn1_w_down, v_w_in, v_w_out, v_ffn2_w_gate, v_ffn2_w_up, v_ffn2_w_down]
    big_names = ["ffn1_w_gate", "ffn1_w_up", "ffn1_w_down", "w_in", "w_out", "ffn2_w_gate", "ffn2_w_up", "ffn2_w_down"]
    flipped = {nm for nm in big_names if nm.endswith("gate") or nm.endswith("up")}
    flip = lambda nm, a: jnp.swapaxes(a, 1, 2) if nm in flipped else a
    big_w, big_m, big_v = ([flip(nm, a) for nm, a in zip(big_names, arrs)] for arrs in (big_w, big_m, big_v))

    shards = [w[0].astype(BF16) for w in big_w]
    start_a = _gather_start("gather_start_up1", shards[:2], ())
    start_b = _gather_start("gather_start_mid", shards[2:5], (start_a[4],))
    start_c = _gather_start("gather_start_ffn2", shards[5:], (start_b[4],))

    pending = {}

    def arrived(tag, started, after):
        send_sem, recv_sem, srcs, outs, _ = started
        return _gather_wait("gather_wait_" + tag, send_sem, recv_sem, srcs, outs, after)

    def first_weights(after):
        return (*_gather_join("gather_join_up1", *arrived("up1", start_a, after)), (start_c[4],))

    def mid_weights(after):
        srcs, outs = arrived("mid", start_b, after)
        (wd1,) = _gather_join("gather_join_wd1", srcs[:1], outs[:1])
        pending["mid"] = _join_start("join_start_mid", srcs[1:], outs[1:])
        return wd1, (pending["mid"][3],)

    def mid_rest(after):
        sems, srcs, outs, _ = pending["mid"]
        win_f, wout_f = _join_wait("join_wait_mid", sems, srcs, outs, after)
        return win_f, wout_f.reshape(wout_f.shape[0] * wout_f.shape[1], d)

    def last_begin(after):
        pending["ffn2"] = _join_start("join_start_ffn2", *arrived("ffn2", start_c, after))
        return (pending["ffn2"][3],)

    def last_weights(after):
        sems, srcs, outs, _ = pending["ffn2"]
        return _join_wait("join_wait_ffn2", sems, srcs, outs, after)

    weights = {"first": first_weights, "mid": mid_weights, "mid_rest": mid_rest, "last_begin": last_begin,
               "last": last_weights}

    core = lax.axis_index("c").astype(jnp.int32).reshape(1)
    chip = (2 * lax.axis_index("x") + lax.axis_index("y")).astype(jnp.int32).reshape(1)
    started = {}

    def on_grads(tag, grads):
        names = list(grads)
        started[tag] = (names, _pair_start("pair_start_" + tag, [grads[nm] for nm in names]))
        return (started[tag][1][4],)

    def grads_sent(tag, after):
        names, (send_sem, recv_sem, grads, lands, token) = started[tag]
        grads, theirs = _pair_wait("pair_wait_" + tag, send_sem, recv_sem, grads, lands, token if after is None else after)
        sums = [_pair_sum("pair_sum_" + nm, g, th, core) for nm, g, th in zip(names, grads, theirs)]
        started[tag] = (names, _scatter_start("scatter_start_" + tag, sums))
        return (started[tag][1][4],)

    grad_x, small_g = _local_step(
        x, loss_target, ffn1_norm_g, mix_norm_g, ffn2_norm_g, attn_q_norm_g, attn_k_norm_g, hgrn_out_norm_g,
        attn_rel_bias[0], hgrn_lower_bounds, weights, on_grads, grads_sent)

    def finish(tag, after):
        names, (send_sem, recv_sem, sums, lands, _) = started[tag]
        sums, lands = _scatter_wait("scatter_wait_" + tag, send_sem, recv_sem, sums, lands, after)
        return names, [_chip_sum("chip_sum_" + nm, sm, ld, chip) for nm, sm, ld in zip(names, sums, lands)]

    by_name = {nm: (w, m, v) for nm, w, m, v in zip(big_names, big_w, big_m, big_v)}
    updated = {}

    def update(names, halves, other_halves):
        for nm, mine, theirs in zip(names, halves, other_halves):
            w, m, v = by_name[nm]
            updated[nm] = _adamw("adamw_" + nm, w, mine, theirs, m, v, core)

    last_token = started["ffn1"][1][4]
    names_a, halves_a = finish("ffn2", last_token)
    names_m, halves_m = finish("mix", last_token)
    names_a, halves_a = names_a + names_m, halves_a + halves_m
    update(names_a, halves_a, _pair_join("pair_join_early", halves_a))
    names_b, halves_b = finish("ffn1", updated[names_a[-1]][1])
    others_b, small_all = _pair_join("pair_join_last", halves_b, small_g)
    update(names_b, halves_b, others_b)
    big_out = [updated[nm] for nm in big_names]

    pack = lambda g1, gm, g2, gq, gk, rel, lbp, go: _pack_small(g1, gm, g2, lbp, rel[0], gq, gk, go)
    small_w = pack(ffn1_norm_g, mix_norm_g, ffn2_norm_g, attn_q_norm_g, attn_k_norm_g, attn_rel_bias, hgrn_lower_bounds, hgrn_out_norm_g)
    small_m = pack(m_ffn1_norm_g, m_mix_norm_g, m_ffn2_norm_g, m_attn_q_norm_g, m_attn_k_norm_g, m_attn_rel_bias, m_hgrn_lower_bounds, m_hgrn_out_norm_g)
    small_v = pack(v_ffn1_norm_g, v_mix_norm_g, v_ffn2_norm_g, v_attn_q_norm_g, v_attn_k_norm_g, v_attn_rel_bias, v_hgrn_lower_bounds, v_hgrn_out_norm_g)
    small_res = _adamw_small("adamw_small", small_w, small_all, small_m, small_v)
    small_out = [_unpack_small(p, d) for p in small_res]
    loss = small_res[0].reshape(-1)[LOSS_SLOT]

    def assemble(kind):
        bg = [flip(nm, o[kind]) for nm, o in zip(big_names, big_out)]
        g1, gm, g2, gq, gk, rel, lbp, go = small_out[kind]
        return [g1, bg[0], bg[1], bg[2], gm, bg[3], gq, gk, rel, lbp, go, bg[4], g2, bg[5], bg[6], bg[7]]

    return (loss, grad_x, *assemble(0), *assemble(1), *assemble(2), *assemble(3))
```
